```python
import jax, jax.numpy as jnp
from jax import lax
import numpy as np

D_MODEL = 1024
BATCH = 8
SEQ = 4096
DEPTH = 1

CHUNK = 64
LEFT_CHUNKS = 8
BAND = (LEFT_CHUNKS + 1) * CHUNK
N_HEADS = 8
HEAD_DIM = 64
D_ATTN = N_HEADS * HEAD_DIM
D_CONV = 512
CONV_K = 31
MAX_REL = 128
D_FF = 2816
FFN_CONV_K = 3
EPS = 1e-6
NEG_INF = -1e30
IN_WIDTHS = (D_ATTN, D_ATTN, D_ATTN, D_CONV, D_CONV, D_MODEL, D_MODEL)
D_IN = D_ATTN * 3 + D_CONV * 2 + D_MODEL * 2

kernel_name = "hybrid_chunked_attn_conformer_conv_convffn_block"


def rms_norm(x, g):
    xf = x.astype(jnp.float32)
    y = xf * lax.rsqrt(jnp.mean(xf * xf, axis=-1, keepdims=True) + EPS)
    return (y * g.astype(jnp.float32)).astype(x.dtype)


def layer_norm(x, g, b):
    xf = x.astype(jnp.float32)
    mu = jnp.mean(xf, axis=-1, keepdims=True)
    var = jnp.mean(jnp.square(xf - mu), axis=-1, keepdims=True)
    y = (xf - mu) * lax.rsqrt(var + EPS)
    return (y * g.astype(jnp.float32) + b.astype(jnp.float32)).astype(x.dtype)


def causal_dwconv(x, w, b):
    k = w.shape[0]
    y = lax.conv_general_dilated(
        x, w[:, None, :].astype(x.dtype), window_strides=(1,), padding=[(k - 1, 0)],
        dimension_numbers=('NWC', 'WIO', 'NWC'), feature_group_count=x.shape[-1])
    return y + b


def chunk_band(t):
    b, s, h, dh = t.shape
    nc = s // CHUNK
    tc = t.reshape(b, nc, CHUNK, h, dh)
    tp = jnp.pad(tc, ((0, 0), (LEFT_CHUNKS, 0), (0, 0), (0, 0), (0, 0)))
    band = jnp.stack([tp[:, j:j + nc] for j in range(LEFT_CHUNKS + 1)], axis=2)
    return band.reshape(b, nc, BAND, h, dh)


def chunked_rel_attention(q, k, v, rel_bias):
    b, s, _ = q.shape
    nc = s // CHUNK
    qc = q.reshape(b, nc, CHUNK, N_HEADS, HEAD_DIM)
    kb = chunk_band(k.reshape(b, s, N_HEADS, HEAD_DIM))
    vb = chunk_band(v.reshape(b, s, N_HEADS, HEAD_DIM))
    scores = jnp.einsum('bcqhd,bckhd->bhcqk', qc, kb).astype(jnp.float32) * (HEAD_DIM ** -0.5)
    qi = jnp.arange(CHUNK)
    kj = jnp.arange(BAND)
    rel = LEFT_CHUNKS * CHUNK + qi[:, None] - kj[None, :]
    idx = jnp.clip(rel, -MAX_REL, MAX_REL) + MAX_REL
    bias = rel_bias[:, idx].astype(jnp.float32)
    scores = scores + bias[None, :, None, :, :]
    key_chunk = jnp.arange(nc)[:, None] - LEFT_CHUNKS + (kj // CHUNK)[None, :]
    valid = key_chunk >= 0
    scores = jnp.where(valid[None, None, :, None, :], scores, NEG_INF)
    probs = jax.nn.softmax(scores, axis=-1).astype(v.dtype)
    out = jnp.einsum('bhcqk,bckhd->bcqhd', probs, vb)
    return out.reshape(b, s, D_ATTN)


def token_mixer(h, w_in, b_in, rel_bias, w_attn_o, w_dw, b_dw, g_ln, b_ln,
                w_conv_o, b_conv_o, w_mix_o):
    z = h @ w_in + b_in
    splits = list(np.cumsum(IN_WIDTHS)[:-1])
    q, k, v, glu_a, glu_b, gate_a, gate_b = jnp.split(z, splits, axis=-1)
    a = chunked_rel_attention(q, k, v, rel_bias) @ w_attn_o
    u = glu_a * jax.nn.sigmoid(glu_b)
    u = causal_dwconv(u, w_dw, b_dw)
    u = jax.nn.silu(layer_norm(u, g_ln, b_ln))
    cb = u @ w_conv_o + b_conv_o
    y = jax.nn.sigmoid(gate_a) * a + jax.nn.sigmoid(gate_b) * cb
    return y @ w_mix_o


def conv_ffn(h, w_up, w_dw, b_dw, w_down):
    u = causal_dwconv(h @ w_up, w_dw, b_dw)
    val, gt = jnp.split(u, 2, axis=-1)
    return (jax.nn.gelu(gt) * val) @ w_down


def _fwd_setup_inputs(seed: int = 0) -> dict:
    key = jax.random.key(seed)
    ks = jax.random.split(key, 26)
    L = DEPTH

    def nrm(k, shape, scale):
        return jax.random.normal(k, shape, jnp.float32) * scale

    def gain(k, shape):
        return 1.0 + 0.1 * jax.random.normal(k, shape, jnp.float32)

    return {
        "x": nrm(ks[0], (BATCH, SEQ, D_MODEL), 1.0),
        "c": nrm(ks[1], (BATCH, D_MODEL), 1.0),
        "w_ada": nrm(ks[2], (L, D_MODEL, 6 * D_MODEL), 0.5 * D_MODEL ** -0.5),
        "b_ada": nrm(ks[3], (L, 6 * D_MODEL), 0.01),
        "g_pre_mix": gain(ks[4], (L, D_MODEL)),
        "g_post_mix": gain(ks[5], (L, D_MODEL)),
        "w_in": nrm(ks[6], (L, D_MODEL, D_IN), D_MODEL ** -0.5),
        "b_in": nrm(ks[7], (L, D_IN), 0.01),
        "rel_bias": nrm(ks[8], (L, N_HEADS, 2 * MAX_REL + 1), 0.5),
        "w_attn_o": nrm(ks[9], (L, D_ATTN, D_MODEL), D_ATTN ** -0.5),
        "w_dw_conv": nrm(ks[10], (L, CONV_K, D_CONV), CONV_K ** -0.5),
        "b_dw_conv": nrm(ks[11], (L, D_CONV), 0.01),
        "g_conv_ln": gain(ks[12], (L, D_CONV)),
        "b_conv_ln": nrm(ks[13], (L, D_CONV), 0.01),
        "w_conv_o": nrm(ks[14], (L, D_CONV, D_MODEL), D_CONV ** -0.5),
        "b_conv_o": nrm(ks[15], (L, D_MODEL), 0.01),
        "w_mix_o": nrm(ks[16], (L, D_MODEL, D_MODEL), D_MODEL ** -0.5),
        "g_pre_ffn": gain(ks[17], (L, D_MODEL)),
        "g_post_ffn": gain(ks[18], (L, D_MODEL)),
        "w_up": nrm(ks[19], (L, D_MODEL, 2 * D_FF), D_MODEL ** -0.5),
        "w_dw_ffn": nrm(ks[20], (L, FFN_CONV_K, 2 * D_FF), FFN_CONV_K ** -0.5),
        "b_dw_ffn": nrm(ks[21], (L, 2 * D_FF), 0.01),
        "w_down": nrm(ks[22], (L, D_FF, D_MODEL), D_FF ** -0.5),
    }


def _fwd_reference(x, c, w_ada, b_ada, g_pre_mix, g_post_mix, w_in, b_in, rel_bias,
              w_attn_o, w_dw_conv, b_dw_conv, g_conv_ln, b_conv_ln, w_conv_o,
              b_conv_o, w_mix_o, g_pre_ffn, g_post_ffn, w_up, w_dw_ffn, b_dw_ffn,
              w_down):
    c_act = jax.nn.silu(c)
    for l in range(DEPTH):
        mod = c_act @ w_ada[l] + b_ada[l]
        sh_m, sc_m, gt_m, sh_f, sc_f, gt_f = [m[:, None, :] for m in jnp.split(mod, 6, axis=-1)]
        h = rms_norm(x, g_pre_mix[l]) * (1.0 + sc_m) + sh_m
        y = token_mixer(h, w_in[l], b_in[l], rel_bias[l], w_attn_o[l], w_dw_conv[l],
                        b_dw_conv[l], g_conv_ln[l], b_conv_ln[l], w_conv_o[l],
                        b_conv_o[l], w_mix_o[l])
        x = x + gt_m * rms_norm(y, g_post_mix[l])
        h = rms_norm(x, g_pre_ffn[l]) * (1.0 + sc_f) + sh_f
        y = conv_ffn(h, w_up[l], w_dw_ffn[l], b_dw_ffn[l], w_down[l])
        x = x + gt_f * rms_norm(y, g_post_ffn[l])
    return x


import jax as _jax
import jax.numpy as _jnp

TWIN_FORMAT = 'train_step'
FWD_PARAMS = ['x', 'c', 'w_ada', 'b_ada', 'g_pre_mix', 'g_post_mix', 'w_in', 'b_in', 'rel_bias', 'w_attn_o', 'w_dw_conv', 'b_dw_conv', 'g_conv_ln', 'b_conv_ln', 'w_conv_o', 'b_conv_o', 'w_mix_o', 'g_pre_ffn', 'g_post_ffn', 'w_up', 'w_dw_ffn', 'b_dw_ffn', 'w_down']
TWIN_WEIGHTS = ['w_ada', 'b_ada', 'g_pre_mix', 'g_post_mix', 'w_in', 'b_in', 'rel_bias', 'w_attn_o', 'w_dw_conv', 'b_dw_conv', 'g_conv_ln', 'b_conv_ln', 'w_conv_o', 'b_conv_o', 'w_mix_o', 'g_pre_ffn', 'g_post_ffn', 'w_up', 'w_dw_ffn', 'b_dw_ffn', 'w_down']
TWIN_DIFF_INPUT = 'x'
TWIN_INPUTS = ['x', 'c', 'w_ada', 'b_ada', 'g_pre_mix', 'g_post_mix', 'w_in', 'b_in', 'rel_bias', 'w_attn_o', 'w_dw_conv', 'b_dw_conv', 'g_conv_ln', 'b_conv_ln', 'w_conv_o', 'b_conv_o', 'w_mix_o', 'g_pre_ffn', 'g_post_ffn', 'w_up', 'w_dw_ffn', 'b_dw_ffn', 'w_down', 'loss_target', 'm_w_ada', 'm_b_ada', 'm_g_pre_mix', 'm_g_post_mix', 'm_w_in', 'm_b_in', 'm_rel_bias', 'm_w_attn_o', 'm_w_dw_conv', 'm_b_dw_conv', 'm_g_conv_ln', 'm_b_conv_ln', 'm_w_conv_o', 'm_b_conv_o', 'm_w_mix_o', 'm_g_pre_ffn', 'm_g_post_ffn', 'm_w_up', 'm_w_dw_ffn', 'm_b_dw_ffn', 'm_w_down', 'v_w_ada', 'v_b_ada', 'v_g_pre_mix', 'v_g_post_mix', 'v_w_in', 'v_b_in', 'v_rel_bias', 'v_w_attn_o', 'v_w_dw_conv', 'v_b_dw_conv', 'v_g_conv_ln', 'v_b_conv_ln', 'v_w_conv_o', 'v_b_conv_o', 'v_w_mix_o', 'v_g_pre_ffn', 'v_g_post_ffn', 'v_w_up', 'v_w_dw_ffn', 'v_b_dw_ffn', 'v_w_down']
TWIN_OUTPUTS = ['loss', 'grad_x', 'grad_w_ada', 'grad_b_ada', 'grad_g_pre_mix', 'grad_g_post_mix', 'grad_w_in', 'grad_b_in', 'grad_rel_bias', 'grad_w_attn_o', 'grad_w_dw_conv', 'grad_b_dw_conv', 'grad_g_conv_ln', 'grad_b_conv_ln', 'grad_w_conv_o', 'grad_b_conv_o', 'grad_w_mix_o', 'grad_g_pre_ffn', 'grad_g_post_ffn', 'grad_w_up', 'grad_w_dw_ffn', 'grad_b_dw_ffn', 'grad_w_down', 'delta_w_ada', 'delta_b_ada', 'delta_g_pre_mix', 'delta_g_post_mix', 'delta_w_in', 'delta_b_in', 'delta_rel_bias', 'delta_w_attn_o', 'delta_w_dw_conv', 'delta_b_dw_conv', 'delta_g_conv_ln', 'delta_b_conv_ln', 'delta_w_conv_o', 'delta_b_conv_o', 'delta_w_mix_o', 'delta_g_pre_ffn', 'delta_g_post_ffn', 'delta_w_up', 'delta_w_dw_ffn', 'delta_b_dw_ffn', 'delta_w_down', 'new_m_w_ada', 'new_m_b_ada', 'new_m_g_pre_mix', 'new_m_g_post_mix', 'new_m_w_in', 'new_m_b_in', 'new_m_rel_bias', 'new_m_w_attn_o', 'new_m_w_dw_conv', 'new_m_b_dw_conv', 'new_m_g_conv_ln', 'new_m_b_conv_ln', 'new_m_w_conv_o', 'new_m_b_conv_o', 'new_m_w_mix_o', 'new_m_g_pre_ffn', 'new_m_g_post_ffn', 'new_m_w_up', 'new_m_w_dw_ffn', 'new_m_b_dw_ffn', 'new_m_w_down', 'new_v_w_ada', 'new_v_b_ada', 'new_v_g_pre_mix', 'new_v_g_post_mix', 'new_v_w_in', 'new_v_b_in', 'new_v_rel_bias', 'new_v_w_attn_o', 'new_v_w_dw_conv', 'new_v_b_dw_conv', 'new_v_g_conv_ln', 'new_v_b_conv_ln', 'new_v_w_conv_o', 'new_v_b_conv_o', 'new_v_w_mix_o', 'new_v_g_pre_ffn', 'new_v_g_post_ffn', 'new_v_w_up', 'new_v_w_dw_ffn', 'new_v_b_dw_ffn', 'new_v_w_down']
TWIN_LEAF_KINDS = {'loss': 'loss', 'grad_x': 'grad_x', 'grad_w_ada': 'grad_w', 'grad_b_ada': 'grad_w', 'grad_g_pre_mix': 'grad_w', 'grad_g_post_mix': 'grad_w', 'grad_w_in': 'grad_w', 'grad_b_in': 'grad_w', 'grad_rel_bias': 'grad_w', 'grad_w_attn_o': 'grad_w', 'grad_w_dw_conv': 'grad_w', 'grad_b_dw_conv': 'grad_w', 'grad_g_conv_ln': 'grad_w', 'grad_b_conv_ln': 'grad_w', 'grad_w_conv_o': 'grad_w', 'grad_b_conv_o': 'grad_w', 'grad_w_mix_o': 'grad_w', 'grad_g_pre_ffn': 'grad_w', 'grad_g_post_ffn': 'grad_w', 'grad_w_up': 'grad_w', 'grad_w_dw_ffn': 'grad_w', 'grad_b_dw_ffn': 'grad_w', 'grad_w_down': 'grad_w', 'delta_w_ada': 'delta_w', 'delta_b_ada': 'delta_w', 'delta_g_pre_mix': 'delta_w', 'delta_g_post_mix': 'delta_w', 'delta_w_in': 'delta_w', 'delta_b_in': 'delta_w', 'delta_rel_bias': 'delta_w', 'delta_w_attn_o': 'delta_w', 'delta_w_dw_conv': 'delta_w', 'delta_b_dw_conv': 'delta_w', 'delta_g_conv_ln': 'delta_w', 'delta_b_conv_ln': 'delta_w', 'delta_w_conv_o': 'delta_w', 'delta_b_conv_o': 'delta_w', 'delta_w_mix_o': 'delta_w', 'delta_g_pre_ffn': 'delta_w', 'delta_g_post_ffn': 'delta_w', 'delta_w_up': 'delta_w', 'delta_w_dw_ffn': 'delta_w', 'delta_b_dw_ffn': 'delta_w', 'delta_w_down': 'delta_w', 'new_m_w_ada': 'new_m', 'new_m_b_ada': 'new_m', 'new_m_g_pre_mix': 'new_m', 'new_m_g_post_mix': 'new_m', 'new_m_w_in': 'new_m', 'new_m_b_in': 'new_m', 'new_m_rel_bias': 'new_m', 'new_m_w_attn_o': 'new_m', 'new_m_w_dw_conv': 'new_m', 'new_m_b_dw_conv': 'new_m', 'new_m_g_conv_ln': 'new_m', 'new_m_b_conv_ln': 'new_m', 'new_m_w_conv_o': 'new_m', 'new_m_b_conv_o': 'new_m', 'new_m_w_mix_o': 'new_m', 'new_m_g_pre_ffn': 'new_m', 'new_m_g_post_ffn': 'new_m', 'new_m_w_up': 'new_m', 'new_m_w_dw_ffn': 'new_m', 'new_m_b_dw_ffn': 'new_m', 'new_m_w_down': 'new_m', 'new_v_w_ada': 'new_v', 'new_v_b_ada': 'new_v', 'new_v_g_pre_mix': 'new_v', 'new_v_g_post_mix': 'new_v', 'new_v_w_in': 'new_v', 'new_v_b_in': 'new_v', 'new_v_rel_bias': 'new_v', 'new_v_w_attn_o': 'new_v', 'new_v_w_dw_conv': 'new_v', 'new_v_b_dw_conv': 'new_v', 'new_v_g_conv_ln': 'new_v', 'new_v_b_conv_ln': 'new_v', 'new_v_w_conv_o': 'new_v', 'new_v_b_conv_o': 'new_v', 'new_v_w_mix_o': 'new_v', 'new_v_g_pre_ffn': 'new_v', 'new_v_g_post_ffn': 'new_v', 'new_v_w_up': 'new_v', 'new_v_w_dw_ffn': 'new_v', 'new_v_b_dw_ffn': 'new_v', 'new_v_w_down': 'new_v'}


def _forward(args):
    return _fwd_reference(*[args[k] for k in FWD_PARAMS])


def _output_shape():
    out = _jax.eval_shape(lambda: _forward(_fwd_setup_inputs(0)))
    return out.shape, out.dtype

N_MICROBATCH = 1
ADAM_LR = 0.001
ADAM_B1 = 0.9
ADAM_B2 = 0.999
ADAM_EPS = 1e-08
ADAM_WD = 0.01
ADAM_STEP = 10
PER_EXAMPLE_BATCH_AXIS = {'x': 0, 'c': 0, 'loss_target': 0}
SHARED_INPUTS = []
_WEIGHT_DTYPES = {'w_ada': _jnp.float32, 'b_ada': _jnp.float32, 'g_pre_mix': _jnp.float32, 'g_post_mix': _jnp.float32, 'w_in': _jnp.float32, 'b_in': _jnp.float32, 'rel_bias': _jnp.float32, 'w_attn_o': _jnp.float32, 'w_dw_conv': _jnp.float32, 'b_dw_conv': _jnp.float32, 'g_conv_ln': _jnp.float32, 'b_conv_ln': _jnp.float32, 'w_conv_o': _jnp.float32, 'b_conv_o': _jnp.float32, 'w_mix_o': _jnp.float32, 'g_pre_ffn': _jnp.float32, 'g_post_ffn': _jnp.float32, 'w_up': _jnp.float32, 'w_dw_ffn': _jnp.float32, 'b_dw_ffn': _jnp.float32, 'w_down': _jnp.float32}
MOMENT_SCALE = {'w_ada': 1.932099e+00, 'b_ada': 3.654610e+00, 'g_pre_mix': 1.087519e-01, 'g_post_mix': 3.889399e+00, 'w_in': 1.935222e-01, 'b_in': 7.821823e-01, 'rel_bias': 1.547802e-02, 'w_attn_o': 4.339620e-01, 'w_dw_conv': 2.668083e-01, 'b_dw_conv': 1.521294e+00, 'g_conv_ln': 6.863439e-01, 'b_conv_ln': 1.002119e+00, 'w_conv_o': 3.200874e-01, 'b_conv_o': 1.456504e+00, 'w_mix_o': 5.575702e-01, 'g_pre_ffn': 1.404417e-01, 'g_post_ffn': 3.811761e+00, 'w_up': 6.975953e-02, 'w_dw_ffn': 8.038441e-02, 'b_dw_ffn': 1.694338e-01, 'w_down': 1.410757e-01}


def _to_microbatches(a, axis):
    t = _jnp.moveaxis(a, axis, 0)
    t = t.reshape((N_MICROBATCH, t.shape[0] // N_MICROBATCH) + t.shape[1:])
    return _jnp.moveaxis(t, 1, axis + 1)


def setup_inputs(seed: int = 0) -> dict:
    inp = _fwd_setup_inputs(seed)
    key = _jax.random.fold_in(_jax.random.key(seed), 7919)
    shape, _ = _output_shape()
    out = dict(inp)
    out["loss_target"] = _jax.random.normal(_jax.random.fold_in(key, 0), shape, _jnp.float32)
    for i, name in enumerate(TWIN_WEIGHTS):
        w = inp[name].astype(_jnp.float32)
        if MOMENT_SCALE is None:
            s = _jnp.sqrt(_jnp.mean(_jnp.square(w)) + 1e-30)
        else:
            s = MOMENT_SCALE[name]
        km, kv = _jax.random.split(_jax.random.fold_in(key, i + 1))
        out[name] = w
        out["m_" + name] = s * _jax.random.normal(km, w.shape, _jnp.float32)
        out["v_" + name] = (s * s) * _jax.random.uniform(kv, w.shape, _jnp.float32, 0.5, 1.5)
    if N_MICROBATCH > 1:
        for name, axis in PER_EXAMPLE_BATCH_AXIS.items():
            out[name] = _to_microbatches(out[name], axis)
    return {'x': out['x'], 'c': out['c'], 'w_ada': out['w_ada'], 'b_ada': out['b_ada'], 'g_pre_mix': out['g_pre_mix'], 'g_post_mix': out['g_post_mix'], 'w_in': out['w_in'], 'b_in': out['b_in'], 'rel_bias': out['rel_bias'], 'w_attn_o': out['w_attn_o'], 'w_dw_conv': out['w_dw_conv'], 'b_dw_conv': out['b_dw_conv'], 'g_conv_ln': out['g_conv_ln'], 'b_conv_ln': out['b_conv_ln'], 'w_conv_o': out['w_conv_o'], 'b_conv_o': out['b_conv_o'], 'w_mix_o': out['w_mix_o'], 'g_pre_ffn': out['g_pre_ffn'], 'g_post_ffn': out['g_post_ffn'], 'w_up': out['w_up'], 'w_dw_ffn': out['w_dw_ffn'], 'b_dw_ffn': out['b_dw_ffn'], 'w_down': out['w_down'], 'loss_target': out['loss_target'], 'm_w_ada': out['m_w_ada'], 'm_b_ada': out['m_b_ada'], 'm_g_pre_mix': out['m_g_pre_mix'], 'm_g_post_mix': out['m_g_post_mix'], 'm_w_in': out['m_w_in'], 'm_b_in': out['m_b_in'], 'm_rel_bias': out['m_rel_bias'], 'm_w_attn_o': out['m_w_attn_o'], 'm_w_dw_conv': out['m_w_dw_conv'], 'm_b_dw_conv': out['m_b_dw_conv'], 'm_g_conv_ln': out['m_g_conv_ln'], 'm_b_conv_ln': out['m_b_conv_ln'], 'm_w_conv_o': out['m_w_conv_o'], 'm_b_conv_o': out['m_b_conv_o'], 'm_w_mix_o': out['m_w_mix_o'], 'm_g_pre_ffn': out['m_g_pre_ffn'], 'm_g_post_ffn': out['m_g_post_ffn'], 'm_w_up': out['m_w_up'], 'm_w_dw_ffn': out['m_w_dw_ffn'], 'm_b_dw_ffn': out['m_b_dw_ffn'], 'm_w_down': out['m_w_down'], 'v_w_ada': out['v_w_ada'], 'v_b_ada': out['v_b_ada'], 'v_g_pre_mix': out['v_g_pre_mix'], 'v_g_post_mix': out['v_g_post_mix'], 'v_w_in': out['v_w_in'], 'v_b_in': out['v_b_in'], 'v_rel_bias': out['v_rel_bias'], 'v_w_attn_o': out['v_w_attn_o'], 'v_w_dw_conv': out['v_w_dw_conv'], 'v_b_dw_conv': out['v_b_dw_conv'], 'v_g_conv_ln': out['v_g_conv_ln'], 'v_b_conv_ln': out['v_b_conv_ln'], 'v_w_conv_o': out['v_w_conv_o'], 'v_b_conv_o': out['v_b_conv_o'], 'v_w_mix_o': out['v_w_mix_o'], 'v_g_pre_ffn': out['v_g_pre_ffn'], 'v_g_post_ffn': out['v_g_post_ffn'], 'v_w_up': out['v_w_up'], 'v_w_dw_ffn': out['v_w_dw_ffn'], 'v_b_dw_ffn': out['v_b_dw_ffn'], 'v_w_down': out['v_w_down']}


def _loss(weights, diff, rest, loss_target):
    with _jax.named_scope("forward"):
        args = {**rest, TWIN_DIFF_INPUT: diff, **{k: w.astype(_WEIGHT_DTYPES[k]) for k, w in weights.items()}}
        y = _forward(args)
    with _jax.named_scope("loss_head"):
        err = _jnp.square(y.astype(_jnp.float32) - loss_target)
        return 0.5 * _jnp.sum(_jnp.mean(err, axis=-1)) if err.ndim else 0.5 * err


def _adamw(w, g, m, v):
    m = ADAM_B1 * m + (1.0 - ADAM_B1) * g
    v = ADAM_B2 * v + (1.0 - ADAM_B2) * _jnp.square(g)
    m_hat = m / (1.0 - ADAM_B1 ** ADAM_STEP)
    v_hat = v / (1.0 - ADAM_B2 ** ADAM_STEP)
    delta = -ADAM_LR * (m_hat / (_jnp.sqrt(v_hat) + ADAM_EPS) + ADAM_WD * w)
    return delta, m, v


def reference(x, c, w_ada, b_ada, g_pre_mix, g_post_mix, w_in, b_in, rel_bias, w_attn_o, w_dw_conv, b_dw_conv, g_conv_ln, b_conv_ln, w_conv_o, b_conv_o, w_mix_o, g_pre_ffn, g_post_ffn, w_up, w_dw_ffn, b_dw_ffn, w_down, loss_target, m_w_ada, m_b_ada, m_g_pre_mix, m_g_post_mix, m_w_in, m_b_in, m_rel_bias, m_w_attn_o, m_w_dw_conv, m_b_dw_conv, m_g_conv_ln, m_b_conv_ln, m_w_conv_o, m_b_conv_o, m_w_mix_o, m_g_pre_ffn, m_g_post_ffn, m_w_up, m_w_dw_ffn, m_b_dw_ffn, m_w_down, v_w_ada, v_b_ada, v_g_pre_mix, v_g_post_mix, v_w_in, v_b_in, v_rel_bias, v_w_attn_o, v_w_dw_conv, v_b_dw_conv, v_g_conv_ln, v_b_conv_ln, v_w_conv_o, v_b_conv_o, v_w_mix_o, v_g_pre_ffn, v_g_post_ffn, v_w_up, v_w_dw_ffn, v_b_dw_ffn, v_w_down):
    given = dict(x=x, c=c, w_ada=w_ada, b_ada=b_ada, g_pre_mix=g_pre_mix, g_post_mix=g_post_mix, w_in=w_in, b_in=b_in, rel_bias=rel_bias, w_attn_o=w_attn_o, w_dw_conv=w_dw_conv, b_dw_conv=b_dw_conv, g_conv_ln=g_conv_ln, b_conv_ln=b_conv_ln, w_conv_o=w_conv_o, b_conv_o=b_conv_o, w_mix_o=w_mix_o, g_pre_ffn=g_pre_ffn, g_post_ffn=g_post_ffn, w_up=w_up, w_dw_ffn=w_dw_ffn, b_dw_ffn=b_dw_ffn, w_down=w_down, loss_target=loss_target, m_w_ada=m_w_ada, m_b_ada=m_b_ada, m_g_pre_mix=m_g_pre_mix, m_g_post_mix=m_g_post_mix, m_w_in=m_w_in, m_b_in=m_b_in, m_rel_bias=m_rel_bias, m_w_attn_o=m_w_attn_o, m_w_dw_conv=m_w_dw_conv, m_b_dw_conv=m_b_dw_conv, m_g_conv_ln=m_g_conv_ln, m_b_conv_ln=m_b_conv_ln, m_w_conv_o=m_w_conv_o, m_b_conv_o=m_b_conv_o, m_w_mix_o=m_w_mix_o, m_g_pre_ffn=m_g_pre_ffn, m_g_post_ffn=m_g_post_ffn, m_w_up=m_w_up, m_w_dw_ffn=m_w_dw_ffn, m_b_dw_ffn=m_b_dw_ffn, m_w_down=m_w_down, v_w_ada=v_w_ada, v_b_ada=v_b_ada, v_g_pre_mix=v_g_pre_mix, v_g_post_mix=v_g_post_mix, v_w_in=v_w_in, v_b_in=v_b_in, v_rel_bias=v_rel_bias, v_w_attn_o=v_w_attn_o, v_w_dw_conv=v_w_dw_conv, v_b_dw_conv=v_b_dw_conv, v_g_conv_ln=v_g_conv_ln, v_b_conv_ln=v_b_conv_ln, v_w_conv_o=v_w_conv_o, v_b_conv_o=v_b_conv_o, v_w_mix_o=v_w_mix_o, v_g_pre_ffn=v_g_pre_ffn, v_g_post_ffn=v_g_post_ffn, v_w_up=v_w_up, v_w_dw_ffn=v_w_dw_ffn, v_b_dw_ffn=v_b_dw_ffn, v_w_down=v_w_down)
    weights = {n: given[n] for n in TWIN_WEIGHTS}
    shared = {n: given[n] for n in SHARED_INPUTS}
    per_example = {n: given[n] for n in ['x', 'c']}
    grad_fn = _jax.value_and_grad(_loss, argnums=(0, 1))

    def one_microbatch(ex, loss_target):
        ex = dict(ex)
        diff = ex.pop(TWIN_DIFF_INPUT)
        return grad_fn(weights, diff, {**shared, **ex}, loss_target)

    if N_MICROBATCH == 1:
        loss, (grad_w, grad_x) = one_microbatch(per_example, given["loss_target"])
    else:
        def body(carry, xs):
            loss_sum, grad_sum = carry
            l_k, (gw_k, gx_k) = one_microbatch(xs[0], xs[1])
            with _jax.named_scope("update"):
                return (loss_sum + l_k, _jax.tree.map(_jnp.add, grad_sum, gw_k)), gx_k

        init = (_jnp.zeros((), _jnp.float32), _jax.tree.map(_jnp.zeros_like, weights))
        (loss, grad_w), grad_x = _jax.lax.scan(body, init, (per_example, given["loss_target"]))
    with _jax.named_scope("update"):
        delta_w, new_m, new_v = {}, {}, {}
        for n in TWIN_WEIGHTS:
            delta_w[n], new_m[n], new_v[n] = _adamw(weights[n], grad_w[n], given["m_" + n], given["v_" + n])
    return (loss, grad_x, *[grad_w[n] for n in TWIN_WEIGHTS], *[delta_w[n] for n in TWIN_WEIGHTS],
            *[new_m[n] for n in TWIN_WEIGHTS], *[new_v[n] for n in TWIN_WEIGHTS])
```

```python
import math

import jax
import jax.numpy as jnp
from jax import lax
from jax.experimental import pallas as pl
from jax.experimental.pallas import tpu as pltpu

F32 = jnp.float32
BF16 = jnp.bfloat16
HIGHEST = lax.Precision.HIGHEST

D_MODEL = 1024
CHUNK = 64
LEFT_CHUNKS = 8
BAND = (LEFT_CHUNKS + 1) * CHUNK
PAD_ROWS = LEFT_CHUNKS * CHUNK
N_HEADS = 8
HEAD_DIM = 64
D_ATTN = 512
D_CONV = 512
CONV_K = 31
CONV_HALO = 32
MAX_REL = 128
N_REL = 2 * MAX_REL + 1
D_FF = 2816
FFN_HALO = 8
FFN_COLS = 256
EPS = 1e-6
NEG_INF = -1e30
N_DEV = 8

ADAM_LR = 0.001
ADAM_B1 = 0.9
ADAM_B2 = 0.999
ADAM_EPS = 1e-08
ADAM_WD = 0.01
ADAM_STEP = 10

VMEM_LIMIT_BYTES = 56 * 1024 * 1024
ADAMW_BLOCK_BYTES = 768 * 1024

MESH = pl.DeviceIdType.MESH
ANY = pl.BlockSpec(memory_space=pl.ANY)

SH_M, SC_M, GT_M, SH_F, SC_F, GT_F = range(6)

SMALL = (("b_ada", 6144), ("g_pre_mix", 1024), ("g_post_mix", 1024), ("b_in", 4608),
         ("rel_bias", 2176), ("b_dw_conv", 512), ("g_conv_ln", 512), ("b_conv_ln", 512),
         ("b_conv_o", 1024), ("g_pre_ffn", 1024), ("g_post_ffn", 1024), ("b_dw_ffn", 5632))
SMALL_TOTAL = sum(n for _, n in SMALL)


def _cparams(n_axes):
    return pltpu.CompilerParams(vmem_limit_bytes=VMEM_LIMIT_BYTES,
                                dimension_semantics=("arbitrary",) * n_axes)


def _sig(v):
    return 1.0 / (1.0 + jnp.exp(-v))


def _pick(n, target):
    if n <= target:
        return n
    t = target - target % 128
    while n % t:
        t -= 128
    return t


def _tile(rows, cols, col=0):
    return pl.BlockSpec((rows, cols), lambda i: (i, col))


def _full(shape):
    zeros = (0,) * len(shape)
    return pl.BlockSpec(shape, lambda i: zeros)


def _prev(halo, cols, rows, col=0):
    return pl.BlockSpec((halo, cols), lambda i: (jnp.maximum(i * (rows // halo) - 1, 0), col))


def _next(halo, cols, rows, n_blocks, col=0):
    return pl.BlockSpec((halo, cols), lambda i: (jnp.minimum((i + 1) * (rows // halo), n_blocks - 1), col))


_DIMS = {"nn": (((1,), (0,)), ((), ())), "nt": (((1,), (1,)), ((), ())), "tn": (((0,), (0,)), ((), ()))}


def _mm(a, b, mode, out_dtype, name, bias=None, tm=1024, tn=512, tk=1024):
    assert a.dtype == BF16 and b.dtype == BF16
    if mode == "tn":
        k_dim, m_dim = a.shape
    else:
        m_dim, k_dim = a.shape
    n_dim = b.shape[0] if mode == "nt" else b.shape[1]
    tm, tn, tk = _pick(m_dim, tm), _pick(n_dim, tn), _pick(k_dim, tk)
    nk = k_dim // tk
    a_spec = (pl.BlockSpec((tk, tm), lambda i, j, k: (k, i)) if mode == "tn"
              else pl.BlockSpec((tm, tk), lambda i, j, k: (i, k)))
    b_spec = (pl.BlockSpec((tn, tk), lambda i, j, k: (j, k)) if mode == "nt"
              else pl.BlockSpec((tk, tn), lambda i, j, k: (k, j)))
    in_specs = [a_spec, b_spec]
    args = [a, b]
    if bias is not None:
        in_specs.append(pl.BlockSpec((1, tn), lambda i, j, k: (0, j)))
        args.append(bias)
    dims = _DIMS[mode]

    def body(*refs):
        a_ref, b_ref = refs[0], refs[1]
        bias_ref = refs[2] if bias is not None else None
        o_ref = refs[3] if bias is not None else refs[2]
        part = lax.dot_general(a_ref[...], b_ref[...], dims, preferred_element_type=F32)

        def finish(total):
            if bias_ref is not None:
                total = total + bias_ref[...]
            o_ref[...] = total.astype(out_dtype)

        if nk == 1:
            finish(part)
        else:
            acc_ref = refs[-1]
            k = pl.program_id(2)

            @pl.when(k == 0)
            def _():
                acc_ref[...] = part

            @pl.when(k > 0)
            def _():
                acc_ref[...] += part

            @pl.when(k == nk - 1)
            def _():
                finish(acc_ref[...])

    return pl.pallas_call(
        body, name=name,
        out_shape=jax.ShapeDtypeStruct((m_dim, n_dim), out_dtype),
        grid=(m_dim // tm, n_dim // tn, nk),
        in_specs=in_specs,
        out_specs=pl.BlockSpec((tm, tn), lambda i, j, k: (i, j)),
        scratch_shapes=[pltpu.VMEM((tm, tn), F32)] if nk > 1 else [],
        compiler_params=_cparams(3),
    )(*args)


def _place():
    return lax.axis_index("x"), lax.axis_index("y"), lax.axis_index("c")


def _all_gather(arrs, name):
    n = len(arrs)

    def body(*refs):
        ins, outs = refs[:n], refs[n:2 * n]
        send_sems, recv_sems, local_sems = refs[2 * n:]
        x, y, c = _place()
        me, sibling = (x, y, c), (x, y, 1 - c)
        chips = [(1 - x, y), (x, 1 - y), (1 - x, 1 - y)]

        def block(k, p):
            return outs[k].at[4 * p[0] + 2 * p[1] + p[2]]

        def copy(k, s, blk, to, src=None):
            return pltpu.make_async_remote_copy(
                src_ref=block(k, blk) if src is None else src, dst_ref=block(k, blk),
                send_sem=send_sems.at[7 * k + s], recv_sem=recv_sems.at[7 * k + s],
                device_id=to, device_id_type=MESH)

        mine = [pltpu.make_async_copy(ins[k], block(k, me), local_sems.at[k]) for k in range(n)]
        for cp in mine:
            cp.start()
        first = []
        for k in range(n):
            first.append(copy(k, 0, me, sibling, src=ins[k]))
            for j, chip in enumerate(chips):
                first.append(copy(k, 1 + j, me, (*chip, c), src=ins[k]))
        for cp in first:
            cp.start()
        passed = []
        for j, chip in enumerate(chips):
            for k in range(n):
                copy(k, 1 + j, (*chip, c), me).wait_recv()
                fwd = copy(k, 4 + j, (*chip, c), sibling)
                fwd.start()
                passed.append(fwd)
        for k in range(n):
            copy(k, 0, sibling, me).wait_recv()
        for j, chip in enumerate(chips):
            for k in range(n):
                copy(k, 4 + j, (*chip, 1 - c), me).wait_recv()
        for cp in first + passed:
            cp.wait_send()
        for cp in mine:
            cp.wait()

    return pl.pallas_call(
        body, name=name,
        out_shape=[jax.ShapeDtypeStruct((N_DEV,) + a.shape, a.dtype) for a in arrs],
        in_specs=[ANY] * n, out_specs=[ANY] * n,
        scratch_shapes=[pltpu.SemaphoreType.DMA((7 * n,)), pltpu.SemaphoreType.DMA((7 * n,)),
                        pltpu.SemaphoreType.DMA((n,))],
    )(*arrs)


def _sibling_exchange(arrs, name):
    n = len(arrs)

    def body(*refs):
        ins, outs = refs[:n], refs[n:2 * n]
        send_sems, recv_sems = refs[2 * n:]
        x, y, c = _place()
        copies = [pltpu.make_async_remote_copy(
            src_ref=ins[k].at[1 - c], dst_ref=outs[k], send_sem=send_sems.at[k], recv_sem=recv_sems.at[k],
            device_id=(x, y, 1 - c), device_id_type=MESH) for k in range(n)]
        for cp in copies:
            cp.start()
        for cp in copies:
            cp.wait()

    return pl.pallas_call(
        body, name=name,
        out_shape=[jax.ShapeDtypeStruct(a.shape[1:], a.dtype) for a in arrs],
        in_specs=[ANY] * n, out_specs=[ANY] * n,
        scratch_shapes=[pltpu.SemaphoreType.DMA((n,)), pltpu.SemaphoreType.DMA((n,))],
    )(*arrs)


def _chip_exchange(arrs, name):
    n = len(arrs)

    def body(*refs):
        ins, outs = refs[:n], refs[n:2 * n]
        send_sems, recv_sems, local_sems = refs[2 * n:]
        x, y, c = _place()
        mine = 2 * x + y
        chips = [(1 - x, y), (x, 1 - y), (1 - x, 1 - y)]
        local = [pltpu.make_async_copy(ins[k].at[mine], outs[k].at[mine], local_sems.at[k]) for k in range(n)]
        for cp in local:
            cp.start()
        sends = []
        for k in range(n):
            for j, (px, py) in enumerate(chips):
                sends.append(pltpu.make_async_remote_copy(
                    src_ref=ins[k].at[2 * px + py], dst_ref=outs[k].at[mine],
                    send_sem=send_sems.at[3 * k + j], recv_sem=recv_sems.at[3 * k + j],
                    device_id=(px, py, c), device_id_type=MESH))
        for cp in sends:
            cp.start()
        for k in range(n):
            for j, (px, py) in enumerate(chips):
                pltpu.make_async_remote_copy(
                    src_ref=ins[k].at[mine], dst_ref=outs[k].at[2 * px + py],
                    send_sem=send_sems.at[3 * k + j], recv_sem=recv_sems.at[3 * k + j],
                    device_id=(px, py, c), device_id_type=MESH).wait_recv()
        for cp in sends:
            cp.wait_send()
        for cp in local:
            cp.wait()

    return pl.pallas_call(
        body, name=name,
        out_shape=[jax.ShapeDtypeStruct(a.shape, a.dtype) for a in arrs],
        in_specs=[ANY] * n, out_specs=[ANY] * n,
        scratch_shapes=[pltpu.SemaphoreType.DMA((3 * n,)), pltpu.SemaphoreType.DMA((3 * n,)),
                        pltpu.SemaphoreType.DMA((n,))],
    )(*arrs)


def _pair_sum(own, got, name):
    _, r, c = own.shape

    def body(a_ref, b_ref, o_ref):
        o_ref[...] = (a_ref[...].astype(F32) + b_ref[...].astype(F32)).astype(BF16)

    spec = pl.BlockSpec((1, r, c), lambda i: (i, 0, 0))
    return pl.pallas_call(
        body, name=name, out_shape=jax.ShapeDtypeStruct(own.shape, BF16), grid=(4,),
        in_specs=[spec, spec], out_specs=spec, compiler_params=_cparams(1),
    )(own, got)


def _adam_math(w, g, m, v):
    m = ADAM_B1 * m + (1.0 - ADAM_B1) * g
    v = ADAM_B2 * v + (1.0 - ADAM_B2) * (g * g)
    m_hat = m / (1.0 - ADAM_B1 ** ADAM_STEP)
    v_hat = v / (1.0 - ADAM_B2 ** ADAM_STEP)
    delta = -ADAM_LR * (m_hat / (jnp.sqrt(v_hat) + ADAM_EPS) + ADAM_WD * w)
    return delta, m, v


def _adamw(w, m, v, name, g=None, parts=None):
    rows, cols = w.shape
    tr = rows
    if rows * cols * 4 > ADAMW_BLOCK_BYTES:
        tr = max(t for t in range(16, rows, 16) if rows % t == 0 and t * cols * 4 <= ADAMW_BLOCK_BYTES)

    def body(w_ref, m_ref, v_ref, g_ref, go_ref, d_ref, mo_ref, vo_ref):
        if parts is None:
            grad = g_ref[...]
        else:
            grad = ((g_ref[0].astype(F32) + g_ref[1].astype(F32)) + g_ref[2].astype(F32)) + g_ref[3].astype(F32)
        delta, m_new, v_new = _adam_math(w_ref[...], grad, m_ref[...], v_ref[...])
        go_ref[...] = grad
        d_ref[...] = delta
        mo_ref[...] = m_new
        vo_ref[...] = v_new

    spec = _tile(tr, cols)
    g_spec = spec if parts is None else pl.BlockSpec((4, tr, cols), lambda i: (0, i, 0))
    shape = jax.ShapeDtypeStruct((rows, cols), F32)
    return pl.pallas_call(
        body, name=name, out_shape=[shape] * 4, grid=(rows // tr,),
        in_specs=[spec, spec, spec, g_spec], out_specs=[spec] * 4, compiler_params=_cparams(1),
    )(w, m, v, g if parts is None else parts)


def _small_adamw(gathered, w, m, v):
    total = gathered.shape[1]

    def body(g_ref, w_ref, m_ref, v_ref, go_ref, d_ref, mo_ref, vo_ref):
        grad = g_ref[0:1, :]
        for d in range(1, N_DEV):
            grad = grad + g_ref[d:d + 1, :]
        go_ref[...] = grad
        delta, m_new, v_new = _adam_math(w_ref[...], grad[:, :SMALL_TOTAL], m_ref[...], v_ref[...])
        d_ref[...] = delta
        mo_ref[...] = m_new
        vo_ref[...] = v_new

    small = jax.ShapeDtypeStruct((1, SMALL_TOTAL), F32)
    return pl.pallas_call(
        body, name="small_adamw",
        out_shape=[jax.ShapeDtypeStruct((1, total), F32), small, small, small],
        compiler_params=pltpu.CompilerParams(vmem_limit_bytes=VMEM_LIMIT_BYTES),
    )(gathered, w, m, v)


def _silu_vec(c):
    def body(c_ref, o_ref):
        v = c_ref[...]
        o_ref[...] = v * _sig(v)

    return pl.pallas_call(body, name="silu_c", out_shape=jax.ShapeDtypeStruct(c.shape, F32))(c)


def _ada_fwd(c_all, w_shard):
    def body(c_ref, w_ref, o_ref):
        o_ref[...] = jnp.dot(c_ref[...], w_ref[...], precision=HIGHEST, preferred_element_type=F32)

    return pl.pallas_call(
        body, name="ada_fwd", out_shape=jax.ShapeDtypeStruct((N_DEV, w_shard.shape[1]), F32),
        compiler_params=pltpu.CompilerParams(vmem_limit_bytes=VMEM_LIMIT_BYTES),
    )(c_all, w_shard)


def _ada_grad(c_all, dmod_shard):
    def body(c_ref, d_ref, o_ref):
        o_ref[...] = lax.dot_general(c_ref[...], d_ref[...], _DIMS["tn"], precision=HIGHEST,
                                     preferred_element_type=F32)

    return pl.pallas_call(
        body, name="ada_grad", out_shape=jax.ShapeDtypeStruct((D_MODEL, dmod_shard.shape[1]), F32),
        compiler_params=pltpu.CompilerParams(vmem_limit_bytes=VMEM_LIMIT_BYTES),
    )(c_all, dmod_shard)


ROWS = 256


def _rms(v):
    r = lax.rsqrt(jnp.mean(v * v, axis=-1, keepdims=True) + EPS)
    return v * r, r


def _rms_bwd(dxn, xn, r):
    return r * (dxn - xn * jnp.mean(dxn * xn, axis=-1, keepdims=True))


def _colsum(v):
    return jnp.sum(v, axis=0, keepdims=True)


def _pre_mix(x, mod6, g1):
    seq = x.shape[0]

    def body(x_ref, mod_ref, g_ref, h_ref):
        xn, _ = _rms(x_ref[...])
        y = xn * g_ref[...]
        h_ref[...] = (y * (1.0 + mod_ref[SC_M:SC_M + 1, :]) + mod_ref[SH_M:SH_M + 1, :]).astype(BF16)

    return pl.pallas_call(
        body, name="pre_mix", out_shape=jax.ShapeDtypeStruct((seq, D_MODEL), BF16), grid=(seq // ROWS,),
        in_specs=[_tile(ROWS, D_MODEL), _full((6, D_MODEL)), _full((1, D_MODEL))],
        out_specs=_tile(ROWS, D_MODEL), compiler_params=_cparams(1),
    )(x, mod6, g1)


def _post_mix_pre_ffn(ymix, x, mod6, g2, g3):
    seq = x.shape[0]

    def body(y_ref, x_ref, mod_ref, g2_ref, g3_ref, x1_ref, h_ref):
        yn, _ = _rms(y_ref[...])
        x1 = x_ref[...] + mod_ref[GT_M:GT_M + 1, :] * (yn * g2_ref[...])
        x1_ref[...] = x1
        xn, _ = _rms(x1)
        y3 = xn * g3_ref[...]
        h_ref[...] = (y3 * (1.0 + mod_ref[SC_F:SC_F + 1, :]) + mod_ref[SH_F:SH_F + 1, :]).astype(BF16)

    return pl.pallas_call(
        body, name="post_mix_pre_ffn",
        out_shape=[jax.ShapeDtypeStruct((seq, D_MODEL), F32), jax.ShapeDtypeStruct((seq, D_MODEL), BF16)],
        grid=(seq // ROWS,),
        in_specs=[_tile(ROWS, D_MODEL), _tile(ROWS, D_MODEL), _full((6, D_MODEL)), _full((1, D_MODEL)),
                  _full((1, D_MODEL))],
        out_specs=[_tile(ROWS, D_MODEL), _tile(ROWS, D_MODEL)], compiler_params=_cparams(1),
    )(ymix, x, mod6, g2, g3)


def _final(yf, x1, target, mod6, g4):
    seq = x1.shape[0]

    def body(y_ref, x1_ref, t_ref, mod_ref, g_ref, loss_ref, dout_ref, dyf_ref, small_ref):
        i = pl.program_id(0)

        @pl.when(i == 0)
        def _():
            loss_ref[...] = jnp.zeros_like(loss_ref)
            small_ref[...] = jnp.zeros_like(small_ref)

        gt = mod_ref[GT_F:GT_F + 1, :]
        g4v = g_ref[...]
        yn, r = _rms(y_ref[...])
        out = x1_ref[...] + gt * (yn * g4v)
        err = out - t_ref[...]
        loss_ref[...] += jnp.sum(jnp.mean(err * err, axis=-1, keepdims=True))
        dout = err * (1.0 / D_MODEL)
        dout_ref[...] = dout
        small_ref[0:1, :] += _colsum(dout * gt * yn)
        small_ref[1:2, :] += _colsum(dout * (yn * g4v))
        dyf_ref[...] = _rms_bwd(dout * gt * g4v, yn, r).astype(BF16)

    return pl.pallas_call(
        body, name="final_loss",
        out_shape=[jax.ShapeDtypeStruct((1, 128), F32), jax.ShapeDtypeStruct((seq, D_MODEL), F32),
                   jax.ShapeDtypeStruct((seq, D_MODEL), BF16), jax.ShapeDtypeStruct((8, D_MODEL), F32)],
        grid=(seq // ROWS,),
        in_specs=[_tile(ROWS, D_MODEL)] * 3 + [_full((6, D_MODEL)), _full((1, D_MODEL))],
        out_specs=[_full((1, 128)), _tile(ROWS, D_MODEL), _tile(ROWS, D_MODEL), _full((8, D_MODEL))],
        compiler_params=_cparams(1),
    )(yf, x1, target, mod6, g4)


def _mid_bwd(dh2, x1, dout, ymix, mod6, g3, g2):
    seq = x1.shape[0]

    def body(dh_ref, x1_ref, dout_ref, y_ref, mod_ref, g3_ref, g2_ref, dx1_ref, dy_ref, small_ref):
        i = pl.program_id(0)

        @pl.when(i == 0)
        def _():
            small_ref[...] = jnp.zeros_like(small_ref)

        dh = dh_ref[...]
        g3v, g2v = g3_ref[...], g2_ref[...]
        xn, r3 = _rms(x1_ref[...])
        y3 = xn * g3v
        dy3 = dh * (1.0 + mod_ref[SC_F:SC_F + 1, :])
        small_ref[0:1, :] += _colsum(dy3 * xn)
        small_ref[1:2, :] += _colsum(dh * y3)
        small_ref[2:3, :] += _colsum(dh)
        dx1 = dout_ref[...] + _rms_bwd(dy3 * g3v, xn, r3)
        dx1_ref[...] = dx1
        gt = mod_ref[GT_M:GT_M + 1, :]
        yn, r2 = _rms(y_ref[...])
        small_ref[3:4, :] += _colsum(dx1 * gt * yn)
        small_ref[4:5, :] += _colsum(dx1 * (yn * g2v))
        dy_ref[...] = _rms_bwd(dx1 * gt * g2v, yn, r2).astype(BF16)

    return pl.pallas_call(
        body, name="mid_bwd",
        out_shape=[jax.ShapeDtypeStruct((seq, D_MODEL), F32), jax.ShapeDtypeStruct((seq, D_MODEL), BF16),
                   jax.ShapeDtypeStruct((8, D_MODEL), F32)],
        grid=(seq // ROWS,),
        in_specs=[_tile(ROWS, D_MODEL)] * 4 + [_full((6, D_MODEL)), _full((1, D_MODEL)), _full((1, D_MODEL))],
        out_specs=[_tile(ROWS, D_MODEL), _tile(ROWS, D_MODEL), _full((8, D_MODEL))],
        compiler_params=_cparams(1),
    )(dh2, x1, dout, ymix, mod6, g3, g2)


def _pre_mix_bwd(dh1, x, dx1, mod6, g1):
    seq = x.shape[0]

    def body(dh_ref, x_ref, dx1_ref, mod_ref, g_ref, dx_ref, small_ref):
        i = pl.program_id(0)

        @pl.when(i == 0)
        def _():
            small_ref[...] = jnp.zeros_like(small_ref)

        dh = dh_ref[...]
        g1v = g_ref[...]
        xn, r = _rms(x_ref[...])
        dy = dh * (1.0 + mod_ref[SC_M:SC_M + 1, :])
        small_ref[0:1, :] += _colsum(dy * xn)
        small_ref[1:2, :] += _colsum(dh * (xn * g1v))
        small_ref[2:3, :] += _colsum(dh)
        dx_ref[...] = dx1_ref[...] + _rms_bwd(dy * g1v, xn, r)

    return pl.pallas_call(
        body, name="pre_mix_bwd",
        out_shape=[jax.ShapeDtypeStruct((seq, D_MODEL), F32), jax.ShapeDtypeStruct((8, D_MODEL), F32)],
        grid=(seq // ROWS,),
        in_specs=[_tile(ROWS, D_MODEL)] * 3 + [_full((6, D_MODEL)), _full((1, D_MODEL))],
        out_specs=[_tile(ROWS, D_MODEL), _full((8, D_MODEL))], compiler_params=_cparams(1),
    )(dh1, x, dx1, mod6, g1)


def _bias_table(rel_bias):
    def body(rb_ref, o_ref):
        rb = rb_ref[...]
        ii = lax.broadcasted_iota(jnp.int32, (N_REL, BAND), 0)
        kj = lax.broadcasted_iota(jnp.int32, (N_REL, BAND), 1)

        def step(qi, carry):
            idx = jnp.clip(PAD_ROWS + qi - kj, -MAX_REL, MAX_REL) + MAX_REL
            onehot = (ii == idx).astype(F32)
            o_ref[qi] = jnp.dot(rb, onehot, precision=HIGHEST, preferred_element_type=F32)
            return carry

        lax.fori_loop(0, CHUNK, step, 0)

    return pl.pallas_call(
        body, name="bias_table", out_shape=jax.ShapeDtypeStruct((CHUNK, N_HEADS, BAND), F32),
        compiler_params=pltpu.CompilerParams(vmem_limit_bytes=VMEM_LIMIT_BYTES),
    )(rel_bias)


def _bias_grad(dbias_q):
    def body(d_ref, o_ref):
        kj = lax.broadcasted_iota(jnp.int32, (BAND, N_REL), 0)
        ii = lax.broadcasted_iota(jnp.int32, (BAND, N_REL), 1)

        def step(qi, acc):
            idx = jnp.clip(PAD_ROWS + qi - kj, -MAX_REL, MAX_REL) + MAX_REL
            onehot = (ii == idx).astype(F32)
            return acc + jnp.dot(d_ref[qi], onehot, precision=HIGHEST, preferred_element_type=F32)

        o_ref[...] = lax.fori_loop(0, CHUNK, step, jnp.zeros((N_HEADS, N_REL), F32))

    return pl.pallas_call(
        body, name="bias_grad", out_shape=jax.ShapeDtypeStruct((N_HEADS, N_REL), F32),
        compiler_params=pltpu.CompilerParams(vmem_limit_bytes=VMEM_LIMIT_BYTES),
    )(dbias_q)


def _probs(q_ref, k_ref, b_ref, h, start, valid):
    cols = slice(h * HEAD_DIM, (h + 1) * HEAD_DIM)
    qh = q_ref[:, cols]
    kh = k_ref[pl.ds(start, BAND), cols]
    s = lax.dot_general(qh, kh, _DIMS["nt"], preferred_element_type=F32) * (HEAD_DIM ** -0.5) + b_ref[h]
    s = jnp.where(valid, s, NEG_INF)
    p = jnp.exp(s - jnp.max(s, axis=-1, keepdims=True))
    return qh, kh, p / jnp.sum(p, axis=-1, keepdims=True)


def _valid_keys(c):
    kj = lax.broadcasted_iota(jnp.int32, (CHUNK, BAND), 1)
    return kj >= (LEFT_CHUNKS - c) * CHUNK


def _attn_fwd(qkv, kpad, vpad, bias):
    seq = qkv.shape[0]

    def body(q_ref, k_ref, v_ref, b_ref, o_ref):
        c = pl.program_id(0)
        start = pl.multiple_of(c * CHUNK, CHUNK)
        valid = _valid_keys(c)
        for h in range(N_HEADS):
            cols = slice(h * HEAD_DIM, (h + 1) * HEAD_DIM)
            _, _, p = _probs(q_ref, k_ref, b_ref, h, start, valid)
            vh = v_ref[pl.ds(start, BAND), cols]
            o_ref[:, cols] = jnp.dot(p.astype(BF16), vh, preferred_element_type=F32).astype(BF16)

    return pl.pallas_call(
        body, name="attn_fwd", out_shape=jax.ShapeDtypeStruct((seq, D_ATTN), BF16), grid=(seq // CHUNK,),
        in_specs=[_tile(CHUNK, D_ATTN), _full(kpad.shape), _full(vpad.shape), _full(bias.shape)],
        out_specs=_tile(CHUNK, D_ATTN), compiler_params=_cparams(1),
    )(qkv, kpad, vpad, bias)


def _attn_bwd(qkv, kpad, vpad, bias, dao):
    seq = qkv.shape[0]
    n_chunks = seq // CHUNK

    def body(q_ref, k_ref, v_ref, b_ref, do_ref, dq_ref, dk_hbm, dv_hbm, db_ref, cs_ref, dk_acc, dv_acc, sems):
        c = pl.program_id(0)

        @pl.when(c == 0)
        def _():
            dk_acc[...] = jnp.zeros_like(dk_acc)
            dv_acc[...] = jnp.zeros_like(dv_acc)
            db_ref[...] = jnp.zeros_like(db_ref)
            cs_ref[...] = jnp.zeros_like(cs_ref)

        start = pl.multiple_of(c * CHUNK, CHUNK)
        valid = _valid_keys(c)
        for h in range(N_HEADS):
            cols = slice(h * HEAD_DIM, (h + 1) * HEAD_DIM)
            qh, kh, p = _probs(q_ref, k_ref, b_ref, h, start, valid)
            vh = v_ref[pl.ds(start, BAND), cols]
            doh = do_ref[:, cols]
            dp = lax.dot_general(doh, vh, _DIMS["nt"], preferred_element_type=F32)
            ds = p * (dp - jnp.sum(dp * p, axis=-1, keepdims=True))
            db_ref[h] += ds
            dsb = (ds * (HEAD_DIM ** -0.5)).astype(BF16)
            dq_ref[:, cols] = jnp.dot(dsb, kh, preferred_element_type=F32).astype(BF16)
            dk_acc[pl.ds(start, BAND), cols] += lax.dot_general(dsb, qh, _DIMS["tn"], preferred_element_type=F32)
            dv_acc[pl.ds(start, BAND), cols] += lax.dot_general(p.astype(BF16), doh, _DIMS["tn"],
                                                                preferred_element_type=F32)
        cs_ref[0:1, :] += _colsum(dq_ref[...].astype(F32))

        @pl.when(c == n_chunks - 1)
        def _():
            cs_ref[1:2, :] = _colsum(dk_acc[pl.ds(PAD_ROWS, seq), :])
            cs_ref[2:3, :] = _colsum(dv_acc[pl.ds(PAD_ROWS, seq), :])
            out_k = pltpu.make_async_copy(dk_acc.at[pl.ds(PAD_ROWS, seq), :], dk_hbm, sems.at[0])
            out_v = pltpu.make_async_copy(dv_acc.at[pl.ds(PAD_ROWS, seq), :], dv_hbm, sems.at[1])
            out_k.start()
            out_v.start()
            out_k.wait()
            out_v.wait()

    return pl.pallas_call(
        body, name="attn_bwd",
        out_shape=[jax.ShapeDtypeStruct((seq, D_ATTN), BF16), jax.ShapeDtypeStruct((seq, D_ATTN), F32),
                   jax.ShapeDtypeStruct((seq, D_ATTN), F32), jax.ShapeDtypeStruct(bias.shape, F32),
                   jax.ShapeDtypeStruct((8, D_ATTN), F32)],
        grid=(n_chunks,),
        in_specs=[_tile(CHUNK, D_ATTN), _full(kpad.shape), _full(vpad.shape), _full(bias.shape),
                  _tile(CHUNK, D_ATTN)],
        out_specs=[_tile(CHUNK, D_ATTN), ANY, ANY, _full(bias.shape), _full((8, D_ATTN))],
        scratch_shapes=[pltpu.VMEM(kpad.shape, F32), pltpu.VMEM(kpad.shape, F32), pltpu.SemaphoreType.DMA((2,))],
        compiler_params=_cparams(1),
    )(qkv, kpad, vpad, bias, dao)


CONV_ROWS = 256


def _ln_silu(u1, g, b):
    mu = jnp.mean(u1, axis=-1, keepdims=True)
    xc = u1 - mu
    rs = lax.rsqrt(jnp.mean(xc * xc, axis=-1, keepdims=True) + EPS)
    xhat = xc * rs
    u2 = xhat * g + b
    return xhat, rs, u2


def _glu_into(s_ref, a_ref, b_ref, ah_ref, bh_ref, first):
    halo = ah_ref[...] * _sig(bh_ref[...])
    s_ref[0:CONV_HALO, :] = jnp.where(first, 0.0, halo)
    s_ref[CONV_HALO:CONV_HALO + CONV_ROWS, :] = a_ref[...] * _sig(b_ref[...])


def _conv_fwd(zr, w_dw, b_dw, g_ln, b_ln):
    seq = zr.shape[0]

    def body(a_ref, b_ref, ah_ref, bh_ref, w_ref, bias_ref, g_ref, bl_ref, u1_ref, u3_ref, s_ref):
        _glu_into(s_ref, a_ref, b_ref, ah_ref, bh_ref, pl.program_id(0) == 0)
        acc = jnp.zeros((CONV_ROWS, D_CONV), F32) + bias_ref[...]
        for j in range(CONV_K):
            acc = acc + w_ref[j:j + 1, :] * s_ref[2 + j:2 + j + CONV_ROWS, :]
        u1_ref[...] = acc
        _, _, u2 = _ln_silu(acc, g_ref[...], bl_ref[...])
        u3_ref[...] = (u2 * _sig(u2)).astype(BF16)

    return pl.pallas_call(
        body, name="conv_fwd",
        out_shape=[jax.ShapeDtypeStruct((seq, D_CONV), F32), jax.ShapeDtypeStruct((seq, D_CONV), BF16)],
        grid=(seq // CONV_ROWS,),
        in_specs=[_tile(CONV_ROWS, D_CONV, 0), _tile(CONV_ROWS, D_CONV, 1),
                  _prev(CONV_HALO, D_CONV, CONV_ROWS, 0), _prev(CONV_HALO, D_CONV, CONV_ROWS, 1),
                  _full((CONV_K, D_CONV)), _full((1, D_CONV)), _full((1, D_CONV)), _full((1, D_CONV))],
        out_specs=[_tile(CONV_ROWS, D_CONV), _tile(CONV_ROWS, D_CONV)],
        scratch_shapes=[pltpu.VMEM((CONV_HALO + CONV_ROWS, D_CONV), F32)],
        compiler_params=_cparams(1),
    )(zr, zr, zr, zr, w_dw, b_dw, g_ln, b_ln)


def _conv_bwd(zr, u1, du3, w_dw, g_ln, b_ln):
    seq = zr.shape[0]
    n_tiles = seq // CONV_ROWS
    n_halo = seq // CONV_HALO
    ext = CONV_ROWS + CONV_HALO

    def body(a_ref, b_ref, ah_ref, bh_ref, u1_ref, u1n_ref, d3_ref, d3n_ref, w_ref, g_ref, bl_ref,
             da_ref, db_ref, dw_ref, small_ref, s_ref, d_ref):
        i = pl.program_id(0)

        @pl.when(i == 0)
        def _():
            dw_ref[...] = jnp.zeros_like(dw_ref)
            small_ref[...] = jnp.zeros_like(small_ref)

        _glu_into(s_ref, a_ref, b_ref, ah_ref, bh_ref, i == 0)
        gv, bv = g_ref[...], bl_ref[...]

        def du1_of(u1, d3):
            xhat, rs, u2 = _ln_silu(u1, gv, bv)
            sg = _sig(u2)
            du2 = d3 * (sg * (1.0 + u2 * (1.0 - sg)))
            dxh = du2 * gv
            du1 = rs * (dxh - jnp.mean(dxh, axis=-1, keepdims=True)
                        - xhat * jnp.mean(dxh * xhat, axis=-1, keepdims=True))
            return du1, du2, xhat

        du1, du2, xhat = du1_of(u1_ref[...], d3_ref[...])
        du1n, _, _ = du1_of(u1n_ref[...], d3n_ref[...])
        d_ref[0:CONV_ROWS, :] = du1
        d_ref[CONV_ROWS:ext, :] = jnp.where(i == n_tiles - 1, 0.0, du1n)
        small_ref[0:1, :] += _colsum(du1)
        small_ref[1:2, :] += _colsum(du2 * xhat)
        small_ref[2:3, :] += _colsum(du2)
        du0 = jnp.zeros((CONV_ROWS, D_CONV), F32)
        for j in range(CONV_K):
            dw_ref[j:j + 1, :] += _colsum(du1 * s_ref[2 + j:2 + j + CONV_ROWS, :])
            du0 = du0 + w_ref[j:j + 1, :] * d_ref[CONV_K - 1 - j:CONV_K - 1 - j + CONV_ROWS, :]
        sb = _sig(b_ref[...])
        da = du0 * sb
        dbv = du0 * a_ref[...] * sb * (1.0 - sb)
        da_ref[...] = da.astype(BF16)
        db_ref[...] = dbv.astype(BF16)
        small_ref[3:4, :] += _colsum(da)
        small_ref[4:5, :] += _colsum(dbv)

    return pl.pallas_call(
        body, name="conv_bwd",
        out_shape=[jax.ShapeDtypeStruct((seq, D_CONV), BF16), jax.ShapeDtypeStruct((seq, D_CONV), BF16),
                   jax.ShapeDtypeStruct((CONV_HALO, D_CONV), F32), jax.ShapeDtypeStruct((8, D_CONV), F32)],
        grid=(n_tiles,),
        in_specs=[_tile(CONV_ROWS, D_CONV, 0), _tile(CONV_ROWS, D_CONV, 1),
                  _prev(CONV_HALO, D_CONV, CONV_ROWS, 0), _prev(CONV_HALO, D_CONV, CONV_ROWS, 1),
                  _tile(CONV_ROWS, D_CONV), _next(CONV_HALO, D_CONV, CONV_ROWS, n_halo),
                  _tile(CONV_ROWS, D_CONV), _next(CONV_HALO, D_CONV, CONV_ROWS, n_halo),
                  _full((CONV_K, D_CONV)), _full((1, D_CONV)), _full((1, D_CONV))],
        out_specs=[_tile(CONV_ROWS, D_CONV), _tile(CONV_ROWS, D_CONV), _full((CONV_HALO, D_CONV)),
                   _full((8, D_CONV))],
        scratch_shapes=[pltpu.VMEM((ext, D_CONV), F32), pltpu.VMEM((ext, D_CONV), F32)],
        compiler_params=_cparams(1),
    )(zr, zr, zr, zr, u1, u1, du3, du3, w_dw, g_ln, b_ln)


MERGE_ROWS = 256


def _merge_fwd(ao, u3, zr, w_ao, w_co, b_co):
    seq = ao.shape[0]

    def body(ao_ref, u3_ref, ga_ref, gb_ref, wa_ref, wc_ref, bc_ref, y_ref, a_ref, cb_ref):
        a = jnp.dot(ao_ref[...], wa_ref[...], preferred_element_type=F32)
        cb = jnp.dot(u3_ref[...], wc_ref[...], preferred_element_type=F32) + bc_ref[...]
        a_ref[...] = a
        cb_ref[...] = cb
        y_ref[...] = (_sig(ga_ref[...]) * a + _sig(gb_ref[...]) * cb).astype(BF16)

    f32_out = jax.ShapeDtypeStruct((seq, D_MODEL), F32)
    return pl.pallas_call(
        body, name="merge_fwd",
        out_shape=[jax.ShapeDtypeStruct((seq, D_MODEL), BF16), f32_out, f32_out],
        grid=(seq // MERGE_ROWS,),
        in_specs=[_tile(MERGE_ROWS, D_ATTN), _tile(MERGE_ROWS, D_CONV), _tile(MERGE_ROWS, D_MODEL, 1),
                  _tile(MERGE_ROWS, D_MODEL, 2), _full(w_ao.shape), _full(w_co.shape), _full((1, D_MODEL))],
        out_specs=[_tile(MERGE_ROWS, D_MODEL)] * 3, compiler_params=_cparams(1),
    )(ao, u3, zr, zr, w_ao, w_co, b_co)


def _merge_bwd(dy, a, cb, zr):
    seq = dy.shape[0]

    def body(dy_ref, a_ref, cb_ref, ga_ref, gb_ref, da_ref, dcb_ref, dga_ref, dgb_ref, small_ref):
        i = pl.program_id(0)

        @pl.when(i == 0)
        def _():
            small_ref[...] = jnp.zeros_like(small_ref)

        dy_v = dy_ref[...]
        sa, sb = _sig(ga_ref[...]), _sig(gb_ref[...])
        dcb = dy_v * sb
        dga = dy_v * a_ref[...] * sa * (1.0 - sa)
        dgb = dy_v * cb_ref[...] * sb * (1.0 - sb)
        da_ref[...] = (dy_v * sa).astype(BF16)
        dcb_ref[...] = dcb.astype(BF16)
        dga_ref[...] = dga.astype(BF16)
        dgb_ref[...] = dgb.astype(BF16)
        small_ref[0:1, :] += _colsum(dga)
        small_ref[1:2, :] += _colsum(dgb)
        small_ref[2:3, :] += _colsum(dcb)

    bf = jax.ShapeDtypeStruct((seq, D_MODEL), BF16)
    return pl.pallas_call(
        body, name="merge_bwd", out_shape=[bf, bf, bf, bf, jax.ShapeDtypeStruct((8, D_MODEL), F32)],
        grid=(seq // MERGE_ROWS,),
        in_specs=[_tile(MERGE_ROWS, D_MODEL)] * 3 + [_tile(MERGE_ROWS, D_MODEL, 1), _tile(MERGE_ROWS, D_MODEL, 2)],
        out_specs=[_tile(MERGE_ROWS, D_MODEL)] * 4 + [_full((8, D_MODEL))], compiler_params=_cparams(1),
    )(dy, a, cb, zr, zr)


FFN_ROWS = 512
FFN_BLOCKS = D_FF // FFN_COLS
GELU_C = math.sqrt(2.0 / math.pi)


def _gelu(v):
    t = jnp.tanh(GELU_C * (v + 0.044715 * (v * v * v)))
    return 0.5 * v * (1.0 + t), t


def _gelu_grad(v, t):
    return 0.5 * (1.0 + t) + 0.5 * v * (1.0 - t * t) * (GELU_C * (1.0 + 3.0 * 0.044715 * (v * v)))


def _conv3(s_ref, w_ref, bias, rows):
    base = FFN_HALO - 2
    return (w_ref[0:1, :] * s_ref[base:base + rows, :] + w_ref[1:2, :] * s_ref[base + 1:base + 1 + rows, :]
            + w_ref[2:3, :] * s_ref[base + 2:base + 2 + rows, :] + bias)


def _ffn_specs(rows):
    tile = lambda off: pl.BlockSpec((rows, FFN_COLS), lambda j, i: (i, j + off))
    prev = lambda off: pl.BlockSpec((FFN_HALO, FFN_COLS),
                                    lambda j, i: (jnp.maximum(i * (rows // FFN_HALO) - 1, 0), j + off))
    wgt = lambda off: pl.BlockSpec((3, FFN_COLS), lambda j, i: (0, j + off))
    vec = lambda off: pl.BlockSpec((1, FFN_COLS), lambda j, i: (0, j + off))
    return tile, prev, wgt, vec


def _ffn_act(up, w_dw, b_dw):
    seq = up.shape[0]
    tile, prev, wgt, vec = _ffn_specs(FFN_ROWS)

    def body(v_ref, g_ref, vp_ref, gp_ref, wv_ref, wg_ref, bv_ref, bg_ref, act_ref, sv_ref, sg_ref):
        first = pl.program_id(1) == 0
        sv_ref[0:FFN_HALO, :] = jnp.where(first, 0.0, vp_ref[...])
        sg_ref[0:FFN_HALO, :] = jnp.where(first, 0.0, gp_ref[...])
        sv_ref[FFN_HALO:FFN_HALO + FFN_ROWS, :] = v_ref[...]
        sg_ref[FFN_HALO:FFN_HALO + FFN_ROWS, :] = g_ref[...]
        val = _conv3(sv_ref, wv_ref, bv_ref[...], FFN_ROWS)
        gate = _conv3(sg_ref, wg_ref, bg_ref[...], FFN_ROWS)
        act_ref[...] = (_gelu(gate)[0] * val).astype(BF16)

    return pl.pallas_call(
        body, name="ffn_act", out_shape=jax.ShapeDtypeStruct((seq, D_FF), BF16),
        grid=(FFN_BLOCKS, seq // FFN_ROWS),
        in_specs=[tile(0), tile(FFN_BLOCKS), prev(0), prev(FFN_BLOCKS), wgt(0), wgt(FFN_BLOCKS),
                  vec(0), vec(FFN_BLOCKS)],
        out_specs=tile(0),
        scratch_shapes=[pltpu.VMEM((FFN_HALO + FFN_ROWS, FFN_COLS), F32)] * 2,
        compiler_params=_cparams(2),
    )(up, up, up, up, w_dw, w_dw, b_dw, b_dw)


def _ffn_act_bwd(up, dact, w_dw, b_dw):
    seq = up.shape[0]
    n_tiles = seq // FFN_ROWS
    n_halo = seq // FFN_HALO
    ext = FFN_ROWS + FFN_HALO
    tile, prev, wgt, vec = _ffn_specs(FFN_ROWS)
    nxt = lambda off: pl.BlockSpec(
        (FFN_HALO, FFN_COLS), lambda j, i: (jnp.minimum((i + 1) * (FFN_ROWS // FFN_HALO), n_halo - 1), j + off))
    acc = lambda off: pl.BlockSpec((8, FFN_COLS), lambda j, i: (0, j + off))

    def body(v_ref, g_ref, vp_ref, gp_ref, vn_ref, gn_ref, da_ref, dan_ref, wv_ref, wg_ref, bv_ref, bg_ref,
             dv_out, dg_out, dwv_ref, dwg_ref, dbv_ref, dbg_ref, sv_ref, sg_ref, dvs_ref, dgs_ref):
        i = pl.program_id(1)
        first, last = i == 0, i == n_tiles - 1

        @pl.when(first)
        def _():
            for r in (dwv_ref, dwg_ref, dbv_ref, dbg_ref):
                r[...] = jnp.zeros_like(r)

        for s_ref, t_ref, p_ref, n_ref in ((sv_ref, v_ref, vp_ref, vn_ref), (sg_ref, g_ref, gp_ref, gn_ref)):
            s_ref[0:FFN_HALO, :] = jnp.where(first, 0.0, p_ref[...])
            s_ref[FFN_HALO:FFN_HALO + FFN_ROWS, :] = t_ref[...]
            s_ref[FFN_HALO + FFN_ROWS:FFN_HALO + ext, :] = n_ref[...]
        val = _conv3(sv_ref, wv_ref, bv_ref[...], ext)
        gate = _conv3(sg_ref, wg_ref, bg_ref[...], ext)
        gel, t = _gelu(gate)
        dvs_ref[0:FFN_ROWS, :] = da_ref[...].astype(F32)
        dvs_ref[FFN_ROWS:ext, :] = jnp.where(last, 0.0, dan_ref[...].astype(F32)[0:FFN_HALO, :])
        dact = dvs_ref[...]
        dvs_ref[...] = dact * gel
        dgs_ref[...] = dact * val * _gelu_grad(gate, t)
        for d_ref, s_ref, w_ref, o_ref, dw_ref, db_ref in ((dvs_ref, sv_ref, wv_ref, dv_out, dwv_ref, dbv_ref),
                                                           (dgs_ref, sg_ref, wg_ref, dg_out, dwg_ref, dbg_ref)):
            d0 = d_ref[0:FFN_ROWS, :]
            o_ref[...] = (w_ref[2:3, :] * d0 + w_ref[1:2, :] * d_ref[1:1 + FFN_ROWS, :]
                          + w_ref[0:1, :] * d_ref[2:2 + FFN_ROWS, :]).astype(BF16)
            db_ref[0:1, :] += _colsum(d0)
            for tap in range(3):
                lo = FFN_HALO - 2 + tap
                dw_ref[tap:tap + 1, :] += _colsum(d0 * s_ref[lo:lo + FFN_ROWS, :])

    half = jax.ShapeDtypeStruct((seq, D_FF), BF16)
    acc_shape = jax.ShapeDtypeStruct((8, D_FF), F32)
    return pl.pallas_call(
        body, name="ffn_act_bwd", out_shape=[half, half, acc_shape, acc_shape, acc_shape, acc_shape],
        grid=(FFN_BLOCKS, n_tiles),
        in_specs=[tile(0), tile(FFN_BLOCKS), prev(0), prev(FFN_BLOCKS), nxt(0), nxt(FFN_BLOCKS),
                  tile(0), pl.BlockSpec((16, FFN_COLS), lambda j, i: (
                      jnp.minimum((i + 1) * (FFN_ROWS // 16), seq // 16 - 1), j)),
                  wgt(0), wgt(FFN_BLOCKS), vec(0), vec(FFN_BLOCKS)],
        out_specs=[tile(0), tile(0), acc(0), acc(0), acc(0), acc(0)],
        scratch_shapes=[pltpu.VMEM((FFN_HALO + ext, FFN_COLS), F32)] * 2 + [pltpu.VMEM((ext, FFN_COLS), F32)] * 2,
        compiler_params=_cparams(2),
    )(up, up, up, up, up, up, dact, dact, w_dw, w_dw, b_dw, b_dw)


def _cols_to_blocks(full_cols):
    k, n8 = full_cols.shape
    return jnp.transpose(full_cols.reshape(k, 4, 2, n8 // N_DEV), (2, 1, 0, 3))


def _rows_to_blocks(full_rows):
    r8, n = full_rows.shape
    return jnp.transpose(full_rows.reshape(4, 2, r8 // N_DEV, n), (1, 0, 2, 3))


def _blocks_to_cols(gathered):
    _, k, n = gathered.shape
    return jnp.transpose(gathered, (1, 0, 2)).reshape(k, N_DEV * n)


def _pack_small(values):
    pieces = []
    for name, width in SMALL:
        flat = values[name].reshape(1, -1)
        if flat.shape[1] < width:
            flat = jnp.pad(flat, ((0, 0), (0, width - flat.shape[1])))
        pieces.append(flat)
    return jnp.concatenate(pieces, axis=1)


def _unpack_small(vec, shapes):
    out, off = {}, 0
    for name, width in SMALL:
        n = math.prod(shapes[name])
        out[name] = vec[:, off:off + n].reshape(shapes[name])
        off += width
    return out


def kernel(x, c, w_ada, b_ada, g_pre_mix, g_post_mix, w_in, b_in, rel_bias, w_attn_o, w_dw_conv, b_dw_conv, g_conv_ln, b_conv_ln, w_conv_o, b_conv_o, w_mix_o, g_pre_ffn, g_post_ffn, w_up, w_dw_ffn, b_dw_ffn, w_down, loss_target, m_w_ada, m_b_ada, m_g_pre_mix, m_g_post_mix, m_w_in, m_b_in, m_rel_bias, m_w_attn_o, m_w_dw_conv, m_b_dw_conv, m_g_conv_ln, m_b_conv_ln, m_w_conv_o, m_b_conv_o, m_w_mix_o, m_g_pre_ffn, m_g_post_ffn, m_w_up, m_w_dw_ffn, m_b_dw_ffn, m_w_down, v_w_ada, v_b_ada, v_g_pre_mix, v_g_post_mix, v_w_in, v_b_in, v_rel_bias, v_w_attn_o, v_w_dw_conv, v_b_dw_conv, v_g_conv_ln, v_b_conv_ln, v_w_conv_o, v_b_conv_o, v_w_mix_o, v_g_pre_ffn, v_g_post_ffn, v_w_up, v_w_dw_ffn, v_b_dw_ffn, v_w_down):
    names = ["w_ada", "b_ada", "g_pre_mix", "g_post_mix", "w_in", "b_in", "rel_bias", "w_attn_o", "w_dw_conv",
             "b_dw_conv", "g_conv_ln", "b_conv_ln", "w_conv_o", "b_conv_o", "w_mix_o", "g_pre_ffn", "g_post_ffn",
             "w_up", "w_dw_ffn", "b_dw_ffn", "w_down"]
    weights = dict(zip(names, [w_ada, b_ada, g_pre_mix, g_post_mix, w_in, b_in, rel_bias, w_attn_o, w_dw_conv,
                               b_dw_conv, g_conv_ln, b_conv_ln, w_conv_o, b_conv_o, w_mix_o, g_pre_ffn,
                               g_post_ffn, w_up, w_dw_ffn, b_dw_ffn, w_down]))
    mom_m = dict(zip(names, [m_w_ada, m_b_ada, m_g_pre_mix, m_g_post_mix, m_w_in, m_b_in, m_rel_bias, m_w_attn_o,
                             m_w_dw_conv, m_b_dw_conv, m_g_conv_ln, m_b_conv_ln, m_w_conv_o, m_b_conv_o,
                             m_w_mix_o, m_g_pre_ffn, m_g_post_ffn, m_w_up, m_w_dw_ffn, m_b_dw_ffn, m_w_down]))
    mom_v = dict(zip(names, [v_w_ada, v_b_ada, v_g_pre_mix, v_g_post_mix, v_w_in, v_b_in, v_rel_bias, v_w_attn_o,
                             v_w_dw_conv, v_b_dw_conv, v_g_conv_ln, v_b_conv_ln, v_w_conv_o, v_b_conv_o,
                             v_w_mix_o, v_g_pre_ffn, v_g_post_ffn, v_w_up, v_w_dw_ffn, v_b_dw_ffn, v_w_down]))
    shapes = {n: w.shape for n, w in weights.items()}

    seq = x.shape[1]
    me = 4 * lax.axis_index("x") + 2 * lax.axis_index("y") + lax.axis_index("c")
    x2 = x.reshape(seq, D_MODEL)
    target = loss_target.reshape(seq, D_MODEL)
    sq = lambda a: a.reshape(a.shape[1:])
    bf = lambda a: sq(a).astype(BF16)

    c_act = _silu_vec(c)
    (c_all, g_in, g_ao, g_co, g_mo, g_up, g_dn, g_dwc, g_dwf) = _all_gather(
        [c_act, bf(w_in), bf(w_attn_o), bf(w_conv_o), bf(w_mix_o), bf(w_up), bf(w_down), sq(w_dw_conv),
         sq(w_dw_ffn)], "gather_weights")
    c_all = c_all.reshape(N_DEV, D_MODEL)
    wf_in = _blocks_to_cols(g_in)
    wf_ao = _blocks_to_cols(g_ao)
    wf_co = _blocks_to_cols(g_co)
    wf_mo = g_mo.reshape(D_MODEL, D_MODEL)
    wf_up = _blocks_to_cols(g_up)
    wf_dn = g_dn.reshape(D_FF, D_MODEL)
    wf_dwc = _blocks_to_cols(g_dwc)
    wf_dwf = _blocks_to_cols(g_dwf)

    (mod_all,) = _all_gather([_ada_fwd(c_all, sq(w_ada))], "gather_mod")
    mod = lax.dynamic_index_in_dim(mod_all, me, axis=1, keepdims=False)
    mod6 = (mod.reshape(1, 6 * D_MODEL) + b_ada).reshape(6, D_MODEL)

    h1 = _pre_mix(x2, mod6, g_pre_mix)
    qkv = _mm(h1, wf_in[:, :3 * D_ATTN], "nn", BF16, "in_proj_qkv", bias=b_in[:, :3 * D_ATTN])
    zr = _mm(h1, wf_in[:, 3 * D_ATTN:], "nn", F32, "in_proj_rest", bias=b_in[:, 3 * D_ATTN:])
    kpad = jnp.pad(qkv[:, D_ATTN:2 * D_ATTN], ((PAD_ROWS, 0), (0, 0)))
    vpad = jnp.pad(qkv[:, 2 * D_ATTN:], ((PAD_ROWS, 0), (0, 0)))
    bias = jnp.transpose(_bias_table(sq(rel_bias)), (1, 0, 2))
    ao = _attn_fwd(qkv, kpad, vpad, bias)
    u1, u3 = _conv_fwd(zr, wf_dwc, b_dw_conv, g_conv_ln, b_conv_ln)
    y, a_br, cb_br = _merge_fwd(ao, u3, zr, wf_ao, wf_co, b_conv_o)
    ymix = _mm(y, wf_mo, "nn", F32, "mix_o")
    x1, h2 = _post_mix_pre_ffn(ymix, x2, mod6, g_post_mix, g_pre_ffn)
    up = _mm(h2, wf_up, "nn", F32, "ffn_up")
    act = _ffn_act(up, wf_dwf, b_dw_ffn)
    yf = _mm(act, wf_dn, "nn", F32, "ffn_down", tm=512, tn=1024, tk=1408)
    loss_lanes, dout, dyf, small_f = _final(yf, x1, target, mod6, g_post_ffn)
    loss = lax.psum(0.5 * loss_lanes[0, 0], ("x", "y", "c"))

    dact = _mm(dyf, wf_dn, "nt", BF16, "ffn_down_dx", tn=256)
    gw_down = _mm(act, dyf, "tn", BF16, "ffn_down_dw", tm=256, tn=1024)
    dup_v, dup_g, dwv, dwg, dbv, dbg = _ffn_act_bwd(up, dact, wf_dwf, b_dw_ffn)
    dup = jnp.concatenate([dup_v, dup_g], axis=1)
    dh2 = _mm(dup, wf_up, "nt", F32, "ffn_up_dx", tk=512)
    gw_up = _mm(h2, dup, "tn", BF16, "ffn_up_dw")
    dx1, dymix, small_m = _mid_bwd(dh2, x1, dout, ymix, mod6, g_pre_ffn, g_post_mix)
    dy = _mm(dymix, wf_mo, "nt", F32, "mix_o_dx")
    gw_mo = _mm(y, dymix, "tn", BF16, "mix_o_dw")
    da, dcb, dga, dgb, small_g = _merge_bwd(dy, a_br, cb_br, zr)
    dao = _mm(da, wf_ao, "nt", BF16, "attn_o_dx")
    gw_ao = _mm(ao, da, "tn", BF16, "attn_o_dw")
    du3 = _mm(dcb, wf_co, "nt", F32, "conv_o_dx")
    gw_co = _mm(u3, dcb, "tn", BF16, "conv_o_dw")
    dq, dk, dv, dbias, small_a = _attn_bwd(qkv, kpad, vpad, bias, dao)
    g_rel = _bias_grad(jnp.transpose(dbias, (1, 0, 2)))
    dglu_a, dglu_b, dw_conv, small_c = _conv_bwd(zr, u1, du3, wf_dwc, g_conv_ln, b_conv_ln)
    dz = jnp.concatenate([dq, dk.astype(BF16), dv.astype(BF16), dglu_a, dglu_b, dga, dgb], axis=1)
    dh1 = _mm(dz, wf_in, "nt", F32, "in_proj_dx", tk=512)
    gw_in = _mm(h1, dz, "tn", BF16, "in_proj_dw")
    grad_x, small_x = _pre_mix_bwd(dh1, x2, dx1, mod6, g_pre_mix)

    dmod = jnp.concatenate([small_x[2:3], small_x[1:2], small_m[4:5], small_m[2:3], small_m[1:2], small_f[1:2]],
                           axis=1)
    partial = {
        "b_ada": dmod, "g_pre_mix": small_x[0:1], "g_post_mix": small_m[3:4],
        "b_in": jnp.concatenate([small_a[0:1], small_a[1:2], small_a[2:3], small_c[3:4], small_c[4:5],
                                 small_g[0:1], small_g[1:2]], axis=1),
        "rel_bias": g_rel, "b_dw_conv": small_c[0:1], "g_conv_ln": small_c[1:2], "b_conv_ln": small_c[2:3],
        "b_conv_o": small_g[2:3], "g_pre_ffn": small_m[0:1], "g_post_ffn": small_f[0:1],
        "b_dw_ffn": jnp.concatenate([dbv[0:1], dbg[0:1]], axis=1),
    }
    dw_ffn = jnp.concatenate([dwv[0:3], dwg[0:3]], axis=1)
    packed = jnp.concatenate([_pack_small(partial), dmod, dw_conv[0:CONV_K].reshape(1, CONV_K * D_CONV),
                              dw_ffn.reshape(1, 3 * 2 * D_FF)], axis=1)
    (gathered,) = _all_gather([packed], "gather_small")
    summed, d_small, m_small, v_small = _small_adamw(
        gathered.reshape(N_DEV, packed.shape[1]), _pack_small(weights), _pack_small(mom_m), _pack_small(mom_v))
    off = SMALL_TOTAL
    dmod_all = gathered.reshape(N_DEV, packed.shape[1])[:, off:off + 6 * D_MODEL]
    off += 6 * D_MODEL
    g_dwc_full = summed[:, off:off + CONV_K * D_CONV].reshape(CONV_K, D_CONV)
    off += CONV_K * D_CONV
    g_dwf_full = summed[:, off:off + 3 * 2 * D_FF].reshape(3, 2 * D_FF)

    grads, deltas, new_m, new_v = {}, {}, {}, {}
    for dst, vec in ((grads, summed), (deltas, d_small), (new_m, m_small), (new_v, v_small)):
        dst.update(_unpack_small(vec, shapes))

    def local_update(name, grad):
        g, d, m_new, v_new = _adamw(sq(weights[name]), sq(mom_m[name]), sq(mom_v[name]), "adamw_" + name, g=grad)
        for dst, val in ((grads, g), (deltas, d), (new_m, m_new), (new_v, v_new)):
            dst[name] = val.reshape(shapes[name])

    local_update("w_dw_conv", lax.dynamic_slice_in_dim(g_dwc_full, me * (D_CONV // N_DEV), D_CONV // N_DEV, 1))
    local_update("w_dw_ffn", lax.dynamic_slice_in_dim(g_dwf_full, me * (2 * D_FF // N_DEV), 2 * D_FF // N_DEV, 1))
    ada_cols = 6 * D_MODEL // N_DEV
    local_update("w_ada", _ada_grad(c_all, lax.dynamic_slice_in_dim(dmod_all, me * ada_cols, ada_cols, 1)))

    big = ["w_in", "w_attn_o", "w_conv_o", "w_mix_o", "w_up", "w_down"]
    blocks = [_cols_to_blocks(gw_in), _cols_to_blocks(gw_ao), _cols_to_blocks(gw_co), _rows_to_blocks(gw_mo),
              _cols_to_blocks(gw_up), _rows_to_blocks(gw_down)]
    got = _sibling_exchange(blocks, "rs_sibling")
    my_c = lax.axis_index("c")
    pair = [_pair_sum(lax.dynamic_index_in_dim(b, my_c, axis=0, keepdims=False), r, "rs_pair_" + n)
            for b, r, n in zip(blocks, got, big)]
    parts = _chip_exchange(pair, "rs_chips")
    for name, part in zip(big, parts):
        g, d, m_new, v_new = _adamw(sq(weights[name]), sq(mom_m[name]), sq(mom_v[name]), "adamw_" + name,
                                    parts=part)
        for dst, val in ((grads, g), (deltas, d), (new_m, m_new), (new_v, v_new)):
            dst[name] = val.reshape(shapes[name])

    return (loss, grad_x.reshape(x.shape), *[grads[n] for n in names], *[deltas[n] for n in names],
            *[new_m[n] for n in names], *[new_v[n] for n in names])
```

```python
import functools
import math

import jax
import jax.numpy as jnp
from jax import lax
from jax.experimental import pallas as pl
from jax.experimental.pallas import tpu as pltpu

F32 = jnp.float32
BF16 = jnp.bfloat16
HIGHEST = lax.Precision.HIGHEST

D_MODEL = 1024
CHUNK = 64
LEFT_CHUNKS = 8
BAND = (LEFT_CHUNKS + 1) * CHUNK
PAD_ROWS = LEFT_CHUNKS * CHUNK
GROUP = 4
GROUP_Q = GROUP * CHUNK
GROUP_K = GROUP_Q + PAD_ROWS
SOFTMAX_ROWS = 16
TOEPLITZ = 640
N_HEADS = 8
HEAD_DIM = 64
D_ATTN = 512
D_CONV = 512
CONV_K = 31
CONV_HALO = 32
MAX_REL = 128
N_REL = 2 * MAX_REL + 1
D_FF = 2816
FFN_HALO = 8
FFN_COLS = 256
EPS = 1e-6
NEG_INF = -1e30
N_DEV = 8

ADAM_LR = 0.001
ADAM_B1 = 0.9
ADAM_B2 = 0.999
ADAM_EPS = 1e-08
ADAM_WD = 0.01
ADAM_STEP = 10

VMEM_LIMIT_BYTES = 56 * 1024 * 1024
ADAMW_BLOCK_BYTES = 768 * 1024

MESH = pl.DeviceIdType.MESH
ANY = pl.BlockSpec(memory_space=pl.ANY)

SH_M, SC_M, GT_M, SH_F, SC_F, GT_F = range(6)

SMALL = (("b_ada", 6144), ("g_pre_mix", 1024), ("g_post_mix", 1024), ("b_in", 4608),
         ("rel_bias", 2176), ("b_dw_conv", 512), ("g_conv_ln", 512), ("b_conv_ln", 512),
         ("b_conv_o", 1024), ("g_pre_ffn", 1024), ("g_post_ffn", 1024), ("b_dw_ffn", 5632))
SMALL_TOTAL = sum(n for _, n in SMALL)


def _cparams(n_axes):
    return pltpu.CompilerParams(vmem_limit_bytes=VMEM_LIMIT_BYTES,
                                dimension_semantics=("arbitrary",) * n_axes)


def _sig(v):
    return 1.0 / (1.0 + jnp.exp(-v))


def _pick(n, target):
    if n <= target:
        return n
    t = target - target % 128
    while n % t:
        t -= 128
    return t


def _tile(rows, cols, col=0):
    return pl.BlockSpec((rows, cols), lambda i: (i, col))


def _full(shape):
    zeros = (0,) * len(shape)
    return pl.BlockSpec(shape, lambda i: zeros)


def _prev(halo, cols, rows, col=0):
    return pl.BlockSpec((halo, cols), lambda i: (jnp.maximum(i * (rows // halo) - 1, 0), col))


def _next(halo, cols, rows, n_blocks, col=0):
    return pl.BlockSpec((halo, cols), lambda i: (jnp.minimum((i + 1) * (rows // halo), n_blocks - 1), col))


class _Comm:
    def __init__(self, inputs, out_shapes, sems, start, finish):
        self.inputs, self.out_shapes, self.sems, self.start, self.finish = inputs, out_shapes, sems, start, finish


def _host_call(body, name, grid, in_specs, out_specs, out_shape, scratch_shapes, args, comm=None):
    n_in, n_out, n_scr = len(args), len(out_shape), len(scratch_shapes)
    c_in = list(comm.inputs) if comm else []
    c_out = list(comm.out_shapes) if comm else []
    c_sem = list(comm.sems) if comm else []

    def full(*refs):
        bounds = [0, n_in, len(c_in), n_out, len(c_out), n_scr, len(c_sem)]
        cuts = [sum(bounds[:i + 1]) for i in range(len(bounds))]
        ins, cins, outs, couts, scr, csems = (refs[lo:hi] for lo, hi in zip(cuts[:-1], cuts[1:]))
        if comm:
            first = functools.reduce(jnp.logical_and, [pl.program_id(ax) == 0 for ax in range(len(grid))])
            pl.when(first)(lambda: comm.start(cins, couts, csems))
        body(ins, outs, scr)
        if comm:
            last = functools.reduce(jnp.logical_and, [pl.program_id(ax) == grid[ax] - 1 for ax in range(len(grid))])
            pl.when(last)(lambda: comm.finish(cins, couts, csems))

    res = pl.pallas_call(
        full, name=name, grid=grid, in_specs=list(in_specs) + [ANY] * len(c_in),
        out_specs=list(out_specs) + [ANY] * len(c_out), out_shape=list(out_shape) + c_out,
        scratch_shapes=list(scratch_shapes) + c_sem, compiler_params=_cparams(len(grid)),
    )(*args, *c_in)
    return list(res[:n_out]), list(res[n_out:])


def _run_comm(comm, name):
    n_in, n_out = len(comm.inputs), len(comm.out_shapes)

    def body(*refs):
        ins, outs, sems = refs[:n_in], refs[n_in:n_in + n_out], refs[n_in + n_out:]
        comm.start(ins, outs, sems)
        comm.finish(ins, outs, sems)

    return pl.pallas_call(
        body, name=name, out_shape=list(comm.out_shapes), in_specs=[ANY] * n_in, out_specs=[ANY] * n_out,
        scratch_shapes=list(comm.sems),
    )(*comm.inputs)


def _place():
    return lax.axis_index("x"), lax.axis_index("y"), lax.axis_index("c")


def _gather_comm(arrs):
    n = len(arrs)

    def plan(ins, outs, sems):
        send_sems, recv_sems, local_sems = sems
        x, y, c = _place()
        me, sibling = (x, y, c), (x, y, 1 - c)
        chips = [(1 - x, y), (x, 1 - y), (1 - x, 1 - y)]

        def block(k, p):
            return outs[k].at[4 * p[0] + 2 * p[1] + p[2]]

        def copy(k, s, blk, to, src=None):
            return pltpu.make_async_remote_copy(
                src_ref=block(k, blk) if src is None else src, dst_ref=block(k, blk),
                send_sem=send_sems.at[7 * k + s], recv_sem=recv_sems.at[7 * k + s],
                device_id=to, device_id_type=MESH)

        mine = [pltpu.make_async_copy(ins[k], block(k, me), local_sems.at[k]) for k in range(n)]
        first = []
        for k in range(n):
            first.append(copy(k, 0, me, sibling, src=ins[k]))
            for j, chip in enumerate(chips):
                first.append(copy(k, 1 + j, me, (*chip, c), src=ins[k]))
        return me, sibling, chips, c, copy, mine, first

    def start(ins, outs, sems):
        *_, mine, first = plan(ins, outs, sems)
        for cp in mine + first:
            cp.start()

    def finish(ins, outs, sems):
        me, sibling, chips, c, copy, mine, first = plan(ins, outs, sems)
        passed = []
        for j, chip in enumerate(chips):
            for k in range(n):
                copy(k, 1 + j, (*chip, c), me).wait_recv()
                fwd = copy(k, 4 + j, (*chip, c), sibling)
                fwd.start()
                passed.append(fwd)
        for k in range(n):
            copy(k, 0, sibling, me).wait_recv()
        for j, chip in enumerate(chips):
            for k in range(n):
                copy(k, 4 + j, (*chip, 1 - c), me).wait_recv()
        for cp in first + passed:
            cp.wait_send()
        for cp in mine:
            cp.wait()

    return _Comm(list(arrs), [jax.ShapeDtypeStruct((N_DEV,) + a.shape, a.dtype) for a in arrs],
                 [pltpu.SemaphoreType.DMA((7 * n,)), pltpu.SemaphoreType.DMA((7 * n,)),
                  pltpu.SemaphoreType.DMA((n,))], start, finish)


def _scatter_comm(blocks):
    n = len(blocks)

    def plan(ins, outs, sems):
        send_sems, recv_sems, local_sems = sems
        x, y, c = _place()
        me = 4 * x + 2 * y + c
        local = [pltpu.make_async_copy(ins[k].at[me], outs[k].at[me], local_sems.at[k]) for k in range(n)]
        sends, recvs = [], []
        for k in range(n):
            for mask in range(1, N_DEV):
                px = 1 - x if mask & 4 else x
                py = 1 - y if mask & 2 else y
                pc = 1 - c if mask & 1 else c
                peer = 4 * px + 2 * py + pc
                sem = 7 * k + mask - 1
                both = dict(send_sem=send_sems.at[sem], recv_sem=recv_sems.at[sem], device_id=(px, py, pc),
                            device_id_type=MESH)
                sends.append(pltpu.make_async_remote_copy(src_ref=ins[k].at[peer], dst_ref=outs[k].at[me], **both))
                recvs.append(pltpu.make_async_remote_copy(src_ref=ins[k].at[me], dst_ref=outs[k].at[peer], **both))
        return local, sends, recvs

    def start(ins, outs, sems):
        local, sends, _ = plan(ins, outs, sems)
        for cp in local + sends:
            cp.start()

    def finish(ins, outs, sems):
        local, sends, recvs = plan(ins, outs, sems)
        for cp in recvs:
            cp.wait_recv()
        for cp in sends:
            cp.wait_send()
        for cp in local:
            cp.wait()

    return _Comm(list(blocks), [jax.ShapeDtypeStruct(b.shape, b.dtype) for b in blocks],
                 [pltpu.SemaphoreType.DMA((7 * n,)), pltpu.SemaphoreType.DMA((7 * n,)),
                  pltpu.SemaphoreType.DMA((n,))], start, finish)


_DIMS = {"nn": (((1,), (0,)), ((), ())), "nt": (((1,), (1,)), ((), ())), "tn": (((0,), (0,)), ((), ()))}


def _mm(a, b, mode, out_dtype, name, bias=None, tm=512, tn=512, comm=None):
    assert a.dtype == BF16 and b.dtype == BF16
    if mode == "tn":
        k_dim, m_dim = a.shape
    else:
        m_dim, k_dim = a.shape
    n_dim = b.shape[0] if mode == "nt" else b.shape[1]
    tm, tn = _pick(m_dim, tm), _pick(n_dim, tn)
    a_spec = (pl.BlockSpec((k_dim, tm), lambda i, j: (0, i)) if mode == "tn"
              else pl.BlockSpec((tm, k_dim), lambda i, j: (i, 0)))
    b_spec = (pl.BlockSpec((tn, k_dim), lambda i, j: (j, 0)) if mode == "nt"
              else pl.BlockSpec((k_dim, tn), lambda i, j: (0, j)))
    in_specs = [a_spec, b_spec]
    args = [a, b]
    if bias is not None:
        in_specs.append(pl.BlockSpec((1, tn), lambda i, j: (0, j)))
        args.append(bias)
    dims = _DIMS[mode]

    def body(ins, outs, scratch):
        total = lax.dot_general(ins[0][...], ins[1][...], dims, preferred_element_type=F32)
        if bias is not None:
            total = total + ins[2][...]
        outs[0][...] = total.astype(out_dtype)

    (out,), extra = _host_call(
        body, name, grid=(m_dim // tm, n_dim // tn), in_specs=in_specs,
        out_specs=[pl.BlockSpec((tm, tn), lambda i, j: (i, j))],
        out_shape=[jax.ShapeDtypeStruct((m_dim, n_dim), out_dtype)], scratch_shapes=[], args=args, comm=comm)
    return out if comm is None else (out, extra)


def _adam_math(w, g, m, v):
    m = ADAM_B1 * m + (1.0 - ADAM_B1) * g
    v = ADAM_B2 * v + (1.0 - ADAM_B2) * (g * g)
    m_hat = m / (1.0 - ADAM_B1 ** ADAM_STEP)
    v_hat = v / (1.0 - ADAM_B2 ** ADAM_STEP)
    delta = -ADAM_LR * (m_hat / (jnp.sqrt(v_hat) + ADAM_EPS) + ADAM_WD * w)
    return delta, m, v


def _adamw(w, m, v, name, g=None, parts=None):
    rows, cols = w.shape
    tr = rows
    if rows * cols * 4 > ADAMW_BLOCK_BYTES:
        tr = max(t for t in range(16, rows, 16) if rows % t == 0 and t * cols * 4 <= ADAMW_BLOCK_BYTES)

    def body(w_ref, m_ref, v_ref, g_ref, go_ref, d_ref, mo_ref, vo_ref):
        if parts is None:
            grad = g_ref[...]
        else:
            grad = g_ref[0].astype(F32)
            for d in range(1, N_DEV):
                grad = grad + g_ref[d].astype(F32)
        delta, m_new, v_new = _adam_math(w_ref[...], grad, m_ref[...], v_ref[...])
        go_ref[...] = grad
        d_ref[...] = delta
        mo_ref[...] = m_new
        vo_ref[...] = v_new

    spec = _tile(tr, cols)
    g_spec = spec if parts is None else pl.BlockSpec((N_DEV, tr, cols), lambda i: (0, i, 0))
    shape = jax.ShapeDtypeStruct((rows, cols), F32)
    return pl.pallas_call(
        body, name=name, out_shape=[shape] * 4, grid=(rows // tr,),
        in_specs=[spec, spec, spec, g_spec], out_specs=[spec] * 4, compiler_params=_cparams(1),
    )(w, m, v, g if parts is None else parts)


def _small_adamw(gathered, w, m, v):
    total = gathered.shape[1]

    def body(g_ref, w_ref, m_ref, v_ref, go_ref, d_ref, mo_ref, vo_ref):
        grad = g_ref[0:1, :]
        for d in range(1, N_DEV):
            grad = grad + g_ref[d:d + 1, :]
        go_ref[...] = grad
        delta, m_new, v_new = _adam_math(w_ref[...], grad[:, :SMALL_TOTAL], m_ref[...], v_ref[...])
        d_ref[...] = delta
        mo_ref[...] = m_new
        vo_ref[...] = v_new

    small = jax.ShapeDtypeStruct((1, SMALL_TOTAL), F32)
    return pl.pallas_call(
        body, name="small_adamw",
        out_shape=[jax.ShapeDtypeStruct((1, total), F32), small, small, small],
        compiler_params=pltpu.CompilerParams(vmem_limit_bytes=VMEM_LIMIT_BYTES),
    )(gathered, w, m, v)


def _silu_vec(c):
    def body(c_ref, o_ref):
        v = c_ref[...]
        o_ref[...] = v * _sig(v)

    return pl.pallas_call(body, name="silu_c", out_shape=jax.ShapeDtypeStruct(c.shape, F32))(c)


def _ada_fwd(c_all, w_shard):
    def body(c_ref, w_ref, o_ref):
        o_ref[...] = jnp.dot(c_ref[...], w_ref[...], precision=HIGHEST, preferred_element_type=F32)

    return pl.pallas_call(
        body, name="ada_fwd", out_shape=jax.ShapeDtypeStruct((N_DEV, w_shard.shape[1]), F32),
        compiler_params=pltpu.CompilerParams(vmem_limit_bytes=VMEM_LIMIT_BYTES),
    )(c_all, w_shard)


def _ada_grad(c_all, dmod_shard):
    def body(c_ref, d_ref, o_ref):
        o_ref[...] = lax.dot_general(c_ref[...], d_ref[...], _DIMS["tn"], precision=HIGHEST,
                                     preferred_element_type=F32)

    return pl.pallas_call(
        body, name="ada_grad", out_shape=jax.ShapeDtypeStruct((D_MODEL, dmod_shard.shape[1]), F32),
        compiler_params=pltpu.CompilerParams(vmem_limit_bytes=VMEM_LIMIT_BYTES),
    )(c_all, dmod_shard)


ROWS = 256


def _rms(v):
    r = lax.rsqrt(jnp.mean(v * v, axis=-1, keepdims=True) + EPS)
    return v * r, r


def _rms_bwd(dxn, xn, r):
    return r * (dxn - xn * jnp.mean(dxn * xn, axis=-1, keepdims=True))


def _colsum(v):
    return jnp.sum(v, axis=0, keepdims=True)


def _pre_mix(x, mod6, g1):
    seq = x.shape[0]

    def body(x_ref, mod_ref, g_ref, h_ref):
        xn, _ = _rms(x_ref[...])
        y = xn * g_ref[...]
        h_ref[...] = (y * (1.0 + mod_ref[SC_M:SC_M + 1, :]) + mod_ref[SH_M:SH_M + 1, :]).astype(BF16)

    return pl.pallas_call(
        body, name="pre_mix", out_shape=jax.ShapeDtypeStruct((seq, D_MODEL), BF16), grid=(seq // ROWS,),
        in_specs=[_tile(ROWS, D_MODEL), _full((6, D_MODEL)), _full((1, D_MODEL))],
        out_specs=_tile(ROWS, D_MODEL), compiler_params=_cparams(1),
    )(x, mod6, g1)


def _post_mix_pre_ffn(ymix, x, mod6, g2, g3):
    seq = x.shape[0]

    def body(y_ref, x_ref, mod_ref, g2_ref, g3_ref, x1_ref, h_ref):
        yn, _ = _rms(y_ref[...])
        x1 = x_ref[...] + mod_ref[GT_M:GT_M + 1, :] * (yn * g2_ref[...])
        x1_ref[...] = x1
        xn, _ = _rms(x1)
        y3 = xn * g3_ref[...]
        h_ref[...] = (y3 * (1.0 + mod_ref[SC_F:SC_F + 1, :]) + mod_ref[SH_F:SH_F + 1, :]).astype(BF16)

    return pl.pallas_call(
        body, name="post_mix_pre_ffn",
        out_shape=[jax.ShapeDtypeStruct((seq, D_MODEL), F32), jax.ShapeDtypeStruct((seq, D_MODEL), BF16)],
        grid=(seq // ROWS,),
        in_specs=[_tile(ROWS, D_MODEL), _tile(ROWS, D_MODEL), _full((6, D_MODEL)), _full((1, D_MODEL)),
                  _full((1, D_MODEL))],
        out_specs=[_tile(ROWS, D_MODEL), _tile(ROWS, D_MODEL)], compiler_params=_cparams(1),
    )(ymix, x, mod6, g2, g3)


def _final(yf, x1, target, mod6, g4):
    seq = x1.shape[0]

    def body(y_ref, x1_ref, t_ref, mod_ref, g_ref, loss_ref, dout_ref, dyf_ref, small_ref):
        i = pl.program_id(0)

        @pl.when(i == 0)
        def _():
            loss_ref[...] = jnp.zeros_like(loss_ref)
            small_ref[...] = jnp.zeros_like(small_ref)

        gt = mod_ref[GT_F:GT_F + 1, :]
        g4v = g_ref[...]
        yn, r = _rms(y_ref[...])
        out = x1_ref[...] + gt * (yn * g4v)
        err = out - t_ref[...]
        loss_ref[...] += jnp.sum(jnp.mean(err * err, axis=-1, keepdims=True))
        dout = err * (1.0 / D_MODEL)
        dout_ref[...] = dout
        small_ref[0:1, :] += _colsum(dout * gt * yn)
        small_ref[1:2, :] += _colsum(dout * (yn * g4v))
        dyf_ref[...] = _rms_bwd(dout * gt * g4v, yn, r).astype(BF16)

    return pl.pallas_call(
        body, name="final_loss",
        out_shape=[jax.ShapeDtypeStruct((1, 128), F32), jax.ShapeDtypeStruct((seq, D_MODEL), F32),
                   jax.ShapeDtypeStruct((seq, D_MODEL), BF16), jax.ShapeDtypeStruct((8, D_MODEL), F32)],
        grid=(seq // ROWS,),
        in_specs=[_tile(ROWS, D_MODEL)] * 3 + [_full((6, D_MODEL)), _full((1, D_MODEL))],
        out_specs=[_full((1, 128)), _tile(ROWS, D_MODEL), _tile(ROWS, D_MODEL), _full((8, D_MODEL))],
        compiler_params=_cparams(1),
    )(yf, x1, target, mod6, g4)


def _mid_bwd(dh2, x1, dout, ymix, mod6, g3, g2):
    seq = x1.shape[0]

    def body(dh_ref, x1_ref, dout_ref, y_ref, mod_ref, g3_ref, g2_ref, dx1_ref, dy_ref, small_ref):
        i = pl.program_id(0)

        @pl.when(i == 0)
        def _():
            small_ref[...] = jnp.zeros_like(small_ref)

        dh = dh_ref[...]
        g3v, g2v = g3_ref[...], g2_ref[...]
        xn, r3 = _rms(x1_ref[...])
        y3 = xn * g3v
        dy3 = dh * (1.0 + mod_ref[SC_F:SC_F + 1, :])
        small_ref[0:1, :] += _colsum(dy3 * xn)
        small_ref[1:2, :] += _colsum(dh * y3)
        small_ref[2:3, :] += _colsum(dh)
        dx1 = dout_ref[...] + _rms_bwd(dy3 * g3v, xn, r3)
        dx1_ref[...] = dx1
        gt = mod_ref[GT_M:GT_M + 1, :]
        yn, r2 = _rms(y_ref[...])
        small_ref[3:4, :] += _colsum(dx1 * gt * yn)
        small_ref[4:5, :] += _colsum(dx1 * (yn * g2v))
        dy_ref[...] = _rms_bwd(dx1 * gt * g2v, yn, r2).astype(BF16)

    return pl.pallas_call(
        body, name="mid_bwd",
        out_shape=[jax.ShapeDtypeStruct((seq, D_MODEL), F32), jax.ShapeDtypeStruct((seq, D_MODEL), BF16),
                   jax.ShapeDtypeStruct((8, D_MODEL), F32)],
        grid=(seq // ROWS,),
        in_specs=[_tile(ROWS, D_MODEL)] * 4 + [_full((6, D_MODEL)), _full((1, D_MODEL)), _full((1, D_MODEL))],
        out_specs=[_tile(ROWS, D_MODEL), _tile(ROWS, D_MODEL), _full((8, D_MODEL))],
        compiler_params=_cparams(1),
    )(dh2, x1, dout, ymix, mod6, g3, g2)


def _pre_mix_bwd(dh1, x, dx1, mod6, g1):
    seq = x.shape[0]

    def body(dh_ref, x_ref, dx1_ref, mod_ref, g_ref, dx_ref, small_ref):
        i = pl.program_id(0)

        @pl.when(i == 0)
        def _():
            small_ref[...] = jnp.zeros_like(small_ref)

        dh = dh_ref[...]
        g1v = g_ref[...]
        xn, r = _rms(x_ref[...])
        dy = dh * (1.0 + mod_ref[SC_M:SC_M + 1, :])
        small_ref[0:1, :] += _colsum(dy * xn)
        small_ref[1:2, :] += _colsum(dh * (xn * g1v))
        small_ref[2:3, :] += _colsum(dh)
        dx_ref[...] = dx1_ref[...] + _rms_bwd(dy * g1v, xn, r)

    return pl.pallas_call(
        body, name="pre_mix_bwd",
        out_shape=[jax.ShapeDtypeStruct((seq, D_MODEL), F32), jax.ShapeDtypeStruct((8, D_MODEL), F32)],
        grid=(seq // ROWS,),
        in_specs=[_tile(ROWS, D_MODEL)] * 3 + [_full((6, D_MODEL)), _full((1, D_MODEL))],
        out_specs=[_tile(ROWS, D_MODEL), _full((8, D_MODEL))], compiler_params=_cparams(1),
    )(dh1, x, dx1, mod6, g1)


def _toeplitz_onehot(shape, offset_axis):
    m = lax.broadcasted_iota(jnp.int32, shape, offset_axis)
    i = lax.broadcasted_iota(jnp.int32, shape, 1 - offset_axis)
    return (i == jnp.clip(BAND - 1 - m, -MAX_REL, MAX_REL) + MAX_REL).astype(F32)


def _bias_table(rel_bias):
    def body(rb_ref, o_ref, t_ref):
        t_ref[...] = jnp.dot(rb_ref[...], _toeplitz_onehot((N_REL, TOEPLITZ), 1), precision=HIGHEST,
                             preferred_element_type=F32)
        for qi in range(CHUNK):
            o_ref[qi] = t_ref[:, CHUNK - 1 - qi:CHUNK - 1 - qi + BAND]

    return pl.pallas_call(
        body, name="bias_table", out_shape=jax.ShapeDtypeStruct((CHUNK, N_HEADS, BAND), F32),
        scratch_shapes=[pltpu.VMEM((N_HEADS, TOEPLITZ), F32)],
    )(rel_bias)


def _bias_grad(dbias_q):
    def body(d_ref, o_ref, t_ref):
        t_ref[...] = jnp.zeros_like(t_ref)
        for qi in range(CHUNK):
            t_ref[:, CHUNK - 1 - qi:CHUNK - 1 - qi + BAND] += d_ref[qi]
        o_ref[...] = jnp.dot(t_ref[...], _toeplitz_onehot((TOEPLITZ, N_REL), 0), precision=HIGHEST,
                             preferred_element_type=F32)

    return pl.pallas_call(
        body, name="bias_grad", out_shape=jax.ShapeDtypeStruct((N_HEADS, N_REL), F32),
        scratch_shapes=[pltpu.VMEM((N_HEADS, TOEPLITZ), F32)],
    )(dbias_q)


def _group_table(bias_q):
    per_chunk = jnp.transpose(bias_q, (1, 0, 2))
    table = jnp.full((N_HEADS, GROUP_Q, GROUP_K), NEG_INF, F32)
    for g in range(GROUP):
        table = table.at[:, g * CHUNK:(g + 1) * CHUNK, g * CHUNK:g * CHUNK + BAND].set(per_chunk)
    return table


def _load_resident(pairs, sems):
    copies = [pltpu.make_async_copy(src, dst, sems.at[n]) for n, (src, dst) in enumerate(pairs)]
    for cp in copies:
        cp.start()
    for cp in copies:
        cp.wait()


def _softmax_rows(s_ref, t_ref, valid, rows):
    s = s_ref[rows, :] * (HEAD_DIM ** -0.5) + t_ref[rows, :]
    s = jnp.where(valid, s, NEG_INF)
    e = jnp.exp(s - jnp.max(s, axis=-1, keepdims=True))
    return e / jnp.sum(e, axis=-1, keepdims=True)


def _valid_keys(g):
    kj = lax.broadcasted_iota(jnp.int32, (SOFTMAX_ROWS, GROUP_K), 1)
    return kj >= PAD_ROWS - g * GROUP_Q


def _attn_fwd(qkv, kpad, vpad, table, comm=None):
    seq = qkv.shape[0]

    def body(ins, outs, scratch):
        q_ref, k_hbm, v_hbm, t_hbm = ins
        (o_ref,) = outs
        k_ref, v_ref, t_ref, s_ref, p_ref, sems = scratch
        g = pl.program_id(0)

        @pl.when(g == 0)
        def _():
            _load_resident(((k_hbm, k_ref), (v_hbm, v_ref), (t_hbm, t_ref)), sems)

        window = pl.ds(pl.multiple_of(g * GROUP_Q, GROUP_Q), GROUP_K)
        valid = _valid_keys(g)
        for h in range(N_HEADS):
            cols = slice(h * HEAD_DIM, (h + 1) * HEAD_DIM)
            buf = h % 2
            s_ref[buf] = lax.dot_general(q_ref[:, cols], k_ref[window, cols], _DIMS["nt"],
                                         preferred_element_type=F32)
            for r in range(GROUP_Q // SOFTMAX_ROWS):
                rows = slice(r * SOFTMAX_ROWS, (r + 1) * SOFTMAX_ROWS)
                p_ref[buf, rows, :] = _softmax_rows(s_ref.at[buf], t_ref.at[h], valid, rows).astype(BF16)
            o_ref[:, cols] = jnp.dot(p_ref[buf], v_ref[window, cols], preferred_element_type=F32).astype(BF16)

    (ao,), extra = _host_call(
        body, "attn_fwd", grid=(seq // GROUP_Q,),
        in_specs=[_tile(GROUP_Q, D_ATTN), ANY, ANY, ANY], out_specs=[_tile(GROUP_Q, D_ATTN)],
        out_shape=[jax.ShapeDtypeStruct((seq, D_ATTN), BF16)],
        scratch_shapes=[pltpu.VMEM(kpad.shape, BF16), pltpu.VMEM(vpad.shape, BF16), pltpu.VMEM(table.shape, F32),
                        pltpu.VMEM((2, GROUP_Q, GROUP_K), F32), pltpu.VMEM((2, GROUP_Q, GROUP_K), BF16),
                        pltpu.SemaphoreType.DMA((3,))],
        args=[qkv, kpad, vpad, table], comm=comm)
    return ao, extra


def _attn_bwd(qkv, kpad, vpad, table, dao, comm=None):
    seq = qkv.shape[0]
    n_groups = seq // GROUP_Q
    fold_w = GROUP_K + (GROUP - 1) * CHUNK

    def body(ins, outs, scratch):
        q_ref, do_ref, k_hbm, v_hbm, t_hbm = ins
        dq_ref, dkt_hbm, dvt_hbm, db_ref, cs_ref, csk_ref, csv_ref = outs
        k_ref, v_ref, t_ref, db_acc, dkt_acc, dvt_acc, s_ref, dp_ref, p_ref, ds_ref, sems = scratch
        g = pl.program_id(0)

        @pl.when(g == 0)
        def _():
            _load_resident(((k_hbm, k_ref), (v_hbm, v_ref), (t_hbm, t_ref)), sems)
            db_acc[...] = jnp.zeros_like(db_acc)
            dkt_acc[...] = jnp.zeros_like(dkt_acc)
            dvt_acc[...] = jnp.zeros_like(dvt_acc)
            cs_ref[...] = jnp.zeros_like(cs_ref)

        window = pl.ds(pl.multiple_of(g * GROUP_Q, GROUP_Q), GROUP_K)
        valid = _valid_keys(g)
        for h in range(N_HEADS):
            cols = slice(h * HEAD_DIM, (h + 1) * HEAD_DIM)
            buf = h % 2
            qh, doh = q_ref[:, cols], do_ref[:, cols]
            kh, vh = k_ref[window, cols], v_ref[window, cols]
            s_ref[buf] = lax.dot_general(qh, kh, _DIMS["nt"], preferred_element_type=F32)
            dp_ref[buf] = lax.dot_general(doh, vh, _DIMS["nt"], preferred_element_type=F32)
            for r in range(GROUP_Q // SOFTMAX_ROWS):
                rows = slice(r * SOFTMAX_ROWS, (r + 1) * SOFTMAX_ROWS)
                p = _softmax_rows(s_ref.at[buf], t_ref.at[h], valid, rows)
                dp = dp_ref[buf, rows, :]
                ds = p * (dp - jnp.sum(dp * p, axis=-1, keepdims=True))
                chunk = (r * SOFTMAX_ROWS) // CHUNK
                shift = (GROUP - 1 - chunk) * CHUNK
                local = slice(r * SOFTMAX_ROWS - chunk * CHUNK, (r + 1) * SOFTMAX_ROWS - chunk * CHUNK)
                db_acc[h, local, shift:shift + GROUP_K] += ds
                p_ref[buf, rows, :] = p.astype(BF16)
                ds_ref[buf, rows, :] = (ds * (HEAD_DIM ** -0.5)).astype(BF16)
            dq_ref[:, cols] = jnp.dot(ds_ref[buf], kh, preferred_element_type=F32).astype(BF16)
            dkt_acc[cols, window] += lax.dot_general(qh, ds_ref[buf], _DIMS["tn"], preferred_element_type=F32)
            dvt_acc[cols, window] += lax.dot_general(doh, p_ref[buf], _DIMS["tn"], preferred_element_type=F32)
        cs_ref[0:1, :] += _colsum(dq_ref[...].astype(F32))

        @pl.when(g == n_groups - 1)
        def _():
            lo = (GROUP - 1) * CHUNK
            for h in range(N_HEADS):
                db_ref[h] = db_acc[h, :, lo:lo + BAND]
            inside = pl.ds(PAD_ROWS, seq)
            csk_ref[...] = jnp.sum(dkt_acc[:, inside], axis=1, keepdims=True)
            csv_ref[...] = jnp.sum(dvt_acc[:, inside], axis=1, keepdims=True)
            out_k = pltpu.make_async_copy(dkt_acc.at[:, inside], dkt_hbm, sems.at[0])
            out_v = pltpu.make_async_copy(dvt_acc.at[:, inside], dvt_hbm, sems.at[1])
            out_k.start()
            out_v.start()
            out_k.wait()
            out_v.wait()

    t_shape = (D_ATTN, seq + PAD_ROWS)
    outs, extra = _host_call(
        body, "attn_bwd", grid=(n_groups,),
        in_specs=[_tile(GROUP_Q, D_ATTN), _tile(GROUP_Q, D_ATTN), ANY, ANY, ANY],
        out_specs=[_tile(GROUP_Q, D_ATTN), ANY, ANY, _full((N_HEADS, CHUNK, BAND)), _full((8, D_ATTN)),
                   _full((D_ATTN, 1)), _full((D_ATTN, 1))],
        out_shape=[jax.ShapeDtypeStruct((seq, D_ATTN), BF16), jax.ShapeDtypeStruct((D_ATTN, seq), F32),
                   jax.ShapeDtypeStruct((D_ATTN, seq), F32), jax.ShapeDtypeStruct((N_HEADS, CHUNK, BAND), F32),
                   jax.ShapeDtypeStruct((8, D_ATTN), F32), jax.ShapeDtypeStruct((D_ATTN, 1), F32),
                   jax.ShapeDtypeStruct((D_ATTN, 1), F32)],
        scratch_shapes=[pltpu.VMEM(kpad.shape, BF16), pltpu.VMEM(vpad.shape, BF16), pltpu.VMEM(table.shape, F32),
                        pltpu.VMEM((N_HEADS, CHUNK, fold_w), F32), pltpu.VMEM(t_shape, F32),
                        pltpu.VMEM(t_shape, F32), pltpu.VMEM((2, GROUP_Q, GROUP_K), F32),
                        pltpu.VMEM((2, GROUP_Q, GROUP_K), F32), pltpu.VMEM((2, GROUP_Q, GROUP_K), BF16),
                        pltpu.VMEM((2, GROUP_Q, GROUP_K), BF16), pltpu.SemaphoreType.DMA((3,))],
        args=[qkv, dao, kpad, vpad, table], comm=comm)
    return outs, extra


CONV_ROWS = 256


def _ln_silu(u1, g, b):
    mu = jnp.mean(u1, axis=-1, keepdims=True)
    xc = u1 - mu
    rs = lax.rsqrt(jnp.mean(xc * xc, axis=-1, keepdims=True) + EPS)
    xhat = xc * rs
    u2 = xhat * g + b
    return xhat, rs, u2


def _glu_into(s_ref, a_ref, b_ref, ah_ref, bh_ref, first):
    halo = ah_ref[...] * _sig(bh_ref[...])
    s_ref[0:CONV_HALO, :] = jnp.where(first, 0.0, halo)
    s_ref[CONV_HALO:CONV_HALO + CONV_ROWS, :] = a_ref[...] * _sig(b_ref[...])


def _conv_fwd(zr, w_dw, b_dw, g_ln, b_ln):
    seq = zr.shape[0]

    def body(a_ref, b_ref, ah_ref, bh_ref, w_ref, bias_ref, g_ref, bl_ref, u1_ref, u3_ref, s_ref):
        _glu_into(s_ref, a_ref, b_ref, ah_ref, bh_ref, pl.program_id(0) == 0)
        acc = jnp.zeros((CONV_ROWS, D_CONV), F32) + bias_ref[...]
        for j in range(CONV_K):
            acc = acc + w_ref[j:j + 1, :] * s_ref[2 + j:2 + j + CONV_ROWS, :]
        u1_ref[...] = acc
        _, _, u2 = _ln_silu(acc, g_ref[...], bl_ref[...])
        u3_ref[...] = (u2 * _sig(u2)).astype(BF16)

    return pl.pallas_call(
        body, name="conv_fwd",
        out_shape=[jax.ShapeDtypeStruct((seq, D_CONV), F32), jax.ShapeDtypeStruct((seq, D_CONV), BF16)],
        grid=(seq // CONV_ROWS,),
        in_specs=[_tile(CONV_ROWS, D_CONV, 0), _tile(CONV_ROWS, D_CONV, 1),
                  _prev(CONV_HALO, D_CONV, CONV_ROWS, 0), _prev(CONV_HALO, D_CONV, CONV_ROWS, 1),
                  _full((CONV_K, D_CONV)), _full((1, D_CONV)), _full((1, D_CONV)), _full((1, D_CONV))],
        out_specs=[_tile(CONV_ROWS, D_CONV), _tile(CONV_ROWS, D_CONV)],
        scratch_shapes=[pltpu.VMEM((CONV_HALO + CONV_ROWS, D_CONV), F32)],
        compiler_params=_cparams(1),
    )(zr, zr, zr, zr, w_dw, b_dw, g_ln, b_ln)


def _conv_bwd(zr, u1, du3, w_dw, g_ln, b_ln):
    seq = zr.shape[0]
    n_tiles = seq // CONV_ROWS
    n_halo = seq // CONV_HALO
    ext = CONV_ROWS + CONV_HALO

    def body(a_ref, b_ref, ah_ref, bh_ref, u1_ref, u1n_ref, d3_ref, d3n_ref, w_ref, g_ref, bl_ref,
             da_ref, db_ref, dw_ref, small_ref, s_ref, d_ref):
        i = pl.program_id(0)

        @pl.when(i == 0)
        def _():
            dw_ref[...] = jnp.zeros_like(dw_ref)
            small_ref[...] = jnp.zeros_like(small_ref)

        _glu_into(s_ref, a_ref, b_ref, ah_ref, bh_ref, i == 0)
        gv, bv = g_ref[...], bl_ref[...]

        def du1_of(u1, d3):
            xhat, rs, u2 = _ln_silu(u1, gv, bv)
            sg = _sig(u2)
            du2 = d3 * (sg * (1.0 + u2 * (1.0 - sg)))
            dxh = du2 * gv
            du1 = rs * (dxh - jnp.mean(dxh, axis=-1, keepdims=True)
                        - xhat * jnp.mean(dxh * xhat, axis=-1, keepdims=True))
            return du1, du2, xhat

        du1, du2, xhat = du1_of(u1_ref[...], d3_ref[...])
        du1n, _, _ = du1_of(u1n_ref[...], d3n_ref[...])
        d_ref[0:CONV_ROWS, :] = du1
        d_ref[CONV_ROWS:ext, :] = jnp.where(i == n_tiles - 1, 0.0, du1n)
        small_ref[0:1, :] += _colsum(du1)
        small_ref[1:2, :] += _colsum(du2 * xhat)
        small_ref[2:3, :] += _colsum(du2)
        du0 = jnp.zeros((CONV_ROWS, D_CONV), F32)
        for j in range(CONV_K):
            dw_ref[j:j + 1, :] += _colsum(du1 * s_ref[2 + j:2 + j + CONV_ROWS, :])
            du0 = du0 + w_ref[j:j + 1, :] * d_ref[CONV_K - 1 - j:CONV_K - 1 - j + CONV_ROWS, :]
        sb = _sig(b_ref[...])
        da = du0 * sb
        dbv = du0 * a_ref[...] * sb * (1.0 - sb)
        da_ref[...] = da.astype(BF16)
        db_ref[...] = dbv.astype(BF16)
        small_ref[3:4, :] += _colsum(da)
        small_ref[4:5, :] += _colsum(dbv)

    return pl.pallas_call(
        body, name="conv_bwd",
        out_shape=[jax.ShapeDtypeStruct((seq, D_CONV), BF16), jax.ShapeDtypeStruct((seq, D_CONV), BF16),
                   jax.ShapeDtypeStruct((CONV_HALO, D_CONV), F32), jax.ShapeDtypeStruct((8, D_CONV), F32)],
        grid=(n_tiles,),
        in_specs=[_tile(CONV_ROWS, D_CONV, 0), _tile(CONV_ROWS, D_CONV, 1),
                  _prev(CONV_HALO, D_CONV, CONV_ROWS, 0), _prev(CONV_HALO, D_CONV, CONV_ROWS, 1),
                  _tile(CONV_ROWS, D_CONV), _next(CONV_HALO, D_CONV, CONV_ROWS, n_halo),
                  _tile(CONV_ROWS, D_CONV), _next(CONV_HALO, D_CONV, CONV_ROWS, n_halo),
                  _full((CONV_K, D_CONV)), _full((1, D_CONV)), _full((1, D_CONV))],
        out_specs=[_tile(CONV_ROWS, D_CONV), _tile(CONV_ROWS, D_CONV), _full((CONV_HALO, D_CONV)),
                   _full((8, D_CONV))],
        scratch_shapes=[pltpu.VMEM((ext, D_CONV), F32), pltpu.VMEM((ext, D_CONV), F32)],
        compiler_params=_cparams(1),
    )(zr, zr, zr, zr, u1, u1, du3, du3, w_dw, g_ln, b_ln)


MERGE_ROWS = 256


def _merge_fwd(ao, u3, zr, w_ao, w_co, b_co):
    seq = ao.shape[0]

    def body(ao_ref, u3_ref, ga_ref, gb_ref, wa_ref, wc_ref, bc_ref, y_ref, a_ref, cb_ref):
        a = jnp.dot(ao_ref[...], wa_ref[...], preferred_element_type=F32)
        cb = jnp.dot(u3_ref[...], wc_ref[...], preferred_element_type=F32) + bc_ref[...]
        a_ref[...] = a
        cb_ref[...] = cb
        y_ref[...] = (_sig(ga_ref[...]) * a + _sig(gb_ref[...]) * cb).astype(BF16)

    f32_out = jax.ShapeDtypeStruct((seq, D_MODEL), F32)
    return pl.pallas_call(
        body, name="merge_fwd",
        out_shape=[jax.ShapeDtypeStruct((seq, D_MODEL), BF16), f32_out, f32_out],
        grid=(seq // MERGE_ROWS,),
        in_specs=[_tile(MERGE_ROWS, D_ATTN), _tile(MERGE_ROWS, D_CONV), _tile(MERGE_ROWS, D_MODEL, 1),
                  _tile(MERGE_ROWS, D_MODEL, 2), _full(w_ao.shape), _full(w_co.shape), _full((1, D_MODEL))],
        out_specs=[_tile(MERGE_ROWS, D_MODEL)] * 3, compiler_params=_cparams(1),
    )(ao, u3, zr, zr, w_ao, w_co, b_co)


def _merge_bwd(dy, a, cb, zr):
    seq = dy.shape[0]

    def body(dy_ref, a_ref, cb_ref, ga_ref, gb_ref, da_ref, dcb_ref, dga_ref, dgb_ref, small_ref):
        i = pl.program_id(0)

        @pl.when(i == 0)
        def _():
            small_ref[...] = jnp.zeros_like(small_ref)

        dy_v = dy_ref[...]
        sa, sb = _sig(ga_ref[...]), _sig(gb_ref[...])
        dcb = dy_v * sb
        dga = dy_v * a_ref[...] * sa * (1.0 - sa)
        dgb = dy_v * cb_ref[...] * sb * (1.0 - sb)
        da_ref[...] = (dy_v * sa).astype(BF16)
        dcb_ref[...] = dcb.astype(BF16)
        dga_ref[...] = dga.astype(BF16)
        dgb_ref[...] = dgb.astype(BF16)
        small_ref[0:1, :] += _colsum(dga)
        small_ref[1:2, :] += _colsum(dgb)
        small_ref[2:3, :] += _colsum(dcb)

    bf = jax.ShapeDtypeStruct((seq, D_MODEL), BF16)
    return pl.pallas_call(
        body, name="merge_bwd", out_shape=[bf, bf, bf, bf, jax.ShapeDtypeStruct((8, D_MODEL), F32)],
        grid=(seq // MERGE_ROWS,),
        in_specs=[_tile(MERGE_ROWS, D_MODEL)] * 3 + [_tile(MERGE_ROWS, D_MODEL, 1), _tile(MERGE_ROWS, D_MODEL, 2)],
        out_specs=[_tile(MERGE_ROWS, D_MODEL)] * 4 + [_full((8, D_MODEL))], compiler_params=_cparams(1),
    )(dy, a, cb, zr, zr)


FFN_ROWS = 512
FFN_BLOCKS = D_FF // FFN_COLS
GELU_C = math.sqrt(2.0 / math.pi)


def _gelu(v):
    t = jnp.tanh(GELU_C * (v + 0.044715 * (v * v * v)))
    return 0.5 * v * (1.0 + t), t


def _gelu_grad(v, t):
    return 0.5 * (1.0 + t) + 0.5 * v * (1.0 - t * t) * (GELU_C * (1.0 + 3.0 * 0.044715 * (v * v)))


def _conv3(s_ref, w_ref, bias, rows):
    base = FFN_HALO - 2
    return (w_ref[0:1, :] * s_ref[base:base + rows, :] + w_ref[1:2, :] * s_ref[base + 1:base + 1 + rows, :]
            + w_ref[2:3, :] * s_ref[base + 2:base + 2 + rows, :] + bias)


def _ffn_specs(rows):
    tile = lambda off: pl.BlockSpec((rows, FFN_COLS), lambda j, i: (i, j + off))
    prev = lambda off: pl.BlockSpec((FFN_HALO, FFN_COLS),
                                    lambda j, i: (jnp.maximum(i * (rows // FFN_HALO) - 1, 0), j + off))
    wgt = lambda off: pl.BlockSpec((3, FFN_COLS), lambda j, i: (0, j + off))
    vec = lambda off: pl.BlockSpec((1, FFN_COLS), lambda j, i: (0, j + off))
    return tile, prev, wgt, vec


def _ffn_act(up, w_dw, b_dw):
    seq = up.shape[0]
    tile, prev, wgt, vec = _ffn_specs(FFN_ROWS)

    def body(v_ref, g_ref, vp_ref, gp_ref, wv_ref, wg_ref, bv_ref, bg_ref, act_ref, sv_ref, sg_ref):
        first = pl.program_id(1) == 0
        sv_ref[0:FFN_HALO, :] = jnp.where(first, 0.0, vp_ref[...])
        sg_ref[0:FFN_HALO, :] = jnp.where(first, 0.0, gp_ref[...])
        sv_ref[FFN_HALO:FFN_HALO + FFN_ROWS, :] = v_ref[...]
        sg_ref[FFN_HALO:FFN_HALO + FFN_ROWS, :] = g_ref[...]
        val = _conv3(sv_ref, wv_ref, bv_ref[...], FFN_ROWS)
        gate = _conv3(sg_ref, wg_ref, bg_ref[...], FFN_ROWS)
        act_ref[...] = (_gelu(gate)[0] * val).astype(BF16)

    return pl.pallas_call(
        body, name="ffn_act", out_shape=jax.ShapeDtypeStruct((seq, D_FF), BF16),
        grid=(FFN_BLOCKS, seq // FFN_ROWS),
        in_specs=[tile(0), tile(FFN_BLOCKS), prev(0), prev(FFN_BLOCKS), wgt(0), wgt(FFN_BLOCKS),
                  vec(0), vec(FFN_BLOCKS)],
        out_specs=tile(0),
        scratch_shapes=[pltpu.VMEM((FFN_HALO + FFN_ROWS, FFN_COLS), F32)] * 2,
        compiler_params=_cparams(2),
    )(up, up, up, up, w_dw, w_dw, b_dw, b_dw)


def _ffn_act_bwd(up, dact, w_dw, b_dw, comm=None):
    seq = up.shape[0]
    n_tiles = seq // FFN_ROWS
    n_halo = seq // FFN_HALO
    ext = FFN_ROWS + FFN_HALO
    tile, prev, wgt, vec = _ffn_specs(FFN_ROWS)
    nxt = lambda off: pl.BlockSpec(
        (FFN_HALO, FFN_COLS), lambda j, i: (jnp.minimum((i + 1) * (FFN_ROWS // FFN_HALO), n_halo - 1), j + off))
    acc = lambda off: pl.BlockSpec((8, FFN_COLS), lambda j, i: (0, j + off))

    def body(v_ref, g_ref, vp_ref, gp_ref, vn_ref, gn_ref, da_ref, dan_ref, wv_ref, wg_ref, bv_ref, bg_ref,
             dv_out, dg_out, dwv_ref, dwg_ref, dbv_ref, dbg_ref, sv_ref, sg_ref, dvs_ref, dgs_ref):
        i = pl.program_id(1)
        first, last = i == 0, i == n_tiles - 1

        @pl.when(first)
        def _():
            for r in (dwv_ref, dwg_ref, dbv_ref, dbg_ref):
                r[...] = jnp.zeros_like(r)

        for s_ref, t_ref, p_ref, n_ref in ((sv_ref, v_ref, vp_ref, vn_ref), (sg_ref, g_ref, gp_ref, gn_ref)):
            s_ref[0:FFN_HALO, :] = jnp.where(first, 0.0, p_ref[...])
            s_ref[FFN_HALO:FFN_HALO + FFN_ROWS, :] = t_ref[...]
            s_ref[FFN_HALO + FFN_ROWS:FFN_HALO + ext, :] = n_ref[...]
        val = _conv3(sv_ref, wv_ref, bv_ref[...], ext)
        gate = _conv3(sg_ref, wg_ref, bg_ref[...], ext)
        gel, t = _gelu(gate)
        dvs_ref[0:FFN_ROWS, :] = da_ref[...].astype(F32)
        dvs_ref[FFN_ROWS:ext, :] = jnp.where(last, 0.0, dan_ref[...].astype(F32)[0:FFN_HALO, :])
        dact = dvs_ref[...]
        dvs_ref[...] = dact * gel
        dgs_ref[...] = dact * val * _gelu_grad(gate, t)
        for d_ref, s_ref, w_ref, o_ref, dw_ref, db_ref in ((dvs_ref, sv_ref, wv_ref, dv_out, dwv_ref, dbv_ref),
                                                           (dgs_ref, sg_ref, wg_ref, dg_out, dwg_ref, dbg_ref)):
            d0 = d_ref[0:FFN_ROWS, :]
            o_ref[...] = (w_ref[2:3, :] * d0 + w_ref[1:2, :] * d_ref[1:1 + FFN_ROWS, :]
                          + w_ref[0:1, :] * d_ref[2:2 + FFN_ROWS, :]).astype(BF16)
            db_ref[0:1, :] += _colsum(d0)
            for tap in range(3):
                lo = FFN_HALO - 2 + tap
                dw_ref[tap:tap + 1, :] += _colsum(d0 * s_ref[lo:lo + FFN_ROWS, :])

    half = jax.ShapeDtypeStruct((seq, D_FF), BF16)
    acc_shape = jax.ShapeDtypeStruct((8, D_FF), F32)
    return _host_call(
        lambda ins, outs, scratch: body(*ins, *outs, *scratch), "ffn_act_bwd", grid=(FFN_BLOCKS, n_tiles),
        in_specs=[tile(0), tile(FFN_BLOCKS), prev(0), prev(FFN_BLOCKS), nxt(0), nxt(FFN_BLOCKS),
                  tile(0), pl.BlockSpec((16, FFN_COLS), lambda j, i: (
                      jnp.minimum((i + 1) * (FFN_ROWS // 16), seq // 16 - 1), j)),
                  wgt(0), wgt(FFN_BLOCKS), vec(0), vec(FFN_BLOCKS)],
        out_specs=[tile(0), tile(0), acc(0), acc(0), acc(0), acc(0)],
        out_shape=[half, half, acc_shape, acc_shape, acc_shape, acc_shape],
        scratch_shapes=[pltpu.VMEM((FFN_HALO + ext, FFN_COLS), F32)] * 2 + [pltpu.VMEM((ext, FFN_COLS), F32)] * 2,
        args=[up, up, up, up, up, up, dact, dact, w_dw, w_dw, b_dw, b_dw], comm=comm)


def _cols_to_blocks(full_cols):
    k, n8 = full_cols.shape
    return jnp.transpose(full_cols.reshape(k, N_DEV, n8 // N_DEV), (1, 0, 2))


def _rows_to_blocks(full_rows):
    r8, n = full_rows.shape
    return full_rows.reshape(N_DEV, r8 // N_DEV, n)


def _blocks_to_cols(gathered):
    _, k, n = gathered.shape
    return jnp.transpose(gathered, (1, 0, 2)).reshape(k, N_DEV * n)


def _pack_small(values):
    pieces = []
    for name, width in SMALL:
        flat = values[name].reshape(1, -1)
        if flat.shape[1] < width:
            flat = jnp.pad(flat, ((0, 0), (0, width - flat.shape[1])))
        pieces.append(flat)
    return jnp.concatenate(pieces, axis=1)


def _unpack_small(vec, shapes):
    out, off = {}, 0
    for name, width in SMALL:
        n = math.prod(shapes[name])
        out[name] = vec[:, off:off + n].reshape(shapes[name])
        off += width
    return out


def kernel(x, c, w_ada, b_ada, g_pre_mix, g_post_mix, w_in, b_in, rel_bias, w_attn_o, w_dw_conv, b_dw_conv, g_conv_ln, b_conv_ln, w_conv_o, b_conv_o, w_mix_o, g_pre_ffn, g_post_ffn, w_up, w_dw_ffn, b_dw_ffn, w_down, loss_target, m_w_ada, m_b_ada, m_g_pre_mix, m_g_post_mix, m_w_in, m_b_in, m_rel_bias, m_w_attn_o, m_w_dw_conv, m_b_dw_conv, m_g_conv_ln, m_b_conv_ln, m_w_conv_o, m_b_conv_o, m_w_mix_o, m_g_pre_ffn, m_g_post_ffn, m_w_up, m_w_dw_ffn, m_b_dw_ffn, m_w_down, v_w_ada, v_b_ada, v_g_pre_mix, v_g_post_mix, v_w_in, v_b_in, v_rel_bias, v_w_attn_o, v_w_dw_conv, v_b_dw_conv, v_g_conv_ln, v_b_conv_ln, v_w_conv_o, v_b_conv_o, v_w_mix_o, v_g_pre_ffn, v_g_post_ffn, v_w_up, v_w_dw_ffn, v_b_dw_ffn, v_w_down):
    names = ["w_ada", "b_ada", "g_pre_mix", "g_post_mix", "w_in", "b_in", "rel_bias", "w_attn_o", "w_dw_conv",
             "b_dw_conv", "g_conv_ln", "b_conv_ln", "w_conv_o", "b_conv_o", "w_mix_o", "g_pre_ffn", "g_post_ffn",
             "w_up", "w_dw_ffn", "b_dw_ffn", "w_down"]
    weights = dict(zip(names, [w_ada, b_ada, g_pre_mix, g_post_mix, w_in, b_in, rel_bias, w_attn_o, w_dw_conv,
                               b_dw_conv, g_conv_ln, b_conv_ln, w_conv_o, b_conv_o, w_mix_o, g_pre_ffn,
                               g_post_ffn, w_up, w_dw_ffn, b_dw_ffn, w_down]))
    mom_m = dict(zip(names, [m_w_ada, m_b_ada, m_g_pre_mix, m_g_post_mix, m_w_in, m_b_in, m_rel_bias, m_w_attn_o,
                             m_w_dw_conv, m_b_dw_conv, m_g_conv_ln, m_b_conv_ln, m_w_conv_o, m_b_conv_o,
                             m_w_mix_o, m_g_pre_ffn, m_g_post_ffn, m_w_up, m_w_dw_ffn, m_b_dw_ffn, m_w_down]))
    mom_v = dict(zip(names, [v_w_ada, v_b_ada, v_g_pre_mix, v_g_post_mix, v_w_in, v_b_in, v_rel_bias, v_w_attn_o,
                             v_w_dw_conv, v_b_dw_conv, v_g_conv_ln, v_b_conv_ln, v_w_conv_o, v_b_conv_o,
                             v_w_mix_o, v_g_pre_ffn, v_g_post_ffn, v_w_up, v_w_dw_ffn, v_b_dw_ffn, v_w_down]))
    shapes = {n: w.shape for n, w in weights.items()}

    seq = x.shape[1]
    me = 4 * lax.axis_index("x") + 2 * lax.axis_index("y") + lax.axis_index("c")
    x2 = x.reshape(seq, D_MODEL)
    target = loss_target.reshape(seq, D_MODEL)
    sq = lambda a: a.reshape(a.shape[1:])
    bf = lambda a: sq(a).astype(BF16)

    c_act = _silu_vec(c)
    c_all, g_in, g_dwc, g_dwf = _run_comm(
        _gather_comm([c_act, bf(w_in), sq(w_dw_conv), sq(w_dw_ffn)]), "gather_first")
    c_all = c_all.reshape(N_DEV, D_MODEL)
    wf_in = _blocks_to_cols(g_in)
    wf_dwc = _blocks_to_cols(g_dwc)
    wf_dwf = _blocks_to_cols(g_dwf)

    (mod_all,) = _run_comm(_gather_comm([_ada_fwd(c_all, sq(w_ada))]), "gather_mod")
    mod = lax.dynamic_index_in_dim(mod_all, me, axis=1, keepdims=False)
    mod6 = (mod.reshape(1, 6 * D_MODEL) + b_ada).reshape(6, D_MODEL)

    h1 = _pre_mix(x2, mod6, g_pre_mix)
    qkv = _mm(h1, wf_in[:, :3 * D_ATTN], "nn", BF16, "in_proj_qkv", bias=b_in[:, :3 * D_ATTN], tm=1024, tn=768)
    zr = _mm(h1, wf_in[:, 3 * D_ATTN:], "nn", F32, "in_proj_rest", bias=b_in[:, 3 * D_ATTN:], tm=1024, tn=1024)
    kpad = jnp.pad(qkv[:, D_ATTN:2 * D_ATTN], ((PAD_ROWS, 0), (0, 0)))
    vpad = jnp.pad(qkv[:, 2 * D_ATTN:], ((PAD_ROWS, 0), (0, 0)))
    table = _group_table(_bias_table(sq(rel_bias)))
    ao, (g_ao, g_co, g_mo, g_up, g_dn) = _attn_fwd(
        qkv, kpad, vpad, table,
        comm=_gather_comm([bf(w_attn_o), bf(w_conv_o), bf(w_mix_o), bf(w_up), bf(w_down)]))
    wf_ao = _blocks_to_cols(g_ao)
    wf_co = _blocks_to_cols(g_co)
    wf_mo = g_mo.reshape(D_MODEL, D_MODEL)
    wf_up = _blocks_to_cols(g_up)
    wf_dn = g_dn.reshape(D_FF, D_MODEL)
    u1, u3 = _conv_fwd(zr, wf_dwc, b_dw_conv, g_conv_ln, b_conv_ln)
    y, a_br, cb_br = _merge_fwd(ao, u3, zr, wf_ao, wf_co, b_conv_o)
    ymix = _mm(y, wf_mo, "nn", F32, "mix_o", tm=1024, tn=1024)
    x1, h2 = _post_mix_pre_ffn(ymix, x2, mod6, g_post_mix, g_pre_ffn)
    up = _mm(h2, wf_up, "nn", F32, "ffn_up", tm=1024, tn=1408)
    act = _ffn_act(up, wf_dwf, b_dw_ffn)
    yf = _mm(act, wf_dn, "nn", F32, "ffn_down", tm=512, tn=1024)
    loss_lanes, dout, dyf, small_f = _final(yf, x1, target, mod6, g_post_ffn)
    loss = lax.psum(0.5 * loss_lanes[0, 0], ("x", "y", "c"))

    dact = _mm(dyf, wf_dn, "nt", BF16, "ffn_down_dx", tm=1024, tn=1408)
    gw_down = _mm(act, dyf, "tn", BF16, "ffn_down_dw", tm=256, tn=1024)
    (dup_v, dup_g, dwv, dwg, dbv, dbg), (parts_down,) = _ffn_act_bwd(
        up, dact, wf_dwf, b_dw_ffn, comm=_scatter_comm([_rows_to_blocks(gw_down)]))
    dup = jnp.concatenate([dup_v, dup_g], axis=1)
    dh2 = _mm(dup, wf_up, "nt", F32, "ffn_up_dx")
    gw_up = _mm(h2, dup, "tn", BF16, "ffn_up_dw")
    dx1, dymix, small_m = _mid_bwd(dh2, x1, dout, ymix, mod6, g_pre_ffn, g_post_mix)
    dy = _mm(dymix, wf_mo, "nt", F32, "mix_o_dx", tm=1024, tn=1024)
    gw_mo = _mm(y, dymix, "tn", BF16, "mix_o_dw")
    da, dcb, dga, dgb, small_g = _merge_bwd(dy, a_br, cb_br, zr)
    dao = _mm(da, wf_ao, "nt", BF16, "attn_o_dx", tm=1024)
    gw_ao = _mm(ao, da, "tn", BF16, "attn_o_dw")
    du3 = _mm(dcb, wf_co, "nt", F32, "conv_o_dx", tm=1024)
    gw_co = _mm(u3, dcb, "tn", BF16, "conv_o_dw")
    (dq, dkt, dvt, dbias, small_a, cs_k, cs_v), (parts_up, parts_mo, parts_ao, parts_co) = _attn_bwd(
        qkv, kpad, vpad, table, dao,
        comm=_scatter_comm([_cols_to_blocks(gw_up), _rows_to_blocks(gw_mo), _cols_to_blocks(gw_ao),
                            _cols_to_blocks(gw_co)]))
    g_rel = _bias_grad(jnp.transpose(dbias, (1, 0, 2)))
    dglu_a, dglu_b, dw_conv, small_c = _conv_bwd(zr, u1, du3, wf_dwc, g_conv_ln, b_conv_ln)
    dz = jnp.concatenate([dq, dkt.T.astype(BF16), dvt.T.astype(BF16), dglu_a, dglu_b, dga, dgb], axis=1)
    gw_in = _mm(h1, dz, "tn", BF16, "in_proj_dw")
    dh1, (parts_in,) = _mm(dz, wf_in, "nt", F32, "in_proj_dx", comm=_scatter_comm([_cols_to_blocks(gw_in)]))
    grad_x, small_x = _pre_mix_bwd(dh1, x2, dx1, mod6, g_pre_mix)

    dmod = jnp.concatenate([small_x[2:3], small_x[1:2], small_m[4:5], small_m[2:3], small_m[1:2], small_f[1:2]],
                           axis=1)
    partial = {
        "b_ada": dmod, "g_pre_mix": small_x[0:1], "g_post_mix": small_m[3:4],
        "b_in": jnp.concatenate([small_a[0:1], cs_k.reshape(1, D_ATTN), cs_v.reshape(1, D_ATTN), small_c[3:4],
                                 small_c[4:5], small_g[0:1], small_g[1:2]], axis=1),
        "rel_bias": g_rel, "b_dw_conv": small_c[0:1], "g_conv_ln": small_c[1:2], "b_conv_ln": small_c[2:3],
        "b_conv_o": small_g[2:3], "g_pre_ffn": small_m[0:1], "g_post_ffn": small_f[0:1],
        "b_dw_ffn": jnp.concatenate([dbv[0:1], dbg[0:1]], axis=1),
    }
    dw_ffn = jnp.concatenate([dwv[0:3], dwg[0:3]], axis=1)
    packed = jnp.concatenate([_pack_small(partial), dmod, dw_conv[0:CONV_K].reshape(1, CONV_K * D_CONV),
                              dw_ffn.reshape(1, 3 * 2 * D_FF)], axis=1)
    (gathered,) = _run_comm(_gather_comm([packed]), "gather_small")
    summed, d_small, m_small, v_small = _small_adamw(
        gathered.reshape(N_DEV, packed.shape[1]), _pack_small(weights), _pack_small(mom_m), _pack_small(mom_v))
    off = SMALL_TOTAL
    dmod_all = gathered.reshape(N_DEV, packed.shape[1])[:, off:off + 6 * D_MODEL]
    off += 6 * D_MODEL
    g_dwc_full = summed[:, off:off + CONV_K * D_CONV].reshape(CONV_K, D_CONV)
    off += CONV_K * D_CONV
    g_dwf_full = summed[:, off:off + 3 * 2 * D_FF].reshape(3, 2 * D_FF)

    grads, deltas, new_m, new_v = {}, {}, {}, {}
    for dst, vec in ((grads, summed), (deltas, d_small), (new_m, m_small), (new_v, v_small)):
        dst.update(_unpack_small(vec, shapes))

    def local_update(name, grad):
        g, d, m_new, v_new = _adamw(sq(weights[name]), sq(mom_m[name]), sq(mom_v[name]), "adamw_" + name, g=grad)
        for dst, val in ((grads, g), (deltas, d), (new_m, m_new), (new_v, v_new)):
            dst[name] = val.reshape(shapes[name])

    local_update("w_dw_conv", lax.dynamic_slice_in_dim(g_dwc_full, me * (D_CONV // N_DEV), D_CONV // N_DEV, 1))
    local_update("w_dw_ffn", lax.dynamic_slice_in_dim(g_dwf_full, me * (2 * D_FF // N_DEV), 2 * D_FF // N_DEV, 1))
    ada_cols = 6 * D_MODEL // N_DEV
    local_update("w_ada", _ada_grad(c_all, lax.dynamic_slice_in_dim(dmod_all, me * ada_cols, ada_cols, 1)))

    for name, part in (("w_in", parts_in), ("w_attn_o", parts_ao), ("w_conv_o", parts_co), ("w_mix_o", parts_mo),
                       ("w_up", parts_up), ("w_down", parts_down)):
        g, d, m_new, v_new = _adamw(sq(weights[name]), sq(mom_m[name]), sq(mom_v[name]), "adamw_" + name,
                                    parts=part)
        for dst, val in ((grads, g), (deltas, d), (new_m, m_new), (new_v, v_new)):
            dst[name] = val.reshape(shapes[name])

    return (loss, grad_x.reshape(x.shape), *[grads[n] for n in names], *[deltas[n] for n in names],
            *[new_m[n] for n in names], *[new_v[n] for n in names])
```

```python
import functools
import math

import jax
import jax.numpy as jnp
from jax import lax
from jax.experimental import pallas as pl
from jax.experimental.pallas import tpu as pltpu

F32 = jnp.float32
BF16 = jnp.bfloat16
HIGHEST = lax.Precision.HIGHEST

D_MODEL = 1024
CHUNK = 64
LEFT_CHUNKS = 8
BAND = (LEFT_CHUNKS + 1) * CHUNK
PAD_ROWS = LEFT_CHUNKS * CHUNK
GROUP = 4
GROUP_Q = GROUP * CHUNK
GROUP_K = GROUP_Q + PAD_ROWS
SOFTMAX_ROWS = 16
TOEPLITZ = 640
N_HEADS = 8
HEAD_DIM = 64
D_ATTN = 512
D_CONV = 512
CONV_K = 31
CONV_HALO = 32
MAX_REL = 128
N_REL = 2 * MAX_REL + 1
D_FF = 2816
FFN_HALO = 8
FFN_COLS = 256
EPS = 1e-6
NEG_INF = -1e30
N_DEV = 8

ADAM_LR = 0.001
ADAM_B1 = 0.9
ADAM_B2 = 0.999
ADAM_EPS = 1e-08
ADAM_WD = 0.01
ADAM_STEP = 10

VMEM_LIMIT_BYTES = 56 * 1024 * 1024
ADAMW_BLOCK_BYTES = 768 * 1024

MESH = pl.DeviceIdType.MESH
ANY = pl.BlockSpec(memory_space=pl.ANY)

SH_M, SC_M, GT_M, SH_F, SC_F, GT_F = range(6)

SMALL = (("b_ada", 6144), ("g_pre_mix", 1024), ("g_post_mix", 1024), ("b_in", 4608),
         ("rel_bias", 2176), ("b_dw_conv", 512), ("g_conv_ln", 512), ("b_conv_ln", 512),
         ("b_conv_o", 1024), ("g_pre_ffn", 1024), ("g_post_ffn", 1024), ("b_dw_ffn", 5632))
SMALL_TOTAL = sum(n for _, n in SMALL)


def _cparams(n_axes):
    return pltpu.CompilerParams(vmem_limit_bytes=VMEM_LIMIT_BYTES,
                                dimension_semantics=("arbitrary",) * n_axes)


def _sig(v):
    return 1.0 / (1.0 + jnp.exp(-v))


def _pick(n, target):
    if n <= target:
        return n
    t = target - target % 128
    while n % t:
        t -= 128
    return t


def _tile(rows, cols, col=0):
    return pl.BlockSpec((rows, cols), lambda i: (i, col))


def _full(shape):
    zeros = (0,) * len(shape)
    return pl.BlockSpec(shape, lambda i: zeros)


def _prev(halo, cols, rows, col=0):
    return pl.BlockSpec((halo, cols), lambda i: (jnp.maximum(i * (rows // halo) - 1, 0), col))


def _next(halo, cols, rows, n_blocks, col=0):
    return pl.BlockSpec((halo, cols), lambda i: (jnp.minimum((i + 1) * (rows // halo), n_blocks - 1), col))


class _Comm:
    def __init__(self, inputs, out_shapes, sems, start, finish):
        self.inputs, self.out_shapes, self.sems, self.start, self.finish = inputs, out_shapes, sems, start, finish


def _host_call(body, name, grid, in_specs, out_specs, out_shape, scratch_shapes, args, comm=None):
    n_in, n_out, n_scr = len(args), len(out_shape), len(scratch_shapes)
    c_in = list(comm.inputs) if comm else []
    c_out = list(comm.out_shapes) if comm else []
    c_sem = list(comm.sems) if comm else []

    def full(*refs):
        bounds = [0, n_in, len(c_in), n_out, len(c_out), n_scr, len(c_sem)]
        cuts = [sum(bounds[:i + 1]) for i in range(len(bounds))]
        ins, cins, outs, couts, scr, csems = (refs[lo:hi] for lo, hi in zip(cuts[:-1], cuts[1:]))
        if comm:
            first = functools.reduce(jnp.logical_and, [pl.program_id(ax) == 0 for ax in range(len(grid))])
            pl.when(first)(lambda: comm.start(cins, couts, csems))
        body(ins, outs, scr)
        if comm:
            last = functools.reduce(jnp.logical_and, [pl.program_id(ax) == grid[ax] - 1 for ax in range(len(grid))])
            pl.when(last)(lambda: comm.finish(cins, couts, csems))

    res = pl.pallas_call(
        full, name=name, grid=grid, in_specs=list(in_specs) + [ANY] * len(c_in),
        out_specs=list(out_specs) + [ANY] * len(c_out), out_shape=list(out_shape) + c_out,
        scratch_shapes=list(scratch_shapes) + c_sem, compiler_params=_cparams(len(grid)),
    )(*args, *c_in)
    return list(res[:n_out]), list(res[n_out:])


def _run_comm(comm, name):
    n_in, n_out = len(comm.inputs), len(comm.out_shapes)

    def body(*refs):
        ins, outs, sems = refs[:n_in], refs[n_in:n_in + n_out], refs[n_in + n_out:]
        comm.start(ins, outs, sems)
        comm.finish(ins, outs, sems)

    return pl.pallas_call(
        body, name=name, out_shape=list(comm.out_shapes), in_specs=[ANY] * n_in, out_specs=[ANY] * n_out,
        scratch_shapes=list(comm.sems),
    )(*comm.inputs)


def _place():
    return lax.axis_index("x"), lax.axis_index("y"), lax.axis_index("c")


def _gather_comm(arrs):
    n = len(arrs)

    def plan(ins, outs, sems):
        send_sems, recv_sems, local_sems = sems
        x, y, c = _place()
        me, sibling = (x, y, c), (x, y, 1 - c)
        chips = [(1 - x, y), (x, 1 - y), (1 - x, 1 - y)]

        def block(k, p):
            return outs[k].at[4 * p[0] + 2 * p[1] + p[2]]

        def copy(k, s, blk, to, src=None):
            return pltpu.make_async_remote_copy(
                src_ref=block(k, blk) if src is None else src, dst_ref=block(k, blk),
                send_sem=send_sems.at[7 * k + s], recv_sem=recv_sems.at[7 * k + s],
                device_id=to, device_id_type=MESH)

        mine = [pltpu.make_async_copy(ins[k], block(k, me), local_sems.at[k]) for k in range(n)]
        first = []
        for k in range(n):
            first.append(copy(k, 0, me, sibling, src=ins[k]))
            for j, chip in enumerate(chips):
                first.append(copy(k, 1 + j, me, (*chip, c), src=ins[k]))
        return me, sibling, chips, c, copy, mine, first

    def start(ins, outs, sems):
        *_, mine, first = plan(ins, outs, sems)
        for cp in mine + first:
            cp.start()

    def finish(ins, outs, sems):
        me, sibling, chips, c, copy, mine, first = plan(ins, outs, sems)
        passed = []
        for j, chip in enumerate(chips):
            for k in range(n):
                copy(k, 1 + j, (*chip, c), me).wait_recv()
                fwd = copy(k, 4 + j, (*chip, c), sibling)
                fwd.start()
                passed.append(fwd)
        for k in range(n):
            copy(k, 0, sibling, me).wait_recv()
        for j, chip in enumerate(chips):
            for k in range(n):
                copy(k, 4 + j, (*chip, 1 - c), me).wait_recv()
        for cp in first + passed:
            cp.wait_send()
        for cp in mine:
            cp.wait()

    return _Comm(list(arrs), [jax.ShapeDtypeStruct((N_DEV,) + a.shape, a.dtype) for a in arrs],
                 [pltpu.SemaphoreType.DMA((7 * n,)), pltpu.SemaphoreType.DMA((7 * n,)),
                  pltpu.SemaphoreType.DMA((n,))], start, finish)


def _scatter_comm(blocks):
    n = len(blocks)

    def plan(ins, outs, sems):
        send_sems, recv_sems, local_sems = sems
        x, y, c = _place()
        me = 4 * x + 2 * y + c
        local = [pltpu.make_async_copy(ins[k].at[me], outs[k].at[me], local_sems.at[k]) for k in range(n)]
        sends, recvs = [], []
        for k in range(n):
            for mask in range(1, N_DEV):
                px = 1 - x if mask & 4 else x
                py = 1 - y if mask & 2 else y
                pc = 1 - c if mask & 1 else c
                peer = 4 * px + 2 * py + pc
                sem = 7 * k + mask - 1
                both = dict(send_sem=send_sems.at[sem], recv_sem=recv_sems.at[sem], device_id=(px, py, pc),
                            device_id_type=MESH)
                sends.append(pltpu.make_async_remote_copy(src_ref=ins[k].at[peer], dst_ref=outs[k].at[me], **both))
                recvs.append(pltpu.make_async_remote_copy(src_ref=ins[k].at[me], dst_ref=outs[k].at[peer], **both))
        return local, sends, recvs

    def start(ins, outs, sems):
        local, sends, _ = plan(ins, outs, sems)
        for cp in local + sends:
            cp.start()

    def finish(ins, outs, sems):
        local, sends, recvs = plan(ins, outs, sems)
        for cp in recvs:
            cp.wait_recv()
        for cp in sends:
            cp.wait_send()
        for cp in local:
            cp.wait()

    return _Comm(list(blocks), [jax.ShapeDtypeStruct(b.shape, b.dtype) for b in blocks],
                 [pltpu.SemaphoreType.DMA((7 * n,)), pltpu.SemaphoreType.DMA((7 * n,)),
                  pltpu.SemaphoreType.DMA((n,))], start, finish)


_DIMS = {"nn": (((1,), (0,)), ((), ())), "nt": (((1,), (1,)), ((), ())), "tn": (((0,), (0,)), ((), ()))}


def _mm(a, b, mode, out_dtype, name, bias=None, tm=512, tn=512, comm=None):
    assert a.dtype == BF16 and b.dtype == BF16
    if mode == "tn":
        k_dim, m_dim = a.shape
    else:
        m_dim, k_dim = a.shape
    n_dim = b.shape[0] if mode == "nt" else b.shape[1]
    tm, tn = _pick(m_dim, tm), _pick(n_dim, tn)
    a_spec = (pl.BlockSpec((k_dim, tm), lambda i, j: (0, i)) if mode == "tn"
              else pl.BlockSpec((tm, k_dim), lambda i, j: (i, 0)))
    b_spec = (pl.BlockSpec((tn, k_dim), lambda i, j: (j, 0)) if mode == "nt"
              else pl.BlockSpec((k_dim, tn), lambda i, j: (0, j)))
    in_specs = [a_spec, b_spec]
    args = [a, b]
    if bias is not None:
        in_specs.append(pl.BlockSpec((1, tn), lambda i, j: (0, j)))
        args.append(bias)
    dims = _DIMS[mode]

    def body(ins, outs, scratch):
        total = lax.dot_general(ins[0][...], ins[1][...], dims, preferred_element_type=F32)
        if bias is not None:
            total = total + ins[2][...]
        outs[0][...] = total.astype(out_dtype)

    (out,), extra = _host_call(
        body, name, grid=(m_dim // tm, n_dim // tn), in_specs=in_specs,
        out_specs=[pl.BlockSpec((tm, tn), lambda i, j: (i, j))],
        out_shape=[jax.ShapeDtypeStruct((m_dim, n_dim), out_dtype)], scratch_shapes=[], args=args, comm=comm)
    return out if comm is None else (out, extra)


def _adam_math(w, g, m, v):
    m = ADAM_B1 * m + (1.0 - ADAM_B1) * g
    v = ADAM_B2 * v + (1.0 - ADAM_B2) * (g * g)
    m_hat = m / (1.0 - ADAM_B1 ** ADAM_STEP)
    v_hat = v / (1.0 - ADAM_B2 ** ADAM_STEP)
    delta = -ADAM_LR * (m_hat / (jnp.sqrt(v_hat) + ADAM_EPS) + ADAM_WD * w)
    return delta, m, v


def _adamw(w, m, v, name, g=None, parts=None):
    rows, cols = w.shape
    tr = rows
    if rows * cols * 4 > ADAMW_BLOCK_BYTES:
        tr = max(t for t in range(16, rows, 16) if rows % t == 0 and t * cols * 4 <= ADAMW_BLOCK_BYTES)

    def body(w_ref, m_ref, v_ref, g_ref, go_ref, d_ref, mo_ref, vo_ref):
        if parts is None:
            grad = g_ref[...]
        else:
            grad = g_ref[0].astype(F32)
            for d in range(1, N_DEV):
                grad = grad + g_ref[d].astype(F32)
        delta, m_new, v_new = _adam_math(w_ref[...], grad, m_ref[...], v_ref[...])
        go_ref[...] = grad
        d_ref[...] = delta
        mo_ref[...] = m_new
        vo_ref[...] = v_new

    spec = _tile(tr, cols)
    g_spec = spec if parts is None else pl.BlockSpec((N_DEV, tr, cols), lambda i: (0, i, 0))
    shape = jax.ShapeDtypeStruct((rows, cols), F32)
    return pl.pallas_call(
        body, name=name, out_shape=[shape] * 4, grid=(rows // tr,),
        in_specs=[spec, spec, spec, g_spec], out_specs=[spec] * 4, compiler_params=_cparams(1),
    )(w, m, v, g if parts is None else parts)


def _small_adamw(gathered, w, m, v):
    total = gathered.shape[1]

    def body(g_ref, w_ref, m_ref, v_ref, go_ref, d_ref, mo_ref, vo_ref):
        grad = g_ref[0:1, :]
        for d in range(1, N_DEV):
            grad = grad + g_ref[d:d + 1, :]
        go_ref[...] = grad
        delta, m_new, v_new = _adam_math(w_ref[...], grad[:, :SMALL_TOTAL], m_ref[...], v_ref[...])
        d_ref[...] = delta
        mo_ref[...] = m_new
        vo_ref[...] = v_new

    small = jax.ShapeDtypeStruct((1, SMALL_TOTAL), F32)
    return pl.pallas_call(
        body, name="small_adamw",
        out_shape=[jax.ShapeDtypeStruct((1, total), F32), small, small, small],
        compiler_params=pltpu.CompilerParams(vmem_limit_bytes=VMEM_LIMIT_BYTES),
    )(gathered, w, m, v)


def _silu_vec(c):
    def body(c_ref, o_ref):
        v = c_ref[...]
        o_ref[...] = v * _sig(v)

    return pl.pallas_call(body, name="silu_c", out_shape=jax.ShapeDtypeStruct(c.shape, F32))(c)


def _ada_fwd(c_all, w_shard):
    def body(c_ref, w_ref, o_ref):
        o_ref[...] = jnp.dot(c_ref[...], w_ref[...], precision=HIGHEST, preferred_element_type=F32)

    return pl.pallas_call(
        body, name="ada_fwd", out_shape=jax.ShapeDtypeStruct((N_DEV, w_shard.shape[1]), F32),
        compiler_params=pltpu.CompilerParams(vmem_limit_bytes=VMEM_LIMIT_BYTES),
    )(c_all, w_shard)


def _ada_grad(c_all, dmod_shard):
    def body(c_ref, d_ref, o_ref):
        o_ref[...] = lax.dot_general(c_ref[...], d_ref[...], _DIMS["tn"], precision=HIGHEST,
                                     preferred_element_type=F32)

    return pl.pallas_call(
        body, name="ada_grad", out_shape=jax.ShapeDtypeStruct((D_MODEL, dmod_shard.shape[1]), F32),
        compiler_params=pltpu.CompilerParams(vmem_limit_bytes=VMEM_LIMIT_BYTES),
    )(c_all, dmod_shard)


ROWS = 256


def _rms(v):
    r = lax.rsqrt(jnp.mean(v * v, axis=-1, keepdims=True) + EPS)
    return v * r, r


def _rms_bwd(dxn, xn, r):
    return r * (dxn - xn * jnp.mean(dxn * xn, axis=-1, keepdims=True))


def _colsum(v):
    return jnp.sum(v, axis=0, keepdims=True)


def _pre_mix(x, mod6, g1):
    seq = x.shape[0]

    def body(x_ref, mod_ref, g_ref, h_ref):
        xn, _ = _rms(x_ref[...])
        y = xn * g_ref[...]
        h_ref[...] = (y * (1.0 + mod_ref[SC_M:SC_M + 1, :]) + mod_ref[SH_M:SH_M + 1, :]).astype(BF16)

    return pl.pallas_call(
        body, name="pre_mix", out_shape=jax.ShapeDtypeStruct((seq, D_MODEL), BF16), grid=(seq // ROWS,),
        in_specs=[_tile(ROWS, D_MODEL), _full((6, D_MODEL)), _full((1, D_MODEL))],
        out_specs=_tile(ROWS, D_MODEL), compiler_params=_cparams(1),
    )(x, mod6, g1)


def _post_mix_pre_ffn(ymix, x, mod6, g2, g3):
    seq = x.shape[0]

    def body(y_ref, x_ref, mod_ref, g2_ref, g3_ref, x1_ref, h_ref):
        yn, _ = _rms(y_ref[...])
        x1 = x_ref[...] + mod_ref[GT_M:GT_M + 1, :] * (yn * g2_ref[...])
        x1_ref[...] = x1
        xn, _ = _rms(x1)
        y3 = xn * g3_ref[...]
        h_ref[...] = (y3 * (1.0 + mod_ref[SC_F:SC_F + 1, :]) + mod_ref[SH_F:SH_F + 1, :]).astype(BF16)

    return pl.pallas_call(
        body, name="post_mix_pre_ffn",
        out_shape=[jax.ShapeDtypeStruct((seq, D_MODEL), F32), jax.ShapeDtypeStruct((seq, D_MODEL), BF16)],
        grid=(seq // ROWS,),
        in_specs=[_tile(ROWS, D_MODEL), _tile(ROWS, D_MODEL), _full((6, D_MODEL)), _full((1, D_MODEL)),
                  _full((1, D_MODEL))],
        out_specs=[_tile(ROWS, D_MODEL), _tile(ROWS, D_MODEL)], compiler_params=_cparams(1),
    )(ymix, x, mod6, g2, g3)


def _final(yf, x1, target, mod6, g4):
    seq = x1.shape[0]

    def body(y_ref, x1_ref, t_ref, mod_ref, g_ref, loss_ref, dout_ref, dyf_ref, small_ref):
        i = pl.program_id(0)

        @pl.when(i == 0)
        def _():
            loss_ref[...] = jnp.zeros_like(loss_ref)
            small_ref[...] = jnp.zeros_like(small_ref)

        gt = mod_ref[GT_F:GT_F + 1, :]
        g4v = g_ref[...]
        yn, r = _rms(y_ref[...])
        out = x1_ref[...] + gt * (yn * g4v)
        err = out - t_ref[...]
        loss_ref[...] += jnp.sum(jnp.mean(err * err, axis=-1, keepdims=True))
        dout = err * (1.0 / D_MODEL)
        dout_ref[...] = dout
        small_ref[0:1, :] += _colsum(dout * gt * yn)
        small_ref[1:2, :] += _colsum(dout * (yn * g4v))
        dyf_ref[...] = _rms_bwd(dout * gt * g4v, yn, r).astype(BF16)

    return pl.pallas_call(
        body, name="final_loss",
        out_shape=[jax.ShapeDtypeStruct((1, 128), F32), jax.ShapeDtypeStruct((seq, D_MODEL), F32),
                   jax.ShapeDtypeStruct((seq, D_MODEL), BF16), jax.ShapeDtypeStruct((8, D_MODEL), F32)],
        grid=(seq // ROWS,),
        in_specs=[_tile(ROWS, D_MODEL)] * 3 + [_full((6, D_MODEL)), _full((1, D_MODEL))],
        out_specs=[_full((1, 128)), _tile(ROWS, D_MODEL), _tile(ROWS, D_MODEL), _full((8, D_MODEL))],
        compiler_params=_cparams(1),
    )(yf, x1, target, mod6, g4)


def _mid_bwd(dh2, x1, dout, ymix, mod6, g3, g2):
    seq = x1.shape[0]

    def body(dh_ref, x1_ref, dout_ref, y_ref, mod_ref, g3_ref, g2_ref, dx1_ref, dy_ref, small_ref):
        i = pl.program_id(0)

        @pl.when(i == 0)
        def _():
            small_ref[...] = jnp.zeros_like(small_ref)

        dh = dh_ref[...]
        g3v, g2v = g3_ref[...], g2_ref[...]
        xn, r3 = _rms(x1_ref[...])
        y3 = xn * g3v
        dy3 = dh * (1.0 + mod_ref[SC_F:SC_F + 1, :])
        small_ref[0:1, :] += _colsum(dy3 * xn)
        small_ref[1:2, :] += _colsum(dh * y3)
        small_ref[2:3, :] += _colsum(dh)
        dx1 = dout_ref[...] + _rms_bwd(dy3 * g3v, xn, r3)
        dx1_ref[...] = dx1
        gt = mod_ref[GT_M:GT_M + 1, :]
        yn, r2 = _rms(y_ref[...])
        small_ref[3:4, :] += _colsum(dx1 * gt * yn)
        small_ref[4:5, :] += _colsum(dx1 * (yn * g2v))
        dy_ref[...] = _rms_bwd(dx1 * gt * g2v, yn, r2).astype(BF16)

    return pl.pallas_call(
        body, name="mid_bwd",
        out_shape=[jax.ShapeDtypeStruct((seq, D_MODEL), F32), jax.ShapeDtypeStruct((seq, D_MODEL), BF16),
                   jax.ShapeDtypeStruct((8, D_MODEL), F32)],
        grid=(seq // ROWS,),
        in_specs=[_tile(ROWS, D_MODEL)] * 4 + [_full((6, D_MODEL)), _full((1, D_MODEL)), _full((1, D_MODEL))],
        out_specs=[_tile(ROWS, D_MODEL), _tile(ROWS, D_MODEL), _full((8, D_MODEL))],
        compiler_params=_cparams(1),
    )(dh2, x1, dout, ymix, mod6, g3, g2)


def _pre_mix_bwd(dh1, x, dx1, mod6, g1):
    seq = x.shape[0]

    def body(dh_ref, x_ref, dx1_ref, mod_ref, g_ref, dx_ref, small_ref):
        i = pl.program_id(0)

        @pl.when(i == 0)
        def _():
            small_ref[...] = jnp.zeros_like(small_ref)

        dh = dh_ref[...]
        g1v = g_ref[...]
        xn, r = _rms(x_ref[...])
        dy = dh * (1.0 + mod_ref[SC_M:SC_M + 1, :])
        small_ref[0:1, :] += _colsum(dy * xn)
        small_ref[1:2, :] += _colsum(dh * (xn * g1v))
        small_ref[2:3, :] += _colsum(dh)
        dx_ref[...] = dx1_ref[...] + _rms_bwd(dy * g1v, xn, r)

    return pl.pallas_call(
        body, name="pre_mix_bwd",
        out_shape=[jax.ShapeDtypeStruct((seq, D_MODEL), F32), jax.ShapeDtypeStruct((8, D_MODEL), F32)],
        grid=(seq // ROWS,),
        in_specs=[_tile(ROWS, D_MODEL)] * 3 + [_full((6, D_MODEL)), _full((1, D_MODEL))],
        out_specs=[_tile(ROWS, D_MODEL), _full((8, D_MODEL))], compiler_params=_cparams(1),
    )(dh1, x, dx1, mod6, g1)


def _toeplitz_onehot(shape, offset_axis, top):
    m = lax.broadcasted_iota(jnp.int32, shape, offset_axis)
    i = lax.broadcasted_iota(jnp.int32, shape, 1 - offset_axis)
    return (i == jnp.clip(top - m, -MAX_REL, MAX_REL) + MAX_REL).astype(F32)


def _bias_table(rel_bias):
    width = GROUP_Q + GROUP_K

    def body(rb_ref, o_ref, t_ref):
        t_ref[...] = jnp.dot(rb_ref[...], _toeplitz_onehot((N_REL, width), 1, GROUP_K - 1), precision=HIGHEST,
                             preferred_element_type=F32)
        lane = lax.broadcasted_iota(jnp.int32, (N_HEADS, GROUP_K), 1)
        for r in range(GROUP_Q):
            first_key = (r // CHUNK) * CHUNK
            band = jnp.logical_and(lane >= first_key, lane < first_key + BAND)
            o_ref[r] = jnp.where(band, t_ref[:, GROUP_Q - 1 - r:GROUP_Q - 1 - r + GROUP_K], NEG_INF)

    return pl.pallas_call(
        body, name="bias_table", out_shape=jax.ShapeDtypeStruct((GROUP_Q, N_HEADS, GROUP_K), F32),
        scratch_shapes=[pltpu.VMEM((N_HEADS, width), F32)],
    )(rel_bias)


def _bias_grad(dbias_q):
    def body(d_ref, o_ref, t_ref):
        t_ref[...] = jnp.zeros_like(t_ref)
        for qi in range(CHUNK):
            t_ref[:, CHUNK - 1 - qi:CHUNK - 1 - qi + BAND] += d_ref[qi]
        o_ref[...] = jnp.dot(t_ref[...], _toeplitz_onehot((TOEPLITZ, N_REL), 0, BAND - 1), precision=HIGHEST,
                             preferred_element_type=F32)

    return pl.pallas_call(
        body, name="bias_grad", out_shape=jax.ShapeDtypeStruct((N_HEADS, N_REL), F32),
        scratch_shapes=[pltpu.VMEM((N_HEADS, TOEPLITZ), F32)],
    )(dbias_q)


def _load_resident(pairs, sems):
    copies = [pltpu.make_async_copy(src, dst, sems.at[n]) for n, (src, dst) in enumerate(pairs)]
    for cp in copies:
        cp.start()
    for cp in copies:
        cp.wait()


def _softmax_rows(s_ref, t_ref, valid, rows):
    s = s_ref[rows, :] * (HEAD_DIM ** -0.5) + t_ref[rows, :]
    s = jnp.where(valid, s, NEG_INF)
    e = jnp.exp(s - jnp.max(s, axis=-1, keepdims=True))
    return e / jnp.sum(e, axis=-1, keepdims=True)


def _valid_keys(g):
    kj = lax.broadcasted_iota(jnp.int32, (SOFTMAX_ROWS, GROUP_K), 1)
    return kj >= PAD_ROWS - g * GROUP_Q


def _attn_fwd(qkv, kpad, vpad, table, comm=None):
    seq = qkv.shape[0]

    def body(ins, outs, scratch):
        q_ref, k_hbm, v_hbm, t_hbm = ins
        (o_ref,) = outs
        k_ref, v_ref, t_ref, s_ref, p_ref, sems = scratch
        g = pl.program_id(0)

        @pl.when(g == 0)
        def _():
            _load_resident(((k_hbm, k_ref), (v_hbm, v_ref), (t_hbm, t_ref)), sems)

        window = pl.ds(pl.multiple_of(g * GROUP_Q, GROUP_Q), GROUP_K)
        valid = _valid_keys(g)
        for h in range(N_HEADS):
            cols = slice(h * HEAD_DIM, (h + 1) * HEAD_DIM)
            buf = h % 2
            s_ref[buf] = lax.dot_general(q_ref[:, cols], k_ref[window, cols], _DIMS["nt"],
                                         preferred_element_type=F32)
            for r in range(GROUP_Q // SOFTMAX_ROWS):
                rows = slice(r * SOFTMAX_ROWS, (r + 1) * SOFTMAX_ROWS)
                p_ref[buf, rows, :] = _softmax_rows(s_ref.at[buf], t_ref.at[h], valid, rows).astype(BF16)
            o_ref[:, cols] = jnp.dot(p_ref[buf], v_ref[window, cols], preferred_element_type=F32).astype(BF16)

    (ao,), extra = _host_call(
        body, "attn_fwd", grid=(seq // GROUP_Q,),
        in_specs=[_tile(GROUP_Q, D_ATTN), ANY, ANY, ANY], out_specs=[_tile(GROUP_Q, D_ATTN)],
        out_shape=[jax.ShapeDtypeStruct((seq, D_ATTN), BF16)],
        scratch_shapes=[pltpu.VMEM(kpad.shape, BF16), pltpu.VMEM(vpad.shape, BF16), pltpu.VMEM(table.shape, F32),
                        pltpu.VMEM((2, GROUP_Q, GROUP_K), F32), pltpu.VMEM((2, GROUP_Q, GROUP_K), BF16),
                        pltpu.SemaphoreType.DMA((3,))],
        args=[qkv, kpad, vpad, table], comm=comm)
    return ao, extra


def _attn_bwd(qkv, kpad, vpad, table, dao, comm=None):
    seq = qkv.shape[0]
    n_groups = seq // GROUP_Q
    fold_w = GROUP_K + (GROUP - 1) * CHUNK

    def body(ins, outs, scratch):
        q_ref, do_ref, k_hbm, v_hbm, t_hbm = ins
        dq_ref, dkt_hbm, dvt_hbm, db_ref, cs_ref, csk_ref, csv_ref = outs
        k_ref, v_ref, t_ref, db_acc, dkt_acc, dvt_acc, s_ref, dp_ref, p_ref, ds_ref, sems = scratch
        g = pl.program_id(0)

        @pl.when(g == 0)
        def _():
            _load_resident(((k_hbm, k_ref), (v_hbm, v_ref), (t_hbm, t_ref)), sems)
            db_acc[...] = jnp.zeros_like(db_acc)
            dkt_acc[...] = jnp.zeros_like(dkt_acc)
            dvt_acc[...] = jnp.zeros_like(dvt_acc)
            cs_ref[...] = jnp.zeros_like(cs_ref)

        window = pl.ds(pl.multiple_of(g * GROUP_Q, GROUP_Q), GROUP_K)
        valid = _valid_keys(g)
        for h in range(N_HEADS):
            cols = slice(h * HEAD_DIM, (h + 1) * HEAD_DIM)
            buf = h % 2
            qh, doh = q_ref[:, cols], do_ref[:, cols]
            kh, vh = k_ref[window, cols], v_ref[window, cols]
            s_ref[buf] = lax.dot_general(qh, kh, _DIMS["nt"], preferred_element_type=F32)
            dp_ref[buf] = lax.dot_general(doh, vh, _DIMS["nt"], preferred_element_type=F32)
            for r in range(GROUP_Q // SOFTMAX_ROWS):
                rows = slice(r * SOFTMAX_ROWS, (r + 1) * SOFTMAX_ROWS)
                p = _softmax_rows(s_ref.at[buf], t_ref.at[h], valid, rows)
                dp = dp_ref[buf, rows, :]
                ds = p * (dp - jnp.sum(dp * p, axis=-1, keepdims=True))
                chunk = (r * SOFTMAX_ROWS) // CHUNK
                shift = (GROUP - 1 - chunk) * CHUNK
                local = slice(r * SOFTMAX_ROWS - chunk * CHUNK, (r + 1) * SOFTMAX_ROWS - chunk * CHUNK)
                db_acc[h, local, shift:shift + GROUP_K] += ds
                p_ref[buf, rows, :] = p.astype(BF16)
                ds_ref[buf, rows, :] = (ds * (HEAD_DIM ** -0.5)).astype(BF16)
            dq_ref[:, cols] = jnp.dot(ds_ref[buf], kh, preferred_element_type=F32).astype(BF16)
            dkt_acc[cols, window] += lax.dot_general(qh, ds_ref[buf], _DIMS["tn"], preferred_element_type=F32)
            dvt_acc[cols, window] += lax.dot_general(doh, p_ref[buf], _DIMS["tn"], preferred_element_type=F32)
        cs_ref[0:1, :] += _colsum(dq_ref[...].astype(F32))

        @pl.when(g == n_groups - 1)
        def _():
            lo = (GROUP - 1) * CHUNK
            for h in range(N_HEADS):
                db_ref[h] = db_acc[h, :, lo:lo + BAND]
            inside = pl.ds(PAD_ROWS, seq)
            csk_ref[...] = jnp.sum(dkt_acc[:, inside], axis=1, keepdims=True)
            csv_ref[...] = jnp.sum(dvt_acc[:, inside], axis=1, keepdims=True)
            out_k = pltpu.make_async_copy(dkt_acc.at[:, inside], dkt_hbm, sems.at[0])
            out_v = pltpu.make_async_copy(dvt_acc.at[:, inside], dvt_hbm, sems.at[1])
            out_k.start()
            out_v.start()
            out_k.wait()
            out_v.wait()

    t_shape = (D_ATTN, seq + PAD_ROWS)
    outs, extra = _host_call(
        body, "attn_bwd", grid=(n_groups,),
        in_specs=[_tile(GROUP_Q, D_ATTN), _tile(GROUP_Q, D_ATTN), ANY, ANY, ANY],
        out_specs=[_tile(GROUP_Q, D_ATTN), ANY, ANY, _full((N_HEADS, CHUNK, BAND)), _full((8, D_ATTN)),
                   _full((D_ATTN, 1)), _full((D_ATTN, 1))],
        out_shape=[jax.ShapeDtypeStruct((seq, D_ATTN), BF16), jax.ShapeDtypeStruct((D_ATTN, seq), F32),
                   jax.ShapeDtypeStruct((D_ATTN, seq), F32), jax.ShapeDtypeStruct((N_HEADS, CHUNK, BAND), F32),
                   jax.ShapeDtypeStruct((8, D_ATTN), F32), jax.ShapeDtypeStruct((D_ATTN, 1), F32),
                   jax.ShapeDtypeStruct((D_ATTN, 1), F32)],
        scratch_shapes=[pltpu.VMEM(kpad.shape, BF16), pltpu.VMEM(vpad.shape, BF16), pltpu.VMEM(table.shape, F32),
                        pltpu.VMEM((N_HEADS, CHUNK, fold_w), F32), pltpu.VMEM(t_shape, F32),
                        pltpu.VMEM(t_shape, F32), pltpu.VMEM((2, GROUP_Q, GROUP_K), F32),
                        pltpu.VMEM((2, GROUP_Q, GROUP_K), F32), pltpu.VMEM((2, GROUP_Q, GROUP_K), BF16),
                        pltpu.VMEM((2, GROUP_Q, GROUP_K), BF16), pltpu.SemaphoreType.DMA((3,))],
        args=[qkv, dao, kpad, vpad, table], comm=comm)
    return outs, extra


CONV_ROWS = 256


def _ln_silu(u1, g, b):
    mu = jnp.mean(u1, axis=-1, keepdims=True)
    xc = u1 - mu
    rs = lax.rsqrt(jnp.mean(xc * xc, axis=-1, keepdims=True) + EPS)
    xhat = xc * rs
    u2 = xhat * g + b
    return xhat, rs, u2


def _glu_into(s_ref, a_ref, b_ref, ah_ref, bh_ref, first):
    halo = ah_ref[...] * _sig(bh_ref[...])
    s_ref[0:CONV_HALO, :] = jnp.where(first, 0.0, halo)
    s_ref[CONV_HALO:CONV_HALO + CONV_ROWS, :] = a_ref[...] * _sig(b_ref[...])


def _conv_fwd(zr, w_dw, b_dw, g_ln, b_ln):
    seq = zr.shape[0]

    def body(a_ref, b_ref, ah_ref, bh_ref, w_ref, bias_ref, g_ref, bl_ref, u1_ref, u3_ref, s_ref):
        _glu_into(s_ref, a_ref, b_ref, ah_ref, bh_ref, pl.program_id(0) == 0)
        acc = jnp.zeros((CONV_ROWS, D_CONV), F32) + bias_ref[...]
        for j in range(CONV_K):
            acc = acc + w_ref[j:j + 1, :] * s_ref[2 + j:2 + j + CONV_ROWS, :]
        u1_ref[...] = acc
        _, _, u2 = _ln_silu(acc, g_ref[...], bl_ref[...])
        u3_ref[...] = (u2 * _sig(u2)).astype(BF16)

    return pl.pallas_call(
        body, name="conv_fwd",
        out_shape=[jax.ShapeDtypeStruct((seq, D_CONV), F32), jax.ShapeDtypeStruct((seq, D_CONV), BF16)],
        grid=(seq // CONV_ROWS,),
        in_specs=[_tile(CONV_ROWS, D_CONV, 0), _tile(CONV_ROWS, D_CONV, 1),
                  _prev(CONV_HALO, D_CONV, CONV_ROWS, 0), _prev(CONV_HALO, D_CONV, CONV_ROWS, 1),
                  _full((CONV_K, D_CONV)), _full((1, D_CONV)), _full((1, D_CONV)), _full((1, D_CONV))],
        out_specs=[_tile(CONV_ROWS, D_CONV), _tile(CONV_ROWS, D_CONV)],
        scratch_shapes=[pltpu.VMEM((CONV_HALO + CONV_ROWS, D_CONV), F32)],
        compiler_params=_cparams(1),
    )(zr, zr, zr, zr, w_dw, b_dw, g_ln, b_ln)


def _conv_bwd(zr, u1, du3, w_dw, g_ln, b_ln):
    seq = zr.shape[0]
    n_tiles = seq // CONV_ROWS
    n_halo = seq // CONV_HALO
    ext = CONV_ROWS + CONV_HALO

    def body(a_ref, b_ref, ah_ref, bh_ref, u1_ref, u1n_ref, d3_ref, d3n_ref, w_ref, g_ref, bl_ref,
             da_ref, db_ref, dw_ref, small_ref, s_ref, d_ref):
        i = pl.program_id(0)

        @pl.when(i == 0)
        def _():
            dw_ref[...] = jnp.zeros_like(dw_ref)
            small_ref[...] = jnp.zeros_like(small_ref)

        _glu_into(s_ref, a_ref, b_ref, ah_ref, bh_ref, i == 0)
        gv, bv = g_ref[...], bl_ref[...]

        def du1_of(u1, d3):
            xhat, rs, u2 = _ln_silu(u1, gv, bv)
            sg = _sig(u2)
            du2 = d3 * (sg * (1.0 + u2 * (1.0 - sg)))
            dxh = du2 * gv
            du1 = rs * (dxh - jnp.mean(dxh, axis=-1, keepdims=True)
                        - xhat * jnp.mean(dxh * xhat, axis=-1, keepdims=True))
            return du1, du2, xhat

        du1, du2, xhat = du1_of(u1_ref[...], d3_ref[...])
        du1n, _, _ = du1_of(u1n_ref[...], d3n_ref[...])
        d_ref[0:CONV_ROWS, :] = du1
        d_ref[CONV_ROWS:ext, :] = jnp.where(i == n_tiles - 1, 0.0, du1n)
        small_ref[0:1, :] += _colsum(du1)
        small_ref[1:2, :] += _colsum(du2 * xhat)
        small_ref[2:3, :] += _colsum(du2)
        du0 = jnp.zeros((CONV_ROWS, D_CONV), F32)
        for j in range(CONV_K):
            dw_ref[j:j + 1, :] += _colsum(du1 * s_ref[2 + j:2 + j + CONV_ROWS, :])
            du0 = du0 + w_ref[j:j + 1, :] * d_ref[CONV_K - 1 - j:CONV_K - 1 - j + CONV_ROWS, :]
        sb = _sig(b_ref[...])
        da = du0 * sb
        dbv = du0 * a_ref[...] * sb * (1.0 - sb)
        da_ref[...] = da.astype(BF16)
        db_ref[...] = dbv.astype(BF16)
        small_ref[3:4, :] += _colsum(da)
        small_ref[4:5, :] += _colsum(dbv)

    return pl.pallas_call(
        body, name="conv_bwd",
        out_shape=[jax.ShapeDtypeStruct((seq, D_CONV), BF16), jax.ShapeDtypeStruct((seq, D_CONV), BF16),
                   jax.ShapeDtypeStruct((CONV_HALO, D_CONV), F32), jax.ShapeDtypeStruct((8, D_CONV), F32)],
        grid=(n_tiles,),
        in_specs=[_tile(CONV_ROWS, D_CONV, 0), _tile(CONV_ROWS, D_CONV, 1),
                  _prev(CONV_HALO, D_CONV, CONV_ROWS, 0), _prev(CONV_HALO, D_CONV, CONV_ROWS, 1),
                  _tile(CONV_ROWS, D_CONV), _next(CONV_HALO, D_CONV, CONV_ROWS, n_halo),
                  _tile(CONV_ROWS, D_CONV), _next(CONV_HALO, D_CONV, CONV_ROWS, n_halo),
                  _full((CONV_K, D_CONV)), _full((1, D_CONV)), _full((1, D_CONV))],
        out_specs=[_tile(CONV_ROWS, D_CONV), _tile(CONV_ROWS, D_CONV), _full((CONV_HALO, D_CONV)),
                   _full((8, D_CONV))],
        scratch_shapes=[pltpu.VMEM((ext, D_CONV), F32), pltpu.VMEM((ext, D_CONV), F32)],
        compiler_params=_cparams(1),
    )(zr, zr, zr, zr, u1, u1, du3, du3, w_dw, g_ln, b_ln)


MERGE_ROWS = 256


def _merge_fwd(ao, u3, zr, w_ao, w_co, b_co):
    seq = ao.shape[0]

    def body(ao_ref, u3_ref, ga_ref, gb_ref, wa_ref, wc_ref, bc_ref, y_ref, a_ref, cb_ref):
        a = jnp.dot(ao_ref[...], wa_ref[...], preferred_element_type=F32)
        cb = jnp.dot(u3_ref[...], wc_ref[...], preferred_element_type=F32) + bc_ref[...]
        a_ref[...] = a
        cb_ref[...] = cb
        y_ref[...] = (_sig(ga_ref[...]) * a + _sig(gb_ref[...]) * cb).astype(BF16)

    f32_out = jax.ShapeDtypeStruct((seq, D_MODEL), F32)
    return pl.pallas_call(
        body, name="merge_fwd",
        out_shape=[jax.ShapeDtypeStruct((seq, D_MODEL), BF16), f32_out, f32_out],
        grid=(seq // MERGE_ROWS,),
        in_specs=[_tile(MERGE_ROWS, D_ATTN), _tile(MERGE_ROWS, D_CONV), _tile(MERGE_ROWS, D_MODEL, 1),
                  _tile(MERGE_ROWS, D_MODEL, 2), _full(w_ao.shape), _full(w_co.shape), _full((1, D_MODEL))],
        out_specs=[_tile(MERGE_ROWS, D_MODEL)] * 3, compiler_params=_cparams(1),
    )(ao, u3, zr, zr, w_ao, w_co, b_co)


def _merge_bwd(dy, a, cb, zr):
    seq = dy.shape[0]

    def body(dy_ref, a_ref, cb_ref, ga_ref, gb_ref, da_ref, dcb_ref, dga_ref, dgb_ref, small_ref):
        i = pl.program_id(0)

        @pl.when(i == 0)
        def _():
            small_ref[...] = jnp.zeros_like(small_ref)

        dy_v = dy_ref[...]
        sa, sb = _sig(ga_ref[...]), _sig(gb_ref[...])
        dcb = dy_v * sb
        dga = dy_v * a_ref[...] * sa * (1.0 - sa)
        dgb = dy_v * cb_ref[...] * sb * (1.0 - sb)
        da_ref[...] = (dy_v * sa).astype(BF16)
        dcb_ref[...] = dcb.astype(BF16)
        dga_ref[...] = dga.astype(BF16)
        dgb_ref[...] = dgb.astype(BF16)
        small_ref[0:1, :] += _colsum(dga)
        small_ref[1:2, :] += _colsum(dgb)
        small_ref[2:3, :] += _colsum(dcb)

    bf = jax.ShapeDtypeStruct((seq, D_MODEL), BF16)
    return pl.pallas_call(
        body, name="merge_bwd", out_shape=[bf, bf, bf, bf, jax.ShapeDtypeStruct((8, D_MODEL), F32)],
        grid=(seq // MERGE_ROWS,),
        in_specs=[_tile(MERGE_ROWS, D_MODEL)] * 3 + [_tile(MERGE_ROWS, D_MODEL, 1), _tile(MERGE_ROWS, D_MODEL, 2)],
        out_specs=[_tile(MERGE_ROWS, D_MODEL)] * 4 + [_full((8, D_MODEL))], compiler_params=_cparams(1),
    )(dy, a, cb, zr, zr)


FFN_ROWS = 512
FFN_BLOCKS = D_FF // FFN_COLS
GELU_C = math.sqrt(2.0 / math.pi)


def _gelu(v):
    t = jnp.tanh(GELU_C * (v + 0.044715 * (v * v * v)))
    return 0.5 * v * (1.0 + t), t


def _gelu_grad(v, t):
    return 0.5 * (1.0 + t) + 0.5 * v * (1.0 - t * t) * (GELU_C * (1.0 + 3.0 * 0.044715 * (v * v)))


def _sublane_rows(ref, n):
    return [jnp.broadcast_to(ref[r:r + 1, :], (8, FFN_COLS)) for r in range(n)]


def _rolls(tile, shifts):
    return tuple(pltpu.roll(tile, s, 0) for s in shifts)


def _behind(prev_rolls, cur, row_id):
    rolls = _rolls(cur, (1, 2))
    x1 = jnp.where(row_id < 1, prev_rolls[0], rolls[0])
    x2 = jnp.where(row_id < 2, prev_rolls[1], rolls[1])
    return (x2, x1, cur), rolls


def _ahead(cur_rolls, next_rolls, row_id):
    return (jnp.where(row_id < 7, cur_rolls[0], next_rolls[0]), jnp.where(row_id < 6, cur_rolls[1], next_rolls[1]))


def _conv3(taps, w, bias):
    return w[0] * taps[0] + w[1] * taps[1] + w[2] * taps[2] + bias


def _ffn_specs(rows):
    tile = lambda off: pl.BlockSpec((rows, FFN_COLS), lambda j, i: (i, j + off))
    prev = lambda off: pl.BlockSpec((FFN_HALO, FFN_COLS),
                                    lambda j, i: (jnp.maximum(i * (rows // FFN_HALO) - 1, 0), j + off))
    wgt = lambda off: pl.BlockSpec((3, FFN_COLS), lambda j, i: (0, j + off))
    vec = lambda off: pl.BlockSpec((1, FFN_COLS), lambda j, i: (0, j + off))
    return tile, prev, wgt, vec


def _ffn_act(up, w_dw, b_dw):
    seq = up.shape[0]
    tile, prev, wgt, vec = _ffn_specs(FFN_ROWS)

    def body(v_ref, g_ref, vp_ref, gp_ref, wv_ref, wg_ref, bv_ref, bg_ref, act_ref):
        first = pl.program_id(1) == 0
        row_id = lax.broadcasted_iota(jnp.int32, (8, FFN_COLS), 0)
        wv, wg = _sublane_rows(wv_ref, 3), _sublane_rows(wg_ref, 3)
        (bv,), (bg,) = _sublane_rows(bv_ref, 1), _sublane_rows(bg_ref, 1)
        rolls_v = _rolls(jnp.where(first, 0.0, vp_ref[...]), (1, 2))
        rolls_g = _rolls(jnp.where(first, 0.0, gp_ref[...]), (1, 2))
        for row in range(0, FFN_ROWS, 16):
            halves = []
            for r in (row, row + 8):
                taps_v, rolls_v = _behind(rolls_v, v_ref[r:r + 8, :], row_id)
                taps_g, rolls_g = _behind(rolls_g, g_ref[r:r + 8, :], row_id)
                halves.append(_gelu(_conv3(taps_g, wg, bg))[0] * _conv3(taps_v, wv, bv))
            act_ref[row:row + 16, :] = jnp.concatenate(halves, axis=0).astype(BF16)

    return pl.pallas_call(
        body, name="ffn_act", out_shape=jax.ShapeDtypeStruct((seq, D_FF), BF16),
        grid=(FFN_BLOCKS, seq // FFN_ROWS),
        in_specs=[tile(0), tile(FFN_BLOCKS), prev(0), prev(FFN_BLOCKS), wgt(0), wgt(FFN_BLOCKS),
                  vec(0), vec(FFN_BLOCKS)],
        out_specs=tile(0), compiler_params=_cparams(2),
    )(up, up, up, up, w_dw, w_dw, b_dw, b_dw)


def _ffn_act_bwd(up, dact, w_dw, b_dw, comm=None):
    seq = up.shape[0]
    n_tiles = seq // FFN_ROWS
    n_halo = seq // FFN_HALO
    tile, prev, wgt, vec = _ffn_specs(FFN_ROWS)
    nxt = lambda off: pl.BlockSpec(
        (FFN_HALO, FFN_COLS), lambda j, i: (jnp.minimum((i + 1) * (FFN_ROWS // FFN_HALO), n_halo - 1), j + off))
    acc = lambda off: pl.BlockSpec((8, FFN_COLS), lambda j, i: (0, j + off))

    def body(v_ref, g_ref, vp_ref, gp_ref, vn_ref, gn_ref, da_ref, dan_ref, wv_ref, wg_ref, bv_ref, bg_ref,
             dv_out, dg_out, dwv_ref, dwg_ref, dbv_ref, dbg_ref):
        i = pl.program_id(1)
        first, last = i == 0, i == n_tiles - 1

        @pl.when(first)
        def _():
            for r in (dwv_ref, dwg_ref, dbv_ref, dbg_ref):
                r[...] = jnp.zeros_like(r)

        row_id = lax.broadcasted_iota(jnp.int32, (8, FFN_COLS), 0)
        wv, wg = _sublane_rows(wv_ref, 3), _sublane_rows(wg_ref, 3)
        (bv,), (bg,) = _sublane_rows(bv_ref, 1), _sublane_rows(bg_ref, 1)
        zero = jnp.zeros((8, FFN_COLS), F32)
        sums_v, sums_g = [zero] * 4, [zero] * 4
        rolls_v = _rolls(jnp.where(first, 0.0, vp_ref[...]), (1, 2))
        rolls_g = _rolls(jnp.where(first, 0.0, gp_ref[...]), (1, 2))
        behind = None
        done_v, done_g = [], []

        def grads(v_tile, g_tile, dact, rolls_v, rolls_g):
            taps_v, rolls_v = _behind(rolls_v, v_tile, row_id)
            taps_g, rolls_g = _behind(rolls_g, g_tile, row_id)
            val, gate = _conv3(taps_v, wv, bv), _conv3(taps_g, wg, bg)
            gel, t = _gelu(gate)
            return dact * gel, dact * val * _gelu_grad(gate, t), taps_v, taps_g, rolls_v, rolls_g

        def finish(tile, nxt, row):
            for (d, d_rolls), (_, n_rolls), w, done, o_ref in ((tile[0], nxt[0], wv, done_v, dv_out),
                                                               (tile[1], nxt[1], wg, done_g, dg_out)):
                d1, d2 = _ahead(d_rolls, n_rolls, row_id)
                done.append(w[2] * d + w[1] * d1 + w[0] * d2)
                if len(done) == 2:
                    o_ref[row - 16:row, :] = jnp.concatenate(done, axis=0).astype(BF16)
                    done.clear()

        for row in range(0, FFN_ROWS, 16):
            dact16 = da_ref[row:row + 16, :].astype(F32)
            for r, dact in ((row, dact16[0:8, :]), (row + 8, dact16[8:16, :])):
                dval, dgate, taps_v, taps_g, rolls_v, rolls_g = grads(v_ref[r:r + 8, :], g_ref[r:r + 8, :], dact,
                                                                      rolls_v, rolls_g)
                sums_v = [s + dval * x for s, x in zip(sums_v, taps_v)] + [sums_v[3] + dval]
                sums_g = [s + dgate * x for s, x in zip(sums_g, taps_g)] + [sums_g[3] + dgate]
                tile = ((dval, _rolls(dval, (7, 6))), (dgate, _rolls(dgate, (7, 6))))
                if behind is not None:
                    finish(behind, tile, r)
                behind = tile
        dact_next = jnp.where(last, 0.0, dan_ref[...].astype(F32)[0:FFN_HALO, :])
        dval, dgate, *_ = grads(vn_ref[...], gn_ref[...], dact_next, rolls_v, rolls_g)
        finish(behind, ((dval, _rolls(dval, (7, 6))), (dgate, _rolls(dgate, (7, 6)))), FFN_ROWS)
        for sums, dw_ref, db_ref in ((sums_v, dwv_ref, dbv_ref), (sums_g, dwg_ref, dbg_ref)):
            for tap in range(3):
                dw_ref[tap:tap + 1, :] += _colsum(sums[tap])
            db_ref[0:1, :] += _colsum(sums[3])

    half = jax.ShapeDtypeStruct((seq, D_FF), BF16)
    acc_shape = jax.ShapeDtypeStruct((8, D_FF), F32)
    return _host_call(
        lambda ins, outs, scratch: body(*ins, *outs, *scratch), "ffn_act_bwd", grid=(FFN_BLOCKS, n_tiles),
        in_specs=[tile(0), tile(FFN_BLOCKS), prev(0), prev(FFN_BLOCKS), nxt(0), nxt(FFN_BLOCKS),
                  tile(0), pl.BlockSpec((16, FFN_COLS), lambda j, i: (
                      jnp.minimum((i + 1) * (FFN_ROWS // 16), seq // 16 - 1), j)),
                  wgt(0), wgt(FFN_BLOCKS), vec(0), vec(FFN_BLOCKS)],
        out_specs=[tile(0), tile(0), acc(0), acc(0), acc(0), acc(0)],
        out_shape=[half, half, acc_shape, acc_shape, acc_shape, acc_shape],
        scratch_shapes=[], args=[up, up, up, up, up, up, dact, dact, w_dw, w_dw, b_dw, b_dw], comm=comm)


def _cols_to_blocks(full_cols):
    k, n8 = full_cols.shape
    return jnp.transpose(full_cols.reshape(k, N_DEV, n8 // N_DEV), (1, 0, 2))


def _rows_to_blocks(full_rows):
    r8, n = full_rows.shape
    return full_rows.reshape(N_DEV, r8 // N_DEV, n)


def _blocks_to_cols(gathered):
    _, k, n = gathered.shape
    return jnp.transpose(gathered, (1, 0, 2)).reshape(k, N_DEV * n)


def _pack_small(values):
    pieces = []
    for name, width in SMALL:
        flat = values[name].reshape(1, -1)
        if flat.shape[1] < width:
            flat = jnp.pad(flat, ((0, 0), (0, width - flat.shape[1])))
        pieces.append(flat)
    return jnp.concatenate(pieces, axis=1)


def _unpack_small(vec, shapes):
    out, off = {}, 0
    for name, width in SMALL:
        n = math.prod(shapes[name])
        out[name] = vec[:, off:off + n].reshape(shapes[name])
        off += width
    return out


def kernel(x, c, w_ada, b_ada, g_pre_mix, g_post_mix, w_in, b_in, rel_bias, w_attn_o, w_dw_conv, b_dw_conv, g_conv_ln, b_conv_ln, w_conv_o, b_conv_o, w_mix_o, g_pre_ffn, g_post_ffn, w_up, w_dw_ffn, b_dw_ffn, w_down, loss_target, m_w_ada, m_b_ada, m_g_pre_mix, m_g_post_mix, m_w_in, m_b_in, m_rel_bias, m_w_attn_o, m_w_dw_conv, m_b_dw_conv, m_g_conv_ln, m_b_conv_ln, m_w_conv_o, m_b_conv_o, m_w_mix_o, m_g_pre_ffn, m_g_post_ffn, m_w_up, m_w_dw_ffn, m_b_dw_ffn, m_w_down, v_w_ada, v_b_ada, v_g_pre_mix, v_g_post_mix, v_w_in, v_b_in, v_rel_bias, v_w_attn_o, v_w_dw_conv, v_b_dw_conv, v_g_conv_ln, v_b_conv_ln, v_w_conv_o, v_b_conv_o, v_w_mix_o, v_g_pre_ffn, v_g_post_ffn, v_w_up, v_w_dw_ffn, v_b_dw_ffn, v_w_down):
    names = ["w_ada", "b_ada", "g_pre_mix", "g_post_mix", "w_in", "b_in", "rel_bias", "w_attn_o", "w_dw_conv",
             "b_dw_conv", "g_conv_ln", "b_conv_ln", "w_conv_o", "b_conv_o", "w_mix_o", "g_pre_ffn", "g_post_ffn",
             "w_up", "w_dw_ffn", "b_dw_ffn", "w_down"]
    weights = dict(zip(names, [w_ada, b_ada, g_pre_mix, g_post_mix, w_in, b_in, rel_bias, w_attn_o, w_dw_conv,
                               b_dw_conv, g_conv_ln, b_conv_ln, w_conv_o, b_conv_o, w_mix_o, g_pre_ffn,
                               g_post_ffn, w_up, w_dw_ffn, b_dw_ffn, w_down]))
    mom_m = dict(zip(names, [m_w_ada, m_b_ada, m_g_pre_mix, m_g_post_mix, m_w_in, m_b_in, m_rel_bias, m_w_attn_o,
                             m_w_dw_conv, m_b_dw_conv, m_g_conv_ln, m_b_conv_ln, m_w_conv_o, m_b_conv_o,
                             m_w_mix_o, m_g_pre_ffn, m_g_post_ffn, m_w_up, m_w_dw_ffn, m_b_dw_ffn, m_w_down]))
    mom_v = dict(zip(names, [v_w_ada, v_b_ada, v_g_pre_mix, v_g_post_mix, v_w_in, v_b_in, v_rel_bias, v_w_attn_o,
                             v_w_dw_conv, v_b_dw_conv, v_g_conv_ln, v_b_conv_ln, v_w_conv_o, v_b_conv_o,
                             v_w_mix_o, v_g_pre_ffn, v_g_post_ffn, v_w_up, v_w_dw_ffn, v_b_dw_ffn, v_w_down]))
    shapes = {n: w.shape for n, w in weights.items()}

    seq = x.shape[1]
    me = 4 * lax.axis_index("x") + 2 * lax.axis_index("y") + lax.axis_index("c")
    x2 = x.reshape(seq, D_MODEL)
    target = loss_target.reshape(seq, D_MODEL)
    sq = lambda a: a.reshape(a.shape[1:])
    bf = lambda a: sq(a).astype(BF16)

    c_act = _silu_vec(c)
    c_all, g_in, g_dwc, g_dwf = _run_comm(
        _gather_comm([c_act, bf(w_in), sq(w_dw_conv), sq(w_dw_ffn)]), "gather_first")
    c_all = c_all.reshape(N_DEV, D_MODEL)
    wf_in = _blocks_to_cols(g_in)
    wf_dwc = _blocks_to_cols(g_dwc)
    wf_dwf = _blocks_to_cols(g_dwf)

    (mod_all,) = _run_comm(_gather_comm([_ada_fwd(c_all, sq(w_ada))]), "gather_mod")
    mod = lax.dynamic_index_in_dim(mod_all, me, axis=1, keepdims=False)
    mod6 = (mod.reshape(1, 6 * D_MODEL) + b_ada).reshape(6, D_MODEL)

    h1 = _pre_mix(x2, mod6, g_pre_mix)
    qkv = _mm(h1, wf_in[:, :3 * D_ATTN], "nn", BF16, "in_proj_qkv", bias=b_in[:, :3 * D_ATTN], tm=1024, tn=768)
    zr = _mm(h1, wf_in[:, 3 * D_ATTN:], "nn", F32, "in_proj_rest", bias=b_in[:, 3 * D_ATTN:], tm=1024, tn=1024)
    kpad = jnp.pad(qkv[:, D_ATTN:2 * D_ATTN], ((PAD_ROWS, 0), (0, 0)))
    vpad = jnp.pad(qkv[:, 2 * D_ATTN:], ((PAD_ROWS, 0), (0, 0)))
    table = jnp.transpose(_bias_table(sq(rel_bias)), (1, 0, 2))
    ao, (g_ao, g_co, g_mo, g_up, g_dn) = _attn_fwd(
        qkv, kpad, vpad, table,
        comm=_gather_comm([bf(w_attn_o), bf(w_conv_o), bf(w_mix_o), bf(w_up), bf(w_down)]))
    wf_ao = _blocks_to_cols(g_ao)
    wf_co = _blocks_to_cols(g_co)
    wf_mo = g_mo.reshape(D_MODEL, D_MODEL)
    wf_up = _blocks_to_cols(g_up)
    wf_dn = g_dn.reshape(D_FF, D_MODEL)
    u1, u3 = _conv_fwd(zr, wf_dwc, b_dw_conv, g_conv_ln, b_conv_ln)
    y, a_br, cb_br = _merge_fwd(ao, u3, zr, wf_ao, wf_co, b_conv_o)
    ymix = _mm(y, wf_mo, "nn", F32, "mix_o", tm=1024, tn=1024)
    x1, h2 = _post_mix_pre_ffn(ymix, x2, mod6, g_post_mix, g_pre_ffn)
    up = _mm(h2, wf_up, "nn", F32, "ffn_up", tm=1024, tn=1408)
    act = _ffn_act(up, wf_dwf, b_dw_ffn)
    yf = _mm(act, wf_dn, "nn", F32, "ffn_down", tm=512, tn=1024)
    loss_lanes, dout, dyf, small_f = _final(yf, x1, target, mod6, g_post_ffn)
    loss = lax.psum(0.5 * loss_lanes[0, 0], ("x", "y", "c"))

    dact = _mm(dyf, wf_dn, "nt", BF16, "ffn_down_dx", tm=1024, tn=1408)
    gw_down = _mm(act, dyf, "tn", BF16, "ffn_down_dw", tm=256, tn=1024)
    (dup_v, dup_g, dwv, dwg, dbv, dbg), (parts_down,) = _ffn_act_bwd(
        up, dact, wf_dwf, b_dw_ffn, comm=_scatter_comm([_rows_to_blocks(gw_down)]))
    dup = jnp.concatenate([dup_v, dup_g], axis=1)
    dh2 = _mm(dup, wf_up, "nt", F32, "ffn_up_dx")
    gw_up = _mm(h2, dup, "tn", BF16, "ffn_up_dw")
    dx1, dymix, small_m = _mid_bwd(dh2, x1, dout, ymix, mod6, g_pre_ffn, g_post_mix)
    dy = _mm(dymix, wf_mo, "nt", F32, "mix_o_dx", tm=1024, tn=1024)
    gw_mo = _mm(y, dymix, "tn", BF16, "mix_o_dw")
    da, dcb, dga, dgb, small_g = _merge_bwd(dy, a_br, cb_br, zr)
    dao = _mm(da, wf_ao, "nt", BF16, "attn_o_dx", tm=1024)
    gw_ao = _mm(ao, da, "tn", BF16, "attn_o_dw")
    du3 = _mm(dcb, wf_co, "nt", F32, "conv_o_dx", tm=1024)
    gw_co = _mm(u3, dcb, "tn", BF16, "conv_o_dw")
    (dq, dkt, dvt, dbias, small_a, cs_k, cs_v), (parts_up, parts_mo, parts_ao, parts_co) = _attn_bwd(
        qkv, kpad, vpad, table, dao,
        comm=_scatter_comm([_cols_to_blocks(gw_up), _rows_to_blocks(gw_mo), _cols_to_blocks(gw_ao),
                            _cols_to_blocks(gw_co)]))
    g_rel = _bias_grad(jnp.transpose(dbias, (1, 0, 2)))
    dglu_a, dglu_b, dw_conv, small_c = _conv_bwd(zr, u1, du3, wf_dwc, g_conv_ln, b_conv_ln)
    dz = jnp.concatenate([dq, dkt.T.astype(BF16), dvt.T.astype(BF16), dglu_a, dglu_b, dga, dgb], axis=1)
    gw_in = _mm(h1, dz, "tn", BF16, "in_proj_dw")
    dh1, (parts_in,) = _mm(dz, wf_in, "nt", F32, "in_proj_dx", comm=_scatter_comm([_cols_to_blocks(gw_in)]))
    grad_x, small_x = _pre_mix_bwd(dh1, x2, dx1, mod6, g_pre_mix)

    dmod = jnp.concatenate([small_x[2:3], small_x[1:2], small_m[4:5], small_m[2:3], small_m[1:2], small_f[1:2]],
                           axis=1)
    partial = {
        "b_ada": dmod, "g_pre_mix": small_x[0:1], "g_post_mix": small_m[3:4],
        "b_in": jnp.concatenate([small_a[0:1], cs_k.reshape(1, D_ATTN), cs_v.reshape(1, D_ATTN), small_c[3:4],
                                 small_c[4:5], small_g[0:1], small_g[1:2]], axis=1),
        "rel_bias": g_rel, "b_dw_conv": small_c[0:1], "g_conv_ln": small_c[1:2], "b_conv_ln": small_c[2:3],
        "b_conv_o": small_g[2:3], "g_pre_ffn": small_m[0:1], "g_post_ffn": small_f[0:1],
        "b_dw_ffn": jnp.concatenate([dbv[0:1], dbg[0:1]], axis=1),
    }
    dw_ffn = jnp.concatenate([dwv[0:3], dwg[0:3]], axis=1)
    packed = jnp.concatenate([_pack_small(partial), dmod, dw_conv[0:CONV_K].reshape(1, CONV_K * D_CONV),
                              dw_ffn.reshape(1, 3 * 2 * D_FF)], axis=1)
    (gathered,) = _run_comm(_gather_comm([packed]), "gather_small")
    summed, d_small, m_small, v_small = _small_adamw(
        gathered.reshape(N_DEV, packed.shape[1]), _pack_small(weights), _pack_small(mom_m), _pack_small(mom_v))
    off = SMALL_TOTAL
    dmod_all = gathered.reshape(N_DEV, packed.shape[1])[:, off:off + 6 * D_MODEL]
    off += 6 * D_MODEL
    g_dwc_full = summed[:, off:off + CONV_K * D_CONV].reshape(CONV_K, D_CONV)
    off += CONV_K * D_CONV
    g_dwf_full = summed[:, off:off + 3 * 2 * D_FF].reshape(3, 2 * D_FF)

    grads, deltas, new_m, new_v = {}, {}, {}, {}
    for dst, vec in ((grads, summed), (deltas, d_small), (new_m, m_small), (new_v, v_small)):
        dst.update(_unpack_small(vec, shapes))

    def local_update(name, grad):
        g, d, m_new, v_new = _adamw(sq(weights[name]), sq(mom_m[name]), sq(mom_v[name]), "adamw_" + name, g=grad)
        for dst, val in ((grads, g), (deltas, d), (new_m, m_new), (new_v, v_new)):
            dst[name] = val.reshape(shapes[name])

    local_update("w_dw_conv", lax.dynamic_slice_in_dim(g_dwc_full, me * (D_CONV // N_DEV), D_CONV // N_DEV, 1))
    local_update("w_dw_ffn", lax.dynamic_slice_in_dim(g_dwf_full, me * (2 * D_FF // N_DEV), 2 * D_FF // N_DEV, 1))
    ada_cols = 6 * D_MODEL // N_DEV
    local_update("w_ada", _ada_grad(c_all, lax.dynamic_slice_in_dim(dmod_all, me * ada_cols, ada_cols, 1)))

    for name, part in (("w_in", parts_in), ("w_attn_o", parts_ao), ("w_conv_o", parts_co), ("w_mix_o", parts_mo),
                       ("w_up", parts_up), ("w_down", parts_down)):
        g, d, m_new, v_new = _adamw(sq(weights[name]), sq(mom_m[name]), sq(mom_v[name]), "adamw_" + name,
                                    parts=part)
        for dst, val in ((grads, g), (deltas, d), (new_m, m_new), (new_v, v_new)):
            dst[name] = val.reshape(shapes[name])

    return (loss, grad_x.reshape(x.shape), *[grads[n] for n in names], *[deltas[n] for n in names],
            *[new_m[n] for n in names], *[new_v[n] for n in names])
```

```python
import functools
import math

import jax
import jax.numpy as jnp
from jax import lax
from jax.experimental import pallas as pl
from jax.experimental.pallas import tpu as pltpu

F32 = jnp.float32
BF16 = jnp.bfloat16
HIGHEST = lax.Precision.HIGHEST

D_MODEL = 1024
CHUNK = 64
LEFT_CHUNKS = 8
BAND = (LEFT_CHUNKS + 1) * CHUNK
PAD_ROWS = LEFT_CHUNKS * CHUNK
GROUP = 4
GROUP_Q = GROUP * CHUNK
GROUP_K = GROUP_Q + PAD_ROWS
SOFTMAX_ROWS = 16
TOEPLITZ = 640
N_HEADS = 8
HEAD_DIM = 64
D_ATTN = 512
D_CONV = 512
CONV_K = 31
CONV_HALO = 32
MAX_REL = 128
N_REL = 2 * MAX_REL + 1
D_FF = 2816
FFN_HALO = 8
FFN_COLS = 256
EPS = 1e-6
NEG_INF = -1e30
N_DEV = 8

ADAM_LR = 0.001
ADAM_B1 = 0.9
ADAM_B2 = 0.999
ADAM_EPS = 1e-08
ADAM_WD = 0.01
ADAM_STEP = 10

VMEM_LIMIT_BYTES = 56 * 1024 * 1024
ADAMW_BLOCK_BYTES = 768 * 1024

MESH = pl.DeviceIdType.MESH
ANY = pl.BlockSpec(memory_space=pl.ANY)

SH_M, SC_M, GT_M, SH_F, SC_F, GT_F = range(6)

SMALL = (("b_ada", 6144), ("g_pre_mix", 1024), ("g_post_mix", 1024), ("b_in", 4608), ("b_dw_conv", 512),
         ("g_conv_ln", 512), ("b_conv_ln", 512), ("b_conv_o", 1024), ("g_pre_ffn", 1024), ("g_post_ffn", 1024),
         ("b_dw_ffn", 5632))
PACKED_TOTAL = sum(n for _, n in SMALL) + CONV_K * D_CONV + 3 * 2 * D_FF


def _cparams(n_axes):
    return pltpu.CompilerParams(vmem_limit_bytes=VMEM_LIMIT_BYTES,
                                dimension_semantics=("arbitrary",) * n_axes)


def _sig(v):
    return 1.0 / (1.0 + jnp.exp(-v))


def _pick(n, target):
    if n <= target:
        return n
    t = target - target % 128
    while n % t:
        t -= 128
    return t


def _tile(rows, cols, col=0):
    return pl.BlockSpec((rows, cols), lambda i: (i, col))


def _full(shape):
    zeros = (0,) * len(shape)
    return pl.BlockSpec(shape, lambda i: zeros)


def _prev(halo, cols, rows, col=0):
    return pl.BlockSpec((halo, cols), lambda i: (jnp.maximum(i * (rows // halo) - 1, 0), col))


def _next(halo, cols, rows, n_blocks, col=0):
    return pl.BlockSpec((halo, cols), lambda i: (jnp.minimum((i + 1) * (rows // halo), n_blocks - 1), col))


class _Comm:
    def __init__(self, inputs, out_shapes, sems, start, finish):
        self.inputs, self.out_shapes, self.sems, self.start, self.finish = inputs, out_shapes, sems, start, finish


def _host_call(body, name, grid, in_specs, out_specs, out_shape, scratch_shapes, args, comm=None):
    n_in, n_out, n_scr = len(args), len(out_shape), len(scratch_shapes)
    c_in = list(comm.inputs) if comm else []
    c_out = list(comm.out_shapes) if comm else []
    c_sem = list(comm.sems) if comm else []

    def full(*refs):
        bounds = [0, n_in, len(c_in), n_out, len(c_out), n_scr, len(c_sem)]
        cuts = [sum(bounds[:i + 1]) for i in range(len(bounds))]
        ins, cins, outs, couts, scr, csems = (refs[lo:hi] for lo, hi in zip(cuts[:-1], cuts[1:]))
        if comm:
            first = functools.reduce(jnp.logical_and, [pl.program_id(ax) == 0 for ax in range(len(grid))])
            pl.when(first)(lambda: comm.start(cins, couts, csems))
        body(ins, outs, scr)
        if comm:
            last = functools.reduce(jnp.logical_and, [pl.program_id(ax) == grid[ax] - 1 for ax in range(len(grid))])
            pl.when(last)(lambda: comm.finish(cins, couts, csems))

    res = pl.pallas_call(
        full, name=name, grid=grid, in_specs=list(in_specs) + [ANY] * len(c_in),
        out_specs=list(out_specs) + [ANY] * len(c_out), out_shape=list(out_shape) + c_out,
        scratch_shapes=list(scratch_shapes) + c_sem, compiler_params=_cparams(len(grid)),
    )(*args, *c_in)
    return list(res[:n_out]), list(res[n_out:])


def _run_comm(comm, name):
    n_in, n_out = len(comm.inputs), len(comm.out_shapes)

    def body(*refs):
        ins, outs, sems = refs[:n_in], refs[n_in:n_in + n_out], refs[n_in + n_out:]
        comm.start(ins, outs, sems)
        comm.finish(ins, outs, sems)

    return pl.pallas_call(
        body, name=name, out_shape=list(comm.out_shapes), in_specs=[ANY] * n_in, out_specs=[ANY] * n_out,
        scratch_shapes=list(comm.sems),
    )(*comm.inputs)


def _place():
    return lax.axis_index("x"), lax.axis_index("y"), lax.axis_index("c")


def _gather_comm(arrs):
    n = len(arrs)

    def plan(ins, outs, sems):
        send_sems, recv_sems, local_sems = sems
        x, y, c = _place()
        me, sibling = (x, y, c), (x, y, 1 - c)
        chips = [(1 - x, y), (x, 1 - y), (1 - x, 1 - y)]

        def block(k, p):
            return outs[k].at[4 * p[0] + 2 * p[1] + p[2]]

        def copy(k, s, blk, to, src=None):
            return pltpu.make_async_remote_copy(
                src_ref=block(k, blk) if src is None else src, dst_ref=block(k, blk),
                send_sem=send_sems.at[7 * k + s], recv_sem=recv_sems.at[7 * k + s],
                device_id=to, device_id_type=MESH)

        mine = [pltpu.make_async_copy(ins[k], block(k, me), local_sems.at[k]) for k in range(n)]
        first = []
        for k in range(n):
            first.append(copy(k, 0, me, sibling, src=ins[k]))
            for j, chip in enumerate(chips):
                first.append(copy(k, 1 + j, me, (*chip, c), src=ins[k]))
        return me, sibling, chips, c, copy, mine, first

    def start(ins, outs, sems):
        *_, mine, first = plan(ins, outs, sems)
        for cp in mine + first:
            cp.start()

    def finish(ins, outs, sems):
        me, sibling, chips, c, copy, mine, first = plan(ins, outs, sems)
        passed = []
        for j, chip in enumerate(chips):
            for k in range(n):
                copy(k, 1 + j, (*chip, c), me).wait_recv()
                fwd = copy(k, 4 + j, (*chip, c), sibling)
                fwd.start()
                passed.append(fwd)
        for k in range(n):
            copy(k, 0, sibling, me).wait_recv()
        for j, chip in enumerate(chips):
            for k in range(n):
                copy(k, 4 + j, (*chip, 1 - c), me).wait_recv()
        for cp in first + passed:
            cp.wait_send()
        for cp in mine:
            cp.wait()

    return _Comm(list(arrs), [jax.ShapeDtypeStruct((N_DEV,) + a.shape, a.dtype) for a in arrs],
                 [pltpu.SemaphoreType.DMA((7 * n,)), pltpu.SemaphoreType.DMA((7 * n,)),
                  pltpu.SemaphoreType.DMA((n,))], start, finish)


def _scatter_comm(blocks):
    n = len(blocks)

    def plan(ins, outs, sems, arrivals):
        send_sems, recv_sems, local_sems = sems
        x, y, c = _place()
        me = 4 * x + 2 * y + c
        local = [pltpu.make_async_copy(ins[k].at[me], outs[k].at[me], local_sems.at[k]) for k in range(n)]
        sends, recvs = [], []
        for k in range(n):
            for mask in range(1, N_DEV):
                px = 1 - x if mask & 4 else x
                py = 1 - y if mask & 2 else y
                pc = 1 - c if mask & 1 else c
                peer = 4 * px + 2 * py + pc
                sem = 7 * k + mask - 1
                both = dict(send_sem=send_sems.at[sem], recv_sem=recv_sems.at[sem], device_id=(px, py, pc),
                            device_id_type=MESH)
                sends.append(pltpu.make_async_remote_copy(src_ref=ins[k].at[peer], dst_ref=outs[k].at[me], **both))
                if arrivals:
                    recvs.append(pltpu.make_async_remote_copy(src_ref=ins[k].at[me], dst_ref=outs[k].at[peer],
                                                              **both))
        return local, sends, recvs

    def start(ins, outs, sems):
        local, sends, _ = plan(ins, outs, sems, arrivals=False)
        for cp in local + sends:
            cp.start()

    def finish(ins, outs, sems):
        local, sends, recvs = plan(ins, outs, sems, arrivals=True)
        for cp in recvs:
            cp.wait_recv()
        for cp in sends:
            cp.wait_send()
        for cp in local:
            cp.wait()

    return _Comm(list(blocks), [jax.ShapeDtypeStruct(b.shape, b.dtype) for b in blocks],
                 [pltpu.SemaphoreType.DMA((7 * n,)), pltpu.SemaphoreType.DMA((7 * n,)),
                  pltpu.SemaphoreType.DMA((n,))], start, finish)


_DIMS = {"nn": (((1,), (0,)), ((), ())), "nt": (((1,), (1,)), ((), ())), "tn": (((0,), (0,)), ((), ()))}


def _mm(a, b, mode, out_dtype, name, bias=None, tm=512, tn=512, comm=None):
    assert a.dtype == BF16 and b.dtype == BF16
    if mode == "tn":
        k_dim, m_dim = a.shape
    else:
        m_dim, k_dim = a.shape
    n_dim = b.shape[0] if mode == "nt" else b.shape[1]
    tm, tn = _pick(m_dim, tm), _pick(n_dim, tn)
    a_spec = (pl.BlockSpec((k_dim, tm), lambda i, j: (0, i)) if mode == "tn"
              else pl.BlockSpec((tm, k_dim), lambda i, j: (i, 0)))
    b_spec = (pl.BlockSpec((tn, k_dim), lambda i, j: (j, 0)) if mode == "nt"
              else pl.BlockSpec((k_dim, tn), lambda i, j: (0, j)))
    in_specs = [a_spec, b_spec]
    args = [a, b]
    if bias is not None:
        in_specs.append(pl.BlockSpec((1, tn), lambda i, j: (0, j)))
        args.append(bias)
    dims = _DIMS[mode]

    def body(ins, outs, scratch):
        total = lax.dot_general(ins[0][...], ins[1][...], dims, preferred_element_type=F32)
        if bias is not None:
            total = total + ins[2][...]
        outs[0][...] = total.astype(out_dtype)

    (out,), extra = _host_call(
        body, name, grid=(m_dim // tm, n_dim // tn), in_specs=in_specs,
        out_specs=[pl.BlockSpec((tm, tn), lambda i, j: (i, j))],
        out_shape=[jax.ShapeDtypeStruct((m_dim, n_dim), out_dtype)], scratch_shapes=[], args=args, comm=comm)
    return out if comm is None else (out, extra)


def _adam_math(w, g, m, v):
    m = ADAM_B1 * m + (1.0 - ADAM_B1) * g
    v = ADAM_B2 * v + (1.0 - ADAM_B2) * (g * g)
    m_hat = m / (1.0 - ADAM_B1 ** ADAM_STEP)
    v_hat = v / (1.0 - ADAM_B2 ** ADAM_STEP)
    delta = -ADAM_LR * (m_hat / (jnp.sqrt(v_hat) + ADAM_EPS) + ADAM_WD * w)
    return delta, m, v


def _adamw(w, m, v, name, g=None, parts=None):
    rows, cols = w.shape
    tr = rows
    if rows * cols * 4 > ADAMW_BLOCK_BYTES:
        tr = max(t for t in range(16, rows, 16) if rows % t == 0 and t * cols * 4 <= ADAMW_BLOCK_BYTES)

    def body(w_ref, m_ref, v_ref, g_ref, go_ref, d_ref, mo_ref, vo_ref):
        if parts is None:
            grad = g_ref[...]
        else:
            grad = g_ref[0].astype(F32)
            for d in range(1, N_DEV):
                grad = grad + g_ref[d].astype(F32)
        delta, m_new, v_new = _adam_math(w_ref[...], grad, m_ref[...], v_ref[...])
        go_ref[...] = grad
        d_ref[...] = delta
        mo_ref[...] = m_new
        vo_ref[...] = v_new

    spec = _tile(tr, cols)
    g_spec = spec if parts is None else pl.BlockSpec((N_DEV, tr, cols), lambda i: (0, i, 0))
    shape = jax.ShapeDtypeStruct((rows, cols), F32)
    return pl.pallas_call(
        body, name=name, out_shape=[shape] * 4, grid=(rows // tr,),
        in_specs=[spec, spec, spec, g_spec], out_specs=[spec] * 4, compiler_params=_cparams(1),
    )(w, m, v, g if parts is None else parts)


def _pack_grads(small_x, small_m, small_f, small_g, small_a, small_c, dbv, dbg, dwv, dwg, dw_conv):
    pieces = [
        (small_x, 2, D_MODEL), (small_x, 1, D_MODEL), (small_m, 4, D_MODEL), (small_m, 2, D_MODEL),
        (small_m, 1, D_MODEL), (small_f, 1, D_MODEL),
        (small_x, 0, D_MODEL), (small_m, 3, D_MODEL),
        (small_a, 0, D_ATTN), (small_a, 1, D_ATTN), (small_a, 2, D_ATTN), (small_c, 3, D_CONV),
        (small_c, 4, D_CONV), (small_g, 0, D_MODEL), (small_g, 1, D_MODEL),
        (small_c, 0, D_CONV), (small_c, 1, D_CONV), (small_c, 2, D_CONV),
        (small_g, 2, D_MODEL), (small_m, 0, D_MODEL), (small_f, 0, D_MODEL),
        (dbv, 0, D_FF), (dbg, 0, D_FF),
    ]
    pieces += [(dw_conv, j, D_CONV) for j in range(CONV_K)]
    pieces += [(src, tap, D_FF) for tap in range(3) for src in (dwv, dwg)]
    sources = [small_x, small_m, small_f, small_g, small_a, small_c, dbv, dbg, dwv, dwg, dw_conv]
    assert sum(width for _, _, width in pieces) == PACKED_TOTAL

    def body(*refs):
        o_ref = refs[-1]
        ref_of = {id(src): ref for src, ref in zip(sources, refs)}
        off = 0
        for src, row, width in pieces:
            o_ref[:, off:off + width] = ref_of[id(src)][row:row + 1, :]
            off += width

    return pl.pallas_call(body, name="pack_grads", out_shape=jax.ShapeDtypeStruct((1, PACKED_TOTAL), F32))(*sources)


def _small_adamw(gathered, gathered_rel, weights, mom_m, mom_v):
    vec_names = [name for name, _ in SMALL]
    states = []
    for name in vec_names + ["rel_bias"]:
        states += [weights[name], mom_m[name], mom_v[name]]
    states = [a.reshape(a.shape[1:]) if a.ndim == 3 else a for a in states]
    n_state = len(states)

    def body(*refs):
        g_ref, rel_ref = refs[0], refs[1]
        state_refs, out_refs = refs[2:2 + n_state], refs[2 + n_state:]
        total = g_ref[0:1, :]
        rel = rel_ref[0]
        for d in range(1, N_DEV):
            total = total + g_ref[d:d + 1, :]
            rel = rel + rel_ref[d]
        off = 0
        for n, (name, width) in enumerate(SMALL):
            grad = total[:, off:off + width]
            w_ref, m_ref, v_ref = state_refs[3 * n:3 * n + 3]
            for ref, val in zip(out_refs[4 * n:4 * n + 4], (grad,) + _adam_math(w_ref[...], grad, m_ref[...], v_ref[...])):
                ref[...] = val
            off += width
        n = len(SMALL)
        w_ref, m_ref, v_ref = state_refs[3 * n:3 * n + 3]
        for ref, val in zip(out_refs[4 * n:4 * n + 4], (rel,) + _adam_math(w_ref[...], rel, m_ref[...], v_ref[...])):
            ref[...] = val
        dwc_ref, dwf_ref = out_refs[4 * n + 4:]
        dwc_ref[...] = jnp.zeros_like(dwc_ref)
        dwf_ref[...] = jnp.zeros_like(dwf_ref)
        for j in range(CONV_K):
            dwc_ref[j:j + 1, :] = total[:, off:off + D_CONV]
            off += D_CONV
        for tap in range(3):
            dwf_ref[tap:tap + 1, :] = total[:, off:off + 2 * D_FF]
            off += 2 * D_FF

    out_shape = []
    for k in range(n_state // 3):
        out_shape += [jax.ShapeDtypeStruct(states[3 * k].shape, F32)] * 4
    out_shape += [jax.ShapeDtypeStruct((CONV_HALO, D_CONV), F32), jax.ShapeDtypeStruct((8, 2 * D_FF), F32)]
    res = pl.pallas_call(
        body, name="small_adamw", out_shape=out_shape,
        compiler_params=pltpu.CompilerParams(vmem_limit_bytes=VMEM_LIMIT_BYTES),
    )(gathered, gathered_rel, *states)
    updates = {name: tuple(res[4 * n:4 * n + 4]) for n, name in enumerate(vec_names + ["rel_bias"])}
    return updates, res[-2], res[-1]


def _silu_vec(c):
    def body(c_ref, o_ref):
        v = c_ref[...]
        o_ref[...] = v * _sig(v)

    return pl.pallas_call(body, name="silu_c", out_shape=jax.ShapeDtypeStruct(c.shape, F32))(c)


def _ada_fwd(c_all, w_shard):
    def body(c_ref, w_ref, o_ref):
        o_ref[...] = jnp.dot(c_ref[...], w_ref[...], precision=HIGHEST, preferred_element_type=F32)

    return pl.pallas_call(
        body, name="ada_fwd", out_shape=jax.ShapeDtypeStruct((N_DEV, w_shard.shape[1]), F32),
        compiler_params=pltpu.CompilerParams(vmem_limit_bytes=VMEM_LIMIT_BYTES),
    )(c_all, w_shard)


def _ada_grad(c_all, dmod_shard):
    def body(c_ref, d_ref, o_ref):
        o_ref[...] = lax.dot_general(c_ref[...], d_ref[...], _DIMS["tn"], precision=HIGHEST,
                                     preferred_element_type=F32)

    return pl.pallas_call(
        body, name="ada_grad", out_shape=jax.ShapeDtypeStruct((D_MODEL, dmod_shard.shape[1]), F32),
        compiler_params=pltpu.CompilerParams(vmem_limit_bytes=VMEM_LIMIT_BYTES),
    )(c_all, dmod_shard)


ROWS = 256


def _rms(v):
    r = lax.rsqrt(jnp.mean(v * v, axis=-1, keepdims=True) + EPS)
    return v * r, r


def _rms_bwd(dxn, xn, r):
    return r * (dxn - xn * jnp.mean(dxn * xn, axis=-1, keepdims=True))


def _colsum(v):
    return jnp.sum(v, axis=0, keepdims=True)


def _pre_mix(x, mod6, g1):
    seq = x.shape[0]

    def body(x_ref, mod_ref, g_ref, h_ref):
        xn, _ = _rms(x_ref[...])
        y = xn * g_ref[...]
        h_ref[...] = (y * (1.0 + mod_ref[SC_M:SC_M + 1, :]) + mod_ref[SH_M:SH_M + 1, :]).astype(BF16)

    return pl.pallas_call(
        body, name="pre_mix", out_shape=jax.ShapeDtypeStruct((seq, D_MODEL), BF16), grid=(seq // ROWS,),
        in_specs=[_tile(ROWS, D_MODEL), _full((6, D_MODEL)), _full((1, D_MODEL))],
        out_specs=_tile(ROWS, D_MODEL), compiler_params=_cparams(1),
    )(x, mod6, g1)


def _post_mix_pre_ffn(ymix, x, mod6, g2, g3):
    seq = x.shape[0]

    def body(y_ref, x_ref, mod_ref, g2_ref, g3_ref, x1_ref, h_ref):
        yn, _ = _rms(y_ref[...])
        x1 = x_ref[...] + mod_ref[GT_M:GT_M + 1, :] * (yn * g2_ref[...])
        x1_ref[...] = x1
        xn, _ = _rms(x1)
        y3 = xn * g3_ref[...]
        h_ref[...] = (y3 * (1.0 + mod_ref[SC_F:SC_F + 1, :]) + mod_ref[SH_F:SH_F + 1, :]).astype(BF16)

    return pl.pallas_call(
        body, name="post_mix_pre_ffn",
        out_shape=[jax.ShapeDtypeStruct((seq, D_MODEL), F32), jax.ShapeDtypeStruct((seq, D_MODEL), BF16)],
        grid=(seq // ROWS,),
        in_specs=[_tile(ROWS, D_MODEL), _tile(ROWS, D_MODEL), _full((6, D_MODEL)), _full((1, D_MODEL)),
                  _full((1, D_MODEL))],
        out_specs=[_tile(ROWS, D_MODEL), _tile(ROWS, D_MODEL)], compiler_params=_cparams(1),
    )(ymix, x, mod6, g2, g3)


def _final(yf, x1, target, mod6, g4):
    seq = x1.shape[0]

    def body(y_ref, x1_ref, t_ref, mod_ref, g_ref, loss_ref, dout_ref, dyf_ref, small_ref):
        i = pl.program_id(0)

        @pl.when(i == 0)
        def _():
            loss_ref[...] = jnp.zeros_like(loss_ref)
            small_ref[...] = jnp.zeros_like(small_ref)

        gt = mod_ref[GT_F:GT_F + 1, :]
        g4v = g_ref[...]
        yn, r = _rms(y_ref[...])
        out = x1_ref[...] + gt * (yn * g4v)
        err = out - t_ref[...]
        loss_ref[...] += jnp.sum(jnp.mean(err * err, axis=-1, keepdims=True))
        dout = err * (1.0 / D_MODEL)
        dout_ref[...] = dout
        small_ref[0:1, :] += _colsum(dout * gt * yn)
        small_ref[1:2, :] += _colsum(dout * (yn * g4v))
        dyf_ref[...] = _rms_bwd(dout * gt * g4v, yn, r).astype(BF16)

    return pl.pallas_call(
        body, name="final_loss",
        out_shape=[jax.ShapeDtypeStruct((1, 128), F32), jax.ShapeDtypeStruct((seq, D_MODEL), F32),
                   jax.ShapeDtypeStruct((seq, D_MODEL), BF16), jax.ShapeDtypeStruct((8, D_MODEL), F32)],
        grid=(seq // ROWS,),
        in_specs=[_tile(ROWS, D_MODEL)] * 3 + [_full((6, D_MODEL)), _full((1, D_MODEL))],
        out_specs=[_full((1, 128)), _tile(ROWS, D_MODEL), _tile(ROWS, D_MODEL), _full((8, D_MODEL))],
        compiler_params=_cparams(1),
    )(yf, x1, target, mod6, g4)


def _mid_bwd(dh2, x1, dout, ymix, mod6, g3, g2):
    seq = x1.shape[0]

    def body(dh_ref, x1_ref, dout_ref, y_ref, mod_ref, g3_ref, g2_ref, dx1_ref, dy_ref, small_ref):
        i = pl.program_id(0)

        @pl.when(i == 0)
        def _():
            small_ref[...] = jnp.zeros_like(small_ref)

        dh = dh_ref[...]
        g3v, g2v = g3_ref[...], g2_ref[...]
        xn, r3 = _rms(x1_ref[...])
        y3 = xn * g3v
        dy3 = dh * (1.0 + mod_ref[SC_F:SC_F + 1, :])
        small_ref[0:1, :] += _colsum(dy3 * xn)
        small_ref[1:2, :] += _colsum(dh * y3)
        small_ref[2:3, :] += _colsum(dh)
        dx1 = dout_ref[...] + _rms_bwd(dy3 * g3v, xn, r3)
        dx1_ref[...] = dx1
        gt = mod_ref[GT_M:GT_M + 1, :]
        yn, r2 = _rms(y_ref[...])
        small_ref[3:4, :] += _colsum(dx1 * gt * yn)
        small_ref[4:5, :] += _colsum(dx1 * (yn * g2v))
        dy_ref[...] = _rms_bwd(dx1 * gt * g2v, yn, r2).astype(BF16)

    return pl.pallas_call(
        body, name="mid_bwd",
        out_shape=[jax.ShapeDtypeStruct((seq, D_MODEL), F32), jax.ShapeDtypeStruct((seq, D_MODEL), BF16),
                   jax.ShapeDtypeStruct((8, D_MODEL), F32)],
        grid=(seq // ROWS,),
        in_specs=[_tile(ROWS, D_MODEL)] * 4 + [_full((6, D_MODEL)), _full((1, D_MODEL)), _full((1, D_MODEL))],
        out_specs=[_tile(ROWS, D_MODEL), _tile(ROWS, D_MODEL), _full((8, D_MODEL))],
        compiler_params=_cparams(1),
    )(dh2, x1, dout, ymix, mod6, g3, g2)


def _pre_mix_bwd(dh1, x, dx1, mod6, g1):
    seq = x.shape[0]

    def body(dh_ref, x_ref, dx1_ref, mod_ref, g_ref, dx_ref, small_ref):
        i = pl.program_id(0)

        @pl.when(i == 0)
        def _():
            small_ref[...] = jnp.zeros_like(small_ref)

        dh = dh_ref[...]
        g1v = g_ref[...]
        xn, r = _rms(x_ref[...])
        dy = dh * (1.0 + mod_ref[SC_M:SC_M + 1, :])
        small_ref[0:1, :] += _colsum(dy * xn)
        small_ref[1:2, :] += _colsum(dh * (xn * g1v))
        small_ref[2:3, :] += _colsum(dh)
        dx_ref[...] = dx1_ref[...] + _rms_bwd(dy * g1v, xn, r)

    return pl.pallas_call(
        body, name="pre_mix_bwd",
        out_shape=[jax.ShapeDtypeStruct((seq, D_MODEL), F32), jax.ShapeDtypeStruct((8, D_MODEL), F32)],
        grid=(seq // ROWS,),
        in_specs=[_tile(ROWS, D_MODEL)] * 3 + [_full((6, D_MODEL)), _full((1, D_MODEL))],
        out_specs=[_tile(ROWS, D_MODEL), _full((8, D_MODEL))], compiler_params=_cparams(1),
    )(dh1, x, dx1, mod6, g1)


def _toeplitz_onehot(shape, offset_axis, top):
    m = lax.broadcasted_iota(jnp.int32, shape, offset_axis)
    i = lax.broadcasted_iota(jnp.int32, shape, 1 - offset_axis)
    return (i == jnp.clip(top - m, -MAX_REL, MAX_REL) + MAX_REL).astype(F32)


def _bias_table(rel_bias):
    width = GROUP_Q + GROUP_K

    def body(rb_ref, o_ref, t_ref):
        t_ref[...] = jnp.dot(rb_ref[...], _toeplitz_onehot((N_REL, width), 1, GROUP_K - 1), precision=HIGHEST,
                             preferred_element_type=F32)
        lane = lax.broadcasted_iota(jnp.int32, (N_HEADS, GROUP_K), 1)
        for r in range(GROUP_Q):
            first_key = (r // CHUNK) * CHUNK
            band = jnp.logical_and(lane >= first_key, lane < first_key + BAND)
            o_ref[r] = jnp.where(band, t_ref[:, GROUP_Q - 1 - r:GROUP_Q - 1 - r + GROUP_K], NEG_INF)

    return pl.pallas_call(
        body, name="bias_table", out_shape=jax.ShapeDtypeStruct((GROUP_Q, N_HEADS, GROUP_K), F32),
        scratch_shapes=[pltpu.VMEM((N_HEADS, width), F32)],
    )(rel_bias)


def _bias_grad(dbias_q):
    def body(d_ref, o_ref, t_ref):
        t_ref[...] = jnp.zeros_like(t_ref)
        for qi in range(CHUNK):
            t_ref[:, CHUNK - 1 - qi:CHUNK - 1 - qi + BAND] += d_ref[qi]
        o_ref[...] = jnp.dot(t_ref[...], _toeplitz_onehot((TOEPLITZ, N_REL), 0, BAND - 1), precision=HIGHEST,
                             preferred_element_type=F32)

    return pl.pallas_call(
        body, name="bias_grad", out_shape=jax.ShapeDtypeStruct((N_HEADS, N_REL), F32),
        scratch_shapes=[pltpu.VMEM((N_HEADS, TOEPLITZ), F32)],
    )(dbias_q)


def _load_resident(pairs, sems):
    copies = [pltpu.make_async_copy(src, dst, sems.at[n]) for n, (src, dst) in enumerate(pairs)]
    for cp in copies:
        cp.start()
    for cp in copies:
        cp.wait()


def _softmax_rows(s_ref, t_ref, valid, rows):
    s = s_ref[rows, :] * (HEAD_DIM ** -0.5) + t_ref[rows, :]
    s = jnp.where(valid, s, NEG_INF)
    e = jnp.exp(s - jnp.max(s, axis=-1, keepdims=True))
    return e / jnp.sum(e, axis=-1, keepdims=True)


def _valid_keys(g):
    kj = lax.broadcasted_iota(jnp.int32, (SOFTMAX_ROWS, GROUP_K), 1)
    return kj >= PAD_ROWS - g * GROUP_Q


def _attn_fwd(qkv, kpad, vpad, table, comm=None):
    seq = qkv.shape[0]

    def body(ins, outs, scratch):
        q_ref, k_hbm, v_hbm, t_hbm = ins
        (o_ref,) = outs
        k_ref, v_ref, t_ref, s_ref, p_ref, sems = scratch
        g = pl.program_id(0)

        @pl.when(g == 0)
        def _():
            _load_resident(((k_hbm, k_ref), (v_hbm, v_ref), (t_hbm, t_ref)), sems)

        window = pl.ds(pl.multiple_of(g * GROUP_Q, GROUP_Q), GROUP_K)
        valid = _valid_keys(g)
        for h in range(N_HEADS):
            cols = slice(h * HEAD_DIM, (h + 1) * HEAD_DIM)
            buf = h % 2
            s_ref[buf] = lax.dot_general(q_ref[:, cols], k_ref[window, cols], _DIMS["nt"],
                                         preferred_element_type=F32)
            for r in range(GROUP_Q // SOFTMAX_ROWS):
                rows = slice(r * SOFTMAX_ROWS, (r + 1) * SOFTMAX_ROWS)
                p_ref[buf, rows, :] = _softmax_rows(s_ref.at[buf], t_ref.at[h], valid, rows).astype(BF16)
            o_ref[:, cols] = jnp.dot(p_ref[buf], v_ref[window, cols], preferred_element_type=F32).astype(BF16)

    (ao,), extra = _host_call(
        body, "attn_fwd", grid=(seq // GROUP_Q,),
        in_specs=[_tile(GROUP_Q, D_ATTN), ANY, ANY, ANY], out_specs=[_tile(GROUP_Q, D_ATTN)],
        out_shape=[jax.ShapeDtypeStruct((seq, D_ATTN), BF16)],
        scratch_shapes=[pltpu.VMEM(kpad.shape, BF16), pltpu.VMEM(vpad.shape, BF16), pltpu.VMEM(table.shape, F32),
                        pltpu.VMEM((2, GROUP_Q, GROUP_K), F32), pltpu.VMEM((2, GROUP_Q, GROUP_K), BF16),
                        pltpu.SemaphoreType.DMA((3,))],
        args=[qkv, kpad, vpad, table], comm=comm)
    return ao, extra


def _attn_bwd(qkv, kpad, vpad, table, dao, comm=None):
    seq = qkv.shape[0]
    n_groups = seq // GROUP_Q
    fold_w = GROUP_K + (GROUP - 1) * CHUNK

    def body(ins, outs, scratch):
        q_ref, do_ref, k_hbm, v_hbm, t_hbm = ins
        dq_ref, dkt_hbm, dvt_hbm, db_ref, cs_ref = outs
        k_ref, v_ref, t_ref, db_acc, dkt_acc, dvt_acc, s_ref, dp_ref, p_ref, ds_ref, sems = scratch
        g = pl.program_id(0)

        @pl.when(g == 0)
        def _():
            _load_resident(((k_hbm, k_ref), (v_hbm, v_ref), (t_hbm, t_ref)), sems)
            db_acc[...] = jnp.zeros_like(db_acc)
            dkt_acc[...] = jnp.zeros_like(dkt_acc)
            dvt_acc[...] = jnp.zeros_like(dvt_acc)
            cs_ref[...] = jnp.zeros_like(cs_ref)

        window = pl.ds(pl.multiple_of(g * GROUP_Q, GROUP_Q), GROUP_K)
        valid = _valid_keys(g)
        for h in range(N_HEADS):
            cols = slice(h * HEAD_DIM, (h + 1) * HEAD_DIM)
            buf = h % 2
            qh, doh = q_ref[:, cols], do_ref[:, cols]
            kh, vh = k_ref[window, cols], v_ref[window, cols]
            s_ref[buf] = lax.dot_general(qh, kh, _DIMS["nt"], preferred_element_type=F32)
            dp_ref[buf] = lax.dot_general(doh, vh, _DIMS["nt"], preferred_element_type=F32)
            for r in range(GROUP_Q // SOFTMAX_ROWS):
                rows = slice(r * SOFTMAX_ROWS, (r + 1) * SOFTMAX_ROWS)
                p = _softmax_rows(s_ref.at[buf], t_ref.at[h], valid, rows)
                dp = dp_ref[buf, rows, :]
                ds = p * (dp - jnp.sum(dp * p, axis=-1, keepdims=True))
                chunk = (r * SOFTMAX_ROWS) // CHUNK
                shift = (GROUP - 1 - chunk) * CHUNK
                local = slice(r * SOFTMAX_ROWS - chunk * CHUNK, (r + 1) * SOFTMAX_ROWS - chunk * CHUNK)
                db_acc[h, local, shift:shift + GROUP_K] += ds
                p_ref[buf, rows, :] = p.astype(BF16)
                ds_ref[buf, rows, :] = (ds * (HEAD_DIM ** -0.5)).astype(BF16)
            dq_ref[:, cols] = jnp.dot(ds_ref[buf], kh, preferred_element_type=F32).astype(BF16)
            dkt_acc[cols, window] += lax.dot_general(qh, ds_ref[buf], _DIMS["tn"], preferred_element_type=F32)
            dvt_acc[cols, window] += lax.dot_general(doh, p_ref[buf], _DIMS["tn"], preferred_element_type=F32)
        cs_ref[0:1, :] += _colsum(dq_ref[...].astype(F32))

        @pl.when(g == n_groups - 1)
        def _():
            lo = (GROUP - 1) * CHUNK
            for h in range(N_HEADS):
                db_ref[h] = db_acc[h, :, lo:lo + BAND]
            inside = pl.ds(PAD_ROWS, seq)
            ones = jnp.ones((8, seq), F32)
            for row, acc in ((1, dkt_acc), (2, dvt_acc)):
                cs_ref[row:row + 1, :] = lax.dot_general(ones, acc[:, inside], _DIMS["nt"], precision=HIGHEST,
                                                         preferred_element_type=F32)[0:1, :]
            out_k = pltpu.make_async_copy(dkt_acc.at[:, inside], dkt_hbm, sems.at[0])
            out_v = pltpu.make_async_copy(dvt_acc.at[:, inside], dvt_hbm, sems.at[1])
            out_k.start()
            out_v.start()
            out_k.wait()
            out_v.wait()

    t_shape = (D_ATTN, seq + PAD_ROWS)
    outs, extra = _host_call(
        body, "attn_bwd", grid=(n_groups,),
        in_specs=[_tile(GROUP_Q, D_ATTN), _tile(GROUP_Q, D_ATTN), ANY, ANY, ANY],
        out_specs=[_tile(GROUP_Q, D_ATTN), ANY, ANY, _full((N_HEADS, CHUNK, BAND)), _full((8, D_ATTN))],
        out_shape=[jax.ShapeDtypeStruct((seq, D_ATTN), BF16), jax.ShapeDtypeStruct((D_ATTN, seq), F32),
                   jax.ShapeDtypeStruct((D_ATTN, seq), F32), jax.ShapeDtypeStruct((N_HEADS, CHUNK, BAND), F32),
                   jax.ShapeDtypeStruct((8, D_ATTN), F32)],
        scratch_shapes=[pltpu.VMEM(kpad.shape, BF16), pltpu.VMEM(vpad.shape, BF16), pltpu.VMEM(table.shape, F32),
                        pltpu.VMEM((N_HEADS, CHUNK, fold_w), F32), pltpu.VMEM(t_shape, F32),
                        pltpu.VMEM(t_shape, F32), pltpu.VMEM((2, GROUP_Q, GROUP_K), F32),
                        pltpu.VMEM((2, GROUP_Q, GROUP_K), F32), pltpu.VMEM((2, GROUP_Q, GROUP_K), BF16),
                        pltpu.VMEM((2, GROUP_Q, GROUP_K), BF16), pltpu.SemaphoreType.DMA((3,))],
        args=[qkv, dao, kpad, vpad, table], comm=comm)
    return outs, extra


CONV_ROWS = 256


def _ln_silu(u1, g, b):
    mu = jnp.mean(u1, axis=-1, keepdims=True)
    xc = u1 - mu
    rs = lax.rsqrt(jnp.mean(xc * xc, axis=-1, keepdims=True) + EPS)
    xhat = xc * rs
    u2 = xhat * g + b
    return xhat, rs, u2


def _glu_into(s_ref, a_ref, b_ref, ah_ref, bh_ref, first):
    halo = ah_ref[...] * _sig(bh_ref[...])
    s_ref[0:CONV_HALO, :] = jnp.where(first, 0.0, halo)
    s_ref[CONV_HALO:CONV_HALO + CONV_ROWS, :] = a_ref[...] * _sig(b_ref[...])


def _conv_fwd(zr, w_dw, b_dw, g_ln, b_ln):
    seq = zr.shape[0]

    def body(a_ref, b_ref, ah_ref, bh_ref, w_ref, bias_ref, g_ref, bl_ref, u1_ref, u3_ref, s_ref):
        _glu_into(s_ref, a_ref, b_ref, ah_ref, bh_ref, pl.program_id(0) == 0)
        acc = jnp.zeros((CONV_ROWS, D_CONV), F32) + bias_ref[...]
        for j in range(CONV_K):
            acc = acc + w_ref[j:j + 1, :] * s_ref[2 + j:2 + j + CONV_ROWS, :]
        u1_ref[...] = acc
        _, _, u2 = _ln_silu(acc, g_ref[...], bl_ref[...])
        u3_ref[...] = (u2 * _sig(u2)).astype(BF16)

    return pl.pallas_call(
        body, name="conv_fwd",
        out_shape=[jax.ShapeDtypeStruct((seq, D_CONV), F32), jax.ShapeDtypeStruct((seq, D_CONV), BF16)],
        grid=(seq // CONV_ROWS,),
        in_specs=[_tile(CONV_ROWS, D_CONV, 0), _tile(CONV_ROWS, D_CONV, 1),
                  _prev(CONV_HALO, D_CONV, CONV_ROWS, 0), _prev(CONV_HALO, D_CONV, CONV_ROWS, 1),
                  _full((CONV_K, D_CONV)), _full((1, D_CONV)), _full((1, D_CONV)), _full((1, D_CONV))],
        out_specs=[_tile(CONV_ROWS, D_CONV), _tile(CONV_ROWS, D_CONV)],
        scratch_shapes=[pltpu.VMEM((CONV_HALO + CONV_ROWS, D_CONV), F32)],
        compiler_params=_cparams(1),
    )(zr, zr, zr, zr, w_dw, b_dw, g_ln, b_ln)


def _conv_bwd(zr, u1, du3, w_dw, g_ln, b_ln):
    seq = zr.shape[0]
    n_tiles = seq // CONV_ROWS
    n_halo = seq // CONV_HALO
    ext = CONV_ROWS + CONV_HALO

    def body(a_ref, b_ref, ah_ref, bh_ref, u1_ref, u1n_ref, d3_ref, d3n_ref, w_ref, g_ref, bl_ref,
             da_ref, db_ref, dw_ref, small_ref, s_ref, d_ref):
        i = pl.program_id(0)

        @pl.when(i == 0)
        def _():
            dw_ref[...] = jnp.zeros_like(dw_ref)
            small_ref[...] = jnp.zeros_like(small_ref)

        _glu_into(s_ref, a_ref, b_ref, ah_ref, bh_ref, i == 0)
        gv, bv = g_ref[...], bl_ref[...]

        def du1_of(u1, d3):
            xhat, rs, u2 = _ln_silu(u1, gv, bv)
            sg = _sig(u2)
            du2 = d3 * (sg * (1.0 + u2 * (1.0 - sg)))
            dxh = du2 * gv
            du1 = rs * (dxh - jnp.mean(dxh, axis=-1, keepdims=True)
                        - xhat * jnp.mean(dxh * xhat, axis=-1, keepdims=True))
            return du1, du2, xhat

        du1, du2, xhat = du1_of(u1_ref[...], d3_ref[...])
        du1n, _, _ = du1_of(u1n_ref[...], d3n_ref[...])
        d_ref[0:CONV_ROWS, :] = du1
        d_ref[CONV_ROWS:ext, :] = jnp.where(i == n_tiles - 1, 0.0, du1n)
        small_ref[0:1, :] += _colsum(du1)
        small_ref[1:2, :] += _colsum(du2 * xhat)
        small_ref[2:3, :] += _colsum(du2)
        du0 = jnp.zeros((CONV_ROWS, D_CONV), F32)
        for j in range(CONV_K):
            dw_ref[j:j + 1, :] += _colsum(du1 * s_ref[2 + j:2 + j + CONV_ROWS, :])
            du0 = du0 + w_ref[j:j + 1, :] * d_ref[CONV_K - 1 - j:CONV_K - 1 - j + CONV_ROWS, :]
        sb = _sig(b_ref[...])
        da = du0 * sb
        dbv = du0 * a_ref[...] * sb * (1.0 - sb)
        da_ref[...] = da.astype(BF16)
        db_ref[...] = dbv.astype(BF16)
        small_ref[3:4, :] += _colsum(da)
        small_ref[4:5, :] += _colsum(dbv)

    return pl.pallas_call(
        body, name="conv_bwd",
        out_shape=[jax.ShapeDtypeStruct((seq, D_CONV), BF16), jax.ShapeDtypeStruct((seq, D_CONV), BF16),
                   jax.ShapeDtypeStruct((CONV_HALO, D_CONV), F32), jax.ShapeDtypeStruct((8, D_CONV), F32)],
        grid=(n_tiles,),
        in_specs=[_tile(CONV_ROWS, D_CONV, 0), _tile(CONV_ROWS, D_CONV, 1),
                  _prev(CONV_HALO, D_CONV, CONV_ROWS, 0), _prev(CONV_HALO, D_CONV, CONV_ROWS, 1),
                  _tile(CONV_ROWS, D_CONV), _next(CONV_HALO, D_CONV, CONV_ROWS, n_halo),
                  _tile(CONV_ROWS, D_CONV), _next(CONV_HALO, D_CONV, CONV_ROWS, n_halo),
                  _full((CONV_K, D_CONV)), _full((1, D_CONV)), _full((1, D_CONV))],
        out_specs=[_tile(CONV_ROWS, D_CONV), _tile(CONV_ROWS, D_CONV), _full((CONV_HALO, D_CONV)),
                   _full((8, D_CONV))],
        scratch_shapes=[pltpu.VMEM((ext, D_CONV), F32), pltpu.VMEM((ext, D_CONV), F32)],
        compiler_params=_cparams(1),
    )(zr, zr, zr, zr, u1, u1, du3, du3, w_dw, g_ln, b_ln)


MERGE_ROWS = 256


def _merge_fwd(ao, u3, zr, w_ao, w_co, b_co):
    seq = ao.shape[0]

    def body(ao_ref, u3_ref, ga_ref, gb_ref, wa_ref, wc_ref, bc_ref, y_ref, a_ref, cb_ref):
        a = jnp.dot(ao_ref[...], wa_ref[...], preferred_element_type=F32)
        cb = jnp.dot(u3_ref[...], wc_ref[...], preferred_element_type=F32) + bc_ref[...]
        a_ref[...] = a
        cb_ref[...] = cb
        y_ref[...] = (_sig(ga_ref[...]) * a + _sig(gb_ref[...]) * cb).astype(BF16)

    f32_out = jax.ShapeDtypeStruct((seq, D_MODEL), F32)
    return pl.pallas_call(
        body, name="merge_fwd",
        out_shape=[jax.ShapeDtypeStruct((seq, D_MODEL), BF16), f32_out, f32_out],
        grid=(seq // MERGE_ROWS,),
        in_specs=[_tile(MERGE_ROWS, D_ATTN), _tile(MERGE_ROWS, D_CONV), _tile(MERGE_ROWS, D_MODEL, 1),
                  _tile(MERGE_ROWS, D_MODEL, 2), _full(w_ao.shape), _full(w_co.shape), _full((1, D_MODEL))],
        out_specs=[_tile(MERGE_ROWS, D_MODEL)] * 3, compiler_params=_cparams(1),
    )(ao, u3, zr, zr, w_ao, w_co, b_co)


def _merge_bwd(dy, a, cb, zr):
    seq = dy.shape[0]

    def body(dy_ref, a_ref, cb_ref, ga_ref, gb_ref, da_ref, dcb_ref, dga_ref, dgb_ref, small_ref):
        i = pl.program_id(0)

        @pl.when(i == 0)
        def _():
            small_ref[...] = jnp.zeros_like(small_ref)

        dy_v = dy_ref[...]
        sa, sb = _sig(ga_ref[...]), _sig(gb_ref[...])
        dcb = dy_v * sb
        dga = dy_v * a_ref[...] * sa * (1.0 - sa)
        dgb = dy_v * cb_ref[...] * sb * (1.0 - sb)
        da_ref[...] = (dy_v * sa).astype(BF16)
        dcb_ref[...] = dcb.astype(BF16)
        dga_ref[...] = dga.astype(BF16)
        dgb_ref[...] = dgb.astype(BF16)
        small_ref[0:1, :] += _colsum(dga)
        small_ref[1:2, :] += _colsum(dgb)
        small_ref[2:3, :] += _colsum(dcb)

    bf = jax.ShapeDtypeStruct((seq, D_MODEL), BF16)
    return pl.pallas_call(
        body, name="merge_bwd", out_shape=[bf, bf, bf, bf, jax.ShapeDtypeStruct((8, D_MODEL), F32)],
        grid=(seq // MERGE_ROWS,),
        in_specs=[_tile(MERGE_ROWS, D_MODEL)] * 3 + [_tile(MERGE_ROWS, D_MODEL, 1), _tile(MERGE_ROWS, D_MODEL, 2)],
        out_specs=[_tile(MERGE_ROWS, D_MODEL)] * 4 + [_full((8, D_MODEL))], compiler_params=_cparams(1),
    )(dy, a, cb, zr, zr)


FFN_ROWS = 512
FFN_BLOCKS = D_FF // FFN_COLS
GELU_C = math.sqrt(2.0 / math.pi)


def _gelu(v):
    t = jnp.tanh(GELU_C * (v + 0.044715 * (v * v * v)))
    return 0.5 * v * (1.0 + t), t


def _gelu_grad(v, t):
    return 0.5 * (1.0 + t) + 0.5 * v * (1.0 - t * t) * (GELU_C * (1.0 + 3.0 * 0.044715 * (v * v)))


def _sublane_rows(ref, n):
    return [jnp.broadcast_to(ref[r:r + 1, :], (8, FFN_COLS)) for r in range(n)]


def _rolls(tile, shifts):
    return tuple(pltpu.roll(tile, s, 0) for s in shifts)


def _behind(prev_rolls, cur, row_id):
    rolls = _rolls(cur, (1, 2))
    x1 = jnp.where(row_id < 1, prev_rolls[0], rolls[0])
    x2 = jnp.where(row_id < 2, prev_rolls[1], rolls[1])
    return (x2, x1, cur), rolls


def _ahead(cur_rolls, next_rolls, row_id):
    return (jnp.where(row_id < 7, cur_rolls[0], next_rolls[0]), jnp.where(row_id < 6, cur_rolls[1], next_rolls[1]))


def _conv3(taps, w, bias):
    return w[0] * taps[0] + w[1] * taps[1] + w[2] * taps[2] + bias


def _ffn_specs(rows):
    tile = lambda off: pl.BlockSpec((rows, FFN_COLS), lambda j, i: (i, j + off))
    prev = lambda off: pl.BlockSpec((FFN_HALO, FFN_COLS),
                                    lambda j, i: (jnp.maximum(i * (rows // FFN_HALO) - 1, 0), j + off))
    wgt = lambda off: pl.BlockSpec((3, FFN_COLS), lambda j, i: (0, j + off))
    vec = lambda off: pl.BlockSpec((1, FFN_COLS), lambda j, i: (0, j + off))
    return tile, prev, wgt, vec


def _ffn_act(up, w_dw, b_dw):
    seq = up.shape[0]
    tile, prev, wgt, vec = _ffn_specs(FFN_ROWS)

    def body(v_ref, g_ref, vp_ref, gp_ref, wv_ref, wg_ref, bv_ref, bg_ref, act_ref):
        first = pl.program_id(1) == 0
        row_id = lax.broadcasted_iota(jnp.int32, (8, FFN_COLS), 0)
        wv, wg = _sublane_rows(wv_ref, 3), _sublane_rows(wg_ref, 3)
        (bv,), (bg,) = _sublane_rows(bv_ref, 1), _sublane_rows(bg_ref, 1)
        rolls_v = _rolls(jnp.where(first, 0.0, vp_ref[...]), (1, 2))
        rolls_g = _rolls(jnp.where(first, 0.0, gp_ref[...]), (1, 2))
        for row in range(0, FFN_ROWS, 16):
            halves = []
            for r in (row, row + 8):
                taps_v, rolls_v = _behind(rolls_v, v_ref[r:r + 8, :], row_id)
                taps_g, rolls_g = _behind(rolls_g, g_ref[r:r + 8, :], row_id)
                halves.append(_gelu(_conv3(taps_g, wg, bg))[0] * _conv3(taps_v, wv, bv))
            act_ref[row:row + 16, :] = jnp.concatenate(halves, axis=0).astype(BF16)

    return pl.pallas_call(
        body, name="ffn_act", out_shape=jax.ShapeDtypeStruct((seq, D_FF), BF16),
        grid=(FFN_BLOCKS, seq // FFN_ROWS),
        in_specs=[tile(0), tile(FFN_BLOCKS), prev(0), prev(FFN_BLOCKS), wgt(0), wgt(FFN_BLOCKS),
                  vec(0), vec(FFN_BLOCKS)],
        out_specs=tile(0), compiler_params=_cparams(2),
    )(up, up, up, up, w_dw, w_dw, b_dw, b_dw)


def _ffn_act_bwd(up, dact, w_dw, b_dw, comm=None):
    seq = up.shape[0]
    n_tiles = seq // FFN_ROWS
    n_halo = seq // FFN_HALO
    tile, prev, wgt, vec = _ffn_specs(FFN_ROWS)
    nxt = lambda off: pl.BlockSpec(
        (FFN_HALO, FFN_COLS), lambda j, i: (jnp.minimum((i + 1) * (FFN_ROWS // FFN_HALO), n_halo - 1), j + off))
    acc = lambda off: pl.BlockSpec((8, FFN_COLS), lambda j, i: (0, j + off))

    def body(v_ref, g_ref, vp_ref, gp_ref, vn_ref, gn_ref, da_ref, dan_ref, wv_ref, wg_ref, bv_ref, bg_ref,
             dv_out, dg_out, dwv_ref, dwg_ref, dbv_ref, dbg_ref):
        i = pl.program_id(1)
        first, last = i == 0, i == n_tiles - 1

        @pl.when(first)
        def _():
            for r in (dwv_ref, dwg_ref, dbv_ref, dbg_ref):
                r[...] = jnp.zeros_like(r)

        row_id = lax.broadcasted_iota(jnp.int32, (8, FFN_COLS), 0)
        wv, wg = _sublane_rows(wv_ref, 3), _sublane_rows(wg_ref, 3)
        (bv,), (bg,) = _sublane_rows(bv_ref, 1), _sublane_rows(bg_ref, 1)
        zero = jnp.zeros((8, FFN_COLS), F32)
        sums_v, sums_g = [zero] * 4, [zero] * 4
        rolls_v = _rolls(jnp.where(first, 0.0, vp_ref[...]), (1, 2))
        rolls_g = _rolls(jnp.where(first, 0.0, gp_ref[...]), (1, 2))
        behind = None
        done_v, done_g = [], []

        def grads(v_tile, g_tile, dact, rolls_v, rolls_g):
            taps_v, rolls_v = _behind(rolls_v, v_tile, row_id)
            taps_g, rolls_g = _behind(rolls_g, g_tile, row_id)
            val, gate = _conv3(taps_v, wv, bv), _conv3(taps_g, wg, bg)
            gel, t = _gelu(gate)
            return dact * gel, dact * val * _gelu_grad(gate, t), taps_v, taps_g, rolls_v, rolls_g

        def finish(tile, nxt, row):
            for (d, d_rolls), (_, n_rolls), w, done, o_ref in ((tile[0], nxt[0], wv, done_v, dv_out),
                                                               (tile[1], nxt[1], wg, done_g, dg_out)):
                d1, d2 = _ahead(d_rolls, n_rolls, row_id)
                done.append(w[2] * d + w[1] * d1 + w[0] * d2)
                if len(done) == 2:
                    o_ref[row - 16:row, :] = jnp.concatenate(done, axis=0).astype(BF16)
                    done.clear()

        for row in range(0, FFN_ROWS, 16):
            dact16 = da_ref[row:row + 16, :].astype(F32)
            for r, dact in ((row, dact16[0:8, :]), (row + 8, dact16[8:16, :])):
                dval, dgate, taps_v, taps_g, rolls_v, rolls_g = grads(v_ref[r:r + 8, :], g_ref[r:r + 8, :], dact,
                                                                      rolls_v, rolls_g)
                sums_v = [s + dval * x for s, x in zip(sums_v, taps_v)] + [sums_v[3] + dval]
                sums_g = [s + dgate * x for s, x in zip(sums_g, taps_g)] + [sums_g[3] + dgate]
                tile = ((dval, _rolls(dval, (7, 6))), (dgate, _rolls(dgate, (7, 6))))
                if behind is not None:
                    finish(behind, tile, r)
                behind = tile
        dact_next = jnp.where(last, 0.0, dan_ref[...].astype(F32)[0:FFN_HALO, :])
        dval, dgate, *_ = grads(vn_ref[...], gn_ref[...], dact_next, rolls_v, rolls_g)
        finish(behind, ((dval, _rolls(dval, (7, 6))), (dgate, _rolls(dgate, (7, 6)))), FFN_ROWS)
        for sums, dw_ref, db_ref in ((sums_v, dwv_ref, dbv_ref), (sums_g, dwg_ref, dbg_ref)):
            for tap in range(3):
                dw_ref[tap:tap + 1, :] += _colsum(sums[tap])
            db_ref[0:1, :] += _colsum(sums[3])

    half = jax.ShapeDtypeStruct((seq, D_FF), BF16)
    acc_shape = jax.ShapeDtypeStruct((8, D_FF), F32)
    return _host_call(
        lambda ins, outs, scratch: body(*ins, *outs, *scratch), "ffn_act_bwd", grid=(FFN_BLOCKS, n_tiles),
        in_specs=[tile(0), tile(FFN_BLOCKS), prev(0), prev(FFN_BLOCKS), nxt(0), nxt(FFN_BLOCKS),
                  tile(0), pl.BlockSpec((16, FFN_COLS), lambda j, i: (
                      jnp.minimum((i + 1) * (FFN_ROWS // 16), seq // 16 - 1), j)),
                  wgt(0), wgt(FFN_BLOCKS), vec(0), vec(FFN_BLOCKS)],
        out_specs=[tile(0), tile(0), acc(0), acc(0), acc(0), acc(0)],
        out_shape=[half, half, acc_shape, acc_shape, acc_shape, acc_shape],
        scratch_shapes=[], args=[up, up, up, up, up, up, dact, dact, w_dw, w_dw, b_dw, b_dw], comm=comm)


def _cols_to_blocks(full_cols):
    k, n8 = full_cols.shape
    return jnp.transpose(full_cols.reshape(k, N_DEV, n8 // N_DEV), (1, 0, 2))


def _rows_to_blocks(full_rows):
    r8, n = full_rows.shape
    return full_rows.reshape(N_DEV, r8 // N_DEV, n)


def _blocks_to_cols(gathered):
    _, k, n = gathered.shape
    return jnp.transpose(gathered, (1, 0, 2)).reshape(k, N_DEV * n)


def kernel(x, c, w_ada, b_ada, g_pre_mix, g_post_mix, w_in, b_in, rel_bias, w_attn_o, w_dw_conv, b_dw_conv, g_conv_ln, b_conv_ln, w_conv_o, b_conv_o, w_mix_o, g_pre_ffn, g_post_ffn, w_up, w_dw_ffn, b_dw_ffn, w_down, loss_target, m_w_ada, m_b_ada, m_g_pre_mix, m_g_post_mix, m_w_in, m_b_in, m_rel_bias, m_w_attn_o, m_w_dw_conv, m_b_dw_conv, m_g_conv_ln, m_b_conv_ln, m_w_conv_o, m_b_conv_o, m_w_mix_o, m_g_pre_ffn, m_g_post_ffn, m_w_up, m_w_dw_ffn, m_b_dw_ffn, m_w_down, v_w_ada, v_b_ada, v_g_pre_mix, v_g_post_mix, v_w_in, v_b_in, v_rel_bias, v_w_attn_o, v_w_dw_conv, v_b_dw_conv, v_g_conv_ln, v_b_conv_ln, v_w_conv_o, v_b_conv_o, v_w_mix_o, v_g_pre_ffn, v_g_post_ffn, v_w_up, v_w_dw_ffn, v_b_dw_ffn, v_w_down):
    names = ["w_ada", "b_ada", "g_pre_mix", "g_post_mix", "w_in", "b_in", "rel_bias", "w_attn_o", "w_dw_conv",
             "b_dw_conv", "g_conv_ln", "b_conv_ln", "w_conv_o", "b_conv_o", "w_mix_o", "g_pre_ffn", "g_post_ffn",
             "w_up", "w_dw_ffn", "b_dw_ffn", "w_down"]
    weights = dict(zip(names, [w_ada, b_ada, g_pre_mix, g_post_mix, w_in, b_in, rel_bias, w_attn_o, w_dw_conv,
                               b_dw_conv, g_conv_ln, b_conv_ln, w_conv_o, b_conv_o, w_mix_o, g_pre_ffn,
                               g_post_ffn, w_up, w_dw_ffn, b_dw_ffn, w_down]))
    mom_m = dict(zip(names, [m_w_ada, m_b_ada, m_g_pre_mix, m_g_post_mix, m_w_in, m_b_in, m_rel_bias, m_w_attn_o,
                             m_w_dw_conv, m_b_dw_conv, m_g_conv_ln, m_b_conv_ln, m_w_conv_o, m_b_conv_o,
                             m_w_mix_o, m_g_pre_ffn, m_g_post_ffn, m_w_up, m_w_dw_ffn, m_b_dw_ffn, m_w_down]))
    mom_v = dict(zip(names, [v_w_ada, v_b_ada, v_g_pre_mix, v_g_post_mix, v_w_in, v_b_in, v_rel_bias, v_w_attn_o,
                             v_w_dw_conv, v_b_dw_conv, v_g_conv_ln, v_b_conv_ln, v_w_conv_o, v_b_conv_o,
                             v_w_mix_o, v_g_pre_ffn, v_g_post_ffn, v_w_up, v_w_dw_ffn, v_b_dw_ffn, v_w_down]))
    shapes = {n: w.shape for n, w in weights.items()}

    seq = x.shape[1]
    me = 4 * lax.axis_index("x") + 2 * lax.axis_index("y") + lax.axis_index("c")
    x2 = x.reshape(seq, D_MODEL)
    target = loss_target.reshape(seq, D_MODEL)
    sq = lambda a: a.reshape(a.shape[1:])
    bf = lambda a: sq(a).astype(BF16)

    c_act = _silu_vec(c)
    c_all, g_in, g_dwc, g_dwf = _run_comm(
        _gather_comm([c_act, bf(w_in), sq(w_dw_conv), sq(w_dw_ffn)]), "gather_first")
    c_all = c_all.reshape(N_DEV, D_MODEL)
    wf_in = _blocks_to_cols(g_in)
    wf_dwc = _blocks_to_cols(g_dwc)
    wf_dwf = _blocks_to_cols(g_dwf)

    (mod_all,) = _run_comm(_gather_comm([_ada_fwd(c_all, sq(w_ada))]), "gather_mod")
    mod = lax.dynamic_index_in_dim(mod_all, me, axis=1, keepdims=False)
    mod6 = (mod.reshape(1, 6 * D_MODEL) + b_ada).reshape(6, D_MODEL)

    h1 = _pre_mix(x2, mod6, g_pre_mix)
    qkv = _mm(h1, wf_in[:, :3 * D_ATTN], "nn", BF16, "in_proj_qkv", bias=b_in[:, :3 * D_ATTN], tm=1024, tn=768)
    zr = _mm(h1, wf_in[:, 3 * D_ATTN:], "nn", F32, "in_proj_rest", bias=b_in[:, 3 * D_ATTN:], tm=1024, tn=1024)
    kpad = jnp.pad(qkv[:, D_ATTN:2 * D_ATTN], ((PAD_ROWS, 0), (0, 0)))
    vpad = jnp.pad(qkv[:, 2 * D_ATTN:], ((PAD_ROWS, 0), (0, 0)))
    table = jnp.transpose(_bias_table(sq(rel_bias)), (1, 0, 2))
    ao, (g_ao, g_co, g_mo, g_up, g_dn) = _attn_fwd(
        qkv, kpad, vpad, table,
        comm=_gather_comm([bf(w_attn_o), bf(w_conv_o), bf(w_mix_o), bf(w_up), bf(w_down)]))
    wf_ao = _blocks_to_cols(g_ao)
    wf_co = _blocks_to_cols(g_co)
    wf_mo = g_mo.reshape(D_MODEL, D_MODEL)
    wf_up = _blocks_to_cols(g_up)
    wf_dn = g_dn.reshape(D_FF, D_MODEL)
    u1, u3 = _conv_fwd(zr, wf_dwc, b_dw_conv, g_conv_ln, b_conv_ln)
    y, a_br, cb_br = _merge_fwd(ao, u3, zr, wf_ao, wf_co, b_conv_o)
    ymix = _mm(y, wf_mo, "nn", F32, "mix_o", tm=1024, tn=1024)
    x1, h2 = _post_mix_pre_ffn(ymix, x2, mod6, g_post_mix, g_pre_ffn)
    up = _mm(h2, wf_up, "nn", F32, "ffn_up", tm=1024, tn=1408)
    act = _ffn_act(up, wf_dwf, b_dw_ffn)
    yf = _mm(act, wf_dn, "nn", F32, "ffn_down", tm=512, tn=1024)
    loss_lanes, dout, dyf, small_f = _final(yf, x1, target, mod6, g_post_ffn)
    loss = lax.psum(0.5 * loss_lanes[0, 0], ("x", "y", "c"))

    dact = _mm(dyf, wf_dn, "nt", BF16, "ffn_down_dx", tm=1024, tn=1408)
    gw_down = _mm(act, dyf, "tn", BF16, "ffn_down_dw", tm=256, tn=1024)
    (dup_v, dup_g, dwv, dwg, dbv, dbg), (parts_down,) = _ffn_act_bwd(
        up, dact, wf_dwf, b_dw_ffn, comm=_scatter_comm([_rows_to_blocks(gw_down)]))
    dup = jnp.concatenate([dup_v, dup_g], axis=1)
    dh2 = _mm(dup, wf_up, "nt", F32, "ffn_up_dx")
    gw_up = _mm(h2, dup, "tn", BF16, "ffn_up_dw")
    dx1, dymix, small_m = _mid_bwd(dh2, x1, dout, ymix, mod6, g_pre_ffn, g_post_mix)
    dy = _mm(dymix, wf_mo, "nt", F32, "mix_o_dx", tm=1024, tn=1024)
    gw_mo = _mm(y, dymix, "tn", BF16, "mix_o_dw")
    da, dcb, dga, dgb, small_g = _merge_bwd(dy, a_br, cb_br, zr)
    dao = _mm(da, wf_ao, "nt", BF16, "attn_o_dx", tm=1024)
    gw_ao = _mm(ao, da, "tn", BF16, "attn_o_dw")
    du3 = _mm(dcb, wf_co, "nt", F32, "conv_o_dx", tm=1024)
    gw_co = _mm(u3, dcb, "tn", BF16, "conv_o_dw")
    (dq, dkt, dvt, dbias, small_a), (parts_up, parts_mo, parts_ao, parts_co) = _attn_bwd(
        qkv, kpad, vpad, table, dao,
        comm=_scatter_comm([_cols_to_blocks(gw_up), _rows_to_blocks(gw_mo), _cols_to_blocks(gw_ao),
                            _cols_to_blocks(gw_co)]))
    g_rel = _bias_grad(jnp.transpose(dbias, (1, 0, 2)))
    dglu_a, dglu_b, dw_conv, small_c = _conv_bwd(zr, u1, du3, wf_dwc, g_conv_ln, b_conv_ln)
    dz = jnp.concatenate([dq, dkt.T.astype(BF16), dvt.T.astype(BF16), dglu_a, dglu_b, dga, dgb], axis=1)
    gw_in = _mm(h1, dz, "tn", BF16, "in_proj_dw")
    dh1, (parts_in,) = _mm(dz, wf_in, "nt", F32, "in_proj_dx", comm=_scatter_comm([_cols_to_blocks(gw_in)]))
    grad_x, small_x = _pre_mix_bwd(dh1, x2, dx1, mod6, g_pre_mix)

    packed = _pack_grads(small_x, small_m, small_f, small_g, small_a, small_c, dbv, dbg, dwv, dwg, dw_conv)
    gathered, gathered_rel = _run_comm(_gather_comm([packed, g_rel]), "gather_small")
    gathered = gathered.reshape(N_DEV, PACKED_TOTAL)
    updates, g_dwc_full, g_dwf_full = _small_adamw(gathered, gathered_rel, weights, mom_m, mom_v)

    grads, deltas, new_m, new_v = {}, {}, {}, {}

    def record(name, update):
        for dst, val in zip((grads, deltas, new_m, new_v), update):
            dst[name] = val.reshape(shapes[name])

    for name, update in updates.items():
        record(name, update)

    def local_update(name, grad):
        record(name, _adamw(sq(weights[name]), sq(mom_m[name]), sq(mom_v[name]), "adamw_" + name, g=grad))

    conv_cols, ffn_cols, ada_cols = D_CONV // N_DEV, 2 * D_FF // N_DEV, 6 * D_MODEL // N_DEV
    local_update("w_dw_conv", lax.dynamic_slice(g_dwc_full, (0, me * conv_cols), (CONV_K, conv_cols)))
    local_update("w_dw_ffn", lax.dynamic_slice(g_dwf_full, (0, me * ffn_cols), (3, ffn_cols)))
    local_update("w_ada", _ada_grad(c_all, lax.dynamic_slice(gathered, (0, me * ada_cols), (N_DEV, ada_cols))))

    for name, part in (("w_in", parts_in), ("w_attn_o", parts_ao), ("w_conv_o", parts_co), ("w_mix_o", parts_mo),
                       ("w_up", parts_up), ("w_down", parts_down)):
        record(name, _adamw(sq(weights[name]), sq(mom_m[name]), sq(mom_v[name]), "adamw_" + name, parts=part))

    return (loss, grad_x.reshape(x.shape), *[grads[n] for n in names], *[deltas[n] for n in names],
            *[new_m[n] for n in names], *[new_v[n] for n in names])
```

```python
import functools
import math

import jax
import jax.numpy as jnp
from jax import lax
from jax.experimental import pallas as pl
from jax.experimental.pallas import tpu as pltpu

F32 = jnp.float32
BF16 = jnp.bfloat16
HIGHEST = lax.Precision.HIGHEST

D_MODEL = 1024
CHUNK = 64
LEFT_CHUNKS = 8
BAND = (LEFT_CHUNKS + 1) * CHUNK
PAD_ROWS = LEFT_CHUNKS * CHUNK
GROUP = 4
GROUP_Q = GROUP * CHUNK
GROUP_K = GROUP_Q + PAD_ROWS
SOFTMAX_ROWS = 16
TOEPLITZ = 640
N_HEADS = 8
HEAD_DIM = 64
D_ATTN = 512
D_CONV = 512
CONV_K = 31
CONV_HALO = 32
MAX_REL = 128
N_REL = 2 * MAX_REL + 1
D_FF = 2816
FFN_HALO = 8
FFN_COLS = 256
EPS = 1e-6
NEG_INF = -1e30
N_DEV = 8

ADAM_LR = 0.001
ADAM_B1 = 0.9
ADAM_B2 = 0.999
ADAM_EPS = 1e-08
ADAM_WD = 0.01
ADAM_STEP = 10

VMEM_LIMIT_BYTES = 56 * 1024 * 1024
ADAMW_BLOCK_BYTES = 768 * 1024

MESH = pl.DeviceIdType.MESH
ANY = pl.BlockSpec(memory_space=pl.ANY)

SH_M, SC_M, GT_M, SH_F, SC_F, GT_F = range(6)

SMALL = (("b_ada", 6144), ("g_pre_mix", 1024), ("g_post_mix", 1024), ("b_in", 4608), ("b_dw_conv", 512),
         ("g_conv_ln", 512), ("b_conv_ln", 512), ("b_conv_o", 1024), ("g_pre_ffn", 1024), ("g_post_ffn", 1024),
         ("b_dw_ffn", 5632))
PACKED_TOTAL = sum(n for _, n in SMALL) + CONV_K * D_CONV + 3 * 2 * D_FF


def _cparams(n_axes):
    return pltpu.CompilerParams(vmem_limit_bytes=VMEM_LIMIT_BYTES,
                                dimension_semantics=("arbitrary",) * n_axes)


def _sig(v):
    return 1.0 / (1.0 + jnp.exp(-v))


def _pick(n, target):
    if n <= target:
        return n
    t = target - target % 128
    while n % t:
        t -= 128
    return t


def _tile(rows, cols, col=0):
    return pl.BlockSpec((rows, cols), lambda i: (i, col))


def _full(shape):
    zeros = (0,) * len(shape)
    return pl.BlockSpec(shape, lambda i: zeros)


def _prev(halo, cols, rows, col=0):
    return pl.BlockSpec((halo, cols), lambda i: (jnp.maximum(i * (rows // halo) - 1, 0), col))


def _next(halo, cols, rows, n_blocks, col=0):
    return pl.BlockSpec((halo, cols), lambda i: (jnp.minimum((i + 1) * (rows // halo), n_blocks - 1), col))


class _Comm:
    def __init__(self, inputs, out_shapes, sems, start, finish):
        self.inputs, self.out_shapes, self.sems, self.start, self.finish = inputs, out_shapes, sems, start, finish


def _host_call(body, name, grid, in_specs, out_specs, out_shape, scratch_shapes, args, comm=None):
    n_in, n_out, n_scr = len(args), len(out_shape), len(scratch_shapes)
    c_in = list(comm.inputs) if comm else []
    c_out = list(comm.out_shapes) if comm else []
    c_sem = list(comm.sems) if comm else []

    def full(*refs):
        bounds = [0, n_in, len(c_in), n_out, len(c_out), n_scr, len(c_sem)]
        cuts = [sum(bounds[:i + 1]) for i in range(len(bounds))]
        ins, cins, outs, couts, scr, csems = (refs[lo:hi] for lo, hi in zip(cuts[:-1], cuts[1:]))
        if comm:
            first = functools.reduce(jnp.logical_and, [pl.program_id(ax) == 0 for ax in range(len(grid))])
            pl.when(first)(lambda: comm.start(cins, couts, csems))
        body(ins, outs, scr)
        if comm:
            last = functools.reduce(jnp.logical_and, [pl.program_id(ax) == grid[ax] - 1 for ax in range(len(grid))])
            pl.when(last)(lambda: comm.finish(cins, couts, csems))

    res = pl.pallas_call(
        full, name=name, grid=grid, in_specs=list(in_specs) + [ANY] * len(c_in),
        out_specs=list(out_specs) + [ANY] * len(c_out), out_shape=list(out_shape) + c_out,
        scratch_shapes=list(scratch_shapes) + c_sem, compiler_params=_cparams(len(grid)),
    )(*args, *c_in)
    return list(res[:n_out]), list(res[n_out:])


def _run_comm(comm, name):
    n_in, n_out = len(comm.inputs), len(comm.out_shapes)

    def body(*refs):
        ins, outs, sems = refs[:n_in], refs[n_in:n_in + n_out], refs[n_in + n_out:]
        comm.start(ins, outs, sems)
        comm.finish(ins, outs, sems)

    return pl.pallas_call(
        body, name=name, out_shape=list(comm.out_shapes), in_specs=[ANY] * n_in, out_specs=[ANY] * n_out,
        scratch_shapes=list(comm.sems),
    )(*comm.inputs)


def _place():
    return lax.axis_index("x"), lax.axis_index("y"), lax.axis_index("c")


def _gather_comm(arrs):
    n = len(arrs)

    def plan(ins, outs, sems):
        send_sems, recv_sems, local_sems = sems
        x, y, c = _place()
        me, sibling = (x, y, c), (x, y, 1 - c)
        chips = [(1 - x, y), (x, 1 - y), (1 - x, 1 - y)]

        def block(k, p):
            return outs[k].at[4 * p[0] + 2 * p[1] + p[2]]

        def copy(k, s, blk, to, src=None):
            return pltpu.make_async_remote_copy(
                src_ref=block(k, blk) if src is None else src, dst_ref=block(k, blk),
                send_sem=send_sems.at[7 * k + s], recv_sem=recv_sems.at[7 * k + s],
                device_id=to, device_id_type=MESH)

        mine = [pltpu.make_async_copy(ins[k], block(k, me), local_sems.at[k]) for k in range(n)]
        first = []
        for k in range(n):
            first.append(copy(k, 0, me, sibling, src=ins[k]))
            for j, chip in enumerate(chips):
                first.append(copy(k, 1 + j, me, (*chip, c), src=ins[k]))
        return me, sibling, chips, c, copy, mine, first

    def start(ins, outs, sems):
        *_, mine, first = plan(ins, outs, sems)
        for cp in mine + first:
            cp.start()

    def finish(ins, outs, sems):
        me, sibling, chips, c, copy, mine, first = plan(ins, outs, sems)
        passed = []
        for j, chip in enumerate(chips):
            for k in range(n):
                copy(k, 1 + j, (*chip, c), me).wait_recv()
                fwd = copy(k, 4 + j, (*chip, c), sibling)
                fwd.start()
                passed.append(fwd)
        for k in range(n):
            copy(k, 0, sibling, me).wait_recv()
        for j, chip in enumerate(chips):
            for k in range(n):
                copy(k, 4 + j, (*chip, 1 - c), me).wait_recv()
        for cp in first + passed:
            cp.wait_send()
        for cp in mine:
            cp.wait()

    return _Comm(list(arrs), [jax.ShapeDtypeStruct((N_DEV,) + a.shape, a.dtype) for a in arrs],
                 [pltpu.SemaphoreType.DMA((7 * n,)), pltpu.SemaphoreType.DMA((7 * n,)),
                  pltpu.SemaphoreType.DMA((n,))], start, finish)


def _scatter_comm(blocks):
    n = len(blocks)

    def plan(ins, outs, sems, arrivals):
        send_sems, recv_sems, local_sems = sems
        x, y, c = _place()
        me = 4 * x + 2 * y + c
        local = [pltpu.make_async_copy(ins[k].at[me], outs[k].at[me], local_sems.at[k]) for k in range(n)]
        sends, recvs = [], []
        for k in range(n):
            for mask in range(1, N_DEV):
                px = 1 - x if mask & 4 else x
                py = 1 - y if mask & 2 else y
                pc = 1 - c if mask & 1 else c
                peer = 4 * px + 2 * py + pc
                sem = 7 * k + mask - 1
                both = dict(send_sem=send_sems.at[sem], recv_sem=recv_sems.at[sem], device_id=(px, py, pc),
                            device_id_type=MESH)
                sends.append(pltpu.make_async_remote_copy(src_ref=ins[k].at[peer], dst_ref=outs[k].at[me], **both))
                if arrivals:
                    recvs.append(pltpu.make_async_remote_copy(src_ref=ins[k].at[me], dst_ref=outs[k].at[peer],
                                                              **both))
        return local, sends, recvs

    def start(ins, outs, sems):
        local, sends, _ = plan(ins, outs, sems, arrivals=False)
        for cp in local + sends:
            cp.start()

    def finish(ins, outs, sems):
        local, sends, recvs = plan(ins, outs, sems, arrivals=True)
        for cp in recvs:
            cp.wait_recv()
        for cp in sends:
            cp.wait_send()
        for cp in local:
            cp.wait()

    return _Comm(list(blocks), [jax.ShapeDtypeStruct(b.shape, b.dtype) for b in blocks],
                 [pltpu.SemaphoreType.DMA((7 * n,)), pltpu.SemaphoreType.DMA((7 * n,)),
                  pltpu.SemaphoreType.DMA((n,))], start, finish)


_DIMS = {"nn": (((1,), (0,)), ((), ())), "nt": (((1,), (1,)), ((), ())), "tn": (((0,), (0,)), ((), ()))}


def _mm(a, b, mode, out_dtype, name, bias=None, tm=512, tn=512, comm=None):
    assert a.dtype == BF16 and b.dtype == BF16
    if mode == "tn":
        k_dim, m_dim = a.shape
    else:
        m_dim, k_dim = a.shape
    n_dim = b.shape[0] if mode == "nt" else b.shape[1]
    tm, tn = _pick(m_dim, tm), _pick(n_dim, tn)
    a_spec = (pl.BlockSpec((k_dim, tm), lambda i, j: (0, i)) if mode == "tn"
              else pl.BlockSpec((tm, k_dim), lambda i, j: (i, 0)))
    b_spec = (pl.BlockSpec((tn, k_dim), lambda i, j: (j, 0)) if mode == "nt"
              else pl.BlockSpec((k_dim, tn), lambda i, j: (0, j)))
    in_specs = [a_spec, b_spec]
    args = [a, b]
    if bias is not None:
        in_specs.append(pl.BlockSpec((1, tn), lambda i, j: (0, j)))
        args.append(bias)
    dims = _DIMS[mode]

    def body(ins, outs, scratch):
        total = lax.dot_general(ins[0][...], ins[1][...], dims, preferred_element_type=F32)
        if bias is not None:
            total = total + ins[2][...]
        outs[0][...] = total.astype(out_dtype)

    (out,), extra = _host_call(
        body, name, grid=(m_dim // tm, n_dim // tn), in_specs=in_specs,
        out_specs=[pl.BlockSpec((tm, tn), lambda i, j: (i, j))],
        out_shape=[jax.ShapeDtypeStruct((m_dim, n_dim), out_dtype)], scratch_shapes=[], args=args, comm=comm)
    return out if comm is None else (out, extra)


def _adam_math(w, g, m, v):
    m = ADAM_B1 * m + (1.0 - ADAM_B1) * g
    v = ADAM_B2 * v + (1.0 - ADAM_B2) * (g * g)
    m_hat = m / (1.0 - ADAM_B1 ** ADAM_STEP)
    v_hat = v / (1.0 - ADAM_B2 ** ADAM_STEP)
    delta = -ADAM_LR * (m_hat / (jnp.sqrt(v_hat) + ADAM_EPS) + ADAM_WD * w)
    return delta, m, v


def _adamw(w, m, v, name, g=None, parts=None):
    rows, cols = w.shape
    tr = rows
    if rows * cols * 4 > ADAMW_BLOCK_BYTES:
        tr = max(t for t in range(16, rows, 16) if rows % t == 0 and t * cols * 4 <= ADAMW_BLOCK_BYTES)

    def body(w_ref, m_ref, v_ref, g_ref, go_ref, d_ref, mo_ref, vo_ref):
        if parts is None:
            grad = g_ref[...]
        else:
            grad = g_ref[0].astype(F32)
            for d in range(1, N_DEV):
                grad = grad + g_ref[d].astype(F32)
        delta, m_new, v_new = _adam_math(w_ref[...], grad, m_ref[...], v_ref[...])
        go_ref[...] = grad
        d_ref[...] = delta
        mo_ref[...] = m_new
        vo_ref[...] = v_new

    spec = _tile(tr, cols)
    g_spec = spec if parts is None else pl.BlockSpec((N_DEV, tr, cols), lambda i: (0, i, 0))
    shape = jax.ShapeDtypeStruct((rows, cols), F32)
    return pl.pallas_call(
        body, name=name, out_shape=[shape] * 4, grid=(rows // tr,),
        in_specs=[spec, spec, spec, g_spec], out_specs=[spec] * 4, compiler_params=_cparams(1),
    )(w, m, v, g if parts is None else parts)


def _pack_grads(small_x, small_m, small_f, small_g, small_a, small_c, dbv, dbg, dwv, dwg, dw_conv):
    pieces = [
        (small_x, 2, D_MODEL), (small_x, 1, D_MODEL), (small_m, 4, D_MODEL), (small_m, 2, D_MODEL),
        (small_m, 1, D_MODEL), (small_f, 1, D_MODEL),
        (small_x, 0, D_MODEL), (small_m, 3, D_MODEL),
        (small_a, 0, D_ATTN), (small_a, 1, D_ATTN), (small_a, 2, D_ATTN), (small_c, 3, D_CONV),
        (small_c, 4, D_CONV), (small_g, 0, D_MODEL), (small_g, 1, D_MODEL),
        (small_c, 0, D_CONV), (small_c, 1, D_CONV), (small_c, 2, D_CONV),
        (small_g, 2, D_MODEL), (small_m, 0, D_MODEL), (small_f, 0, D_MODEL),
        (dbv, 0, D_FF), (dbg, 0, D_FF),
    ]
    pieces += [(dw_conv, j, D_CONV) for j in range(CONV_K)]
    pieces += [(src, tap, D_FF) for tap in range(3) for src in (dwv, dwg)]
    sources = [small_x, small_m, small_f, small_g, small_a, small_c, dbv, dbg, dwv, dwg, dw_conv]
    assert sum(width for _, _, width in pieces) == PACKED_TOTAL

    def body(*refs):
        o_ref = refs[-1]
        ref_of = {id(src): ref for src, ref in zip(sources, refs)}
        off = 0
        for src, row, width in pieces:
            o_ref[:, off:off + width] = ref_of[id(src)][row:row + 1, :]
            off += width

    return pl.pallas_call(body, name="pack_grads", out_shape=jax.ShapeDtypeStruct((1, PACKED_TOTAL), F32))(*sources)


def _small_adamw(gathered, gathered_rel, weights, mom_m, mom_v):
    vec_names = [name for name, _ in SMALL]
    states = []
    for name in vec_names + ["rel_bias"]:
        states += [weights[name], mom_m[name], mom_v[name]]
    states = [a.reshape(a.shape[1:]) if a.ndim == 3 else a for a in states]
    n_state = len(states)

    def body(*refs):
        g_ref, rel_ref = refs[0], refs[1]
        state_refs, out_refs = refs[2:2 + n_state], refs[2 + n_state:]
        total = g_ref[0:1, :]
        rel = rel_ref[0]
        for d in range(1, N_DEV):
            total = total + g_ref[d:d + 1, :]
            rel = rel + rel_ref[d]
        off = 0
        for n, (name, width) in enumerate(SMALL):
            grad = total[:, off:off + width]
            w_ref, m_ref, v_ref = state_refs[3 * n:3 * n + 3]
            for ref, val in zip(out_refs[4 * n:4 * n + 4], (grad,) + _adam_math(w_ref[...], grad, m_ref[...], v_ref[...])):
                ref[...] = val
            off += width
        n = len(SMALL)
        w_ref, m_ref, v_ref = state_refs[3 * n:3 * n + 3]
        for ref, val in zip(out_refs[4 * n:4 * n + 4], (rel,) + _adam_math(w_ref[...], rel, m_ref[...], v_ref[...])):
            ref[...] = val
        dwc_ref, dwf_ref = out_refs[4 * n + 4:]
        dwc_ref[...] = jnp.zeros_like(dwc_ref)
        dwf_ref[...] = jnp.zeros_like(dwf_ref)
        for j in range(CONV_K):
            dwc_ref[j:j + 1, :] = total[:, off:off + D_CONV]
            off += D_CONV
        for tap in range(3):
            dwf_ref[tap:tap + 1, :] = total[:, off:off + 2 * D_FF]
            off += 2 * D_FF

    out_shape = []
    for k in range(n_state // 3):
        out_shape += [jax.ShapeDtypeStruct(states[3 * k].shape, F32)] * 4
    out_shape += [jax.ShapeDtypeStruct((CONV_HALO, D_CONV), F32), jax.ShapeDtypeStruct((8, 2 * D_FF), F32)]
    res = pl.pallas_call(
        body, name="small_adamw", out_shape=out_shape,
        compiler_params=pltpu.CompilerParams(vmem_limit_bytes=VMEM_LIMIT_BYTES),
    )(gathered, gathered_rel, *states)
    updates = {name: tuple(res[4 * n:4 * n + 4]) for n, name in enumerate(vec_names + ["rel_bias"])}
    return updates, res[-2], res[-1]


def _silu_vec(c):
    def body(c_ref, o_ref):
        v = c_ref[...]
        o_ref[...] = v * _sig(v)

    return pl.pallas_call(body, name="silu_c", out_shape=jax.ShapeDtypeStruct(c.shape, F32))(c)


def _ada_fwd(c_all, w_shard):
    def body(c_ref, w_ref, o_ref):
        o_ref[...] = jnp.dot(c_ref[...], w_ref[...], precision=HIGHEST, preferred_element_type=F32)

    return pl.pallas_call(
        body, name="ada_fwd", out_shape=jax.ShapeDtypeStruct((N_DEV, w_shard.shape[1]), F32),
        compiler_params=pltpu.CompilerParams(vmem_limit_bytes=VMEM_LIMIT_BYTES),
    )(c_all, w_shard)


def _ada_grad(c_all, dmod_shard):
    def body(c_ref, d_ref, o_ref):
        o_ref[...] = lax.dot_general(c_ref[...], d_ref[...], _DIMS["tn"], precision=HIGHEST,
                                     preferred_element_type=F32)

    return pl.pallas_call(
        body, name="ada_grad", out_shape=jax.ShapeDtypeStruct((D_MODEL, dmod_shard.shape[1]), F32),
        compiler_params=pltpu.CompilerParams(vmem_limit_bytes=VMEM_LIMIT_BYTES),
    )(c_all, dmod_shard)


ROWS = 256


def _rms(v):
    r = lax.rsqrt(jnp.mean(v * v, axis=-1, keepdims=True) + EPS)
    return v * r, r


def _rms_bwd(dxn, xn, r):
    return r * (dxn - xn * jnp.mean(dxn * xn, axis=-1, keepdims=True))


def _colsum(v):
    return jnp.sum(v, axis=0, keepdims=True)


def _pre_mix(x, mod6, g1):
    seq = x.shape[0]

    def body(x_ref, mod_ref, g_ref, h_ref):
        xn, _ = _rms(x_ref[...])
        y = xn * g_ref[...]
        h_ref[...] = (y * (1.0 + mod_ref[SC_M:SC_M + 1, :]) + mod_ref[SH_M:SH_M + 1, :]).astype(BF16)

    return pl.pallas_call(
        body, name="pre_mix", out_shape=jax.ShapeDtypeStruct((seq, D_MODEL), BF16), grid=(seq // ROWS,),
        in_specs=[_tile(ROWS, D_MODEL), _full((6, D_MODEL)), _full((1, D_MODEL))],
        out_specs=_tile(ROWS, D_MODEL), compiler_params=_cparams(1),
    )(x, mod6, g1)


def _post_mix_pre_ffn(ymix, x, mod6, g2, g3):
    seq = x.shape[0]

    def body(y_ref, x_ref, mod_ref, g2_ref, g3_ref, x1_ref, h_ref):
        yn, _ = _rms(y_ref[...])
        x1 = x_ref[...] + mod_ref[GT_M:GT_M + 1, :] * (yn * g2_ref[...])
        x1_ref[...] = x1
        xn, _ = _rms(x1)
        y3 = xn * g3_ref[...]
        h_ref[...] = (y3 * (1.0 + mod_ref[SC_F:SC_F + 1, :]) + mod_ref[SH_F:SH_F + 1, :]).astype(BF16)

    return pl.pallas_call(
        body, name="post_mix_pre_ffn",
        out_shape=[jax.ShapeDtypeStruct((seq, D_MODEL), F32), jax.ShapeDtypeStruct((seq, D_MODEL), BF16)],
        grid=(seq // ROWS,),
        in_specs=[_tile(ROWS, D_MODEL), _tile(ROWS, D_MODEL), _full((6, D_MODEL)), _full((1, D_MODEL)),
                  _full((1, D_MODEL))],
        out_specs=[_tile(ROWS, D_MODEL), _tile(ROWS, D_MODEL)], compiler_params=_cparams(1),
    )(ymix, x, mod6, g2, g3)


def _final(yf, x1, target, mod6, g4):
    seq = x1.shape[0]

    def body(y_ref, x1_ref, t_ref, mod_ref, g_ref, loss_ref, dout_ref, dyf_ref, small_ref):
        i = pl.program_id(0)

        @pl.when(i == 0)
        def _():
            loss_ref[...] = jnp.zeros_like(loss_ref)
            small_ref[...] = jnp.zeros_like(small_ref)

        gt = mod_ref[GT_F:GT_F + 1, :]
        g4v = g_ref[...]
        yn, r = _rms(y_ref[...])
        out = x1_ref[...] + gt * (yn * g4v)
        err = out - t_ref[...]
        loss_ref[...] += jnp.sum(jnp.mean(err * err, axis=-1, keepdims=True))
        dout = err * (1.0 / D_MODEL)
        dout_ref[...] = dout
        small_ref[0:1, :] += _colsum(dout * gt * yn)
        small_ref[1:2, :] += _colsum(dout * (yn * g4v))
        dyf_ref[...] = _rms_bwd(dout * gt * g4v, yn, r).astype(BF16)

    return pl.pallas_call(
        body, name="final_loss",
        out_shape=[jax.ShapeDtypeStruct((1, 128), F32), jax.ShapeDtypeStruct((seq, D_MODEL), F32),
                   jax.ShapeDtypeStruct((seq, D_MODEL), BF16), jax.ShapeDtypeStruct((8, D_MODEL), F32)],
        grid=(seq // ROWS,),
        in_specs=[_tile(ROWS, D_MODEL)] * 3 + [_full((6, D_MODEL)), _full((1, D_MODEL))],
        out_specs=[_full((1, 128)), _tile(ROWS, D_MODEL), _tile(ROWS, D_MODEL), _full((8, D_MODEL))],
        compiler_params=_cparams(1),
    )(yf, x1, target, mod6, g4)


def _mid_bwd(dh2, x1, dout, ymix, mod6, g3, g2):
    seq = x1.shape[0]

    def body(dh_ref, x1_ref, dout_ref, y_ref, mod_ref, g3_ref, g2_ref, dx1_ref, dy_ref, small_ref):
        i = pl.program_id(0)

        @pl.when(i == 0)
        def _():
            small_ref[...] = jnp.zeros_like(small_ref)

        dh = dh_ref[...]
        g3v, g2v = g3_ref[...], g2_ref[...]
        xn, r3 = _rms(x1_ref[...])
        y3 = xn * g3v
        dy3 = dh * (1.0 + mod_ref[SC_F:SC_F + 1, :])
        small_ref[0:1, :] += _colsum(dy3 * xn)
        small_ref[1:2, :] += _colsum(dh * y3)
        small_ref[2:3, :] += _colsum(dh)
        dx1 = dout_ref[...] + _rms_bwd(dy3 * g3v, xn, r3)
        dx1_ref[...] = dx1
        gt = mod_ref[GT_M:GT_M + 1, :]
        yn, r2 = _rms(y_ref[...])
        small_ref[3:4, :] += _colsum(dx1 * gt * yn)
        small_ref[4:5, :] += _colsum(dx1 * (yn * g2v))
        dy_ref[...] = _rms_bwd(dx1 * gt * g2v, yn, r2).astype(BF16)

    return pl.pallas_call(
        body, name="mid_bwd",
        out_shape=[jax.ShapeDtypeStruct((seq, D_MODEL), F32), jax.ShapeDtypeStruct((seq, D_MODEL), BF16),
                   jax.ShapeDtypeStruct((8, D_MODEL), F32)],
        grid=(seq // ROWS,),
        in_specs=[_tile(ROWS, D_MODEL)] * 4 + [_full((6, D_MODEL)), _full((1, D_MODEL)), _full((1, D_MODEL))],
        out_specs=[_tile(ROWS, D_MODEL), _tile(ROWS, D_MODEL), _full((8, D_MODEL))],
        compiler_params=_cparams(1),
    )(dh2, x1, dout, ymix, mod6, g3, g2)


def _pre_mix_bwd(dh1, x, dx1, mod6, g1):
    seq = x.shape[0]

    def body(dh_ref, x_ref, dx1_ref, mod_ref, g_ref, dx_ref, small_ref):
        i = pl.program_id(0)

        @pl.when(i == 0)
        def _():
            small_ref[...] = jnp.zeros_like(small_ref)

        dh = dh_ref[...]
        g1v = g_ref[...]
        xn, r = _rms(x_ref[...])
        dy = dh * (1.0 + mod_ref[SC_M:SC_M + 1, :])
        small_ref[0:1, :] += _colsum(dy * xn)
        small_ref[1:2, :] += _colsum(dh * (xn * g1v))
        small_ref[2:3, :] += _colsum(dh)
        dx_ref[...] = dx1_ref[...] + _rms_bwd(dy * g1v, xn, r)

    return pl.pallas_call(
        body, name="pre_mix_bwd",
        out_shape=[jax.ShapeDtypeStruct((seq, D_MODEL), F32), jax.ShapeDtypeStruct((8, D_MODEL), F32)],
        grid=(seq // ROWS,),
        in_specs=[_tile(ROWS, D_MODEL)] * 3 + [_full((6, D_MODEL)), _full((1, D_MODEL))],
        out_specs=[_tile(ROWS, D_MODEL), _full((8, D_MODEL))], compiler_params=_cparams(1),
    )(dh1, x, dx1, mod6, g1)


def _toeplitz_onehot(shape, offset_axis, top):
    m = lax.broadcasted_iota(jnp.int32, shape, offset_axis)
    i = lax.broadcasted_iota(jnp.int32, shape, 1 - offset_axis)
    return (i == jnp.clip(top - m, -MAX_REL, MAX_REL) + MAX_REL).astype(F32)


def _bias_table(rel_bias):
    width = GROUP_Q + GROUP_K

    def body(rb_ref, o_ref, t_ref):
        t_ref[...] = jnp.dot(rb_ref[...], _toeplitz_onehot((N_REL, width), 1, GROUP_K - 1), precision=HIGHEST,
                             preferred_element_type=F32)
        lane = lax.broadcasted_iota(jnp.int32, (N_HEADS, GROUP_K), 1)
        for r in range(GROUP_Q):
            first_key = (r // CHUNK) * CHUNK
            band = jnp.logical_and(lane >= first_key, lane < first_key + BAND)
            o_ref[r] = jnp.where(band, t_ref[:, GROUP_Q - 1 - r:GROUP_Q - 1 - r + GROUP_K], NEG_INF)

    return pl.pallas_call(
        body, name="bias_table", out_shape=jax.ShapeDtypeStruct((GROUP_Q, N_HEADS, GROUP_K), F32),
        scratch_shapes=[pltpu.VMEM((N_HEADS, width), F32)],
    )(rel_bias)


def _bias_grad(dbias_q):
    def body(d_ref, o_ref, t_ref):
        t_ref[...] = jnp.zeros_like(t_ref)
        for qi in range(CHUNK):
            t_ref[:, CHUNK - 1 - qi:CHUNK - 1 - qi + BAND] += d_ref[qi]
        o_ref[...] = jnp.dot(t_ref[...], _toeplitz_onehot((TOEPLITZ, N_REL), 0, BAND - 1), precision=HIGHEST,
                             preferred_element_type=F32)

    return pl.pallas_call(
        body, name="bias_grad", out_shape=jax.ShapeDtypeStruct((N_HEADS, N_REL), F32),
        scratch_shapes=[pltpu.VMEM((N_HEADS, TOEPLITZ), F32)],
    )(dbias_q)


def _load_resident(pairs, sems):
    copies = [pltpu.make_async_copy(src, dst, sems.at[n]) for n, (src, dst) in enumerate(pairs)]
    for cp in copies:
        cp.start()
    for cp in copies:
        cp.wait()


def _softmax_rows(s_ref, t_ref, valid, rows):
    s = s_ref[rows, :] * (HEAD_DIM ** -0.5) + t_ref[rows, :]
    s = jnp.where(valid, s, NEG_INF)
    e = jnp.exp(s - jnp.max(s, axis=-1, keepdims=True))
    return e / jnp.sum(e, axis=-1, keepdims=True)


def _valid_keys(g):
    kj = lax.broadcasted_iota(jnp.int32, (SOFTMAX_ROWS, GROUP_K), 1)
    return kj >= PAD_ROWS - g * GROUP_Q


def _attn_fwd(qkv, kpad, vpad, table, comm=None):
    seq = qkv.shape[0]

    def body(ins, outs, scratch):
        q_ref, k_hbm, v_hbm, t_hbm = ins
        (o_ref,) = outs
        k_ref, v_ref, t_ref, s_ref, p_ref, sems = scratch
        g = pl.program_id(0)

        @pl.when(g == 0)
        def _():
            _load_resident(((k_hbm, k_ref), (v_hbm, v_ref), (t_hbm, t_ref)), sems)

        window = pl.ds(pl.multiple_of(g * GROUP_Q, GROUP_Q), GROUP_K)
        valid = _valid_keys(g)
        for h in range(N_HEADS):
            cols = slice(h * HEAD_DIM, (h + 1) * HEAD_DIM)
            buf = h % 2
            s_ref[buf] = lax.dot_general(q_ref[:, cols], k_ref[window, cols], _DIMS["nt"],
                                         preferred_element_type=F32)
            for r in range(GROUP_Q // SOFTMAX_ROWS):
                rows = slice(r * SOFTMAX_ROWS, (r + 1) * SOFTMAX_ROWS)
                p_ref[buf, rows, :] = _softmax_rows(s_ref.at[buf], t_ref.at[h], valid, rows).astype(BF16)
            o_ref[:, cols] = jnp.dot(p_ref[buf], v_ref[window, cols], preferred_element_type=F32).astype(BF16)

    (ao,), extra = _host_call(
        body, "attn_fwd", grid=(seq // GROUP_Q,),
        in_specs=[_tile(GROUP_Q, D_ATTN), ANY, ANY, ANY], out_specs=[_tile(GROUP_Q, D_ATTN)],
        out_shape=[jax.ShapeDtypeStruct((seq, D_ATTN), BF16)],
        scratch_shapes=[pltpu.VMEM(kpad.shape, BF16), pltpu.VMEM(vpad.shape, BF16), pltpu.VMEM(table.shape, F32),
                        pltpu.VMEM((2, GROUP_Q, GROUP_K), F32), pltpu.VMEM((2, GROUP_Q, GROUP_K), BF16),
                        pltpu.SemaphoreType.DMA((3,))],
        args=[qkv, kpad, vpad, table], comm=comm)
    return ao, extra


def _attn_bwd(qkv, kpad, vpad, table, dao, comm=None):
    seq = qkv.shape[0]
    n_groups = seq // GROUP_Q
    fold_w = GROUP_K + (GROUP - 1) * CHUNK

    def body(ins, outs, scratch):
        q_ref, do_ref, k_hbm, v_hbm, t_hbm = ins
        dq_ref, dkt_hbm, dvt_hbm, db_ref, cs_ref = outs
        k_ref, v_ref, t_ref, db_acc, dkt_acc, dvt_acc, s_ref, dp_ref, p_ref, ds_ref, sems = scratch
        g = pl.program_id(0)

        @pl.when(g == 0)
        def _():
            _load_resident(((k_hbm, k_ref), (v_hbm, v_ref), (t_hbm, t_ref)), sems)
            db_acc[...] = jnp.zeros_like(db_acc)
            dkt_acc[...] = jnp.zeros_like(dkt_acc)
            dvt_acc[...] = jnp.zeros_like(dvt_acc)
            cs_ref[...] = jnp.zeros_like(cs_ref)

        window = pl.ds(pl.multiple_of(g * GROUP_Q, GROUP_Q), GROUP_K)
        valid = _valid_keys(g)
        for h in range(N_HEADS):
            cols = slice(h * HEAD_DIM, (h + 1) * HEAD_DIM)
            buf = h % 2
            qh, doh = q_ref[:, cols], do_ref[:, cols]
            kh, vh = k_ref[window, cols], v_ref[window, cols]
            s_ref[buf] = lax.dot_general(qh, kh, _DIMS["nt"], preferred_element_type=F32)
            dp_ref[buf] = lax.dot_general(doh, vh, _DIMS["nt"], preferred_element_type=F32)
            for r in range(GROUP_Q // SOFTMAX_ROWS):
                rows = slice(r * SOFTMAX_ROWS, (r + 1) * SOFTMAX_ROWS)
                p = _softmax_rows(s_ref.at[buf], t_ref.at[h], valid, rows)
                dp = dp_ref[buf, rows, :]
                ds = p * (dp - jnp.sum(dp * p, axis=-1, keepdims=True))
                chunk = (r * SOFTMAX_ROWS) // CHUNK
                shift = (GROUP - 1 - chunk) * CHUNK
                local = slice(r * SOFTMAX_ROWS - chunk * CHUNK, (r + 1) * SOFTMAX_ROWS - chunk * CHUNK)
                db_acc[h, local, shift:shift + GROUP_K] += ds
                p_ref[buf, rows, :] = p.astype(BF16)
                ds_ref[buf, rows, :] = (ds * (HEAD_DIM ** -0.5)).astype(BF16)
            dq_ref[:, cols] = jnp.dot(ds_ref[buf], kh, preferred_element_type=F32).astype(BF16)
            dkt_acc[cols, window] += lax.dot_general(qh, ds_ref[buf], _DIMS["tn"], preferred_element_type=F32)
            dvt_acc[cols, window] += lax.dot_general(doh, p_ref[buf], _DIMS["tn"], preferred_element_type=F32)
        cs_ref[0:1, :] += _colsum(dq_ref[...].astype(F32))

        @pl.when(g == n_groups - 1)
        def _():
            lo = (GROUP - 1) * CHUNK
            for h in range(N_HEADS):
                db_ref[h] = db_acc[h, :, lo:lo + BAND]
            inside = pl.ds(PAD_ROWS, seq)
            ones = jnp.ones((8, seq), F32)
            for row, acc in ((1, dkt_acc), (2, dvt_acc)):
                cs_ref[row:row + 1, :] = lax.dot_general(ones, acc[:, inside], _DIMS["nt"], precision=HIGHEST,
                                                         preferred_element_type=F32)[0:1, :]
            out_k = pltpu.make_async_copy(dkt_acc.at[:, inside], dkt_hbm, sems.at[0])
            out_v = pltpu.make_async_copy(dvt_acc.at[:, inside], dvt_hbm, sems.at[1])
            out_k.start()
            out_v.start()
            out_k.wait()
            out_v.wait()

    t_shape = (D_ATTN, seq + PAD_ROWS)
    outs, extra = _host_call(
        body, "attn_bwd", grid=(n_groups,),
        in_specs=[_tile(GROUP_Q, D_ATTN), _tile(GROUP_Q, D_ATTN), ANY, ANY, ANY],
        out_specs=[_tile(GROUP_Q, D_ATTN), ANY, ANY, _full((N_HEADS, CHUNK, BAND)), _full((8, D_ATTN))],
        out_shape=[jax.ShapeDtypeStruct((seq, D_ATTN), BF16), jax.ShapeDtypeStruct((D_ATTN, seq), F32),
                   jax.ShapeDtypeStruct((D_ATTN, seq), F32), jax.ShapeDtypeStruct((N_HEADS, CHUNK, BAND), F32),
                   jax.ShapeDtypeStruct((8, D_ATTN), F32)],
        scratch_shapes=[pltpu.VMEM(kpad.shape, BF16), pltpu.VMEM(vpad.shape, BF16), pltpu.VMEM(table.shape, F32),
                        pltpu.VMEM((N_HEADS, CHUNK, fold_w), F32), pltpu.VMEM(t_shape, F32),
                        pltpu.VMEM(t_shape, F32), pltpu.VMEM((2, GROUP_Q, GROUP_K), F32),
                        pltpu.VMEM((2, GROUP_Q, GROUP_K), F32), pltpu.VMEM((2, GROUP_Q, GROUP_K), BF16),
                        pltpu.VMEM((2, GROUP_Q, GROUP_K), BF16), pltpu.SemaphoreType.DMA((3,))],
        args=[qkv, dao, kpad, vpad, table], comm=comm)
    return outs, extra


def _assemble_dz(dq, dkt, dvt, dglu_a, dglu_b, dga, dgb):
    seq = dq.shape[0]
    rows = 512
    transposed = pl.BlockSpec((D_ATTN, rows), lambda i: (0, i))

    def body(dq_ref, dkt_ref, dvt_ref, da_ref, db_ref, dga_ref, dgb_ref, o_ref):
        o_ref[:, 0:D_ATTN] = dq_ref[...]
        o_ref[:, D_ATTN:2 * D_ATTN] = dkt_ref[...].T.astype(BF16)
        o_ref[:, 2 * D_ATTN:3 * D_ATTN] = dvt_ref[...].T.astype(BF16)
        off = 3 * D_ATTN
        for ref in (da_ref, db_ref, dga_ref, dgb_ref):
            width = ref.shape[1]
            o_ref[:, off:off + width] = ref[...]
            off += width

    width = 3 * D_ATTN + 2 * D_CONV + 2 * D_MODEL
    return pl.pallas_call(
        body, name="assemble_dz", out_shape=jax.ShapeDtypeStruct((seq, width), BF16), grid=(seq // rows,),
        in_specs=[_tile(rows, D_ATTN), transposed, transposed, _tile(rows, D_CONV), _tile(rows, D_CONV),
                  _tile(rows, D_MODEL), _tile(rows, D_MODEL)],
        out_specs=_tile(rows, width), compiler_params=_cparams(1),
    )(dq, dkt, dvt, dglu_a, dglu_b, dga, dgb)


CONV_ROWS = 256


def _ln_silu(u1, g, b):
    mu = jnp.mean(u1, axis=-1, keepdims=True)
    xc = u1 - mu
    rs = lax.rsqrt(jnp.mean(xc * xc, axis=-1, keepdims=True) + EPS)
    xhat = xc * rs
    u2 = xhat * g + b
    return xhat, rs, u2


def _glu_into(s_ref, a_ref, b_ref, ah_ref, bh_ref, first):
    halo = ah_ref[...] * _sig(bh_ref[...])
    s_ref[0:CONV_HALO, :] = jnp.where(first, 0.0, halo)
    s_ref[CONV_HALO:CONV_HALO + CONV_ROWS, :] = a_ref[...] * _sig(b_ref[...])


def _conv_fwd(zr, w_dw, b_dw, g_ln, b_ln, comm=None):
    seq = zr.shape[0]

    def body(a_ref, b_ref, ah_ref, bh_ref, w_ref, bias_ref, g_ref, bl_ref, u1_ref, u3_ref, s_ref):
        _glu_into(s_ref, a_ref, b_ref, ah_ref, bh_ref, pl.program_id(0) == 0)
        acc = jnp.zeros((CONV_ROWS, D_CONV), F32) + bias_ref[...]
        for j in range(CONV_K):
            acc = acc + w_ref[j:j + 1, :] * s_ref[2 + j:2 + j + CONV_ROWS, :]
        u1_ref[...] = acc
        _, _, u2 = _ln_silu(acc, g_ref[...], bl_ref[...])
        u3_ref[...] = (u2 * _sig(u2)).astype(BF16)

    return _host_call(
        lambda ins, outs, scratch: body(*ins, *outs, *scratch), "conv_fwd", grid=(seq // CONV_ROWS,),
        in_specs=[_tile(CONV_ROWS, D_CONV, 0), _tile(CONV_ROWS, D_CONV, 1),
                  _prev(CONV_HALO, D_CONV, CONV_ROWS, 0), _prev(CONV_HALO, D_CONV, CONV_ROWS, 1),
                  _full((CONV_K, D_CONV)), _full((1, D_CONV)), _full((1, D_CONV)), _full((1, D_CONV))],
        out_specs=[_tile(CONV_ROWS, D_CONV), _tile(CONV_ROWS, D_CONV)],
        out_shape=[jax.ShapeDtypeStruct((seq, D_CONV), F32), jax.ShapeDtypeStruct((seq, D_CONV), BF16)],
        scratch_shapes=[pltpu.VMEM((CONV_HALO + CONV_ROWS, D_CONV), F32)],
        args=[zr, zr, zr, zr, w_dw, b_dw, g_ln, b_ln], comm=comm)


def _conv_bwd(zr, u1, du3, w_dw, g_ln, b_ln, comm=None):
    seq = zr.shape[0]
    n_tiles = seq // CONV_ROWS
    n_halo = seq // CONV_HALO
    ext = CONV_ROWS + CONV_HALO

    def body(a_ref, b_ref, ah_ref, bh_ref, u1_ref, u1n_ref, d3_ref, d3n_ref, w_ref, g_ref, bl_ref,
             da_ref, db_ref, dw_ref, small_ref, s_ref, d_ref):
        i = pl.program_id(0)

        @pl.when(i == 0)
        def _():
            dw_ref[...] = jnp.zeros_like(dw_ref)
            small_ref[...] = jnp.zeros_like(small_ref)

        _glu_into(s_ref, a_ref, b_ref, ah_ref, bh_ref, i == 0)
        gv, bv = g_ref[...], bl_ref[...]

        def du1_of(u1, d3):
            xhat, rs, u2 = _ln_silu(u1, gv, bv)
            sg = _sig(u2)
            du2 = d3 * (sg * (1.0 + u2 * (1.0 - sg)))
            dxh = du2 * gv
            du1 = rs * (dxh - jnp.mean(dxh, axis=-1, keepdims=True)
                        - xhat * jnp.mean(dxh * xhat, axis=-1, keepdims=True))
            return du1, du2, xhat

        du1, du2, xhat = du1_of(u1_ref[...], d3_ref[...])
        du1n, _, _ = du1_of(u1n_ref[...], d3n_ref[...])
        d_ref[0:CONV_ROWS, :] = du1
        d_ref[CONV_ROWS:ext, :] = jnp.where(i == n_tiles - 1, 0.0, du1n)
        small_ref[0:1, :] += _colsum(du1)
        small_ref[1:2, :] += _colsum(du2 * xhat)
        small_ref[2:3, :] += _colsum(du2)
        du0 = jnp.zeros((CONV_ROWS, D_CONV), F32)
        for j in range(CONV_K):
            dw_ref[j:j + 1, :] += _colsum(du1 * s_ref[2 + j:2 + j + CONV_ROWS, :])
            du0 = du0 + w_ref[j:j + 1, :] * d_ref[CONV_K - 1 - j:CONV_K - 1 - j + CONV_ROWS, :]
        sb = _sig(b_ref[...])
        da = du0 * sb
        dbv = du0 * a_ref[...] * sb * (1.0 - sb)
        da_ref[...] = da.astype(BF16)
        db_ref[...] = dbv.astype(BF16)
        small_ref[3:4, :] += _colsum(da)
        small_ref[4:5, :] += _colsum(dbv)

    return _host_call(
        lambda ins, outs, scratch: body(*ins, *outs, *scratch), "conv_bwd", grid=(n_tiles,),
        in_specs=[_tile(CONV_ROWS, D_CONV, 0), _tile(CONV_ROWS, D_CONV, 1),
                  _prev(CONV_HALO, D_CONV, CONV_ROWS, 0), _prev(CONV_HALO, D_CONV, CONV_ROWS, 1),
                  _tile(CONV_ROWS, D_CONV), _next(CONV_HALO, D_CONV, CONV_ROWS, n_halo),
                  _tile(CONV_ROWS, D_CONV), _next(CONV_HALO, D_CONV, CONV_ROWS, n_halo),
                  _full((CONV_K, D_CONV)), _full((1, D_CONV)), _full((1, D_CONV))],
        out_specs=[_tile(CONV_ROWS, D_CONV), _tile(CONV_ROWS, D_CONV), _full((CONV_HALO, D_CONV)),
                   _full((8, D_CONV))],
        out_shape=[jax.ShapeDtypeStruct((seq, D_CONV), BF16), jax.ShapeDtypeStruct((seq, D_CONV), BF16),
                   jax.ShapeDtypeStruct((CONV_HALO, D_CONV), F32), jax.ShapeDtypeStruct((8, D_CONV), F32)],
        scratch_shapes=[pltpu.VMEM((ext, D_CONV), F32), pltpu.VMEM((ext, D_CONV), F32)],
        args=[zr, zr, zr, zr, u1, u1, du3, du3, w_dw, g_ln, b_ln], comm=comm)


MERGE_ROWS = 256


def _merge_fwd(ao, u3, zr, w_ao, w_co, b_co):
    seq = ao.shape[0]

    def body(ao_ref, u3_ref, ga_ref, gb_ref, wa_ref, wc_ref, bc_ref, y_ref, a_ref, cb_ref):
        a = jnp.dot(ao_ref[...], wa_ref[...], preferred_element_type=F32)
        cb = jnp.dot(u3_ref[...], wc_ref[...], preferred_element_type=F32) + bc_ref[...]
        a_ref[...] = a
        cb_ref[...] = cb
        y_ref[...] = (_sig(ga_ref[...]) * a + _sig(gb_ref[...]) * cb).astype(BF16)

    f32_out = jax.ShapeDtypeStruct((seq, D_MODEL), F32)
    return pl.pallas_call(
        body, name="merge_fwd",
        out_shape=[jax.ShapeDtypeStruct((seq, D_MODEL), BF16), f32_out, f32_out],
        grid=(seq // MERGE_ROWS,),
        in_specs=[_tile(MERGE_ROWS, D_ATTN), _tile(MERGE_ROWS, D_CONV), _tile(MERGE_ROWS, D_MODEL, 1),
                  _tile(MERGE_ROWS, D_MODEL, 2), _full(w_ao.shape), _full(w_co.shape), _full((1, D_MODEL))],
        out_specs=[_tile(MERGE_ROWS, D_MODEL)] * 3, compiler_params=_cparams(1),
    )(ao, u3, zr, zr, w_ao, w_co, b_co)


def _merge_bwd(dy, a, cb, zr):
    seq = dy.shape[0]

    def body(dy_ref, a_ref, cb_ref, ga_ref, gb_ref, da_ref, dcb_ref, dga_ref, dgb_ref, small_ref):
        i = pl.program_id(0)

        @pl.when(i == 0)
        def _():
            small_ref[...] = jnp.zeros_like(small_ref)

        dy_v = dy_ref[...]
        sa, sb = _sig(ga_ref[...]), _sig(gb_ref[...])
        dcb = dy_v * sb
        dga = dy_v * a_ref[...] * sa * (1.0 - sa)
        dgb = dy_v * cb_ref[...] * sb * (1.0 - sb)
        da_ref[...] = (dy_v * sa).astype(BF16)
        dcb_ref[...] = dcb.astype(BF16)
        dga_ref[...] = dga.astype(BF16)
        dgb_ref[...] = dgb.astype(BF16)
        small_ref[0:1, :] += _colsum(dga)
        small_ref[1:2, :] += _colsum(dgb)
        small_ref[2:3, :] += _colsum(dcb)

    bf = jax.ShapeDtypeStruct((seq, D_MODEL), BF16)
    return pl.pallas_call(
        body, name="merge_bwd", out_shape=[bf, bf, bf, bf, jax.ShapeDtypeStruct((8, D_MODEL), F32)],
        grid=(seq // MERGE_ROWS,),
        in_specs=[_tile(MERGE_ROWS, D_MODEL)] * 3 + [_tile(MERGE_ROWS, D_MODEL, 1), _tile(MERGE_ROWS, D_MODEL, 2)],
        out_specs=[_tile(MERGE_ROWS, D_MODEL)] * 4 + [_full((8, D_MODEL))], compiler_params=_cparams(1),
    )(dy, a, cb, zr, zr)


FFN_ROWS = 512
FFN_BLOCKS = D_FF // FFN_COLS
GELU_C = math.sqrt(2.0 / math.pi)


def _gelu(v):
    t = jnp.tanh(GELU_C * (v + 0.044715 * (v * v * v)))
    return 0.5 * v * (1.0 + t), t


def _gelu_grad(v, t):
    return 0.5 * (1.0 + t) + 0.5 * v * (1.0 - t * t) * (GELU_C * (1.0 + 3.0 * 0.044715 * (v * v)))


def _sublane_rows(ref, n):
    return [jnp.broadcast_to(ref[r:r + 1, :], (8, FFN_COLS)) for r in range(n)]


def _rolls(tile, shifts):
    return tuple(pltpu.roll(tile, s, 0) for s in shifts)


def _behind(prev_rolls, cur, row_id):
    rolls = _rolls(cur, (1, 2))
    x1 = jnp.where(row_id < 1, prev_rolls[0], rolls[0])
    x2 = jnp.where(row_id < 2, prev_rolls[1], rolls[1])
    return (x2, x1, cur), rolls


def _ahead(cur_rolls, next_rolls, row_id):
    return (jnp.where(row_id < 7, cur_rolls[0], next_rolls[0]), jnp.where(row_id < 6, cur_rolls[1], next_rolls[1]))


def _conv3(taps, w, bias):
    return w[0] * taps[0] + w[1] * taps[1] + w[2] * taps[2] + bias


def _ffn_specs(rows):
    tile = lambda off: pl.BlockSpec((rows, FFN_COLS), lambda j, i: (i, j + off))
    prev = lambda off: pl.BlockSpec((FFN_HALO, FFN_COLS),
                                    lambda j, i: (jnp.maximum(i * (rows // FFN_HALO) - 1, 0), j + off))
    wgt = lambda off: pl.BlockSpec((3, FFN_COLS), lambda j, i: (0, j + off))
    vec = lambda off: pl.BlockSpec((1, FFN_COLS), lambda j, i: (0, j + off))
    return tile, prev, wgt, vec


def _ffn_act(up, w_dw, b_dw):
    seq = up.shape[0]
    tile, prev, wgt, vec = _ffn_specs(FFN_ROWS)

    def body(v_ref, g_ref, vp_ref, gp_ref, wv_ref, wg_ref, bv_ref, bg_ref, act_ref):
        first = pl.program_id(1) == 0
        row_id = lax.broadcasted_iota(jnp.int32, (8, FFN_COLS), 0)
        wv, wg = _sublane_rows(wv_ref, 3), _sublane_rows(wg_ref, 3)
        (bv,), (bg,) = _sublane_rows(bv_ref, 1), _sublane_rows(bg_ref, 1)
        rolls_v = _rolls(jnp.where(first, 0.0, vp_ref[...]), (1, 2))
        rolls_g = _rolls(jnp.where(first, 0.0, gp_ref[...]), (1, 2))
        for row in range(0, FFN_ROWS, 16):
            halves = []
            for r in (row, row + 8):
                taps_v, rolls_v = _behind(rolls_v, v_ref[r:r + 8, :], row_id)
                taps_g, rolls_g = _behind(rolls_g, g_ref[r:r + 8, :], row_id)
                halves.append(_gelu(_conv3(taps_g, wg, bg))[0] * _conv3(taps_v, wv, bv))
            act_ref[row:row + 16, :] = jnp.concatenate(halves, axis=0).astype(BF16)

    return pl.pallas_call(
        body, name="ffn_act", out_shape=jax.ShapeDtypeStruct((seq, D_FF), BF16),
        grid=(FFN_BLOCKS, seq // FFN_ROWS),
        in_specs=[tile(0), tile(FFN_BLOCKS), prev(0), prev(FFN_BLOCKS), wgt(0), wgt(FFN_BLOCKS),
                  vec(0), vec(FFN_BLOCKS)],
        out_specs=tile(0), compiler_params=_cparams(2),
    )(up, up, up, up, w_dw, w_dw, b_dw, b_dw)


def _ffn_act_bwd(up, dact, w_dw, b_dw, comm=None):
    seq = up.shape[0]
    n_tiles = seq // FFN_ROWS
    n_halo = seq // FFN_HALO
    tile, prev, wgt, vec = _ffn_specs(FFN_ROWS)
    nxt = lambda off: pl.BlockSpec(
        (FFN_HALO, FFN_COLS), lambda j, i: (jnp.minimum((i + 1) * (FFN_ROWS // FFN_HALO), n_halo - 1), j + off))
    acc = lambda off: pl.BlockSpec((8, FFN_COLS), lambda j, i: (0, j + off))

    def body(v_ref, g_ref, vp_ref, gp_ref, vn_ref, gn_ref, da_ref, dan_ref, wv_ref, wg_ref, bv_ref, bg_ref,
             dv_out, dg_out, dwv_ref, dwg_ref, dbv_ref, dbg_ref):
        i = pl.program_id(1)
        first, last = i == 0, i == n_tiles - 1

        @pl.when(first)
        def _():
            for r in (dwv_ref, dwg_ref, dbv_ref, dbg_ref):
                r[...] = jnp.zeros_like(r)

        row_id = lax.broadcasted_iota(jnp.int32, (8, FFN_COLS), 0)
        wv, wg = _sublane_rows(wv_ref, 3), _sublane_rows(wg_ref, 3)
        (bv,), (bg,) = _sublane_rows(bv_ref, 1), _sublane_rows(bg_ref, 1)
        zero = jnp.zeros((8, FFN_COLS), F32)
        sums_v, sums_g = [zero] * 4, [zero] * 4
        rolls_v = _rolls(jnp.where(first, 0.0, vp_ref[...]), (1, 2))
        rolls_g = _rolls(jnp.where(first, 0.0, gp_ref[...]), (1, 2))
        behind = None
        done_v, done_g = [], []

        def grads(v_tile, g_tile, dact, rolls_v, rolls_g):
            taps_v, rolls_v = _behind(rolls_v, v_tile, row_id)
            taps_g, rolls_g = _behind(rolls_g, g_tile, row_id)
            val, gate = _conv3(taps_v, wv, bv), _conv3(taps_g, wg, bg)
            gel, t = _gelu(gate)
            return dact * gel, dact * val * _gelu_grad(gate, t), taps_v, taps_g, rolls_v, rolls_g

        def finish(tile, nxt, row):
            for (d, d_rolls), (_, n_rolls), w, done, o_ref in ((tile[0], nxt[0], wv, done_v, dv_out),
                                                               (tile[1], nxt[1], wg, done_g, dg_out)):
                d1, d2 = _ahead(d_rolls, n_rolls, row_id)
                done.append(w[2] * d + w[1] * d1 + w[0] * d2)
                if len(done) == 2:
                    o_ref[row - 16:row, :] = jnp.concatenate(done, axis=0).astype(BF16)
                    done.clear()

        for row in range(0, FFN_ROWS, 16):
            dact16 = da_ref[row:row + 16, :].astype(F32)
            for r, dact in ((row, dact16[0:8, :]), (row + 8, dact16[8:16, :])):
                dval, dgate, taps_v, taps_g, rolls_v, rolls_g = grads(v_ref[r:r + 8, :], g_ref[r:r + 8, :], dact,
                                                                      rolls_v, rolls_g)
                sums_v = [s + dval * x for s, x in zip(sums_v, taps_v)] + [sums_v[3] + dval]
                sums_g = [s + dgate * x for s, x in zip(sums_g, taps_g)] + [sums_g[3] + dgate]
                tile = ((dval, _rolls(dval, (7, 6))), (dgate, _rolls(dgate, (7, 6))))
                if behind is not None:
                    finish(behind, tile, r)
                behind = tile
        dact_next = jnp.where(last, 0.0, dan_ref[...].astype(F32)[0:FFN_HALO, :])
        dval, dgate, *_ = grads(vn_ref[...], gn_ref[...], dact_next, rolls_v, rolls_g)
        finish(behind, ((dval, _rolls(dval, (7, 6))), (dgate, _rolls(dgate, (7, 6)))), FFN_ROWS)
        for sums, dw_ref, db_ref in ((sums_v, dwv_ref, dbv_ref), (sums_g, dwg_ref, dbg_ref)):
            for tap in range(3):
                dw_ref[tap:tap + 1, :] += _colsum(sums[tap])
            db_ref[0:1, :] += _colsum(sums[3])

    half = jax.ShapeDtypeStruct((seq, D_FF), BF16)
    acc_shape = jax.ShapeDtypeStruct((8, D_FF), F32)
    return _host_call(
        lambda ins, outs, scratch: body(*ins, *outs, *scratch), "ffn_act_bwd", grid=(FFN_BLOCKS, n_tiles),
        in_specs=[tile(0), tile(FFN_BLOCKS), prev(0), prev(FFN_BLOCKS), nxt(0), nxt(FFN_BLOCKS),
                  tile(0), pl.BlockSpec((16, FFN_COLS), lambda j, i: (
                      jnp.minimum((i + 1) * (FFN_ROWS // 16), seq // 16 - 1), j)),
                  wgt(0), wgt(FFN_BLOCKS), vec(0), vec(FFN_BLOCKS)],
        out_specs=[tile(0), tile(0), acc(0), acc(0), acc(0), acc(0)],
        out_shape=[half, half, acc_shape, acc_shape, acc_shape, acc_shape],
        scratch_shapes=[], args=[up, up, up, up, up, up, dact, dact, w_dw, w_dw, b_dw, b_dw], comm=comm)


def _cols_to_blocks(full_cols):
    k, n8 = full_cols.shape
    return jnp.transpose(full_cols.reshape(k, N_DEV, n8 // N_DEV), (1, 0, 2))


def _rows_to_blocks(full_rows):
    r8, n = full_rows.shape
    return full_rows.reshape(N_DEV, r8 // N_DEV, n)


def _blocks_to_cols(gathered):
    _, k, n = gathered.shape
    return jnp.transpose(gathered, (1, 0, 2)).reshape(k, N_DEV * n)


def kernel(x, c, w_ada, b_ada, g_pre_mix, g_post_mix, w_in, b_in, rel_bias, w_attn_o, w_dw_conv, b_dw_conv, g_conv_ln, b_conv_ln, w_conv_o, b_conv_o, w_mix_o, g_pre_ffn, g_post_ffn, w_up, w_dw_ffn, b_dw_ffn, w_down, loss_target, m_w_ada, m_b_ada, m_g_pre_mix, m_g_post_mix, m_w_in, m_b_in, m_rel_bias, m_w_attn_o, m_w_dw_conv, m_b_dw_conv, m_g_conv_ln, m_b_conv_ln, m_w_conv_o, m_b_conv_o, m_w_mix_o, m_g_pre_ffn, m_g_post_ffn, m_w_up, m_w_dw_ffn, m_b_dw_ffn, m_w_down, v_w_ada, v_b_ada, v_g_pre_mix, v_g_post_mix, v_w_in, v_b_in, v_rel_bias, v_w_attn_o, v_w_dw_conv, v_b_dw_conv, v_g_conv_ln, v_b_conv_ln, v_w_conv_o, v_b_conv_o, v_w_mix_o, v_g_pre_ffn, v_g_post_ffn, v_w_up, v_w_dw_ffn, v_b_dw_ffn, v_w_down):
    names = ["w_ada", "b_ada", "g_pre_mix", "g_post_mix", "w_in", "b_in", "rel_bias", "w_attn_o", "w_dw_conv",
             "b_dw_conv", "g_conv_ln", "b_conv_ln", "w_conv_o", "b_conv_o", "w_mix_o", "g_pre_ffn", "g_post_ffn",
             "w_up", "w_dw_ffn", "b_dw_ffn", "w_down"]
    weights = dict(zip(names, [w_ada, b_ada, g_pre_mix, g_post_mix, w_in, b_in, rel_bias, w_attn_o, w_dw_conv,
                               b_dw_conv, g_conv_ln, b_conv_ln, w_conv_o, b_conv_o, w_mix_o, g_pre_ffn,
                               g_post_ffn, w_up, w_dw_ffn, b_dw_ffn, w_down]))
    mom_m = dict(zip(names, [m_w_ada, m_b_ada, m_g_pre_mix, m_g_post_mix, m_w_in, m_b_in, m_rel_bias, m_w_attn_o,
                             m_w_dw_conv, m_b_dw_conv, m_g_conv_ln, m_b_conv_ln, m_w_conv_o, m_b_conv_o,
                             m_w_mix_o, m_g_pre_ffn, m_g_post_ffn, m_w_up, m_w_dw_ffn, m_b_dw_ffn, m_w_down]))
    mom_v = dict(zip(names, [v_w_ada, v_b_ada, v_g_pre_mix, v_g_post_mix, v_w_in, v_b_in, v_rel_bias, v_w_attn_o,
                             v_w_dw_conv, v_b_dw_conv, v_g_conv_ln, v_b_conv_ln, v_w_conv_o, v_b_conv_o,
                             v_w_mix_o, v_g_pre_ffn, v_g_post_ffn, v_w_up, v_w_dw_ffn, v_b_dw_ffn, v_w_down]))
    shapes = {n: w.shape for n, w in weights.items()}

    seq = x.shape[1]
    me = 4 * lax.axis_index("x") + 2 * lax.axis_index("y") + lax.axis_index("c")
    x2 = x.reshape(seq, D_MODEL)
    target = loss_target.reshape(seq, D_MODEL)
    sq = lambda a: a.reshape(a.shape[1:])
    bf = lambda a: sq(a).astype(BF16)

    c_act = _silu_vec(c)
    c_all, g_in, g_dwc, g_dwf = _run_comm(
        _gather_comm([c_act, bf(w_in), sq(w_dw_conv), sq(w_dw_ffn)]), "gather_first")
    c_all = c_all.reshape(N_DEV, D_MODEL)
    wf_in = _blocks_to_cols(g_in)
    wf_dwc = _blocks_to_cols(g_dwc)
    wf_dwf = _blocks_to_cols(g_dwf)

    (mod_all,) = _run_comm(_gather_comm([_ada_fwd(c_all, sq(w_ada))]), "gather_mod")
    mod = lax.dynamic_index_in_dim(mod_all, me, axis=1, keepdims=False)
    mod6 = (mod.reshape(1, 6 * D_MODEL) + b_ada).reshape(6, D_MODEL)

    h1 = _pre_mix(x2, mod6, g_pre_mix)
    qkv = _mm(h1, wf_in[:, :3 * D_ATTN], "nn", BF16, "in_proj_qkv", bias=b_in[:, :3 * D_ATTN], tm=1024, tn=768)
    zr, (g_ao, g_co, g_mo) = _mm(h1, wf_in[:, 3 * D_ATTN:], "nn", F32, "in_proj_rest", bias=b_in[:, 3 * D_ATTN:],
                                 tm=1024, tn=1024, comm=_gather_comm([bf(w_attn_o), bf(w_conv_o), bf(w_mix_o)]))
    kpad = jnp.pad(qkv[:, D_ATTN:2 * D_ATTN], ((PAD_ROWS, 0), (0, 0)))
    vpad = jnp.pad(qkv[:, 2 * D_ATTN:], ((PAD_ROWS, 0), (0, 0)))
    table = jnp.transpose(_bias_table(sq(rel_bias)), (1, 0, 2))
    ao, (g_up,) = _attn_fwd(qkv, kpad, vpad, table, comm=_gather_comm([bf(w_up)]))
    (u1, u3), (g_dn,) = _conv_fwd(zr, wf_dwc, b_dw_conv, g_conv_ln, b_conv_ln, comm=_gather_comm([bf(w_down)]))
    wf_ao = _blocks_to_cols(g_ao)
    wf_co = _blocks_to_cols(g_co)
    wf_mo = g_mo.reshape(D_MODEL, D_MODEL)
    wf_up = _blocks_to_cols(g_up)
    wf_dn = g_dn.reshape(D_FF, D_MODEL)
    y, a_br, cb_br = _merge_fwd(ao, u3, zr, wf_ao, wf_co, b_conv_o)
    ymix = _mm(y, wf_mo, "nn", F32, "mix_o", tm=1024, tn=1024)
    x1, h2 = _post_mix_pre_ffn(ymix, x2, mod6, g_post_mix, g_pre_ffn)
    up = _mm(h2, wf_up, "nn", F32, "ffn_up", tm=1024, tn=1408)
    act = _ffn_act(up, wf_dwf, b_dw_ffn)
    yf = _mm(act, wf_dn, "nn", F32, "ffn_down", tm=512, tn=1024)
    loss_lanes, dout, dyf, small_f = _final(yf, x1, target, mod6, g_post_ffn)
    loss = lax.psum(0.5 * loss_lanes[0, 0], ("x", "y", "c"))

    dact = _mm(dyf, wf_dn, "nt", BF16, "ffn_down_dx", tm=1024, tn=1408)
    gw_down = _mm(act, dyf, "tn", BF16, "ffn_down_dw", tm=256, tn=1024)
    (dup_v, dup_g, dwv, dwg, dbv, dbg), (parts_down,) = _ffn_act_bwd(
        up, dact, wf_dwf, b_dw_ffn, comm=_scatter_comm([_rows_to_blocks(gw_down)]))
    dup = jnp.concatenate([dup_v, dup_g], axis=1)
    dh2 = _mm(dup, wf_up, "nt", F32, "ffn_up_dx")
    gw_up = _mm(h2, dup, "tn", BF16, "ffn_up_dw")
    dx1, dymix, small_m = _mid_bwd(dh2, x1, dout, ymix, mod6, g_pre_ffn, g_post_mix)
    dy = _mm(dymix, wf_mo, "nt", F32, "mix_o_dx", tm=1024, tn=1024)
    gw_mo = _mm(y, dymix, "tn", BF16, "mix_o_dw")
    da, dcb, dga, dgb, small_g = _merge_bwd(dy, a_br, cb_br, zr)
    dao = _mm(da, wf_ao, "nt", BF16, "attn_o_dx", tm=1024)
    gw_ao = _mm(ao, da, "tn", BF16, "attn_o_dw")
    du3 = _mm(dcb, wf_co, "nt", F32, "conv_o_dx", tm=1024)
    gw_co = _mm(u3, dcb, "tn", BF16, "conv_o_dw")
    (dq, dkt, dvt, dbias, small_a), (parts_up,) = _attn_bwd(
        qkv, kpad, vpad, table, dao, comm=_scatter_comm([_cols_to_blocks(gw_up)]))
    g_rel = _bias_grad(jnp.transpose(dbias, (1, 0, 2)))
    (dglu_a, dglu_b, dw_conv, small_c), (parts_mo, parts_ao, parts_co) = _conv_bwd(
        zr, u1, du3, wf_dwc, g_conv_ln, b_conv_ln,
        comm=_scatter_comm([_rows_to_blocks(gw_mo), _cols_to_blocks(gw_ao), _cols_to_blocks(gw_co)]))
    dz = _assemble_dz(dq, dkt, dvt, dglu_a, dglu_b, dga, dgb)
    gw_in = _mm(h1, dz, "tn", BF16, "in_proj_dw")
    dh1, (parts_in,) = _mm(dz, wf_in, "nt", F32, "in_proj_dx", comm=_scatter_comm([_cols_to_blocks(gw_in)]))
    grad_x, small_x = _pre_mix_bwd(dh1, x2, dx1, mod6, g_pre_mix)

    packed = _pack_grads(small_x, small_m, small_f, small_g, small_a, small_c, dbv, dbg, dwv, dwg, dw_conv)
    gathered, gathered_rel = _run_comm(_gather_comm([packed, g_rel]), "gather_small")
    gathered = gathered.reshape(N_DEV, PACKED_TOTAL)
    updates, g_dwc_full, g_dwf_full = _small_adamw(gathered, gathered_rel, weights, mom_m, mom_v)

    grads, deltas, new_m, new_v = {}, {}, {}, {}

    def record(name, update):
        for dst, val in zip((grads, deltas, new_m, new_v), update):
            dst[name] = val.reshape(shapes[name])

    for name, update in updates.items():
        record(name, update)

    def local_update(name, grad):
        record(name, _adamw(sq(weights[name]), sq(mom_m[name]), sq(mom_v[name]), "adamw_" + name, g=grad))

    conv_cols, ffn_cols, ada_cols = D_CONV // N_DEV, 2 * D_FF // N_DEV, 6 * D_MODEL // N_DEV
    local_update("w_dw_conv", lax.dynamic_slice(g_dwc_full, (0, me * conv_cols), (CONV_K, conv_cols)))
    local_update("w_dw_ffn", lax.dynamic_slice(g_dwf_full, (0, me * ffn_cols), (3, ffn_cols)))
    local_update("w_ada", _ada_grad(c_all, lax.dynamic_slice(gathered, (0, me * ada_cols), (N_DEV, ada_cols))))

    for name, part in (("w_in", parts_in), ("w_attn_o", parts_ao), ("w_conv_o", parts_co), ("w_mix_o", parts_mo),
                       ("w_up", parts_up), ("w_down", parts_down)):
        record(name, _adamw(sq(weights[name]), sq(mom_m[name]), sq(mom_v[name]), "adamw_" + name, parts=part))

    return (loss, grad_x.reshape(x.shape), *[grads[n] for n in names], *[deltas[n] for n in names],
            *[new_m[n] for n in names], *[new_v[n] for n in names])
```

```python
import functools
import math

import jax
import jax.numpy as jnp
from jax import lax
from jax.experimental import pallas as pl
from jax.experimental.pallas import tpu as pltpu

F32 = jnp.float32
BF16 = jnp.bfloat16
HIGHEST = lax.Precision.HIGHEST

D_MODEL = 1024
CHUNK = 64
LEFT_CHUNKS = 8
BAND = (LEFT_CHUNKS + 1) * CHUNK
PAD_ROWS = LEFT_CHUNKS * CHUNK
GROUP = 4
GROUP_Q = GROUP * CHUNK
GROUP_K = GROUP_Q + PAD_ROWS
SOFTMAX_ROWS = 16
TOEPLITZ = 640
N_HEADS = 8
HEAD_DIM = 64
D_ATTN = 512
D_CONV = 512
CONV_K = 31
CONV_HALO = 32
MAX_REL = 128
N_REL = 2 * MAX_REL + 1
D_FF = 2816
FFN_HALO = 8
FFN_COLS = 256
EPS = 1e-6
NEG_INF = -1e30
N_DEV = 8

ADAM_LR = 0.001
ADAM_B1 = 0.9
ADAM_B2 = 0.999
ADAM_EPS = 1e-08
ADAM_WD = 0.01
ADAM_STEP = 10

VMEM_LIMIT_BYTES = 56 * 1024 * 1024
ADAMW_BLOCK_BYTES = 768 * 1024

MESH = pl.DeviceIdType.MESH
ANY = pl.BlockSpec(memory_space=pl.ANY)

SH_M, SC_M, GT_M, SH_F, SC_F, GT_F = range(6)

SMALL = (("b_ada", 6144), ("g_pre_mix", 1024), ("g_post_mix", 1024), ("b_in", 4608), ("b_dw_conv", 512),
         ("g_conv_ln", 512), ("b_conv_ln", 512), ("b_conv_o", 1024), ("g_pre_ffn", 1024), ("g_post_ffn", 1024),
         ("b_dw_ffn", 5632))
PACKED_TOTAL = sum(n for _, n in SMALL) + CONV_K * D_CONV + 3 * 2 * D_FF


def _cparams(n_axes):
    return pltpu.CompilerParams(vmem_limit_bytes=VMEM_LIMIT_BYTES,
                                dimension_semantics=("arbitrary",) * n_axes)


def _sig(v):
    return 1.0 / (1.0 + jnp.exp(-v))


def _pick(n, target):
    if n <= target:
        return n
    t = target - target % 128
    while n % t:
        t -= 128
    return t


def _tile(rows, cols, col=0):
    return pl.BlockSpec((rows, cols), lambda i: (i, col))


def _full(shape):
    zeros = (0,) * len(shape)
    return pl.BlockSpec(shape, lambda i: zeros)


def _prev(halo, cols, rows, col=0):
    return pl.BlockSpec((halo, cols), lambda i: (jnp.maximum(i * (rows // halo) - 1, 0), col))


def _next(halo, cols, rows, n_blocks, col=0):
    return pl.BlockSpec((halo, cols), lambda i: (jnp.minimum((i + 1) * (rows // halo), n_blocks - 1), col))


class _Comm:
    def __init__(self, inputs, out_shapes, sems, start, finish):
        self.inputs, self.out_shapes, self.sems, self.start, self.finish = inputs, out_shapes, sems, start, finish


def _host_call(body, name, grid, in_specs, out_specs, out_shape, scratch_shapes, args, comm=None):
    n_in, n_out, n_scr = len(args), len(out_shape), len(scratch_shapes)
    c_in = list(comm.inputs) if comm else []
    c_out = list(comm.out_shapes) if comm else []
    c_sem = list(comm.sems) if comm else []

    def full(*refs):
        bounds = [0, n_in, len(c_in), n_out, len(c_out), n_scr, len(c_sem)]
        cuts = [sum(bounds[:i + 1]) for i in range(len(bounds))]
        ins, cins, outs, couts, scr, csems = (refs[lo:hi] for lo, hi in zip(cuts[:-1], cuts[1:]))
        if comm:
            first = functools.reduce(jnp.logical_and, [pl.program_id(ax) == 0 for ax in range(len(grid))])
            pl.when(first)(lambda: comm.start(cins, couts, csems))
        body(ins, outs, scr)
        if comm:
            last = functools.reduce(jnp.logical_and, [pl.program_id(ax) == grid[ax] - 1 for ax in range(len(grid))])
            pl.when(last)(lambda: comm.finish(cins, couts, csems))

    res = pl.pallas_call(
        full, name=name, grid=grid, in_specs=list(in_specs) + [ANY] * len(c_in),
        out_specs=list(out_specs) + [ANY] * len(c_out), out_shape=list(out_shape) + c_out,
        scratch_shapes=list(scratch_shapes) + c_sem, compiler_params=_cparams(len(grid)),
    )(*args, *c_in)
    return list(res[:n_out]), list(res[n_out:])


def _run_comm(comm, name):
    n_in, n_out = len(comm.inputs), len(comm.out_shapes)

    def body(*refs):
        ins, outs, sems = refs[:n_in], refs[n_in:n_in + n_out], refs[n_in + n_out:]
        comm.start(ins, outs, sems)
        comm.finish(ins, outs, sems)

    return pl.pallas_call(
        body, name=name, out_shape=list(comm.out_shapes), in_specs=[ANY] * n_in, out_specs=[ANY] * n_out,
        scratch_shapes=list(comm.sems),
    )(*comm.inputs)


def _place():
    return lax.axis_index("x"), lax.axis_index("y"), lax.axis_index("c")


def _gather_comm(arrs):
    n = len(arrs)

    def plan(ins, outs, sems):
        send_sems, recv_sems, local_sems = sems
        x, y, c = _place()
        me, sibling = (x, y, c), (x, y, 1 - c)
        chips = [(1 - x, y), (x, 1 - y), (1 - x, 1 - y)]

        def block(k, p):
            return outs[k].at[4 * p[0] + 2 * p[1] + p[2]]

        def copy(k, s, blk, to, src=None):
            return pltpu.make_async_remote_copy(
                src_ref=block(k, blk) if src is None else src, dst_ref=block(k, blk),
                send_sem=send_sems.at[7 * k + s], recv_sem=recv_sems.at[7 * k + s],
                device_id=to, device_id_type=MESH)

        mine = [pltpu.make_async_copy(ins[k], block(k, me), local_sems.at[k]) for k in range(n)]
        first = []
        for k in range(n):
            first.append(copy(k, 0, me, sibling, src=ins[k]))
            for j, chip in enumerate(chips):
                first.append(copy(k, 1 + j, me, (*chip, c), src=ins[k]))
        return me, sibling, chips, c, copy, mine, first

    def start(ins, outs, sems):
        *_, mine, first = plan(ins, outs, sems)
        for cp in mine + first:
            cp.start()

    def finish(ins, outs, sems):
        me, sibling, chips, c, copy, mine, first = plan(ins, outs, sems)
        passed = []
        for j, chip in enumerate(chips):
            for k in range(n):
                copy(k, 1 + j, (*chip, c), me).wait_recv()
                fwd = copy(k, 4 + j, (*chip, c), sibling)
                fwd.start()
                passed.append(fwd)
        for k in range(n):
            copy(k, 0, sibling, me).wait_recv()
        for j, chip in enumerate(chips):
            for k in range(n):
                copy(k, 4 + j, (*chip, 1 - c), me).wait_recv()
        for cp in first + passed:
            cp.wait_send()
        for cp in mine:
            cp.wait()

    return _Comm(list(arrs), [jax.ShapeDtypeStruct((N_DEV,) + a.shape, a.dtype) for a in arrs],
                 [pltpu.SemaphoreType.DMA((7 * n,)), pltpu.SemaphoreType.DMA((7 * n,)),
                  pltpu.SemaphoreType.DMA((n,))], start, finish)


def _scatter_comm(blocks):
    n = len(blocks)

    def plan(ins, outs, sems, arrivals):
        send_sems, recv_sems, local_sems = sems
        x, y, c = _place()
        me = 4 * x + 2 * y + c
        local = [pltpu.make_async_copy(ins[k].at[me], outs[k].at[me], local_sems.at[k]) for k in range(n)]
        sends, recvs = [], []
        for k in range(n):
            for mask in range(1, N_DEV):
                px = 1 - x if mask & 4 else x
                py = 1 - y if mask & 2 else y
                pc = 1 - c if mask & 1 else c
                peer = 4 * px + 2 * py + pc
                sem = 7 * k + mask - 1
                both = dict(send_sem=send_sems.at[sem], recv_sem=recv_sems.at[sem], device_id=(px, py, pc),
                            device_id_type=MESH)
                sends.append(pltpu.make_async_remote_copy(src_ref=ins[k].at[peer], dst_ref=outs[k].at[me], **both))
                if arrivals:
                    recvs.append(pltpu.make_async_remote_copy(src_ref=ins[k].at[me], dst_ref=outs[k].at[peer],
                                                              **both))
        return local, sends, recvs

    def start(ins, outs, sems):
        local, sends, _ = plan(ins, outs, sems, arrivals=False)
        for cp in local + sends:
            cp.start()

    def finish(ins, outs, sems):
        local, sends, recvs = plan(ins, outs, sems, arrivals=True)
        for cp in recvs:
            cp.wait_recv()
        for cp in sends:
            cp.wait_send()
        for cp in local:
            cp.wait()

    return _Comm(list(blocks), [jax.ShapeDtypeStruct(b.shape, b.dtype) for b in blocks],
                 [pltpu.SemaphoreType.DMA((7 * n,)), pltpu.SemaphoreType.DMA((7 * n,)),
                  pltpu.SemaphoreType.DMA((n,))], start, finish)


_DIMS = {"nn": (((1,), (0,)), ((), ())), "nt": (((1,), (1,)), ((), ())), "tn": (((0,), (0,)), ((), ()))}


def _mm(a, b, mode, out_dtype, name, bias=None, tm=512, tn=512, comm=None, cols=None):
    pieces = a if isinstance(a, (list, tuple)) else [a]
    assert all(p.dtype == BF16 for p in pieces) and b.dtype == BF16
    a = pieces[0]
    if mode == "tn":
        k_dim, m_dim = a.shape
    else:
        m_dim, k_dim = a.shape
    n_dim = b.shape[0] if mode == "nt" else b.shape[1]
    col0 = 0
    if cols is not None:
        assert mode == "nn" and cols[0] % tn == 0 and cols[1] % tn == 0
        col0, n_dim = cols[0] // tn, cols[1]
    tm, tn = _pick(m_dim, tm), _pick(n_dim, tn)
    a_specs = [pl.BlockSpec((k_dim, tm), lambda i, j: (0, i)) if mode == "tn"
               else pl.BlockSpec((tm, k_dim), lambda i, j: (i, 0))] * len(pieces)
    if mode == "nt":
        b_specs = [pl.BlockSpec((tn, k_dim), lambda i, j, p=p: (j, p)) for p in range(len(pieces))]
    else:
        assert len(pieces) == 1
        b_specs = [pl.BlockSpec((k_dim, tn), lambda i, j: (0, j + col0))]
    in_specs = a_specs + b_specs
    args = list(pieces) + [b] * len(pieces)
    if bias is not None:
        in_specs.append(pl.BlockSpec((1, tn), lambda i, j: (0, j + col0)))
        args.append(bias)
    dims = _DIMS[mode]
    n_pieces = len(pieces)

    def body(ins, outs, scratch):
        total = lax.dot_general(ins[0][...], ins[n_pieces][...], dims, preferred_element_type=F32)
        for p in range(1, n_pieces):
            total = total + lax.dot_general(ins[p][...], ins[n_pieces + p][...], dims, preferred_element_type=F32)
        if bias is not None:
            total = total + ins[2 * n_pieces][...]
        outs[0][...] = total.astype(out_dtype)

    (out,), extra = _host_call(
        body, name, grid=(m_dim // tm, n_dim // tn), in_specs=in_specs,
        out_specs=[pl.BlockSpec((tm, tn), lambda i, j: (i, j))],
        out_shape=[jax.ShapeDtypeStruct((m_dim, n_dim), out_dtype)], scratch_shapes=[], args=args, comm=comm)
    return out if comm is None else (out, extra)


def _mm_tn_blocks(a, bs, n_dev_cols, name, tm=512, k_steps=1):
    k_dim, m_dim = a.shape
    n = n_dev_cols
    pair = 2 * n
    assert pair % 128 == 0 and all(b.shape[1] % pair == 0 for b in bs) and k_dim % k_steps == 0
    counts = [b.shape[1] // pair for b in bs]
    firsts = [sum(counts[:q]) for q in range(len(bs))]
    assert sum(counts) == N_DEV // 2
    tm, tk = _pick(m_dim, tm), k_dim // k_steps

    def b_spec(first, count):
        return pl.BlockSpec((tk, pair), lambda i, j, k: (k, jnp.clip(j - first, 0, count - 1)))

    def body(ins, outs, scratch):
        a_ref, b_refs, o_ref, s_ref = ins[0], ins[1:], outs[0], scratch[0]
        j, k = pl.program_id(1), pl.program_id(2)
        for b_ref, first, count in zip(b_refs, firsts, counts):
            @pl.when(jnp.logical_and(j >= first, j < first + count))
            def _(b_ref=b_ref):
                part = lax.dot_general(a_ref[...], b_ref[...], _DIMS["tn"], preferred_element_type=F32)
                if k_steps == 1:
                    s_ref[...] = part
                else:
                    @pl.when(k == 0)
                    def _():
                        s_ref[...] = part

                    @pl.when(k > 0)
                    def _():
                        s_ref[...] += part

        @pl.when(k == k_steps - 1)
        def _():
            o_ref[0] = s_ref[:, 0:n].astype(BF16)
            o_ref[1] = s_ref[:, n:pair].astype(BF16)

    (out,), _ = _host_call(
        body, name, grid=(m_dim // tm, N_DEV // 2, k_steps),
        in_specs=[pl.BlockSpec((tk, tm), lambda i, j, k: (k, i))] + [b_spec(f, c) for f, c in zip(firsts, counts)],
        out_specs=[pl.BlockSpec((2, tm, n), lambda i, j, k: (j, i, 0))],
        out_shape=[jax.ShapeDtypeStruct((N_DEV, m_dim, n), BF16)],
        scratch_shapes=[pltpu.VMEM((tm, pair), F32)], args=[a] + list(bs))
    return out


def _adam_math(w, g, m, v):
    m = ADAM_B1 * m + (1.0 - ADAM_B1) * g
    v = ADAM_B2 * v + (1.0 - ADAM_B2) * (g * g)
    m_hat = m / (1.0 - ADAM_B1 ** ADAM_STEP)
    v_hat = v / (1.0 - ADAM_B2 ** ADAM_STEP)
    delta = -ADAM_LR * (m_hat / (jnp.sqrt(v_hat) + ADAM_EPS) + ADAM_WD * w)
    return delta, m, v


def _adamw(w, m, v, name, g=None, parts=None):
    rows, cols = w.shape
    tr = rows
    if rows * cols * 4 > ADAMW_BLOCK_BYTES:
        tr = max(t for t in range(16, rows, 16) if rows % t == 0 and t * cols * 4 <= ADAMW_BLOCK_BYTES)

    def body(w_ref, m_ref, v_ref, g_ref, go_ref, d_ref, mo_ref, vo_ref):
        if parts is None:
            grad = g_ref[...]
        else:
            grad = g_ref[0].astype(F32)
            for d in range(1, N_DEV):
                grad = grad + g_ref[d].astype(F32)
        delta, m_new, v_new = _adam_math(w_ref[...], grad, m_ref[...], v_ref[...])
        go_ref[...] = grad
        d_ref[...] = delta
        mo_ref[...] = m_new
        vo_ref[...] = v_new

    spec = _tile(tr, cols)
    g_spec = spec if parts is None else pl.BlockSpec((N_DEV, tr, cols), lambda i: (0, i, 0))
    shape = jax.ShapeDtypeStruct((rows, cols), F32)
    return pl.pallas_call(
        body, name=name, out_shape=[shape] * 4, grid=(rows // tr,),
        in_specs=[spec, spec, spec, g_spec], out_specs=[spec] * 4, compiler_params=_cparams(1),
    )(w, m, v, g if parts is None else parts)


def _pack_grads(small_x, small_m, small_f, small_g, small_a, small_c, dbv, dbg, dwv, dwg, dw_conv):
    pieces = [
        (small_x, 2, D_MODEL), (small_x, 1, D_MODEL), (small_m, 4, D_MODEL), (small_m, 2, D_MODEL),
        (small_m, 1, D_MODEL), (small_f, 1, D_MODEL),
        (small_x, 0, D_MODEL), (small_m, 3, D_MODEL),
        (small_a, 0, D_ATTN), (small_a, 1, D_ATTN), (small_a, 2, D_ATTN), (small_c, 3, D_CONV),
        (small_c, 4, D_CONV), (small_g, 0, D_MODEL), (small_g, 1, D_MODEL),
        (small_c, 0, D_CONV), (small_c, 1, D_CONV), (small_c, 2, D_CONV),
        (small_g, 2, D_MODEL), (small_m, 0, D_MODEL), (small_f, 0, D_MODEL),
        (dbv, 0, D_FF), (dbg, 0, D_FF),
    ]
    pieces += [(dw_conv, j, D_CONV) for j in range(CONV_K)]
    pieces += [(src, tap, D_FF) for tap in range(3) for src in (dwv, dwg)]
    sources = [small_x, small_m, small_f, small_g, small_a, small_c, dbv, dbg, dwv, dwg, dw_conv]
    assert sum(width for _, _, width in pieces) == PACKED_TOTAL

    def body(*refs):
        o_ref = refs[-1]
        ref_of = {id(src): ref for src, ref in zip(sources, refs)}
        off = 0
        for src, row, width in pieces:
            o_ref[:, off:off + width] = ref_of[id(src)][row:row + 1, :]
            off += width

    return pl.pallas_call(body, name="pack_grads", out_shape=jax.ShapeDtypeStruct((1, PACKED_TOTAL), F32))(*sources)


def _small_adamw(gathered, gathered_rel, gathered_loss, weights, mom_m, mom_v):
    vec_names = [name for name, _ in SMALL]
    states = []
    for name in vec_names + ["rel_bias"]:
        states += [weights[name], mom_m[name], mom_v[name]]
    states = [a.reshape(a.shape[1:]) if a.ndim == 3 else a for a in states]
    n_state = len(states)

    def body(*refs):
        g_ref, rel_ref, loss_ref = refs[0], refs[1], refs[2]
        state_refs, out_refs = refs[3:3 + n_state], refs[3 + n_state:]
        total = g_ref[0:1, :]
        rel = rel_ref[0]
        loss = loss_ref[0]
        for d in range(1, N_DEV):
            total = total + g_ref[d:d + 1, :]
            rel = rel + rel_ref[d]
            loss = loss + loss_ref[d]
        off = 0
        for n, (name, width) in enumerate(SMALL):
            grad = total[:, off:off + width]
            w_ref, m_ref, v_ref = state_refs[3 * n:3 * n + 3]
            for ref, val in zip(out_refs[4 * n:4 * n + 4], (grad,) + _adam_math(w_ref[...], grad, m_ref[...], v_ref[...])):
                ref[...] = val
            off += width
        n = len(SMALL)
        w_ref, m_ref, v_ref = state_refs[3 * n:3 * n + 3]
        for ref, val in zip(out_refs[4 * n:4 * n + 4], (rel,) + _adam_math(w_ref[...], rel, m_ref[...], v_ref[...])):
            ref[...] = val
        dwc_ref, dwf_ref, loss_out = out_refs[4 * n + 4:]
        loss_out[...] = 0.5 * loss
        dwc_ref[...] = jnp.zeros_like(dwc_ref)
        dwf_ref[...] = jnp.zeros_like(dwf_ref)
        for j in range(CONV_K):
            dwc_ref[j:j + 1, :] = total[:, off:off + D_CONV]
            off += D_CONV
        for tap in range(3):
            dwf_ref[tap:tap + 1, :] = total[:, off:off + 2 * D_FF]
            off += 2 * D_FF

    out_shape = []
    for k in range(n_state // 3):
        out_shape += [jax.ShapeDtypeStruct(states[3 * k].shape, F32)] * 4
    out_shape += [jax.ShapeDtypeStruct((CONV_HALO, D_CONV), F32), jax.ShapeDtypeStruct((8, 2 * D_FF), F32),
                  jax.ShapeDtypeStruct((1, 128), F32)]
    res = pl.pallas_call(
        body, name="small_adamw", out_shape=out_shape,
        compiler_params=pltpu.CompilerParams(vmem_limit_bytes=VMEM_LIMIT_BYTES),
    )(gathered, gathered_rel, gathered_loss, *states)
    updates = {name: tuple(res[4 * n:4 * n + 4]) for n, name in enumerate(vec_names + ["rel_bias"])}
    return updates, res[-3], res[-2], res[-1]


def _silu_vec(c):
    def body(c_ref, o_ref):
        v = c_ref[...]
        o_ref[...] = v * _sig(v)

    return pl.pallas_call(body, name="silu_c", out_shape=jax.ShapeDtypeStruct(c.shape, F32))(c)


def _ada_fwd(c_all, w_shard):
    def body(c_ref, w_ref, o_ref):
        o_ref[...] = jnp.dot(c_ref[...], w_ref[...], precision=HIGHEST, preferred_element_type=F32)

    return pl.pallas_call(
        body, name="ada_fwd", out_shape=jax.ShapeDtypeStruct((N_DEV, w_shard.shape[1]), F32),
        compiler_params=pltpu.CompilerParams(vmem_limit_bytes=VMEM_LIMIT_BYTES),
    )(c_all, w_shard)


def _ada_grad(c_all, dmod_shard):
    def body(c_ref, d_ref, o_ref):
        o_ref[...] = lax.dot_general(c_ref[...], d_ref[...], _DIMS["tn"], precision=HIGHEST,
                                     preferred_element_type=F32)

    return pl.pallas_call(
        body, name="ada_grad", out_shape=jax.ShapeDtypeStruct((D_MODEL, dmod_shard.shape[1]), F32),
        compiler_params=pltpu.CompilerParams(vmem_limit_bytes=VMEM_LIMIT_BYTES),
    )(c_all, dmod_shard)


ROWS = 256


def _rms(v):
    r = lax.rsqrt(jnp.mean(v * v, axis=-1, keepdims=True) + EPS)
    return v * r, r


def _rms_bwd(dxn, xn, r):
    return r * (dxn - xn * jnp.mean(dxn * xn, axis=-1, keepdims=True))


def _colsum(v):
    return jnp.sum(v, axis=0, keepdims=True)


def _pre_mix(x, mod6, g1):
    seq = x.shape[0]

    def body(x_ref, mod_ref, g_ref, h_ref):
        xn, _ = _rms(x_ref[...])
        y = xn * g_ref[...]
        h_ref[...] = (y * (1.0 + mod_ref[SC_M:SC_M + 1, :]) + mod_ref[SH_M:SH_M + 1, :]).astype(BF16)

    return pl.pallas_call(
        body, name="pre_mix", out_shape=jax.ShapeDtypeStruct((seq, D_MODEL), BF16), grid=(seq // ROWS,),
        in_specs=[_tile(ROWS, D_MODEL), _full((6, D_MODEL)), _full((1, D_MODEL))],
        out_specs=_tile(ROWS, D_MODEL), compiler_params=_cparams(1),
    )(x, mod6, g1)


def _post_mix_pre_ffn(ymix, x, mod6, g2, g3):
    seq = x.shape[0]

    def body(y_ref, x_ref, mod_ref, g2_ref, g3_ref, x1_ref, h_ref):
        yn, _ = _rms(y_ref[...])
        x1 = x_ref[...] + mod_ref[GT_M:GT_M + 1, :] * (yn * g2_ref[...])
        x1_ref[...] = x1
        xn, _ = _rms(x1)
        y3 = xn * g3_ref[...]
        h_ref[...] = (y3 * (1.0 + mod_ref[SC_F:SC_F + 1, :]) + mod_ref[SH_F:SH_F + 1, :]).astype(BF16)

    return pl.pallas_call(
        body, name="post_mix_pre_ffn",
        out_shape=[jax.ShapeDtypeStruct((seq, D_MODEL), F32), jax.ShapeDtypeStruct((seq, D_MODEL), BF16)],
        grid=(seq // ROWS,),
        in_specs=[_tile(ROWS, D_MODEL), _tile(ROWS, D_MODEL), _full((6, D_MODEL)), _full((1, D_MODEL)),
                  _full((1, D_MODEL))],
        out_specs=[_tile(ROWS, D_MODEL), _tile(ROWS, D_MODEL)], compiler_params=_cparams(1),
    )(ymix, x, mod6, g2, g3)


def _final(yf, x1, target, mod6, g4):
    seq = x1.shape[0]

    def body(y_ref, x1_ref, t_ref, mod_ref, g_ref, loss_ref, dout_ref, dyf_ref, small_ref):
        i = pl.program_id(0)

        @pl.when(i == 0)
        def _():
            loss_ref[...] = jnp.zeros_like(loss_ref)
            small_ref[...] = jnp.zeros_like(small_ref)

        gt = mod_ref[GT_F:GT_F + 1, :]
        g4v = g_ref[...]
        yn, r = _rms(y_ref[...])
        out = x1_ref[...] + gt * (yn * g4v)
        err = out - t_ref[...]
        loss_ref[...] += jnp.sum(jnp.mean(err * err, axis=-1, keepdims=True))
        dout = err * (1.0 / D_MODEL)
        dout_ref[...] = dout
        small_ref[0:1, :] += _colsum(dout * gt * yn)
        small_ref[1:2, :] += _colsum(dout * (yn * g4v))
        dyf_ref[...] = _rms_bwd(dout * gt * g4v, yn, r).astype(BF16)

    return pl.pallas_call(
        body, name="final_loss",
        out_shape=[jax.ShapeDtypeStruct((1, 128), F32), jax.ShapeDtypeStruct((seq, D_MODEL), F32),
                   jax.ShapeDtypeStruct((seq, D_MODEL), BF16), jax.ShapeDtypeStruct((8, D_MODEL), F32)],
        grid=(seq // ROWS,),
        in_specs=[_tile(ROWS, D_MODEL)] * 3 + [_full((6, D_MODEL)), _full((1, D_MODEL))],
        out_specs=[_full((1, 128)), _tile(ROWS, D_MODEL), _tile(ROWS, D_MODEL), _full((8, D_MODEL))],
        compiler_params=_cparams(1),
    )(yf, x1, target, mod6, g4)


def _mid_bwd(dh2, x1, dout, ymix, mod6, g3, g2):
    seq = x1.shape[0]

    def body(dh_ref, x1_ref, dout_ref, y_ref, mod_ref, g3_ref, g2_ref, dx1_ref, dy_ref, small_ref):
        i = pl.program_id(0)

        @pl.when(i == 0)
        def _():
            small_ref[...] = jnp.zeros_like(small_ref)

        dh = dh_ref[...]
        g3v, g2v = g3_ref[...], g2_ref[...]
        xn, r3 = _rms(x1_ref[...])
        y3 = xn * g3v
        dy3 = dh * (1.0 + mod_ref[SC_F:SC_F + 1, :])
        small_ref[0:1, :] += _colsum(dy3 * xn)
        small_ref[1:2, :] += _colsum(dh * y3)
        small_ref[2:3, :] += _colsum(dh)
        dx1 = dout_ref[...] + _rms_bwd(dy3 * g3v, xn, r3)
        dx1_ref[...] = dx1
        gt = mod_ref[GT_M:GT_M + 1, :]
        yn, r2 = _rms(y_ref[...])
        small_ref[3:4, :] += _colsum(dx1 * gt * yn)
        small_ref[4:5, :] += _colsum(dx1 * (yn * g2v))
        dy_ref[...] = _rms_bwd(dx1 * gt * g2v, yn, r2).astype(BF16)

    return pl.pallas_call(
        body, name="mid_bwd",
        out_shape=[jax.ShapeDtypeStruct((seq, D_MODEL), F32), jax.ShapeDtypeStruct((seq, D_MODEL), BF16),
                   jax.ShapeDtypeStruct((8, D_MODEL), F32)],
        grid=(seq // ROWS,),
        in_specs=[_tile(ROWS, D_MODEL)] * 4 + [_full((6, D_MODEL)), _full((1, D_MODEL)), _full((1, D_MODEL))],
        out_specs=[_tile(ROWS, D_MODEL), _tile(ROWS, D_MODEL), _full((8, D_MODEL))],
        compiler_params=_cparams(1),
    )(dh2, x1, dout, ymix, mod6, g3, g2)


def _pre_mix_bwd(dh1, x, dx1, mod6, g1):
    seq = x.shape[0]

    def body(dh_ref, x_ref, dx1_ref, mod_ref, g_ref, dx_ref, small_ref):
        i = pl.program_id(0)

        @pl.when(i == 0)
        def _():
            small_ref[...] = jnp.zeros_like(small_ref)

        dh = dh_ref[...]
        g1v = g_ref[...]
        xn, r = _rms(x_ref[...])
        dy = dh * (1.0 + mod_ref[SC_M:SC_M + 1, :])
        small_ref[0:1, :] += _colsum(dy * xn)
        small_ref[1:2, :] += _colsum(dh * (xn * g1v))
        small_ref[2:3, :] += _colsum(dh)
        dx_ref[...] = dx1_ref[...] + _rms_bwd(dy * g1v, xn, r)

    return pl.pallas_call(
        body, name="pre_mix_bwd",
        out_shape=[jax.ShapeDtypeStruct((seq, D_MODEL), F32), jax.ShapeDtypeStruct((8, D_MODEL), F32)],
        grid=(seq // ROWS,),
        in_specs=[_tile(ROWS, D_MODEL)] * 3 + [_full((6, D_MODEL)), _full((1, D_MODEL))],
        out_specs=[_tile(ROWS, D_MODEL), _full((8, D_MODEL))], compiler_params=_cparams(1),
    )(dh1, x, dx1, mod6, g1)


def _toeplitz_onehot(shape, offset_axis, top):
    m = lax.broadcasted_iota(jnp.int32, shape, offset_axis)
    i = lax.broadcasted_iota(jnp.int32, shape, 1 - offset_axis)
    return (i == jnp.clip(top - m, -MAX_REL, MAX_REL) + MAX_REL).astype(F32)


def _bias_table(rel_bias):
    width = GROUP_Q + GROUP_K

    def body(rb_ref, o_ref, t_ref):
        t_ref[...] = jnp.dot(rb_ref[...], _toeplitz_onehot((N_REL, width), 1, GROUP_K - 1), precision=HIGHEST,
                             preferred_element_type=F32)
        lane = lax.broadcasted_iota(jnp.int32, (N_HEADS, GROUP_K), 1)
        for r in range(GROUP_Q):
            first_key = (r // CHUNK) * CHUNK
            band = jnp.logical_and(lane >= first_key, lane < first_key + BAND)
            o_ref[r] = jnp.where(band, t_ref[:, GROUP_Q - 1 - r:GROUP_Q - 1 - r + GROUP_K], NEG_INF)

    return pl.pallas_call(
        body, name="bias_table", out_shape=jax.ShapeDtypeStruct((GROUP_Q, N_HEADS, GROUP_K), F32),
        scratch_shapes=[pltpu.VMEM((N_HEADS, width), F32)],
    )(rel_bias)


def _bias_grad(dbias_q):
    def body(d_ref, o_ref, t_ref):
        t_ref[...] = jnp.zeros_like(t_ref)
        for qi in range(CHUNK):
            t_ref[:, CHUNK - 1 - qi:CHUNK - 1 - qi + BAND] += d_ref[qi]
        o_ref[...] = jnp.dot(t_ref[...], _toeplitz_onehot((TOEPLITZ, N_REL), 0, BAND - 1), precision=HIGHEST,
                             preferred_element_type=F32)

    return pl.pallas_call(
        body, name="bias_grad", out_shape=jax.ShapeDtypeStruct((N_HEADS, N_REL), F32),
        scratch_shapes=[pltpu.VMEM((N_HEADS, TOEPLITZ), F32)],
    )(dbias_q)


def _load_resident(pairs, sems):
    copies = [pltpu.make_async_copy(src, dst, sems.at[n]) for n, (src, dst) in enumerate(pairs)]
    for cp in copies:
        cp.start()
    for cp in copies:
        cp.wait()


def _softmax_rows(s_ref, t_ref, valid, rows):
    s = s_ref[rows, :] * (HEAD_DIM ** -0.5) + t_ref[rows, :]
    s = jnp.where(valid, s, NEG_INF)
    e = jnp.exp(s - jnp.max(s, axis=-1, keepdims=True))
    return e / jnp.sum(e, axis=-1, keepdims=True)


def _valid_keys(g):
    kj = lax.broadcasted_iota(jnp.int32, (SOFTMAX_ROWS, GROUP_K), 1)
    return kj >= PAD_ROWS - g * GROUP_Q


def _attn_fwd(qkv, kpad, vpad, table, comm=None):
    seq = qkv.shape[0]

    def body(ins, outs, scratch):
        q_ref, k_hbm, v_hbm, t_hbm = ins
        (o_ref,) = outs
        k_ref, v_ref, t_ref, s_ref, p_ref, sems = scratch
        g = pl.program_id(0)

        @pl.when(g == 0)
        def _():
            _load_resident(((k_hbm, k_ref), (v_hbm, v_ref), (t_hbm, t_ref)), sems)

        window = pl.ds(pl.multiple_of(g * GROUP_Q, GROUP_Q), GROUP_K)
        valid = _valid_keys(g)
        for h in range(N_HEADS):
            cols = slice(h * HEAD_DIM, (h + 1) * HEAD_DIM)
            buf = h % 2
            s_ref[buf] = lax.dot_general(q_ref[:, cols], k_ref[window, cols], _DIMS["nt"],
                                         preferred_element_type=F32)
            for r in range(GROUP_Q // SOFTMAX_ROWS):
                rows = slice(r * SOFTMAX_ROWS, (r + 1) * SOFTMAX_ROWS)
                p_ref[buf, rows, :] = _softmax_rows(s_ref.at[buf], t_ref.at[h], valid, rows).astype(BF16)
            o_ref[:, cols] = jnp.dot(p_ref[buf], v_ref[window, cols], preferred_element_type=F32).astype(BF16)

    (ao,), extra = _host_call(
        body, "attn_fwd", grid=(seq // GROUP_Q,),
        in_specs=[_tile(GROUP_Q, D_ATTN), ANY, ANY, ANY], out_specs=[_tile(GROUP_Q, D_ATTN)],
        out_shape=[jax.ShapeDtypeStruct((seq, D_ATTN), BF16)],
        scratch_shapes=[pltpu.VMEM(kpad.shape, BF16), pltpu.VMEM(vpad.shape, BF16), pltpu.VMEM(table.shape, F32),
                        pltpu.VMEM((2, GROUP_Q, GROUP_K), F32), pltpu.VMEM((2, GROUP_Q, GROUP_K), BF16),
                        pltpu.SemaphoreType.DMA((3,))],
        args=[qkv, kpad, vpad, table], comm=comm)
    return ao, extra


def _attn_bwd(qkv, kpad, vpad, table, dao, comm=None):
    seq = qkv.shape[0]
    n_groups = seq // GROUP_Q
    fold_w = GROUP_K + (GROUP - 1) * CHUNK

    def body(ins, outs, scratch):
        q_ref, do_ref, k_hbm, v_hbm, t_hbm = ins
        dq_ref, dkt_hbm, dvt_hbm, db_ref, cs_ref = outs
        k_ref, v_ref, t_ref, db_acc, dkt_acc, dvt_acc, s_ref, dp_ref, p_ref, ds_ref, sems = scratch
        g = pl.program_id(0)

        @pl.when(g == 0)
        def _():
            _load_resident(((k_hbm, k_ref), (v_hbm, v_ref), (t_hbm, t_ref)), sems)
            db_acc[...] = jnp.zeros_like(db_acc)
            dkt_acc[...] = jnp.zeros_like(dkt_acc)
            dvt_acc[...] = jnp.zeros_like(dvt_acc)
            cs_ref[...] = jnp.zeros_like(cs_ref)

        window = pl.ds(pl.multiple_of(g * GROUP_Q, GROUP_Q), GROUP_K)
        valid = _valid_keys(g)
        for h in range(N_HEADS):
            cols = slice(h * HEAD_DIM, (h + 1) * HEAD_DIM)
            buf = h % 2
            qh, doh = q_ref[:, cols], do_ref[:, cols]
            kh, vh = k_ref[window, cols], v_ref[window, cols]
            s_ref[buf] = lax.dot_general(qh, kh, _DIMS["nt"], preferred_element_type=F32)
            dp_ref[buf] = lax.dot_general(doh, vh, _DIMS["nt"], preferred_element_type=F32)
            for r in range(GROUP_Q // SOFTMAX_ROWS):
                rows = slice(r * SOFTMAX_ROWS, (r + 1) * SOFTMAX_ROWS)
                p = _softmax_rows(s_ref.at[buf], t_ref.at[h], valid, rows)
                dp = dp_ref[buf, rows, :]
                ds = p * (dp - jnp.sum(dp * p, axis=-1, keepdims=True))
                chunk = (r * SOFTMAX_ROWS) // CHUNK
                shift = (GROUP - 1 - chunk) * CHUNK
                local = slice(r * SOFTMAX_ROWS - chunk * CHUNK, (r + 1) * SOFTMAX_ROWS - chunk * CHUNK)
                db_acc[h, local, shift:shift + GROUP_K] += ds
                p_ref[buf, rows, :] = p.astype(BF16)
                ds_ref[buf, rows, :] = (ds * (HEAD_DIM ** -0.5)).astype(BF16)
            dq_ref[:, cols] = jnp.dot(ds_ref[buf], kh, preferred_element_type=F32).astype(BF16)
            dkt_acc[cols, window] += lax.dot_general(qh, ds_ref[buf], _DIMS["tn"], preferred_element_type=F32)
            dvt_acc[cols, window] += lax.dot_general(doh, p_ref[buf], _DIMS["tn"], preferred_element_type=F32)
        cs_ref[0:1, :] += _colsum(dq_ref[...].astype(F32))

        @pl.when(g == n_groups - 1)
        def _():
            lo = (GROUP - 1) * CHUNK
            for h in range(N_HEADS):
                db_ref[h] = db_acc[h, :, lo:lo + BAND]
            inside = pl.ds(PAD_ROWS, seq)
            ones = jnp.ones((8, seq), F32)
            for row, acc in ((1, dkt_acc), (2, dvt_acc)):
                cs_ref[row:row + 1, :] = lax.dot_general(ones, acc[:, inside], _DIMS["nt"], precision=HIGHEST,
                                                         preferred_element_type=F32)[0:1, :]
            out_k = pltpu.make_async_copy(dkt_acc.at[:, inside], dkt_hbm, sems.at[0])
            out_v = pltpu.make_async_copy(dvt_acc.at[:, inside], dvt_hbm, sems.at[1])
            out_k.start()
            out_v.start()
            out_k.wait()
            out_v.wait()

    t_shape = (D_ATTN, seq + PAD_ROWS)
    outs, extra = _host_call(
        body, "attn_bwd", grid=(n_groups,),
        in_specs=[_tile(GROUP_Q, D_ATTN), _tile(GROUP_Q, D_ATTN), ANY, ANY, ANY],
        out_specs=[_tile(GROUP_Q, D_ATTN), ANY, ANY, _full((N_HEADS, CHUNK, BAND)), _full((8, D_ATTN))],
        out_shape=[jax.ShapeDtypeStruct((seq, D_ATTN), BF16), jax.ShapeDtypeStruct((D_ATTN, seq), F32),
                   jax.ShapeDtypeStruct((D_ATTN, seq), F32), jax.ShapeDtypeStruct((N_HEADS, CHUNK, BAND), F32),
                   jax.ShapeDtypeStruct((8, D_ATTN), F32)],
        scratch_shapes=[pltpu.VMEM(kpad.shape, BF16), pltpu.VMEM(vpad.shape, BF16), pltpu.VMEM(table.shape, F32),
                        pltpu.VMEM((N_HEADS, CHUNK, fold_w), F32), pltpu.VMEM(t_shape, F32),
                        pltpu.VMEM(t_shape, F32), pltpu.VMEM((2, GROUP_Q, GROUP_K), F32),
                        pltpu.VMEM((2, GROUP_Q, GROUP_K), F32), pltpu.VMEM((2, GROUP_Q, GROUP_K), BF16),
                        pltpu.VMEM((2, GROUP_Q, GROUP_K), BF16), pltpu.SemaphoreType.DMA((3,))],
        args=[qkv, dao, kpad, vpad, table], comm=comm)
    return outs, extra


def _assemble_dz(dq, dkt, dvt, dglu_a, dglu_b, dga, dgb):
    seq = dq.shape[0]
    rows = 512
    transposed = pl.BlockSpec((D_ATTN, rows), lambda i: (0, i))

    def body(dq_ref, dkt_ref, dvt_ref, da_ref, db_ref, dga_ref, dgb_ref, o_ref):
        o_ref[:, 0:D_ATTN] = dq_ref[...]
        o_ref[:, D_ATTN:2 * D_ATTN] = dkt_ref[...].T.astype(BF16)
        o_ref[:, 2 * D_ATTN:3 * D_ATTN] = dvt_ref[...].T.astype(BF16)
        off = 3 * D_ATTN
        for ref in (da_ref, db_ref, dga_ref, dgb_ref):
            width = ref.shape[1]
            o_ref[:, off:off + width] = ref[...]
            off += width

    width = 3 * D_ATTN + 2 * D_CONV + 2 * D_MODEL
    return pl.pallas_call(
        body, name="assemble_dz", out_shape=jax.ShapeDtypeStruct((seq, width), BF16), grid=(seq // rows,),
        in_specs=[_tile(rows, D_ATTN), transposed, transposed, _tile(rows, D_CONV), _tile(rows, D_CONV),
                  _tile(rows, D_MODEL), _tile(rows, D_MODEL)],
        out_specs=_tile(rows, width), compiler_params=_cparams(1),
    )(dq, dkt, dvt, dglu_a, dglu_b, dga, dgb)


CONV_ROWS = 256


def _ln_silu(u1, g, b):
    mu = jnp.mean(u1, axis=-1, keepdims=True)
    xc = u1 - mu
    rs = lax.rsqrt(jnp.mean(xc * xc, axis=-1, keepdims=True) + EPS)
    xhat = xc * rs
    u2 = xhat * g + b
    return xhat, rs, u2


def _glu_into(s_ref, a_ref, b_ref, ah_ref, bh_ref, first):
    halo = ah_ref[...] * _sig(bh_ref[...])
    s_ref[0:CONV_HALO, :] = jnp.where(first, 0.0, halo)
    s_ref[CONV_HALO:CONV_HALO + CONV_ROWS, :] = a_ref[...] * _sig(b_ref[...])


def _conv_fwd(zr, w_dw, b_dw, g_ln, b_ln, comm=None):
    seq = zr.shape[0]

    def body(a_ref, b_ref, ah_ref, bh_ref, w_ref, bias_ref, g_ref, bl_ref, u1_ref, u3_ref, s_ref):
        _glu_into(s_ref, a_ref, b_ref, ah_ref, bh_ref, pl.program_id(0) == 0)
        acc = jnp.zeros((CONV_ROWS, D_CONV), F32) + bias_ref[...]
        for j in range(CONV_K):
            acc = acc + w_ref[j:j + 1, :] * s_ref[2 + j:2 + j + CONV_ROWS, :]
        u1_ref[...] = acc
        _, _, u2 = _ln_silu(acc, g_ref[...], bl_ref[...])
        u3_ref[...] = (u2 * _sig(u2)).astype(BF16)

    return _host_call(
        lambda ins, outs, scratch: body(*ins, *outs, *scratch), "conv_fwd", grid=(seq // CONV_ROWS,),
        in_specs=[_tile(CONV_ROWS, D_CONV, 0), _tile(CONV_ROWS, D_CONV, 1),
                  _prev(CONV_HALO, D_CONV, CONV_ROWS, 0), _prev(CONV_HALO, D_CONV, CONV_ROWS, 1),
                  _full((CONV_K, D_CONV)), _full((1, D_CONV)), _full((1, D_CONV)), _full((1, D_CONV))],
        out_specs=[_tile(CONV_ROWS, D_CONV), _tile(CONV_ROWS, D_CONV)],
        out_shape=[jax.ShapeDtypeStruct((seq, D_CONV), F32), jax.ShapeDtypeStruct((seq, D_CONV), BF16)],
        scratch_shapes=[pltpu.VMEM((CONV_HALO + CONV_ROWS, D_CONV), F32)],
        args=[zr, zr, zr, zr, w_dw, b_dw, g_ln, b_ln], comm=comm)


def _conv_bwd(zr, u1, du3, w_dw, g_ln, b_ln, comm=None):
    seq = zr.shape[0]
    n_tiles = seq // CONV_ROWS
    n_halo = seq // CONV_HALO
    ext = CONV_ROWS + CONV_HALO

    def body(a_ref, b_ref, ah_ref, bh_ref, u1_ref, u1n_ref, d3_ref, d3n_ref, w_ref, g_ref, bl_ref,
             da_ref, db_ref, dw_ref, small_ref, s_ref, d_ref):
        i = pl.program_id(0)

        @pl.when(i == 0)
        def _():
            dw_ref[...] = jnp.zeros_like(dw_ref)
            small_ref[...] = jnp.zeros_like(small_ref)

        _glu_into(s_ref, a_ref, b_ref, ah_ref, bh_ref, i == 0)
        gv, bv = g_ref[...], bl_ref[...]

        def du1_of(u1, d3):
            xhat, rs, u2 = _ln_silu(u1, gv, bv)
            sg = _sig(u2)
            du2 = d3 * (sg * (1.0 + u2 * (1.0 - sg)))
            dxh = du2 * gv
            du1 = rs * (dxh - jnp.mean(dxh, axis=-1, keepdims=True)
                        - xhat * jnp.mean(dxh * xhat, axis=-1, keepdims=True))
            return du1, du2, xhat

        du1, du2, xhat = du1_of(u1_ref[...], d3_ref[...])
        du1n, _, _ = du1_of(u1n_ref[...], d3n_ref[...])
        d_ref[0:CONV_ROWS, :] = du1
        d_ref[CONV_ROWS:ext, :] = jnp.where(i == n_tiles - 1, 0.0, du1n)
        small_ref[0:1, :] += _colsum(du1)
        small_ref[1:2, :] += _colsum(du2 * xhat)
        small_ref[2:3, :] += _colsum(du2)
        du0 = jnp.zeros((CONV_ROWS, D_CONV), F32)
        for j in range(CONV_K):
            dw_ref[j:j + 1, :] += _colsum(du1 * s_ref[2 + j:2 + j + CONV_ROWS, :])
            du0 = du0 + w_ref[j:j + 1, :] * d_ref[CONV_K - 1 - j:CONV_K - 1 - j + CONV_ROWS, :]
        sb = _sig(b_ref[...])
        da = du0 * sb
        dbv = du0 * a_ref[...] * sb * (1.0 - sb)
        da_ref[...] = da.astype(BF16)
        db_ref[...] = dbv.astype(BF16)
        small_ref[3:4, :] += _colsum(da)
        small_ref[4:5, :] += _colsum(dbv)

    return _host_call(
        lambda ins, outs, scratch: body(*ins, *outs, *scratch), "conv_bwd", grid=(n_tiles,),
        in_specs=[_tile(CONV_ROWS, D_CONV, 0), _tile(CONV_ROWS, D_CONV, 1),
                  _prev(CONV_HALO, D_CONV, CONV_ROWS, 0), _prev(CONV_HALO, D_CONV, CONV_ROWS, 1),
                  _tile(CONV_ROWS, D_CONV), _next(CONV_HALO, D_CONV, CONV_ROWS, n_halo),
                  _tile(CONV_ROWS, D_CONV), _next(CONV_HALO, D_CONV, CONV_ROWS, n_halo),
                  _full((CONV_K, D_CONV)), _full((1, D_CONV)), _full((1, D_CONV))],
        out_specs=[_tile(CONV_ROWS, D_CONV), _tile(CONV_ROWS, D_CONV), _full((CONV_HALO, D_CONV)),
                   _full((8, D_CONV))],
        out_shape=[jax.ShapeDtypeStruct((seq, D_CONV), BF16), jax.ShapeDtypeStruct((seq, D_CONV), BF16),
                   jax.ShapeDtypeStruct((CONV_HALO, D_CONV), F32), jax.ShapeDtypeStruct((8, D_CONV), F32)],
        scratch_shapes=[pltpu.VMEM((ext, D_CONV), F32), pltpu.VMEM((ext, D_CONV), F32)],
        args=[zr, zr, zr, zr, u1, u1, du3, du3, w_dw, g_ln, b_ln], comm=comm)


MERGE_ROWS = 256


def _merge_fwd(ao, u3, zr, w_ao, w_co, b_co):
    seq = ao.shape[0]

    def body(ao_ref, u3_ref, ga_ref, gb_ref, wa_ref, wc_ref, bc_ref, y_ref, a_ref, cb_ref):
        a = jnp.dot(ao_ref[...], wa_ref[...], preferred_element_type=F32)
        cb = jnp.dot(u3_ref[...], wc_ref[...], preferred_element_type=F32) + bc_ref[...]
        a_ref[...] = a
        cb_ref[...] = cb
        y_ref[...] = (_sig(ga_ref[...]) * a + _sig(gb_ref[...]) * cb).astype(BF16)

    f32_out = jax.ShapeDtypeStruct((seq, D_MODEL), F32)
    return pl.pallas_call(
        body, name="merge_fwd",
        out_shape=[jax.ShapeDtypeStruct((seq, D_MODEL), BF16), f32_out, f32_out],
        grid=(seq // MERGE_ROWS,),
        in_specs=[_tile(MERGE_ROWS, D_ATTN), _tile(MERGE_ROWS, D_CONV), _tile(MERGE_ROWS, D_MODEL, 1),
                  _tile(MERGE_ROWS, D_MODEL, 2), _full(w_ao.shape), _full(w_co.shape), _full((1, D_MODEL))],
        out_specs=[_tile(MERGE_ROWS, D_MODEL)] * 3, compiler_params=_cparams(1),
    )(ao, u3, zr, zr, w_ao, w_co, b_co)


def _merge_bwd(dy, a, cb, zr):
    seq = dy.shape[0]

    def body(dy_ref, a_ref, cb_ref, ga_ref, gb_ref, da_ref, dcb_ref, dga_ref, dgb_ref, small_ref):
        i = pl.program_id(0)

        @pl.when(i == 0)
        def _():
            small_ref[...] = jnp.zeros_like(small_ref)

        dy_v = dy_ref[...]
        sa, sb = _sig(ga_ref[...]), _sig(gb_ref[...])
        dcb = dy_v * sb
        dga = dy_v * a_ref[...] * sa * (1.0 - sa)
        dgb = dy_v * cb_ref[...] * sb * (1.0 - sb)
        da_ref[...] = (dy_v * sa).astype(BF16)
        dcb_ref[...] = dcb.astype(BF16)
        dga_ref[...] = dga.astype(BF16)
        dgb_ref[...] = dgb.astype(BF16)
        small_ref[0:1, :] += _colsum(dga)
        small_ref[1:2, :] += _colsum(dgb)
        small_ref[2:3, :] += _colsum(dcb)

    bf = jax.ShapeDtypeStruct((seq, D_MODEL), BF16)
    return pl.pallas_call(
        body, name="merge_bwd", out_shape=[bf, bf, bf, bf, jax.ShapeDtypeStruct((8, D_MODEL), F32)],
        grid=(seq // MERGE_ROWS,),
        in_specs=[_tile(MERGE_ROWS, D_MODEL)] * 3 + [_tile(MERGE_ROWS, D_MODEL, 1), _tile(MERGE_ROWS, D_MODEL, 2)],
        out_specs=[_tile(MERGE_ROWS, D_MODEL)] * 4 + [_full((8, D_MODEL))], compiler_params=_cparams(1),
    )(dy, a, cb, zr, zr)


FFN_ROWS = 512
FFN_BLOCKS = D_FF // FFN_COLS
GELU_C = math.sqrt(2.0 / math.pi)


def _gelu(v):
    t = jnp.tanh(GELU_C * (v + 0.044715 * (v * v * v)))
    return 0.5 * v * (1.0 + t), t


def _gelu_grad(v, t):
    return 0.5 * (1.0 + t) + 0.5 * v * (1.0 - t * t) * (GELU_C * (1.0 + 3.0 * 0.044715 * (v * v)))


def _sublane_rows(ref, n):
    return [jnp.broadcast_to(ref[r:r + 1, :], (8, FFN_COLS)) for r in range(n)]


def _rolls(tile, shifts):
    return tuple(pltpu.roll(tile, s, 0) for s in shifts)


def _behind(prev_rolls, cur, row_id):
    rolls = _rolls(cur, (1, 2))
    x1 = jnp.where(row_id < 1, prev_rolls[0], rolls[0])
    x2 = jnp.where(row_id < 2, prev_rolls[1], rolls[1])
    return (x2, x1, cur), rolls


def _ahead(cur_rolls, next_rolls, row_id):
    return (jnp.where(row_id < 7, cur_rolls[0], next_rolls[0]), jnp.where(row_id < 6, cur_rolls[1], next_rolls[1]))


def _conv3(taps, w, bias):
    return w[0] * taps[0] + w[1] * taps[1] + w[2] * taps[2] + bias


def _ffn_specs(rows):
    tile = lambda off: pl.BlockSpec((rows, FFN_COLS), lambda j, i: (i, j + off))
    prev = lambda off: pl.BlockSpec((FFN_HALO, FFN_COLS),
                                    lambda j, i: (jnp.maximum(i * (rows // FFN_HALO) - 1, 0), j + off))
    wgt = lambda off: pl.BlockSpec((3, FFN_COLS), lambda j, i: (0, j + off))
    vec = lambda off: pl.BlockSpec((1, FFN_COLS), lambda j, i: (0, j + off))
    return tile, prev, wgt, vec


def _ffn_act(up, w_dw, b_dw):
    seq = up.shape[0]
    tile, prev, wgt, vec = _ffn_specs(FFN_ROWS)

    def body(v_ref, g_ref, vp_ref, gp_ref, wv_ref, wg_ref, bv_ref, bg_ref, act_ref):
        first = pl.program_id(1) == 0
        row_id = lax.broadcasted_iota(jnp.int32, (8, FFN_COLS), 0)
        wv, wg = _sublane_rows(wv_ref, 3), _sublane_rows(wg_ref, 3)
        (bv,), (bg,) = _sublane_rows(bv_ref, 1), _sublane_rows(bg_ref, 1)
        rolls_v = _rolls(jnp.where(first, 0.0, vp_ref[...]), (1, 2))
        rolls_g = _rolls(jnp.where(first, 0.0, gp_ref[...]), (1, 2))
        for row in range(0, FFN_ROWS, 16):
            halves = []
            for r in (row, row + 8):
                taps_v, rolls_v = _behind(rolls_v, v_ref[r:r + 8, :], row_id)
                taps_g, rolls_g = _behind(rolls_g, g_ref[r:r + 8, :], row_id)
                halves.append(_gelu(_conv3(taps_g, wg, bg))[0] * _conv3(taps_v, wv, bv))
            act_ref[row:row + 16, :] = jnp.concatenate(halves, axis=0).astype(BF16)

    return pl.pallas_call(
        body, name="ffn_act", out_shape=jax.ShapeDtypeStruct((seq, D_FF), BF16),
        grid=(FFN_BLOCKS, seq // FFN_ROWS),
        in_specs=[tile(0), tile(FFN_BLOCKS), prev(0), prev(FFN_BLOCKS), wgt(0), wgt(FFN_BLOCKS),
                  vec(0), vec(FFN_BLOCKS)],
        out_specs=tile(0), compiler_params=_cparams(2),
    )(up, up, up, up, w_dw, w_dw, b_dw, b_dw)


def _ffn_act_bwd(up, dact, w_dw, b_dw, comm=None):
    seq = up.shape[0]
    n_tiles = seq // FFN_ROWS
    n_halo = seq // FFN_HALO
    tile, prev, wgt, vec = _ffn_specs(FFN_ROWS)
    nxt = lambda off: pl.BlockSpec(
        (FFN_HALO, FFN_COLS), lambda j, i: (jnp.minimum((i + 1) * (FFN_ROWS // FFN_HALO), n_halo - 1), j + off))
    acc = lambda off: pl.BlockSpec((8, FFN_COLS), lambda j, i: (0, j + off))

    def body(v_ref, g_ref, vp_ref, gp_ref, vn_ref, gn_ref, da_ref, dan_ref, wv_ref, wg_ref, bv_ref, bg_ref,
             dv_out, dg_out, dwv_ref, dwg_ref, dbv_ref, dbg_ref):
        i = pl.program_id(1)
        first, last = i == 0, i == n_tiles - 1

        @pl.when(first)
        def _():
            for r in (dwv_ref, dwg_ref, dbv_ref, dbg_ref):
                r[...] = jnp.zeros_like(r)

        row_id = lax.broadcasted_iota(jnp.int32, (8, FFN_COLS), 0)
        wv, wg = _sublane_rows(wv_ref, 3), _sublane_rows(wg_ref, 3)
        (bv,), (bg,) = _sublane_rows(bv_ref, 1), _sublane_rows(bg_ref, 1)
        zero = jnp.zeros((8, FFN_COLS), F32)
        sums_v, sums_g = [zero] * 4, [zero] * 4
        rolls_v = _rolls(jnp.where(first, 0.0, vp_ref[...]), (1, 2))
        rolls_g = _rolls(jnp.where(first, 0.0, gp_ref[...]), (1, 2))
        behind = None
        done_v, done_g = [], []

        def grads(v_tile, g_tile, dact, rolls_v, rolls_g):
            taps_v, rolls_v = _behind(rolls_v, v_tile, row_id)
            taps_g, rolls_g = _behind(rolls_g, g_tile, row_id)
            val, gate = _conv3(taps_v, wv, bv), _conv3(taps_g, wg, bg)
            gel, t = _gelu(gate)
            return dact * gel, dact * val * _gelu_grad(gate, t), taps_v, taps_g, rolls_v, rolls_g

        def finish(tile, nxt, row):
            for (d, d_rolls), (_, n_rolls), w, done, o_ref in ((tile[0], nxt[0], wv, done_v, dv_out),
                                                               (tile[1], nxt[1], wg, done_g, dg_out)):
                d1, d2 = _ahead(d_rolls, n_rolls, row_id)
                done.append(w[2] * d + w[1] * d1 + w[0] * d2)
                if len(done) == 2:
                    o_ref[row - 16:row, :] = jnp.concatenate(done, axis=0).astype(BF16)
                    done.clear()

        for row in range(0, FFN_ROWS, 16):
            dact16 = da_ref[row:row + 16, :].astype(F32)
            for r, dact in ((row, dact16[0:8, :]), (row + 8, dact16[8:16, :])):
                dval, dgate, taps_v, taps_g, rolls_v, rolls_g = grads(v_ref[r:r + 8, :], g_ref[r:r + 8, :], dact,
                                                                      rolls_v, rolls_g)
                sums_v = [s + dval * x for s, x in zip(sums_v, taps_v)] + [sums_v[3] + dval]
                sums_g = [s + dgate * x for s, x in zip(sums_g, taps_g)] + [sums_g[3] + dgate]
                tile = ((dval, _rolls(dval, (7, 6))), (dgate, _rolls(dgate, (7, 6))))
                if behind is not None:
                    finish(behind, tile, r)
                behind = tile
        dact_next = jnp.where(last, 0.0, dan_ref[...].astype(F32)[0:FFN_HALO, :])
        dval, dgate, *_ = grads(vn_ref[...], gn_ref[...], dact_next, rolls_v, rolls_g)
        finish(behind, ((dval, _rolls(dval, (7, 6))), (dgate, _rolls(dgate, (7, 6)))), FFN_ROWS)
        for sums, dw_ref, db_ref in ((sums_v, dwv_ref, dbv_ref), (sums_g, dwg_ref, dbg_ref)):
            for tap in range(3):
                dw_ref[tap:tap + 1, :] += _colsum(sums[tap])
            db_ref[0:1, :] += _colsum(sums[3])

    half = jax.ShapeDtypeStruct((seq, D_FF), BF16)
    acc_shape = jax.ShapeDtypeStruct((8, D_FF), F32)
    return _host_call(
        lambda ins, outs, scratch: body(*ins, *outs, *scratch), "ffn_act_bwd", grid=(FFN_BLOCKS, n_tiles),
        in_specs=[tile(0), tile(FFN_BLOCKS), prev(0), prev(FFN_BLOCKS), nxt(0), nxt(FFN_BLOCKS),
                  tile(0), pl.BlockSpec((16, FFN_COLS), lambda j, i: (
                      jnp.minimum((i + 1) * (FFN_ROWS // 16), seq // 16 - 1), j)),
                  wgt(0), wgt(FFN_BLOCKS), vec(0), vec(FFN_BLOCKS)],
        out_specs=[tile(0), tile(0), acc(0), acc(0), acc(0), acc(0)],
        out_shape=[half, half, acc_shape, acc_shape, acc_shape, acc_shape],
        scratch_shapes=[], args=[up, up, up, up, up, up, dact, dact, w_dw, w_dw, b_dw, b_dw], comm=comm)


def _cols_to_blocks(full_cols):
    k, n8 = full_cols.shape
    return jnp.transpose(full_cols.reshape(k, N_DEV, n8 // N_DEV), (1, 0, 2))


def _rows_to_blocks(full_rows):
    r8, n = full_rows.shape
    return full_rows.reshape(N_DEV, r8 // N_DEV, n)


def _blocks_to_cols(gathered):
    _, k, n = gathered.shape
    return jnp.transpose(gathered, (1, 0, 2)).reshape(k, N_DEV * n)


def kernel(x, c, w_ada, b_ada, g_pre_mix, g_post_mix, w_in, b_in, rel_bias, w_attn_o, w_dw_conv, b_dw_conv, g_conv_ln, b_conv_ln, w_conv_o, b_conv_o, w_mix_o, g_pre_ffn, g_post_ffn, w_up, w_dw_ffn, b_dw_ffn, w_down, loss_target, m_w_ada, m_b_ada, m_g_pre_mix, m_g_post_mix, m_w_in, m_b_in, m_rel_bias, m_w_attn_o, m_w_dw_conv, m_b_dw_conv, m_g_conv_ln, m_b_conv_ln, m_w_conv_o, m_b_conv_o, m_w_mix_o, m_g_pre_ffn, m_g_post_ffn, m_w_up, m_w_dw_ffn, m_b_dw_ffn, m_w_down, v_w_ada, v_b_ada, v_g_pre_mix, v_g_post_mix, v_w_in, v_b_in, v_rel_bias, v_w_attn_o, v_w_dw_conv, v_b_dw_conv, v_g_conv_ln, v_b_conv_ln, v_w_conv_o, v_b_conv_o, v_w_mix_o, v_g_pre_ffn, v_g_post_ffn, v_w_up, v_w_dw_ffn, v_b_dw_ffn, v_w_down):
    names = ["w_ada", "b_ada", "g_pre_mix", "g_post_mix", "w_in", "b_in", "rel_bias", "w_attn_o", "w_dw_conv",
             "b_dw_conv", "g_conv_ln", "b_conv_ln", "w_conv_o", "b_conv_o", "w_mix_o", "g_pre_ffn", "g_post_ffn",
             "w_up", "w_dw_ffn", "b_dw_ffn", "w_down"]
    weights = dict(zip(names, [w_ada, b_ada, g_pre_mix, g_post_mix, w_in, b_in, rel_bias, w_attn_o, w_dw_conv,
                               b_dw_conv, g_conv_ln, b_conv_ln, w_conv_o, b_conv_o, w_mix_o, g_pre_ffn,
                               g_post_ffn, w_up, w_dw_ffn, b_dw_ffn, w_down]))
    mom_m = dict(zip(names, [m_w_ada, m_b_ada, m_g_pre_mix, m_g_post_mix, m_w_in, m_b_in, m_rel_bias, m_w_attn_o,
                             m_w_dw_conv, m_b_dw_conv, m_g_conv_ln, m_b_conv_ln, m_w_conv_o, m_b_conv_o,
                             m_w_mix_o, m_g_pre_ffn, m_g_post_ffn, m_w_up, m_w_dw_ffn, m_b_dw_ffn, m_w_down]))
    mom_v = dict(zip(names, [v_w_ada, v_b_ada, v_g_pre_mix, v_g_post_mix, v_w_in, v_b_in, v_rel_bias, v_w_attn_o,
                             v_w_dw_conv, v_b_dw_conv, v_g_conv_ln, v_b_conv_ln, v_w_conv_o, v_b_conv_o,
                             v_w_mix_o, v_g_pre_ffn, v_g_post_ffn, v_w_up, v_w_dw_ffn, v_b_dw_ffn, v_w_down]))
    shapes = {n: w.shape for n, w in weights.items()}

    seq = x.shape[1]
    me = 4 * lax.axis_index("x") + 2 * lax.axis_index("y") + lax.axis_index("c")
    x2 = x.reshape(seq, D_MODEL)
    target = loss_target.reshape(seq, D_MODEL)
    sq = lambda a: a.reshape(a.shape[1:])
    bf = lambda a: sq(a).astype(BF16)

    c_act = _silu_vec(c)
    c_all, g_in, g_dwc, g_dwf = _run_comm(
        _gather_comm([c_act, bf(w_in), sq(w_dw_conv), sq(w_dw_ffn)]), "gather_first")
    c_all = c_all.reshape(N_DEV, D_MODEL)
    wf_in = _blocks_to_cols(g_in)
    wf_dwc = _blocks_to_cols(g_dwc)
    wf_dwf = _blocks_to_cols(g_dwf)

    (mod_all,) = _run_comm(_gather_comm([_ada_fwd(c_all, sq(w_ada))]), "gather_mod")
    mod = lax.dynamic_index_in_dim(mod_all, me, axis=1, keepdims=False)
    mod6 = (mod.reshape(1, 6 * D_MODEL) + b_ada).reshape(6, D_MODEL)

    h1 = _pre_mix(x2, mod6, g_pre_mix)
    qkv = _mm(h1, wf_in, "nn", BF16, "in_proj_qkv", bias=b_in, tm=1024, tn=768, cols=(0, 3 * D_ATTN))
    zr, (g_ao, g_co, g_mo) = _mm(h1, wf_in, "nn", F32, "in_proj_rest", bias=b_in, tm=1024, tn=3 * D_ATTN,
                                 cols=(3 * D_ATTN, 2 * D_CONV + 2 * D_MODEL),
                                 comm=_gather_comm([bf(w_attn_o), bf(w_conv_o), bf(w_mix_o)]))
    kpad = jnp.pad(qkv[:, D_ATTN:2 * D_ATTN], ((PAD_ROWS, 0), (0, 0)))
    vpad = jnp.pad(qkv[:, 2 * D_ATTN:], ((PAD_ROWS, 0), (0, 0)))
    table = jnp.transpose(_bias_table(sq(rel_bias)), (1, 0, 2))
    ao, (g_up,) = _attn_fwd(qkv, kpad, vpad, table, comm=_gather_comm([bf(w_up)]))
    (u1, u3), (g_dn,) = _conv_fwd(zr, wf_dwc, b_dw_conv, g_conv_ln, b_conv_ln, comm=_gather_comm([bf(w_down)]))
    wf_ao = _blocks_to_cols(g_ao)
    wf_co = _blocks_to_cols(g_co)
    wf_mo = g_mo.reshape(D_MODEL, D_MODEL)
    wf_up = _blocks_to_cols(g_up)
    wf_dn = g_dn.reshape(D_FF, D_MODEL)
    y, a_br, cb_br = _merge_fwd(ao, u3, zr, wf_ao, wf_co, b_conv_o)
    ymix = _mm(y, wf_mo, "nn", F32, "mix_o", tm=1024, tn=1024)
    x1, h2 = _post_mix_pre_ffn(ymix, x2, mod6, g_post_mix, g_pre_ffn)
    up = _mm(h2, wf_up, "nn", F32, "ffn_up", tm=1024, tn=1408)
    act = _ffn_act(up, wf_dwf, b_dw_ffn)
    yf = _mm(act, wf_dn, "nn", F32, "ffn_down", tm=512, tn=1024)
    loss_lanes, dout, dyf, small_f = _final(yf, x1, target, mod6, g_post_ffn)

    dact = _mm(dyf, wf_dn, "nt", BF16, "ffn_down_dx", tm=1024, tn=1408)
    gw_down = _mm(act, dyf, "tn", BF16, "ffn_down_dw", tm=256, tn=1024)
    (dup_v, dup_g, dwv, dwg, dbv, dbg), (parts_down,) = _ffn_act_bwd(
        up, dact, wf_dwf, b_dw_ffn, comm=_scatter_comm([_rows_to_blocks(gw_down)]))
    dh2 = _mm([dup_v, dup_g], wf_up, "nt", F32, "ffn_up_dx")
    blocks_up = _mm_tn_blocks(h2, [dup_v, dup_g], 2 * D_FF // N_DEV, "ffn_up_dw", k_steps=2)
    dx1, dymix, small_m = _mid_bwd(dh2, x1, dout, ymix, mod6, g_pre_ffn, g_post_mix)
    dy = _mm(dymix, wf_mo, "nt", F32, "mix_o_dx", tm=1024, tn=1024)
    gw_mo = _mm(y, dymix, "tn", BF16, "mix_o_dw")
    da, dcb, dga, dgb, small_g = _merge_bwd(dy, a_br, cb_br, zr)
    dao = _mm(da, wf_ao, "nt", BF16, "attn_o_dx", tm=1024)
    gw_ao = _mm(ao, da, "tn", BF16, "attn_o_dw")
    du3 = _mm(dcb, wf_co, "nt", F32, "conv_o_dx", tm=1024)
    gw_co = _mm(u3, dcb, "tn", BF16, "conv_o_dw")
    (dq, dkt, dvt, dbias, small_a), (parts_up,) = _attn_bwd(
        qkv, kpad, vpad, table, dao, comm=_scatter_comm([blocks_up]))
    g_rel = _bias_grad(jnp.transpose(dbias, (1, 0, 2)))
    (dglu_a, dglu_b, dw_conv, small_c), (parts_mo, parts_ao, parts_co) = _conv_bwd(
        zr, u1, du3, wf_dwc, g_conv_ln, b_conv_ln,
        comm=_scatter_comm([_rows_to_blocks(gw_mo), _cols_to_blocks(gw_ao), _cols_to_blocks(gw_co)]))
    dz = _assemble_dz(dq, dkt, dvt, dglu_a, dglu_b, dga, dgb)
    blocks_in = _mm_tn_blocks(h1, [dz], dz.shape[1] // N_DEV, "in_proj_dw")
    dh1, (parts_in,) = _mm(dz, wf_in, "nt", F32, "in_proj_dx", comm=_scatter_comm([blocks_in]))
    grad_x, small_x = _pre_mix_bwd(dh1, x2, dx1, mod6, g_pre_mix)

    packed = _pack_grads(small_x, small_m, small_f, small_g, small_a, small_c, dbv, dbg, dwv, dwg, dw_conv)
    gathered, gathered_rel, gathered_loss = _run_comm(_gather_comm([packed, g_rel, loss_lanes]), "gather_small")
    gathered = gathered.reshape(N_DEV, PACKED_TOTAL)
    updates, g_dwc_full, g_dwf_full, loss_all = _small_adamw(gathered, gathered_rel, gathered_loss, weights, mom_m,
                                                             mom_v)
    loss = loss_all[0, 0]

    grads, deltas, new_m, new_v = {}, {}, {}, {}

    def record(name, update):
        for dst, val in zip((grads, deltas, new_m, new_v), update):
            dst[name] = val.reshape(shapes[name])

    for name, update in updates.items():
        record(name, update)

    def local_update(name, grad):
        record(name, _adamw(sq(weights[name]), sq(mom_m[name]), sq(mom_v[name]), "adamw_" + name, g=grad))

    conv_cols, ffn_cols, ada_cols = D_CONV // N_DEV, 2 * D_FF // N_DEV, 6 * D_MODEL // N_DEV
    local_update("w_dw_conv", lax.dynamic_slice(g_dwc_full, (0, me * conv_cols), (CONV_K, conv_cols)))
    local_update("w_dw_ffn", lax.dynamic_slice(g_dwf_full, (0, me * ffn_cols), (3, ffn_cols)))
    local_update("w_ada", _ada_grad(c_all, lax.dynamic_slice(gathered, (0, me * ada_cols), (N_DEV, ada_cols))))

    for name, part in (("w_in", parts_in), ("w_attn_o", parts_ao), ("w_conv_o", parts_co), ("w_mix_o", parts_mo),
                       ("w_up", parts_up), ("w_down", parts_down)):
        record(name, _adamw(sq(weights[name]), sq(mom_m[name]), sq(mom_v[name]), "adamw_" + name, parts=part))

    return (loss, grad_x.reshape(x.shape), *[grads[n] for n in names], *[deltas[n] for n in names],
            *[new_m[n] for n in names], *[new_v[n] for n in names])
```

```python
import functools
import math

import jax
import jax.numpy as jnp
from jax import lax
from jax.experimental import pallas as pl
from jax.experimental.pallas import tpu as pltpu

F32 = jnp.float32
BF16 = jnp.bfloat16
HIGHEST = lax.Precision.HIGHEST

D_MODEL = 1024
CHUNK = 64
LEFT_CHUNKS = 8
BAND = (LEFT_CHUNKS + 1) * CHUNK
PAD_ROWS = LEFT_CHUNKS * CHUNK
GROUP = 4
GROUP_Q = GROUP * CHUNK
GROUP_K = GROUP_Q + PAD_ROWS
SOFTMAX_ROWS = 16
TOEPLITZ = 640
N_HEADS = 8
HEAD_DIM = 64
D_ATTN = 512
D_CONV = 512
CONV_K = 31
CONV_HALO = 32
MAX_REL = 128
N_REL = 2 * MAX_REL + 1
D_FF = 2816
FFN_HALO = 8
FFN_COLS = 256
EPS = 1e-6
NEG_INF = -1e30
N_DEV = 8

ADAM_LR = 0.001
ADAM_B1 = 0.9
ADAM_B2 = 0.999
ADAM_EPS = 1e-08
ADAM_WD = 0.01
ADAM_STEP = 10

VMEM_LIMIT_BYTES = 56 * 1024 * 1024
ADAMW_BLOCK_BYTES = 768 * 1024

MESH = pl.DeviceIdType.MESH
ANY = pl.BlockSpec(memory_space=pl.ANY)

SH_M, SC_M, GT_M, SH_F, SC_F, GT_F = range(6)

SMALL = (("b_ada", 6144), ("g_pre_mix", 1024), ("g_post_mix", 1024), ("b_in", 4608), ("b_dw_conv", 512),
         ("g_conv_ln", 512), ("b_conv_ln", 512), ("b_conv_o", 1024), ("g_pre_ffn", 1024), ("g_post_ffn", 1024),
         ("b_dw_ffn", 5632))
PACKED_TOTAL = sum(n for _, n in SMALL) + CONV_K * D_CONV + 3 * 2 * D_FF


def _cparams(n_axes):
    return pltpu.CompilerParams(vmem_limit_bytes=VMEM_LIMIT_BYTES,
                                dimension_semantics=("arbitrary",) * n_axes)


def _sig(v):
    return 1.0 / (1.0 + jnp.exp(-v))


def _pick(n, target):
    if n <= target:
        return n
    t = target - target % 128
    while n % t:
        t -= 128
    return t


def _tile(rows, cols, col=0):
    return pl.BlockSpec((rows, cols), lambda i: (i, col))


def _full(shape):
    zeros = (0,) * len(shape)
    return pl.BlockSpec(shape, lambda i: zeros)


def _prev(halo, cols, rows, col=0):
    return pl.BlockSpec((halo, cols), lambda i: (jnp.maximum(i * (rows // halo) - 1, 0), col))


def _next(halo, cols, rows, n_blocks, col=0):
    return pl.BlockSpec((halo, cols), lambda i: (jnp.minimum((i + 1) * (rows // halo), n_blocks - 1), col))


class _Comm:
    def __init__(self, inputs, out_shapes, sems, start, finish):
        self.inputs, self.out_shapes, self.sems, self.start, self.finish = inputs, out_shapes, sems, start, finish


def _host_call(body, name, grid, in_specs, out_specs, out_shape, scratch_shapes, args, comm=None):
    n_in, n_out, n_scr = len(args), len(out_shape), len(scratch_shapes)
    c_in = list(comm.inputs) if comm else []
    c_out = list(comm.out_shapes) if comm else []
    c_sem = list(comm.sems) if comm else []

    def full(*refs):
        bounds = [0, n_in, len(c_in), n_out, len(c_out), n_scr, len(c_sem)]
        cuts = [sum(bounds[:i + 1]) for i in range(len(bounds))]
        ins, cins, outs, couts, scr, csems = (refs[lo:hi] for lo, hi in zip(cuts[:-1], cuts[1:]))
        if comm:
            first = functools.reduce(jnp.logical_and, [pl.program_id(ax) == 0 for ax in range(len(grid))])
            pl.when(first)(lambda: comm.start(cins, couts, csems))
        body(ins, outs, scr)
        if comm:
            last = functools.reduce(jnp.logical_and, [pl.program_id(ax) == grid[ax] - 1 for ax in range(len(grid))])
            pl.when(last)(lambda: comm.finish(cins, couts, csems))

    res = pl.pallas_call(
        full, name=name, grid=grid, in_specs=list(in_specs) + [ANY] * len(c_in),
        out_specs=list(out_specs) + [ANY] * len(c_out), out_shape=list(out_shape) + c_out,
        scratch_shapes=list(scratch_shapes) + c_sem, compiler_params=_cparams(len(grid)),
    )(*args, *c_in)
    return list(res[:n_out]), list(res[n_out:])


def _run_comm(comm, name):
    n_in, n_out = len(comm.inputs), len(comm.out_shapes)

    def body(*refs):
        ins, outs, sems = refs[:n_in], refs[n_in:n_in + n_out], refs[n_in + n_out:]
        comm.start(ins, outs, sems)
        comm.finish(ins, outs, sems)

    return pl.pallas_call(
        body, name=name, out_shape=list(comm.out_shapes), in_specs=[ANY] * n_in, out_specs=[ANY] * n_out,
        scratch_shapes=list(comm.sems),
    )(*comm.inputs)


def _place():
    return lax.axis_index("x"), lax.axis_index("y"), lax.axis_index("c")


def _gather_comm(arrs):
    n = len(arrs)

    def plan(ins, outs, sems):
        send_sems, recv_sems, local_sems = sems
        x, y, c = _place()
        me, sibling = (x, y, c), (x, y, 1 - c)
        chips = [(1 - x, y), (x, 1 - y), (1 - x, 1 - y)]

        def block(k, p):
            return outs[k].at[4 * p[0] + 2 * p[1] + p[2]]

        def copy(k, s, blk, to, src=None):
            return pltpu.make_async_remote_copy(
                src_ref=block(k, blk) if src is None else src, dst_ref=block(k, blk),
                send_sem=send_sems.at[7 * k + s], recv_sem=recv_sems.at[7 * k + s],
                device_id=to, device_id_type=MESH)

        mine = [pltpu.make_async_copy(ins[k], block(k, me), local_sems.at[k]) for k in range(n)]
        first = []
        for k in range(n):
            first.append(copy(k, 0, me, sibling, src=ins[k]))
            for j, chip in enumerate(chips):
                first.append(copy(k, 1 + j, me, (*chip, c), src=ins[k]))
        return me, sibling, chips, c, copy, mine, first

    def start(ins, outs, sems):
        *_, mine, first = plan(ins, outs, sems)
        for cp in mine + first:
            cp.start()

    def finish(ins, outs, sems):
        me, sibling, chips, c, copy, mine, first = plan(ins, outs, sems)
        passed = []
        for j, chip in enumerate(chips):
            for k in range(n):
                copy(k, 1 + j, (*chip, c), me).wait_recv()
                fwd = copy(k, 4 + j, (*chip, c), sibling)
                fwd.start()
                passed.append(fwd)
        for k in range(n):
            copy(k, 0, sibling, me).wait_recv()
        for j, chip in enumerate(chips):
            for k in range(n):
                copy(k, 4 + j, (*chip, 1 - c), me).wait_recv()
        for cp in first + passed:
            cp.wait_send()
        for cp in mine:
            cp.wait()

    return _Comm(list(arrs), [jax.ShapeDtypeStruct((N_DEV,) + a.shape, a.dtype) for a in arrs],
                 [pltpu.SemaphoreType.DMA((7 * n,)), pltpu.SemaphoreType.DMA((7 * n,)),
                  pltpu.SemaphoreType.DMA((n,))], start, finish)


def _scatter_comm(blocks):
    n = len(blocks)

    def plan(ins, outs, sems, arrivals):
        send_sems, recv_sems, local_sems = sems
        x, y, c = _place()
        me = 4 * x + 2 * y + c
        local = [pltpu.make_async_copy(ins[k].at[me], outs[k].at[me], local_sems.at[k]) for k in range(n)]
        sends, recvs = [], []
        for k in range(n):
            for mask in range(1, N_DEV):
                px = 1 - x if mask & 4 else x
                py = 1 - y if mask & 2 else y
                pc = 1 - c if mask & 1 else c
                peer = 4 * px + 2 * py + pc
                sem = 7 * k + mask - 1
                both = dict(send_sem=send_sems.at[sem], recv_sem=recv_sems.at[sem], device_id=(px, py, pc),
                            device_id_type=MESH)
                sends.append(pltpu.make_async_remote_copy(src_ref=ins[k].at[peer], dst_ref=outs[k].at[me], **both))
                if arrivals:
                    recvs.append(pltpu.make_async_remote_copy(src_ref=ins[k].at[me], dst_ref=outs[k].at[peer],
                                                              **both))
        return local, sends, recvs

    def start(ins, outs, sems):
        local, sends, _ = plan(ins, outs, sems, arrivals=False)
        for cp in local + sends:
            cp.start()

    def finish(ins, outs, sems):
        local, sends, recvs = plan(ins, outs, sems, arrivals=True)
        for cp in recvs:
            cp.wait_recv()
        for cp in sends:
            cp.wait_send()
        for cp in local:
            cp.wait()

    return _Comm(list(blocks), [jax.ShapeDtypeStruct(b.shape, b.dtype) for b in blocks],
                 [pltpu.SemaphoreType.DMA((7 * n,)), pltpu.SemaphoreType.DMA((7 * n,)),
                  pltpu.SemaphoreType.DMA((n,))], start, finish)


_DIMS = {"nn": (((1,), (0,)), ((), ())), "nt": (((1,), (1,)), ((), ())), "tn": (((0,), (0,)), ((), ()))}


class _Epilogue:
    def __init__(self, args, in_specs, out_shapes, out_specs, fn, keep_product):
        self.args, self.in_specs, self.out_shapes, self.out_specs = args, in_specs, out_shapes, out_specs
        self.fn, self.keep_product = fn, keep_product


def _row_tile(rows, cols):
    return pl.BlockSpec((rows, cols), lambda i, j: (i, 0))


def _whole(shape):
    zeros = (0,) * len(shape)
    return pl.BlockSpec(shape, lambda i, j: zeros)


def _mm(a, b, mode, out_dtype, name, bias=None, tm=512, tn=512, comm=None, cols=None, epilogue=None):
    pieces = a if isinstance(a, (list, tuple)) else [a]
    assert all(p.dtype == BF16 for p in pieces) and b.dtype == BF16
    a = pieces[0]
    if mode == "tn":
        k_dim, m_dim = a.shape
    else:
        m_dim, k_dim = a.shape
    n_dim = b.shape[0] if mode == "nt" else b.shape[1]
    col0 = 0
    if cols is not None:
        assert mode == "nn" and cols[0] % tn == 0 and cols[1] % tn == 0
        col0, n_dim = cols[0] // tn, cols[1]
    tm, tn = _pick(m_dim, tm), _pick(n_dim, tn)
    a_specs = [pl.BlockSpec((k_dim, tm), lambda i, j: (0, i)) if mode == "tn"
               else pl.BlockSpec((tm, k_dim), lambda i, j: (i, 0))] * len(pieces)
    if mode == "nt":
        b_specs = [pl.BlockSpec((tn, k_dim), lambda i, j, p=p: (j, p)) for p in range(len(pieces))]
    else:
        assert len(pieces) == 1
        b_specs = [pl.BlockSpec((k_dim, tn), lambda i, j: (0, j + col0))]
    in_specs = a_specs + b_specs
    args = list(pieces) + [b] * len(pieces)
    if bias is not None:
        in_specs.append(pl.BlockSpec((1, tn), lambda i, j: (0, j + col0)))
        args.append(bias)
    dims = _DIMS[mode]
    n_pieces = len(pieces)
    n_own = len(args)
    keep = epilogue is None or epilogue.keep_product
    out_specs = [pl.BlockSpec((tm, tn), lambda i, j: (i, j))] if keep else []
    out_shape = [jax.ShapeDtypeStruct((m_dim, n_dim), out_dtype)] if keep else []
    if epilogue is not None:
        assert tn == n_dim
        in_specs, args = in_specs + list(epilogue.in_specs), args + list(epilogue.args)
        out_specs, out_shape = out_specs + list(epilogue.out_specs), out_shape + list(epilogue.out_shapes)

    def body(ins, outs, scratch):
        total = lax.dot_general(ins[0][...], ins[n_pieces][...], dims, preferred_element_type=F32)
        for p in range(1, n_pieces):
            total = total + lax.dot_general(ins[p][...], ins[n_pieces + p][...], dims, preferred_element_type=F32)
        if bias is not None:
            total = total + ins[2 * n_pieces][...]
        if keep:
            outs[0][...] = total.astype(out_dtype)
        if epilogue is not None:
            epilogue.fn(total, pl.program_id(0) == 0, ins[n_own:], outs[1:] if keep else outs)

    outs, extra = _host_call(body, name, grid=(m_dim // tm, n_dim // tn), in_specs=in_specs, out_specs=out_specs,
                             out_shape=out_shape, scratch_shapes=[], args=args, comm=comm)
    product = outs[0] if keep else None
    if comm is None and epilogue is None:
        return product
    return product, outs[1:] if keep else outs, extra


def _mm_tn_blocks(a, bs, n_dev_cols, name, tm=512, k_steps=1):
    k_dim, m_dim = a.shape
    n = n_dev_cols
    pair = 2 * n
    assert pair % 128 == 0 and all(b.shape[1] % pair == 0 for b in bs) and k_dim % k_steps == 0
    counts = [b.shape[1] // pair for b in bs]
    firsts = [sum(counts[:q]) for q in range(len(bs))]
    assert sum(counts) == N_DEV // 2
    tm, tk = _pick(m_dim, tm), k_dim // k_steps

    def b_spec(first, count):
        return pl.BlockSpec((tk, pair), lambda i, j, k: (k, jnp.clip(j - first, 0, count - 1)))

    def body(ins, outs, scratch):
        a_ref, b_refs, o_ref, s_ref = ins[0], ins[1:], outs[0], scratch[0]
        j, k = pl.program_id(1), pl.program_id(2)
        for b_ref, first, count in zip(b_refs, firsts, counts):
            @pl.when(jnp.logical_and(j >= first, j < first + count))
            def _(b_ref=b_ref):
                part = lax.dot_general(a_ref[...], b_ref[...], _DIMS["tn"], preferred_element_type=F32)
                if k_steps == 1:
                    s_ref[...] = part
                else:
                    @pl.when(k == 0)
                    def _():
                        s_ref[...] = part

                    @pl.when(k > 0)
                    def _():
                        s_ref[...] += part

        @pl.when(k == k_steps - 1)
        def _():
            o_ref[0] = s_ref[:, 0:n].astype(BF16)
            o_ref[1] = s_ref[:, n:pair].astype(BF16)

    (out,), _ = _host_call(
        body, name, grid=(m_dim // tm, N_DEV // 2, k_steps),
        in_specs=[pl.BlockSpec((tk, tm), lambda i, j, k: (k, i))] + [b_spec(f, c) for f, c in zip(firsts, counts)],
        out_specs=[pl.BlockSpec((2, tm, n), lambda i, j, k: (j, i, 0))],
        out_shape=[jax.ShapeDtypeStruct((N_DEV, m_dim, n), BF16)],
        scratch_shapes=[pltpu.VMEM((tm, pair), F32)], args=[a] + list(bs))
    return out


def _adam_math(w, g, m, v):
    m = ADAM_B1 * m + (1.0 - ADAM_B1) * g
    v = ADAM_B2 * v + (1.0 - ADAM_B2) * (g * g)
    m_hat = m / (1.0 - ADAM_B1 ** ADAM_STEP)
    v_hat = v / (1.0 - ADAM_B2 ** ADAM_STEP)
    delta = -ADAM_LR * (m_hat / (jnp.sqrt(v_hat) + ADAM_EPS) + ADAM_WD * w)
    return delta, m, v


def _adamw(w, m, v, name, g=None, parts=None):
    rows, cols = w.shape
    tr = rows
    if rows * cols * 4 > ADAMW_BLOCK_BYTES:
        tr = max(t for t in range(16, rows, 16) if rows % t == 0 and t * cols * 4 <= ADAMW_BLOCK_BYTES)

    def body(w_ref, m_ref, v_ref, g_ref, go_ref, d_ref, mo_ref, vo_ref):
        if parts is None:
            grad = g_ref[...]
        else:
            grad = g_ref[0].astype(F32)
            for d in range(1, N_DEV):
                grad = grad + g_ref[d].astype(F32)
        delta, m_new, v_new = _adam_math(w_ref[...], grad, m_ref[...], v_ref[...])
        go_ref[...] = grad
        d_ref[...] = delta
        mo_ref[...] = m_new
        vo_ref[...] = v_new

    spec = _tile(tr, cols)
    g_spec = spec if parts is None else pl.BlockSpec((N_DEV, tr, cols), lambda i: (0, i, 0))
    shape = jax.ShapeDtypeStruct((rows, cols), F32)
    return pl.pallas_call(
        body, name=name, out_shape=[shape] * 4, grid=(rows // tr,),
        in_specs=[spec, spec, spec, g_spec], out_specs=[spec] * 4, compiler_params=_cparams(1),
    )(w, m, v, g if parts is None else parts)


def _pack_grads(small_x, small_m, small_f, small_g, small_a, small_c, dbv, dbg, dwv, dwg, dw_conv):
    pieces = [
        (small_x, 2, D_MODEL), (small_x, 1, D_MODEL), (small_m, 4, D_MODEL), (small_m, 2, D_MODEL),
        (small_m, 1, D_MODEL), (small_f, 1, D_MODEL),
        (small_x, 0, D_MODEL), (small_m, 3, D_MODEL),
        (small_a, 0, D_ATTN), (small_a, 1, D_ATTN), (small_a, 2, D_ATTN), (small_c, 3, D_CONV),
        (small_c, 4, D_CONV), (small_g, 0, D_MODEL), (small_g, 1, D_MODEL),
        (small_c, 0, D_CONV), (small_c, 1, D_CONV), (small_c, 2, D_CONV),
        (small_g, 2, D_MODEL), (small_m, 0, D_MODEL), (small_f, 0, D_MODEL),
        (dbv, 0, D_FF), (dbg, 0, D_FF),
    ]
    pieces += [(dw_conv, j, D_CONV) for j in range(CONV_K)]
    pieces += [(src, tap, D_FF) for tap in range(3) for src in (dwv, dwg)]
    sources = [small_x, small_m, small_f, small_g, small_a, small_c, dbv, dbg, dwv, dwg, dw_conv]
    assert sum(width for _, _, width in pieces) == PACKED_TOTAL

    def body(*refs):
        o_ref = refs[-1]
        ref_of = {id(src): ref for src, ref in zip(sources, refs)}
        off = 0
        for src, row, width in pieces:
            o_ref[:, off:off + width] = ref_of[id(src)][row:row + 1, :]
            off += width

    return pl.pallas_call(body, name="pack_grads", out_shape=jax.ShapeDtypeStruct((1, PACKED_TOTAL), F32))(*sources)


def _small_adamw(gathered, gathered_rel, gathered_loss, weights, mom_m, mom_v):
    vec_names = [name for name, _ in SMALL]
    states = []
    for name in vec_names + ["rel_bias"]:
        states += [weights[name], mom_m[name], mom_v[name]]
    states = [a.reshape(a.shape[1:]) if a.ndim == 3 else a for a in states]
    n_state = len(states)

    def body(*refs):
        g_ref, rel_ref, loss_ref = refs[0], refs[1], refs[2]
        state_refs, out_refs = refs[3:3 + n_state], refs[3 + n_state:]
        total = g_ref[0:1, :]
        rel = rel_ref[0]
        loss = loss_ref[0]
        for d in range(1, N_DEV):
            total = total + g_ref[d:d + 1, :]
            rel = rel + rel_ref[d]
            loss = loss + loss_ref[d]
        off = 0
        for n, (name, width) in enumerate(SMALL):
            grad = total[:, off:off + width]
            w_ref, m_ref, v_ref = state_refs[3 * n:3 * n + 3]
            for ref, val in zip(out_refs[4 * n:4 * n + 4], (grad,) + _adam_math(w_ref[...], grad, m_ref[...], v_ref[...])):
                ref[...] = val
            off += width
        n = len(SMALL)
        w_ref, m_ref, v_ref = state_refs[3 * n:3 * n + 3]
        for ref, val in zip(out_refs[4 * n:4 * n + 4], (rel,) + _adam_math(w_ref[...], rel, m_ref[...], v_ref[...])):
            ref[...] = val
        dwc_ref, dwf_ref, loss_out = out_refs[4 * n + 4:]
        loss_out[...] = 0.5 * loss
        dwc_ref[...] = jnp.zeros_like(dwc_ref)
        dwf_ref[...] = jnp.zeros_like(dwf_ref)
        for j in range(CONV_K):
            dwc_ref[j:j + 1, :] = total[:, off:off + D_CONV]
            off += D_CONV
        for tap in range(3):
            dwf_ref[tap:tap + 1, :] = total[:, off:off + 2 * D_FF]
            off += 2 * D_FF

    out_shape = []
    for k in range(n_state // 3):
        out_shape += [jax.ShapeDtypeStruct(states[3 * k].shape, F32)] * 4
    out_shape += [jax.ShapeDtypeStruct((CONV_HALO, D_CONV), F32), jax.ShapeDtypeStruct((8, 2 * D_FF), F32),
                  jax.ShapeDtypeStruct((1, 128), F32)]
    res = pl.pallas_call(
        body, name="small_adamw", out_shape=out_shape,
        compiler_params=pltpu.CompilerParams(vmem_limit_bytes=VMEM_LIMIT_BYTES),
    )(gathered, gathered_rel, gathered_loss, *states)
    updates = {name: tuple(res[4 * n:4 * n + 4]) for n, name in enumerate(vec_names + ["rel_bias"])}
    return updates, res[-3], res[-2], res[-1]


def _silu_vec(c):
    def body(c_ref, o_ref):
        v = c_ref[...]
        o_ref[...] = v * _sig(v)

    return pl.pallas_call(body, name="silu_c", out_shape=jax.ShapeDtypeStruct(c.shape, F32))(c)


def _ada_fwd(c_all, w_shard):
    def body(c_ref, w_ref, o_ref):
        o_ref[...] = jnp.dot(c_ref[...], w_ref[...], precision=HIGHEST, preferred_element_type=F32)

    return pl.pallas_call(
        body, name="ada_fwd", out_shape=jax.ShapeDtypeStruct((N_DEV, w_shard.shape[1]), F32),
        compiler_params=pltpu.CompilerParams(vmem_limit_bytes=VMEM_LIMIT_BYTES),
    )(c_all, w_shard)


def _ada_grad(c_all, dmod_shard):
    def body(c_ref, d_ref, o_ref):
        o_ref[...] = lax.dot_general(c_ref[...], d_ref[...], _DIMS["tn"], precision=HIGHEST,
                                     preferred_element_type=F32)

    return pl.pallas_call(
        body, name="ada_grad", out_shape=jax.ShapeDtypeStruct((D_MODEL, dmod_shard.shape[1]), F32),
        compiler_params=pltpu.CompilerParams(vmem_limit_bytes=VMEM_LIMIT_BYTES),
    )(c_all, dmod_shard)


ROWS = 256


def _rms(v):
    r = lax.rsqrt(jnp.mean(v * v, axis=-1, keepdims=True) + EPS)
    return v * r, r


def _rms_bwd(dxn, xn, r):
    return r * (dxn - xn * jnp.mean(dxn * xn, axis=-1, keepdims=True))


def _colsum(v):
    return jnp.sum(v, axis=0, keepdims=True)


def _pre_mix(x, mod6, g1):
    seq = x.shape[0]

    def body(x_ref, mod_ref, g_ref, h_ref):
        xn, _ = _rms(x_ref[...])
        y = xn * g_ref[...]
        h_ref[...] = (y * (1.0 + mod_ref[SC_M:SC_M + 1, :]) + mod_ref[SH_M:SH_M + 1, :]).astype(BF16)

    return pl.pallas_call(
        body, name="pre_mix", out_shape=jax.ShapeDtypeStruct((seq, D_MODEL), BF16), grid=(seq // ROWS,),
        in_specs=[_tile(ROWS, D_MODEL), _full((6, D_MODEL)), _full((1, D_MODEL))],
        out_specs=_tile(ROWS, D_MODEL), compiler_params=_cparams(1),
    )(x, mod6, g1)


def _post_mix_pre_ffn(x, mod6, g2, g3, rows):
    seq = x.shape[0]

    def fn(y, first, ins, outs):
        x_ref, mod_ref, g2_ref, g3_ref = ins
        x1_ref, h_ref = outs
        yn, _ = _rms(y)
        x1 = x_ref[...] + mod_ref[GT_M:GT_M + 1, :] * (yn * g2_ref[...])
        x1_ref[...] = x1
        xn, _ = _rms(x1)
        y3 = xn * g3_ref[...]
        h_ref[...] = (y3 * (1.0 + mod_ref[SC_F:SC_F + 1, :]) + mod_ref[SH_F:SH_F + 1, :]).astype(BF16)

    return _Epilogue(
        [x, mod6, g2, g3], [_row_tile(rows, D_MODEL), _whole((6, D_MODEL)), _whole((1, D_MODEL)), _whole((1, D_MODEL))],
        [jax.ShapeDtypeStruct((seq, D_MODEL), F32), jax.ShapeDtypeStruct((seq, D_MODEL), BF16)],
        [_row_tile(rows, D_MODEL), _row_tile(rows, D_MODEL)], fn, keep_product=True)


def _final(x1, target, mod6, g4, rows):
    seq = x1.shape[0]

    def fn(y, first, ins, outs):
        x1_ref, t_ref, mod_ref, g_ref = ins
        loss_ref, dout_ref, dyf_ref, small_ref = outs

        @pl.when(first)
        def _():
            loss_ref[...] = jnp.zeros_like(loss_ref)
            small_ref[...] = jnp.zeros_like(small_ref)

        gt = mod_ref[GT_F:GT_F + 1, :]
        g4v = g_ref[...]
        yn, r = _rms(y)
        out = x1_ref[...] + gt * (yn * g4v)
        err = out - t_ref[...]
        loss_ref[...] += jnp.sum(jnp.mean(err * err, axis=-1, keepdims=True))
        dout = err * (1.0 / D_MODEL)
        dout_ref[...] = dout
        small_ref[0:1, :] += _colsum(dout * gt * yn)
        small_ref[1:2, :] += _colsum(dout * (yn * g4v))
        dyf_ref[...] = _rms_bwd(dout * gt * g4v, yn, r).astype(BF16)

    return _Epilogue(
        [x1, target, mod6, g4],
        [_row_tile(rows, D_MODEL), _row_tile(rows, D_MODEL), _whole((6, D_MODEL)), _whole((1, D_MODEL))],
        [jax.ShapeDtypeStruct((1, 128), F32), jax.ShapeDtypeStruct((seq, D_MODEL), F32),
         jax.ShapeDtypeStruct((seq, D_MODEL), BF16), jax.ShapeDtypeStruct((8, D_MODEL), F32)],
        [_whole((1, 128)), _row_tile(rows, D_MODEL), _row_tile(rows, D_MODEL), _whole((8, D_MODEL))],
        fn, keep_product=False)


def _mid_bwd(dh2, x1, dout, ymix, mod6, g3, g2):
    seq = x1.shape[0]

    def body(dh_ref, x1_ref, dout_ref, y_ref, mod_ref, g3_ref, g2_ref, dx1_ref, dy_ref, small_ref):
        i = pl.program_id(0)

        @pl.when(i == 0)
        def _():
            small_ref[...] = jnp.zeros_like(small_ref)

        dh = dh_ref[...]
        g3v, g2v = g3_ref[...], g2_ref[...]
        xn, r3 = _rms(x1_ref[...])
        y3 = xn * g3v
        dy3 = dh * (1.0 + mod_ref[SC_F:SC_F + 1, :])
        small_ref[0:1, :] += _colsum(dy3 * xn)
        small_ref[1:2, :] += _colsum(dh * y3)
        small_ref[2:3, :] += _colsum(dh)
        dx1 = dout_ref[...] + _rms_bwd(dy3 * g3v, xn, r3)
        dx1_ref[...] = dx1
        gt = mod_ref[GT_M:GT_M + 1, :]
        yn, r2 = _rms(y_ref[...])
        small_ref[3:4, :] += _colsum(dx1 * gt * yn)
        small_ref[4:5, :] += _colsum(dx1 * (yn * g2v))
        dy_ref[...] = _rms_bwd(dx1 * gt * g2v, yn, r2).astype(BF16)

    return pl.pallas_call(
        body, name="mid_bwd",
        out_shape=[jax.ShapeDtypeStruct((seq, D_MODEL), F32), jax.ShapeDtypeStruct((seq, D_MODEL), BF16),
                   jax.ShapeDtypeStruct((8, D_MODEL), F32)],
        grid=(seq // ROWS,),
        in_specs=[_tile(ROWS, D_MODEL)] * 4 + [_full((6, D_MODEL)), _full((1, D_MODEL)), _full((1, D_MODEL))],
        out_specs=[_tile(ROWS, D_MODEL), _tile(ROWS, D_MODEL), _full((8, D_MODEL))],
        compiler_params=_cparams(1),
    )(dh2, x1, dout, ymix, mod6, g3, g2)


def _pre_mix_bwd(x, dx1, mod6, g1, rows):
    seq = x.shape[0]

    def fn(dh, first, ins, outs):
        x_ref, dx1_ref, mod_ref, g_ref = ins
        dx_ref, small_ref = outs

        @pl.when(first)
        def _():
            small_ref[...] = jnp.zeros_like(small_ref)

        g1v = g_ref[...]
        xn, r = _rms(x_ref[...])
        dy = dh * (1.0 + mod_ref[SC_M:SC_M + 1, :])
        small_ref[0:1, :] += _colsum(dy * xn)
        small_ref[1:2, :] += _colsum(dh * (xn * g1v))
        small_ref[2:3, :] += _colsum(dh)
        dx_ref[...] = dx1_ref[...] + _rms_bwd(dy * g1v, xn, r)

    return _Epilogue(
        [x, dx1, mod6, g1],
        [_row_tile(rows, D_MODEL), _row_tile(rows, D_MODEL), _whole((6, D_MODEL)), _whole((1, D_MODEL))],
        [jax.ShapeDtypeStruct((seq, D_MODEL), F32), jax.ShapeDtypeStruct((8, D_MODEL), F32)],
        [_row_tile(rows, D_MODEL), _whole((8, D_MODEL))], fn, keep_product=False)


def _toeplitz_onehot(shape, offset_axis, top):
    m = lax.broadcasted_iota(jnp.int32, shape, offset_axis)
    i = lax.broadcasted_iota(jnp.int32, shape, 1 - offset_axis)
    return (i == jnp.clip(top - m, -MAX_REL, MAX_REL) + MAX_REL).astype(F32)


def _bias_table(rel_bias):
    width = GROUP_Q + GROUP_K

    def body(rb_ref, o_ref, t_ref):
        t_ref[...] = jnp.dot(rb_ref[...], _toeplitz_onehot((N_REL, width), 1, GROUP_K - 1), precision=HIGHEST,
                             preferred_element_type=F32)
        lane = lax.broadcasted_iota(jnp.int32, (N_HEADS, GROUP_K), 1)
        for r in range(GROUP_Q):
            first_key = (r // CHUNK) * CHUNK
            band = jnp.logical_and(lane >= first_key, lane < first_key + BAND)
            o_ref[r] = jnp.where(band, t_ref[:, GROUP_Q - 1 - r:GROUP_Q - 1 - r + GROUP_K], NEG_INF)

    return pl.pallas_call(
        body, name="bias_table", out_shape=jax.ShapeDtypeStruct((GROUP_Q, N_HEADS, GROUP_K), F32),
        scratch_shapes=[pltpu.VMEM((N_HEADS, width), F32)],
    )(rel_bias)


def _bias_grad(dbias_q):
    def body(d_ref, o_ref, t_ref):
        t_ref[...] = jnp.zeros_like(t_ref)
        for qi in range(CHUNK):
            t_ref[:, CHUNK - 1 - qi:CHUNK - 1 - qi + BAND] += d_ref[qi]
        o_ref[...] = jnp.dot(t_ref[...], _toeplitz_onehot((TOEPLITZ, N_REL), 0, BAND - 1), precision=HIGHEST,
                             preferred_element_type=F32)

    return pl.pallas_call(
        body, name="bias_grad", out_shape=jax.ShapeDtypeStruct((N_HEADS, N_REL), F32),
        scratch_shapes=[pltpu.VMEM((N_HEADS, TOEPLITZ), F32)],
    )(dbias_q)


def _load_resident(pairs, sems):
    copies = [pltpu.make_async_copy(src, dst, sems.at[n]) for n, (src, dst) in enumerate(pairs)]
    for cp in copies:
        cp.start()
    for cp in copies:
        cp.wait()


def _softmax_rows(s_ref, t_ref, valid, rows):
    s = s_ref[rows, :] * (HEAD_DIM ** -0.5) + t_ref[rows, :]
    s = jnp.where(valid, s, NEG_INF)
    e = jnp.exp(s - jnp.max(s, axis=-1, keepdims=True))
    return e / jnp.sum(e, axis=-1, keepdims=True)


def _valid_keys(g):
    kj = lax.broadcasted_iota(jnp.int32, (SOFTMAX_ROWS, GROUP_K), 1)
    return kj >= PAD_ROWS - g * GROUP_Q


def _attn_fwd(qkv, kpad, vpad, table, comm=None):
    seq = qkv.shape[0]

    def body(ins, outs, scratch):
        q_ref, k_hbm, v_hbm, t_hbm = ins
        (o_ref,) = outs
        k_ref, v_ref, t_ref, s_ref, p_ref, sems = scratch
        g = pl.program_id(0)

        @pl.when(g == 0)
        def _():
            _load_resident(((k_hbm, k_ref), (v_hbm, v_ref), (t_hbm, t_ref)), sems)

        window = pl.ds(pl.multiple_of(g * GROUP_Q, GROUP_Q), GROUP_K)
        valid = _valid_keys(g)
        for h in range(N_HEADS):
            cols = slice(h * HEAD_DIM, (h + 1) * HEAD_DIM)
            buf = h % 2
            s_ref[buf] = lax.dot_general(q_ref[:, cols], k_ref[window, cols], _DIMS["nt"],
                                         preferred_element_type=F32)
            for r in range(GROUP_Q // SOFTMAX_ROWS):
                rows = slice(r * SOFTMAX_ROWS, (r + 1) * SOFTMAX_ROWS)
                p_ref[buf, rows, :] = _softmax_rows(s_ref.at[buf], t_ref.at[h], valid, rows).astype(BF16)
            o_ref[:, cols] = jnp.dot(p_ref[buf], v_ref[window, cols], preferred_element_type=F32).astype(BF16)

    (ao,), extra = _host_call(
        body, "attn_fwd", grid=(seq // GROUP_Q,),
        in_specs=[_tile(GROUP_Q, D_ATTN), ANY, ANY, ANY], out_specs=[_tile(GROUP_Q, D_ATTN)],
        out_shape=[jax.ShapeDtypeStruct((seq, D_ATTN), BF16)],
        scratch_shapes=[pltpu.VMEM(kpad.shape, BF16), pltpu.VMEM(vpad.shape, BF16), pltpu.VMEM(table.shape, F32),
                        pltpu.VMEM((2, GROUP_Q, GROUP_K), F32), pltpu.VMEM((2, GROUP_Q, GROUP_K), BF16),
                        pltpu.SemaphoreType.DMA((3,))],
        args=[qkv, kpad, vpad, table], comm=comm)
    return ao, extra


def _attn_bwd(qkv, kpad, vpad, table, dao, comm=None):
    seq = qkv.shape[0]
    n_groups = seq // GROUP_Q
    fold_w = GROUP_K + (GROUP - 1) * CHUNK

    def body(ins, outs, scratch):
        q_ref, do_ref, k_hbm, v_hbm, t_hbm = ins
        dq_ref, dkt_hbm, dvt_hbm, db_ref, cs_ref = outs
        k_ref, v_ref, t_ref, db_acc, dkt_acc, dvt_acc, s_ref, dp_ref, p_ref, ds_ref, sems = scratch
        g = pl.program_id(0)

        @pl.when(g == 0)
        def _():
            _load_resident(((k_hbm, k_ref), (v_hbm, v_ref), (t_hbm, t_ref)), sems)
            db_acc[...] = jnp.zeros_like(db_acc)
            dkt_acc[...] = jnp.zeros_like(dkt_acc)
            dvt_acc[...] = jnp.zeros_like(dvt_acc)
            cs_ref[...] = jnp.zeros_like(cs_ref)

        window = pl.ds(pl.multiple_of(g * GROUP_Q, GROUP_Q), GROUP_K)
        valid = _valid_keys(g)
        for h in range(N_HEADS):
            cols = slice(h * HEAD_DIM, (h + 1) * HEAD_DIM)
            buf = h % 2
            qh, doh = q_ref[:, cols], do_ref[:, cols]
            kh, vh = k_ref[window, cols], v_ref[window, cols]
            s_ref[buf] = lax.dot_general(qh, kh, _DIMS["nt"], preferred_element_type=F32)
            dp_ref[buf] = lax.dot_general(doh, vh, _DIMS["nt"], preferred_element_type=F32)
            for r in range(GROUP_Q // SOFTMAX_ROWS):
                rows = slice(r * SOFTMAX_ROWS, (r + 1) * SOFTMAX_ROWS)
                p = _softmax_rows(s_ref.at[buf], t_ref.at[h], valid, rows)
                dp = dp_ref[buf, rows, :]
                ds = p * (dp - jnp.sum(dp * p, axis=-1, keepdims=True))
                chunk = (r * SOFTMAX_ROWS) // CHUNK
                shift = (GROUP - 1 - chunk) * CHUNK
                local = slice(r * SOFTMAX_ROWS - chunk * CHUNK, (r + 1) * SOFTMAX_ROWS - chunk * CHUNK)
                db_acc[h, local, shift:shift + GROUP_K] += ds
                p_ref[buf, rows, :] = p.astype(BF16)
                ds_ref[buf, rows, :] = (ds * (HEAD_DIM ** -0.5)).astype(BF16)
            dq_ref[:, cols] = jnp.dot(ds_ref[buf], kh, preferred_element_type=F32).astype(BF16)
            dkt_acc[cols, window] += lax.dot_general(qh, ds_ref[buf], _DIMS["tn"], preferred_element_type=F32)
            dvt_acc[cols, window] += lax.dot_general(doh, p_ref[buf], _DIMS["tn"], preferred_element_type=F32)
        cs_ref[0:1, :] += _colsum(dq_ref[...].astype(F32))

        @pl.when(g == n_groups - 1)
        def _():
            lo = (GROUP - 1) * CHUNK
            for h in range(N_HEADS):
                db_ref[h] = db_acc[h, :, lo:lo + BAND]
            inside = pl.ds(PAD_ROWS, seq)
            ones = jnp.ones((8, seq), F32)
            for row, acc in ((1, dkt_acc), (2, dvt_acc)):
                cs_ref[row:row + 1, :] = lax.dot_general(ones, acc[:, inside], _DIMS["nt"], precision=HIGHEST,
                                                         preferred_element_type=F32)[0:1, :]
            out_k = pltpu.make_async_copy(dkt_acc.at[:, inside], dkt_hbm, sems.at[0])
            out_v = pltpu.make_async_copy(dvt_acc.at[:, inside], dvt_hbm, sems.at[1])
            out_k.start()
            out_v.start()
            out_k.wait()
            out_v.wait()

    t_shape = (D_ATTN, seq + PAD_ROWS)
    outs, extra = _host_call(
        body, "attn_bwd", grid=(n_groups,),
        in_specs=[_tile(GROUP_Q, D_ATTN), _tile(GROUP_Q, D_ATTN), ANY, ANY, ANY],
        out_specs=[_tile(GROUP_Q, D_ATTN), ANY, ANY, _full((N_HEADS, CHUNK, BAND)), _full((8, D_ATTN))],
        out_shape=[jax.ShapeDtypeStruct((seq, D_ATTN), BF16), jax.ShapeDtypeStruct((D_ATTN, seq), F32),
                   jax.ShapeDtypeStruct((D_ATTN, seq), F32), jax.ShapeDtypeStruct((N_HEADS, CHUNK, BAND), F32),
                   jax.ShapeDtypeStruct((8, D_ATTN), F32)],
        scratch_shapes=[pltpu.VMEM(kpad.shape, BF16), pltpu.VMEM(vpad.shape, BF16), pltpu.VMEM(table.shape, F32),
                        pltpu.VMEM((N_HEADS, CHUNK, fold_w), F32), pltpu.VMEM(t_shape, F32),
                        pltpu.VMEM(t_shape, F32), pltpu.VMEM((2, GROUP_Q, GROUP_K), F32),
                        pltpu.VMEM((2, GROUP_Q, GROUP_K), F32), pltpu.VMEM((2, GROUP_Q, GROUP_K), BF16),
                        pltpu.VMEM((2, GROUP_Q, GROUP_K), BF16), pltpu.SemaphoreType.DMA((3,))],
        args=[qkv, dao, kpad, vpad, table], comm=comm)
    return outs, extra


def _assemble_dz(dq, dkt, dvt, dglu_a, dglu_b, dga, dgb):
    seq = dq.shape[0]
    rows = 512
    transposed = pl.BlockSpec((D_ATTN, rows), lambda i: (0, i))

    def body(dq_ref, dkt_ref, dvt_ref, da_ref, db_ref, dga_ref, dgb_ref, o_ref):
        o_ref[:, 0:D_ATTN] = dq_ref[...]
        o_ref[:, D_ATTN:2 * D_ATTN] = dkt_ref[...].T.astype(BF16)
        o_ref[:, 2 * D_ATTN:3 * D_ATTN] = dvt_ref[...].T.astype(BF16)
        off = 3 * D_ATTN
        for ref in (da_ref, db_ref, dga_ref, dgb_ref):
            width = ref.shape[1]
            o_ref[:, off:off + width] = ref[...]
            off += width

    width = 3 * D_ATTN + 2 * D_CONV + 2 * D_MODEL
    return pl.pallas_call(
        body, name="assemble_dz", out_shape=jax.ShapeDtypeStruct((seq, width), BF16), grid=(seq // rows,),
        in_specs=[_tile(rows, D_ATTN), transposed, transposed, _tile(rows, D_CONV), _tile(rows, D_CONV),
                  _tile(rows, D_MODEL), _tile(rows, D_MODEL)],
        out_specs=_tile(rows, width), compiler_params=_cparams(1),
    )(dq, dkt, dvt, dglu_a, dglu_b, dga, dgb)


CONV_ROWS = 256


def _ln_silu(u1, g, b):
    mu = jnp.mean(u1, axis=-1, keepdims=True)
    xc = u1 - mu
    rs = lax.rsqrt(jnp.mean(xc * xc, axis=-1, keepdims=True) + EPS)
    xhat = xc * rs
    u2 = xhat * g + b
    return xhat, rs, u2


def _glu_into(s_ref, a_ref, b_ref, ah_ref, bh_ref, first):
    halo = ah_ref[...] * _sig(bh_ref[...])
    s_ref[0:CONV_HALO, :] = jnp.where(first, 0.0, halo)
    s_ref[CONV_HALO:CONV_HALO + CONV_ROWS, :] = a_ref[...] * _sig(b_ref[...])


def _conv_fwd(zr, w_dw, b_dw, g_ln, b_ln, comm=None):
    seq = zr.shape[0]

    def body(a_ref, b_ref, ah_ref, bh_ref, w_ref, bias_ref, g_ref, bl_ref, u1_ref, u3_ref, s_ref):
        _glu_into(s_ref, a_ref, b_ref, ah_ref, bh_ref, pl.program_id(0) == 0)
        acc = jnp.zeros((CONV_ROWS, D_CONV), F32) + bias_ref[...]
        for j in range(CONV_K):
            acc = acc + w_ref[j:j + 1, :] * s_ref[2 + j:2 + j + CONV_ROWS, :]
        u1_ref[...] = acc
        _, _, u2 = _ln_silu(acc, g_ref[...], bl_ref[...])
        u3_ref[...] = (u2 * _sig(u2)).astype(BF16)

    return _host_call(
        lambda ins, outs, scratch: body(*ins, *outs, *scratch), "conv_fwd", grid=(seq // CONV_ROWS,),
        in_specs=[_tile(CONV_ROWS, D_CONV, 0), _tile(CONV_ROWS, D_CONV, 1),
                  _prev(CONV_HALO, D_CONV, CONV_ROWS, 0), _prev(CONV_HALO, D_CONV, CONV_ROWS, 1),
                  _full((CONV_K, D_CONV)), _full((1, D_CONV)), _full((1, D_CONV)), _full((1, D_CONV))],
        out_specs=[_tile(CONV_ROWS, D_CONV), _tile(CONV_ROWS, D_CONV)],
        out_shape=[jax.ShapeDtypeStruct((seq, D_CONV), F32), jax.ShapeDtypeStruct((seq, D_CONV), BF16)],
        scratch_shapes=[pltpu.VMEM((CONV_HALO + CONV_ROWS, D_CONV), F32)],
        args=[zr, zr, zr, zr, w_dw, b_dw, g_ln, b_ln], comm=comm)


def _conv_bwd(zr, u1, du3, w_dw, g_ln, b_ln, comm=None):
    seq = zr.shape[0]
    n_tiles = seq // CONV_ROWS
    n_halo = seq // CONV_HALO
    ext = CONV_ROWS + CONV_HALO

    def body(a_ref, b_ref, ah_ref, bh_ref, u1_ref, u1n_ref, d3_ref, d3n_ref, w_ref, g_ref, bl_ref,
             da_ref, db_ref, dw_ref, small_ref, s_ref, d_ref):
        i = pl.program_id(0)

        @pl.when(i == 0)
        def _():
            dw_ref[...] = jnp.zeros_like(dw_ref)
            small_ref[...] = jnp.zeros_like(small_ref)

        _glu_into(s_ref, a_ref, b_ref, ah_ref, bh_ref, i == 0)
        gv, bv = g_ref[...], bl_ref[...]

        def du1_of(u1, d3):
            xhat, rs, u2 = _ln_silu(u1, gv, bv)
            sg = _sig(u2)
            du2 = d3 * (sg * (1.0 + u2 * (1.0 - sg)))
            dxh = du2 * gv
            du1 = rs * (dxh - jnp.mean(dxh, axis=-1, keepdims=True)
                        - xhat * jnp.mean(dxh * xhat, axis=-1, keepdims=True))
            return du1, du2, xhat

        du1, du2, xhat = du1_of(u1_ref[...], d3_ref[...])
        du1n, _, _ = du1_of(u1n_ref[...], d3n_ref[...])
        d_ref[0:CONV_ROWS, :] = du1
        d_ref[CONV_ROWS:ext, :] = jnp.where(i == n_tiles - 1, 0.0, du1n)
        small_ref[0:1, :] += _colsum(du1)
        small_ref[1:2, :] += _colsum(du2 * xhat)
        small_ref[2:3, :] += _colsum(du2)
        du0 = jnp.zeros((CONV_ROWS, D_CONV), F32)
        for j in range(CONV_K):
            dw_ref[j:j + 1, :] += _colsum(du1 * s_ref[2 + j:2 + j + CONV_ROWS, :])
            du0 = du0 + w_ref[j:j + 1, :] * d_ref[CONV_K - 1 - j:CONV_K - 1 - j + CONV_ROWS, :]
        sb = _sig(b_ref[...])
        da = du0 * sb
        dbv = du0 * a_ref[...] * sb * (1.0 - sb)
        da_ref[...] = da.astype(BF16)
        db_ref[...] = dbv.astype(BF16)
        small_ref[3:4, :] += _colsum(da)
        small_ref[4:5, :] += _colsum(dbv)

    return _host_call(
        lambda ins, outs, scratch: body(*ins, *outs, *scratch), "conv_bwd", grid=(n_tiles,),
        in_specs=[_tile(CONV_ROWS, D_CONV, 0), _tile(CONV_ROWS, D_CONV, 1),
                  _prev(CONV_HALO, D_CONV, CONV_ROWS, 0), _prev(CONV_HALO, D_CONV, CONV_ROWS, 1),
                  _tile(CONV_ROWS, D_CONV), _next(CONV_HALO, D_CONV, CONV_ROWS, n_halo),
                  _tile(CONV_ROWS, D_CONV), _next(CONV_HALO, D_CONV, CONV_ROWS, n_halo),
                  _full((CONV_K, D_CONV)), _full((1, D_CONV)), _full((1, D_CONV))],
        out_specs=[_tile(CONV_ROWS, D_CONV), _tile(CONV_ROWS, D_CONV), _full((CONV_HALO, D_CONV)),
                   _full((8, D_CONV))],
        out_shape=[jax.ShapeDtypeStruct((seq, D_CONV), BF16), jax.ShapeDtypeStruct((seq, D_CONV), BF16),
                   jax.ShapeDtypeStruct((CONV_HALO, D_CONV), F32), jax.ShapeDtypeStruct((8, D_CONV), F32)],
        scratch_shapes=[pltpu.VMEM((ext, D_CONV), F32), pltpu.VMEM((ext, D_CONV), F32)],
        args=[zr, zr, zr, zr, u1, u1, du3, du3, w_dw, g_ln, b_ln], comm=comm)


MERGE_ROWS = 256


def _merge_fwd(ao, u3, zr, w_ao, w_co, b_co):
    seq = ao.shape[0]

    def body(ao_ref, u3_ref, ga_ref, gb_ref, wa_ref, wc_ref, bc_ref, y_ref, a_ref, cb_ref):
        a = jnp.dot(ao_ref[...], wa_ref[...], preferred_element_type=F32)
        cb = jnp.dot(u3_ref[...], wc_ref[...], preferred_element_type=F32) + bc_ref[...]
        a_ref[...] = a
        cb_ref[...] = cb
        y_ref[...] = (_sig(ga_ref[...]) * a + _sig(gb_ref[...]) * cb).astype(BF16)

    f32_out = jax.ShapeDtypeStruct((seq, D_MODEL), F32)
    return pl.pallas_call(
        body, name="merge_fwd",
        out_shape=[jax.ShapeDtypeStruct((seq, D_MODEL), BF16), f32_out, f32_out],
        grid=(seq // MERGE_ROWS,),
        in_specs=[_tile(MERGE_ROWS, D_ATTN), _tile(MERGE_ROWS, D_CONV), _tile(MERGE_ROWS, D_MODEL, 1),
                  _tile(MERGE_ROWS, D_MODEL, 2), _full(w_ao.shape), _full(w_co.shape), _full((1, D_MODEL))],
        out_specs=[_tile(MERGE_ROWS, D_MODEL)] * 3, compiler_params=_cparams(1),
    )(ao, u3, zr, zr, w_ao, w_co, b_co)


def _merge_bwd(dy, a, cb, zr):
    seq = dy.shape[0]

    def body(dy_ref, a_ref, cb_ref, ga_ref, gb_ref, da_ref, dcb_ref, dga_ref, dgb_ref, small_ref):
        i = pl.program_id(0)

        @pl.when(i == 0)
        def _():
            small_ref[...] = jnp.zeros_like(small_ref)

        dy_v = dy_ref[...]
        sa, sb = _sig(ga_ref[...]), _sig(gb_ref[...])
        dcb = dy_v * sb
        dga = dy_v * a_ref[...] * sa * (1.0 - sa)
        dgb = dy_v * cb_ref[...] * sb * (1.0 - sb)
        da_ref[...] = (dy_v * sa).astype(BF16)
        dcb_ref[...] = dcb.astype(BF16)
        dga_ref[...] = dga.astype(BF16)
        dgb_ref[...] = dgb.astype(BF16)
        small_ref[0:1, :] += _colsum(dga)
        small_ref[1:2, :] += _colsum(dgb)
        small_ref[2:3, :] += _colsum(dcb)

    bf = jax.ShapeDtypeStruct((seq, D_MODEL), BF16)
    return pl.pallas_call(
        body, name="merge_bwd", out_shape=[bf, bf, bf, bf, jax.ShapeDtypeStruct((8, D_MODEL), F32)],
        grid=(seq // MERGE_ROWS,),
        in_specs=[_tile(MERGE_ROWS, D_MODEL)] * 3 + [_tile(MERGE_ROWS, D_MODEL, 1), _tile(MERGE_ROWS, D_MODEL, 2)],
        out_specs=[_tile(MERGE_ROWS, D_MODEL)] * 4 + [_full((8, D_MODEL))], compiler_params=_cparams(1),
    )(dy, a, cb, zr, zr)


FFN_ROWS = 512
FFN_BLOCKS = D_FF // FFN_COLS
GELU_C = math.sqrt(2.0 / math.pi)


def _gelu(v):
    t = jnp.tanh(GELU_C * (v + 0.044715 * (v * v * v)))
    return 0.5 * v * (1.0 + t), t


def _gelu_grad(v, t):
    return 0.5 * (1.0 + t) + 0.5 * v * (1.0 - t * t) * (GELU_C * (1.0 + 3.0 * 0.044715 * (v * v)))


def _sublane_rows(ref, n):
    return [jnp.broadcast_to(ref[r:r + 1, :], (8, FFN_COLS)) for r in range(n)]


def _rolls(tile, shifts):
    return tuple(pltpu.roll(tile, s, 0) for s in shifts)


def _behind(prev_rolls, cur, row_id):
    rolls = _rolls(cur, (1, 2))
    x1 = jnp.where(row_id < 1, prev_rolls[0], rolls[0])
    x2 = jnp.where(row_id < 2, prev_rolls[1], rolls[1])
    return (x2, x1, cur), rolls


def _ahead(cur_rolls, next_rolls, row_id):
    return (jnp.where(row_id < 7, cur_rolls[0], next_rolls[0]), jnp.where(row_id < 6, cur_rolls[1], next_rolls[1]))


def _conv3(taps, w, bias):
    return w[0] * taps[0] + w[1] * taps[1] + w[2] * taps[2] + bias


def _ffn_specs(rows):
    tile = lambda off: pl.BlockSpec((rows, FFN_COLS), lambda j, i: (i, j + off))
    prev = lambda off: pl.BlockSpec((FFN_HALO, FFN_COLS),
                                    lambda j, i: (jnp.maximum(i * (rows // FFN_HALO) - 1, 0), j + off))
    wgt = lambda off: pl.BlockSpec((3, FFN_COLS), lambda j, i: (0, j + off))
    vec = lambda off: pl.BlockSpec((1, FFN_COLS), lambda j, i: (0, j + off))
    return tile, prev, wgt, vec


def _ffn_act(up, w_dw, b_dw):
    seq = up.shape[0]
    tile, prev, wgt, vec = _ffn_specs(FFN_ROWS)

    def body(v_ref, g_ref, vp_ref, gp_ref, wv_ref, wg_ref, bv_ref, bg_ref, act_ref):
        first = pl.program_id(1) == 0
        row_id = lax.broadcasted_iota(jnp.int32, (8, FFN_COLS), 0)
        wv, wg = _sublane_rows(wv_ref, 3), _sublane_rows(wg_ref, 3)
        (bv,), (bg,) = _sublane_rows(bv_ref, 1), _sublane_rows(bg_ref, 1)
        rolls_v = _rolls(jnp.where(first, 0.0, vp_ref[...]), (1, 2))
        rolls_g = _rolls(jnp.where(first, 0.0, gp_ref[...]), (1, 2))
        for row in range(0, FFN_ROWS, 16):
            halves = []
            for r in (row, row + 8):
                taps_v, rolls_v = _behind(rolls_v, v_ref[r:r + 8, :], row_id)
                taps_g, rolls_g = _behind(rolls_g, g_ref[r:r + 8, :], row_id)
                halves.append(_gelu(_conv3(taps_g, wg, bg))[0] * _conv3(taps_v, wv, bv))
            act_ref[row:row + 16, :] = jnp.concatenate(halves, axis=0).astype(BF16)

    return pl.pallas_call(
        body, name="ffn_act", out_shape=jax.ShapeDtypeStruct((seq, D_FF), BF16),
        grid=(FFN_BLOCKS, seq // FFN_ROWS),
        in_specs=[tile(0), tile(FFN_BLOCKS), prev(0), prev(FFN_BLOCKS), wgt(0), wgt(FFN_BLOCKS),
                  vec(0), vec(FFN_BLOCKS)],
        out_specs=tile(0), compiler_params=_cparams(2),
    )(up, up, up, up, w_dw, w_dw, b_dw, b_dw)


def _ffn_act_bwd(up, dact, w_dw, b_dw, comm=None):
    seq = up.shape[0]
    n_tiles = seq // FFN_ROWS
    n_halo = seq // FFN_HALO
    tile, prev, wgt, vec = _ffn_specs(FFN_ROWS)
    nxt = lambda off: pl.BlockSpec(
        (FFN_HALO, FFN_COLS), lambda j, i: (jnp.minimum((i + 1) * (FFN_ROWS // FFN_HALO), n_halo - 1), j + off))
    acc = lambda off: pl.BlockSpec((8, FFN_COLS), lambda j, i: (0, j + off))

    def body(v_ref, g_ref, vp_ref, gp_ref, vn_ref, gn_ref, da_ref, dan_ref, wv_ref, wg_ref, bv_ref, bg_ref,
             dv_out, dg_out, dwv_ref, dwg_ref, dbv_ref, dbg_ref):
        i = pl.program_id(1)
        first, last = i == 0, i == n_tiles - 1

        @pl.when(first)
        def _():
            for r in (dwv_ref, dwg_ref, dbv_ref, dbg_ref):
                r[...] = jnp.zeros_like(r)

        row_id = lax.broadcasted_iota(jnp.int32, (8, FFN_COLS), 0)
        wv, wg = _sublane_rows(wv_ref, 3), _sublane_rows(wg_ref, 3)
        (bv,), (bg,) = _sublane_rows(bv_ref, 1), _sublane_rows(bg_ref, 1)
        zero = jnp.zeros((8, FFN_COLS), F32)
        sums_v, sums_g = [zero] * 4, [zero] * 4
        rolls_v = _rolls(jnp.where(first, 0.0, vp_ref[...]), (1, 2))
        rolls_g = _rolls(jnp.where(first, 0.0, gp_ref[...]), (1, 2))
        behind = None
        done_v, done_g = [], []

        def grads(v_tile, g_tile, dact, rolls_v, rolls_g):
            taps_v, rolls_v = _behind(rolls_v, v_tile, row_id)
            taps_g, rolls_g = _behind(rolls_g, g_tile, row_id)
            val, gate = _conv3(taps_v, wv, bv), _conv3(taps_g, wg, bg)
            gel, t = _gelu(gate)
            return dact * gel, dact * val * _gelu_grad(gate, t), taps_v, taps_g, rolls_v, rolls_g

        def finish(tile, nxt, row):
            for (d, d_rolls), (_, n_rolls), w, done, o_ref in ((tile[0], nxt[0], wv, done_v, dv_out),
                                                               (tile[1], nxt[1], wg, done_g, dg_out)):
                d1, d2 = _ahead(d_rolls, n_rolls, row_id)
                done.append(w[2] * d + w[1] * d1 + w[0] * d2)
                if len(done) == 2:
                    o_ref[row - 16:row, :] = jnp.concatenate(done, axis=0).astype(BF16)
                    done.clear()

        for row in range(0, FFN_ROWS, 16):
            dact16 = da_ref[row:row + 16, :].astype(F32)
            for r, dact in ((row, dact16[0:8, :]), (row + 8, dact16[8:16, :])):
                dval, dgate, taps_v, taps_g, rolls_v, rolls_g = grads(v_ref[r:r + 8, :], g_ref[r:r + 8, :], dact,
                                                                      rolls_v, rolls_g)
                sums_v = [s + dval * x for s, x in zip(sums_v, taps_v)] + [sums_v[3] + dval]
                sums_g = [s + dgate * x for s, x in zip(sums_g, taps_g)] + [sums_g[3] + dgate]
                tile = ((dval, _rolls(dval, (7, 6))), (dgate, _rolls(dgate, (7, 6))))
                if behind is not None:
                    finish(behind, tile, r)
                behind = tile
        dact_next = jnp.where(last, 0.0, dan_ref[...].astype(F32)[0:FFN_HALO, :])
        dval, dgate, *_ = grads(vn_ref[...], gn_ref[...], dact_next, rolls_v, rolls_g)
        finish(behind, ((dval, _rolls(dval, (7, 6))), (dgate, _rolls(dgate, (7, 6)))), FFN_ROWS)
        for sums, dw_ref, db_ref in ((sums_v, dwv_ref, dbv_ref), (sums_g, dwg_ref, dbg_ref)):
            for tap in range(3):
                dw_ref[tap:tap + 1, :] += _colsum(sums[tap])
            db_ref[0:1, :] += _colsum(sums[3])

    half = jax.ShapeDtypeStruct((seq, D_FF), BF16)
    acc_shape = jax.ShapeDtypeStruct((8, D_FF), F32)
    return _host_call(
        lambda ins, outs, scratch: body(*ins, *outs, *scratch), "ffn_act_bwd", grid=(FFN_BLOCKS, n_tiles),
        in_specs=[tile(0), tile(FFN_BLOCKS), prev(0), prev(FFN_BLOCKS), nxt(0), nxt(FFN_BLOCKS),
                  tile(0), pl.BlockSpec((16, FFN_COLS), lambda j, i: (
                      jnp.minimum((i + 1) * (FFN_ROWS // 16), seq // 16 - 1), j)),
                  wgt(0), wgt(FFN_BLOCKS), vec(0), vec(FFN_BLOCKS)],
        out_specs=[tile(0), tile(0), acc(0), acc(0), acc(0), acc(0)],
        out_shape=[half, half, acc_shape, acc_shape, acc_shape, acc_shape],
        scratch_shapes=[], args=[up, up, up, up, up, up, dact, dact, w_dw, w_dw, b_dw, b_dw], comm=comm)


def _cols_to_blocks(full_cols):
    k, n8 = full_cols.shape
    return jnp.transpose(full_cols.reshape(k, N_DEV, n8 // N_DEV), (1, 0, 2))


def _rows_to_blocks(full_rows):
    r8, n = full_rows.shape
    return full_rows.reshape(N_DEV, r8 // N_DEV, n)


def _blocks_to_cols(gathered):
    _, k, n = gathered.shape
    return jnp.transpose(gathered, (1, 0, 2)).reshape(k, N_DEV * n)


def kernel(x, c, w_ada, b_ada, g_pre_mix, g_post_mix, w_in, b_in, rel_bias, w_attn_o, w_dw_conv, b_dw_conv, g_conv_ln, b_conv_ln, w_conv_o, b_conv_o, w_mix_o, g_pre_ffn, g_post_ffn, w_up, w_dw_ffn, b_dw_ffn, w_down, loss_target, m_w_ada, m_b_ada, m_g_pre_mix, m_g_post_mix, m_w_in, m_b_in, m_rel_bias, m_w_attn_o, m_w_dw_conv, m_b_dw_conv, m_g_conv_ln, m_b_conv_ln, m_w_conv_o, m_b_conv_o, m_w_mix_o, m_g_pre_ffn, m_g_post_ffn, m_w_up, m_w_dw_ffn, m_b_dw_ffn, m_w_down, v_w_ada, v_b_ada, v_g_pre_mix, v_g_post_mix, v_w_in, v_b_in, v_rel_bias, v_w_attn_o, v_w_dw_conv, v_b_dw_conv, v_g_conv_ln, v_b_conv_ln, v_w_conv_o, v_b_conv_o, v_w_mix_o, v_g_pre_ffn, v_g_post_ffn, v_w_up, v_w_dw_ffn, v_b_dw_ffn, v_w_down):
    names = ["w_ada", "b_ada", "g_pre_mix", "g_post_mix", "w_in", "b_in", "rel_bias", "w_attn_o", "w_dw_conv",
             "b_dw_conv", "g_conv_ln", "b_conv_ln", "w_conv_o", "b_conv_o", "w_mix_o", "g_pre_ffn", "g_post_ffn",
             "w_up", "w_dw_ffn", "b_dw_ffn", "w_down"]
    weights = dict(zip(names, [w_ada, b_ada, g_pre_mix, g_post_mix, w_in, b_in, rel_bias, w_attn_o, w_dw_conv,
                               b_dw_conv, g_conv_ln, b_conv_ln, w_conv_o, b_conv_o, w_mix_o, g_pre_ffn,
                               g_post_ffn, w_up, w_dw_ffn, b_dw_ffn, w_down]))
    mom_m = dict(zip(names, [m_w_ada, m_b_ada, m_g_pre_mix, m_g_post_mix, m_w_in, m_b_in, m_rel_bias, m_w_attn_o,
                             m_w_dw_conv, m_b_dw_conv, m_g_conv_ln, m_b_conv_ln, m_w_conv_o, m_b_conv_o,
                             m_w_mix_o, m_g_pre_ffn, m_g_post_ffn, m_w_up, m_w_dw_ffn, m_b_dw_ffn, m_w_down]))
    mom_v = dict(zip(names, [v_w_ada, v_b_ada, v_g_pre_mix, v_g_post_mix, v_w_in, v_b_in, v_rel_bias, v_w_attn_o,
                             v_w_dw_conv, v_b_dw_conv, v_g_conv_ln, v_b_conv_ln, v_w_conv_o, v_b_conv_o,
                             v_w_mix_o, v_g_pre_ffn, v_g_post_ffn, v_w_up, v_w_dw_ffn, v_b_dw_ffn, v_w_down]))
    shapes = {n: w.shape for n, w in weights.items()}

    seq = x.shape[1]
    me = 4 * lax.axis_index("x") + 2 * lax.axis_index("y") + lax.axis_index("c")
    x2 = x.reshape(seq, D_MODEL)
    target = loss_target.reshape(seq, D_MODEL)
    sq = lambda a: a.reshape(a.shape[1:])
    bf = lambda a: sq(a).astype(BF16)

    c_act = _silu_vec(c)
    c_all, g_in, g_dwc, g_dwf = _run_comm(
        _gather_comm([c_act, bf(w_in), sq(w_dw_conv), sq(w_dw_ffn)]), "gather_first")
    c_all = c_all.reshape(N_DEV, D_MODEL)
    wf_in = _blocks_to_cols(g_in)
    wf_dwc = _blocks_to_cols(g_dwc)
    wf_dwf = _blocks_to_cols(g_dwf)

    (mod_all,) = _run_comm(_gather_comm([_ada_fwd(c_all, sq(w_ada))]), "gather_mod")
    mod = lax.dynamic_index_in_dim(mod_all, me, axis=1, keepdims=False)
    mod6 = (mod.reshape(1, 6 * D_MODEL) + b_ada).reshape(6, D_MODEL)

    h1 = _pre_mix(x2, mod6, g_pre_mix)
    qkv = _mm(h1, wf_in, "nn", BF16, "in_proj_qkv", bias=b_in, tm=1024, tn=768, cols=(0, 3 * D_ATTN))
    zr, _, (g_ao, g_co, g_mo) = _mm(h1, wf_in, "nn", F32, "in_proj_rest", bias=b_in, tm=1024, tn=3 * D_ATTN,
                                 cols=(3 * D_ATTN, 2 * D_CONV + 2 * D_MODEL),
                                 comm=_gather_comm([bf(w_attn_o), bf(w_conv_o), bf(w_mix_o)]))
    kpad = jnp.pad(qkv[:, D_ATTN:2 * D_ATTN], ((PAD_ROWS, 0), (0, 0)))
    vpad = jnp.pad(qkv[:, 2 * D_ATTN:], ((PAD_ROWS, 0), (0, 0)))
    table = jnp.transpose(_bias_table(sq(rel_bias)), (1, 0, 2))
    ao, (g_up,) = _attn_fwd(qkv, kpad, vpad, table, comm=_gather_comm([bf(w_up)]))
    (u1, u3), (g_dn,) = _conv_fwd(zr, wf_dwc, b_dw_conv, g_conv_ln, b_conv_ln, comm=_gather_comm([bf(w_down)]))
    wf_ao = _blocks_to_cols(g_ao)
    wf_co = _blocks_to_cols(g_co)
    wf_mo = g_mo.reshape(D_MODEL, D_MODEL)
    wf_up = _blocks_to_cols(g_up)
    wf_dn = g_dn.reshape(D_FF, D_MODEL)
    y, a_br, cb_br = _merge_fwd(ao, u3, zr, wf_ao, wf_co, b_conv_o)
    ymix, (x1, h2), _ = _mm(y, wf_mo, "nn", F32, "mix_o", tm=512, tn=D_MODEL,
                            epilogue=_post_mix_pre_ffn(x2, mod6, g_post_mix, g_pre_ffn, 512))
    up = _mm(h2, wf_up, "nn", F32, "ffn_up", tm=1024, tn=1408)
    act = _ffn_act(up, wf_dwf, b_dw_ffn)
    _, (loss_lanes, dout, dyf, small_f), _ = _mm(act, wf_dn, "nn", F32, "ffn_down", tm=512, tn=D_MODEL,
                                                 epilogue=_final(x1, target, mod6, g_post_ffn, 512))

    dact = _mm(dyf, wf_dn, "nt", BF16, "ffn_down_dx", tm=1024, tn=1408)
    gw_down = _mm(act, dyf, "tn", BF16, "ffn_down_dw", tm=256, tn=1024)
    (dup_v, dup_g, dwv, dwg, dbv, dbg), (parts_down,) = _ffn_act_bwd(
        up, dact, wf_dwf, b_dw_ffn, comm=_scatter_comm([_rows_to_blocks(gw_down)]))
    dh2 = _mm([dup_v, dup_g], wf_up, "nt", F32, "ffn_up_dx")
    blocks_up = _mm_tn_blocks(h2, [dup_v, dup_g], 2 * D_FF // N_DEV, "ffn_up_dw", k_steps=2)
    dx1, dymix, small_m = _mid_bwd(dh2, x1, dout, ymix, mod6, g_pre_ffn, g_post_mix)
    dy = _mm(dymix, wf_mo, "nt", F32, "mix_o_dx", tm=1024, tn=1024)
    gw_mo = _mm(y, dymix, "tn", BF16, "mix_o_dw")
    da, dcb, dga, dgb, small_g = _merge_bwd(dy, a_br, cb_br, zr)
    dao = _mm(da, wf_ao, "nt", BF16, "attn_o_dx", tm=1024)
    gw_ao = _mm(ao, da, "tn", BF16, "attn_o_dw")
    du3 = _mm(dcb, wf_co, "nt", F32, "conv_o_dx", tm=1024)
    gw_co = _mm(u3, dcb, "tn", BF16, "conv_o_dw")
    (dq, dkt, dvt, dbias, small_a), (parts_up,) = _attn_bwd(
        qkv, kpad, vpad, table, dao, comm=_scatter_comm([blocks_up]))
    g_rel = _bias_grad(jnp.transpose(dbias, (1, 0, 2)))
    (dglu_a, dglu_b, dw_conv, small_c), (parts_mo, parts_ao, parts_co) = _conv_bwd(
        zr, u1, du3, wf_dwc, g_conv_ln, b_conv_ln,
        comm=_scatter_comm([_rows_to_blocks(gw_mo), _cols_to_blocks(gw_ao), _cols_to_blocks(gw_co)]))
    dz = _assemble_dz(dq, dkt, dvt, dglu_a, dglu_b, dga, dgb)
    blocks_in = _mm_tn_blocks(h1, [dz], dz.shape[1] // N_DEV, "in_proj_dw")
    _, (grad_x, small_x), (parts_in,) = _mm(dz, wf_in, "nt", F32, "in_proj_dx", tm=512, tn=D_MODEL,
                                            comm=_scatter_comm([blocks_in]),
                                            epilogue=_pre_mix_bwd(x2, dx1, mod6, g_pre_mix, 512))

    packed = _pack_grads(small_x, small_m, small_f, small_g, small_a, small_c, dbv, dbg, dwv, dwg, dw_conv)
    gathered, gathered_rel, gathered_loss = _run_comm(_gather_comm([packed, g_rel, loss_lanes]), "gather_small")
    gathered = gathered.reshape(N_DEV, PACKED_TOTAL)
    updates, g_dwc_full, g_dwf_full, loss_all = _small_adamw(gathered, gathered_rel, gathered_loss, weights, mom_m,
                                                             mom_v)
    loss = loss_all[0, 0]

    grads, deltas, new_m, new_v = {}, {}, {}, {}

    def record(name, update):
        for dst, val in zip((grads, deltas, new_m, new_v), update):
            dst[name] = val.reshape(shapes[name])

    for name, update in updates.items():
        record(name, update)

    def local_update(name, grad):
        record(name, _adamw(sq(weights[name]), sq(mom_m[name]), sq(mom_v[name]), "adamw_" + name, g=grad))

    conv_cols, ffn_cols, ada_cols = D_CONV // N_DEV, 2 * D_FF // N_DEV, 6 * D_MODEL // N_DEV
    local_update("w_dw_conv", lax.dynamic_slice(g_dwc_full, (0, me * conv_cols), (CONV_K, conv_cols)))
    local_update("w_dw_ffn", lax.dynamic_slice(g_dwf_full, (0, me * ffn_cols), (3, ffn_cols)))
    local_update("w_ada", _ada_grad(c_all, lax.dynamic_slice(gathered, (0, me * ada_cols), (N_DEV, ada_cols))))

    for name, part in (("w_in", parts_in), ("w_attn_o", parts_ao), ("w_conv_o", parts_co), ("w_mix_o", parts_mo),
                       ("w_up", parts_up), ("w_down", parts_down)):
        record(name, _adamw(sq(weights[name]), sq(mom_m[name]), sq(mom_v[name]), "adamw_" + name, parts=part))

    return (loss, grad_x.reshape(x.shape), *[grads[n] for n in names], *[deltas[n] for n in names],
            *[new_m[n] for n in names], *[new_v[n] for n in names])
```

```python
import functools
import math

import jax
import jax.numpy as jnp
from jax import lax
from jax.experimental import pallas as pl
from jax.experimental.pallas import tpu as pltpu

F32 = jnp.float32
BF16 = jnp.bfloat16
HIGHEST = lax.Precision.HIGHEST

D_MODEL = 1024
CHUNK = 64
LEFT_CHUNKS = 8
BAND = (LEFT_CHUNKS + 1) * CHUNK
PAD_ROWS = LEFT_CHUNKS * CHUNK
GROUP = 4
GROUP_Q = GROUP * CHUNK
GROUP_K = GROUP_Q + PAD_ROWS
SOFTMAX_ROWS = 16
TOEPLITZ = 640
N_HEADS = 8
HEAD_DIM = 64
D_ATTN = 512
D_CONV = 512
CONV_K = 31
CONV_HALO = 32
MAX_REL = 128
N_REL = 2 * MAX_REL + 1
D_FF = 2816
FFN_HALO = 8
FFN_COLS = 256
EPS = 1e-6
NEG_INF = -1e30
N_DEV = 8

ADAM_LR = 0.001
ADAM_B1 = 0.9
ADAM_B2 = 0.999
ADAM_EPS = 1e-08
ADAM_WD = 0.01
ADAM_STEP = 10

VMEM_LIMIT_BYTES = 56 * 1024 * 1024
ADAMW_BLOCK_BYTES = 768 * 1024

MESH = pl.DeviceIdType.MESH
ANY = pl.BlockSpec(memory_space=pl.ANY)

SH_M, SC_M, GT_M, SH_F, SC_F, GT_F = range(6)

SMALL = (("b_ada", 6144), ("g_pre_mix", 1024), ("g_post_mix", 1024), ("b_in", 4608), ("b_dw_conv", 512),
         ("g_conv_ln", 512), ("b_conv_ln", 512), ("b_conv_o", 1024), ("g_pre_ffn", 1024), ("g_post_ffn", 1024),
         ("b_dw_ffn", 5632))
PACKED_TOTAL = sum(n for _, n in SMALL) + CONV_K * D_CONV + 3 * 2 * D_FF


def _cparams(n_axes):
    return pltpu.CompilerParams(vmem_limit_bytes=VMEM_LIMIT_BYTES,
                                dimension_semantics=("arbitrary",) * n_axes)


def _sig(v):
    return 1.0 / (1.0 + jnp.exp(-v))


def _pick(n, target):
    if n <= target:
        return n
    t = target - target % 128
    while n % t:
        t -= 128
    return t


def _tile(rows, cols, col=0):
    return pl.BlockSpec((rows, cols), lambda i: (i, col))


def _full(shape):
    zeros = (0,) * len(shape)
    return pl.BlockSpec(shape, lambda i: zeros)


def _prev(halo, cols, rows, col=0):
    return pl.BlockSpec((halo, cols), lambda i: (jnp.maximum(i * (rows // halo) - 1, 0), col))


def _next(halo, cols, rows, n_blocks, col=0):
    return pl.BlockSpec((halo, cols), lambda i: (jnp.minimum((i + 1) * (rows // halo), n_blocks - 1), col))


class _Comm:
    def __init__(self, inputs, out_shapes, sems, start, finish):
        self.inputs, self.out_shapes, self.sems, self.start, self.finish = inputs, out_shapes, sems, start, finish


def _host_call(body, name, grid, in_specs, out_specs, out_shape, scratch_shapes, args, comm=None):
    n_in, n_out, n_scr = len(args), len(out_shape), len(scratch_shapes)
    c_in = list(comm.inputs) if comm else []
    c_out = list(comm.out_shapes) if comm else []
    c_sem = list(comm.sems) if comm else []

    def full(*refs):
        bounds = [0, n_in, len(c_in), n_out, len(c_out), n_scr, len(c_sem)]
        cuts = [sum(bounds[:i + 1]) for i in range(len(bounds))]
        ins, cins, outs, couts, scr, csems = (refs[lo:hi] for lo, hi in zip(cuts[:-1], cuts[1:]))
        if comm:
            first = functools.reduce(jnp.logical_and, [pl.program_id(ax) == 0 for ax in range(len(grid))])
            pl.when(first)(lambda: comm.start(cins, couts, csems))
        body(ins, outs, scr)
        if comm:
            last = functools.reduce(jnp.logical_and, [pl.program_id(ax) == grid[ax] - 1 for ax in range(len(grid))])
            pl.when(last)(lambda: comm.finish(cins, couts, csems))

    res = pl.pallas_call(
        full, name=name, grid=grid, in_specs=list(in_specs) + [ANY] * len(c_in),
        out_specs=list(out_specs) + [ANY] * len(c_out), out_shape=list(out_shape) + c_out,
        scratch_shapes=list(scratch_shapes) + c_sem, compiler_params=_cparams(len(grid)),
    )(*args, *c_in)
    return list(res[:n_out]), list(res[n_out:])


def _run_comm(comm, name):
    n_in, n_out = len(comm.inputs), len(comm.out_shapes)

    def body(*refs):
        ins, outs, sems = refs[:n_in], refs[n_in:n_in + n_out], refs[n_in + n_out:]
        comm.start(ins, outs, sems)
        comm.finish(ins, outs, sems)

    return pl.pallas_call(
        body, name=name, out_shape=list(comm.out_shapes), in_specs=[ANY] * n_in, out_specs=[ANY] * n_out,
        scratch_shapes=list(comm.sems),
    )(*comm.inputs)


def _place():
    return lax.axis_index("x"), lax.axis_index("y"), lax.axis_index("c")


def _gather_comm(arrs):
    n = len(arrs)

    def plan(ins, outs, sems):
        send_sems, recv_sems, local_sems = sems
        x, y, c = _place()
        me, sibling = (x, y, c), (x, y, 1 - c)
        chips = [(1 - x, y), (x, 1 - y), (1 - x, 1 - y)]

        def block(k, p):
            return outs[k].at[4 * p[0] + 2 * p[1] + p[2]]

        def copy(k, s, blk, to, src=None):
            return pltpu.make_async_remote_copy(
                src_ref=block(k, blk) if src is None else src, dst_ref=block(k, blk),
                send_sem=send_sems.at[7 * k + s], recv_sem=recv_sems.at[7 * k + s],
                device_id=to, device_id_type=MESH)

        mine = [pltpu.make_async_copy(ins[k], block(k, me), local_sems.at[k]) for k in range(n)]
        first = []
        for k in range(n):
            first.append(copy(k, 0, me, sibling, src=ins[k]))
            for j, chip in enumerate(chips):
                first.append(copy(k, 1 + j, me, (*chip, c), src=ins[k]))
        return me, sibling, chips, c, copy, mine, first

    def start(ins, outs, sems):
        *_, mine, first = plan(ins, outs, sems)
        for cp in mine + first:
            cp.start()

    def finish(ins, outs, sems):
        me, sibling, chips, c, copy, mine, first = plan(ins, outs, sems)
        passed = []
        for j, chip in enumerate(chips):
            for k in range(n):
                copy(k, 1 + j, (*chip, c), me).wait_recv()
                fwd = copy(k, 4 + j, (*chip, c), sibling)
                fwd.start()
                passed.append(fwd)
        for k in range(n):
            copy(k, 0, sibling, me).wait_recv()
        for j, chip in enumerate(chips):
            for k in range(n):
                copy(k, 4 + j, (*chip, 1 - c), me).wait_recv()
        for cp in first + passed:
            cp.wait_send()
        for cp in mine:
            cp.wait()

    return _Comm(list(arrs), [jax.ShapeDtypeStruct((N_DEV,) + a.shape, a.dtype) for a in arrs],
                 [pltpu.SemaphoreType.DMA((7 * n,)), pltpu.SemaphoreType.DMA((7 * n,)),
                  pltpu.SemaphoreType.DMA((n,))], start, finish)


def _scatter_comm(blocks):
    n = len(blocks)

    def plan(ins, outs, sems, arrivals):
        send_sems, recv_sems, local_sems = sems
        x, y, c = _place()
        me = 4 * x + 2 * y + c
        local = [pltpu.make_async_copy(ins[k].at[me], outs[k].at[me], local_sems.at[k]) for k in range(n)]
        sends, recvs = [], []
        for k in range(n):
            for mask in range(1, N_DEV):
                px = 1 - x if mask & 4 else x
                py = 1 - y if mask & 2 else y
                pc = 1 - c if mask & 1 else c
                peer = 4 * px + 2 * py + pc
                sem = 7 * k + mask - 1
                both = dict(send_sem=send_sems.at[sem], recv_sem=recv_sems.at[sem], device_id=(px, py, pc),
                            device_id_type=MESH)
                sends.append(pltpu.make_async_remote_copy(src_ref=ins[k].at[peer], dst_ref=outs[k].at[me], **both))
                if arrivals:
                    recvs.append(pltpu.make_async_remote_copy(src_ref=ins[k].at[me], dst_ref=outs[k].at[peer],
                                                              **both))
        return local, sends, recvs

    def start(ins, outs, sems):
        local, sends, _ = plan(ins, outs, sems, arrivals=False)
        for cp in local + sends:
            cp.start()

    def finish(ins, outs, sems):
        local, sends, recvs = plan(ins, outs, sems, arrivals=True)
        for cp in recvs:
            cp.wait_recv()
        for cp in sends:
            cp.wait_send()
        for cp in local:
            cp.wait()

    return _Comm(list(blocks), [jax.ShapeDtypeStruct(b.shape, b.dtype) for b in blocks],
                 [pltpu.SemaphoreType.DMA((7 * n,)), pltpu.SemaphoreType.DMA((7 * n,)),
                  pltpu.SemaphoreType.DMA((n,))], start, finish)


_DIMS = {"nn": (((1,), (0,)), ((), ())), "nt": (((1,), (1,)), ((), ())), "tn": (((0,), (0,)), ((), ()))}


class _Epilogue:
    def __init__(self, args, in_specs, out_shapes, out_specs, fn, keep_product):
        self.args, self.in_specs, self.out_shapes, self.out_specs = args, in_specs, out_shapes, out_specs
        self.fn, self.keep_product = fn, keep_product


def _row_tile(rows, cols):
    return pl.BlockSpec((rows, cols), lambda i, j: (i, 0))


def _whole(shape):
    zeros = (0,) * len(shape)
    return pl.BlockSpec(shape, lambda i, j: zeros)


def _mm(a, b, mode, out_dtype, name, bias=None, tm=512, tn=512, comm=None, cols=None, epilogue=None):
    pieces = a if isinstance(a, (list, tuple)) else [a]
    assert all(p.dtype == BF16 for p in pieces) and b.dtype == BF16
    a = pieces[0]
    if mode == "tn":
        k_dim, m_dim = a.shape
    else:
        m_dim, k_dim = a.shape
    n_dim = b.shape[0] if mode == "nt" else b.shape[1]
    col0 = 0
    if cols is not None:
        assert mode == "nn" and cols[0] % tn == 0 and cols[1] % tn == 0
        col0, n_dim = cols[0] // tn, cols[1]
    tm, tn = _pick(m_dim, tm), _pick(n_dim, tn)
    a_specs = [pl.BlockSpec((k_dim, tm), lambda i, j: (0, i)) if mode == "tn"
               else pl.BlockSpec((tm, k_dim), lambda i, j: (i, 0))] * len(pieces)
    if mode == "nt":
        b_specs = [pl.BlockSpec((tn, k_dim), lambda i, j, p=p: (j, p)) for p in range(len(pieces))]
    else:
        assert len(pieces) == 1
        b_specs = [pl.BlockSpec((k_dim, tn), lambda i, j: (0, j + col0))]
    in_specs = a_specs + b_specs
    args = list(pieces) + [b] * len(pieces)
    if bias is not None:
        in_specs.append(pl.BlockSpec((1, tn), lambda i, j: (0, j + col0)))
        args.append(bias)
    dims = _DIMS[mode]
    n_pieces = len(pieces)
    n_own = len(args)
    keep = epilogue is None or epilogue.keep_product
    out_specs = [pl.BlockSpec((tm, tn), lambda i, j: (i, j))] if keep else []
    out_shape = [jax.ShapeDtypeStruct((m_dim, n_dim), out_dtype)] if keep else []
    if epilogue is not None:
        assert tn == n_dim
        in_specs, args = in_specs + list(epilogue.in_specs), args + list(epilogue.args)
        out_specs, out_shape = out_specs + list(epilogue.out_specs), out_shape + list(epilogue.out_shapes)

    def body(ins, outs, scratch):
        total = lax.dot_general(ins[0][...], ins[n_pieces][...], dims, preferred_element_type=F32)
        for p in range(1, n_pieces):
            total = total + lax.dot_general(ins[p][...], ins[n_pieces + p][...], dims, preferred_element_type=F32)
        if bias is not None:
            total = total + ins[2 * n_pieces][...]
        if keep:
            outs[0][...] = total.astype(out_dtype)
        if epilogue is not None:
            epilogue.fn(total, pl.program_id(0) == 0, ins[n_own:], outs[1:] if keep else outs)

    outs, extra = _host_call(body, name, grid=(m_dim // tm, n_dim // tn), in_specs=in_specs, out_specs=out_specs,
                             out_shape=out_shape, scratch_shapes=[], args=args, comm=comm)
    product = outs[0] if keep else None
    if comm is None and epilogue is None:
        return product
    return product, outs[1:] if keep else outs, extra


def _mm_tn_blocks(a, bs, n_dev_cols, name, tm=512, k_steps=1):
    k_dim, m_dim = a.shape
    n = n_dev_cols
    pair = 2 * n
    assert pair % 128 == 0 and all(b.shape[1] % pair == 0 for b in bs) and k_dim % k_steps == 0
    counts = [b.shape[1] // pair for b in bs]
    firsts = [sum(counts[:q]) for q in range(len(bs))]
    assert sum(counts) == N_DEV // 2
    tm, tk = _pick(m_dim, tm), k_dim // k_steps

    def b_spec(first, count):
        return pl.BlockSpec((tk, pair), lambda i, j, k: (k, jnp.clip(j - first, 0, count - 1)))

    def body(ins, outs, scratch):
        a_ref, b_refs, o_ref, s_ref = ins[0], ins[1:], outs[0], scratch[0]
        j, k = pl.program_id(1), pl.program_id(2)
        for b_ref, first, count in zip(b_refs, firsts, counts):
            @pl.when(jnp.logical_and(j >= first, j < first + count))
            def _(b_ref=b_ref):
                part = lax.dot_general(a_ref[...], b_ref[...], _DIMS["tn"], preferred_element_type=F32)
                if k_steps == 1:
                    s_ref[...] = part
                else:
                    @pl.when(k == 0)
                    def _():
                        s_ref[...] = part

                    @pl.when(k > 0)
                    def _():
                        s_ref[...] += part

        @pl.when(k == k_steps - 1)
        def _():
            o_ref[0] = s_ref[:, 0:n].astype(BF16)
            o_ref[1] = s_ref[:, n:pair].astype(BF16)

    (out,), _ = _host_call(
        body, name, grid=(m_dim // tm, N_DEV // 2, k_steps),
        in_specs=[pl.BlockSpec((tk, tm), lambda i, j, k: (k, i))] + [b_spec(f, c) for f, c in zip(firsts, counts)],
        out_specs=[pl.BlockSpec((2, tm, n), lambda i, j, k: (j, i, 0))],
        out_shape=[jax.ShapeDtypeStruct((N_DEV, m_dim, n), BF16)],
        scratch_shapes=[pltpu.VMEM((tm, pair), F32)], args=[a] + list(bs))
    return out


def _adam_math(w, g, m, v):
    m = ADAM_B1 * m + (1.0 - ADAM_B1) * g
    v = ADAM_B2 * v + (1.0 - ADAM_B2) * (g * g)
    m_hat = m / (1.0 - ADAM_B1 ** ADAM_STEP)
    v_hat = v / (1.0 - ADAM_B2 ** ADAM_STEP)
    delta = -ADAM_LR * (m_hat / (jnp.sqrt(v_hat) + ADAM_EPS) + ADAM_WD * w)
    return delta, m, v


def _adamw(w, m, v, name, g=None, parts=None):
    rows, cols = w.shape
    tr = rows
    if rows * cols * 4 > ADAMW_BLOCK_BYTES:
        tr = max(t for t in range(16, rows, 16) if rows % t == 0 and t * cols * 4 <= ADAMW_BLOCK_BYTES)

    def body(w_ref, m_ref, v_ref, g_ref, go_ref, d_ref, mo_ref, vo_ref):
        if parts is None:
            grad = g_ref[...]
        else:
            grad = g_ref[0].astype(F32)
            for d in range(1, N_DEV):
                grad = grad + g_ref[d].astype(F32)
        delta, m_new, v_new = _adam_math(w_ref[...], grad, m_ref[...], v_ref[...])
        go_ref[...] = grad
        d_ref[...] = delta
        mo_ref[...] = m_new
        vo_ref[...] = v_new

    spec = _tile(tr, cols)
    g_spec = spec if parts is None else pl.BlockSpec((N_DEV, tr, cols), lambda i: (0, i, 0))
    shape = jax.ShapeDtypeStruct((rows, cols), F32)
    return pl.pallas_call(
        body, name=name, out_shape=[shape] * 4, grid=(rows // tr,),
        in_specs=[spec, spec, spec, g_spec], out_specs=[spec] * 4, compiler_params=_cparams(1),
    )(w, m, v, g if parts is None else parts)


def _pack_grads(small_x, small_m, small_f, small_g, small_a, small_c, dbv, dbg, dwv, dwg, dw_conv):
    pieces = [
        (small_x, 2, D_MODEL), (small_x, 1, D_MODEL), (small_m, 4, D_MODEL), (small_m, 2, D_MODEL),
        (small_m, 1, D_MODEL), (small_f, 1, D_MODEL),
        (small_x, 0, D_MODEL), (small_m, 3, D_MODEL),
        (small_a, 0, D_ATTN), (small_a, 1, D_ATTN), (small_a, 2, D_ATTN), (small_c, 3, D_CONV),
        (small_c, 4, D_CONV), (small_g, 0, D_MODEL), (small_g, 1, D_MODEL),
        (small_c, 0, D_CONV), (small_c, 1, D_CONV), (small_c, 2, D_CONV),
        (small_g, 2, D_MODEL), (small_m, 0, D_MODEL), (small_f, 0, D_MODEL),
        (dbv, 0, D_FF), (dbg, 0, D_FF),
    ]
    pieces += [(dw_conv, j, D_CONV) for j in range(CONV_K)]
    pieces += [(src, tap, D_FF) for tap in range(3) for src in (dwv, dwg)]
    sources = [small_x, small_m, small_f, small_g, small_a, small_c, dbv, dbg, dwv, dwg, dw_conv]
    assert sum(width for _, _, width in pieces) == PACKED_TOTAL

    def body(*refs):
        o_ref = refs[-1]
        ref_of = {id(src): ref for src, ref in zip(sources, refs)}
        off = 0
        for src, row, width in pieces:
            o_ref[:, off:off + width] = ref_of[id(src)][row:row + 1, :]
            off += width

    return pl.pallas_call(body, name="pack_grads", out_shape=jax.ShapeDtypeStruct((1, PACKED_TOTAL), F32))(*sources)


def _small_adamw(gathered, gathered_rel, gathered_loss, weights, mom_m, mom_v):
    vec_names = [name for name, _ in SMALL]
    states = []
    for name in vec_names + ["rel_bias"]:
        states += [weights[name], mom_m[name], mom_v[name]]
    states = [a.reshape(a.shape[1:]) if a.ndim == 3 else a for a in states]
    n_state = len(states)

    def body(*refs):
        g_ref, rel_ref, loss_ref = refs[0], refs[1], refs[2]
        state_refs, out_refs = refs[3:3 + n_state], refs[3 + n_state:]
        total = g_ref[0:1, :]
        rel = rel_ref[0]
        loss = loss_ref[0]
        for d in range(1, N_DEV):
            total = total + g_ref[d:d + 1, :]
            rel = rel + rel_ref[d]
            loss = loss + loss_ref[d]
        off = 0
        for n, (name, width) in enumerate(SMALL):
            grad = total[:, off:off + width]
            w_ref, m_ref, v_ref = state_refs[3 * n:3 * n + 3]
            for ref, val in zip(out_refs[4 * n:4 * n + 4], (grad,) + _adam_math(w_ref[...], grad, m_ref[...], v_ref[...])):
                ref[...] = val
            off += width
        n = len(SMALL)
        w_ref, m_ref, v_ref = state_refs[3 * n:3 * n + 3]
        for ref, val in zip(out_refs[4 * n:4 * n + 4], (rel,) + _adam_math(w_ref[...], rel, m_ref[...], v_ref[...])):
            ref[...] = val
        dwc_ref, dwf_ref, loss_out = out_refs[4 * n + 4:]
        loss_out[...] = 0.5 * loss
        dwc_ref[...] = jnp.zeros_like(dwc_ref)
        dwf_ref[...] = jnp.zeros_like(dwf_ref)
        for j in range(CONV_K):
            dwc_ref[j:j + 1, :] = total[:, off:off + D_CONV]
            off += D_CONV
        for tap in range(3):
            dwf_ref[tap:tap + 1, :] = total[:, off:off + 2 * D_FF]
            off += 2 * D_FF

    out_shape = []
    for k in range(n_state // 3):
        out_shape += [jax.ShapeDtypeStruct(states[3 * k].shape, F32)] * 4
    out_shape += [jax.ShapeDtypeStruct((CONV_HALO, D_CONV), F32), jax.ShapeDtypeStruct((8, 2 * D_FF), F32),
                  jax.ShapeDtypeStruct((1, 128), F32)]
    res = pl.pallas_call(
        body, name="small_adamw", out_shape=out_shape,
        compiler_params=pltpu.CompilerParams(vmem_limit_bytes=VMEM_LIMIT_BYTES),
    )(gathered, gathered_rel, gathered_loss, *states)
    updates = {name: tuple(res[4 * n:4 * n + 4]) for n, name in enumerate(vec_names + ["rel_bias"])}
    return updates, res[-3], res[-2], res[-1]


def _silu_vec(c):
    def body(c_ref, o_ref):
        v = c_ref[...]
        o_ref[...] = v * _sig(v)

    return pl.pallas_call(body, name="silu_c", out_shape=jax.ShapeDtypeStruct(c.shape, F32))(c)


def _ada_fwd(c_all, w_shard):
    def body(c_ref, w_ref, o_ref):
        o_ref[...] = jnp.dot(c_ref[...], w_ref[...], precision=HIGHEST, preferred_element_type=F32)

    return pl.pallas_call(
        body, name="ada_fwd", out_shape=jax.ShapeDtypeStruct((N_DEV, w_shard.shape[1]), F32),
        compiler_params=pltpu.CompilerParams(vmem_limit_bytes=VMEM_LIMIT_BYTES),
    )(c_all, w_shard)


def _ada_grad(c_all, dmod_shard):
    def body(c_ref, d_ref, o_ref):
        o_ref[...] = lax.dot_general(c_ref[...], d_ref[...], _DIMS["tn"], precision=HIGHEST,
                                     preferred_element_type=F32)

    return pl.pallas_call(
        body, name="ada_grad", out_shape=jax.ShapeDtypeStruct((D_MODEL, dmod_shard.shape[1]), F32),
        compiler_params=pltpu.CompilerParams(vmem_limit_bytes=VMEM_LIMIT_BYTES),
    )(c_all, dmod_shard)


ROWS = 256


def _rms(v):
    r = lax.rsqrt(jnp.mean(v * v, axis=-1, keepdims=True) + EPS)
    return v * r, r


def _rms_bwd(dxn, xn, r):
    return r * (dxn - xn * jnp.mean(dxn * xn, axis=-1, keepdims=True))


def _colsum(v):
    return jnp.sum(v, axis=0, keepdims=True)


def _pre_mix(x, mod6, g1):
    seq = x.shape[0]

    def body(x_ref, mod_ref, g_ref, h_ref):
        xn, _ = _rms(x_ref[...])
        y = xn * g_ref[...]
        h_ref[...] = (y * (1.0 + mod_ref[SC_M:SC_M + 1, :]) + mod_ref[SH_M:SH_M + 1, :]).astype(BF16)

    return pl.pallas_call(
        body, name="pre_mix", out_shape=jax.ShapeDtypeStruct((seq, D_MODEL), BF16), grid=(seq // ROWS,),
        in_specs=[_tile(ROWS, D_MODEL), _full((6, D_MODEL)), _full((1, D_MODEL))],
        out_specs=_tile(ROWS, D_MODEL), compiler_params=_cparams(1),
    )(x, mod6, g1)


def _post_mix_pre_ffn(x, mod6, g2, g3, rows):
    seq = x.shape[0]

    def fn(y, first, ins, outs):
        x_ref, mod_ref, g2_ref, g3_ref = ins
        x1_ref, h_ref = outs
        yn, _ = _rms(y)
        x1 = x_ref[...] + mod_ref[GT_M:GT_M + 1, :] * (yn * g2_ref[...])
        x1_ref[...] = x1
        xn, _ = _rms(x1)
        y3 = xn * g3_ref[...]
        h_ref[...] = (y3 * (1.0 + mod_ref[SC_F:SC_F + 1, :]) + mod_ref[SH_F:SH_F + 1, :]).astype(BF16)

    return _Epilogue(
        [x, mod6, g2, g3], [_row_tile(rows, D_MODEL), _whole((6, D_MODEL)), _whole((1, D_MODEL)), _whole((1, D_MODEL))],
        [jax.ShapeDtypeStruct((seq, D_MODEL), F32), jax.ShapeDtypeStruct((seq, D_MODEL), BF16)],
        [_row_tile(rows, D_MODEL), _row_tile(rows, D_MODEL)], fn, keep_product=True)


def _final(x1, target, mod6, g4, rows):
    seq = x1.shape[0]

    def fn(y, first, ins, outs):
        x1_ref, t_ref, mod_ref, g_ref = ins
        loss_ref, dout_ref, dyf_ref, small_ref = outs

        @pl.when(first)
        def _():
            loss_ref[...] = jnp.zeros_like(loss_ref)
            small_ref[...] = jnp.zeros_like(small_ref)

        gt = mod_ref[GT_F:GT_F + 1, :]
        g4v = g_ref[...]
        yn, r = _rms(y)
        out = x1_ref[...] + gt * (yn * g4v)
        err = out - t_ref[...]
        loss_ref[...] += jnp.sum(jnp.mean(err * err, axis=-1, keepdims=True))
        dout = err * (1.0 / D_MODEL)
        dout_ref[...] = dout
        small_ref[0:1, :] += _colsum(dout * gt * yn)
        small_ref[1:2, :] += _colsum(dout * (yn * g4v))
        dyf_ref[...] = _rms_bwd(dout * gt * g4v, yn, r).astype(BF16)

    return _Epilogue(
        [x1, target, mod6, g4],
        [_row_tile(rows, D_MODEL), _row_tile(rows, D_MODEL), _whole((6, D_MODEL)), _whole((1, D_MODEL))],
        [jax.ShapeDtypeStruct((1, 128), F32), jax.ShapeDtypeStruct((seq, D_MODEL), F32),
         jax.ShapeDtypeStruct((seq, D_MODEL), BF16), jax.ShapeDtypeStruct((8, D_MODEL), F32)],
        [_whole((1, 128)), _row_tile(rows, D_MODEL), _row_tile(rows, D_MODEL), _whole((8, D_MODEL))],
        fn, keep_product=False)


def _mid_bwd(x1, dout, ymix, mod6, g3, g2, rows):
    seq = x1.shape[0]

    def fn(dh, first, ins, outs):
        x1_ref, dout_ref, y_ref, mod_ref, g3_ref, g2_ref = ins
        dx1_ref, dy_ref, small_ref = outs

        @pl.when(first)
        def _():
            small_ref[...] = jnp.zeros_like(small_ref)

        g3v, g2v = g3_ref[...], g2_ref[...]
        xn, r3 = _rms(x1_ref[...])
        y3 = xn * g3v
        dy3 = dh * (1.0 + mod_ref[SC_F:SC_F + 1, :])
        small_ref[0:1, :] += _colsum(dy3 * xn)
        small_ref[1:2, :] += _colsum(dh * y3)
        small_ref[2:3, :] += _colsum(dh)
        dx1 = dout_ref[...] + _rms_bwd(dy3 * g3v, xn, r3)
        dx1_ref[...] = dx1
        gt = mod_ref[GT_M:GT_M + 1, :]
        yn, r2 = _rms(y_ref[...])
        small_ref[3:4, :] += _colsum(dx1 * gt * yn)
        small_ref[4:5, :] += _colsum(dx1 * (yn * g2v))
        dy_ref[...] = _rms_bwd(dx1 * gt * g2v, yn, r2).astype(BF16)

    return _Epilogue(
        [x1, dout, ymix, mod6, g3, g2],
        [_row_tile(rows, D_MODEL)] * 3 + [_whole((6, D_MODEL)), _whole((1, D_MODEL)), _whole((1, D_MODEL))],
        [jax.ShapeDtypeStruct((seq, D_MODEL), F32), jax.ShapeDtypeStruct((seq, D_MODEL), BF16),
         jax.ShapeDtypeStruct((8, D_MODEL), F32)],
        [_row_tile(rows, D_MODEL), _row_tile(rows, D_MODEL), _whole((8, D_MODEL))], fn, keep_product=False)


def _pre_mix_bwd(x, dx1, mod6, g1, rows):
    seq = x.shape[0]

    def fn(dh, first, ins, outs):
        x_ref, dx1_ref, mod_ref, g_ref = ins
        dx_ref, small_ref = outs

        @pl.when(first)
        def _():
            small_ref[...] = jnp.zeros_like(small_ref)

        g1v = g_ref[...]
        xn, r = _rms(x_ref[...])
        dy = dh * (1.0 + mod_ref[SC_M:SC_M + 1, :])
        small_ref[0:1, :] += _colsum(dy * xn)
        small_ref[1:2, :] += _colsum(dh * (xn * g1v))
        small_ref[2:3, :] += _colsum(dh)
        dx_ref[...] = dx1_ref[...] + _rms_bwd(dy * g1v, xn, r)

    return _Epilogue(
        [x, dx1, mod6, g1],
        [_row_tile(rows, D_MODEL), _row_tile(rows, D_MODEL), _whole((6, D_MODEL)), _whole((1, D_MODEL))],
        [jax.ShapeDtypeStruct((seq, D_MODEL), F32), jax.ShapeDtypeStruct((8, D_MODEL), F32)],
        [_row_tile(rows, D_MODEL), _whole((8, D_MODEL))], fn, keep_product=False)


def _toeplitz_onehot(shape, offset_axis, top):
    m = lax.broadcasted_iota(jnp.int32, shape, offset_axis)
    i = lax.broadcasted_iota(jnp.int32, shape, 1 - offset_axis)
    return (i == jnp.clip(top - m, -MAX_REL, MAX_REL) + MAX_REL).astype(F32)


def _bias_table(rel_bias):
    width = GROUP_Q + GROUP_K

    def body(rb_ref, o_ref, t_ref):
        t_ref[...] = jnp.dot(rb_ref[...], _toeplitz_onehot((N_REL, width), 1, GROUP_K - 1), precision=HIGHEST,
                             preferred_element_type=F32)
        lane = lax.broadcasted_iota(jnp.int32, (N_HEADS, GROUP_K), 1)
        for r in range(GROUP_Q):
            first_key = (r // CHUNK) * CHUNK
            band = jnp.logical_and(lane >= first_key, lane < first_key + BAND)
            o_ref[r] = jnp.where(band, t_ref[:, GROUP_Q - 1 - r:GROUP_Q - 1 - r + GROUP_K], NEG_INF)

    return pl.pallas_call(
        body, name="bias_table", out_shape=jax.ShapeDtypeStruct((GROUP_Q, N_HEADS, GROUP_K), F32),
        scratch_shapes=[pltpu.VMEM((N_HEADS, width), F32)],
    )(rel_bias)


def _bias_grad(dbias_q):
    def body(d_ref, o_ref, t_ref):
        t_ref[...] = jnp.zeros_like(t_ref)
        for qi in range(CHUNK):
            t_ref[:, CHUNK - 1 - qi:CHUNK - 1 - qi + BAND] += d_ref[qi]
        o_ref[...] = jnp.dot(t_ref[...], _toeplitz_onehot((TOEPLITZ, N_REL), 0, BAND - 1), precision=HIGHEST,
                             preferred_element_type=F32)

    return pl.pallas_call(
        body, name="bias_grad", out_shape=jax.ShapeDtypeStruct((N_HEADS, N_REL), F32),
        scratch_shapes=[pltpu.VMEM((N_HEADS, TOEPLITZ), F32)],
    )(dbias_q)


def _load_resident(pairs, sems):
    copies = [pltpu.make_async_copy(src, dst, sems.at[n]) for n, (src, dst) in enumerate(pairs)]
    for cp in copies:
        cp.start()
    for cp in copies:
        cp.wait()


def _softmax_rows(s_ref, t_ref, valid, rows):
    s = s_ref[rows, :] * (HEAD_DIM ** -0.5) + t_ref[rows, :]
    s = jnp.where(valid, s, NEG_INF)
    e = jnp.exp(s - jnp.max(s, axis=-1, keepdims=True))
    return e / jnp.sum(e, axis=-1, keepdims=True)


def _valid_keys(g):
    kj = lax.broadcasted_iota(jnp.int32, (SOFTMAX_ROWS, GROUP_K), 1)
    return kj >= PAD_ROWS - g * GROUP_Q


def _attn_fwd(qkv, kpad, vpad, table, comm=None):
    seq = qkv.shape[0]

    def body(ins, outs, scratch):
        q_ref, k_hbm, v_hbm, t_hbm = ins
        (o_ref,) = outs
        k_ref, v_ref, t_ref, s_ref, p_ref, sems = scratch
        g = pl.program_id(0)

        @pl.when(g == 0)
        def _():
            _load_resident(((k_hbm, k_ref), (v_hbm, v_ref), (t_hbm, t_ref)), sems)

        window = pl.ds(pl.multiple_of(g * GROUP_Q, GROUP_Q), GROUP_K)
        valid = _valid_keys(g)
        for h in range(N_HEADS):
            cols = slice(h * HEAD_DIM, (h + 1) * HEAD_DIM)
            buf = h % 2
            s_ref[buf] = lax.dot_general(q_ref[:, cols], k_ref[window, cols], _DIMS["nt"],
                                         preferred_element_type=F32)
            for r in range(GROUP_Q // SOFTMAX_ROWS):
                rows = slice(r * SOFTMAX_ROWS, (r + 1) * SOFTMAX_ROWS)
                p_ref[buf, rows, :] = _softmax_rows(s_ref.at[buf], t_ref.at[h], valid, rows).astype(BF16)
            o_ref[:, cols] = jnp.dot(p_ref[buf], v_ref[window, cols], preferred_element_type=F32).astype(BF16)

    (ao,), extra = _host_call(
        body, "attn_fwd", grid=(seq // GROUP_Q,),
        in_specs=[_tile(GROUP_Q, D_ATTN), ANY, ANY, ANY], out_specs=[_tile(GROUP_Q, D_ATTN)],
        out_shape=[jax.ShapeDtypeStruct((seq, D_ATTN), BF16)],
        scratch_shapes=[pltpu.VMEM(kpad.shape, BF16), pltpu.VMEM(vpad.shape, BF16), pltpu.VMEM(table.shape, F32),
                        pltpu.VMEM((2, GROUP_Q, GROUP_K), F32), pltpu.VMEM((2, GROUP_Q, GROUP_K), BF16),
                        pltpu.SemaphoreType.DMA((3,))],
        args=[qkv, kpad, vpad, table], comm=comm)
    return ao, extra


def _attn_bwd(qkv, kpad, vpad, table, dao, comm=None):
    seq = qkv.shape[0]
    n_groups = seq // GROUP_Q
    fold_w = GROUP_K + (GROUP - 1) * CHUNK

    def body(ins, outs, scratch):
        q_ref, do_ref, k_hbm, v_hbm, t_hbm = ins
        dq_ref, dkt_hbm, dvt_hbm, db_ref, cs_ref = outs
        k_ref, v_ref, t_ref, db_acc, dkt_acc, dvt_acc, s_ref, dp_ref, p_ref, ds_ref, sems = scratch
        g = pl.program_id(0)

        @pl.when(g == 0)
        def _():
            _load_resident(((k_hbm, k_ref), (v_hbm, v_ref), (t_hbm, t_ref)), sems)
            db_acc[...] = jnp.zeros_like(db_acc)
            dkt_acc[...] = jnp.zeros_like(dkt_acc)
            dvt_acc[...] = jnp.zeros_like(dvt_acc)
            cs_ref[...] = jnp.zeros_like(cs_ref)

        window = pl.ds(pl.multiple_of(g * GROUP_Q, GROUP_Q), GROUP_K)
        valid = _valid_keys(g)
        for h in range(N_HEADS):
            cols = slice(h * HEAD_DIM, (h + 1) * HEAD_DIM)
            buf = h % 2
            qh, doh = q_ref[:, cols], do_ref[:, cols]
            kh, vh = k_ref[window, cols], v_ref[window, cols]
            s_ref[buf] = lax.dot_general(qh, kh, _DIMS["nt"], preferred_element_type=F32)
            dp_ref[buf] = lax.dot_general(doh, vh, _DIMS["nt"], preferred_element_type=F32)
            for r in range(GROUP_Q // SOFTMAX_ROWS):
                rows = slice(r * SOFTMAX_ROWS, (r + 1) * SOFTMAX_ROWS)
                p = _softmax_rows(s_ref.at[buf], t_ref.at[h], valid, rows)
                dp = dp_ref[buf, rows, :]
                ds = p * (dp - jnp.sum(dp * p, axis=-1, keepdims=True))
                chunk = (r * SOFTMAX_ROWS) // CHUNK
                shift = (GROUP - 1 - chunk) * CHUNK
                local = slice(r * SOFTMAX_ROWS - chunk * CHUNK, (r + 1) * SOFTMAX_ROWS - chunk * CHUNK)
                db_acc[h, local, shift:shift + GROUP_K] += ds
                p_ref[buf, rows, :] = p.astype(BF16)
                ds_ref[buf, rows, :] = (ds * (HEAD_DIM ** -0.5)).astype(BF16)
            dq_ref[:, cols] = jnp.dot(ds_ref[buf], kh, preferred_element_type=F32).astype(BF16)
            dkt_acc[cols, window] += lax.dot_general(qh, ds_ref[buf], _DIMS["tn"], preferred_element_type=F32)
            dvt_acc[cols, window] += lax.dot_general(doh, p_ref[buf], _DIMS["tn"], preferred_element_type=F32)
        cs_ref[0:1, :] += _colsum(dq_ref[...].astype(F32))

        @pl.when(g == n_groups - 1)
        def _():
            lo = (GROUP - 1) * CHUNK
            for h in range(N_HEADS):
                db_ref[h] = db_acc[h, :, lo:lo + BAND]
            inside = pl.ds(PAD_ROWS, seq)
            ones = jnp.ones((8, seq), F32)
            for row, acc in ((1, dkt_acc), (2, dvt_acc)):
                cs_ref[row:row + 1, :] = lax.dot_general(ones, acc[:, inside], _DIMS["nt"], precision=HIGHEST,
                                                         preferred_element_type=F32)[0:1, :]
            out_k = pltpu.make_async_copy(dkt_acc.at[:, inside], dkt_hbm, sems.at[0])
            out_v = pltpu.make_async_copy(dvt_acc.at[:, inside], dvt_hbm, sems.at[1])
            out_k.start()
            out_v.start()
            out_k.wait()
            out_v.wait()

    t_shape = (D_ATTN, seq + PAD_ROWS)
    outs, extra = _host_call(
        body, "attn_bwd", grid=(n_groups,),
        in_specs=[_tile(GROUP_Q, D_ATTN), _tile(GROUP_Q, D_ATTN), ANY, ANY, ANY],
        out_specs=[_tile(GROUP_Q, D_ATTN), ANY, ANY, _full((N_HEADS, CHUNK, BAND)), _full((8, D_ATTN))],
        out_shape=[jax.ShapeDtypeStruct((seq, D_ATTN), BF16), jax.ShapeDtypeStruct((D_ATTN, seq), F32),
                   jax.ShapeDtypeStruct((D_ATTN, seq), F32), jax.ShapeDtypeStruct((N_HEADS, CHUNK, BAND), F32),
                   jax.ShapeDtypeStruct((8, D_ATTN), F32)],
        scratch_shapes=[pltpu.VMEM(kpad.shape, BF16), pltpu.VMEM(vpad.shape, BF16), pltpu.VMEM(table.shape, F32),
                        pltpu.VMEM((N_HEADS, CHUNK, fold_w), F32), pltpu.VMEM(t_shape, F32),
                        pltpu.VMEM(t_shape, F32), pltpu.VMEM((2, GROUP_Q, GROUP_K), F32),
                        pltpu.VMEM((2, GROUP_Q, GROUP_K), F32), pltpu.VMEM((2, GROUP_Q, GROUP_K), BF16),
                        pltpu.VMEM((2, GROUP_Q, GROUP_K), BF16), pltpu.SemaphoreType.DMA((3,))],
        args=[qkv, dao, kpad, vpad, table], comm=comm)
    return outs, extra


def _assemble_dz(dq, dkt, dvt, dglu_a, dglu_b, dga, dgb):
    seq = dq.shape[0]
    rows = 512
    transposed = pl.BlockSpec((D_ATTN, rows), lambda i: (0, i))

    def body(dq_ref, dkt_ref, dvt_ref, da_ref, db_ref, dga_ref, dgb_ref, o_ref):
        o_ref[:, 0:D_ATTN] = dq_ref[...]
        o_ref[:, D_ATTN:2 * D_ATTN] = dkt_ref[...].T.astype(BF16)
        o_ref[:, 2 * D_ATTN:3 * D_ATTN] = dvt_ref[...].T.astype(BF16)
        off = 3 * D_ATTN
        for ref in (da_ref, db_ref, dga_ref, dgb_ref):
            width = ref.shape[1]
            o_ref[:, off:off + width] = ref[...]
            off += width

    width = 3 * D_ATTN + 2 * D_CONV + 2 * D_MODEL
    return pl.pallas_call(
        body, name="assemble_dz", out_shape=jax.ShapeDtypeStruct((seq, width), BF16), grid=(seq // rows,),
        in_specs=[_tile(rows, D_ATTN), transposed, transposed, _tile(rows, D_CONV), _tile(rows, D_CONV),
                  _tile(rows, D_MODEL), _tile(rows, D_MODEL)],
        out_specs=_tile(rows, width), compiler_params=_cparams(1),
    )(dq, dkt, dvt, dglu_a, dglu_b, dga, dgb)


CONV_ROWS = 256


def _ln_silu(u1, g, b):
    mu = jnp.mean(u1, axis=-1, keepdims=True)
    xc = u1 - mu
    rs = lax.rsqrt(jnp.mean(xc * xc, axis=-1, keepdims=True) + EPS)
    xhat = xc * rs
    u2 = xhat * g + b
    return xhat, rs, u2


def _glu_into(s_ref, a_ref, b_ref, ah_ref, bh_ref, first):
    halo = ah_ref[...] * _sig(bh_ref[...])
    s_ref[0:CONV_HALO, :] = jnp.where(first, 0.0, halo)
    s_ref[CONV_HALO:CONV_HALO + CONV_ROWS, :] = a_ref[...] * _sig(b_ref[...])


def _conv_fwd(zr, w_dw, b_dw, g_ln, b_ln, comm=None):
    seq = zr.shape[0]

    def body(a_ref, b_ref, ah_ref, bh_ref, w_ref, bias_ref, g_ref, bl_ref, u1_ref, u3_ref, s_ref):
        _glu_into(s_ref, a_ref, b_ref, ah_ref, bh_ref, pl.program_id(0) == 0)
        acc = jnp.zeros((CONV_ROWS, D_CONV), F32) + bias_ref[...]
        for j in range(CONV_K):
            acc = acc + w_ref[j:j + 1, :] * s_ref[2 + j:2 + j + CONV_ROWS, :]
        u1_ref[...] = acc
        _, _, u2 = _ln_silu(acc, g_ref[...], bl_ref[...])
        u3_ref[...] = (u2 * _sig(u2)).astype(BF16)

    return _host_call(
        lambda ins, outs, scratch: body(*ins, *outs, *scratch), "conv_fwd", grid=(seq // CONV_ROWS,),
        in_specs=[_tile(CONV_ROWS, D_CONV, 0), _tile(CONV_ROWS, D_CONV, 1),
                  _prev(CONV_HALO, D_CONV, CONV_ROWS, 0), _prev(CONV_HALO, D_CONV, CONV_ROWS, 1),
                  _full((CONV_K, D_CONV)), _full((1, D_CONV)), _full((1, D_CONV)), _full((1, D_CONV))],
        out_specs=[_tile(CONV_ROWS, D_CONV), _tile(CONV_ROWS, D_CONV)],
        out_shape=[jax.ShapeDtypeStruct((seq, D_CONV), F32), jax.ShapeDtypeStruct((seq, D_CONV), BF16)],
        scratch_shapes=[pltpu.VMEM((CONV_HALO + CONV_ROWS, D_CONV), F32)],
        args=[zr, zr, zr, zr, w_dw, b_dw, g_ln, b_ln], comm=comm)


def _conv_bwd(zr, u1, du3, w_dw, g_ln, b_ln, comm=None):
    seq = zr.shape[0]
    n_tiles = seq // CONV_ROWS
    n_halo = seq // CONV_HALO
    ext = CONV_ROWS + CONV_HALO

    def body(a_ref, b_ref, ah_ref, bh_ref, u1_ref, u1n_ref, d3_ref, d3n_ref, w_ref, g_ref, bl_ref,
             da_ref, db_ref, dw_ref, small_ref, s_ref, d_ref):
        i = pl.program_id(0)

        @pl.when(i == 0)
        def _():
            dw_ref[...] = jnp.zeros_like(dw_ref)
            small_ref[...] = jnp.zeros_like(small_ref)

        _glu_into(s_ref, a_ref, b_ref, ah_ref, bh_ref, i == 0)
        gv, bv = g_ref[...], bl_ref[...]

        def du1_of(u1, d3):
            xhat, rs, u2 = _ln_silu(u1, gv, bv)
            sg = _sig(u2)
            du2 = d3 * (sg * (1.0 + u2 * (1.0 - sg)))
            dxh = du2 * gv
            du1 = rs * (dxh - jnp.mean(dxh, axis=-1, keepdims=True)
                        - xhat * jnp.mean(dxh * xhat, axis=-1, keepdims=True))
            return du1, du2, xhat

        du1, du2, xhat = du1_of(u1_ref[...], d3_ref[...])
        du1n, _, _ = du1_of(u1n_ref[...], d3n_ref[...])
        d_ref[0:CONV_ROWS, :] = du1
        d_ref[CONV_ROWS:ext, :] = jnp.where(i == n_tiles - 1, 0.0, du1n)
        small_ref[0:1, :] += _colsum(du1)
        small_ref[1:2, :] += _colsum(du2 * xhat)
        small_ref[2:3, :] += _colsum(du2)
        du0 = jnp.zeros((CONV_ROWS, D_CONV), F32)
        for j in range(CONV_K):
            dw_ref[j:j + 1, :] += _colsum(du1 * s_ref[2 + j:2 + j + CONV_ROWS, :])
            du0 = du0 + w_ref[j:j + 1, :] * d_ref[CONV_K - 1 - j:CONV_K - 1 - j + CONV_ROWS, :]
        sb = _sig(b_ref[...])
        da = du0 * sb
        dbv = du0 * a_ref[...] * sb * (1.0 - sb)
        da_ref[...] = da.astype(BF16)
        db_ref[...] = dbv.astype(BF16)
        small_ref[3:4, :] += _colsum(da)
        small_ref[4:5, :] += _colsum(dbv)

    return _host_call(
        lambda ins, outs, scratch: body(*ins, *outs, *scratch), "conv_bwd", grid=(n_tiles,),
        in_specs=[_tile(CONV_ROWS, D_CONV, 0), _tile(CONV_ROWS, D_CONV, 1),
                  _prev(CONV_HALO, D_CONV, CONV_ROWS, 0), _prev(CONV_HALO, D_CONV, CONV_ROWS, 1),
                  _tile(CONV_ROWS, D_CONV), _next(CONV_HALO, D_CONV, CONV_ROWS, n_halo),
                  _tile(CONV_ROWS, D_CONV), _next(CONV_HALO, D_CONV, CONV_ROWS, n_halo),
                  _full((CONV_K, D_CONV)), _full((1, D_CONV)), _full((1, D_CONV))],
        out_specs=[_tile(CONV_ROWS, D_CONV), _tile(CONV_ROWS, D_CONV), _full((CONV_HALO, D_CONV)),
                   _full((8, D_CONV))],
        out_shape=[jax.ShapeDtypeStruct((seq, D_CONV), BF16), jax.ShapeDtypeStruct((seq, D_CONV), BF16),
                   jax.ShapeDtypeStruct((CONV_HALO, D_CONV), F32), jax.ShapeDtypeStruct((8, D_CONV), F32)],
        scratch_shapes=[pltpu.VMEM((ext, D_CONV), F32), pltpu.VMEM((ext, D_CONV), F32)],
        args=[zr, zr, zr, zr, u1, u1, du3, du3, w_dw, g_ln, b_ln], comm=comm)


MERGE_ROWS = 256


def _merge_fwd(ao, u3, zr, w_ao, w_co, b_co):
    seq = ao.shape[0]

    def body(ao_ref, u3_ref, ga_ref, gb_ref, wa_ref, wc_ref, bc_ref, y_ref, a_ref, cb_ref):
        a = jnp.dot(ao_ref[...], wa_ref[...], preferred_element_type=F32)
        cb = jnp.dot(u3_ref[...], wc_ref[...], preferred_element_type=F32) + bc_ref[...]
        a_ref[...] = a
        cb_ref[...] = cb
        y_ref[...] = (_sig(ga_ref[...]) * a + _sig(gb_ref[...]) * cb).astype(BF16)

    f32_out = jax.ShapeDtypeStruct((seq, D_MODEL), F32)
    return pl.pallas_call(
        body, name="merge_fwd",
        out_shape=[jax.ShapeDtypeStruct((seq, D_MODEL), BF16), f32_out, f32_out],
        grid=(seq // MERGE_ROWS,),
        in_specs=[_tile(MERGE_ROWS, D_ATTN), _tile(MERGE_ROWS, D_CONV), _tile(MERGE_ROWS, D_MODEL, 1),
                  _tile(MERGE_ROWS, D_MODEL, 2), _full(w_ao.shape), _full(w_co.shape), _full((1, D_MODEL))],
        out_specs=[_tile(MERGE_ROWS, D_MODEL)] * 3, compiler_params=_cparams(1),
    )(ao, u3, zr, zr, w_ao, w_co, b_co)


def _merge_bwd(a, cb, zr, rows):
    seq = a.shape[0]

    def fn(dy_v, first, ins, outs):
        a_ref, cb_ref, ga_ref, gb_ref = ins
        da_ref, dcb_ref, dga_ref, dgb_ref, small_ref = outs

        @pl.when(first)
        def _():
            small_ref[...] = jnp.zeros_like(small_ref)

        sa, sb = _sig(ga_ref[...]), _sig(gb_ref[...])
        dcb = dy_v * sb
        dga = dy_v * a_ref[...] * sa * (1.0 - sa)
        dgb = dy_v * cb_ref[...] * sb * (1.0 - sb)
        da_ref[...] = (dy_v * sa).astype(BF16)
        dcb_ref[...] = dcb.astype(BF16)
        dga_ref[...] = dga.astype(BF16)
        dgb_ref[...] = dgb.astype(BF16)
        small_ref[0:1, :] += _colsum(dga)
        small_ref[1:2, :] += _colsum(dgb)
        small_ref[2:3, :] += _colsum(dcb)

    bf = jax.ShapeDtypeStruct((seq, D_MODEL), BF16)
    gate = lambda col: pl.BlockSpec((rows, D_MODEL), lambda i, j: (i, col))
    return _Epilogue(
        [a, cb, zr, zr], [_row_tile(rows, D_MODEL), _row_tile(rows, D_MODEL), gate(1), gate(2)],
        [bf, bf, bf, bf, jax.ShapeDtypeStruct((8, D_MODEL), F32)],
        [_row_tile(rows, D_MODEL)] * 4 + [_whole((8, D_MODEL))], fn, keep_product=False)


FFN_ROWS = 1024
FFN_BLOCKS = D_FF // FFN_COLS
GELU_C = math.sqrt(2.0 / math.pi)


def _gelu(v):
    t = jnp.tanh(GELU_C * (v + 0.044715 * (v * v * v)))
    return 0.5 * v * (1.0 + t), t


def _gelu_grad(v, t):
    return 0.5 * (1.0 + t) + 0.5 * v * (1.0 - t * t) * (GELU_C * (1.0 + 3.0 * 0.044715 * (v * v)))


def _sublane_rows(ref, n):
    return [jnp.broadcast_to(ref[r:r + 1, :], (8, FFN_COLS)) for r in range(n)]


def _rolls(tile, shifts):
    return tuple(pltpu.roll(tile, s, 0) for s in shifts)


def _behind(prev_rolls, cur, row_id):
    rolls = _rolls(cur, (1, 2))
    x1 = jnp.where(row_id < 1, prev_rolls[0], rolls[0])
    x2 = jnp.where(row_id < 2, prev_rolls[1], rolls[1])
    return (x2, x1, cur), rolls


def _ahead(cur_rolls, next_rolls, row_id):
    return (jnp.where(row_id < 7, cur_rolls[0], next_rolls[0]), jnp.where(row_id < 6, cur_rolls[1], next_rolls[1]))


def _conv3(taps, w, bias):
    return w[0] * taps[0] + w[1] * taps[1] + w[2] * taps[2] + bias


def _ffn_specs(rows):
    tile = lambda off: pl.BlockSpec((rows, FFN_COLS), lambda j, i: (i, j + off))
    prev = lambda off: pl.BlockSpec((FFN_HALO, FFN_COLS),
                                    lambda j, i: (jnp.maximum(i * (rows // FFN_HALO) - 1, 0), j + off))
    wgt = lambda off: pl.BlockSpec((3, FFN_COLS), lambda j, i: (0, j + off))
    vec = lambda off: pl.BlockSpec((1, FFN_COLS), lambda j, i: (0, j + off))
    return tile, prev, wgt, vec


def _ffn_act(up, w_dw, b_dw):
    seq = up.shape[0]
    tile, prev, wgt, vec = _ffn_specs(FFN_ROWS)

    def body(v_ref, g_ref, vp_ref, gp_ref, wv_ref, wg_ref, bv_ref, bg_ref, act_ref):
        first = pl.program_id(1) == 0
        row_id = lax.broadcasted_iota(jnp.int32, (8, FFN_COLS), 0)
        wv, wg = _sublane_rows(wv_ref, 3), _sublane_rows(wg_ref, 3)
        (bv,), (bg,) = _sublane_rows(bv_ref, 1), _sublane_rows(bg_ref, 1)
        rolls_v = _rolls(jnp.where(first, 0.0, vp_ref[...]), (1, 2))
        rolls_g = _rolls(jnp.where(first, 0.0, gp_ref[...]), (1, 2))
        for row in range(0, FFN_ROWS, 16):
            halves = []
            for r in (row, row + 8):
                taps_v, rolls_v = _behind(rolls_v, v_ref[r:r + 8, :], row_id)
                taps_g, rolls_g = _behind(rolls_g, g_ref[r:r + 8, :], row_id)
                halves.append(_gelu(_conv3(taps_g, wg, bg))[0] * _conv3(taps_v, wv, bv))
            act_ref[row:row + 16, :] = jnp.concatenate(halves, axis=0).astype(BF16)

    return pl.pallas_call(
        body, name="ffn_act", out_shape=jax.ShapeDtypeStruct((seq, D_FF), BF16),
        grid=(FFN_BLOCKS, seq // FFN_ROWS),
        in_specs=[tile(0), tile(FFN_BLOCKS), prev(0), prev(FFN_BLOCKS), wgt(0), wgt(FFN_BLOCKS),
                  vec(0), vec(FFN_BLOCKS)],
        out_specs=tile(0), compiler_params=_cparams(2),
    )(up, up, up, up, w_dw, w_dw, b_dw, b_dw)


def _ffn_act_bwd(up, dact, w_dw, b_dw, comm=None):
    seq = up.shape[0]
    n_tiles = seq // FFN_ROWS
    n_halo = seq // FFN_HALO
    tile, prev, wgt, vec = _ffn_specs(FFN_ROWS)
    nxt = lambda off: pl.BlockSpec(
        (FFN_HALO, FFN_COLS), lambda j, i: (jnp.minimum((i + 1) * (FFN_ROWS // FFN_HALO), n_halo - 1), j + off))
    acc = lambda off: pl.BlockSpec((8, FFN_COLS), lambda j, i: (0, j + off))

    def body(v_ref, g_ref, vp_ref, gp_ref, vn_ref, gn_ref, da_ref, dan_ref, wv_ref, wg_ref, bv_ref, bg_ref,
             dv_out, dg_out, dwv_ref, dwg_ref, dbv_ref, dbg_ref):
        i = pl.program_id(1)
        first, last = i == 0, i == n_tiles - 1

        @pl.when(first)
        def _():
            for r in (dwv_ref, dwg_ref, dbv_ref, dbg_ref):
                r[...] = jnp.zeros_like(r)

        row_id = lax.broadcasted_iota(jnp.int32, (8, FFN_COLS), 0)
        wv, wg = _sublane_rows(wv_ref, 3), _sublane_rows(wg_ref, 3)
        (bv,), (bg,) = _sublane_rows(bv_ref, 1), _sublane_rows(bg_ref, 1)
        zero = jnp.zeros((8, FFN_COLS), F32)
        sums_v, sums_g = [zero] * 4, [zero] * 4
        rolls_v = _rolls(jnp.where(first, 0.0, vp_ref[...]), (1, 2))
        rolls_g = _rolls(jnp.where(first, 0.0, gp_ref[...]), (1, 2))
        behind = None
        done_v, done_g = [], []

        def grads(v_tile, g_tile, dact, rolls_v, rolls_g):
            taps_v, rolls_v = _behind(rolls_v, v_tile, row_id)
            taps_g, rolls_g = _behind(rolls_g, g_tile, row_id)
            val, gate = _conv3(taps_v, wv, bv), _conv3(taps_g, wg, bg)
            gel, t = _gelu(gate)
            return dact * gel, dact * val * _gelu_grad(gate, t), taps_v, taps_g, rolls_v, rolls_g

        def finish(tile, nxt, row):
            for (d, d_rolls), (_, n_rolls), w, done, o_ref in ((tile[0], nxt[0], wv, done_v, dv_out),
                                                               (tile[1], nxt[1], wg, done_g, dg_out)):
                d1, d2 = _ahead(d_rolls, n_rolls, row_id)
                done.append(w[2] * d + w[1] * d1 + w[0] * d2)
                if len(done) == 2:
                    o_ref[row - 16:row, :] = jnp.concatenate(done, axis=0).astype(BF16)
                    done.clear()

        for row in range(0, FFN_ROWS, 16):
            dact16 = da_ref[row:row + 16, :].astype(F32)
            for r, dact in ((row, dact16[0:8, :]), (row + 8, dact16[8:16, :])):
                dval, dgate, taps_v, taps_g, rolls_v, rolls_g = grads(v_ref[r:r + 8, :], g_ref[r:r + 8, :], dact,
                                                                      rolls_v, rolls_g)
                sums_v = [s + dval * x for s, x in zip(sums_v, taps_v)] + [sums_v[3] + dval]
                sums_g = [s + dgate * x for s, x in zip(sums_g, taps_g)] + [sums_g[3] + dgate]
                tile = ((dval, _rolls(dval, (7, 6))), (dgate, _rolls(dgate, (7, 6))))
                if behind is not None:
                    finish(behind, tile, r)
                behind = tile
        dact_next = jnp.where(last, 0.0, dan_ref[...].astype(F32)[0:FFN_HALO, :])
        dval, dgate, *_ = grads(vn_ref[...], gn_ref[...], dact_next, rolls_v, rolls_g)
        finish(behind, ((dval, _rolls(dval, (7, 6))), (dgate, _rolls(dgate, (7, 6)))), FFN_ROWS)
        for sums, dw_ref, db_ref in ((sums_v, dwv_ref, dbv_ref), (sums_g, dwg_ref, dbg_ref)):
            for tap in range(3):
                dw_ref[tap:tap + 1, :] += _colsum(sums[tap])
            db_ref[0:1, :] += _colsum(sums[3])

    half = jax.ShapeDtypeStruct((seq, D_FF), BF16)
    acc_shape = jax.ShapeDtypeStruct((8, D_FF), F32)
    return _host_call(
        lambda ins, outs, scratch: body(*ins, *outs, *scratch), "ffn_act_bwd", grid=(FFN_BLOCKS, n_tiles),
        in_specs=[tile(0), tile(FFN_BLOCKS), prev(0), prev(FFN_BLOCKS), nxt(0), nxt(FFN_BLOCKS),
                  tile(0), pl.BlockSpec((16, FFN_COLS), lambda j, i: (
                      jnp.minimum((i + 1) * (FFN_ROWS // 16), seq // 16 - 1), j)),
                  wgt(0), wgt(FFN_BLOCKS), vec(0), vec(FFN_BLOCKS)],
        out_specs=[tile(0), tile(0), acc(0), acc(0), acc(0), acc(0)],
        out_shape=[half, half, acc_shape, acc_shape, acc_shape, acc_shape],
        scratch_shapes=[], args=[up, up, up, up, up, up, dact, dact, w_dw, w_dw, b_dw, b_dw], comm=comm)


def _cols_to_blocks(full_cols):
    k, n8 = full_cols.shape
    return jnp.transpose(full_cols.reshape(k, N_DEV, n8 // N_DEV), (1, 0, 2))


def _rows_to_blocks(full_rows):
    r8, n = full_rows.shape
    return full_rows.reshape(N_DEV, r8 // N_DEV, n)


def _blocks_to_cols(gathered):
    _, k, n = gathered.shape
    return jnp.transpose(gathered, (1, 0, 2)).reshape(k, N_DEV * n)


def kernel(x, c, w_ada, b_ada, g_pre_mix, g_post_mix, w_in, b_in, rel_bias, w_attn_o, w_dw_conv, b_dw_conv, g_conv_ln, b_conv_ln, w_conv_o, b_conv_o, w_mix_o, g_pre_ffn, g_post_ffn, w_up, w_dw_ffn, b_dw_ffn, w_down, loss_target, m_w_ada, m_b_ada, m_g_pre_mix, m_g_post_mix, m_w_in, m_b_in, m_rel_bias, m_w_attn_o, m_w_dw_conv, m_b_dw_conv, m_g_conv_ln, m_b_conv_ln, m_w_conv_o, m_b_conv_o, m_w_mix_o, m_g_pre_ffn, m_g_post_ffn, m_w_up, m_w_dw_ffn, m_b_dw_ffn, m_w_down, v_w_ada, v_b_ada, v_g_pre_mix, v_g_post_mix, v_w_in, v_b_in, v_rel_bias, v_w_attn_o, v_w_dw_conv, v_b_dw_conv, v_g_conv_ln, v_b_conv_ln, v_w_conv_o, v_b_conv_o, v_w_mix_o, v_g_pre_ffn, v_g_post_ffn, v_w_up, v_w_dw_ffn, v_b_dw_ffn, v_w_down):
    names = ["w_ada", "b_ada", "g_pre_mix", "g_post_mix", "w_in", "b_in", "rel_bias", "w_attn_o", "w_dw_conv",
             "b_dw_conv", "g_conv_ln", "b_conv_ln", "w_conv_o", "b_conv_o", "w_mix_o", "g_pre_ffn", "g_post_ffn",
             "w_up", "w_dw_ffn", "b_dw_ffn", "w_down"]
    weights = dict(zip(names, [w_ada, b_ada, g_pre_mix, g_post_mix, w_in, b_in, rel_bias, w_attn_o, w_dw_conv,
                               b_dw_conv, g_conv_ln, b_conv_ln, w_conv_o, b_conv_o, w_mix_o, g_pre_ffn,
                               g_post_ffn, w_up, w_dw_ffn, b_dw_ffn, w_down]))
    mom_m = dict(zip(names, [m_w_ada, m_b_ada, m_g_pre_mix, m_g_post_mix, m_w_in, m_b_in, m_rel_bias, m_w_attn_o,
                             m_w_dw_conv, m_b_dw_conv, m_g_conv_ln, m_b_conv_ln, m_w_conv_o, m_b_conv_o,
                             m_w_mix_o, m_g_pre_ffn, m_g_post_ffn, m_w_up, m_w_dw_ffn, m_b_dw_ffn, m_w_down]))
    mom_v = dict(zip(names, [v_w_ada, v_b_ada, v_g_pre_mix, v_g_post_mix, v_w_in, v_b_in, v_rel_bias, v_w_attn_o,
                             v_w_dw_conv, v_b_dw_conv, v_g_conv_ln, v_b_conv_ln, v_w_conv_o, v_b_conv_o,
                             v_w_mix_o, v_g_pre_ffn, v_g_post_ffn, v_w_up, v_w_dw_ffn, v_b_dw_ffn, v_w_down]))
    shapes = {n: w.shape for n, w in weights.items()}

    seq = x.shape[1]
    me = 4 * lax.axis_index("x") + 2 * lax.axis_index("y") + lax.axis_index("c")
    x2 = x.reshape(seq, D_MODEL)
    target = loss_target.reshape(seq, D_MODEL)
    sq = lambda a: a.reshape(a.shape[1:])
    bf = lambda a: sq(a).astype(BF16)

    c_act = _silu_vec(c)
    c_all, g_in, g_dwc, g_dwf = _run_comm(
        _gather_comm([c_act, bf(w_in), sq(w_dw_conv), sq(w_dw_ffn)]), "gather_first")
    c_all = c_all.reshape(N_DEV, D_MODEL)
    wf_in = _blocks_to_cols(g_in)
    wf_dwc = _blocks_to_cols(g_dwc)
    wf_dwf = _blocks_to_cols(g_dwf)

    (mod_all,) = _run_comm(_gather_comm([_ada_fwd(c_all, sq(w_ada))]), "gather_mod")
    mod = lax.dynamic_index_in_dim(mod_all, me, axis=1, keepdims=False)
    mod6 = (mod.reshape(1, 6 * D_MODEL) + b_ada).reshape(6, D_MODEL)

    h1 = _pre_mix(x2, mod6, g_pre_mix)
    qkv = _mm(h1, wf_in, "nn", BF16, "in_proj_qkv", bias=b_in, tm=1024, tn=768, cols=(0, 3 * D_ATTN))
    zr, _, (g_ao, g_co, g_mo) = _mm(h1, wf_in, "nn", F32, "in_proj_rest", bias=b_in, tm=1024, tn=3 * D_ATTN,
                                 cols=(3 * D_ATTN, 2 * D_CONV + 2 * D_MODEL),
                                 comm=_gather_comm([bf(w_attn_o), bf(w_conv_o), bf(w_mix_o)]))
    kpad = jnp.pad(qkv[:, D_ATTN:2 * D_ATTN], ((PAD_ROWS, 0), (0, 0)))
    vpad = jnp.pad(qkv[:, 2 * D_ATTN:], ((PAD_ROWS, 0), (0, 0)))
    table = jnp.transpose(_bias_table(sq(rel_bias)), (1, 0, 2))
    ao, (g_up,) = _attn_fwd(qkv, kpad, vpad, table, comm=_gather_comm([bf(w_up)]))
    (u1, u3), (g_dn,) = _conv_fwd(zr, wf_dwc, b_dw_conv, g_conv_ln, b_conv_ln, comm=_gather_comm([bf(w_down)]))
    wf_ao = _blocks_to_cols(g_ao)
    wf_co = _blocks_to_cols(g_co)
    wf_mo = g_mo.reshape(D_MODEL, D_MODEL)
    wf_up = _blocks_to_cols(g_up)
    wf_dn = g_dn.reshape(D_FF, D_MODEL)
    y, a_br, cb_br = _merge_fwd(ao, u3, zr, wf_ao, wf_co, b_conv_o)
    ymix, (x1, h2), _ = _mm(y, wf_mo, "nn", F32, "mix_o", tm=512, tn=D_MODEL,
                            epilogue=_post_mix_pre_ffn(x2, mod6, g_post_mix, g_pre_ffn, 512))
    up = _mm(h2, wf_up, "nn", F32, "ffn_up", tm=1024, tn=1408)
    act = _ffn_act(up, wf_dwf, b_dw_ffn)
    _, (loss_lanes, dout, dyf, small_f), _ = _mm(act, wf_dn, "nn", F32, "ffn_down", tm=512, tn=D_MODEL,
                                                 epilogue=_final(x1, target, mod6, g_post_ffn, 512))

    dact = _mm(dyf, wf_dn, "nt", BF16, "ffn_down_dx", tm=1024, tn=1408)
    gw_down = _mm(act, dyf, "tn", BF16, "ffn_down_dw", tm=256, tn=1024)
    (dup_v, dup_g, dwv, dwg, dbv, dbg), (parts_down,) = _ffn_act_bwd(
        up, dact, wf_dwf, b_dw_ffn, comm=_scatter_comm([_rows_to_blocks(gw_down)]))
    _, (dx1, dymix, small_m), _ = _mm([dup_v, dup_g], wf_up, "nt", F32, "ffn_up_dx", tm=256, tn=D_MODEL,
                                      epilogue=_mid_bwd(x1, dout, ymix, mod6, g_pre_ffn, g_post_mix, 256))
    blocks_up = _mm_tn_blocks(h2, [dup_v, dup_g], 2 * D_FF // N_DEV, "ffn_up_dw", k_steps=2)
    _, (da, dcb, dga, dgb, small_g), _ = _mm(dymix, wf_mo, "nt", F32, "mix_o_dx", tm=512, tn=D_MODEL,
                                             epilogue=_merge_bwd(a_br, cb_br, zr, 512))
    gw_mo = _mm(y, dymix, "tn", BF16, "mix_o_dw")
    dao = _mm(da, wf_ao, "nt", BF16, "attn_o_dx", tm=1024)
    gw_ao = _mm(ao, da, "tn", BF16, "attn_o_dw")
    du3 = _mm(dcb, wf_co, "nt", F32, "conv_o_dx", tm=1024)
    gw_co = _mm(u3, dcb, "tn", BF16, "conv_o_dw")
    (dq, dkt, dvt, dbias, small_a), (parts_up,) = _attn_bwd(
        qkv, kpad, vpad, table, dao, comm=_scatter_comm([blocks_up]))
    g_rel = _bias_grad(jnp.transpose(dbias, (1, 0, 2)))
    (dglu_a, dglu_b, dw_conv, small_c), (parts_mo, parts_ao, parts_co) = _conv_bwd(
        zr, u1, du3, wf_dwc, g_conv_ln, b_conv_ln,
        comm=_scatter_comm([_rows_to_blocks(gw_mo), _cols_to_blocks(gw_ao), _cols_to_blocks(gw_co)]))
    dz = _assemble_dz(dq, dkt, dvt, dglu_a, dglu_b, dga, dgb)
    blocks_in = _mm_tn_blocks(h1, [dz], dz.shape[1] // N_DEV, "in_proj_dw")
    _, (grad_x, small_x), (parts_in,) = _mm(dz, wf_in, "nt", F32, "in_proj_dx", tm=512, tn=D_MODEL,
                                            comm=_scatter_comm([blocks_in]),
                                            epilogue=_pre_mix_bwd(x2, dx1, mod6, g_pre_mix, 512))

    packed = _pack_grads(small_x, small_m, small_f, small_g, small_a, small_c, dbv, dbg, dwv, dwg, dw_conv)
    gathered, gathered_rel, gathered_loss = _run_comm(_gather_comm([packed, g_rel, loss_lanes]), "gather_small")
    gathered = gathered.reshape(N_DEV, PACKED_TOTAL)
    updates, g_dwc_full, g_dwf_full, loss_all = _small_adamw(gathered, gathered_rel, gathered_loss, weights, mom_m,
                                                             mom_v)
    loss = loss_all[0, 0]

    grads, deltas, new_m, new_v = {}, {}, {}, {}

    def record(name, update):
        for dst, val in zip((grads, deltas, new_m, new_v), update):
            dst[name] = val.reshape(shapes[name])

    for name, update in updates.items():
        record(name, update)

    def local_update(name, grad):
        record(name, _adamw(sq(weights[name]), sq(mom_m[name]), sq(mom_v[name]), "adamw_" + name, g=grad))

    conv_cols, ffn_cols, ada_cols = D_CONV // N_DEV, 2 * D_FF // N_DEV, 6 * D_MODEL // N_DEV
    local_update("w_dw_conv", lax.dynamic_slice(g_dwc_full, (0, me * conv_cols), (CONV_K, conv_cols)))
    local_update("w_dw_ffn", lax.dynamic_slice(g_dwf_full, (0, me * ffn_cols), (3, ffn_cols)))
    local_update("w_ada", _ada_grad(c_all, lax.dynamic_slice(gathered, (0, me * ada_cols), (N_DEV, ada_cols))))

    for name, part in (("w_in", parts_in), ("w_attn_o", parts_ao), ("w_conv_o", parts_co), ("w_mix_o", parts_mo),
                       ("w_up", parts_up), ("w_down", parts_down)):
        record(name, _adamw(sq(weights[name]), sq(mom_m[name]), sq(mom_v[name]), "adamw_" + name, parts=part))

    return (loss, grad_x.reshape(x.shape), *[grads[n] for n in names], *[deltas[n] for n in names],
            *[new_m[n] for n in names], *[new_v[n] for n in names])
```

```python
import functools
import math

import jax
import jax.numpy as jnp
from jax import lax
from jax.experimental import pallas as pl
from jax.experimental.pallas import tpu as pltpu

F32 = jnp.float32
BF16 = jnp.bfloat16
HIGHEST = lax.Precision.HIGHEST

D_MODEL = 1024
CHUNK = 64
LEFT_CHUNKS = 8
BAND = (LEFT_CHUNKS + 1) * CHUNK
PAD_ROWS = LEFT_CHUNKS * CHUNK
GROUP = 4
GROUP_Q = GROUP * CHUNK
GROUP_K = GROUP_Q + PAD_ROWS
SOFTMAX_ROWS = 16
TOEPLITZ = 640
N_HEADS = 8
HEAD_DIM = 64
D_ATTN = 512
D_CONV = 512
CONV_K = 31
CONV_HALO = 32
MAX_REL = 128
N_REL = 2 * MAX_REL + 1
D_FF = 2816
FFN_HALO = 8
FFN_COLS = 256
EPS = 1e-6
NEG_INF = -1e30
N_DEV = 8

ADAM_LR = 0.001
ADAM_B1 = 0.9
ADAM_B2 = 0.999
ADAM_EPS = 1e-08
ADAM_WD = 0.01
ADAM_STEP = 10

VMEM_LIMIT_BYTES = 56 * 1024 * 1024
ADAMW_BLOCK_BYTES = 768 * 1024

MESH = pl.DeviceIdType.MESH
ANY = pl.BlockSpec(memory_space=pl.ANY)

SH_M, SC_M, GT_M, SH_F, SC_F, GT_F = range(6)

SMALL = (("b_ada", 6144), ("g_pre_mix", 1024), ("g_post_mix", 1024), ("b_in", 4608), ("b_dw_conv", 512),
         ("g_conv_ln", 512), ("b_conv_ln", 512), ("b_conv_o", 1024), ("g_pre_ffn", 1024), ("g_post_ffn", 1024),
         ("b_dw_ffn", 5632))
PACKED_TOTAL = sum(n for _, n in SMALL) + CONV_K * D_CONV + 3 * 2 * D_FF


def _cparams(n_axes):
    return pltpu.CompilerParams(vmem_limit_bytes=VMEM_LIMIT_BYTES,
                                dimension_semantics=("arbitrary",) * n_axes)


def _sig(v):
    return 1.0 / (1.0 + jnp.exp(-v))


def _pick(n, target):
    if n <= target:
        return n
    t = target - target % 128
    while n % t:
        t -= 128
    return t


def _tile(rows, cols, col=0):
    return pl.BlockSpec((rows, cols), lambda i: (i, col))


def _full(shape):
    zeros = (0,) * len(shape)
    return pl.BlockSpec(shape, lambda i: zeros)


def _prev(halo, cols, rows, col=0):
    return pl.BlockSpec((halo, cols), lambda i: (jnp.maximum(i * (rows // halo) - 1, 0), col))


def _next(halo, cols, rows, n_blocks, col=0):
    return pl.BlockSpec((halo, cols), lambda i: (jnp.minimum((i + 1) * (rows // halo), n_blocks - 1), col))


class _Comm:
    def __init__(self, inputs, out_shapes, sems, start, finish, relay=None):
        self.inputs, self.out_shapes, self.sems, self.start, self.finish = inputs, out_shapes, sems, start, finish
        self.relay = relay


def _host_call(body, name, grid, in_specs, out_specs, out_shape, scratch_shapes, args, comm=None):
    n_in, n_out, n_scr = len(args), len(out_shape), len(scratch_shapes)
    c_in = list(comm.inputs) if comm else []
    c_out = list(comm.out_shapes) if comm else []
    c_sem = list(comm.sems) if comm else []

    def full(*refs):
        bounds = [0, n_in, len(c_in), n_out, len(c_out), n_scr, len(c_sem)]
        cuts = [sum(bounds[:i + 1]) for i in range(len(bounds))]
        ins, cins, outs, couts, scr, csems = (refs[lo:hi] for lo, hi in zip(cuts[:-1], cuts[1:]))
        if comm:
            first = functools.reduce(jnp.logical_and, [pl.program_id(ax) == 0 for ax in range(len(grid))])
            pl.when(first)(lambda: comm.start(cins, couts, csems))
            last = functools.reduce(jnp.logical_and, [pl.program_id(ax) == grid[ax] - 1 for ax in range(len(grid))])
            if comm.relay is not None:
                pl.when(last)(lambda: comm.relay(cins, couts, csems))
        body(ins, outs, scr)
        if comm:
            pl.when(last)(lambda: comm.finish(cins, couts, csems))

    res = pl.pallas_call(
        full, name=name, grid=grid, in_specs=list(in_specs) + [ANY] * len(c_in),
        out_specs=list(out_specs) + [ANY] * len(c_out), out_shape=list(out_shape) + c_out,
        scratch_shapes=list(scratch_shapes) + c_sem, compiler_params=_cparams(len(grid)),
    )(*args, *c_in)
    return list(res[:n_out]), list(res[n_out:])


def _run_comm(comm, name):
    n_in, n_out = len(comm.inputs), len(comm.out_shapes)

    def body(*refs):
        ins, outs, sems = refs[:n_in], refs[n_in:n_in + n_out], refs[n_in + n_out:]
        comm.start(ins, outs, sems)
        if comm.relay is not None:
            comm.relay(ins, outs, sems)
        comm.finish(ins, outs, sems)

    return pl.pallas_call(
        body, name=name, out_shape=list(comm.out_shapes), in_specs=[ANY] * n_in, out_specs=[ANY] * n_out,
        scratch_shapes=list(comm.sems),
    )(*comm.inputs)


def _place():
    return lax.axis_index("x"), lax.axis_index("y"), lax.axis_index("c")


def _gather_comm(arrs):
    n = len(arrs)

    def plan(ins, outs, sems):
        send_sems, recv_sems, local_sems = sems
        x, y, c = _place()
        me, sibling = (x, y, c), (x, y, 1 - c)
        chips = [(1 - x, y), (x, 1 - y), (1 - x, 1 - y)]

        def block(k, p):
            return outs[k].at[4 * p[0] + 2 * p[1] + p[2]]

        def copy(k, s, blk, to, src=None):
            return pltpu.make_async_remote_copy(
                src_ref=block(k, blk) if src is None else src, dst_ref=block(k, blk),
                send_sem=send_sems.at[7 * k + s], recv_sem=recv_sems.at[7 * k + s],
                device_id=to, device_id_type=MESH)

        mine = [pltpu.make_async_copy(ins[k], block(k, me), local_sems.at[k]) for k in range(n)]
        first = []
        for k in range(n):
            first.append(copy(k, 0, me, sibling, src=ins[k]))
            for j, chip in enumerate(chips):
                first.append(copy(k, 1 + j, me, (*chip, c), src=ins[k]))
        return me, sibling, chips, c, copy, mine, first

    def start(ins, outs, sems):
        *_, mine, first = plan(ins, outs, sems)
        for cp in mine + first:
            cp.start()

    def relay(ins, outs, sems):
        me, sibling, chips, c, copy, _, _ = plan(ins, outs, sems)
        for j, chip in enumerate(chips):
            for k in range(n):
                copy(k, 1 + j, (*chip, c), me).wait_recv()
                copy(k, 4 + j, (*chip, c), sibling).start()

    def finish(ins, outs, sems):
        me, sibling, chips, c, copy, mine, first = plan(ins, outs, sems)
        passed = [copy(k, 4 + j, (*chip, c), sibling) for j, chip in enumerate(chips) for k in range(n)]
        for k in range(n):
            copy(k, 0, sibling, me).wait_recv()
        for j, chip in enumerate(chips):
            for k in range(n):
                copy(k, 4 + j, (*chip, 1 - c), me).wait_recv()
        for cp in first + passed:
            cp.wait_send()
        for cp in mine:
            cp.wait()

    return _Comm(list(arrs), [jax.ShapeDtypeStruct((N_DEV,) + a.shape, a.dtype) for a in arrs],
                 [pltpu.SemaphoreType.DMA((7 * n,)), pltpu.SemaphoreType.DMA((7 * n,)),
                  pltpu.SemaphoreType.DMA((n,))], start, finish, relay)


def _scatter_comm(blocks):
    n = len(blocks)

    def plan(ins, outs, sems, arrivals):
        send_sems, recv_sems, local_sems = sems
        x, y, c = _place()
        me = 4 * x + 2 * y + c
        local = [pltpu.make_async_copy(ins[k].at[me], outs[k].at[me], local_sems.at[k]) for k in range(n)]
        sends, recvs = [], []
        for k in range(n):
            for mask in range(1, N_DEV):
                px = 1 - x if mask & 4 else x
                py = 1 - y if mask & 2 else y
                pc = 1 - c if mask & 1 else c
                peer = 4 * px + 2 * py + pc
                sem = 7 * k + mask - 1
                both = dict(send_sem=send_sems.at[sem], recv_sem=recv_sems.at[sem], device_id=(px, py, pc),
                            device_id_type=MESH)
                sends.append(pltpu.make_async_remote_copy(src_ref=ins[k].at[peer], dst_ref=outs[k].at[me], **both))
                if arrivals:
                    recvs.append(pltpu.make_async_remote_copy(src_ref=ins[k].at[me], dst_ref=outs[k].at[peer],
                                                              **both))
        return local, sends, recvs

    def start(ins, outs, sems):
        local, sends, _ = plan(ins, outs, sems, arrivals=False)
        for cp in local + sends:
            cp.start()

    def finish(ins, outs, sems):
        local, sends, recvs = plan(ins, outs, sems, arrivals=True)
        for cp in recvs:
            cp.wait_recv()
        for cp in sends:
            cp.wait_send()
        for cp in local:
            cp.wait()

    return _Comm(list(blocks), [jax.ShapeDtypeStruct(b.shape, b.dtype) for b in blocks],
                 [pltpu.SemaphoreType.DMA((7 * n,)), pltpu.SemaphoreType.DMA((7 * n,)),
                  pltpu.SemaphoreType.DMA((n,))], start, finish)


_DIMS = {"nn": (((1,), (0,)), ((), ())), "nt": (((1,), (1,)), ((), ())), "tn": (((0,), (0,)), ((), ()))}


class _Epilogue:
    def __init__(self, args, in_specs, out_shapes, out_specs, fn, keep_product):
        self.args, self.in_specs, self.out_shapes, self.out_specs = args, in_specs, out_shapes, out_specs
        self.fn, self.keep_product = fn, keep_product


def _row_tile(rows, cols):
    return pl.BlockSpec((rows, cols), lambda i, j: (i, 0))


def _whole(shape):
    zeros = (0,) * len(shape)
    return pl.BlockSpec(shape, lambda i, j: zeros)


def _mm(a, b, mode, out_dtype, name, bias=None, tm=512, tn=512, comm=None, cols=None, epilogue=None):
    pieces = a if isinstance(a, (list, tuple)) else [a]
    assert all(p.dtype == BF16 for p in pieces) and b.dtype == BF16
    a = pieces[0]
    if mode == "tn":
        k_dim, m_dim = a.shape
    else:
        m_dim, k_dim = a.shape
    n_dim = b.shape[0] if mode == "nt" else b.shape[1]
    col0 = 0
    if cols is not None:
        assert mode == "nn" and cols[0] % tn == 0 and cols[1] % tn == 0
        col0, n_dim = cols[0] // tn, cols[1]
    tm, tn = _pick(m_dim, tm), _pick(n_dim, tn)
    a_specs = [pl.BlockSpec((k_dim, tm), lambda i, j: (0, i)) if mode == "tn"
               else pl.BlockSpec((tm, k_dim), lambda i, j: (i, 0))] * len(pieces)
    if mode == "nt":
        b_specs = [pl.BlockSpec((tn, k_dim), lambda i, j, p=p: (j, p)) for p in range(len(pieces))]
    else:
        assert len(pieces) == 1
        b_specs = [pl.BlockSpec((k_dim, tn), lambda i, j: (0, j + col0))]
    in_specs = a_specs + b_specs
    args = list(pieces) + [b] * len(pieces)
    if bias is not None:
        in_specs.append(pl.BlockSpec((1, tn), lambda i, j: (0, j + col0)))
        args.append(bias)
    dims = _DIMS[mode]
    n_pieces = len(pieces)
    n_own = len(args)
    keep = epilogue is None or epilogue.keep_product
    out_specs = [pl.BlockSpec((tm, tn), lambda i, j: (i, j))] if keep else []
    out_shape = [jax.ShapeDtypeStruct((m_dim, n_dim), out_dtype)] if keep else []
    if epilogue is not None:
        assert tn == n_dim
        in_specs, args = in_specs + list(epilogue.in_specs), args + list(epilogue.args)
        out_specs, out_shape = out_specs + list(epilogue.out_specs), out_shape + list(epilogue.out_shapes)

    def body(ins, outs, scratch):
        total = lax.dot_general(ins[0][...], ins[n_pieces][...], dims, preferred_element_type=F32)
        for p in range(1, n_pieces):
            total = total + lax.dot_general(ins[p][...], ins[n_pieces + p][...], dims, preferred_element_type=F32)
        if bias is not None:
            total = total + ins[2 * n_pieces][...]
        if keep:
            outs[0][...] = total.astype(out_dtype)
        if epilogue is not None:
            epilogue.fn(total, pl.program_id(0) == 0, ins[n_own:], outs[1:] if keep else outs)

    outs, extra = _host_call(body, name, grid=(m_dim // tm, n_dim // tn), in_specs=in_specs, out_specs=out_specs,
                             out_shape=out_shape, scratch_shapes=[], args=args, comm=comm)
    product = outs[0] if keep else None
    if comm is None and epilogue is None:
        return product
    return product, outs[1:] if keep else outs, extra


def _mm_tn_blocks(a, bs, n_dev_cols, name, tm=512, k_steps=1):
    k_dim, m_dim = a.shape
    n = n_dev_cols
    pair = 2 * n
    assert pair % 128 == 0 and all(b.shape[1] % pair == 0 for b in bs) and k_dim % k_steps == 0
    counts = [b.shape[1] // pair for b in bs]
    firsts = [sum(counts[:q]) for q in range(len(bs))]
    assert sum(counts) == N_DEV // 2
    tm, tk = _pick(m_dim, tm), k_dim // k_steps

    def b_spec(first, count):
        return pl.BlockSpec((tk, pair), lambda i, j, k: (k, jnp.clip(j - first, 0, count - 1)))

    def body(ins, outs, scratch):
        a_ref, b_refs, o_ref, s_ref = ins[0], ins[1:], outs[0], scratch[0]
        j, k = pl.program_id(1), pl.program_id(2)
        for b_ref, first, count in zip(b_refs, firsts, counts):
            @pl.when(jnp.logical_and(j >= first, j < first + count))
            def _(b_ref=b_ref):
                part = lax.dot_general(a_ref[...], b_ref[...], _DIMS["tn"], preferred_element_type=F32)
                if k_steps == 1:
                    s_ref[...] = part
                else:
                    @pl.when(k == 0)
                    def _():
                        s_ref[...] = part

                    @pl.when(k > 0)
                    def _():
                        s_ref[...] += part

        @pl.when(k == k_steps - 1)
        def _():
            o_ref[0] = s_ref[:, 0:n].astype(BF16)
            o_ref[1] = s_ref[:, n:pair].astype(BF16)

    (out,), _ = _host_call(
        body, name, grid=(m_dim // tm, N_DEV // 2, k_steps),
        in_specs=[pl.BlockSpec((tk, tm), lambda i, j, k: (k, i))] + [b_spec(f, c) for f, c in zip(firsts, counts)],
        out_specs=[pl.BlockSpec((2, tm, n), lambda i, j, k: (j, i, 0))],
        out_shape=[jax.ShapeDtypeStruct((N_DEV, m_dim, n), BF16)],
        scratch_shapes=[pltpu.VMEM((tm, pair), F32)], args=[a] + list(bs))
    return out


def _adam_math(w, g, m, v):
    m = ADAM_B1 * m + (1.0 - ADAM_B1) * g
    v = ADAM_B2 * v + (1.0 - ADAM_B2) * (g * g)
    m_hat = m / (1.0 - ADAM_B1 ** ADAM_STEP)
    v_hat = v / (1.0 - ADAM_B2 ** ADAM_STEP)
    delta = -ADAM_LR * (m_hat / (jnp.sqrt(v_hat) + ADAM_EPS) + ADAM_WD * w)
    return delta, m, v


def _adamw(w, m, v, name, g=None, parts=None):
    rows, cols = w.shape
    tr = rows
    if rows * cols * 4 > ADAMW_BLOCK_BYTES:
        tr = max(t for t in range(16, rows, 16) if rows % t == 0 and t * cols * 4 <= ADAMW_BLOCK_BYTES)

    def body(w_ref, m_ref, v_ref, g_ref, go_ref, d_ref, mo_ref, vo_ref):
        if parts is None:
            grad = g_ref[...]
        else:
            grad = g_ref[0].astype(F32)
            for d in range(1, N_DEV):
                grad = grad + g_ref[d].astype(F32)
        delta, m_new, v_new = _adam_math(w_ref[...], grad, m_ref[...], v_ref[...])
        go_ref[...] = grad
        d_ref[...] = delta
        mo_ref[...] = m_new
        vo_ref[...] = v_new

    spec = _tile(tr, cols)
    g_spec = spec if parts is None else pl.BlockSpec((N_DEV, tr, cols), lambda i: (0, i, 0))
    shape = jax.ShapeDtypeStruct((rows, cols), F32)
    return pl.pallas_call(
        body, name=name, out_shape=[shape] * 4, grid=(rows // tr,),
        in_specs=[spec, spec, spec, g_spec], out_specs=[spec] * 4, compiler_params=_cparams(1),
    )(w, m, v, g if parts is None else parts)


def _pack_grads(small_x, small_m, small_f, small_g, small_a, small_c, dbv, dbg, dwv, dwg, dw_conv):
    pieces = [
        (small_x, 2, D_MODEL), (small_x, 1, D_MODEL), (small_m, 4, D_MODEL), (small_m, 2, D_MODEL),
        (small_m, 1, D_MODEL), (small_f, 1, D_MODEL),
        (small_x, 0, D_MODEL), (small_m, 3, D_MODEL),
        (small_a, 0, D_ATTN), (small_a, 1, D_ATTN), (small_a, 2, D_ATTN), (small_c, 3, D_CONV),
        (small_c, 4, D_CONV), (small_g, 0, D_MODEL), (small_g, 1, D_MODEL),
        (small_c, 0, D_CONV), (small_c, 1, D_CONV), (small_c, 2, D_CONV),
        (small_g, 2, D_MODEL), (small_m, 0, D_MODEL), (small_f, 0, D_MODEL),
        (dbv, 0, D_FF), (dbg, 0, D_FF),
    ]
    pieces += [(dw_conv, j, D_CONV) for j in range(CONV_K)]
    pieces += [(src, tap, D_FF) for tap in range(3) for src in (dwv, dwg)]
    sources = [small_x, small_m, small_f, small_g, small_a, small_c, dbv, dbg, dwv, dwg, dw_conv]
    assert sum(width for _, _, width in pieces) == PACKED_TOTAL

    def body(*refs):
        o_ref = refs[-1]
        ref_of = {id(src): ref for src, ref in zip(sources, refs)}
        off = 0
        for src, row, width in pieces:
            o_ref[:, off:off + width] = ref_of[id(src)][row:row + 1, :]
            off += width

    return pl.pallas_call(body, name="pack_grads", out_shape=jax.ShapeDtypeStruct((1, PACKED_TOTAL), F32))(*sources)


def _small_adamw(gathered, gathered_rel, gathered_loss, weights, mom_m, mom_v):
    vec_names = [name for name, _ in SMALL]
    states = []
    for name in vec_names + ["rel_bias"]:
        states += [weights[name], mom_m[name], mom_v[name]]
    states = [a.reshape(a.shape[1:]) if a.ndim == 3 else a for a in states]
    n_state = len(states)

    def body(*refs):
        g_ref, rel_ref, loss_ref = refs[0], refs[1], refs[2]
        state_refs, out_refs = refs[3:3 + n_state], refs[3 + n_state:]
        total = g_ref[0:1, :]
        rel = rel_ref[0]
        loss = loss_ref[0]
        for d in range(1, N_DEV):
            total = total + g_ref[d:d + 1, :]
            rel = rel + rel_ref[d]
            loss = loss + loss_ref[d]
        off = 0
        for n, (name, width) in enumerate(SMALL):
            grad = total[:, off:off + width]
            w_ref, m_ref, v_ref = state_refs[3 * n:3 * n + 3]
            for ref, val in zip(out_refs[4 * n:4 * n + 4], (grad,) + _adam_math(w_ref[...], grad, m_ref[...], v_ref[...])):
                ref[...] = val
            off += width
        n = len(SMALL)
        w_ref, m_ref, v_ref = state_refs[3 * n:3 * n + 3]
        for ref, val in zip(out_refs[4 * n:4 * n + 4], (rel,) + _adam_math(w_ref[...], rel, m_ref[...], v_ref[...])):
            ref[...] = val
        dwc_ref, dwf_ref, loss_out = out_refs[4 * n + 4:]
        loss_out[...] = 0.5 * loss
        dwc_ref[...] = jnp.zeros_like(dwc_ref)
        dwf_ref[...] = jnp.zeros_like(dwf_ref)
        for j in range(CONV_K):
            dwc_ref[j:j + 1, :] = total[:, off:off + D_CONV]
            off += D_CONV
        for tap in range(3):
            dwf_ref[tap:tap + 1, :] = total[:, off:off + 2 * D_FF]
            off += 2 * D_FF

    out_shape = []
    for k in range(n_state // 3):
        out_shape += [jax.ShapeDtypeStruct(states[3 * k].shape, F32)] * 4
    out_shape += [jax.ShapeDtypeStruct((CONV_HALO, D_CONV), F32), jax.ShapeDtypeStruct((8, 2 * D_FF), F32),
                  jax.ShapeDtypeStruct((1, 128), F32)]
    res = pl.pallas_call(
        body, name="small_adamw", out_shape=out_shape,
        compiler_params=pltpu.CompilerParams(vmem_limit_bytes=VMEM_LIMIT_BYTES),
    )(gathered, gathered_rel, gathered_loss, *states)
    updates = {name: tuple(res[4 * n:4 * n + 4]) for n, name in enumerate(vec_names + ["rel_bias"])}
    return updates, res[-3], res[-2], res[-1]


def _silu_vec(c):
    def body(c_ref, o_ref):
        v = c_ref[...]
        o_ref[...] = v * _sig(v)

    return pl.pallas_call(body, name="silu_c", out_shape=jax.ShapeDtypeStruct(c.shape, F32))(c)


def _ada_fwd(c_all, w_shard):
    def body(c_ref, w_ref, o_ref):
        o_ref[...] = jnp.dot(c_ref[...], w_ref[...], precision=HIGHEST, preferred_element_type=F32)

    return pl.pallas_call(
        body, name="ada_fwd", out_shape=jax.ShapeDtypeStruct((N_DEV, w_shard.shape[1]), F32),
        compiler_params=pltpu.CompilerParams(vmem_limit_bytes=VMEM_LIMIT_BYTES),
    )(c_all, w_shard)


def _ada_grad(c_all, dmod_shard):
    def body(c_ref, d_ref, o_ref):
        o_ref[...] = lax.dot_general(c_ref[...], d_ref[...], _DIMS["tn"], precision=HIGHEST,
                                     preferred_element_type=F32)

    return pl.pallas_call(
        body, name="ada_grad", out_shape=jax.ShapeDtypeStruct((D_MODEL, dmod_shard.shape[1]), F32),
        compiler_params=pltpu.CompilerParams(vmem_limit_bytes=VMEM_LIMIT_BYTES),
    )(c_all, dmod_shard)


ROWS = 256


def _rms(v):
    r = lax.rsqrt(jnp.mean(v * v, axis=-1, keepdims=True) + EPS)
    return v * r, r


def _rms_bwd(dxn, xn, r):
    return r * (dxn - xn * jnp.mean(dxn * xn, axis=-1, keepdims=True))


def _colsum(v):
    return jnp.sum(v, axis=0, keepdims=True)


def _pre_mix(x, mod6, g1):
    seq = x.shape[0]

    def body(x_ref, mod_ref, g_ref, h_ref):
        xn, _ = _rms(x_ref[...])
        y = xn * g_ref[...]
        h_ref[...] = (y * (1.0 + mod_ref[SC_M:SC_M + 1, :]) + mod_ref[SH_M:SH_M + 1, :]).astype(BF16)

    return pl.pallas_call(
        body, name="pre_mix", out_shape=jax.ShapeDtypeStruct((seq, D_MODEL), BF16), grid=(seq // ROWS,),
        in_specs=[_tile(ROWS, D_MODEL), _full((6, D_MODEL)), _full((1, D_MODEL))],
        out_specs=_tile(ROWS, D_MODEL), compiler_params=_cparams(1),
    )(x, mod6, g1)


def _post_mix_pre_ffn(x, mod6, g2, g3, rows):
    seq = x.shape[0]

    def fn(y, first, ins, outs):
        x_ref, mod_ref, g2_ref, g3_ref = ins
        x1_ref, h_ref = outs
        yn, _ = _rms(y)
        x1 = x_ref[...] + mod_ref[GT_M:GT_M + 1, :] * (yn * g2_ref[...])
        x1_ref[...] = x1
        xn, _ = _rms(x1)
        y3 = xn * g3_ref[...]
        h_ref[...] = (y3 * (1.0 + mod_ref[SC_F:SC_F + 1, :]) + mod_ref[SH_F:SH_F + 1, :]).astype(BF16)

    return _Epilogue(
        [x, mod6, g2, g3], [_row_tile(rows, D_MODEL), _whole((6, D_MODEL)), _whole((1, D_MODEL)), _whole((1, D_MODEL))],
        [jax.ShapeDtypeStruct((seq, D_MODEL), F32), jax.ShapeDtypeStruct((seq, D_MODEL), BF16)],
        [_row_tile(rows, D_MODEL), _row_tile(rows, D_MODEL)], fn, keep_product=True)


def _final(x1, target, mod6, g4, rows):
    seq = x1.shape[0]

    def fn(y, first, ins, outs):
        x1_ref, t_ref, mod_ref, g_ref = ins
        loss_ref, dout_ref, dyf_ref, small_ref = outs

        @pl.when(first)
        def _():
            loss_ref[...] = jnp.zeros_like(loss_ref)
            small_ref[...] = jnp.zeros_like(small_ref)

        gt = mod_ref[GT_F:GT_F + 1, :]
        g4v = g_ref[...]
        yn, r = _rms(y)
        out = x1_ref[...] + gt * (yn * g4v)
        err = out - t_ref[...]
        loss_ref[...] += jnp.sum(jnp.mean(err * err, axis=-1, keepdims=True))
        dout = err * (1.0 / D_MODEL)
        dout_ref[...] = dout
        small_ref[0:1, :] += _colsum(dout * gt * yn)
        small_ref[1:2, :] += _colsum(dout * (yn * g4v))
        dyf_ref[...] = _rms_bwd(dout * gt * g4v, yn, r).astype(BF16)

    return _Epilogue(
        [x1, target, mod6, g4],
        [_row_tile(rows, D_MODEL), _row_tile(rows, D_MODEL), _whole((6, D_MODEL)), _whole((1, D_MODEL))],
        [jax.ShapeDtypeStruct((1, 128), F32), jax.ShapeDtypeStruct((seq, D_MODEL), F32),
         jax.ShapeDtypeStruct((seq, D_MODEL), BF16), jax.ShapeDtypeStruct((8, D_MODEL), F32)],
        [_whole((1, 128)), _row_tile(rows, D_MODEL), _row_tile(rows, D_MODEL), _whole((8, D_MODEL))],
        fn, keep_product=False)


def _mid_bwd(x1, dout, ymix, mod6, g3, g2, rows):
    seq = x1.shape[0]

    def fn(dh, first, ins, outs):
        x1_ref, dout_ref, y_ref, mod_ref, g3_ref, g2_ref = ins
        dx1_ref, dy_ref, small_ref = outs

        @pl.when(first)
        def _():
            small_ref[...] = jnp.zeros_like(small_ref)

        g3v, g2v = g3_ref[...], g2_ref[...]
        xn, r3 = _rms(x1_ref[...])
        y3 = xn * g3v
        dy3 = dh * (1.0 + mod_ref[SC_F:SC_F + 1, :])
        small_ref[0:1, :] += _colsum(dy3 * xn)
        small_ref[1:2, :] += _colsum(dh * y3)
        small_ref[2:3, :] += _colsum(dh)
        dx1 = dout_ref[...] + _rms_bwd(dy3 * g3v, xn, r3)
        dx1_ref[...] = dx1
        gt = mod_ref[GT_M:GT_M + 1, :]
        yn, r2 = _rms(y_ref[...])
        small_ref[3:4, :] += _colsum(dx1 * gt * yn)
        small_ref[4:5, :] += _colsum(dx1 * (yn * g2v))
        dy_ref[...] = _rms_bwd(dx1 * gt * g2v, yn, r2).astype(BF16)

    return _Epilogue(
        [x1, dout, ymix, mod6, g3, g2],
        [_row_tile(rows, D_MODEL)] * 3 + [_whole((6, D_MODEL)), _whole((1, D_MODEL)), _whole((1, D_MODEL))],
        [jax.ShapeDtypeStruct((seq, D_MODEL), F32), jax.ShapeDtypeStruct((seq, D_MODEL), BF16),
         jax.ShapeDtypeStruct((8, D_MODEL), F32)],
        [_row_tile(rows, D_MODEL), _row_tile(rows, D_MODEL), _whole((8, D_MODEL))], fn, keep_product=False)


def _pre_mix_bwd(x, dx1, mod6, g1, rows):
    seq = x.shape[0]

    def fn(dh, first, ins, outs):
        x_ref, dx1_ref, mod_ref, g_ref = ins
        dx_ref, small_ref = outs

        @pl.when(first)
        def _():
            small_ref[...] = jnp.zeros_like(small_ref)

        g1v = g_ref[...]
        xn, r = _rms(x_ref[...])
        dy = dh * (1.0 + mod_ref[SC_M:SC_M + 1, :])
        small_ref[0:1, :] += _colsum(dy * xn)
        small_ref[1:2, :] += _colsum(dh * (xn * g1v))
        small_ref[2:3, :] += _colsum(dh)
        dx_ref[...] = dx1_ref[...] + _rms_bwd(dy * g1v, xn, r)

    return _Epilogue(
        [x, dx1, mod6, g1],
        [_row_tile(rows, D_MODEL), _row_tile(rows, D_MODEL), _whole((6, D_MODEL)), _whole((1, D_MODEL))],
        [jax.ShapeDtypeStruct((seq, D_MODEL), F32), jax.ShapeDtypeStruct((8, D_MODEL), F32)],
        [_row_tile(rows, D_MODEL), _whole((8, D_MODEL))], fn, keep_product=False)


def _toeplitz_onehot(shape, offset_axis, top):
    m = lax.broadcasted_iota(jnp.int32, shape, offset_axis)
    i = lax.broadcasted_iota(jnp.int32, shape, 1 - offset_axis)
    return (i == jnp.clip(top - m, -MAX_REL, MAX_REL) + MAX_REL).astype(F32)


def _bias_table(rel_bias):
    width = GROUP_Q + GROUP_K

    def body(rb_ref, o_ref, t_ref):
        t_ref[...] = jnp.dot(rb_ref[...], _toeplitz_onehot((N_REL, width), 1, GROUP_K - 1), precision=HIGHEST,
                             preferred_element_type=F32)
        lane = lax.broadcasted_iota(jnp.int32, (N_HEADS, GROUP_K), 1)
        for r in range(GROUP_Q):
            first_key = (r // CHUNK) * CHUNK
            band = jnp.logical_and(lane >= first_key, lane < first_key + BAND)
            o_ref[r] = jnp.where(band, t_ref[:, GROUP_Q - 1 - r:GROUP_Q - 1 - r + GROUP_K], NEG_INF)

    return pl.pallas_call(
        body, name="bias_table", out_shape=jax.ShapeDtypeStruct((GROUP_Q, N_HEADS, GROUP_K), F32),
        scratch_shapes=[pltpu.VMEM((N_HEADS, width), F32)],
    )(rel_bias)


def _bias_grad(dbias_q):
    def body(d_ref, o_ref, t_ref):
        t_ref[...] = jnp.zeros_like(t_ref)
        for qi in range(CHUNK):
            t_ref[:, CHUNK - 1 - qi:CHUNK - 1 - qi + BAND] += d_ref[qi]
        o_ref[...] = jnp.dot(t_ref[...], _toeplitz_onehot((TOEPLITZ, N_REL), 0, BAND - 1), precision=HIGHEST,
                             preferred_element_type=F32)

    return pl.pallas_call(
        body, name="bias_grad", out_shape=jax.ShapeDtypeStruct((N_HEADS, N_REL), F32),
        scratch_shapes=[pltpu.VMEM((N_HEADS, TOEPLITZ), F32)],
    )(dbias_q)


def _load_resident(pairs, sems):
    copies = [pltpu.make_async_copy(src, dst, sems.at[n]) for n, (src, dst) in enumerate(pairs)]
    for cp in copies:
        cp.start()
    for cp in copies:
        cp.wait()


def _softmax_rows(s_ref, t_ref, valid, rows):
    s = s_ref[rows, :] * (HEAD_DIM ** -0.5) + t_ref[rows, :]
    s = jnp.where(valid, s, NEG_INF)
    e = jnp.exp(s - jnp.max(s, axis=-1, keepdims=True))
    return e / jnp.sum(e, axis=-1, keepdims=True)


def _valid_keys(g):
    kj = lax.broadcasted_iota(jnp.int32, (SOFTMAX_ROWS, GROUP_K), 1)
    return kj >= PAD_ROWS - g * GROUP_Q


def _attn_fwd(qkv, kpad, vpad, table, comm=None):
    seq = qkv.shape[0]

    def body(ins, outs, scratch):
        q_ref, k_hbm, v_hbm, t_hbm = ins
        (o_ref,) = outs
        k_ref, v_ref, t_ref, s_ref, p_ref, sems = scratch
        g = pl.program_id(0)

        @pl.when(g == 0)
        def _():
            _load_resident(((k_hbm, k_ref), (v_hbm, v_ref), (t_hbm, t_ref)), sems)

        window = pl.ds(pl.multiple_of(g * GROUP_Q, GROUP_Q), GROUP_K)
        valid = _valid_keys(g)
        for h in range(N_HEADS):
            cols = slice(h * HEAD_DIM, (h + 1) * HEAD_DIM)
            buf = h % 2
            s_ref[buf] = lax.dot_general(q_ref[:, cols], k_ref[window, cols], _DIMS["nt"],
                                         preferred_element_type=F32)
            for r in range(GROUP_Q // SOFTMAX_ROWS):
                rows = slice(r * SOFTMAX_ROWS, (r + 1) * SOFTMAX_ROWS)
                p_ref[buf, rows, :] = _softmax_rows(s_ref.at[buf], t_ref.at[h], valid, rows).astype(BF16)
            o_ref[:, cols] = jnp.dot(p_ref[buf], v_ref[window, cols], preferred_element_type=F32).astype(BF16)

    (ao,), extra = _host_call(
        body, "attn_fwd", grid=(seq // GROUP_Q,),
        in_specs=[_tile(GROUP_Q, D_ATTN), ANY, ANY, ANY], out_specs=[_tile(GROUP_Q, D_ATTN)],
        out_shape=[jax.ShapeDtypeStruct((seq, D_ATTN), BF16)],
        scratch_shapes=[pltpu.VMEM(kpad.shape, BF16), pltpu.VMEM(vpad.shape, BF16), pltpu.VMEM(table.shape, F32),
                        pltpu.VMEM((2, GROUP_Q, GROUP_K), F32), pltpu.VMEM((2, GROUP_Q, GROUP_K), BF16),
                        pltpu.SemaphoreType.DMA((3,))],
        args=[qkv, kpad, vpad, table], comm=comm)
    return ao, extra


def _attn_bwd(qkv, kpad, vpad, table, dao, comm=None):
    seq = qkv.shape[0]
    n_groups = seq // GROUP_Q
    fold_w = GROUP_K + (GROUP - 1) * CHUNK

    def body(ins, outs, scratch):
        q_ref, do_ref, k_hbm, v_hbm, t_hbm = ins
        dq_ref, dkt_hbm, dvt_hbm, db_ref, cs_ref = outs
        k_ref, v_ref, t_ref, db_acc, dkt_acc, dvt_acc, s_ref, dp_ref, p_ref, ds_ref, sems = scratch
        g = pl.program_id(0)

        @pl.when(g == 0)
        def _():
            _load_resident(((k_hbm, k_ref), (v_hbm, v_ref), (t_hbm, t_ref)), sems)
            db_acc[...] = jnp.zeros_like(db_acc)
            dkt_acc[...] = jnp.zeros_like(dkt_acc)
            dvt_acc[...] = jnp.zeros_like(dvt_acc)
            cs_ref[...] = jnp.zeros_like(cs_ref)

        window = pl.ds(pl.multiple_of(g * GROUP_Q, GROUP_Q), GROUP_K)
        valid = _valid_keys(g)
        for h in range(N_HEADS):
            cols = slice(h * HEAD_DIM, (h + 1) * HEAD_DIM)
            buf = h % 2
            qh, doh = q_ref[:, cols], do_ref[:, cols]
            kh, vh = k_ref[window, cols], v_ref[window, cols]
            s_ref[buf] = lax.dot_general(qh, kh, _DIMS["nt"], preferred_element_type=F32)
            dp_ref[buf] = lax.dot_general(doh, vh, _DIMS["nt"], preferred_element_type=F32)
            for r in range(GROUP_Q // SOFTMAX_ROWS):
                rows = slice(r * SOFTMAX_ROWS, (r + 1) * SOFTMAX_ROWS)
                p = _softmax_rows(s_ref.at[buf], t_ref.at[h], valid, rows)
                dp = dp_ref[buf, rows, :]
                ds = p * (dp - jnp.sum(dp * p, axis=-1, keepdims=True))
                chunk = (r * SOFTMAX_ROWS) // CHUNK
                shift = (GROUP - 1 - chunk) * CHUNK
                local = slice(r * SOFTMAX_ROWS - chunk * CHUNK, (r + 1) * SOFTMAX_ROWS - chunk * CHUNK)
                db_acc[h, local, shift:shift + GROUP_K] += ds
                p_ref[buf, rows, :] = p.astype(BF16)
                ds_ref[buf, rows, :] = (ds * (HEAD_DIM ** -0.5)).astype(BF16)
            dq_ref[:, cols] = jnp.dot(ds_ref[buf], kh, preferred_element_type=F32).astype(BF16)
            dkt_acc[cols, window] += lax.dot_general(qh, ds_ref[buf], _DIMS["tn"], preferred_element_type=F32)
            dvt_acc[cols, window] += lax.dot_general(doh, p_ref[buf], _DIMS["tn"], preferred_element_type=F32)
        cs_ref[0:1, :] += _colsum(dq_ref[...].astype(F32))

        @pl.when(g == n_groups - 1)
        def _():
            lo = (GROUP - 1) * CHUNK
            for h in range(N_HEADS):
                db_ref[h] = db_acc[h, :, lo:lo + BAND]
            inside = pl.ds(PAD_ROWS, seq)
            on_diagonal = (lax.broadcasted_iota(jnp.int32, (D_ATTN, D_ATTN), 0)
                           == lax.broadcasted_iota(jnp.int32, (D_ATTN, D_ATTN), 1))
            for row, acc in ((1, dkt_acc), (2, dvt_acc)):
                column = jnp.sum(acc[:, inside], axis=1, keepdims=True)
                cs_ref[row:row + 1, :] = _colsum(jnp.where(on_diagonal, column, 0.0))
            out_k = pltpu.make_async_copy(dkt_acc.at[:, inside], dkt_hbm, sems.at[0])
            out_v = pltpu.make_async_copy(dvt_acc.at[:, inside], dvt_hbm, sems.at[1])
            out_k.start()
            out_v.start()
            out_k.wait()
            out_v.wait()

    t_shape = (D_ATTN, seq + PAD_ROWS)
    outs, extra = _host_call(
        body, "attn_bwd", grid=(n_groups,),
        in_specs=[_tile(GROUP_Q, D_ATTN), _tile(GROUP_Q, D_ATTN), ANY, ANY, ANY],
        out_specs=[_tile(GROUP_Q, D_ATTN), ANY, ANY, _full((N_HEADS, CHUNK, BAND)), _full((8, D_ATTN))],
        out_shape=[jax.ShapeDtypeStruct((seq, D_ATTN), BF16), jax.ShapeDtypeStruct((D_ATTN, seq), F32),
                   jax.ShapeDtypeStruct((D_ATTN, seq), F32), jax.ShapeDtypeStruct((N_HEADS, CHUNK, BAND), F32),
                   jax.ShapeDtypeStruct((8, D_ATTN), F32)],
        scratch_shapes=[pltpu.VMEM(kpad.shape, BF16), pltpu.VMEM(vpad.shape, BF16), pltpu.VMEM(table.shape, F32),
                        pltpu.VMEM((N_HEADS, CHUNK, fold_w), F32), pltpu.VMEM(t_shape, F32),
                        pltpu.VMEM(t_shape, F32), pltpu.VMEM((2, GROUP_Q, GROUP_K), F32),
                        pltpu.VMEM((2, GROUP_Q, GROUP_K), F32), pltpu.VMEM((2, GROUP_Q, GROUP_K), BF16),
                        pltpu.VMEM((2, GROUP_Q, GROUP_K), BF16), pltpu.SemaphoreType.DMA((3,))],
        args=[qkv, dao, kpad, vpad, table], comm=comm)
    return outs, extra


def _assemble_dz(dq, dkt, dvt, dglu_a, dglu_b, dga, dgb):
    seq = dq.shape[0]
    rows = 512
    transposed = pl.BlockSpec((D_ATTN, rows), lambda i: (0, i))

    def body(dq_ref, dkt_ref, dvt_ref, da_ref, db_ref, dga_ref, dgb_ref, o_ref):
        o_ref[:, 0:D_ATTN] = dq_ref[...]
        o_ref[:, D_ATTN:2 * D_ATTN] = dkt_ref[...].T.astype(BF16)
        o_ref[:, 2 * D_ATTN:3 * D_ATTN] = dvt_ref[...].T.astype(BF16)
        off = 3 * D_ATTN
        for ref in (da_ref, db_ref, dga_ref, dgb_ref):
            width = ref.shape[1]
            o_ref[:, off:off + width] = ref[...]
            off += width

    width = 3 * D_ATTN + 2 * D_CONV + 2 * D_MODEL
    return pl.pallas_call(
        body, name="assemble_dz", out_shape=jax.ShapeDtypeStruct((seq, width), BF16), grid=(seq // rows,),
        in_specs=[_tile(rows, D_ATTN), transposed, transposed, _tile(rows, D_CONV), _tile(rows, D_CONV),
                  _tile(rows, D_MODEL), _tile(rows, D_MODEL)],
        out_specs=_tile(rows, width), compiler_params=_cparams(1),
    )(dq, dkt, dvt, dglu_a, dglu_b, dga, dgb)


CONV_ROWS = 256


def _ln_silu(u1, g, b):
    mu = jnp.mean(u1, axis=-1, keepdims=True)
    xc = u1 - mu
    rs = lax.rsqrt(jnp.mean(xc * xc, axis=-1, keepdims=True) + EPS)
    xhat = xc * rs
    u2 = xhat * g + b
    return xhat, rs, u2


def _glu_into(s_ref, a_ref, b_ref, ah_ref, bh_ref, first):
    halo = ah_ref[...] * _sig(bh_ref[...])
    s_ref[0:CONV_HALO, :] = jnp.where(first, 0.0, halo)
    s_ref[CONV_HALO:CONV_HALO + CONV_ROWS, :] = a_ref[...] * _sig(b_ref[...])


def _conv_fwd(zr, w_dw, b_dw, g_ln, b_ln, comm=None):
    seq = zr.shape[0]

    def body(a_ref, b_ref, ah_ref, bh_ref, w_ref, bias_ref, g_ref, bl_ref, u1_ref, u3_ref, s_ref):
        _glu_into(s_ref, a_ref, b_ref, ah_ref, bh_ref, pl.program_id(0) == 0)
        acc = jnp.zeros((CONV_ROWS, D_CONV), F32) + bias_ref[...]
        for j in range(CONV_K):
            acc = acc + w_ref[j:j + 1, :] * s_ref[2 + j:2 + j + CONV_ROWS, :]
        u1_ref[...] = acc
        _, _, u2 = _ln_silu(acc, g_ref[...], bl_ref[...])
        u3_ref[...] = (u2 * _sig(u2)).astype(BF16)

    return _host_call(
        lambda ins, outs, scratch: body(*ins, *outs, *scratch), "conv_fwd", grid=(seq // CONV_ROWS,),
        in_specs=[_tile(CONV_ROWS, D_CONV, 0), _tile(CONV_ROWS, D_CONV, 1),
                  _prev(CONV_HALO, D_CONV, CONV_ROWS, 0), _prev(CONV_HALO, D_CONV, CONV_ROWS, 1),
                  _full((CONV_K, D_CONV)), _full((1, D_CONV)), _full((1, D_CONV)), _full((1, D_CONV))],
        out_specs=[_tile(CONV_ROWS, D_CONV), _tile(CONV_ROWS, D_CONV)],
        out_shape=[jax.ShapeDtypeStruct((seq, D_CONV), F32), jax.ShapeDtypeStruct((seq, D_CONV), BF16)],
        scratch_shapes=[pltpu.VMEM((CONV_HALO + CONV_ROWS, D_CONV), F32)],
        args=[zr, zr, zr, zr, w_dw, b_dw, g_ln, b_ln], comm=comm)


def _conv_bwd(zr, u1, du3, w_dw, g_ln, b_ln, comm=None):
    seq = zr.shape[0]
    n_tiles = seq // CONV_ROWS
    n_halo = seq // CONV_HALO
    ext = CONV_ROWS + CONV_HALO

    def body(a_ref, b_ref, ah_ref, bh_ref, u1_ref, u1n_ref, d3_ref, d3n_ref, w_ref, g_ref, bl_ref,
             da_ref, db_ref, dw_ref, small_ref, s_ref, d_ref):
        i = pl.program_id(0)

        @pl.when(i == 0)
        def _():
            dw_ref[...] = jnp.zeros_like(dw_ref)
            small_ref[...] = jnp.zeros_like(small_ref)

        _glu_into(s_ref, a_ref, b_ref, ah_ref, bh_ref, i == 0)
        gv, bv = g_ref[...], bl_ref[...]

        def du1_of(u1, d3):
            xhat, rs, u2 = _ln_silu(u1, gv, bv)
            sg = _sig(u2)
            du2 = d3 * (sg * (1.0 + u2 * (1.0 - sg)))
            dxh = du2 * gv
            du1 = rs * (dxh - jnp.mean(dxh, axis=-1, keepdims=True)
                        - xhat * jnp.mean(dxh * xhat, axis=-1, keepdims=True))
            return du1, du2, xhat

        du1, du2, xhat = du1_of(u1_ref[...], d3_ref[...])
        du1n, _, _ = du1_of(u1n_ref[...], d3n_ref[...])
        d_ref[0:CONV_ROWS, :] = du1
        d_ref[CONV_ROWS:ext, :] = jnp.where(i == n_tiles - 1, 0.0, du1n)
        small_ref[0:1, :] += _colsum(du1)
        small_ref[1:2, :] += _colsum(du2 * xhat)
        small_ref[2:3, :] += _colsum(du2)
        du0 = jnp.zeros((CONV_ROWS, D_CONV), F32)
        for j in range(CONV_K):
            dw_ref[j:j + 1, :] += _colsum(du1 * s_ref[2 + j:2 + j + CONV_ROWS, :])
            du0 = du0 + w_ref[j:j + 1, :] * d_ref[CONV_K - 1 - j:CONV_K - 1 - j + CONV_ROWS, :]
        sb = _sig(b_ref[...])
        da = du0 * sb
        dbv = du0 * a_ref[...] * sb * (1.0 - sb)
        da_ref[...] = da.astype(BF16)
        db_ref[...] = dbv.astype(BF16)
        small_ref[3:4, :] += _colsum(da)
        small_ref[4:5, :] += _colsum(dbv)

    return _host_call(
        lambda ins, outs, scratch: body(*ins, *outs, *scratch), "conv_bwd", grid=(n_tiles,),
        in_specs=[_tile(CONV_ROWS, D_CONV, 0), _tile(CONV_ROWS, D_CONV, 1),
                  _prev(CONV_HALO, D_CONV, CONV_ROWS, 0), _prev(CONV_HALO, D_CONV, CONV_ROWS, 1),
                  _tile(CONV_ROWS, D_CONV), _next(CONV_HALO, D_CONV, CONV_ROWS, n_halo),
                  _tile(CONV_ROWS, D_CONV), _next(CONV_HALO, D_CONV, CONV_ROWS, n_halo),
                  _full((CONV_K, D_CONV)), _full((1, D_CONV)), _full((1, D_CONV))],
        out_specs=[_tile(CONV_ROWS, D_CONV), _tile(CONV_ROWS, D_CONV), _full((CONV_HALO, D_CONV)),
                   _full((8, D_CONV))],
        out_shape=[jax.ShapeDtypeStruct((seq, D_CONV), BF16), jax.ShapeDtypeStruct((seq, D_CONV), BF16),
                   jax.ShapeDtypeStruct((CONV_HALO, D_CONV), F32), jax.ShapeDtypeStruct((8, D_CONV), F32)],
        scratch_shapes=[pltpu.VMEM((ext, D_CONV), F32), pltpu.VMEM((ext, D_CONV), F32)],
        args=[zr, zr, zr, zr, u1, u1, du3, du3, w_dw, g_ln, b_ln], comm=comm)


MERGE_ROWS = 256


def _merge_fwd(ao, u3, zr, w_ao, w_co, b_co):
    seq = ao.shape[0]

    def body(ao_ref, u3_ref, ga_ref, gb_ref, wa_ref, wc_ref, bc_ref, y_ref, a_ref, cb_ref):
        a = jnp.dot(ao_ref[...], wa_ref[...], preferred_element_type=F32)
        cb = jnp.dot(u3_ref[...], wc_ref[...], preferred_element_type=F32) + bc_ref[...]
        a_ref[...] = a
        cb_ref[...] = cb
        y_ref[...] = (_sig(ga_ref[...]) * a + _sig(gb_ref[...]) * cb).astype(BF16)

    f32_out = jax.ShapeDtypeStruct((seq, D_MODEL), F32)
    return pl.pallas_call(
        body, name="merge_fwd",
        out_shape=[jax.ShapeDtypeStruct((seq, D_MODEL), BF16), f32_out, f32_out],
        grid=(seq // MERGE_ROWS,),
        in_specs=[_tile(MERGE_ROWS, D_ATTN), _tile(MERGE_ROWS, D_CONV), _tile(MERGE_ROWS, D_MODEL, 1),
                  _tile(MERGE_ROWS, D_MODEL, 2), _full(w_ao.shape), _full(w_co.shape), _full((1, D_MODEL))],
        out_specs=[_tile(MERGE_ROWS, D_MODEL)] * 3, compiler_params=_cparams(1),
    )(ao, u3, zr, zr, w_ao, w_co, b_co)


def _merge_bwd(a, cb, zr, rows):
    seq = a.shape[0]

    def fn(dy_v, first, ins, outs):
        a_ref, cb_ref, ga_ref, gb_ref = ins
        da_ref, dcb_ref, dga_ref, dgb_ref, small_ref = outs

        @pl.when(first)
        def _():
            small_ref[...] = jnp.zeros_like(small_ref)

        sa, sb = _sig(ga_ref[...]), _sig(gb_ref[...])
        dcb = dy_v * sb
        dga = dy_v * a_ref[...] * sa * (1.0 - sa)
        dgb = dy_v * cb_ref[...] * sb * (1.0 - sb)
        da_ref[...] = (dy_v * sa).astype(BF16)
        dcb_ref[...] = dcb.astype(BF16)
        dga_ref[...] = dga.astype(BF16)
        dgb_ref[...] = dgb.astype(BF16)
        small_ref[0:1, :] += _colsum(dga)
        small_ref[1:2, :] += _colsum(dgb)
        small_ref[2:3, :] += _colsum(dcb)

    bf = jax.ShapeDtypeStruct((seq, D_MODEL), BF16)
    gate = lambda col: pl.BlockSpec((rows, D_MODEL), lambda i, j: (i, col))
    return _Epilogue(
        [a, cb, zr, zr], [_row_tile(rows, D_MODEL), _row_tile(rows, D_MODEL), gate(1), gate(2)],
        [bf, bf, bf, bf, jax.ShapeDtypeStruct((8, D_MODEL), F32)],
        [_row_tile(rows, D_MODEL)] * 4 + [_whole((8, D_MODEL))], fn, keep_product=False)


FFN_ROWS = 2048
FFN_BLOCKS = D_FF // FFN_COLS
GELU_C = math.sqrt(2.0 / math.pi)


def _gelu(v):
    t = jnp.tanh(GELU_C * (v + 0.044715 * (v * v * v)))
    return 0.5 * v * (1.0 + t), t


def _gelu_grad(v, t):
    return 0.5 * (1.0 + t) + 0.5 * v * (1.0 - t * t) * (GELU_C * (1.0 + 3.0 * 0.044715 * (v * v)))


def _sublane_rows(ref, n):
    return [jnp.broadcast_to(ref[r:r + 1, :], (8, FFN_COLS)) for r in range(n)]


def _rolls(tile, shifts):
    return tuple(pltpu.roll(tile, s, 0) for s in shifts)


def _behind(prev_rolls, cur, row_id):
    rolls = _rolls(cur, (1, 2))
    x1 = jnp.where(row_id < 1, prev_rolls[0], rolls[0])
    x2 = jnp.where(row_id < 2, prev_rolls[1], rolls[1])
    return (x2, x1, cur), rolls


def _ahead(cur_rolls, next_rolls, row_id):
    return (jnp.where(row_id < 7, cur_rolls[0], next_rolls[0]), jnp.where(row_id < 6, cur_rolls[1], next_rolls[1]))


def _conv3(taps, w, bias):
    return w[0] * taps[0] + w[1] * taps[1] + w[2] * taps[2] + bias


def _ffn_specs(rows):
    tile = lambda off: pl.BlockSpec((rows, FFN_COLS), lambda j, i: (i, j + off))
    prev = lambda off: pl.BlockSpec((FFN_HALO, FFN_COLS),
                                    lambda j, i: (jnp.maximum(i * (rows // FFN_HALO) - 1, 0), j + off))
    wgt = lambda off: pl.BlockSpec((3, FFN_COLS), lambda j, i: (0, j + off))
    vec = lambda off: pl.BlockSpec((1, FFN_COLS), lambda j, i: (0, j + off))
    return tile, prev, wgt, vec


def _ffn_act(up, w_dw, b_dw):
    seq = up.shape[0]
    tile, prev, wgt, vec = _ffn_specs(FFN_ROWS)

    def body(v_ref, g_ref, vp_ref, gp_ref, wv_ref, wg_ref, bv_ref, bg_ref, act_ref):
        first = pl.program_id(1) == 0
        row_id = lax.broadcasted_iota(jnp.int32, (8, FFN_COLS), 0)
        wv, wg = _sublane_rows(wv_ref, 3), _sublane_rows(wg_ref, 3)
        (bv,), (bg,) = _sublane_rows(bv_ref, 1), _sublane_rows(bg_ref, 1)
        rolls_v = _rolls(jnp.where(first, 0.0, vp_ref[...]), (1, 2))
        rolls_g = _rolls(jnp.where(first, 0.0, gp_ref[...]), (1, 2))
        for row in range(0, FFN_ROWS, 16):
            halves = []
            for r in (row, row + 8):
                taps_v, rolls_v = _behind(rolls_v, v_ref[r:r + 8, :], row_id)
                taps_g, rolls_g = _behind(rolls_g, g_ref[r:r + 8, :], row_id)
                halves.append(_gelu(_conv3(taps_g, wg, bg))[0] * _conv3(taps_v, wv, bv))
            act_ref[row:row + 16, :] = jnp.concatenate(halves, axis=0).astype(BF16)

    return pl.pallas_call(
        body, name="ffn_act", out_shape=jax.ShapeDtypeStruct((seq, D_FF), BF16),
        grid=(FFN_BLOCKS, seq // FFN_ROWS),
        in_specs=[tile(0), tile(FFN_BLOCKS), prev(0), prev(FFN_BLOCKS), wgt(0), wgt(FFN_BLOCKS),
                  vec(0), vec(FFN_BLOCKS)],
        out_specs=tile(0), compiler_params=_cparams(2),
    )(up, up, up, up, w_dw, w_dw, b_dw, b_dw)


def _ffn_act_bwd(up, dact, w_dw, b_dw, comm=None):
    seq = up.shape[0]
    n_tiles = seq // FFN_ROWS
    n_halo = seq // FFN_HALO
    tile, prev, wgt, vec = _ffn_specs(FFN_ROWS)
    nxt = lambda off: pl.BlockSpec(
        (FFN_HALO, FFN_COLS), lambda j, i: (jnp.minimum((i + 1) * (FFN_ROWS // FFN_HALO), n_halo - 1), j + off))
    acc = lambda off: pl.BlockSpec((8, FFN_COLS), lambda j, i: (0, j + off))

    def body(v_ref, g_ref, vp_ref, gp_ref, vn_ref, gn_ref, da_ref, dan_ref, wv_ref, wg_ref, bv_ref, bg_ref,
             dv_out, dg_out, dwv_ref, dwg_ref, dbv_ref, dbg_ref):
        i = pl.program_id(1)
        first, last = i == 0, i == n_tiles - 1

        @pl.when(first)
        def _():
            for r in (dwv_ref, dwg_ref, dbv_ref, dbg_ref):
                r[...] = jnp.zeros_like(r)

        row_id = lax.broadcasted_iota(jnp.int32, (8, FFN_COLS), 0)
        wv, wg = _sublane_rows(wv_ref, 3), _sublane_rows(wg_ref, 3)
        (bv,), (bg,) = _sublane_rows(bv_ref, 1), _sublane_rows(bg_ref, 1)
        zero = jnp.zeros((8, FFN_COLS), F32)
        sums_v, sums_g = [zero] * 4, [zero] * 4
        rolls_v = _rolls(jnp.where(first, 0.0, vp_ref[...]), (1, 2))
        rolls_g = _rolls(jnp.where(first, 0.0, gp_ref[...]), (1, 2))
        behind = None
        done_v, done_g = [], []

        def grads(v_tile, g_tile, dact, rolls_v, rolls_g):
            taps_v, rolls_v = _behind(rolls_v, v_tile, row_id)
            taps_g, rolls_g = _behind(rolls_g, g_tile, row_id)
            val, gate = _conv3(taps_v, wv, bv), _conv3(taps_g, wg, bg)
            gel, t = _gelu(gate)
            return dact * gel, dact * val * _gelu_grad(gate, t), taps_v, taps_g, rolls_v, rolls_g

        def finish(tile, nxt, row):
            for (d, d_rolls), (_, n_rolls), w, done, o_ref in ((tile[0], nxt[0], wv, done_v, dv_out),
                                                               (tile[1], nxt[1], wg, done_g, dg_out)):
                d1, d2 = _ahead(d_rolls, n_rolls, row_id)
                done.append(w[2] * d + w[1] * d1 + w[0] * d2)
                if len(done) == 2:
                    o_ref[row - 16:row, :] = jnp.concatenate(done, axis=0).astype(BF16)
                    done.clear()

        for row in range(0, FFN_ROWS, 16):
            dact16 = da_ref[row:row + 16, :].astype(F32)
            for r, dact in ((row, dact16[0:8, :]), (row + 8, dact16[8:16, :])):
                dval, dgate, taps_v, taps_g, rolls_v, rolls_g = grads(v_ref[r:r + 8, :], g_ref[r:r + 8, :], dact,
                                                                      rolls_v, rolls_g)
                sums_v = [s + dval * x for s, x in zip(sums_v, taps_v)] + [sums_v[3] + dval]
                sums_g = [s + dgate * x for s, x in zip(sums_g, taps_g)] + [sums_g[3] + dgate]
                tile = ((dval, _rolls(dval, (7, 6))), (dgate, _rolls(dgate, (7, 6))))
                if behind is not None:
                    finish(behind, tile, r)
                behind = tile
        dact_next = jnp.where(last, 0.0, dan_ref[...].astype(F32)[0:FFN_HALO, :])
        dval, dgate, *_ = grads(vn_ref[...], gn_ref[...], dact_next, rolls_v, rolls_g)
        finish(behind, ((dval, _rolls(dval, (7, 6))), (dgate, _rolls(dgate, (7, 6)))), FFN_ROWS)
        for sums, dw_ref, db_ref in ((sums_v, dwv_ref, dbv_ref), (sums_g, dwg_ref, dbg_ref)):
            for tap in range(3):
                dw_ref[tap:tap + 1, :] += _colsum(sums[tap])
            db_ref[0:1, :] += _colsum(sums[3])

    half = jax.ShapeDtypeStruct((seq, D_FF), BF16)
    acc_shape = jax.ShapeDtypeStruct((8, D_FF), F32)
    return _host_call(
        lambda ins, outs, scratch: body(*ins, *outs, *scratch), "ffn_act_bwd", grid=(FFN_BLOCKS, n_tiles),
        in_specs=[tile(0), tile(FFN_BLOCKS), prev(0), prev(FFN_BLOCKS), nxt(0), nxt(FFN_BLOCKS),
                  tile(0), pl.BlockSpec((16, FFN_COLS), lambda j, i: (
                      jnp.minimum((i + 1) * (FFN_ROWS // 16), seq // 16 - 1), j)),
                  wgt(0), wgt(FFN_BLOCKS), vec(0), vec(FFN_BLOCKS)],
        out_specs=[tile(0), tile(0), acc(0), acc(0), acc(0), acc(0)],
        out_shape=[half, half, acc_shape, acc_shape, acc_shape, acc_shape],
        scratch_shapes=[], args=[up, up, up, up, up, up, dact, dact, w_dw, w_dw, b_dw, b_dw], comm=comm)


def _cols_to_blocks(full_cols):
    k, n8 = full_cols.shape
    return jnp.transpose(full_cols.reshape(k, N_DEV, n8 // N_DEV), (1, 0, 2))


def _rows_to_blocks(full_rows):
    r8, n = full_rows.shape
    return full_rows.reshape(N_DEV, r8 // N_DEV, n)


def _blocks_to_cols(gathered):
    _, k, n = gathered.shape
    return jnp.transpose(gathered, (1, 0, 2)).reshape(k, N_DEV * n)


def kernel(x, c, w_ada, b_ada, g_pre_mix, g_post_mix, w_in, b_in, rel_bias, w_attn_o, w_dw_conv, b_dw_conv, g_conv_ln, b_conv_ln, w_conv_o, b_conv_o, w_mix_o, g_pre_ffn, g_post_ffn, w_up, w_dw_ffn, b_dw_ffn, w_down, loss_target, m_w_ada, m_b_ada, m_g_pre_mix, m_g_post_mix, m_w_in, m_b_in, m_rel_bias, m_w_attn_o, m_w_dw_conv, m_b_dw_conv, m_g_conv_ln, m_b_conv_ln, m_w_conv_o, m_b_conv_o, m_w_mix_o, m_g_pre_ffn, m_g_post_ffn, m_w_up, m_w_dw_ffn, m_b_dw_ffn, m_w_down, v_w_ada, v_b_ada, v_g_pre_mix, v_g_post_mix, v_w_in, v_b_in, v_rel_bias, v_w_attn_o, v_w_dw_conv, v_b_dw_conv, v_g_conv_ln, v_b_conv_ln, v_w_conv_o, v_b_conv_o, v_w_mix_o, v_g_pre_ffn, v_g_post_ffn, v_w_up, v_w_dw_ffn, v_b_dw_ffn, v_w_down):
    names = ["w_ada", "b_ada", "g_pre_mix", "g_post_mix", "w_in", "b_in", "rel_bias", "w_attn_o", "w_dw_conv",
             "b_dw_conv", "g_conv_ln", "b_conv_ln", "w_conv_o", "b_conv_o", "w_mix_o", "g_pre_ffn", "g_post_ffn",
             "w_up", "w_dw_ffn", "b_dw_ffn", "w_down"]
    weights = dict(zip(names, [w_ada, b_ada, g_pre_mix, g_post_mix, w_in, b_in, rel_bias, w_attn_o, w_dw_conv,
                               b_dw_conv, g_conv_ln, b_conv_ln, w_conv_o, b_conv_o, w_mix_o, g_pre_ffn,
                               g_post_ffn, w_up, w_dw_ffn, b_dw_ffn, w_down]))
    mom_m = dict(zip(names, [m_w_ada, m_b_ada, m_g_pre_mix, m_g_post_mix, m_w_in, m_b_in, m_rel_bias, m_w_attn_o,
                             m_w_dw_conv, m_b_dw_conv, m_g_conv_ln, m_b_conv_ln, m_w_conv_o, m_b_conv_o,
                             m_w_mix_o, m_g_pre_ffn, m_g_post_ffn, m_w_up, m_w_dw_ffn, m_b_dw_ffn, m_w_down]))
    mom_v = dict(zip(names, [v_w_ada, v_b_ada, v_g_pre_mix, v_g_post_mix, v_w_in, v_b_in, v_rel_bias, v_w_attn_o,
                             v_w_dw_conv, v_b_dw_conv, v_g_conv_ln, v_b_conv_ln, v_w_conv_o, v_b_conv_o,
                             v_w_mix_o, v_g_pre_ffn, v_g_post_ffn, v_w_up, v_w_dw_ffn, v_b_dw_ffn, v_w_down]))
    shapes = {n: w.shape for n, w in weights.items()}

    seq = x.shape[1]
    me = 4 * lax.axis_index("x") + 2 * lax.axis_index("y") + lax.axis_index("c")
    x2 = x.reshape(seq, D_MODEL)
    target = loss_target.reshape(seq, D_MODEL)
    sq = lambda a: a.reshape(a.shape[1:])
    bf = lambda a: sq(a).astype(BF16)

    c_act = _silu_vec(c)
    c_all, g_in, g_dwc, g_dwf = _run_comm(
        _gather_comm([c_act, bf(w_in), sq(w_dw_conv), sq(w_dw_ffn)]), "gather_first")
    c_all = c_all.reshape(N_DEV, D_MODEL)
    wf_in = _blocks_to_cols(g_in)
    wf_dwc = _blocks_to_cols(g_dwc)
    wf_dwf = _blocks_to_cols(g_dwf)

    (mod_all,) = _run_comm(_gather_comm([_ada_fwd(c_all, sq(w_ada))]), "gather_mod")
    mod = lax.dynamic_index_in_dim(mod_all, me, axis=1, keepdims=False)
    mod6 = (mod.reshape(1, 6 * D_MODEL) + b_ada).reshape(6, D_MODEL)

    h1 = _pre_mix(x2, mod6, g_pre_mix)
    qkv = _mm(h1, wf_in, "nn", BF16, "in_proj_qkv", bias=b_in, tm=1024, tn=768, cols=(0, 3 * D_ATTN))
    zr, _, (g_ao, g_co, g_mo) = _mm(h1, wf_in, "nn", F32, "in_proj_rest", bias=b_in, tm=1024, tn=3 * D_ATTN,
                                 cols=(3 * D_ATTN, 2 * D_CONV + 2 * D_MODEL),
                                 comm=_gather_comm([bf(w_attn_o), bf(w_conv_o), bf(w_mix_o)]))
    kpad = jnp.pad(qkv[:, D_ATTN:2 * D_ATTN], ((PAD_ROWS, 0), (0, 0)))
    vpad = jnp.pad(qkv[:, 2 * D_ATTN:], ((PAD_ROWS, 0), (0, 0)))
    table = jnp.transpose(_bias_table(sq(rel_bias)), (1, 0, 2))
    ao, (g_up,) = _attn_fwd(qkv, kpad, vpad, table, comm=_gather_comm([bf(w_up)]))
    (u1, u3), (g_dn,) = _conv_fwd(zr, wf_dwc, b_dw_conv, g_conv_ln, b_conv_ln, comm=_gather_comm([bf(w_down)]))
    wf_ao = _blocks_to_cols(g_ao)
    wf_co = _blocks_to_cols(g_co)
    wf_mo = g_mo.reshape(D_MODEL, D_MODEL)
    wf_up = _blocks_to_cols(g_up)
    wf_dn = g_dn.reshape(D_FF, D_MODEL)
    y, a_br, cb_br = _merge_fwd(ao, u3, zr, wf_ao, wf_co, b_conv_o)
    ymix, (x1, h2), _ = _mm(y, wf_mo, "nn", F32, "mix_o", tm=512, tn=D_MODEL,
                            epilogue=_post_mix_pre_ffn(x2, mod6, g_post_mix, g_pre_ffn, 512))
    up = _mm(h2, wf_up, "nn", F32, "ffn_up", tm=1024, tn=1408)
    act = _ffn_act(up, wf_dwf, b_dw_ffn)
    _, (loss_lanes, dout, dyf, small_f), _ = _mm(act, wf_dn, "nn", F32, "ffn_down", tm=512, tn=D_MODEL,
                                                 epilogue=_final(x1, target, mod6, g_post_ffn, 512))

    dact = _mm(dyf, wf_dn, "nt", BF16, "ffn_down_dx", tm=1024, tn=1408)
    gw_down = _mm(act, dyf, "tn", BF16, "ffn_down_dw", tm=256, tn=1024)
    (dup_v, dup_g, dwv, dwg, dbv, dbg), (parts_down,) = _ffn_act_bwd(
        up, dact, wf_dwf, b_dw_ffn, comm=_scatter_comm([_rows_to_blocks(gw_down)]))
    _, (dx1, dymix, small_m), _ = _mm([dup_v, dup_g], wf_up, "nt", F32, "ffn_up_dx", tm=256, tn=D_MODEL,
                                      epilogue=_mid_bwd(x1, dout, ymix, mod6, g_pre_ffn, g_post_mix, 256))
    blocks_up = _mm_tn_blocks(h2, [dup_v, dup_g], 2 * D_FF // N_DEV, "ffn_up_dw", k_steps=2)
    _, (da, dcb, dga, dgb, small_g), _ = _mm(dymix, wf_mo, "nt", F32, "mix_o_dx", tm=512, tn=D_MODEL,
                                             epilogue=_merge_bwd(a_br, cb_br, zr, 512))
    gw_mo = _mm(y, dymix, "tn", BF16, "mix_o_dw")
    dao = _mm(da, wf_ao, "nt", BF16, "attn_o_dx", tm=1024)
    gw_ao = _mm(ao, da, "tn", BF16, "attn_o_dw")
    du3 = _mm(dcb, wf_co, "nt", F32, "conv_o_dx", tm=1024)
    gw_co = _mm(u3, dcb, "tn", BF16, "conv_o_dw")
    (dq, dkt, dvt, dbias, small_a), (parts_up,) = _attn_bwd(
        qkv, kpad, vpad, table, dao, comm=_scatter_comm([blocks_up]))
    g_rel = _bias_grad(jnp.transpose(dbias, (1, 0, 2)))
    (dglu_a, dglu_b, dw_conv, small_c), (parts_mo, parts_ao, parts_co) = _conv_bwd(
        zr, u1, du3, wf_dwc, g_conv_ln, b_conv_ln,
        comm=_scatter_comm([_rows_to_blocks(gw_mo), _cols_to_blocks(gw_ao), _cols_to_blocks(gw_co)]))
    dz = _assemble_dz(dq, dkt, dvt, dglu_a, dglu_b, dga, dgb)
    blocks_in = _mm_tn_blocks(h1, [dz], dz.shape[1] // N_DEV, "in_proj_dw")
    _, (grad_x, small_x), (parts_in,) = _mm(dz, wf_in, "nt", F32, "in_proj_dx", tm=512, tn=D_MODEL,
                                            comm=_scatter_comm([blocks_in]),
                                            epilogue=_pre_mix_bwd(x2, dx1, mod6, g_pre_mix, 512))

    packed = _pack_grads(small_x, small_m, small_f, small_g, small_a, small_c, dbv, dbg, dwv, dwg, dw_conv)
    gathered, gathered_rel, gathered_loss = _run_comm(_gather_comm([packed, g_rel, loss_lanes]), "gather_small")
    gathered = gathered.reshape(N_DEV, PACKED_TOTAL)
    updates, g_dwc_full, g_dwf_full, loss_all = _small_adamw(gathered, gathered_rel, gathered_loss, weights, mom_m,
                                                             mom_v)
    loss = loss_all[0, 0]

    grads, deltas, new_m, new_v = {}, {}, {}, {}

    def record(name, update):
        for dst, val in zip((grads, deltas, new_m, new_v), update):
            dst[name] = val.reshape(shapes[name])

    for name, update in updates.items():
        record(name, update)

    def local_update(name, grad):
        record(name, _adamw(sq(weights[name]), sq(mom_m[name]), sq(mom_v[name]), "adamw_" + name, g=grad))

    conv_cols, ffn_cols, ada_cols = D_CONV // N_DEV, 2 * D_FF // N_DEV, 6 * D_MODEL // N_DEV
    local_update("w_dw_conv", lax.dynamic_slice(g_dwc_full, (0, me * conv_cols), (CONV_K, conv_cols)))
    local_update("w_dw_ffn", lax.dynamic_slice(g_dwf_full, (0, me * ffn_cols), (3, ffn_cols)))
    local_update("w_ada", _ada_grad(c_all, lax.dynamic_slice(gathered, (0, me * ada_cols), (N_DEV, ada_cols))))

    for name, part in (("w_in", parts_in), ("w_attn_o", parts_ao), ("w_conv_o", parts_co), ("w_mix_o", parts_mo),
                       ("w_up", parts_up), ("w_down", parts_down)):
        record(name, _adamw(sq(weights[name]), sq(mom_m[name]), sq(mom_v[name]), "adamw_" + name, parts=part))

    return (loss, grad_x.reshape(x.shape), *[grads[n] for n in names], *[deltas[n] for n in names],
            *[new_m[n] for n in names], *[new_v[n] for n in names])
```

```python
import functools
import math

import jax
import jax.numpy as jnp
from jax import lax
from jax.experimental import pallas as pl
from jax.experimental.pallas import tpu as pltpu

F32 = jnp.float32
BF16 = jnp.bfloat16
HIGHEST = lax.Precision.HIGHEST

D_MODEL = 1024
CHUNK = 64
LEFT_CHUNKS = 8
BAND = (LEFT_CHUNKS + 1) * CHUNK
PAD_ROWS = LEFT_CHUNKS * CHUNK
GROUP = 4
GROUP_Q = GROUP * CHUNK
GROUP_K = GROUP_Q + PAD_ROWS
SOFTMAX_ROWS = 16
TOEPLITZ = 640
N_HEADS = 8
HEAD_DIM = 64
D_ATTN = 512
D_CONV = 512
CONV_K = 31
CONV_HALO = 32
MAX_REL = 128
N_REL = 2 * MAX_REL + 1
D_FF = 2816
FFN_HALO = 8
FFN_COLS = 256
EPS = 1e-6
NEG_INF = -1e30
N_DEV = 8

ADAM_LR = 0.001
ADAM_B1 = 0.9
ADAM_B2 = 0.999
ADAM_EPS = 1e-08
ADAM_WD = 0.01
ADAM_STEP = 10

VMEM_LIMIT_BYTES = 56 * 1024 * 1024
ADAMW_BLOCK_BYTES = 768 * 1024

MESH = pl.DeviceIdType.MESH
ANY = pl.BlockSpec(memory_space=pl.ANY)

SH_M, SC_M, GT_M, SH_F, SC_F, GT_F = range(6)

SMALL = (("b_ada", 6144), ("g_pre_mix", 1024), ("g_post_mix", 1024), ("b_in", 4608), ("b_dw_conv", 512),
         ("g_conv_ln", 512), ("b_conv_ln", 512), ("b_conv_o", 1024), ("g_pre_ffn", 1024), ("g_post_ffn", 1024),
         ("b_dw_ffn", 5632))
PACKED_TOTAL = sum(n for _, n in SMALL) + CONV_K * D_CONV + 3 * 2 * D_FF


def _cparams(n_axes):
    return pltpu.CompilerParams(vmem_limit_bytes=VMEM_LIMIT_BYTES,
                                dimension_semantics=("arbitrary",) * n_axes)


def _sig(v):
    return 1.0 / (1.0 + jnp.exp(-v))


def _pick(n, target):
    if n <= target:
        return n
    t = target - target % 128
    while n % t:
        t -= 128
    return t


def _tile(rows, cols, col=0):
    return pl.BlockSpec((rows, cols), lambda i: (i, col))


def _full(shape):
    zeros = (0,) * len(shape)
    return pl.BlockSpec(shape, lambda i: zeros)


def _prev(halo, cols, rows, col=0):
    return pl.BlockSpec((halo, cols), lambda i: (jnp.maximum(i * (rows // halo) - 1, 0), col))


def _next(halo, cols, rows, n_blocks, col=0):
    return pl.BlockSpec((halo, cols), lambda i: (jnp.minimum((i + 1) * (rows // halo), n_blocks - 1), col))


class _Comm:
    def __init__(self, inputs, out_shapes, sems, start, finish, relay=None):
        self.inputs, self.out_shapes, self.sems, self.start, self.finish = inputs, out_shapes, sems, start, finish
        self.relay = relay


def _host_call(body, name, grid, in_specs, out_specs, out_shape, scratch_shapes, args, comm=None):
    n_in, n_out, n_scr = len(args), len(out_shape), len(scratch_shapes)
    c_in = list(comm.inputs) if comm else []
    c_out = list(comm.out_shapes) if comm else []
    c_sem = list(comm.sems) if comm else []

    def full(*refs):
        bounds = [0, n_in, len(c_in), n_out, len(c_out), n_scr, len(c_sem)]
        cuts = [sum(bounds[:i + 1]) for i in range(len(bounds))]
        ins, cins, outs, couts, scr, csems = (refs[lo:hi] for lo, hi in zip(cuts[:-1], cuts[1:]))
        if comm:
            first = functools.reduce(jnp.logical_and, [pl.program_id(ax) == 0 for ax in range(len(grid))])
            pl.when(first)(lambda: comm.start(cins, couts, csems))
            last = functools.reduce(jnp.logical_and, [pl.program_id(ax) == grid[ax] - 1 for ax in range(len(grid))])
            if comm.relay is not None:
                pl.when(last)(lambda: comm.relay(cins, couts, csems))
        body(ins, outs, scr)
        if comm:
            pl.when(last)(lambda: comm.finish(cins, couts, csems))

    res = pl.pallas_call(
        full, name=name, grid=grid, in_specs=list(in_specs) + [ANY] * len(c_in),
        out_specs=list(out_specs) + [ANY] * len(c_out), out_shape=list(out_shape) + c_out,
        scratch_shapes=list(scratch_shapes) + c_sem, compiler_params=_cparams(len(grid)),
    )(*args, *c_in)
    return list(res[:n_out]), list(res[n_out:])


def _run_comm(comm, name):
    n_in, n_out = len(comm.inputs), len(comm.out_shapes)

    def body(*refs):
        ins, outs, sems = refs[:n_in], refs[n_in:n_in + n_out], refs[n_in + n_out:]
        comm.start(ins, outs, sems)
        if comm.relay is not None:
            comm.relay(ins, outs, sems)
        comm.finish(ins, outs, sems)

    return pl.pallas_call(
        body, name=name, out_shape=list(comm.out_shapes), in_specs=[ANY] * n_in, out_specs=[ANY] * n_out,
        scratch_shapes=list(comm.sems),
    )(*comm.inputs)


def _place():
    return lax.axis_index("x"), lax.axis_index("y"), lax.axis_index("c")


def _gather_comm(arrs):
    n = len(arrs)

    def plan(ins, outs, sems):
        send_sems, recv_sems, local_sems = sems
        x, y, c = _place()
        me, sibling = (x, y, c), (x, y, 1 - c)
        chips = [(1 - x, y), (x, 1 - y), (1 - x, 1 - y)]

        def block(k, p):
            return outs[k].at[4 * p[0] + 2 * p[1] + p[2]]

        def copy(k, s, blk, to, src=None):
            return pltpu.make_async_remote_copy(
                src_ref=block(k, blk) if src is None else src, dst_ref=block(k, blk),
                send_sem=send_sems.at[7 * k + s], recv_sem=recv_sems.at[7 * k + s],
                device_id=to, device_id_type=MESH)

        mine = [pltpu.make_async_copy(ins[k], block(k, me), local_sems.at[k]) for k in range(n)]
        first = []
        for k in range(n):
            first.append(copy(k, 0, me, sibling, src=ins[k]))
            for j, chip in enumerate(chips):
                first.append(copy(k, 1 + j, me, (*chip, c), src=ins[k]))
        return me, sibling, chips, c, copy, mine, first

    def start(ins, outs, sems):
        *_, mine, first = plan(ins, outs, sems)
        for cp in mine + first:
            cp.start()

    def relay(ins, outs, sems):
        me, sibling, chips, c, copy, _, _ = plan(ins, outs, sems)
        for j, chip in enumerate(chips):
            for k in range(n):
                copy(k, 1 + j, (*chip, c), me).wait_recv()
                copy(k, 4 + j, (*chip, c), sibling).start()

    def finish(ins, outs, sems):
        me, sibling, chips, c, copy, mine, first = plan(ins, outs, sems)
        passed = [copy(k, 4 + j, (*chip, c), sibling) for j, chip in enumerate(chips) for k in range(n)]
        for k in range(n):
            copy(k, 0, sibling, me).wait_recv()
        for j, chip in enumerate(chips):
            for k in range(n):
                copy(k, 4 + j, (*chip, 1 - c), me).wait_recv()
        for cp in first + passed:
            cp.wait_send()
        for cp in mine:
            cp.wait()

    return _Comm(list(arrs), [jax.ShapeDtypeStruct((N_DEV,) + a.shape, a.dtype) for a in arrs],
                 [pltpu.SemaphoreType.DMA((7 * n,)), pltpu.SemaphoreType.DMA((7 * n,)),
                  pltpu.SemaphoreType.DMA((n,))], start, finish, relay)


def _scatter_comm(blocks):
    n = len(blocks)

    def plan(ins, outs, sems, arrivals):
        send_sems, recv_sems, local_sems = sems
        x, y, c = _place()
        me = 4 * x + 2 * y + c
        local = [pltpu.make_async_copy(ins[k].at[me], outs[k].at[me], local_sems.at[k]) for k in range(n)]
        sends, recvs = [], []
        for k in range(n):
            for mask in range(1, N_DEV):
                px = 1 - x if mask & 4 else x
                py = 1 - y if mask & 2 else y
                pc = 1 - c if mask & 1 else c
                peer = 4 * px + 2 * py + pc
                sem = 7 * k + mask - 1
                both = dict(send_sem=send_sems.at[sem], recv_sem=recv_sems.at[sem], device_id=(px, py, pc),
                            device_id_type=MESH)
                sends.append(pltpu.make_async_remote_copy(src_ref=ins[k].at[peer], dst_ref=outs[k].at[me], **both))
                if arrivals:
                    recvs.append(pltpu.make_async_remote_copy(src_ref=ins[k].at[me], dst_ref=outs[k].at[peer],
                                                              **both))
        return local, sends, recvs

    def start(ins, outs, sems):
        local, sends, _ = plan(ins, outs, sems, arrivals=False)
        for cp in local + sends:
            cp.start()

    def finish(ins, outs, sems):
        local, sends, recvs = plan(ins, outs, sems, arrivals=True)
        for cp in recvs:
            cp.wait_recv()
        for cp in sends:
            cp.wait_send()
        for cp in local:
            cp.wait()

    return _Comm(list(blocks), [jax.ShapeDtypeStruct(b.shape, b.dtype) for b in blocks],
                 [pltpu.SemaphoreType.DMA((7 * n,)), pltpu.SemaphoreType.DMA((7 * n,)),
                  pltpu.SemaphoreType.DMA((n,))], start, finish)


_DIMS = {"nn": (((1,), (0,)), ((), ())), "nt": (((1,), (1,)), ((), ())), "tn": (((0,), (0,)), ((), ()))}


class _Epilogue:
    def __init__(self, args, in_specs, out_shapes, out_specs, fn, keep_product):
        self.args, self.in_specs, self.out_shapes, self.out_specs = args, in_specs, out_shapes, out_specs
        self.fn, self.keep_product = fn, keep_product


def _row_tile(rows, cols):
    return pl.BlockSpec((rows, cols), lambda i, j: (i, 0))


def _whole(shape):
    zeros = (0,) * len(shape)
    return pl.BlockSpec(shape, lambda i, j: zeros)


def _mm(a, b, mode, out_dtype, name, bias=None, tm=512, tn=512, comm=None, cols=None, epilogue=None):
    pieces = a if isinstance(a, (list, tuple)) else [a]
    assert all(p.dtype == BF16 for p in pieces) and b.dtype == BF16
    a = pieces[0]
    if mode == "tn":
        k_dim, m_dim = a.shape
    else:
        m_dim, k_dim = a.shape
    n_dim = b.shape[0] if mode == "nt" else b.shape[1]
    col0 = 0
    if cols is not None:
        assert mode == "nn" and cols[0] % tn == 0 and cols[1] % tn == 0
        col0, n_dim = cols[0] // tn, cols[1]
    tm, tn = _pick(m_dim, tm), _pick(n_dim, tn)
    a_specs = [pl.BlockSpec((k_dim, tm), lambda i, j: (0, i)) if mode == "tn"
               else pl.BlockSpec((tm, k_dim), lambda i, j: (i, 0))] * len(pieces)
    if mode == "nt":
        b_specs = [pl.BlockSpec((tn, k_dim), lambda i, j, p=p: (j, p)) for p in range(len(pieces))]
    else:
        assert len(pieces) == 1
        b_specs = [pl.BlockSpec((k_dim, tn), lambda i, j: (0, j + col0))]
    in_specs = a_specs + b_specs
    args = list(pieces) + [b] * len(pieces)
    if bias is not None:
        in_specs.append(pl.BlockSpec((1, tn), lambda i, j: (0, j + col0)))
        args.append(bias)
    dims = _DIMS[mode]
    n_pieces = len(pieces)
    n_own = len(args)
    keep = epilogue is None or epilogue.keep_product
    out_specs = [pl.BlockSpec((tm, tn), lambda i, j: (i, j))] if keep else []
    out_shape = [jax.ShapeDtypeStruct((m_dim, n_dim), out_dtype)] if keep else []
    if epilogue is not None:
        assert tn == n_dim
        in_specs, args = in_specs + list(epilogue.in_specs), args + list(epilogue.args)
        out_specs, out_shape = out_specs + list(epilogue.out_specs), out_shape + list(epilogue.out_shapes)

    def body(ins, outs, scratch):
        total = lax.dot_general(ins[0][...], ins[n_pieces][...], dims, preferred_element_type=F32)
        for p in range(1, n_pieces):
            total = total + lax.dot_general(ins[p][...], ins[n_pieces + p][...], dims, preferred_element_type=F32)
        if bias is not None:
            total = total + ins[2 * n_pieces][...]
        if keep:
            outs[0][...] = total.astype(out_dtype)
        if epilogue is not None:
            epilogue.fn(total, pl.program_id(0) == 0, ins[n_own:], outs[1:] if keep else outs)

    outs, extra = _host_call(body, name, grid=(m_dim // tm, n_dim // tn), in_specs=in_specs, out_specs=out_specs,
                             out_shape=out_shape, scratch_shapes=[], args=args, comm=comm)
    product = outs[0] if keep else None
    if comm is None and epilogue is None:
        return product
    return product, outs[1:] if keep else outs, extra


def _mm_tn_blocks(a, bs, n_dev_cols, name, tm=512, k_steps=1):
    k_dim, m_dim = a.shape
    n = n_dev_cols
    pair = 2 * n
    assert pair % 128 == 0 and all(b.shape[1] % pair == 0 for b in bs) and k_dim % k_steps == 0
    counts = [b.shape[1] // pair for b in bs]
    firsts = [sum(counts[:q]) for q in range(len(bs))]
    assert sum(counts) == N_DEV // 2
    tm, tk = _pick(m_dim, tm), k_dim // k_steps

    def b_spec(first, count):
        return pl.BlockSpec((tk, pair), lambda i, j, k: (k, jnp.clip(j - first, 0, count - 1)))

    def body(ins, outs, scratch):
        a_ref, b_refs, o_ref, s_ref = ins[0], ins[1:], outs[0], scratch[0]
        j, k = pl.program_id(1), pl.program_id(2)
        for b_ref, first, count in zip(b_refs, firsts, counts):
            @pl.when(jnp.logical_and(j >= first, j < first + count))
            def _(b_ref=b_ref):
                part = lax.dot_general(a_ref[...], b_ref[...], _DIMS["tn"], preferred_element_type=F32)
                if k_steps == 1:
                    s_ref[...] = part
                else:
                    @pl.when(k == 0)
                    def _():
                        s_ref[...] = part

                    @pl.when(k > 0)
                    def _():
                        s_ref[...] += part

        @pl.when(k == k_steps - 1)
        def _():
            o_ref[0] = s_ref[:, 0:n].astype(BF16)
            o_ref[1] = s_ref[:, n:pair].astype(BF16)

    (out,), _ = _host_call(
        body, name, grid=(m_dim // tm, N_DEV // 2, k_steps),
        in_specs=[pl.BlockSpec((tk, tm), lambda i, j, k: (k, i))] + [b_spec(f, c) for f, c in zip(firsts, counts)],
        out_specs=[pl.BlockSpec((2, tm, n), lambda i, j, k: (j, i, 0))],
        out_shape=[jax.ShapeDtypeStruct((N_DEV, m_dim, n), BF16)],
        scratch_shapes=[pltpu.VMEM((tm, pair), F32)], args=[a] + list(bs))
    return out


def _adam_math(w, g, m, v):
    m = ADAM_B1 * m + (1.0 - ADAM_B1) * g
    v = ADAM_B2 * v + (1.0 - ADAM_B2) * (g * g)
    m_hat = m / (1.0 - ADAM_B1 ** ADAM_STEP)
    v_hat = v / (1.0 - ADAM_B2 ** ADAM_STEP)
    delta = -ADAM_LR * (m_hat / (jnp.sqrt(v_hat) + ADAM_EPS) + ADAM_WD * w)
    return delta, m, v


def _adamw(w, m, v, name, g=None, parts=None):
    rows, cols = w.shape
    tr = rows
    if rows * cols * 4 > ADAMW_BLOCK_BYTES:
        tr = max(t for t in range(16, rows, 16) if rows % t == 0 and t * cols * 4 <= ADAMW_BLOCK_BYTES)

    def body(w_ref, m_ref, v_ref, g_ref, go_ref, d_ref, mo_ref, vo_ref):
        if parts is None:
            grad = g_ref[...]
        else:
            grad = g_ref[0].astype(F32)
            for d in range(1, N_DEV):
                grad = grad + g_ref[d].astype(F32)
        delta, m_new, v_new = _adam_math(w_ref[...], grad, m_ref[...], v_ref[...])
        go_ref[...] = grad
        d_ref[...] = delta
        mo_ref[...] = m_new
        vo_ref[...] = v_new

    spec = _tile(tr, cols)
    g_spec = spec if parts is None else pl.BlockSpec((N_DEV, tr, cols), lambda i: (0, i, 0))
    shape = jax.ShapeDtypeStruct((rows, cols), F32)
    return pl.pallas_call(
        body, name=name, out_shape=[shape] * 4, grid=(rows // tr,),
        in_specs=[spec, spec, spec, g_spec], out_specs=[spec] * 4, compiler_params=_cparams(1),
    )(w, m, v, g if parts is None else parts)


def _pack_grads(small_x, small_m, small_f, small_g, small_a, small_c, dbv, dbg, dwv, dwg, dw_conv):
    pieces = [
        (small_x, 2, D_MODEL), (small_x, 1, D_MODEL), (small_m, 4, D_MODEL), (small_m, 2, D_MODEL),
        (small_m, 1, D_MODEL), (small_f, 1, D_MODEL),
        (small_x, 0, D_MODEL), (small_m, 3, D_MODEL),
        (small_a, 0, D_ATTN), (small_a, 1, D_ATTN), (small_a, 2, D_ATTN), (small_c, 3, D_CONV),
        (small_c, 4, D_CONV), (small_g, 0, D_MODEL), (small_g, 1, D_MODEL),
        (small_c, 0, D_CONV), (small_c, 1, D_CONV), (small_c, 2, D_CONV),
        (small_g, 2, D_MODEL), (small_m, 0, D_MODEL), (small_f, 0, D_MODEL),
        (dbv, 0, D_FF), (dbg, 0, D_FF),
    ]
    pieces += [(dw_conv, j, D_CONV) for j in range(CONV_K)]
    pieces += [(src, tap, D_FF) for tap in range(3) for src in (dwv, dwg)]
    sources = [small_x, small_m, small_f, small_g, small_a, small_c, dbv, dbg, dwv, dwg, dw_conv]
    assert sum(width for _, _, width in pieces) == PACKED_TOTAL

    def body(*refs):
        o_ref = refs[-1]
        ref_of = {id(src): ref for src, ref in zip(sources, refs)}
        off = 0
        for src, row, width in pieces:
            o_ref[:, off:off + width] = ref_of[id(src)][row:row + 1, :]
            off += width

    return pl.pallas_call(body, name="pack_grads", out_shape=jax.ShapeDtypeStruct((1, PACKED_TOTAL), F32))(*sources)


def _small_adamw(gathered, gathered_rel, gathered_loss, weights, mom_m, mom_v):
    vec_names = [name for name, _ in SMALL]
    states = []
    for name in vec_names + ["rel_bias"]:
        states += [weights[name], mom_m[name], mom_v[name]]
    states = [a.reshape(a.shape[1:]) if a.ndim == 3 else a for a in states]
    n_state = len(states)

    def body(*refs):
        g_ref, rel_ref, loss_ref = refs[0], refs[1], refs[2]
        state_refs, out_refs = refs[3:3 + n_state], refs[3 + n_state:]
        total = g_ref[0:1, :]
        rel = rel_ref[0]
        loss = loss_ref[0]
        for d in range(1, N_DEV):
            total = total + g_ref[d:d + 1, :]
            rel = rel + rel_ref[d]
            loss = loss + loss_ref[d]
        off = 0
        for n, (name, width) in enumerate(SMALL):
            grad = total[:, off:off + width]
            w_ref, m_ref, v_ref = state_refs[3 * n:3 * n + 3]
            for ref, val in zip(out_refs[4 * n:4 * n + 4], (grad,) + _adam_math(w_ref[...], grad, m_ref[...], v_ref[...])):
                ref[...] = val
            off += width
        n = len(SMALL)
        w_ref, m_ref, v_ref = state_refs[3 * n:3 * n + 3]
        for ref, val in zip(out_refs[4 * n:4 * n + 4], (rel,) + _adam_math(w_ref[...], rel, m_ref[...], v_ref[...])):
            ref[...] = val
        dwc_ref, dwf_ref, loss_out = out_refs[4 * n + 4:]
        loss_out[...] = 0.5 * loss
        dwc_ref[...] = jnp.zeros_like(dwc_ref)
        dwf_ref[...] = jnp.zeros_like(dwf_ref)
        for j in range(CONV_K):
            dwc_ref[j:j + 1, :] = total[:, off:off + D_CONV]
            off += D_CONV
        for tap in range(3):
            dwf_ref[tap:tap + 1, :] = total[:, off:off + 2 * D_FF]
            off += 2 * D_FF

    out_shape = []
    for k in range(n_state // 3):
        out_shape += [jax.ShapeDtypeStruct(states[3 * k].shape, F32)] * 4
    out_shape += [jax.ShapeDtypeStruct((CONV_HALO, D_CONV), F32), jax.ShapeDtypeStruct((8, 2 * D_FF), F32),
                  jax.ShapeDtypeStruct((1, 128), F32)]
    res = pl.pallas_call(
        body, name="small_adamw", out_shape=out_shape,
        compiler_params=pltpu.CompilerParams(vmem_limit_bytes=VMEM_LIMIT_BYTES),
    )(gathered, gathered_rel, gathered_loss, *states)
    updates = {name: tuple(res[4 * n:4 * n + 4]) for n, name in enumerate(vec_names + ["rel_bias"])}
    return updates, res[-3], res[-2], res[-1]


def _silu_vec(c):
    def body(c_ref, o_ref):
        v = c_ref[...]
        o_ref[...] = v * _sig(v)

    return pl.pallas_call(body, name="silu_c", out_shape=jax.ShapeDtypeStruct(c.shape, F32))(c)


def _ada_fwd(c_all, w_shard):
    def body(c_ref, w_ref, o_ref):
        o_ref[...] = jnp.dot(c_ref[...], w_ref[...], precision=HIGHEST, preferred_element_type=F32)

    return pl.pallas_call(
        body, name="ada_fwd", out_shape=jax.ShapeDtypeStruct((N_DEV, w_shard.shape[1]), F32),
        compiler_params=pltpu.CompilerParams(vmem_limit_bytes=VMEM_LIMIT_BYTES),
    )(c_all, w_shard)


def _ada_grad(c_all, dmod_shard):
    def body(c_ref, d_ref, o_ref):
        o_ref[...] = lax.dot_general(c_ref[...], d_ref[...], _DIMS["tn"], precision=HIGHEST,
                                     preferred_element_type=F32)

    return pl.pallas_call(
        body, name="ada_grad", out_shape=jax.ShapeDtypeStruct((D_MODEL, dmod_shard.shape[1]), F32),
        compiler_params=pltpu.CompilerParams(vmem_limit_bytes=VMEM_LIMIT_BYTES),
    )(c_all, dmod_shard)


ROWS = 256


def _rms(v):
    r = lax.rsqrt(jnp.mean(v * v, axis=-1, keepdims=True) + EPS)
    return v * r, r


def _rms_bwd(dxn, xn, r):
    return r * (dxn - xn * jnp.mean(dxn * xn, axis=-1, keepdims=True))


def _colsum(v):
    return jnp.sum(v, axis=0, keepdims=True)


def _pre_mix(x, mod6, g1):
    seq = x.shape[0]

    def body(x_ref, mod_ref, g_ref, h_ref):
        xn, _ = _rms(x_ref[...])
        y = xn * g_ref[...]
        h_ref[...] = (y * (1.0 + mod_ref[SC_M:SC_M + 1, :]) + mod_ref[SH_M:SH_M + 1, :]).astype(BF16)

    return pl.pallas_call(
        body, name="pre_mix", out_shape=jax.ShapeDtypeStruct((seq, D_MODEL), BF16), grid=(seq // ROWS,),
        in_specs=[_tile(ROWS, D_MODEL), _full((6, D_MODEL)), _full((1, D_MODEL))],
        out_specs=_tile(ROWS, D_MODEL), compiler_params=_cparams(1),
    )(x, mod6, g1)


def _post_mix_pre_ffn(x, mod6, g2, g3, rows):
    seq = x.shape[0]

    def fn(y, first, ins, outs):
        x_ref, mod_ref, g2_ref, g3_ref = ins
        x1_ref, h_ref = outs
        yn, _ = _rms(y)
        x1 = x_ref[...] + mod_ref[GT_M:GT_M + 1, :] * (yn * g2_ref[...])
        x1_ref[...] = x1
        xn, _ = _rms(x1)
        y3 = xn * g3_ref[...]
        h_ref[...] = (y3 * (1.0 + mod_ref[SC_F:SC_F + 1, :]) + mod_ref[SH_F:SH_F + 1, :]).astype(BF16)

    return _Epilogue(
        [x, mod6, g2, g3], [_row_tile(rows, D_MODEL), _whole((6, D_MODEL)), _whole((1, D_MODEL)), _whole((1, D_MODEL))],
        [jax.ShapeDtypeStruct((seq, D_MODEL), F32), jax.ShapeDtypeStruct((seq, D_MODEL), BF16)],
        [_row_tile(rows, D_MODEL), _row_tile(rows, D_MODEL)], fn, keep_product=True)


def _final(x1, target, mod6, g4, rows):
    seq = x1.shape[0]

    def fn(y, first, ins, outs):
        x1_ref, t_ref, mod_ref, g_ref = ins
        loss_ref, dout_ref, dyf_ref, small_ref = outs

        @pl.when(first)
        def _():
            loss_ref[...] = jnp.zeros_like(loss_ref)
            small_ref[...] = jnp.zeros_like(small_ref)

        gt = mod_ref[GT_F:GT_F + 1, :]
        g4v = g_ref[...]
        yn, r = _rms(y)
        out = x1_ref[...] + gt * (yn * g4v)
        err = out - t_ref[...]
        loss_ref[...] += jnp.sum(jnp.mean(err * err, axis=-1, keepdims=True))
        dout = err * (1.0 / D_MODEL)
        dout_ref[...] = dout
        small_ref[0:1, :] += _colsum(dout * gt * yn)
        small_ref[1:2, :] += _colsum(dout * (yn * g4v))
        dyf_ref[...] = _rms_bwd(dout * gt * g4v, yn, r).astype(BF16)

    return _Epilogue(
        [x1, target, mod6, g4],
        [_row_tile(rows, D_MODEL), _row_tile(rows, D_MODEL), _whole((6, D_MODEL)), _whole((1, D_MODEL))],
        [jax.ShapeDtypeStruct((1, 128), F32), jax.ShapeDtypeStruct((seq, D_MODEL), F32),
         jax.ShapeDtypeStruct((seq, D_MODEL), BF16), jax.ShapeDtypeStruct((8, D_MODEL), F32)],
        [_whole((1, 128)), _row_tile(rows, D_MODEL), _row_tile(rows, D_MODEL), _whole((8, D_MODEL))],
        fn, keep_product=False)


def _mid_bwd(x1, dout, ymix, mod6, g3, g2, rows):
    seq = x1.shape[0]

    def fn(dh, first, ins, outs):
        x1_ref, dout_ref, y_ref, mod_ref, g3_ref, g2_ref = ins
        dx1_ref, dy_ref, small_ref = outs

        @pl.when(first)
        def _():
            small_ref[...] = jnp.zeros_like(small_ref)

        g3v, g2v = g3_ref[...], g2_ref[...]
        xn, r3 = _rms(x1_ref[...])
        y3 = xn * g3v
        dy3 = dh * (1.0 + mod_ref[SC_F:SC_F + 1, :])
        small_ref[0:1, :] += _colsum(dy3 * xn)
        small_ref[1:2, :] += _colsum(dh * y3)
        small_ref[2:3, :] += _colsum(dh)
        dx1 = dout_ref[...] + _rms_bwd(dy3 * g3v, xn, r3)
        dx1_ref[...] = dx1
        gt = mod_ref[GT_M:GT_M + 1, :]
        yn, r2 = _rms(y_ref[...])
        small_ref[3:4, :] += _colsum(dx1 * gt * yn)
        small_ref[4:5, :] += _colsum(dx1 * (yn * g2v))
        dy_ref[...] = _rms_bwd(dx1 * gt * g2v, yn, r2).astype(BF16)

    return _Epilogue(
        [x1, dout, ymix, mod6, g3, g2],
        [_row_tile(rows, D_MODEL)] * 3 + [_whole((6, D_MODEL)), _whole((1, D_MODEL)), _whole((1, D_MODEL))],
        [jax.ShapeDtypeStruct((seq, D_MODEL), F32), jax.ShapeDtypeStruct((seq, D_MODEL), BF16),
         jax.ShapeDtypeStruct((8, D_MODEL), F32)],
        [_row_tile(rows, D_MODEL), _row_tile(rows, D_MODEL), _whole((8, D_MODEL))], fn, keep_product=False)


def _pre_mix_bwd(x, dx1, mod6, g1, rows):
    seq = x.shape[0]

    def fn(dh, first, ins, outs):
        x_ref, dx1_ref, mod_ref, g_ref = ins
        dx_ref, small_ref = outs

        @pl.when(first)
        def _():
            small_ref[...] = jnp.zeros_like(small_ref)

        g1v = g_ref[...]
        xn, r = _rms(x_ref[...])
        dy = dh * (1.0 + mod_ref[SC_M:SC_M + 1, :])
        small_ref[0:1, :] += _colsum(dy * xn)
        small_ref[1:2, :] += _colsum(dh * (xn * g1v))
        small_ref[2:3, :] += _colsum(dh)
        dx_ref[...] = dx1_ref[...] + _rms_bwd(dy * g1v, xn, r)

    return _Epilogue(
        [x, dx1, mod6, g1],
        [_row_tile(rows, D_MODEL), _row_tile(rows, D_MODEL), _whole((6, D_MODEL)), _whole((1, D_MODEL))],
        [jax.ShapeDtypeStruct((seq, D_MODEL), F32), jax.ShapeDtypeStruct((8, D_MODEL), F32)],
        [_row_tile(rows, D_MODEL), _whole((8, D_MODEL))], fn, keep_product=False)


def _toeplitz_onehot(shape, offset_axis, top):
    m = lax.broadcasted_iota(jnp.int32, shape, offset_axis)
    i = lax.broadcasted_iota(jnp.int32, shape, 1 - offset_axis)
    return (i == jnp.clip(top - m, -MAX_REL, MAX_REL) + MAX_REL).astype(F32)


def _bias_table(rel_bias):
    width = GROUP_Q + GROUP_K

    def body(rb_ref, o_ref, t_ref):
        t_ref[...] = jnp.dot(rb_ref[...], _toeplitz_onehot((N_REL, width), 1, GROUP_K - 1), precision=HIGHEST,
                             preferred_element_type=F32)
        lane = lax.broadcasted_iota(jnp.int32, (N_HEADS, GROUP_K), 1)
        for r in range(GROUP_Q):
            first_key = (r // CHUNK) * CHUNK
            band = jnp.logical_and(lane >= first_key, lane < first_key + BAND)
            o_ref[r] = jnp.where(band, t_ref[:, GROUP_Q - 1 - r:GROUP_Q - 1 - r + GROUP_K], NEG_INF)

    return pl.pallas_call(
        body, name="bias_table", out_shape=jax.ShapeDtypeStruct((GROUP_Q, N_HEADS, GROUP_K), F32),
        scratch_shapes=[pltpu.VMEM((N_HEADS, width), F32)],
    )(rel_bias)


def _bias_grad(dbias_q):
    def body(d_ref, o_ref, t_ref):
        t_ref[...] = jnp.zeros_like(t_ref)
        for qi in range(CHUNK):
            t_ref[:, CHUNK - 1 - qi:CHUNK - 1 - qi + BAND] += d_ref[qi]
        o_ref[...] = jnp.dot(t_ref[...], _toeplitz_onehot((TOEPLITZ, N_REL), 0, BAND - 1), precision=HIGHEST,
                             preferred_element_type=F32)

    return pl.pallas_call(
        body, name="bias_grad", out_shape=jax.ShapeDtypeStruct((N_HEADS, N_REL), F32),
        scratch_shapes=[pltpu.VMEM((N_HEADS, TOEPLITZ), F32)],
    )(dbias_q)


def _load_resident(pairs, sems):
    copies = [pltpu.make_async_copy(src, dst, sems.at[n]) for n, (src, dst) in enumerate(pairs)]
    for cp in copies:
        cp.start()
    for cp in copies:
        cp.wait()


def _softmax_rows(s_ref, t_ref, valid, rows):
    s = s_ref[rows, :] * (HEAD_DIM ** -0.5) + t_ref[rows, :]
    s = jnp.where(valid, s, NEG_INF)
    e = jnp.exp(s - jnp.max(s, axis=-1, keepdims=True))
    return e / jnp.sum(e, axis=-1, keepdims=True)


def _valid_keys(g):
    kj = lax.broadcasted_iota(jnp.int32, (SOFTMAX_ROWS, GROUP_K), 1)
    return kj >= PAD_ROWS - g * GROUP_Q


def _attn_fwd(qkv, kpad, vpad, table, comm=None):
    seq = qkv.shape[0]

    def body(ins, outs, scratch):
        q_ref, k_hbm, v_hbm, t_hbm = ins
        (o_ref,) = outs
        k_ref, v_ref, t_ref, s_ref, p_ref, sems = scratch
        g = pl.program_id(0)

        @pl.when(g == 0)
        def _():
            _load_resident(((k_hbm, k_ref), (v_hbm, v_ref), (t_hbm, t_ref)), sems)

        window = pl.ds(pl.multiple_of(g * GROUP_Q, GROUP_Q), GROUP_K)
        valid = _valid_keys(g)
        for h in range(N_HEADS):
            cols = slice(h * HEAD_DIM, (h + 1) * HEAD_DIM)
            buf = h % 2
            s_ref[buf] = lax.dot_general(q_ref[:, cols], k_ref[window, cols], _DIMS["nt"],
                                         preferred_element_type=F32)
            for r in range(GROUP_Q // SOFTMAX_ROWS):
                rows = slice(r * SOFTMAX_ROWS, (r + 1) * SOFTMAX_ROWS)
                p_ref[buf, rows, :] = _softmax_rows(s_ref.at[buf], t_ref.at[h], valid, rows).astype(BF16)
            o_ref[:, cols] = jnp.dot(p_ref[buf], v_ref[window, cols], preferred_element_type=F32).astype(BF16)

    (ao,), extra = _host_call(
        body, "attn_fwd", grid=(seq // GROUP_Q,),
        in_specs=[_tile(GROUP_Q, D_ATTN), ANY, ANY, ANY], out_specs=[_tile(GROUP_Q, D_ATTN)],
        out_shape=[jax.ShapeDtypeStruct((seq, D_ATTN), BF16)],
        scratch_shapes=[pltpu.VMEM(kpad.shape, BF16), pltpu.VMEM(vpad.shape, BF16), pltpu.VMEM(table.shape, F32),
                        pltpu.VMEM((2, GROUP_Q, GROUP_K), F32), pltpu.VMEM((2, GROUP_Q, GROUP_K), BF16),
                        pltpu.SemaphoreType.DMA((3,))],
        args=[qkv, kpad, vpad, table], comm=comm)
    return ao, extra


def _attn_bwd(qkv, kpad, vpad, table, dao, comm=None):
    seq = qkv.shape[0]
    n_groups = seq // GROUP_Q
    fold_w = GROUP_K + (GROUP - 1) * CHUNK

    def body(ins, outs, scratch):
        q_ref, do_ref, k_hbm, v_hbm, t_hbm = ins
        dq_ref, dkt_hbm, dvt_hbm, db_ref, cs_ref = outs
        k_ref, v_ref, t_ref, db_acc, dkt_acc, dvt_acc, s_ref, dp_ref, p_ref, ds_ref, sems = scratch
        g = pl.program_id(0)

        @pl.when(g == 0)
        def _():
            _load_resident(((k_hbm, k_ref), (v_hbm, v_ref), (t_hbm, t_ref)), sems)
            db_acc[...] = jnp.zeros_like(db_acc)
            dkt_acc[...] = jnp.zeros_like(dkt_acc)
            dvt_acc[...] = jnp.zeros_like(dvt_acc)
            cs_ref[...] = jnp.zeros_like(cs_ref)

        window = pl.ds(pl.multiple_of(g * GROUP_Q, GROUP_Q), GROUP_K)
        valid = _valid_keys(g)
        for h in range(N_HEADS):
            cols = slice(h * HEAD_DIM, (h + 1) * HEAD_DIM)
            buf = h % 2
            qh, doh = q_ref[:, cols], do_ref[:, cols]
            kh, vh = k_ref[window, cols], v_ref[window, cols]
            s_ref[buf] = lax.dot_general(qh, kh, _DIMS["nt"], preferred_element_type=F32)
            dp_ref[buf] = lax.dot_general(doh, vh, _DIMS["nt"], preferred_element_type=F32)
            for r in range(GROUP_Q // SOFTMAX_ROWS):
                rows = slice(r * SOFTMAX_ROWS, (r + 1) * SOFTMAX_ROWS)
                p = _softmax_rows(s_ref.at[buf], t_ref.at[h], valid, rows)
                dp = dp_ref[buf, rows, :]
                ds = p * (dp - jnp.sum(dp * p, axis=-1, keepdims=True))
                chunk = (r * SOFTMAX_ROWS) // CHUNK
                shift = (GROUP - 1 - chunk) * CHUNK
                local = slice(r * SOFTMAX_ROWS - chunk * CHUNK, (r + 1) * SOFTMAX_ROWS - chunk * CHUNK)
                db_acc[h, local, shift:shift + GROUP_K] += ds
                p_ref[buf, rows, :] = p.astype(BF16)
                ds_ref[buf, rows, :] = (ds * (HEAD_DIM ** -0.5)).astype(BF16)
            dq_ref[:, cols] = jnp.dot(ds_ref[buf], kh, preferred_element_type=F32).astype(BF16)
            dkt_acc[cols, window] += lax.dot_general(qh, ds_ref[buf], _DIMS["tn"], preferred_element_type=F32)
            dvt_acc[cols, window] += lax.dot_general(doh, p_ref[buf], _DIMS["tn"], preferred_element_type=F32)
        cs_ref[0:1, :] += _colsum(dq_ref[...].astype(F32))

        @pl.when(g == n_groups - 1)
        def _():
            lo = (GROUP - 1) * CHUNK
            for h in range(N_HEADS):
                db_ref[h] = db_acc[h, :, lo:lo + BAND]
            inside = pl.ds(PAD_ROWS, seq)
            on_diagonal = (lax.broadcasted_iota(jnp.int32, (D_ATTN, D_ATTN), 0)
                           == lax.broadcasted_iota(jnp.int32, (D_ATTN, D_ATTN), 1))
            for row, acc in ((1, dkt_acc), (2, dvt_acc)):
                column = jnp.sum(acc[:, inside], axis=1, keepdims=True)
                cs_ref[row:row + 1, :] = _colsum(jnp.where(on_diagonal, column, 0.0))
            out_k = pltpu.make_async_copy(dkt_acc.at[:, inside], dkt_hbm, sems.at[0])
            out_v = pltpu.make_async_copy(dvt_acc.at[:, inside], dvt_hbm, sems.at[1])
            out_k.start()
            out_v.start()
            out_k.wait()
            out_v.wait()

    t_shape = (D_ATTN, seq + PAD_ROWS)
    outs, extra = _host_call(
        body, "attn_bwd", grid=(n_groups,),
        in_specs=[_tile(GROUP_Q, D_ATTN), _tile(GROUP_Q, D_ATTN), ANY, ANY, ANY],
        out_specs=[_tile(GROUP_Q, D_ATTN), ANY, ANY, _full((N_HEADS, CHUNK, BAND)), _full((8, D_ATTN))],
        out_shape=[jax.ShapeDtypeStruct((seq, D_ATTN), BF16), jax.ShapeDtypeStruct((D_ATTN, seq), F32),
                   jax.ShapeDtypeStruct((D_ATTN, seq), F32), jax.ShapeDtypeStruct((N_HEADS, CHUNK, BAND), F32),
                   jax.ShapeDtypeStruct((8, D_ATTN), F32)],
        scratch_shapes=[pltpu.VMEM(kpad.shape, BF16), pltpu.VMEM(vpad.shape, BF16), pltpu.VMEM(table.shape, F32),
                        pltpu.VMEM((N_HEADS, CHUNK, fold_w), F32), pltpu.VMEM(t_shape, F32),
                        pltpu.VMEM(t_shape, F32), pltpu.VMEM((2, GROUP_Q, GROUP_K), F32),
                        pltpu.VMEM((2, GROUP_Q, GROUP_K), F32), pltpu.VMEM((2, GROUP_Q, GROUP_K), BF16),
                        pltpu.VMEM((2, GROUP_Q, GROUP_K), BF16), pltpu.SemaphoreType.DMA((3,))],
        args=[qkv, dao, kpad, vpad, table], comm=comm)
    return outs, extra


def _assemble_dz(dq, dkt, dvt, dglu_a, dglu_b, dga, dgb):
    seq = dq.shape[0]
    rows = 512
    transposed = pl.BlockSpec((D_ATTN, rows), lambda i: (0, i))

    def body(dq_ref, dkt_ref, dvt_ref, da_ref, db_ref, dga_ref, dgb_ref, o_ref):
        o_ref[:, 0:D_ATTN] = dq_ref[...]
        o_ref[:, D_ATTN:2 * D_ATTN] = dkt_ref[...].T.astype(BF16)
        o_ref[:, 2 * D_ATTN:3 * D_ATTN] = dvt_ref[...].T.astype(BF16)
        off = 3 * D_ATTN
        for ref in (da_ref, db_ref, dga_ref, dgb_ref):
            width = ref.shape[1]
            o_ref[:, off:off + width] = ref[...]
            off += width

    width = 3 * D_ATTN + 2 * D_CONV + 2 * D_MODEL
    return pl.pallas_call(
        body, name="assemble_dz", out_shape=jax.ShapeDtypeStruct((seq, width), BF16), grid=(seq // rows,),
        in_specs=[_tile(rows, D_ATTN), transposed, transposed, _tile(rows, D_CONV), _tile(rows, D_CONV),
                  _tile(rows, D_MODEL), _tile(rows, D_MODEL)],
        out_specs=_tile(rows, width), compiler_params=_cparams(1),
    )(dq, dkt, dvt, dglu_a, dglu_b, dga, dgb)


CONV_ROWS = 256


def _ln_silu(u1, g, b):
    mu = jnp.mean(u1, axis=-1, keepdims=True)
    xc = u1 - mu
    rs = lax.rsqrt(jnp.mean(xc * xc, axis=-1, keepdims=True) + EPS)
    xhat = xc * rs
    u2 = xhat * g + b
    return xhat, rs, u2


def _glu_into(s_ref, a_ref, b_ref, ah_ref, bh_ref, first):
    halo = ah_ref[...] * _sig(bh_ref[...])
    s_ref[0:CONV_HALO, :] = jnp.where(first, 0.0, halo)
    s_ref[CONV_HALO:CONV_HALO + CONV_ROWS, :] = a_ref[...] * _sig(b_ref[...])


CONV_LANES = 128
CONV_TILES = CONV_ROWS // 8


def _lag_weights(w_ref, lanes):
    return {e: jnp.broadcast_to(w_ref[CONV_K - 1 - e:CONV_K - e, lanes], (8, CONV_LANES)) for e in range(CONV_K)}


def _class_sums(w, tiles, k):
    total = None
    for a, tile in enumerate(tiles):
        if 8 * a + k < CONV_K:
            term = w[8 * a + k] * tile
            total = term if total is None else total + term
    return total


def _conv_back(src_ref, first_tile, w, lanes, row_id, emit):
    before = None
    for m in range(-1, CONV_TILES):
        tiles = [src_ref[8 * (first_tile + m - a):8 * (first_tile + m - a) + 8, lanes] for a in range(4)]
        rolled = [None] + [pltpu.roll(_class_sums(w, tiles, k), k, 0) for k in range(1, 8)]
        if m >= 0:
            out = _class_sums(w, tiles, 0)
            for k in range(1, 8):
                out = out + jnp.where(row_id < k, before[k], rolled[k])
            emit(m, out)
        before = rolled


def _conv_ahead(src_ref, w, lanes, row_id, emit):
    before = None
    for m in range(CONV_TILES + 1):
        tiles = [src_ref[8 * (m + a):8 * (m + a) + 8, lanes] for a in range(4)]
        rolled = [None] + [pltpu.roll(_class_sums(w, tiles, k), 8 - k, 0) for k in range(1, 8)]
        if m >= 1:
            out = before[0]
            for k in range(1, 8):
                out = out + jnp.where(row_id < 8 - k, before[k], rolled[k])
            emit(m - 1, out)
        before = [_class_sums(w, tiles, 0) if m < CONV_TILES else None] + rolled[1:]


def _conv_weight_sums(d_ref, s_ref, lanes, row_id, whole_shifts):
    zero = jnp.zeros((8, CONV_LANES), F32)
    sums = {8 * a + k: zero for a in whole_shifts for k in range(8) if 8 * a + k < CONV_K}

    def d_tile(m):
        return d_ref[8 * m:8 * m + 8, lanes] if 0 <= m < CONV_TILES else zero

    rolled = [None] + [zero] * 7
    for m in range(-1, CONV_TILES):
        cur, nxt = d_tile(m), d_tile(m + 1)
        rolled_next = [None] + [pltpu.roll(nxt, 8 - k, 0) for k in range(1, 8)]
        shifted = [cur] + [jnp.where(row_id < 8 - k, rolled[k], rolled_next[k]) for k in range(1, 8)]
        for a in whole_shifts:
            tile = s_ref[8 * (CONV_HALO // 8 + m - a):8 * (CONV_HALO // 8 + m - a) + 8, lanes]
            for k in range(8):
                if 8 * a + k < CONV_K and not (m < 0 and k == 0):
                    sums[8 * a + k] = sums[8 * a + k] + shifted[k] * tile
        rolled = rolled_next
    return sums


def _conv_fwd(zr, w_dw, b_dw, g_ln, b_ln, comm=None):
    seq = zr.shape[0]

    def body(a_ref, b_ref, ah_ref, bh_ref, w_ref, bias_ref, g_ref, bl_ref, u1_ref, u3_ref, s_ref):
        _glu_into(s_ref, a_ref, b_ref, ah_ref, bh_ref, pl.program_id(0) == 0)
        row_id = lax.broadcasted_iota(jnp.int32, (8, CONV_LANES), 0)
        for lo in range(0, D_CONV, CONV_LANES):
            lanes = slice(lo, lo + CONV_LANES)
            bias = jnp.broadcast_to(bias_ref[:, lanes], (8, CONV_LANES))

            def emit(m, out, lanes=lanes, bias=bias):
                u1_ref[8 * m:8 * m + 8, lanes] = out + bias

            _conv_back(s_ref, CONV_HALO // 8, _lag_weights(w_ref, lanes), lanes, row_id, emit)
        _, _, u2 = _ln_silu(u1_ref[...], g_ref[...], bl_ref[...])
        u3_ref[...] = (u2 * _sig(u2)).astype(BF16)

    return _host_call(
        lambda ins, outs, scratch: body(*ins, *outs, *scratch), "conv_fwd", grid=(seq // CONV_ROWS,),
        in_specs=[_tile(CONV_ROWS, D_CONV, 0), _tile(CONV_ROWS, D_CONV, 1),
                  _prev(CONV_HALO, D_CONV, CONV_ROWS, 0), _prev(CONV_HALO, D_CONV, CONV_ROWS, 1),
                  _full((CONV_K, D_CONV)), _full((1, D_CONV)), _full((1, D_CONV)), _full((1, D_CONV))],
        out_specs=[_tile(CONV_ROWS, D_CONV), _tile(CONV_ROWS, D_CONV)],
        out_shape=[jax.ShapeDtypeStruct((seq, D_CONV), F32), jax.ShapeDtypeStruct((seq, D_CONV), BF16)],
        scratch_shapes=[pltpu.VMEM((CONV_HALO + CONV_ROWS, D_CONV), F32)],
        args=[zr, zr, zr, zr, w_dw, b_dw, g_ln, b_ln], comm=comm)


def _conv_bwd(zr, u1, du3, w_dw, g_ln, b_ln, comm=None):
    seq = zr.shape[0]
    n_tiles = seq // CONV_ROWS
    n_halo = seq // CONV_HALO
    ext = CONV_ROWS + CONV_HALO

    def body(a_ref, b_ref, ah_ref, bh_ref, u1_ref, u1n_ref, d3_ref, d3n_ref, w_ref, g_ref, bl_ref,
             da_ref, db_ref, dw_ref, small_ref, s_ref, d_ref, du0_ref):
        i = pl.program_id(0)

        @pl.when(i == 0)
        def _():
            dw_ref[...] = jnp.zeros_like(dw_ref)
            small_ref[...] = jnp.zeros_like(small_ref)

        _glu_into(s_ref, a_ref, b_ref, ah_ref, bh_ref, i == 0)
        gv, bv = g_ref[...], bl_ref[...]

        def du1_of(u1, d3):
            xhat, rs, u2 = _ln_silu(u1, gv, bv)
            sg = _sig(u2)
            du2 = d3 * (sg * (1.0 + u2 * (1.0 - sg)))
            dxh = du2 * gv
            du1 = rs * (dxh - jnp.mean(dxh, axis=-1, keepdims=True)
                        - xhat * jnp.mean(dxh * xhat, axis=-1, keepdims=True))
            return du1, du2, xhat

        du1, du2, xhat = du1_of(u1_ref[...], d3_ref[...])
        du1n, _, _ = du1_of(u1n_ref[...], d3n_ref[...])
        d_ref[0:CONV_ROWS, :] = du1
        d_ref[CONV_ROWS:ext, :] = jnp.where(i == n_tiles - 1, 0.0, du1n)
        small_ref[0:1, :] += _colsum(du1)
        small_ref[1:2, :] += _colsum(du2 * xhat)
        small_ref[2:3, :] += _colsum(du2)
        row_id = lax.broadcasted_iota(jnp.int32, (8, CONV_LANES), 0)
        for lo in range(0, D_CONV, CONV_LANES):
            lanes = slice(lo, lo + CONV_LANES)

            def emit(m, out, lanes=lanes):
                du0_ref[8 * m:8 * m + 8, lanes] = out

            _conv_ahead(d_ref, _lag_weights(w_ref, lanes), lanes, row_id, emit)
            for whole_shifts in ((0, 1), (2, 3)):
                for e, total in _conv_weight_sums(d_ref, s_ref, lanes, row_id, whole_shifts).items():
                    dw_ref[CONV_K - 1 - e:CONV_K - e, lanes] += _colsum(total)
        du0 = du0_ref[...]
        sb = _sig(b_ref[...])
        da = du0 * sb
        dbv = du0 * a_ref[...] * sb * (1.0 - sb)
        da_ref[...] = da.astype(BF16)
        db_ref[...] = dbv.astype(BF16)
        small_ref[3:4, :] += _colsum(da)
        small_ref[4:5, :] += _colsum(dbv)

    return _host_call(
        lambda ins, outs, scratch: body(*ins, *outs, *scratch), "conv_bwd", grid=(n_tiles,),
        in_specs=[_tile(CONV_ROWS, D_CONV, 0), _tile(CONV_ROWS, D_CONV, 1),
                  _prev(CONV_HALO, D_CONV, CONV_ROWS, 0), _prev(CONV_HALO, D_CONV, CONV_ROWS, 1),
                  _tile(CONV_ROWS, D_CONV), _next(CONV_HALO, D_CONV, CONV_ROWS, n_halo),
                  _tile(CONV_ROWS, D_CONV), _next(CONV_HALO, D_CONV, CONV_ROWS, n_halo),
                  _full((CONV_K, D_CONV)), _full((1, D_CONV)), _full((1, D_CONV))],
        out_specs=[_tile(CONV_ROWS, D_CONV), _tile(CONV_ROWS, D_CONV), _full((CONV_HALO, D_CONV)),
                   _full((8, D_CONV))],
        out_shape=[jax.ShapeDtypeStruct((seq, D_CONV), BF16), jax.ShapeDtypeStruct((seq, D_CONV), BF16),
                   jax.ShapeDtypeStruct((CONV_HALO, D_CONV), F32), jax.ShapeDtypeStruct((8, D_CONV), F32)],
        scratch_shapes=[pltpu.VMEM((ext, D_CONV), F32), pltpu.VMEM((ext, D_CONV), F32),
                        pltpu.VMEM((CONV_ROWS, D_CONV), F32)],
        args=[zr, zr, zr, zr, u1, u1, du3, du3, w_dw, g_ln, b_ln], comm=comm)


MERGE_ROWS = 256


def _merge_fwd(ao, u3, zr, w_ao, w_co, b_co):
    seq = ao.shape[0]

    def body(ao_ref, u3_ref, ga_ref, gb_ref, wa_ref, wc_ref, bc_ref, y_ref, a_ref, cb_ref):
        a = jnp.dot(ao_ref[...], wa_ref[...], preferred_element_type=F32)
        cb = jnp.dot(u3_ref[...], wc_ref[...], preferred_element_type=F32) + bc_ref[...]
        a_ref[...] = a
        cb_ref[...] = cb
        y_ref[...] = (_sig(ga_ref[...]) * a + _sig(gb_ref[...]) * cb).astype(BF16)

    f32_out = jax.ShapeDtypeStruct((seq, D_MODEL), F32)
    return pl.pallas_call(
        body, name="merge_fwd",
        out_shape=[jax.ShapeDtypeStruct((seq, D_MODEL), BF16), f32_out, f32_out],
        grid=(seq // MERGE_ROWS,),
        in_specs=[_tile(MERGE_ROWS, D_ATTN), _tile(MERGE_ROWS, D_CONV), _tile(MERGE_ROWS, D_MODEL, 1),
                  _tile(MERGE_ROWS, D_MODEL, 2), _full(w_ao.shape), _full(w_co.shape), _full((1, D_MODEL))],
        out_specs=[_tile(MERGE_ROWS, D_MODEL)] * 3, compiler_params=_cparams(1),
    )(ao, u3, zr, zr, w_ao, w_co, b_co)


def _merge_bwd(a, cb, zr, rows):
    seq = a.shape[0]

    def fn(dy_v, first, ins, outs):
        a_ref, cb_ref, ga_ref, gb_ref = ins
        da_ref, dcb_ref, dga_ref, dgb_ref, small_ref = outs

        @pl.when(first)
        def _():
            small_ref[...] = jnp.zeros_like(small_ref)

        sa, sb = _sig(ga_ref[...]), _sig(gb_ref[...])
        dcb = dy_v * sb
        dga = dy_v * a_ref[...] * sa * (1.0 - sa)
        dgb = dy_v * cb_ref[...] * sb * (1.0 - sb)
        da_ref[...] = (dy_v * sa).astype(BF16)
        dcb_ref[...] = dcb.astype(BF16)
        dga_ref[...] = dga.astype(BF16)
        dgb_ref[...] = dgb.astype(BF16)
        small_ref[0:1, :] += _colsum(dga)
        small_ref[1:2, :] += _colsum(dgb)
        small_ref[2:3, :] += _colsum(dcb)

    bf = jax.ShapeDtypeStruct((seq, D_MODEL), BF16)
    gate = lambda col: pl.BlockSpec((rows, D_MODEL), lambda i, j: (i, col))
    return _Epilogue(
        [a, cb, zr, zr], [_row_tile(rows, D_MODEL), _row_tile(rows, D_MODEL), gate(1), gate(2)],
        [bf, bf, bf, bf, jax.ShapeDtypeStruct((8, D_MODEL), F32)],
        [_row_tile(rows, D_MODEL)] * 4 + [_whole((8, D_MODEL))], fn, keep_product=False)


FFN_ROWS = 2048
FFN_BLOCKS = D_FF // FFN_COLS
GELU_C = math.sqrt(2.0 / math.pi)


def _gelu(v):
    t = jnp.tanh(GELU_C * (v + 0.044715 * (v * v * v)))
    return 0.5 * v * (1.0 + t), t


def _gelu_grad(v, t):
    return 0.5 * (1.0 + t) + 0.5 * v * (1.0 - t * t) * (GELU_C * (1.0 + 3.0 * 0.044715 * (v * v)))


def _sublane_rows(ref, n):
    return [jnp.broadcast_to(ref[r:r + 1, :], (8, FFN_COLS)) for r in range(n)]


def _rolls(tile, shifts):
    return tuple(pltpu.roll(tile, s, 0) for s in shifts)


def _behind(prev_rolls, cur, row_id):
    rolls = _rolls(cur, (1, 2))
    x1 = jnp.where(row_id < 1, prev_rolls[0], rolls[0])
    x2 = jnp.where(row_id < 2, prev_rolls[1], rolls[1])
    return (x2, x1, cur), rolls


def _ahead(cur_rolls, next_rolls, row_id):
    return (jnp.where(row_id < 7, cur_rolls[0], next_rolls[0]), jnp.where(row_id < 6, cur_rolls[1], next_rolls[1]))


def _conv3(taps, w, bias):
    return w[0] * taps[0] + w[1] * taps[1] + w[2] * taps[2] + bias


def _ffn_specs(rows):
    tile = lambda off: pl.BlockSpec((rows, FFN_COLS), lambda j, i: (i, j + off))
    prev = lambda off: pl.BlockSpec((FFN_HALO, FFN_COLS),
                                    lambda j, i: (jnp.maximum(i * (rows // FFN_HALO) - 1, 0), j + off))
    wgt = lambda off: pl.BlockSpec((3, FFN_COLS), lambda j, i: (0, j + off))
    vec = lambda off: pl.BlockSpec((1, FFN_COLS), lambda j, i: (0, j + off))
    return tile, prev, wgt, vec


def _ffn_act(up, w_dw, b_dw):
    seq = up.shape[0]
    tile, prev, wgt, vec = _ffn_specs(FFN_ROWS)

    def body(v_ref, g_ref, vp_ref, gp_ref, wv_ref, wg_ref, bv_ref, bg_ref, act_ref):
        first = pl.program_id(1) == 0
        row_id = lax.broadcasted_iota(jnp.int32, (8, FFN_COLS), 0)
        wv, wg = _sublane_rows(wv_ref, 3), _sublane_rows(wg_ref, 3)
        (bv,), (bg,) = _sublane_rows(bv_ref, 1), _sublane_rows(bg_ref, 1)
        rolls_v = _rolls(jnp.where(first, 0.0, vp_ref[...]), (1, 2))
        rolls_g = _rolls(jnp.where(first, 0.0, gp_ref[...]), (1, 2))
        for row in range(0, FFN_ROWS, 16):
            halves = []
            for r in (row, row + 8):
                taps_v, rolls_v = _behind(rolls_v, v_ref[r:r + 8, :], row_id)
                taps_g, rolls_g = _behind(rolls_g, g_ref[r:r + 8, :], row_id)
                halves.append(_gelu(_conv3(taps_g, wg, bg))[0] * _conv3(taps_v, wv, bv))
            act_ref[row:row + 16, :] = jnp.concatenate(halves, axis=0).astype(BF16)

    return pl.pallas_call(
        body, name="ffn_act", out_shape=jax.ShapeDtypeStruct((seq, D_FF), BF16),
        grid=(FFN_BLOCKS, seq // FFN_ROWS),
        in_specs=[tile(0), tile(FFN_BLOCKS), prev(0), prev(FFN_BLOCKS), wgt(0), wgt(FFN_BLOCKS),
                  vec(0), vec(FFN_BLOCKS)],
        out_specs=tile(0), compiler_params=_cparams(2),
    )(up, up, up, up, w_dw, w_dw, b_dw, b_dw)


def _ffn_act_bwd(up, dact, w_dw, b_dw, comm=None):
    seq = up.shape[0]
    n_tiles = seq // FFN_ROWS
    n_halo = seq // FFN_HALO
    tile, prev, wgt, vec = _ffn_specs(FFN_ROWS)
    nxt = lambda off: pl.BlockSpec(
        (FFN_HALO, FFN_COLS), lambda j, i: (jnp.minimum((i + 1) * (FFN_ROWS // FFN_HALO), n_halo - 1), j + off))
    acc = lambda off: pl.BlockSpec((8, FFN_COLS), lambda j, i: (0, j + off))

    def body(v_ref, g_ref, vp_ref, gp_ref, vn_ref, gn_ref, da_ref, dan_ref, wv_ref, wg_ref, bv_ref, bg_ref,
             dv_out, dg_out, dwv_ref, dwg_ref, dbv_ref, dbg_ref):
        i = pl.program_id(1)
        first, last = i == 0, i == n_tiles - 1

        @pl.when(first)
        def _():
            for r in (dwv_ref, dwg_ref, dbv_ref, dbg_ref):
                r[...] = jnp.zeros_like(r)

        row_id = lax.broadcasted_iota(jnp.int32, (8, FFN_COLS), 0)
        wv, wg = _sublane_rows(wv_ref, 3), _sublane_rows(wg_ref, 3)
        (bv,), (bg,) = _sublane_rows(bv_ref, 1), _sublane_rows(bg_ref, 1)
        zero = jnp.zeros((8, FFN_COLS), F32)
        sums_v, sums_g = [zero] * 4, [zero] * 4
        rolls_v = _rolls(jnp.where(first, 0.0, vp_ref[...]), (1, 2))
        rolls_g = _rolls(jnp.where(first, 0.0, gp_ref[...]), (1, 2))
        behind = None
        done_v, done_g = [], []

        def grads(v_tile, g_tile, dact, rolls_v, rolls_g):
            taps_v, rolls_v = _behind(rolls_v, v_tile, row_id)
            taps_g, rolls_g = _behind(rolls_g, g_tile, row_id)
            val, gate = _conv3(taps_v, wv, bv), _conv3(taps_g, wg, bg)
            gel, t = _gelu(gate)
            return dact * gel, dact * val * _gelu_grad(gate, t), taps_v, taps_g, rolls_v, rolls_g

        def finish(tile, nxt, row):
            for (d, d_rolls), (_, n_rolls), w, done, o_ref in ((tile[0], nxt[0], wv, done_v, dv_out),
                                                               (tile[1], nxt[1], wg, done_g, dg_out)):
                d1, d2 = _ahead(d_rolls, n_rolls, row_id)
                done.append(w[2] * d + w[1] * d1 + w[0] * d2)
                if len(done) == 2:
                    o_ref[row - 16:row, :] = jnp.concatenate(done, axis=0).astype(BF16)
                    done.clear()

        for row in range(0, FFN_ROWS, 16):
            dact16 = da_ref[row:row + 16, :].astype(F32)
            for r, dact in ((row, dact16[0:8, :]), (row + 8, dact16[8:16, :])):
                dval, dgate, taps_v, taps_g, rolls_v, rolls_g = grads(v_ref[r:r + 8, :], g_ref[r:r + 8, :], dact,
                                                                      rolls_v, rolls_g)
                sums_v = [s + dval * x for s, x in zip(sums_v, taps_v)] + [sums_v[3] + dval]
                sums_g = [s + dgate * x for s, x in zip(sums_g, taps_g)] + [sums_g[3] + dgate]
                tile = ((dval, _rolls(dval, (7, 6))), (dgate, _rolls(dgate, (7, 6))))
                if behind is not None:
                    finish(behind, tile, r)
                behind = tile
        dact_next = jnp.where(last, 0.0, dan_ref[...].astype(F32)[0:FFN_HALO, :])
        dval, dgate, *_ = grads(vn_ref[...], gn_ref[...], dact_next, rolls_v, rolls_g)
        finish(behind, ((dval, _rolls(dval, (7, 6))), (dgate, _rolls(dgate, (7, 6)))), FFN_ROWS)
        for sums, dw_ref, db_ref in ((sums_v, dwv_ref, dbv_ref), (sums_g, dwg_ref, dbg_ref)):
            for tap in range(3):
                dw_ref[tap:tap + 1, :] += _colsum(sums[tap])
            db_ref[0:1, :] += _colsum(sums[3])

    half = jax.ShapeDtypeStruct((seq, D_FF), BF16)
    acc_shape = jax.ShapeDtypeStruct((8, D_FF), F32)
    return _host_call(
        lambda ins, outs, scratch: body(*ins, *outs, *scratch), "ffn_act_bwd", grid=(FFN_BLOCKS, n_tiles),
        in_specs=[tile(0), tile(FFN_BLOCKS), prev(0), prev(FFN_BLOCKS), nxt(0), nxt(FFN_BLOCKS),
                  tile(0), pl.BlockSpec((16, FFN_COLS), lambda j, i: (
                      jnp.minimum((i + 1) * (FFN_ROWS // 16), seq // 16 - 1), j)),
                  wgt(0), wgt(FFN_BLOCKS), vec(0), vec(FFN_BLOCKS)],
        out_specs=[tile(0), tile(0), acc(0), acc(0), acc(0), acc(0)],
        out_shape=[half, half, acc_shape, acc_shape, acc_shape, acc_shape],
        scratch_shapes=[], args=[up, up, up, up, up, up, dact, dact, w_dw, w_dw, b_dw, b_dw], comm=comm)


def _cols_to_blocks(full_cols):
    k, n8 = full_cols.shape
    return jnp.transpose(full_cols.reshape(k, N_DEV, n8 // N_DEV), (1, 0, 2))


def _rows_to_blocks(full_rows):
    r8, n = full_rows.shape
    return full_rows.reshape(N_DEV, r8 // N_DEV, n)


def _blocks_to_cols(gathered):
    _, k, n = gathered.shape
    return jnp.transpose(gathered, (1, 0, 2)).reshape(k, N_DEV * n)


def kernel(x, c, w_ada, b_ada, g_pre_mix, g_post_mix, w_in, b_in, rel_bias, w_attn_o, w_dw_conv, b_dw_conv, g_conv_ln, b_conv_ln, w_conv_o, b_conv_o, w_mix_o, g_pre_ffn, g_post_ffn, w_up, w_dw_ffn, b_dw_ffn, w_down, loss_target, m_w_ada, m_b_ada, m_g_pre_mix, m_g_post_mix, m_w_in, m_b_in, m_rel_bias, m_w_attn_o, m_w_dw_conv, m_b_dw_conv, m_g_conv_ln, m_b_conv_ln, m_w_conv_o, m_b_conv_o, m_w_mix_o, m_g_pre_ffn, m_g_post_ffn, m_w_up, m_w_dw_ffn, m_b_dw_ffn, m_w_down, v_w_ada, v_b_ada, v_g_pre_mix, v_g_post_mix, v_w_in, v_b_in, v_rel_bias, v_w_attn_o, v_w_dw_conv, v_b_dw_conv, v_g_conv_ln, v_b_conv_ln, v_w_conv_o, v_b_conv_o, v_w_mix_o, v_g_pre_ffn, v_g_post_ffn, v_w_up, v_w_dw_ffn, v_b_dw_ffn, v_w_down):
    names = ["w_ada", "b_ada", "g_pre_mix", "g_post_mix", "w_in", "b_in", "rel_bias", "w_attn_o", "w_dw_conv",
             "b_dw_conv", "g_conv_ln", "b_conv_ln", "w_conv_o", "b_conv_o", "w_mix_o", "g_pre_ffn", "g_post_ffn",
             "w_up", "w_dw_ffn", "b_dw_ffn", "w_down"]
    weights = dict(zip(names, [w_ada, b_ada, g_pre_mix, g_post_mix, w_in, b_in, rel_bias, w_attn_o, w_dw_conv,
                               b_dw_conv, g_conv_ln, b_conv_ln, w_conv_o, b_conv_o, w_mix_o, g_pre_ffn,
                               g_post_ffn, w_up, w_dw_ffn, b_dw_ffn, w_down]))
    mom_m = dict(zip(names, [m_w_ada, m_b_ada, m_g_pre_mix, m_g_post_mix, m_w_in, m_b_in, m_rel_bias, m_w_attn_o,
                             m_w_dw_conv, m_b_dw_conv, m_g_conv_ln, m_b_conv_ln, m_w_conv_o, m_b_conv_o,
                             m_w_mix_o, m_g_pre_ffn, m_g_post_ffn, m_w_up, m_w_dw_ffn, m_b_dw_ffn, m_w_down]))
    mom_v = dict(zip(names, [v_w_ada, v_b_ada, v_g_pre_mix, v_g_post_mix, v_w_in, v_b_in, v_rel_bias, v_w_attn_o,
                             v_w_dw_conv, v_b_dw_conv, v_g_conv_ln, v_b_conv_ln, v_w_conv_o, v_b_conv_o,
                             v_w_mix_o, v_g_pre_ffn, v_g_post_ffn, v_w_up, v_w_dw_ffn, v_b_dw_ffn, v_w_down]))
    shapes = {n: w.shape for n, w in weights.items()}

    seq = x.shape[1]
    me = 4 * lax.axis_index("x") + 2 * lax.axis_index("y") + lax.axis_index("c")
    x2 = x.reshape(seq, D_MODEL)
    target = loss_target.reshape(seq, D_MODEL)
    sq = lambda a: a.reshape(a.shape[1:])
    bf = lambda a: sq(a).astype(BF16)

    c_act = _silu_vec(c)
    c_all, g_in, g_dwc, g_dwf = _run_comm(
        _gather_comm([c_act, bf(w_in), sq(w_dw_conv), sq(w_dw_ffn)]), "gather_first")
    c_all = c_all.reshape(N_DEV, D_MODEL)
    wf_in = _blocks_to_cols(g_in)
    wf_dwc = _blocks_to_cols(g_dwc)
    wf_dwf = _blocks_to_cols(g_dwf)

    (mod_all,) = _run_comm(_gather_comm([_ada_fwd(c_all, sq(w_ada))]), "gather_mod")
    mod = lax.dynamic_index_in_dim(mod_all, me, axis=1, keepdims=False)
    mod6 = (mod.reshape(1, 6 * D_MODEL) + b_ada).reshape(6, D_MODEL)

    h1 = _pre_mix(x2, mod6, g_pre_mix)
    qkv = _mm(h1, wf_in, "nn", BF16, "in_proj_qkv", bias=b_in, tm=1024, tn=768, cols=(0, 3 * D_ATTN))
    zr, _, (g_ao, g_co, g_mo) = _mm(h1, wf_in, "nn", F32, "in_proj_rest", bias=b_in, tm=1024, tn=3 * D_ATTN,
                                 cols=(3 * D_ATTN, 2 * D_CONV + 2 * D_MODEL),
                                 comm=_gather_comm([bf(w_attn_o), bf(w_conv_o), bf(w_mix_o)]))
    kpad = jnp.pad(qkv[:, D_ATTN:2 * D_ATTN], ((PAD_ROWS, 0), (0, 0)))
    vpad = jnp.pad(qkv[:, 2 * D_ATTN:], ((PAD_ROWS, 0), (0, 0)))
    table = jnp.transpose(_bias_table(sq(rel_bias)), (1, 0, 2))
    ao, (g_up,) = _attn_fwd(qkv, kpad, vpad, table, comm=_gather_comm([bf(w_up)]))
    (u1, u3), (g_dn,) = _conv_fwd(zr, wf_dwc, b_dw_conv, g_conv_ln, b_conv_ln, comm=_gather_comm([bf(w_down)]))
    wf_ao = _blocks_to_cols(g_ao)
    wf_co = _blocks_to_cols(g_co)
    wf_mo = g_mo.reshape(D_MODEL, D_MODEL)
    wf_up = _blocks_to_cols(g_up)
    wf_dn = g_dn.reshape(D_FF, D_MODEL)
    y, a_br, cb_br = _merge_fwd(ao, u3, zr, wf_ao, wf_co, b_conv_o)
    ymix, (x1, h2), _ = _mm(y, wf_mo, "nn", F32, "mix_o", tm=512, tn=D_MODEL,
                            epilogue=_post_mix_pre_ffn(x2, mod6, g_post_mix, g_pre_ffn, 512))
    up = _mm(h2, wf_up, "nn", F32, "ffn_up", tm=1024, tn=1408)
    act = _ffn_act(up, wf_dwf, b_dw_ffn)
    _, (loss_lanes, dout, dyf, small_f), _ = _mm(act, wf_dn, "nn", F32, "ffn_down", tm=512, tn=D_MODEL,
                                                 epilogue=_final(x1, target, mod6, g_post_ffn, 512))

    dact = _mm(dyf, wf_dn, "nt", BF16, "ffn_down_dx", tm=1024, tn=1408)
    gw_down = _mm(act, dyf, "tn", BF16, "ffn_down_dw", tm=256, tn=1024)
    (dup_v, dup_g, dwv, dwg, dbv, dbg), (parts_down,) = _ffn_act_bwd(
        up, dact, wf_dwf, b_dw_ffn, comm=_scatter_comm([_rows_to_blocks(gw_down)]))
    _, (dx1, dymix, small_m), _ = _mm([dup_v, dup_g], wf_up, "nt", F32, "ffn_up_dx", tm=256, tn=D_MODEL,
                                      epilogue=_mid_bwd(x1, dout, ymix, mod6, g_pre_ffn, g_post_mix, 256))
    blocks_up = _mm_tn_blocks(h2, [dup_v, dup_g], 2 * D_FF // N_DEV, "ffn_up_dw", k_steps=2)
    _, (da, dcb, dga, dgb, small_g), _ = _mm(dymix, wf_mo, "nt", F32, "mix_o_dx", tm=512, tn=D_MODEL,
                                             epilogue=_merge_bwd(a_br, cb_br, zr, 512))
    gw_mo = _mm(y, dymix, "tn", BF16, "mix_o_dw")
    dao = _mm(da, wf_ao, "nt", BF16, "attn_o_dx", tm=1024)
    gw_ao = _mm(ao, da, "tn", BF16, "attn_o_dw")
    du3 = _mm(dcb, wf_co, "nt", F32, "conv_o_dx", tm=1024)
    gw_co = _mm(u3, dcb, "tn", BF16, "conv_o_dw")
    (dq, dkt, dvt, dbias, small_a), (parts_up,) = _attn_bwd(
        qkv, kpad, vpad, table, dao, comm=_scatter_comm([blocks_up]))
    g_rel = _bias_grad(jnp.transpose(dbias, (1, 0, 2)))
    (dglu_a, dglu_b, dw_conv, small_c), (parts_mo, parts_ao, parts_co) = _conv_bwd(
        zr, u1, du3, wf_dwc, g_conv_ln, b_conv_ln,
        comm=_scatter_comm([_rows_to_blocks(gw_mo), _cols_to_blocks(gw_ao), _cols_to_blocks(gw_co)]))
    dz = _assemble_dz(dq, dkt, dvt, dglu_a, dglu_b, dga, dgb)
    blocks_in = _mm_tn_blocks(h1, [dz], dz.shape[1] // N_DEV, "in_proj_dw")
    _, (grad_x, small_x), (parts_in,) = _mm(dz, wf_in, "nt", F32, "in_proj_dx", tm=512, tn=D_MODEL,
                                            comm=_scatter_comm([blocks_in]),
                                            epilogue=_pre_mix_bwd(x2, dx1, mod6, g_pre_mix, 512))

    packed = _pack_grads(small_x, small_m, small_f, small_g, small_a, small_c, dbv, dbg, dwv, dwg, dw_conv)
    gathered, gathered_rel, gathered_loss = _run_comm(_gather_comm([packed, g_rel, loss_lanes]), "gather_small")
    gathered = gathered.reshape(N_DEV, PACKED_TOTAL)
    updates, g_dwc_full, g_dwf_full, loss_all = _small_adamw(gathered, gathered_rel, gathered_loss, weights, mom_m,
                                                             mom_v)
    loss = loss_all[0, 0]

    grads, deltas, new_m, new_v = {}, {}, {}, {}

    def record(name, update):
        for dst, val in zip((grads, deltas, new_m, new_v), update):
            dst[name] = val.reshape(shapes[name])

    for name, update in updates.items():
        record(name, update)

    def local_update(name, grad):
        record(name, _adamw(sq(weights[name]), sq(mom_m[name]), sq(mom_v[name]), "adamw_" + name, g=grad))

    conv_cols, ffn_cols, ada_cols = D_CONV // N_DEV, 2 * D_FF // N_DEV, 6 * D_MODEL // N_DEV
    local_update("w_dw_conv", lax.dynamic_slice(g_dwc_full, (0, me * conv_cols), (CONV_K, conv_cols)))
    local_update("w_dw_ffn", lax.dynamic_slice(g_dwf_full, (0, me * ffn_cols), (3, ffn_cols)))
    local_update("w_ada", _ada_grad(c_all, lax.dynamic_slice(gathered, (0, me * ada_cols), (N_DEV, ada_cols))))

    for name, part in (("w_in", parts_in), ("w_attn_o", parts_ao), ("w_conv_o", parts_co), ("w_mix_o", parts_mo),
                       ("w_up", parts_up), ("w_down", parts_down)):
        record(name, _adamw(sq(weights[name]), sq(mom_m[name]), sq(mom_v[name]), "adamw_" + name, parts=part))

    return (loss, grad_x.reshape(x.shape), *[grads[n] for n in names], *[deltas[n] for n in names],
            *[new_m[n] for n in names], *[new_v[n] for n in names])
```

```python
import functools
import math

import jax
import jax.numpy as jnp
from jax import lax
from jax.experimental import pallas as pl
from jax.experimental.pallas import tpu as pltpu

F32 = jnp.float32
BF16 = jnp.bfloat16
HIGHEST = lax.Precision.HIGHEST

D_MODEL = 1024
CHUNK = 64
LEFT_CHUNKS = 8
BAND = (LEFT_CHUNKS + 1) * CHUNK
PAD_ROWS = LEFT_CHUNKS * CHUNK
GROUP = 4
GROUP_Q = GROUP * CHUNK
GROUP_K = GROUP_Q + PAD_ROWS
SOFTMAX_ROWS = 16
TOEPLITZ = 640
N_HEADS = 8
HEAD_DIM = 64
D_ATTN = 512
D_CONV = 512
CONV_K = 31
CONV_HALO = 32
MAX_REL = 128
N_REL = 2 * MAX_REL + 1
D_FF = 2816
FFN_HALO = 8
FFN_COLS = 256
EPS = 1e-6
NEG_INF = -1e30
N_DEV = 8

ADAM_LR = 0.001
ADAM_B1 = 0.9
ADAM_B2 = 0.999
ADAM_EPS = 1e-08
ADAM_WD = 0.01
ADAM_STEP = 10

VMEM_LIMIT_BYTES = 56 * 1024 * 1024
ADAMW_BLOCK_BYTES = 768 * 1024

MESH = pl.DeviceIdType.MESH
ANY = pl.BlockSpec(memory_space=pl.ANY)

SH_M, SC_M, GT_M, SH_F, SC_F, GT_F = range(6)

SMALL = (("b_ada", 6144), ("g_pre_mix", 1024), ("g_post_mix", 1024), ("b_in", 4608), ("b_dw_conv", 512),
         ("g_conv_ln", 512), ("b_conv_ln", 512), ("b_conv_o", 1024), ("g_pre_ffn", 1024), ("g_post_ffn", 1024),
         ("b_dw_ffn", 5632))
PACKED_TOTAL = sum(n for _, n in SMALL) + CONV_K * D_CONV + 3 * 2 * D_FF


def _cparams(n_axes):
    return pltpu.CompilerParams(vmem_limit_bytes=VMEM_LIMIT_BYTES,
                                dimension_semantics=("arbitrary",) * n_axes)


def _sig(v):
    return 1.0 / (1.0 + jnp.exp(-v))


def _pick(n, target):
    if n <= target:
        return n
    t = target - target % 128
    while n % t:
        t -= 128
    return t


def _tile(rows, cols, col=0):
    return pl.BlockSpec((rows, cols), lambda i: (i, col))


def _full(shape):
    zeros = (0,) * len(shape)
    return pl.BlockSpec(shape, lambda i: zeros)


def _prev(halo, cols, rows, col=0):
    return pl.BlockSpec((halo, cols), lambda i: (jnp.maximum(i * (rows // halo) - 1, 0), col))


def _next(halo, cols, rows, n_blocks, col=0):
    return pl.BlockSpec((halo, cols), lambda i: (jnp.minimum((i + 1) * (rows // halo), n_blocks - 1), col))


class _Comm:
    def __init__(self, inputs, out_shapes, sems, start, finish, relay=None):
        self.inputs, self.out_shapes, self.sems, self.start, self.finish = inputs, out_shapes, sems, start, finish
        self.relay = relay


def _host_call(body, name, grid, in_specs, out_specs, out_shape, scratch_shapes, args, comm=None):
    n_in, n_out, n_scr = len(args), len(out_shape), len(scratch_shapes)
    c_in = list(comm.inputs) if comm else []
    c_out = list(comm.out_shapes) if comm else []
    c_sem = list(comm.sems) if comm else []

    def full(*refs):
        bounds = [0, n_in, len(c_in), n_out, len(c_out), n_scr, len(c_sem)]
        cuts = [sum(bounds[:i + 1]) for i in range(len(bounds))]
        ins, cins, outs, couts, scr, csems = (refs[lo:hi] for lo, hi in zip(cuts[:-1], cuts[1:]))
        if comm:
            first = functools.reduce(jnp.logical_and, [pl.program_id(ax) == 0 for ax in range(len(grid))])
            pl.when(first)(lambda: comm.start(cins, couts, csems))
            last = functools.reduce(jnp.logical_and, [pl.program_id(ax) == grid[ax] - 1 for ax in range(len(grid))])
            if comm.relay is not None:
                pl.when(last)(lambda: comm.relay(cins, couts, csems))
        body(ins, outs, scr)
        if comm:
            pl.when(last)(lambda: comm.finish(cins, couts, csems))

    res = pl.pallas_call(
        full, name=name, grid=grid, in_specs=list(in_specs) + [ANY] * len(c_in),
        out_specs=list(out_specs) + [ANY] * len(c_out), out_shape=list(out_shape) + c_out,
        scratch_shapes=list(scratch_shapes) + c_sem, compiler_params=_cparams(len(grid)),
    )(*args, *c_in)
    return list(res[:n_out]), list(res[n_out:])


def _run_comm(comm, name):
    n_in, n_out = len(comm.inputs), len(comm.out_shapes)

    def body(*refs):
        ins, outs, sems = refs[:n_in], refs[n_in:n_in + n_out], refs[n_in + n_out:]
        comm.start(ins, outs, sems)
        if comm.relay is not None:
            comm.relay(ins, outs, sems)
        comm.finish(ins, outs, sems)

    return pl.pallas_call(
        body, name=name, out_shape=list(comm.out_shapes), in_specs=[ANY] * n_in, out_specs=[ANY] * n_out,
        scratch_shapes=list(comm.sems),
    )(*comm.inputs)


def _place():
    return lax.axis_index("x"), lax.axis_index("y"), lax.axis_index("c")


def _gather_comm(arrs):
    n = len(arrs)

    def plan(ins, outs, sems):
        send_sems, recv_sems, local_sems = sems
        x, y, c = _place()
        me, sibling = (x, y, c), (x, y, 1 - c)
        chips = [(1 - x, y), (x, 1 - y), (1 - x, 1 - y)]

        def block(k, p):
            return outs[k].at[4 * p[0] + 2 * p[1] + p[2]]

        def copy(k, s, blk, to, src=None):
            return pltpu.make_async_remote_copy(
                src_ref=block(k, blk) if src is None else src, dst_ref=block(k, blk),
                send_sem=send_sems.at[7 * k + s], recv_sem=recv_sems.at[7 * k + s],
                device_id=to, device_id_type=MESH)

        mine = [pltpu.make_async_copy(ins[k], block(k, me), local_sems.at[k]) for k in range(n)]
        first = []
        for k in range(n):
            first.append(copy(k, 0, me, sibling, src=ins[k]))
            for j, chip in enumerate(chips):
                first.append(copy(k, 1 + j, me, (*chip, c), src=ins[k]))
        return me, sibling, chips, c, copy, mine, first

    def start(ins, outs, sems):
        *_, mine, first = plan(ins, outs, sems)
        for cp in mine + first:
            cp.start()

    def relay(ins, outs, sems):
        me, sibling, chips, c, copy, _, _ = plan(ins, outs, sems)
        for j, chip in enumerate(chips):
            for k in range(n):
                copy(k, 1 + j, (*chip, c), me).wait_recv()
                copy(k, 4 + j, (*chip, c), sibling).start()

    def finish(ins, outs, sems):
        me, sibling, chips, c, copy, mine, first = plan(ins, outs, sems)
        passed = [copy(k, 4 + j, (*chip, c), sibling) for j, chip in enumerate(chips) for k in range(n)]
        for k in range(n):
            copy(k, 0, sibling, me).wait_recv()
        for j, chip in enumerate(chips):
            for k in range(n):
                copy(k, 4 + j, (*chip, 1 - c), me).wait_recv()
        for cp in first + passed:
            cp.wait_send()
        for cp in mine:
            cp.wait()

    return _Comm(list(arrs), [jax.ShapeDtypeStruct((N_DEV,) + a.shape, a.dtype) for a in arrs],
                 [pltpu.SemaphoreType.DMA((7 * n,)), pltpu.SemaphoreType.DMA((7 * n,)),
                  pltpu.SemaphoreType.DMA((n,))], start, finish, relay)


def _scatter_comm(blocks):
    n = len(blocks)

    def plan(ins, outs, sems, arrivals):
        send_sems, recv_sems, local_sems = sems
        x, y, c = _place()
        me = 4 * x + 2 * y + c
        local = [pltpu.make_async_copy(ins[k].at[me], outs[k].at[me], local_sems.at[k]) for k in range(n)]
        sends, recvs = [], []
        for k in range(n):
            for mask in range(1, N_DEV):
                px = 1 - x if mask & 4 else x
                py = 1 - y if mask & 2 else y
                pc = 1 - c if mask & 1 else c
                peer = 4 * px + 2 * py + pc
                sem = 7 * k + mask - 1
                both = dict(send_sem=send_sems.at[sem], recv_sem=recv_sems.at[sem], device_id=(px, py, pc),
                            device_id_type=MESH)
                sends.append(pltpu.make_async_remote_copy(src_ref=ins[k].at[peer], dst_ref=outs[k].at[me], **both))
                if arrivals:
                    recvs.append(pltpu.make_async_remote_copy(src_ref=ins[k].at[me], dst_ref=outs[k].at[peer],
                                                              **both))
        return local, sends, recvs

    def start(ins, outs, sems):
        local, sends, _ = plan(ins, outs, sems, arrivals=False)
        for cp in local + sends:
            cp.start()

    def finish(ins, outs, sems):
        local, sends, recvs = plan(ins, outs, sems, arrivals=True)
        for cp in recvs:
            cp.wait_recv()
        for cp in sends:
            cp.wait_send()
        for cp in local:
            cp.wait()

    return _Comm(list(blocks), [jax.ShapeDtypeStruct(b.shape, b.dtype) for b in blocks],
                 [pltpu.SemaphoreType.DMA((7 * n,)), pltpu.SemaphoreType.DMA((7 * n,)),
                  pltpu.SemaphoreType.DMA((n,))], start, finish)


_DIMS = {"nn": (((1,), (0,)), ((), ())), "nt": (((1,), (1,)), ((), ())), "tn": (((0,), (0,)), ((), ()))}


class _Epilogue:
    def __init__(self, args, in_specs, out_shapes, out_specs, fn, keep_product):
        self.args, self.in_specs, self.out_shapes, self.out_specs = args, in_specs, out_shapes, out_specs
        self.fn, self.keep_product = fn, keep_product


def _row_tile(rows, cols):
    return pl.BlockSpec((rows, cols), lambda i, j: (i, 0))


def _whole(shape):
    zeros = (0,) * len(shape)
    return pl.BlockSpec(shape, lambda i, j: zeros)


def _mm(a, b, mode, out_dtype, name, bias=None, tm=512, tn=512, comm=None, cols=None, epilogue=None):
    pieces = a if isinstance(a, (list, tuple)) else [a]
    assert all(p.dtype == BF16 for p in pieces) and b.dtype == BF16
    a = pieces[0]
    if mode == "tn":
        k_dim, m_dim = a.shape
    else:
        m_dim, k_dim = a.shape
    n_dim = b.shape[0] if mode == "nt" else b.shape[1]
    col0 = 0
    if cols is not None:
        assert mode != "tn" and cols[0] % tn == 0 and cols[1] % tn == 0
        col0, n_dim = cols[0] // tn, cols[1]
    tm, tn = _pick(m_dim, tm), _pick(n_dim, tn)
    assert mode != "tn" or len(pieces) == 1
    a_specs = [pl.BlockSpec((k_dim, tm), lambda i, j: (0, i)) if mode == "tn"
               else pl.BlockSpec((tm, k_dim), lambda i, j: (i, 0))] * len(pieces)
    if mode == "nt":
        b_specs = [pl.BlockSpec((tn, k_dim), lambda i, j, p=p: (j + col0, p)) for p in range(len(pieces))]
    else:
        b_specs = [pl.BlockSpec((k_dim, tn), lambda i, j, p=p: (p, j + col0)) for p in range(len(pieces))]
    in_specs = a_specs + b_specs
    args = list(pieces) + [b] * len(pieces)
    if bias is not None:
        in_specs.append(pl.BlockSpec((1, tn), lambda i, j: (0, j + col0)))
        args.append(bias)
    dims = _DIMS[mode]
    n_pieces = len(pieces)
    n_own = len(args)
    keep = epilogue is None or epilogue.keep_product
    out_specs = [pl.BlockSpec((tm, tn), lambda i, j: (i, j))] if keep else []
    out_shape = [jax.ShapeDtypeStruct((m_dim, n_dim), out_dtype)] if keep else []
    if epilogue is not None:
        assert tn == n_dim
        in_specs, args = in_specs + list(epilogue.in_specs), args + list(epilogue.args)
        out_specs, out_shape = out_specs + list(epilogue.out_specs), out_shape + list(epilogue.out_shapes)

    def body(ins, outs, scratch):
        total = lax.dot_general(ins[0][...], ins[n_pieces][...], dims, preferred_element_type=F32)
        for p in range(1, n_pieces):
            total = total + lax.dot_general(ins[p][...], ins[n_pieces + p][...], dims, preferred_element_type=F32)
        if bias is not None:
            total = total + ins[2 * n_pieces][...]
        if keep:
            outs[0][...] = total.astype(out_dtype)
        if epilogue is not None:
            epilogue.fn(total, pl.program_id(0) == 0, ins[n_own:], outs[1:] if keep else outs)

    outs, extra = _host_call(body, name, grid=(m_dim // tm, n_dim // tn), in_specs=in_specs, out_specs=out_specs,
                             out_shape=out_shape, scratch_shapes=[], args=args, comm=comm)
    product = outs[0] if keep else None
    if comm is None and epilogue is None:
        return product
    return product, outs[1:] if keep else outs, extra


def _mm_tn_rows(pieces, b, name, tm=256):
    k_dim, n_dim = b.shape
    counts = [p.shape[1] // tm for p in pieces]
    assert all(p.shape[1] % tm == 0 for p in pieces)
    firsts = [sum(counts[:q]) for q in range(len(pieces))]

    def a_spec(first, count):
        return pl.BlockSpec((k_dim, tm), lambda i: (0, jnp.clip(i - first, 0, count - 1)))

    def body(ins, outs, scratch):
        i = pl.program_id(0)
        for a_ref, first, count in zip(ins[:-1], firsts, counts):
            @pl.when(jnp.logical_and(i >= first, i < first + count))
            def _(a_ref=a_ref):
                outs[0][...] = lax.dot_general(a_ref[...], ins[-1][...], _DIMS["tn"],
                                               preferred_element_type=F32).astype(BF16)

    (out,), _ = _host_call(
        body, name, grid=(sum(counts),),
        in_specs=[a_spec(f, c) for f, c in zip(firsts, counts)] + [_full((k_dim, n_dim))],
        out_specs=[_tile(tm, n_dim)], out_shape=[jax.ShapeDtypeStruct((sum(counts) * tm, n_dim), BF16)],
        scratch_shapes=[], args=list(pieces) + [b])
    return out


def _adam_math(w, g, m, v):
    m = ADAM_B1 * m + (1.0 - ADAM_B1) * g
    v = ADAM_B2 * v + (1.0 - ADAM_B2) * (g * g)
    m_hat = m / (1.0 - ADAM_B1 ** ADAM_STEP)
    v_hat = v / (1.0 - ADAM_B2 ** ADAM_STEP)
    delta = -ADAM_LR * (m_hat / (jnp.sqrt(v_hat) + ADAM_EPS) + ADAM_WD * w)
    return delta, m, v


def _adamw(w, m, v, name, g=None, parts=None):
    rows, cols = w.shape
    tr = rows
    if rows * cols * 4 > ADAMW_BLOCK_BYTES:
        tr = max(t for t in range(16, rows, 16) if rows % t == 0 and t * cols * 4 <= ADAMW_BLOCK_BYTES)

    def body(w_ref, m_ref, v_ref, g_ref, go_ref, d_ref, mo_ref, vo_ref):
        if parts is None:
            grad = g_ref[...]
        else:
            grad = g_ref[0].astype(F32)
            for d in range(1, N_DEV):
                grad = grad + g_ref[d].astype(F32)
        delta, m_new, v_new = _adam_math(w_ref[...], grad, m_ref[...], v_ref[...])
        go_ref[...] = grad
        d_ref[...] = delta
        mo_ref[...] = m_new
        vo_ref[...] = v_new

    spec = _tile(tr, cols)
    g_spec = spec if parts is None else pl.BlockSpec((N_DEV, tr, cols), lambda i: (0, i, 0))
    shape = jax.ShapeDtypeStruct((rows, cols), F32)
    return pl.pallas_call(
        body, name=name, out_shape=[shape] * 4, grid=(rows // tr,),
        in_specs=[spec, spec, spec, g_spec], out_specs=[spec] * 4, compiler_params=_cparams(1),
    )(w, m, v, g if parts is None else parts)


def _pack_grads(small_x, small_m, small_f, small_g, small_a, small_c, dbv, dbg, dwv, dwg, dw_conv):
    pieces = [
        (small_x, 2, D_MODEL), (small_x, 1, D_MODEL), (small_m, 4, D_MODEL), (small_m, 2, D_MODEL),
        (small_m, 1, D_MODEL), (small_f, 1, D_MODEL),
        (small_x, 0, D_MODEL), (small_m, 3, D_MODEL),
        (small_a, 0, D_ATTN), (small_a, 1, D_ATTN), (small_a, 2, D_ATTN), (small_c, 3, D_CONV),
        (small_c, 4, D_CONV), (small_g, 0, D_MODEL), (small_g, 1, D_MODEL),
        (small_c, 0, D_CONV), (small_c, 1, D_CONV), (small_c, 2, D_CONV),
        (small_g, 2, D_MODEL), (small_m, 0, D_MODEL), (small_f, 0, D_MODEL),
        (dbv, 0, D_FF), (dbg, 0, D_FF),
    ]
    pieces += [(dw_conv, j, D_CONV) for j in range(CONV_K)]
    pieces += [(src, tap, D_FF) for tap in range(3) for src in (dwv, dwg)]
    sources = [small_x, small_m, small_f, small_g, small_a, small_c, dbv, dbg, dwv, dwg, dw_conv]
    assert sum(width for _, _, width in pieces) == PACKED_TOTAL

    def body(*refs):
        o_ref = refs[-1]
        ref_of = {id(src): ref for src, ref in zip(sources, refs)}
        off = 0
        for src, row, width in pieces:
            o_ref[:, off:off + width] = ref_of[id(src)][row:row + 1, :]
            off += width

    return pl.pallas_call(body, name="pack_grads", out_shape=jax.ShapeDtypeStruct((1, PACKED_TOTAL), F32))(*sources)


def _small_adamw(gathered, gathered_rel, gathered_loss, weights, mom_m, mom_v):
    vec_names = [name for name, _ in SMALL]
    states = []
    for name in vec_names + ["rel_bias"]:
        states += [weights[name], mom_m[name], mom_v[name]]
    states = [a.reshape(a.shape[1:]) if a.ndim == 3 else a for a in states]
    n_state = len(states)

    def body(*refs):
        g_ref, rel_ref, loss_ref = refs[0], refs[1], refs[2]
        state_refs, out_refs = refs[3:3 + n_state], refs[3 + n_state:]
        total = g_ref[0:1, :]
        rel = rel_ref[0]
        loss = loss_ref[0]
        for d in range(1, N_DEV):
            total = total + g_ref[d:d + 1, :]
            rel = rel + rel_ref[d]
            loss = loss + loss_ref[d]
        off = 0
        for n, (name, width) in enumerate(SMALL):
            grad = total[:, off:off + width]
            w_ref, m_ref, v_ref = state_refs[3 * n:3 * n + 3]
            for ref, val in zip(out_refs[4 * n:4 * n + 4], (grad,) + _adam_math(w_ref[...], grad, m_ref[...], v_ref[...])):
                ref[...] = val
            off += width
        n = len(SMALL)
        w_ref, m_ref, v_ref = state_refs[3 * n:3 * n + 3]
        for ref, val in zip(out_refs[4 * n:4 * n + 4], (rel,) + _adam_math(w_ref[...], rel, m_ref[...], v_ref[...])):
            ref[...] = val
        dwc_ref, dwf_ref, loss_out = out_refs[4 * n + 4:]
        loss_out[...] = 0.5 * loss
        dwc_ref[...] = jnp.zeros_like(dwc_ref)
        dwf_ref[...] = jnp.zeros_like(dwf_ref)
        for j in range(CONV_K):
            dwc_ref[j:j + 1, :] = total[:, off:off + D_CONV]
            off += D_CONV
        for tap in range(3):
            dwf_ref[tap:tap + 1, :] = total[:, off:off + 2 * D_FF]
            off += 2 * D_FF

    out_shape = []
    for k in range(n_state // 3):
        out_shape += [jax.ShapeDtypeStruct(states[3 * k].shape, F32)] * 4
    out_shape += [jax.ShapeDtypeStruct((CONV_HALO, D_CONV), F32), jax.ShapeDtypeStruct((8, 2 * D_FF), F32),
                  jax.ShapeDtypeStruct((1, 128), F32)]
    res = pl.pallas_call(
        body, name="small_adamw", out_shape=out_shape,
        compiler_params=pltpu.CompilerParams(vmem_limit_bytes=VMEM_LIMIT_BYTES),
    )(gathered, gathered_rel, gathered_loss, *states)
    updates = {name: tuple(res[4 * n:4 * n + 4]) for n, name in enumerate(vec_names + ["rel_bias"])}
    return updates, res[-3], res[-2], res[-1]


def _silu_vec(c):
    def body(c_ref, o_ref):
        v = c_ref[...]
        o_ref[...] = v * _sig(v)

    return pl.pallas_call(body, name="silu_c", out_shape=jax.ShapeDtypeStruct(c.shape, F32))(c)


def _ada_fwd(c_all, w_shard):
    def body(c_ref, w_ref, o_ref):
        o_ref[...] = jnp.dot(c_ref[...], w_ref[...], precision=HIGHEST, preferred_element_type=F32)

    return pl.pallas_call(
        body, name="ada_fwd", out_shape=jax.ShapeDtypeStruct((N_DEV, w_shard.shape[1]), F32),
        compiler_params=pltpu.CompilerParams(vmem_limit_bytes=VMEM_LIMIT_BYTES),
    )(c_all, w_shard)


def _ada_grad(c_all, dmod_shard):
    def body(c_ref, d_ref, o_ref):
        o_ref[...] = lax.dot_general(c_ref[...], d_ref[...], _DIMS["tn"], precision=HIGHEST,
                                     preferred_element_type=F32)

    return pl.pallas_call(
        body, name="ada_grad", out_shape=jax.ShapeDtypeStruct((D_MODEL, dmod_shard.shape[1]), F32),
        compiler_params=pltpu.CompilerParams(vmem_limit_bytes=VMEM_LIMIT_BYTES),
    )(c_all, dmod_shard)


ROWS = 256


def _rms(v):
    r = lax.rsqrt(jnp.mean(v * v, axis=-1, keepdims=True) + EPS)
    return v * r, r


def _rms_bwd(dxn, xn, r):
    return r * (dxn - xn * jnp.mean(dxn * xn, axis=-1, keepdims=True))


def _colsum(v):
    return jnp.sum(v, axis=0, keepdims=True)


def _pre_mix(x, mod6, g1):
    seq = x.shape[0]

    def body(x_ref, mod_ref, g_ref, h_ref):
        xn, _ = _rms(x_ref[...])
        y = xn * g_ref[...]
        h_ref[...] = (y * (1.0 + mod_ref[SC_M:SC_M + 1, :]) + mod_ref[SH_M:SH_M + 1, :]).astype(BF16)

    return pl.pallas_call(
        body, name="pre_mix", out_shape=jax.ShapeDtypeStruct((seq, D_MODEL), BF16), grid=(seq // ROWS,),
        in_specs=[_tile(ROWS, D_MODEL), _full((6, D_MODEL)), _full((1, D_MODEL))],
        out_specs=_tile(ROWS, D_MODEL), compiler_params=_cparams(1),
    )(x, mod6, g1)


def _post_mix_pre_ffn(x, mod6, g2, g3, rows):
    seq = x.shape[0]

    def fn(y, first, ins, outs):
        x_ref, mod_ref, g2_ref, g3_ref = ins
        x1_ref, h_ref = outs
        yn, _ = _rms(y)
        x1 = x_ref[...] + mod_ref[GT_M:GT_M + 1, :] * (yn * g2_ref[...])
        x1_ref[...] = x1
        xn, _ = _rms(x1)
        y3 = xn * g3_ref[...]
        h_ref[...] = (y3 * (1.0 + mod_ref[SC_F:SC_F + 1, :]) + mod_ref[SH_F:SH_F + 1, :]).astype(BF16)

    return _Epilogue(
        [x, mod6, g2, g3], [_row_tile(rows, D_MODEL), _whole((6, D_MODEL)), _whole((1, D_MODEL)), _whole((1, D_MODEL))],
        [jax.ShapeDtypeStruct((seq, D_MODEL), F32), jax.ShapeDtypeStruct((seq, D_MODEL), BF16)],
        [_row_tile(rows, D_MODEL), _row_tile(rows, D_MODEL)], fn, keep_product=True)


def _final(x1, target, mod6, g4, rows):
    seq = x1.shape[0]

    def fn(y, first, ins, outs):
        x1_ref, t_ref, mod_ref, g_ref = ins
        loss_ref, dout_ref, dyf_ref, small_ref = outs

        @pl.when(first)
        def _():
            loss_ref[...] = jnp.zeros_like(loss_ref)
            small_ref[...] = jnp.zeros_like(small_ref)

        gt = mod_ref[GT_F:GT_F + 1, :]
        g4v = g_ref[...]
        yn, r = _rms(y)
        out = x1_ref[...] + gt * (yn * g4v)
        err = out - t_ref[...]
        loss_ref[...] += jnp.sum(jnp.mean(err * err, axis=-1, keepdims=True))
        dout = err * (1.0 / D_MODEL)
        dout_ref[...] = dout
        small_ref[0:1, :] += _colsum(dout * gt * yn)
        small_ref[1:2, :] += _colsum(dout * (yn * g4v))
        dyf_ref[...] = _rms_bwd(dout * gt * g4v, yn, r).astype(BF16)

    return _Epilogue(
        [x1, target, mod6, g4],
        [_row_tile(rows, D_MODEL), _row_tile(rows, D_MODEL), _whole((6, D_MODEL)), _whole((1, D_MODEL))],
        [jax.ShapeDtypeStruct((1, 128), F32), jax.ShapeDtypeStruct((seq, D_MODEL), F32),
         jax.ShapeDtypeStruct((seq, D_MODEL), BF16), jax.ShapeDtypeStruct((8, D_MODEL), F32)],
        [_whole((1, 128)), _row_tile(rows, D_MODEL), _row_tile(rows, D_MODEL), _whole((8, D_MODEL))],
        fn, keep_product=False)


def _mid_bwd(x1, dout, ymix, mod6, g3, g2, rows):
    seq = x1.shape[0]

    def fn(dh, first, ins, outs):
        x1_ref, dout_ref, y_ref, mod_ref, g3_ref, g2_ref = ins
        dx1_ref, dy_ref, small_ref = outs

        @pl.when(first)
        def _():
            small_ref[...] = jnp.zeros_like(small_ref)

        g3v, g2v = g3_ref[...], g2_ref[...]
        xn, r3 = _rms(x1_ref[...])
        y3 = xn * g3v
        dy3 = dh * (1.0 + mod_ref[SC_F:SC_F + 1, :])
        small_ref[0:1, :] += _colsum(dy3 * xn)
        small_ref[1:2, :] += _colsum(dh * y3)
        small_ref[2:3, :] += _colsum(dh)
        dx1 = dout_ref[...] + _rms_bwd(dy3 * g3v, xn, r3)
        dx1_ref[...] = dx1
        gt = mod_ref[GT_M:GT_M + 1, :]
        yn, r2 = _rms(y_ref[...])
        small_ref[3:4, :] += _colsum(dx1 * gt * yn)
        small_ref[4:5, :] += _colsum(dx1 * (yn * g2v))
        dy_ref[...] = _rms_bwd(dx1 * gt * g2v, yn, r2).astype(BF16)

    return _Epilogue(
        [x1, dout, ymix, mod6, g3, g2],
        [_row_tile(rows, D_MODEL)] * 3 + [_whole((6, D_MODEL)), _whole((1, D_MODEL)), _whole((1, D_MODEL))],
        [jax.ShapeDtypeStruct((seq, D_MODEL), F32), jax.ShapeDtypeStruct((seq, D_MODEL), BF16),
         jax.ShapeDtypeStruct((8, D_MODEL), F32)],
        [_row_tile(rows, D_MODEL), _row_tile(rows, D_MODEL), _whole((8, D_MODEL))], fn, keep_product=False)


def _pre_mix_bwd(x, dx1, mod6, g1, rows):
    seq = x.shape[0]

    def fn(dh, first, ins, outs):
        x_ref, dx1_ref, mod_ref, g_ref = ins
        dx_ref, small_ref = outs

        @pl.when(first)
        def _():
            small_ref[...] = jnp.zeros_like(small_ref)

        g1v = g_ref[...]
        xn, r = _rms(x_ref[...])
        dy = dh * (1.0 + mod_ref[SC_M:SC_M + 1, :])
        small_ref[0:1, :] += _colsum(dy * xn)
        small_ref[1:2, :] += _colsum(dh * (xn * g1v))
        small_ref[2:3, :] += _colsum(dh)
        dx_ref[...] = dx1_ref[...] + _rms_bwd(dy * g1v, xn, r)

    return _Epilogue(
        [x, dx1, mod6, g1],
        [_row_tile(rows, D_MODEL), _row_tile(rows, D_MODEL), _whole((6, D_MODEL)), _whole((1, D_MODEL))],
        [jax.ShapeDtypeStruct((seq, D_MODEL), F32), jax.ShapeDtypeStruct((8, D_MODEL), F32)],
        [_row_tile(rows, D_MODEL), _whole((8, D_MODEL))], fn, keep_product=False)


def _toeplitz_onehot(shape, offset_axis, top):
    m = lax.broadcasted_iota(jnp.int32, shape, offset_axis)
    i = lax.broadcasted_iota(jnp.int32, shape, 1 - offset_axis)
    return (i == jnp.clip(top - m, -MAX_REL, MAX_REL) + MAX_REL).astype(F32)


def _bias_table(rel_bias):
    width = GROUP_Q + GROUP_K

    def body(rb_ref, o_ref, t_ref):
        t_ref[...] = jnp.dot(rb_ref[...], _toeplitz_onehot((N_REL, width), 1, GROUP_K - 1), precision=HIGHEST,
                             preferred_element_type=F32)
        lane = lax.broadcasted_iota(jnp.int32, (N_HEADS, GROUP_K), 1)
        for r in range(GROUP_Q):
            first_key = (r // CHUNK) * CHUNK
            band = jnp.logical_and(lane >= first_key, lane < first_key + BAND)
            o_ref[r] = jnp.where(band, t_ref[:, GROUP_Q - 1 - r:GROUP_Q - 1 - r + GROUP_K], NEG_INF)

    return pl.pallas_call(
        body, name="bias_table", out_shape=jax.ShapeDtypeStruct((GROUP_Q, N_HEADS, GROUP_K), F32),
        scratch_shapes=[pltpu.VMEM((N_HEADS, width), F32)],
    )(rel_bias)


def _bias_grad(dbias_q):
    def body(d_ref, o_ref, t_ref):
        t_ref[...] = jnp.zeros_like(t_ref)
        for qi in range(CHUNK):
            t_ref[:, CHUNK - 1 - qi:CHUNK - 1 - qi + BAND] += d_ref[qi]
        o_ref[...] = jnp.dot(t_ref[...], _toeplitz_onehot((TOEPLITZ, N_REL), 0, BAND - 1), precision=HIGHEST,
                             preferred_element_type=F32)

    return pl.pallas_call(
        body, name="bias_grad", out_shape=jax.ShapeDtypeStruct((N_HEADS, N_REL), F32),
        scratch_shapes=[pltpu.VMEM((N_HEADS, TOEPLITZ), F32)],
    )(dbias_q)


def _load_resident(pairs, sems):
    copies = [pltpu.make_async_copy(src, dst, sems.at[n]) for n, (src, dst) in enumerate(pairs)]
    for cp in copies:
        cp.start()
    for cp in copies:
        cp.wait()


def _softmax_rows(s_ref, t_ref, valid, rows):
    s = s_ref[rows, :] * (HEAD_DIM ** -0.5) + t_ref[rows, :]
    s = jnp.where(valid, s, NEG_INF)
    e = jnp.exp(s - jnp.max(s, axis=-1, keepdims=True))
    return e / jnp.sum(e, axis=-1, keepdims=True)


def _valid_keys(g):
    kj = lax.broadcasted_iota(jnp.int32, (SOFTMAX_ROWS, GROUP_K), 1)
    return kj >= PAD_ROWS - g * GROUP_Q


def _attn_fwd(qkv, kpad, vpad, table, comm=None):
    seq = qkv.shape[0]

    def body(ins, outs, scratch):
        q_ref, k_hbm, v_hbm, t_hbm = ins
        (o_ref,) = outs
        k_ref, v_ref, t_ref, s_ref, p_ref, sems = scratch
        g = pl.program_id(0)

        @pl.when(g == 0)
        def _():
            _load_resident(((k_hbm, k_ref), (v_hbm, v_ref), (t_hbm, t_ref)), sems)

        window = pl.ds(pl.multiple_of(g * GROUP_Q, GROUP_Q), GROUP_K)
        valid = _valid_keys(g)
        for h in range(N_HEADS):
            cols = slice(h * HEAD_DIM, (h + 1) * HEAD_DIM)
            buf = h % 2
            s_ref[buf] = lax.dot_general(q_ref[:, cols], k_ref[window, cols], _DIMS["nt"],
                                         preferred_element_type=F32)
            for r in range(GROUP_Q // SOFTMAX_ROWS):
                rows = slice(r * SOFTMAX_ROWS, (r + 1) * SOFTMAX_ROWS)
                p_ref[buf, rows, :] = _softmax_rows(s_ref.at[buf], t_ref.at[h], valid, rows).astype(BF16)
            o_ref[:, cols] = jnp.dot(p_ref[buf], v_ref[window, cols], preferred_element_type=F32).astype(BF16)

    (ao,), extra = _host_call(
        body, "attn_fwd", grid=(seq // GROUP_Q,),
        in_specs=[_tile(GROUP_Q, D_ATTN), ANY, ANY, ANY], out_specs=[_tile(GROUP_Q, D_ATTN)],
        out_shape=[jax.ShapeDtypeStruct((seq, D_ATTN), BF16)],
        scratch_shapes=[pltpu.VMEM(kpad.shape, BF16), pltpu.VMEM(vpad.shape, BF16), pltpu.VMEM(table.shape, F32),
                        pltpu.VMEM((2, GROUP_Q, GROUP_K), F32), pltpu.VMEM((2, GROUP_Q, GROUP_K), BF16),
                        pltpu.SemaphoreType.DMA((3,))],
        args=[qkv, kpad, vpad, table], comm=comm)
    return ao, extra


def _attn_bwd(qkv, kpad, vpad, table, dao, comm=None):
    seq = qkv.shape[0]
    n_groups = seq // GROUP_Q
    fold_w = GROUP_K + (GROUP - 1) * CHUNK

    def body(ins, outs, scratch):
        q_ref, do_ref, k_hbm, v_hbm, t_hbm = ins
        dq_ref, dkt_hbm, dvt_hbm, db_ref, cs_ref = outs
        k_ref, v_ref, t_ref, db_acc, dkt_acc, dvt_acc, s_ref, dp_ref, p_ref, ds_ref, sems = scratch
        g = pl.program_id(0)

        @pl.when(g == 0)
        def _():
            _load_resident(((k_hbm, k_ref), (v_hbm, v_ref), (t_hbm, t_ref)), sems)
            db_acc[...] = jnp.zeros_like(db_acc)
            dkt_acc[...] = jnp.zeros_like(dkt_acc)
            dvt_acc[...] = jnp.zeros_like(dvt_acc)
            cs_ref[...] = jnp.zeros_like(cs_ref)

        window = pl.ds(pl.multiple_of(g * GROUP_Q, GROUP_Q), GROUP_K)
        valid = _valid_keys(g)
        for h in range(N_HEADS):
            cols = slice(h * HEAD_DIM, (h + 1) * HEAD_DIM)
            buf = h % 2
            qh, doh = q_ref[:, cols], do_ref[:, cols]
            kh, vh = k_ref[window, cols], v_ref[window, cols]
            s_ref[buf] = lax.dot_general(qh, kh, _DIMS["nt"], preferred_element_type=F32)
            dp_ref[buf] = lax.dot_general(doh, vh, _DIMS["nt"], preferred_element_type=F32)
            for r in range(GROUP_Q // SOFTMAX_ROWS):
                rows = slice(r * SOFTMAX_ROWS, (r + 1) * SOFTMAX_ROWS)
                p = _softmax_rows(s_ref.at[buf], t_ref.at[h], valid, rows)
                dp = dp_ref[buf, rows, :]
                ds = p * (dp - jnp.sum(dp * p, axis=-1, keepdims=True))
                chunk = (r * SOFTMAX_ROWS) // CHUNK
                shift = (GROUP - 1 - chunk) * CHUNK
                local = slice(r * SOFTMAX_ROWS - chunk * CHUNK, (r + 1) * SOFTMAX_ROWS - chunk * CHUNK)
                db_acc[h, local, shift:shift + GROUP_K] += ds
                p_ref[buf, rows, :] = p.astype(BF16)
                ds_ref[buf, rows, :] = (ds * (HEAD_DIM ** -0.5)).astype(BF16)
            dq_ref[:, cols] = jnp.dot(ds_ref[buf], kh, preferred_element_type=F32).astype(BF16)
            dkt_acc[cols, window] += lax.dot_general(qh, ds_ref[buf], _DIMS["tn"], preferred_element_type=F32)
            dvt_acc[cols, window] += lax.dot_general(doh, p_ref[buf], _DIMS["tn"], preferred_element_type=F32)
        cs_ref[0:1, :] += _colsum(dq_ref[...].astype(F32))

        @pl.when(g == n_groups - 1)
        def _():
            lo = (GROUP - 1) * CHUNK
            for h in range(N_HEADS):
                db_ref[h] = db_acc[h, :, lo:lo + BAND]
            inside = pl.ds(PAD_ROWS, seq)
            on_diagonal = (lax.broadcasted_iota(jnp.int32, (D_ATTN, D_ATTN), 0)
                           == lax.broadcasted_iota(jnp.int32, (D_ATTN, D_ATTN), 1))
            for row, acc in ((1, dkt_acc), (2, dvt_acc)):
                column = jnp.sum(acc[:, inside], axis=1, keepdims=True)
                cs_ref[row:row + 1, :] = _colsum(jnp.where(on_diagonal, column, 0.0))
            out_k = pltpu.make_async_copy(dkt_acc.at[:, inside], dkt_hbm, sems.at[0])
            out_v = pltpu.make_async_copy(dvt_acc.at[:, inside], dvt_hbm, sems.at[1])
            out_k.start()
            out_v.start()
            out_k.wait()
            out_v.wait()

    t_shape = (D_ATTN, seq + PAD_ROWS)
    outs, extra = _host_call(
        body, "attn_bwd", grid=(n_groups,),
        in_specs=[_tile(GROUP_Q, D_ATTN), _tile(GROUP_Q, D_ATTN), ANY, ANY, ANY],
        out_specs=[_tile(GROUP_Q, D_ATTN), ANY, ANY, _full((N_HEADS, CHUNK, BAND)), _full((8, D_ATTN))],
        out_shape=[jax.ShapeDtypeStruct((seq, D_ATTN), BF16), jax.ShapeDtypeStruct((D_ATTN, seq), F32),
                   jax.ShapeDtypeStruct((D_ATTN, seq), F32), jax.ShapeDtypeStruct((N_HEADS, CHUNK, BAND), F32),
                   jax.ShapeDtypeStruct((8, D_ATTN), F32)],
        scratch_shapes=[pltpu.VMEM(kpad.shape, BF16), pltpu.VMEM(vpad.shape, BF16), pltpu.VMEM(table.shape, F32),
                        pltpu.VMEM((N_HEADS, CHUNK, fold_w), F32), pltpu.VMEM(t_shape, F32),
                        pltpu.VMEM(t_shape, F32), pltpu.VMEM((2, GROUP_Q, GROUP_K), F32),
                        pltpu.VMEM((2, GROUP_Q, GROUP_K), F32), pltpu.VMEM((2, GROUP_Q, GROUP_K), BF16),
                        pltpu.VMEM((2, GROUP_Q, GROUP_K), BF16), pltpu.SemaphoreType.DMA((3,))],
        args=[qkv, dao, kpad, vpad, table], comm=comm)
    return outs, extra


def _assemble_dz(dq, dkt, dvt, dglu_a, dglu_b, dga, dgb):
    seq = dq.shape[0]
    rows = 512
    transposed = pl.BlockSpec((D_ATTN, rows), lambda i: (0, i))

    def body(dq_ref, dkt_ref, dvt_ref, da_ref, db_ref, dga_ref, dgb_ref, o_ref):
        o_ref[:, 0:D_ATTN] = dq_ref[...]
        o_ref[:, D_ATTN:2 * D_ATTN] = dkt_ref[...].T.astype(BF16)
        o_ref[:, 2 * D_ATTN:3 * D_ATTN] = dvt_ref[...].T.astype(BF16)
        off = 3 * D_ATTN
        for ref in (da_ref, db_ref, dga_ref, dgb_ref):
            width = ref.shape[1]
            o_ref[:, off:off + width] = ref[...]
            off += width

    width = 3 * D_ATTN + 2 * D_CONV + 2 * D_MODEL
    return pl.pallas_call(
        body, name="assemble_dz", out_shape=jax.ShapeDtypeStruct((seq, width), BF16), grid=(seq // rows,),
        in_specs=[_tile(rows, D_ATTN), transposed, transposed, _tile(rows, D_CONV), _tile(rows, D_CONV),
                  _tile(rows, D_MODEL), _tile(rows, D_MODEL)],
        out_specs=_tile(rows, width), compiler_params=_cparams(1),
    )(dq, dkt, dvt, dglu_a, dglu_b, dga, dgb)


CONV_ROWS = 256


def _ln_silu(u1, g, b):
    mu = jnp.mean(u1, axis=-1, keepdims=True)
    xc = u1 - mu
    rs = lax.rsqrt(jnp.mean(xc * xc, axis=-1, keepdims=True) + EPS)
    xhat = xc * rs
    u2 = xhat * g + b
    return xhat, rs, u2


def _glu_into(s_ref, a_ref, b_ref, ah_ref, bh_ref, first):
    halo = ah_ref[...] * _sig(bh_ref[...])
    s_ref[0:CONV_HALO, :] = jnp.where(first, 0.0, halo)
    s_ref[CONV_HALO:CONV_HALO + CONV_ROWS, :] = a_ref[...] * _sig(b_ref[...])


CONV_LANES = 128
CONV_TILES = CONV_ROWS // 8


def _lag_weights(w_ref, lanes):
    return {e: jnp.broadcast_to(w_ref[CONV_K - 1 - e:CONV_K - e, lanes], (8, CONV_LANES)) for e in range(CONV_K)}


def _class_sums(w, tiles, k):
    total = None
    for a, tile in enumerate(tiles):
        if 8 * a + k < CONV_K:
            term = w[8 * a + k] * tile
            total = term if total is None else total + term
    return total


def _conv_back(src_ref, first_tile, w, lanes, row_id, emit):
    before = None
    for m in range(-1, CONV_TILES):
        tiles = [src_ref[8 * (first_tile + m - a):8 * (first_tile + m - a) + 8, lanes] for a in range(4)]
        rolled = [None] + [pltpu.roll(_class_sums(w, tiles, k), k, 0) for k in range(1, 8)]
        if m >= 0:
            out = _class_sums(w, tiles, 0)
            for k in range(1, 8):
                out = out + jnp.where(row_id < k, before[k], rolled[k])
            emit(m, out)
        before = rolled


def _conv_ahead(src_ref, w, lanes, row_id, emit):
    before = None
    for m in range(CONV_TILES + 1):
        tiles = [src_ref[8 * (m + a):8 * (m + a) + 8, lanes] for a in range(4)]
        rolled = [None] + [pltpu.roll(_class_sums(w, tiles, k), 8 - k, 0) for k in range(1, 8)]
        if m >= 1:
            out = before[0]
            for k in range(1, 8):
                out = out + jnp.where(row_id < 8 - k, before[k], rolled[k])
            emit(m - 1, out)
        before = [_class_sums(w, tiles, 0) if m < CONV_TILES else None] + rolled[1:]


def _conv_weight_sums(d_ref, s_ref, lanes, row_id, whole_shifts):
    zero = jnp.zeros((8, CONV_LANES), F32)
    sums = {8 * a + k: zero for a in whole_shifts for k in range(8) if 8 * a + k < CONV_K}

    def d_tile(m):
        return d_ref[8 * m:8 * m + 8, lanes] if 0 <= m < CONV_TILES else zero

    rolled = [None] + [zero] * 7
    for m in range(-1, CONV_TILES):
        cur, nxt = d_tile(m), d_tile(m + 1)
        rolled_next = [None] + [pltpu.roll(nxt, 8 - k, 0) for k in range(1, 8)]
        shifted = [cur] + [jnp.where(row_id < 8 - k, rolled[k], rolled_next[k]) for k in range(1, 8)]
        for a in whole_shifts:
            tile = s_ref[8 * (CONV_HALO // 8 + m - a):8 * (CONV_HALO // 8 + m - a) + 8, lanes]
            for k in range(8):
                if 8 * a + k < CONV_K and not (m < 0 and k == 0):
                    sums[8 * a + k] = sums[8 * a + k] + shifted[k] * tile
        rolled = rolled_next
    return sums


def _conv_fwd(zr, w_dw, b_dw, g_ln, b_ln, comm=None):
    seq = zr.shape[0]

    def body(a_ref, b_ref, ah_ref, bh_ref, w_ref, bias_ref, g_ref, bl_ref, u1_ref, u3_ref, s_ref):
        _glu_into(s_ref, a_ref, b_ref, ah_ref, bh_ref, pl.program_id(0) == 0)
        row_id = lax.broadcasted_iota(jnp.int32, (8, CONV_LANES), 0)
        for lo in range(0, D_CONV, CONV_LANES):
            lanes = slice(lo, lo + CONV_LANES)
            bias = jnp.broadcast_to(bias_ref[:, lanes], (8, CONV_LANES))

            def emit(m, out, lanes=lanes, bias=bias):
                u1_ref[8 * m:8 * m + 8, lanes] = out + bias

            _conv_back(s_ref, CONV_HALO // 8, _lag_weights(w_ref, lanes), lanes, row_id, emit)
        _, _, u2 = _ln_silu(u1_ref[...], g_ref[...], bl_ref[...])
        u3_ref[...] = (u2 * _sig(u2)).astype(BF16)

    return _host_call(
        lambda ins, outs, scratch: body(*ins, *outs, *scratch), "conv_fwd", grid=(seq // CONV_ROWS,),
        in_specs=[_tile(CONV_ROWS, D_CONV, 0), _tile(CONV_ROWS, D_CONV, 1),
                  _prev(CONV_HALO, D_CONV, CONV_ROWS, 0), _prev(CONV_HALO, D_CONV, CONV_ROWS, 1),
                  _full((CONV_K, D_CONV)), _full((1, D_CONV)), _full((1, D_CONV)), _full((1, D_CONV))],
        out_specs=[_tile(CONV_ROWS, D_CONV), _tile(CONV_ROWS, D_CONV)],
        out_shape=[jax.ShapeDtypeStruct((seq, D_CONV), F32), jax.ShapeDtypeStruct((seq, D_CONV), BF16)],
        scratch_shapes=[pltpu.VMEM((CONV_HALO + CONV_ROWS, D_CONV), F32)],
        args=[zr, zr, zr, zr, w_dw, b_dw, g_ln, b_ln], comm=comm)


def _conv_bwd(zr, u1, du3, w_dw, g_ln, b_ln, comm=None):
    seq = zr.shape[0]
    n_tiles = seq // CONV_ROWS
    n_halo = seq // CONV_HALO
    ext = CONV_ROWS + CONV_HALO

    def body(a_ref, b_ref, ah_ref, bh_ref, u1_ref, u1n_ref, d3_ref, d3n_ref, w_ref, g_ref, bl_ref,
             da_ref, db_ref, dw_ref, small_ref, s_ref, d_ref, du0_ref):
        i = pl.program_id(0)

        @pl.when(i == 0)
        def _():
            dw_ref[...] = jnp.zeros_like(dw_ref)
            small_ref[...] = jnp.zeros_like(small_ref)

        _glu_into(s_ref, a_ref, b_ref, ah_ref, bh_ref, i == 0)
        gv, bv = g_ref[...], bl_ref[...]

        def du1_of(u1, d3):
            xhat, rs, u2 = _ln_silu(u1, gv, bv)
            sg = _sig(u2)
            du2 = d3 * (sg * (1.0 + u2 * (1.0 - sg)))
            dxh = du2 * gv
            du1 = rs * (dxh - jnp.mean(dxh, axis=-1, keepdims=True)
                        - xhat * jnp.mean(dxh * xhat, axis=-1, keepdims=True))
            return du1, du2, xhat

        du1, du2, xhat = du1_of(u1_ref[...], d3_ref[...])
        du1n, _, _ = du1_of(u1n_ref[...], d3n_ref[...])
        d_ref[0:CONV_ROWS, :] = du1
        d_ref[CONV_ROWS:ext, :] = jnp.where(i == n_tiles - 1, 0.0, du1n)
        small_ref[0:1, :] += _colsum(du1)
        small_ref[1:2, :] += _colsum(du2 * xhat)
        small_ref[2:3, :] += _colsum(du2)
        row_id = lax.broadcasted_iota(jnp.int32, (8, CONV_LANES), 0)
        for lo in range(0, D_CONV, CONV_LANES):
            lanes = slice(lo, lo + CONV_LANES)

            def emit(m, out, lanes=lanes):
                du0_ref[8 * m:8 * m + 8, lanes] = out

            _conv_ahead(d_ref, _lag_weights(w_ref, lanes), lanes, row_id, emit)
            for whole_shifts in ((0, 1), (2, 3)):
                for e, total in _conv_weight_sums(d_ref, s_ref, lanes, row_id, whole_shifts).items():
                    dw_ref[CONV_K - 1 - e:CONV_K - e, lanes] += _colsum(total)
        du0 = du0_ref[...]
        sb = _sig(b_ref[...])
        da = du0 * sb
        dbv = du0 * a_ref[...] * sb * (1.0 - sb)
        da_ref[...] = da.astype(BF16)
        db_ref[...] = dbv.astype(BF16)
        small_ref[3:4, :] += _colsum(da)
        small_ref[4:5, :] += _colsum(dbv)

    return _host_call(
        lambda ins, outs, scratch: body(*ins, *outs, *scratch), "conv_bwd", grid=(n_tiles,),
        in_specs=[_tile(CONV_ROWS, D_CONV, 0), _tile(CONV_ROWS, D_CONV, 1),
                  _prev(CONV_HALO, D_CONV, CONV_ROWS, 0), _prev(CONV_HALO, D_CONV, CONV_ROWS, 1),
                  _tile(CONV_ROWS, D_CONV), _next(CONV_HALO, D_CONV, CONV_ROWS, n_halo),
                  _tile(CONV_ROWS, D_CONV), _next(CONV_HALO, D_CONV, CONV_ROWS, n_halo),
                  _full((CONV_K, D_CONV)), _full((1, D_CONV)), _full((1, D_CONV))],
        out_specs=[_tile(CONV_ROWS, D_CONV), _tile(CONV_ROWS, D_CONV), _full((CONV_HALO, D_CONV)),
                   _full((8, D_CONV))],
        out_shape=[jax.ShapeDtypeStruct((seq, D_CONV), BF16), jax.ShapeDtypeStruct((seq, D_CONV), BF16),
                   jax.ShapeDtypeStruct((CONV_HALO, D_CONV), F32), jax.ShapeDtypeStruct((8, D_CONV), F32)],
        scratch_shapes=[pltpu.VMEM((ext, D_CONV), F32), pltpu.VMEM((ext, D_CONV), F32),
                        pltpu.VMEM((CONV_ROWS, D_CONV), F32)],
        args=[zr, zr, zr, zr, u1, u1, du3, du3, w_dw, g_ln, b_ln], comm=comm)


MERGE_ROWS = 256


def _merge_fwd(ao, u3, zr, w_ao, w_co, b_co):
    seq = ao.shape[0]

    def body(ao_ref, u3_ref, ga_ref, gb_ref, wa_ref, wc_ref, bc_ref, y_ref, a_ref, cb_ref):
        a = jnp.dot(ao_ref[...], wa_ref[...], preferred_element_type=F32)
        cb = jnp.dot(u3_ref[...], wc_ref[...], preferred_element_type=F32) + bc_ref[...]
        a_ref[...] = a
        cb_ref[...] = cb
        y_ref[...] = (_sig(ga_ref[...]) * a + _sig(gb_ref[...]) * cb).astype(BF16)

    f32_out = jax.ShapeDtypeStruct((seq, D_MODEL), F32)
    return pl.pallas_call(
        body, name="merge_fwd",
        out_shape=[jax.ShapeDtypeStruct((seq, D_MODEL), BF16), f32_out, f32_out],
        grid=(seq // MERGE_ROWS,),
        in_specs=[_tile(MERGE_ROWS, D_ATTN), _tile(MERGE_ROWS, D_CONV), _tile(MERGE_ROWS, D_MODEL, 1),
                  _tile(MERGE_ROWS, D_MODEL, 2), _full(w_ao.shape), _full(w_co.shape), _full((1, D_MODEL))],
        out_specs=[_tile(MERGE_ROWS, D_MODEL)] * 3, compiler_params=_cparams(1),
    )(ao, u3, zr, zr, w_ao, w_co, b_co)


def _merge_bwd(a, cb, zr, rows):
    seq = a.shape[0]

    def fn(dy_v, first, ins, outs):
        a_ref, cb_ref, ga_ref, gb_ref = ins
        da_ref, dcb_ref, dga_ref, dgb_ref, small_ref = outs

        @pl.when(first)
        def _():
            small_ref[...] = jnp.zeros_like(small_ref)

        sa, sb = _sig(ga_ref[...]), _sig(gb_ref[...])
        dcb = dy_v * sb
        dga = dy_v * a_ref[...] * sa * (1.0 - sa)
        dgb = dy_v * cb_ref[...] * sb * (1.0 - sb)
        da_ref[...] = (dy_v * sa).astype(BF16)
        dcb_ref[...] = dcb.astype(BF16)
        dga_ref[...] = dga.astype(BF16)
        dgb_ref[...] = dgb.astype(BF16)
        small_ref[0:1, :] += _colsum(dga)
        small_ref[1:2, :] += _colsum(dgb)
        small_ref[2:3, :] += _colsum(dcb)

    bf = jax.ShapeDtypeStruct((seq, D_MODEL), BF16)
    gate = lambda col: pl.BlockSpec((rows, D_MODEL), lambda i, j: (i, col))
    return _Epilogue(
        [a, cb, zr, zr], [_row_tile(rows, D_MODEL), _row_tile(rows, D_MODEL), gate(1), gate(2)],
        [bf, bf, bf, bf, jax.ShapeDtypeStruct((8, D_MODEL), F32)],
        [_row_tile(rows, D_MODEL)] * 4 + [_whole((8, D_MODEL))], fn, keep_product=False)


FFN_ROWS = 2048
FFN_BLOCKS = D_FF // FFN_COLS
GELU_C = math.sqrt(2.0 / math.pi)


def _gelu(v):
    t = jnp.tanh(GELU_C * (v + 0.044715 * (v * v * v)))
    return 0.5 * v * (1.0 + t), t


def _gelu_grad(v, t):
    return 0.5 * (1.0 + t) + 0.5 * v * (1.0 - t * t) * (GELU_C * (1.0 + 3.0 * 0.044715 * (v * v)))


def _sublane_rows(ref, n):
    return [jnp.broadcast_to(ref[r:r + 1, :], (8, FFN_COLS)) for r in range(n)]


def _rolls(tile, shifts):
    return tuple(pltpu.roll(tile, s, 0) for s in shifts)


def _behind(prev_rolls, cur, row_id):
    rolls = _rolls(cur, (1, 2))
    x1 = jnp.where(row_id < 1, prev_rolls[0], rolls[0])
    x2 = jnp.where(row_id < 2, prev_rolls[1], rolls[1])
    return (x2, x1, cur), rolls


def _ahead(cur_rolls, next_rolls, row_id):
    return (jnp.where(row_id < 7, cur_rolls[0], next_rolls[0]), jnp.where(row_id < 6, cur_rolls[1], next_rolls[1]))


def _conv3(taps, w, bias):
    return w[0] * taps[0] + w[1] * taps[1] + w[2] * taps[2] + bias


def _ffn_specs(rows):
    tile = lambda off: pl.BlockSpec((rows, FFN_COLS), lambda j, i: (i, j + off))
    prev = lambda off: pl.BlockSpec((FFN_HALO, FFN_COLS),
                                    lambda j, i: (jnp.maximum(i * (rows // FFN_HALO) - 1, 0), j + off))
    wgt = lambda off: pl.BlockSpec((3, FFN_COLS), lambda j, i: (0, j + off))
    vec = lambda off: pl.BlockSpec((1, FFN_COLS), lambda j, i: (0, j + off))
    return tile, prev, wgt, vec


def _ffn_act(up, w_dw, b_dw):
    seq = up.shape[0]
    tile, prev, wgt, vec = _ffn_specs(FFN_ROWS)

    def body(v_ref, g_ref, vp_ref, gp_ref, wv_ref, wg_ref, bv_ref, bg_ref, act_ref):
        first = pl.program_id(1) == 0
        row_id = lax.broadcasted_iota(jnp.int32, (8, FFN_COLS), 0)
        wv, wg = _sublane_rows(wv_ref, 3), _sublane_rows(wg_ref, 3)
        (bv,), (bg,) = _sublane_rows(bv_ref, 1), _sublane_rows(bg_ref, 1)
        rolls_v = _rolls(jnp.where(first, 0.0, vp_ref[...]), (1, 2))
        rolls_g = _rolls(jnp.where(first, 0.0, gp_ref[...]), (1, 2))
        for row in range(0, FFN_ROWS, 16):
            halves = []
            for r in (row, row + 8):
                taps_v, rolls_v = _behind(rolls_v, v_ref[r:r + 8, :], row_id)
                taps_g, rolls_g = _behind(rolls_g, g_ref[r:r + 8, :], row_id)
                halves.append(_gelu(_conv3(taps_g, wg, bg))[0] * _conv3(taps_v, wv, bv))
            act_ref[row:row + 16, :] = jnp.concatenate(halves, axis=0).astype(BF16)

    return pl.pallas_call(
        body, name="ffn_act", out_shape=jax.ShapeDtypeStruct((seq, D_FF), BF16),
        grid=(FFN_BLOCKS, seq // FFN_ROWS),
        in_specs=[tile(0), tile(FFN_BLOCKS), prev(0), prev(FFN_BLOCKS), wgt(0), wgt(FFN_BLOCKS),
                  vec(0), vec(FFN_BLOCKS)],
        out_specs=tile(0), compiler_params=_cparams(2),
    )(up, up, up, up, w_dw, w_dw, b_dw, b_dw)


def _ffn_act_bwd(up, dact, w_dw, b_dw, comm=None):
    seq = up.shape[0]
    n_tiles = seq // FFN_ROWS
    n_halo = seq // FFN_HALO
    tile, prev, wgt, vec = _ffn_specs(FFN_ROWS)
    nxt = lambda off: pl.BlockSpec(
        (FFN_HALO, FFN_COLS), lambda j, i: (jnp.minimum((i + 1) * (FFN_ROWS // FFN_HALO), n_halo - 1), j + off))
    acc = lambda off: pl.BlockSpec((8, FFN_COLS), lambda j, i: (0, j + off))

    def body(v_ref, g_ref, vp_ref, gp_ref, vn_ref, gn_ref, da_ref, dan_ref, wv_ref, wg_ref, bv_ref, bg_ref,
             dv_out, dg_out, dwv_ref, dwg_ref, dbv_ref, dbg_ref):
        i = pl.program_id(1)
        first, last = i == 0, i == n_tiles - 1

        @pl.when(first)
        def _():
            for r in (dwv_ref, dwg_ref, dbv_ref, dbg_ref):
                r[...] = jnp.zeros_like(r)

        row_id = lax.broadcasted_iota(jnp.int32, (8, FFN_COLS), 0)
        wv, wg = _sublane_rows(wv_ref, 3), _sublane_rows(wg_ref, 3)
        (bv,), (bg,) = _sublane_rows(bv_ref, 1), _sublane_rows(bg_ref, 1)
        zero = jnp.zeros((8, FFN_COLS), F32)
        sums_v, sums_g = [zero] * 4, [zero] * 4
        rolls_v = _rolls(jnp.where(first, 0.0, vp_ref[...]), (1, 2))
        rolls_g = _rolls(jnp.where(first, 0.0, gp_ref[...]), (1, 2))
        behind = None
        done_v, done_g = [], []

        def grads(v_tile, g_tile, dact, rolls_v, rolls_g):
            taps_v, rolls_v = _behind(rolls_v, v_tile, row_id)
            taps_g, rolls_g = _behind(rolls_g, g_tile, row_id)
            val, gate = _conv3(taps_v, wv, bv), _conv3(taps_g, wg, bg)
            gel, t = _gelu(gate)
            return dact * gel, dact * val * _gelu_grad(gate, t), taps_v, taps_g, rolls_v, rolls_g

        def finish(tile, nxt, row):
            for (d, d_rolls), (_, n_rolls), w, done, o_ref in ((tile[0], nxt[0], wv, done_v, dv_out),
                                                               (tile[1], nxt[1], wg, done_g, dg_out)):
                d1, d2 = _ahead(d_rolls, n_rolls, row_id)
                done.append(w[2] * d + w[1] * d1 + w[0] * d2)
                if len(done) == 2:
                    o_ref[row - 16:row, :] = jnp.concatenate(done, axis=0).astype(BF16)
                    done.clear()

        for row in range(0, FFN_ROWS, 16):
            dact16 = da_ref[row:row + 16, :].astype(F32)
            for r, dact in ((row, dact16[0:8, :]), (row + 8, dact16[8:16, :])):
                dval, dgate, taps_v, taps_g, rolls_v, rolls_g = grads(v_ref[r:r + 8, :], g_ref[r:r + 8, :], dact,
                                                                      rolls_v, rolls_g)
                sums_v = [s + dval * x for s, x in zip(sums_v, taps_v)] + [sums_v[3] + dval]
                sums_g = [s + dgate * x for s, x in zip(sums_g, taps_g)] + [sums_g[3] + dgate]
                tile = ((dval, _rolls(dval, (7, 6))), (dgate, _rolls(dgate, (7, 6))))
                if behind is not None:
                    finish(behind, tile, r)
                behind = tile
        dact_next = jnp.where(last, 0.0, dan_ref[...].astype(F32)[0:FFN_HALO, :])
        dval, dgate, *_ = grads(vn_ref[...], gn_ref[...], dact_next, rolls_v, rolls_g)
        finish(behind, ((dval, _rolls(dval, (7, 6))), (dgate, _rolls(dgate, (7, 6)))), FFN_ROWS)
        for sums, dw_ref, db_ref in ((sums_v, dwv_ref, dbv_ref), (sums_g, dwg_ref, dbg_ref)):
            for tap in range(3):
                dw_ref[tap:tap + 1, :] += _colsum(sums[tap])
            db_ref[0:1, :] += _colsum(sums[3])

    half = jax.ShapeDtypeStruct((seq, D_FF), BF16)
    acc_shape = jax.ShapeDtypeStruct((8, D_FF), F32)
    return _host_call(
        lambda ins, outs, scratch: body(*ins, *outs, *scratch), "ffn_act_bwd", grid=(FFN_BLOCKS, n_tiles),
        in_specs=[tile(0), tile(FFN_BLOCKS), prev(0), prev(FFN_BLOCKS), nxt(0), nxt(FFN_BLOCKS),
                  tile(0), pl.BlockSpec((16, FFN_COLS), lambda j, i: (
                      jnp.minimum((i + 1) * (FFN_ROWS // 16), seq // 16 - 1), j)),
                  wgt(0), wgt(FFN_BLOCKS), vec(0), vec(FFN_BLOCKS)],
        out_specs=[tile(0), tile(0), acc(0), acc(0), acc(0), acc(0)],
        out_shape=[half, half, acc_shape, acc_shape, acc_shape, acc_shape],
        scratch_shapes=[], args=[up, up, up, up, up, up, dact, dact, w_dw, w_dw, b_dw, b_dw], comm=comm)


def _cols_to_blocks(full_cols):
    k, n8 = full_cols.shape
    return jnp.transpose(full_cols.reshape(k, N_DEV, n8 // N_DEV), (1, 0, 2))


def _rows_to_blocks(full_rows):
    r8, n = full_rows.shape
    return full_rows.reshape(N_DEV, r8 // N_DEV, n)


def _blocks_to_cols(gathered):
    _, k, n = gathered.shape
    return jnp.transpose(gathered, (1, 0, 2)).reshape(k, N_DEV * n)


def kernel(x, c, w_ada, b_ada, g_pre_mix, g_post_mix, w_in, b_in, rel_bias, w_attn_o, w_dw_conv, b_dw_conv, g_conv_ln, b_conv_ln, w_conv_o, b_conv_o, w_mix_o, g_pre_ffn, g_post_ffn, w_up, w_dw_ffn, b_dw_ffn, w_down, loss_target, m_w_ada, m_b_ada, m_g_pre_mix, m_g_post_mix, m_w_in, m_b_in, m_rel_bias, m_w_attn_o, m_w_dw_conv, m_b_dw_conv, m_g_conv_ln, m_b_conv_ln, m_w_conv_o, m_b_conv_o, m_w_mix_o, m_g_pre_ffn, m_g_post_ffn, m_w_up, m_w_dw_ffn, m_b_dw_ffn, m_w_down, v_w_ada, v_b_ada, v_g_pre_mix, v_g_post_mix, v_w_in, v_b_in, v_rel_bias, v_w_attn_o, v_w_dw_conv, v_b_dw_conv, v_g_conv_ln, v_b_conv_ln, v_w_conv_o, v_b_conv_o, v_w_mix_o, v_g_pre_ffn, v_g_post_ffn, v_w_up, v_w_dw_ffn, v_b_dw_ffn, v_w_down):
    names = ["w_ada", "b_ada", "g_pre_mix", "g_post_mix", "w_in", "b_in", "rel_bias", "w_attn_o", "w_dw_conv",
             "b_dw_conv", "g_conv_ln", "b_conv_ln", "w_conv_o", "b_conv_o", "w_mix_o", "g_pre_ffn", "g_post_ffn",
             "w_up", "w_dw_ffn", "b_dw_ffn", "w_down"]
    weights = dict(zip(names, [w_ada, b_ada, g_pre_mix, g_post_mix, w_in, b_in, rel_bias, w_attn_o, w_dw_conv,
                               b_dw_conv, g_conv_ln, b_conv_ln, w_conv_o, b_conv_o, w_mix_o, g_pre_ffn,
                               g_post_ffn, w_up, w_dw_ffn, b_dw_ffn, w_down]))
    mom_m = dict(zip(names, [m_w_ada, m_b_ada, m_g_pre_mix, m_g_post_mix, m_w_in, m_b_in, m_rel_bias, m_w_attn_o,
                             m_w_dw_conv, m_b_dw_conv, m_g_conv_ln, m_b_conv_ln, m_w_conv_o, m_b_conv_o,
                             m_w_mix_o, m_g_pre_ffn, m_g_post_ffn, m_w_up, m_w_dw_ffn, m_b_dw_ffn, m_w_down]))
    mom_v = dict(zip(names, [v_w_ada, v_b_ada, v_g_pre_mix, v_g_post_mix, v_w_in, v_b_in, v_rel_bias, v_w_attn_o,
                             v_w_dw_conv, v_b_dw_conv, v_g_conv_ln, v_b_conv_ln, v_w_conv_o, v_b_conv_o,
                             v_w_mix_o, v_g_pre_ffn, v_g_post_ffn, v_w_up, v_w_dw_ffn, v_b_dw_ffn, v_w_down]))
    shapes = {n: w.shape for n, w in weights.items()}

    seq = x.shape[1]
    me = 4 * lax.axis_index("x") + 2 * lax.axis_index("y") + lax.axis_index("c")
    x2 = x.reshape(seq, D_MODEL)
    target = loss_target.reshape(seq, D_MODEL)
    sq = lambda a: a.reshape(a.shape[1:])
    bf = lambda a: sq(a).astype(BF16)

    c_act = _silu_vec(c)
    transposed = lambda a: jnp.swapaxes(sq(a), 0, 1)
    c_all, g_in, g_dwc, g_dwf = _run_comm(
        _gather_comm([c_act, transposed(w_in).astype(BF16), sq(w_dw_conv), sq(w_dw_ffn)]), "gather_first")
    c_all = c_all.reshape(N_DEV, D_MODEL)
    wt_in = g_in.reshape(g_in.shape[0] * g_in.shape[1], D_MODEL)
    wf_dwc = _blocks_to_cols(g_dwc)
    wf_dwf = _blocks_to_cols(g_dwf)

    (mod_all,) = _run_comm(_gather_comm([_ada_fwd(c_all, sq(w_ada))]), "gather_mod")
    mod = lax.dynamic_index_in_dim(mod_all, me, axis=1, keepdims=False)
    mod6 = (mod.reshape(1, 6 * D_MODEL) + b_ada).reshape(6, D_MODEL)

    h1 = _pre_mix(x2, mod6, g_pre_mix)
    qkv = _mm(h1, wt_in, "nt", BF16, "in_proj_qkv", bias=b_in, tm=1024, tn=768, cols=(0, 3 * D_ATTN))
    zr, _, (g_ao, g_co, g_mo) = _mm(h1, wt_in, "nt", F32, "in_proj_rest", bias=b_in, tm=1024, tn=3 * D_ATTN,
                                 cols=(3 * D_ATTN, 2 * D_CONV + 2 * D_MODEL),
                                 comm=_gather_comm([bf(w_attn_o), bf(w_conv_o), bf(w_mix_o)]))
    kpad = jnp.pad(qkv[:, D_ATTN:2 * D_ATTN], ((PAD_ROWS, 0), (0, 0)))
    vpad = jnp.pad(qkv[:, 2 * D_ATTN:], ((PAD_ROWS, 0), (0, 0)))
    table = jnp.transpose(_bias_table(sq(rel_bias)), (1, 0, 2))
    ao, (g_up,) = _attn_fwd(qkv, kpad, vpad, table, comm=_gather_comm([transposed(w_up).astype(BF16)]))
    (u1, u3), (g_dn,) = _conv_fwd(zr, wf_dwc, b_dw_conv, g_conv_ln, b_conv_ln, comm=_gather_comm([bf(w_down)]))
    wf_ao = _blocks_to_cols(g_ao)
    wf_co = _blocks_to_cols(g_co)
    wf_mo = g_mo.reshape(D_MODEL, D_MODEL)
    wt_up = g_up.reshape(g_up.shape[0] * g_up.shape[1], D_MODEL)
    wf_dn = g_dn.reshape(D_FF, D_MODEL)
    y, a_br, cb_br = _merge_fwd(ao, u3, zr, wf_ao, wf_co, b_conv_o)
    ymix, (x1, h2), _ = _mm(y, wf_mo, "nn", F32, "mix_o", tm=512, tn=D_MODEL,
                            epilogue=_post_mix_pre_ffn(x2, mod6, g_post_mix, g_pre_ffn, 512))
    up = _mm(h2, wt_up, "nt", F32, "ffn_up", tm=1024, tn=1408)
    act = _ffn_act(up, wf_dwf, b_dw_ffn)
    _, (loss_lanes, dout, dyf, small_f), _ = _mm(act, wf_dn, "nn", F32, "ffn_down", tm=512, tn=D_MODEL,
                                                 epilogue=_final(x1, target, mod6, g_post_ffn, 512))

    dact = _mm(dyf, wf_dn, "nt", BF16, "ffn_down_dx", tm=1024, tn=1408)
    gw_down = _mm(act, dyf, "tn", BF16, "ffn_down_dw", tm=256, tn=1024)
    (dup_v, dup_g, dwv, dwg, dbv, dbg), (parts_down,) = _ffn_act_bwd(
        up, dact, wf_dwf, b_dw_ffn, comm=_scatter_comm([_rows_to_blocks(gw_down)]))
    _, (dx1, dymix, small_m), _ = _mm([dup_v, dup_g], wt_up, "nn", F32, "ffn_up_dx", tm=256, tn=D_MODEL,
                                      epilogue=_mid_bwd(x1, dout, ymix, mod6, g_pre_ffn, g_post_mix, 256))
    blocks_up = _rows_to_blocks(_mm_tn_rows([dup_v, dup_g], h2, "ffn_up_dw"))
    _, (da, dcb, dga, dgb, small_g), _ = _mm(dymix, wf_mo, "nt", F32, "mix_o_dx", tm=512, tn=D_MODEL,
                                             epilogue=_merge_bwd(a_br, cb_br, zr, 512))
    gw_mo = _mm(y, dymix, "tn", BF16, "mix_o_dw")
    dao = _mm(da, wf_ao, "nt", BF16, "attn_o_dx", tm=1024)
    gw_ao = _mm(ao, da, "tn", BF16, "attn_o_dw")
    du3 = _mm(dcb, wf_co, "nt", F32, "conv_o_dx", tm=1024)
    gw_co = _mm(u3, dcb, "tn", BF16, "conv_o_dw")
    (dq, dkt, dvt, dbias, small_a), (parts_up,) = _attn_bwd(
        qkv, kpad, vpad, table, dao, comm=_scatter_comm([blocks_up]))
    g_rel = _bias_grad(jnp.transpose(dbias, (1, 0, 2)))
    (dglu_a, dglu_b, dw_conv, small_c), (parts_mo, parts_ao, parts_co) = _conv_bwd(
        zr, u1, du3, wf_dwc, g_conv_ln, b_conv_ln,
        comm=_scatter_comm([_rows_to_blocks(gw_mo), _cols_to_blocks(gw_ao), _cols_to_blocks(gw_co)]))
    dz = _assemble_dz(dq, dkt, dvt, dglu_a, dglu_b, dga, dgb)
    blocks_in = _rows_to_blocks(_mm(dz, h1, "tn", BF16, "in_proj_dw", tm=512, tn=D_MODEL))
    _, (grad_x, small_x), (parts_in,) = _mm(dz, wt_in, "nn", F32, "in_proj_dx", tm=512, tn=D_MODEL,
                                            comm=_scatter_comm([blocks_in]),
                                            epilogue=_pre_mix_bwd(x2, dx1, mod6, g_pre_mix, 512))

    packed = _pack_grads(small_x, small_m, small_f, small_g, small_a, small_c, dbv, dbg, dwv, dwg, dw_conv)
    gathered, gathered_rel, gathered_loss = _run_comm(_gather_comm([packed, g_rel, loss_lanes]), "gather_small")
    gathered = gathered.reshape(N_DEV, PACKED_TOTAL)
    updates, g_dwc_full, g_dwf_full, loss_all = _small_adamw(gathered, gathered_rel, gathered_loss, weights, mom_m,
                                                             mom_v)
    loss = loss_all[0, 0]

    grads, deltas, new_m, new_v = {}, {}, {}, {}

    def record(name, update, is_transposed=False):
        for dst, val in zip((grads, deltas, new_m, new_v), update):
            dst[name] = (jnp.swapaxes(val, 0, 1) if is_transposed else val).reshape(shapes[name])

    for name, update in updates.items():
        record(name, update)

    def local_update(name, grad):
        record(name, _adamw(sq(weights[name]), sq(mom_m[name]), sq(mom_v[name]), "adamw_" + name, g=grad))

    conv_cols, ffn_cols, ada_cols = D_CONV // N_DEV, 2 * D_FF // N_DEV, 6 * D_MODEL // N_DEV
    local_update("w_dw_conv", lax.dynamic_slice(g_dwc_full, (0, me * conv_cols), (CONV_K, conv_cols)))
    local_update("w_dw_ffn", lax.dynamic_slice(g_dwf_full, (0, me * ffn_cols), (3, ffn_cols)))
    local_update("w_ada", _ada_grad(c_all, lax.dynamic_slice(gathered, (0, me * ada_cols), (N_DEV, ada_cols))))

    for name, part in (("w_attn_o", parts_ao), ("w_conv_o", parts_co), ("w_mix_o", parts_mo), ("w_down", parts_down)):
        record(name, _adamw(sq(weights[name]), sq(mom_m[name]), sq(mom_v[name]), "adamw_" + name, parts=part))
    for name, part in (("w_in", parts_in), ("w_up", parts_up)):
        record(name, _adamw(transposed(weights[name]), transposed(mom_m[name]), transposed(mom_v[name]),
                            "adamw_" + name, parts=part), is_transposed=True)

    return (loss, grad_x.reshape(x.shape), *[grads[n] for n in names], *[deltas[n] for n in names],
            *[new_m[n] for n in names], *[new_v[n] for n in names])
```

```python
import functools
import math

import jax
import jax.numpy as jnp
from jax import lax
from jax.experimental import pallas as pl
from jax.experimental.pallas import tpu as pltpu

F32 = jnp.float32
BF16 = jnp.bfloat16
HIGHEST = lax.Precision.HIGHEST

D_MODEL = 1024
CHUNK = 64
LEFT_CHUNKS = 8
BAND = (LEFT_CHUNKS + 1) * CHUNK
PAD_ROWS = LEFT_CHUNKS * CHUNK
GROUP = 4
GROUP_Q = GROUP * CHUNK
GROUP_K = GROUP_Q + PAD_ROWS
SOFTMAX_ROWS = 16
TOEPLITZ = 640
N_HEADS = 8
HEAD_DIM = 64
D_ATTN = 512
D_CONV = 512
CONV_K = 31
CONV_HALO = 32
MAX_REL = 128
N_REL = 2 * MAX_REL + 1
D_FF = 2816
FFN_HALO = 8
FFN_COLS = 256
EPS = 1e-6
NEG_INF = -1e30
N_DEV = 8

ADAM_LR = 0.001
ADAM_B1 = 0.9
ADAM_B2 = 0.999
ADAM_EPS = 1e-08
ADAM_WD = 0.01
ADAM_STEP = 10

VMEM_LIMIT_BYTES = 56 * 1024 * 1024
ADAMW_BLOCK_BYTES = 768 * 1024

MESH = pl.DeviceIdType.MESH
ANY = pl.BlockSpec(memory_space=pl.ANY)

SH_M, SC_M, GT_M, SH_F, SC_F, GT_F = range(6)

SMALL = (("b_ada", 6144), ("g_pre_mix", 1024), ("g_post_mix", 1024), ("b_in", 4608), ("b_dw_conv", 512),
         ("g_conv_ln", 512), ("b_conv_ln", 512), ("b_conv_o", 1024), ("g_pre_ffn", 1024), ("g_post_ffn", 1024),
         ("b_dw_ffn", 5632))
PACKED_TOTAL = sum(n for _, n in SMALL) + CONV_K * D_CONV + 3 * 2 * D_FF


def _cparams(n_axes):
    return pltpu.CompilerParams(vmem_limit_bytes=VMEM_LIMIT_BYTES,
                                dimension_semantics=("arbitrary",) * n_axes)


def _sig(v):
    return 1.0 / (1.0 + jnp.exp(-v))


def _pick(n, target):
    if n <= target:
        return n
    t = target - target % 128
    while n % t:
        t -= 128
    return t


def _tile(rows, cols, col=0):
    return pl.BlockSpec((rows, cols), lambda i: (i, col))


def _full(shape):
    zeros = (0,) * len(shape)
    return pl.BlockSpec(shape, lambda i: zeros)


def _prev(halo, cols, rows, col=0):
    return pl.BlockSpec((halo, cols), lambda i: (jnp.maximum(i * (rows // halo) - 1, 0), col))


def _next(halo, cols, rows, n_blocks, col=0):
    return pl.BlockSpec((halo, cols), lambda i: (jnp.minimum((i + 1) * (rows // halo), n_blocks - 1), col))


class _Comm:
    def __init__(self, inputs, out_shapes, sems, start, finish, relay=None):
        self.inputs, self.out_shapes, self.sems, self.start, self.finish = inputs, out_shapes, sems, start, finish
        self.relay = relay


def _host_call(body, name, grid, in_specs, out_specs, out_shape, scratch_shapes, args, comm=None):
    n_in, n_out, n_scr = len(args), len(out_shape), len(scratch_shapes)
    c_in = list(comm.inputs) if comm else []
    c_out = list(comm.out_shapes) if comm else []
    c_sem = list(comm.sems) if comm else []

    def full(*refs):
        bounds = [0, n_in, len(c_in), n_out, len(c_out), n_scr, len(c_sem)]
        cuts = [sum(bounds[:i + 1]) for i in range(len(bounds))]
        ins, cins, outs, couts, scr, csems = (refs[lo:hi] for lo, hi in zip(cuts[:-1], cuts[1:]))
        if comm:
            first = functools.reduce(jnp.logical_and, [pl.program_id(ax) == 0 for ax in range(len(grid))])
            pl.when(first)(lambda: comm.start(cins, couts, csems))
            last = functools.reduce(jnp.logical_and, [pl.program_id(ax) == grid[ax] - 1 for ax in range(len(grid))])
            if comm.relay is not None:
                pl.when(last)(lambda: comm.relay(cins, couts, csems))
        body(ins, outs, scr)
        if comm:
            pl.when(last)(lambda: comm.finish(cins, couts, csems))

    res = pl.pallas_call(
        full, name=name, grid=grid, in_specs=list(in_specs) + [ANY] * len(c_in),
        out_specs=list(out_specs) + [ANY] * len(c_out), out_shape=list(out_shape) + c_out,
        scratch_shapes=list(scratch_shapes) + c_sem, compiler_params=_cparams(len(grid)),
    )(*args, *c_in)
    return list(res[:n_out]), list(res[n_out:])


def _run_comm(comm, name):
    n_in, n_out = len(comm.inputs), len(comm.out_shapes)

    def body(*refs):
        ins, outs, sems = refs[:n_in], refs[n_in:n_in + n_out], refs[n_in + n_out:]
        comm.start(ins, outs, sems)
        if comm.relay is not None:
            comm.relay(ins, outs, sems)
        comm.finish(ins, outs, sems)

    return pl.pallas_call(
        body, name=name, out_shape=list(comm.out_shapes), in_specs=[ANY] * n_in, out_specs=[ANY] * n_out,
        scratch_shapes=list(comm.sems),
    )(*comm.inputs)


def _place():
    return lax.axis_index("x"), lax.axis_index("y"), lax.axis_index("c")


def _gather_comm(arrs):
    n = len(arrs)

    def plan(ins, outs, sems):
        send_sems, recv_sems, local_sems = sems
        x, y, c = _place()
        me, sibling = (x, y, c), (x, y, 1 - c)
        chips = [(1 - x, y), (x, 1 - y), (1 - x, 1 - y)]

        def block(k, p):
            return outs[k].at[4 * p[0] + 2 * p[1] + p[2]]

        def copy(k, s, blk, to, src=None):
            return pltpu.make_async_remote_copy(
                src_ref=block(k, blk) if src is None else src, dst_ref=block(k, blk),
                send_sem=send_sems.at[7 * k + s], recv_sem=recv_sems.at[7 * k + s],
                device_id=to, device_id_type=MESH)

        mine = [pltpu.make_async_copy(ins[k], block(k, me), local_sems.at[k]) for k in range(n)]
        first = []
        for k in range(n):
            first.append(copy(k, 0, me, sibling, src=ins[k]))
            for j, chip in enumerate(chips):
                first.append(copy(k, 1 + j, me, (*chip, c), src=ins[k]))
        return me, sibling, chips, c, copy, mine, first

    def start(ins, outs, sems):
        *_, mine, first = plan(ins, outs, sems)
        for cp in mine + first:
            cp.start()

    def relay(ins, outs, sems):
        me, sibling, chips, c, copy, _, _ = plan(ins, outs, sems)
        for j, chip in enumerate(chips):
            for k in range(n):
                copy(k, 1 + j, (*chip, c), me).wait_recv()
                copy(k, 4 + j, (*chip, c), sibling).start()

    def finish(ins, outs, sems):
        me, sibling, chips, c, copy, mine, first = plan(ins, outs, sems)
        passed = [copy(k, 4 + j, (*chip, c), sibling) for j, chip in enumerate(chips) for k in range(n)]
        for k in range(n):
            copy(k, 0, sibling, me).wait_recv()
        for j, chip in enumerate(chips):
            for k in range(n):
                copy(k, 4 + j, (*chip, 1 - c), me).wait_recv()
        for cp in first + passed:
            cp.wait_send()
        for cp in mine:
            cp.wait()

    return _Comm(list(arrs), [jax.ShapeDtypeStruct((N_DEV,) + a.shape, a.dtype) for a in arrs],
                 [pltpu.SemaphoreType.DMA((7 * n,)), pltpu.SemaphoreType.DMA((7 * n,)),
                  pltpu.SemaphoreType.DMA((n,))], start, finish, relay)


def _scatter_comm(blocks):
    n = len(blocks)

    def plan(ins, outs, sems, arrivals):
        send_sems, recv_sems, local_sems = sems
        x, y, c = _place()
        me = 4 * x + 2 * y + c
        local = [pltpu.make_async_copy(ins[k].at[me], outs[k].at[me], local_sems.at[k]) for k in range(n)]
        sends, recvs = [], []
        for k in range(n):
            for mask in range(1, N_DEV):
                px = 1 - x if mask & 4 else x
                py = 1 - y if mask & 2 else y
                pc = 1 - c if mask & 1 else c
                peer = 4 * px + 2 * py + pc
                sem = 7 * k + mask - 1
                both = dict(send_sem=send_sems.at[sem], recv_sem=recv_sems.at[sem], device_id=(px, py, pc),
                            device_id_type=MESH)
                sends.append(pltpu.make_async_remote_copy(src_ref=ins[k].at[peer], dst_ref=outs[k].at[me], **both))
                if arrivals:
                    recvs.append(pltpu.make_async_remote_copy(src_ref=ins[k].at[me], dst_ref=outs[k].at[peer],
                                                              **both))
        return local, sends, recvs

    def start(ins, outs, sems):
        local, sends, _ = plan(ins, outs, sems, arrivals=False)
        for cp in local + sends:
            cp.start()

    def finish(ins, outs, sems):
        local, sends, recvs = plan(ins, outs, sems, arrivals=True)
        for cp in recvs:
            cp.wait_recv()
        for cp in sends:
            cp.wait_send()
        for cp in local:
            cp.wait()

    return _Comm(list(blocks), [jax.ShapeDtypeStruct(b.shape, b.dtype) for b in blocks],
                 [pltpu.SemaphoreType.DMA((7 * n,)), pltpu.SemaphoreType.DMA((7 * n,)),
                  pltpu.SemaphoreType.DMA((n,))], start, finish)


_DIMS = {"nn": (((1,), (0,)), ((), ())), "nt": (((1,), (1,)), ((), ())), "tn": (((0,), (0,)), ((), ()))}


class _Epilogue:
    def __init__(self, args, in_specs, out_shapes, out_specs, fn, keep_product):
        self.args, self.in_specs, self.out_shapes, self.out_specs = args, in_specs, out_shapes, out_specs
        self.fn, self.keep_product = fn, keep_product


def _row_tile(rows, cols):
    return pl.BlockSpec((rows, cols), lambda i, j: (i, 0))


def _whole(shape):
    zeros = (0,) * len(shape)
    return pl.BlockSpec(shape, lambda i, j: zeros)


def _mm(a, b, mode, out_dtype, name, bias=None, tm=512, tn=512, comm=None, cols=None, epilogue=None):
    pieces = a if isinstance(a, (list, tuple)) else [a]
    assert all(p.dtype == BF16 for p in pieces) and b.dtype == BF16
    a = pieces[0]
    if mode == "tn":
        k_dim, m_dim = a.shape
    else:
        m_dim, k_dim = a.shape
    n_dim = b.shape[0] if mode == "nt" else b.shape[1]
    col0 = 0
    if cols is not None:
        assert mode != "tn" and cols[0] % tn == 0 and cols[1] % tn == 0
        col0, n_dim = cols[0] // tn, cols[1]
    tm, tn = _pick(m_dim, tm), _pick(n_dim, tn)
    assert mode != "tn" or len(pieces) == 1
    a_specs = [pl.BlockSpec((k_dim, tm), lambda i, j: (0, i)) if mode == "tn"
               else pl.BlockSpec((tm, k_dim), lambda i, j: (i, 0))] * len(pieces)
    if mode == "nt":
        b_specs = [pl.BlockSpec((tn, k_dim), lambda i, j, p=p: (j + col0, p)) for p in range(len(pieces))]
    else:
        b_specs = [pl.BlockSpec((k_dim, tn), lambda i, j, p=p: (p, j + col0)) for p in range(len(pieces))]
    in_specs = a_specs + b_specs
    args = list(pieces) + [b] * len(pieces)
    if bias is not None:
        in_specs.append(pl.BlockSpec((1, tn), lambda i, j: (0, j + col0)))
        args.append(bias)
    dims = _DIMS[mode]
    n_pieces = len(pieces)
    n_own = len(args)
    keep = epilogue is None or epilogue.keep_product
    out_specs = [pl.BlockSpec((tm, tn), lambda i, j: (i, j))] if keep else []
    out_shape = [jax.ShapeDtypeStruct((m_dim, n_dim), out_dtype)] if keep else []
    if epilogue is not None:
        assert tn == n_dim
        in_specs, args = in_specs + list(epilogue.in_specs), args + list(epilogue.args)
        out_specs, out_shape = out_specs + list(epilogue.out_specs), out_shape + list(epilogue.out_shapes)

    def body(ins, outs, scratch):
        total = lax.dot_general(ins[0][...], ins[n_pieces][...], dims, preferred_element_type=F32)
        for p in range(1, n_pieces):
            total = total + lax.dot_general(ins[p][...], ins[n_pieces + p][...], dims, preferred_element_type=F32)
        if bias is not None:
            total = total + ins[2 * n_pieces][...]
        if keep:
            outs[0][...] = total.astype(out_dtype)
        if epilogue is not None:
            epilogue.fn(total, pl.program_id(0) == 0, ins[n_own:], outs[1:] if keep else outs)

    outs, extra = _host_call(body, name, grid=(m_dim // tm, n_dim // tn), in_specs=in_specs, out_specs=out_specs,
                             out_shape=out_shape, scratch_shapes=[], args=args, comm=comm)
    product = outs[0] if keep else None
    if comm is None and epilogue is None:
        return product
    return product, outs[1:] if keep else outs, extra


def _mm_tn_rows(pieces, b, name, tm=256):
    k_dim, n_dim = b.shape
    counts = [p.shape[1] // tm for p in pieces]
    assert all(p.shape[1] % tm == 0 for p in pieces)
    firsts = [sum(counts[:q]) for q in range(len(pieces))]

    def a_spec(first, count):
        return pl.BlockSpec((k_dim, tm), lambda i: (0, jnp.clip(i - first, 0, count - 1)))

    def body(ins, outs, scratch):
        i = pl.program_id(0)
        for a_ref, first, count in zip(ins[:-1], firsts, counts):
            @pl.when(jnp.logical_and(i >= first, i < first + count))
            def _(a_ref=a_ref):
                outs[0][...] = lax.dot_general(a_ref[...], ins[-1][...], _DIMS["tn"],
                                               preferred_element_type=F32).astype(BF16)

    (out,), _ = _host_call(
        body, name, grid=(sum(counts),),
        in_specs=[a_spec(f, c) for f, c in zip(firsts, counts)] + [_full((k_dim, n_dim))],
        out_specs=[_tile(tm, n_dim)], out_shape=[jax.ShapeDtypeStruct((sum(counts) * tm, n_dim), BF16)],
        scratch_shapes=[], args=list(pieces) + [b])
    return out


def _adam_math(w, g, m, v):
    m = ADAM_B1 * m + (1.0 - ADAM_B1) * g
    v = ADAM_B2 * v + (1.0 - ADAM_B2) * (g * g)
    m_hat = m / (1.0 - ADAM_B1 ** ADAM_STEP)
    v_hat = v / (1.0 - ADAM_B2 ** ADAM_STEP)
    delta = -ADAM_LR * (m_hat / (jnp.sqrt(v_hat) + ADAM_EPS) + ADAM_WD * w)
    return delta, m, v


def _adamw(w, m, v, name, g=None, parts=None):
    rows, cols = w.shape
    tr = rows
    if rows * cols * 4 > ADAMW_BLOCK_BYTES:
        tr = max(t for t in range(16, rows, 16) if rows % t == 0 and t * cols * 4 <= ADAMW_BLOCK_BYTES)

    def body(w_ref, m_ref, v_ref, g_ref, go_ref, d_ref, mo_ref, vo_ref):
        if parts is None:
            grad = g_ref[...]
        else:
            grad = g_ref[0].astype(F32)
            for d in range(1, N_DEV):
                grad = grad + g_ref[d].astype(F32)
        delta, m_new, v_new = _adam_math(w_ref[...], grad, m_ref[...], v_ref[...])
        go_ref[...] = grad
        d_ref[...] = delta
        mo_ref[...] = m_new
        vo_ref[...] = v_new

    spec = _tile(tr, cols)
    g_spec = spec if parts is None else pl.BlockSpec((N_DEV, tr, cols), lambda i: (0, i, 0))
    shape = jax.ShapeDtypeStruct((rows, cols), F32)
    return pl.pallas_call(
        body, name=name, out_shape=[shape] * 4, grid=(rows // tr,),
        in_specs=[spec, spec, spec, g_spec], out_specs=[spec] * 4, compiler_params=_cparams(1),
    )(w, m, v, g if parts is None else parts)


def _pack_grads(small_x, small_m, small_f, small_g, small_a, small_c, dbv, dbg, dwv, dwg, dw_conv):
    pieces = [
        (small_x, 2, D_MODEL), (small_x, 1, D_MODEL), (small_m, 4, D_MODEL), (small_m, 2, D_MODEL),
        (small_m, 1, D_MODEL), (small_f, 1, D_MODEL),
        (small_x, 0, D_MODEL), (small_m, 3, D_MODEL),
        (small_a, 0, D_ATTN), (small_a, 1, D_ATTN), (small_a, 2, D_ATTN), (small_c, 3, D_CONV),
        (small_c, 4, D_CONV), (small_g, 0, D_MODEL), (small_g, 1, D_MODEL),
        (small_c, 0, D_CONV), (small_c, 1, D_CONV), (small_c, 2, D_CONV),
        (small_g, 2, D_MODEL), (small_m, 0, D_MODEL), (small_f, 0, D_MODEL),
        (dbv, 0, D_FF), (dbg, 0, D_FF),
    ]
    pieces += [(dw_conv, j, D_CONV) for j in range(CONV_K)]
    pieces += [(src, tap, D_FF) for tap in range(3) for src in (dwv, dwg)]
    sources = [small_x, small_m, small_f, small_g, small_a, small_c, dbv, dbg, dwv, dwg, dw_conv]
    assert sum(width for _, _, width in pieces) == PACKED_TOTAL

    def body(*refs):
        o_ref = refs[-1]
        ref_of = {id(src): ref for src, ref in zip(sources, refs)}
        off = 0
        for src, row, width in pieces:
            o_ref[:, off:off + width] = ref_of[id(src)][row:row + 1, :]
            off += width

    return pl.pallas_call(body, name="pack_grads", out_shape=jax.ShapeDtypeStruct((1, PACKED_TOTAL), F32))(*sources)


def _small_adamw(gathered, gathered_rel, gathered_loss, weights, mom_m, mom_v):
    vec_names = [name for name, _ in SMALL]
    states = []
    for name in vec_names + ["rel_bias"]:
        states += [weights[name], mom_m[name], mom_v[name]]
    states = [a.reshape(a.shape[1:]) if a.ndim == 3 else a for a in states]
    n_state = len(states)

    def body(*refs):
        g_ref, rel_ref, loss_ref = refs[0], refs[1], refs[2]
        state_refs, out_refs = refs[3:3 + n_state], refs[3 + n_state:]
        total = g_ref[0:1, :]
        rel = rel_ref[0]
        loss = loss_ref[0]
        for d in range(1, N_DEV):
            total = total + g_ref[d:d + 1, :]
            rel = rel + rel_ref[d]
            loss = loss + loss_ref[d]
        off = 0
        for n, (name, width) in enumerate(SMALL):
            grad = total[:, off:off + width]
            w_ref, m_ref, v_ref = state_refs[3 * n:3 * n + 3]
            for ref, val in zip(out_refs[4 * n:4 * n + 4], (grad,) + _adam_math(w_ref[...], grad, m_ref[...], v_ref[...])):
                ref[...] = val
            off += width
        n = len(SMALL)
        w_ref, m_ref, v_ref = state_refs[3 * n:3 * n + 3]
        for ref, val in zip(out_refs[4 * n:4 * n + 4], (rel,) + _adam_math(w_ref[...], rel, m_ref[...], v_ref[...])):
            ref[...] = val
        dwc_ref, dwf_ref, loss_out = out_refs[4 * n + 4:]
        loss_out[...] = 0.5 * loss
        dwc_ref[...] = jnp.zeros_like(dwc_ref)
        dwf_ref[...] = jnp.zeros_like(dwf_ref)
        for j in range(CONV_K):
            dwc_ref[j:j + 1, :] = total[:, off:off + D_CONV]
            off += D_CONV
        for tap in range(3):
            dwf_ref[tap:tap + 1, :] = total[:, off:off + 2 * D_FF]
            off += 2 * D_FF

    out_shape = []
    for k in range(n_state // 3):
        out_shape += [jax.ShapeDtypeStruct(states[3 * k].shape, F32)] * 4
    out_shape += [jax.ShapeDtypeStruct((CONV_HALO, D_CONV), F32), jax.ShapeDtypeStruct((8, 2 * D_FF), F32),
                  jax.ShapeDtypeStruct((1, 128), F32)]
    res = pl.pallas_call(
        body, name="small_adamw", out_shape=out_shape,
        compiler_params=pltpu.CompilerParams(vmem_limit_bytes=VMEM_LIMIT_BYTES),
    )(gathered, gathered_rel, gathered_loss, *states)
    updates = {name: tuple(res[4 * n:4 * n + 4]) for n, name in enumerate(vec_names + ["rel_bias"])}
    return updates, res[-3], res[-2], res[-1]


def _silu_vec(c):
    def body(c_ref, o_ref):
        v = c_ref[...]
        o_ref[...] = v * _sig(v)

    return pl.pallas_call(body, name="silu_c", out_shape=jax.ShapeDtypeStruct(c.shape, F32))(c)


def _ada_fwd(c_all, w_shard):
    def body(c_ref, w_ref, o_ref):
        o_ref[...] = jnp.dot(c_ref[...], w_ref[...], precision=HIGHEST, preferred_element_type=F32)

    return pl.pallas_call(
        body, name="ada_fwd", out_shape=jax.ShapeDtypeStruct((N_DEV, w_shard.shape[1]), F32),
        compiler_params=pltpu.CompilerParams(vmem_limit_bytes=VMEM_LIMIT_BYTES),
    )(c_all, w_shard)


def _ada_grad(c_all, dmod_shard):
    def body(c_ref, d_ref, o_ref):
        o_ref[...] = lax.dot_general(c_ref[...], d_ref[...], _DIMS["tn"], precision=HIGHEST,
                                     preferred_element_type=F32)

    return pl.pallas_call(
        body, name="ada_grad", out_shape=jax.ShapeDtypeStruct((D_MODEL, dmod_shard.shape[1]), F32),
        compiler_params=pltpu.CompilerParams(vmem_limit_bytes=VMEM_LIMIT_BYTES),
    )(c_all, dmod_shard)


ROWS = 256


def _rms(v):
    r = lax.rsqrt(jnp.mean(v * v, axis=-1, keepdims=True) + EPS)
    return v * r, r


def _rms_bwd(dxn, xn, r):
    return r * (dxn - xn * jnp.mean(dxn * xn, axis=-1, keepdims=True))


def _colsum(v):
    return jnp.sum(v, axis=0, keepdims=True)


def _pre_mix(x, mod6, g1, comm=None):
    seq = x.shape[0]

    def body(ins, outs, scratch):
        x_ref, mod_ref, g_ref = ins
        xn, _ = _rms(x_ref[...])
        y = xn * g_ref[...]
        outs[0][...] = (y * (1.0 + mod_ref[SC_M:SC_M + 1, :]) + mod_ref[SH_M:SH_M + 1, :]).astype(BF16)

    (h,), extra = _host_call(
        body, "pre_mix", grid=(seq // ROWS,),
        in_specs=[_tile(ROWS, D_MODEL), _full((6, D_MODEL)), _full((1, D_MODEL))], out_specs=[_tile(ROWS, D_MODEL)],
        out_shape=[jax.ShapeDtypeStruct((seq, D_MODEL), BF16)], scratch_shapes=[], args=[x, mod6, g1], comm=comm)
    return h, extra


def _post_mix_pre_ffn(x, mod6, g2, g3, rows):
    seq = x.shape[0]

    def fn(y, first, ins, outs):
        x_ref, mod_ref, g2_ref, g3_ref = ins
        x1_ref, h_ref = outs
        yn, _ = _rms(y)
        x1 = x_ref[...] + mod_ref[GT_M:GT_M + 1, :] * (yn * g2_ref[...])
        x1_ref[...] = x1
        xn, _ = _rms(x1)
        y3 = xn * g3_ref[...]
        h_ref[...] = (y3 * (1.0 + mod_ref[SC_F:SC_F + 1, :]) + mod_ref[SH_F:SH_F + 1, :]).astype(BF16)

    return _Epilogue(
        [x, mod6, g2, g3], [_row_tile(rows, D_MODEL), _whole((6, D_MODEL)), _whole((1, D_MODEL)), _whole((1, D_MODEL))],
        [jax.ShapeDtypeStruct((seq, D_MODEL), F32), jax.ShapeDtypeStruct((seq, D_MODEL), BF16)],
        [_row_tile(rows, D_MODEL), _row_tile(rows, D_MODEL)], fn, keep_product=True)


def _final(x1, target, mod6, g4, rows):
    seq = x1.shape[0]

    def fn(y, first, ins, outs):
        x1_ref, t_ref, mod_ref, g_ref = ins
        loss_ref, dout_ref, dyf_ref, small_ref = outs

        @pl.when(first)
        def _():
            loss_ref[...] = jnp.zeros_like(loss_ref)
            small_ref[...] = jnp.zeros_like(small_ref)

        gt = mod_ref[GT_F:GT_F + 1, :]
        g4v = g_ref[...]
        yn, r = _rms(y)
        out = x1_ref[...] + gt * (yn * g4v)
        err = out - t_ref[...]
        loss_ref[...] += jnp.sum(jnp.mean(err * err, axis=-1, keepdims=True))
        dout = err * (1.0 / D_MODEL)
        dout_ref[...] = dout
        small_ref[0:1, :] += _colsum(dout * gt * yn)
        small_ref[1:2, :] += _colsum(dout * (yn * g4v))
        dyf_ref[...] = _rms_bwd(dout * gt * g4v, yn, r).astype(BF16)

    return _Epilogue(
        [x1, target, mod6, g4],
        [_row_tile(rows, D_MODEL), _row_tile(rows, D_MODEL), _whole((6, D_MODEL)), _whole((1, D_MODEL))],
        [jax.ShapeDtypeStruct((1, 128), F32), jax.ShapeDtypeStruct((seq, D_MODEL), F32),
         jax.ShapeDtypeStruct((seq, D_MODEL), BF16), jax.ShapeDtypeStruct((8, D_MODEL), F32)],
        [_whole((1, 128)), _row_tile(rows, D_MODEL), _row_tile(rows, D_MODEL), _whole((8, D_MODEL))],
        fn, keep_product=False)


def _mid_bwd(x1, dout, ymix, mod6, g3, g2, rows):
    seq = x1.shape[0]

    def fn(dh, first, ins, outs):
        x1_ref, dout_ref, y_ref, mod_ref, g3_ref, g2_ref = ins
        dx1_ref, dy_ref, small_ref = outs

        @pl.when(first)
        def _():
            small_ref[...] = jnp.zeros_like(small_ref)

        g3v, g2v = g3_ref[...], g2_ref[...]
        xn, r3 = _rms(x1_ref[...])
        y3 = xn * g3v
        dy3 = dh * (1.0 + mod_ref[SC_F:SC_F + 1, :])
        small_ref[0:1, :] += _colsum(dy3 * xn)
        small_ref[1:2, :] += _colsum(dh * y3)
        small_ref[2:3, :] += _colsum(dh)
        dx1 = dout_ref[...] + _rms_bwd(dy3 * g3v, xn, r3)
        dx1_ref[...] = dx1
        gt = mod_ref[GT_M:GT_M + 1, :]
        yn, r2 = _rms(y_ref[...])
        small_ref[3:4, :] += _colsum(dx1 * gt * yn)
        small_ref[4:5, :] += _colsum(dx1 * (yn * g2v))
        dy_ref[...] = _rms_bwd(dx1 * gt * g2v, yn, r2).astype(BF16)

    return _Epilogue(
        [x1, dout, ymix, mod6, g3, g2],
        [_row_tile(rows, D_MODEL)] * 3 + [_whole((6, D_MODEL)), _whole((1, D_MODEL)), _whole((1, D_MODEL))],
        [jax.ShapeDtypeStruct((seq, D_MODEL), F32), jax.ShapeDtypeStruct((seq, D_MODEL), BF16),
         jax.ShapeDtypeStruct((8, D_MODEL), F32)],
        [_row_tile(rows, D_MODEL), _row_tile(rows, D_MODEL), _whole((8, D_MODEL))], fn, keep_product=False)


def _pre_mix_bwd(x, dx1, mod6, g1, rows):
    seq = x.shape[0]

    def fn(dh, first, ins, outs):
        x_ref, dx1_ref, mod_ref, g_ref = ins
        dx_ref, small_ref = outs

        @pl.when(first)
        def _():
            small_ref[...] = jnp.zeros_like(small_ref)

        g1v = g_ref[...]
        xn, r = _rms(x_ref[...])
        dy = dh * (1.0 + mod_ref[SC_M:SC_M + 1, :])
        small_ref[0:1, :] += _colsum(dy * xn)
        small_ref[1:2, :] += _colsum(dh * (xn * g1v))
        small_ref[2:3, :] += _colsum(dh)
        dx_ref[...] = dx1_ref[...] + _rms_bwd(dy * g1v, xn, r)

    return _Epilogue(
        [x, dx1, mod6, g1],
        [_row_tile(rows, D_MODEL), _row_tile(rows, D_MODEL), _whole((6, D_MODEL)), _whole((1, D_MODEL))],
        [jax.ShapeDtypeStruct((seq, D_MODEL), F32), jax.ShapeDtypeStruct((8, D_MODEL), F32)],
        [_row_tile(rows, D_MODEL), _whole((8, D_MODEL))], fn, keep_product=False)


def _toeplitz_onehot(shape, offset_axis, top):
    m = lax.broadcasted_iota(jnp.int32, shape, offset_axis)
    i = lax.broadcasted_iota(jnp.int32, shape, 1 - offset_axis)
    return (i == jnp.clip(top - m, -MAX_REL, MAX_REL) + MAX_REL).astype(F32)


def _bias_table(rel_bias):
    width = GROUP_Q + GROUP_K

    def body(rb_ref, o_ref, t_ref):
        t_ref[...] = jnp.dot(rb_ref[...], _toeplitz_onehot((N_REL, width), 1, GROUP_K - 1), precision=HIGHEST,
                             preferred_element_type=F32)
        lane = lax.broadcasted_iota(jnp.int32, (N_HEADS, GROUP_K), 1)
        for r in range(GROUP_Q):
            first_key = (r // CHUNK) * CHUNK
            band = jnp.logical_and(lane >= first_key, lane < first_key + BAND)
            o_ref[r] = jnp.where(band, t_ref[:, GROUP_Q - 1 - r:GROUP_Q - 1 - r + GROUP_K], NEG_INF)

    return pl.pallas_call(
        body, name="bias_table", out_shape=jax.ShapeDtypeStruct((GROUP_Q, N_HEADS, GROUP_K), F32),
        scratch_shapes=[pltpu.VMEM((N_HEADS, width), F32)],
    )(rel_bias)


def _bias_grad(dbias_q):
    def body(d_ref, o_ref, t_ref):
        t_ref[...] = jnp.zeros_like(t_ref)
        for qi in range(CHUNK):
            t_ref[:, CHUNK - 1 - qi:CHUNK - 1 - qi + BAND] += d_ref[qi]
        o_ref[...] = jnp.dot(t_ref[...], _toeplitz_onehot((TOEPLITZ, N_REL), 0, BAND - 1), precision=HIGHEST,
                             preferred_element_type=F32)

    return pl.pallas_call(
        body, name="bias_grad", out_shape=jax.ShapeDtypeStruct((N_HEADS, N_REL), F32),
        scratch_shapes=[pltpu.VMEM((N_HEADS, TOEPLITZ), F32)],
    )(dbias_q)


def _load_resident(pairs, sems):
    copies = [pltpu.make_async_copy(src, dst, sems.at[n]) for n, (src, dst) in enumerate(pairs)]
    for cp in copies:
        cp.start()
    for cp in copies:
        cp.wait()


def _softmax_rows(s_ref, t_ref, valid, rows):
    s = s_ref[rows, :] * (HEAD_DIM ** -0.5) + t_ref[rows, :]
    s = jnp.where(valid, s, NEG_INF)
    e = jnp.exp(s - jnp.max(s, axis=-1, keepdims=True))
    return e / jnp.sum(e, axis=-1, keepdims=True)


def _valid_keys(g):
    kj = lax.broadcasted_iota(jnp.int32, (SOFTMAX_ROWS, GROUP_K), 1)
    return kj >= PAD_ROWS - g * GROUP_Q


def _attn_fwd(qkv, kpad, vpad, table, comm=None):
    seq = qkv.shape[0]

    def body(ins, outs, scratch):
        q_ref, k_hbm, v_hbm, t_hbm = ins
        (o_ref,) = outs
        k_ref, v_ref, t_ref, s_ref, p_ref, sems = scratch
        g = pl.program_id(0)

        @pl.when(g == 0)
        def _():
            _load_resident(((k_hbm, k_ref), (v_hbm, v_ref), (t_hbm, t_ref)), sems)

        window = pl.ds(pl.multiple_of(g * GROUP_Q, GROUP_Q), GROUP_K)
        valid = _valid_keys(g)
        for h in range(N_HEADS):
            cols = slice(h * HEAD_DIM, (h + 1) * HEAD_DIM)
            buf = h % 2
            s_ref[buf] = lax.dot_general(q_ref[:, cols], k_ref[window, cols], _DIMS["nt"],
                                         preferred_element_type=F32)
            for r in range(GROUP_Q // SOFTMAX_ROWS):
                rows = slice(r * SOFTMAX_ROWS, (r + 1) * SOFTMAX_ROWS)
                p_ref[buf, rows, :] = _softmax_rows(s_ref.at[buf], t_ref.at[h], valid, rows).astype(BF16)
            o_ref[:, cols] = jnp.dot(p_ref[buf], v_ref[window, cols], preferred_element_type=F32).astype(BF16)

    (ao,), extra = _host_call(
        body, "attn_fwd", grid=(seq // GROUP_Q,),
        in_specs=[_tile(GROUP_Q, D_ATTN), ANY, ANY, ANY], out_specs=[_tile(GROUP_Q, D_ATTN)],
        out_shape=[jax.ShapeDtypeStruct((seq, D_ATTN), BF16)],
        scratch_shapes=[pltpu.VMEM(kpad.shape, BF16), pltpu.VMEM(vpad.shape, BF16), pltpu.VMEM(table.shape, F32),
                        pltpu.VMEM((2, GROUP_Q, GROUP_K), F32), pltpu.VMEM((2, GROUP_Q, GROUP_K), BF16),
                        pltpu.SemaphoreType.DMA((3,))],
        args=[qkv, kpad, vpad, table], comm=comm)
    return ao, extra


def _attn_bwd(qkv, kpad, vpad, table, dao, comm=None):
    seq = qkv.shape[0]
    n_groups = seq // GROUP_Q
    fold_w = GROUP_K + (GROUP - 1) * CHUNK

    def body(ins, outs, scratch):
        q_ref, do_ref, k_hbm, v_hbm, t_hbm = ins
        dq_ref, dkt_hbm, dvt_hbm, db_ref, cs_ref = outs
        k_ref, v_ref, t_ref, db_acc, dkt_acc, dvt_acc, s_ref, dp_ref, p_ref, ds_ref, sems = scratch
        g = pl.program_id(0)

        @pl.when(g == 0)
        def _():
            _load_resident(((k_hbm, k_ref), (v_hbm, v_ref), (t_hbm, t_ref)), sems)
            db_acc[...] = jnp.zeros_like(db_acc)
            dkt_acc[...] = jnp.zeros_like(dkt_acc)
            dvt_acc[...] = jnp.zeros_like(dvt_acc)
            cs_ref[...] = jnp.zeros_like(cs_ref)

        window = pl.ds(pl.multiple_of(g * GROUP_Q, GROUP_Q), GROUP_K)
        valid = _valid_keys(g)
        for h in range(N_HEADS):
            cols = slice(h * HEAD_DIM, (h + 1) * HEAD_DIM)
            buf = h % 2
            qh, doh = q_ref[:, cols], do_ref[:, cols]
            kh, vh = k_ref[window, cols], v_ref[window, cols]
            s_ref[buf] = lax.dot_general(qh, kh, _DIMS["nt"], preferred_element_type=F32)
            dp_ref[buf] = lax.dot_general(doh, vh, _DIMS["nt"], preferred_element_type=F32)
            for r in range(GROUP_Q // SOFTMAX_ROWS):
                rows = slice(r * SOFTMAX_ROWS, (r + 1) * SOFTMAX_ROWS)
                p = _softmax_rows(s_ref.at[buf], t_ref.at[h], valid, rows)
                dp = dp_ref[buf, rows, :]
                ds = p * (dp - jnp.sum(dp * p, axis=-1, keepdims=True))
                chunk = (r * SOFTMAX_ROWS) // CHUNK
                shift = (GROUP - 1 - chunk) * CHUNK
                local = slice(r * SOFTMAX_ROWS - chunk * CHUNK, (r + 1) * SOFTMAX_ROWS - chunk * CHUNK)
                db_acc[h, local, shift:shift + GROUP_K] += ds
                p_ref[buf, rows, :] = p.astype(BF16)
                ds_ref[buf, rows, :] = (ds * (HEAD_DIM ** -0.5)).astype(BF16)
            dq_ref[:, cols] = jnp.dot(ds_ref[buf], kh, preferred_element_type=F32).astype(BF16)
            dkt_acc[cols, window] += lax.dot_general(qh, ds_ref[buf], _DIMS["tn"], preferred_element_type=F32)
            dvt_acc[cols, window] += lax.dot_general(doh, p_ref[buf], _DIMS["tn"], preferred_element_type=F32)
        cs_ref[0:1, :] += _colsum(dq_ref[...].astype(F32))

        @pl.when(g == n_groups - 1)
        def _():
            lo = (GROUP - 1) * CHUNK
            for h in range(N_HEADS):
                db_ref[h] = db_acc[h, :, lo:lo + BAND]
            inside = pl.ds(PAD_ROWS, seq)
            on_diagonal = (lax.broadcasted_iota(jnp.int32, (D_ATTN, D_ATTN), 0)
                           == lax.broadcasted_iota(jnp.int32, (D_ATTN, D_ATTN), 1))
            for row, acc in ((1, dkt_acc), (2, dvt_acc)):
                column = jnp.sum(acc[:, inside], axis=1, keepdims=True)
                cs_ref[row:row + 1, :] = _colsum(jnp.where(on_diagonal, column, 0.0))
            out_k = pltpu.make_async_copy(dkt_acc.at[:, inside], dkt_hbm, sems.at[0])
            out_v = pltpu.make_async_copy(dvt_acc.at[:, inside], dvt_hbm, sems.at[1])
            out_k.start()
            out_v.start()
            out_k.wait()
            out_v.wait()

    t_shape = (D_ATTN, seq + PAD_ROWS)
    outs, extra = _host_call(
        body, "attn_bwd", grid=(n_groups,),
        in_specs=[_tile(GROUP_Q, D_ATTN), _tile(GROUP_Q, D_ATTN), ANY, ANY, ANY],
        out_specs=[_tile(GROUP_Q, D_ATTN), ANY, ANY, _full((N_HEADS, CHUNK, BAND)), _full((8, D_ATTN))],
        out_shape=[jax.ShapeDtypeStruct((seq, D_ATTN), BF16), jax.ShapeDtypeStruct((D_ATTN, seq), F32),
                   jax.ShapeDtypeStruct((D_ATTN, seq), F32), jax.ShapeDtypeStruct((N_HEADS, CHUNK, BAND), F32),
                   jax.ShapeDtypeStruct((8, D_ATTN), F32)],
        scratch_shapes=[pltpu.VMEM(kpad.shape, BF16), pltpu.VMEM(vpad.shape, BF16), pltpu.VMEM(table.shape, F32),
                        pltpu.VMEM((N_HEADS, CHUNK, fold_w), F32), pltpu.VMEM(t_shape, F32),
                        pltpu.VMEM(t_shape, F32), pltpu.VMEM((2, GROUP_Q, GROUP_K), F32),
                        pltpu.VMEM((2, GROUP_Q, GROUP_K), F32), pltpu.VMEM((2, GROUP_Q, GROUP_K), BF16),
                        pltpu.VMEM((2, GROUP_Q, GROUP_K), BF16), pltpu.SemaphoreType.DMA((3,))],
        args=[qkv, dao, kpad, vpad, table], comm=comm)
    return outs, extra


def _assemble_dz(dq, dkt, dvt, dglu_a, dglu_b, dga, dgb):
    seq = dq.shape[0]
    rows = 512
    transposed = pl.BlockSpec((D_ATTN, rows), lambda i: (0, i))

    def body(dq_ref, dkt_ref, dvt_ref, da_ref, db_ref, dga_ref, dgb_ref, o_ref):
        o_ref[:, 0:D_ATTN] = dq_ref[...]
        o_ref[:, D_ATTN:2 * D_ATTN] = dkt_ref[...].T.astype(BF16)
        o_ref[:, 2 * D_ATTN:3 * D_ATTN] = dvt_ref[...].T.astype(BF16)
        off = 3 * D_ATTN
        for ref in (da_ref, db_ref, dga_ref, dgb_ref):
            width = ref.shape[1]
            o_ref[:, off:off + width] = ref[...]
            off += width

    width = 3 * D_ATTN + 2 * D_CONV + 2 * D_MODEL
    return pl.pallas_call(
        body, name="assemble_dz", out_shape=jax.ShapeDtypeStruct((seq, width), BF16), grid=(seq // rows,),
        in_specs=[_tile(rows, D_ATTN), transposed, transposed, _tile(rows, D_CONV), _tile(rows, D_CONV),
                  _tile(rows, D_MODEL), _tile(rows, D_MODEL)],
        out_specs=_tile(rows, width), compiler_params=_cparams(1),
    )(dq, dkt, dvt, dglu_a, dglu_b, dga, dgb)


CONV_ROWS = 256


def _ln_silu(u1, g, b):
    mu = jnp.mean(u1, axis=-1, keepdims=True)
    xc = u1 - mu
    rs = lax.rsqrt(jnp.mean(xc * xc, axis=-1, keepdims=True) + EPS)
    xhat = xc * rs
    u2 = xhat * g + b
    return xhat, rs, u2


def _glu_into(s_ref, a_ref, b_ref, ah_ref, bh_ref, first):
    halo = ah_ref[...] * _sig(bh_ref[...])
    s_ref[0:CONV_HALO, :] = jnp.where(first, 0.0, halo)
    s_ref[CONV_HALO:CONV_HALO + CONV_ROWS, :] = a_ref[...] * _sig(b_ref[...])


CONV_LANES = 128
CONV_TILES = CONV_ROWS // 8


def _lag_weights(w_ref, lanes):
    return {e: jnp.broadcast_to(w_ref[CONV_K - 1 - e:CONV_K - e, lanes], (8, CONV_LANES)) for e in range(CONV_K)}


def _class_sums(w, tiles, k):
    total = None
    for a, tile in enumerate(tiles):
        if 8 * a + k < CONV_K:
            term = w[8 * a + k] * tile
            total = term if total is None else total + term
    return total


def _conv_back(src_ref, first_tile, w, lanes, row_id, emit):
    before = None
    for m in range(-1, CONV_TILES):
        tiles = [src_ref[8 * (first_tile + m - a):8 * (first_tile + m - a) + 8, lanes] for a in range(4)]
        rolled = [None] + [pltpu.roll(_class_sums(w, tiles, k), k, 0) for k in range(1, 8)]
        if m >= 0:
            out = _class_sums(w, tiles, 0)
            for k in range(1, 8):
                out = out + jnp.where(row_id < k, before[k], rolled[k])
            emit(m, out)
        before = rolled


def _conv_ahead(src_ref, w, lanes, row_id, emit):
    before = None
    for m in range(CONV_TILES + 1):
        tiles = [src_ref[8 * (m + a):8 * (m + a) + 8, lanes] for a in range(4)]
        rolled = [None] + [pltpu.roll(_class_sums(w, tiles, k), 8 - k, 0) for k in range(1, 8)]
        if m >= 1:
            out = before[0]
            for k in range(1, 8):
                out = out + jnp.where(row_id < 8 - k, before[k], rolled[k])
            emit(m - 1, out)
        before = [_class_sums(w, tiles, 0) if m < CONV_TILES else None] + rolled[1:]


def _conv_weight_sums(d_ref, s_ref, lanes, row_id, whole_shifts):
    zero = jnp.zeros((8, CONV_LANES), F32)
    sums = {8 * a + k: zero for a in whole_shifts for k in range(8) if 8 * a + k < CONV_K}

    def d_tile(m):
        return d_ref[8 * m:8 * m + 8, lanes] if 0 <= m < CONV_TILES else zero

    rolled = [None] + [zero] * 7
    for m in range(-1, CONV_TILES):
        cur, nxt = d_tile(m), d_tile(m + 1)
        rolled_next = [None] + [pltpu.roll(nxt, 8 - k, 0) for k in range(1, 8)]
        shifted = [cur] + [jnp.where(row_id < 8 - k, rolled[k], rolled_next[k]) for k in range(1, 8)]
        for a in whole_shifts:
            tile = s_ref[8 * (CONV_HALO // 8 + m - a):8 * (CONV_HALO // 8 + m - a) + 8, lanes]
            for k in range(8):
                if 8 * a + k < CONV_K and not (m < 0 and k == 0):
                    sums[8 * a + k] = sums[8 * a + k] + shifted[k] * tile
        rolled = rolled_next
    return sums


def _conv_fwd(zr, w_dw, b_dw, g_ln, b_ln, comm=None):
    seq = zr.shape[0]

    def body(a_ref, b_ref, ah_ref, bh_ref, w_ref, bias_ref, g_ref, bl_ref, u1_ref, u3_ref, s_ref):
        _glu_into(s_ref, a_ref, b_ref, ah_ref, bh_ref, pl.program_id(0) == 0)
        row_id = lax.broadcasted_iota(jnp.int32, (8, CONV_LANES), 0)
        for lo in range(0, D_CONV, CONV_LANES):
            lanes = slice(lo, lo + CONV_LANES)
            bias = jnp.broadcast_to(bias_ref[:, lanes], (8, CONV_LANES))

            def emit(m, out, lanes=lanes, bias=bias):
                u1_ref[8 * m:8 * m + 8, lanes] = out + bias

            _conv_back(s_ref, CONV_HALO // 8, _lag_weights(w_ref, lanes), lanes, row_id, emit)
        _, _, u2 = _ln_silu(u1_ref[...], g_ref[...], bl_ref[...])
        u3_ref[...] = (u2 * _sig(u2)).astype(BF16)

    return _host_call(
        lambda ins, outs, scratch: body(*ins, *outs, *scratch), "conv_fwd", grid=(seq // CONV_ROWS,),
        in_specs=[_tile(CONV_ROWS, D_CONV, 0), _tile(CONV_ROWS, D_CONV, 1),
                  _prev(CONV_HALO, D_CONV, CONV_ROWS, 0), _prev(CONV_HALO, D_CONV, CONV_ROWS, 1),
                  _full((CONV_K, D_CONV)), _full((1, D_CONV)), _full((1, D_CONV)), _full((1, D_CONV))],
        out_specs=[_tile(CONV_ROWS, D_CONV), _tile(CONV_ROWS, D_CONV)],
        out_shape=[jax.ShapeDtypeStruct((seq, D_CONV), F32), jax.ShapeDtypeStruct((seq, D_CONV), BF16)],
        scratch_shapes=[pltpu.VMEM((CONV_HALO + CONV_ROWS, D_CONV), F32)],
        args=[zr, zr, zr, zr, w_dw, b_dw, g_ln, b_ln], comm=comm)


def _conv_bwd(zr, u1, du3, w_dw, g_ln, b_ln, comm=None):
    seq = zr.shape[0]
    n_tiles = seq // CONV_ROWS
    n_halo = seq // CONV_HALO
    ext = CONV_ROWS + CONV_HALO

    def body(a_ref, b_ref, ah_ref, bh_ref, u1_ref, u1n_ref, d3_ref, d3n_ref, w_ref, g_ref, bl_ref,
             da_ref, db_ref, dw_ref, small_ref, s_ref, d_ref, du0_ref):
        i = pl.program_id(0)

        @pl.when(i == 0)
        def _():
            dw_ref[...] = jnp.zeros_like(dw_ref)
            small_ref[...] = jnp.zeros_like(small_ref)

        _glu_into(s_ref, a_ref, b_ref, ah_ref, bh_ref, i == 0)
        gv, bv = g_ref[...], bl_ref[...]

        def du1_of(u1, d3):
            xhat, rs, u2 = _ln_silu(u1, gv, bv)
            sg = _sig(u2)
            du2 = d3 * (sg * (1.0 + u2 * (1.0 - sg)))
            dxh = du2 * gv
            du1 = rs * (dxh - jnp.mean(dxh, axis=-1, keepdims=True)
                        - xhat * jnp.mean(dxh * xhat, axis=-1, keepdims=True))
            return du1, du2, xhat

        du1, du2, xhat = du1_of(u1_ref[...], d3_ref[...])
        du1n, _, _ = du1_of(u1n_ref[...], d3n_ref[...])
        d_ref[0:CONV_ROWS, :] = du1
        d_ref[CONV_ROWS:ext, :] = jnp.where(i == n_tiles - 1, 0.0, du1n)
        small_ref[0:1, :] += _colsum(du1)
        small_ref[1:2, :] += _colsum(du2 * xhat)
        small_ref[2:3, :] += _colsum(du2)
        row_id = lax.broadcasted_iota(jnp.int32, (8, CONV_LANES), 0)
        for lo in range(0, D_CONV, CONV_LANES):
            lanes = slice(lo, lo + CONV_LANES)

            def emit(m, out, lanes=lanes):
                du0_ref[8 * m:8 * m + 8, lanes] = out

            _conv_ahead(d_ref, _lag_weights(w_ref, lanes), lanes, row_id, emit)
            for whole_shifts in ((0, 1), (2, 3)):
                for e, total in _conv_weight_sums(d_ref, s_ref, lanes, row_id, whole_shifts).items():
                    dw_ref[CONV_K - 1 - e:CONV_K - e, lanes] += _colsum(total)
        du0 = du0_ref[...]
        sb = _sig(b_ref[...])
        da = du0 * sb
        dbv = du0 * a_ref[...] * sb * (1.0 - sb)
        da_ref[...] = da.astype(BF16)
        db_ref[...] = dbv.astype(BF16)
        small_ref[3:4, :] += _colsum(da)
        small_ref[4:5, :] += _colsum(dbv)

    return _host_call(
        lambda ins, outs, scratch: body(*ins, *outs, *scratch), "conv_bwd", grid=(n_tiles,),
        in_specs=[_tile(CONV_ROWS, D_CONV, 0), _tile(CONV_ROWS, D_CONV, 1),
                  _prev(CONV_HALO, D_CONV, CONV_ROWS, 0), _prev(CONV_HALO, D_CONV, CONV_ROWS, 1),
                  _tile(CONV_ROWS, D_CONV), _next(CONV_HALO, D_CONV, CONV_ROWS, n_halo),
                  _tile(CONV_ROWS, D_CONV), _next(CONV_HALO, D_CONV, CONV_ROWS, n_halo),
                  _full((CONV_K, D_CONV)), _full((1, D_CONV)), _full((1, D_CONV))],
        out_specs=[_tile(CONV_ROWS, D_CONV), _tile(CONV_ROWS, D_CONV), _full((CONV_HALO, D_CONV)),
                   _full((8, D_CONV))],
        out_shape=[jax.ShapeDtypeStruct((seq, D_CONV), BF16), jax.ShapeDtypeStruct((seq, D_CONV), BF16),
                   jax.ShapeDtypeStruct((CONV_HALO, D_CONV), F32), jax.ShapeDtypeStruct((8, D_CONV), F32)],
        scratch_shapes=[pltpu.VMEM((ext, D_CONV), F32), pltpu.VMEM((ext, D_CONV), F32),
                        pltpu.VMEM((CONV_ROWS, D_CONV), F32)],
        args=[zr, zr, zr, zr, u1, u1, du3, du3, w_dw, g_ln, b_ln], comm=comm)


MERGE_ROWS = 256


def _merge_fwd(ao, u3, zr, w_ao, w_co, b_co):
    seq = ao.shape[0]

    def body(ao_ref, u3_ref, ga_ref, gb_ref, wa_ref, wc_ref, bc_ref, y_ref, a_ref, cb_ref):
        a = jnp.dot(ao_ref[...], wa_ref[...], preferred_element_type=F32)
        cb = jnp.dot(u3_ref[...], wc_ref[...], preferred_element_type=F32) + bc_ref[...]
        a_ref[...] = a
        cb_ref[...] = cb
        y_ref[...] = (_sig(ga_ref[...]) * a + _sig(gb_ref[...]) * cb).astype(BF16)

    f32_out = jax.ShapeDtypeStruct((seq, D_MODEL), F32)
    return pl.pallas_call(
        body, name="merge_fwd",
        out_shape=[jax.ShapeDtypeStruct((seq, D_MODEL), BF16), f32_out, f32_out],
        grid=(seq // MERGE_ROWS,),
        in_specs=[_tile(MERGE_ROWS, D_ATTN), _tile(MERGE_ROWS, D_CONV), _tile(MERGE_ROWS, D_MODEL, 1),
                  _tile(MERGE_ROWS, D_MODEL, 2), _full(w_ao.shape), _full(w_co.shape), _full((1, D_MODEL))],
        out_specs=[_tile(MERGE_ROWS, D_MODEL)] * 3, compiler_params=_cparams(1),
    )(ao, u3, zr, zr, w_ao, w_co, b_co)


def _merge_bwd(a, cb, zr, rows):
    seq = a.shape[0]

    def fn(dy_v, first, ins, outs):
        a_ref, cb_ref, ga_ref, gb_ref = ins
        da_ref, dcb_ref, dga_ref, dgb_ref, small_ref = outs

        @pl.when(first)
        def _():
            small_ref[...] = jnp.zeros_like(small_ref)

        sa, sb = _sig(ga_ref[...]), _sig(gb_ref[...])
        dcb = dy_v * sb
        dga = dy_v * a_ref[...] * sa * (1.0 - sa)
        dgb = dy_v * cb_ref[...] * sb * (1.0 - sb)
        da_ref[...] = (dy_v * sa).astype(BF16)
        dcb_ref[...] = dcb.astype(BF16)
        dga_ref[...] = dga.astype(BF16)
        dgb_ref[...] = dgb.astype(BF16)
        small_ref[0:1, :] += _colsum(dga)
        small_ref[1:2, :] += _colsum(dgb)
        small_ref[2:3, :] += _colsum(dcb)

    bf = jax.ShapeDtypeStruct((seq, D_MODEL), BF16)
    gate = lambda col: pl.BlockSpec((rows, D_MODEL), lambda i, j: (i, col))
    return _Epilogue(
        [a, cb, zr, zr], [_row_tile(rows, D_MODEL), _row_tile(rows, D_MODEL), gate(1), gate(2)],
        [bf, bf, bf, bf, jax.ShapeDtypeStruct((8, D_MODEL), F32)],
        [_row_tile(rows, D_MODEL)] * 4 + [_whole((8, D_MODEL))], fn, keep_product=False)


FFN_ROWS = 2048
FFN_BLOCKS = D_FF // FFN_COLS
GELU_C = math.sqrt(2.0 / math.pi)


def _gelu(v):
    t = jnp.tanh(GELU_C * (v + 0.044715 * (v * v * v)))
    return 0.5 * v * (1.0 + t), t


def _gelu_grad(v, t):
    return 0.5 * (1.0 + t) + 0.5 * v * (1.0 - t * t) * (GELU_C * (1.0 + 3.0 * 0.044715 * (v * v)))


def _sublane_rows(ref, n):
    return [jnp.broadcast_to(ref[r:r + 1, :], (8, FFN_COLS)) for r in range(n)]


def _rolls(tile, shifts):
    return tuple(pltpu.roll(tile, s, 0) for s in shifts)


def _behind(prev_rolls, cur, row_id):
    rolls = _rolls(cur, (1, 2))
    x1 = jnp.where(row_id < 1, prev_rolls[0], rolls[0])
    x2 = jnp.where(row_id < 2, prev_rolls[1], rolls[1])
    return (x2, x1, cur), rolls


def _ahead(cur_rolls, next_rolls, row_id):
    return (jnp.where(row_id < 7, cur_rolls[0], next_rolls[0]), jnp.where(row_id < 6, cur_rolls[1], next_rolls[1]))


def _conv3(taps, w, bias):
    return w[0] * taps[0] + w[1] * taps[1] + w[2] * taps[2] + bias


def _ffn_specs(rows):
    tile = lambda off: pl.BlockSpec((rows, FFN_COLS), lambda j, i: (i, j + off))
    prev = lambda off: pl.BlockSpec((FFN_HALO, FFN_COLS),
                                    lambda j, i: (jnp.maximum(i * (rows // FFN_HALO) - 1, 0), j + off))
    wgt = lambda off: pl.BlockSpec((3, FFN_COLS), lambda j, i: (0, j + off))
    vec = lambda off: pl.BlockSpec((1, FFN_COLS), lambda j, i: (0, j + off))
    return tile, prev, wgt, vec


def _ffn_act(up, w_dw, b_dw):
    seq = up.shape[0]
    tile, prev, wgt, vec = _ffn_specs(FFN_ROWS)

    def body(v_ref, g_ref, vp_ref, gp_ref, wv_ref, wg_ref, bv_ref, bg_ref, act_ref):
        first = pl.program_id(1) == 0
        row_id = lax.broadcasted_iota(jnp.int32, (8, FFN_COLS), 0)
        wv, wg = _sublane_rows(wv_ref, 3), _sublane_rows(wg_ref, 3)
        (bv,), (bg,) = _sublane_rows(bv_ref, 1), _sublane_rows(bg_ref, 1)
        rolls_v = _rolls(jnp.where(first, 0.0, vp_ref[...]), (1, 2))
        rolls_g = _rolls(jnp.where(first, 0.0, gp_ref[...]), (1, 2))
        for row in range(0, FFN_ROWS, 16):
            halves = []
            for r in (row, row + 8):
                taps_v, rolls_v = _behind(rolls_v, v_ref[r:r + 8, :], row_id)
                taps_g, rolls_g = _behind(rolls_g, g_ref[r:r + 8, :], row_id)
                halves.append(_gelu(_conv3(taps_g, wg, bg))[0] * _conv3(taps_v, wv, bv))
            act_ref[row:row + 16, :] = jnp.concatenate(halves, axis=0).astype(BF16)

    return pl.pallas_call(
        body, name="ffn_act", out_shape=jax.ShapeDtypeStruct((seq, D_FF), BF16),
        grid=(FFN_BLOCKS, seq // FFN_ROWS),
        in_specs=[tile(0), tile(FFN_BLOCKS), prev(0), prev(FFN_BLOCKS), wgt(0), wgt(FFN_BLOCKS),
                  vec(0), vec(FFN_BLOCKS)],
        out_specs=tile(0), compiler_params=_cparams(2),
    )(up, up, up, up, w_dw, w_dw, b_dw, b_dw)


def _ffn_act_bwd(up, dact, w_dw, b_dw, comm=None):
    seq = up.shape[0]
    n_tiles = seq // FFN_ROWS
    n_halo = seq // FFN_HALO
    tile, prev, wgt, vec = _ffn_specs(FFN_ROWS)
    nxt = lambda off: pl.BlockSpec(
        (FFN_HALO, FFN_COLS), lambda j, i: (jnp.minimum((i + 1) * (FFN_ROWS // FFN_HALO), n_halo - 1), j + off))
    acc = lambda off: pl.BlockSpec((8, FFN_COLS), lambda j, i: (0, j + off))

    def body(v_ref, g_ref, vp_ref, gp_ref, vn_ref, gn_ref, da_ref, dan_ref, wv_ref, wg_ref, bv_ref, bg_ref,
             dv_out, dg_out, dwv_ref, dwg_ref, dbv_ref, dbg_ref):
        i = pl.program_id(1)
        first, last = i == 0, i == n_tiles - 1

        @pl.when(first)
        def _():
            for r in (dwv_ref, dwg_ref, dbv_ref, dbg_ref):
                r[...] = jnp.zeros_like(r)

        row_id = lax.broadcasted_iota(jnp.int32, (8, FFN_COLS), 0)
        wv, wg = _sublane_rows(wv_ref, 3), _sublane_rows(wg_ref, 3)
        (bv,), (bg,) = _sublane_rows(bv_ref, 1), _sublane_rows(bg_ref, 1)
        zero = jnp.zeros((8, FFN_COLS), F32)
        sums_v, sums_g = [zero] * 4, [zero] * 4
        rolls_v = _rolls(jnp.where(first, 0.0, vp_ref[...]), (1, 2))
        rolls_g = _rolls(jnp.where(first, 0.0, gp_ref[...]), (1, 2))
        behind = None
        done_v, done_g = [], []

        def grads(v_tile, g_tile, dact, rolls_v, rolls_g):
            taps_v, rolls_v = _behind(rolls_v, v_tile, row_id)
            taps_g, rolls_g = _behind(rolls_g, g_tile, row_id)
            val, gate = _conv3(taps_v, wv, bv), _conv3(taps_g, wg, bg)
            gel, t = _gelu(gate)
            return dact * gel, dact * val * _gelu_grad(gate, t), taps_v, taps_g, rolls_v, rolls_g

        def finish(tile, nxt, row):
            for (d, d_rolls), (_, n_rolls), w, done, o_ref in ((tile[0], nxt[0], wv, done_v, dv_out),
                                                               (tile[1], nxt[1], wg, done_g, dg_out)):
                d1, d2 = _ahead(d_rolls, n_rolls, row_id)
                done.append(w[2] * d + w[1] * d1 + w[0] * d2)
                if len(done) == 2:
                    o_ref[row - 16:row, :] = jnp.concatenate(done, axis=0).astype(BF16)
                    done.clear()

        for row in range(0, FFN_ROWS, 16):
            dact16 = da_ref[row:row + 16, :].astype(F32)
            for r, dact in ((row, dact16[0:8, :]), (row + 8, dact16[8:16, :])):
                dval, dgate, taps_v, taps_g, rolls_v, rolls_g = grads(v_ref[r:r + 8, :], g_ref[r:r + 8, :], dact,
                                                                      rolls_v, rolls_g)
                sums_v = [s + dval * x for s, x in zip(sums_v, taps_v)] + [sums_v[3] + dval]
                sums_g = [s + dgate * x for s, x in zip(sums_g, taps_g)] + [sums_g[3] + dgate]
                tile = ((dval, _rolls(dval, (7, 6))), (dgate, _rolls(dgate, (7, 6))))
                if behind is not None:
                    finish(behind, tile, r)
                behind = tile
        dact_next = jnp.where(last, 0.0, dan_ref[...].astype(F32)[0:FFN_HALO, :])
        dval, dgate, *_ = grads(vn_ref[...], gn_ref[...], dact_next, rolls_v, rolls_g)
        finish(behind, ((dval, _rolls(dval, (7, 6))), (dgate, _rolls(dgate, (7, 6)))), FFN_ROWS)
        for sums, dw_ref, db_ref in ((sums_v, dwv_ref, dbv_ref), (sums_g, dwg_ref, dbg_ref)):
            for tap in range(3):
                dw_ref[tap:tap + 1, :] += _colsum(sums[tap])
            db_ref[0:1, :] += _colsum(sums[3])

    half = jax.ShapeDtypeStruct((seq, D_FF), BF16)
    acc_shape = jax.ShapeDtypeStruct((8, D_FF), F32)
    return _host_call(
        lambda ins, outs, scratch: body(*ins, *outs, *scratch), "ffn_act_bwd", grid=(FFN_BLOCKS, n_tiles),
        in_specs=[tile(0), tile(FFN_BLOCKS), prev(0), prev(FFN_BLOCKS), nxt(0), nxt(FFN_BLOCKS),
                  tile(0), pl.BlockSpec((16, FFN_COLS), lambda j, i: (
                      jnp.minimum((i + 1) * (FFN_ROWS // 16), seq // 16 - 1), j)),
                  wgt(0), wgt(FFN_BLOCKS), vec(0), vec(FFN_BLOCKS)],
        out_specs=[tile(0), tile(0), acc(0), acc(0), acc(0), acc(0)],
        out_shape=[half, half, acc_shape, acc_shape, acc_shape, acc_shape],
        scratch_shapes=[], args=[up, up, up, up, up, up, dact, dact, w_dw, w_dw, b_dw, b_dw], comm=comm)


def _cols_to_blocks(full_cols):
    k, n8 = full_cols.shape
    return jnp.transpose(full_cols.reshape(k, N_DEV, n8 // N_DEV), (1, 0, 2))


def _rows_to_blocks(full_rows):
    r8, n = full_rows.shape
    return full_rows.reshape(N_DEV, r8 // N_DEV, n)


def _blocks_to_cols(gathered):
    _, k, n = gathered.shape
    return jnp.transpose(gathered, (1, 0, 2)).reshape(k, N_DEV * n)


def kernel(x, c, w_ada, b_ada, g_pre_mix, g_post_mix, w_in, b_in, rel_bias, w_attn_o, w_dw_conv, b_dw_conv, g_conv_ln, b_conv_ln, w_conv_o, b_conv_o, w_mix_o, g_pre_ffn, g_post_ffn, w_up, w_dw_ffn, b_dw_ffn, w_down, loss_target, m_w_ada, m_b_ada, m_g_pre_mix, m_g_post_mix, m_w_in, m_b_in, m_rel_bias, m_w_attn_o, m_w_dw_conv, m_b_dw_conv, m_g_conv_ln, m_b_conv_ln, m_w_conv_o, m_b_conv_o, m_w_mix_o, m_g_pre_ffn, m_g_post_ffn, m_w_up, m_w_dw_ffn, m_b_dw_ffn, m_w_down, v_w_ada, v_b_ada, v_g_pre_mix, v_g_post_mix, v_w_in, v_b_in, v_rel_bias, v_w_attn_o, v_w_dw_conv, v_b_dw_conv, v_g_conv_ln, v_b_conv_ln, v_w_conv_o, v_b_conv_o, v_w_mix_o, v_g_pre_ffn, v_g_post_ffn, v_w_up, v_w_dw_ffn, v_b_dw_ffn, v_w_down):
    names = ["w_ada", "b_ada", "g_pre_mix", "g_post_mix", "w_in", "b_in", "rel_bias", "w_attn_o", "w_dw_conv",
             "b_dw_conv", "g_conv_ln", "b_conv_ln", "w_conv_o", "b_conv_o", "w_mix_o", "g_pre_ffn", "g_post_ffn",
             "w_up", "w_dw_ffn", "b_dw_ffn", "w_down"]
    weights = dict(zip(names, [w_ada, b_ada, g_pre_mix, g_post_mix, w_in, b_in, rel_bias, w_attn_o, w_dw_conv,
                               b_dw_conv, g_conv_ln, b_conv_ln, w_conv_o, b_conv_o, w_mix_o, g_pre_ffn,
                               g_post_ffn, w_up, w_dw_ffn, b_dw_ffn, w_down]))
    mom_m = dict(zip(names, [m_w_ada, m_b_ada, m_g_pre_mix, m_g_post_mix, m_w_in, m_b_in, m_rel_bias, m_w_attn_o,
                             m_w_dw_conv, m_b_dw_conv, m_g_conv_ln, m_b_conv_ln, m_w_conv_o, m_b_conv_o,
                             m_w_mix_o, m_g_pre_ffn, m_g_post_ffn, m_w_up, m_w_dw_ffn, m_b_dw_ffn, m_w_down]))
    mom_v = dict(zip(names, [v_w_ada, v_b_ada, v_g_pre_mix, v_g_post_mix, v_w_in, v_b_in, v_rel_bias, v_w_attn_o,
                             v_w_dw_conv, v_b_dw_conv, v_g_conv_ln, v_b_conv_ln, v_w_conv_o, v_b_conv_o,
                             v_w_mix_o, v_g_pre_ffn, v_g_post_ffn, v_w_up, v_w_dw_ffn, v_b_dw_ffn, v_w_down]))
    shapes = {n: w.shape for n, w in weights.items()}

    seq = x.shape[1]
    me = 4 * lax.axis_index("x") + 2 * lax.axis_index("y") + lax.axis_index("c")
    x2 = x.reshape(seq, D_MODEL)
    target = loss_target.reshape(seq, D_MODEL)
    sq = lambda a: a.reshape(a.shape[1:])
    bf = lambda a: sq(a).astype(BF16)

    c_act = _silu_vec(c)
    transposed = lambda a: jnp.swapaxes(sq(a), 0, 1)
    (c_all,) = _run_comm(_gather_comm([c_act]), "gather_c")
    c_all = c_all.reshape(N_DEV, D_MODEL)

    (mod_all,) = _run_comm(_gather_comm([_ada_fwd(c_all, sq(w_ada))]), "gather_mod")
    mod = lax.dynamic_index_in_dim(mod_all, me, axis=1, keepdims=False)
    mod6 = (mod.reshape(1, 6 * D_MODEL) + b_ada).reshape(6, D_MODEL)

    h1, (g_in, g_dwc, g_dwf) = _pre_mix(
        x2, mod6, g_pre_mix, comm=_gather_comm([transposed(w_in).astype(BF16), sq(w_dw_conv), sq(w_dw_ffn)]))
    wt_in = g_in.reshape(g_in.shape[0] * g_in.shape[1], D_MODEL)
    wf_dwc = _blocks_to_cols(g_dwc)
    wf_dwf = _blocks_to_cols(g_dwf)
    qkv = _mm(h1, wt_in, "nt", BF16, "in_proj_qkv", bias=b_in, tm=1024, tn=768, cols=(0, 3 * D_ATTN))
    zr, _, (g_ao, g_co, g_mo) = _mm(h1, wt_in, "nt", F32, "in_proj_rest", bias=b_in, tm=1024, tn=3 * D_ATTN,
                                 cols=(3 * D_ATTN, 2 * D_CONV + 2 * D_MODEL),
                                 comm=_gather_comm([bf(w_attn_o), bf(w_conv_o), bf(w_mix_o)]))
    kpad = jnp.pad(qkv[:, D_ATTN:2 * D_ATTN], ((PAD_ROWS, 0), (0, 0)))
    vpad = jnp.pad(qkv[:, 2 * D_ATTN:], ((PAD_ROWS, 0), (0, 0)))
    table = jnp.transpose(_bias_table(sq(rel_bias)), (1, 0, 2))
    ao, (g_up,) = _attn_fwd(qkv, kpad, vpad, table, comm=_gather_comm([transposed(w_up).astype(BF16)]))
    (u1, u3), (g_dn,) = _conv_fwd(zr, wf_dwc, b_dw_conv, g_conv_ln, b_conv_ln, comm=_gather_comm([bf(w_down)]))
    wf_ao = _blocks_to_cols(g_ao)
    wf_co = _blocks_to_cols(g_co)
    wf_mo = g_mo.reshape(D_MODEL, D_MODEL)
    wt_up = g_up.reshape(g_up.shape[0] * g_up.shape[1], D_MODEL)
    wf_dn = g_dn.reshape(D_FF, D_MODEL)
    y, a_br, cb_br = _merge_fwd(ao, u3, zr, wf_ao, wf_co, b_conv_o)
    ymix, (x1, h2), _ = _mm(y, wf_mo, "nn", F32, "mix_o", tm=512, tn=D_MODEL,
                            epilogue=_post_mix_pre_ffn(x2, mod6, g_post_mix, g_pre_ffn, 512))
    up = _mm(h2, wt_up, "nt", F32, "ffn_up", tm=1024, tn=1408)
    act = _ffn_act(up, wf_dwf, b_dw_ffn)
    _, (loss_lanes, dout, dyf, small_f), _ = _mm(act, wf_dn, "nn", F32, "ffn_down", tm=512, tn=D_MODEL,
                                                 epilogue=_final(x1, target, mod6, g_post_ffn, 512))

    dact = _mm(dyf, wf_dn, "nt", BF16, "ffn_down_dx", tm=1024, tn=1408)
    gw_down = _mm(act, dyf, "tn", BF16, "ffn_down_dw", tm=256, tn=1024)
    (dup_v, dup_g, dwv, dwg, dbv, dbg), (parts_down,) = _ffn_act_bwd(
        up, dact, wf_dwf, b_dw_ffn, comm=_scatter_comm([_rows_to_blocks(gw_down)]))
    _, (dx1, dymix, small_m), _ = _mm([dup_v, dup_g], wt_up, "nn", F32, "ffn_up_dx", tm=256, tn=D_MODEL,
                                      epilogue=_mid_bwd(x1, dout, ymix, mod6, g_pre_ffn, g_post_mix, 256))
    blocks_up = _rows_to_blocks(_mm_tn_rows([dup_v, dup_g], h2, "ffn_up_dw"))
    _, (da, dcb, dga, dgb, small_g), _ = _mm(dymix, wf_mo, "nt", F32, "mix_o_dx", tm=512, tn=D_MODEL,
                                             epilogue=_merge_bwd(a_br, cb_br, zr, 512))
    gw_mo = _mm(y, dymix, "tn", BF16, "mix_o_dw")
    dao = _mm(da, wf_ao, "nt", BF16, "attn_o_dx", tm=1024)
    gw_ao = _mm(ao, da, "tn", BF16, "attn_o_dw")
    du3 = _mm(dcb, wf_co, "nt", F32, "conv_o_dx", tm=1024)
    gw_co = _mm(u3, dcb, "tn", BF16, "conv_o_dw")
    (dq, dkt, dvt, dbias, small_a), (parts_up,) = _attn_bwd(
        qkv, kpad, vpad, table, dao, comm=_scatter_comm([blocks_up]))
    g_rel = _bias_grad(jnp.transpose(dbias, (1, 0, 2)))
    (dglu_a, dglu_b, dw_conv, small_c), (parts_mo, parts_ao, parts_co) = _conv_bwd(
        zr, u1, du3, wf_dwc, g_conv_ln, b_conv_ln,
        comm=_scatter_comm([_rows_to_blocks(gw_mo), _cols_to_blocks(gw_ao), _cols_to_blocks(gw_co)]))
    dz = _assemble_dz(dq, dkt, dvt, dglu_a, dglu_b, dga, dgb)
    blocks_in = _rows_to_blocks(_mm(dz, h1, "tn", BF16, "in_proj_dw", tm=512, tn=D_MODEL))
    _, (grad_x, small_x), (parts_in,) = _mm(dz, wt_in, "nn", F32, "in_proj_dx", tm=512, tn=D_MODEL,
                                            comm=_scatter_comm([blocks_in]),
                                            epilogue=_pre_mix_bwd(x2, dx1, mod6, g_pre_mix, 512))

    packed = _pack_grads(small_x, small_m, small_f, small_g, small_a, small_c, dbv, dbg, dwv, dwg, dw_conv)
    gathered, gathered_rel, gathered_loss = _run_comm(_gather_comm([packed, g_rel, loss_lanes]), "gather_small")
    gathered = gathered.reshape(N_DEV, PACKED_TOTAL)
    updates, g_dwc_full, g_dwf_full, loss_all = _small_adamw(gathered, gathered_rel, gathered_loss, weights, mom_m,
                                                             mom_v)
    loss = loss_all[0, 0]

    grads, deltas, new_m, new_v = {}, {}, {}, {}

    def record(name, update, is_transposed=False):
        for dst, val in zip((grads, deltas, new_m, new_v), update):
            dst[name] = (jnp.swapaxes(val, 0, 1) if is_transposed else val).reshape(shapes[name])

    for name, update in updates.items():
        record(name, update)

    def local_update(name, grad):
        record(name, _adamw(sq(weights[name]), sq(mom_m[name]), sq(mom_v[name]), "adamw_" + name, g=grad))

    conv_cols, ffn_cols, ada_cols = D_CONV // N_DEV, 2 * D_FF // N_DEV, 6 * D_MODEL // N_DEV
    local_update("w_dw_conv", lax.dynamic_slice(g_dwc_full, (0, me * conv_cols), (CONV_K, conv_cols)))
    local_update("w_dw_ffn", lax.dynamic_slice(g_dwf_full, (0, me * ffn_cols), (3, ffn_cols)))
    local_update("w_ada", _ada_grad(c_all, lax.dynamic_slice(gathered, (0, me * ada_cols), (N_DEV, ada_cols))))

    for name, part in (("w_attn_o", parts_ao), ("w_conv_o", parts_co), ("w_mix_o", parts_mo), ("w_down", parts_down)):
        record(name, _adamw(sq(weights[name]), sq(mom_m[name]), sq(mom_v[name]), "adamw_" + name, parts=part))
    for name, part in (("w_in", parts_in), ("w_up", parts_up)):
        record(name, _adamw(transposed(weights[name]), transposed(mom_m[name]), transposed(mom_v[name]),
                            "adamw_" + name, parts=part), is_transposed=True)

    return (loss, grad_x.reshape(x.shape), *[grads[n] for n in names], *[deltas[n] for n in names],
            *[new_m[n] for n in names], *[new_v[n] for n in names])
```

```python
import functools
import math

import jax
import jax.numpy as jnp
from jax import lax
from jax.experimental import pallas as pl
from jax.experimental.pallas import tpu as pltpu

F32 = jnp.float32
BF16 = jnp.bfloat16
HIGHEST = lax.Precision.HIGHEST

D_MODEL = 1024
CHUNK = 64
LEFT_CHUNKS = 8
BAND = (LEFT_CHUNKS + 1) * CHUNK
PAD_ROWS = LEFT_CHUNKS * CHUNK
GROUP = 4
GROUP_Q = GROUP * CHUNK
GROUP_K = GROUP_Q + PAD_ROWS
SOFTMAX_ROWS = 16
TOEPLITZ = 640
N_HEADS = 8
HEAD_DIM = 64
D_ATTN = 512
D_CONV = 512
CONV_K = 31
CONV_HALO = 32
MAX_REL = 128
N_REL = 2 * MAX_REL + 1
D_FF = 2816
FFN_HALO = 8
FFN_COLS = 256
EPS = 1e-6
NEG_INF = -1e30
N_DEV = 8

ADAM_LR = 0.001
ADAM_B1 = 0.9
ADAM_B2 = 0.999
ADAM_EPS = 1e-08
ADAM_WD = 0.01
ADAM_STEP = 10

VMEM_LIMIT_BYTES = 56 * 1024 * 1024
ADAMW_BLOCK_BYTES = 768 * 1024

MESH = pl.DeviceIdType.MESH
ANY = pl.BlockSpec(memory_space=pl.ANY)

SH_M, SC_M, GT_M, SH_F, SC_F, GT_F = range(6)

SMALL = (("b_ada", 6144), ("g_pre_mix", 1024), ("g_post_mix", 1024), ("b_in", 4608), ("b_dw_conv", 512),
         ("g_conv_ln", 512), ("b_conv_ln", 512), ("b_conv_o", 1024), ("g_pre_ffn", 1024), ("g_post_ffn", 1024),
         ("b_dw_ffn", 5632))
PACKED_TOTAL = sum(n for _, n in SMALL) + CONV_K * D_CONV + 3 * 2 * D_FF


def _cparams(n_axes):
    return pltpu.CompilerParams(vmem_limit_bytes=VMEM_LIMIT_BYTES,
                                dimension_semantics=("arbitrary",) * n_axes)


def _sig(v):
    return 1.0 / (1.0 + jnp.exp(-v))


def _pick(n, target):
    if n <= target:
        return n
    t = target - target % 128
    while n % t:
        t -= 128
    return t


def _tile(rows, cols, col=0):
    return pl.BlockSpec((rows, cols), lambda i: (i, col))


def _full(shape):
    zeros = (0,) * len(shape)
    return pl.BlockSpec(shape, lambda i: zeros)


def _prev(halo, cols, rows, col=0):
    return pl.BlockSpec((halo, cols), lambda i: (jnp.maximum(i * (rows // halo) - 1, 0), col))


def _next(halo, cols, rows, n_blocks, col=0):
    return pl.BlockSpec((halo, cols), lambda i: (jnp.minimum((i + 1) * (rows // halo), n_blocks - 1), col))


class _Comm:
    def __init__(self, inputs, out_shapes, sems, start, finish, relay=None):
        self.inputs, self.out_shapes, self.sems, self.start, self.finish = inputs, out_shapes, sems, start, finish
        self.relay = relay


def _host_call(body, name, grid, in_specs, out_specs, out_shape, scratch_shapes, args, comm=None):
    n_in, n_out, n_scr = len(args), len(out_shape), len(scratch_shapes)
    c_in = list(comm.inputs) if comm else []
    c_out = list(comm.out_shapes) if comm else []
    c_sem = list(comm.sems) if comm else []

    def full(*refs):
        bounds = [0, n_in, len(c_in), n_out, len(c_out), n_scr, len(c_sem)]
        cuts = [sum(bounds[:i + 1]) for i in range(len(bounds))]
        ins, cins, outs, couts, scr, csems = (refs[lo:hi] for lo, hi in zip(cuts[:-1], cuts[1:]))
        if comm:
            first = functools.reduce(jnp.logical_and, [pl.program_id(ax) == 0 for ax in range(len(grid))])
            pl.when(first)(lambda: comm.start(cins, couts, csems))
            last = functools.reduce(jnp.logical_and, [pl.program_id(ax) == grid[ax] - 1 for ax in range(len(grid))])
            if comm.relay is not None:
                pl.when(last)(lambda: comm.relay(cins, couts, csems))
        body(ins, outs, scr)
        if comm:
            pl.when(last)(lambda: comm.finish(cins, couts, csems))

    res = pl.pallas_call(
        full, name=name, grid=grid, in_specs=list(in_specs) + [ANY] * len(c_in),
        out_specs=list(out_specs) + [ANY] * len(c_out), out_shape=list(out_shape) + c_out,
        scratch_shapes=list(scratch_shapes) + c_sem, compiler_params=_cparams(len(grid)),
    )(*args, *c_in)
    return list(res[:n_out]), list(res[n_out:])


def _run_comm(comm, name):
    n_in, n_out = len(comm.inputs), len(comm.out_shapes)

    def body(*refs):
        ins, outs, sems = refs[:n_in], refs[n_in:n_in + n_out], refs[n_in + n_out:]
        comm.start(ins, outs, sems)
        if comm.relay is not None:
            comm.relay(ins, outs, sems)
        comm.finish(ins, outs, sems)

    return pl.pallas_call(
        body, name=name, out_shape=list(comm.out_shapes), in_specs=[ANY] * n_in, out_specs=[ANY] * n_out,
        scratch_shapes=list(comm.sems),
    )(*comm.inputs)


def _place():
    return lax.axis_index("x"), lax.axis_index("y"), lax.axis_index("c")


def _gather_comm(arrs):
    n = len(arrs)

    def plan(ins, outs, sems):
        send_sems, recv_sems, local_sems = sems
        x, y, c = _place()
        me, sibling = (x, y, c), (x, y, 1 - c)
        chips = [(1 - x, y), (x, 1 - y), (1 - x, 1 - y)]

        def block(k, p):
            return outs[k].at[4 * p[0] + 2 * p[1] + p[2]]

        def copy(k, s, blk, to, src=None):
            return pltpu.make_async_remote_copy(
                src_ref=block(k, blk) if src is None else src, dst_ref=block(k, blk),
                send_sem=send_sems.at[7 * k + s], recv_sem=recv_sems.at[7 * k + s],
                device_id=to, device_id_type=MESH)

        mine = [pltpu.make_async_copy(ins[k], block(k, me), local_sems.at[k]) for k in range(n)]
        first = []
        for k in range(n):
            first.append(copy(k, 0, me, sibling, src=ins[k]))
            for j, chip in enumerate(chips):
                first.append(copy(k, 1 + j, me, (*chip, c), src=ins[k]))
        return me, sibling, chips, c, copy, mine, first

    def start(ins, outs, sems):
        *_, mine, first = plan(ins, outs, sems)
        for cp in mine + first:
            cp.start()

    def relay(ins, outs, sems):
        me, sibling, chips, c, copy, _, _ = plan(ins, outs, sems)
        for j, chip in enumerate(chips):
            for k in range(n):
                copy(k, 1 + j, (*chip, c), me).wait_recv()
                copy(k, 4 + j, (*chip, c), sibling).start()

    def finish(ins, outs, sems):
        me, sibling, chips, c, copy, mine, first = plan(ins, outs, sems)
        passed = [copy(k, 4 + j, (*chip, c), sibling) for j, chip in enumerate(chips) for k in range(n)]
        for k in range(n):
            copy(k, 0, sibling, me).wait_recv()
        for j, chip in enumerate(chips):
            for k in range(n):
                copy(k, 4 + j, (*chip, 1 - c), me).wait_recv()
        for cp in first + passed:
            cp.wait_send()
        for cp in mine:
            cp.wait()

    return _Comm(list(arrs), [jax.ShapeDtypeStruct((N_DEV,) + a.shape, a.dtype) for a in arrs],
                 [pltpu.SemaphoreType.DMA((7 * n,)), pltpu.SemaphoreType.DMA((7 * n,)),
                  pltpu.SemaphoreType.DMA((n,))], start, finish, relay)


def _scatter_comm(blocks):
    n = len(blocks)

    def plan(ins, outs, sems, arrivals):
        send_sems, recv_sems, local_sems = sems
        x, y, c = _place()
        me = 4 * x + 2 * y + c
        local = [pltpu.make_async_copy(ins[k].at[me], outs[k].at[me], local_sems.at[k]) for k in range(n)]
        sends, recvs = [], []
        for k in range(n):
            for mask in range(1, N_DEV):
                px = 1 - x if mask & 4 else x
                py = 1 - y if mask & 2 else y
                pc = 1 - c if mask & 1 else c
                peer = 4 * px + 2 * py + pc
                sem = 7 * k + mask - 1
                both = dict(send_sem=send_sems.at[sem], recv_sem=recv_sems.at[sem], device_id=(px, py, pc),
                            device_id_type=MESH)
                sends.append(pltpu.make_async_remote_copy(src_ref=ins[k].at[peer], dst_ref=outs[k].at[me], **both))
                if arrivals:
                    recvs.append(pltpu.make_async_remote_copy(src_ref=ins[k].at[me], dst_ref=outs[k].at[peer],
                                                              **both))
        return local, sends, recvs

    def start(ins, outs, sems):
        local, sends, _ = plan(ins, outs, sems, arrivals=False)
        for cp in local + sends:
            cp.start()

    def finish(ins, outs, sems):
        local, sends, recvs = plan(ins, outs, sems, arrivals=True)
        for cp in recvs:
            cp.wait_recv()
        for cp in sends:
            cp.wait_send()
        for cp in local:
            cp.wait()

    return _Comm(list(blocks), [jax.ShapeDtypeStruct(b.shape, b.dtype) for b in blocks],
                 [pltpu.SemaphoreType.DMA((7 * n,)), pltpu.SemaphoreType.DMA((7 * n,)),
                  pltpu.SemaphoreType.DMA((n,))], start, finish)


_DIMS = {"nn": (((1,), (0,)), ((), ())), "nt": (((1,), (1,)), ((), ())), "tn": (((0,), (0,)), ((), ()))}


class _Epilogue:
    def __init__(self, args, in_specs, out_shapes, out_specs, fn, keep_product):
        self.args, self.in_specs, self.out_shapes, self.out_specs = args, in_specs, out_shapes, out_specs
        self.fn, self.keep_product = fn, keep_product


def _row_tile(rows, cols):
    return pl.BlockSpec((rows, cols), lambda i, j: (i, 0))


def _whole(shape):
    zeros = (0,) * len(shape)
    return pl.BlockSpec(shape, lambda i, j: zeros)


def _mm(a, b, mode, out_dtype, name, bias=None, tm=512, tn=512, comm=None, cols=None, epilogue=None):
    pieces = a if isinstance(a, (list, tuple)) else [a]
    assert all(p.dtype == BF16 for p in pieces) and b.dtype == BF16
    a = pieces[0]
    if mode == "tn":
        k_dim, m_dim = a.shape
    else:
        m_dim, k_dim = a.shape
    n_dim = b.shape[0] if mode == "nt" else b.shape[1]
    col0 = 0
    if cols is not None:
        assert mode != "tn" and cols[0] % tn == 0 and cols[1] % tn == 0
        col0, n_dim = cols[0] // tn, cols[1]
    tm, tn = _pick(m_dim, tm), _pick(n_dim, tn)
    assert mode != "tn" or len(pieces) == 1
    a_specs = [pl.BlockSpec((k_dim, tm), lambda i, j: (0, i)) if mode == "tn"
               else pl.BlockSpec((tm, k_dim), lambda i, j: (i, 0))] * len(pieces)
    if mode == "nt":
        b_specs = [pl.BlockSpec((tn, k_dim), lambda i, j, p=p: (j + col0, p)) for p in range(len(pieces))]
    else:
        b_specs = [pl.BlockSpec((k_dim, tn), lambda i, j, p=p: (p, j + col0)) for p in range(len(pieces))]
    in_specs = a_specs + b_specs
    args = list(pieces) + [b] * len(pieces)
    if bias is not None:
        in_specs.append(pl.BlockSpec((1, tn), lambda i, j: (0, j + col0)))
        args.append(bias)
    dims = _DIMS[mode]
    n_pieces = len(pieces)
    n_own = len(args)
    keep = epilogue is None or epilogue.keep_product
    out_specs = [pl.BlockSpec((tm, tn), lambda i, j: (i, j))] if keep else []
    out_shape = [jax.ShapeDtypeStruct((m_dim, n_dim), out_dtype)] if keep else []
    if epilogue is not None:
        assert tn == n_dim
        in_specs, args = in_specs + list(epilogue.in_specs), args + list(epilogue.args)
        out_specs, out_shape = out_specs + list(epilogue.out_specs), out_shape + list(epilogue.out_shapes)

    def body(ins, outs, scratch):
        total = lax.dot_general(ins[0][...], ins[n_pieces][...], dims, preferred_element_type=F32)
        for p in range(1, n_pieces):
            total = total + lax.dot_general(ins[p][...], ins[n_pieces + p][...], dims, preferred_element_type=F32)
        if bias is not None:
            total = total + ins[2 * n_pieces][...]
        if keep:
            outs[0][...] = total.astype(out_dtype)
        if epilogue is not None:
            epilogue.fn(total, pl.program_id(0) == 0, ins[n_own:], outs[1:] if keep else outs)

    outs, extra = _host_call(body, name, grid=(m_dim // tm, n_dim // tn), in_specs=in_specs, out_specs=out_specs,
                             out_shape=out_shape, scratch_shapes=[], args=args, comm=comm)
    product = outs[0] if keep else None
    if comm is None and epilogue is None:
        return product
    return product, outs[1:] if keep else outs, extra


def _mm_tn_rows(pieces, b, name, tm=256):
    k_dim, n_dim = b.shape
    counts = [p.shape[1] // tm for p in pieces]
    assert all(p.shape[1] % tm == 0 for p in pieces)
    firsts = [sum(counts[:q]) for q in range(len(pieces))]

    def a_spec(first, count):
        return pl.BlockSpec((k_dim, tm), lambda i: (0, jnp.clip(i - first, 0, count - 1)))

    def body(ins, outs, scratch):
        i = pl.program_id(0)
        for a_ref, first, count in zip(ins[:-1], firsts, counts):
            @pl.when(jnp.logical_and(i >= first, i < first + count))
            def _(a_ref=a_ref):
                outs[0][...] = lax.dot_general(a_ref[...], ins[-1][...], _DIMS["tn"],
                                               preferred_element_type=F32).astype(BF16)

    (out,), _ = _host_call(
        body, name, grid=(sum(counts),),
        in_specs=[a_spec(f, c) for f, c in zip(firsts, counts)] + [_full((k_dim, n_dim))],
        out_specs=[_tile(tm, n_dim)], out_shape=[jax.ShapeDtypeStruct((sum(counts) * tm, n_dim), BF16)],
        scratch_shapes=[], args=list(pieces) + [b])
    return out


def _adam_math(w, g, m, v):
    m = ADAM_B1 * m + (1.0 - ADAM_B1) * g
    v = ADAM_B2 * v + (1.0 - ADAM_B2) * (g * g)
    m_hat = m / (1.0 - ADAM_B1 ** ADAM_STEP)
    v_hat = v / (1.0 - ADAM_B2 ** ADAM_STEP)
    delta = -ADAM_LR * (m_hat / (jnp.sqrt(v_hat) + ADAM_EPS) + ADAM_WD * w)
    return delta, m, v


def _adamw(w, m, v, name, g=None, parts=None):
    rows, cols = w.shape
    tr = rows
    if rows * cols * 4 > ADAMW_BLOCK_BYTES:
        tr = max(t for t in range(16, rows, 16) if rows % t == 0 and t * cols * 4 <= ADAMW_BLOCK_BYTES)

    def body(w_ref, m_ref, v_ref, g_ref, go_ref, d_ref, mo_ref, vo_ref):
        if parts is None:
            grad = g_ref[...]
        else:
            grad = g_ref[0].astype(F32)
            for d in range(1, N_DEV):
                grad = grad + g_ref[d].astype(F32)
        delta, m_new, v_new = _adam_math(w_ref[...], grad, m_ref[...], v_ref[...])
        go_ref[...] = grad
        d_ref[...] = delta
        mo_ref[...] = m_new
        vo_ref[...] = v_new

    spec = _tile(tr, cols)
    g_spec = spec if parts is None else pl.BlockSpec((N_DEV, tr, cols), lambda i: (0, i, 0))
    shape = jax.ShapeDtypeStruct((rows, cols), F32)
    return pl.pallas_call(
        body, name=name, out_shape=[shape] * 4, grid=(rows // tr,),
        in_specs=[spec, spec, spec, g_spec], out_specs=[spec] * 4, compiler_params=_cparams(1),
    )(w, m, v, g if parts is None else parts)


def _pack_grads(small_x, small_m, small_f, small_g, small_a, small_c, dbv, dbg, dwv, dwg, dw_conv):
    pieces = [
        (small_x, 2, D_MODEL), (small_x, 1, D_MODEL), (small_m, 4, D_MODEL), (small_m, 2, D_MODEL),
        (small_m, 1, D_MODEL), (small_f, 1, D_MODEL),
        (small_x, 0, D_MODEL), (small_m, 3, D_MODEL),
        (small_a, 0, D_ATTN), (small_a, 1, D_ATTN), (small_a, 2, D_ATTN), (small_c, 3, D_CONV),
        (small_c, 4, D_CONV), (small_g, 0, D_MODEL), (small_g, 1, D_MODEL),
        (small_c, 0, D_CONV), (small_c, 1, D_CONV), (small_c, 2, D_CONV),
        (small_g, 2, D_MODEL), (small_m, 0, D_MODEL), (small_f, 0, D_MODEL),
        (dbv, 0, D_FF), (dbg, 0, D_FF),
    ]
    pieces += [(dw_conv, j, D_CONV) for j in range(CONV_K)]
    pieces += [(src, tap, D_FF) for tap in range(3) for src in (dwv, dwg)]
    sources = [small_x, small_m, small_f, small_g, small_a, small_c, dbv, dbg, dwv, dwg, dw_conv]
    assert sum(width for _, _, width in pieces) == PACKED_TOTAL

    def body(*refs):
        o_ref = refs[-1]
        ref_of = {id(src): ref for src, ref in zip(sources, refs)}
        off = 0
        for src, row, width in pieces:
            o_ref[:, off:off + width] = ref_of[id(src)][row:row + 1, :]
            off += width

    return pl.pallas_call(body, name="pack_grads", out_shape=jax.ShapeDtypeStruct((1, PACKED_TOTAL), F32))(*sources)


def _small_adamw(gathered, gathered_rel, gathered_loss, weights, mom_m, mom_v):
    vec_names = [name for name, _ in SMALL]
    states = []
    for name in vec_names + ["rel_bias"]:
        states += [weights[name], mom_m[name], mom_v[name]]
    states = [a.reshape(a.shape[1:]) if a.ndim == 3 else a for a in states]
    n_state = len(states)

    def body(*refs):
        g_ref, rel_ref, loss_ref = refs[0], refs[1], refs[2]
        state_refs, out_refs = refs[3:3 + n_state], refs[3 + n_state:]
        total = g_ref[0:1, :]
        rel = rel_ref[0]
        loss = loss_ref[0]
        for d in range(1, N_DEV):
            total = total + g_ref[d:d + 1, :]
            rel = rel + rel_ref[d]
            loss = loss + loss_ref[d]
        off = 0
        for n, (name, width) in enumerate(SMALL):
            grad = total[:, off:off + width]
            w_ref, m_ref, v_ref = state_refs[3 * n:3 * n + 3]
            for ref, val in zip(out_refs[4 * n:4 * n + 4], (grad,) + _adam_math(w_ref[...], grad, m_ref[...], v_ref[...])):
                ref[...] = val
            off += width
        n = len(SMALL)
        w_ref, m_ref, v_ref = state_refs[3 * n:3 * n + 3]
        for ref, val in zip(out_refs[4 * n:4 * n + 4], (rel,) + _adam_math(w_ref[...], rel, m_ref[...], v_ref[...])):
            ref[...] = val
        dwc_ref, dwf_ref, loss_out = out_refs[4 * n + 4:]
        loss_out[...] = 0.5 * loss
        dwc_ref[...] = jnp.zeros_like(dwc_ref)
        dwf_ref[...] = jnp.zeros_like(dwf_ref)
        for j in range(CONV_K):
            dwc_ref[j:j + 1, :] = total[:, off:off + D_CONV]
            off += D_CONV
        for tap in range(3):
            dwf_ref[tap:tap + 1, :] = total[:, off:off + 2 * D_FF]
            off += 2 * D_FF

    out_shape = []
    for k in range(n_state // 3):
        out_shape += [jax.ShapeDtypeStruct(states[3 * k].shape, F32)] * 4
    out_shape += [jax.ShapeDtypeStruct((CONV_HALO, D_CONV), F32), jax.ShapeDtypeStruct((8, 2 * D_FF), F32),
                  jax.ShapeDtypeStruct((1, 128), F32)]
    res = pl.pallas_call(
        body, name="small_adamw", out_shape=out_shape,
        compiler_params=pltpu.CompilerParams(vmem_limit_bytes=VMEM_LIMIT_BYTES),
    )(gathered, gathered_rel, gathered_loss, *states)
    updates = {name: tuple(res[4 * n:4 * n + 4]) for n, name in enumerate(vec_names + ["rel_bias"])}
    return updates, res[-3], res[-2], res[-1]


def _ada_mod(c, w_shard):
    cols = w_shard.shape[1]

    def body(c_ref, w_ref, call_ref, mod_ref, send_sems, recv_sems):
        x, y, cc = _place()
        me = 4 * x + 2 * y + cc

        def exchange(ref, phase):
            sends, arrivals = [], []
            for mask in range(1, N_DEV):
                px = 1 - x if mask & 4 else x
                py = 1 - y if mask & 2 else y
                pc = 1 - cc if mask & 1 else cc
                both = dict(send_sem=send_sems.at[7 * phase + mask - 1], recv_sem=recv_sems.at[7 * phase + mask - 1],
                            device_id=(px, py, pc), device_id_type=MESH)
                sends.append(pltpu.make_async_remote_copy(src_ref=ref.at[me], dst_ref=ref.at[me], **both))
                arrivals.append(pltpu.make_async_remote_copy(src_ref=ref.at[me], dst_ref=ref.at[4 * px + 2 * py + pc],
                                                             **both))
            for cp in sends:
                cp.start()
            for cp in arrivals:
                cp.wait_recv()
            for cp in sends:
                cp.wait_send()

        v = c_ref[...]
        call_ref[me] = v * _sig(v)
        exchange(call_ref, 0)
        c_all = jnp.concatenate([call_ref[d] for d in range(N_DEV)], axis=0)
        mod_ref[me] = jnp.dot(c_all, w_ref[...], precision=HIGHEST, preferred_element_type=F32)
        exchange(mod_ref, 1)

    return pl.pallas_call(
        body, name="ada_mod",
        out_shape=[jax.ShapeDtypeStruct((N_DEV, 1, D_MODEL), F32), jax.ShapeDtypeStruct((N_DEV, N_DEV, cols), F32)],
        scratch_shapes=[pltpu.SemaphoreType.DMA((14,)), pltpu.SemaphoreType.DMA((14,))],
        compiler_params=pltpu.CompilerParams(vmem_limit_bytes=VMEM_LIMIT_BYTES),
    )(c, w_shard)


def _ada_grad(c_all, dmod_shard):
    def body(c_ref, d_ref, o_ref):
        o_ref[...] = lax.dot_general(c_ref[...], d_ref[...], _DIMS["tn"], precision=HIGHEST,
                                     preferred_element_type=F32)

    return pl.pallas_call(
        body, name="ada_grad", out_shape=jax.ShapeDtypeStruct((D_MODEL, dmod_shard.shape[1]), F32),
        compiler_params=pltpu.CompilerParams(vmem_limit_bytes=VMEM_LIMIT_BYTES),
    )(c_all, dmod_shard)


ROWS = 256


def _rms(v):
    r = lax.rsqrt(jnp.mean(v * v, axis=-1, keepdims=True) + EPS)
    return v * r, r


def _rms_bwd(dxn, xn, r):
    return r * (dxn - xn * jnp.mean(dxn * xn, axis=-1, keepdims=True))


def _colsum(v):
    return jnp.sum(v, axis=0, keepdims=True)


def _pre_mix(x, mod6, g1, comm=None):
    seq = x.shape[0]

    def body(ins, outs, scratch):
        x_ref, mod_ref, g_ref = ins
        xn, _ = _rms(x_ref[...])
        y = xn * g_ref[...]
        outs[0][...] = (y * (1.0 + mod_ref[SC_M:SC_M + 1, :]) + mod_ref[SH_M:SH_M + 1, :]).astype(BF16)

    (h,), extra = _host_call(
        body, "pre_mix", grid=(seq // ROWS,),
        in_specs=[_tile(ROWS, D_MODEL), _full((6, D_MODEL)), _full((1, D_MODEL))], out_specs=[_tile(ROWS, D_MODEL)],
        out_shape=[jax.ShapeDtypeStruct((seq, D_MODEL), BF16)], scratch_shapes=[], args=[x, mod6, g1], comm=comm)
    return h, extra


def _post_mix_pre_ffn(x, mod6, g2, g3, rows):
    seq = x.shape[0]

    def fn(y, first, ins, outs):
        x_ref, mod_ref, g2_ref, g3_ref = ins
        x1_ref, h_ref = outs
        yn, _ = _rms(y)
        x1 = x_ref[...] + mod_ref[GT_M:GT_M + 1, :] * (yn * g2_ref[...])
        x1_ref[...] = x1
        xn, _ = _rms(x1)
        y3 = xn * g3_ref[...]
        h_ref[...] = (y3 * (1.0 + mod_ref[SC_F:SC_F + 1, :]) + mod_ref[SH_F:SH_F + 1, :]).astype(BF16)

    return _Epilogue(
        [x, mod6, g2, g3], [_row_tile(rows, D_MODEL), _whole((6, D_MODEL)), _whole((1, D_MODEL)), _whole((1, D_MODEL))],
        [jax.ShapeDtypeStruct((seq, D_MODEL), F32), jax.ShapeDtypeStruct((seq, D_MODEL), BF16)],
        [_row_tile(rows, D_MODEL), _row_tile(rows, D_MODEL)], fn, keep_product=True)


def _final(x1, target, mod6, g4, rows):
    seq = x1.shape[0]

    def fn(y, first, ins, outs):
        x1_ref, t_ref, mod_ref, g_ref = ins
        loss_ref, dout_ref, dyf_ref, small_ref = outs

        @pl.when(first)
        def _():
            loss_ref[...] = jnp.zeros_like(loss_ref)
            small_ref[...] = jnp.zeros_like(small_ref)

        gt = mod_ref[GT_F:GT_F + 1, :]
        g4v = g_ref[...]
        yn, r = _rms(y)
        out = x1_ref[...] + gt * (yn * g4v)
        err = out - t_ref[...]
        loss_ref[...] += jnp.sum(jnp.mean(err * err, axis=-1, keepdims=True))
        dout = err * (1.0 / D_MODEL)
        dout_ref[...] = dout
        small_ref[0:1, :] += _colsum(dout * gt * yn)
        small_ref[1:2, :] += _colsum(dout * (yn * g4v))
        dyf_ref[...] = _rms_bwd(dout * gt * g4v, yn, r).astype(BF16)

    return _Epilogue(
        [x1, target, mod6, g4],
        [_row_tile(rows, D_MODEL), _row_tile(rows, D_MODEL), _whole((6, D_MODEL)), _whole((1, D_MODEL))],
        [jax.ShapeDtypeStruct((1, 128), F32), jax.ShapeDtypeStruct((seq, D_MODEL), F32),
         jax.ShapeDtypeStruct((seq, D_MODEL), BF16), jax.ShapeDtypeStruct((8, D_MODEL), F32)],
        [_whole((1, 128)), _row_tile(rows, D_MODEL), _row_tile(rows, D_MODEL), _whole((8, D_MODEL))],
        fn, keep_product=False)


def _mid_bwd(x1, dout, ymix, mod6, g3, g2, rows):
    seq = x1.shape[0]

    def fn(dh, first, ins, outs):
        x1_ref, dout_ref, y_ref, mod_ref, g3_ref, g2_ref = ins
        dx1_ref, dy_ref, small_ref = outs

        @pl.when(first)
        def _():
            small_ref[...] = jnp.zeros_like(small_ref)

        g3v, g2v = g3_ref[...], g2_ref[...]
        xn, r3 = _rms(x1_ref[...])
        y3 = xn * g3v
        dy3 = dh * (1.0 + mod_ref[SC_F:SC_F + 1, :])
        small_ref[0:1, :] += _colsum(dy3 * xn)
        small_ref[1:2, :] += _colsum(dh * y3)
        small_ref[2:3, :] += _colsum(dh)
        dx1 = dout_ref[...] + _rms_bwd(dy3 * g3v, xn, r3)
        dx1_ref[...] = dx1
        gt = mod_ref[GT_M:GT_M + 1, :]
        yn, r2 = _rms(y_ref[...])
        small_ref[3:4, :] += _colsum(dx1 * gt * yn)
        small_ref[4:5, :] += _colsum(dx1 * (yn * g2v))
        dy_ref[...] = _rms_bwd(dx1 * gt * g2v, yn, r2).astype(BF16)

    return _Epilogue(
        [x1, dout, ymix, mod6, g3, g2],
        [_row_tile(rows, D_MODEL)] * 3 + [_whole((6, D_MODEL)), _whole((1, D_MODEL)), _whole((1, D_MODEL))],
        [jax.ShapeDtypeStruct((seq, D_MODEL), F32), jax.ShapeDtypeStruct((seq, D_MODEL), BF16),
         jax.ShapeDtypeStruct((8, D_MODEL), F32)],
        [_row_tile(rows, D_MODEL), _row_tile(rows, D_MODEL), _whole((8, D_MODEL))], fn, keep_product=False)


def _pre_mix_bwd(x, dx1, mod6, g1, rows):
    seq = x.shape[0]

    def fn(dh, first, ins, outs):
        x_ref, dx1_ref, mod_ref, g_ref = ins
        dx_ref, small_ref = outs

        @pl.when(first)
        def _():
            small_ref[...] = jnp.zeros_like(small_ref)

        g1v = g_ref[...]
        xn, r = _rms(x_ref[...])
        dy = dh * (1.0 + mod_ref[SC_M:SC_M + 1, :])
        small_ref[0:1, :] += _colsum(dy * xn)
        small_ref[1:2, :] += _colsum(dh * (xn * g1v))
        small_ref[2:3, :] += _colsum(dh)
        dx_ref[...] = dx1_ref[...] + _rms_bwd(dy * g1v, xn, r)

    return _Epilogue(
        [x, dx1, mod6, g1],
        [_row_tile(rows, D_MODEL), _row_tile(rows, D_MODEL), _whole((6, D_MODEL)), _whole((1, D_MODEL))],
        [jax.ShapeDtypeStruct((seq, D_MODEL), F32), jax.ShapeDtypeStruct((8, D_MODEL), F32)],
        [_row_tile(rows, D_MODEL), _whole((8, D_MODEL))], fn, keep_product=False)


def _toeplitz_onehot(shape, offset_axis, top):
    m = lax.broadcasted_iota(jnp.int32, shape, offset_axis)
    i = lax.broadcasted_iota(jnp.int32, shape, 1 - offset_axis)
    return (i == jnp.clip(top - m, -MAX_REL, MAX_REL) + MAX_REL).astype(F32)


def _bias_table(rel_bias):
    width = GROUP_Q + GROUP_K

    def body(rb_ref, o_ref, t_ref):
        t_ref[...] = jnp.dot(rb_ref[...], _toeplitz_onehot((N_REL, width), 1, GROUP_K - 1), precision=HIGHEST,
                             preferred_element_type=F32)
        lane = lax.broadcasted_iota(jnp.int32, (N_HEADS, GROUP_K), 1)
        for r in range(GROUP_Q):
            first_key = (r // CHUNK) * CHUNK
            band = jnp.logical_and(lane >= first_key, lane < first_key + BAND)
            o_ref[r] = jnp.where(band, t_ref[:, GROUP_Q - 1 - r:GROUP_Q - 1 - r + GROUP_K], NEG_INF)

    return pl.pallas_call(
        body, name="bias_table", out_shape=jax.ShapeDtypeStruct((GROUP_Q, N_HEADS, GROUP_K), F32),
        scratch_shapes=[pltpu.VMEM((N_HEADS, width), F32)],
    )(rel_bias)


def _bias_grad(dbias_q):
    def body(d_ref, o_ref, t_ref):
        t_ref[...] = jnp.zeros_like(t_ref)
        for qi in range(CHUNK):
            t_ref[:, CHUNK - 1 - qi:CHUNK - 1 - qi + BAND] += d_ref[qi]
        o_ref[...] = jnp.dot(t_ref[...], _toeplitz_onehot((TOEPLITZ, N_REL), 0, BAND - 1), precision=HIGHEST,
                             preferred_element_type=F32)

    return pl.pallas_call(
        body, name="bias_grad", out_shape=jax.ShapeDtypeStruct((N_HEADS, N_REL), F32),
        scratch_shapes=[pltpu.VMEM((N_HEADS, TOEPLITZ), F32)],
    )(dbias_q)


def _load_resident(pairs, sems):
    copies = [pltpu.make_async_copy(src, dst, sems.at[n]) for n, (src, dst) in enumerate(pairs)]
    for cp in copies:
        cp.start()
    for cp in copies:
        cp.wait()


def _softmax_rows(s_ref, t_ref, before_start, rows):
    s = s_ref[rows, :] * (HEAD_DIM ** -0.5) + t_ref[rows, :] + before_start
    e = jnp.exp(s - jnp.max(s, axis=-1, keepdims=True))
    return e / jnp.sum(e, axis=-1, keepdims=True)


def _before_start(g):
    kj = lax.broadcasted_iota(jnp.int32, (8, GROUP_K), 1)
    return jnp.where(kj >= PAD_ROWS - g * GROUP_Q, 0.0, NEG_INF)


def _attn_fwd(qkv, kpad, vpad, table, comm=None):
    seq = qkv.shape[0]

    def body(ins, outs, scratch):
        q_ref, k_hbm, v_hbm, t_hbm = ins
        (o_ref,) = outs
        k_ref, v_ref, t_ref, s_ref, p_ref, sems = scratch
        g = pl.program_id(0)

        @pl.when(g == 0)
        def _():
            _load_resident(((k_hbm, k_ref), (v_hbm, v_ref), (t_hbm, t_ref)), sems)

        window = pl.ds(pl.multiple_of(g * GROUP_Q, GROUP_Q), GROUP_K)
        before_start = _before_start(g)
        for h in range(N_HEADS):
            cols = slice(h * HEAD_DIM, (h + 1) * HEAD_DIM)
            buf = h % 2
            s_ref[buf] = lax.dot_general(q_ref[:, cols], k_ref[window, cols], _DIMS["nt"],
                                         preferred_element_type=F32)
            for row in range(0, GROUP_Q, SOFTMAX_ROWS):
                halves = [_softmax_rows(s_ref.at[buf], t_ref.at[h], before_start, slice(r, r + 8))
                          for r in (row, row + 8)]
                p_ref[buf, row:row + SOFTMAX_ROWS, :] = jnp.concatenate(halves, axis=0).astype(BF16)
            o_ref[:, cols] = jnp.dot(p_ref[buf], v_ref[window, cols], preferred_element_type=F32).astype(BF16)

    (ao,), extra = _host_call(
        body, "attn_fwd", grid=(seq // GROUP_Q,),
        in_specs=[_tile(GROUP_Q, D_ATTN), ANY, ANY, ANY], out_specs=[_tile(GROUP_Q, D_ATTN)],
        out_shape=[jax.ShapeDtypeStruct((seq, D_ATTN), BF16)],
        scratch_shapes=[pltpu.VMEM(kpad.shape, BF16), pltpu.VMEM(vpad.shape, BF16), pltpu.VMEM(table.shape, F32),
                        pltpu.VMEM((2, GROUP_Q, GROUP_K), F32), pltpu.VMEM((2, GROUP_Q, GROUP_K), BF16),
                        pltpu.SemaphoreType.DMA((3,))],
        args=[qkv, kpad, vpad, table], comm=comm)
    return ao, extra


def _attn_bwd(qkv, kpad, vpad, table, dao, comm=None):
    seq = qkv.shape[0]
    n_groups = seq // GROUP_Q
    fold_w = GROUP_K + (GROUP - 1) * CHUNK

    def body(ins, outs, scratch):
        q_ref, do_ref, k_hbm, v_hbm, t_hbm = ins
        dq_ref, dkt_hbm, dvt_hbm, db_ref, cs_ref = outs
        k_ref, v_ref, t_ref, db_acc, dkt_acc, dvt_acc, s_ref, dp_ref, p_ref, ds_ref, sems = scratch
        g = pl.program_id(0)

        @pl.when(g == 0)
        def _():
            _load_resident(((k_hbm, k_ref), (v_hbm, v_ref), (t_hbm, t_ref)), sems)
            db_acc[...] = jnp.zeros_like(db_acc)
            dkt_acc[...] = jnp.zeros_like(dkt_acc)
            dvt_acc[...] = jnp.zeros_like(dvt_acc)
            cs_ref[...] = jnp.zeros_like(cs_ref)

        window = pl.ds(pl.multiple_of(g * GROUP_Q, GROUP_Q), GROUP_K)
        before_start = _before_start(g)
        for h in range(N_HEADS):
            cols = slice(h * HEAD_DIM, (h + 1) * HEAD_DIM)
            buf = h % 2
            qh, doh = q_ref[:, cols], do_ref[:, cols]
            kh, vh = k_ref[window, cols], v_ref[window, cols]
            s_ref[buf] = lax.dot_general(qh, kh, _DIMS["nt"], preferred_element_type=F32)
            dp_ref[buf] = lax.dot_general(doh, vh, _DIMS["nt"], preferred_element_type=F32)
            for row in range(0, GROUP_Q, SOFTMAX_ROWS):
                p_halves, ds_halves = [], []
                for r in (row, row + 8):
                    p = _softmax_rows(s_ref.at[buf], t_ref.at[h], before_start, slice(r, r + 8))
                    dp = dp_ref[buf, r:r + 8, :]
                    ds = p * (dp - jnp.sum(dp * p, axis=-1, keepdims=True))
                    chunk = r // CHUNK
                    shift = (GROUP - 1 - chunk) * CHUNK
                    db_acc[h, r - chunk * CHUNK:r - chunk * CHUNK + 8, shift:shift + GROUP_K] += ds
                    p_halves.append(p)
                    ds_halves.append(ds * (HEAD_DIM ** -0.5))
                p_ref[buf, row:row + SOFTMAX_ROWS, :] = jnp.concatenate(p_halves, axis=0).astype(BF16)
                ds_ref[buf, row:row + SOFTMAX_ROWS, :] = jnp.concatenate(ds_halves, axis=0).astype(BF16)
            dq_ref[:, cols] = jnp.dot(ds_ref[buf], kh, preferred_element_type=F32).astype(BF16)
            dkt_acc[cols, window] += lax.dot_general(qh, ds_ref[buf], _DIMS["tn"], preferred_element_type=F32)
            dvt_acc[cols, window] += lax.dot_general(doh, p_ref[buf], _DIMS["tn"], preferred_element_type=F32)
        cs_ref[0:1, :] += _colsum(dq_ref[...].astype(F32))

        @pl.when(g == n_groups - 1)
        def _():
            lo = (GROUP - 1) * CHUNK
            for h in range(N_HEADS):
                db_ref[h] = db_acc[h, :, lo:lo + BAND]
            inside = pl.ds(PAD_ROWS, seq)
            on_diagonal = (lax.broadcasted_iota(jnp.int32, (D_ATTN, D_ATTN), 0)
                           == lax.broadcasted_iota(jnp.int32, (D_ATTN, D_ATTN), 1))
            for row, acc in ((1, dkt_acc), (2, dvt_acc)):
                column = jnp.sum(acc[:, inside], axis=1, keepdims=True)
                cs_ref[row:row + 1, :] = _colsum(jnp.where(on_diagonal, column, 0.0))
            out_k = pltpu.make_async_copy(dkt_acc.at[:, inside], dkt_hbm, sems.at[0])
            out_v = pltpu.make_async_copy(dvt_acc.at[:, inside], dvt_hbm, sems.at[1])
            out_k.start()
            out_v.start()
            out_k.wait()
            out_v.wait()

    t_shape = (D_ATTN, seq + PAD_ROWS)
    outs, extra = _host_call(
        body, "attn_bwd", grid=(n_groups,),
        in_specs=[_tile(GROUP_Q, D_ATTN), _tile(GROUP_Q, D_ATTN), ANY, ANY, ANY],
        out_specs=[_tile(GROUP_Q, D_ATTN), ANY, ANY, _full((N_HEADS, CHUNK, BAND)), _full((8, D_ATTN))],
        out_shape=[jax.ShapeDtypeStruct((seq, D_ATTN), BF16), jax.ShapeDtypeStruct((D_ATTN, seq), F32),
                   jax.ShapeDtypeStruct((D_ATTN, seq), F32), jax.ShapeDtypeStruct((N_HEADS, CHUNK, BAND), F32),
                   jax.ShapeDtypeStruct((8, D_ATTN), F32)],
        scratch_shapes=[pltpu.VMEM(kpad.shape, BF16), pltpu.VMEM(vpad.shape, BF16), pltpu.VMEM(table.shape, F32),
                        pltpu.VMEM((N_HEADS, CHUNK, fold_w), F32), pltpu.VMEM(t_shape, F32),
                        pltpu.VMEM(t_shape, F32), pltpu.VMEM((2, GROUP_Q, GROUP_K), F32),
                        pltpu.VMEM((2, GROUP_Q, GROUP_K), F32), pltpu.VMEM((2, GROUP_Q, GROUP_K), BF16),
                        pltpu.VMEM((2, GROUP_Q, GROUP_K), BF16), pltpu.SemaphoreType.DMA((3,))],
        args=[qkv, dao, kpad, vpad, table], comm=comm)
    return outs, extra


def _assemble_dz(dq, dkt, dvt, dglu_a, dglu_b, dga, dgb):
    seq = dq.shape[0]
    rows = 512
    transposed = pl.BlockSpec((D_ATTN, rows), lambda i: (0, i))

    def body(dq_ref, dkt_ref, dvt_ref, da_ref, db_ref, dga_ref, dgb_ref, o_ref):
        o_ref[:, 0:D_ATTN] = dq_ref[...]
        o_ref[:, D_ATTN:2 * D_ATTN] = dkt_ref[...].T.astype(BF16)
        o_ref[:, 2 * D_ATTN:3 * D_ATTN] = dvt_ref[...].T.astype(BF16)
        off = 3 * D_ATTN
        for ref in (da_ref, db_ref, dga_ref, dgb_ref):
            width = ref.shape[1]
            o_ref[:, off:off + width] = ref[...]
            off += width

    width = 3 * D_ATTN + 2 * D_CONV + 2 * D_MODEL
    return pl.pallas_call(
        body, name="assemble_dz", out_shape=jax.ShapeDtypeStruct((seq, width), BF16), grid=(seq // rows,),
        in_specs=[_tile(rows, D_ATTN), transposed, transposed, _tile(rows, D_CONV), _tile(rows, D_CONV),
                  _tile(rows, D_MODEL), _tile(rows, D_MODEL)],
        out_specs=_tile(rows, width), compiler_params=_cparams(1),
    )(dq, dkt, dvt, dglu_a, dglu_b, dga, dgb)


CONV_ROWS = 256


def _ln_silu(u1, g, b):
    mu = jnp.mean(u1, axis=-1, keepdims=True)
    xc = u1 - mu
    rs = lax.rsqrt(jnp.mean(xc * xc, axis=-1, keepdims=True) + EPS)
    xhat = xc * rs
    u2 = xhat * g + b
    return xhat, rs, u2


def _glu_into(s_ref, a_ref, b_ref, ah_ref, bh_ref, first):
    halo = ah_ref[...] * _sig(bh_ref[...])
    s_ref[0:CONV_HALO, :] = jnp.where(first, 0.0, halo)
    s_ref[CONV_HALO:CONV_HALO + CONV_ROWS, :] = a_ref[...] * _sig(b_ref[...])


CONV_LANES = 128
CONV_TILES = CONV_ROWS // 8


def _lag_weights(w_ref, lanes):
    return {e: jnp.broadcast_to(w_ref[CONV_K - 1 - e:CONV_K - e, lanes], (8, CONV_LANES)) for e in range(CONV_K)}


def _class_sums(w, tiles, k):
    total = None
    for a, tile in enumerate(tiles):
        if 8 * a + k < CONV_K:
            term = w[8 * a + k] * tile
            total = term if total is None else total + term
    return total


def _conv_back(src_ref, first_tile, w, lanes, row_id, emit):
    before = None
    for m in range(-1, CONV_TILES):
        tiles = [src_ref[8 * (first_tile + m - a):8 * (first_tile + m - a) + 8, lanes] for a in range(4)]
        rolled = [None] + [pltpu.roll(_class_sums(w, tiles, k), k, 0) for k in range(1, 8)]
        if m >= 0:
            out = _class_sums(w, tiles, 0)
            for k in range(1, 8):
                out = out + jnp.where(row_id < k, before[k], rolled[k])
            emit(m, out)
        before = rolled


def _conv_ahead(src_ref, w, lanes, row_id, emit):
    before = None
    for m in range(CONV_TILES + 1):
        tiles = [src_ref[8 * (m + a):8 * (m + a) + 8, lanes] for a in range(4)]
        rolled = [None] + [pltpu.roll(_class_sums(w, tiles, k), 8 - k, 0) for k in range(1, 8)]
        if m >= 1:
            out = before[0]
            for k in range(1, 8):
                out = out + jnp.where(row_id < 8 - k, before[k], rolled[k])
            emit(m - 1, out)
        before = [_class_sums(w, tiles, 0) if m < CONV_TILES else None] + rolled[1:]


def _conv_weight_sums(d_ref, s_ref, lanes, row_id, whole_shifts):
    zero = jnp.zeros((8, CONV_LANES), F32)
    sums = {8 * a + k: zero for a in whole_shifts for k in range(8) if 8 * a + k < CONV_K}

    def d_tile(m):
        return d_ref[8 * m:8 * m + 8, lanes] if 0 <= m < CONV_TILES else zero

    rolled = [None] + [zero] * 7
    for m in range(-1, CONV_TILES):
        cur, nxt = d_tile(m), d_tile(m + 1)
        rolled_next = [None] + [pltpu.roll(nxt, 8 - k, 0) for k in range(1, 8)]
        shifted = [cur] + [jnp.where(row_id < 8 - k, rolled[k], rolled_next[k]) for k in range(1, 8)]
        for a in whole_shifts:
            tile = s_ref[8 * (CONV_HALO // 8 + m - a):8 * (CONV_HALO // 8 + m - a) + 8, lanes]
            for k in range(8):
                if 8 * a + k < CONV_K and not (m < 0 and k == 0):
                    sums[8 * a + k] = sums[8 * a + k] + shifted[k] * tile
        rolled = rolled_next
    return sums


def _conv_fwd(zr, w_dw, b_dw, g_ln, b_ln, comm=None):
    seq = zr.shape[0]

    def body(a_ref, b_ref, ah_ref, bh_ref, w_ref, bias_ref, g_ref, bl_ref, u1_ref, u3_ref, s_ref):
        _glu_into(s_ref, a_ref, b_ref, ah_ref, bh_ref, pl.program_id(0) == 0)
        row_id = lax.broadcasted_iota(jnp.int32, (8, CONV_LANES), 0)
        for lo in range(0, D_CONV, CONV_LANES):
            lanes = slice(lo, lo + CONV_LANES)
            bias = jnp.broadcast_to(bias_ref[:, lanes], (8, CONV_LANES))

            def emit(m, out, lanes=lanes, bias=bias):
                u1_ref[8 * m:8 * m + 8, lanes] = out + bias

            _conv_back(s_ref, CONV_HALO // 8, _lag_weights(w_ref, lanes), lanes, row_id, emit)
        _, _, u2 = _ln_silu(u1_ref[...], g_ref[...], bl_ref[...])
        u3_ref[...] = (u2 * _sig(u2)).astype(BF16)

    return _host_call(
        lambda ins, outs, scratch: body(*ins, *outs, *scratch), "conv_fwd", grid=(seq // CONV_ROWS,),
        in_specs=[_tile(CONV_ROWS, D_CONV, 0), _tile(CONV_ROWS, D_CONV, 1),
                  _prev(CONV_HALO, D_CONV, CONV_ROWS, 0), _prev(CONV_HALO, D_CONV, CONV_ROWS, 1),
                  _full((CONV_K, D_CONV)), _full((1, D_CONV)), _full((1, D_CONV)), _full((1, D_CONV))],
        out_specs=[_tile(CONV_ROWS, D_CONV), _tile(CONV_ROWS, D_CONV)],
        out_shape=[jax.ShapeDtypeStruct((seq, D_CONV), F32), jax.ShapeDtypeStruct((seq, D_CONV), BF16)],
        scratch_shapes=[pltpu.VMEM((CONV_HALO + CONV_ROWS, D_CONV), F32)],
        args=[zr, zr, zr, zr, w_dw, b_dw, g_ln, b_ln], comm=comm)


def _conv_bwd(zr, u1, du3, w_dw, g_ln, b_ln, comm=None):
    seq = zr.shape[0]
    n_tiles = seq // CONV_ROWS
    n_halo = seq // CONV_HALO
    ext = CONV_ROWS + CONV_HALO

    def body(a_ref, b_ref, ah_ref, bh_ref, u1_ref, u1n_ref, d3_ref, d3n_ref, w_ref, g_ref, bl_ref,
             da_ref, db_ref, dw_ref, small_ref, s_ref, d_ref, du0_ref):
        i = pl.program_id(0)

        @pl.when(i == 0)
        def _():
            dw_ref[...] = jnp.zeros_like(dw_ref)
            small_ref[...] = jnp.zeros_like(small_ref)

        _glu_into(s_ref, a_ref, b_ref, ah_ref, bh_ref, i == 0)
        gv, bv = g_ref[...], bl_ref[...]

        def du1_of(u1, d3):
            xhat, rs, u2 = _ln_silu(u1, gv, bv)
            sg = _sig(u2)
            du2 = d3 * (sg * (1.0 + u2 * (1.0 - sg)))
            dxh = du2 * gv
            du1 = rs * (dxh - jnp.mean(dxh, axis=-1, keepdims=True)
                        - xhat * jnp.mean(dxh * xhat, axis=-1, keepdims=True))
            return du1, du2, xhat

        du1, du2, xhat = du1_of(u1_ref[...], d3_ref[...])
        du1n, _, _ = du1_of(u1n_ref[...], d3n_ref[...])
        d_ref[0:CONV_ROWS, :] = du1
        d_ref[CONV_ROWS:ext, :] = jnp.where(i == n_tiles - 1, 0.0, du1n)
        small_ref[0:1, :] += _colsum(du1)
        small_ref[1:2, :] += _colsum(du2 * xhat)
        small_ref[2:3, :] += _colsum(du2)
        row_id = lax.broadcasted_iota(jnp.int32, (8, CONV_LANES), 0)
        for lo in range(0, D_CONV, CONV_LANES):
            lanes = slice(lo, lo + CONV_LANES)

            def emit(m, out, lanes=lanes):
                du0_ref[8 * m:8 * m + 8, lanes] = out

            _conv_ahead(d_ref, _lag_weights(w_ref, lanes), lanes, row_id, emit)
            for whole_shifts in ((0, 1), (2, 3)):
                for e, total in _conv_weight_sums(d_ref, s_ref, lanes, row_id, whole_shifts).items():
                    dw_ref[CONV_K - 1 - e:CONV_K - e, lanes] += _colsum(total)
        du0 = du0_ref[...]
        sb = _sig(b_ref[...])
        da = du0 * sb
        dbv = du0 * a_ref[...] * sb * (1.0 - sb)
        da_ref[...] = da.astype(BF16)
        db_ref[...] = dbv.astype(BF16)
        small_ref[3:4, :] += _colsum(da)
        small_ref[4:5, :] += _colsum(dbv)

    return _host_call(
        lambda ins, outs, scratch: body(*ins, *outs, *scratch), "conv_bwd", grid=(n_tiles,),
        in_specs=[_tile(CONV_ROWS, D_CONV, 0), _tile(CONV_ROWS, D_CONV, 1),
                  _prev(CONV_HALO, D_CONV, CONV_ROWS, 0), _prev(CONV_HALO, D_CONV, CONV_ROWS, 1),
                  _tile(CONV_ROWS, D_CONV), _next(CONV_HALO, D_CONV, CONV_ROWS, n_halo),
                  _tile(CONV_ROWS, D_CONV), _next(CONV_HALO, D_CONV, CONV_ROWS, n_halo),
                  _full((CONV_K, D_CONV)), _full((1, D_CONV)), _full((1, D_CONV))],
        out_specs=[_tile(CONV_ROWS, D_CONV), _tile(CONV_ROWS, D_CONV), _full((CONV_HALO, D_CONV)),
                   _full((8, D_CONV))],
        out_shape=[jax.ShapeDtypeStruct((seq, D_CONV), BF16), jax.ShapeDtypeStruct((seq, D_CONV), BF16),
                   jax.ShapeDtypeStruct((CONV_HALO, D_CONV), F32), jax.ShapeDtypeStruct((8, D_CONV), F32)],
        scratch_shapes=[pltpu.VMEM((ext, D_CONV), F32), pltpu.VMEM((ext, D_CONV), F32),
                        pltpu.VMEM((CONV_ROWS, D_CONV), F32)],
        args=[zr, zr, zr, zr, u1, u1, du3, du3, w_dw, g_ln, b_ln], comm=comm)


MERGE_ROWS = 256


def _merge_fwd(ao, u3, zr, w_ao, w_co, b_co):
    seq = ao.shape[0]

    def body(ao_ref, u3_ref, ga_ref, gb_ref, wa_ref, wc_ref, bc_ref, y_ref, a_ref, cb_ref):
        a = jnp.dot(ao_ref[...], wa_ref[...], preferred_element_type=F32)
        cb = jnp.dot(u3_ref[...], wc_ref[...], preferred_element_type=F32) + bc_ref[...]
        a_ref[...] = a
        cb_ref[...] = cb
        y_ref[...] = (_sig(ga_ref[...]) * a + _sig(gb_ref[...]) * cb).astype(BF16)

    f32_out = jax.ShapeDtypeStruct((seq, D_MODEL), F32)
    return pl.pallas_call(
        body, name="merge_fwd",
        out_shape=[jax.ShapeDtypeStruct((seq, D_MODEL), BF16), f32_out, f32_out],
        grid=(seq // MERGE_ROWS,),
        in_specs=[_tile(MERGE_ROWS, D_ATTN), _tile(MERGE_ROWS, D_CONV), _tile(MERGE_ROWS, D_MODEL, 1),
                  _tile(MERGE_ROWS, D_MODEL, 2), _full(w_ao.shape), _full(w_co.shape), _full((1, D_MODEL))],
        out_specs=[_tile(MERGE_ROWS, D_MODEL)] * 3, compiler_params=_cparams(1),
    )(ao, u3, zr, zr, w_ao, w_co, b_co)


def _merge_bwd(a, cb, zr, rows):
    seq = a.shape[0]

    def fn(dy_v, first, ins, outs):
        a_ref, cb_ref, ga_ref, gb_ref = ins
        da_ref, dcb_ref, dga_ref, dgb_ref, small_ref = outs

        @pl.when(first)
        def _():
            small_ref[...] = jnp.zeros_like(small_ref)

        sa, sb = _sig(ga_ref[...]), _sig(gb_ref[...])
        dcb = dy_v * sb
        dga = dy_v * a_ref[...] * sa * (1.0 - sa)
        dgb = dy_v * cb_ref[...] * sb * (1.0 - sb)
        da_ref[...] = (dy_v * sa).astype(BF16)
        dcb_ref[...] = dcb.astype(BF16)
        dga_ref[...] = dga.astype(BF16)
        dgb_ref[...] = dgb.astype(BF16)
        small_ref[0:1, :] += _colsum(dga)
        small_ref[1:2, :] += _colsum(dgb)
        small_ref[2:3, :] += _colsum(dcb)

    bf = jax.ShapeDtypeStruct((seq, D_MODEL), BF16)
    gate = lambda col: pl.BlockSpec((rows, D_MODEL), lambda i, j: (i, col))
    return _Epilogue(
        [a, cb, zr, zr], [_row_tile(rows, D_MODEL), _row_tile(rows, D_MODEL), gate(1), gate(2)],
        [bf, bf, bf, bf, jax.ShapeDtypeStruct((8, D_MODEL), F32)],
        [_row_tile(rows, D_MODEL)] * 4 + [_whole((8, D_MODEL))], fn, keep_product=False)


FFN_ROWS = 2048
FFN_BLOCKS = D_FF // FFN_COLS
GELU_C = math.sqrt(2.0 / math.pi)


def _gelu(v):
    t = jnp.tanh(GELU_C * (v + 0.044715 * (v * v * v)))
    return 0.5 * v * (1.0 + t), t


def _gelu_grad(v, t):
    return 0.5 * (1.0 + t) + 0.5 * v * (1.0 - t * t) * (GELU_C * (1.0 + 3.0 * 0.044715 * (v * v)))


def _sublane_rows(ref, n):
    return [jnp.broadcast_to(ref[r:r + 1, :], (8, FFN_COLS)) for r in range(n)]


def _rolls(tile, shifts):
    return tuple(pltpu.roll(tile, s, 0) for s in shifts)


def _behind(prev_rolls, cur, row_id):
    rolls = _rolls(cur, (1, 2))
    x1 = jnp.where(row_id < 1, prev_rolls[0], rolls[0])
    x2 = jnp.where(row_id < 2, prev_rolls[1], rolls[1])
    return (x2, x1, cur), rolls


def _ahead(cur_rolls, next_rolls, row_id):
    return (jnp.where(row_id < 7, cur_rolls[0], next_rolls[0]), jnp.where(row_id < 6, cur_rolls[1], next_rolls[1]))


def _conv3(taps, w, bias):
    return w[0] * taps[0] + w[1] * taps[1] + w[2] * taps[2] + bias


def _ffn_specs(rows):
    tile = lambda off: pl.BlockSpec((rows, FFN_COLS), lambda j, i: (i, j + off))
    prev = lambda off: pl.BlockSpec((FFN_HALO, FFN_COLS),
                                    lambda j, i: (jnp.maximum(i * (rows // FFN_HALO) - 1, 0), j + off))
    wgt = lambda off: pl.BlockSpec((3, FFN_COLS), lambda j, i: (0, j + off))
    vec = lambda off: pl.BlockSpec((1, FFN_COLS), lambda j, i: (0, j + off))
    return tile, prev, wgt, vec


def _ffn_act(up, w_dw, b_dw):
    seq = up.shape[0]
    tile, prev, wgt, vec = _ffn_specs(FFN_ROWS)

    def body(v_ref, g_ref, vp_ref, gp_ref, wv_ref, wg_ref, bv_ref, bg_ref, act_ref):
        first = pl.program_id(1) == 0
        row_id = lax.broadcasted_iota(jnp.int32, (8, FFN_COLS), 0)
        wv, wg = _sublane_rows(wv_ref, 3), _sublane_rows(wg_ref, 3)
        (bv,), (bg,) = _sublane_rows(bv_ref, 1), _sublane_rows(bg_ref, 1)
        rolls_v = _rolls(jnp.where(first, 0.0, vp_ref[...]), (1, 2))
        rolls_g = _rolls(jnp.where(first, 0.0, gp_ref[...]), (1, 2))
        for row in range(0, FFN_ROWS, 16):
            halves = []
            for r in (row, row + 8):
                taps_v, rolls_v = _behind(rolls_v, v_ref[r:r + 8, :], row_id)
                taps_g, rolls_g = _behind(rolls_g, g_ref[r:r + 8, :], row_id)
                halves.append(_gelu(_conv3(taps_g, wg, bg))[0] * _conv3(taps_v, wv, bv))
            act_ref[row:row + 16, :] = jnp.concatenate(halves, axis=0).astype(BF16)

    return pl.pallas_call(
        body, name="ffn_act", out_shape=jax.ShapeDtypeStruct((seq, D_FF), BF16),
        grid=(FFN_BLOCKS, seq // FFN_ROWS),
        in_specs=[tile(0), tile(FFN_BLOCKS), prev(0), prev(FFN_BLOCKS), wgt(0), wgt(FFN_BLOCKS),
                  vec(0), vec(FFN_BLOCKS)],
        out_specs=tile(0), compiler_params=_cparams(2),
    )(up, up, up, up, w_dw, w_dw, b_dw, b_dw)


def _ffn_act_bwd(up, dact, w_dw, b_dw, comm=None):
    seq = up.shape[0]
    n_tiles = seq // FFN_ROWS
    n_halo = seq // FFN_HALO
    tile, prev, wgt, vec = _ffn_specs(FFN_ROWS)
    nxt = lambda off: pl.BlockSpec(
        (FFN_HALO, FFN_COLS), lambda j, i: (jnp.minimum((i + 1) * (FFN_ROWS // FFN_HALO), n_halo - 1), j + off))
    acc = lambda off: pl.BlockSpec((8, FFN_COLS), lambda j, i: (0, j + off))

    def body(v_ref, g_ref, vp_ref, gp_ref, vn_ref, gn_ref, da_ref, dan_ref, wv_ref, wg_ref, bv_ref, bg_ref,
             dv_out, dg_out, dwv_ref, dwg_ref, dbv_ref, dbg_ref):
        i = pl.program_id(1)
        first, last = i == 0, i == n_tiles - 1

        @pl.when(first)
        def _():
            for r in (dwv_ref, dwg_ref, dbv_ref, dbg_ref):
                r[...] = jnp.zeros_like(r)

        row_id = lax.broadcasted_iota(jnp.int32, (8, FFN_COLS), 0)
        wv, wg = _sublane_rows(wv_ref, 3), _sublane_rows(wg_ref, 3)
        (bv,), (bg,) = _sublane_rows(bv_ref, 1), _sublane_rows(bg_ref, 1)
        zero = jnp.zeros((8, FFN_COLS), F32)
        sums_v, sums_g = [zero] * 4, [zero] * 4
        rolls_v = _rolls(jnp.where(first, 0.0, vp_ref[...]), (1, 2))
        rolls_g = _rolls(jnp.where(first, 0.0, gp_ref[...]), (1, 2))
        behind = None
        done_v, done_g = [], []

        def grads(v_tile, g_tile, dact, rolls_v, rolls_g):
            taps_v, rolls_v = _behind(rolls_v, v_tile, row_id)
            taps_g, rolls_g = _behind(rolls_g, g_tile, row_id)
            val, gate = _conv3(taps_v, wv, bv), _conv3(taps_g, wg, bg)
            gel, t = _gelu(gate)
            return dact * gel, dact * val * _gelu_grad(gate, t), taps_v, taps_g, rolls_v, rolls_g

        def finish(tile, nxt, row):
            for (d, d_rolls), (_, n_rolls), w, done, o_ref in ((tile[0], nxt[0], wv, done_v, dv_out),
                                                               (tile[1], nxt[1], wg, done_g, dg_out)):
                d1, d2 = _ahead(d_rolls, n_rolls, row_id)
                done.append(w[2] * d + w[1] * d1 + w[0] * d2)
                if len(done) == 2:
                    o_ref[row - 16:row, :] = jnp.concatenate(done, axis=0).astype(BF16)
                    done.clear()

        for row in range(0, FFN_ROWS, 16):
            dact16 = da_ref[row:row + 16, :].astype(F32)
            for r, dact in ((row, dact16[0:8, :]), (row + 8, dact16[8:16, :])):
                dval, dgate, taps_v, taps_g, rolls_v, rolls_g = grads(v_ref[r:r + 8, :], g_ref[r:r + 8, :], dact,
                                                                      rolls_v, rolls_g)
                sums_v = [s + dval * x for s, x in zip(sums_v, taps_v)] + [sums_v[3] + dval]
                sums_g = [s + dgate * x for s, x in zip(sums_g, taps_g)] + [sums_g[3] + dgate]
                tile = ((dval, _rolls(dval, (7, 6))), (dgate, _rolls(dgate, (7, 6))))
                if behind is not None:
                    finish(behind, tile, r)
                behind = tile
        dact_next = jnp.where(last, 0.0, dan_ref[...].astype(F32)[0:FFN_HALO, :])
        dval, dgate, *_ = grads(vn_ref[...], gn_ref[...], dact_next, rolls_v, rolls_g)
        finish(behind, ((dval, _rolls(dval, (7, 6))), (dgate, _rolls(dgate, (7, 6)))), FFN_ROWS)
        for sums, dw_ref, db_ref in ((sums_v, dwv_ref, dbv_ref), (sums_g, dwg_ref, dbg_ref)):
            for tap in range(3):
                dw_ref[tap:tap + 1, :] += _colsum(sums[tap])
            db_ref[0:1, :] += _colsum(sums[3])

    half = jax.ShapeDtypeStruct((seq, D_FF), BF16)
    acc_shape = jax.ShapeDtypeStruct((8, D_FF), F32)
    return _host_call(
        lambda ins, outs, scratch: body(*ins, *outs, *scratch), "ffn_act_bwd", grid=(FFN_BLOCKS, n_tiles),
        in_specs=[tile(0), tile(FFN_BLOCKS), prev(0), prev(FFN_BLOCKS), nxt(0), nxt(FFN_BLOCKS),
                  tile(0), pl.BlockSpec((16, FFN_COLS), lambda j, i: (
                      jnp.minimum((i + 1) * (FFN_ROWS // 16), seq // 16 - 1), j)),
                  wgt(0), wgt(FFN_BLOCKS), vec(0), vec(FFN_BLOCKS)],
        out_specs=[tile(0), tile(0), acc(0), acc(0), acc(0), acc(0)],
        out_shape=[half, half, acc_shape, acc_shape, acc_shape, acc_shape],
        scratch_shapes=[], args=[up, up, up, up, up, up, dact, dact, w_dw, w_dw, b_dw, b_dw], comm=comm)


def _cols_to_blocks(full_cols):
    k, n8 = full_cols.shape
    return jnp.transpose(full_cols.reshape(k, N_DEV, n8 // N_DEV), (1, 0, 2))


def _rows_to_blocks(full_rows):
    r8, n = full_rows.shape
    return full_rows.reshape(N_DEV, r8 // N_DEV, n)


def _blocks_to_cols(gathered):
    _, k, n = gathered.shape
    return jnp.transpose(gathered, (1, 0, 2)).reshape(k, N_DEV * n)


def kernel(x, c, w_ada, b_ada, g_pre_mix, g_post_mix, w_in, b_in, rel_bias, w_attn_o, w_dw_conv, b_dw_conv, g_conv_ln, b_conv_ln, w_conv_o, b_conv_o, w_mix_o, g_pre_ffn, g_post_ffn, w_up, w_dw_ffn, b_dw_ffn, w_down, loss_target, m_w_ada, m_b_ada, m_g_pre_mix, m_g_post_mix, m_w_in, m_b_in, m_rel_bias, m_w_attn_o, m_w_dw_conv, m_b_dw_conv, m_g_conv_ln, m_b_conv_ln, m_w_conv_o, m_b_conv_o, m_w_mix_o, m_g_pre_ffn, m_g_post_ffn, m_w_up, m_w_dw_ffn, m_b_dw_ffn, m_w_down, v_w_ada, v_b_ada, v_g_pre_mix, v_g_post_mix, v_w_in, v_b_in, v_rel_bias, v_w_attn_o, v_w_dw_conv, v_b_dw_conv, v_g_conv_ln, v_b_conv_ln, v_w_conv_o, v_b_conv_o, v_w_mix_o, v_g_pre_ffn, v_g_post_ffn, v_w_up, v_w_dw_ffn, v_b_dw_ffn, v_w_down):
    names = ["w_ada", "b_ada", "g_pre_mix", "g_post_mix", "w_in", "b_in", "rel_bias", "w_attn_o", "w_dw_conv",
             "b_dw_conv", "g_conv_ln", "b_conv_ln", "w_conv_o", "b_conv_o", "w_mix_o", "g_pre_ffn", "g_post_ffn",
             "w_up", "w_dw_ffn", "b_dw_ffn", "w_down"]
    weights = dict(zip(names, [w_ada, b_ada, g_pre_mix, g_post_mix, w_in, b_in, rel_bias, w_attn_o, w_dw_conv,
                               b_dw_conv, g_conv_ln, b_conv_ln, w_conv_o, b_conv_o, w_mix_o, g_pre_ffn,
                               g_post_ffn, w_up, w_dw_ffn, b_dw_ffn, w_down]))
    mom_m = dict(zip(names, [m_w_ada, m_b_ada, m_g_pre_mix, m_g_post_mix, m_w_in, m_b_in, m_rel_bias, m_w_attn_o,
                             m_w_dw_conv, m_b_dw_conv, m_g_conv_ln, m_b_conv_ln, m_w_conv_o, m_b_conv_o,
                             m_w_mix_o, m_g_pre_ffn, m_g_post_ffn, m_w_up, m_w_dw_ffn, m_b_dw_ffn, m_w_down]))
    mom_v = dict(zip(names, [v_w_ada, v_b_ada, v_g_pre_mix, v_g_post_mix, v_w_in, v_b_in, v_rel_bias, v_w_attn_o,
                             v_w_dw_conv, v_b_dw_conv, v_g_conv_ln, v_b_conv_ln, v_w_conv_o, v_b_conv_o,
                             v_w_mix_o, v_g_pre_ffn, v_g_post_ffn, v_w_up, v_w_dw_ffn, v_b_dw_ffn, v_w_down]))
    shapes = {n: w.shape for n, w in weights.items()}

    seq = x.shape[1]
    me = 4 * lax.axis_index("x") + 2 * lax.axis_index("y") + lax.axis_index("c")
    x2 = x.reshape(seq, D_MODEL)
    target = loss_target.reshape(seq, D_MODEL)
    sq = lambda a: a.reshape(a.shape[1:])
    bf = lambda a: sq(a).astype(BF16)

    transposed = lambda a: jnp.swapaxes(sq(a), 0, 1)

    c_all, mod_all = _ada_mod(c, sq(w_ada))
    c_all = c_all.reshape(N_DEV, D_MODEL)
    mod = lax.dynamic_index_in_dim(mod_all, me, axis=1, keepdims=False)
    mod6 = (mod.reshape(1, 6 * D_MODEL) + b_ada).reshape(6, D_MODEL)

    h1, (g_in, g_dwc, g_dwf) = _pre_mix(
        x2, mod6, g_pre_mix, comm=_gather_comm([transposed(w_in).astype(BF16), sq(w_dw_conv), sq(w_dw_ffn)]))
    wt_in = g_in.reshape(g_in.shape[0] * g_in.shape[1], D_MODEL)
    wf_dwc = _blocks_to_cols(g_dwc)
    wf_dwf = _blocks_to_cols(g_dwf)
    qkv = _mm(h1, wt_in, "nt", BF16, "in_proj_qkv", bias=b_in, tm=1024, tn=768, cols=(0, 3 * D_ATTN))
    zr, _, (g_ao, g_co, g_mo) = _mm(h1, wt_in, "nt", F32, "in_proj_rest", bias=b_in, tm=1024, tn=3 * D_ATTN,
                                 cols=(3 * D_ATTN, 2 * D_CONV + 2 * D_MODEL),
                                 comm=_gather_comm([bf(w_attn_o), bf(w_conv_o), bf(w_mix_o)]))
    kpad = jnp.pad(qkv[:, D_ATTN:2 * D_ATTN], ((PAD_ROWS, 0), (0, 0)))
    vpad = jnp.pad(qkv[:, 2 * D_ATTN:], ((PAD_ROWS, 0), (0, 0)))
    table = jnp.transpose(_bias_table(sq(rel_bias)), (1, 0, 2))
    ao, (g_up,) = _attn_fwd(qkv, kpad, vpad, table, comm=_gather_comm([transposed(w_up).astype(BF16)]))
    (u1, u3), (g_dn,) = _conv_fwd(zr, wf_dwc, b_dw_conv, g_conv_ln, b_conv_ln, comm=_gather_comm([bf(w_down)]))
    wf_ao = _blocks_to_cols(g_ao)
    wf_co = _blocks_to_cols(g_co)
    wf_mo = g_mo.reshape(D_MODEL, D_MODEL)
    wt_up = g_up.reshape(g_up.shape[0] * g_up.shape[1], D_MODEL)
    wf_dn = g_dn.reshape(D_FF, D_MODEL)
    y, a_br, cb_br = _merge_fwd(ao, u3, zr, wf_ao, wf_co, b_conv_o)
    ymix, (x1, h2), _ = _mm(y, wf_mo, "nn", F32, "mix_o", tm=512, tn=D_MODEL,
                            epilogue=_post_mix_pre_ffn(x2, mod6, g_post_mix, g_pre_ffn, 512))
    up = _mm(h2, wt_up, "nt", F32, "ffn_up", tm=1024, tn=1408)
    act = _ffn_act(up, wf_dwf, b_dw_ffn)
    _, (loss_lanes, dout, dyf, small_f), _ = _mm(act, wf_dn, "nn", F32, "ffn_down", tm=512, tn=D_MODEL,
                                                 epilogue=_final(x1, target, mod6, g_post_ffn, 512))

    dact = _mm(dyf, wf_dn, "nt", BF16, "ffn_down_dx", tm=1024, tn=1408)
    gw_down = _mm(act, dyf, "tn", BF16, "ffn_down_dw", tm=256, tn=1024)
    (dup_v, dup_g, dwv, dwg, dbv, dbg), (parts_down,) = _ffn_act_bwd(
        up, dact, wf_dwf, b_dw_ffn, comm=_scatter_comm([_rows_to_blocks(gw_down)]))
    _, (dx1, dymix, small_m), _ = _mm([dup_v, dup_g], wt_up, "nn", F32, "ffn_up_dx", tm=256, tn=D_MODEL,
                                      epilogue=_mid_bwd(x1, dout, ymix, mod6, g_pre_ffn, g_post_mix, 256))
    blocks_up = _rows_to_blocks(_mm_tn_rows([dup_v, dup_g], h2, "ffn_up_dw"))
    _, (da, dcb, dga, dgb, small_g), _ = _mm(dymix, wf_mo, "nt", F32, "mix_o_dx", tm=512, tn=D_MODEL,
                                             epilogue=_merge_bwd(a_br, cb_br, zr, 512))
    gw_mo = _mm(y, dymix, "tn", BF16, "mix_o_dw")
    dao = _mm(da, wf_ao, "nt", BF16, "attn_o_dx", tm=1024)
    gw_ao = _mm(ao, da, "tn", BF16, "attn_o_dw")
    du3 = _mm(dcb, wf_co, "nt", F32, "conv_o_dx", tm=1024)
    gw_co = _mm(u3, dcb, "tn", BF16, "conv_o_dw")
    (dq, dkt, dvt, dbias, small_a), (parts_up,) = _attn_bwd(
        qkv, kpad, vpad, table, dao, comm=_scatter_comm([blocks_up]))
    g_rel = _bias_grad(jnp.transpose(dbias, (1, 0, 2)))
    (dglu_a, dglu_b, dw_conv, small_c), (parts_mo, parts_ao, parts_co) = _conv_bwd(
        zr, u1, du3, wf_dwc, g_conv_ln, b_conv_ln,
        comm=_scatter_comm([_rows_to_blocks(gw_mo), _cols_to_blocks(gw_ao), _cols_to_blocks(gw_co)]))
    dz = _assemble_dz(dq, dkt, dvt, dglu_a, dglu_b, dga, dgb)
    blocks_in = _rows_to_blocks(_mm(dz, h1, "tn", BF16, "in_proj_dw", tm=512, tn=D_MODEL))
    _, (grad_x, small_x), (parts_in,) = _mm(dz, wt_in, "nn", F32, "in_proj_dx", tm=512, tn=D_MODEL,
                                            comm=_scatter_comm([blocks_in]),
                                            epilogue=_pre_mix_bwd(x2, dx1, mod6, g_pre_mix, 512))

    packed = _pack_grads(small_x, small_m, small_f, small_g, small_a, small_c, dbv, dbg, dwv, dwg, dw_conv)
    gathered, gathered_rel, gathered_loss = _run_comm(_gather_comm([packed, g_rel, loss_lanes]), "gather_small")
    gathered = gathered.reshape(N_DEV, PACKED_TOTAL)
    updates, g_dwc_full, g_dwf_full, loss_all = _small_adamw(gathered, gathered_rel, gathered_loss, weights, mom_m,
                                                             mom_v)
    loss = loss_all[0, 0]

    grads, deltas, new_m, new_v = {}, {}, {}, {}

    def record(name, update, is_transposed=False):
        for dst, val in zip((grads, deltas, new_m, new_v), update):
            dst[name] = (jnp.swapaxes(val, 0, 1) if is_transposed else val).reshape(shapes[name])

    for name, update in updates.items():
        record(name, update)

    def local_update(name, grad):
        record(name, _adamw(sq(weights[name]), sq(mom_m[name]), sq(mom_v[name]), "adamw_" + name, g=grad))

    conv_cols, ffn_cols, ada_cols = D_CONV // N_DEV, 2 * D_FF // N_DEV, 6 * D_MODEL // N_DEV
    local_update("w_dw_conv", lax.dynamic_slice(g_dwc_full, (0, me * conv_cols), (CONV_K, conv_cols)))
    local_update("w_dw_ffn", lax.dynamic_slice(g_dwf_full, (0, me * ffn_cols), (3, ffn_cols)))
    local_update("w_ada", _ada_grad(c_all, lax.dynamic_slice(gathered, (0, me * ada_cols), (N_DEV, ada_cols))))

    for name, part in (("w_attn_o", parts_ao), ("w_conv_o", parts_co), ("w_mix_o", parts_mo), ("w_down", parts_down)):
        record(name, _adamw(sq(weights[name]), sq(mom_m[name]), sq(mom_v[name]), "adamw_" + name, parts=part))
    for name, part in (("w_in", parts_in), ("w_up", parts_up)):
        record(name, _adamw(transposed(weights[name]), transposed(mom_m[name]), transposed(mom_v[name]),
                            "adamw_" + name, parts=part), is_transposed=True)

    return (loss, grad_x.reshape(x.shape), *[grads[n] for n in names], *[deltas[n] for n in names],
            *[new_m[n] for n in names], *[new_v[n] for n in names])
```

```python
import functools
import math

import jax
import jax.numpy as jnp
from jax import lax
from jax.experimental import pallas as pl
from jax.experimental.pallas import tpu as pltpu

F32 = jnp.float32
BF16 = jnp.bfloat16
HIGHEST = lax.Precision.HIGHEST

D_MODEL = 1024
CHUNK = 64
LEFT_CHUNKS = 8
BAND = (LEFT_CHUNKS + 1) * CHUNK
PAD_ROWS = LEFT_CHUNKS * CHUNK
GROUP = 4
GROUP_Q = GROUP * CHUNK
GROUP_K = GROUP_Q + PAD_ROWS
SOFTMAX_ROWS = 16
TOEPLITZ = 640
N_HEADS = 8
HEAD_DIM = 64
D_ATTN = 512
D_CONV = 512
CONV_K = 31
CONV_HALO = 32
MAX_REL = 128
N_REL = 2 * MAX_REL + 1
D_FF = 2816
FFN_HALO = 8
FFN_COLS = 256
EPS = 1e-6
NEG_INF = -1e30
N_DEV = 8

ADAM_LR = 0.001
ADAM_B1 = 0.9
ADAM_B2 = 0.999
ADAM_EPS = 1e-08
ADAM_WD = 0.01
ADAM_STEP = 10

VMEM_LIMIT_BYTES = 56 * 1024 * 1024
ADAMW_BLOCK_BYTES = 768 * 1024

MESH = pl.DeviceIdType.MESH
ANY = pl.BlockSpec(memory_space=pl.ANY)

SH_M, SC_M, GT_M, SH_F, SC_F, GT_F = range(6)

SMALL = (("b_ada", 6144), ("g_pre_mix", 1024), ("g_post_mix", 1024), ("b_in", 4608), ("b_dw_conv", 512),
         ("g_conv_ln", 512), ("b_conv_ln", 512), ("b_conv_o", 1024), ("g_pre_ffn", 1024), ("g_post_ffn", 1024),
         ("b_dw_ffn", 5632))
PACKED_TOTAL = sum(n for _, n in SMALL) + CONV_K * D_CONV + 3 * 2 * D_FF


def _cparams(n_axes):
    return pltpu.CompilerParams(vmem_limit_bytes=VMEM_LIMIT_BYTES,
                                dimension_semantics=("arbitrary",) * n_axes)


def _sig(v):
    return 1.0 / (1.0 + jnp.exp(-v))


def _pick(n, target):
    if n <= target:
        return n
    t = target - target % 128
    while n % t:
        t -= 128
    return t


def _tile(rows, cols, col=0):
    return pl.BlockSpec((rows, cols), lambda i: (i, col))


def _full(shape):
    zeros = (0,) * len(shape)
    return pl.BlockSpec(shape, lambda i: zeros)


def _prev(halo, cols, rows, col=0):
    return pl.BlockSpec((halo, cols), lambda i: (jnp.maximum(i * (rows // halo) - 1, 0), col))


def _next(halo, cols, rows, n_blocks, col=0):
    return pl.BlockSpec((halo, cols), lambda i: (jnp.minimum((i + 1) * (rows // halo), n_blocks - 1), col))


class _Comm:
    def __init__(self, inputs, out_shapes, sems, start, finish, relay=None, early=None):
        self.inputs, self.out_shapes, self.sems, self.start, self.finish = inputs, out_shapes, sems, start, finish
        self.relay, self.early = relay, early


def _host_call(body, name, grid, in_specs, out_specs, out_shape, scratch_shapes, args, comm=None):
    n_in, n_out, n_scr = len(args), len(out_shape), len(scratch_shapes)
    c_in = list(comm.inputs) if comm else []
    c_out = list(comm.out_shapes) if comm else []
    c_sem = list(comm.sems) if comm else []

    def full(*refs):
        bounds = [0, n_in, len(c_in), n_out, len(c_out), n_scr, len(c_sem)]
        cuts = [sum(bounds[:i + 1]) for i in range(len(bounds))]
        ins, cins, outs, couts, scr, csems = (refs[lo:hi] for lo, hi in zip(cuts[:-1], cuts[1:]))
        if comm:
            first = functools.reduce(jnp.logical_and, [pl.program_id(ax) == 0 for ax in range(len(grid))])
            pl.when(first)(lambda: comm.start(cins, couts, csems))
            if comm.early is not None:
                strides = [math.prod(grid[ax + 1:]) for ax in range(len(grid))]
                step = sum(pl.program_id(ax) * strides[ax] for ax in range(len(grid)))
                pl.when(step == 1)(lambda: comm.early(cins, couts, csems))
            last = functools.reduce(jnp.logical_and, [pl.program_id(ax) == grid[ax] - 1 for ax in range(len(grid))])
            if comm.relay is not None:
                pl.when(last)(lambda: comm.relay(cins, couts, csems))
        body(ins, outs, scr)
        if comm:
            pl.when(last)(lambda: comm.finish(cins, couts, csems))

    res = pl.pallas_call(
        full, name=name, grid=grid, in_specs=list(in_specs) + [ANY] * len(c_in),
        out_specs=list(out_specs) + [ANY] * len(c_out), out_shape=list(out_shape) + c_out,
        scratch_shapes=list(scratch_shapes) + c_sem, compiler_params=_cparams(len(grid)),
    )(*args, *c_in)
    return list(res[:n_out]), list(res[n_out:])


def _run_comm(comm, name):
    n_in, n_out = len(comm.inputs), len(comm.out_shapes)

    def body(*refs):
        ins, outs, sems = refs[:n_in], refs[n_in:n_in + n_out], refs[n_in + n_out:]
        comm.start(ins, outs, sems)
        if comm.relay is not None:
            comm.relay(ins, outs, sems)
        comm.finish(ins, outs, sems)

    return pl.pallas_call(
        body, name=name, out_shape=list(comm.out_shapes), in_specs=[ANY] * n_in, out_specs=[ANY] * n_out,
        scratch_shapes=list(comm.sems),
    )(*comm.inputs)


def _place():
    return lax.axis_index("x"), lax.axis_index("y"), lax.axis_index("c")


def _gather_comm(arrs):
    n = len(arrs)

    def plan(ins, outs, sems):
        send_sems, recv_sems, local_sems = sems
        x, y, c = _place()
        me, sibling = (x, y, c), (x, y, 1 - c)
        chips = [(1 - x, y), (x, 1 - y), (1 - x, 1 - y)]

        def block(k, p):
            return outs[k].at[4 * p[0] + 2 * p[1] + p[2]]

        def copy(k, s, blk, to, src=None):
            return pltpu.make_async_remote_copy(
                src_ref=block(k, blk) if src is None else src, dst_ref=block(k, blk),
                send_sem=send_sems.at[7 * k + s], recv_sem=recv_sems.at[7 * k + s],
                device_id=to, device_id_type=MESH)

        mine = [pltpu.make_async_copy(ins[k], block(k, me), local_sems.at[k]) for k in range(n)]
        first = []
        for k in range(n):
            first.append(copy(k, 0, me, sibling, src=ins[k]))
            for j, chip in enumerate(chips):
                first.append(copy(k, 1 + j, me, (*chip, c), src=ins[k]))
        return me, sibling, chips, c, copy, mine, first

    def start(ins, outs, sems):
        *_, mine, first = plan(ins, outs, sems)
        for cp in mine + first:
            cp.start()

    def relay(ins, outs, sems):
        me, sibling, chips, c, copy, _, _ = plan(ins, outs, sems)
        for j, chip in enumerate(chips):
            for k in range(n):
                copy(k, 1 + j, (*chip, c), me).wait_recv()
                copy(k, 4 + j, (*chip, c), sibling).start()

    def finish(ins, outs, sems):
        me, sibling, chips, c, copy, mine, first = plan(ins, outs, sems)
        passed = [copy(k, 4 + j, (*chip, c), sibling) for j, chip in enumerate(chips) for k in range(n)]
        for k in range(n):
            copy(k, 0, sibling, me).wait_recv()
        for j, chip in enumerate(chips):
            for k in range(n):
                copy(k, 4 + j, (*chip, 1 - c), me).wait_recv()
        for cp in first + passed:
            cp.wait_send()
        for cp in mine:
            cp.wait()

    return _Comm(list(arrs), [jax.ShapeDtypeStruct((N_DEV,) + a.shape, a.dtype) for a in arrs],
                 [pltpu.SemaphoreType.DMA((7 * n,)), pltpu.SemaphoreType.DMA((7 * n,)),
                  pltpu.SemaphoreType.DMA((n,))], start, finish, relay)


def _scatter_comm(blocks):
    n = len(blocks)

    def plan(ins, outs, sems, arrivals):
        send_sems, recv_sems, local_sems = sems
        x, y, c = _place()
        me = 4 * x + 2 * y + c
        local = [pltpu.make_async_copy(ins[k].at[me], outs[k].at[me], local_sems.at[k]) for k in range(n)]
        sends, recvs = [], []
        for k in range(n):
            for mask in range(1, N_DEV):
                px = 1 - x if mask & 4 else x
                py = 1 - y if mask & 2 else y
                pc = 1 - c if mask & 1 else c
                peer = 4 * px + 2 * py + pc
                sem = 7 * k + mask - 1
                both = dict(send_sem=send_sems.at[sem], recv_sem=recv_sems.at[sem], device_id=(px, py, pc),
                            device_id_type=MESH)
                sends.append(pltpu.make_async_remote_copy(src_ref=ins[k].at[peer], dst_ref=outs[k].at[me], **both))
                if arrivals:
                    recvs.append(pltpu.make_async_remote_copy(src_ref=ins[k].at[me], dst_ref=outs[k].at[peer],
                                                              **both))
        return local, sends, recvs

    def start(ins, outs, sems):
        local, sends, _ = plan(ins, outs, sems, arrivals=False)
        for cp in local + sends:
            cp.start()

    def finish(ins, outs, sems):
        local, sends, recvs = plan(ins, outs, sems, arrivals=True)
        for cp in recvs:
            cp.wait_recv()
        for cp in sends:
            cp.wait_send()
        for cp in local:
            cp.wait()

    return _Comm(list(blocks), [jax.ShapeDtypeStruct(b.shape, b.dtype) for b in blocks],
                 [pltpu.SemaphoreType.DMA((7 * n,)), pltpu.SemaphoreType.DMA((7 * n,)),
                  pltpu.SemaphoreType.DMA((n,))], start, finish)


def _pair_scatter_comm(block):
    _, r, c = block.shape
    quarter = jax.ShapeDtypeStruct((4, r, c), block.dtype)

    def plan(ins, outs, sems):
        parts, got, pair = outs
        d2d_send, d2d_recv, ici_send, ici_recv, local, buf_a, buf_b = sems
        x, y, cc = _place()
        mine = 2 * x + y
        chips = [(1 - x, y), (x, 1 - y), (1 - x, 1 - y)]
        to_sibling = [pltpu.make_async_remote_copy(
            src_ref=ins[0].at[2 * q + 1 - cc], dst_ref=got.at[q], send_sem=d2d_send.at[q], recv_sem=d2d_recv.at[q],
            device_id=(x, y, 1 - cc), device_id_type=MESH) for q in range(4)]
        to_chips = [pltpu.make_async_remote_copy(
            src_ref=pair.at[2 * px + py], dst_ref=parts.at[mine], send_sem=ici_send.at[j], recv_sem=ici_recv.at[j],
            device_id=(px, py, cc), device_id_type=MESH) for j, (px, py) in enumerate(chips)]
        from_chips = [pltpu.make_async_remote_copy(
            src_ref=pair.at[mine], dst_ref=parts.at[2 * px + py], send_sem=ici_send.at[j], recv_sem=ici_recv.at[j],
            device_id=(px, py, cc), device_id_type=MESH) for j, (px, py) in enumerate(chips)]
        own = pltpu.make_async_copy(pair.at[mine], parts.at[mine], local.at[2])
        return cc, got, pair, local, buf_a, buf_b, to_sibling, to_chips, from_chips, own

    def start(ins, outs, sems):
        for cp in plan(ins, outs, sems)[6]:
            cp.start()

    def early(ins, outs, sems):
        cc, got, pair, local, buf_a, buf_b, to_sibling, to_chips, _, own = plan(ins, outs, sems)
        for q in range(4):
            to_sibling[q].wait_recv()
            loads = [pltpu.make_async_copy(ins[0].at[2 * q + cc], buf_a, local.at[0]),
                     pltpu.make_async_copy(got.at[q], buf_b, local.at[1])]
            for cp in loads:
                cp.start()
            for cp in loads:
                cp.wait()
            buf_a[...] = (buf_a[...].astype(F32) + buf_b[...].astype(F32)).astype(block.dtype)
            store = pltpu.make_async_copy(buf_a, pair.at[q], local.at[0])
            store.start()
            store.wait()
        for cp in to_chips + [own]:
            cp.start()

    def finish(ins, outs, sems):
        *_, to_sibling, to_chips, from_chips, own = plan(ins, outs, sems)
        for cp in from_chips:
            cp.wait_recv()
        for cp in to_chips + to_sibling:
            cp.wait_send()
        own.wait()

    return _Comm([block], [quarter, quarter, quarter],
                 [pltpu.SemaphoreType.DMA((4,)), pltpu.SemaphoreType.DMA((4,)), pltpu.SemaphoreType.DMA((3,)),
                  pltpu.SemaphoreType.DMA((3,)), pltpu.SemaphoreType.DMA((3,)), pltpu.VMEM((r, c), block.dtype),
                  pltpu.VMEM((r, c), block.dtype)], start, finish, early=early)


_DIMS = {"nn": (((1,), (0,)), ((), ())), "nt": (((1,), (1,)), ((), ())), "tn": (((0,), (0,)), ((), ()))}


class _Epilogue:
    def __init__(self, args, in_specs, out_shapes, out_specs, fn, keep_product):
        self.args, self.in_specs, self.out_shapes, self.out_specs = args, in_specs, out_shapes, out_specs
        self.fn, self.keep_product = fn, keep_product


def _row_tile(rows, cols):
    return pl.BlockSpec((rows, cols), lambda i, j: (i, 0))


def _whole(shape):
    zeros = (0,) * len(shape)
    return pl.BlockSpec(shape, lambda i, j: zeros)


def _mm(a, b, mode, out_dtype, name, bias=None, tm=512, tn=512, comm=None, cols=None, epilogue=None):
    pieces = a if isinstance(a, (list, tuple)) else [a]
    assert all(p.dtype == BF16 for p in pieces) and b.dtype == BF16
    a = pieces[0]
    if mode == "tn":
        k_dim, m_dim = a.shape
    else:
        m_dim, k_dim = a.shape
    n_dim = b.shape[0] if mode == "nt" else b.shape[1]
    col0 = 0
    if cols is not None:
        assert mode != "tn" and cols[0] % tn == 0 and cols[1] % tn == 0
        col0, n_dim = cols[0] // tn, cols[1]
    tm, tn = _pick(m_dim, tm), _pick(n_dim, tn)
    assert mode != "tn" or len(pieces) == 1
    a_specs = [pl.BlockSpec((k_dim, tm), lambda i, j: (0, i)) if mode == "tn"
               else pl.BlockSpec((tm, k_dim), lambda i, j: (i, 0))] * len(pieces)
    once = dict(pipeline_mode=pl.Buffered(1)) if tn == n_dim else {}
    if mode == "nt":
        b_specs = [pl.BlockSpec((tn, k_dim), lambda i, j, p=p: (j + col0, p), **once) for p in range(len(pieces))]
    else:
        b_specs = [pl.BlockSpec((k_dim, tn), lambda i, j, p=p: (p, j + col0), **once) for p in range(len(pieces))]
    in_specs = a_specs + b_specs
    args = list(pieces) + [b] * len(pieces)
    if bias is not None:
        in_specs.append(pl.BlockSpec((1, tn), lambda i, j: (0, j + col0)))
        args.append(bias)
    dims = _DIMS[mode]
    n_pieces = len(pieces)
    n_own = len(args)
    keep = epilogue is None or epilogue.keep_product
    out_specs = [pl.BlockSpec((tm, tn), lambda i, j: (i, j))] if keep else []
    out_shape = [jax.ShapeDtypeStruct((m_dim, n_dim), out_dtype)] if keep else []
    if epilogue is not None:
        assert tn == n_dim
        in_specs, args = in_specs + list(epilogue.in_specs), args + list(epilogue.args)
        out_specs, out_shape = out_specs + list(epilogue.out_specs), out_shape + list(epilogue.out_shapes)

    def body(ins, outs, scratch):
        total = lax.dot_general(ins[0][...], ins[n_pieces][...], dims, preferred_element_type=F32)
        for p in range(1, n_pieces):
            total = total + lax.dot_general(ins[p][...], ins[n_pieces + p][...], dims, preferred_element_type=F32)
        if bias is not None:
            total = total + ins[2 * n_pieces][...]
        if keep:
            outs[0][...] = total.astype(out_dtype)
        if epilogue is not None:
            epilogue.fn(total, pl.program_id(0) == 0, ins[n_own:], outs[1:] if keep else outs)

    outs, extra = _host_call(body, name, grid=(m_dim // tm, n_dim // tn), in_specs=in_specs, out_specs=out_specs,
                             out_shape=out_shape, scratch_shapes=[], args=args, comm=comm)
    product = outs[0] if keep else None
    if comm is None and epilogue is None:
        return product
    return product, outs[1:] if keep else outs, extra


def _mm_tn_rows(pieces, b, name, tm=256):
    k_dim, n_dim = b.shape
    counts = [p.shape[1] // tm for p in pieces]
    assert all(p.shape[1] % tm == 0 for p in pieces)
    firsts = [sum(counts[:q]) for q in range(len(pieces))]

    def a_spec(first, count):
        return pl.BlockSpec((k_dim, tm), lambda i: (0, jnp.clip(i - first, 0, count - 1)))

    def body(ins, outs, scratch):
        i = pl.program_id(0)
        for a_ref, first, count in zip(ins[:-1], firsts, counts):
            @pl.when(jnp.logical_and(i >= first, i < first + count))
            def _(a_ref=a_ref):
                outs[0][...] = lax.dot_general(a_ref[...], ins[-1][...], _DIMS["tn"],
                                               preferred_element_type=F32).astype(BF16)

    (out,), _ = _host_call(
        body, name, grid=(sum(counts),),
        in_specs=[a_spec(f, c) for f, c in zip(firsts, counts)] + [_full((k_dim, n_dim))],
        out_specs=[_tile(tm, n_dim)], out_shape=[jax.ShapeDtypeStruct((sum(counts) * tm, n_dim), BF16)],
        scratch_shapes=[], args=list(pieces) + [b])
    return out


def _adam_math(w, g, m, v):
    m = ADAM_B1 * m + (1.0 - ADAM_B1) * g
    v = ADAM_B2 * v + (1.0 - ADAM_B2) * (g * g)
    m_hat = m / (1.0 - ADAM_B1 ** ADAM_STEP)
    v_hat = v / (1.0 - ADAM_B2 ** ADAM_STEP)
    delta = -ADAM_LR * (m_hat / (jnp.sqrt(v_hat) + ADAM_EPS) + ADAM_WD * w)
    return delta, m, v


def _adamw(w, m, v, name, g=None, parts=None):
    rows, cols = w.shape
    tr = rows
    if rows * cols * 4 > ADAMW_BLOCK_BYTES:
        tr = max(t for t in range(16, rows, 16) if rows % t == 0 and t * cols * 4 <= ADAMW_BLOCK_BYTES)

    def body(w_ref, m_ref, v_ref, g_ref, go_ref, d_ref, mo_ref, vo_ref):
        if parts is None:
            grad = g_ref[...]
        else:
            grad = g_ref[0].astype(F32)
            for d in range(1, parts.shape[0]):
                grad = grad + g_ref[d].astype(F32)
        delta, m_new, v_new = _adam_math(w_ref[...], grad, m_ref[...], v_ref[...])
        go_ref[...] = grad
        d_ref[...] = delta
        mo_ref[...] = m_new
        vo_ref[...] = v_new

    spec = _tile(tr, cols)
    g_spec = spec if parts is None else pl.BlockSpec((parts.shape[0], tr, cols), lambda i: (0, i, 0))
    shape = jax.ShapeDtypeStruct((rows, cols), F32)
    return pl.pallas_call(
        body, name=name, out_shape=[shape] * 4, grid=(rows // tr,),
        in_specs=[spec, spec, spec, g_spec], out_specs=[spec] * 4, compiler_params=_cparams(1),
    )(w, m, v, g if parts is None else parts)


def _pack_grads(small_x, small_m, small_f, small_g, small_a, small_c, dbv, dbg, dwv, dwg, dw_conv):
    pieces = [
        (small_x, 2, D_MODEL), (small_x, 1, D_MODEL), (small_m, 4, D_MODEL), (small_m, 2, D_MODEL),
        (small_m, 1, D_MODEL), (small_f, 1, D_MODEL),
        (small_x, 0, D_MODEL), (small_m, 3, D_MODEL),
        (small_a, 0, D_ATTN), (small_a, 1, D_ATTN), (small_a, 2, D_ATTN), (small_c, 3, D_CONV),
        (small_c, 4, D_CONV), (small_g, 0, D_MODEL), (small_g, 1, D_MODEL),
        (small_c, 0, D_CONV), (small_c, 1, D_CONV), (small_c, 2, D_CONV),
        (small_g, 2, D_MODEL), (small_m, 0, D_MODEL), (small_f, 0, D_MODEL),
        (dbv, 0, D_FF), (dbg, 0, D_FF),
    ]
    pieces += [(dw_conv, j, D_CONV) for j in range(CONV_K)]
    pieces += [(src, tap, D_FF) for tap in range(3) for src in (dwv, dwg)]
    sources = [small_x, small_m, small_f, small_g, small_a, small_c, dbv, dbg, dwv, dwg, dw_conv]
    assert sum(width for _, _, width in pieces) == PACKED_TOTAL

    def body(*refs):
        o_ref = refs[-1]
        ref_of = {id(src): ref for src, ref in zip(sources, refs)}
        off = 0
        for src, row, width in pieces:
            o_ref[:, off:off + width] = ref_of[id(src)][row:row + 1, :]
            off += width

    return pl.pallas_call(body, name="pack_grads", out_shape=jax.ShapeDtypeStruct((1, PACKED_TOTAL), F32))(*sources)


def _small_adamw(gathered, gathered_rel, gathered_loss, weights, mom_m, mom_v):
    vec_names = [name for name, _ in SMALL]
    states = []
    for name in vec_names + ["rel_bias"]:
        states += [weights[name], mom_m[name], mom_v[name]]
    states = [a.reshape(a.shape[1:]) if a.ndim == 3 else a for a in states]
    n_state = len(states)

    def body(*refs):
        g_ref, rel_ref, loss_ref = refs[0], refs[1], refs[2]
        state_refs, out_refs = refs[3:3 + n_state], refs[3 + n_state:]
        total = g_ref[0:1, :]
        rel = rel_ref[0]
        loss = loss_ref[0]
        for d in range(1, N_DEV):
            total = total + g_ref[d:d + 1, :]
            rel = rel + rel_ref[d]
            loss = loss + loss_ref[d]
        off = 0
        for n, (name, width) in enumerate(SMALL):
            grad = total[:, off:off + width]
            w_ref, m_ref, v_ref = state_refs[3 * n:3 * n + 3]
            for ref, val in zip(out_refs[4 * n:4 * n + 4], (grad,) + _adam_math(w_ref[...], grad, m_ref[...], v_ref[...])):
                ref[...] = val
            off += width
        n = len(SMALL)
        w_ref, m_ref, v_ref = state_refs[3 * n:3 * n + 3]
        for ref, val in zip(out_refs[4 * n:4 * n + 4], (rel,) + _adam_math(w_ref[...], rel, m_ref[...], v_ref[...])):
            ref[...] = val
        dwc_ref, dwf_ref, loss_out = out_refs[4 * n + 4:]
        loss_out[...] = 0.5 * loss
        dwc_ref[...] = jnp.zeros_like(dwc_ref)
        dwf_ref[...] = jnp.zeros_like(dwf_ref)
        for j in range(CONV_K):
            dwc_ref[j:j + 1, :] = total[:, off:off + D_CONV]
            off += D_CONV
        for tap in range(3):
            dwf_ref[tap:tap + 1, :] = total[:, off:off + 2 * D_FF]
            off += 2 * D_FF

    out_shape = []
    for k in range(n_state // 3):
        out_shape += [jax.ShapeDtypeStruct(states[3 * k].shape, F32)] * 4
    out_shape += [jax.ShapeDtypeStruct((CONV_HALO, D_CONV), F32), jax.ShapeDtypeStruct((8, 2 * D_FF), F32),
                  jax.ShapeDtypeStruct((1, 128), F32)]
    res = pl.pallas_call(
        body, name="small_adamw", out_shape=out_shape,
        compiler_params=pltpu.CompilerParams(vmem_limit_bytes=VMEM_LIMIT_BYTES),
    )(gathered, gathered_rel, gathered_loss, *states)
    updates = {name: tuple(res[4 * n:4 * n + 4]) for n, name in enumerate(vec_names + ["rel_bias"])}
    return updates, res[-3], res[-2], res[-1]


def _ada_mod(c, w_shard):
    cols = w_shard.shape[1]

    def body(c_ref, w_ref, call_ref, mod_ref, send_sems, recv_sems):
        x, y, cc = _place()
        me = 4 * x + 2 * y + cc

        def exchange(ref, phase):
            sends, arrivals = [], []
            for mask in range(1, N_DEV):
                px = 1 - x if mask & 4 else x
                py = 1 - y if mask & 2 else y
                pc = 1 - cc if mask & 1 else cc
                both = dict(send_sem=send_sems.at[7 * phase + mask - 1], recv_sem=recv_sems.at[7 * phase + mask - 1],
                            device_id=(px, py, pc), device_id_type=MESH)
                sends.append(pltpu.make_async_remote_copy(src_ref=ref.at[me], dst_ref=ref.at[me], **both))
                arrivals.append(pltpu.make_async_remote_copy(src_ref=ref.at[me], dst_ref=ref.at[4 * px + 2 * py + pc],
                                                             **both))
            for cp in sends:
                cp.start()
            for cp in arrivals:
                cp.wait_recv()
            for cp in sends:
                cp.wait_send()

        v = c_ref[...]
        call_ref[me] = v * _sig(v)
        exchange(call_ref, 0)
        c_all = jnp.concatenate([call_ref[d] for d in range(N_DEV)], axis=0)
        mod_ref[me] = jnp.dot(c_all, w_ref[...], precision=HIGHEST, preferred_element_type=F32)
        exchange(mod_ref, 1)

    return pl.pallas_call(
        body, name="ada_mod",
        out_shape=[jax.ShapeDtypeStruct((N_DEV, 1, D_MODEL), F32), jax.ShapeDtypeStruct((N_DEV, N_DEV, cols), F32)],
        scratch_shapes=[pltpu.SemaphoreType.DMA((14,)), pltpu.SemaphoreType.DMA((14,))],
        compiler_params=pltpu.CompilerParams(vmem_limit_bytes=VMEM_LIMIT_BYTES),
    )(c, w_shard)


def _ada_grad(c_all, dmod_shard):
    def body(c_ref, d_ref, o_ref):
        o_ref[...] = lax.dot_general(c_ref[...], d_ref[...], _DIMS["tn"], precision=HIGHEST,
                                     preferred_element_type=F32)

    return pl.pallas_call(
        body, name="ada_grad", out_shape=jax.ShapeDtypeStruct((D_MODEL, dmod_shard.shape[1]), F32),
        compiler_params=pltpu.CompilerParams(vmem_limit_bytes=VMEM_LIMIT_BYTES),
    )(c_all, dmod_shard)


ROWS = 256


def _rms(v):
    r = lax.rsqrt(jnp.mean(v * v, axis=-1, keepdims=True) + EPS)
    return v * r, r


def _rms_bwd(dxn, xn, r):
    return r * (dxn - xn * jnp.mean(dxn * xn, axis=-1, keepdims=True))


def _colsum(v):
    return jnp.sum(v, axis=0, keepdims=True)


def _pre_mix(x, mod6, g1, comm=None):
    seq = x.shape[0]

    def body(ins, outs, scratch):
        x_ref, mod_ref, g_ref = ins
        xn, _ = _rms(x_ref[...])
        y = xn * g_ref[...]
        outs[0][...] = (y * (1.0 + mod_ref[SC_M:SC_M + 1, :]) + mod_ref[SH_M:SH_M + 1, :]).astype(BF16)

    (h,), extra = _host_call(
        body, "pre_mix", grid=(seq // ROWS,),
        in_specs=[_tile(ROWS, D_MODEL), _full((6, D_MODEL)), _full((1, D_MODEL))], out_specs=[_tile(ROWS, D_MODEL)],
        out_shape=[jax.ShapeDtypeStruct((seq, D_MODEL), BF16)], scratch_shapes=[], args=[x, mod6, g1], comm=comm)
    return h, extra


def _post_mix_pre_ffn(x, mod6, g2, g3, rows):
    seq = x.shape[0]

    def fn(y, first, ins, outs):
        x_ref, mod_ref, g2_ref, g3_ref = ins
        x1_ref, h_ref = outs
        yn, _ = _rms(y)
        x1 = x_ref[...] + mod_ref[GT_M:GT_M + 1, :] * (yn * g2_ref[...])
        x1_ref[...] = x1
        xn, _ = _rms(x1)
        y3 = xn * g3_ref[...]
        h_ref[...] = (y3 * (1.0 + mod_ref[SC_F:SC_F + 1, :]) + mod_ref[SH_F:SH_F + 1, :]).astype(BF16)

    return _Epilogue(
        [x, mod6, g2, g3], [_row_tile(rows, D_MODEL), _whole((6, D_MODEL)), _whole((1, D_MODEL)), _whole((1, D_MODEL))],
        [jax.ShapeDtypeStruct((seq, D_MODEL), F32), jax.ShapeDtypeStruct((seq, D_MODEL), BF16)],
        [_row_tile(rows, D_MODEL), _row_tile(rows, D_MODEL)], fn, keep_product=True)


def _final(x1, target, mod6, g4, rows):
    seq = x1.shape[0]

    def fn(y, first, ins, outs):
        x1_ref, t_ref, mod_ref, g_ref = ins
        loss_ref, dout_ref, dyf_ref, small_ref = outs

        @pl.when(first)
        def _():
            loss_ref[...] = jnp.zeros_like(loss_ref)
            small_ref[...] = jnp.zeros_like(small_ref)

        gt = mod_ref[GT_F:GT_F + 1, :]
        g4v = g_ref[...]
        yn, r = _rms(y)
        out = x1_ref[...] + gt * (yn * g4v)
        err = out - t_ref[...]
        loss_ref[...] += jnp.sum(jnp.mean(err * err, axis=-1, keepdims=True))
        dout = err * (1.0 / D_MODEL)
        dout_ref[...] = dout
        small_ref[0:1, :] += _colsum(dout * gt * yn)
        small_ref[1:2, :] += _colsum(dout * (yn * g4v))
        dyf_ref[...] = _rms_bwd(dout * gt * g4v, yn, r).astype(BF16)

    return _Epilogue(
        [x1, target, mod6, g4],
        [_row_tile(rows, D_MODEL), _row_tile(rows, D_MODEL), _whole((6, D_MODEL)), _whole((1, D_MODEL))],
        [jax.ShapeDtypeStruct((1, 128), F32), jax.ShapeDtypeStruct((seq, D_MODEL), F32),
         jax.ShapeDtypeStruct((seq, D_MODEL), BF16), jax.ShapeDtypeStruct((8, D_MODEL), F32)],
        [_whole((1, 128)), _row_tile(rows, D_MODEL), _row_tile(rows, D_MODEL), _whole((8, D_MODEL))],
        fn, keep_product=False)


def _mid_bwd(x1, dout, ymix, mod6, g3, g2, rows):
    seq = x1.shape[0]

    def fn(dh, first, ins, outs):
        x1_ref, dout_ref, y_ref, mod_ref, g3_ref, g2_ref = ins
        dx1_ref, dy_ref, small_ref = outs

        @pl.when(first)
        def _():
            small_ref[...] = jnp.zeros_like(small_ref)

        g3v, g2v = g3_ref[...], g2_ref[...]
        xn, r3 = _rms(x1_ref[...])
        y3 = xn * g3v
        dy3 = dh * (1.0 + mod_ref[SC_F:SC_F + 1, :])
        small_ref[0:1, :] += _colsum(dy3 * xn)
        small_ref[1:2, :] += _colsum(dh * y3)
        small_ref[2:3, :] += _colsum(dh)
        dx1 = dout_ref[...] + _rms_bwd(dy3 * g3v, xn, r3)
        dx1_ref[...] = dx1
        gt = mod_ref[GT_M:GT_M + 1, :]
        yn, r2 = _rms(y_ref[...])
        small_ref[3:4, :] += _colsum(dx1 * gt * yn)
        small_ref[4:5, :] += _colsum(dx1 * (yn * g2v))
        dy_ref[...] = _rms_bwd(dx1 * gt * g2v, yn, r2).astype(BF16)

    return _Epilogue(
        [x1, dout, ymix, mod6, g3, g2],
        [_row_tile(rows, D_MODEL)] * 3 + [_whole((6, D_MODEL)), _whole((1, D_MODEL)), _whole((1, D_MODEL))],
        [jax.ShapeDtypeStruct((seq, D_MODEL), F32), jax.ShapeDtypeStruct((seq, D_MODEL), BF16),
         jax.ShapeDtypeStruct((8, D_MODEL), F32)],
        [_row_tile(rows, D_MODEL), _row_tile(rows, D_MODEL), _whole((8, D_MODEL))], fn, keep_product=False)


def _pre_mix_bwd(x, dx1, mod6, g1, rows):
    seq = x.shape[0]

    def fn(dh, first, ins, outs):
        x_ref, dx1_ref, mod_ref, g_ref = ins
        dx_ref, small_ref = outs

        @pl.when(first)
        def _():
            small_ref[...] = jnp.zeros_like(small_ref)

        g1v = g_ref[...]
        xn, r = _rms(x_ref[...])
        dy = dh * (1.0 + mod_ref[SC_M:SC_M + 1, :])
        small_ref[0:1, :] += _colsum(dy * xn)
        small_ref[1:2, :] += _colsum(dh * (xn * g1v))
        small_ref[2:3, :] += _colsum(dh)
        dx_ref[...] = dx1_ref[...] + _rms_bwd(dy * g1v, xn, r)

    return _Epilogue(
        [x, dx1, mod6, g1],
        [_row_tile(rows, D_MODEL), _row_tile(rows, D_MODEL), _whole((6, D_MODEL)), _whole((1, D_MODEL))],
        [jax.ShapeDtypeStruct((seq, D_MODEL), F32), jax.ShapeDtypeStruct((8, D_MODEL), F32)],
        [_row_tile(rows, D_MODEL), _whole((8, D_MODEL))], fn, keep_product=False)


def _toeplitz_onehot(shape, offset_axis, top):
    m = lax.broadcasted_iota(jnp.int32, shape, offset_axis)
    i = lax.broadcasted_iota(jnp.int32, shape, 1 - offset_axis)
    return (i == jnp.clip(top - m, -MAX_REL, MAX_REL) + MAX_REL).astype(F32)


def _bias_table(rel_bias):
    width = GROUP_Q + GROUP_K

    def body(rb_ref, o_ref, t_ref):
        t_ref[...] = jnp.dot(rb_ref[...], _toeplitz_onehot((N_REL, width), 1, GROUP_K - 1), precision=HIGHEST,
                             preferred_element_type=F32)
        lane = lax.broadcasted_iota(jnp.int32, (N_HEADS, GROUP_K), 1)
        for r in range(GROUP_Q):
            first_key = (r // CHUNK) * CHUNK
            band = jnp.logical_and(lane >= first_key, lane < first_key + BAND)
            o_ref[r] = jnp.where(band, t_ref[:, GROUP_Q - 1 - r:GROUP_Q - 1 - r + GROUP_K], NEG_INF)

    return pl.pallas_call(
        body, name="bias_table", out_shape=jax.ShapeDtypeStruct((GROUP_Q, N_HEADS, GROUP_K), F32),
        scratch_shapes=[pltpu.VMEM((N_HEADS, width), F32)],
    )(rel_bias)


def _bias_grad(dbias_q):
    def body(d_ref, o_ref, t_ref):
        t_ref[...] = jnp.zeros_like(t_ref)
        for qi in range(CHUNK):
            t_ref[:, CHUNK - 1 - qi:CHUNK - 1 - qi + BAND] += d_ref[qi]
        o_ref[...] = jnp.dot(t_ref[...], _toeplitz_onehot((TOEPLITZ, N_REL), 0, BAND - 1), precision=HIGHEST,
                             preferred_element_type=F32)

    return pl.pallas_call(
        body, name="bias_grad", out_shape=jax.ShapeDtypeStruct((N_HEADS, N_REL), F32),
        scratch_shapes=[pltpu.VMEM((N_HEADS, TOEPLITZ), F32)],
    )(dbias_q)


def _load_resident(pairs, sems):
    copies = [pltpu.make_async_copy(src, dst, sems.at[n]) for n, (src, dst) in enumerate(pairs)]
    for cp in copies:
        cp.start()
    for cp in copies:
        cp.wait()


def _softmax_rows(s_ref, t_ref, before_start, rows):
    s = s_ref[rows, :] * (HEAD_DIM ** -0.5) + t_ref[rows, :] + before_start
    e = jnp.exp(s - jnp.max(s, axis=-1, keepdims=True))
    return e / jnp.sum(e, axis=-1, keepdims=True)


def _before_start(g):
    kj = lax.broadcasted_iota(jnp.int32, (8, GROUP_K), 1)
    return jnp.where(kj >= PAD_ROWS - g * GROUP_Q, 0.0, NEG_INF)


def _attn_fwd(qkv, kpad, vpad, table, comm=None):
    seq = qkv.shape[0]

    def body(ins, outs, scratch):
        q_ref, k_hbm, v_hbm, t_hbm = ins
        (o_ref,) = outs
        k_ref, v_ref, t_ref, s_ref, p_ref, sems = scratch
        g = pl.program_id(0)

        @pl.when(g == 0)
        def _():
            _load_resident(((k_hbm, k_ref), (v_hbm, v_ref), (t_hbm, t_ref)), sems)

        window = pl.ds(pl.multiple_of(g * GROUP_Q, GROUP_Q), GROUP_K)
        before_start = _before_start(g)
        for h in range(N_HEADS):
            cols = slice(h * HEAD_DIM, (h + 1) * HEAD_DIM)
            buf = h % 2
            s_ref[buf] = lax.dot_general(q_ref[:, cols], k_ref[window, cols], _DIMS["nt"],
                                         preferred_element_type=F32)
            for row in range(0, GROUP_Q, SOFTMAX_ROWS):
                halves = [_softmax_rows(s_ref.at[buf], t_ref.at[h], before_start, slice(r, r + 8))
                          for r in (row, row + 8)]
                p_ref[buf, row:row + SOFTMAX_ROWS, :] = jnp.concatenate(halves, axis=0).astype(BF16)
            o_ref[:, cols] = jnp.dot(p_ref[buf], v_ref[window, cols], preferred_element_type=F32).astype(BF16)

    (ao,), extra = _host_call(
        body, "attn_fwd", grid=(seq // GROUP_Q,),
        in_specs=[_tile(GROUP_Q, D_ATTN), ANY, ANY, ANY], out_specs=[_tile(GROUP_Q, D_ATTN)],
        out_shape=[jax.ShapeDtypeStruct((seq, D_ATTN), BF16)],
        scratch_shapes=[pltpu.VMEM(kpad.shape, BF16), pltpu.VMEM(vpad.shape, BF16), pltpu.VMEM(table.shape, F32),
                        pltpu.VMEM((2, GROUP_Q, GROUP_K), F32), pltpu.VMEM((2, GROUP_Q, GROUP_K), BF16),
                        pltpu.SemaphoreType.DMA((3,))],
        args=[qkv, kpad, vpad, table], comm=comm)
    return ao, extra


def _attn_bwd(qkv, kpad, vpad, table, dao, comm=None):
    seq = qkv.shape[0]
    n_groups = seq // GROUP_Q
    fold_w = GROUP_K + (GROUP - 1) * CHUNK

    def body(ins, outs, scratch):
        q_ref, do_ref, k_hbm, v_hbm, t_hbm = ins
        dq_ref, dkt_hbm, dvt_hbm, db_ref, cs_ref = outs
        k_ref, v_ref, t_ref, db_acc, dkt_acc, dvt_acc, s_ref, dp_ref, p_ref, ds_ref, sems = scratch
        g = pl.program_id(0)

        @pl.when(g == 0)
        def _():
            _load_resident(((k_hbm, k_ref), (v_hbm, v_ref), (t_hbm, t_ref)), sems)
            db_acc[...] = jnp.zeros_like(db_acc)
            dkt_acc[...] = jnp.zeros_like(dkt_acc)
            dvt_acc[...] = jnp.zeros_like(dvt_acc)
            cs_ref[...] = jnp.zeros_like(cs_ref)

        window = pl.ds(pl.multiple_of(g * GROUP_Q, GROUP_Q), GROUP_K)
        before_start = _before_start(g)
        for h in range(N_HEADS):
            cols = slice(h * HEAD_DIM, (h + 1) * HEAD_DIM)
            buf = h % 2
            qh, doh = q_ref[:, cols], do_ref[:, cols]
            kh, vh = k_ref[window, cols], v_ref[window, cols]
            s_ref[buf] = lax.dot_general(qh, kh, _DIMS["nt"], preferred_element_type=F32)
            dp_ref[buf] = lax.dot_general(doh, vh, _DIMS["nt"], preferred_element_type=F32)
            for row in range(0, GROUP_Q, SOFTMAX_ROWS):
                p_halves, ds_halves = [], []
                for r in (row, row + 8):
                    p = _softmax_rows(s_ref.at[buf], t_ref.at[h], before_start, slice(r, r + 8))
                    dp = dp_ref[buf, r:r + 8, :]
                    ds = p * (dp - jnp.sum(dp * p, axis=-1, keepdims=True))
                    chunk = r // CHUNK
                    shift = (GROUP - 1 - chunk) * CHUNK
                    db_acc[h, r - chunk * CHUNK:r - chunk * CHUNK + 8, shift:shift + GROUP_K] += ds
                    p_halves.append(p)
                    ds_halves.append(ds * (HEAD_DIM ** -0.5))
                p_ref[buf, row:row + SOFTMAX_ROWS, :] = jnp.concatenate(p_halves, axis=0).astype(BF16)
                ds_ref[buf, row:row + SOFTMAX_ROWS, :] = jnp.concatenate(ds_halves, axis=0).astype(BF16)
            dq_ref[:, cols] = jnp.dot(ds_ref[buf], kh, preferred_element_type=F32).astype(BF16)
            dkt_acc[cols, window] += lax.dot_general(qh, ds_ref[buf], _DIMS["tn"], preferred_element_type=F32)
            dvt_acc[cols, window] += lax.dot_general(doh, p_ref[buf], _DIMS["tn"], preferred_element_type=F32)
        cs_ref[0:1, :] += _colsum(dq_ref[...].astype(F32))

        @pl.when(g == n_groups - 1)
        def _():
            lo = (GROUP - 1) * CHUNK
            for h in range(N_HEADS):
                db_ref[h] = db_acc[h, :, lo:lo + BAND]
            inside = pl.ds(PAD_ROWS, seq)
            on_diagonal = (lax.broadcasted_iota(jnp.int32, (D_ATTN, D_ATTN), 0)
                           == lax.broadcasted_iota(jnp.int32, (D_ATTN, D_ATTN), 1))
            for row, acc in ((1, dkt_acc), (2, dvt_acc)):
                column = jnp.sum(acc[:, inside], axis=1, keepdims=True)
                cs_ref[row:row + 1, :] = _colsum(jnp.where(on_diagonal, column, 0.0))
            out_k = pltpu.make_async_copy(dkt_acc.at[:, inside], dkt_hbm, sems.at[0])
            out_v = pltpu.make_async_copy(dvt_acc.at[:, inside], dvt_hbm, sems.at[1])
            out_k.start()
            out_v.start()
            out_k.wait()
            out_v.wait()

    t_shape = (D_ATTN, seq + PAD_ROWS)
    outs, extra = _host_call(
        body, "attn_bwd", grid=(n_groups,),
        in_specs=[_tile(GROUP_Q, D_ATTN), _tile(GROUP_Q, D_ATTN), ANY, ANY, ANY],
        out_specs=[_tile(GROUP_Q, D_ATTN), ANY, ANY, _full((N_HEADS, CHUNK, BAND)), _full((8, D_ATTN))],
        out_shape=[jax.ShapeDtypeStruct((seq, D_ATTN), BF16), jax.ShapeDtypeStruct((D_ATTN, seq), F32),
                   jax.ShapeDtypeStruct((D_ATTN, seq), F32), jax.ShapeDtypeStruct((N_HEADS, CHUNK, BAND), F32),
                   jax.ShapeDtypeStruct((8, D_ATTN), F32)],
        scratch_shapes=[pltpu.VMEM(kpad.shape, BF16), pltpu.VMEM(vpad.shape, BF16), pltpu.VMEM(table.shape, F32),
                        pltpu.VMEM((N_HEADS, CHUNK, fold_w), F32), pltpu.VMEM(t_shape, F32),
                        pltpu.VMEM(t_shape, F32), pltpu.VMEM((2, GROUP_Q, GROUP_K), F32),
                        pltpu.VMEM((2, GROUP_Q, GROUP_K), F32), pltpu.VMEM((2, GROUP_Q, GROUP_K), BF16),
                        pltpu.VMEM((2, GROUP_Q, GROUP_K), BF16), pltpu.SemaphoreType.DMA((3,))],
        args=[qkv, dao, kpad, vpad, table], comm=comm)
    return outs, extra


def _assemble_dz(dq, dkt, dvt, dglu_a, dglu_b, dga, dgb):
    seq = dq.shape[0]
    rows = 512
    transposed = pl.BlockSpec((D_ATTN, rows), lambda i: (0, i))

    def body(dq_ref, dkt_ref, dvt_ref, da_ref, db_ref, dga_ref, dgb_ref, o_ref):
        o_ref[:, 0:D_ATTN] = dq_ref[...]
        o_ref[:, D_ATTN:2 * D_ATTN] = dkt_ref[...].T.astype(BF16)
        o_ref[:, 2 * D_ATTN:3 * D_ATTN] = dvt_ref[...].T.astype(BF16)
        off = 3 * D_ATTN
        for ref in (da_ref, db_ref, dga_ref, dgb_ref):
            width = ref.shape[1]
            o_ref[:, off:off + width] = ref[...]
            off += width

    width = 3 * D_ATTN + 2 * D_CONV + 2 * D_MODEL
    return pl.pallas_call(
        body, name="assemble_dz", out_shape=jax.ShapeDtypeStruct((seq, width), BF16), grid=(seq // rows,),
        in_specs=[_tile(rows, D_ATTN), transposed, transposed, _tile(rows, D_CONV), _tile(rows, D_CONV),
                  _tile(rows, D_MODEL), _tile(rows, D_MODEL)],
        out_specs=_tile(rows, width), compiler_params=_cparams(1),
    )(dq, dkt, dvt, dglu_a, dglu_b, dga, dgb)


CONV_ROWS = 256


def _ln_silu(u1, g, b):
    mu = jnp.mean(u1, axis=-1, keepdims=True)
    xc = u1 - mu
    rs = lax.rsqrt(jnp.mean(xc * xc, axis=-1, keepdims=True) + EPS)
    xhat = xc * rs
    u2 = xhat * g + b
    return xhat, rs, u2


def _glu_into(s_ref, a_ref, b_ref, ah_ref, bh_ref, first):
    halo = ah_ref[...] * _sig(bh_ref[...])
    s_ref[0:CONV_HALO, :] = jnp.where(first, 0.0, halo)
    s_ref[CONV_HALO:CONV_HALO + CONV_ROWS, :] = a_ref[...] * _sig(b_ref[...])


CONV_LANES = 128
CONV_TILES = CONV_ROWS // 8


def _lag_weights(w_ref, lanes):
    return {e: jnp.broadcast_to(w_ref[CONV_K - 1 - e:CONV_K - e, lanes], (8, CONV_LANES)) for e in range(CONV_K)}


def _class_sums(w, tiles, k):
    total = None
    for a, tile in enumerate(tiles):
        if 8 * a + k < CONV_K:
            term = w[8 * a + k] * tile
            total = term if total is None else total + term
    return total


def _conv_back(src_ref, first_tile, w, lanes, row_id, emit):
    before = None
    for m in range(-1, CONV_TILES):
        tiles = [src_ref[8 * (first_tile + m - a):8 * (first_tile + m - a) + 8, lanes] for a in range(4)]
        rolled = [None] + [pltpu.roll(_class_sums(w, tiles, k), k, 0) for k in range(1, 8)]
        if m >= 0:
            out = _class_sums(w, tiles, 0)
            for k in range(1, 8):
                out = out + jnp.where(row_id < k, before[k], rolled[k])
            emit(m, out)
        before = rolled


def _conv_ahead(src_ref, w, lanes, row_id, emit):
    before = None
    for m in range(CONV_TILES + 1):
        tiles = [src_ref[8 * (m + a):8 * (m + a) + 8, lanes] for a in range(4)]
        rolled = [None] + [pltpu.roll(_class_sums(w, tiles, k), 8 - k, 0) for k in range(1, 8)]
        if m >= 1:
            out = before[0]
            for k in range(1, 8):
                out = out + jnp.where(row_id < 8 - k, before[k], rolled[k])
            emit(m - 1, out)
        before = [_class_sums(w, tiles, 0) if m < CONV_TILES else None] + rolled[1:]


def _conv_weight_sums(d_ref, s_ref, lanes, row_id, whole_shifts):
    zero = jnp.zeros((8, CONV_LANES), F32)
    sums = {8 * a + k: zero for a in whole_shifts for k in range(8) if 8 * a + k < CONV_K}

    def d_tile(m):
        return d_ref[8 * m:8 * m + 8, lanes] if 0 <= m < CONV_TILES else zero

    rolled = [None] + [zero] * 7
    for m in range(-1, CONV_TILES):
        cur, nxt = d_tile(m), d_tile(m + 1)
        rolled_next = [None] + [pltpu.roll(nxt, 8 - k, 0) for k in range(1, 8)]
        shifted = [cur] + [jnp.where(row_id < 8 - k, rolled[k], rolled_next[k]) for k in range(1, 8)]
        for a in whole_shifts:
            tile = s_ref[8 * (CONV_HALO // 8 + m - a):8 * (CONV_HALO // 8 + m - a) + 8, lanes]
            for k in range(8):
                if 8 * a + k < CONV_K and not (m < 0 and k == 0):
                    sums[8 * a + k] = sums[8 * a + k] + shifted[k] * tile
        rolled = rolled_next
    return sums


def _conv_fwd(zr, w_dw, b_dw, g_ln, b_ln, comm=None):
    seq = zr.shape[0]

    def body(a_ref, b_ref, ah_ref, bh_ref, w_ref, bias_ref, g_ref, bl_ref, u1_ref, u3_ref, s_ref):
        _glu_into(s_ref, a_ref, b_ref, ah_ref, bh_ref, pl.program_id(0) == 0)
        row_id = lax.broadcasted_iota(jnp.int32, (8, CONV_LANES), 0)
        for lo in range(0, D_CONV, CONV_LANES):
            lanes = slice(lo, lo + CONV_LANES)
            bias = jnp.broadcast_to(bias_ref[:, lanes], (8, CONV_LANES))

            def emit(m, out, lanes=lanes, bias=bias):
                u1_ref[8 * m:8 * m + 8, lanes] = out + bias

            _conv_back(s_ref, CONV_HALO // 8, _lag_weights(w_ref, lanes), lanes, row_id, emit)
        _, _, u2 = _ln_silu(u1_ref[...], g_ref[...], bl_ref[...])
        u3_ref[...] = (u2 * _sig(u2)).astype(BF16)

    return _host_call(
        lambda ins, outs, scratch: body(*ins, *outs, *scratch), "conv_fwd", grid=(seq // CONV_ROWS,),
        in_specs=[_tile(CONV_ROWS, D_CONV, 0), _tile(CONV_ROWS, D_CONV, 1),
                  _prev(CONV_HALO, D_CONV, CONV_ROWS, 0), _prev(CONV_HALO, D_CONV, CONV_ROWS, 1),
                  _full((CONV_K, D_CONV)), _full((1, D_CONV)), _full((1, D_CONV)), _full((1, D_CONV))],
        out_specs=[_tile(CONV_ROWS, D_CONV), _tile(CONV_ROWS, D_CONV)],
        out_shape=[jax.ShapeDtypeStruct((seq, D_CONV), F32), jax.ShapeDtypeStruct((seq, D_CONV), BF16)],
        scratch_shapes=[pltpu.VMEM((CONV_HALO + CONV_ROWS, D_CONV), F32)],
        args=[zr, zr, zr, zr, w_dw, b_dw, g_ln, b_ln], comm=comm)


def _conv_bwd(zr, u1, du3, w_dw, g_ln, b_ln, comm=None):
    seq = zr.shape[0]
    n_tiles = seq // CONV_ROWS
    n_halo = seq // CONV_HALO
    ext = CONV_ROWS + CONV_HALO

    def body(a_ref, b_ref, ah_ref, bh_ref, u1_ref, u1n_ref, d3_ref, d3n_ref, w_ref, g_ref, bl_ref,
             da_ref, db_ref, dw_ref, small_ref, s_ref, d_ref, du0_ref):
        i = pl.program_id(0)

        @pl.when(i == 0)
        def _():
            dw_ref[...] = jnp.zeros_like(dw_ref)
            small_ref[...] = jnp.zeros_like(small_ref)

        _glu_into(s_ref, a_ref, b_ref, ah_ref, bh_ref, i == 0)
        gv, bv = g_ref[...], bl_ref[...]

        def du1_of(u1, d3):
            xhat, rs, u2 = _ln_silu(u1, gv, bv)
            sg = _sig(u2)
            du2 = d3 * (sg * (1.0 + u2 * (1.0 - sg)))
            dxh = du2 * gv
            du1 = rs * (dxh - jnp.mean(dxh, axis=-1, keepdims=True)
                        - xhat * jnp.mean(dxh * xhat, axis=-1, keepdims=True))
            return du1, du2, xhat

        du1, du2, xhat = du1_of(u1_ref[...], d3_ref[...])
        du1n, _, _ = du1_of(u1n_ref[...], d3n_ref[...])
        d_ref[0:CONV_ROWS, :] = du1
        d_ref[CONV_ROWS:ext, :] = jnp.where(i == n_tiles - 1, 0.0, du1n)
        small_ref[0:1, :] += _colsum(du1)
        small_ref[1:2, :] += _colsum(du2 * xhat)
        small_ref[2:3, :] += _colsum(du2)
        row_id = lax.broadcasted_iota(jnp.int32, (8, CONV_LANES), 0)
        for lo in range(0, D_CONV, CONV_LANES):
            lanes = slice(lo, lo + CONV_LANES)

            def emit(m, out, lanes=lanes):
                du0_ref[8 * m:8 * m + 8, lanes] = out

            _conv_ahead(d_ref, _lag_weights(w_ref, lanes), lanes, row_id, emit)
            for whole_shifts in ((0, 1), (2, 3)):
                for e, total in _conv_weight_sums(d_ref, s_ref, lanes, row_id, whole_shifts).items():
                    dw_ref[CONV_K - 1 - e:CONV_K - e, lanes] += _colsum(total)
        du0 = du0_ref[...]
        sb = _sig(b_ref[...])
        da = du0 * sb
        dbv = du0 * a_ref[...] * sb * (1.0 - sb)
        da_ref[...] = da.astype(BF16)
        db_ref[...] = dbv.astype(BF16)
        small_ref[3:4, :] += _colsum(da)
        small_ref[4:5, :] += _colsum(dbv)

    return _host_call(
        lambda ins, outs, scratch: body(*ins, *outs, *scratch), "conv_bwd", grid=(n_tiles,),
        in_specs=[_tile(CONV_ROWS, D_CONV, 0), _tile(CONV_ROWS, D_CONV, 1),
                  _prev(CONV_HALO, D_CONV, CONV_ROWS, 0), _prev(CONV_HALO, D_CONV, CONV_ROWS, 1),
                  _tile(CONV_ROWS, D_CONV), _next(CONV_HALO, D_CONV, CONV_ROWS, n_halo),
                  _tile(CONV_ROWS, D_CONV), _next(CONV_HALO, D_CONV, CONV_ROWS, n_halo),
                  _full((CONV_K, D_CONV)), _full((1, D_CONV)), _full((1, D_CONV))],
        out_specs=[_tile(CONV_ROWS, D_CONV), _tile(CONV_ROWS, D_CONV), _full((CONV_HALO, D_CONV)),
                   _full((8, D_CONV))],
        out_shape=[jax.ShapeDtypeStruct((seq, D_CONV), BF16), jax.ShapeDtypeStruct((seq, D_CONV), BF16),
                   jax.ShapeDtypeStruct((CONV_HALO, D_CONV), F32), jax.ShapeDtypeStruct((8, D_CONV), F32)],
        scratch_shapes=[pltpu.VMEM((ext, D_CONV), F32), pltpu.VMEM((ext, D_CONV), F32),
                        pltpu.VMEM((CONV_ROWS, D_CONV), F32)],
        args=[zr, zr, zr, zr, u1, u1, du3, du3, w_dw, g_ln, b_ln], comm=comm)


MERGE_ROWS = 256


def _merge_fwd(ao, u3, zr, w_ao, w_co, b_co):
    seq = ao.shape[0]

    def body(ao_ref, u3_ref, ga_ref, gb_ref, wa_ref, wc_ref, bc_ref, y_ref, a_ref, cb_ref):
        a = jnp.dot(ao_ref[...], wa_ref[...], preferred_element_type=F32)
        cb = jnp.dot(u3_ref[...], wc_ref[...], preferred_element_type=F32) + bc_ref[...]
        a_ref[...] = a
        cb_ref[...] = cb
        y_ref[...] = (_sig(ga_ref[...]) * a + _sig(gb_ref[...]) * cb).astype(BF16)

    f32_out = jax.ShapeDtypeStruct((seq, D_MODEL), F32)
    return pl.pallas_call(
        body, name="merge_fwd",
        out_shape=[jax.ShapeDtypeStruct((seq, D_MODEL), BF16), f32_out, f32_out],
        grid=(seq // MERGE_ROWS,),
        in_specs=[_tile(MERGE_ROWS, D_ATTN), _tile(MERGE_ROWS, D_CONV), _tile(MERGE_ROWS, D_MODEL, 1),
                  _tile(MERGE_ROWS, D_MODEL, 2), _full(w_ao.shape), _full(w_co.shape), _full((1, D_MODEL))],
        out_specs=[_tile(MERGE_ROWS, D_MODEL)] * 3, compiler_params=_cparams(1),
    )(ao, u3, zr, zr, w_ao, w_co, b_co)


def _merge_bwd(a, cb, zr, rows):
    seq = a.shape[0]

    def fn(dy_v, first, ins, outs):
        a_ref, cb_ref, ga_ref, gb_ref = ins
        da_ref, dcb_ref, dga_ref, dgb_ref, small_ref = outs

        @pl.when(first)
        def _():
            small_ref[...] = jnp.zeros_like(small_ref)

        sa, sb = _sig(ga_ref[...]), _sig(gb_ref[...])
        dcb = dy_v * sb
        dga = dy_v * a_ref[...] * sa * (1.0 - sa)
        dgb = dy_v * cb_ref[...] * sb * (1.0 - sb)
        da_ref[...] = (dy_v * sa).astype(BF16)
        dcb_ref[...] = dcb.astype(BF16)
        dga_ref[...] = dga.astype(BF16)
        dgb_ref[...] = dgb.astype(BF16)
        small_ref[0:1, :] += _colsum(dga)
        small_ref[1:2, :] += _colsum(dgb)
        small_ref[2:3, :] += _colsum(dcb)

    bf = jax.ShapeDtypeStruct((seq, D_MODEL), BF16)
    gate = lambda col: pl.BlockSpec((rows, D_MODEL), lambda i, j: (i, col))
    return _Epilogue(
        [a, cb, zr, zr], [_row_tile(rows, D_MODEL), _row_tile(rows, D_MODEL), gate(1), gate(2)],
        [bf, bf, bf, bf, jax.ShapeDtypeStruct((8, D_MODEL), F32)],
        [_row_tile(rows, D_MODEL)] * 4 + [_whole((8, D_MODEL))], fn, keep_product=False)


FFN_ROWS = 2048
FFN_BLOCKS = D_FF // FFN_COLS
GELU_C = math.sqrt(2.0 / math.pi)


def _gelu(v):
    t = jnp.tanh(GELU_C * (v + 0.044715 * (v * v * v)))
    return 0.5 * v * (1.0 + t), t


def _gelu_grad(v, t):
    return 0.5 * (1.0 + t) + 0.5 * v * (1.0 - t * t) * (GELU_C * (1.0 + 3.0 * 0.044715 * (v * v)))


def _sublane_rows(ref, n):
    return [jnp.broadcast_to(ref[r:r + 1, :], (8, FFN_COLS)) for r in range(n)]


def _rolls(tile, shifts):
    return tuple(pltpu.roll(tile, s, 0) for s in shifts)


def _behind(prev_rolls, cur, row_id):
    rolls = _rolls(cur, (1, 2))
    x1 = jnp.where(row_id < 1, prev_rolls[0], rolls[0])
    x2 = jnp.where(row_id < 2, prev_rolls[1], rolls[1])
    return (x2, x1, cur), rolls


def _ahead(cur_rolls, next_rolls, row_id):
    return (jnp.where(row_id < 7, cur_rolls[0], next_rolls[0]), jnp.where(row_id < 6, cur_rolls[1], next_rolls[1]))


def _conv3(taps, w, bias):
    return w[0] * taps[0] + w[1] * taps[1] + w[2] * taps[2] + bias


def _ffn_specs(rows):
    tile = lambda off: pl.BlockSpec((rows, FFN_COLS), lambda j, i: (i, j + off))
    prev = lambda off: pl.BlockSpec((FFN_HALO, FFN_COLS),
                                    lambda j, i: (jnp.maximum(i * (rows // FFN_HALO) - 1, 0), j + off))
    wgt = lambda off: pl.BlockSpec((3, FFN_COLS), lambda j, i: (0, j + off))
    vec = lambda off: pl.BlockSpec((1, FFN_COLS), lambda j, i: (0, j + off))
    return tile, prev, wgt, vec


def _ffn_act(up, w_dw, b_dw):
    seq = up.shape[0]
    tile, prev, wgt, vec = _ffn_specs(FFN_ROWS)

    def body(v_ref, g_ref, vp_ref, gp_ref, wv_ref, wg_ref, bv_ref, bg_ref, act_ref):
        first = pl.program_id(1) == 0
        row_id = lax.broadcasted_iota(jnp.int32, (8, FFN_COLS), 0)
        wv, wg = _sublane_rows(wv_ref, 3), _sublane_rows(wg_ref, 3)
        (bv,), (bg,) = _sublane_rows(bv_ref, 1), _sublane_rows(bg_ref, 1)
        rolls_v = _rolls(jnp.where(first, 0.0, vp_ref[...]), (1, 2))
        rolls_g = _rolls(jnp.where(first, 0.0, gp_ref[...]), (1, 2))
        for row in range(0, FFN_ROWS, 16):
            halves = []
            for r in (row, row + 8):
                taps_v, rolls_v = _behind(rolls_v, v_ref[r:r + 8, :], row_id)
                taps_g, rolls_g = _behind(rolls_g, g_ref[r:r + 8, :], row_id)
                halves.append(_gelu(_conv3(taps_g, wg, bg))[0] * _conv3(taps_v, wv, bv))
            act_ref[row:row + 16, :] = jnp.concatenate(halves, axis=0).astype(BF16)

    return pl.pallas_call(
        body, name="ffn_act", out_shape=jax.ShapeDtypeStruct((seq, D_FF), BF16),
        grid=(FFN_BLOCKS, seq // FFN_ROWS),
        in_specs=[tile(0), tile(FFN_BLOCKS), prev(0), prev(FFN_BLOCKS), wgt(0), wgt(FFN_BLOCKS),
                  vec(0), vec(FFN_BLOCKS)],
        out_specs=tile(0), compiler_params=_cparams(2),
    )(up, up, up, up, w_dw, w_dw, b_dw, b_dw)


def _ffn_act_bwd(up, dact, w_dw, b_dw, comm=None):
    seq = up.shape[0]
    n_tiles = seq // FFN_ROWS
    n_halo = seq // FFN_HALO
    tile, prev, wgt, vec = _ffn_specs(FFN_ROWS)
    nxt = lambda off: pl.BlockSpec(
        (FFN_HALO, FFN_COLS), lambda j, i: (jnp.minimum((i + 1) * (FFN_ROWS // FFN_HALO), n_halo - 1), j + off))
    acc = lambda off: pl.BlockSpec((8, FFN_COLS), lambda j, i: (0, j + off))

    def body(v_ref, g_ref, vp_ref, gp_ref, vn_ref, gn_ref, da_ref, dan_ref, wv_ref, wg_ref, bv_ref, bg_ref,
             dv_out, dg_out, dwv_ref, dwg_ref, dbv_ref, dbg_ref):
        i = pl.program_id(1)
        first, last = i == 0, i == n_tiles - 1

        @pl.when(first)
        def _():
            for r in (dwv_ref, dwg_ref, dbv_ref, dbg_ref):
                r[...] = jnp.zeros_like(r)

        row_id = lax.broadcasted_iota(jnp.int32, (8, FFN_COLS), 0)
        wv, wg = _sublane_rows(wv_ref, 3), _sublane_rows(wg_ref, 3)
        (bv,), (bg,) = _sublane_rows(bv_ref, 1), _sublane_rows(bg_ref, 1)
        zero = jnp.zeros((8, FFN_COLS), F32)
        sums_v, sums_g = [zero] * 4, [zero] * 4
        rolls_v = _rolls(jnp.where(first, 0.0, vp_ref[...]), (1, 2))
        rolls_g = _rolls(jnp.where(first, 0.0, gp_ref[...]), (1, 2))
        behind = None
        done_v, done_g = [], []

        def grads(v_tile, g_tile, dact, rolls_v, rolls_g):
            taps_v, rolls_v = _behind(rolls_v, v_tile, row_id)
            taps_g, rolls_g = _behind(rolls_g, g_tile, row_id)
            val, gate = _conv3(taps_v, wv, bv), _conv3(taps_g, wg, bg)
            gel, t = _gelu(gate)
            return dact * gel, dact * val * _gelu_grad(gate, t), taps_v, taps_g, rolls_v, rolls_g

        def finish(tile, nxt, row):
            for (d, d_rolls), (_, n_rolls), w, done, o_ref in ((tile[0], nxt[0], wv, done_v, dv_out),
                                                               (tile[1], nxt[1], wg, done_g, dg_out)):
                d1, d2 = _ahead(d_rolls, n_rolls, row_id)
                done.append(w[2] * d + w[1] * d1 + w[0] * d2)
                if len(done) == 2:
                    o_ref[row - 16:row, :] = jnp.concatenate(done, axis=0).astype(BF16)
                    done.clear()

        for row in range(0, FFN_ROWS, 16):
            dact16 = da_ref[row:row + 16, :].astype(F32)
            for r, dact in ((row, dact16[0:8, :]), (row + 8, dact16[8:16, :])):
                dval, dgate, taps_v, taps_g, rolls_v, rolls_g = grads(v_ref[r:r + 8, :], g_ref[r:r + 8, :], dact,
                                                                      rolls_v, rolls_g)
                sums_v = [s + dval * x for s, x in zip(sums_v, taps_v)] + [sums_v[3] + dval]
                sums_g = [s + dgate * x for s, x in zip(sums_g, taps_g)] + [sums_g[3] + dgate]
                tile = ((dval, _rolls(dval, (7, 6))), (dgate, _rolls(dgate, (7, 6))))
                if behind is not None:
                    finish(behind, tile, r)
                behind = tile
        dact_next = jnp.where(last, 0.0, dan_ref[...].astype(F32)[0:FFN_HALO, :])
        dval, dgate, *_ = grads(vn_ref[...], gn_ref[...], dact_next, rolls_v, rolls_g)
        finish(behind, ((dval, _rolls(dval, (7, 6))), (dgate, _rolls(dgate, (7, 6)))), FFN_ROWS)
        for sums, dw_ref, db_ref in ((sums_v, dwv_ref, dbv_ref), (sums_g, dwg_ref, dbg_ref)):
            for tap in range(3):
                dw_ref[tap:tap + 1, :] += _colsum(sums[tap])
            db_ref[0:1, :] += _colsum(sums[3])

    half = jax.ShapeDtypeStruct((seq, D_FF), BF16)
    acc_shape = jax.ShapeDtypeStruct((8, D_FF), F32)
    return _host_call(
        lambda ins, outs, scratch: body(*ins, *outs, *scratch), "ffn_act_bwd", grid=(FFN_BLOCKS, n_tiles),
        in_specs=[tile(0), tile(FFN_BLOCKS), prev(0), prev(FFN_BLOCKS), nxt(0), nxt(FFN_BLOCKS),
                  tile(0), pl.BlockSpec((16, FFN_COLS), lambda j, i: (
                      jnp.minimum((i + 1) * (FFN_ROWS // 16), seq // 16 - 1), j)),
                  wgt(0), wgt(FFN_BLOCKS), vec(0), vec(FFN_BLOCKS)],
        out_specs=[tile(0), tile(0), acc(0), acc(0), acc(0), acc(0)],
        out_shape=[half, half, acc_shape, acc_shape, acc_shape, acc_shape],
        scratch_shapes=[], args=[up, up, up, up, up, up, dact, dact, w_dw, w_dw, b_dw, b_dw], comm=comm)


def _cols_to_blocks(full_cols):
    k, n8 = full_cols.shape
    return jnp.transpose(full_cols.reshape(k, N_DEV, n8 // N_DEV), (1, 0, 2))


def _rows_to_blocks(full_rows):
    r8, n = full_rows.shape
    return full_rows.reshape(N_DEV, r8 // N_DEV, n)


def _blocks_to_cols(gathered):
    _, k, n = gathered.shape
    return jnp.transpose(gathered, (1, 0, 2)).reshape(k, N_DEV * n)


def kernel(x, c, w_ada, b_ada, g_pre_mix, g_post_mix, w_in, b_in, rel_bias, w_attn_o, w_dw_conv, b_dw_conv, g_conv_ln, b_conv_ln, w_conv_o, b_conv_o, w_mix_o, g_pre_ffn, g_post_ffn, w_up, w_dw_ffn, b_dw_ffn, w_down, loss_target, m_w_ada, m_b_ada, m_g_pre_mix, m_g_post_mix, m_w_in, m_b_in, m_rel_bias, m_w_attn_o, m_w_dw_conv, m_b_dw_conv, m_g_conv_ln, m_b_conv_ln, m_w_conv_o, m_b_conv_o, m_w_mix_o, m_g_pre_ffn, m_g_post_ffn, m_w_up, m_w_dw_ffn, m_b_dw_ffn, m_w_down, v_w_ada, v_b_ada, v_g_pre_mix, v_g_post_mix, v_w_in, v_b_in, v_rel_bias, v_w_attn_o, v_w_dw_conv, v_b_dw_conv, v_g_conv_ln, v_b_conv_ln, v_w_conv_o, v_b_conv_o, v_w_mix_o, v_g_pre_ffn, v_g_post_ffn, v_w_up, v_w_dw_ffn, v_b_dw_ffn, v_w_down):
    names = ["w_ada", "b_ada", "g_pre_mix", "g_post_mix", "w_in", "b_in", "rel_bias", "w_attn_o", "w_dw_conv",
             "b_dw_conv", "g_conv_ln", "b_conv_ln", "w_conv_o", "b_conv_o", "w_mix_o", "g_pre_ffn", "g_post_ffn",
             "w_up", "w_dw_ffn", "b_dw_ffn", "w_down"]
    weights = dict(zip(names, [w_ada, b_ada, g_pre_mix, g_post_mix, w_in, b_in, rel_bias, w_attn_o, w_dw_conv,
                               b_dw_conv, g_conv_ln, b_conv_ln, w_conv_o, b_conv_o, w_mix_o, g_pre_ffn,
                               g_post_ffn, w_up, w_dw_ffn, b_dw_ffn, w_down]))
    mom_m = dict(zip(names, [m_w_ada, m_b_ada, m_g_pre_mix, m_g_post_mix, m_w_in, m_b_in, m_rel_bias, m_w_attn_o,
                             m_w_dw_conv, m_b_dw_conv, m_g_conv_ln, m_b_conv_ln, m_w_conv_o, m_b_conv_o,
                             m_w_mix_o, m_g_pre_ffn, m_g_post_ffn, m_w_up, m_w_dw_ffn, m_b_dw_ffn, m_w_down]))
    mom_v = dict(zip(names, [v_w_ada, v_b_ada, v_g_pre_mix, v_g_post_mix, v_w_in, v_b_in, v_rel_bias, v_w_attn_o,
                             v_w_dw_conv, v_b_dw_conv, v_g_conv_ln, v_b_conv_ln, v_w_conv_o, v_b_conv_o,
                             v_w_mix_o, v_g_pre_ffn, v_g_post_ffn, v_w_up, v_w_dw_ffn, v_b_dw_ffn, v_w_down]))
    shapes = {n: w.shape for n, w in weights.items()}

    seq = x.shape[1]
    me = 4 * lax.axis_index("x") + 2 * lax.axis_index("y") + lax.axis_index("c")
    x2 = x.reshape(seq, D_MODEL)
    target = loss_target.reshape(seq, D_MODEL)
    sq = lambda a: a.reshape(a.shape[1:])
    bf = lambda a: sq(a).astype(BF16)

    transposed = lambda a: jnp.swapaxes(sq(a), 0, 1)

    c_all, mod_all = _ada_mod(c, sq(w_ada))
    c_all = c_all.reshape(N_DEV, D_MODEL)
    mod = lax.dynamic_index_in_dim(mod_all, me, axis=1, keepdims=False)
    mod6 = (mod.reshape(1, 6 * D_MODEL) + b_ada).reshape(6, D_MODEL)

    h1, (g_in, g_dwc, g_dwf) = _pre_mix(
        x2, mod6, g_pre_mix, comm=_gather_comm([transposed(w_in).astype(BF16), sq(w_dw_conv), sq(w_dw_ffn)]))
    wt_in = g_in.reshape(g_in.shape[0] * g_in.shape[1], D_MODEL)
    wf_dwc = _blocks_to_cols(g_dwc)
    wf_dwf = _blocks_to_cols(g_dwf)
    qkv = _mm(h1, wt_in, "nt", BF16, "in_proj_qkv", bias=b_in, tm=1024, tn=768, cols=(0, 3 * D_ATTN))
    zr, _, (g_ao, g_co, g_mo) = _mm(h1, wt_in, "nt", F32, "in_proj_rest", bias=b_in, tm=1024, tn=3 * D_ATTN,
                                 cols=(3 * D_ATTN, 2 * D_CONV + 2 * D_MODEL),
                                 comm=_gather_comm([bf(w_attn_o), bf(w_conv_o), bf(w_mix_o)]))
    kpad = jnp.pad(qkv[:, D_ATTN:2 * D_ATTN], ((PAD_ROWS, 0), (0, 0)))
    vpad = jnp.pad(qkv[:, 2 * D_ATTN:], ((PAD_ROWS, 0), (0, 0)))
    table = jnp.transpose(_bias_table(sq(rel_bias)), (1, 0, 2))
    ao, (g_up,) = _attn_fwd(qkv, kpad, vpad, table, comm=_gather_comm([transposed(w_up).astype(BF16)]))
    (u1, u3), (g_dn,) = _conv_fwd(zr, wf_dwc, b_dw_conv, g_conv_ln, b_conv_ln, comm=_gather_comm([bf(w_down)]))
    wf_ao = _blocks_to_cols(g_ao)
    wf_co = _blocks_to_cols(g_co)
    wf_mo = g_mo.reshape(D_MODEL, D_MODEL)
    wt_up = g_up.reshape(g_up.shape[0] * g_up.shape[1], D_MODEL)
    wf_dn = g_dn.reshape(D_FF, D_MODEL)
    y, a_br, cb_br = _merge_fwd(ao, u3, zr, wf_ao, wf_co, b_conv_o)
    ymix, (x1, h2), _ = _mm(y, wf_mo, "nn", F32, "mix_o", tm=512, tn=D_MODEL,
                            epilogue=_post_mix_pre_ffn(x2, mod6, g_post_mix, g_pre_ffn, 512))
    up = _mm(h2, wt_up, "nt", F32, "ffn_up", tm=1024, tn=1408)
    act = _ffn_act(up, wf_dwf, b_dw_ffn)
    _, (loss_lanes, dout, dyf, small_f), _ = _mm(act, wf_dn, "nn", F32, "ffn_down", tm=512, tn=D_MODEL,
                                                 epilogue=_final(x1, target, mod6, g_post_ffn, 512))

    dact = _mm(dyf, wf_dn, "nt", BF16, "ffn_down_dx", tm=1024, tn=1408)
    gw_down = _mm(act, dyf, "tn", BF16, "ffn_down_dw", tm=256, tn=1024)
    (dup_v, dup_g, dwv, dwg, dbv, dbg), (parts_down,) = _ffn_act_bwd(
        up, dact, wf_dwf, b_dw_ffn, comm=_scatter_comm([_rows_to_blocks(gw_down)]))
    _, (dx1, dymix, small_m), _ = _mm([dup_v, dup_g], wt_up, "nn", F32, "ffn_up_dx", tm=512, tn=D_MODEL,
                                      epilogue=_mid_bwd(x1, dout, ymix, mod6, g_pre_ffn, g_post_mix, 512))
    blocks_up = _rows_to_blocks(_mm_tn_rows([dup_v, dup_g], h2, "ffn_up_dw"))
    _, (da, dcb, dga, dgb, small_g), _ = _mm(dymix, wf_mo, "nt", F32, "mix_o_dx", tm=512, tn=D_MODEL,
                                             epilogue=_merge_bwd(a_br, cb_br, zr, 512))
    gw_mo = _mm(y, dymix, "tn", BF16, "mix_o_dw")
    dao = _mm(da, wf_ao, "nt", BF16, "attn_o_dx", tm=1024)
    gw_ao = _mm(ao, da, "tn", BF16, "attn_o_dw")
    du3 = _mm(dcb, wf_co, "nt", F32, "conv_o_dx", tm=1024)
    gw_co = _mm(u3, dcb, "tn", BF16, "conv_o_dw")
    (dq, dkt, dvt, dbias, small_a), (parts_up,) = _attn_bwd(
        qkv, kpad, vpad, table, dao, comm=_scatter_comm([blocks_up]))
    g_rel = _bias_grad(jnp.transpose(dbias, (1, 0, 2)))
    (dglu_a, dglu_b, dw_conv, small_c), (parts_mo, parts_ao, parts_co) = _conv_bwd(
        zr, u1, du3, wf_dwc, g_conv_ln, b_conv_ln,
        comm=_scatter_comm([_rows_to_blocks(gw_mo), _cols_to_blocks(gw_ao), _cols_to_blocks(gw_co)]))
    dz = _assemble_dz(dq, dkt, dvt, dglu_a, dglu_b, dga, dgb)
    blocks_in = _rows_to_blocks(_mm(dz, h1, "tn", BF16, "in_proj_dw", tm=512, tn=D_MODEL))
    _, (grad_x, small_x), (parts_in, _, _) = _mm(dz, wt_in, "nn", F32, "in_proj_dx", tm=512, tn=D_MODEL,
                                                 comm=_pair_scatter_comm(blocks_in),
                                                 epilogue=_pre_mix_bwd(x2, dx1, mod6, g_pre_mix, 512))

    packed = _pack_grads(small_x, small_m, small_f, small_g, small_a, small_c, dbv, dbg, dwv, dwg, dw_conv)
    gathered, gathered_rel, gathered_loss = _run_comm(_gather_comm([packed, g_rel, loss_lanes]), "gather_small")
    gathered = gathered.reshape(N_DEV, PACKED_TOTAL)
    updates, g_dwc_full, g_dwf_full, loss_all = _small_adamw(gathered, gathered_rel, gathered_loss, weights, mom_m,
                                                             mom_v)
    loss = loss_all[0, 0]

    grads, deltas, new_m, new_v = {}, {}, {}, {}

    def record(name, update, is_transposed=False):
        for dst, val in zip((grads, deltas, new_m, new_v), update):
            dst[name] = (jnp.swapaxes(val, 0, 1) if is_transposed else val).reshape(shapes[name])

    for name, update in updates.items():
        record(name, update)

    def local_update(name, grad):
        record(name, _adamw(sq(weights[name]), sq(mom_m[name]), sq(mom_v[name]), "adamw_" + name, g=grad))

    conv_cols, ffn_cols, ada_cols = D_CONV // N_DEV, 2 * D_FF // N_DEV, 6 * D_MODEL // N_DEV
    local_update("w_dw_conv", lax.dynamic_slice(g_dwc_full, (0, me * conv_cols), (CONV_K, conv_cols)))
    local_update("w_dw_ffn", lax.dynamic_slice(g_dwf_full, (0, me * ffn_cols), (3, ffn_cols)))
    local_update("w_ada", _ada_grad(c_all, lax.dynamic_slice(gathered, (0, me * ada_cols), (N_DEV, ada_cols))))

    for name, part in (("w_attn_o", parts_ao), ("w_conv_o", parts_co), ("w_mix_o", parts_mo), ("w_down", parts_down)):
        record(name, _adamw(sq(weights[name]), sq(mom_m[name]), sq(mom_v[name]), "adamw_" + name, parts=part))
    for name, part in (("w_in", parts_in), ("w_up", parts_up)):
        record(name, _adamw(transposed(weights[name]), transposed(mom_m[name]), transposed(mom_v[name]),
                            "adamw_" + name, parts=part), is_transposed=True)

    return (loss, grad_x.reshape(x.shape), *[grads[n] for n in names], *[deltas[n] for n in names],
            *[new_m[n] for n in names], *[new_v[n] for n in names])
```

```python
import functools
import math

import jax
import jax.numpy as jnp
from jax import lax
from jax.experimental import pallas as pl
from jax.experimental.pallas import tpu as pltpu

F32 = jnp.float32
BF16 = jnp.bfloat16
HIGHEST = lax.Precision.HIGHEST

D_MODEL = 1024
CHUNK = 64
LEFT_CHUNKS = 8
BAND = (LEFT_CHUNKS + 1) * CHUNK
PAD_ROWS = LEFT_CHUNKS * CHUNK
GROUP = 4
GROUP_Q = GROUP * CHUNK
GROUP_K = GROUP_Q + PAD_ROWS
SOFTMAX_ROWS = 16
TOEPLITZ = 640
N_HEADS = 8
HEAD_DIM = 64
D_ATTN = 512
D_CONV = 512
CONV_K = 31
CONV_HALO = 32
MAX_REL = 128
N_REL = 2 * MAX_REL + 1
D_FF = 2816
FFN_HALO = 8
FFN_COLS = 256
EPS = 1e-6
NEG_INF = -1e30
N_DEV = 8

ADAM_LR = 0.001
ADAM_B1 = 0.9
ADAM_B2 = 0.999
ADAM_EPS = 1e-08
ADAM_WD = 0.01
ADAM_STEP = 10

VMEM_LIMIT_BYTES = 56 * 1024 * 1024
ADAMW_BLOCK_BYTES = 768 * 1024

MESH = pl.DeviceIdType.MESH
ANY = pl.BlockSpec(memory_space=pl.ANY)

SH_M, SC_M, GT_M, SH_F, SC_F, GT_F = range(6)

SMALL = (("b_ada", 6144), ("g_pre_mix", 1024), ("g_post_mix", 1024), ("b_in", 4608), ("b_dw_conv", 512),
         ("g_conv_ln", 512), ("b_conv_ln", 512), ("b_conv_o", 1024), ("g_pre_ffn", 1024), ("g_post_ffn", 1024),
         ("b_dw_ffn", 5632))
PACKED_TOTAL = sum(n for _, n in SMALL) + CONV_K * D_CONV + 3 * 2 * D_FF


def _cparams(n_axes):
    return pltpu.CompilerParams(vmem_limit_bytes=VMEM_LIMIT_BYTES,
                                dimension_semantics=("arbitrary",) * n_axes)


def _sig(v):
    return 1.0 / (1.0 + jnp.exp(-v))


def _pick(n, target):
    if n <= target:
        return n
    t = target - target % 128
    while n % t:
        t -= 128
    return t


def _tile(rows, cols, col=0):
    return pl.BlockSpec((rows, cols), lambda i: (i, col))


def _full(shape):
    zeros = (0,) * len(shape)
    return pl.BlockSpec(shape, lambda i: zeros)


def _prev(halo, cols, rows, col=0):
    return pl.BlockSpec((halo, cols), lambda i: (jnp.maximum(i * (rows // halo) - 1, 0), col))


def _next(halo, cols, rows, n_blocks, col=0):
    return pl.BlockSpec((halo, cols), lambda i: (jnp.minimum((i + 1) * (rows // halo), n_blocks - 1), col))


class _Comm:
    def __init__(self, inputs, out_shapes, sems, start, finish, relay=None, early=None):
        self.inputs, self.out_shapes, self.sems, self.start, self.finish = inputs, out_shapes, sems, start, finish
        self.relay, self.early = relay, early


def _host_call(body, name, grid, in_specs, out_specs, out_shape, scratch_shapes, args, comm=None):
    n_in, n_out, n_scr = len(args), len(out_shape), len(scratch_shapes)
    c_in = list(comm.inputs) if comm else []
    c_out = list(comm.out_shapes) if comm else []
    c_sem = list(comm.sems) if comm else []

    def full(*refs):
        bounds = [0, n_in, len(c_in), n_out, len(c_out), n_scr, len(c_sem)]
        cuts = [sum(bounds[:i + 1]) for i in range(len(bounds))]
        ins, cins, outs, couts, scr, csems = (refs[lo:hi] for lo, hi in zip(cuts[:-1], cuts[1:]))
        if comm:
            first = functools.reduce(jnp.logical_and, [pl.program_id(ax) == 0 for ax in range(len(grid))])
            pl.when(first)(lambda: comm.start(cins, couts, csems))
            if comm.early is not None:
                strides = [math.prod(grid[ax + 1:]) for ax in range(len(grid))]
                step = sum(pl.program_id(ax) * strides[ax] for ax in range(len(grid)))
                pl.when(step == 1)(lambda: comm.early(cins, couts, csems))
            last = functools.reduce(jnp.logical_and, [pl.program_id(ax) == grid[ax] - 1 for ax in range(len(grid))])
            if comm.relay is not None:
                pl.when(last)(lambda: comm.relay(cins, couts, csems))
        body(ins, outs, scr)
        if comm:
            pl.when(last)(lambda: comm.finish(cins, couts, csems))

    res = pl.pallas_call(
        full, name=name, grid=grid, in_specs=list(in_specs) + [ANY] * len(c_in),
        out_specs=list(out_specs) + [ANY] * len(c_out), out_shape=list(out_shape) + c_out,
        scratch_shapes=list(scratch_shapes) + c_sem, compiler_params=_cparams(len(grid)),
    )(*args, *c_in)
    return list(res[:n_out]), list(res[n_out:])


def _run_comm(comm, name):
    n_in, n_out = len(comm.inputs), len(comm.out_shapes)

    def body(*refs):
        ins, outs, sems = refs[:n_in], refs[n_in:n_in + n_out], refs[n_in + n_out:]
        comm.start(ins, outs, sems)
        if comm.relay is not None:
            comm.relay(ins, outs, sems)
        comm.finish(ins, outs, sems)

    return pl.pallas_call(
        body, name=name, out_shape=list(comm.out_shapes), in_specs=[ANY] * n_in, out_specs=[ANY] * n_out,
        scratch_shapes=list(comm.sems),
    )(*comm.inputs)


def _place():
    return lax.axis_index("x"), lax.axis_index("y"), lax.axis_index("c")


def _gather_comm(arrs):
    n = len(arrs)

    def plan(ins, outs, sems):
        send_sems, recv_sems, local_sems = sems
        x, y, c = _place()
        me, sibling = (x, y, c), (x, y, 1 - c)
        chips = [(1 - x, y), (x, 1 - y), (1 - x, 1 - y)]

        def block(k, p):
            return outs[k].at[4 * p[0] + 2 * p[1] + p[2]]

        def copy(k, s, blk, to, src=None):
            return pltpu.make_async_remote_copy(
                src_ref=block(k, blk) if src is None else src, dst_ref=block(k, blk),
                send_sem=send_sems.at[7 * k + s], recv_sem=recv_sems.at[7 * k + s],
                device_id=to, device_id_type=MESH)

        mine = [pltpu.make_async_copy(ins[k], block(k, me), local_sems.at[k]) for k in range(n)]
        first = []
        for k in range(n):
            first.append(copy(k, 0, me, sibling, src=ins[k]))
            for j, chip in enumerate(chips):
                first.append(copy(k, 1 + j, me, (*chip, c), src=ins[k]))
        return me, sibling, chips, c, copy, mine, first

    def start(ins, outs, sems):
        *_, mine, first = plan(ins, outs, sems)
        for cp in mine + first:
            cp.start()

    def relay(ins, outs, sems):
        me, sibling, chips, c, copy, _, _ = plan(ins, outs, sems)
        for j, chip in enumerate(chips):
            for k in range(n):
                copy(k, 1 + j, (*chip, c), me).wait_recv()
                copy(k, 4 + j, (*chip, c), sibling).start()

    def finish(ins, outs, sems):
        me, sibling, chips, c, copy, mine, first = plan(ins, outs, sems)
        passed = [copy(k, 4 + j, (*chip, c), sibling) for j, chip in enumerate(chips) for k in range(n)]
        for k in range(n):
            copy(k, 0, sibling, me).wait_recv()
        for j, chip in enumerate(chips):
            for k in range(n):
                copy(k, 4 + j, (*chip, 1 - c), me).wait_recv()
        for cp in first + passed:
            cp.wait_send()
        for cp in mine:
            cp.wait()

    return _Comm(list(arrs), [jax.ShapeDtypeStruct((N_DEV,) + a.shape, a.dtype) for a in arrs],
                 [pltpu.SemaphoreType.DMA((7 * n,)), pltpu.SemaphoreType.DMA((7 * n,)),
                  pltpu.SemaphoreType.DMA((n,))], start, finish, relay)


def _scatter_comm(blocks):
    n = len(blocks)

    def plan(ins, outs, sems, arrivals):
        send_sems, recv_sems, local_sems = sems
        x, y, c = _place()
        me = 4 * x + 2 * y + c
        local = [pltpu.make_async_copy(ins[k].at[me], outs[k].at[me], local_sems.at[k]) for k in range(n)]
        sends, recvs = [], []
        for k in range(n):
            for mask in range(1, N_DEV):
                px = 1 - x if mask & 4 else x
                py = 1 - y if mask & 2 else y
                pc = 1 - c if mask & 1 else c
                peer = 4 * px + 2 * py + pc
                sem = 7 * k + mask - 1
                both = dict(send_sem=send_sems.at[sem], recv_sem=recv_sems.at[sem], device_id=(px, py, pc),
                            device_id_type=MESH)
                sends.append(pltpu.make_async_remote_copy(src_ref=ins[k].at[peer], dst_ref=outs[k].at[me], **both))
                if arrivals:
                    recvs.append(pltpu.make_async_remote_copy(src_ref=ins[k].at[me], dst_ref=outs[k].at[peer],
                                                              **both))
        return local, sends, recvs

    def start(ins, outs, sems):
        local, sends, _ = plan(ins, outs, sems, arrivals=False)
        for cp in local + sends:
            cp.start()

    def finish(ins, outs, sems):
        local, sends, recvs = plan(ins, outs, sems, arrivals=True)
        for cp in recvs:
            cp.wait_recv()
        for cp in sends:
            cp.wait_send()
        for cp in local:
            cp.wait()

    return _Comm(list(blocks), [jax.ShapeDtypeStruct(b.shape, b.dtype) for b in blocks],
                 [pltpu.SemaphoreType.DMA((7 * n,)), pltpu.SemaphoreType.DMA((7 * n,)),
                  pltpu.SemaphoreType.DMA((n,))], start, finish)


def _pair_scatter_comm(block):
    _, r, c = block.shape
    quarter = jax.ShapeDtypeStruct((4, r, c), block.dtype)

    def plan(ins, outs, sems):
        parts, got, pair = outs
        d2d_send, d2d_recv, ici_send, ici_recv, local, buf_a, buf_b = sems
        x, y, cc = _place()
        mine = 2 * x + y
        chips = [(1 - x, y), (x, 1 - y), (1 - x, 1 - y)]
        to_sibling = [pltpu.make_async_remote_copy(
            src_ref=ins[0].at[2 * q + 1 - cc], dst_ref=got.at[q], send_sem=d2d_send.at[q], recv_sem=d2d_recv.at[q],
            device_id=(x, y, 1 - cc), device_id_type=MESH) for q in range(4)]
        to_chips = [pltpu.make_async_remote_copy(
            src_ref=pair.at[2 * px + py], dst_ref=parts.at[mine], send_sem=ici_send.at[j], recv_sem=ici_recv.at[j],
            device_id=(px, py, cc), device_id_type=MESH) for j, (px, py) in enumerate(chips)]
        from_chips = [pltpu.make_async_remote_copy(
            src_ref=pair.at[mine], dst_ref=parts.at[2 * px + py], send_sem=ici_send.at[j], recv_sem=ici_recv.at[j],
            device_id=(px, py, cc), device_id_type=MESH) for j, (px, py) in enumerate(chips)]
        own = pltpu.make_async_copy(pair.at[mine], parts.at[mine], local.at[2])
        return cc, got, pair, local, buf_a, buf_b, to_sibling, to_chips, from_chips, own

    def start(ins, outs, sems):
        for cp in plan(ins, outs, sems)[6]:
            cp.start()

    def early(ins, outs, sems):
        cc, got, pair, local, buf_a, buf_b, to_sibling, to_chips, _, own = plan(ins, outs, sems)
        for q in range(4):
            to_sibling[q].wait_recv()
            loads = [pltpu.make_async_copy(ins[0].at[2 * q + cc], buf_a, local.at[0]),
                     pltpu.make_async_copy(got.at[q], buf_b, local.at[1])]
            for cp in loads:
                cp.start()
            for cp in loads:
                cp.wait()
            buf_a[...] = (buf_a[...].astype(F32) + buf_b[...].astype(F32)).astype(block.dtype)
            store = pltpu.make_async_copy(buf_a, pair.at[q], local.at[0])
            store.start()
            store.wait()
        for cp in to_chips + [own]:
            cp.start()

    def finish(ins, outs, sems):
        *_, to_sibling, to_chips, from_chips, own = plan(ins, outs, sems)
        for cp in from_chips:
            cp.wait_recv()
        for cp in to_chips + to_sibling:
            cp.wait_send()
        own.wait()

    return _Comm([block], [quarter, quarter, quarter],
                 [pltpu.SemaphoreType.DMA((4,)), pltpu.SemaphoreType.DMA((4,)), pltpu.SemaphoreType.DMA((3,)),
                  pltpu.SemaphoreType.DMA((3,)), pltpu.SemaphoreType.DMA((3,)), pltpu.VMEM((r, c), block.dtype),
                  pltpu.VMEM((r, c), block.dtype)], start, finish, early=early)


_DIMS = {"nn": (((1,), (0,)), ((), ())), "nt": (((1,), (1,)), ((), ())), "tn": (((0,), (0,)), ((), ()))}


class _Epilogue:
    def __init__(self, args, in_specs, out_shapes, out_specs, fn, keep_product):
        self.args, self.in_specs, self.out_shapes, self.out_specs = args, in_specs, out_shapes, out_specs
        self.fn, self.keep_product = fn, keep_product


def _row_tile(rows, cols):
    return pl.BlockSpec((rows, cols), lambda i, j: (i, 0))


def _whole(shape):
    zeros = (0,) * len(shape)
    return pl.BlockSpec(shape, lambda i, j: zeros)


def _mm(a, b, mode, out_dtype, name, bias=None, tm=512, tn=512, comm=None, cols=None, epilogue=None):
    pieces = a if isinstance(a, (list, tuple)) else [a]
    assert all(p.dtype == BF16 for p in pieces) and b.dtype == BF16
    a = pieces[0]
    if mode == "tn":
        k_dim, m_dim = a.shape
    else:
        m_dim, k_dim = a.shape
    n_dim = b.shape[0] if mode == "nt" else b.shape[1]
    col0 = 0
    if cols is not None:
        assert mode != "tn" and cols[0] % tn == 0 and cols[1] % tn == 0
        col0, n_dim = cols[0] // tn, cols[1]
    tm, tn = _pick(m_dim, tm), _pick(n_dim, tn)
    assert mode != "tn" or len(pieces) == 1
    a_specs = [pl.BlockSpec((k_dim, tm), lambda i, j: (0, i)) if mode == "tn"
               else pl.BlockSpec((tm, k_dim), lambda i, j: (i, 0))] * len(pieces)
    once = dict(pipeline_mode=pl.Buffered(1)) if tn == n_dim else {}
    if mode == "nt":
        b_specs = [pl.BlockSpec((tn, k_dim), lambda i, j, p=p: (j + col0, p), **once) for p in range(len(pieces))]
    else:
        b_specs = [pl.BlockSpec((k_dim, tn), lambda i, j, p=p: (p, j + col0), **once) for p in range(len(pieces))]
    in_specs = a_specs + b_specs
    args = list(pieces) + [b] * len(pieces)
    if bias is not None:
        in_specs.append(pl.BlockSpec((1, tn), lambda i, j: (0, j + col0)))
        args.append(bias)
    dims = _DIMS[mode]
    n_pieces = len(pieces)
    n_own = len(args)
    keep = epilogue is None or epilogue.keep_product
    out_specs = [pl.BlockSpec((tm, tn), lambda i, j: (i, j))] if keep else []
    out_shape = [jax.ShapeDtypeStruct((m_dim, n_dim), out_dtype)] if keep else []
    if epilogue is not None:
        assert tn == n_dim
        in_specs, args = in_specs + list(epilogue.in_specs), args + list(epilogue.args)
        out_specs, out_shape = out_specs + list(epilogue.out_specs), out_shape + list(epilogue.out_shapes)

    def body(ins, outs, scratch):
        total = lax.dot_general(ins[0][...], ins[n_pieces][...], dims, preferred_element_type=F32)
        for p in range(1, n_pieces):
            total = total + lax.dot_general(ins[p][...], ins[n_pieces + p][...], dims, preferred_element_type=F32)
        if bias is not None:
            total = total + ins[2 * n_pieces][...]
        if keep:
            outs[0][...] = total.astype(out_dtype)
        if epilogue is not None:
            epilogue.fn(total, pl.program_id(0) == 0, ins[n_own:], outs[1:] if keep else outs)

    outs, extra = _host_call(body, name, grid=(m_dim // tm, n_dim // tn), in_specs=in_specs, out_specs=out_specs,
                             out_shape=out_shape, scratch_shapes=[], args=args, comm=comm)
    product = outs[0] if keep else None
    if comm is None and epilogue is None:
        return product
    return product, outs[1:] if keep else outs, extra


def _mm_tn_rows(pieces, b, name, tm=256):
    k_dim, n_dim = b.shape
    counts = [p.shape[1] // tm for p in pieces]
    assert all(p.shape[1] % tm == 0 for p in pieces)
    firsts = [sum(counts[:q]) for q in range(len(pieces))]

    def a_spec(first, count):
        return pl.BlockSpec((k_dim, tm), lambda i: (0, jnp.clip(i - first, 0, count - 1)))

    def body(ins, outs, scratch):
        i = pl.program_id(0)
        for a_ref, first, count in zip(ins[:-1], firsts, counts):
            @pl.when(jnp.logical_and(i >= first, i < first + count))
            def _(a_ref=a_ref):
                outs[0][...] = lax.dot_general(a_ref[...], ins[-1][...], _DIMS["tn"],
                                               preferred_element_type=F32).astype(BF16)

    (out,), _ = _host_call(
        body, name, grid=(sum(counts),),
        in_specs=[a_spec(f, c) for f, c in zip(firsts, counts)] + [_full((k_dim, n_dim))],
        out_specs=[_tile(tm, n_dim)], out_shape=[jax.ShapeDtypeStruct((sum(counts) * tm, n_dim), BF16)],
        scratch_shapes=[], args=list(pieces) + [b])
    return out


def _adam_math(w, g, m, v):
    m = ADAM_B1 * m + (1.0 - ADAM_B1) * g
    v = ADAM_B2 * v + (1.0 - ADAM_B2) * (g * g)
    m_hat = m / (1.0 - ADAM_B1 ** ADAM_STEP)
    v_hat = v / (1.0 - ADAM_B2 ** ADAM_STEP)
    delta = -ADAM_LR * (m_hat / (jnp.sqrt(v_hat) + ADAM_EPS) + ADAM_WD * w)
    return delta, m, v


def _adamw(w, m, v, name, g=None, parts=None):
    rows, cols = w.shape
    tr = rows
    if rows * cols * 4 > ADAMW_BLOCK_BYTES:
        tr = max(t for t in range(16, rows, 16) if rows % t == 0 and t * cols * 4 <= ADAMW_BLOCK_BYTES)

    def body(w_ref, m_ref, v_ref, g_ref, go_ref, d_ref, mo_ref, vo_ref):
        if parts is None:
            grad = g_ref[...]
        else:
            grad = g_ref[0].astype(F32)
            for d in range(1, parts.shape[0]):
                grad = grad + g_ref[d].astype(F32)
        delta, m_new, v_new = _adam_math(w_ref[...], grad, m_ref[...], v_ref[...])
        go_ref[...] = grad
        d_ref[...] = delta
        mo_ref[...] = m_new
        vo_ref[...] = v_new

    spec = _tile(tr, cols)
    g_spec = spec if parts is None else pl.BlockSpec((parts.shape[0], tr, cols), lambda i: (0, i, 0))
    shape = jax.ShapeDtypeStruct((rows, cols), F32)
    return pl.pallas_call(
        body, name=name, out_shape=[shape] * 4, grid=(rows // tr,),
        in_specs=[spec, spec, spec, g_spec], out_specs=[spec] * 4, compiler_params=_cparams(1),
    )(w, m, v, g if parts is None else parts)


def _pack_grads(small_x, small_m, small_f, small_g, small_a, small_c, dbv, dbg, dwv, dwg, dw_conv):
    pieces = [
        (small_x, 2, D_MODEL), (small_x, 1, D_MODEL), (small_m, 4, D_MODEL), (small_m, 2, D_MODEL),
        (small_m, 1, D_MODEL), (small_f, 1, D_MODEL),
        (small_x, 0, D_MODEL), (small_m, 3, D_MODEL),
        (small_a, 0, D_ATTN), (small_a, 1, D_ATTN), (small_a, 2, D_ATTN), (small_c, 3, D_CONV),
        (small_c, 4, D_CONV), (small_g, 0, D_MODEL), (small_g, 1, D_MODEL),
        (small_c, 0, D_CONV), (small_c, 1, D_CONV), (small_c, 2, D_CONV),
        (small_g, 2, D_MODEL), (small_m, 0, D_MODEL), (small_f, 0, D_MODEL),
        (dbv, 0, D_FF), (dbg, 0, D_FF),
    ]
    pieces += [(dw_conv, j, D_CONV) for j in range(CONV_K)]
    pieces += [(src, tap, D_FF) for tap in range(3) for src in (dwv, dwg)]
    sources = [small_x, small_m, small_f, small_g, small_a, small_c, dbv, dbg, dwv, dwg, dw_conv]
    assert sum(width for _, _, width in pieces) == PACKED_TOTAL

    def body(*refs):
        o_ref = refs[-1]
        ref_of = {id(src): ref for src, ref in zip(sources, refs)}
        off = 0
        for src, row, width in pieces:
            o_ref[:, off:off + width] = ref_of[id(src)][row:row + 1, :]
            off += width

    return pl.pallas_call(body, name="pack_grads", out_shape=jax.ShapeDtypeStruct((1, PACKED_TOTAL), F32))(*sources)


def _small_adamw(gathered, gathered_rel, gathered_loss, weights, mom_m, mom_v):
    vec_names = [name for name, _ in SMALL]
    states = []
    for name in vec_names + ["rel_bias"]:
        states += [weights[name], mom_m[name], mom_v[name]]
    states = [a.reshape(a.shape[1:]) if a.ndim == 3 else a for a in states]
    n_state = len(states)

    def body(*refs):
        g_ref, rel_ref, loss_ref = refs[0], refs[1], refs[2]
        state_refs, out_refs = refs[3:3 + n_state], refs[3 + n_state:]
        total = g_ref[0:1, :]
        rel = rel_ref[0]
        loss = loss_ref[0]
        for d in range(1, N_DEV):
            total = total + g_ref[d:d + 1, :]
            rel = rel + rel_ref[d]
            loss = loss + loss_ref[d]
        off = 0
        for n, (name, width) in enumerate(SMALL):
            grad = total[:, off:off + width]
            w_ref, m_ref, v_ref = state_refs[3 * n:3 * n + 3]
            for ref, val in zip(out_refs[4 * n:4 * n + 4], (grad,) + _adam_math(w_ref[...], grad, m_ref[...], v_ref[...])):
                ref[...] = val
            off += width
        n = len(SMALL)
        w_ref, m_ref, v_ref = state_refs[3 * n:3 * n + 3]
        for ref, val in zip(out_refs[4 * n:4 * n + 4], (rel,) + _adam_math(w_ref[...], rel, m_ref[...], v_ref[...])):
            ref[...] = val
        dwc_ref, dwf_ref, loss_out = out_refs[4 * n + 4:]
        loss_out[...] = 0.5 * loss
        dwc_ref[...] = jnp.zeros_like(dwc_ref)
        dwf_ref[...] = jnp.zeros_like(dwf_ref)
        for j in range(CONV_K):
            dwc_ref[j:j + 1, :] = total[:, off:off + D_CONV]
            off += D_CONV
        for tap in range(3):
            dwf_ref[tap:tap + 1, :] = total[:, off:off + 2 * D_FF]
            off += 2 * D_FF

    out_shape = []
    for k in range(n_state // 3):
        out_shape += [jax.ShapeDtypeStruct(states[3 * k].shape, F32)] * 4
    out_shape += [jax.ShapeDtypeStruct((CONV_HALO, D_CONV), F32), jax.ShapeDtypeStruct((8, 2 * D_FF), F32),
                  jax.ShapeDtypeStruct((1, 128), F32)]
    res = pl.pallas_call(
        body, name="small_adamw", out_shape=out_shape,
        compiler_params=pltpu.CompilerParams(vmem_limit_bytes=VMEM_LIMIT_BYTES),
    )(gathered, gathered_rel, gathered_loss, *states)
    updates = {name: tuple(res[4 * n:4 * n + 4]) for n, name in enumerate(vec_names + ["rel_bias"])}
    return updates, res[-3], res[-2], res[-1]


def _ada_grad(c_all, dmod_shard):
    def body(c_ref, d_ref, o_ref):
        o_ref[...] = lax.dot_general(c_ref[...], d_ref[...], _DIMS["tn"], precision=HIGHEST,
                                     preferred_element_type=F32)

    return pl.pallas_call(
        body, name="ada_grad", out_shape=jax.ShapeDtypeStruct((D_MODEL, dmod_shard.shape[1]), F32),
        compiler_params=pltpu.CompilerParams(vmem_limit_bytes=VMEM_LIMIT_BYTES),
    )(c_all, dmod_shard)


ROWS = 256


def _rms(v):
    r = lax.rsqrt(jnp.mean(v * v, axis=-1, keepdims=True) + EPS)
    return v * r, r


def _rms_bwd(dxn, xn, r):
    return r * (dxn - xn * jnp.mean(dxn * xn, axis=-1, keepdims=True))


def _colsum(v):
    return jnp.sum(v, axis=0, keepdims=True)


def _pre_mix(x, c, w_ada, b_ada, g1, comm=None):
    seq = x.shape[0]
    cols = w_ada.shape[1]

    def body(ins, outs, scratch):
        x_ref, c_ref, w_ref, bias_ref, g_ref = ins
        h_ref, call_out, mod_out = outs
        call_ref, prod_ref, mod_ref, send_sems, recv_sems = scratch

        @pl.when(pl.program_id(0) == 0)
        def _():
            px_, py_, pc_ = _place()
            me = 4 * px_ + 2 * py_ + pc_

            def exchange(ref, phase):
                sends, arrivals = [], []
                for mask in range(1, N_DEV):
                    px = 1 - px_ if mask & 4 else px_
                    py = 1 - py_ if mask & 2 else py_
                    pc = 1 - pc_ if mask & 1 else pc_
                    both = dict(send_sem=send_sems.at[7 * phase + mask - 1],
                                recv_sem=recv_sems.at[7 * phase + mask - 1], device_id=(px, py, pc),
                                device_id_type=MESH)
                    sends.append(pltpu.make_async_remote_copy(src_ref=ref.at[me], dst_ref=ref.at[me], **both))
                    arrivals.append(pltpu.make_async_remote_copy(
                        src_ref=ref.at[me], dst_ref=ref.at[4 * px + 2 * py + pc], **both))
                for cp in sends:
                    cp.start()
                for cp in arrivals:
                    cp.wait_recv()
                for cp in sends:
                    cp.wait_send()

            v = c_ref[...]
            call_ref[me] = v * _sig(v)
            exchange(call_ref, 0)
            c_all = jnp.concatenate([call_ref[d] for d in range(N_DEV)], axis=0)
            call_out[...] = c_all
            prod_ref[me] = jnp.dot(c_all, w_ref[...], precision=HIGHEST, preferred_element_type=F32)
            exchange(prod_ref, 1)
            for d in range(N_DEV):
                piece = prod_ref[d, pl.ds(me, 1), :]
                lo = d * cols
                done = 0
                while done < cols:
                    row, col = divmod(lo + done, D_MODEL)
                    n = min(cols - done, D_MODEL - col)
                    mod_ref[row:row + 1, col:col + n] = piece[:, done:done + n] + bias_ref[:, lo + done:lo + done + n]
                    done += n
            mod_out[...] = mod_ref[...]

        xn, _ = _rms(x_ref[...])
        y = xn * g_ref[...]
        h_ref[...] = (y * (1.0 + mod_ref[SC_M:SC_M + 1, :]) + mod_ref[SH_M:SH_M + 1, :]).astype(BF16)

    (h, c_all, mod6), extra = _host_call(
        body, "pre_mix", grid=(seq // ROWS,),
        in_specs=[_tile(ROWS, D_MODEL), _full((1, D_MODEL)), _full(w_ada.shape), _full(b_ada.shape),
                  _full((1, D_MODEL))],
        out_specs=[_tile(ROWS, D_MODEL), _full((N_DEV, D_MODEL)), _full((6, D_MODEL))],
        out_shape=[jax.ShapeDtypeStruct((seq, D_MODEL), BF16), jax.ShapeDtypeStruct((N_DEV, D_MODEL), F32),
                   jax.ShapeDtypeStruct((6, D_MODEL), F32)],
        scratch_shapes=[pltpu.VMEM((N_DEV, 1, D_MODEL), F32), pltpu.VMEM((N_DEV, N_DEV, cols), F32),
                        pltpu.VMEM((6, D_MODEL), F32), pltpu.SemaphoreType.DMA((14,)), pltpu.SemaphoreType.DMA((14,))],
        args=[x, c, w_ada, b_ada, g1], comm=comm)
    return h, c_all, mod6, extra


def _post_mix_pre_ffn(x, mod6, g2, g3, rows):
    seq = x.shape[0]

    def fn(y, first, ins, outs):
        x_ref, mod_ref, g2_ref, g3_ref = ins
        x1_ref, h_ref = outs
        yn, _ = _rms(y)
        x1 = x_ref[...] + mod_ref[GT_M:GT_M + 1, :] * (yn * g2_ref[...])
        x1_ref[...] = x1
        xn, _ = _rms(x1)
        y3 = xn * g3_ref[...]
        h_ref[...] = (y3 * (1.0 + mod_ref[SC_F:SC_F + 1, :]) + mod_ref[SH_F:SH_F + 1, :]).astype(BF16)

    return _Epilogue(
        [x, mod6, g2, g3], [_row_tile(rows, D_MODEL), _whole((6, D_MODEL)), _whole((1, D_MODEL)), _whole((1, D_MODEL))],
        [jax.ShapeDtypeStruct((seq, D_MODEL), F32), jax.ShapeDtypeStruct((seq, D_MODEL), BF16)],
        [_row_tile(rows, D_MODEL), _row_tile(rows, D_MODEL)], fn, keep_product=True)


def _final(x1, target, mod6, g4, rows):
    seq = x1.shape[0]

    def fn(y, first, ins, outs):
        x1_ref, t_ref, mod_ref, g_ref = ins
        loss_ref, dout_ref, dyf_ref, small_ref = outs

        @pl.when(first)
        def _():
            loss_ref[...] = jnp.zeros_like(loss_ref)
            small_ref[...] = jnp.zeros_like(small_ref)

        gt = mod_ref[GT_F:GT_F + 1, :]
        g4v = g_ref[...]
        yn, r = _rms(y)
        out = x1_ref[...] + gt * (yn * g4v)
        err = out - t_ref[...]
        loss_ref[...] += jnp.sum(jnp.mean(err * err, axis=-1, keepdims=True))
        dout = err * (1.0 / D_MODEL)
        dout_ref[...] = dout
        small_ref[0:1, :] += _colsum(dout * gt * yn)
        small_ref[1:2, :] += _colsum(dout * (yn * g4v))
        dyf_ref[...] = _rms_bwd(dout * gt * g4v, yn, r).astype(BF16)

    return _Epilogue(
        [x1, target, mod6, g4],
        [_row_tile(rows, D_MODEL), _row_tile(rows, D_MODEL), _whole((6, D_MODEL)), _whole((1, D_MODEL))],
        [jax.ShapeDtypeStruct((1, 128), F32), jax.ShapeDtypeStruct((seq, D_MODEL), F32),
         jax.ShapeDtypeStruct((seq, D_MODEL), BF16), jax.ShapeDtypeStruct((8, D_MODEL), F32)],
        [_whole((1, 128)), _row_tile(rows, D_MODEL), _row_tile(rows, D_MODEL), _whole((8, D_MODEL))],
        fn, keep_product=False)


def _mid_bwd(x1, dout, ymix, mod6, g3, g2, rows):
    seq = x1.shape[0]

    def fn(dh, first, ins, outs):
        x1_ref, dout_ref, y_ref, mod_ref, g3_ref, g2_ref = ins
        dx1_ref, dy_ref, small_ref = outs

        @pl.when(first)
        def _():
            small_ref[...] = jnp.zeros_like(small_ref)

        g3v, g2v = g3_ref[...], g2_ref[...]
        xn, r3 = _rms(x1_ref[...])
        y3 = xn * g3v
        dy3 = dh * (1.0 + mod_ref[SC_F:SC_F + 1, :])
        small_ref[0:1, :] += _colsum(dy3 * xn)
        small_ref[1:2, :] += _colsum(dh * y3)
        small_ref[2:3, :] += _colsum(dh)
        dx1 = dout_ref[...] + _rms_bwd(dy3 * g3v, xn, r3)
        dx1_ref[...] = dx1
        gt = mod_ref[GT_M:GT_M + 1, :]
        yn, r2 = _rms(y_ref[...])
        small_ref[3:4, :] += _colsum(dx1 * gt * yn)
        small_ref[4:5, :] += _colsum(dx1 * (yn * g2v))
        dy_ref[...] = _rms_bwd(dx1 * gt * g2v, yn, r2).astype(BF16)

    return _Epilogue(
        [x1, dout, ymix, mod6, g3, g2],
        [_row_tile(rows, D_MODEL)] * 3 + [_whole((6, D_MODEL)), _whole((1, D_MODEL)), _whole((1, D_MODEL))],
        [jax.ShapeDtypeStruct((seq, D_MODEL), F32), jax.ShapeDtypeStruct((seq, D_MODEL), BF16),
         jax.ShapeDtypeStruct((8, D_MODEL), F32)],
        [_row_tile(rows, D_MODEL), _row_tile(rows, D_MODEL), _whole((8, D_MODEL))], fn, keep_product=False)


def _pre_mix_bwd(x, dx1, mod6, g1, rows):
    seq = x.shape[0]

    def fn(dh, first, ins, outs):
        x_ref, dx1_ref, mod_ref, g_ref = ins
        dx_ref, small_ref = outs

        @pl.when(first)
        def _():
            small_ref[...] = jnp.zeros_like(small_ref)

        g1v = g_ref[...]
        xn, r = _rms(x_ref[...])
        dy = dh * (1.0 + mod_ref[SC_M:SC_M + 1, :])
        small_ref[0:1, :] += _colsum(dy * xn)
        small_ref[1:2, :] += _colsum(dh * (xn * g1v))
        small_ref[2:3, :] += _colsum(dh)
        dx_ref[...] = dx1_ref[...] + _rms_bwd(dy * g1v, xn, r)

    return _Epilogue(
        [x, dx1, mod6, g1],
        [_row_tile(rows, D_MODEL), _row_tile(rows, D_MODEL), _whole((6, D_MODEL)), _whole((1, D_MODEL))],
        [jax.ShapeDtypeStruct((seq, D_MODEL), F32), jax.ShapeDtypeStruct((8, D_MODEL), F32)],
        [_row_tile(rows, D_MODEL), _whole((8, D_MODEL))], fn, keep_product=False)


def _toeplitz_onehot(shape, offset_axis, top):
    m = lax.broadcasted_iota(jnp.int32, shape, offset_axis)
    i = lax.broadcasted_iota(jnp.int32, shape, 1 - offset_axis)
    return (i == jnp.clip(top - m, -MAX_REL, MAX_REL) + MAX_REL).astype(F32)


def _bias_table(rel_bias):
    width = GROUP_Q + GROUP_K

    def body(rb_ref, o_ref, t_ref):
        t_ref[...] = jnp.dot(rb_ref[...], _toeplitz_onehot((N_REL, width), 1, GROUP_K - 1), precision=HIGHEST,
                             preferred_element_type=F32)
        lane = lax.broadcasted_iota(jnp.int32, (N_HEADS, GROUP_K), 1)
        for r in range(GROUP_Q):
            first_key = (r // CHUNK) * CHUNK
            band = jnp.logical_and(lane >= first_key, lane < first_key + BAND)
            o_ref[r] = jnp.where(band, t_ref[:, GROUP_Q - 1 - r:GROUP_Q - 1 - r + GROUP_K], NEG_INF)

    return pl.pallas_call(
        body, name="bias_table", out_shape=jax.ShapeDtypeStruct((GROUP_Q, N_HEADS, GROUP_K), F32),
        scratch_shapes=[pltpu.VMEM((N_HEADS, width), F32)],
    )(rel_bias)


def _bias_grad(dbias_q):
    def body(d_ref, o_ref, t_ref):
        t_ref[...] = jnp.zeros_like(t_ref)
        for qi in range(CHUNK):
            t_ref[:, CHUNK - 1 - qi:CHUNK - 1 - qi + BAND] += d_ref[qi]
        o_ref[...] = jnp.dot(t_ref[...], _toeplitz_onehot((TOEPLITZ, N_REL), 0, BAND - 1), precision=HIGHEST,
                             preferred_element_type=F32)

    return pl.pallas_call(
        body, name="bias_grad", out_shape=jax.ShapeDtypeStruct((N_HEADS, N_REL), F32),
        scratch_shapes=[pltpu.VMEM((N_HEADS, TOEPLITZ), F32)],
    )(dbias_q)


def _load_resident(pairs, sems):
    copies = [pltpu.make_async_copy(src, dst, sems.at[n]) for n, (src, dst) in enumerate(pairs)]
    for cp in copies:
        cp.start()
    for cp in copies:
        cp.wait()


def _softmax_rows(s_ref, t_ref, before_start, rows):
    s = s_ref[rows, :] * (HEAD_DIM ** -0.5) + t_ref[rows, :] + before_start
    e = jnp.exp(s - jnp.max(s, axis=-1, keepdims=True))
    return e / jnp.sum(e, axis=-1, keepdims=True)


def _before_start(g):
    kj = lax.broadcasted_iota(jnp.int32, (8, GROUP_K), 1)
    return jnp.where(kj >= PAD_ROWS - g * GROUP_Q, 0.0, NEG_INF)


def _attn_fwd(qkv, kpad, vpad, table, comm=None):
    seq = qkv.shape[0]

    def body(ins, outs, scratch):
        q_ref, k_hbm, v_hbm, t_hbm = ins
        (o_ref,) = outs
        k_ref, v_ref, t_ref, s_ref, p_ref, sems = scratch
        g = pl.program_id(0)

        @pl.when(g == 0)
        def _():
            _load_resident(((k_hbm, k_ref), (v_hbm, v_ref), (t_hbm, t_ref)), sems)

        window = pl.ds(pl.multiple_of(g * GROUP_Q, GROUP_Q), GROUP_K)
        before_start = _before_start(g)
        for h in range(N_HEADS):
            cols = slice(h * HEAD_DIM, (h + 1) * HEAD_DIM)
            buf = h % 2
            s_ref[buf] = lax.dot_general(q_ref[:, cols], k_ref[window, cols], _DIMS["nt"],
                                         preferred_element_type=F32)
            for row in range(0, GROUP_Q, SOFTMAX_ROWS):
                halves = [_softmax_rows(s_ref.at[buf], t_ref.at[h], before_start, slice(r, r + 8))
                          for r in (row, row + 8)]
                p_ref[buf, row:row + SOFTMAX_ROWS, :] = jnp.concatenate(halves, axis=0).astype(BF16)
            o_ref[:, cols] = jnp.dot(p_ref[buf], v_ref[window, cols], preferred_element_type=F32).astype(BF16)

    (ao,), extra = _host_call(
        body, "attn_fwd", grid=(seq // GROUP_Q,),
        in_specs=[_tile(GROUP_Q, D_ATTN), ANY, ANY, ANY], out_specs=[_tile(GROUP_Q, D_ATTN)],
        out_shape=[jax.ShapeDtypeStruct((seq, D_ATTN), BF16)],
        scratch_shapes=[pltpu.VMEM(kpad.shape, BF16), pltpu.VMEM(vpad.shape, BF16), pltpu.VMEM(table.shape, F32),
                        pltpu.VMEM((2, GROUP_Q, GROUP_K), F32), pltpu.VMEM((2, GROUP_Q, GROUP_K), BF16),
                        pltpu.SemaphoreType.DMA((3,))],
        args=[qkv, kpad, vpad, table], comm=comm)
    return ao, extra


def _attn_bwd(qkv, kpad, vpad, table, dao, comm=None):
    seq = qkv.shape[0]
    n_groups = seq // GROUP_Q
    fold_w = GROUP_K + (GROUP - 1) * CHUNK

    def body(ins, outs, scratch):
        q_ref, do_ref, k_hbm, v_hbm, t_hbm = ins
        dq_ref, dkt_hbm, dvt_hbm, db_ref, cs_ref = outs
        k_ref, v_ref, t_ref, db_acc, dkt_acc, dvt_acc, s_ref, dp_ref, p_ref, ds_ref, sems = scratch
        g = pl.program_id(0)

        @pl.when(g == 0)
        def _():
            _load_resident(((k_hbm, k_ref), (v_hbm, v_ref), (t_hbm, t_ref)), sems)
            db_acc[...] = jnp.zeros_like(db_acc)
            dkt_acc[...] = jnp.zeros_like(dkt_acc)
            dvt_acc[...] = jnp.zeros_like(dvt_acc)
            cs_ref[...] = jnp.zeros_like(cs_ref)

        window = pl.ds(pl.multiple_of(g * GROUP_Q, GROUP_Q), GROUP_K)
        before_start = _before_start(g)
        for h in range(N_HEADS):
            cols = slice(h * HEAD_DIM, (h + 1) * HEAD_DIM)
            buf = h % 2
            qh, doh = q_ref[:, cols], do_ref[:, cols]
            kh, vh = k_ref[window, cols], v_ref[window, cols]
            s_ref[buf] = lax.dot_general(qh, kh, _DIMS["nt"], preferred_element_type=F32)
            dp_ref[buf] = lax.dot_general(doh, vh, _DIMS["nt"], preferred_element_type=F32)
            for row in range(0, GROUP_Q, SOFTMAX_ROWS):
                p_halves, ds_halves = [], []
                for r in (row, row + 8):
                    p = _softmax_rows(s_ref.at[buf], t_ref.at[h], before_start, slice(r, r + 8))
                    dp = dp_ref[buf, r:r + 8, :]
                    ds = p * (dp - jnp.sum(dp * p, axis=-1, keepdims=True))
                    chunk = r // CHUNK
                    shift = (GROUP - 1 - chunk) * CHUNK
                    db_acc[h, r - chunk * CHUNK:r - chunk * CHUNK + 8, shift:shift + GROUP_K] += ds
                    p_halves.append(p)
                    ds_halves.append(ds * (HEAD_DIM ** -0.5))
                p_ref[buf, row:row + SOFTMAX_ROWS, :] = jnp.concatenate(p_halves, axis=0).astype(BF16)
                ds_ref[buf, row:row + SOFTMAX_ROWS, :] = jnp.concatenate(ds_halves, axis=0).astype(BF16)
            dq_ref[:, cols] = jnp.dot(ds_ref[buf], kh, preferred_element_type=F32).astype(BF16)
            dkt_acc[cols, window] += lax.dot_general(qh, ds_ref[buf], _DIMS["tn"], preferred_element_type=F32)
            dvt_acc[cols, window] += lax.dot_general(doh, p_ref[buf], _DIMS["tn"], preferred_element_type=F32)
        cs_ref[0:1, :] += _colsum(dq_ref[...].astype(F32))

        @pl.when(g == n_groups - 1)
        def _():
            lo = (GROUP - 1) * CHUNK
            for h in range(N_HEADS):
                db_ref[h] = db_acc[h, :, lo:lo + BAND]
            inside = pl.ds(PAD_ROWS, seq)
            on_diagonal = (lax.broadcasted_iota(jnp.int32, (D_ATTN, D_ATTN), 0)
                           == lax.broadcasted_iota(jnp.int32, (D_ATTN, D_ATTN), 1))
            for row, acc in ((1, dkt_acc), (2, dvt_acc)):
                column = jnp.sum(acc[:, inside], axis=1, keepdims=True)
                cs_ref[row:row + 1, :] = _colsum(jnp.where(on_diagonal, column, 0.0))
            out_k = pltpu.make_async_copy(dkt_acc.at[:, inside], dkt_hbm, sems.at[0])
            out_v = pltpu.make_async_copy(dvt_acc.at[:, inside], dvt_hbm, sems.at[1])
            out_k.start()
            out_v.start()
            out_k.wait()
            out_v.wait()

    t_shape = (D_ATTN, seq + PAD_ROWS)
    outs, extra = _host_call(
        body, "attn_bwd", grid=(n_groups,),
        in_specs=[_tile(GROUP_Q, D_ATTN), _tile(GROUP_Q, D_ATTN), ANY, ANY, ANY],
        out_specs=[_tile(GROUP_Q, D_ATTN), ANY, ANY, _full((N_HEADS, CHUNK, BAND)), _full((8, D_ATTN))],
        out_shape=[jax.ShapeDtypeStruct((seq, D_ATTN), BF16), jax.ShapeDtypeStruct((D_ATTN, seq), F32),
                   jax.ShapeDtypeStruct((D_ATTN, seq), F32), jax.ShapeDtypeStruct((N_HEADS, CHUNK, BAND), F32),
                   jax.ShapeDtypeStruct((8, D_ATTN), F32)],
        scratch_shapes=[pltpu.VMEM(kpad.shape, BF16), pltpu.VMEM(vpad.shape, BF16), pltpu.VMEM(table.shape, F32),
                        pltpu.VMEM((N_HEADS, CHUNK, fold_w), F32), pltpu.VMEM(t_shape, F32),
                        pltpu.VMEM(t_shape, F32), pltpu.VMEM((2, GROUP_Q, GROUP_K), F32),
                        pltpu.VMEM((2, GROUP_Q, GROUP_K), F32), pltpu.VMEM((2, GROUP_Q, GROUP_K), BF16),
                        pltpu.VMEM((2, GROUP_Q, GROUP_K), BF16), pltpu.SemaphoreType.DMA((3,))],
        args=[qkv, dao, kpad, vpad, table], comm=comm)
    return outs, extra


def _assemble_dz(dq, dkt, dvt, dglu_a, dglu_b, dga, dgb):
    seq = dq.shape[0]
    rows = 512
    transposed = pl.BlockSpec((D_ATTN, rows), lambda i: (0, i))

    def body(dq_ref, dkt_ref, dvt_ref, da_ref, db_ref, dga_ref, dgb_ref, o_ref):
        o_ref[:, 0:D_ATTN] = dq_ref[...]
        o_ref[:, D_ATTN:2 * D_ATTN] = dkt_ref[...].T.astype(BF16)
        o_ref[:, 2 * D_ATTN:3 * D_ATTN] = dvt_ref[...].T.astype(BF16)
        off = 3 * D_ATTN
        for ref in (da_ref, db_ref, dga_ref, dgb_ref):
            width = ref.shape[1]
            o_ref[:, off:off + width] = ref[...]
            off += width

    width = 3 * D_ATTN + 2 * D_CONV + 2 * D_MODEL
    return pl.pallas_call(
        body, name="assemble_dz", out_shape=jax.ShapeDtypeStruct((seq, width), BF16), grid=(seq // rows,),
        in_specs=[_tile(rows, D_ATTN), transposed, transposed, _tile(rows, D_CONV), _tile(rows, D_CONV),
                  _tile(rows, D_MODEL), _tile(rows, D_MODEL)],
        out_specs=_tile(rows, width), compiler_params=_cparams(1),
    )(dq, dkt, dvt, dglu_a, dglu_b, dga, dgb)


CONV_ROWS = 256


def _ln_silu(u1, g, b):
    mu = jnp.mean(u1, axis=-1, keepdims=True)
    xc = u1 - mu
    rs = lax.rsqrt(jnp.mean(xc * xc, axis=-1, keepdims=True) + EPS)
    xhat = xc * rs
    u2 = xhat * g + b
    return xhat, rs, u2


def _glu_into(s_ref, a_ref, b_ref, ah_ref, bh_ref, first):
    halo = ah_ref[...] * _sig(bh_ref[...])
    s_ref[0:CONV_HALO, :] = jnp.where(first, 0.0, halo)
    s_ref[CONV_HALO:CONV_HALO + CONV_ROWS, :] = a_ref[...] * _sig(b_ref[...])


CONV_LANES = 128
CONV_TILES = CONV_ROWS // 8


def _lag_weights(w_ref, lanes):
    return {e: jnp.broadcast_to(w_ref[CONV_K - 1 - e:CONV_K - e, lanes], (8, CONV_LANES)) for e in range(CONV_K)}


def _class_sums(w, tiles, k):
    total = None
    for a, tile in enumerate(tiles):
        if 8 * a + k < CONV_K:
            term = w[8 * a + k] * tile
            total = term if total is None else total + term
    return total


def _conv_back(src_ref, first_tile, w, lanes, row_id, emit):
    before = None
    for m in range(-1, CONV_TILES):
        tiles = [src_ref[8 * (first_tile + m - a):8 * (first_tile + m - a) + 8, lanes] for a in range(4)]
        rolled = [None] + [pltpu.roll(_class_sums(w, tiles, k), k, 0) for k in range(1, 8)]
        if m >= 0:
            out = _class_sums(w, tiles, 0)
            for k in range(1, 8):
                out = out + jnp.where(row_id < k, before[k], rolled[k])
            emit(m, out)
        before = rolled


def _conv_ahead(src_ref, w, lanes, row_id, emit):
    before = None
    for m in range(CONV_TILES + 1):
        tiles = [src_ref[8 * (m + a):8 * (m + a) + 8, lanes] for a in range(4)]
        rolled = [None] + [pltpu.roll(_class_sums(w, tiles, k), 8 - k, 0) for k in range(1, 8)]
        if m >= 1:
            out = before[0]
            for k in range(1, 8):
                out = out + jnp.where(row_id < 8 - k, before[k], rolled[k])
            emit(m - 1, out)
        before = [_class_sums(w, tiles, 0) if m < CONV_TILES else None] + rolled[1:]


def _conv_weight_sums(d_ref, s_ref, lanes, row_id, whole_shifts):
    zero = jnp.zeros((8, CONV_LANES), F32)
    sums = {8 * a + k: zero for a in whole_shifts for k in range(8) if 8 * a + k < CONV_K}

    def d_tile(m):
        return d_ref[8 * m:8 * m + 8, lanes] if 0 <= m < CONV_TILES else zero

    rolled = [None] + [zero] * 7
    for m in range(-1, CONV_TILES):
        cur, nxt = d_tile(m), d_tile(m + 1)
        rolled_next = [None] + [pltpu.roll(nxt, 8 - k, 0) for k in range(1, 8)]
        shifted = [cur] + [jnp.where(row_id < 8 - k, rolled[k], rolled_next[k]) for k in range(1, 8)]
        for a in whole_shifts:
            tile = s_ref[8 * (CONV_HALO // 8 + m - a):8 * (CONV_HALO // 8 + m - a) + 8, lanes]
            for k in range(8):
                if 8 * a + k < CONV_K and not (m < 0 and k == 0):
                    sums[8 * a + k] = sums[8 * a + k] + shifted[k] * tile
        rolled = rolled_next
    return sums


def _conv_fwd(zr, w_dw, b_dw, g_ln, b_ln, comm=None):
    seq = zr.shape[0]

    def body(a_ref, b_ref, ah_ref, bh_ref, w_ref, bias_ref, g_ref, bl_ref, u1_ref, u3_ref, s_ref):
        _glu_into(s_ref, a_ref, b_ref, ah_ref, bh_ref, pl.program_id(0) == 0)
        row_id = lax.broadcasted_iota(jnp.int32, (8, CONV_LANES), 0)
        for lo in range(0, D_CONV, CONV_LANES):
            lanes = slice(lo, lo + CONV_LANES)
            bias = jnp.broadcast_to(bias_ref[:, lanes], (8, CONV_LANES))

            def emit(m, out, lanes=lanes, bias=bias):
                u1_ref[8 * m:8 * m + 8, lanes] = out + bias

            _conv_back(s_ref, CONV_HALO // 8, _lag_weights(w_ref, lanes), lanes, row_id, emit)
        _, _, u2 = _ln_silu(u1_ref[...], g_ref[...], bl_ref[...])
        u3_ref[...] = (u2 * _sig(u2)).astype(BF16)

    return _host_call(
        lambda ins, outs, scratch: body(*ins, *outs, *scratch), "conv_fwd", grid=(seq // CONV_ROWS,),
        in_specs=[_tile(CONV_ROWS, D_CONV, 0), _tile(CONV_ROWS, D_CONV, 1),
                  _prev(CONV_HALO, D_CONV, CONV_ROWS, 0), _prev(CONV_HALO, D_CONV, CONV_ROWS, 1),
                  _full((CONV_K, D_CONV)), _full((1, D_CONV)), _full((1, D_CONV)), _full((1, D_CONV))],
        out_specs=[_tile(CONV_ROWS, D_CONV), _tile(CONV_ROWS, D_CONV)],
        out_shape=[jax.ShapeDtypeStruct((seq, D_CONV), F32), jax.ShapeDtypeStruct((seq, D_CONV), BF16)],
        scratch_shapes=[pltpu.VMEM((CONV_HALO + CONV_ROWS, D_CONV), F32)],
        args=[zr, zr, zr, zr, w_dw, b_dw, g_ln, b_ln], comm=comm)


def _conv_bwd(zr, u1, du3, w_dw, g_ln, b_ln, comm=None):
    seq = zr.shape[0]
    n_tiles = seq // CONV_ROWS
    n_halo = seq // CONV_HALO
    ext = CONV_ROWS + CONV_HALO

    def body(a_ref, b_ref, ah_ref, bh_ref, u1_ref, u1n_ref, d3_ref, d3n_ref, w_ref, g_ref, bl_ref,
             da_ref, db_ref, dw_ref, small_ref, s_ref, d_ref, du0_ref):
        i = pl.program_id(0)

        @pl.when(i == 0)
        def _():
            dw_ref[...] = jnp.zeros_like(dw_ref)
            small_ref[...] = jnp.zeros_like(small_ref)

        _glu_into(s_ref, a_ref, b_ref, ah_ref, bh_ref, i == 0)
        gv, bv = g_ref[...], bl_ref[...]

        def du1_of(u1, d3):
            xhat, rs, u2 = _ln_silu(u1, gv, bv)
            sg = _sig(u2)
            du2 = d3 * (sg * (1.0 + u2 * (1.0 - sg)))
            dxh = du2 * gv
            du1 = rs * (dxh - jnp.mean(dxh, axis=-1, keepdims=True)
                        - xhat * jnp.mean(dxh * xhat, axis=-1, keepdims=True))
            return du1, du2, xhat

        du1, du2, xhat = du1_of(u1_ref[...], d3_ref[...])
        du1n, _, _ = du1_of(u1n_ref[...], d3n_ref[...])
        d_ref[0:CONV_ROWS, :] = du1
        d_ref[CONV_ROWS:ext, :] = jnp.where(i == n_tiles - 1, 0.0, du1n)
        small_ref[0:1, :] += _colsum(du1)
        small_ref[1:2, :] += _colsum(du2 * xhat)
        small_ref[2:3, :] += _colsum(du2)
        row_id = lax.broadcasted_iota(jnp.int32, (8, CONV_LANES), 0)
        for lo in range(0, D_CONV, CONV_LANES):
            lanes = slice(lo, lo + CONV_LANES)

            def emit(m, out, lanes=lanes):
                du0_ref[8 * m:8 * m + 8, lanes] = out

            _conv_ahead(d_ref, _lag_weights(w_ref, lanes), lanes, row_id, emit)
            for whole_shifts in ((0, 1), (2, 3)):
                for e, total in _conv_weight_sums(d_ref, s_ref, lanes, row_id, whole_shifts).items():
                    dw_ref[CONV_K - 1 - e:CONV_K - e, lanes] += _colsum(total)
        du0 = du0_ref[...]
        sb = _sig(b_ref[...])
        da = du0 * sb
        dbv = du0 * a_ref[...] * sb * (1.0 - sb)
        da_ref[...] = da.astype(BF16)
        db_ref[...] = dbv.astype(BF16)
        small_ref[3:4, :] += _colsum(da)
        small_ref[4:5, :] += _colsum(dbv)

    return _host_call(
        lambda ins, outs, scratch: body(*ins, *outs, *scratch), "conv_bwd", grid=(n_tiles,),
        in_specs=[_tile(CONV_ROWS, D_CONV, 0), _tile(CONV_ROWS, D_CONV, 1),
                  _prev(CONV_HALO, D_CONV, CONV_ROWS, 0), _prev(CONV_HALO, D_CONV, CONV_ROWS, 1),
                  _tile(CONV_ROWS, D_CONV), _next(CONV_HALO, D_CONV, CONV_ROWS, n_halo),
                  _tile(CONV_ROWS, D_CONV), _next(CONV_HALO, D_CONV, CONV_ROWS, n_halo),
                  _full((CONV_K, D_CONV)), _full((1, D_CONV)), _full((1, D_CONV))],
        out_specs=[_tile(CONV_ROWS, D_CONV), _tile(CONV_ROWS, D_CONV), _full((CONV_HALO, D_CONV)),
                   _full((8, D_CONV))],
        out_shape=[jax.ShapeDtypeStruct((seq, D_CONV), BF16), jax.ShapeDtypeStruct((seq, D_CONV), BF16),
                   jax.ShapeDtypeStruct((CONV_HALO, D_CONV), F32), jax.ShapeDtypeStruct((8, D_CONV), F32)],
        scratch_shapes=[pltpu.VMEM((ext, D_CONV), F32), pltpu.VMEM((ext, D_CONV), F32),
                        pltpu.VMEM((CONV_ROWS, D_CONV), F32)],
        args=[zr, zr, zr, zr, u1, u1, du3, du3, w_dw, g_ln, b_ln], comm=comm)


MERGE_ROWS = 256


def _merge_fwd(ao, u3, zr, w_ao, w_co, b_co):
    seq = ao.shape[0]

    def body(ao_ref, u3_ref, ga_ref, gb_ref, wa_ref, wc_ref, bc_ref, y_ref, a_ref, cb_ref):
        a = jnp.dot(ao_ref[...], wa_ref[...], preferred_element_type=F32)
        cb = jnp.dot(u3_ref[...], wc_ref[...], preferred_element_type=F32) + bc_ref[...]
        a_ref[...] = a
        cb_ref[...] = cb
        y_ref[...] = (_sig(ga_ref[...]) * a + _sig(gb_ref[...]) * cb).astype(BF16)

    f32_out = jax.ShapeDtypeStruct((seq, D_MODEL), F32)
    return pl.pallas_call(
        body, name="merge_fwd",
        out_shape=[jax.ShapeDtypeStruct((seq, D_MODEL), BF16), f32_out, f32_out],
        grid=(seq // MERGE_ROWS,),
        in_specs=[_tile(MERGE_ROWS, D_ATTN), _tile(MERGE_ROWS, D_CONV), _tile(MERGE_ROWS, D_MODEL, 1),
                  _tile(MERGE_ROWS, D_MODEL, 2), _full(w_ao.shape), _full(w_co.shape), _full((1, D_MODEL))],
        out_specs=[_tile(MERGE_ROWS, D_MODEL)] * 3, compiler_params=_cparams(1),
    )(ao, u3, zr, zr, w_ao, w_co, b_co)


def _merge_bwd(a, cb, zr, rows):
    seq = a.shape[0]

    def fn(dy_v, first, ins, outs):
        a_ref, cb_ref, ga_ref, gb_ref = ins
        da_ref, dcb_ref, dga_ref, dgb_ref, small_ref = outs

        @pl.when(first)
        def _():
            small_ref[...] = jnp.zeros_like(small_ref)

        sa, sb = _sig(ga_ref[...]), _sig(gb_ref[...])
        dcb = dy_v * sb
        dga = dy_v * a_ref[...] * sa * (1.0 - sa)
        dgb = dy_v * cb_ref[...] * sb * (1.0 - sb)
        da_ref[...] = (dy_v * sa).astype(BF16)
        dcb_ref[...] = dcb.astype(BF16)
        dga_ref[...] = dga.astype(BF16)
        dgb_ref[...] = dgb.astype(BF16)
        small_ref[0:1, :] += _colsum(dga)
        small_ref[1:2, :] += _colsum(dgb)
        small_ref[2:3, :] += _colsum(dcb)

    bf = jax.ShapeDtypeStruct((seq, D_MODEL), BF16)
    gate = lambda col: pl.BlockSpec((rows, D_MODEL), lambda i, j: (i, col))
    return _Epilogue(
        [a, cb, zr, zr], [_row_tile(rows, D_MODEL), _row_tile(rows, D_MODEL), gate(1), gate(2)],
        [bf, bf, bf, bf, jax.ShapeDtypeStruct((8, D_MODEL), F32)],
        [_row_tile(rows, D_MODEL)] * 4 + [_whole((8, D_MODEL))], fn, keep_product=False)


FFN_ROWS = 2048
FFN_BLOCKS = D_FF // FFN_COLS
GELU_C = math.sqrt(2.0 / math.pi)


def _gelu(v):
    t = jnp.tanh(GELU_C * (v + 0.044715 * (v * v * v)))
    return 0.5 * v * (1.0 + t), t


def _gelu_grad(v, t):
    return 0.5 * (1.0 + t) + 0.5 * v * (1.0 - t * t) * (GELU_C * (1.0 + 3.0 * 0.044715 * (v * v)))


def _sublane_rows(ref, n):
    return [jnp.broadcast_to(ref[r:r + 1, :], (8, FFN_COLS)) for r in range(n)]


def _rolls(tile, shifts):
    return tuple(pltpu.roll(tile, s, 0) for s in shifts)


def _behind(prev_rolls, cur, row_id):
    rolls = _rolls(cur, (1, 2))
    x1 = jnp.where(row_id < 1, prev_rolls[0], rolls[0])
    x2 = jnp.where(row_id < 2, prev_rolls[1], rolls[1])
    return (x2, x1, cur), rolls


def _ahead(cur_rolls, next_rolls, row_id):
    return (jnp.where(row_id < 7, cur_rolls[0], next_rolls[0]), jnp.where(row_id < 6, cur_rolls[1], next_rolls[1]))


def _conv3(taps, w, bias):
    return w[0] * taps[0] + w[1] * taps[1] + w[2] * taps[2] + bias


def _ffn_specs(rows):
    tile = lambda off: pl.BlockSpec((rows, FFN_COLS), lambda j, i: (i, j + off))
    prev = lambda off: pl.BlockSpec((FFN_HALO, FFN_COLS),
                                    lambda j, i: (jnp.maximum(i * (rows // FFN_HALO) - 1, 0), j + off))
    wgt = lambda off: pl.BlockSpec((3, FFN_COLS), lambda j, i: (0, j + off))
    vec = lambda off: pl.BlockSpec((1, FFN_COLS), lambda j, i: (0, j + off))
    return tile, prev, wgt, vec


def _ffn_act(up, w_dw, b_dw):
    seq = up.shape[0]
    tile, prev, wgt, vec = _ffn_specs(FFN_ROWS)

    def body(v_ref, g_ref, vp_ref, gp_ref, wv_ref, wg_ref, bv_ref, bg_ref, act_ref):
        first = pl.program_id(1) == 0
        row_id = lax.broadcasted_iota(jnp.int32, (8, FFN_COLS), 0)
        wv, wg = _sublane_rows(wv_ref, 3), _sublane_rows(wg_ref, 3)
        (bv,), (bg,) = _sublane_rows(bv_ref, 1), _sublane_rows(bg_ref, 1)
        rolls_v = _rolls(jnp.where(first, 0.0, vp_ref[...]), (1, 2))
        rolls_g = _rolls(jnp.where(first, 0.0, gp_ref[...]), (1, 2))
        for row in range(0, FFN_ROWS, 16):
            halves = []
            for r in (row, row + 8):
                taps_v, rolls_v = _behind(rolls_v, v_ref[r:r + 8, :], row_id)
                taps_g, rolls_g = _behind(rolls_g, g_ref[r:r + 8, :], row_id)
                halves.append(_gelu(_conv3(taps_g, wg, bg))[0] * _conv3(taps_v, wv, bv))
            act_ref[row:row + 16, :] = jnp.concatenate(halves, axis=0).astype(BF16)

    return pl.pallas_call(
        body, name="ffn_act", out_shape=jax.ShapeDtypeStruct((seq, D_FF), BF16),
        grid=(FFN_BLOCKS, seq // FFN_ROWS),
        in_specs=[tile(0), tile(FFN_BLOCKS), prev(0), prev(FFN_BLOCKS), wgt(0), wgt(FFN_BLOCKS),
                  vec(0), vec(FFN_BLOCKS)],
        out_specs=tile(0), compiler_params=_cparams(2),
    )(up, up, up, up, w_dw, w_dw, b_dw, b_dw)


def _ffn_act_bwd(up, dact, w_dw, b_dw, comm=None):
    seq = up.shape[0]
    n_tiles = seq // FFN_ROWS
    n_halo = seq // FFN_HALO
    tile, prev, wgt, vec = _ffn_specs(FFN_ROWS)
    nxt = lambda off: pl.BlockSpec(
        (FFN_HALO, FFN_COLS), lambda j, i: (jnp.minimum((i + 1) * (FFN_ROWS // FFN_HALO), n_halo - 1), j + off))
    acc = lambda off: pl.BlockSpec((8, FFN_COLS), lambda j, i: (0, j + off))

    def body(v_ref, g_ref, vp_ref, gp_ref, vn_ref, gn_ref, da_ref, dan_ref, wv_ref, wg_ref, bv_ref, bg_ref,
             dv_out, dg_out, dwv_ref, dwg_ref, dbv_ref, dbg_ref):
        i = pl.program_id(1)
        first, last = i == 0, i == n_tiles - 1

        @pl.when(first)
        def _():
            for r in (dwv_ref, dwg_ref, dbv_ref, dbg_ref):
                r[...] = jnp.zeros_like(r)

        row_id = lax.broadcasted_iota(jnp.int32, (8, FFN_COLS), 0)
        wv, wg = _sublane_rows(wv_ref, 3), _sublane_rows(wg_ref, 3)
        (bv,), (bg,) = _sublane_rows(bv_ref, 1), _sublane_rows(bg_ref, 1)
        zero = jnp.zeros((8, FFN_COLS), F32)
        sums_v, sums_g = [zero] * 4, [zero] * 4
        rolls_v = _rolls(jnp.where(first, 0.0, vp_ref[...]), (1, 2))
        rolls_g = _rolls(jnp.where(first, 0.0, gp_ref[...]), (1, 2))
        behind = None
        done_v, done_g = [], []

        def grads(v_tile, g_tile, dact, rolls_v, rolls_g):
            taps_v, rolls_v = _behind(rolls_v, v_tile, row_id)
            taps_g, rolls_g = _behind(rolls_g, g_tile, row_id)
            val, gate = _conv3(taps_v, wv, bv), _conv3(taps_g, wg, bg)
            gel, t = _gelu(gate)
            return dact * gel, dact * val * _gelu_grad(gate, t), taps_v, taps_g, rolls_v, rolls_g

        def finish(tile, nxt, row):
            for (d, d_rolls), (_, n_rolls), w, done, o_ref in ((tile[0], nxt[0], wv, done_v, dv_out),
                                                               (tile[1], nxt[1], wg, done_g, dg_out)):
                d1, d2 = _ahead(d_rolls, n_rolls, row_id)
                done.append(w[2] * d + w[1] * d1 + w[0] * d2)
                if len(done) == 2:
                    o_ref[row - 16:row, :] = jnp.concatenate(done, axis=0).astype(BF16)
                    done.clear()

        for row in range(0, FFN_ROWS, 16):
            dact16 = da_ref[row:row + 16, :].astype(F32)
            for r, dact in ((row, dact16[0:8, :]), (row + 8, dact16[8:16, :])):
                dval, dgate, taps_v, taps_g, rolls_v, rolls_g = grads(v_ref[r:r + 8, :], g_ref[r:r + 8, :], dact,
                                                                      rolls_v, rolls_g)
                sums_v = [s + dval * x for s, x in zip(sums_v, taps_v)] + [sums_v[3] + dval]
                sums_g = [s + dgate * x for s, x in zip(sums_g, taps_g)] + [sums_g[3] + dgate]
                tile = ((dval, _rolls(dval, (7, 6))), (dgate, _rolls(dgate, (7, 6))))
                if behind is not None:
                    finish(behind, tile, r)
                behind = tile
        dact_next = jnp.where(last, 0.0, dan_ref[...].astype(F32)[0:FFN_HALO, :])
        dval, dgate, *_ = grads(vn_ref[...], gn_ref[...], dact_next, rolls_v, rolls_g)
        finish(behind, ((dval, _rolls(dval, (7, 6))), (dgate, _rolls(dgate, (7, 6)))), FFN_ROWS)
        for sums, dw_ref, db_ref in ((sums_v, dwv_ref, dbv_ref), (sums_g, dwg_ref, dbg_ref)):
            for tap in range(3):
                dw_ref[tap:tap + 1, :] += _colsum(sums[tap])
            db_ref[0:1, :] += _colsum(sums[3])

    half = jax.ShapeDtypeStruct((seq, D_FF), BF16)
    acc_shape = jax.ShapeDtypeStruct((8, D_FF), F32)
    return _host_call(
        lambda ins, outs, scratch: body(*ins, *outs, *scratch), "ffn_act_bwd", grid=(FFN_BLOCKS, n_tiles),
        in_specs=[tile(0), tile(FFN_BLOCKS), prev(0), prev(FFN_BLOCKS), nxt(0), nxt(FFN_BLOCKS),
                  tile(0), pl.BlockSpec((16, FFN_COLS), lambda j, i: (
                      jnp.minimum((i + 1) * (FFN_ROWS // 16), seq // 16 - 1), j)),
                  wgt(0), wgt(FFN_BLOCKS), vec(0), vec(FFN_BLOCKS)],
        out_specs=[tile(0), tile(0), acc(0), acc(0), acc(0), acc(0)],
        out_shape=[half, half, acc_shape, acc_shape, acc_shape, acc_shape],
        scratch_shapes=[], args=[up, up, up, up, up, up, dact, dact, w_dw, w_dw, b_dw, b_dw], comm=comm)


def _cols_to_blocks(full_cols):
    k, n8 = full_cols.shape
    return jnp.transpose(full_cols.reshape(k, N_DEV, n8 // N_DEV), (1, 0, 2))


def _rows_to_blocks(full_rows):
    r8, n = full_rows.shape
    return full_rows.reshape(N_DEV, r8 // N_DEV, n)


def _blocks_to_cols(gathered):
    _, k, n = gathered.shape
    return jnp.transpose(gathered, (1, 0, 2)).reshape(k, N_DEV * n)


def kernel(x, c, w_ada, b_ada, g_pre_mix, g_post_mix, w_in, b_in, rel_bias, w_attn_o, w_dw_conv, b_dw_conv, g_conv_ln, b_conv_ln, w_conv_o, b_conv_o, w_mix_o, g_pre_ffn, g_post_ffn, w_up, w_dw_ffn, b_dw_ffn, w_down, loss_target, m_w_ada, m_b_ada, m_g_pre_mix, m_g_post_mix, m_w_in, m_b_in, m_rel_bias, m_w_attn_o, m_w_dw_conv, m_b_dw_conv, m_g_conv_ln, m_b_conv_ln, m_w_conv_o, m_b_conv_o, m_w_mix_o, m_g_pre_ffn, m_g_post_ffn, m_w_up, m_w_dw_ffn, m_b_dw_ffn, m_w_down, v_w_ada, v_b_ada, v_g_pre_mix, v_g_post_mix, v_w_in, v_b_in, v_rel_bias, v_w_attn_o, v_w_dw_conv, v_b_dw_conv, v_g_conv_ln, v_b_conv_ln, v_w_conv_o, v_b_conv_o, v_w_mix_o, v_g_pre_ffn, v_g_post_ffn, v_w_up, v_w_dw_ffn, v_b_dw_ffn, v_w_down):
    names = ["w_ada", "b_ada", "g_pre_mix", "g_post_mix", "w_in", "b_in", "rel_bias", "w_attn_o", "w_dw_conv",
             "b_dw_conv", "g_conv_ln", "b_conv_ln", "w_conv_o", "b_conv_o", "w_mix_o", "g_pre_ffn", "g_post_ffn",
             "w_up", "w_dw_ffn", "b_dw_ffn", "w_down"]
    weights = dict(zip(names, [w_ada, b_ada, g_pre_mix, g_post_mix, w_in, b_in, rel_bias, w_attn_o, w_dw_conv,
                               b_dw_conv, g_conv_ln, b_conv_ln, w_conv_o, b_conv_o, w_mix_o, g_pre_ffn,
                               g_post_ffn, w_up, w_dw_ffn, b_dw_ffn, w_down]))
    mom_m = dict(zip(names, [m_w_ada, m_b_ada, m_g_pre_mix, m_g_post_mix, m_w_in, m_b_in, m_rel_bias, m_w_attn_o,
                             m_w_dw_conv, m_b_dw_conv, m_g_conv_ln, m_b_conv_ln, m_w_conv_o, m_b_conv_o,
                             m_w_mix_o, m_g_pre_ffn, m_g_post_ffn, m_w_up, m_w_dw_ffn, m_b_dw_ffn, m_w_down]))
    mom_v = dict(zip(names, [v_w_ada, v_b_ada, v_g_pre_mix, v_g_post_mix, v_w_in, v_b_in, v_rel_bias, v_w_attn_o,
                             v_w_dw_conv, v_b_dw_conv, v_g_conv_ln, v_b_conv_ln, v_w_conv_o, v_b_conv_o,
                             v_w_mix_o, v_g_pre_ffn, v_g_post_ffn, v_w_up, v_w_dw_ffn, v_b_dw_ffn, v_w_down]))
    shapes = {n: w.shape for n, w in weights.items()}

    seq = x.shape[1]
    me = 4 * lax.axis_index("x") + 2 * lax.axis_index("y") + lax.axis_index("c")
    x2 = x.reshape(seq, D_MODEL)
    target = loss_target.reshape(seq, D_MODEL)
    sq = lambda a: a.reshape(a.shape[1:])
    bf = lambda a: sq(a).astype(BF16)

    transposed = lambda a: jnp.swapaxes(sq(a), 0, 1)

    h1, c_all, mod6, (g_in, g_dwc, g_dwf) = _pre_mix(
        x2, c, sq(w_ada), b_ada, g_pre_mix,
        comm=_gather_comm([transposed(w_in).astype(BF16), sq(w_dw_conv), sq(w_dw_ffn)]))
    wt_in = g_in.reshape(g_in.shape[0] * g_in.shape[1], D_MODEL)
    wf_dwc = _blocks_to_cols(g_dwc)
    wf_dwf = _blocks_to_cols(g_dwf)
    qkv = _mm(h1, wt_in, "nt", BF16, "in_proj_qkv", bias=b_in, tm=1024, tn=768, cols=(0, 3 * D_ATTN))
    zr, _, (g_ao, g_co, g_mo) = _mm(h1, wt_in, "nt", F32, "in_proj_rest", bias=b_in, tm=1024, tn=3 * D_ATTN,
                                 cols=(3 * D_ATTN, 2 * D_CONV + 2 * D_MODEL),
                                 comm=_gather_comm([bf(w_attn_o), bf(w_conv_o), bf(w_mix_o)]))
    kpad = jnp.pad(qkv[:, D_ATTN:2 * D_ATTN], ((PAD_ROWS, 0), (0, 0)))
    vpad = jnp.pad(qkv[:, 2 * D_ATTN:], ((PAD_ROWS, 0), (0, 0)))
    table = jnp.transpose(_bias_table(sq(rel_bias)), (1, 0, 2))
    ao, (g_up,) = _attn_fwd(qkv, kpad, vpad, table, comm=_gather_comm([transposed(w_up).astype(BF16)]))
    (u1, u3), (g_dn,) = _conv_fwd(zr, wf_dwc, b_dw_conv, g_conv_ln, b_conv_ln, comm=_gather_comm([bf(w_down)]))
    wf_ao = _blocks_to_cols(g_ao)
    wf_co = _blocks_to_cols(g_co)
    wf_mo = g_mo.reshape(D_MODEL, D_MODEL)
    wt_up = g_up.reshape(g_up.shape[0] * g_up.shape[1], D_MODEL)
    wf_dn = g_dn.reshape(D_FF, D_MODEL)
    y, a_br, cb_br = _merge_fwd(ao, u3, zr, wf_ao, wf_co, b_conv_o)
    ymix, (x1, h2), _ = _mm(y, wf_mo, "nn", F32, "mix_o", tm=512, tn=D_MODEL,
                            epilogue=_post_mix_pre_ffn(x2, mod6, g_post_mix, g_pre_ffn, 512))
    up = _mm(h2, wt_up, "nt", F32, "ffn_up", tm=1024, tn=1408)
    act = _ffn_act(up, wf_dwf, b_dw_ffn)
    _, (loss_lanes, dout, dyf, small_f), _ = _mm(act, wf_dn, "nn", F32, "ffn_down", tm=512, tn=D_MODEL,
                                                 epilogue=_final(x1, target, mod6, g_post_ffn, 512))

    dact = _mm(dyf, wf_dn, "nt", BF16, "ffn_down_dx", tm=1024, tn=1408)
    gw_down = _mm(act, dyf, "tn", BF16, "ffn_down_dw", tm=256, tn=1024)
    (dup_v, dup_g, dwv, dwg, dbv, dbg), (parts_down,) = _ffn_act_bwd(
        up, dact, wf_dwf, b_dw_ffn, comm=_scatter_comm([_rows_to_blocks(gw_down)]))
    _, (dx1, dymix, small_m), _ = _mm([dup_v, dup_g], wt_up, "nn", F32, "ffn_up_dx", tm=512, tn=D_MODEL,
                                      epilogue=_mid_bwd(x1, dout, ymix, mod6, g_pre_ffn, g_post_mix, 512))
    blocks_up = _rows_to_blocks(_mm_tn_rows([dup_v, dup_g], h2, "ffn_up_dw"))
    _, (da, dcb, dga, dgb, small_g), _ = _mm(dymix, wf_mo, "nt", F32, "mix_o_dx", tm=512, tn=D_MODEL,
                                             epilogue=_merge_bwd(a_br, cb_br, zr, 512))
    gw_mo = _mm(y, dymix, "tn", BF16, "mix_o_dw")
    dao = _mm(da, wf_ao, "nt", BF16, "attn_o_dx", tm=1024)
    gw_ao = _mm(ao, da, "tn", BF16, "attn_o_dw")
    du3 = _mm(dcb, wf_co, "nt", F32, "conv_o_dx", tm=1024)
    gw_co = _mm(u3, dcb, "tn", BF16, "conv_o_dw")
    (dq, dkt, dvt, dbias, small_a), (parts_up,) = _attn_bwd(
        qkv, kpad, vpad, table, dao, comm=_scatter_comm([blocks_up]))
    g_rel = _bias_grad(jnp.transpose(dbias, (1, 0, 2)))
    (dglu_a, dglu_b, dw_conv, small_c), (parts_mo, parts_ao, parts_co) = _conv_bwd(
        zr, u1, du3, wf_dwc, g_conv_ln, b_conv_ln,
        comm=_scatter_comm([_rows_to_blocks(gw_mo), _cols_to_blocks(gw_ao), _cols_to_blocks(gw_co)]))
    dz = _assemble_dz(dq, dkt, dvt, dglu_a, dglu_b, dga, dgb)
    blocks_in = _rows_to_blocks(_mm(dz, h1, "tn", BF16, "in_proj_dw", tm=512, tn=D_MODEL))
    _, (grad_x, small_x), (parts_in, _, _) = _mm(dz, wt_in, "nn", F32, "in_proj_dx", tm=512, tn=D_MODEL,
                                                 comm=_pair_scatter_comm(blocks_in),
                                                 epilogue=_pre_mix_bwd(x2, dx1, mod6, g_pre_mix, 512))

    packed = _pack_grads(small_x, small_m, small_f, small_g, small_a, small_c, dbv, dbg, dwv, dwg, dw_conv)
    gathered, gathered_rel, gathered_loss = _run_comm(_gather_comm([packed, g_rel, loss_lanes]), "gather_small")
    gathered = gathered.reshape(N_DEV, PACKED_TOTAL)
    updates, g_dwc_full, g_dwf_full, loss_all = _small_adamw(gathered, gathered_rel, gathered_loss, weights, mom_m,
                                                             mom_v)
    loss = loss_all[0, 0]

    grads, deltas, new_m, new_v = {}, {}, {}, {}

    def record(name, update, is_transposed=False):
        for dst, val in zip((grads, deltas, new_m, new_v), update):
            dst[name] = (jnp.swapaxes(val, 0, 1) if is_transposed else val).reshape(shapes[name])

    for name, update in updates.items():
        record(name, update)

    def local_update(name, grad):
        record(name, _adamw(sq(weights[name]), sq(mom_m[name]), sq(mom_v[name]), "adamw_" + name, g=grad))

    conv_cols, ffn_cols, ada_cols = D_CONV // N_DEV, 2 * D_FF // N_DEV, 6 * D_MODEL // N_DEV
    local_update("w_dw_conv", lax.dynamic_slice(g_dwc_full, (0, me * conv_cols), (CONV_K, conv_cols)))
    local_update("w_dw_ffn", lax.dynamic_slice(g_dwf_full, (0, me * ffn_cols), (3, ffn_cols)))
    local_update("w_ada", _ada_grad(c_all, lax.dynamic_slice(gathered, (0, me * ada_cols), (N_DEV, ada_cols))))

    for name, part in (("w_attn_o", parts_ao), ("w_conv_o", parts_co), ("w_mix_o", parts_mo), ("w_down", parts_down)):
        record(name, _adamw(sq(weights[name]), sq(mom_m[name]), sq(mom_v[name]), "adamw_" + name, parts=part))
    for name, part in (("w_in", parts_in), ("w_up", parts_up)):
        record(name, _adamw(transposed(weights[name]), transposed(mom_m[name]), transposed(mom_v[name]),
                            "adamw_" + name, parts=part), is_transposed=True)

    return (loss, grad_x.reshape(x.shape), *[grads[n] for n in names], *[deltas[n] for n in names],
            *[new_m[n] for n in names], *[new_v[n] for n in names])
```

```python
import functools
import math

import jax
import jax.numpy as jnp
from jax import lax
from jax.experimental import pallas as pl
from jax.experimental.pallas import tpu as pltpu

F32 = jnp.float32
BF16 = jnp.bfloat16
HIGHEST = lax.Precision.HIGHEST

D_MODEL = 1024
CHUNK = 64
LEFT_CHUNKS = 8
BAND = (LEFT_CHUNKS + 1) * CHUNK
PAD_ROWS = LEFT_CHUNKS * CHUNK
GROUP = 4
GROUP_Q = GROUP * CHUNK
GROUP_K = GROUP_Q + PAD_ROWS
SOFTMAX_ROWS = 16
TOEPLITZ = 640
N_HEADS = 8
HEAD_DIM = 64
D_ATTN = 512
D_CONV = 512
CONV_K = 31
CONV_HALO = 32
MAX_REL = 128
N_REL = 2 * MAX_REL + 1
D_FF = 2816
FFN_HALO = 8
FFN_COLS = 256
EPS = 1e-6
NEG_INF = -1e30
N_DEV = 8

ADAM_LR = 0.001
ADAM_B1 = 0.9
ADAM_B2 = 0.999
ADAM_EPS = 1e-08
ADAM_WD = 0.01
ADAM_STEP = 10

VMEM_LIMIT_BYTES = 56 * 1024 * 1024
ADAMW_BLOCK_BYTES = 768 * 1024

MESH = pl.DeviceIdType.MESH
ANY = pl.BlockSpec(memory_space=pl.ANY)

SH_M, SC_M, GT_M, SH_F, SC_F, GT_F = range(6)

SMALL = (("b_ada", 6144), ("g_pre_mix", 1024), ("g_post_mix", 1024), ("b_in", 4608), ("b_dw_conv", 512),
         ("g_conv_ln", 512), ("b_conv_ln", 512), ("b_conv_o", 1024), ("g_pre_ffn", 1024), ("g_post_ffn", 1024),
         ("b_dw_ffn", 5632))
PACKED_TOTAL = sum(n for _, n in SMALL) + CONV_K * D_CONV + 3 * 2 * D_FF


def _cparams(n_axes):
    return pltpu.CompilerParams(vmem_limit_bytes=VMEM_LIMIT_BYTES,
                                dimension_semantics=("arbitrary",) * n_axes)


def _sig(v):
    return 1.0 / (1.0 + jnp.exp(-v))


def _pick(n, target):
    if n <= target:
        return n
    t = target - target % 128
    while n % t:
        t -= 128
    return t


def _tile(rows, cols, col=0):
    return pl.BlockSpec((rows, cols), lambda i: (i, col))


def _full(shape):
    zeros = (0,) * len(shape)
    return pl.BlockSpec(shape, lambda i: zeros)


def _prev(halo, cols, rows, col=0):
    return pl.BlockSpec((halo, cols), lambda i: (jnp.maximum(i * (rows // halo) - 1, 0), col))


def _next(halo, cols, rows, n_blocks, col=0):
    return pl.BlockSpec((halo, cols), lambda i: (jnp.minimum((i + 1) * (rows // halo), n_blocks - 1), col))


class _Comm:
    def __init__(self, inputs, out_shapes, sems, start, finish, relay=None, early=None):
        self.inputs, self.out_shapes, self.sems, self.start, self.finish = inputs, out_shapes, sems, start, finish
        self.relay, self.early = relay, early


def _host_call(body, name, grid, in_specs, out_specs, out_shape, scratch_shapes, args, comm=None):
    n_in, n_out, n_scr = len(args), len(out_shape), len(scratch_shapes)
    c_in = list(comm.inputs) if comm else []
    c_out = list(comm.out_shapes) if comm else []
    c_sem = list(comm.sems) if comm else []

    def full(*refs):
        bounds = [0, n_in, len(c_in), n_out, len(c_out), n_scr, len(c_sem)]
        cuts = [sum(bounds[:i + 1]) for i in range(len(bounds))]
        ins, cins, outs, couts, scr, csems = (refs[lo:hi] for lo, hi in zip(cuts[:-1], cuts[1:]))
        if comm:
            first = functools.reduce(jnp.logical_and, [pl.program_id(ax) == 0 for ax in range(len(grid))])
            pl.when(first)(lambda: comm.start(cins, couts, csems))
            if comm.early is not None:
                strides = [math.prod(grid[ax + 1:]) for ax in range(len(grid))]
                step = sum(pl.program_id(ax) * strides[ax] for ax in range(len(grid)))
                pl.when(step == 1)(lambda: comm.early(cins, couts, csems))
            last = functools.reduce(jnp.logical_and, [pl.program_id(ax) == grid[ax] - 1 for ax in range(len(grid))])
            if comm.relay is not None:
                pl.when(last)(lambda: comm.relay(cins, couts, csems))
        body(ins, outs, scr)
        if comm:
            pl.when(last)(lambda: comm.finish(cins, couts, csems))

    res = pl.pallas_call(
        full, name=name, grid=grid, in_specs=list(in_specs) + [ANY] * len(c_in),
        out_specs=list(out_specs) + [ANY] * len(c_out), out_shape=list(out_shape) + c_out,
        scratch_shapes=list(scratch_shapes) + c_sem, compiler_params=_cparams(len(grid)),
    )(*args, *c_in)
    return list(res[:n_out]), list(res[n_out:])


def _run_comm(comm, name):
    n_in, n_out = len(comm.inputs), len(comm.out_shapes)

    def body(*refs):
        ins, outs, sems = refs[:n_in], refs[n_in:n_in + n_out], refs[n_in + n_out:]
        comm.start(ins, outs, sems)
        if comm.relay is not None:
            comm.relay(ins, outs, sems)
        comm.finish(ins, outs, sems)

    return pl.pallas_call(
        body, name=name, out_shape=list(comm.out_shapes), in_specs=[ANY] * n_in, out_specs=[ANY] * n_out,
        scratch_shapes=list(comm.sems),
    )(*comm.inputs)


def _place():
    return lax.axis_index("x"), lax.axis_index("y"), lax.axis_index("c")


def _gather_comm(arrs):
    n = len(arrs)

    def plan(ins, outs, sems):
        send_sems, recv_sems, local_sems = sems
        x, y, c = _place()
        me, sibling = (x, y, c), (x, y, 1 - c)
        chips = [(1 - x, y), (x, 1 - y), (1 - x, 1 - y)]

        def block(k, p):
            return outs[k].at[4 * p[0] + 2 * p[1] + p[2]]

        def copy(k, s, blk, to, src=None):
            return pltpu.make_async_remote_copy(
                src_ref=block(k, blk) if src is None else src, dst_ref=block(k, blk),
                send_sem=send_sems.at[7 * k + s], recv_sem=recv_sems.at[7 * k + s],
                device_id=to, device_id_type=MESH)

        mine = [pltpu.make_async_copy(ins[k], block(k, me), local_sems.at[k]) for k in range(n)]
        first = []
        for k in range(n):
            first.append(copy(k, 0, me, sibling, src=ins[k]))
            for j, chip in enumerate(chips):
                first.append(copy(k, 1 + j, me, (*chip, c), src=ins[k]))
        return me, sibling, chips, c, copy, mine, first

    def start(ins, outs, sems):
        *_, mine, first = plan(ins, outs, sems)
        for cp in mine + first:
            cp.start()

    def relay(ins, outs, sems):
        me, sibling, chips, c, copy, _, _ = plan(ins, outs, sems)
        for j, chip in enumerate(chips):
            for k in range(n):
                copy(k, 1 + j, (*chip, c), me).wait_recv()
                copy(k, 4 + j, (*chip, c), sibling).start()

    def finish(ins, outs, sems):
        me, sibling, chips, c, copy, mine, first = plan(ins, outs, sems)
        passed = [copy(k, 4 + j, (*chip, c), sibling) for j, chip in enumerate(chips) for k in range(n)]
        for k in range(n):
            copy(k, 0, sibling, me).wait_recv()
        for j, chip in enumerate(chips):
            for k in range(n):
                copy(k, 4 + j, (*chip, 1 - c), me).wait_recv()
        for cp in first + passed:
            cp.wait_send()
        for cp in mine:
            cp.wait()

    return _Comm(list(arrs), [jax.ShapeDtypeStruct((N_DEV,) + a.shape, a.dtype) for a in arrs],
                 [pltpu.SemaphoreType.DMA((7 * n,)), pltpu.SemaphoreType.DMA((7 * n,)),
                  pltpu.SemaphoreType.DMA((n,))], start, finish, relay)


def _scatter_comm(blocks):
    n = len(blocks)

    def plan(ins, outs, sems, arrivals):
        send_sems, recv_sems, local_sems = sems
        x, y, c = _place()
        me = 4 * x + 2 * y + c
        local = [pltpu.make_async_copy(ins[k].at[me], outs[k].at[me], local_sems.at[k]) for k in range(n)]
        sends, recvs = [], []
        for k in range(n):
            for mask in range(1, N_DEV):
                px = 1 - x if mask & 4 else x
                py = 1 - y if mask & 2 else y
                pc = 1 - c if mask & 1 else c
                peer = 4 * px + 2 * py + pc
                sem = 7 * k + mask - 1
                both = dict(send_sem=send_sems.at[sem], recv_sem=recv_sems.at[sem], device_id=(px, py, pc),
                            device_id_type=MESH)
                sends.append(pltpu.make_async_remote_copy(src_ref=ins[k].at[peer], dst_ref=outs[k].at[me], **both))
                if arrivals:
                    recvs.append(pltpu.make_async_remote_copy(src_ref=ins[k].at[me], dst_ref=outs[k].at[peer],
                                                              **both))
        return local, sends, recvs

    def start(ins, outs, sems):
        local, sends, _ = plan(ins, outs, sems, arrivals=False)
        for cp in local + sends:
            cp.start()

    def finish(ins, outs, sems):
        local, sends, recvs = plan(ins, outs, sems, arrivals=True)
        for cp in recvs:
            cp.wait_recv()
        for cp in sends:
            cp.wait_send()
        for cp in local:
            cp.wait()

    return _Comm(list(blocks), [jax.ShapeDtypeStruct(b.shape, b.dtype) for b in blocks],
                 [pltpu.SemaphoreType.DMA((7 * n,)), pltpu.SemaphoreType.DMA((7 * n,)),
                  pltpu.SemaphoreType.DMA((n,))], start, finish)


def _pair_scatter_comm(block):
    _, r, c = block.shape
    quarter = jax.ShapeDtypeStruct((4, r, c), block.dtype)

    def plan(ins, outs, sems):
        parts, got, pair = outs
        d2d_send, d2d_recv, ici_send, ici_recv, local, buf_a, buf_b = sems
        x, y, cc = _place()
        mine = 2 * x + y
        chips = [(1 - x, y), (x, 1 - y), (1 - x, 1 - y)]
        to_sibling = [pltpu.make_async_remote_copy(
            src_ref=ins[0].at[2 * q + 1 - cc], dst_ref=got.at[q], send_sem=d2d_send.at[q], recv_sem=d2d_recv.at[q],
            device_id=(x, y, 1 - cc), device_id_type=MESH) for q in range(4)]
        to_chips = [pltpu.make_async_remote_copy(
            src_ref=pair.at[2 * px + py], dst_ref=parts.at[mine], send_sem=ici_send.at[j], recv_sem=ici_recv.at[j],
            device_id=(px, py, cc), device_id_type=MESH) for j, (px, py) in enumerate(chips)]
        from_chips = [pltpu.make_async_remote_copy(
            src_ref=pair.at[mine], dst_ref=parts.at[2 * px + py], send_sem=ici_send.at[j], recv_sem=ici_recv.at[j],
            device_id=(px, py, cc), device_id_type=MESH) for j, (px, py) in enumerate(chips)]
        own = pltpu.make_async_copy(pair.at[mine], parts.at[mine], local.at[2])
        return cc, got, pair, local, buf_a, buf_b, to_sibling, to_chips, from_chips, own

    def start(ins, outs, sems):
        for cp in plan(ins, outs, sems)[6]:
            cp.start()

    def early(ins, outs, sems):
        cc, got, pair, local, buf_a, buf_b, to_sibling, to_chips, _, own = plan(ins, outs, sems)
        for q in range(4):
            to_sibling[q].wait_recv()
            loads = [pltpu.make_async_copy(ins[0].at[2 * q + cc], buf_a, local.at[0]),
                     pltpu.make_async_copy(got.at[q], buf_b, local.at[1])]
            for cp in loads:
                cp.start()
            for cp in loads:
                cp.wait()
            buf_a[...] = (buf_a[...].astype(F32) + buf_b[...].astype(F32)).astype(block.dtype)
            store = pltpu.make_async_copy(buf_a, pair.at[q], local.at[0])
            store.start()
            store.wait()
        for cp in to_chips + [own]:
            cp.start()

    def finish(ins, outs, sems):
        *_, to_sibling, to_chips, from_chips, own = plan(ins, outs, sems)
        for cp in from_chips:
            cp.wait_recv()
        for cp in to_chips + to_sibling:
            cp.wait_send()
        own.wait()

    return _Comm([block], [quarter, quarter, quarter],
                 [pltpu.SemaphoreType.DMA((4,)), pltpu.SemaphoreType.DMA((4,)), pltpu.SemaphoreType.DMA((3,)),
                  pltpu.SemaphoreType.DMA((3,)), pltpu.SemaphoreType.DMA((3,)), pltpu.VMEM((r, c), block.dtype),
                  pltpu.VMEM((r, c), block.dtype)], start, finish, early=early)


_DIMS = {"nn": (((1,), (0,)), ((), ())), "nt": (((1,), (1,)), ((), ())), "tn": (((0,), (0,)), ((), ()))}


class _Epilogue:
    def __init__(self, args, in_specs, out_shapes, out_specs, fn, keep_product):
        self.args, self.in_specs, self.out_shapes, self.out_specs = args, in_specs, out_shapes, out_specs
        self.fn, self.keep_product = fn, keep_product


def _row_tile(rows, cols):
    return pl.BlockSpec((rows, cols), lambda i, j: (i, 0))


def _whole(shape):
    zeros = (0,) * len(shape)
    return pl.BlockSpec(shape, lambda i, j: zeros)


def _mm(a, b, mode, out_dtype, name, bias=None, tm=512, tn=512, comm=None, cols=None, epilogue=None):
    pieces = a if isinstance(a, (list, tuple)) else [a]
    assert all(p.dtype == BF16 for p in pieces) and b.dtype == BF16
    a = pieces[0]
    if mode == "tn":
        k_dim, m_dim = a.shape
    else:
        m_dim, k_dim = a.shape
    n_dim = b.shape[0] if mode == "nt" else b.shape[1]
    col0 = 0
    if cols is not None:
        assert mode != "tn" and cols[0] % tn == 0 and cols[1] % tn == 0
        col0, n_dim = cols[0] // tn, cols[1]
    tm, tn = _pick(m_dim, tm), _pick(n_dim, tn)
    assert mode != "tn" or len(pieces) == 1
    a_specs = [pl.BlockSpec((k_dim, tm), lambda i, j: (0, i)) if mode == "tn"
               else pl.BlockSpec((tm, k_dim), lambda i, j: (i, 0))] * len(pieces)
    once = dict(pipeline_mode=pl.Buffered(1)) if tn == n_dim else {}
    if mode == "nt":
        b_specs = [pl.BlockSpec((tn, k_dim), lambda i, j, p=p: (j + col0, p), **once) for p in range(len(pieces))]
    else:
        b_specs = [pl.BlockSpec((k_dim, tn), lambda i, j, p=p: (p, j + col0), **once) for p in range(len(pieces))]
    in_specs = a_specs + b_specs
    args = list(pieces) + [b] * len(pieces)
    if bias is not None:
        in_specs.append(pl.BlockSpec((1, tn), lambda i, j: (0, j + col0)))
        args.append(bias)
    dims = _DIMS[mode]
    n_pieces = len(pieces)
    n_own = len(args)
    keep = epilogue is None or epilogue.keep_product
    out_specs = [pl.BlockSpec((tm, tn), lambda i, j: (i, j))] if keep else []
    out_shape = [jax.ShapeDtypeStruct((m_dim, n_dim), out_dtype)] if keep else []
    if epilogue is not None:
        assert tn == n_dim
        in_specs, args = in_specs + list(epilogue.in_specs), args + list(epilogue.args)
        out_specs, out_shape = out_specs + list(epilogue.out_specs), out_shape + list(epilogue.out_shapes)

    def body(ins, outs, scratch):
        total = lax.dot_general(ins[0][...], ins[n_pieces][...], dims, preferred_element_type=F32)
        for p in range(1, n_pieces):
            total = total + lax.dot_general(ins[p][...], ins[n_pieces + p][...], dims, preferred_element_type=F32)
        if bias is not None:
            total = total + ins[2 * n_pieces][...]
        if keep:
            outs[0][...] = total.astype(out_dtype)
        if epilogue is not None:
            epilogue.fn(total, pl.program_id(0) == 0, ins[n_own:], outs[1:] if keep else outs)

    outs, extra = _host_call(body, name, grid=(m_dim // tm, n_dim // tn), in_specs=in_specs, out_specs=out_specs,
                             out_shape=out_shape, scratch_shapes=[], args=args, comm=comm)
    product = outs[0] if keep else None
    if comm is None and epilogue is None:
        return product
    return product, outs[1:] if keep else outs, extra


def _mm_tn_rows(pieces, b, name, tm=256):
    k_dim, n_dim = b.shape
    counts = [p.shape[1] // tm for p in pieces]
    assert all(p.shape[1] % tm == 0 for p in pieces)
    firsts = [sum(counts[:q]) for q in range(len(pieces))]

    def a_spec(first, count):
        return pl.BlockSpec((k_dim, tm), lambda i: (0, jnp.clip(i - first, 0, count - 1)))

    def body(ins, outs, scratch):
        i = pl.program_id(0)
        for a_ref, first, count in zip(ins[:-1], firsts, counts):
            @pl.when(jnp.logical_and(i >= first, i < first + count))
            def _(a_ref=a_ref):
                outs[0][...] = lax.dot_general(a_ref[...], ins[-1][...], _DIMS["tn"],
                                               preferred_element_type=F32).astype(BF16)

    (out,), _ = _host_call(
        body, name, grid=(sum(counts),),
        in_specs=[a_spec(f, c) for f, c in zip(firsts, counts)] + [_full((k_dim, n_dim))],
        out_specs=[_tile(tm, n_dim)], out_shape=[jax.ShapeDtypeStruct((sum(counts) * tm, n_dim), BF16)],
        scratch_shapes=[], args=list(pieces) + [b])
    return out


def _adam_math(w, g, m, v):
    m = ADAM_B1 * m + (1.0 - ADAM_B1) * g
    v = ADAM_B2 * v + (1.0 - ADAM_B2) * (g * g)
    m_hat = m / (1.0 - ADAM_B1 ** ADAM_STEP)
    v_hat = v / (1.0 - ADAM_B2 ** ADAM_STEP)
    delta = -ADAM_LR * (m_hat / (jnp.sqrt(v_hat) + ADAM_EPS) + ADAM_WD * w)
    return delta, m, v


def _adamw(w, m, v, name, g=None, parts=None):
    rows, cols = w.shape
    tr = rows
    if rows * cols * 4 > ADAMW_BLOCK_BYTES:
        tr = max(t for t in range(16, rows, 16) if rows % t == 0 and t * cols * 4 <= ADAMW_BLOCK_BYTES)

    def body(w_ref, m_ref, v_ref, g_ref, go_ref, d_ref, mo_ref, vo_ref):
        if parts is None:
            grad = g_ref[...]
        else:
            grad = g_ref[0].astype(F32)
            for d in range(1, parts.shape[0]):
                grad = grad + g_ref[d].astype(F32)
        delta, m_new, v_new = _adam_math(w_ref[...], grad, m_ref[...], v_ref[...])
        go_ref[...] = grad
        d_ref[...] = delta
        mo_ref[...] = m_new
        vo_ref[...] = v_new

    spec = _tile(tr, cols)
    g_spec = spec if parts is None else pl.BlockSpec((parts.shape[0], tr, cols), lambda i: (0, i, 0))
    shape = jax.ShapeDtypeStruct((rows, cols), F32)
    return pl.pallas_call(
        body, name=name, out_shape=[shape] * 4, grid=(rows // tr,),
        in_specs=[spec, spec, spec, g_spec], out_specs=[spec] * 4, compiler_params=_cparams(1),
    )(w, m, v, g if parts is None else parts)


def _pack_grads(small_x, small_m, small_f, small_g, small_a, small_c, dbv, dbg, dwv, dwg, dw_conv):
    pieces = [
        (small_x, 2, D_MODEL), (small_x, 1, D_MODEL), (small_m, 4, D_MODEL), (small_m, 2, D_MODEL),
        (small_m, 1, D_MODEL), (small_f, 1, D_MODEL),
        (small_x, 0, D_MODEL), (small_m, 3, D_MODEL),
        (small_a, 0, D_ATTN), (small_a, 1, D_ATTN), (small_a, 2, D_ATTN), (small_c, 3, D_CONV),
        (small_c, 4, D_CONV), (small_g, 0, D_MODEL), (small_g, 1, D_MODEL),
        (small_c, 0, D_CONV), (small_c, 1, D_CONV), (small_c, 2, D_CONV),
        (small_g, 2, D_MODEL), (small_m, 0, D_MODEL), (small_f, 0, D_MODEL),
        (dbv, 0, D_FF), (dbg, 0, D_FF),
    ]
    pieces += [(dw_conv, j, D_CONV) for j in range(CONV_K)]
    pieces += [(src, tap, D_FF) for tap in range(3) for src in (dwv, dwg)]
    sources = [small_x, small_m, small_f, small_g, small_a, small_c, dbv, dbg, dwv, dwg, dw_conv]
    assert sum(width for _, _, width in pieces) == PACKED_TOTAL

    def body(*refs):
        o_ref = refs[-1]
        ref_of = {id(src): ref for src, ref in zip(sources, refs)}
        off = 0
        for src, row, width in pieces:
            o_ref[:, off:off + width] = ref_of[id(src)][row:row + 1, :]
            off += width

    return pl.pallas_call(body, name="pack_grads", out_shape=jax.ShapeDtypeStruct((1, PACKED_TOTAL), F32))(*sources)


def _small_adamw(gathered, gathered_rel, gathered_loss, weights, mom_m, mom_v):
    vec_names = [name for name, _ in SMALL]
    states = []
    for name in vec_names + ["rel_bias"]:
        states += [weights[name], mom_m[name], mom_v[name]]
    states = [a.reshape(a.shape[1:]) if a.ndim == 3 else a for a in states]
    n_state = len(states)

    def body(*refs):
        g_ref, rel_ref, loss_ref = refs[0], refs[1], refs[2]
        state_refs, out_refs = refs[3:3 + n_state], refs[3 + n_state:]
        total = g_ref[0:1, :]
        rel = rel_ref[0]
        loss = loss_ref[0]
        for d in range(1, N_DEV):
            total = total + g_ref[d:d + 1, :]
            rel = rel + rel_ref[d]
            loss = loss + loss_ref[d]
        off = 0
        for n, (name, width) in enumerate(SMALL):
            grad = total[:, off:off + width]
            w_ref, m_ref, v_ref = state_refs[3 * n:3 * n + 3]
            for ref, val in zip(out_refs[4 * n:4 * n + 4], (grad,) + _adam_math(w_ref[...], grad, m_ref[...], v_ref[...])):
                ref[...] = val
            off += width
        n = len(SMALL)
        w_ref, m_ref, v_ref = state_refs[3 * n:3 * n + 3]
        for ref, val in zip(out_refs[4 * n:4 * n + 4], (rel,) + _adam_math(w_ref[...], rel, m_ref[...], v_ref[...])):
            ref[...] = val
        dwc_ref, dwf_ref, loss_out = out_refs[4 * n + 4:]
        loss_out[...] = 0.5 * loss
        dwc_ref[...] = jnp.zeros_like(dwc_ref)
        dwf_ref[...] = jnp.zeros_like(dwf_ref)
        for j in range(CONV_K):
            dwc_ref[j:j + 1, :] = total[:, off:off + D_CONV]
            off += D_CONV
        for tap in range(3):
            dwf_ref[tap:tap + 1, :] = total[:, off:off + 2 * D_FF]
            off += 2 * D_FF

    out_shape = []
    for k in range(n_state // 3):
        out_shape += [jax.ShapeDtypeStruct(states[3 * k].shape, F32)] * 4
    out_shape += [jax.ShapeDtypeStruct((CONV_HALO, D_CONV), F32), jax.ShapeDtypeStruct((8, 2 * D_FF), F32),
                  jax.ShapeDtypeStruct((1, 128), F32)]
    res = pl.pallas_call(
        body, name="small_adamw", out_shape=out_shape,
        compiler_params=pltpu.CompilerParams(vmem_limit_bytes=VMEM_LIMIT_BYTES),
    )(gathered, gathered_rel, gathered_loss, *states)
    updates = {name: tuple(res[4 * n:4 * n + 4]) for n, name in enumerate(vec_names + ["rel_bias"])}
    return updates, res[-3], res[-2], res[-1]


def _ada_mod(c, w_shard):
    cols = w_shard.shape[1]

    def body(c_ref, w_ref, call_ref, mod_ref, send_sems, recv_sems):
        x, y, cc = _place()
        me = 4 * x + 2 * y + cc

        def exchange(ref, phase):
            sends, arrivals = [], []
            for mask in range(1, N_DEV):
                px = 1 - x if mask & 4 else x
                py = 1 - y if mask & 2 else y
                pc = 1 - cc if mask & 1 else cc
                both = dict(send_sem=send_sems.at[7 * phase + mask - 1], recv_sem=recv_sems.at[7 * phase + mask - 1],
                            device_id=(px, py, pc), device_id_type=MESH)
                sends.append(pltpu.make_async_remote_copy(src_ref=ref.at[me], dst_ref=ref.at[me], **both))
                arrivals.append(pltpu.make_async_remote_copy(src_ref=ref.at[me], dst_ref=ref.at[4 * px + 2 * py + pc],
                                                             **both))
            for cp in sends:
                cp.start()
            for cp in arrivals:
                cp.wait_recv()
            for cp in sends:
                cp.wait_send()

        v = c_ref[...]
        call_ref[me] = v * _sig(v)
        exchange(call_ref, 0)
        c_all = jnp.concatenate([call_ref[d] for d in range(N_DEV)], axis=0)
        mod_ref[me] = jnp.dot(c_all, w_ref[...], precision=HIGHEST, preferred_element_type=F32)
        exchange(mod_ref, 1)

    return pl.pallas_call(
        body, name="ada_mod",
        out_shape=[jax.ShapeDtypeStruct((N_DEV, 1, D_MODEL), F32), jax.ShapeDtypeStruct((N_DEV, N_DEV, cols), F32)],
        scratch_shapes=[pltpu.SemaphoreType.DMA((14,)), pltpu.SemaphoreType.DMA((14,))],
        compiler_params=pltpu.CompilerParams(vmem_limit_bytes=VMEM_LIMIT_BYTES),
    )(c, w_shard)


def _ada_grad(c_all, dmod_shard):
    def body(c_ref, d_ref, o_ref):
        o_ref[...] = lax.dot_general(c_ref[...], d_ref[...], _DIMS["tn"], precision=HIGHEST,
                                     preferred_element_type=F32)

    return pl.pallas_call(
        body, name="ada_grad", out_shape=jax.ShapeDtypeStruct((D_MODEL, dmod_shard.shape[1]), F32),
        compiler_params=pltpu.CompilerParams(vmem_limit_bytes=VMEM_LIMIT_BYTES),
    )(c_all, dmod_shard)


ROWS = 256


def _rms(v):
    r = lax.rsqrt(jnp.mean(v * v, axis=-1, keepdims=True) + EPS)
    return v * r, r


def _rms_bwd(dxn, xn, r):
    return r * (dxn - xn * jnp.mean(dxn * xn, axis=-1, keepdims=True))


def _colsum(v):
    return jnp.sum(v, axis=0, keepdims=True)


def _pre_mix(x, mod6, g1, comm=None):
    seq = x.shape[0]

    def body(ins, outs, scratch):
        x_ref, mod_ref, g_ref = ins
        xn, _ = _rms(x_ref[...])
        y = xn * g_ref[...]
        outs[0][...] = (y * (1.0 + mod_ref[SC_M:SC_M + 1, :]) + mod_ref[SH_M:SH_M + 1, :]).astype(BF16)

    (h,), extra = _host_call(
        body, "pre_mix", grid=(seq // ROWS,),
        in_specs=[_tile(ROWS, D_MODEL), _full((6, D_MODEL)), _full((1, D_MODEL))], out_specs=[_tile(ROWS, D_MODEL)],
        out_shape=[jax.ShapeDtypeStruct((seq, D_MODEL), BF16)], scratch_shapes=[], args=[x, mod6, g1], comm=comm)
    return h, extra


def _post_mix_pre_ffn(x, mod6, g2, g3, rows):
    seq = x.shape[0]

    def fn(y, first, ins, outs):
        x_ref, mod_ref, g2_ref, g3_ref = ins
        x1_ref, h_ref = outs
        yn, _ = _rms(y)
        x1 = x_ref[...] + mod_ref[GT_M:GT_M + 1, :] * (yn * g2_ref[...])
        x1_ref[...] = x1
        xn, _ = _rms(x1)
        y3 = xn * g3_ref[...]
        h_ref[...] = (y3 * (1.0 + mod_ref[SC_F:SC_F + 1, :]) + mod_ref[SH_F:SH_F + 1, :]).astype(BF16)

    return _Epilogue(
        [x, mod6, g2, g3], [_row_tile(rows, D_MODEL), _whole((6, D_MODEL)), _whole((1, D_MODEL)), _whole((1, D_MODEL))],
        [jax.ShapeDtypeStruct((seq, D_MODEL), F32), jax.ShapeDtypeStruct((seq, D_MODEL), BF16)],
        [_row_tile(rows, D_MODEL), _row_tile(rows, D_MODEL)], fn, keep_product=True)


def _final(x1, target, mod6, g4, rows):
    seq = x1.shape[0]

    def fn(y, first, ins, outs):
        x1_ref, t_ref, mod_ref, g_ref = ins
        loss_ref, dout_ref, dyf_ref, small_ref = outs

        @pl.when(first)
        def _():
            loss_ref[...] = jnp.zeros_like(loss_ref)
            small_ref[...] = jnp.zeros_like(small_ref)

        gt = mod_ref[GT_F:GT_F + 1, :]
        g4v = g_ref[...]
        yn, r = _rms(y)
        out = x1_ref[...] + gt * (yn * g4v)
        err = out - t_ref[...]
        loss_ref[...] += jnp.sum(jnp.mean(err * err, axis=-1, keepdims=True))
        dout = err * (1.0 / D_MODEL)
        dout_ref[...] = dout
        small_ref[0:1, :] += _colsum(dout * gt * yn)
        small_ref[1:2, :] += _colsum(dout * (yn * g4v))
        dyf_ref[...] = _rms_bwd(dout * gt * g4v, yn, r).astype(BF16)

    return _Epilogue(
        [x1, target, mod6, g4],
        [_row_tile(rows, D_MODEL), _row_tile(rows, D_MODEL), _whole((6, D_MODEL)), _whole((1, D_MODEL))],
        [jax.ShapeDtypeStruct((1, 128), F32), jax.ShapeDtypeStruct((seq, D_MODEL), F32),
         jax.ShapeDtypeStruct((seq, D_MODEL), BF16), jax.ShapeDtypeStruct((8, D_MODEL), F32)],
        [_whole((1, 128)), _row_tile(rows, D_MODEL), _row_tile(rows, D_MODEL), _whole((8, D_MODEL))],
        fn, keep_product=False)


def _mid_bwd(x1, dout, ymix, mod6, g3, g2, rows):
    seq = x1.shape[0]

    def fn(dh, first, ins, outs):
        x1_ref, dout_ref, y_ref, mod_ref, g3_ref, g2_ref = ins
        dx1_ref, dy_ref, small_ref = outs

        @pl.when(first)
        def _():
            small_ref[...] = jnp.zeros_like(small_ref)

        g3v, g2v = g3_ref[...], g2_ref[...]
        xn, r3 = _rms(x1_ref[...])
        y3 = xn * g3v
        dy3 = dh * (1.0 + mod_ref[SC_F:SC_F + 1, :])
        small_ref[0:1, :] += _colsum(dy3 * xn)
        small_ref[1:2, :] += _colsum(dh * y3)
        small_ref[2:3, :] += _colsum(dh)
        dx1 = dout_ref[...] + _rms_bwd(dy3 * g3v, xn, r3)
        dx1_ref[...] = dx1
        gt = mod_ref[GT_M:GT_M + 1, :]
        yn, r2 = _rms(y_ref[...])
        small_ref[3:4, :] += _colsum(dx1 * gt * yn)
        small_ref[4:5, :] += _colsum(dx1 * (yn * g2v))
        dy_ref[...] = _rms_bwd(dx1 * gt * g2v, yn, r2).astype(BF16)

    return _Epilogue(
        [x1, dout, ymix, mod6, g3, g2],
        [_row_tile(rows, D_MODEL)] * 3 + [_whole((6, D_MODEL)), _whole((1, D_MODEL)), _whole((1, D_MODEL))],
        [jax.ShapeDtypeStruct((seq, D_MODEL), F32), jax.ShapeDtypeStruct((seq, D_MODEL), BF16),
         jax.ShapeDtypeStruct((8, D_MODEL), F32)],
        [_row_tile(rows, D_MODEL), _row_tile(rows, D_MODEL), _whole((8, D_MODEL))], fn, keep_product=False)


def _pre_mix_bwd(x, dx1, mod6, g1, rows):
    seq = x.shape[0]

    def fn(dh, first, ins, outs):
        x_ref, dx1_ref, mod_ref, g_ref = ins
        dx_ref, small_ref = outs

        @pl.when(first)
        def _():
            small_ref[...] = jnp.zeros_like(small_ref)

        g1v = g_ref[...]
        xn, r = _rms(x_ref[...])
        dy = dh * (1.0 + mod_ref[SC_M:SC_M + 1, :])
        small_ref[0:1, :] += _colsum(dy * xn)
        small_ref[1:2, :] += _colsum(dh * (xn * g1v))
        small_ref[2:3, :] += _colsum(dh)
        dx_ref[...] = dx1_ref[...] + _rms_bwd(dy * g1v, xn, r)

    return _Epilogue(
        [x, dx1, mod6, g1],
        [_row_tile(rows, D_MODEL), _row_tile(rows, D_MODEL), _whole((6, D_MODEL)), _whole((1, D_MODEL))],
        [jax.ShapeDtypeStruct((seq, D_MODEL), F32), jax.ShapeDtypeStruct((8, D_MODEL), F32)],
        [_row_tile(rows, D_MODEL), _whole((8, D_MODEL))], fn, keep_product=False)


def _toeplitz_onehot(shape, offset_axis, top):
    m = lax.broadcasted_iota(jnp.int32, shape, offset_axis)
    i = lax.broadcasted_iota(jnp.int32, shape, 1 - offset_axis)
    return (i == jnp.clip(top - m, -MAX_REL, MAX_REL) + MAX_REL).astype(F32)


def _bias_table(rel_bias):
    width = GROUP_Q + GROUP_K

    def body(rb_ref, o_ref, t_ref):
        t_ref[...] = jnp.dot(rb_ref[...], _toeplitz_onehot((N_REL, width), 1, GROUP_K - 1), precision=HIGHEST,
                             preferred_element_type=F32)
        lane = lax.broadcasted_iota(jnp.int32, (N_HEADS, GROUP_K), 1)
        for r in range(GROUP_Q):
            first_key = (r // CHUNK) * CHUNK
            band = jnp.logical_and(lane >= first_key, lane < first_key + BAND)
            o_ref[r] = jnp.where(band, t_ref[:, GROUP_Q - 1 - r:GROUP_Q - 1 - r + GROUP_K], NEG_INF)

    return pl.pallas_call(
        body, name="bias_table", out_shape=jax.ShapeDtypeStruct((GROUP_Q, N_HEADS, GROUP_K), F32),
        scratch_shapes=[pltpu.VMEM((N_HEADS, width), F32)],
    )(rel_bias)


def _bias_grad(dbias_q):
    def body(d_ref, o_ref, t_ref):
        t_ref[...] = jnp.zeros_like(t_ref)
        for qi in range(CHUNK):
            t_ref[:, CHUNK - 1 - qi:CHUNK - 1 - qi + BAND] += d_ref[qi]
        o_ref[...] = jnp.dot(t_ref[...], _toeplitz_onehot((TOEPLITZ, N_REL), 0, BAND - 1), precision=HIGHEST,
                             preferred_element_type=F32)

    return pl.pallas_call(
        body, name="bias_grad", out_shape=jax.ShapeDtypeStruct((N_HEADS, N_REL), F32),
        scratch_shapes=[pltpu.VMEM((N_HEADS, TOEPLITZ), F32)],
    )(dbias_q)


def _resident_copies(qkv_hbm, t_hbm, k_ref, v_ref, t_ref, sems):
    inside = pl.ds(PAD_ROWS, qkv_hbm.shape[0])
    return (pltpu.make_async_copy(qkv_hbm.at[:, pl.ds(D_ATTN, D_ATTN)], k_ref.at[inside, :], sems.at[0]),
            pltpu.make_async_copy(qkv_hbm.at[:, pl.ds(2 * D_ATTN, D_ATTN)], v_ref.at[inside, :], sems.at[1]),
            pltpu.make_async_copy(t_hbm, t_ref, sems.at[2]))


def _start_resident(copies, k_ref, v_ref):
    k_ref[0:PAD_ROWS, :] = jnp.zeros((PAD_ROWS, D_ATTN), BF16)
    v_ref[0:PAD_ROWS, :] = jnp.zeros((PAD_ROWS, D_ATTN), BF16)
    for cp in copies:
        cp.start()


def _softmax_rows(s_ref, t_ref, h, before_start, rows):
    s = s_ref[rows, :] * (HEAD_DIM ** -0.5) + t_ref[h, rows, :] + before_start
    e = jnp.exp(s - jnp.max(s, axis=-1, keepdims=True))
    return e / jnp.sum(e, axis=-1, keepdims=True)


def _before_start(g):
    kj = lax.broadcasted_iota(jnp.int32, (8, GROUP_K), 1)
    return jnp.where(kj >= PAD_ROWS - g * GROUP_Q, 0.0, NEG_INF)


def _attn_fwd(qkv, table, comm=None):
    seq = qkv.shape[0]

    def body(ins, outs, scratch):
        q_ref, qkv_hbm, t_hbm = ins
        (o_ref,) = outs
        k_ref, v_ref, t_ref, s_ref, p_ref, sems = scratch
        g = pl.program_id(0)
        load_k, load_v, load_t = _resident_copies(qkv_hbm, t_hbm, k_ref, v_ref, t_ref, sems)

        @pl.when(g == 0)
        def _():
            _start_resident((load_k, load_v, load_t), k_ref, v_ref)
            load_k.wait()

        window = pl.ds(pl.multiple_of(g * GROUP_Q, GROUP_Q), GROUP_K)
        before_start = _before_start(g)
        for h in range(N_HEADS):
            cols = slice(h * HEAD_DIM, (h + 1) * HEAD_DIM)
            buf = h % 2
            s_ref[buf] = lax.dot_general(q_ref[:, cols], k_ref[window, cols], _DIMS["nt"],
                                         preferred_element_type=F32)
            if h == 0:
                pl.when(g == 0)(load_t.wait)
            for row in range(0, GROUP_Q, SOFTMAX_ROWS):
                halves = [_softmax_rows(s_ref.at[buf], t_ref, h, before_start, slice(r, r + 8))
                          for r in (row, row + 8)]
                p_ref[buf, row:row + SOFTMAX_ROWS, :] = jnp.concatenate(halves, axis=0).astype(BF16)
            if h == 0:
                pl.when(g == 0)(load_v.wait)
            o_ref[:, cols] = jnp.dot(p_ref[buf], v_ref[window, cols], preferred_element_type=F32).astype(BF16)

    (ao,), extra = _host_call(
        body, "attn_fwd", grid=(seq // GROUP_Q,),
        in_specs=[_tile(GROUP_Q, D_ATTN), ANY, ANY], out_specs=[_tile(GROUP_Q, D_ATTN)],
        out_shape=[jax.ShapeDtypeStruct((seq, D_ATTN), BF16)],
        scratch_shapes=[pltpu.VMEM((seq + PAD_ROWS, D_ATTN), BF16), pltpu.VMEM((seq + PAD_ROWS, D_ATTN), BF16),
                        pltpu.VMEM(table.shape, F32),
                        pltpu.VMEM((2, GROUP_Q, GROUP_K), F32), pltpu.VMEM((2, GROUP_Q, GROUP_K), BF16),
                        pltpu.SemaphoreType.DMA((3,))],
        args=[qkv, qkv, table], comm=comm)
    return ao, extra


def _attn_bwd(qkv, table, dao, comm=None):
    seq = qkv.shape[0]
    n_groups = seq // GROUP_Q
    fold_w = GROUP_K + (GROUP - 1) * CHUNK

    def body(ins, outs, scratch):
        q_ref, do_ref, qkv_hbm, t_hbm = ins
        dq_ref, dkt_hbm, dvt_hbm, db_ref, cs_ref = outs
        k_ref, v_ref, t_ref, db_acc, dkt_acc, dvt_acc, s_ref, dp_ref, p_ref, ds_ref, sems = scratch
        g = pl.program_id(0)

        load_k, load_v, load_t = _resident_copies(qkv_hbm, t_hbm, k_ref, v_ref, t_ref, sems)

        @pl.when(g == 0)
        def _():
            _start_resident((load_k, load_v, load_t), k_ref, v_ref)
            db_acc[...] = jnp.zeros_like(db_acc)
            dkt_acc[...] = jnp.zeros_like(dkt_acc)
            dvt_acc[...] = jnp.zeros_like(dvt_acc)
            cs_ref[...] = jnp.zeros_like(cs_ref)
            load_k.wait()
            load_v.wait()

        window = pl.ds(pl.multiple_of(g * GROUP_Q, GROUP_Q), GROUP_K)
        before_start = _before_start(g)
        for h in range(N_HEADS):
            cols = slice(h * HEAD_DIM, (h + 1) * HEAD_DIM)
            buf = h % 2
            qh, doh = q_ref[:, cols], do_ref[:, cols]
            kh, vh = k_ref[window, cols], v_ref[window, cols]
            s_ref[buf] = lax.dot_general(qh, kh, _DIMS["nt"], preferred_element_type=F32)
            dp_ref[buf] = lax.dot_general(doh, vh, _DIMS["nt"], preferred_element_type=F32)
            if h == 0:
                pl.when(g == 0)(load_t.wait)
            for row in range(0, GROUP_Q, SOFTMAX_ROWS):
                p_halves, ds_halves = [], []
                for r in (row, row + 8):
                    p = _softmax_rows(s_ref.at[buf], t_ref, h, before_start, slice(r, r + 8))
                    dp = dp_ref[buf, r:r + 8, :]
                    ds = p * (dp - jnp.sum(dp * p, axis=-1, keepdims=True))
                    chunk = r // CHUNK
                    shift = (GROUP - 1 - chunk) * CHUNK
                    db_acc[h, r - chunk * CHUNK:r - chunk * CHUNK + 8, shift:shift + GROUP_K] += ds
                    p_halves.append(p)
                    ds_halves.append(ds * (HEAD_DIM ** -0.5))
                p_ref[buf, row:row + SOFTMAX_ROWS, :] = jnp.concatenate(p_halves, axis=0).astype(BF16)
                ds_ref[buf, row:row + SOFTMAX_ROWS, :] = jnp.concatenate(ds_halves, axis=0).astype(BF16)
            dq_ref[:, cols] = jnp.dot(ds_ref[buf], kh, preferred_element_type=F32).astype(BF16)
            dkt_acc[cols, window] += lax.dot_general(qh, ds_ref[buf], _DIMS["tn"], preferred_element_type=F32)
            dvt_acc[cols, window] += lax.dot_general(doh, p_ref[buf], _DIMS["tn"], preferred_element_type=F32)
        cs_ref[0:1, :] += _colsum(dq_ref[...].astype(F32))

        @pl.when(g == n_groups - 1)
        def _():
            lo = (GROUP - 1) * CHUNK
            for h in range(N_HEADS):
                db_ref[h] = db_acc[h, :, lo:lo + BAND]
            inside = pl.ds(PAD_ROWS, seq)
            on_diagonal = (lax.broadcasted_iota(jnp.int32, (D_ATTN, D_ATTN), 0)
                           == lax.broadcasted_iota(jnp.int32, (D_ATTN, D_ATTN), 1))
            for row, acc in ((1, dkt_acc), (2, dvt_acc)):
                column = jnp.sum(acc[:, inside], axis=1, keepdims=True)
                cs_ref[row:row + 1, :] = _colsum(jnp.where(on_diagonal, column, 0.0))
            out_k = pltpu.make_async_copy(dkt_acc.at[:, inside], dkt_hbm, sems.at[0])
            out_v = pltpu.make_async_copy(dvt_acc.at[:, inside], dvt_hbm, sems.at[1])
            out_k.start()
            out_v.start()
            out_k.wait()
            out_v.wait()

    t_shape = (D_ATTN, seq + PAD_ROWS)
    outs, extra = _host_call(
        body, "attn_bwd", grid=(n_groups,),
        in_specs=[_tile(GROUP_Q, D_ATTN), _tile(GROUP_Q, D_ATTN), ANY, ANY],
        out_specs=[_tile(GROUP_Q, D_ATTN), ANY, ANY, _full((N_HEADS, CHUNK, BAND)), _full((8, D_ATTN))],
        out_shape=[jax.ShapeDtypeStruct((seq, D_ATTN), BF16), jax.ShapeDtypeStruct((D_ATTN, seq), F32),
                   jax.ShapeDtypeStruct((D_ATTN, seq), F32), jax.ShapeDtypeStruct((N_HEADS, CHUNK, BAND), F32),
                   jax.ShapeDtypeStruct((8, D_ATTN), F32)],
        scratch_shapes=[pltpu.VMEM((seq + PAD_ROWS, D_ATTN), BF16), pltpu.VMEM((seq + PAD_ROWS, D_ATTN), BF16),
                        pltpu.VMEM(table.shape, F32), pltpu.VMEM((N_HEADS, CHUNK, fold_w), F32), pltpu.VMEM(t_shape, F32),
                        pltpu.VMEM(t_shape, F32), pltpu.VMEM((2, GROUP_Q, GROUP_K), F32),
                        pltpu.VMEM((2, GROUP_Q, GROUP_K), F32), pltpu.VMEM((2, GROUP_Q, GROUP_K), BF16),
                        pltpu.VMEM((2, GROUP_Q, GROUP_K), BF16), pltpu.SemaphoreType.DMA((3,))],
        args=[qkv, dao, qkv, table], comm=comm)
    return outs, extra


def _assemble_dz(dq, dkt, dvt, dglu_a, dglu_b, dga, dgb):
    seq = dq.shape[0]
    rows = 512
    transposed = pl.BlockSpec((D_ATTN, rows), lambda i: (0, i))

    def body(dq_ref, dkt_ref, dvt_ref, da_ref, db_ref, dga_ref, dgb_ref, o_ref):
        o_ref[:, 0:D_ATTN] = dq_ref[...]
        o_ref[:, D_ATTN:2 * D_ATTN] = dkt_ref[...].T.astype(BF16)
        o_ref[:, 2 * D_ATTN:3 * D_ATTN] = dvt_ref[...].T.astype(BF16)
        off = 3 * D_ATTN
        for ref in (da_ref, db_ref, dga_ref, dgb_ref):
            width = ref.shape[1]
            o_ref[:, off:off + width] = ref[...]
            off += width

    width = 3 * D_ATTN + 2 * D_CONV + 2 * D_MODEL
    return pl.pallas_call(
        body, name="assemble_dz", out_shape=jax.ShapeDtypeStruct((seq, width), BF16), grid=(seq // rows,),
        in_specs=[_tile(rows, D_ATTN), transposed, transposed, _tile(rows, D_CONV), _tile(rows, D_CONV),
                  _tile(rows, D_MODEL), _tile(rows, D_MODEL)],
        out_specs=_tile(rows, width), compiler_params=_cparams(1),
    )(dq, dkt, dvt, dglu_a, dglu_b, dga, dgb)


CONV_ROWS = 256


def _ln_silu(u1, g, b):
    mu = jnp.mean(u1, axis=-1, keepdims=True)
    xc = u1 - mu
    rs = lax.rsqrt(jnp.mean(xc * xc, axis=-1, keepdims=True) + EPS)
    xhat = xc * rs
    u2 = xhat * g + b
    return xhat, rs, u2


def _glu_into(s_ref, a_ref, b_ref, ah_ref, bh_ref, first):
    halo = ah_ref[...] * _sig(bh_ref[...])
    s_ref[0:CONV_HALO, :] = jnp.where(first, 0.0, halo)
    s_ref[CONV_HALO:CONV_HALO + CONV_ROWS, :] = a_ref[...] * _sig(b_ref[...])


CONV_LANES = 128
CONV_TILES = CONV_ROWS // 8


def _lag_weights(w_ref, lanes):
    return {e: jnp.broadcast_to(w_ref[CONV_K - 1 - e:CONV_K - e, lanes], (8, CONV_LANES)) for e in range(CONV_K)}


def _class_sums(w, tiles, k):
    total = None
    for a, tile in enumerate(tiles):
        if 8 * a + k < CONV_K:
            term = w[8 * a + k] * tile
            total = term if total is None else total + term
    return total


def _conv_back(src_ref, first_tile, w, lanes, row_id, emit):
    before = None
    for m in range(-1, CONV_TILES):
        tiles = [src_ref[8 * (first_tile + m - a):8 * (first_tile + m - a) + 8, lanes] for a in range(4)]
        rolled = [None] + [pltpu.roll(_class_sums(w, tiles, k), k, 0) for k in range(1, 8)]
        if m >= 0:
            out = _class_sums(w, tiles, 0)
            for k in range(1, 8):
                out = out + jnp.where(row_id < k, before[k], rolled[k])
            emit(m, out)
        before = rolled


def _conv_ahead(src_ref, w, lanes, row_id, emit):
    before = None
    for m in range(CONV_TILES + 1):
        tiles = [src_ref[8 * (m + a):8 * (m + a) + 8, lanes] for a in range(4)]
        rolled = [None] + [pltpu.roll(_class_sums(w, tiles, k), 8 - k, 0) for k in range(1, 8)]
        if m >= 1:
            out = before[0]
            for k in range(1, 8):
                out = out + jnp.where(row_id < 8 - k, before[k], rolled[k])
            emit(m - 1, out)
        before = [_class_sums(w, tiles, 0) if m < CONV_TILES else None] + rolled[1:]


def _conv_weight_sums(d_ref, s_ref, lanes, row_id, whole_shifts):
    zero = jnp.zeros((8, CONV_LANES), F32)
    sums = {8 * a + k: zero for a in whole_shifts for k in range(8) if 8 * a + k < CONV_K}

    def d_tile(m):
        return d_ref[8 * m:8 * m + 8, lanes] if 0 <= m < CONV_TILES else zero

    rolled = [None] + [zero] * 7
    for m in range(-1, CONV_TILES):
        cur, nxt = d_tile(m), d_tile(m + 1)
        rolled_next = [None] + [pltpu.roll(nxt, 8 - k, 0) for k in range(1, 8)]
        shifted = [cur] + [jnp.where(row_id < 8 - k, rolled[k], rolled_next[k]) for k in range(1, 8)]
        for a in whole_shifts:
            tile = s_ref[8 * (CONV_HALO // 8 + m - a):8 * (CONV_HALO // 8 + m - a) + 8, lanes]
            for k in range(8):
                if 8 * a + k < CONV_K and not (m < 0 and k == 0):
                    sums[8 * a + k] = sums[8 * a + k] + shifted[k] * tile
        rolled = rolled_next
    return sums


def _conv_fwd(zr, w_dw, b_dw, g_ln, b_ln, comm=None):
    seq = zr.shape[0]

    def body(a_ref, b_ref, ah_ref, bh_ref, w_ref, bias_ref, g_ref, bl_ref, u1_ref, u3_ref, s_ref):
        _glu_into(s_ref, a_ref, b_ref, ah_ref, bh_ref, pl.program_id(0) == 0)
        row_id = lax.broadcasted_iota(jnp.int32, (8, CONV_LANES), 0)
        for lo in range(0, D_CONV, CONV_LANES):
            lanes = slice(lo, lo + CONV_LANES)
            bias = jnp.broadcast_to(bias_ref[:, lanes], (8, CONV_LANES))

            def emit(m, out, lanes=lanes, bias=bias):
                u1_ref[8 * m:8 * m + 8, lanes] = out + bias

            _conv_back(s_ref, CONV_HALO // 8, _lag_weights(w_ref, lanes), lanes, row_id, emit)
        _, _, u2 = _ln_silu(u1_ref[...], g_ref[...], bl_ref[...])
        u3_ref[...] = (u2 * _sig(u2)).astype(BF16)

    return _host_call(
        lambda ins, outs, scratch: body(*ins, *outs, *scratch), "conv_fwd", grid=(seq // CONV_ROWS,),
        in_specs=[_tile(CONV_ROWS, D_CONV, 0), _tile(CONV_ROWS, D_CONV, 1),
                  _prev(CONV_HALO, D_CONV, CONV_ROWS, 0), _prev(CONV_HALO, D_CONV, CONV_ROWS, 1),
                  _full((CONV_K, D_CONV)), _full((1, D_CONV)), _full((1, D_CONV)), _full((1, D_CONV))],
        out_specs=[_tile(CONV_ROWS, D_CONV), _tile(CONV_ROWS, D_CONV)],
        out_shape=[jax.ShapeDtypeStruct((seq, D_CONV), F32), jax.ShapeDtypeStruct((seq, D_CONV), BF16)],
        scratch_shapes=[pltpu.VMEM((CONV_HALO + CONV_ROWS, D_CONV), F32)],
        args=[zr, zr, zr, zr, w_dw, b_dw, g_ln, b_ln], comm=comm)


def _conv_bwd(zr, u1, du3, w_dw, g_ln, b_ln, comm=None):
    seq = zr.shape[0]
    n_tiles = seq // CONV_ROWS
    n_halo = seq // CONV_HALO
    ext = CONV_ROWS + CONV_HALO

    def body(a_ref, b_ref, ah_ref, bh_ref, u1_ref, u1n_ref, d3_ref, d3n_ref, w_ref, g_ref, bl_ref,
             da_ref, db_ref, dw_ref, small_ref, s_ref, d_ref, du0_ref):
        i = pl.program_id(0)

        @pl.when(i == 0)
        def _():
            dw_ref[...] = jnp.zeros_like(dw_ref)
            small_ref[...] = jnp.zeros_like(small_ref)

        _glu_into(s_ref, a_ref, b_ref, ah_ref, bh_ref, i == 0)
        gv, bv = g_ref[...], bl_ref[...]

        def du1_of(u1, d3):
            xhat, rs, u2 = _ln_silu(u1, gv, bv)
            sg = _sig(u2)
            du2 = d3 * (sg * (1.0 + u2 * (1.0 - sg)))
            dxh = du2 * gv
            du1 = rs * (dxh - jnp.mean(dxh, axis=-1, keepdims=True)
                        - xhat * jnp.mean(dxh * xhat, axis=-1, keepdims=True))
            return du1, du2, xhat

        du1, du2, xhat = du1_of(u1_ref[...], d3_ref[...])
        du1n, _, _ = du1_of(u1n_ref[...], d3n_ref[...])
        d_ref[0:CONV_ROWS, :] = du1
        d_ref[CONV_ROWS:ext, :] = jnp.where(i == n_tiles - 1, 0.0, du1n)
        small_ref[0:1, :] += _colsum(du1)
        small_ref[1:2, :] += _colsum(du2 * xhat)
        small_ref[2:3, :] += _colsum(du2)
        row_id = lax.broadcasted_iota(jnp.int32, (8, CONV_LANES), 0)
        for lo in range(0, D_CONV, CONV_LANES):
            lanes = slice(lo, lo + CONV_LANES)

            def emit(m, out, lanes=lanes):
                du0_ref[8 * m:8 * m + 8, lanes] = out

            _conv_ahead(d_ref, _lag_weights(w_ref, lanes), lanes, row_id, emit)
            for whole_shifts in ((0, 1), (2, 3)):
                for e, total in _conv_weight_sums(d_ref, s_ref, lanes, row_id, whole_shifts).items():
                    dw_ref[CONV_K - 1 - e:CONV_K - e, lanes] += _colsum(total)
        du0 = du0_ref[...]
        sb = _sig(b_ref[...])
        da = du0 * sb
        dbv = du0 * a_ref[...] * sb * (1.0 - sb)
        da_ref[...] = da.astype(BF16)
        db_ref[...] = dbv.astype(BF16)
        small_ref[3:4, :] += _colsum(da)
        small_ref[4:5, :] += _colsum(dbv)

    return _host_call(
        lambda ins, outs, scratch: body(*ins, *outs, *scratch), "conv_bwd", grid=(n_tiles,),
        in_specs=[_tile(CONV_ROWS, D_CONV, 0), _tile(CONV_ROWS, D_CONV, 1),
                  _prev(CONV_HALO, D_CONV, CONV_ROWS, 0), _prev(CONV_HALO, D_CONV, CONV_ROWS, 1),
                  _tile(CONV_ROWS, D_CONV), _next(CONV_HALO, D_CONV, CONV_ROWS, n_halo),
                  _tile(CONV_ROWS, D_CONV), _next(CONV_HALO, D_CONV, CONV_ROWS, n_halo),
                  _full((CONV_K, D_CONV)), _full((1, D_CONV)), _full((1, D_CONV))],
        out_specs=[_tile(CONV_ROWS, D_CONV), _tile(CONV_ROWS, D_CONV), _full((CONV_HALO, D_CONV)),
                   _full((8, D_CONV))],
        out_shape=[jax.ShapeDtypeStruct((seq, D_CONV), BF16), jax.ShapeDtypeStruct((seq, D_CONV), BF16),
                   jax.ShapeDtypeStruct((CONV_HALO, D_CONV), F32), jax.ShapeDtypeStruct((8, D_CONV), F32)],
        scratch_shapes=[pltpu.VMEM((ext, D_CONV), F32), pltpu.VMEM((ext, D_CONV), F32),
                        pltpu.VMEM((CONV_ROWS, D_CONV), F32)],
        args=[zr, zr, zr, zr, u1, u1, du3, du3, w_dw, g_ln, b_ln], comm=comm)


MERGE_ROWS = 256


def _merge_fwd(ao, u3, zr, w_ao, w_co, b_co):
    seq = ao.shape[0]

    def body(ao_ref, u3_ref, ga_ref, gb_ref, wa_ref, wc_ref, bc_ref, y_ref, a_ref, cb_ref):
        a = jnp.dot(ao_ref[...], wa_ref[...], preferred_element_type=F32)
        cb = jnp.dot(u3_ref[...], wc_ref[...], preferred_element_type=F32) + bc_ref[...]
        a_ref[...] = a
        cb_ref[...] = cb
        y_ref[...] = (_sig(ga_ref[...]) * a + _sig(gb_ref[...]) * cb).astype(BF16)

    f32_out = jax.ShapeDtypeStruct((seq, D_MODEL), F32)
    return pl.pallas_call(
        body, name="merge_fwd",
        out_shape=[jax.ShapeDtypeStruct((seq, D_MODEL), BF16), f32_out, f32_out],
        grid=(seq // MERGE_ROWS,),
        in_specs=[_tile(MERGE_ROWS, D_ATTN), _tile(MERGE_ROWS, D_CONV), _tile(MERGE_ROWS, D_MODEL, 1),
                  _tile(MERGE_ROWS, D_MODEL, 2), _full(w_ao.shape), _full(w_co.shape), _full((1, D_MODEL))],
        out_specs=[_tile(MERGE_ROWS, D_MODEL)] * 3, compiler_params=_cparams(1),
    )(ao, u3, zr, zr, w_ao, w_co, b_co)


def _merge_bwd(a, cb, zr, rows):
    seq = a.shape[0]

    def fn(dy_v, first, ins, outs):
        a_ref, cb_ref, ga_ref, gb_ref = ins
        da_ref, dcb_ref, dga_ref, dgb_ref, small_ref = outs

        @pl.when(first)
        def _():
            small_ref[...] = jnp.zeros_like(small_ref)

        sa, sb = _sig(ga_ref[...]), _sig(gb_ref[...])
        dcb = dy_v * sb
        dga = dy_v * a_ref[...] * sa * (1.0 - sa)
        dgb = dy_v * cb_ref[...] * sb * (1.0 - sb)
        da_ref[...] = (dy_v * sa).astype(BF16)
        dcb_ref[...] = dcb.astype(BF16)
        dga_ref[...] = dga.astype(BF16)
        dgb_ref[...] = dgb.astype(BF16)
        small_ref[0:1, :] += _colsum(dga)
        small_ref[1:2, :] += _colsum(dgb)
        small_ref[2:3, :] += _colsum(dcb)

    bf = jax.ShapeDtypeStruct((seq, D_MODEL), BF16)
    gate = lambda col: pl.BlockSpec((rows, D_MODEL), lambda i, j: (i, col))
    return _Epilogue(
        [a, cb, zr, zr], [_row_tile(rows, D_MODEL), _row_tile(rows, D_MODEL), gate(1), gate(2)],
        [bf, bf, bf, bf, jax.ShapeDtypeStruct((8, D_MODEL), F32)],
        [_row_tile(rows, D_MODEL)] * 4 + [_whole((8, D_MODEL))], fn, keep_product=False)


FFN_ROWS = 2048
FFN_BLOCKS = D_FF // FFN_COLS
GELU_C = math.sqrt(2.0 / math.pi)


def _gelu(v):
    t = jnp.tanh(GELU_C * (v + 0.044715 * (v * v * v)))
    return 0.5 * v * (1.0 + t), t


def _gelu_grad(v, t):
    return 0.5 * (1.0 + t) + 0.5 * v * (1.0 - t * t) * (GELU_C * (1.0 + 3.0 * 0.044715 * (v * v)))


def _sublane_rows(ref, n):
    return [jnp.broadcast_to(ref[r:r + 1, :], (8, FFN_COLS)) for r in range(n)]


def _rolls(tile, shifts):
    return tuple(pltpu.roll(tile, s, 0) for s in shifts)


def _behind(prev_rolls, cur, row_id):
    rolls = _rolls(cur, (1, 2))
    x1 = jnp.where(row_id < 1, prev_rolls[0], rolls[0])
    x2 = jnp.where(row_id < 2, prev_rolls[1], rolls[1])
    return (x2, x1, cur), rolls


def _ahead(cur_rolls, next_rolls, row_id):
    return (jnp.where(row_id < 7, cur_rolls[0], next_rolls[0]), jnp.where(row_id < 6, cur_rolls[1], next_rolls[1]))


def _conv3(taps, w, bias):
    return w[0] * taps[0] + w[1] * taps[1] + w[2] * taps[2] + bias


def _ffn_specs(rows):
    tile = lambda off: pl.BlockSpec((rows, FFN_COLS), lambda j, i: (i, j + off))
    prev = lambda off: pl.BlockSpec((FFN_HALO, FFN_COLS),
                                    lambda j, i: (jnp.maximum(i * (rows // FFN_HALO) - 1, 0), j + off))
    wgt = lambda off: pl.BlockSpec((3, FFN_COLS), lambda j, i: (0, j + off))
    vec = lambda off: pl.BlockSpec((1, FFN_COLS), lambda j, i: (0, j + off))
    return tile, prev, wgt, vec


def _ffn_act(up, w_dw, b_dw):
    seq = up.shape[0]
    tile, prev, wgt, vec = _ffn_specs(FFN_ROWS)

    def body(v_ref, g_ref, vp_ref, gp_ref, wv_ref, wg_ref, bv_ref, bg_ref, act_ref):
        first = pl.program_id(1) == 0
        row_id = lax.broadcasted_iota(jnp.int32, (8, FFN_COLS), 0)
        wv, wg = _sublane_rows(wv_ref, 3), _sublane_rows(wg_ref, 3)
        (bv,), (bg,) = _sublane_rows(bv_ref, 1), _sublane_rows(bg_ref, 1)
        rolls_v = _rolls(jnp.where(first, 0.0, vp_ref[...]), (1, 2))
        rolls_g = _rolls(jnp.where(first, 0.0, gp_ref[...]), (1, 2))
        for row in range(0, FFN_ROWS, 16):
            halves = []
            for r in (row, row + 8):
                taps_v, rolls_v = _behind(rolls_v, v_ref[r:r + 8, :], row_id)
                taps_g, rolls_g = _behind(rolls_g, g_ref[r:r + 8, :], row_id)
                halves.append(_gelu(_conv3(taps_g, wg, bg))[0] * _conv3(taps_v, wv, bv))
            act_ref[row:row + 16, :] = jnp.concatenate(halves, axis=0).astype(BF16)

    return pl.pallas_call(
        body, name="ffn_act", out_shape=jax.ShapeDtypeStruct((seq, D_FF), BF16),
        grid=(FFN_BLOCKS, seq // FFN_ROWS),
        in_specs=[tile(0), tile(FFN_BLOCKS), prev(0), prev(FFN_BLOCKS), wgt(0), wgt(FFN_BLOCKS),
                  vec(0), vec(FFN_BLOCKS)],
        out_specs=tile(0), compiler_params=_cparams(2),
    )(up, up, up, up, w_dw, w_dw, b_dw, b_dw)


def _ffn_act_bwd(up, dact, w_dw, b_dw, comm=None):
    seq = up.shape[0]
    n_tiles = seq // FFN_ROWS
    n_halo = seq // FFN_HALO
    tile, prev, wgt, vec = _ffn_specs(FFN_ROWS)
    nxt = lambda off: pl.BlockSpec(
        (FFN_HALO, FFN_COLS), lambda j, i: (jnp.minimum((i + 1) * (FFN_ROWS // FFN_HALO), n_halo - 1), j + off))
    acc = lambda off: pl.BlockSpec((8, FFN_COLS), lambda j, i: (0, j + off))

    def body(v_ref, g_ref, vp_ref, gp_ref, vn_ref, gn_ref, da_ref, dan_ref, wv_ref, wg_ref, bv_ref, bg_ref,
             dv_out, dg_out, dwv_ref, dwg_ref, dbv_ref, dbg_ref):
        i = pl.program_id(1)
        first, last = i == 0, i == n_tiles - 1

        @pl.when(first)
        def _():
            for r in (dwv_ref, dwg_ref, dbv_ref, dbg_ref):
                r[...] = jnp.zeros_like(r)

        row_id = lax.broadcasted_iota(jnp.int32, (8, FFN_COLS), 0)
        wv, wg = _sublane_rows(wv_ref, 3), _sublane_rows(wg_ref, 3)
        (bv,), (bg,) = _sublane_rows(bv_ref, 1), _sublane_rows(bg_ref, 1)
        zero = jnp.zeros((8, FFN_COLS), F32)
        sums_v, sums_g = [zero] * 4, [zero] * 4
        rolls_v = _rolls(jnp.where(first, 0.0, vp_ref[...]), (1, 2))
        rolls_g = _rolls(jnp.where(first, 0.0, gp_ref[...]), (1, 2))
        behind = None
        done_v, done_g = [], []

        def grads(v_tile, g_tile, dact, rolls_v, rolls_g):
            taps_v, rolls_v = _behind(rolls_v, v_tile, row_id)
            taps_g, rolls_g = _behind(rolls_g, g_tile, row_id)
            val, gate = _conv3(taps_v, wv, bv), _conv3(taps_g, wg, bg)
            gel, t = _gelu(gate)
            return dact * gel, dact * val * _gelu_grad(gate, t), taps_v, taps_g, rolls_v, rolls_g

        def finish(tile, nxt, row):
            for (d, d_rolls), (_, n_rolls), w, done, o_ref in ((tile[0], nxt[0], wv, done_v, dv_out),
                                                               (tile[1], nxt[1], wg, done_g, dg_out)):
                d1, d2 = _ahead(d_rolls, n_rolls, row_id)
                done.append(w[2] * d + w[1] * d1 + w[0] * d2)
                if len(done) == 2:
                    o_ref[row - 16:row, :] = jnp.concatenate(done, axis=0).astype(BF16)
                    done.clear()

        for row in range(0, FFN_ROWS, 16):
            dact16 = da_ref[row:row + 16, :].astype(F32)
            for r, dact in ((row, dact16[0:8, :]), (row + 8, dact16[8:16, :])):
                dval, dgate, taps_v, taps_g, rolls_v, rolls_g = grads(v_ref[r:r + 8, :], g_ref[r:r + 8, :], dact,
                                                                      rolls_v, rolls_g)
                sums_v = [s + dval * x for s, x in zip(sums_v, taps_v)] + [sums_v[3] + dval]
                sums_g = [s + dgate * x for s, x in zip(sums_g, taps_g)] + [sums_g[3] + dgate]
                tile = ((dval, _rolls(dval, (7, 6))), (dgate, _rolls(dgate, (7, 6))))
                if behind is not None:
                    finish(behind, tile, r)
                behind = tile
        dact_next = jnp.where(last, 0.0, dan_ref[...].astype(F32)[0:FFN_HALO, :])
        dval, dgate, *_ = grads(vn_ref[...], gn_ref[...], dact_next, rolls_v, rolls_g)
        finish(behind, ((dval, _rolls(dval, (7, 6))), (dgate, _rolls(dgate, (7, 6)))), FFN_ROWS)
        for sums, dw_ref, db_ref in ((sums_v, dwv_ref, dbv_ref), (sums_g, dwg_ref, dbg_ref)):
            for tap in range(3):
                dw_ref[tap:tap + 1, :] += _colsum(sums[tap])
            db_ref[0:1, :] += _colsum(sums[3])

    half = jax.ShapeDtypeStruct((seq, D_FF), BF16)
    acc_shape = jax.ShapeDtypeStruct((8, D_FF), F32)
    return _host_call(
        lambda ins, outs, scratch: body(*ins, *outs, *scratch), "ffn_act_bwd", grid=(FFN_BLOCKS, n_tiles),
        in_specs=[tile(0), tile(FFN_BLOCKS), prev(0), prev(FFN_BLOCKS), nxt(0), nxt(FFN_BLOCKS),
                  tile(0), pl.BlockSpec((16, FFN_COLS), lambda j, i: (
                      jnp.minimum((i + 1) * (FFN_ROWS // 16), seq // 16 - 1), j)),
                  wgt(0), wgt(FFN_BLOCKS), vec(0), vec(FFN_BLOCKS)],
        out_specs=[tile(0), tile(0), acc(0), acc(0), acc(0), acc(0)],
        out_shape=[half, half, acc_shape, acc_shape, acc_shape, acc_shape],
        scratch_shapes=[], args=[up, up, up, up, up, up, dact, dact, w_dw, w_dw, b_dw, b_dw], comm=comm)


def _cols_to_blocks(full_cols):
    k, n8 = full_cols.shape
    return jnp.transpose(full_cols.reshape(k, N_DEV, n8 // N_DEV), (1, 0, 2))


def _rows_to_blocks(full_rows):
    r8, n = full_rows.shape
    return full_rows.reshape(N_DEV, r8 // N_DEV, n)


def _blocks_to_cols(gathered):
    _, k, n = gathered.shape
    return jnp.transpose(gathered, (1, 0, 2)).reshape(k, N_DEV * n)


def kernel(x, c, w_ada, b_ada, g_pre_mix, g_post_mix, w_in, b_in, rel_bias, w_attn_o, w_dw_conv, b_dw_conv, g_conv_ln, b_conv_ln, w_conv_o, b_conv_o, w_mix_o, g_pre_ffn, g_post_ffn, w_up, w_dw_ffn, b_dw_ffn, w_down, loss_target, m_w_ada, m_b_ada, m_g_pre_mix, m_g_post_mix, m_w_in, m_b_in, m_rel_bias, m_w_attn_o, m_w_dw_conv, m_b_dw_conv, m_g_conv_ln, m_b_conv_ln, m_w_conv_o, m_b_conv_o, m_w_mix_o, m_g_pre_ffn, m_g_post_ffn, m_w_up, m_w_dw_ffn, m_b_dw_ffn, m_w_down, v_w_ada, v_b_ada, v_g_pre_mix, v_g_post_mix, v_w_in, v_b_in, v_rel_bias, v_w_attn_o, v_w_dw_conv, v_b_dw_conv, v_g_conv_ln, v_b_conv_ln, v_w_conv_o, v_b_conv_o, v_w_mix_o, v_g_pre_ffn, v_g_post_ffn, v_w_up, v_w_dw_ffn, v_b_dw_ffn, v_w_down):
    names = ["w_ada", "b_ada", "g_pre_mix", "g_post_mix", "w_in", "b_in", "rel_bias", "w_attn_o", "w_dw_conv",
             "b_dw_conv", "g_conv_ln", "b_conv_ln", "w_conv_o", "b_conv_o", "w_mix_o", "g_pre_ffn", "g_post_ffn",
             "w_up", "w_dw_ffn", "b_dw_ffn", "w_down"]
    weights = dict(zip(names, [w_ada, b_ada, g_pre_mix, g_post_mix, w_in, b_in, rel_bias, w_attn_o, w_dw_conv,
                               b_dw_conv, g_conv_ln, b_conv_ln, w_conv_o, b_conv_o, w_mix_o, g_pre_ffn,
                               g_post_ffn, w_up, w_dw_ffn, b_dw_ffn, w_down]))
    mom_m = dict(zip(names, [m_w_ada, m_b_ada, m_g_pre_mix, m_g_post_mix, m_w_in, m_b_in, m_rel_bias, m_w_attn_o,
                             m_w_dw_conv, m_b_dw_conv, m_g_conv_ln, m_b_conv_ln, m_w_conv_o, m_b_conv_o,
                             m_w_mix_o, m_g_pre_ffn, m_g_post_ffn, m_w_up, m_w_dw_ffn, m_b_dw_ffn, m_w_down]))
    mom_v = dict(zip(names, [v_w_ada, v_b_ada, v_g_pre_mix, v_g_post_mix, v_w_in, v_b_in, v_rel_bias, v_w_attn_o,
                             v_w_dw_conv, v_b_dw_conv, v_g_conv_ln, v_b_conv_ln, v_w_conv_o, v_b_conv_o,
                             v_w_mix_o, v_g_pre_ffn, v_g_post_ffn, v_w_up, v_w_dw_ffn, v_b_dw_ffn, v_w_down]))
    shapes = {n: w.shape for n, w in weights.items()}

    seq = x.shape[1]
    me = 4 * lax.axis_index("x") + 2 * lax.axis_index("y") + lax.axis_index("c")
    x2 = x.reshape(seq, D_MODEL)
    target = loss_target.reshape(seq, D_MODEL)
    sq = lambda a: a.reshape(a.shape[1:])
    bf = lambda a: sq(a).astype(BF16)

    transposed = lambda a: jnp.swapaxes(sq(a), 0, 1)

    c_all, mod_all = _ada_mod(c, sq(w_ada))
    c_all = c_all.reshape(N_DEV, D_MODEL)
    mod = lax.dynamic_index_in_dim(mod_all, me, axis=1, keepdims=False)
    mod6 = (mod.reshape(1, 6 * D_MODEL) + b_ada).reshape(6, D_MODEL)

    h1, (g_in, g_dwc, g_dwf) = _pre_mix(
        x2, mod6, g_pre_mix, comm=_gather_comm([transposed(w_in).astype(BF16), sq(w_dw_conv), sq(w_dw_ffn)]))
    wt_in = g_in.reshape(g_in.shape[0] * g_in.shape[1], D_MODEL)
    wf_dwc = _blocks_to_cols(g_dwc)
    wf_dwf = _blocks_to_cols(g_dwf)
    qkv = _mm(h1, wt_in, "nt", BF16, "in_proj_qkv", bias=b_in, tm=1024, tn=768, cols=(0, 3 * D_ATTN))
    zr, _, (g_ao, g_co, g_mo) = _mm(h1, wt_in, "nt", F32, "in_proj_rest", bias=b_in, tm=1024, tn=3 * D_ATTN,
                                 cols=(3 * D_ATTN, 2 * D_CONV + 2 * D_MODEL),
                                 comm=_gather_comm([bf(w_attn_o), bf(w_conv_o), bf(w_mix_o)]))
    table = jnp.transpose(_bias_table(sq(rel_bias)), (1, 0, 2))
    ao, (g_up,) = _attn_fwd(qkv, table, comm=_gather_comm([transposed(w_up).astype(BF16)]))
    (u1, u3), (g_dn,) = _conv_fwd(zr, wf_dwc, b_dw_conv, g_conv_ln, b_conv_ln, comm=_gather_comm([bf(w_down)]))
    wf_ao = _blocks_to_cols(g_ao)
    wf_co = _blocks_to_cols(g_co)
    wf_mo = g_mo.reshape(D_MODEL, D_MODEL)
    wt_up = g_up.reshape(g_up.shape[0] * g_up.shape[1], D_MODEL)
    wf_dn = g_dn.reshape(D_FF, D_MODEL)
    y, a_br, cb_br = _merge_fwd(ao, u3, zr, wf_ao, wf_co, b_conv_o)
    ymix, (x1, h2), _ = _mm(y, wf_mo, "nn", F32, "mix_o", tm=512, tn=D_MODEL,
                            epilogue=_post_mix_pre_ffn(x2, mod6, g_post_mix, g_pre_ffn, 512))
    up = _mm(h2, wt_up, "nt", F32, "ffn_up", tm=1024, tn=1408)
    act = _ffn_act(up, wf_dwf, b_dw_ffn)
    _, (loss_lanes, dout, dyf, small_f), _ = _mm(act, wf_dn, "nn", F32, "ffn_down", tm=512, tn=D_MODEL,
                                                 epilogue=_final(x1, target, mod6, g_post_ffn, 512))

    dact = _mm(dyf, wf_dn, "nt", BF16, "ffn_down_dx", tm=1024, tn=1408)
    gw_down = _mm(act, dyf, "tn", BF16, "ffn_down_dw", tm=256, tn=1024)
    (dup_v, dup_g, dwv, dwg, dbv, dbg), (parts_down,) = _ffn_act_bwd(
        up, dact, wf_dwf, b_dw_ffn, comm=_scatter_comm([_rows_to_blocks(gw_down)]))
    _, (dx1, dymix, small_m), _ = _mm([dup_v, dup_g], wt_up, "nn", F32, "ffn_up_dx", tm=512, tn=D_MODEL,
                                      epilogue=_mid_bwd(x1, dout, ymix, mod6, g_pre_ffn, g_post_mix, 512))
    blocks_up = _rows_to_blocks(_mm_tn_rows([dup_v, dup_g], h2, "ffn_up_dw"))
    _, (da, dcb, dga, dgb, small_g), _ = _mm(dymix, wf_mo, "nt", F32, "mix_o_dx", tm=512, tn=D_MODEL,
                                             epilogue=_merge_bwd(a_br, cb_br, zr, 512))
    gw_mo = _mm(y, dymix, "tn", BF16, "mix_o_dw")
    dao = _mm(da, wf_ao, "nt", BF16, "attn_o_dx", tm=1024)
    gw_ao = _mm(ao, da, "tn", BF16, "attn_o_dw")
    du3 = _mm(dcb, wf_co, "nt", F32, "conv_o_dx", tm=1024)
    gw_co = _mm(u3, dcb, "tn", BF16, "conv_o_dw")
    (dq, dkt, dvt, dbias, small_a), (parts_up,) = _attn_bwd(
        qkv, table, dao, comm=_scatter_comm([blocks_up]))
    g_rel = _bias_grad(jnp.transpose(dbias, (1, 0, 2)))
    (dglu_a, dglu_b, dw_conv, small_c), (parts_mo, parts_ao, parts_co) = _conv_bwd(
        zr, u1, du3, wf_dwc, g_conv_ln, b_conv_ln,
        comm=_scatter_comm([_rows_to_blocks(gw_mo), _cols_to_blocks(gw_ao), _cols_to_blocks(gw_co)]))
    dz = _assemble_dz(dq, dkt, dvt, dglu_a, dglu_b, dga, dgb)
    blocks_in = _rows_to_blocks(_mm(dz, h1, "tn", BF16, "in_proj_dw", tm=512, tn=D_MODEL))
    _, (grad_x, small_x), (parts_in, _, _) = _mm(dz, wt_in, "nn", F32, "in_proj_dx", tm=512, tn=D_MODEL,
                                                 comm=_pair_scatter_comm(blocks_in),
                                                 epilogue=_pre_mix_bwd(x2, dx1, mod6, g_pre_mix, 512))

    packed = _pack_grads(small_x, small_m, small_f, small_g, small_a, small_c, dbv, dbg, dwv, dwg, dw_conv)
    gathered, gathered_rel, gathered_loss = _run_comm(_gather_comm([packed, g_rel, loss_lanes]), "gather_small")
    gathered = gathered.reshape(N_DEV, PACKED_TOTAL)
    updates, g_dwc_full, g_dwf_full, loss_all = _small_adamw(gathered, gathered_rel, gathered_loss, weights, mom_m,
                                                             mom_v)
    loss = loss_all[0, 0]

    grads, deltas, new_m, new_v = {}, {}, {}, {}

    def record(name, update, is_transposed=False):
        for dst, val in zip((grads, deltas, new_m, new_v), update):
            dst[name] = (jnp.swapaxes(val, 0, 1) if is_transposed else val).reshape(shapes[name])

    for name, update in updates.items():
        record(name, update)

    def local_update(name, grad):
        record(name, _adamw(sq(weights[name]), sq(mom_m[name]), sq(mom_v[name]), "adamw_" + name, g=grad))

    conv_cols, ffn_cols, ada_cols = D_CONV // N_DEV, 2 * D_FF // N_DEV, 6 * D_MODEL // N_DEV
    local_update("w_dw_conv", lax.dynamic_slice(g_dwc_full, (0, me * conv_cols), (CONV_K, conv_cols)))
    local_update("w_dw_ffn", lax.dynamic_slice(g_dwf_full, (0, me * ffn_cols), (3, ffn_cols)))
    local_update("w_ada", _ada_grad(c_all, lax.dynamic_slice(gathered, (0, me * ada_cols), (N_DEV, ada_cols))))

    for name, part in (("w_attn_o", parts_ao), ("w_conv_o", parts_co), ("w_mix_o", parts_mo), ("w_down", parts_down)):
        record(name, _adamw(sq(weights[name]), sq(mom_m[name]), sq(mom_v[name]), "adamw_" + name, parts=part))
    for name, part in (("w_in", parts_in), ("w_up", parts_up)):
        record(name, _adamw(transposed(weights[name]), transposed(mom_m[name]), transposed(mom_v[name]),
                            "adamw_" + name, parts=part), is_transposed=True)

    return (loss, grad_x.reshape(x.shape), *[grads[n] for n in names], *[deltas[n] for n in names],
            *[new_m[n] for n in names], *[new_v[n] for n in names])
```

```python
import functools
import math

import jax
import jax.numpy as jnp
from jax import lax
from jax.experimental import pallas as pl
from jax.experimental.pallas import tpu as pltpu

F32 = jnp.float32
BF16 = jnp.bfloat16
HIGHEST = lax.Precision.HIGHEST

D_MODEL = 1024
CHUNK = 64
LEFT_CHUNKS = 8
BAND = (LEFT_CHUNKS + 1) * CHUNK
PAD_ROWS = LEFT_CHUNKS * CHUNK
GROUP = 4
GROUP_Q = GROUP * CHUNK
GROUP_K = GROUP_Q + PAD_ROWS
SOFTMAX_ROWS = 16
TOEPLITZ = 640
N_HEADS = 8
HEAD_DIM = 64
D_ATTN = 512
D_CONV = 512
CONV_K = 31
CONV_HALO = 32
MAX_REL = 128
N_REL = 2 * MAX_REL + 1
D_FF = 2816
FFN_HALO = 8
FFN_COLS = 256
EPS = 1e-6
NEG_INF = -1e30
N_DEV = 8

ADAM_LR = 0.001
ADAM_B1 = 0.9
ADAM_B2 = 0.999
ADAM_EPS = 1e-08
ADAM_WD = 0.01
ADAM_STEP = 10

VMEM_LIMIT_BYTES = 56 * 1024 * 1024
ADAMW_BLOCK_BYTES = 768 * 1024

MESH = pl.DeviceIdType.MESH
ANY = pl.BlockSpec(memory_space=pl.ANY)

SH_M, SC_M, GT_M, SH_F, SC_F, GT_F = range(6)

SMALL = (("b_ada", 6144), ("g_pre_mix", 1024), ("g_post_mix", 1024), ("b_in", 4608), ("b_dw_conv", 512),
         ("g_conv_ln", 512), ("b_conv_ln", 512), ("b_conv_o", 1024), ("g_pre_ffn", 1024), ("g_post_ffn", 1024),
         ("b_dw_ffn", 5632))
PACKED_TOTAL = sum(n for _, n in SMALL) + CONV_K * D_CONV + 3 * 2 * D_FF


def _cparams(n_axes):
    return pltpu.CompilerParams(vmem_limit_bytes=VMEM_LIMIT_BYTES,
                                dimension_semantics=("arbitrary",) * n_axes)


def _sig(v):
    return 1.0 / (1.0 + jnp.exp(-v))


def _pick(n, target):
    if n <= target:
        return n
    t = target - target % 128
    while n % t:
        t -= 128
    return t


def _tile(rows, cols, col=0):
    return pl.BlockSpec((rows, cols), lambda i: (i, col))


def _full(shape):
    zeros = (0,) * len(shape)
    return pl.BlockSpec(shape, lambda i: zeros)


def _prev(halo, cols, rows, col=0):
    return pl.BlockSpec((halo, cols), lambda i: (jnp.maximum(i * (rows // halo) - 1, 0), col))


def _next(halo, cols, rows, n_blocks, col=0):
    return pl.BlockSpec((halo, cols), lambda i: (jnp.minimum((i + 1) * (rows // halo), n_blocks - 1), col))


class _Comm:
    def __init__(self, inputs, out_shapes, sems, start, finish, relay=None, early=None):
        self.inputs, self.out_shapes, self.sems, self.start, self.finish = inputs, out_shapes, sems, start, finish
        self.relay, self.early = relay, early


def _host_call(body, name, grid, in_specs, out_specs, out_shape, scratch_shapes, args, comm=None):
    n_in, n_out, n_scr = len(args), len(out_shape), len(scratch_shapes)
    c_in = list(comm.inputs) if comm else []
    c_out = list(comm.out_shapes) if comm else []
    c_sem = list(comm.sems) if comm else []

    def full(*refs):
        bounds = [0, n_in, len(c_in), n_out, len(c_out), n_scr, len(c_sem)]
        cuts = [sum(bounds[:i + 1]) for i in range(len(bounds))]
        ins, cins, outs, couts, scr, csems = (refs[lo:hi] for lo, hi in zip(cuts[:-1], cuts[1:]))
        if comm:
            first = functools.reduce(jnp.logical_and, [pl.program_id(ax) == 0 for ax in range(len(grid))])
            pl.when(first)(lambda: comm.start(cins, couts, csems))
            if comm.early is not None:
                strides = [math.prod(grid[ax + 1:]) for ax in range(len(grid))]
                step = sum(pl.program_id(ax) * strides[ax] for ax in range(len(grid)))
                pl.when(step == 1)(lambda: comm.early(cins, couts, csems))
            last = functools.reduce(jnp.logical_and, [pl.program_id(ax) == grid[ax] - 1 for ax in range(len(grid))])
            if comm.relay is not None:
                pl.when(last)(lambda: comm.relay(cins, couts, csems))
        body(ins, outs, scr)
        if comm:
            pl.when(last)(lambda: comm.finish(cins, couts, csems))

    res = pl.pallas_call(
        full, name=name, grid=grid, in_specs=list(in_specs) + [ANY] * len(c_in),
        out_specs=list(out_specs) + [ANY] * len(c_out), out_shape=list(out_shape) + c_out,
        scratch_shapes=list(scratch_shapes) + c_sem, compiler_params=_cparams(len(grid)),
    )(*args, *c_in)
    return list(res[:n_out]), list(res[n_out:])


def _run_comm(comm, name):
    n_in, n_out = len(comm.inputs), len(comm.out_shapes)

    def body(*refs):
        ins, outs, sems = refs[:n_in], refs[n_in:n_in + n_out], refs[n_in + n_out:]
        comm.start(ins, outs, sems)
        if comm.relay is not None:
            comm.relay(ins, outs, sems)
        comm.finish(ins, outs, sems)

    return pl.pallas_call(
        body, name=name, out_shape=list(comm.out_shapes), in_specs=[ANY] * n_in, out_specs=[ANY] * n_out,
        scratch_shapes=list(comm.sems),
    )(*comm.inputs)


def _place():
    return lax.axis_index("x"), lax.axis_index("y"), lax.axis_index("c")


def _gather_comm(arrs):
    n = len(arrs)

    def plan(ins, outs, sems):
        send_sems, recv_sems, local_sems = sems
        x, y, c = _place()
        me, sibling = (x, y, c), (x, y, 1 - c)
        chips = [(1 - x, y), (x, 1 - y), (1 - x, 1 - y)]

        def block(k, p):
            return outs[k].at[4 * p[0] + 2 * p[1] + p[2]]

        def copy(k, s, blk, to, src=None):
            return pltpu.make_async_remote_copy(
                src_ref=block(k, blk) if src is None else src, dst_ref=block(k, blk),
                send_sem=send_sems.at[7 * k + s], recv_sem=recv_sems.at[7 * k + s],
                device_id=to, device_id_type=MESH)

        mine = [pltpu.make_async_copy(ins[k], block(k, me), local_sems.at[k]) for k in range(n)]
        first = []
        for k in range(n):
            first.append(copy(k, 0, me, sibling, src=ins[k]))
            for j, chip in enumerate(chips):
                first.append(copy(k, 1 + j, me, (*chip, c), src=ins[k]))
        return me, sibling, chips, c, copy, mine, first

    def start(ins, outs, sems):
        *_, mine, first = plan(ins, outs, sems)
        for cp in mine + first:
            cp.start()

    def relay(ins, outs, sems):
        me, sibling, chips, c, copy, _, _ = plan(ins, outs, sems)
        for j, chip in enumerate(chips):
            for k in range(n):
                copy(k, 1 + j, (*chip, c), me).wait_recv()
                copy(k, 4 + j, (*chip, c), sibling).start()

    def finish(ins, outs, sems):
        me, sibling, chips, c, copy, mine, first = plan(ins, outs, sems)
        passed = [copy(k, 4 + j, (*chip, c), sibling) for j, chip in enumerate(chips) for k in range(n)]
        for k in range(n):
            copy(k, 0, sibling, me).wait_recv()
        for j, chip in enumerate(chips):
            for k in range(n):
                copy(k, 4 + j, (*chip, 1 - c), me).wait_recv()
        for cp in first + passed:
            cp.wait_send()
        for cp in mine:
            cp.wait()

    return _Comm(list(arrs), [jax.ShapeDtypeStruct((N_DEV,) + a.shape, a.dtype) for a in arrs],
                 [pltpu.SemaphoreType.DMA((7 * n,)), pltpu.SemaphoreType.DMA((7 * n,)),
                  pltpu.SemaphoreType.DMA((n,))], start, finish, relay)


def _scatter_comm(blocks):
    n = len(blocks)

    def plan(ins, outs, sems, arrivals):
        send_sems, recv_sems, local_sems = sems
        x, y, c = _place()
        me = 4 * x + 2 * y + c
        local = [pltpu.make_async_copy(ins[k].at[me], outs[k].at[me], local_sems.at[k]) for k in range(n)]
        sends, recvs = [], []
        for k in range(n):
            for mask in range(1, N_DEV):
                px = 1 - x if mask & 4 else x
                py = 1 - y if mask & 2 else y
                pc = 1 - c if mask & 1 else c
                peer = 4 * px + 2 * py + pc
                sem = 7 * k + mask - 1
                both = dict(send_sem=send_sems.at[sem], recv_sem=recv_sems.at[sem], device_id=(px, py, pc),
                            device_id_type=MESH)
                sends.append(pltpu.make_async_remote_copy(src_ref=ins[k].at[peer], dst_ref=outs[k].at[me], **both))
                if arrivals:
                    recvs.append(pltpu.make_async_remote_copy(src_ref=ins[k].at[me], dst_ref=outs[k].at[peer],
                                                              **both))
        return local, sends, recvs

    def start(ins, outs, sems):
        local, sends, _ = plan(ins, outs, sems, arrivals=False)
        for cp in local + sends:
            cp.start()

    def finish(ins, outs, sems):
        local, sends, recvs = plan(ins, outs, sems, arrivals=True)
        for cp in recvs:
            cp.wait_recv()
        for cp in sends:
            cp.wait_send()
        for cp in local:
            cp.wait()

    return _Comm(list(blocks), [jax.ShapeDtypeStruct(b.shape, b.dtype) for b in blocks],
                 [pltpu.SemaphoreType.DMA((7 * n,)), pltpu.SemaphoreType.DMA((7 * n,)),
                  pltpu.SemaphoreType.DMA((n,))], start, finish)


def _pair_scatter_comm(block):
    _, r, c = block.shape
    quarter = jax.ShapeDtypeStruct((4, r, c), block.dtype)

    def plan(ins, outs, sems):
        parts, got, pair = outs
        d2d_send, d2d_recv, ici_send, ici_recv, local, buf_a, buf_b = sems
        x, y, cc = _place()
        mine = 2 * x + y
        chips = [(1 - x, y), (x, 1 - y), (1 - x, 1 - y)]
        to_sibling = [pltpu.make_async_remote_copy(
            src_ref=ins[0].at[2 * q + 1 - cc], dst_ref=got.at[q], send_sem=d2d_send.at[q], recv_sem=d2d_recv.at[q],
            device_id=(x, y, 1 - cc), device_id_type=MESH) for q in range(4)]
        to_chips = [pltpu.make_async_remote_copy(
            src_ref=pair.at[2 * px + py], dst_ref=parts.at[mine], send_sem=ici_send.at[j], recv_sem=ici_recv.at[j],
            device_id=(px, py, cc), device_id_type=MESH) for j, (px, py) in enumerate(chips)]
        from_chips = [pltpu.make_async_remote_copy(
            src_ref=pair.at[mine], dst_ref=parts.at[2 * px + py], send_sem=ici_send.at[j], recv_sem=ici_recv.at[j],
            device_id=(px, py, cc), device_id_type=MESH) for j, (px, py) in enumerate(chips)]
        own = pltpu.make_async_copy(pair.at[mine], parts.at[mine], local.at[2])
        return cc, got, pair, local, buf_a, buf_b, to_sibling, to_chips, from_chips, own

    def start(ins, outs, sems):
        for cp in plan(ins, outs, sems)[6]:
            cp.start()

    def early(ins, outs, sems):
        cc, got, pair, local, buf_a, buf_b, to_sibling, to_chips, _, own = plan(ins, outs, sems)
        for q in range(4):
            to_sibling[q].wait_recv()
            loads = [pltpu.make_async_copy(ins[0].at[2 * q + cc], buf_a, local.at[0]),
                     pltpu.make_async_copy(got.at[q], buf_b, local.at[1])]
            for cp in loads:
                cp.start()
            for cp in loads:
                cp.wait()
            buf_a[...] = (buf_a[...].astype(F32) + buf_b[...].astype(F32)).astype(block.dtype)
            store = pltpu.make_async_copy(buf_a, pair.at[q], local.at[0])
            store.start()
            store.wait()
        for cp in to_chips + [own]:
            cp.start()

    def finish(ins, outs, sems):
        *_, to_sibling, to_chips, from_chips, own = plan(ins, outs, sems)
        for cp in from_chips:
            cp.wait_recv()
        for cp in to_chips + to_sibling:
            cp.wait_send()
        own.wait()

    return _Comm([block], [quarter, quarter, quarter],
                 [pltpu.SemaphoreType.DMA((4,)), pltpu.SemaphoreType.DMA((4,)), pltpu.SemaphoreType.DMA((3,)),
                  pltpu.SemaphoreType.DMA((3,)), pltpu.SemaphoreType.DMA((3,)), pltpu.VMEM((r, c), block.dtype),
                  pltpu.VMEM((r, c), block.dtype)], start, finish, early=early)


_DIMS = {"nn": (((1,), (0,)), ((), ())), "nt": (((1,), (1,)), ((), ())), "tn": (((0,), (0,)), ((), ()))}


class _Epilogue:
    def __init__(self, args, in_specs, out_shapes, out_specs, fn, keep_product):
        self.args, self.in_specs, self.out_shapes, self.out_specs = args, in_specs, out_shapes, out_specs
        self.fn, self.keep_product = fn, keep_product


def _row_tile(rows, cols):
    return pl.BlockSpec((rows, cols), lambda i, j: (i, 0))


def _whole(shape):
    zeros = (0,) * len(shape)
    return pl.BlockSpec(shape, lambda i, j: zeros)


def _mm(a, b, mode, out_dtype, name, bias=None, tm=512, tn=512, comm=None, cols=None, epilogue=None, stacked=False):
    pieces = a if isinstance(a, (list, tuple)) else [a]
    assert all(p.dtype == BF16 for p in pieces) and b.dtype == BF16
    a = pieces[0]
    if mode == "tn":
        k_dim, m_dim = a.shape
    else:
        m_dim, k_dim = a.shape
    n_dim = b.shape[0] if mode == "nt" else b.shape[1]
    col0 = 0
    if cols is not None:
        assert mode != "tn" and cols[0] % tn == 0 and cols[1] % tn == 0
        col0, n_dim = cols[0] // tn, cols[1]
    tm, tn = _pick(m_dim, tm), _pick(n_dim, tn)
    assert mode != "tn" or len(pieces) == 1
    a_specs = [pl.BlockSpec((k_dim, tm), lambda i, j: (0, i)) if mode == "tn"
               else pl.BlockSpec((tm, k_dim), lambda i, j: (i, 0))] * len(pieces)
    once = dict(pipeline_mode=pl.Buffered(1)) if tn == n_dim else {}
    if mode == "nt":
        b_specs = [pl.BlockSpec((tn, k_dim), lambda i, j, p=p: (j + col0, p), **once) for p in range(len(pieces))]
    else:
        b_specs = [pl.BlockSpec((k_dim, tn), lambda i, j, p=p: (p, j + col0), **once) for p in range(len(pieces))]
    in_specs = a_specs + b_specs
    args = list(pieces) + [b] * len(pieces)
    if bias is not None:
        in_specs.append(pl.BlockSpec((1, tn), lambda i, j: (0, j + col0)))
        args.append(bias)
    dims = _DIMS[mode]
    n_pieces = len(pieces)
    n_own = len(args)
    keep = epilogue is None or epilogue.keep_product
    out_specs = [pl.BlockSpec((tm, tn), lambda i, j: (i, j))] if keep else []
    out_shape = [jax.ShapeDtypeStruct((m_dim, n_dim), out_dtype)] if keep else []
    if stacked:
        out_specs = [pl.BlockSpec((None, tm, tn), lambda i, j: (j, i, 0))]
        out_shape = [jax.ShapeDtypeStruct((n_dim // tn, m_dim, tn), out_dtype)]
    if epilogue is not None:
        assert tn == n_dim
        in_specs, args = in_specs + list(epilogue.in_specs), args + list(epilogue.args)
        out_specs, out_shape = out_specs + list(epilogue.out_specs), out_shape + list(epilogue.out_shapes)

    def body(ins, outs, scratch):
        total = lax.dot_general(ins[0][...], ins[n_pieces][...], dims, preferred_element_type=F32)
        for p in range(1, n_pieces):
            total = total + lax.dot_general(ins[p][...], ins[n_pieces + p][...], dims, preferred_element_type=F32)
        if bias is not None:
            total = total + ins[2 * n_pieces][...]
        if keep:
            outs[0][...] = total.astype(out_dtype)
        if epilogue is not None:
            epilogue.fn(total, pl.program_id(0) == 0, ins[n_own:], outs[1:] if keep else outs)

    outs, extra = _host_call(body, name, grid=(m_dim // tm, n_dim // tn), in_specs=in_specs, out_specs=out_specs,
                             out_shape=out_shape, scratch_shapes=[], args=args, comm=comm)
    product = outs[0] if keep else None
    if comm is None and epilogue is None:
        return product
    return product, outs[1:] if keep else outs, extra


def _mm_tn_rows(pieces, b, name, tm=256):
    k_dim, n_dim = b.shape
    counts = [p.shape[1] // tm for p in pieces]
    assert all(p.shape[1] % tm == 0 for p in pieces)
    firsts = [sum(counts[:q]) for q in range(len(pieces))]

    def a_spec(first, count):
        return pl.BlockSpec((k_dim, tm), lambda i: (0, jnp.clip(i - first, 0, count - 1)))

    def body(ins, outs, scratch):
        i = pl.program_id(0)
        for a_ref, first, count in zip(ins[:-1], firsts, counts):
            @pl.when(jnp.logical_and(i >= first, i < first + count))
            def _(a_ref=a_ref):
                outs[0][...] = lax.dot_general(a_ref[...], ins[-1][...], _DIMS["tn"],
                                               preferred_element_type=F32).astype(BF16)

    (out,), _ = _host_call(
        body, name, grid=(sum(counts),),
        in_specs=[a_spec(f, c) for f, c in zip(firsts, counts)] + [_full((k_dim, n_dim))],
        out_specs=[_tile(tm, n_dim)], out_shape=[jax.ShapeDtypeStruct((sum(counts) * tm, n_dim), BF16)],
        scratch_shapes=[], args=list(pieces) + [b])
    return out


def _adam_math(w, g, m, v):
    m = ADAM_B1 * m + (1.0 - ADAM_B1) * g
    v = ADAM_B2 * v + (1.0 - ADAM_B2) * (g * g)
    m_hat = m / (1.0 - ADAM_B1 ** ADAM_STEP)
    v_hat = v / (1.0 - ADAM_B2 ** ADAM_STEP)
    delta = -ADAM_LR * (m_hat / (jnp.sqrt(v_hat) + ADAM_EPS) + ADAM_WD * w)
    return delta, m, v


def _adamw(w, m, v, name, g=None, parts=None):
    rows, cols = w.shape
    tr = rows
    if rows * cols * 4 > ADAMW_BLOCK_BYTES:
        tr = max(t for t in range(16, rows, 16) if rows % t == 0 and t * cols * 4 <= ADAMW_BLOCK_BYTES)

    def body(w_ref, m_ref, v_ref, g_ref, go_ref, d_ref, mo_ref, vo_ref):
        if parts is None:
            grad = g_ref[...]
        else:
            grad = g_ref[0].astype(F32)
            for d in range(1, parts.shape[0]):
                grad = grad + g_ref[d].astype(F32)
        delta, m_new, v_new = _adam_math(w_ref[...], grad, m_ref[...], v_ref[...])
        go_ref[...] = grad
        d_ref[...] = delta
        mo_ref[...] = m_new
        vo_ref[...] = v_new

    spec = _tile(tr, cols)
    g_spec = spec if parts is None else pl.BlockSpec((parts.shape[0], tr, cols), lambda i: (0, i, 0))
    shape = jax.ShapeDtypeStruct((rows, cols), F32)
    return pl.pallas_call(
        body, name=name, out_shape=[shape] * 4, grid=(rows // tr,),
        in_specs=[spec, spec, spec, g_spec], out_specs=[spec] * 4, compiler_params=_cparams(1),
    )(w, m, v, g if parts is None else parts)


def _pack_grads(small_x, small_m, small_f, small_g, small_a, small_c, dbv, dbg, dwv, dwg, dw_conv):
    pieces = [
        (small_x, 2, D_MODEL), (small_x, 1, D_MODEL), (small_m, 4, D_MODEL), (small_m, 2, D_MODEL),
        (small_m, 1, D_MODEL), (small_f, 1, D_MODEL),
        (small_x, 0, D_MODEL), (small_m, 3, D_MODEL),
        (small_a, 0, D_ATTN), (small_a, 1, D_ATTN), (small_a, 2, D_ATTN), (small_c, 3, D_CONV),
        (small_c, 4, D_CONV), (small_g, 0, D_MODEL), (small_g, 1, D_MODEL),
        (small_c, 0, D_CONV), (small_c, 1, D_CONV), (small_c, 2, D_CONV),
        (small_g, 2, D_MODEL), (small_m, 0, D_MODEL), (small_f, 0, D_MODEL),
        (dbv, 0, D_FF), (dbg, 0, D_FF),
    ]
    pieces += [(dw_conv, j, D_CONV) for j in range(CONV_K)]
    pieces += [(src, tap, D_FF) for tap in range(3) for src in (dwv, dwg)]
    sources = [small_x, small_m, small_f, small_g, small_a, small_c, dbv, dbg, dwv, dwg, dw_conv]
    assert sum(width for _, _, width in pieces) == PACKED_TOTAL

    def body(*refs):
        o_ref = refs[-1]
        ref_of = {id(src): ref for src, ref in zip(sources, refs)}
        off = 0
        for src, row, width in pieces:
            o_ref[:, off:off + width] = ref_of[id(src)][row:row + 1, :]
            off += width

    return pl.pallas_call(body, name="pack_grads", out_shape=jax.ShapeDtypeStruct((1, PACKED_TOTAL), F32))(*sources)


def _small_adamw(gathered, gathered_rel, gathered_loss, weights, mom_m, mom_v):
    vec_names = [name for name, _ in SMALL]
    states = []
    for name in vec_names + ["rel_bias"]:
        states += [weights[name], mom_m[name], mom_v[name]]
    states = [a.reshape(a.shape[1:]) if a.ndim == 3 else a for a in states]
    n_state = len(states)

    def body(*refs):
        g_ref, rel_ref, loss_ref = refs[0], refs[1], refs[2]
        state_refs, out_refs = refs[3:3 + n_state], refs[3 + n_state:]
        total = g_ref[0:1, :]
        rel = rel_ref[0]
        loss = loss_ref[0]
        for d in range(1, N_DEV):
            total = total + g_ref[d:d + 1, :]
            rel = rel + rel_ref[d]
            loss = loss + loss_ref[d]
        off = 0
        for n, (name, width) in enumerate(SMALL):
            grad = total[:, off:off + width]
            w_ref, m_ref, v_ref = state_refs[3 * n:3 * n + 3]
            for ref, val in zip(out_refs[4 * n:4 * n + 4], (grad,) + _adam_math(w_ref[...], grad, m_ref[...], v_ref[...])):
                ref[...] = val
            off += width
        n = len(SMALL)
        w_ref, m_ref, v_ref = state_refs[3 * n:3 * n + 3]
        for ref, val in zip(out_refs[4 * n:4 * n + 4], (rel,) + _adam_math(w_ref[...], rel, m_ref[...], v_ref[...])):
            ref[...] = val
        dwc_ref, dwf_ref, loss_out = out_refs[4 * n + 4:]
        loss_out[...] = 0.5 * loss
        dwc_ref[...] = jnp.zeros_like(dwc_ref)
        dwf_ref[...] = jnp.zeros_like(dwf_ref)
        for j in range(CONV_K):
            dwc_ref[j:j + 1, :] = total[:, off:off + D_CONV]
            off += D_CONV
        for tap in range(3):
            dwf_ref[tap:tap + 1, :] = total[:, off:off + 2 * D_FF]
            off += 2 * D_FF

    out_shape = []
    for k in range(n_state // 3):
        out_shape += [jax.ShapeDtypeStruct(states[3 * k].shape, F32)] * 4
    out_shape += [jax.ShapeDtypeStruct((CONV_HALO, D_CONV), F32), jax.ShapeDtypeStruct((8, 2 * D_FF), F32),
                  jax.ShapeDtypeStruct((1, 128), F32)]
    res = pl.pallas_call(
        body, name="small_adamw", out_shape=out_shape,
        compiler_params=pltpu.CompilerParams(vmem_limit_bytes=VMEM_LIMIT_BYTES),
    )(gathered, gathered_rel, gathered_loss, *states)
    updates = {name: tuple(res[4 * n:4 * n + 4]) for n, name in enumerate(vec_names + ["rel_bias"])}
    return updates, res[-3], res[-2], res[-1]


def _ada_mod(c, w_shard):
    cols = w_shard.shape[1]

    def body(c_ref, w_ref, call_ref, mod_ref, send_sems, recv_sems):
        x, y, cc = _place()
        me = 4 * x + 2 * y + cc

        def exchange(ref, phase):
            sends, arrivals = [], []
            for mask in range(1, N_DEV):
                px = 1 - x if mask & 4 else x
                py = 1 - y if mask & 2 else y
                pc = 1 - cc if mask & 1 else cc
                both = dict(send_sem=send_sems.at[7 * phase + mask - 1], recv_sem=recv_sems.at[7 * phase + mask - 1],
                            device_id=(px, py, pc), device_id_type=MESH)
                sends.append(pltpu.make_async_remote_copy(src_ref=ref.at[me], dst_ref=ref.at[me], **both))
                arrivals.append(pltpu.make_async_remote_copy(src_ref=ref.at[me], dst_ref=ref.at[4 * px + 2 * py + pc],
                                                             **both))
            for cp in sends:
                cp.start()
            for cp in arrivals:
                cp.wait_recv()
            for cp in sends:
                cp.wait_send()

        v = c_ref[...]
        call_ref[me] = v * _sig(v)
        exchange(call_ref, 0)
        c_all = jnp.concatenate([call_ref[d] for d in range(N_DEV)], axis=0)
        mod_ref[me] = jnp.dot(c_all, w_ref[...], precision=HIGHEST, preferred_element_type=F32)
        exchange(mod_ref, 1)

    return pl.pallas_call(
        body, name="ada_mod",
        out_shape=[jax.ShapeDtypeStruct((N_DEV, 1, D_MODEL), F32), jax.ShapeDtypeStruct((N_DEV, N_DEV, cols), F32)],
        scratch_shapes=[pltpu.SemaphoreType.DMA((14,)), pltpu.SemaphoreType.DMA((14,))],
        compiler_params=pltpu.CompilerParams(vmem_limit_bytes=VMEM_LIMIT_BYTES),
    )(c, w_shard)


def _ada_grad(c_all, dmod_shard):
    def body(c_ref, d_ref, o_ref):
        o_ref[...] = lax.dot_general(c_ref[...], d_ref[...], _DIMS["tn"], precision=HIGHEST,
                                     preferred_element_type=F32)

    return pl.pallas_call(
        body, name="ada_grad", out_shape=jax.ShapeDtypeStruct((D_MODEL, dmod_shard.shape[1]), F32),
        compiler_params=pltpu.CompilerParams(vmem_limit_bytes=VMEM_LIMIT_BYTES),
    )(c_all, dmod_shard)


ROWS = 256


def _rms(v):
    r = lax.rsqrt(jnp.mean(v * v, axis=-1, keepdims=True) + EPS)
    return v * r, r


def _rms_bwd(dxn, xn, r):
    return r * (dxn - xn * jnp.mean(dxn * xn, axis=-1, keepdims=True))


def _colsum(v):
    return jnp.sum(v, axis=0, keepdims=True)


def _pre_mix(x, mod6, g1, comm=None):
    seq = x.shape[0]

    def body(ins, outs, scratch):
        x_ref, mod_ref, g_ref = ins
        xn, _ = _rms(x_ref[...])
        y = xn * g_ref[...]
        outs[0][...] = (y * (1.0 + mod_ref[SC_M:SC_M + 1, :]) + mod_ref[SH_M:SH_M + 1, :]).astype(BF16)

    (h,), extra = _host_call(
        body, "pre_mix", grid=(seq // ROWS,),
        in_specs=[_tile(ROWS, D_MODEL), _full((6, D_MODEL)), _full((1, D_MODEL))], out_specs=[_tile(ROWS, D_MODEL)],
        out_shape=[jax.ShapeDtypeStruct((seq, D_MODEL), BF16)], scratch_shapes=[], args=[x, mod6, g1], comm=comm)
    return h, extra


def _post_mix_pre_ffn(x, mod6, g2, g3, rows):
    seq = x.shape[0]

    def fn(y, first, ins, outs):
        x_ref, mod_ref, g2_ref, g3_ref = ins
        x1_ref, h_ref = outs
        yn, _ = _rms(y)
        x1 = x_ref[...] + mod_ref[GT_M:GT_M + 1, :] * (yn * g2_ref[...])
        x1_ref[...] = x1
        xn, _ = _rms(x1)
        y3 = xn * g3_ref[...]
        h_ref[...] = (y3 * (1.0 + mod_ref[SC_F:SC_F + 1, :]) + mod_ref[SH_F:SH_F + 1, :]).astype(BF16)

    return _Epilogue(
        [x, mod6, g2, g3], [_row_tile(rows, D_MODEL), _whole((6, D_MODEL)), _whole((1, D_MODEL)), _whole((1, D_MODEL))],
        [jax.ShapeDtypeStruct((seq, D_MODEL), F32), jax.ShapeDtypeStruct((seq, D_MODEL), BF16)],
        [_row_tile(rows, D_MODEL), _row_tile(rows, D_MODEL)], fn, keep_product=True)


def _final(x1, target, mod6, g4, rows):
    seq = x1.shape[0]

    def fn(y, first, ins, outs):
        x1_ref, t_ref, mod_ref, g_ref = ins
        loss_ref, dout_ref, dyf_ref, small_ref = outs

        @pl.when(first)
        def _():
            loss_ref[...] = jnp.zeros_like(loss_ref)
            small_ref[...] = jnp.zeros_like(small_ref)

        gt = mod_ref[GT_F:GT_F + 1, :]
        g4v = g_ref[...]
        yn, r = _rms(y)
        out = x1_ref[...] + gt * (yn * g4v)
        err = out - t_ref[...]
        loss_ref[...] += jnp.sum(jnp.mean(err * err, axis=-1, keepdims=True))
        dout = err * (1.0 / D_MODEL)
        dout_ref[...] = dout
        small_ref[0:1, :] += _colsum(dout * gt * yn)
        small_ref[1:2, :] += _colsum(dout * (yn * g4v))
        dyf_ref[...] = _rms_bwd(dout * gt * g4v, yn, r).astype(BF16)

    return _Epilogue(
        [x1, target, mod6, g4],
        [_row_tile(rows, D_MODEL), _row_tile(rows, D_MODEL), _whole((6, D_MODEL)), _whole((1, D_MODEL))],
        [jax.ShapeDtypeStruct((1, 128), F32), jax.ShapeDtypeStruct((seq, D_MODEL), F32),
         jax.ShapeDtypeStruct((seq, D_MODEL), BF16), jax.ShapeDtypeStruct((8, D_MODEL), F32)],
        [_whole((1, 128)), _row_tile(rows, D_MODEL), _row_tile(rows, D_MODEL), _whole((8, D_MODEL))],
        fn, keep_product=False)


def _mid_bwd(x1, dout, ymix, mod6, g3, g2, rows):
    seq = x1.shape[0]

    def fn(dh, first, ins, outs):
        x1_ref, dout_ref, y_ref, mod_ref, g3_ref, g2_ref = ins
        dx1_ref, dy_ref, small_ref = outs

        @pl.when(first)
        def _():
            small_ref[...] = jnp.zeros_like(small_ref)

        g3v, g2v = g3_ref[...], g2_ref[...]
        xn, r3 = _rms(x1_ref[...])
        y3 = xn * g3v
        dy3 = dh * (1.0 + mod_ref[SC_F:SC_F + 1, :])
        small_ref[0:1, :] += _colsum(dy3 * xn)
        small_ref[1:2, :] += _colsum(dh * y3)
        small_ref[2:3, :] += _colsum(dh)
        dx1 = dout_ref[...] + _rms_bwd(dy3 * g3v, xn, r3)
        dx1_ref[...] = dx1
        gt = mod_ref[GT_M:GT_M + 1, :]
        yn, r2 = _rms(y_ref[...])
        small_ref[3:4, :] += _colsum(dx1 * gt * yn)
        small_ref[4:5, :] += _colsum(dx1 * (yn * g2v))
        dy_ref[...] = _rms_bwd(dx1 * gt * g2v, yn, r2).astype(BF16)

    return _Epilogue(
        [x1, dout, ymix, mod6, g3, g2],
        [_row_tile(rows, D_MODEL)] * 3 + [_whole((6, D_MODEL)), _whole((1, D_MODEL)), _whole((1, D_MODEL))],
        [jax.ShapeDtypeStruct((seq, D_MODEL), F32), jax.ShapeDtypeStruct((seq, D_MODEL), BF16),
         jax.ShapeDtypeStruct((8, D_MODEL), F32)],
        [_row_tile(rows, D_MODEL), _row_tile(rows, D_MODEL), _whole((8, D_MODEL))], fn, keep_product=False)


def _pre_mix_bwd(x, dx1, mod6, g1, rows):
    seq = x.shape[0]

    def fn(dh, first, ins, outs):
        x_ref, dx1_ref, mod_ref, g_ref = ins
        dx_ref, small_ref = outs

        @pl.when(first)
        def _():
            small_ref[...] = jnp.zeros_like(small_ref)

        g1v = g_ref[...]
        xn, r = _rms(x_ref[...])
        dy = dh * (1.0 + mod_ref[SC_M:SC_M + 1, :])
        small_ref[0:1, :] += _colsum(dy * xn)
        small_ref[1:2, :] += _colsum(dh * (xn * g1v))
        small_ref[2:3, :] += _colsum(dh)
        dx_ref[...] = dx1_ref[...] + _rms_bwd(dy * g1v, xn, r)

    return _Epilogue(
        [x, dx1, mod6, g1],
        [_row_tile(rows, D_MODEL), _row_tile(rows, D_MODEL), _whole((6, D_MODEL)), _whole((1, D_MODEL))],
        [jax.ShapeDtypeStruct((seq, D_MODEL), F32), jax.ShapeDtypeStruct((8, D_MODEL), F32)],
        [_row_tile(rows, D_MODEL), _whole((8, D_MODEL))], fn, keep_product=False)


def _toeplitz_onehot(shape, offset_axis, top):
    m = lax.broadcasted_iota(jnp.int32, shape, offset_axis)
    i = lax.broadcasted_iota(jnp.int32, shape, 1 - offset_axis)
    return (i == jnp.clip(top - m, -MAX_REL, MAX_REL) + MAX_REL).astype(F32)


def _bias_table(rel_bias):
    width = GROUP_Q + GROUP_K

    def body(rb_ref, o_ref, t_ref):
        t_ref[...] = jnp.dot(rb_ref[...], _toeplitz_onehot((N_REL, width), 1, GROUP_K - 1), precision=HIGHEST,
                             preferred_element_type=F32)
        lane = lax.broadcasted_iota(jnp.int32, (N_HEADS, GROUP_K), 1)
        for r in range(GROUP_Q):
            first_key = (r // CHUNK) * CHUNK
            band = jnp.logical_and(lane >= first_key, lane < first_key + BAND)
            o_ref[r] = jnp.where(band, t_ref[:, GROUP_Q - 1 - r:GROUP_Q - 1 - r + GROUP_K], NEG_INF)

    return pl.pallas_call(
        body, name="bias_table", out_shape=jax.ShapeDtypeStruct((GROUP_Q, N_HEADS, GROUP_K), F32),
        scratch_shapes=[pltpu.VMEM((N_HEADS, width), F32)],
    )(rel_bias)


def _bias_grad(dbias_q):
    def body(d_ref, o_ref, t_ref):
        t_ref[...] = jnp.zeros_like(t_ref)
        for qi in range(CHUNK):
            t_ref[:, CHUNK - 1 - qi:CHUNK - 1 - qi + BAND] += d_ref[qi]
        o_ref[...] = jnp.dot(t_ref[...], _toeplitz_onehot((TOEPLITZ, N_REL), 0, BAND - 1), precision=HIGHEST,
                             preferred_element_type=F32)

    return pl.pallas_call(
        body, name="bias_grad", out_shape=jax.ShapeDtypeStruct((N_HEADS, N_REL), F32),
        scratch_shapes=[pltpu.VMEM((N_HEADS, TOEPLITZ), F32)],
    )(dbias_q)


def _resident_copies(qkv_hbm, t_hbm, k_ref, v_ref, t_ref, sems):
    inside = pl.ds(PAD_ROWS, qkv_hbm.shape[1])
    return (pltpu.make_async_copy(qkv_hbm.at[1], k_ref.at[inside, :], sems.at[0]),
            pltpu.make_async_copy(qkv_hbm.at[2], v_ref.at[inside, :], sems.at[1]),
            pltpu.make_async_copy(t_hbm, t_ref, sems.at[2]))


def _start_resident(copies, k_ref, v_ref):
    k_ref[0:PAD_ROWS, :] = jnp.zeros((PAD_ROWS, D_ATTN), BF16)
    v_ref[0:PAD_ROWS, :] = jnp.zeros((PAD_ROWS, D_ATTN), BF16)
    for cp in copies:
        cp.start()


def _softmax_rows(s_ref, t_ref, h, before_start, rows):
    s = s_ref[rows, :] * (HEAD_DIM ** -0.5) + t_ref[h, rows, :] + before_start
    e = jnp.exp(s - jnp.max(s, axis=-1, keepdims=True))
    return e / jnp.sum(e, axis=-1, keepdims=True)


def _before_start(g):
    kj = lax.broadcasted_iota(jnp.int32, (8, GROUP_K), 1)
    return jnp.where(kj >= PAD_ROWS - g * GROUP_Q, 0.0, NEG_INF)


def _attn_fwd(qkv, table, comm=None):
    seq = qkv.shape[1]

    def body(ins, outs, scratch):
        q_ref, qkv_hbm, t_hbm = ins
        (o_ref,) = outs
        k_ref, v_ref, t_ref, s_ref, p_ref, sems = scratch
        g = pl.program_id(0)
        load_k, load_v, load_t = _resident_copies(qkv_hbm, t_hbm, k_ref, v_ref, t_ref, sems)

        @pl.when(g == 0)
        def _():
            _start_resident((load_k, load_v, load_t), k_ref, v_ref)
            load_k.wait()

        window = pl.ds(pl.multiple_of(g * GROUP_Q, GROUP_Q), GROUP_K)
        before_start = _before_start(g)
        for h in range(N_HEADS):
            cols = slice(h * HEAD_DIM, (h + 1) * HEAD_DIM)
            buf = h % 2
            s_ref[buf] = lax.dot_general(q_ref[:, cols], k_ref[window, cols], _DIMS["nt"],
                                         preferred_element_type=F32)
            if h == 0:
                pl.when(g == 0)(load_t.wait)
            for row in range(0, GROUP_Q, SOFTMAX_ROWS):
                halves = [_softmax_rows(s_ref.at[buf], t_ref, h, before_start, slice(r, r + 8))
                          for r in (row, row + 8)]
                p_ref[buf, row:row + SOFTMAX_ROWS, :] = jnp.concatenate(halves, axis=0).astype(BF16)
            if h == 0:
                pl.when(g == 0)(load_v.wait)
            o_ref[:, cols] = jnp.dot(p_ref[buf], v_ref[window, cols], preferred_element_type=F32).astype(BF16)

    q_tiles = pl.BlockSpec((None, GROUP_Q, D_ATTN), lambda g: (0, g, 0))
    (ao,), extra = _host_call(
        body, "attn_fwd", grid=(seq // GROUP_Q,),
        in_specs=[q_tiles, ANY, ANY], out_specs=[_tile(GROUP_Q, D_ATTN)],
        out_shape=[jax.ShapeDtypeStruct((seq, D_ATTN), BF16)],
        scratch_shapes=[pltpu.VMEM((seq + PAD_ROWS, D_ATTN), BF16), pltpu.VMEM((seq + PAD_ROWS, D_ATTN), BF16),
                        pltpu.VMEM(table.shape, F32),
                        pltpu.VMEM((2, GROUP_Q, GROUP_K), F32), pltpu.VMEM((2, GROUP_Q, GROUP_K), BF16),
                        pltpu.SemaphoreType.DMA((3,))],
        args=[qkv, qkv, table], comm=comm)
    return ao, extra


def _attn_bwd(qkv, table, dao, comm=None):
    seq = qkv.shape[1]
    n_groups = seq // GROUP_Q
    fold_w = GROUP_K + (GROUP - 1) * CHUNK

    def body(ins, outs, scratch):
        q_ref, do_ref, qkv_hbm, t_hbm = ins
        dq_ref, dkt_hbm, dvt_hbm, db_ref, cs_ref = outs
        k_ref, v_ref, t_ref, db_acc, dkt_acc, dvt_acc, s_ref, dp_ref, p_ref, ds_ref, sems = scratch
        g = pl.program_id(0)

        load_k, load_v, load_t = _resident_copies(qkv_hbm, t_hbm, k_ref, v_ref, t_ref, sems)

        @pl.when(g == 0)
        def _():
            _start_resident((load_k, load_v, load_t), k_ref, v_ref)
            db_acc[...] = jnp.zeros_like(db_acc)
            dkt_acc[...] = jnp.zeros_like(dkt_acc)
            dvt_acc[...] = jnp.zeros_like(dvt_acc)
            cs_ref[...] = jnp.zeros_like(cs_ref)
            load_k.wait()
            load_v.wait()

        window = pl.ds(pl.multiple_of(g * GROUP_Q, GROUP_Q), GROUP_K)
        before_start = _before_start(g)
        for h in range(N_HEADS):
            cols = slice(h * HEAD_DIM, (h + 1) * HEAD_DIM)
            buf = h % 2
            qh, doh = q_ref[:, cols], do_ref[:, cols]
            kh, vh = k_ref[window, cols], v_ref[window, cols]
            s_ref[buf] = lax.dot_general(qh, kh, _DIMS["nt"], preferred_element_type=F32)
            dp_ref[buf] = lax.dot_general(doh, vh, _DIMS["nt"], preferred_element_type=F32)
            if h == 0:
                pl.when(g == 0)(load_t.wait)
            for row in range(0, GROUP_Q, SOFTMAX_ROWS):
                p_halves, ds_halves = [], []
                for r in (row, row + 8):
                    p = _softmax_rows(s_ref.at[buf], t_ref, h, before_start, slice(r, r + 8))
                    dp = dp_ref[buf, r:r + 8, :]
                    ds = p * (dp - jnp.sum(dp * p, axis=-1, keepdims=True))
                    chunk = r // CHUNK
                    shift = (GROUP - 1 - chunk) * CHUNK
                    db_acc[h, r - chunk * CHUNK:r - chunk * CHUNK + 8, shift:shift + GROUP_K] += ds
                    p_halves.append(p)
                    ds_halves.append(ds * (HEAD_DIM ** -0.5))
                p_ref[buf, row:row + SOFTMAX_ROWS, :] = jnp.concatenate(p_halves, axis=0).astype(BF16)
                ds_ref[buf, row:row + SOFTMAX_ROWS, :] = jnp.concatenate(ds_halves, axis=0).astype(BF16)
            dq_ref[:, cols] = jnp.dot(ds_ref[buf], kh, preferred_element_type=F32).astype(BF16)
            dkt_acc[cols, window] += lax.dot_general(qh, ds_ref[buf], _DIMS["tn"], preferred_element_type=F32)
            dvt_acc[cols, window] += lax.dot_general(doh, p_ref[buf], _DIMS["tn"], preferred_element_type=F32)
        cs_ref[0:1, :] += _colsum(dq_ref[...].astype(F32))

        @pl.when(g == n_groups - 1)
        def _():
            lo = (GROUP - 1) * CHUNK
            for h in range(N_HEADS):
                db_ref[h] = db_acc[h, :, lo:lo + BAND]
            inside = pl.ds(PAD_ROWS, seq)
            on_diagonal = (lax.broadcasted_iota(jnp.int32, (D_ATTN, D_ATTN), 0)
                           == lax.broadcasted_iota(jnp.int32, (D_ATTN, D_ATTN), 1))
            for row, acc in ((1, dkt_acc), (2, dvt_acc)):
                column = jnp.sum(acc[:, inside], axis=1, keepdims=True)
                cs_ref[row:row + 1, :] = _colsum(jnp.where(on_diagonal, column, 0.0))
            out_k = pltpu.make_async_copy(dkt_acc.at[:, inside], dkt_hbm, sems.at[0])
            out_v = pltpu.make_async_copy(dvt_acc.at[:, inside], dvt_hbm, sems.at[1])
            out_k.start()
            out_v.start()
            out_k.wait()
            out_v.wait()

    t_shape = (D_ATTN, seq + PAD_ROWS)
    q_tiles = pl.BlockSpec((None, GROUP_Q, D_ATTN), lambda g: (0, g, 0))
    outs, extra = _host_call(
        body, "attn_bwd", grid=(n_groups,),
        in_specs=[q_tiles, _tile(GROUP_Q, D_ATTN), ANY, ANY],
        out_specs=[_tile(GROUP_Q, D_ATTN), ANY, ANY, _full((N_HEADS, CHUNK, BAND)), _full((8, D_ATTN))],
        out_shape=[jax.ShapeDtypeStruct((seq, D_ATTN), BF16), jax.ShapeDtypeStruct((D_ATTN, seq), F32),
                   jax.ShapeDtypeStruct((D_ATTN, seq), F32), jax.ShapeDtypeStruct((N_HEADS, CHUNK, BAND), F32),
                   jax.ShapeDtypeStruct((8, D_ATTN), F32)],
        scratch_shapes=[pltpu.VMEM((seq + PAD_ROWS, D_ATTN), BF16), pltpu.VMEM((seq + PAD_ROWS, D_ATTN), BF16),
                        pltpu.VMEM(table.shape, F32), pltpu.VMEM((N_HEADS, CHUNK, fold_w), F32), pltpu.VMEM(t_shape, F32),
                        pltpu.VMEM(t_shape, F32), pltpu.VMEM((2, GROUP_Q, GROUP_K), F32),
                        pltpu.VMEM((2, GROUP_Q, GROUP_K), F32), pltpu.VMEM((2, GROUP_Q, GROUP_K), BF16),
                        pltpu.VMEM((2, GROUP_Q, GROUP_K), BF16), pltpu.SemaphoreType.DMA((3,))],
        args=[qkv, dao, qkv, table], comm=comm)
    return outs, extra


def _assemble_dz(dq, dkt, dvt, dglu_a, dglu_b, dga, dgb):
    seq = dq.shape[0]
    rows = 512
    transposed = pl.BlockSpec((D_ATTN, rows), lambda i: (0, i))

    def body(dq_ref, dkt_ref, dvt_ref, da_ref, db_ref, dga_ref, dgb_ref, o_ref):
        o_ref[:, 0:D_ATTN] = dq_ref[...]
        o_ref[:, D_ATTN:2 * D_ATTN] = dkt_ref[...].T.astype(BF16)
        o_ref[:, 2 * D_ATTN:3 * D_ATTN] = dvt_ref[...].T.astype(BF16)
        off = 3 * D_ATTN
        for ref in (da_ref, db_ref, dga_ref, dgb_ref):
            width = ref.shape[1]
            o_ref[:, off:off + width] = ref[...]
            off += width

    width = 3 * D_ATTN + 2 * D_CONV + 2 * D_MODEL
    return pl.pallas_call(
        body, name="assemble_dz", out_shape=jax.ShapeDtypeStruct((seq, width), BF16), grid=(seq // rows,),
        in_specs=[_tile(rows, D_ATTN), transposed, transposed, _tile(rows, D_CONV), _tile(rows, D_CONV),
                  _tile(rows, D_MODEL), _tile(rows, D_MODEL)],
        out_specs=_tile(rows, width), compiler_params=_cparams(1),
    )(dq, dkt, dvt, dglu_a, dglu_b, dga, dgb)


CONV_ROWS = 256


def _ln_silu(u1, g, b):
    mu = jnp.mean(u1, axis=-1, keepdims=True)
    xc = u1 - mu
    rs = lax.rsqrt(jnp.mean(xc * xc, axis=-1, keepdims=True) + EPS)
    xhat = xc * rs
    u2 = xhat * g + b
    return xhat, rs, u2


def _glu_into(s_ref, a_ref, b_ref, ah_ref, bh_ref, first):
    halo = ah_ref[...] * _sig(bh_ref[...])
    s_ref[0:CONV_HALO, :] = jnp.where(first, 0.0, halo)
    s_ref[CONV_HALO:CONV_HALO + CONV_ROWS, :] = a_ref[...] * _sig(b_ref[...])


CONV_LANES = 128
CONV_TILES = CONV_ROWS // 8


def _lag_weights(w_ref, lanes):
    return {e: jnp.broadcast_to(w_ref[CONV_K - 1 - e:CONV_K - e, lanes], (8, CONV_LANES)) for e in range(CONV_K)}


def _class_sums(w, tiles, k):
    total = None
    for a, tile in enumerate(tiles):
        if 8 * a + k < CONV_K:
            term = w[8 * a + k] * tile
            total = term if total is None else total + term
    return total


def _conv_back(src_ref, first_tile, w, lanes, row_id, emit):
    before = None
    for m in range(-1, CONV_TILES):
        tiles = [src_ref[8 * (first_tile + m - a):8 * (first_tile + m - a) + 8, lanes] for a in range(4)]
        rolled = [None] + [pltpu.roll(_class_sums(w, tiles, k), k, 0) for k in range(1, 8)]
        if m >= 0:
            out = _class_sums(w, tiles, 0)
            for k in range(1, 8):
                out = out + jnp.where(row_id < k, before[k], rolled[k])
            emit(m, out)
        before = rolled


def _conv_ahead(src_ref, w, lanes, row_id, emit):
    before = None
    for m in range(CONV_TILES + 1):
        tiles = [src_ref[8 * (m + a):8 * (m + a) + 8, lanes] for a in range(4)]
        rolled = [None] + [pltpu.roll(_class_sums(w, tiles, k), 8 - k, 0) for k in range(1, 8)]
        if m >= 1:
            out = before[0]
            for k in range(1, 8):
                out = out + jnp.where(row_id < 8 - k, before[k], rolled[k])
            emit(m - 1, out)
        before = [_class_sums(w, tiles, 0) if m < CONV_TILES else None] + rolled[1:]


def _conv_weight_sums(d_ref, s_ref, lanes, row_id, whole_shifts):
    zero = jnp.zeros((8, CONV_LANES), F32)
    sums = {8 * a + k: zero for a in whole_shifts for k in range(8) if 8 * a + k < CONV_K}

    def d_tile(m):
        return d_ref[8 * m:8 * m + 8, lanes] if 0 <= m < CONV_TILES else zero

    rolled = [None] + [zero] * 7
    for m in range(-1, CONV_TILES):
        cur, nxt = d_tile(m), d_tile(m + 1)
        rolled_next = [None] + [pltpu.roll(nxt, 8 - k, 0) for k in range(1, 8)]
        shifted = [cur] + [jnp.where(row_id < 8 - k, rolled[k], rolled_next[k]) for k in range(1, 8)]
        for a in whole_shifts:
            tile = s_ref[8 * (CONV_HALO // 8 + m - a):8 * (CONV_HALO // 8 + m - a) + 8, lanes]
            for k in range(8):
                if 8 * a + k < CONV_K and not (m < 0 and k == 0):
                    sums[8 * a + k] = sums[8 * a + k] + shifted[k] * tile
        rolled = rolled_next
    return sums


def _conv_fwd(zr, w_dw, b_dw, g_ln, b_ln, comm=None):
    seq = zr.shape[0]

    def body(a_ref, b_ref, ah_ref, bh_ref, w_ref, bias_ref, g_ref, bl_ref, u1_ref, u3_ref, s_ref):
        _glu_into(s_ref, a_ref, b_ref, ah_ref, bh_ref, pl.program_id(0) == 0)
        row_id = lax.broadcasted_iota(jnp.int32, (8, CONV_LANES), 0)
        for lo in range(0, D_CONV, CONV_LANES):
            lanes = slice(lo, lo + CONV_LANES)
            bias = jnp.broadcast_to(bias_ref[:, lanes], (8, CONV_LANES))

            def emit(m, out, lanes=lanes, bias=bias):
                u1_ref[8 * m:8 * m + 8, lanes] = out + bias

            _conv_back(s_ref, CONV_HALO // 8, _lag_weights(w_ref, lanes), lanes, row_id, emit)
        _, _, u2 = _ln_silu(u1_ref[...], g_ref[...], bl_ref[...])
        u3_ref[...] = (u2 * _sig(u2)).astype(BF16)

    return _host_call(
        lambda ins, outs, scratch: body(*ins, *outs, *scratch), "conv_fwd", grid=(seq // CONV_ROWS,),
        in_specs=[_tile(CONV_ROWS, D_CONV, 0), _tile(CONV_ROWS, D_CONV, 1),
                  _prev(CONV_HALO, D_CONV, CONV_ROWS, 0), _prev(CONV_HALO, D_CONV, CONV_ROWS, 1),
                  _full((CONV_K, D_CONV)), _full((1, D_CONV)), _full((1, D_CONV)), _full((1, D_CONV))],
        out_specs=[_tile(CONV_ROWS, D_CONV), _tile(CONV_ROWS, D_CONV)],
        out_shape=[jax.ShapeDtypeStruct((seq, D_CONV), F32), jax.ShapeDtypeStruct((seq, D_CONV), BF16)],
        scratch_shapes=[pltpu.VMEM((CONV_HALO + CONV_ROWS, D_CONV), F32)],
        args=[zr, zr, zr, zr, w_dw, b_dw, g_ln, b_ln], comm=comm)


def _conv_bwd(zr, u1, du3, w_dw, g_ln, b_ln, comm=None):
    seq = zr.shape[0]
    n_tiles = seq // CONV_ROWS
    n_halo = seq // CONV_HALO
    ext = CONV_ROWS + CONV_HALO

    def body(a_ref, b_ref, ah_ref, bh_ref, u1_ref, u1n_ref, d3_ref, d3n_ref, w_ref, g_ref, bl_ref,
             da_ref, db_ref, dw_ref, small_ref, s_ref, d_ref, du0_ref):
        i = pl.program_id(0)

        @pl.when(i == 0)
        def _():
            dw_ref[...] = jnp.zeros_like(dw_ref)
            small_ref[...] = jnp.zeros_like(small_ref)

        _glu_into(s_ref, a_ref, b_ref, ah_ref, bh_ref, i == 0)
        gv, bv = g_ref[...], bl_ref[...]

        def du1_of(u1, d3):
            xhat, rs, u2 = _ln_silu(u1, gv, bv)
            sg = _sig(u2)
            du2 = d3 * (sg * (1.0 + u2 * (1.0 - sg)))
            dxh = du2 * gv
            du1 = rs * (dxh - jnp.mean(dxh, axis=-1, keepdims=True)
                        - xhat * jnp.mean(dxh * xhat, axis=-1, keepdims=True))
            return du1, du2, xhat

        du1, du2, xhat = du1_of(u1_ref[...], d3_ref[...])
        du1n, _, _ = du1_of(u1n_ref[...], d3n_ref[...])
        d_ref[0:CONV_ROWS, :] = du1
        d_ref[CONV_ROWS:ext, :] = jnp.where(i == n_tiles - 1, 0.0, du1n)
        small_ref[0:1, :] += _colsum(du1)
        small_ref[1:2, :] += _colsum(du2 * xhat)
        small_ref[2:3, :] += _colsum(du2)
        row_id = lax.broadcasted_iota(jnp.int32, (8, CONV_LANES), 0)
        for lo in range(0, D_CONV, CONV_LANES):
            lanes = slice(lo, lo + CONV_LANES)

            def emit(m, out, lanes=lanes):
                du0_ref[8 * m:8 * m + 8, lanes] = out

            _conv_ahead(d_ref, _lag_weights(w_ref, lanes), lanes, row_id, emit)
            for whole_shifts in ((0, 1), (2, 3)):
                for e, total in _conv_weight_sums(d_ref, s_ref, lanes, row_id, whole_shifts).items():
                    dw_ref[CONV_K - 1 - e:CONV_K - e, lanes] += _colsum(total)
        du0 = du0_ref[...]
        sb = _sig(b_ref[...])
        da = du0 * sb
        dbv = du0 * a_ref[...] * sb * (1.0 - sb)
        da_ref[...] = da.astype(BF16)
        db_ref[...] = dbv.astype(BF16)
        small_ref[3:4, :] += _colsum(da)
        small_ref[4:5, :] += _colsum(dbv)

    return _host_call(
        lambda ins, outs, scratch: body(*ins, *outs, *scratch), "conv_bwd", grid=(n_tiles,),
        in_specs=[_tile(CONV_ROWS, D_CONV, 0), _tile(CONV_ROWS, D_CONV, 1),
                  _prev(CONV_HALO, D_CONV, CONV_ROWS, 0), _prev(CONV_HALO, D_CONV, CONV_ROWS, 1),
                  _tile(CONV_ROWS, D_CONV), _next(CONV_HALO, D_CONV, CONV_ROWS, n_halo),
                  _tile(CONV_ROWS, D_CONV), _next(CONV_HALO, D_CONV, CONV_ROWS, n_halo),
                  _full((CONV_K, D_CONV)), _full((1, D_CONV)), _full((1, D_CONV))],
        out_specs=[_tile(CONV_ROWS, D_CONV), _tile(CONV_ROWS, D_CONV), _full((CONV_HALO, D_CONV)),
                   _full((8, D_CONV))],
        out_shape=[jax.ShapeDtypeStruct((seq, D_CONV), BF16), jax.ShapeDtypeStruct((seq, D_CONV), BF16),
                   jax.ShapeDtypeStruct((CONV_HALO, D_CONV), F32), jax.ShapeDtypeStruct((8, D_CONV), F32)],
        scratch_shapes=[pltpu.VMEM((ext, D_CONV), F32), pltpu.VMEM((ext, D_CONV), F32),
                        pltpu.VMEM((CONV_ROWS, D_CONV), F32)],
        args=[zr, zr, zr, zr, u1, u1, du3, du3, w_dw, g_ln, b_ln], comm=comm)


MERGE_ROWS = 256


def _merge_fwd(ao, u3, zr, w_ao, w_co, b_co):
    seq = ao.shape[0]

    def body(ao_ref, u3_ref, ga_ref, gb_ref, wa_ref, wc_ref, bc_ref, y_ref, a_ref, cb_ref):
        a = jnp.dot(ao_ref[...], wa_ref[...], preferred_element_type=F32)
        cb = jnp.dot(u3_ref[...], wc_ref[...], preferred_element_type=F32) + bc_ref[...]
        a_ref[...] = a
        cb_ref[...] = cb
        y_ref[...] = (_sig(ga_ref[...]) * a + _sig(gb_ref[...]) * cb).astype(BF16)

    f32_out = jax.ShapeDtypeStruct((seq, D_MODEL), F32)
    return pl.pallas_call(
        body, name="merge_fwd",
        out_shape=[jax.ShapeDtypeStruct((seq, D_MODEL), BF16), f32_out, f32_out],
        grid=(seq // MERGE_ROWS,),
        in_specs=[_tile(MERGE_ROWS, D_ATTN), _tile(MERGE_ROWS, D_CONV), _tile(MERGE_ROWS, D_MODEL, 1),
                  _tile(MERGE_ROWS, D_MODEL, 2), _full(w_ao.shape), _full(w_co.shape), _full((1, D_MODEL))],
        out_specs=[_tile(MERGE_ROWS, D_MODEL)] * 3, compiler_params=_cparams(1),
    )(ao, u3, zr, zr, w_ao, w_co, b_co)


def _merge_bwd(a, cb, zr, rows):
    seq = a.shape[0]

    def fn(dy_v, first, ins, outs):
        a_ref, cb_ref, ga_ref, gb_ref = ins
        da_ref, dcb_ref, dga_ref, dgb_ref, small_ref = outs

        @pl.when(first)
        def _():
            small_ref[...] = jnp.zeros_like(small_ref)

        sa, sb = _sig(ga_ref[...]), _sig(gb_ref[...])
        dcb = dy_v * sb
        dga = dy_v * a_ref[...] * sa * (1.0 - sa)
        dgb = dy_v * cb_ref[...] * sb * (1.0 - sb)
        da_ref[...] = (dy_v * sa).astype(BF16)
        dcb_ref[...] = dcb.astype(BF16)
        dga_ref[...] = dga.astype(BF16)
        dgb_ref[...] = dgb.astype(BF16)
        small_ref[0:1, :] += _colsum(dga)
        small_ref[1:2, :] += _colsum(dgb)
        small_ref[2:3, :] += _colsum(dcb)

    bf = jax.ShapeDtypeStruct((seq, D_MODEL), BF16)
    gate = lambda col: pl.BlockSpec((rows, D_MODEL), lambda i, j: (i, col))
    return _Epilogue(
        [a, cb, zr, zr], [_row_tile(rows, D_MODEL), _row_tile(rows, D_MODEL), gate(1), gate(2)],
        [bf, bf, bf, bf, jax.ShapeDtypeStruct((8, D_MODEL), F32)],
        [_row_tile(rows, D_MODEL)] * 4 + [_whole((8, D_MODEL))], fn, keep_product=False)


FFN_ROWS = 2048
FFN_BLOCKS = D_FF // FFN_COLS
GELU_C = math.sqrt(2.0 / math.pi)


def _gelu(v):
    t = jnp.tanh(GELU_C * (v + 0.044715 * (v * v * v)))
    return 0.5 * v * (1.0 + t), t


def _gelu_grad(v, t):
    return 0.5 * (1.0 + t) + 0.5 * v * (1.0 - t * t) * (GELU_C * (1.0 + 3.0 * 0.044715 * (v * v)))


def _sublane_rows(ref, n):
    return [jnp.broadcast_to(ref[r:r + 1, :], (8, FFN_COLS)) for r in range(n)]


def _rolls(tile, shifts):
    return tuple(pltpu.roll(tile, s, 0) for s in shifts)


def _behind(prev_rolls, cur, row_id):
    rolls = _rolls(cur, (1, 2))
    x1 = jnp.where(row_id < 1, prev_rolls[0], rolls[0])
    x2 = jnp.where(row_id < 2, prev_rolls[1], rolls[1])
    return (x2, x1, cur), rolls


def _ahead(cur_rolls, next_rolls, row_id):
    return (jnp.where(row_id < 7, cur_rolls[0], next_rolls[0]), jnp.where(row_id < 6, cur_rolls[1], next_rolls[1]))


def _conv3(taps, w, bias):
    return w[0] * taps[0] + w[1] * taps[1] + w[2] * taps[2] + bias


def _ffn_specs(rows):
    tile = lambda off: pl.BlockSpec((rows, FFN_COLS), lambda j, i: (i, j + off))
    prev = lambda off: pl.BlockSpec((FFN_HALO, FFN_COLS),
                                    lambda j, i: (jnp.maximum(i * (rows // FFN_HALO) - 1, 0), j + off))
    wgt = lambda off: pl.BlockSpec((3, FFN_COLS), lambda j, i: (0, j + off))
    vec = lambda off: pl.BlockSpec((1, FFN_COLS), lambda j, i: (0, j + off))
    return tile, prev, wgt, vec


def _ffn_act(up, w_dw, b_dw):
    seq = up.shape[0]
    tile, prev, wgt, vec = _ffn_specs(FFN_ROWS)

    def body(v_ref, g_ref, vp_ref, gp_ref, wv_ref, wg_ref, bv_ref, bg_ref, act_ref):
        first = pl.program_id(1) == 0
        row_id = lax.broadcasted_iota(jnp.int32, (8, FFN_COLS), 0)
        wv, wg = _sublane_rows(wv_ref, 3), _sublane_rows(wg_ref, 3)
        (bv,), (bg,) = _sublane_rows(bv_ref, 1), _sublane_rows(bg_ref, 1)
        rolls_v = _rolls(jnp.where(first, 0.0, vp_ref[...]), (1, 2))
        rolls_g = _rolls(jnp.where(first, 0.0, gp_ref[...]), (1, 2))
        for row in range(0, FFN_ROWS, 16):
            halves = []
            for r in (row, row + 8):
                taps_v, rolls_v = _behind(rolls_v, v_ref[r:r + 8, :], row_id)
                taps_g, rolls_g = _behind(rolls_g, g_ref[r:r + 8, :], row_id)
                halves.append(_gelu(_conv3(taps_g, wg, bg))[0] * _conv3(taps_v, wv, bv))
            act_ref[row:row + 16, :] = jnp.concatenate(halves, axis=0).astype(BF16)

    return pl.pallas_call(
        body, name="ffn_act", out_shape=jax.ShapeDtypeStruct((seq, D_FF), BF16),
        grid=(FFN_BLOCKS, seq // FFN_ROWS),
        in_specs=[tile(0), tile(FFN_BLOCKS), prev(0), prev(FFN_BLOCKS), wgt(0), wgt(FFN_BLOCKS),
                  vec(0), vec(FFN_BLOCKS)],
        out_specs=tile(0), compiler_params=_cparams(2),
    )(up, up, up, up, w_dw, w_dw, b_dw, b_dw)


def _ffn_act_bwd(up, dact, w_dw, b_dw, comm=None):
    seq = up.shape[0]
    n_tiles = seq // FFN_ROWS
    n_halo = seq // FFN_HALO
    tile, prev, wgt, vec = _ffn_specs(FFN_ROWS)
    nxt = lambda off: pl.BlockSpec(
        (FFN_HALO, FFN_COLS), lambda j, i: (jnp.minimum((i + 1) * (FFN_ROWS // FFN_HALO), n_halo - 1), j + off))
    acc = lambda off: pl.BlockSpec((8, FFN_COLS), lambda j, i: (0, j + off))

    def body(v_ref, g_ref, vp_ref, gp_ref, vn_ref, gn_ref, da_ref, dan_ref, wv_ref, wg_ref, bv_ref, bg_ref,
             dv_out, dg_out, dwv_ref, dwg_ref, dbv_ref, dbg_ref):
        i = pl.program_id(1)
        first, last = i == 0, i == n_tiles - 1

        @pl.when(first)
        def _():
            for r in (dwv_ref, dwg_ref, dbv_ref, dbg_ref):
                r[...] = jnp.zeros_like(r)

        row_id = lax.broadcasted_iota(jnp.int32, (8, FFN_COLS), 0)
        wv, wg = _sublane_rows(wv_ref, 3), _sublane_rows(wg_ref, 3)
        (bv,), (bg,) = _sublane_rows(bv_ref, 1), _sublane_rows(bg_ref, 1)
        zero = jnp.zeros((8, FFN_COLS), F32)
        sums_v, sums_g = [zero] * 4, [zero] * 4
        rolls_v = _rolls(jnp.where(first, 0.0, vp_ref[...]), (1, 2))
        rolls_g = _rolls(jnp.where(first, 0.0, gp_ref[...]), (1, 2))
        behind = None
        done_v, done_g = [], []

        def grads(v_tile, g_tile, dact, rolls_v, rolls_g):
            taps_v, rolls_v = _behind(rolls_v, v_tile, row_id)
            taps_g, rolls_g = _behind(rolls_g, g_tile, row_id)
            val, gate = _conv3(taps_v, wv, bv), _conv3(taps_g, wg, bg)
            gel, t = _gelu(gate)
            return dact * gel, dact * val * _gelu_grad(gate, t), taps_v, taps_g, rolls_v, rolls_g

        def finish(tile, nxt, row):
            for (d, d_rolls), (_, n_rolls), w, done, o_ref in ((tile[0], nxt[0], wv, done_v, dv_out),
                                                               (tile[1], nxt[1], wg, done_g, dg_out)):
                d1, d2 = _ahead(d_rolls, n_rolls, row_id)
                done.append(w[2] * d + w[1] * d1 + w[0] * d2)
                if len(done) == 2:
                    o_ref[row - 16:row, :] = jnp.concatenate(done, axis=0).astype(BF16)
                    done.clear()

        for row in range(0, FFN_ROWS, 16):
            dact16 = da_ref[row:row + 16, :].astype(F32)
            for r, dact in ((row, dact16[0:8, :]), (row + 8, dact16[8:16, :])):
                dval, dgate, taps_v, taps_g, rolls_v, rolls_g = grads(v_ref[r:r + 8, :], g_ref[r:r + 8, :], dact,
                                                                      rolls_v, rolls_g)
                sums_v = [s + dval * x for s, x in zip(sums_v, taps_v)] + [sums_v[3] + dval]
                sums_g = [s + dgate * x for s, x in zip(sums_g, taps_g)] + [sums_g[3] + dgate]
                tile = ((dval, _rolls(dval, (7, 6))), (dgate, _rolls(dgate, (7, 6))))
                if behind is not None:
                    finish(behind, tile, r)
                behind = tile
        dact_next = jnp.where(last, 0.0, dan_ref[...].astype(F32)[0:FFN_HALO, :])
        dval, dgate, *_ = grads(vn_ref[...], gn_ref[...], dact_next, rolls_v, rolls_g)
        finish(behind, ((dval, _rolls(dval, (7, 6))), (dgate, _rolls(dgate, (7, 6)))), FFN_ROWS)
        for sums, dw_ref, db_ref in ((sums_v, dwv_ref, dbv_ref), (sums_g, dwg_ref, dbg_ref)):
            for tap in range(3):
                dw_ref[tap:tap + 1, :] += _colsum(sums[tap])
            db_ref[0:1, :] += _colsum(sums[3])

    half = jax.ShapeDtypeStruct((seq, D_FF), BF16)
    acc_shape = jax.ShapeDtypeStruct((8, D_FF), F32)
    return _host_call(
        lambda ins, outs, scratch: body(*ins, *outs, *scratch), "ffn_act_bwd", grid=(FFN_BLOCKS, n_tiles),
        in_specs=[tile(0), tile(FFN_BLOCKS), prev(0), prev(FFN_BLOCKS), nxt(0), nxt(FFN_BLOCKS),
                  tile(0), pl.BlockSpec((16, FFN_COLS), lambda j, i: (
                      jnp.minimum((i + 1) * (FFN_ROWS // 16), seq // 16 - 1), j)),
                  wgt(0), wgt(FFN_BLOCKS), vec(0), vec(FFN_BLOCKS)],
        out_specs=[tile(0), tile(0), acc(0), acc(0), acc(0), acc(0)],
        out_shape=[half, half, acc_shape, acc_shape, acc_shape, acc_shape],
        scratch_shapes=[], args=[up, up, up, up, up, up, dact, dact, w_dw, w_dw, b_dw, b_dw], comm=comm)


def _cols_to_blocks(full_cols):
    k, n8 = full_cols.shape
    return jnp.transpose(full_cols.reshape(k, N_DEV, n8 // N_DEV), (1, 0, 2))


def _rows_to_blocks(full_rows):
    r8, n = full_rows.shape
    return full_rows.reshape(N_DEV, r8 // N_DEV, n)


def _blocks_to_cols(gathered):
    _, k, n = gathered.shape
    return jnp.transpose(gathered, (1, 0, 2)).reshape(k, N_DEV * n)


def kernel(x, c, w_ada, b_ada, g_pre_mix, g_post_mix, w_in, b_in, rel_bias, w_attn_o, w_dw_conv, b_dw_conv, g_conv_ln, b_conv_ln, w_conv_o, b_conv_o, w_mix_o, g_pre_ffn, g_post_ffn, w_up, w_dw_ffn, b_dw_ffn, w_down, loss_target, m_w_ada, m_b_ada, m_g_pre_mix, m_g_post_mix, m_w_in, m_b_in, m_rel_bias, m_w_attn_o, m_w_dw_conv, m_b_dw_conv, m_g_conv_ln, m_b_conv_ln, m_w_conv_o, m_b_conv_o, m_w_mix_o, m_g_pre_ffn, m_g_post_ffn, m_w_up, m_w_dw_ffn, m_b_dw_ffn, m_w_down, v_w_ada, v_b_ada, v_g_pre_mix, v_g_post_mix, v_w_in, v_b_in, v_rel_bias, v_w_attn_o, v_w_dw_conv, v_b_dw_conv, v_g_conv_ln, v_b_conv_ln, v_w_conv_o, v_b_conv_o, v_w_mix_o, v_g_pre_ffn, v_g_post_ffn, v_w_up, v_w_dw_ffn, v_b_dw_ffn, v_w_down):
    names = ["w_ada", "b_ada", "g_pre_mix", "g_post_mix", "w_in", "b_in", "rel_bias", "w_attn_o", "w_dw_conv",
             "b_dw_conv", "g_conv_ln", "b_conv_ln", "w_conv_o", "b_conv_o", "w_mix_o", "g_pre_ffn", "g_post_ffn",
             "w_up", "w_dw_ffn", "b_dw_ffn", "w_down"]
    weights = dict(zip(names, [w_ada, b_ada, g_pre_mix, g_post_mix, w_in, b_in, rel_bias, w_attn_o, w_dw_conv,
                               b_dw_conv, g_conv_ln, b_conv_ln, w_conv_o, b_conv_o, w_mix_o, g_pre_ffn,
                               g_post_ffn, w_up, w_dw_ffn, b_dw_ffn, w_down]))
    mom_m = dict(zip(names, [m_w_ada, m_b_ada, m_g_pre_mix, m_g_post_mix, m_w_in, m_b_in, m_rel_bias, m_w_attn_o,
                             m_w_dw_conv, m_b_dw_conv, m_g_conv_ln, m_b_conv_ln, m_w_conv_o, m_b_conv_o,
                             m_w_mix_o, m_g_pre_ffn, m_g_post_ffn, m_w_up, m_w_dw_ffn, m_b_dw_ffn, m_w_down]))
    mom_v = dict(zip(names, [v_w_ada, v_b_ada, v_g_pre_mix, v_g_post_mix, v_w_in, v_b_in, v_rel_bias, v_w_attn_o,
                             v_w_dw_conv, v_b_dw_conv, v_g_conv_ln, v_b_conv_ln, v_w_conv_o, v_b_conv_o,
                             v_w_mix_o, v_g_pre_ffn, v_g_post_ffn, v_w_up, v_w_dw_ffn, v_b_dw_ffn, v_w_down]))
    shapes = {n: w.shape for n, w in weights.items()}

    seq = x.shape[1]
    me = 4 * lax.axis_index("x") + 2 * lax.axis_index("y") + lax.axis_index("c")
    x2 = x.reshape(seq, D_MODEL)
    target = loss_target.reshape(seq, D_MODEL)
    sq = lambda a: a.reshape(a.shape[1:])
    bf = lambda a: sq(a).astype(BF16)

    transposed = lambda a: jnp.swapaxes(sq(a), 0, 1)

    c_all, mod_all = _ada_mod(c, sq(w_ada))
    c_all = c_all.reshape(N_DEV, D_MODEL)
    mod = lax.dynamic_index_in_dim(mod_all, me, axis=1, keepdims=False)
    mod6 = (mod.reshape(1, 6 * D_MODEL) + b_ada).reshape(6, D_MODEL)

    h1, (g_in, g_dwc, g_dwf) = _pre_mix(
        x2, mod6, g_pre_mix, comm=_gather_comm([transposed(w_in).astype(BF16), sq(w_dw_conv), sq(w_dw_ffn)]))
    wt_in = g_in.reshape(g_in.shape[0] * g_in.shape[1], D_MODEL)
    wf_dwc = _blocks_to_cols(g_dwc)
    wf_dwf = _blocks_to_cols(g_dwf)
    qkv = _mm(h1, wt_in, "nt", BF16, "in_proj_qkv", bias=b_in, tm=1024, tn=D_ATTN, cols=(0, 3 * D_ATTN),
              stacked=True)
    zr, _, (g_ao, g_co, g_mo) = _mm(h1, wt_in, "nt", F32, "in_proj_rest", bias=b_in, tm=1024, tn=3 * D_ATTN,
                                 cols=(3 * D_ATTN, 2 * D_CONV + 2 * D_MODEL),
                                 comm=_gather_comm([bf(w_attn_o), bf(w_conv_o), bf(w_mix_o)]))
    table = jnp.transpose(_bias_table(sq(rel_bias)), (1, 0, 2))
    ao, (g_up,) = _attn_fwd(qkv, table, comm=_gather_comm([transposed(w_up).astype(BF16)]))
    (u1, u3), (g_dn,) = _conv_fwd(zr, wf_dwc, b_dw_conv, g_conv_ln, b_conv_ln, comm=_gather_comm([bf(w_down)]))
    wf_ao = _blocks_to_cols(g_ao)
    wf_co = _blocks_to_cols(g_co)
    wf_mo = g_mo.reshape(D_MODEL, D_MODEL)
    wt_up = g_up.reshape(g_up.shape[0] * g_up.shape[1], D_MODEL)
    wf_dn = g_dn.reshape(D_FF, D_MODEL)
    y, a_br, cb_br = _merge_fwd(ao, u3, zr, wf_ao, wf_co, b_conv_o)
    ymix, (x1, h2), _ = _mm(y, wf_mo, "nn", F32, "mix_o", tm=512, tn=D_MODEL,
                            epilogue=_post_mix_pre_ffn(x2, mod6, g_post_mix, g_pre_ffn, 512))
    up = _mm(h2, wt_up, "nt", F32, "ffn_up", tm=1024, tn=1408)
    act = _ffn_act(up, wf_dwf, b_dw_ffn)
    _, (loss_lanes, dout, dyf, small_f), _ = _mm(act, wf_dn, "nn", F32, "ffn_down", tm=512, tn=D_MODEL,
                                                 epilogue=_final(x1, target, mod6, g_post_ffn, 512))

    dact = _mm(dyf, wf_dn, "nt", BF16, "ffn_down_dx", tm=1024, tn=1408)
    gw_down = _mm(act, dyf, "tn", BF16, "ffn_down_dw", tm=256, tn=1024)
    (dup_v, dup_g, dwv, dwg, dbv, dbg), (parts_down,) = _ffn_act_bwd(
        up, dact, wf_dwf, b_dw_ffn, comm=_scatter_comm([_rows_to_blocks(gw_down)]))
    _, (dx1, dymix, small_m), _ = _mm([dup_v, dup_g], wt_up, "nn", F32, "ffn_up_dx", tm=512, tn=D_MODEL,
                                      epilogue=_mid_bwd(x1, dout, ymix, mod6, g_pre_ffn, g_post_mix, 512))
    blocks_up = _rows_to_blocks(_mm_tn_rows([dup_v, dup_g], h2, "ffn_up_dw"))
    _, (da, dcb, dga, dgb, small_g), _ = _mm(dymix, wf_mo, "nt", F32, "mix_o_dx", tm=512, tn=D_MODEL,
                                             epilogue=_merge_bwd(a_br, cb_br, zr, 512))
    gw_mo = _mm(y, dymix, "tn", BF16, "mix_o_dw")
    dao = _mm(da, wf_ao, "nt", BF16, "attn_o_dx", tm=1024)
    gw_ao = _mm(ao, da, "tn", BF16, "attn_o_dw")
    du3 = _mm(dcb, wf_co, "nt", F32, "conv_o_dx", tm=1024)
    gw_co = _mm(u3, dcb, "tn", BF16, "conv_o_dw")
    (dq, dkt, dvt, dbias, small_a), (parts_up,) = _attn_bwd(
        qkv, table, dao, comm=_scatter_comm([blocks_up]))
    g_rel = _bias_grad(jnp.transpose(dbias, (1, 0, 2)))
    (dglu_a, dglu_b, dw_conv, small_c), (parts_mo, parts_ao, parts_co) = _conv_bwd(
        zr, u1, du3, wf_dwc, g_conv_ln, b_conv_ln,
        comm=_scatter_comm([_rows_to_blocks(gw_mo), _cols_to_blocks(gw_ao), _cols_to_blocks(gw_co)]))
    dz = _assemble_dz(dq, dkt, dvt, dglu_a, dglu_b, dga, dgb)
    blocks_in = _rows_to_blocks(_mm(dz, h1, "tn", BF16, "in_proj_dw", tm=512, tn=D_MODEL))
    _, (grad_x, small_x), (parts_in, _, _) = _mm(dz, wt_in, "nn", F32, "in_proj_dx", tm=512, tn=D_MODEL,
                                                 comm=_pair_scatter_comm(blocks_in),
                                                 epilogue=_pre_mix_bwd(x2, dx1, mod6, g_pre_mix, 512))

    packed = _pack_grads(small_x, small_m, small_f, small_g, small_a, small_c, dbv, dbg, dwv, dwg, dw_conv)
    gathered, gathered_rel, gathered_loss = _run_comm(_gather_comm([packed, g_rel, loss_lanes]), "gather_small")
    gathered = gathered.reshape(N_DEV, PACKED_TOTAL)
    updates, g_dwc_full, g_dwf_full, loss_all = _small_adamw(gathered, gathered_rel, gathered_loss, weights, mom_m,
                                                             mom_v)
    loss = loss_all[0, 0]

    grads, deltas, new_m, new_v = {}, {}, {}, {}

    def record(name, update, is_transposed=False):
        for dst, val in zip((grads, deltas, new_m, new_v), update):
            dst[name] = (jnp.swapaxes(val, 0, 1) if is_transposed else val).reshape(shapes[name])

    for name, update in updates.items():
        record(name, update)

    def local_update(name, grad):
        record(name, _adamw(sq(weights[name]), sq(mom_m[name]), sq(mom_v[name]), "adamw_" + name, g=grad))

    conv_cols, ffn_cols, ada_cols = D_CONV // N_DEV, 2 * D_FF // N_DEV, 6 * D_MODEL // N_DEV
    local_update("w_dw_conv", lax.dynamic_slice(g_dwc_full, (0, me * conv_cols), (CONV_K, conv_cols)))
    local_update("w_dw_ffn", lax.dynamic_slice(g_dwf_full, (0, me * ffn_cols), (3, ffn_cols)))
    local_update("w_ada", _ada_grad(c_all, lax.dynamic_slice(gathered, (0, me * ada_cols), (N_DEV, ada_cols))))

    for name, part in (("w_attn_o", parts_ao), ("w_conv_o", parts_co), ("w_mix_o", parts_mo), ("w_down", parts_down)):
        record(name, _adamw(sq(weights[name]), sq(mom_m[name]), sq(mom_v[name]), "adamw_" + name, parts=part))
    for name, part in (("w_in", parts_in), ("w_up", parts_up)):
        record(name, _adamw(transposed(weights[name]), transposed(mom_m[name]), transposed(mom_v[name]),
                            "adamw_" + name, parts=part), is_transposed=True)

    return (loss, grad_x.reshape(x.shape), *[grads[n] for n in names], *[deltas[n] for n in names],
            *[new_m[n] for n in names], *[new_v[n] for n in names])
```

```python
import functools
import math

import jax
import jax.numpy as jnp
from jax import lax
from jax.experimental import pallas as pl
from jax.experimental.pallas import tpu as pltpu

F32 = jnp.float32
BF16 = jnp.bfloat16
HIGHEST = lax.Precision.HIGHEST

D_MODEL = 1024
CHUNK = 64
LEFT_CHUNKS = 8
BAND = (LEFT_CHUNKS + 1) * CHUNK
PAD_ROWS = LEFT_CHUNKS * CHUNK
GROUP = 4
GROUP_Q = GROUP * CHUNK
GROUP_K = GROUP_Q + PAD_ROWS
SOFTMAX_ROWS = 16
TOEPLITZ = 640
N_HEADS = 8
HEAD_DIM = 64
D_ATTN = 512
D_CONV = 512
CONV_K = 31
CONV_HALO = 32
MAX_REL = 128
N_REL = 2 * MAX_REL + 1
D_FF = 2816
FFN_HALO = 8
FFN_COLS = 256
EPS = 1e-6
NEG_INF = -1e30
N_DEV = 8

ADAM_LR = 0.001
ADAM_B1 = 0.9
ADAM_B2 = 0.999
ADAM_EPS = 1e-08
ADAM_WD = 0.01
ADAM_STEP = 10

VMEM_LIMIT_BYTES = 56 * 1024 * 1024
ADAMW_BLOCK_BYTES = 768 * 1024

MESH = pl.DeviceIdType.MESH
ANY = pl.BlockSpec(memory_space=pl.ANY)

SH_M, SC_M, GT_M, SH_F, SC_F, GT_F = range(6)

SMALL = (("b_ada", 6144), ("g_pre_mix", 1024), ("g_post_mix", 1024), ("b_in", 4608), ("b_dw_conv", 512),
         ("g_conv_ln", 512), ("b_conv_ln", 512), ("b_conv_o", 1024), ("g_pre_ffn", 1024), ("g_post_ffn", 1024),
         ("b_dw_ffn", 5632))
PACKED_TOTAL = sum(n for _, n in SMALL) + CONV_K * D_CONV + 3 * 2 * D_FF


def _cparams(n_axes):
    return pltpu.CompilerParams(vmem_limit_bytes=VMEM_LIMIT_BYTES,
                                dimension_semantics=("arbitrary",) * n_axes)


def _sig(v):
    return 1.0 / (1.0 + jnp.exp(-v))


def _pick(n, target):
    if n <= target:
        return n
    t = target - target % 128
    while n % t:
        t -= 128
    return t


def _tile(rows, cols, col=0):
    return pl.BlockSpec((rows, cols), lambda i: (i, col))


def _full(shape):
    zeros = (0,) * len(shape)
    return pl.BlockSpec(shape, lambda i: zeros)


def _prev(halo, cols, rows, col=0):
    return pl.BlockSpec((halo, cols), lambda i: (jnp.maximum(i * (rows // halo) - 1, 0), col))


def _next(halo, cols, rows, n_blocks, col=0):
    return pl.BlockSpec((halo, cols), lambda i: (jnp.minimum((i + 1) * (rows // halo), n_blocks - 1), col))


class _Comm:
    def __init__(self, inputs, out_shapes, sems, start, finish, relay=None, early=None):
        self.inputs, self.out_shapes, self.sems, self.start, self.finish = inputs, out_shapes, sems, start, finish
        self.relay, self.early = relay, early


def _host_call(body, name, grid, in_specs, out_specs, out_shape, scratch_shapes, args, comm=None):
    n_in, n_out, n_scr = len(args), len(out_shape), len(scratch_shapes)
    c_in = list(comm.inputs) if comm else []
    c_out = list(comm.out_shapes) if comm else []
    c_sem = list(comm.sems) if comm else []

    def full(*refs):
        bounds = [0, n_in, len(c_in), n_out, len(c_out), n_scr, len(c_sem)]
        cuts = [sum(bounds[:i + 1]) for i in range(len(bounds))]
        ins, cins, outs, couts, scr, csems = (refs[lo:hi] for lo, hi in zip(cuts[:-1], cuts[1:]))
        if comm:
            first = functools.reduce(jnp.logical_and, [pl.program_id(ax) == 0 for ax in range(len(grid))])
            pl.when(first)(lambda: comm.start(cins, couts, csems))
            if comm.early is not None:
                strides = [math.prod(grid[ax + 1:]) for ax in range(len(grid))]
                step = sum(pl.program_id(ax) * strides[ax] for ax in range(len(grid)))
                pl.when(step == 1)(lambda: comm.early(cins, couts, csems))
            last = functools.reduce(jnp.logical_and, [pl.program_id(ax) == grid[ax] - 1 for ax in range(len(grid))])
            if comm.relay is not None:
                pl.when(last)(lambda: comm.relay(cins, couts, csems))
        body(ins, outs, scr)
        if comm:
            pl.when(last)(lambda: comm.finish(cins, couts, csems))

    res = pl.pallas_call(
        full, name=name, grid=grid, in_specs=list(in_specs) + [ANY] * len(c_in),
        out_specs=list(out_specs) + [ANY] * len(c_out), out_shape=list(out_shape) + c_out,
        scratch_shapes=list(scratch_shapes) + c_sem, compiler_params=_cparams(len(grid)),
    )(*args, *c_in)
    return list(res[:n_out]), list(res[n_out:])


def _run_comm(comm, name):
    n_in, n_out = len(comm.inputs), len(comm.out_shapes)

    def body(*refs):
        ins, outs, sems = refs[:n_in], refs[n_in:n_in + n_out], refs[n_in + n_out:]
        comm.start(ins, outs, sems)
        if comm.relay is not None:
            comm.relay(ins, outs, sems)
        comm.finish(ins, outs, sems)

    return pl.pallas_call(
        body, name=name, out_shape=list(comm.out_shapes), in_specs=[ANY] * n_in, out_specs=[ANY] * n_out,
        scratch_shapes=list(comm.sems),
    )(*comm.inputs)


def _place():
    return lax.axis_index("x"), lax.axis_index("y"), lax.axis_index("c")


def _gather_comm(arrs):
    n = len(arrs)

    def plan(ins, outs, sems):
        send_sems, recv_sems, local_sems = sems
        x, y, c = _place()
        me, sibling = (x, y, c), (x, y, 1 - c)
        chips = [(1 - x, y), (x, 1 - y), (1 - x, 1 - y)]

        def block(k, p):
            return outs[k].at[4 * p[0] + 2 * p[1] + p[2]]

        def copy(k, s, blk, to, src=None):
            return pltpu.make_async_remote_copy(
                src_ref=block(k, blk) if src is None else src, dst_ref=block(k, blk),
                send_sem=send_sems.at[7 * k + s], recv_sem=recv_sems.at[7 * k + s],
                device_id=to, device_id_type=MESH)

        mine = [pltpu.make_async_copy(ins[k], block(k, me), local_sems.at[k]) for k in range(n)]
        first = []
        for k in range(n):
            first.append(copy(k, 0, me, sibling, src=ins[k]))
            for j, chip in enumerate(chips):
                first.append(copy(k, 1 + j, me, (*chip, c), src=ins[k]))
        return me, sibling, chips, c, copy, mine, first

    def start(ins, outs, sems):
        *_, mine, first = plan(ins, outs, sems)
        for cp in mine + first:
            cp.start()

    def relay(ins, outs, sems):
        me, sibling, chips, c, copy, _, _ = plan(ins, outs, sems)
        for j, chip in enumerate(chips):
            for k in range(n):
                copy(k, 1 + j, (*chip, c), me).wait_recv()
                copy(k, 4 + j, (*chip, c), sibling).start()

    def finish(ins, outs, sems):
        me, sibling, chips, c, copy, mine, first = plan(ins, outs, sems)
        passed = [copy(k, 4 + j, (*chip, c), sibling) for j, chip in enumerate(chips) for k in range(n)]
        for k in range(n):
            copy(k, 0, sibling, me).wait_recv()
        for j, chip in enumerate(chips):
            for k in range(n):
                copy(k, 4 + j, (*chip, 1 - c), me).wait_recv()
        for cp in first + passed:
            cp.wait_send()
        for cp in mine:
            cp.wait()

    return _Comm(list(arrs), [jax.ShapeDtypeStruct((N_DEV,) + a.shape, a.dtype) for a in arrs],
                 [pltpu.SemaphoreType.DMA((7 * n,)), pltpu.SemaphoreType.DMA((7 * n,)),
                  pltpu.SemaphoreType.DMA((n,))], start, finish, relay)


def _scatter_comm(blocks):
    n = len(blocks)

    def plan(ins, outs, sems, arrivals):
        send_sems, recv_sems, local_sems = sems
        x, y, c = _place()
        me = 4 * x + 2 * y + c
        local = [pltpu.make_async_copy(ins[k].at[me], outs[k].at[me], local_sems.at[k]) for k in range(n)]
        sends, recvs = [], []
        for k in range(n):
            for mask in range(1, N_DEV):
                px = 1 - x if mask & 4 else x
                py = 1 - y if mask & 2 else y
                pc = 1 - c if mask & 1 else c
                peer = 4 * px + 2 * py + pc
                sem = 7 * k + mask - 1
                both = dict(send_sem=send_sems.at[sem], recv_sem=recv_sems.at[sem], device_id=(px, py, pc),
                            device_id_type=MESH)
                sends.append(pltpu.make_async_remote_copy(src_ref=ins[k].at[peer], dst_ref=outs[k].at[me], **both))
                if arrivals:
                    recvs.append(pltpu.make_async_remote_copy(src_ref=ins[k].at[me], dst_ref=outs[k].at[peer],
                                                              **both))
        return local, sends, recvs

    def start(ins, outs, sems):
        local, sends, _ = plan(ins, outs, sems, arrivals=False)
        for cp in local + sends:
            cp.start()

    def finish(ins, outs, sems):
        local, sends, recvs = plan(ins, outs, sems, arrivals=True)
        for cp in recvs:
            cp.wait_recv()
        for cp in sends:
            cp.wait_send()
        for cp in local:
            cp.wait()

    return _Comm(list(blocks), [jax.ShapeDtypeStruct(b.shape, b.dtype) for b in blocks],
                 [pltpu.SemaphoreType.DMA((7 * n,)), pltpu.SemaphoreType.DMA((7 * n,)),
                  pltpu.SemaphoreType.DMA((n,))], start, finish)


def _pair_scatter_comm(block):
    _, r, c = block.shape
    quarter = jax.ShapeDtypeStruct((4, r, c), block.dtype)

    def plan(ins, outs, sems):
        parts, got, pair = outs
        d2d_send, d2d_recv, ici_send, ici_recv, local, buf_a, buf_b = sems
        x, y, cc = _place()
        mine = 2 * x + y
        chips = [(1 - x, y), (x, 1 - y), (1 - x, 1 - y)]
        to_sibling = [pltpu.make_async_remote_copy(
            src_ref=ins[0].at[2 * q + 1 - cc], dst_ref=got.at[q], send_sem=d2d_send.at[q], recv_sem=d2d_recv.at[q],
            device_id=(x, y, 1 - cc), device_id_type=MESH) for q in range(4)]
        to_chips = [pltpu.make_async_remote_copy(
            src_ref=pair.at[2 * px + py], dst_ref=parts.at[mine], send_sem=ici_send.at[j], recv_sem=ici_recv.at[j],
            device_id=(px, py, cc), device_id_type=MESH) for j, (px, py) in enumerate(chips)]
        from_chips = [pltpu.make_async_remote_copy(
            src_ref=pair.at[mine], dst_ref=parts.at[2 * px + py], send_sem=ici_send.at[j], recv_sem=ici_recv.at[j],
            device_id=(px, py, cc), device_id_type=MESH) for j, (px, py) in enumerate(chips)]
        own = pltpu.make_async_copy(pair.at[mine], parts.at[mine], local.at[2])
        return cc, got, pair, local, buf_a, buf_b, to_sibling, to_chips, from_chips, own

    def start(ins, outs, sems):
        for cp in plan(ins, outs, sems)[6]:
            cp.start()

    def early(ins, outs, sems):
        cc, got, pair, local, buf_a, buf_b, to_sibling, to_chips, _, own = plan(ins, outs, sems)
        for q in range(4):
            to_sibling[q].wait_recv()
            loads = [pltpu.make_async_copy(ins[0].at[2 * q + cc], buf_a, local.at[0]),
                     pltpu.make_async_copy(got.at[q], buf_b, local.at[1])]
            for cp in loads:
                cp.start()
            for cp in loads:
                cp.wait()
            buf_a[...] = (buf_a[...].astype(F32) + buf_b[...].astype(F32)).astype(block.dtype)
            store = pltpu.make_async_copy(buf_a, pair.at[q], local.at[0])
            store.start()
            store.wait()
        for cp in to_chips + [own]:
            cp.start()

    def finish(ins, outs, sems):
        *_, to_sibling, to_chips, from_chips, own = plan(ins, outs, sems)
        for cp in from_chips:
            cp.wait_recv()
        for cp in to_chips + to_sibling:
            cp.wait_send()
        own.wait()

    return _Comm([block], [quarter, quarter, quarter],
                 [pltpu.SemaphoreType.DMA((4,)), pltpu.SemaphoreType.DMA((4,)), pltpu.SemaphoreType.DMA((3,)),
                  pltpu.SemaphoreType.DMA((3,)), pltpu.SemaphoreType.DMA((3,)), pltpu.VMEM((r, c), block.dtype),
                  pltpu.VMEM((r, c), block.dtype)], start, finish, early=early)


_DIMS = {"nn": (((1,), (0,)), ((), ())), "nt": (((1,), (1,)), ((), ())), "tn": (((0,), (0,)), ((), ()))}


class _Epilogue:
    def __init__(self, args, in_specs, out_shapes, out_specs, fn, keep_product):
        self.args, self.in_specs, self.out_shapes, self.out_specs = args, in_specs, out_shapes, out_specs
        self.fn, self.keep_product = fn, keep_product


def _row_tile(rows, cols):
    return pl.BlockSpec((rows, cols), lambda i, j: (i, 0))


def _whole(shape):
    zeros = (0,) * len(shape)
    return pl.BlockSpec(shape, lambda i, j: zeros)


def _mm(a, b, mode, out_dtype, name, bias=None, tm=512, tn=512, comm=None, cols=None, epilogue=None):
    pieces = a if isinstance(a, (list, tuple)) else [a]
    assert all(p.dtype == BF16 for p in pieces) and b.dtype == BF16
    a = pieces[0]
    if mode == "tn":
        k_dim, m_dim = a.shape
    else:
        m_dim, k_dim = a.shape
    n_dim = b.shape[0] if mode == "nt" else b.shape[1]
    col0 = 0
    if cols is not None:
        assert mode != "tn" and cols[0] % tn == 0 and cols[1] % tn == 0
        col0, n_dim = cols[0] // tn, cols[1]
    tm, tn = _pick(m_dim, tm), _pick(n_dim, tn)
    assert mode != "tn" or len(pieces) == 1
    a_specs = [pl.BlockSpec((k_dim, tm), lambda i, j: (0, i)) if mode == "tn"
               else pl.BlockSpec((tm, k_dim), lambda i, j: (i, 0))] * len(pieces)
    once = dict(pipeline_mode=pl.Buffered(1)) if tn == n_dim else {}
    if mode == "nt":
        b_specs = [pl.BlockSpec((tn, k_dim), lambda i, j, p=p: (j + col0, p), **once) for p in range(len(pieces))]
    else:
        b_specs = [pl.BlockSpec((k_dim, tn), lambda i, j, p=p: (p, j + col0), **once) for p in range(len(pieces))]
    in_specs = a_specs + b_specs
    args = list(pieces) + [b] * len(pieces)
    if bias is not None:
        in_specs.append(pl.BlockSpec((1, tn), lambda i, j: (0, j + col0)))
        args.append(bias)
    dims = _DIMS[mode]
    n_pieces = len(pieces)
    n_own = len(args)
    keep = epilogue is None or epilogue.keep_product
    out_specs = [pl.BlockSpec((tm, tn), lambda i, j: (i, j))] if keep else []
    out_shape = [jax.ShapeDtypeStruct((m_dim, n_dim), out_dtype)] if keep else []
    if epilogue is not None:
        assert tn == n_dim
        in_specs, args = in_specs + list(epilogue.in_specs), args + list(epilogue.args)
        out_specs, out_shape = out_specs + list(epilogue.out_specs), out_shape + list(epilogue.out_shapes)

    def body(ins, outs, scratch):
        total = lax.dot_general(ins[0][...], ins[n_pieces][...], dims, preferred_element_type=F32)
        for p in range(1, n_pieces):
            total = total + lax.dot_general(ins[p][...], ins[n_pieces + p][...], dims, preferred_element_type=F32)
        if bias is not None:
            total = total + ins[2 * n_pieces][...]
        if keep:
            outs[0][...] = total.astype(out_dtype)
        if epilogue is not None:
            epilogue.fn(total, pl.program_id(0) == 0, ins[n_own:], outs[1:] if keep else outs)

    outs, extra = _host_call(body, name, grid=(m_dim // tm, n_dim // tn), in_specs=in_specs, out_specs=out_specs,
                             out_shape=out_shape, scratch_shapes=[], args=args, comm=comm)
    product = outs[0] if keep else None
    if comm is None and epilogue is None:
        return product
    return product, outs[1:] if keep else outs, extra


def _mm_tn_rows(pieces, b, name, tm=256):
    k_dim, n_dim = b.shape
    counts = [p.shape[1] // tm for p in pieces]
    assert all(p.shape[1] % tm == 0 for p in pieces)
    firsts = [sum(counts[:q]) for q in range(len(pieces))]

    def a_spec(first, count):
        return pl.BlockSpec((k_dim, tm), lambda i: (0, jnp.clip(i - first, 0, count - 1)))

    def body(ins, outs, scratch):
        i = pl.program_id(0)
        for a_ref, first, count in zip(ins[:-1], firsts, counts):
            @pl.when(jnp.logical_and(i >= first, i < first + count))
            def _(a_ref=a_ref):
                outs[0][...] = lax.dot_general(a_ref[...], ins[-1][...], _DIMS["tn"],
                                               preferred_element_type=F32).astype(BF16)

    (out,), _ = _host_call(
        body, name, grid=(sum(counts),),
        in_specs=[a_spec(f, c) for f, c in zip(firsts, counts)] + [_full((k_dim, n_dim))],
        out_specs=[_tile(tm, n_dim)], out_shape=[jax.ShapeDtypeStruct((sum(counts) * tm, n_dim), BF16)],
        scratch_shapes=[], args=list(pieces) + [b])
    return out


def _adam_math(w, g, m, v):
    m = ADAM_B1 * m + (1.0 - ADAM_B1) * g
    v = ADAM_B2 * v + (1.0 - ADAM_B2) * (g * g)
    m_hat = m / (1.0 - ADAM_B1 ** ADAM_STEP)
    v_hat = v / (1.0 - ADAM_B2 ** ADAM_STEP)
    delta = -ADAM_LR * (m_hat / (jnp.sqrt(v_hat) + ADAM_EPS) + ADAM_WD * w)
    return delta, m, v


def _adamw(w, m, v, name, g=None, parts=None):
    rows, cols = w.shape[0], w.shape[-1]
    tr = rows
    if rows * cols * 4 > ADAMW_BLOCK_BYTES:
        tr = max(t for t in range(16, rows, 16) if rows % t == 0 and t * cols * 4 <= ADAMW_BLOCK_BYTES)

    def body(w_ref, m_ref, v_ref, g_ref, go_ref, d_ref, mo_ref, vo_ref):
        if parts is None:
            grad = g_ref[...]
        else:
            grad = g_ref[0].astype(F32)
            for d in range(1, parts.shape[0]):
                grad = grad + g_ref[d].astype(F32)
        delta, m_new, v_new = _adam_math(w_ref[...], grad, m_ref[...], v_ref[...])
        go_ref[...] = grad
        d_ref[...] = delta
        mo_ref[...] = m_new
        vo_ref[...] = v_new

    spec = _tile(tr, cols) if w.ndim == 2 else _full(w.shape)
    g_spec = spec if parts is None else pl.BlockSpec((parts.shape[0], tr, cols), lambda i: (0, i, 0))
    shape = jax.ShapeDtypeStruct(w.shape, F32)
    return pl.pallas_call(
        body, name=name, out_shape=[shape] * 4, grid=(rows // tr,),
        in_specs=[spec, spec, spec, g_spec], out_specs=[spec] * 4, compiler_params=_cparams(1),
    )(w, m, v, g if parts is None else parts)


def _pack_grads(small_x, small_m, small_f, small_g, small_a, small_c, dbv, dbg, dwv, dwg, dw_conv):
    pieces = [
        (small_x, 2, D_MODEL), (small_x, 1, D_MODEL), (small_m, 4, D_MODEL), (small_m, 2, D_MODEL),
        (small_m, 1, D_MODEL), (small_f, 1, D_MODEL),
        (small_x, 0, D_MODEL), (small_m, 3, D_MODEL),
        (small_a, 0, D_ATTN), (small_a, 1, D_ATTN), (small_a, 2, D_ATTN), (small_c, 3, D_CONV),
        (small_c, 4, D_CONV), (small_g, 0, D_MODEL), (small_g, 1, D_MODEL),
        (small_c, 0, D_CONV), (small_c, 1, D_CONV), (small_c, 2, D_CONV),
        (small_g, 2, D_MODEL), (small_m, 0, D_MODEL), (small_f, 0, D_MODEL),
        (dbv, 0, D_FF), (dbg, 0, D_FF),
    ]
    pieces += [(dw_conv, j, D_CONV) for j in range(CONV_K)]
    pieces += [(src, tap, D_FF) for tap in range(3) for src in (dwv, dwg)]
    sources = [small_x, small_m, small_f, small_g, small_a, small_c, dbv, dbg, dwv, dwg, dw_conv]
    assert sum(width for _, _, width in pieces) == PACKED_TOTAL

    def body(*refs):
        o_ref = refs[-1]
        ref_of = {id(src): ref for src, ref in zip(sources, refs)}
        off = 0
        for src, row, width in pieces:
            o_ref[:, off:off + width] = ref_of[id(src)][row:row + 1, :]
            off += width

    return pl.pallas_call(body, name="pack_grads", out_shape=jax.ShapeDtypeStruct((1, PACKED_TOTAL), F32))(*sources)


def _small_adamw(gathered, gathered_rel, gathered_loss, weights, mom_m, mom_v):
    vec_names = [name for name, _ in SMALL]
    states = []
    for name in vec_names + ["rel_bias"]:
        states += [weights[name], mom_m[name], mom_v[name]]
    states = [a.reshape(a.shape[1:]) if a.ndim == 3 else a for a in states]
    n_state = len(states)

    def body(*refs):
        g_ref, rel_ref, loss_ref = refs[0], refs[1], refs[2]
        state_refs, out_refs = refs[3:3 + n_state], refs[3 + n_state:]
        total = g_ref[0:1, :]
        rel = rel_ref[0]
        loss = loss_ref[0]
        for d in range(1, N_DEV):
            total = total + g_ref[d:d + 1, :]
            rel = rel + rel_ref[d]
            loss = loss + loss_ref[d]
        off = 0
        for n, (name, width) in enumerate(SMALL):
            grad = total[:, off:off + width]
            w_ref, m_ref, v_ref = state_refs[3 * n:3 * n + 3]
            for ref, val in zip(out_refs[4 * n:4 * n + 4], (grad,) + _adam_math(w_ref[...], grad, m_ref[...], v_ref[...])):
                ref[...] = val
            off += width
        n = len(SMALL)
        w_ref, m_ref, v_ref = state_refs[3 * n:3 * n + 3]
        for ref, val in zip(out_refs[4 * n:4 * n + 4], (rel,) + _adam_math(w_ref[...], rel, m_ref[...], v_ref[...])):
            ref[...] = val
        dwc_ref, dwf_ref, loss_out = out_refs[4 * n + 4:]
        loss_out[...] = 0.5 * loss
        dwc_ref[...] = jnp.zeros_like(dwc_ref)
        dwf_ref[...] = jnp.zeros_like(dwf_ref)
        for j in range(CONV_K):
            dwc_ref[j:j + 1, :] = total[:, off:off + D_CONV]
            off += D_CONV
        for tap in range(3):
            dwf_ref[tap:tap + 1, :] = total[:, off:off + 2 * D_FF]
            off += 2 * D_FF

    out_shape = []
    for k in range(n_state // 3):
        out_shape += [jax.ShapeDtypeStruct(states[3 * k].shape, F32)] * 4
    out_shape += [jax.ShapeDtypeStruct((CONV_HALO, D_CONV), F32), jax.ShapeDtypeStruct((8, 2 * D_FF), F32),
                  jax.ShapeDtypeStruct((1, 128), F32)]
    res = pl.pallas_call(
        body, name="small_adamw", out_shape=out_shape,
        compiler_params=pltpu.CompilerParams(vmem_limit_bytes=VMEM_LIMIT_BYTES),
    )(gathered, gathered_rel, gathered_loss, *states)
    updates = {name: tuple(res[4 * n:4 * n + 4]) for n, name in enumerate(vec_names + ["rel_bias"])}
    return updates, res[-3], res[-2], res[-1]


def _ada_mod(c, w_shard):
    cols = w_shard.shape[1]

    def body(c_ref, w_ref, call_ref, mod_ref, send_sems, recv_sems):
        x, y, cc = _place()
        me = 4 * x + 2 * y + cc

        def exchange(ref, phase):
            sends, arrivals = [], []
            for mask in range(1, N_DEV):
                px = 1 - x if mask & 4 else x
                py = 1 - y if mask & 2 else y
                pc = 1 - cc if mask & 1 else cc
                both = dict(send_sem=send_sems.at[7 * phase + mask - 1], recv_sem=recv_sems.at[7 * phase + mask - 1],
                            device_id=(px, py, pc), device_id_type=MESH)
                sends.append(pltpu.make_async_remote_copy(src_ref=ref.at[me], dst_ref=ref.at[me], **both))
                arrivals.append(pltpu.make_async_remote_copy(src_ref=ref.at[me], dst_ref=ref.at[4 * px + 2 * py + pc],
                                                             **both))
            for cp in sends:
                cp.start()
            for cp in arrivals:
                cp.wait_recv()
            for cp in sends:
                cp.wait_send()

        v = c_ref[...]
        call_ref[me] = v * _sig(v)
        exchange(call_ref, 0)
        c_all = jnp.concatenate([call_ref[d] for d in range(N_DEV)], axis=0)
        mod_ref[me] = jnp.dot(c_all, w_ref[...], precision=HIGHEST, preferred_element_type=F32)
        exchange(mod_ref, 1)

    return pl.pallas_call(
        body, name="ada_mod",
        out_shape=[jax.ShapeDtypeStruct((N_DEV, 1, D_MODEL), F32), jax.ShapeDtypeStruct((N_DEV, N_DEV, cols), F32)],
        scratch_shapes=[pltpu.SemaphoreType.DMA((14,)), pltpu.SemaphoreType.DMA((14,))],
        compiler_params=pltpu.CompilerParams(vmem_limit_bytes=VMEM_LIMIT_BYTES),
    )(c, w_shard)


def _ada_grad(c_all, dmod_shard):
    def body(c_ref, d_ref, o_ref):
        o_ref[...] = lax.dot_general(c_ref[...], d_ref[...], _DIMS["tn"], precision=HIGHEST,
                                     preferred_element_type=F32)

    return pl.pallas_call(
        body, name="ada_grad", out_shape=jax.ShapeDtypeStruct((D_MODEL, dmod_shard.shape[1]), F32),
        compiler_params=pltpu.CompilerParams(vmem_limit_bytes=VMEM_LIMIT_BYTES),
    )(c_all, dmod_shard)


ROWS = 256


def _rms(v):
    r = lax.rsqrt(jnp.mean(v * v, axis=-1, keepdims=True) + EPS)
    return v * r, r


def _rms_bwd(dxn, xn, r):
    return r * (dxn - xn * jnp.mean(dxn * xn, axis=-1, keepdims=True))


def _colsum(v):
    return jnp.sum(v, axis=0, keepdims=True)


def _pre_mix(x, mod6, g1, comm=None):
    seq = x.shape[0]

    def body(ins, outs, scratch):
        x_ref, mod_ref, g_ref = ins
        xn, _ = _rms(x_ref[...])
        y = xn * g_ref[...]
        outs[0][...] = (y * (1.0 + mod_ref[SC_M:SC_M + 1, :]) + mod_ref[SH_M:SH_M + 1, :]).astype(BF16)

    (h,), extra = _host_call(
        body, "pre_mix", grid=(seq // ROWS,),
        in_specs=[_tile(ROWS, D_MODEL), _full((6, D_MODEL)), _full((1, D_MODEL))], out_specs=[_tile(ROWS, D_MODEL)],
        out_shape=[jax.ShapeDtypeStruct((seq, D_MODEL), BF16)], scratch_shapes=[], args=[x, mod6, g1], comm=comm)
    return h, extra


def _post_mix_pre_ffn(x, mod6, g2, g3, rows):
    seq = x.shape[0]

    def fn(y, first, ins, outs):
        x_ref, mod_ref, g2_ref, g3_ref = ins
        x1_ref, h_ref = outs
        yn, _ = _rms(y)
        x1 = x_ref[...] + mod_ref[GT_M:GT_M + 1, :] * (yn * g2_ref[...])
        x1_ref[...] = x1
        xn, _ = _rms(x1)
        y3 = xn * g3_ref[...]
        h_ref[...] = (y3 * (1.0 + mod_ref[SC_F:SC_F + 1, :]) + mod_ref[SH_F:SH_F + 1, :]).astype(BF16)

    return _Epilogue(
        [x, mod6, g2, g3], [_row_tile(rows, D_MODEL), _whole((6, D_MODEL)), _whole((1, D_MODEL)), _whole((1, D_MODEL))],
        [jax.ShapeDtypeStruct((seq, D_MODEL), F32), jax.ShapeDtypeStruct((seq, D_MODEL), BF16)],
        [_row_tile(rows, D_MODEL), _row_tile(rows, D_MODEL)], fn, keep_product=True)


def _final(x1, target, mod6, g4, rows):
    seq = x1.shape[0]

    def fn(y, first, ins, outs):
        x1_ref, t_ref, mod_ref, g_ref = ins
        loss_ref, dout_ref, dyf_ref, small_ref = outs

        @pl.when(first)
        def _():
            loss_ref[...] = jnp.zeros_like(loss_ref)
            small_ref[...] = jnp.zeros_like(small_ref)

        gt = mod_ref[GT_F:GT_F + 1, :]
        g4v = g_ref[...]
        yn, r = _rms(y)
        out = x1_ref[...] + gt * (yn * g4v)
        err = out - t_ref[...]
        loss_ref[...] += jnp.sum(jnp.mean(err * err, axis=-1, keepdims=True))
        dout = err * (1.0 / D_MODEL)
        dout_ref[...] = dout
        small_ref[0:1, :] += _colsum(dout * gt * yn)
        small_ref[1:2, :] += _colsum(dout * (yn * g4v))
        dyf_ref[...] = _rms_bwd(dout * gt * g4v, yn, r).astype(BF16)

    return _Epilogue(
        [x1, target, mod6, g4],
        [_row_tile(rows, D_MODEL), _row_tile(rows, D_MODEL), _whole((6, D_MODEL)), _whole((1, D_MODEL))],
        [jax.ShapeDtypeStruct((1, 128), F32), jax.ShapeDtypeStruct((seq, D_MODEL), F32),
         jax.ShapeDtypeStruct((seq, D_MODEL), BF16), jax.ShapeDtypeStruct((8, D_MODEL), F32)],
        [_whole((1, 128)), _row_tile(rows, D_MODEL), _row_tile(rows, D_MODEL), _whole((8, D_MODEL))],
        fn, keep_product=False)


def _mid_bwd(x1, dout, ymix, mod6, g3, g2, rows):
    seq = x1.shape[0]

    def fn(dh, first, ins, outs):
        x1_ref, dout_ref, y_ref, mod_ref, g3_ref, g2_ref = ins
        dx1_ref, dy_ref, small_ref = outs

        @pl.when(first)
        def _():
            small_ref[...] = jnp.zeros_like(small_ref)

        g3v, g2v = g3_ref[...], g2_ref[...]
        xn, r3 = _rms(x1_ref[...])
        y3 = xn * g3v
        dy3 = dh * (1.0 + mod_ref[SC_F:SC_F + 1, :])
        small_ref[0:1, :] += _colsum(dy3 * xn)
        small_ref[1:2, :] += _colsum(dh * y3)
        small_ref[2:3, :] += _colsum(dh)
        dx1 = dout_ref[...] + _rms_bwd(dy3 * g3v, xn, r3)
        dx1_ref[...] = dx1
        gt = mod_ref[GT_M:GT_M + 1, :]
        yn, r2 = _rms(y_ref[...])
        small_ref[3:4, :] += _colsum(dx1 * gt * yn)
        small_ref[4:5, :] += _colsum(dx1 * (yn * g2v))
        dy_ref[...] = _rms_bwd(dx1 * gt * g2v, yn, r2).astype(BF16)

    return _Epilogue(
        [x1, dout, ymix, mod6, g3, g2],
        [_row_tile(rows, D_MODEL)] * 3 + [_whole((6, D_MODEL)), _whole((1, D_MODEL)), _whole((1, D_MODEL))],
        [jax.ShapeDtypeStruct((seq, D_MODEL), F32), jax.ShapeDtypeStruct((seq, D_MODEL), BF16),
         jax.ShapeDtypeStruct((8, D_MODEL), F32)],
        [_row_tile(rows, D_MODEL), _row_tile(rows, D_MODEL), _whole((8, D_MODEL))], fn, keep_product=False)


def _pre_mix_bwd(x, dx1, mod6, g1, rows):
    seq = x.shape[0]

    def fn(dh, first, ins, outs):
        x_ref, dx1_ref, mod_ref, g_ref = ins
        dx_ref, small_ref = outs

        @pl.when(first)
        def _():
            small_ref[...] = jnp.zeros_like(small_ref)

        g1v = g_ref[...]
        xn, r = _rms(x_ref[...])
        dy = dh * (1.0 + mod_ref[SC_M:SC_M + 1, :])
        small_ref[0:1, :] += _colsum(dy * xn)
        small_ref[1:2, :] += _colsum(dh * (xn * g1v))
        small_ref[2:3, :] += _colsum(dh)
        dx_ref[...] = dx1_ref[...] + _rms_bwd(dy * g1v, xn, r)

    return _Epilogue(
        [x, dx1, mod6, g1],
        [_row_tile(rows, D_MODEL), _row_tile(rows, D_MODEL), _whole((6, D_MODEL)), _whole((1, D_MODEL))],
        [jax.ShapeDtypeStruct((seq, D_MODEL), F32), jax.ShapeDtypeStruct((8, D_MODEL), F32)],
        [_row_tile(rows, D_MODEL), _whole((8, D_MODEL))], fn, keep_product=False)


def _toeplitz_onehot(shape, offset_axis, top):
    m = lax.broadcasted_iota(jnp.int32, shape, offset_axis)
    i = lax.broadcasted_iota(jnp.int32, shape, 1 - offset_axis)
    return (i == jnp.clip(top - m, -MAX_REL, MAX_REL) + MAX_REL).astype(F32)


def _bias_table(rel_bias):
    width = GROUP_Q + GROUP_K

    def body(rb_ref, o_ref, t_ref):
        t_ref[...] = jnp.dot(rb_ref[...], _toeplitz_onehot((N_REL, width), 1, GROUP_K - 1), precision=HIGHEST,
                             preferred_element_type=F32)
        lane = lax.broadcasted_iota(jnp.int32, (N_HEADS, GROUP_K), 1)
        for r in range(GROUP_Q):
            first_key = (r // CHUNK) * CHUNK
            band = jnp.logical_and(lane >= first_key, lane < first_key + BAND)
            o_ref[r] = jnp.where(band, t_ref[:, GROUP_Q - 1 - r:GROUP_Q - 1 - r + GROUP_K], NEG_INF)

    return pl.pallas_call(
        body, name="bias_table", out_shape=jax.ShapeDtypeStruct((GROUP_Q, N_HEADS, GROUP_K), F32),
        scratch_shapes=[pltpu.VMEM((N_HEADS, width), F32)],
    )(rel_bias)


def _bias_grad(dbias_q):
    def body(d_ref, o_ref, t_ref):
        t_ref[...] = jnp.zeros_like(t_ref)
        for qi in range(CHUNK):
            t_ref[:, CHUNK - 1 - qi:CHUNK - 1 - qi + BAND] += d_ref[qi]
        o_ref[...] = jnp.dot(t_ref[...], _toeplitz_onehot((TOEPLITZ, N_REL), 0, BAND - 1), precision=HIGHEST,
                             preferred_element_type=F32)

    return pl.pallas_call(
        body, name="bias_grad", out_shape=jax.ShapeDtypeStruct((N_HEADS, N_REL), F32),
        scratch_shapes=[pltpu.VMEM((N_HEADS, TOEPLITZ), F32)],
    )(dbias_q)


def _resident_copies(qkv_hbm, t_hbm, k_ref, v_ref, t_ref, sems):
    inside = pl.ds(PAD_ROWS, qkv_hbm.shape[0])
    return (pltpu.make_async_copy(qkv_hbm.at[:, pl.ds(D_ATTN, D_ATTN)], k_ref.at[inside, :], sems.at[0]),
            pltpu.make_async_copy(qkv_hbm.at[:, pl.ds(2 * D_ATTN, D_ATTN)], v_ref.at[inside, :], sems.at[1]),
            pltpu.make_async_copy(t_hbm, t_ref, sems.at[2]))


def _start_resident(copies, k_ref, v_ref):
    k_ref[0:PAD_ROWS, :] = jnp.zeros((PAD_ROWS, D_ATTN), BF16)
    v_ref[0:PAD_ROWS, :] = jnp.zeros((PAD_ROWS, D_ATTN), BF16)
    for cp in copies:
        cp.start()


def _softmax_rows(s_ref, t_ref, h, before_start, rows):
    s = s_ref[rows, :] * (HEAD_DIM ** -0.5) + t_ref[h, rows, :] + before_start
    e = jnp.exp(s - jnp.max(s, axis=-1, keepdims=True))
    return e / jnp.sum(e, axis=-1, keepdims=True)


def _before_start(g):
    kj = lax.broadcasted_iota(jnp.int32, (8, GROUP_K), 1)
    return jnp.where(kj >= PAD_ROWS - g * GROUP_Q, 0.0, NEG_INF)


def _attn_fwd(qkv, table, comm=None):
    seq = qkv.shape[0]

    def body(ins, outs, scratch):
        q_ref, qkv_hbm, t_hbm = ins
        (o_ref,) = outs
        k_ref, v_ref, t_ref, s_ref, p_ref, sems = scratch
        g = pl.program_id(0)
        load_k, load_v, load_t = _resident_copies(qkv_hbm, t_hbm, k_ref, v_ref, t_ref, sems)

        @pl.when(g == 0)
        def _():
            _start_resident((load_k, load_v, load_t), k_ref, v_ref)
            load_k.wait()

        window = pl.ds(pl.multiple_of(g * GROUP_Q, GROUP_Q), GROUP_K)
        before_start = _before_start(g)
        for h in range(N_HEADS):
            cols = slice(h * HEAD_DIM, (h + 1) * HEAD_DIM)
            buf = h % 2
            s_ref[buf] = lax.dot_general(q_ref[:, cols], k_ref[window, cols], _DIMS["nt"],
                                         preferred_element_type=F32)
            if h == 0:
                pl.when(g == 0)(load_t.wait)
            for row in range(0, GROUP_Q, SOFTMAX_ROWS):
                halves = [_softmax_rows(s_ref.at[buf], t_ref, h, before_start, slice(r, r + 8))
                          for r in (row, row + 8)]
                p_ref[buf, row:row + SOFTMAX_ROWS, :] = jnp.concatenate(halves, axis=0).astype(BF16)
            if h == 0:
                pl.when(g == 0)(load_v.wait)
            o_ref[:, cols] = jnp.dot(p_ref[buf], v_ref[window, cols], preferred_element_type=F32).astype(BF16)

    (ao,), extra = _host_call(
        body, "attn_fwd", grid=(seq // GROUP_Q,),
        in_specs=[_tile(GROUP_Q, D_ATTN), ANY, ANY], out_specs=[_tile(GROUP_Q, D_ATTN)],
        out_shape=[jax.ShapeDtypeStruct((seq, D_ATTN), BF16)],
        scratch_shapes=[pltpu.VMEM((seq + PAD_ROWS, D_ATTN), BF16), pltpu.VMEM((seq + PAD_ROWS, D_ATTN), BF16),
                        pltpu.VMEM(table.shape, F32),
                        pltpu.VMEM((2, GROUP_Q, GROUP_K), F32), pltpu.VMEM((2, GROUP_Q, GROUP_K), BF16),
                        pltpu.SemaphoreType.DMA((3,))],
        args=[qkv, qkv, table], comm=comm)
    return ao, extra


def _attn_bwd(qkv, table, dao, comm=None):
    seq = qkv.shape[0]
    n_groups = seq // GROUP_Q
    fold_w = GROUP_K + (GROUP - 1) * CHUNK

    def body(ins, outs, scratch):
        q_ref, do_ref, qkv_hbm, t_hbm = ins
        dq_ref, dkt_hbm, dvt_hbm, db_ref, cs_ref = outs
        k_ref, v_ref, t_ref, db_acc, dkt_acc, dvt_acc, s_ref, dp_ref, p_ref, ds_ref, sems = scratch
        g = pl.program_id(0)

        load_k, load_v, load_t = _resident_copies(qkv_hbm, t_hbm, k_ref, v_ref, t_ref, sems)

        @pl.when(g == 0)
        def _():
            _start_resident((load_k, load_v, load_t), k_ref, v_ref)
            db_acc[...] = jnp.zeros_like(db_acc)
            dkt_acc[...] = jnp.zeros_like(dkt_acc)
            dvt_acc[...] = jnp.zeros_like(dvt_acc)
            cs_ref[...] = jnp.zeros_like(cs_ref)
            load_k.wait()
            load_v.wait()

        window = pl.ds(pl.multiple_of(g * GROUP_Q, GROUP_Q), GROUP_K)
        before_start = _before_start(g)
        for h in range(N_HEADS):
            cols = slice(h * HEAD_DIM, (h + 1) * HEAD_DIM)
            buf = h % 2
            qh, doh = q_ref[:, cols], do_ref[:, cols]
            kh, vh = k_ref[window, cols], v_ref[window, cols]
            s_ref[buf] = lax.dot_general(qh, kh, _DIMS["nt"], preferred_element_type=F32)
            dp_ref[buf] = lax.dot_general(doh, vh, _DIMS["nt"], preferred_element_type=F32)
            if h == 0:
                pl.when(g == 0)(load_t.wait)
            for row in range(0, GROUP_Q, SOFTMAX_ROWS):
                p_halves, ds_halves = [], []
                for r in (row, row + 8):
                    p = _softmax_rows(s_ref.at[buf], t_ref, h, before_start, slice(r, r + 8))
                    dp = dp_ref[buf, r:r + 8, :]
                    ds = p * (dp - jnp.sum(dp * p, axis=-1, keepdims=True))
                    chunk = r // CHUNK
                    shift = (GROUP - 1 - chunk) * CHUNK
                    db_acc[h, r - chunk * CHUNK:r - chunk * CHUNK + 8, shift:shift + GROUP_K] += ds
                    p_halves.append(p)
                    ds_halves.append(ds * (HEAD_DIM ** -0.5))
                p_ref[buf, row:row + SOFTMAX_ROWS, :] = jnp.concatenate(p_halves, axis=0).astype(BF16)
                ds_ref[buf, row:row + SOFTMAX_ROWS, :] = jnp.concatenate(ds_halves, axis=0).astype(BF16)
            dq_ref[:, cols] = jnp.dot(ds_ref[buf], kh, preferred_element_type=F32).astype(BF16)
            dkt_acc[cols, window] += lax.dot_general(qh, ds_ref[buf], _DIMS["tn"], preferred_element_type=F32)
            dvt_acc[cols, window] += lax.dot_general(doh, p_ref[buf], _DIMS["tn"], preferred_element_type=F32)
        cs_ref[0:1, :] += _colsum(dq_ref[...].astype(F32))

        @pl.when(g == n_groups - 1)
        def _():
            lo = (GROUP - 1) * CHUNK
            for h in range(N_HEADS):
                db_ref[h] = db_acc[h, :, lo:lo + BAND]
            inside = pl.ds(PAD_ROWS, seq)
            on_diagonal = (lax.broadcasted_iota(jnp.int32, (D_ATTN, D_ATTN), 0)
                           == lax.broadcasted_iota(jnp.int32, (D_ATTN, D_ATTN), 1))
            for row, acc in ((1, dkt_acc), (2, dvt_acc)):
                column = jnp.sum(acc[:, inside], axis=1, keepdims=True)
                cs_ref[row:row + 1, :] = _colsum(jnp.where(on_diagonal, column, 0.0))
            out_k = pltpu.make_async_copy(dkt_acc.at[:, inside], dkt_hbm, sems.at[0])
            out_v = pltpu.make_async_copy(dvt_acc.at[:, inside], dvt_hbm, sems.at[1])
            out_k.start()
            out_v.start()
            out_k.wait()
            out_v.wait()

    t_shape = (D_ATTN, seq + PAD_ROWS)
    outs, extra = _host_call(
        body, "attn_bwd", grid=(n_groups,),
        in_specs=[_tile(GROUP_Q, D_ATTN), _tile(GROUP_Q, D_ATTN), ANY, ANY],
        out_specs=[_tile(GROUP_Q, D_ATTN), ANY, ANY, _full((N_HEADS, CHUNK, BAND)), _full((8, D_ATTN))],
        out_shape=[jax.ShapeDtypeStruct((seq, D_ATTN), BF16), jax.ShapeDtypeStruct((D_ATTN, seq), F32),
                   jax.ShapeDtypeStruct((D_ATTN, seq), F32), jax.ShapeDtypeStruct((N_HEADS, CHUNK, BAND), F32),
                   jax.ShapeDtypeStruct((8, D_ATTN), F32)],
        scratch_shapes=[pltpu.VMEM((seq + PAD_ROWS, D_ATTN), BF16), pltpu.VMEM((seq + PAD_ROWS, D_ATTN), BF16),
                        pltpu.VMEM(table.shape, F32), pltpu.VMEM((N_HEADS, CHUNK, fold_w), F32), pltpu.VMEM(t_shape, F32),
                        pltpu.VMEM(t_shape, F32), pltpu.VMEM((2, GROUP_Q, GROUP_K), F32),
                        pltpu.VMEM((2, GROUP_Q, GROUP_K), F32), pltpu.VMEM((2, GROUP_Q, GROUP_K), BF16),
                        pltpu.VMEM((2, GROUP_Q, GROUP_K), BF16), pltpu.SemaphoreType.DMA((3,))],
        args=[qkv, dao, qkv, table], comm=comm)
    return outs, extra


def _assemble_dz(dq, dkt, dvt, dglu_a, dglu_b, dga, dgb):
    seq = dq.shape[0]
    rows = 512
    transposed = pl.BlockSpec((D_ATTN, rows), lambda i: (0, i))

    def body(dq_ref, dkt_ref, dvt_ref, da_ref, db_ref, dga_ref, dgb_ref, o_ref):
        o_ref[:, 0:D_ATTN] = dq_ref[...]
        o_ref[:, D_ATTN:2 * D_ATTN] = dkt_ref[...].T.astype(BF16)
        o_ref[:, 2 * D_ATTN:3 * D_ATTN] = dvt_ref[...].T.astype(BF16)
        off = 3 * D_ATTN
        for ref in (da_ref, db_ref, dga_ref, dgb_ref):
            width = ref.shape[1]
            o_ref[:, off:off + width] = ref[...]
            off += width

    width = 3 * D_ATTN + 2 * D_CONV + 2 * D_MODEL
    return pl.pallas_call(
        body, name="assemble_dz", out_shape=jax.ShapeDtypeStruct((seq, width), BF16), grid=(seq // rows,),
        in_specs=[_tile(rows, D_ATTN), transposed, transposed, _tile(rows, D_CONV), _tile(rows, D_CONV),
                  _tile(rows, D_MODEL), _tile(rows, D_MODEL)],
        out_specs=_tile(rows, width), compiler_params=_cparams(1),
    )(dq, dkt, dvt, dglu_a, dglu_b, dga, dgb)


CONV_ROWS = 256


def _ln_silu(u1, g, b):
    mu = jnp.mean(u1, axis=-1, keepdims=True)
    xc = u1 - mu
    rs = lax.rsqrt(jnp.mean(xc * xc, axis=-1, keepdims=True) + EPS)
    xhat = xc * rs
    u2 = xhat * g + b
    return xhat, rs, u2


def _glu_into(s_ref, a_ref, b_ref, ah_ref, bh_ref, first):
    halo = ah_ref[...] * _sig(bh_ref[...])
    s_ref[0:CONV_HALO, :] = jnp.where(first, 0.0, halo)
    s_ref[CONV_HALO:CONV_HALO + CONV_ROWS, :] = a_ref[...] * _sig(b_ref[...])


CONV_LANES = 128
CONV_TILES = CONV_ROWS // 8


def _lag_weights(w_ref, lanes):
    return {e: jnp.broadcast_to(w_ref[CONV_K - 1 - e:CONV_K - e, lanes], (8, CONV_LANES)) for e in range(CONV_K)}


def _class_sums(w, tiles, k):
    total = None
    for a, tile in enumerate(tiles):
        if 8 * a + k < CONV_K:
            term = w[8 * a + k] * tile
            total = term if total is None else total + term
    return total


def _conv_back(src_ref, first_tile, w, lanes, row_id, emit):
    before = None
    for m in range(-1, CONV_TILES):
        tiles = [src_ref[8 * (first_tile + m - a):8 * (first_tile + m - a) + 8, lanes] for a in range(4)]
        rolled = [None] + [pltpu.roll(_class_sums(w, tiles, k), k, 0) for k in range(1, 8)]
        if m >= 0:
            out = _class_sums(w, tiles, 0)
            for k in range(1, 8):
                out = out + jnp.where(row_id < k, before[k], rolled[k])
            emit(m, out)
        before = rolled


def _conv_ahead(src_ref, w, lanes, row_id, emit):
    before = None
    for m in range(CONV_TILES + 1):
        tiles = [src_ref[8 * (m + a):8 * (m + a) + 8, lanes] for a in range(4)]
        rolled = [None] + [pltpu.roll(_class_sums(w, tiles, k), 8 - k, 0) for k in range(1, 8)]
        if m >= 1:
            out = before[0]
            for k in range(1, 8):
                out = out + jnp.where(row_id < 8 - k, before[k], rolled[k])
            emit(m - 1, out)
        before = [_class_sums(w, tiles, 0) if m < CONV_TILES else None] + rolled[1:]


def _conv_weight_sums(d_ref, s_ref, lanes, row_id, whole_shifts):
    zero = jnp.zeros((8, CONV_LANES), F32)
    sums = {8 * a + k: zero for a in whole_shifts for k in range(8) if 8 * a + k < CONV_K}

    def d_tile(m):
        return d_ref[8 * m:8 * m + 8, lanes] if 0 <= m < CONV_TILES else zero

    rolled = [None] + [zero] * 7
    for m in range(-1, CONV_TILES):
        cur, nxt = d_tile(m), d_tile(m + 1)
        rolled_next = [None] + [pltpu.roll(nxt, 8 - k, 0) for k in range(1, 8)]
        shifted = [cur] + [jnp.where(row_id < 8 - k, rolled[k], rolled_next[k]) for k in range(1, 8)]
        for a in whole_shifts:
            tile = s_ref[8 * (CONV_HALO // 8 + m - a):8 * (CONV_HALO // 8 + m - a) + 8, lanes]
            for k in range(8):
                if 8 * a + k < CONV_K and not (m < 0 and k == 0):
                    sums[8 * a + k] = sums[8 * a + k] + shifted[k] * tile
        rolled = rolled_next
    return sums


def _conv_fwd(zr, w_dw, b_dw, g_ln, b_ln, comm=None):
    seq = zr.shape[0]

    def body(a_ref, b_ref, ah_ref, bh_ref, w_ref, bias_ref, g_ref, bl_ref, u1_ref, u3_ref, s_ref):
        _glu_into(s_ref, a_ref, b_ref, ah_ref, bh_ref, pl.program_id(0) == 0)
        row_id = lax.broadcasted_iota(jnp.int32, (8, CONV_LANES), 0)
        for lo in range(0, D_CONV, CONV_LANES):
            lanes = slice(lo, lo + CONV_LANES)
            bias = jnp.broadcast_to(bias_ref[:, lanes], (8, CONV_LANES))

            def emit(m, out, lanes=lanes, bias=bias):
                u1_ref[8 * m:8 * m + 8, lanes] = out + bias

            _conv_back(s_ref, CONV_HALO // 8, _lag_weights(w_ref, lanes), lanes, row_id, emit)
        _, _, u2 = _ln_silu(u1_ref[...], g_ref[...], bl_ref[...])
        u3_ref[...] = (u2 * _sig(u2)).astype(BF16)

    return _host_call(
        lambda ins, outs, scratch: body(*ins, *outs, *scratch), "conv_fwd", grid=(seq // CONV_ROWS,),
        in_specs=[_tile(CONV_ROWS, D_CONV, 0), _tile(CONV_ROWS, D_CONV, 1),
                  _prev(CONV_HALO, D_CONV, CONV_ROWS, 0), _prev(CONV_HALO, D_CONV, CONV_ROWS, 1),
                  _full((CONV_K, D_CONV)), _full((1, D_CONV)), _full((1, D_CONV)), _full((1, D_CONV))],
        out_specs=[_tile(CONV_ROWS, D_CONV), _tile(CONV_ROWS, D_CONV)],
        out_shape=[jax.ShapeDtypeStruct((seq, D_CONV), F32), jax.ShapeDtypeStruct((seq, D_CONV), BF16)],
        scratch_shapes=[pltpu.VMEM((CONV_HALO + CONV_ROWS, D_CONV), F32)],
        args=[zr, zr, zr, zr, w_dw, b_dw, g_ln, b_ln], comm=comm)


def _conv_bwd(zr, u1, du3, w_dw, g_ln, b_ln, comm=None):
    seq = zr.shape[0]
    n_tiles = seq // CONV_ROWS
    n_halo = seq // CONV_HALO
    ext = CONV_ROWS + CONV_HALO

    def body(a_ref, b_ref, ah_ref, bh_ref, u1_ref, u1n_ref, d3_ref, d3n_ref, w_ref, g_ref, bl_ref,
             da_ref, db_ref, dw_ref, small_ref, s_ref, d_ref, du0_ref):
        i = pl.program_id(0)

        @pl.when(i == 0)
        def _():
            dw_ref[...] = jnp.zeros_like(dw_ref)
            small_ref[...] = jnp.zeros_like(small_ref)

        _glu_into(s_ref, a_ref, b_ref, ah_ref, bh_ref, i == 0)
        gv, bv = g_ref[...], bl_ref[...]

        def du1_of(u1, d3):
            xhat, rs, u2 = _ln_silu(u1, gv, bv)
            sg = _sig(u2)
            du2 = d3 * (sg * (1.0 + u2 * (1.0 - sg)))
            dxh = du2 * gv
            du1 = rs * (dxh - jnp.mean(dxh, axis=-1, keepdims=True)
                        - xhat * jnp.mean(dxh * xhat, axis=-1, keepdims=True))
            return du1, du2, xhat

        du1, du2, xhat = du1_of(u1_ref[...], d3_ref[...])
        du1n, _, _ = du1_of(u1n_ref[...], d3n_ref[...])
        d_ref[0:CONV_ROWS, :] = du1
        d_ref[CONV_ROWS:ext, :] = jnp.where(i == n_tiles - 1, 0.0, du1n)
        small_ref[0:1, :] += _colsum(du1)
        small_ref[1:2, :] += _colsum(du2 * xhat)
        small_ref[2:3, :] += _colsum(du2)
        row_id = lax.broadcasted_iota(jnp.int32, (8, CONV_LANES), 0)
        for lo in range(0, D_CONV, CONV_LANES):
            lanes = slice(lo, lo + CONV_LANES)

            def emit(m, out, lanes=lanes):
                du0_ref[8 * m:8 * m + 8, lanes] = out

            _conv_ahead(d_ref, _lag_weights(w_ref, lanes), lanes, row_id, emit)
            for whole_shifts in ((0, 1), (2, 3)):
                for e, total in _conv_weight_sums(d_ref, s_ref, lanes, row_id, whole_shifts).items():
                    dw_ref[CONV_K - 1 - e:CONV_K - e, lanes] += _colsum(total)
        du0 = du0_ref[...]
        sb = _sig(b_ref[...])
        da = du0 * sb
        dbv = du0 * a_ref[...] * sb * (1.0 - sb)
        da_ref[...] = da.astype(BF16)
        db_ref[...] = dbv.astype(BF16)
        small_ref[3:4, :] += _colsum(da)
        small_ref[4:5, :] += _colsum(dbv)

    return _host_call(
        lambda ins, outs, scratch: body(*ins, *outs, *scratch), "conv_bwd", grid=(n_tiles,),
        in_specs=[_tile(CONV_ROWS, D_CONV, 0), _tile(CONV_ROWS, D_CONV, 1),
                  _prev(CONV_HALO, D_CONV, CONV_ROWS, 0), _prev(CONV_HALO, D_CONV, CONV_ROWS, 1),
                  _tile(CONV_ROWS, D_CONV), _next(CONV_HALO, D_CONV, CONV_ROWS, n_halo),
                  _tile(CONV_ROWS, D_CONV), _next(CONV_HALO, D_CONV, CONV_ROWS, n_halo),
                  _full((CONV_K, D_CONV)), _full((1, D_CONV)), _full((1, D_CONV))],
        out_specs=[_tile(CONV_ROWS, D_CONV), _tile(CONV_ROWS, D_CONV), _full((CONV_HALO, D_CONV)),
                   _full((8, D_CONV))],
        out_shape=[jax.ShapeDtypeStruct((seq, D_CONV), BF16), jax.ShapeDtypeStruct((seq, D_CONV), BF16),
                   jax.ShapeDtypeStruct((CONV_HALO, D_CONV), F32), jax.ShapeDtypeStruct((8, D_CONV), F32)],
        scratch_shapes=[pltpu.VMEM((ext, D_CONV), F32), pltpu.VMEM((ext, D_CONV), F32),
                        pltpu.VMEM((CONV_ROWS, D_CONV), F32)],
        args=[zr, zr, zr, zr, u1, u1, du3, du3, w_dw, g_ln, b_ln], comm=comm)


MERGE_ROWS = 256


def _merge_fwd(ao, u3, zr, w_ao, w_co, b_co):
    seq = ao.shape[0]

    def body(ao_ref, u3_ref, ga_ref, gb_ref, wa_ref, wc_ref, bc_ref, y_ref, a_ref, cb_ref):
        a = jnp.dot(ao_ref[...], wa_ref[...], preferred_element_type=F32)
        cb = jnp.dot(u3_ref[...], wc_ref[...], preferred_element_type=F32) + bc_ref[...]
        a_ref[...] = a
        cb_ref[...] = cb
        y_ref[...] = (_sig(ga_ref[...]) * a + _sig(gb_ref[...]) * cb).astype(BF16)

    f32_out = jax.ShapeDtypeStruct((seq, D_MODEL), F32)
    return pl.pallas_call(
        body, name="merge_fwd",
        out_shape=[jax.ShapeDtypeStruct((seq, D_MODEL), BF16), f32_out, f32_out],
        grid=(seq // MERGE_ROWS,),
        in_specs=[_tile(MERGE_ROWS, D_ATTN), _tile(MERGE_ROWS, D_CONV), _tile(MERGE_ROWS, D_MODEL, 1),
                  _tile(MERGE_ROWS, D_MODEL, 2), _full(w_ao.shape), _full(w_co.shape), _full((1, D_MODEL))],
        out_specs=[_tile(MERGE_ROWS, D_MODEL)] * 3, compiler_params=_cparams(1),
    )(ao, u3, zr, zr, w_ao, w_co, b_co)


def _merge_bwd(a, cb, zr, rows):
    seq = a.shape[0]

    def fn(dy_v, first, ins, outs):
        a_ref, cb_ref, ga_ref, gb_ref = ins
        da_ref, dcb_ref, dga_ref, dgb_ref, small_ref = outs

        @pl.when(first)
        def _():
            small_ref[...] = jnp.zeros_like(small_ref)

        sa, sb = _sig(ga_ref[...]), _sig(gb_ref[...])
        dcb = dy_v * sb
        dga = dy_v * a_ref[...] * sa * (1.0 - sa)
        dgb = dy_v * cb_ref[...] * sb * (1.0 - sb)
        da_ref[...] = (dy_v * sa).astype(BF16)
        dcb_ref[...] = dcb.astype(BF16)
        dga_ref[...] = dga.astype(BF16)
        dgb_ref[...] = dgb.astype(BF16)
        small_ref[0:1, :] += _colsum(dga)
        small_ref[1:2, :] += _colsum(dgb)
        small_ref[2:3, :] += _colsum(dcb)

    bf = jax.ShapeDtypeStruct((seq, D_MODEL), BF16)
    gate = lambda col: pl.BlockSpec((rows, D_MODEL), lambda i, j: (i, col))
    return _Epilogue(
        [a, cb, zr, zr], [_row_tile(rows, D_MODEL), _row_tile(rows, D_MODEL), gate(1), gate(2)],
        [bf, bf, bf, bf, jax.ShapeDtypeStruct((8, D_MODEL), F32)],
        [_row_tile(rows, D_MODEL)] * 4 + [_whole((8, D_MODEL))], fn, keep_product=False)


FFN_ROWS = 2048
FFN_BLOCKS = D_FF // FFN_COLS
GELU_C = math.sqrt(2.0 / math.pi)


def _gelu(v):
    t = jnp.tanh(GELU_C * (v + 0.044715 * (v * v * v)))
    return 0.5 * v * (1.0 + t), t


def _gelu_grad(v, t):
    return 0.5 * (1.0 + t) + 0.5 * v * (1.0 - t * t) * (GELU_C * (1.0 + 3.0 * 0.044715 * (v * v)))


def _sublane_rows(ref, n):
    return [jnp.broadcast_to(ref[r:r + 1, :], (8, FFN_COLS)) for r in range(n)]


def _rolls(tile, shifts):
    return tuple(pltpu.roll(tile, s, 0) for s in shifts)


def _behind(prev_rolls, cur, row_id):
    rolls = _rolls(cur, (1, 2))
    x1 = jnp.where(row_id < 1, prev_rolls[0], rolls[0])
    x2 = jnp.where(row_id < 2, prev_rolls[1], rolls[1])
    return (x2, x1, cur), rolls


def _ahead(cur_rolls, next_rolls, row_id):
    return (jnp.where(row_id < 7, cur_rolls[0], next_rolls[0]), jnp.where(row_id < 6, cur_rolls[1], next_rolls[1]))


def _conv3(taps, w, bias):
    return w[0] * taps[0] + w[1] * taps[1] + w[2] * taps[2] + bias


def _ffn_specs(rows):
    tile = lambda off: pl.BlockSpec((rows, FFN_COLS), lambda j, i: (i, j + off))
    prev = lambda off: pl.BlockSpec((FFN_HALO, FFN_COLS),
                                    lambda j, i: (jnp.maximum(i * (rows // FFN_HALO) - 1, 0), j + off))
    wgt = lambda off: pl.BlockSpec((3, FFN_COLS), lambda j, i: (0, j + off))
    vec = lambda off: pl.BlockSpec((1, FFN_COLS), lambda j, i: (0, j + off))
    return tile, prev, wgt, vec


def _ffn_act(up, w_dw, b_dw):
    seq = up.shape[0]
    tile, prev, wgt, vec = _ffn_specs(FFN_ROWS)

    def body(v_ref, g_ref, vp_ref, gp_ref, wv_ref, wg_ref, bv_ref, bg_ref, act_ref):
        first = pl.program_id(1) == 0
        row_id = lax.broadcasted_iota(jnp.int32, (8, FFN_COLS), 0)
        wv, wg = _sublane_rows(wv_ref, 3), _sublane_rows(wg_ref, 3)
        (bv,), (bg,) = _sublane_rows(bv_ref, 1), _sublane_rows(bg_ref, 1)
        rolls_v = _rolls(jnp.where(first, 0.0, vp_ref[...]), (1, 2))
        rolls_g = _rolls(jnp.where(first, 0.0, gp_ref[...]), (1, 2))
        for row in range(0, FFN_ROWS, 16):
            halves = []
            for r in (row, row + 8):
                taps_v, rolls_v = _behind(rolls_v, v_ref[r:r + 8, :], row_id)
                taps_g, rolls_g = _behind(rolls_g, g_ref[r:r + 8, :], row_id)
                halves.append(_gelu(_conv3(taps_g, wg, bg))[0] * _conv3(taps_v, wv, bv))
            act_ref[row:row + 16, :] = jnp.concatenate(halves, axis=0).astype(BF16)

    return pl.pallas_call(
        body, name="ffn_act", out_shape=jax.ShapeDtypeStruct((seq, D_FF), BF16),
        grid=(FFN_BLOCKS, seq // FFN_ROWS),
        in_specs=[tile(0), tile(FFN_BLOCKS), prev(0), prev(FFN_BLOCKS), wgt(0), wgt(FFN_BLOCKS),
                  vec(0), vec(FFN_BLOCKS)],
        out_specs=tile(0), compiler_params=_cparams(2),
    )(up, up, up, up, w_dw, w_dw, b_dw, b_dw)


def _ffn_act_bwd(up, dact, w_dw, b_dw, comm=None):
    seq = up.shape[0]
    n_tiles = seq // FFN_ROWS
    n_halo = seq // FFN_HALO
    tile, prev, wgt, vec = _ffn_specs(FFN_ROWS)
    nxt = lambda off: pl.BlockSpec(
        (FFN_HALO, FFN_COLS), lambda j, i: (jnp.minimum((i + 1) * (FFN_ROWS // FFN_HALO), n_halo - 1), j + off))
    acc = lambda off: pl.BlockSpec((8, FFN_COLS), lambda j, i: (0, j + off))

    def body(v_ref, g_ref, vp_ref, gp_ref, vn_ref, gn_ref, da_ref, dan_ref, wv_ref, wg_ref, bv_ref, bg_ref,
             dv_out, dg_out, dwv_ref, dwg_ref, dbv_ref, dbg_ref):
        i = pl.program_id(1)
        first, last = i == 0, i == n_tiles - 1

        @pl.when(first)
        def _():
            for r in (dwv_ref, dwg_ref, dbv_ref, dbg_ref):
                r[...] = jnp.zeros_like(r)

        row_id = lax.broadcasted_iota(jnp.int32, (8, FFN_COLS), 0)
        wv, wg = _sublane_rows(wv_ref, 3), _sublane_rows(wg_ref, 3)
        (bv,), (bg,) = _sublane_rows(bv_ref, 1), _sublane_rows(bg_ref, 1)
        zero = jnp.zeros((8, FFN_COLS), F32)
        sums_v, sums_g = [zero] * 4, [zero] * 4
        rolls_v = _rolls(jnp.where(first, 0.0, vp_ref[...]), (1, 2))
        rolls_g = _rolls(jnp.where(first, 0.0, gp_ref[...]), (1, 2))
        behind = None
        done_v, done_g = [], []

        def grads(v_tile, g_tile, dact, rolls_v, rolls_g):
            taps_v, rolls_v = _behind(rolls_v, v_tile, row_id)
            taps_g, rolls_g = _behind(rolls_g, g_tile, row_id)
            val, gate = _conv3(taps_v, wv, bv), _conv3(taps_g, wg, bg)
            gel, t = _gelu(gate)
            return dact * gel, dact * val * _gelu_grad(gate, t), taps_v, taps_g, rolls_v, rolls_g

        def finish(tile, nxt, row):
            for (d, d_rolls), (_, n_rolls), w, done, o_ref in ((tile[0], nxt[0], wv, done_v, dv_out),
                                                               (tile[1], nxt[1], wg, done_g, dg_out)):
                d1, d2 = _ahead(d_rolls, n_rolls, row_id)
                done.append(w[2] * d + w[1] * d1 + w[0] * d2)
                if len(done) == 2:
                    o_ref[row - 16:row, :] = jnp.concatenate(done, axis=0).astype(BF16)
                    done.clear()

        for row in range(0, FFN_ROWS, 16):
            dact16 = da_ref[row:row + 16, :].astype(F32)
            for r, dact in ((row, dact16[0:8, :]), (row + 8, dact16[8:16, :])):
                dval, dgate, taps_v, taps_g, rolls_v, rolls_g = grads(v_ref[r:r + 8, :], g_ref[r:r + 8, :], dact,
                                                                      rolls_v, rolls_g)
                sums_v = [s + dval * x for s, x in zip(sums_v, taps_v)] + [sums_v[3] + dval]
                sums_g = [s + dgate * x for s, x in zip(sums_g, taps_g)] + [sums_g[3] + dgate]
                tile = ((dval, _rolls(dval, (7, 6))), (dgate, _rolls(dgate, (7, 6))))
                if behind is not None:
                    finish(behind, tile, r)
                behind = tile
        dact_next = jnp.where(last, 0.0, dan_ref[...].astype(F32)[0:FFN_HALO, :])
        dval, dgate, *_ = grads(vn_ref[...], gn_ref[...], dact_next, rolls_v, rolls_g)
        finish(behind, ((dval, _rolls(dval, (7, 6))), (dgate, _rolls(dgate, (7, 6)))), FFN_ROWS)
        for sums, dw_ref, db_ref in ((sums_v, dwv_ref, dbv_ref), (sums_g, dwg_ref, dbg_ref)):
            for tap in range(3):
                dw_ref[tap:tap + 1, :] += _colsum(sums[tap])
            db_ref[0:1, :] += _colsum(sums[3])

    half = jax.ShapeDtypeStruct((seq, D_FF), BF16)
    acc_shape = jax.ShapeDtypeStruct((8, D_FF), F32)
    return _host_call(
        lambda ins, outs, scratch: body(*ins, *outs, *scratch), "ffn_act_bwd", grid=(FFN_BLOCKS, n_tiles),
        in_specs=[tile(0), tile(FFN_BLOCKS), prev(0), prev(FFN_BLOCKS), nxt(0), nxt(FFN_BLOCKS),
                  tile(0), pl.BlockSpec((16, FFN_COLS), lambda j, i: (
                      jnp.minimum((i + 1) * (FFN_ROWS // 16), seq // 16 - 1), j)),
                  wgt(0), wgt(FFN_BLOCKS), vec(0), vec(FFN_BLOCKS)],
        out_specs=[tile(0), tile(0), acc(0), acc(0), acc(0), acc(0)],
        out_shape=[half, half, acc_shape, acc_shape, acc_shape, acc_shape],
        scratch_shapes=[], args=[up, up, up, up, up, up, dact, dact, w_dw, w_dw, b_dw, b_dw], comm=comm)


def _cols_to_blocks(full_cols):
    k, n8 = full_cols.shape
    return jnp.transpose(full_cols.reshape(k, N_DEV, n8 // N_DEV), (1, 0, 2))


def _rows_to_blocks(full_rows):
    r8, n = full_rows.shape
    return full_rows.reshape(N_DEV, r8 // N_DEV, n)


def _blocks_to_cols(gathered):
    _, k, n = gathered.shape
    return jnp.transpose(gathered, (1, 0, 2)).reshape(k, N_DEV * n)


def kernel(x, c, w_ada, b_ada, g_pre_mix, g_post_mix, w_in, b_in, rel_bias, w_attn_o, w_dw_conv, b_dw_conv, g_conv_ln, b_conv_ln, w_conv_o, b_conv_o, w_mix_o, g_pre_ffn, g_post_ffn, w_up, w_dw_ffn, b_dw_ffn, w_down, loss_target, m_w_ada, m_b_ada, m_g_pre_mix, m_g_post_mix, m_w_in, m_b_in, m_rel_bias, m_w_attn_o, m_w_dw_conv, m_b_dw_conv, m_g_conv_ln, m_b_conv_ln, m_w_conv_o, m_b_conv_o, m_w_mix_o, m_g_pre_ffn, m_g_post_ffn, m_w_up, m_w_dw_ffn, m_b_dw_ffn, m_w_down, v_w_ada, v_b_ada, v_g_pre_mix, v_g_post_mix, v_w_in, v_b_in, v_rel_bias, v_w_attn_o, v_w_dw_conv, v_b_dw_conv, v_g_conv_ln, v_b_conv_ln, v_w_conv_o, v_b_conv_o, v_w_mix_o, v_g_pre_ffn, v_g_post_ffn, v_w_up, v_w_dw_ffn, v_b_dw_ffn, v_w_down):
    names = ["w_ada", "b_ada", "g_pre_mix", "g_post_mix", "w_in", "b_in", "rel_bias", "w_attn_o", "w_dw_conv",
             "b_dw_conv", "g_conv_ln", "b_conv_ln", "w_conv_o", "b_conv_o", "w_mix_o", "g_pre_ffn", "g_post_ffn",
             "w_up", "w_dw_ffn", "b_dw_ffn", "w_down"]
    weights = dict(zip(names, [w_ada, b_ada, g_pre_mix, g_post_mix, w_in, b_in, rel_bias, w_attn_o, w_dw_conv,
                               b_dw_conv, g_conv_ln, b_conv_ln, w_conv_o, b_conv_o, w_mix_o, g_pre_ffn,
                               g_post_ffn, w_up, w_dw_ffn, b_dw_ffn, w_down]))
    mom_m = dict(zip(names, [m_w_ada, m_b_ada, m_g_pre_mix, m_g_post_mix, m_w_in, m_b_in, m_rel_bias, m_w_attn_o,
                             m_w_dw_conv, m_b_dw_conv, m_g_conv_ln, m_b_conv_ln, m_w_conv_o, m_b_conv_o,
                             m_w_mix_o, m_g_pre_ffn, m_g_post_ffn, m_w_up, m_w_dw_ffn, m_b_dw_ffn, m_w_down]))
    mom_v = dict(zip(names, [v_w_ada, v_b_ada, v_g_pre_mix, v_g_post_mix, v_w_in, v_b_in, v_rel_bias, v_w_attn_o,
                             v_w_dw_conv, v_b_dw_conv, v_g_conv_ln, v_b_conv_ln, v_w_conv_o, v_b_conv_o,
                             v_w_mix_o, v_g_pre_ffn, v_g_post_ffn, v_w_up, v_w_dw_ffn, v_b_dw_ffn, v_w_down]))
    shapes = {n: w.shape for n, w in weights.items()}

    seq = x.shape[1]
    me = 4 * lax.axis_index("x") + 2 * lax.axis_index("y") + lax.axis_index("c")
    x2 = x.reshape(seq, D_MODEL)
    target = loss_target.reshape(seq, D_MODEL)
    sq = lambda a: a.reshape(a.shape[1:])
    bf = lambda a: sq(a).astype(BF16)

    transposed = lambda a: jnp.swapaxes(sq(a), 0, 1)

    c_all, mod_all = _ada_mod(c, sq(w_ada))
    c_all = c_all.reshape(N_DEV, D_MODEL)
    mod = lax.dynamic_index_in_dim(mod_all, me, axis=1, keepdims=False)
    mod6 = (mod.reshape(1, 6 * D_MODEL) + b_ada).reshape(6, D_MODEL)

    h1, (g_in, g_dwc, g_dwf) = _pre_mix(
        x2, mod6, g_pre_mix, comm=_gather_comm([transposed(w_in).astype(BF16), sq(w_dw_conv), sq(w_dw_ffn)]))
    wt_in = g_in.reshape(g_in.shape[0] * g_in.shape[1], D_MODEL)
    wf_dwc = _blocks_to_cols(g_dwc)
    wf_dwf = _blocks_to_cols(g_dwf)
    qkv = _mm(h1, wt_in, "nt", BF16, "in_proj_qkv", bias=b_in, tm=1024, tn=768, cols=(0, 3 * D_ATTN))
    zr, _, (g_ao, g_co, g_mo) = _mm(h1, wt_in, "nt", F32, "in_proj_rest", bias=b_in, tm=1024, tn=3 * D_ATTN,
                                 cols=(3 * D_ATTN, 2 * D_CONV + 2 * D_MODEL),
                                 comm=_gather_comm([bf(w_attn_o), bf(w_conv_o), bf(w_mix_o)]))
    table = jnp.transpose(_bias_table(sq(rel_bias)), (1, 0, 2))
    ao, (g_up,) = _attn_fwd(qkv, table, comm=_gather_comm([transposed(w_up).astype(BF16)]))
    (u1, u3), (g_dn,) = _conv_fwd(zr, wf_dwc, b_dw_conv, g_conv_ln, b_conv_ln, comm=_gather_comm([bf(w_down)]))
    wf_ao = _blocks_to_cols(g_ao)
    wf_co = _blocks_to_cols(g_co)
    wf_mo = g_mo.reshape(D_MODEL, D_MODEL)
    wt_up = g_up.reshape(g_up.shape[0] * g_up.shape[1], D_MODEL)
    wf_dn = g_dn.reshape(D_FF, D_MODEL)
    y, a_br, cb_br = _merge_fwd(ao, u3, zr, wf_ao, wf_co, b_conv_o)
    ymix, (x1, h2), _ = _mm(y, wf_mo, "nn", F32, "mix_o", tm=512, tn=D_MODEL,
                            epilogue=_post_mix_pre_ffn(x2, mod6, g_post_mix, g_pre_ffn, 512))
    up = _mm(h2, wt_up, "nt", F32, "ffn_up", tm=1024, tn=1408)
    act = _ffn_act(up, wf_dwf, b_dw_ffn)
    _, (loss_lanes, dout, dyf, small_f), _ = _mm(act, wf_dn, "nn", F32, "ffn_down", tm=512, tn=D_MODEL,
                                                 epilogue=_final(x1, target, mod6, g_post_ffn, 512))

    dact = _mm(dyf, wf_dn, "nt", BF16, "ffn_down_dx", tm=1024, tn=1408)
    gw_down = _mm(act, dyf, "tn", BF16, "ffn_down_dw", tm=256, tn=1024)
    (dup_v, dup_g, dwv, dwg, dbv, dbg), (parts_down,) = _ffn_act_bwd(
        up, dact, wf_dwf, b_dw_ffn, comm=_scatter_comm([_rows_to_blocks(gw_down)]))
    _, (dx1, dymix, small_m), _ = _mm([dup_v, dup_g], wt_up, "nn", F32, "ffn_up_dx", tm=512, tn=D_MODEL,
                                      epilogue=_mid_bwd(x1, dout, ymix, mod6, g_pre_ffn, g_post_mix, 512))
    blocks_up = _rows_to_blocks(_mm_tn_rows([dup_v, dup_g], h2, "ffn_up_dw"))
    _, (da, dcb, dga, dgb, small_g), _ = _mm(dymix, wf_mo, "nt", F32, "mix_o_dx", tm=512, tn=D_MODEL,
                                             epilogue=_merge_bwd(a_br, cb_br, zr, 512))
    gw_mo = _mm(y, dymix, "tn", BF16, "mix_o_dw")
    dao = _mm(da, wf_ao, "nt", BF16, "attn_o_dx", tm=1024)
    gw_ao = _mm(ao, da, "tn", BF16, "attn_o_dw")
    du3 = _mm(dcb, wf_co, "nt", F32, "conv_o_dx", tm=1024)
    gw_co = _mm(u3, dcb, "tn", BF16, "conv_o_dw")
    (dq, dkt, dvt, dbias, small_a), (parts_up,) = _attn_bwd(
        qkv, table, dao, comm=_scatter_comm([blocks_up]))
    g_rel = _bias_grad(jnp.transpose(dbias, (1, 0, 2)))
    (dglu_a, dglu_b, dw_conv, small_c), (parts_mo, parts_ao, parts_co) = _conv_bwd(
        zr, u1, du3, wf_dwc, g_conv_ln, b_conv_ln,
        comm=_scatter_comm([_rows_to_blocks(gw_mo), _cols_to_blocks(gw_ao), _cols_to_blocks(gw_co)]))
    dz = _assemble_dz(dq, dkt, dvt, dglu_a, dglu_b, dga, dgb)
    blocks_in = _rows_to_blocks(_mm(dz, h1, "tn", BF16, "in_proj_dw", tm=512, tn=D_MODEL))
    _, (grad_x, small_x), (parts_in, _, _) = _mm(dz, wt_in, "nn", F32, "in_proj_dx", tm=512, tn=D_MODEL,
                                                 comm=_pair_scatter_comm(blocks_in),
                                                 epilogue=_pre_mix_bwd(x2, dx1, mod6, g_pre_mix, 512))

    packed = _pack_grads(small_x, small_m, small_f, small_g, small_a, small_c, dbv, dbg, dwv, dwg, dw_conv)
    gathered, gathered_rel, gathered_loss = _run_comm(_gather_comm([packed, g_rel, loss_lanes]), "gather_small")
    gathered = gathered.reshape(N_DEV, PACKED_TOTAL)
    updates, g_dwc_full, g_dwf_full, loss_all = _small_adamw(gathered, gathered_rel, gathered_loss, weights, mom_m,
                                                             mom_v)
    loss = loss_all[0, 0]

    grads, deltas, new_m, new_v = {}, {}, {}, {}

    def record(name, update, is_transposed=False):
        for dst, val in zip((grads, deltas, new_m, new_v), update):
            dst[name] = (jnp.swapaxes(val, 0, 1) if is_transposed else val).reshape(shapes[name])

    for name, update in updates.items():
        record(name, update)

    def local_update(name, grad, view=sq):
        record(name, _adamw(view(weights[name]), view(mom_m[name]), view(mom_v[name]), "adamw_" + name, g=view(grad)))

    def taps_major(a):
        return a.reshape(a.shape[1], 1, a.shape[2])

    conv_cols, ffn_cols, ada_cols = D_CONV // N_DEV, 2 * D_FF // N_DEV, 6 * D_MODEL // N_DEV
    local_update("w_dw_conv", lax.dynamic_slice(g_dwc_full, (0, me * conv_cols), (CONV_K, conv_cols))[None], taps_major)
    local_update("w_dw_ffn", lax.dynamic_slice(g_dwf_full, (0, me * ffn_cols), (3, ffn_cols))[None], taps_major)
    local_update("w_ada", _ada_grad(c_all, lax.dynamic_slice(gathered, (0, me * ada_cols), (N_DEV, ada_cols)))[None])

    for name, part in (("w_attn_o", parts_ao), ("w_conv_o", parts_co), ("w_mix_o", parts_mo), ("w_down", parts_down)):
        record(name, _adamw(sq(weights[name]), sq(mom_m[name]), sq(mom_v[name]), "adamw_" + name, parts=part))
    for name, part in (("w_in", parts_in), ("w_up", parts_up)):
        record(name, _adamw(transposed(weights[name]), transposed(mom_m[name]), transposed(mom_v[name]),
                            "adamw_" + name, parts=part), is_transposed=True)

    return (loss, grad_x.reshape(x.shape), *[grads[n] for n in names], *[deltas[n] for n in names],
            *[new_m[n] for n in names], *[new_v[n] for n in names])
```

```python
import functools
import math

import jax
import jax.numpy as jnp
from jax import lax
from jax.experimental import pallas as pl
from jax.experimental.pallas import tpu as pltpu

F32 = jnp.float32
BF16 = jnp.bfloat16
HIGHEST = lax.Precision.HIGHEST

D_MODEL = 1024
CHUNK = 64
LEFT_CHUNKS = 8
BAND = (LEFT_CHUNKS + 1) * CHUNK
PAD_ROWS = LEFT_CHUNKS * CHUNK
GROUP = 4
GROUP_Q = GROUP * CHUNK
GROUP_K = GROUP_Q + PAD_ROWS
SOFTMAX_ROWS = 16
TOEPLITZ = 640
N_HEADS = 8
HEAD_DIM = 64
D_ATTN = 512
D_CONV = 512
CONV_K = 31
CONV_HALO = 32
MAX_REL = 128
N_REL = 2 * MAX_REL + 1
D_FF = 2816
FFN_HALO = 8
FFN_COLS = 256
EPS = 1e-6
NEG_INF = -1e30
N_DEV = 8

ADAM_LR = 0.001
ADAM_B1 = 0.9
ADAM_B2 = 0.999
ADAM_EPS = 1e-08
ADAM_WD = 0.01
ADAM_STEP = 10

VMEM_LIMIT_BYTES = 56 * 1024 * 1024
ADAMW_BLOCK_BYTES = 768 * 1024

MESH = pl.DeviceIdType.MESH
ANY = pl.BlockSpec(memory_space=pl.ANY)

SH_M, SC_M, GT_M, SH_F, SC_F, GT_F = range(6)

SMALL = (("b_ada", 6144), ("g_pre_mix", 1024), ("g_post_mix", 1024), ("b_in", 4608), ("b_dw_conv", 512),
         ("g_conv_ln", 512), ("b_conv_ln", 512), ("b_conv_o", 1024), ("g_pre_ffn", 1024), ("g_post_ffn", 1024),
         ("b_dw_ffn", 5632))
PACKED_TOTAL = sum(n for _, n in SMALL) + CONV_K * D_CONV + 3 * 2 * D_FF


def _cparams(n_axes):
    return pltpu.CompilerParams(vmem_limit_bytes=VMEM_LIMIT_BYTES,
                                dimension_semantics=("arbitrary",) * n_axes)


def _sig(v):
    return 1.0 / (1.0 + jnp.exp(-v))


def _pick(n, target):
    if n <= target:
        return n
    t = target - target % 128
    while n % t:
        t -= 128
    return t


def _tile(rows, cols, col=0):
    return pl.BlockSpec((rows, cols), lambda i: (i, col))


def _full(shape):
    zeros = (0,) * len(shape)
    return pl.BlockSpec(shape, lambda i: zeros)


def _prev(halo, cols, rows, col=0):
    return pl.BlockSpec((halo, cols), lambda i: (jnp.maximum(i * (rows // halo) - 1, 0), col))


def _next(halo, cols, rows, n_blocks, col=0):
    return pl.BlockSpec((halo, cols), lambda i: (jnp.minimum((i + 1) * (rows // halo), n_blocks - 1), col))


class _Comm:
    def __init__(self, inputs, out_shapes, sems, start, finish, relay=None, early=None):
        self.inputs, self.out_shapes, self.sems, self.start, self.finish = inputs, out_shapes, sems, start, finish
        self.relay, self.early = relay, early


def _host_call(body, name, grid, in_specs, out_specs, out_shape, scratch_shapes, args, comm=None):
    n_in, n_out, n_scr = len(args), len(out_shape), len(scratch_shapes)
    c_in = list(comm.inputs) if comm else []
    c_out = list(comm.out_shapes) if comm else []
    c_sem = list(comm.sems) if comm else []

    def full(*refs):
        bounds = [0, n_in, len(c_in), n_out, len(c_out), n_scr, len(c_sem)]
        cuts = [sum(bounds[:i + 1]) for i in range(len(bounds))]
        ins, cins, outs, couts, scr, csems = (refs[lo:hi] for lo, hi in zip(cuts[:-1], cuts[1:]))
        if comm:
            first = functools.reduce(jnp.logical_and, [pl.program_id(ax) == 0 for ax in range(len(grid))])
            pl.when(first)(lambda: comm.start(cins, couts, csems))
            if comm.early is not None:
                strides = [math.prod(grid[ax + 1:]) for ax in range(len(grid))]
                step = sum(pl.program_id(ax) * strides[ax] for ax in range(len(grid)))
                pl.when(step == 1)(lambda: comm.early(cins, couts, csems))
            last = functools.reduce(jnp.logical_and, [pl.program_id(ax) == grid[ax] - 1 for ax in range(len(grid))])
            if comm.relay is not None:
                pl.when(last)(lambda: comm.relay(cins, couts, csems))
        body(ins, outs, scr)
        if comm:
            pl.when(last)(lambda: comm.finish(cins, couts, csems))

    res = pl.pallas_call(
        full, name=name, grid=grid, in_specs=list(in_specs) + [ANY] * len(c_in),
        out_specs=list(out_specs) + [ANY] * len(c_out), out_shape=list(out_shape) + c_out,
        scratch_shapes=list(scratch_shapes) + c_sem, compiler_params=_cparams(len(grid)),
    )(*args, *c_in)
    return list(res[:n_out]), list(res[n_out:])


def _run_comm(comm, name):
    n_in, n_out = len(comm.inputs), len(comm.out_shapes)

    def body(*refs):
        ins, outs, sems = refs[:n_in], refs[n_in:n_in + n_out], refs[n_in + n_out:]
        comm.start(ins, outs, sems)
        if comm.relay is not None:
            comm.relay(ins, outs, sems)
        comm.finish(ins, outs, sems)

    return pl.pallas_call(
        body, name=name, out_shape=list(comm.out_shapes), in_specs=[ANY] * n_in, out_specs=[ANY] * n_out,
        scratch_shapes=list(comm.sems),
    )(*comm.inputs)


def _place():
    return lax.axis_index("x"), lax.axis_index("y"), lax.axis_index("c")


def _gather_comm(arrs):
    n = len(arrs)

    def plan(ins, outs, sems):
        send_sems, recv_sems, local_sems = sems
        x, y, c = _place()
        me, sibling = (x, y, c), (x, y, 1 - c)
        chips = [(1 - x, y), (x, 1 - y), (1 - x, 1 - y)]

        def block(k, p):
            return outs[k].at[4 * p[0] + 2 * p[1] + p[2]]

        def copy(k, s, blk, to, src=None):
            return pltpu.make_async_remote_copy(
                src_ref=block(k, blk) if src is None else src, dst_ref=block(k, blk),
                send_sem=send_sems.at[7 * k + s], recv_sem=recv_sems.at[7 * k + s],
                device_id=to, device_id_type=MESH)

        mine = [pltpu.make_async_copy(ins[k], block(k, me), local_sems.at[k]) for k in range(n)]
        first = []
        for k in range(n):
            first.append(copy(k, 0, me, sibling, src=ins[k]))
            for j, chip in enumerate(chips):
                first.append(copy(k, 1 + j, me, (*chip, c), src=ins[k]))
        return me, sibling, chips, c, copy, mine, first

    def start(ins, outs, sems):
        *_, mine, first = plan(ins, outs, sems)
        for cp in mine + first:
            cp.start()

    def relay(ins, outs, sems):
        me, sibling, chips, c, copy, _, _ = plan(ins, outs, sems)
        for j, chip in enumerate(chips):
            for k in range(n):
                copy(k, 1 + j, (*chip, c), me).wait_recv()
                copy(k, 4 + j, (*chip, c), sibling).start()

    def finish(ins, outs, sems):
        me, sibling, chips, c, copy, mine, first = plan(ins, outs, sems)
        passed = [copy(k, 4 + j, (*chip, c), sibling) for j, chip in enumerate(chips) for k in range(n)]
        for k in range(n):
            copy(k, 0, sibling, me).wait_recv()
        for j, chip in enumerate(chips):
            for k in range(n):
                copy(k, 4 + j, (*chip, 1 - c), me).wait_recv()
        for cp in first + passed:
            cp.wait_send()
        for cp in mine:
            cp.wait()

    return _Comm(list(arrs), [jax.ShapeDtypeStruct((N_DEV,) + a.shape, a.dtype) for a in arrs],
                 [pltpu.SemaphoreType.DMA((7 * n,)), pltpu.SemaphoreType.DMA((7 * n,)),
                  pltpu.SemaphoreType.DMA((n,))], start, finish, relay)


def _scatter_comm(blocks):
    n = len(blocks)

    def plan(ins, outs, sems, arrivals):
        send_sems, recv_sems, local_sems = sems
        x, y, c = _place()
        me = 4 * x + 2 * y + c
        local = [pltpu.make_async_copy(ins[k].at[me], outs[k].at[me], local_sems.at[k]) for k in range(n)]
        sends, recvs = [], []
        for k in range(n):
            for mask in range(1, N_DEV):
                px = 1 - x if mask & 4 else x
                py = 1 - y if mask & 2 else y
                pc = 1 - c if mask & 1 else c
                peer = 4 * px + 2 * py + pc
                sem = 7 * k + mask - 1
                both = dict(send_sem=send_sems.at[sem], recv_sem=recv_sems.at[sem], device_id=(px, py, pc),
                            device_id_type=MESH)
                sends.append(pltpu.make_async_remote_copy(src_ref=ins[k].at[peer], dst_ref=outs[k].at[me], **both))
                if arrivals:
                    recvs.append(pltpu.make_async_remote_copy(src_ref=ins[k].at[me], dst_ref=outs[k].at[peer],
                                                              **both))
        return local, sends, recvs

    def start(ins, outs, sems):
        local, sends, _ = plan(ins, outs, sems, arrivals=False)
        for cp in local + sends:
            cp.start()

    def finish(ins, outs, sems):
        local, sends, recvs = plan(ins, outs, sems, arrivals=True)
        for cp in recvs:
            cp.wait_recv()
        for cp in sends:
            cp.wait_send()
        for cp in local:
            cp.wait()

    return _Comm(list(blocks), [jax.ShapeDtypeStruct(b.shape, b.dtype) for b in blocks],
                 [pltpu.SemaphoreType.DMA((7 * n,)), pltpu.SemaphoreType.DMA((7 * n,)),
                  pltpu.SemaphoreType.DMA((n,))], start, finish)


def _pair_scatter_comm(block):
    _, r, c = block.shape
    quarter = jax.ShapeDtypeStruct((4, r, c), block.dtype)

    def plan(ins, outs, sems):
        parts, got, pair = outs
        d2d_send, d2d_recv, ici_send, ici_recv, local, *bufs = sems
        x, y, cc = _place()
        mine = 2 * x + y
        chips = [(1 - x, y), (x, 1 - y), (1 - x, 1 - y)]
        to_sibling = [pltpu.make_async_remote_copy(
            src_ref=ins[0].at[2 * q + 1 - cc], dst_ref=got.at[q], send_sem=d2d_send.at[q], recv_sem=d2d_recv.at[q],
            device_id=(x, y, 1 - cc), device_id_type=MESH) for q in range(4)]
        to_chips = [pltpu.make_async_remote_copy(
            src_ref=pair.at[2 * px + py], dst_ref=parts.at[mine], send_sem=ici_send.at[j], recv_sem=ici_recv.at[j],
            device_id=(px, py, cc), device_id_type=MESH) for j, (px, py) in enumerate(chips)]
        from_chips = [pltpu.make_async_remote_copy(
            src_ref=pair.at[mine], dst_ref=parts.at[2 * px + py], send_sem=ici_send.at[j], recv_sem=ici_recv.at[j],
            device_id=(px, py, cc), device_id_type=MESH) for j, (px, py) in enumerate(chips)]
        own = pltpu.make_async_copy(pair.at[mine], parts.at[mine], local.at[5])
        order = [2 * px + py for px, py in chips] + [mine]
        return cc, got, pair, local, bufs, order, to_sibling, to_chips, from_chips, own

    def start(ins, outs, sems):
        for cp in plan(ins, outs, sems)[6]:
            cp.start()

    def early(ins, outs, sems):
        cc, got, pair, local, bufs, order, to_sibling, to_chips, _, own = plan(ins, outs, sems)
        for cp in to_sibling:
            cp.wait_recv()

        def loads(k):
            return [pltpu.make_async_copy(ins[0].at[2 * order[k] + cc], bufs[2 * (k % 2)], local.at[k % 2]),
                    pltpu.make_async_copy(got.at[order[k]], bufs[2 * (k % 2) + 1], local.at[2 + k % 2])]

        for cp in loads(0):
            cp.start()
        for k, send in enumerate(to_chips + [own]):
            if k + 1 < len(order):
                for cp in loads(k + 1):
                    cp.start()
            for cp in loads(k):
                cp.wait()
            kept, came = bufs[2 * (k % 2)], bufs[2 * (k % 2) + 1]
            kept[...] = (kept[...].astype(F32) + came[...].astype(F32)).astype(block.dtype)
            store = pltpu.make_async_copy(kept, pair.at[order[k]], local.at[4])
            store.start()
            store.wait()
            send.start()

    def finish(ins, outs, sems):
        *_, to_sibling, to_chips, from_chips, own = plan(ins, outs, sems)
        for cp in from_chips:
            cp.wait_recv()
        for cp in to_chips + to_sibling:
            cp.wait_send()
        own.wait()

    return _Comm([block], [quarter, quarter, quarter],
                 [pltpu.SemaphoreType.DMA((4,)), pltpu.SemaphoreType.DMA((4,)), pltpu.SemaphoreType.DMA((3,)),
                  pltpu.SemaphoreType.DMA((3,)), pltpu.SemaphoreType.DMA((6,))]
                 + [pltpu.VMEM((r, c), block.dtype)] * 4, start, finish, early=early)


_DIMS = {"nn": (((1,), (0,)), ((), ())), "nt": (((1,), (1,)), ((), ())), "tn": (((0,), (0,)), ((), ()))}


class _Epilogue:
    def __init__(self, args, in_specs, out_shapes, out_specs, fn, keep_product):
        self.args, self.in_specs, self.out_shapes, self.out_specs = args, in_specs, out_shapes, out_specs
        self.fn, self.keep_product = fn, keep_product


def _row_tile(rows, cols):
    return pl.BlockSpec((rows, cols), lambda i, j: (i, 0))


def _whole(shape):
    zeros = (0,) * len(shape)
    return pl.BlockSpec(shape, lambda i, j: zeros)


def _mm(a, b, mode, out_dtype, name, bias=None, tm=512, tn=512, comm=None, cols=None, epilogue=None):
    pieces = a if isinstance(a, (list, tuple)) else [a]
    assert all(p.dtype == BF16 for p in pieces) and b.dtype == BF16
    a = pieces[0]
    if mode == "tn":
        k_dim, m_dim = a.shape
    else:
        m_dim, k_dim = a.shape
    n_dim = b.shape[0] if mode == "nt" else b.shape[1]
    col0 = 0
    if cols is not None:
        assert mode != "tn" and cols[0] % tn == 0 and cols[1] % tn == 0
        col0, n_dim = cols[0] // tn, cols[1]
    tm, tn = _pick(m_dim, tm), _pick(n_dim, tn)
    assert mode != "tn" or len(pieces) == 1
    a_specs = [pl.BlockSpec((k_dim, tm), lambda i, j: (0, i)) if mode == "tn"
               else pl.BlockSpec((tm, k_dim), lambda i, j: (i, 0))] * len(pieces)
    once = dict(pipeline_mode=pl.Buffered(1)) if tn == n_dim else {}
    if mode == "nt":
        b_specs = [pl.BlockSpec((tn, k_dim), lambda i, j, p=p: (j + col0, p), **once) for p in range(len(pieces))]
    else:
        b_specs = [pl.BlockSpec((k_dim, tn), lambda i, j, p=p: (p, j + col0), **once) for p in range(len(pieces))]
    in_specs = a_specs + b_specs
    args = list(pieces) + [b] * len(pieces)
    if bias is not None:
        in_specs.append(pl.BlockSpec((1, tn), lambda i, j: (0, j + col0)))
        args.append(bias)
    dims = _DIMS[mode]
    n_pieces = len(pieces)
    n_own = len(args)
    keep = epilogue is None or epilogue.keep_product
    out_specs = [pl.BlockSpec((tm, tn), lambda i, j: (i, j))] if keep else []
    out_shape = [jax.ShapeDtypeStruct((m_dim, n_dim), out_dtype)] if keep else []
    if epilogue is not None:
        assert tn == n_dim
        in_specs, args = in_specs + list(epilogue.in_specs), args + list(epilogue.args)
        out_specs, out_shape = out_specs + list(epilogue.out_specs), out_shape + list(epilogue.out_shapes)

    def body(ins, outs, scratch):
        total = lax.dot_general(ins[0][...], ins[n_pieces][...], dims, preferred_element_type=F32)
        for p in range(1, n_pieces):
            total = total + lax.dot_general(ins[p][...], ins[n_pieces + p][...], dims, preferred_element_type=F32)
        if bias is not None:
            total = total + ins[2 * n_pieces][...]
        if keep:
            outs[0][...] = total.astype(out_dtype)
        if epilogue is not None:
            epilogue.fn(total, pl.program_id(0) == 0, ins[n_own:], outs[1:] if keep else outs)

    outs, extra = _host_call(body, name, grid=(m_dim // tm, n_dim // tn), in_specs=in_specs, out_specs=out_specs,
                             out_shape=out_shape, scratch_shapes=[], args=args, comm=comm)
    product = outs[0] if keep else None
    if comm is None and epilogue is None:
        return product
    return product, outs[1:] if keep else outs, extra


def _mm_tn_rows(pieces, b, name, tm=256):
    k_dim, n_dim = b.shape
    counts = [p.shape[1] // tm for p in pieces]
    assert all(p.shape[1] % tm == 0 for p in pieces)
    firsts = [sum(counts[:q]) for q in range(len(pieces))]

    def a_spec(first, count):
        return pl.BlockSpec((k_dim, tm), lambda i: (0, jnp.clip(i - first, 0, count - 1)))

    def body(ins, outs, scratch):
        i = pl.program_id(0)
        for a_ref, first, count in zip(ins[:-1], firsts, counts):
            @pl.when(jnp.logical_and(i >= first, i < first + count))
            def _(a_ref=a_ref):
                outs[0][...] = lax.dot_general(a_ref[...], ins[-1][...], _DIMS["tn"],
                                               preferred_element_type=F32).astype(BF16)

    (out,), _ = _host_call(
        body, name, grid=(sum(counts),),
        in_specs=[a_spec(f, c) for f, c in zip(firsts, counts)] + [_full((k_dim, n_dim))],
        out_specs=[_tile(tm, n_dim)], out_shape=[jax.ShapeDtypeStruct((sum(counts) * tm, n_dim), BF16)],
        scratch_shapes=[], args=list(pieces) + [b])
    return out


def _adam_math(w, g, m, v):
    m = ADAM_B1 * m + (1.0 - ADAM_B1) * g
    v = ADAM_B2 * v + (1.0 - ADAM_B2) * (g * g)
    m_hat = m / (1.0 - ADAM_B1 ** ADAM_STEP)
    v_hat = v / (1.0 - ADAM_B2 ** ADAM_STEP)
    delta = -ADAM_LR * (m_hat / (jnp.sqrt(v_hat) + ADAM_EPS) + ADAM_WD * w)
    return delta, m, v


def _adamw(w, m, v, name, g=None, parts=None):
    rows, cols = w.shape[0], w.shape[-1]
    tr = rows
    if rows * cols * 4 > ADAMW_BLOCK_BYTES:
        tr = max(t for t in range(16, rows, 16) if rows % t == 0 and t * cols * 4 <= ADAMW_BLOCK_BYTES)

    def body(w_ref, m_ref, v_ref, g_ref, go_ref, d_ref, mo_ref, vo_ref):
        if parts is None:
            grad = g_ref[...]
        else:
            grad = g_ref[0].astype(F32)
            for d in range(1, parts.shape[0]):
                grad = grad + g_ref[d].astype(F32)
        delta, m_new, v_new = _adam_math(w_ref[...], grad, m_ref[...], v_ref[...])
        go_ref[...] = grad
        d_ref[...] = delta
        mo_ref[...] = m_new
        vo_ref[...] = v_new

    spec = _tile(tr, cols) if w.ndim == 2 else _full(w.shape)
    g_spec = spec if parts is None else pl.BlockSpec((parts.shape[0], tr, cols), lambda i: (0, i, 0))
    shape = jax.ShapeDtypeStruct(w.shape, F32)
    return pl.pallas_call(
        body, name=name, out_shape=[shape] * 4, grid=(rows // tr,),
        in_specs=[spec, spec, spec, g_spec], out_specs=[spec] * 4, compiler_params=_cparams(1),
    )(w, m, v, g if parts is None else parts)


def _pack_grads(small_x, small_m, small_f, small_g, small_a, small_c, dbv, dbg, dwv, dwg, dw_conv):
    pieces = [
        (small_x, 2, D_MODEL), (small_x, 1, D_MODEL), (small_m, 4, D_MODEL), (small_m, 2, D_MODEL),
        (small_m, 1, D_MODEL), (small_f, 1, D_MODEL),
        (small_x, 0, D_MODEL), (small_m, 3, D_MODEL),
        (small_a, 0, D_ATTN), (small_a, 1, D_ATTN), (small_a, 2, D_ATTN), (small_c, 3, D_CONV),
        (small_c, 4, D_CONV), (small_g, 0, D_MODEL), (small_g, 1, D_MODEL),
        (small_c, 0, D_CONV), (small_c, 1, D_CONV), (small_c, 2, D_CONV),
        (small_g, 2, D_MODEL), (small_m, 0, D_MODEL), (small_f, 0, D_MODEL),
        (dbv, 0, D_FF), (dbg, 0, D_FF),
    ]
    pieces += [(dw_conv, j, D_CONV) for j in range(CONV_K)]
    pieces += [(src, tap, D_FF) for tap in range(3) for src in (dwv, dwg)]
    sources = [small_x, small_m, small_f, small_g, small_a, small_c, dbv, dbg, dwv, dwg, dw_conv]
    assert sum(width for _, _, width in pieces) == PACKED_TOTAL

    def body(*refs):
        o_ref = refs[-1]
        ref_of = {id(src): ref for src, ref in zip(sources, refs)}
        off = 0
        for src, row, width in pieces:
            o_ref[:, off:off + width] = ref_of[id(src)][row:row + 1, :]
            off += width

    return pl.pallas_call(body, name="pack_grads", out_shape=jax.ShapeDtypeStruct((1, PACKED_TOTAL), F32))(*sources)


def _small_adamw(gathered, gathered_rel, gathered_loss, weights, mom_m, mom_v):
    vec_names = [name for name, _ in SMALL]
    states = []
    for name in vec_names + ["rel_bias"]:
        states += [weights[name], mom_m[name], mom_v[name]]
    states = [a.reshape(a.shape[1:]) if a.ndim == 3 else a for a in states]
    n_state = len(states)

    def body(*refs):
        g_ref, rel_ref, loss_ref = refs[0], refs[1], refs[2]
        state_refs, out_refs = refs[3:3 + n_state], refs[3 + n_state:]
        total = g_ref[0:1, :]
        rel = rel_ref[0]
        loss = loss_ref[0]
        for d in range(1, N_DEV):
            total = total + g_ref[d:d + 1, :]
            rel = rel + rel_ref[d]
            loss = loss + loss_ref[d]
        off = 0
        for n, (name, width) in enumerate(SMALL):
            grad = total[:, off:off + width]
            w_ref, m_ref, v_ref = state_refs[3 * n:3 * n + 3]
            for ref, val in zip(out_refs[4 * n:4 * n + 4], (grad,) + _adam_math(w_ref[...], grad, m_ref[...], v_ref[...])):
                ref[...] = val
            off += width
        n = len(SMALL)
        w_ref, m_ref, v_ref = state_refs[3 * n:3 * n + 3]
        for ref, val in zip(out_refs[4 * n:4 * n + 4], (rel,) + _adam_math(w_ref[...], rel, m_ref[...], v_ref[...])):
            ref[...] = val
        dwc_ref, dwf_ref, loss_out = out_refs[4 * n + 4:]
        loss_out[...] = 0.5 * loss
        dwc_ref[...] = jnp.zeros_like(dwc_ref)
        dwf_ref[...] = jnp.zeros_like(dwf_ref)
        for j in range(CONV_K):
            dwc_ref[j:j + 1, :] = total[:, off:off + D_CONV]
            off += D_CONV
        for tap in range(3):
            dwf_ref[tap:tap + 1, :] = total[:, off:off + 2 * D_FF]
            off += 2 * D_FF

    out_shape = []
    for k in range(n_state // 3):
        out_shape += [jax.ShapeDtypeStruct(states[3 * k].shape, F32)] * 4
    out_shape += [jax.ShapeDtypeStruct((CONV_HALO, D_CONV), F32), jax.ShapeDtypeStruct((8, 2 * D_FF), F32),
                  jax.ShapeDtypeStruct((1, 128), F32)]
    res = pl.pallas_call(
        body, name="small_adamw", out_shape=out_shape,
        compiler_params=pltpu.CompilerParams(vmem_limit_bytes=VMEM_LIMIT_BYTES),
    )(gathered, gathered_rel, gathered_loss, *states)
    updates = {name: tuple(res[4 * n:4 * n + 4]) for n, name in enumerate(vec_names + ["rel_bias"])}
    return updates, res[-3], res[-2], res[-1]


def _ada_mod(c, w_shard):
    cols = w_shard.shape[1]

    def body(c_ref, w_ref, call_ref, mod_ref, send_sems, recv_sems):
        x, y, cc = _place()
        me = 4 * x + 2 * y + cc

        def exchange(ref, phase):
            sends, arrivals = [], []
            for mask in range(1, N_DEV):
                px = 1 - x if mask & 4 else x
                py = 1 - y if mask & 2 else y
                pc = 1 - cc if mask & 1 else cc
                both = dict(send_sem=send_sems.at[7 * phase + mask - 1], recv_sem=recv_sems.at[7 * phase + mask - 1],
                            device_id=(px, py, pc), device_id_type=MESH)
                sends.append(pltpu.make_async_remote_copy(src_ref=ref.at[me], dst_ref=ref.at[me], **both))
                arrivals.append(pltpu.make_async_remote_copy(src_ref=ref.at[me], dst_ref=ref.at[4 * px + 2 * py + pc],
                                                             **both))
            for cp in sends:
                cp.start()
            for cp in arrivals:
                cp.wait_recv()
            for cp in sends:
                cp.wait_send()

        v = c_ref[...]
        call_ref[me] = v * _sig(v)
        exchange(call_ref, 0)
        c_all = jnp.concatenate([call_ref[d] for d in range(N_DEV)], axis=0)
        mod_ref[me] = jnp.dot(c_all, w_ref[...], precision=HIGHEST, preferred_element_type=F32)
        exchange(mod_ref, 1)

    return pl.pallas_call(
        body, name="ada_mod",
        out_shape=[jax.ShapeDtypeStruct((N_DEV, 1, D_MODEL), F32), jax.ShapeDtypeStruct((N_DEV, N_DEV, cols), F32)],
        scratch_shapes=[pltpu.SemaphoreType.DMA((14,)), pltpu.SemaphoreType.DMA((14,))],
        compiler_params=pltpu.CompilerParams(vmem_limit_bytes=VMEM_LIMIT_BYTES),
    )(c, w_shard)


def _ada_grad(c_all, dmod_shard):
    def body(c_ref, d_ref, o_ref):
        o_ref[...] = lax.dot_general(c_ref[...], d_ref[...], _DIMS["tn"], precision=HIGHEST,
                                     preferred_element_type=F32)

    return pl.pallas_call(
        body, name="ada_grad", out_shape=jax.ShapeDtypeStruct((D_MODEL, dmod_shard.shape[1]), F32),
        compiler_params=pltpu.CompilerParams(vmem_limit_bytes=VMEM_LIMIT_BYTES),
    )(c_all, dmod_shard)


ROWS = 256


def _rms(v):
    r = lax.rsqrt(jnp.mean(v * v, axis=-1, keepdims=True) + EPS)
    return v * r, r


def _rms_bwd(dxn, xn, r):
    return r * (dxn - xn * jnp.mean(dxn * xn, axis=-1, keepdims=True))


def _colsum(v):
    return jnp.sum(v, axis=0, keepdims=True)


def _pre_mix(x, mod6, g1, comm=None):
    seq = x.shape[0]

    def body(ins, outs, scratch):
        x_ref, mod_ref, g_ref = ins
        xn, _ = _rms(x_ref[...])
        y = xn * g_ref[...]
        outs[0][...] = (y * (1.0 + mod_ref[SC_M:SC_M + 1, :]) + mod_ref[SH_M:SH_M + 1, :]).astype(BF16)

    (h,), extra = _host_call(
        body, "pre_mix", grid=(seq // ROWS,),
        in_specs=[_tile(ROWS, D_MODEL), _full((6, D_MODEL)), _full((1, D_MODEL))], out_specs=[_tile(ROWS, D_MODEL)],
        out_shape=[jax.ShapeDtypeStruct((seq, D_MODEL), BF16)], scratch_shapes=[], args=[x, mod6, g1], comm=comm)
    return h, extra


def _post_mix_pre_ffn(x, mod6, g2, g3, rows):
    seq = x.shape[0]

    def fn(y, first, ins, outs):
        x_ref, mod_ref, g2_ref, g3_ref = ins
        x1_ref, h_ref = outs
        yn, _ = _rms(y)
        x1 = x_ref[...] + mod_ref[GT_M:GT_M + 1, :] * (yn * g2_ref[...])
        x1_ref[...] = x1
        xn, _ = _rms(x1)
        y3 = xn * g3_ref[...]
        h_ref[...] = (y3 * (1.0 + mod_ref[SC_F:SC_F + 1, :]) + mod_ref[SH_F:SH_F + 1, :]).astype(BF16)

    return _Epilogue(
        [x, mod6, g2, g3], [_row_tile(rows, D_MODEL), _whole((6, D_MODEL)), _whole((1, D_MODEL)), _whole((1, D_MODEL))],
        [jax.ShapeDtypeStruct((seq, D_MODEL), F32), jax.ShapeDtypeStruct((seq, D_MODEL), BF16)],
        [_row_tile(rows, D_MODEL), _row_tile(rows, D_MODEL)], fn, keep_product=True)


def _final(x1, target, mod6, g4, rows):
    seq = x1.shape[0]

    def fn(y, first, ins, outs):
        x1_ref, t_ref, mod_ref, g_ref = ins
        loss_ref, dout_ref, dyf_ref, small_ref = outs

        @pl.when(first)
        def _():
            loss_ref[...] = jnp.zeros_like(loss_ref)
            small_ref[...] = jnp.zeros_like(small_ref)

        gt = mod_ref[GT_F:GT_F + 1, :]
        g4v = g_ref[...]
        yn, r = _rms(y)
        out = x1_ref[...] + gt * (yn * g4v)
        err = out - t_ref[...]
        loss_ref[...] += jnp.sum(jnp.mean(err * err, axis=-1, keepdims=True))
        dout = err * (1.0 / D_MODEL)
        dout_ref[...] = dout
        small_ref[0:1, :] += _colsum(dout * gt * yn)
        small_ref[1:2, :] += _colsum(dout * (yn * g4v))
        dyf_ref[...] = _rms_bwd(dout * gt * g4v, yn, r).astype(BF16)

    return _Epilogue(
        [x1, target, mod6, g4],
        [_row_tile(rows, D_MODEL), _row_tile(rows, D_MODEL), _whole((6, D_MODEL)), _whole((1, D_MODEL))],
        [jax.ShapeDtypeStruct((1, 128), F32), jax.ShapeDtypeStruct((seq, D_MODEL), F32),
         jax.ShapeDtypeStruct((seq, D_MODEL), BF16), jax.ShapeDtypeStruct((8, D_MODEL), F32)],
        [_whole((1, 128)), _row_tile(rows, D_MODEL), _row_tile(rows, D_MODEL), _whole((8, D_MODEL))],
        fn, keep_product=False)


def _mid_bwd(x1, dout, ymix, mod6, g3, g2, rows):
    seq = x1.shape[0]

    def fn(dh, first, ins, outs):
        x1_ref, dout_ref, y_ref, mod_ref, g3_ref, g2_ref = ins
        dx1_ref, dy_ref, small_ref = outs

        @pl.when(first)
        def _():
            small_ref[...] = jnp.zeros_like(small_ref)

        g3v, g2v = g3_ref[...], g2_ref[...]
        xn, r3 = _rms(x1_ref[...])
        y3 = xn * g3v
        dy3 = dh * (1.0 + mod_ref[SC_F:SC_F + 1, :])
        small_ref[0:1, :] += _colsum(dy3 * xn)
        small_ref[1:2, :] += _colsum(dh * y3)
        small_ref[2:3, :] += _colsum(dh)
        dx1 = dout_ref[...] + _rms_bwd(dy3 * g3v, xn, r3)
        dx1_ref[...] = dx1
        gt = mod_ref[GT_M:GT_M + 1, :]
        yn, r2 = _rms(y_ref[...])
        small_ref[3:4, :] += _colsum(dx1 * gt * yn)
        small_ref[4:5, :] += _colsum(dx1 * (yn * g2v))
        dy_ref[...] = _rms_bwd(dx1 * gt * g2v, yn, r2).astype(BF16)

    return _Epilogue(
        [x1, dout, ymix, mod6, g3, g2],
        [_row_tile(rows, D_MODEL)] * 3 + [_whole((6, D_MODEL)), _whole((1, D_MODEL)), _whole((1, D_MODEL))],
        [jax.ShapeDtypeStruct((seq, D_MODEL), F32), jax.ShapeDtypeStruct((seq, D_MODEL), BF16),
         jax.ShapeDtypeStruct((8, D_MODEL), F32)],
        [_row_tile(rows, D_MODEL), _row_tile(rows, D_MODEL), _whole((8, D_MODEL))], fn, keep_product=False)


def _pre_mix_bwd(x, dx1, mod6, g1, rows):
    seq = x.shape[0]

    def fn(dh, first, ins, outs):
        x_ref, dx1_ref, mod_ref, g_ref = ins
        dx_ref, small_ref = outs

        @pl.when(first)
        def _():
            small_ref[...] = jnp.zeros_like(small_ref)

        g1v = g_ref[...]
        xn, r = _rms(x_ref[...])
        dy = dh * (1.0 + mod_ref[SC_M:SC_M + 1, :])
        small_ref[0:1, :] += _colsum(dy * xn)
        small_ref[1:2, :] += _colsum(dh * (xn * g1v))
        small_ref[2:3, :] += _colsum(dh)
        dx_ref[...] = dx1_ref[...] + _rms_bwd(dy * g1v, xn, r)

    return _Epilogue(
        [x, dx1, mod6, g1],
        [_row_tile(rows, D_MODEL), _row_tile(rows, D_MODEL), _whole((6, D_MODEL)), _whole((1, D_MODEL))],
        [jax.ShapeDtypeStruct((seq, D_MODEL), F32), jax.ShapeDtypeStruct((8, D_MODEL), F32)],
        [_row_tile(rows, D_MODEL), _whole((8, D_MODEL))], fn, keep_product=False)


def _toeplitz_onehot(shape, offset_axis, top):
    m = lax.broadcasted_iota(jnp.int32, shape, offset_axis)
    i = lax.broadcasted_iota(jnp.int32, shape, 1 - offset_axis)
    return (i == jnp.clip(top - m, -MAX_REL, MAX_REL) + MAX_REL).astype(F32)


def _bias_table(rel_bias):
    width = GROUP_Q + GROUP_K

    def body(rb_ref, o_ref, t_ref):
        t_ref[...] = jnp.dot(rb_ref[...], _toeplitz_onehot((N_REL, width), 1, GROUP_K - 1), precision=HIGHEST,
                             preferred_element_type=F32)
        lane = lax.broadcasted_iota(jnp.int32, (N_HEADS, GROUP_K), 1)
        for r in range(GROUP_Q):
            first_key = (r // CHUNK) * CHUNK
            band = jnp.logical_and(lane >= first_key, lane < first_key + BAND)
            o_ref[r] = jnp.where(band, t_ref[:, GROUP_Q - 1 - r:GROUP_Q - 1 - r + GROUP_K], NEG_INF)

    return pl.pallas_call(
        body, name="bias_table", out_shape=jax.ShapeDtypeStruct((GROUP_Q, N_HEADS, GROUP_K), F32),
        scratch_shapes=[pltpu.VMEM((N_HEADS, width), F32)],
    )(rel_bias)


def _bias_grad(dbias_q):
    def body(d_ref, o_ref, t_ref):
        t_ref[...] = jnp.zeros_like(t_ref)
        for qi in range(CHUNK):
            t_ref[:, CHUNK - 1 - qi:CHUNK - 1 - qi + BAND] += d_ref[qi]
        o_ref[...] = jnp.dot(t_ref[...], _toeplitz_onehot((TOEPLITZ, N_REL), 0, BAND - 1), precision=HIGHEST,
                             preferred_element_type=F32)

    return pl.pallas_call(
        body, name="bias_grad", out_shape=jax.ShapeDtypeStruct((N_HEADS, N_REL), F32),
        scratch_shapes=[pltpu.VMEM((N_HEADS, TOEPLITZ), F32)],
    )(dbias_q)


def _resident_copies(qkv_hbm, t_hbm, k_ref, v_ref, t_ref, sems):
    inside = pl.ds(PAD_ROWS, qkv_hbm.shape[0])
    return (pltpu.make_async_copy(qkv_hbm.at[:, pl.ds(D_ATTN, D_ATTN)], k_ref.at[inside, :], sems.at[0]),
            pltpu.make_async_copy(qkv_hbm.at[:, pl.ds(2 * D_ATTN, D_ATTN)], v_ref.at[inside, :], sems.at[1]),
            pltpu.make_async_copy(t_hbm, t_ref, sems.at[2]))


def _start_resident(copies, k_ref, v_ref):
    k_ref[0:PAD_ROWS, :] = jnp.zeros((PAD_ROWS, D_ATTN), BF16)
    v_ref[0:PAD_ROWS, :] = jnp.zeros((PAD_ROWS, D_ATTN), BF16)
    for cp in copies:
        cp.start()


def _softmax_rows(s_ref, t_ref, h, before_start, rows):
    s = s_ref[rows, :] * (HEAD_DIM ** -0.5) + t_ref[h, rows, :] + before_start
    e = jnp.exp(s - jnp.max(s, axis=-1, keepdims=True))
    return e / jnp.sum(e, axis=-1, keepdims=True)


def _before_start(g):
    kj = lax.broadcasted_iota(jnp.int32, (8, GROUP_K), 1)
    return jnp.where(kj >= PAD_ROWS - g * GROUP_Q, 0.0, NEG_INF)


def _attn_fwd(qkv, table, comm=None):
    seq = qkv.shape[0]

    def body(ins, outs, scratch):
        q_ref, qkv_hbm, t_hbm = ins
        (o_ref,) = outs
        k_ref, v_ref, t_ref, s_ref, p_ref, sems = scratch
        g = pl.program_id(0)
        load_k, load_v, load_t = _resident_copies(qkv_hbm, t_hbm, k_ref, v_ref, t_ref, sems)

        @pl.when(g == 0)
        def _():
            _start_resident((load_k, load_v, load_t), k_ref, v_ref)
            load_k.wait()

        window = pl.ds(pl.multiple_of(g * GROUP_Q, GROUP_Q), GROUP_K)
        before_start = _before_start(g)
        for h in range(N_HEADS):
            cols = slice(h * HEAD_DIM, (h + 1) * HEAD_DIM)
            buf = h % 2
            s_ref[buf] = lax.dot_general(q_ref[:, cols], k_ref[window, cols], _DIMS["nt"],
                                         preferred_element_type=F32)
            if h == 0:
                pl.when(g == 0)(load_t.wait)
            for row in range(0, GROUP_Q, SOFTMAX_ROWS):
                halves = [_softmax_rows(s_ref.at[buf], t_ref, h, before_start, slice(r, r + 8))
                          for r in (row, row + 8)]
                p_ref[buf, row:row + SOFTMAX_ROWS, :] = jnp.concatenate(halves, axis=0).astype(BF16)
            if h == 0:
                pl.when(g == 0)(load_v.wait)
            o_ref[:, cols] = jnp.dot(p_ref[buf], v_ref[window, cols], preferred_element_type=F32).astype(BF16)

    (ao,), extra = _host_call(
        body, "attn_fwd", grid=(seq // GROUP_Q,),
        in_specs=[_tile(GROUP_Q, D_ATTN), ANY, ANY], out_specs=[_tile(GROUP_Q, D_ATTN)],
        out_shape=[jax.ShapeDtypeStruct((seq, D_ATTN), BF16)],
        scratch_shapes=[pltpu.VMEM((seq + PAD_ROWS, D_ATTN), BF16), pltpu.VMEM((seq + PAD_ROWS, D_ATTN), BF16),
                        pltpu.VMEM(table.shape, F32),
                        pltpu.VMEM((2, GROUP_Q, GROUP_K), F32), pltpu.VMEM((2, GROUP_Q, GROUP_K), BF16),
                        pltpu.SemaphoreType.DMA((3,))],
        args=[qkv, qkv, table], comm=comm)
    return ao, extra


def _attn_bwd(qkv, table, dao, comm=None):
    seq = qkv.shape[0]
    n_groups = seq // GROUP_Q
    fold_w = GROUP_K + (GROUP - 1) * CHUNK

    def body(ins, outs, scratch):
        q_ref, do_ref, qkv_hbm, t_hbm = ins
        dq_ref, dkt_hbm, dvt_hbm, db_ref, cs_ref = outs
        k_ref, v_ref, t_ref, db_acc, dkt_acc, dvt_acc, s_ref, dp_ref, p_ref, ds_ref, sems = scratch
        g = pl.program_id(0)

        load_k, load_v, load_t = _resident_copies(qkv_hbm, t_hbm, k_ref, v_ref, t_ref, sems)

        @pl.when(g == 0)
        def _():
            _start_resident((load_k, load_v, load_t), k_ref, v_ref)
            db_acc[...] = jnp.zeros_like(db_acc)
            dkt_acc[...] = jnp.zeros_like(dkt_acc)
            dvt_acc[...] = jnp.zeros_like(dvt_acc)
            cs_ref[...] = jnp.zeros_like(cs_ref)
            load_k.wait()
            load_v.wait()

        window = pl.ds(pl.multiple_of(g * GROUP_Q, GROUP_Q), GROUP_K)
        before_start = _before_start(g)
        for h in range(N_HEADS):
            cols = slice(h * HEAD_DIM, (h + 1) * HEAD_DIM)
            buf = h % 2
            qh, doh = q_ref[:, cols], do_ref[:, cols]
            kh, vh = k_ref[window, cols], v_ref[window, cols]
            s_ref[buf] = lax.dot_general(qh, kh, _DIMS["nt"], preferred_element_type=F32)
            dp_ref[buf] = lax.dot_general(doh, vh, _DIMS["nt"], preferred_element_type=F32)
            if h == 0:
                pl.when(g == 0)(load_t.wait)
            for row in range(0, GROUP_Q, SOFTMAX_ROWS):
                p_halves, ds_halves = [], []
                for r in (row, row + 8):
                    p = _softmax_rows(s_ref.at[buf], t_ref, h, before_start, slice(r, r + 8))
                    dp = dp_ref[buf, r:r + 8, :]
                    ds = p * (dp - jnp.sum(dp * p, axis=-1, keepdims=True))
                    chunk = r // CHUNK
                    shift = (GROUP - 1 - chunk) * CHUNK
                    db_acc[h, r - chunk * CHUNK:r - chunk * CHUNK + 8, shift:shift + GROUP_K] += ds
                    p_halves.append(p)
                    ds_halves.append(ds * (HEAD_DIM ** -0.5))
                p_ref[buf, row:row + SOFTMAX_ROWS, :] = jnp.concatenate(p_halves, axis=0).astype(BF16)
                ds_ref[buf, row:row + SOFTMAX_ROWS, :] = jnp.concatenate(ds_halves, axis=0).astype(BF16)
            dq_ref[:, cols] = jnp.dot(ds_ref[buf], kh, preferred_element_type=F32).astype(BF16)
            dkt_acc[cols, window] += lax.dot_general(qh, ds_ref[buf], _DIMS["tn"], preferred_element_type=F32)
            dvt_acc[cols, window] += lax.dot_general(doh, p_ref[buf], _DIMS["tn"], preferred_element_type=F32)
        cs_ref[0:1, :] += _colsum(dq_ref[...].astype(F32))

        @pl.when(g == n_groups - 1)
        def _():
            lo = (GROUP - 1) * CHUNK
            for h in range(N_HEADS):
                db_ref[h] = db_acc[h, :, lo:lo + BAND]
            inside = pl.ds(PAD_ROWS, seq)
            on_diagonal = (lax.broadcasted_iota(jnp.int32, (D_ATTN, D_ATTN), 0)
                           == lax.broadcasted_iota(jnp.int32, (D_ATTN, D_ATTN), 1))
            for row, acc in ((1, dkt_acc), (2, dvt_acc)):
                column = jnp.sum(acc[:, inside], axis=1, keepdims=True)
                cs_ref[row:row + 1, :] = _colsum(jnp.where(on_diagonal, column, 0.0))
            out_k = pltpu.make_async_copy(dkt_acc.at[:, inside], dkt_hbm, sems.at[0])
            out_v = pltpu.make_async_copy(dvt_acc.at[:, inside], dvt_hbm, sems.at[1])
            out_k.start()
            out_v.start()
            out_k.wait()
            out_v.wait()

    t_shape = (D_ATTN, seq + PAD_ROWS)
    outs, extra = _host_call(
        body, "attn_bwd", grid=(n_groups,),
        in_specs=[_tile(GROUP_Q, D_ATTN), _tile(GROUP_Q, D_ATTN), ANY, ANY],
        out_specs=[_tile(GROUP_Q, D_ATTN), ANY, ANY, _full((N_HEADS, CHUNK, BAND)), _full((8, D_ATTN))],
        out_shape=[jax.ShapeDtypeStruct((seq, D_ATTN), BF16), jax.ShapeDtypeStruct((D_ATTN, seq), F32),
                   jax.ShapeDtypeStruct((D_ATTN, seq), F32), jax.ShapeDtypeStruct((N_HEADS, CHUNK, BAND), F32),
                   jax.ShapeDtypeStruct((8, D_ATTN), F32)],
        scratch_shapes=[pltpu.VMEM((seq + PAD_ROWS, D_ATTN), BF16), pltpu.VMEM((seq + PAD_ROWS, D_ATTN), BF16),
                        pltpu.VMEM(table.shape, F32), pltpu.VMEM((N_HEADS, CHUNK, fold_w), F32), pltpu.VMEM(t_shape, F32),
                        pltpu.VMEM(t_shape, F32), pltpu.VMEM((2, GROUP_Q, GROUP_K), F32),
                        pltpu.VMEM((2, GROUP_Q, GROUP_K), F32), pltpu.VMEM((2, GROUP_Q, GROUP_K), BF16),
                        pltpu.VMEM((2, GROUP_Q, GROUP_K), BF16), pltpu.SemaphoreType.DMA((3,))],
        args=[qkv, dao, qkv, table], comm=comm)
    return outs, extra


def _assemble_dz(dq, dkt, dvt, dglu_a, dglu_b, dga, dgb):
    seq = dq.shape[0]
    rows = 512
    transposed = pl.BlockSpec((D_ATTN, rows), lambda i: (0, i))

    def body(dq_ref, dkt_ref, dvt_ref, da_ref, db_ref, dga_ref, dgb_ref, o_ref):
        o_ref[:, 0:D_ATTN] = dq_ref[...]
        o_ref[:, D_ATTN:2 * D_ATTN] = dkt_ref[...].T.astype(BF16)
        o_ref[:, 2 * D_ATTN:3 * D_ATTN] = dvt_ref[...].T.astype(BF16)
        off = 3 * D_ATTN
        for ref in (da_ref, db_ref, dga_ref, dgb_ref):
            width = ref.shape[1]
            o_ref[:, off:off + width] = ref[...]
            off += width

    width = 3 * D_ATTN + 2 * D_CONV + 2 * D_MODEL
    return pl.pallas_call(
        body, name="assemble_dz", out_shape=jax.ShapeDtypeStruct((seq, width), BF16), grid=(seq // rows,),
        in_specs=[_tile(rows, D_ATTN), transposed, transposed, _tile(rows, D_CONV), _tile(rows, D_CONV),
                  _tile(rows, D_MODEL), _tile(rows, D_MODEL)],
        out_specs=_tile(rows, width), compiler_params=_cparams(1),
    )(dq, dkt, dvt, dglu_a, dglu_b, dga, dgb)


CONV_ROWS = 256


def _ln_silu(u1, g, b):
    mu = jnp.mean(u1, axis=-1, keepdims=True)
    xc = u1 - mu
    rs = lax.rsqrt(jnp.mean(xc * xc, axis=-1, keepdims=True) + EPS)
    xhat = xc * rs
    u2 = xhat * g + b
    return xhat, rs, u2


def _glu_into(s_ref, a_ref, b_ref, ah_ref, bh_ref, first):
    halo = ah_ref[...] * _sig(bh_ref[...])
    s_ref[0:CONV_HALO, :] = jnp.where(first, 0.0, halo)
    s_ref[CONV_HALO:CONV_HALO + CONV_ROWS, :] = a_ref[...] * _sig(b_ref[...])


CONV_LANES = 128
CONV_TILES = CONV_ROWS // 8


def _lag_weights(w_ref, lanes):
    return {e: jnp.broadcast_to(w_ref[CONV_K - 1 - e:CONV_K - e, lanes], (8, CONV_LANES)) for e in range(CONV_K)}


def _class_sums(w, tiles, k):
    total = None
    for a, tile in enumerate(tiles):
        if 8 * a + k < CONV_K:
            term = w[8 * a + k] * tile
            total = term if total is None else total + term
    return total


def _conv_back(src_ref, first_tile, w, lanes, row_id, emit):
    before = None
    for m in range(-1, CONV_TILES):
        tiles = [src_ref[8 * (first_tile + m - a):8 * (first_tile + m - a) + 8, lanes] for a in range(4)]
        rolled = [None] + [pltpu.roll(_class_sums(w, tiles, k), k, 0) for k in range(1, 8)]
        if m >= 0:
            out = _class_sums(w, tiles, 0)
            for k in range(1, 8):
                out = out + jnp.where(row_id < k, before[k], rolled[k])
            emit(m, out)
        before = rolled


def _conv_ahead(src_ref, w, lanes, row_id, emit):
    before = None
    for m in range(CONV_TILES + 1):
        tiles = [src_ref[8 * (m + a):8 * (m + a) + 8, lanes] for a in range(4)]
        rolled = [None] + [pltpu.roll(_class_sums(w, tiles, k), 8 - k, 0) for k in range(1, 8)]
        if m >= 1:
            out = before[0]
            for k in range(1, 8):
                out = out + jnp.where(row_id < 8 - k, before[k], rolled[k])
            emit(m - 1, out)
        before = [_class_sums(w, tiles, 0) if m < CONV_TILES else None] + rolled[1:]


def _conv_weight_sums(d_ref, s_ref, lanes, row_id, whole_shifts):
    zero = jnp.zeros((8, CONV_LANES), F32)
    sums = {8 * a + k: zero for a in whole_shifts for k in range(8) if 8 * a + k < CONV_K}

    def d_tile(m):
        return d_ref[8 * m:8 * m + 8, lanes] if 0 <= m < CONV_TILES else zero

    rolled = [None] + [zero] * 7
    for m in range(-1, CONV_TILES):
        cur, nxt = d_tile(m), d_tile(m + 1)
        rolled_next = [None] + [pltpu.roll(nxt, 8 - k, 0) for k in range(1, 8)]
        shifted = [cur] + [jnp.where(row_id < 8 - k, rolled[k], rolled_next[k]) for k in range(1, 8)]
        for a in whole_shifts:
            tile = s_ref[8 * (CONV_HALO // 8 + m - a):8 * (CONV_HALO // 8 + m - a) + 8, lanes]
            for k in range(8):
                if 8 * a + k < CONV_K and not (m < 0 and k == 0):
                    sums[8 * a + k] = sums[8 * a + k] + shifted[k] * tile
        rolled = rolled_next
    return sums


def _conv_fwd(zr, w_dw, b_dw, g_ln, b_ln, comm=None):
    seq = zr.shape[0]

    def body(a_ref, b_ref, ah_ref, bh_ref, w_ref, bias_ref, g_ref, bl_ref, u1_ref, u3_ref, s_ref):
        _glu_into(s_ref, a_ref, b_ref, ah_ref, bh_ref, pl.program_id(0) == 0)
        row_id = lax.broadcasted_iota(jnp.int32, (8, CONV_LANES), 0)
        for lo in range(0, D_CONV, CONV_LANES):
            lanes = slice(lo, lo + CONV_LANES)
            bias = jnp.broadcast_to(bias_ref[:, lanes], (8, CONV_LANES))

            def emit(m, out, lanes=lanes, bias=bias):
                u1_ref[8 * m:8 * m + 8, lanes] = out + bias

            _conv_back(s_ref, CONV_HALO // 8, _lag_weights(w_ref, lanes), lanes, row_id, emit)
        _, _, u2 = _ln_silu(u1_ref[...], g_ref[...], bl_ref[...])
        u3_ref[...] = (u2 * _sig(u2)).astype(BF16)

    return _host_call(
        lambda ins, outs, scratch: body(*ins, *outs, *scratch), "conv_fwd", grid=(seq // CONV_ROWS,),
        in_specs=[_tile(CONV_ROWS, D_CONV, 0), _tile(CONV_ROWS, D_CONV, 1),
                  _prev(CONV_HALO, D_CONV, CONV_ROWS, 0), _prev(CONV_HALO, D_CONV, CONV_ROWS, 1),
                  _full((CONV_K, D_CONV)), _full((1, D_CONV)), _full((1, D_CONV)), _full((1, D_CONV))],
        out_specs=[_tile(CONV_ROWS, D_CONV), _tile(CONV_ROWS, D_CONV)],
        out_shape=[jax.ShapeDtypeStruct((seq, D_CONV), F32), jax.ShapeDtypeStruct((seq, D_CONV), BF16)],
        scratch_shapes=[pltpu.VMEM((CONV_HALO + CONV_ROWS, D_CONV), F32)],
        args=[zr, zr, zr, zr, w_dw, b_dw, g_ln, b_ln], comm=comm)


def _conv_bwd(zr, u1, du3, w_dw, g_ln, b_ln, comm=None):
    seq = zr.shape[0]
    n_tiles = seq // CONV_ROWS
    n_halo = seq // CONV_HALO
    ext = CONV_ROWS + CONV_HALO

    def body(a_ref, b_ref, ah_ref, bh_ref, u1_ref, u1n_ref, d3_ref, d3n_ref, w_ref, g_ref, bl_ref,
             da_ref, db_ref, dw_ref, small_ref, s_ref, d_ref, du0_ref):
        i = pl.program_id(0)

        @pl.when(i == 0)
        def _():
            dw_ref[...] = jnp.zeros_like(dw_ref)
            small_ref[...] = jnp.zeros_like(small_ref)

        _glu_into(s_ref, a_ref, b_ref, ah_ref, bh_ref, i == 0)
        gv, bv = g_ref[...], bl_ref[...]

        def du1_of(u1, d3):
            xhat, rs, u2 = _ln_silu(u1, gv, bv)
            sg = _sig(u2)
            du2 = d3 * (sg * (1.0 + u2 * (1.0 - sg)))
            dxh = du2 * gv
            du1 = rs * (dxh - jnp.mean(dxh, axis=-1, keepdims=True)
                        - xhat * jnp.mean(dxh * xhat, axis=-1, keepdims=True))
            return du1, du2, xhat

        du1, du2, xhat = du1_of(u1_ref[...], d3_ref[...])
        du1n, _, _ = du1_of(u1n_ref[...], d3n_ref[...])
        d_ref[0:CONV_ROWS, :] = du1
        d_ref[CONV_ROWS:ext, :] = jnp.where(i == n_tiles - 1, 0.0, du1n)
        small_ref[0:1, :] += _colsum(du1)
        small_ref[1:2, :] += _colsum(du2 * xhat)
        small_ref[2:3, :] += _colsum(du2)
        row_id = lax.broadcasted_iota(jnp.int32, (8, CONV_LANES), 0)
        for lo in range(0, D_CONV, CONV_LANES):
            lanes = slice(lo, lo + CONV_LANES)

            def emit(m, out, lanes=lanes):
                du0_ref[8 * m:8 * m + 8, lanes] = out

            _conv_ahead(d_ref, _lag_weights(w_ref, lanes), lanes, row_id, emit)
            for whole_shifts in ((0, 1), (2, 3)):
                for e, total in _conv_weight_sums(d_ref, s_ref, lanes, row_id, whole_shifts).items():
                    dw_ref[CONV_K - 1 - e:CONV_K - e, lanes] += _colsum(total)
        du0 = du0_ref[...]
        sb = _sig(b_ref[...])
        da = du0 * sb
        dbv = du0 * a_ref[...] * sb * (1.0 - sb)
        da_ref[...] = da.astype(BF16)
        db_ref[...] = dbv.astype(BF16)
        small_ref[3:4, :] += _colsum(da)
        small_ref[4:5, :] += _colsum(dbv)

    return _host_call(
        lambda ins, outs, scratch: body(*ins, *outs, *scratch), "conv_bwd", grid=(n_tiles,),
        in_specs=[_tile(CONV_ROWS, D_CONV, 0), _tile(CONV_ROWS, D_CONV, 1),
                  _prev(CONV_HALO, D_CONV, CONV_ROWS, 0), _prev(CONV_HALO, D_CONV, CONV_ROWS, 1),
                  _tile(CONV_ROWS, D_CONV), _next(CONV_HALO, D_CONV, CONV_ROWS, n_halo),
                  _tile(CONV_ROWS, D_CONV), _next(CONV_HALO, D_CONV, CONV_ROWS, n_halo),
                  _full((CONV_K, D_CONV)), _full((1, D_CONV)), _full((1, D_CONV))],
        out_specs=[_tile(CONV_ROWS, D_CONV), _tile(CONV_ROWS, D_CONV), _full((CONV_HALO, D_CONV)),
                   _full((8, D_CONV))],
        out_shape=[jax.ShapeDtypeStruct((seq, D_CONV), BF16), jax.ShapeDtypeStruct((seq, D_CONV), BF16),
                   jax.ShapeDtypeStruct((CONV_HALO, D_CONV), F32), jax.ShapeDtypeStruct((8, D_CONV), F32)],
        scratch_shapes=[pltpu.VMEM((ext, D_CONV), F32), pltpu.VMEM((ext, D_CONV), F32),
                        pltpu.VMEM((CONV_ROWS, D_CONV), F32)],
        args=[zr, zr, zr, zr, u1, u1, du3, du3, w_dw, g_ln, b_ln], comm=comm)


MERGE_ROWS = 256


def _merge_fwd(ao, u3, zr, w_ao, w_co, b_co):
    seq = ao.shape[0]

    def body(ao_ref, u3_ref, ga_ref, gb_ref, wa_ref, wc_ref, bc_ref, y_ref, a_ref, cb_ref):
        a = jnp.dot(ao_ref[...], wa_ref[...], preferred_element_type=F32)
        cb = jnp.dot(u3_ref[...], wc_ref[...], preferred_element_type=F32) + bc_ref[...]
        a_ref[...] = a
        cb_ref[...] = cb
        y_ref[...] = (_sig(ga_ref[...]) * a + _sig(gb_ref[...]) * cb).astype(BF16)

    f32_out = jax.ShapeDtypeStruct((seq, D_MODEL), F32)
    return pl.pallas_call(
        body, name="merge_fwd",
        out_shape=[jax.ShapeDtypeStruct((seq, D_MODEL), BF16), f32_out, f32_out],
        grid=(seq // MERGE_ROWS,),
        in_specs=[_tile(MERGE_ROWS, D_ATTN), _tile(MERGE_ROWS, D_CONV), _tile(MERGE_ROWS, D_MODEL, 1),
                  _tile(MERGE_ROWS, D_MODEL, 2), _full(w_ao.shape), _full(w_co.shape), _full((1, D_MODEL))],
        out_specs=[_tile(MERGE_ROWS, D_MODEL)] * 3, compiler_params=_cparams(1),
    )(ao, u3, zr, zr, w_ao, w_co, b_co)


def _merge_bwd(a, cb, zr, rows):
    seq = a.shape[0]

    def fn(dy_v, first, ins, outs):
        a_ref, cb_ref, ga_ref, gb_ref = ins
        da_ref, dcb_ref, dga_ref, dgb_ref, small_ref = outs

        @pl.when(first)
        def _():
            small_ref[...] = jnp.zeros_like(small_ref)

        sa, sb = _sig(ga_ref[...]), _sig(gb_ref[...])
        dcb = dy_v * sb
        dga = dy_v * a_ref[...] * sa * (1.0 - sa)
        dgb = dy_v * cb_ref[...] * sb * (1.0 - sb)
        da_ref[...] = (dy_v * sa).astype(BF16)
        dcb_ref[...] = dcb.astype(BF16)
        dga_ref[...] = dga.astype(BF16)
        dgb_ref[...] = dgb.astype(BF16)
        small_ref[0:1, :] += _colsum(dga)
        small_ref[1:2, :] += _colsum(dgb)
        small_ref[2:3, :] += _colsum(dcb)

    bf = jax.ShapeDtypeStruct((seq, D_MODEL), BF16)
    gate = lambda col: pl.BlockSpec((rows, D_MODEL), lambda i, j: (i, col))
    return _Epilogue(
        [a, cb, zr, zr], [_row_tile(rows, D_MODEL), _row_tile(rows, D_MODEL), gate(1), gate(2)],
        [bf, bf, bf, bf, jax.ShapeDtypeStruct((8, D_MODEL), F32)],
        [_row_tile(rows, D_MODEL)] * 4 + [_whole((8, D_MODEL))], fn, keep_product=False)


FFN_ROWS = 2048
FFN_BLOCKS = D_FF // FFN_COLS
GELU_C = math.sqrt(2.0 / math.pi)


def _gelu(v):
    t = jnp.tanh(GELU_C * (v + 0.044715 * (v * v * v)))
    return 0.5 * v * (1.0 + t), t


def _gelu_grad(v, t):
    return 0.5 * (1.0 + t) + 0.5 * v * (1.0 - t * t) * (GELU_C * (1.0 + 3.0 * 0.044715 * (v * v)))


def _sublane_rows(ref, n):
    return [jnp.broadcast_to(ref[r:r + 1, :], (8, FFN_COLS)) for r in range(n)]


def _rolls(tile, shifts):
    return tuple(pltpu.roll(tile, s, 0) for s in shifts)


def _behind(prev_rolls, cur, row_id):
    rolls = _rolls(cur, (1, 2))
    x1 = jnp.where(row_id < 1, prev_rolls[0], rolls[0])
    x2 = jnp.where(row_id < 2, prev_rolls[1], rolls[1])
    return (x2, x1, cur), rolls


def _ahead(cur_rolls, next_rolls, row_id):
    return (jnp.where(row_id < 7, cur_rolls[0], next_rolls[0]), jnp.where(row_id < 6, cur_rolls[1], next_rolls[1]))


def _conv3(taps, w, bias):
    return w[0] * taps[0] + w[1] * taps[1] + w[2] * taps[2] + bias


def _ffn_specs(rows):
    tile = lambda off: pl.BlockSpec((rows, FFN_COLS), lambda j, i: (i, j + off))
    prev = lambda off: pl.BlockSpec((FFN_HALO, FFN_COLS),
                                    lambda j, i: (jnp.maximum(i * (rows // FFN_HALO) - 1, 0), j + off))
    wgt = lambda off: pl.BlockSpec((3, FFN_COLS), lambda j, i: (0, j + off))
    vec = lambda off: pl.BlockSpec((1, FFN_COLS), lambda j, i: (0, j + off))
    return tile, prev, wgt, vec


def _ffn_act(up, w_dw, b_dw):
    seq = up.shape[0]
    tile, prev, wgt, vec = _ffn_specs(FFN_ROWS)

    def body(v_ref, g_ref, vp_ref, gp_ref, wv_ref, wg_ref, bv_ref, bg_ref, act_ref):
        first = pl.program_id(1) == 0
        row_id = lax.broadcasted_iota(jnp.int32, (8, FFN_COLS), 0)
        wv, wg = _sublane_rows(wv_ref, 3), _sublane_rows(wg_ref, 3)
        (bv,), (bg,) = _sublane_rows(bv_ref, 1), _sublane_rows(bg_ref, 1)
        rolls_v = _rolls(jnp.where(first, 0.0, vp_ref[...]), (1, 2))
        rolls_g = _rolls(jnp.where(first, 0.0, gp_ref[...]), (1, 2))
        for row in range(0, FFN_ROWS, 16):
            halves = []
            for r in (row, row + 8):
                taps_v, rolls_v = _behind(rolls_v, v_ref[r:r + 8, :], row_id)
                taps_g, rolls_g = _behind(rolls_g, g_ref[r:r + 8, :], row_id)
                halves.append(_gelu(_conv3(taps_g, wg, bg))[0] * _conv3(taps_v, wv, bv))
            act_ref[row:row + 16, :] = jnp.concatenate(halves, axis=0).astype(BF16)

    return pl.pallas_call(
        body, name="ffn_act", out_shape=jax.ShapeDtypeStruct((seq, D_FF), BF16),
        grid=(FFN_BLOCKS, seq // FFN_ROWS),
        in_specs=[tile(0), tile(FFN_BLOCKS), prev(0), prev(FFN_BLOCKS), wgt(0), wgt(FFN_BLOCKS),
                  vec(0), vec(FFN_BLOCKS)],
        out_specs=tile(0), compiler_params=_cparams(2),
    )(up, up, up, up, w_dw, w_dw, b_dw, b_dw)


def _ffn_act_bwd(up, dact, w_dw, b_dw, comm=None):
    seq = up.shape[0]
    n_tiles = seq // FFN_ROWS
    n_halo = seq // FFN_HALO
    tile, prev, wgt, vec = _ffn_specs(FFN_ROWS)
    nxt = lambda off: pl.BlockSpec(
        (FFN_HALO, FFN_COLS), lambda j, i: (jnp.minimum((i + 1) * (FFN_ROWS // FFN_HALO), n_halo - 1), j + off))
    acc = lambda off: pl.BlockSpec((8, FFN_COLS), lambda j, i: (0, j + off))

    def body(v_ref, g_ref, vp_ref, gp_ref, vn_ref, gn_ref, da_ref, dan_ref, wv_ref, wg_ref, bv_ref, bg_ref,
             dv_out, dg_out, dwv_ref, dwg_ref, dbv_ref, dbg_ref):
        i = pl.program_id(1)
        first, last = i == 0, i == n_tiles - 1

        @pl.when(first)
        def _():
            for r in (dwv_ref, dwg_ref, dbv_ref, dbg_ref):
                r[...] = jnp.zeros_like(r)

        row_id = lax.broadcasted_iota(jnp.int32, (8, FFN_COLS), 0)
        wv, wg = _sublane_rows(wv_ref, 3), _sublane_rows(wg_ref, 3)
        (bv,), (bg,) = _sublane_rows(bv_ref, 1), _sublane_rows(bg_ref, 1)
        zero = jnp.zeros((8, FFN_COLS), F32)
        sums_v, sums_g = [zero] * 4, [zero] * 4
        rolls_v = _rolls(jnp.where(first, 0.0, vp_ref[...]), (1, 2))
        rolls_g = _rolls(jnp.where(first, 0.0, gp_ref[...]), (1, 2))
        behind = None
        done_v, done_g = [], []

        def grads(v_tile, g_tile, dact, rolls_v, rolls_g):
            taps_v, rolls_v = _behind(rolls_v, v_tile, row_id)
            taps_g, rolls_g = _behind(rolls_g, g_tile, row_id)
            val, gate = _conv3(taps_v, wv, bv), _conv3(taps_g, wg, bg)
            gel, t = _gelu(gate)
            return dact * gel, dact * val * _gelu_grad(gate, t), taps_v, taps_g, rolls_v, rolls_g

        def finish(tile, nxt, row):
            for (d, d_rolls), (_, n_rolls), w, done, o_ref in ((tile[0], nxt[0], wv, done_v, dv_out),
                                                               (tile[1], nxt[1], wg, done_g, dg_out)):
                d1, d2 = _ahead(d_rolls, n_rolls, row_id)
                done.append(w[2] * d + w[1] * d1 + w[0] * d2)
                if len(done) == 2:
                    o_ref[row - 16:row, :] = jnp.concatenate(done, axis=0).astype(BF16)
                    done.clear()

        for row in range(0, FFN_ROWS, 16):
            dact16 = da_ref[row:row + 16, :].astype(F32)
            for r, dact in ((row, dact16[0:8, :]), (row + 8, dact16[8:16, :])):
                dval, dgate, taps_v, taps_g, rolls_v, rolls_g = grads(v_ref[r:r + 8, :], g_ref[r:r + 8, :], dact,
                                                                      rolls_v, rolls_g)
                sums_v = [s + dval * x for s, x in zip(sums_v, taps_v)] + [sums_v[3] + dval]
                sums_g = [s + dgate * x for s, x in zip(sums_g, taps_g)] + [sums_g[3] + dgate]
                tile = ((dval, _rolls(dval, (7, 6))), (dgate, _rolls(dgate, (7, 6))))
                if behind is not None:
                    finish(behind, tile, r)
                behind = tile
        dact_next = jnp.where(last, 0.0, dan_ref[...].astype(F32)[0:FFN_HALO, :])
        dval, dgate, *_ = grads(vn_ref[...], gn_ref[...], dact_next, rolls_v, rolls_g)
        finish(behind, ((dval, _rolls(dval, (7, 6))), (dgate, _rolls(dgate, (7, 6)))), FFN_ROWS)
        for sums, dw_ref, db_ref in ((sums_v, dwv_ref, dbv_ref), (sums_g, dwg_ref, dbg_ref)):
            for tap in range(3):
                dw_ref[tap:tap + 1, :] += _colsum(sums[tap])
            db_ref[0:1, :] += _colsum(sums[3])

    half = jax.ShapeDtypeStruct((seq, D_FF), BF16)
    acc_shape = jax.ShapeDtypeStruct((8, D_FF), F32)
    return _host_call(
        lambda ins, outs, scratch: body(*ins, *outs, *scratch), "ffn_act_bwd", grid=(FFN_BLOCKS, n_tiles),
        in_specs=[tile(0), tile(FFN_BLOCKS), prev(0), prev(FFN_BLOCKS), nxt(0), nxt(FFN_BLOCKS),
                  tile(0), pl.BlockSpec((16, FFN_COLS), lambda j, i: (
                      jnp.minimum((i + 1) * (FFN_ROWS // 16), seq // 16 - 1), j)),
                  wgt(0), wgt(FFN_BLOCKS), vec(0), vec(FFN_BLOCKS)],
        out_specs=[tile(0), tile(0), acc(0), acc(0), acc(0), acc(0)],
        out_shape=[half, half, acc_shape, acc_shape, acc_shape, acc_shape],
        scratch_shapes=[], args=[up, up, up, up, up, up, dact, dact, w_dw, w_dw, b_dw, b_dw], comm=comm)


def _cols_to_blocks(full_cols):
    k, n8 = full_cols.shape
    return jnp.transpose(full_cols.reshape(k, N_DEV, n8 // N_DEV), (1, 0, 2))


def _rows_to_blocks(full_rows):
    r8, n = full_rows.shape
    return full_rows.reshape(N_DEV, r8 // N_DEV, n)


def _blocks_to_cols(gathered):
    _, k, n = gathered.shape
    return jnp.transpose(gathered, (1, 0, 2)).reshape(k, N_DEV * n)


def kernel(x, c, w_ada, b_ada, g_pre_mix, g_post_mix, w_in, b_in, rel_bias, w_attn_o, w_dw_conv, b_dw_conv, g_conv_ln, b_conv_ln, w_conv_o, b_conv_o, w_mix_o, g_pre_ffn, g_post_ffn, w_up, w_dw_ffn, b_dw_ffn, w_down, loss_target, m_w_ada, m_b_ada, m_g_pre_mix, m_g_post_mix, m_w_in, m_b_in, m_rel_bias, m_w_attn_o, m_w_dw_conv, m_b_dw_conv, m_g_conv_ln, m_b_conv_ln, m_w_conv_o, m_b_conv_o, m_w_mix_o, m_g_pre_ffn, m_g_post_ffn, m_w_up, m_w_dw_ffn, m_b_dw_ffn, m_w_down, v_w_ada, v_b_ada, v_g_pre_mix, v_g_post_mix, v_w_in, v_b_in, v_rel_bias, v_w_attn_o, v_w_dw_conv, v_b_dw_conv, v_g_conv_ln, v_b_conv_ln, v_w_conv_o, v_b_conv_o, v_w_mix_o, v_g_pre_ffn, v_g_post_ffn, v_w_up, v_w_dw_ffn, v_b_dw_ffn, v_w_down):
    names = ["w_ada", "b_ada", "g_pre_mix", "g_post_mix", "w_in", "b_in", "rel_bias", "w_attn_o", "w_dw_conv",
             "b_dw_conv", "g_conv_ln", "b_conv_ln", "w_conv_o", "b_conv_o", "w_mix_o", "g_pre_ffn", "g_post_ffn",
             "w_up", "w_dw_ffn", "b_dw_ffn", "w_down"]
    weights = dict(zip(names, [w_ada, b_ada, g_pre_mix, g_post_mix, w_in, b_in, rel_bias, w_attn_o, w_dw_conv,
                               b_dw_conv, g_conv_ln, b_conv_ln, w_conv_o, b_conv_o, w_mix_o, g_pre_ffn,
                               g_post_ffn, w_up, w_dw_ffn, b_dw_ffn, w_down]))
    mom_m = dict(zip(names, [m_w_ada, m_b_ada, m_g_pre_mix, m_g_post_mix, m_w_in, m_b_in, m_rel_bias, m_w_attn_o,
                             m_w_dw_conv, m_b_dw_conv, m_g_conv_ln, m_b_conv_ln, m_w_conv_o, m_b_conv_o,
                             m_w_mix_o, m_g_pre_ffn, m_g_post_ffn, m_w_up, m_w_dw_ffn, m_b_dw_ffn, m_w_down]))
    mom_v = dict(zip(names, [v_w_ada, v_b_ada, v_g_pre_mix, v_g_post_mix, v_w_in, v_b_in, v_rel_bias, v_w_attn_o,
                             v_w_dw_conv, v_b_dw_conv, v_g_conv_ln, v_b_conv_ln, v_w_conv_o, v_b_conv_o,
                             v_w_mix_o, v_g_pre_ffn, v_g_post_ffn, v_w_up, v_w_dw_ffn, v_b_dw_ffn, v_w_down]))
    shapes = {n: w.shape for n, w in weights.items()}

    seq = x.shape[1]
    me = 4 * lax.axis_index("x") + 2 * lax.axis_index("y") + lax.axis_index("c")
    x2 = x.reshape(seq, D_MODEL)
    target = loss_target.reshape(seq, D_MODEL)
    sq = lambda a: a.reshape(a.shape[1:])
    bf = lambda a: sq(a).astype(BF16)

    transposed = lambda a: jnp.swapaxes(sq(a), 0, 1)

    c_all, mod_all = _ada_mod(c, sq(w_ada))
    c_all = c_all.reshape(N_DEV, D_MODEL)
    mod = lax.dynamic_index_in_dim(mod_all, me, axis=1, keepdims=False)
    mod6 = (mod.reshape(1, 6 * D_MODEL) + b_ada).reshape(6, D_MODEL)

    h1, (g_in, g_dwc, g_dwf) = _pre_mix(
        x2, mod6, g_pre_mix, comm=_gather_comm([transposed(w_in).astype(BF16), sq(w_dw_conv), sq(w_dw_ffn)]))
    wt_in = g_in.reshape(g_in.shape[0] * g_in.shape[1], D_MODEL)
    wf_dwc = _blocks_to_cols(g_dwc)
    wf_dwf = _blocks_to_cols(g_dwf)
    qkv = _mm(h1, wt_in, "nt", BF16, "in_proj_qkv", bias=b_in, tm=1024, tn=768, cols=(0, 3 * D_ATTN))
    zr, _, (g_ao, g_co, g_mo) = _mm(h1, wt_in, "nt", F32, "in_proj_rest", bias=b_in, tm=1024, tn=3 * D_ATTN,
                                 cols=(3 * D_ATTN, 2 * D_CONV + 2 * D_MODEL),
                                 comm=_gather_comm([bf(w_attn_o), bf(w_conv_o), bf(w_mix_o)]))
    table = jnp.transpose(_bias_table(sq(rel_bias)), (1, 0, 2))
    ao, (g_up,) = _attn_fwd(qkv, table, comm=_gather_comm([transposed(w_up).astype(BF16)]))
    (u1, u3), (g_dn,) = _conv_fwd(zr, wf_dwc, b_dw_conv, g_conv_ln, b_conv_ln, comm=_gather_comm([bf(w_down)]))
    wf_ao = _blocks_to_cols(g_ao)
    wf_co = _blocks_to_cols(g_co)
    wf_mo = g_mo.reshape(D_MODEL, D_MODEL)
    wt_up = g_up.reshape(g_up.shape[0] * g_up.shape[1], D_MODEL)
    wf_dn = g_dn.reshape(D_FF, D_MODEL)
    y, a_br, cb_br = _merge_fwd(ao, u3, zr, wf_ao, wf_co, b_conv_o)
    ymix, (x1, h2), _ = _mm(y, wf_mo, "nn", F32, "mix_o", tm=512, tn=D_MODEL,
                            epilogue=_post_mix_pre_ffn(x2, mod6, g_post_mix, g_pre_ffn, 512))
    up = _mm(h2, wt_up, "nt", F32, "ffn_up", tm=1024, tn=1408)
    act = _ffn_act(up, wf_dwf, b_dw_ffn)
    _, (loss_lanes, dout, dyf, small_f), _ = _mm(act, wf_dn, "nn", F32, "ffn_down", tm=512, tn=D_MODEL,
                                                 epilogue=_final(x1, target, mod6, g_post_ffn, 512))

    dact = _mm(dyf, wf_dn, "nt", BF16, "ffn_down_dx", tm=1024, tn=1408)
    gw_down = _mm(act, dyf, "tn", BF16, "ffn_down_dw", tm=256, tn=1024)
    (dup_v, dup_g, dwv, dwg, dbv, dbg), (parts_down,) = _ffn_act_bwd(
        up, dact, wf_dwf, b_dw_ffn, comm=_scatter_comm([_rows_to_blocks(gw_down)]))
    _, (dx1, dymix, small_m), _ = _mm([dup_v, dup_g], wt_up, "nn", F32, "ffn_up_dx", tm=512, tn=D_MODEL,
                                      epilogue=_mid_bwd(x1, dout, ymix, mod6, g_pre_ffn, g_post_mix, 512))
    blocks_up = _rows_to_blocks(_mm_tn_rows([dup_v, dup_g], h2, "ffn_up_dw"))
    _, (da, dcb, dga, dgb, small_g), _ = _mm(dymix, wf_mo, "nt", F32, "mix_o_dx", tm=512, tn=D_MODEL,
                                             epilogue=_merge_bwd(a_br, cb_br, zr, 512))
    gw_mo = _mm(y, dymix, "tn", BF16, "mix_o_dw")
    dao = _mm(da, wf_ao, "nt", BF16, "attn_o_dx", tm=1024)
    gw_ao = _mm(ao, da, "tn", BF16, "attn_o_dw")
    du3 = _mm(dcb, wf_co, "nt", F32, "conv_o_dx", tm=1024)
    gw_co = _mm(u3, dcb, "tn", BF16, "conv_o_dw")
    (dq, dkt, dvt, dbias, small_a), (parts_up,) = _attn_bwd(
        qkv, table, dao, comm=_scatter_comm([blocks_up]))
    g_rel = _bias_grad(jnp.transpose(dbias, (1, 0, 2)))
    (dglu_a, dglu_b, dw_conv, small_c), (parts_mo, parts_ao, parts_co) = _conv_bwd(
        zr, u1, du3, wf_dwc, g_conv_ln, b_conv_ln,
        comm=_scatter_comm([_rows_to_blocks(gw_mo), _cols_to_blocks(gw_ao), _cols_to_blocks(gw_co)]))
    dz = _assemble_dz(dq, dkt, dvt, dglu_a, dglu_b, dga, dgb)
    blocks_in = _rows_to_blocks(_mm(dz, h1, "tn", BF16, "in_proj_dw", tm=512, tn=D_MODEL))
    _, (grad_x, small_x), (parts_in, _, _) = _mm(dz, wt_in, "nn", F32, "in_proj_dx", tm=512, tn=D_MODEL,
                                                 comm=_pair_scatter_comm(blocks_in),
                                                 epilogue=_pre_mix_bwd(x2, dx1, mod6, g_pre_mix, 512))

    packed = _pack_grads(small_x, small_m, small_f, small_g, small_a, small_c, dbv, dbg, dwv, dwg, dw_conv)
    gathered, gathered_rel, gathered_loss = _run_comm(_gather_comm([packed, g_rel, loss_lanes]), "gather_small")
    gathered = gathered.reshape(N_DEV, PACKED_TOTAL)
    updates, g_dwc_full, g_dwf_full, loss_all = _small_adamw(gathered, gathered_rel, gathered_loss, weights, mom_m,
                                                             mom_v)
    loss = loss_all[0, 0]

    grads, deltas, new_m, new_v = {}, {}, {}, {}

    def record(name, update, is_transposed=False):
        for dst, val in zip((grads, deltas, new_m, new_v), update):
            dst[name] = (jnp.swapaxes(val, 0, 1) if is_transposed else val).reshape(shapes[name])

    for name, update in updates.items():
        record(name, update)

    def local_update(name, grad, view=sq):
        record(name, _adamw(view(weights[name]), view(mom_m[name]), view(mom_v[name]), "adamw_" + name, g=view(grad)))

    def taps_major(a):
        return a.reshape(a.shape[1], 1, a.shape[2])

    conv_cols, ffn_cols, ada_cols = D_CONV // N_DEV, 2 * D_FF // N_DEV, 6 * D_MODEL // N_DEV
    local_update("w_dw_conv", lax.dynamic_slice(g_dwc_full, (0, me * conv_cols), (CONV_K, conv_cols))[None], taps_major)
    local_update("w_dw_ffn", lax.dynamic_slice(g_dwf_full, (0, me * ffn_cols), (3, ffn_cols))[None], taps_major)
    local_update("w_ada", _ada_grad(c_all, lax.dynamic_slice(gathered, (0, me * ada_cols), (N_DEV, ada_cols)))[None])

    for name, part in (("w_attn_o", parts_ao), ("w_conv_o", parts_co), ("w_mix_o", parts_mo), ("w_down", parts_down)):
        record(name, _adamw(sq(weights[name]), sq(mom_m[name]), sq(mom_v[name]), "adamw_" + name, parts=part))
    for name, part in (("w_in", parts_in), ("w_up", parts_up)):
        record(name, _adamw(transposed(weights[name]), transposed(mom_m[name]), transposed(mom_v[name]),
                            "adamw_" + name, parts=part), is_transposed=True)

    return (loss, grad_x.reshape(x.shape), *[grads[n] for n in names], *[deltas[n] for n in names],
            *[new_m[n] for n in names], *[new_v[n] for n in names])
```

```python
import functools
import math

import jax
import jax.numpy as jnp
from jax import lax
from jax.experimental import pallas as pl
from jax.experimental.pallas import tpu as pltpu

F32 = jnp.float32
BF16 = jnp.bfloat16
HIGHEST = lax.Precision.HIGHEST

D_MODEL = 1024
CHUNK = 64
LEFT_CHUNKS = 8
BAND = (LEFT_CHUNKS + 1) * CHUNK
PAD_ROWS = LEFT_CHUNKS * CHUNK
GROUP = 4
GROUP_Q = GROUP * CHUNK
GROUP_K = GROUP_Q + PAD_ROWS
SOFTMAX_ROWS = 16
TOEPLITZ = 640
N_HEADS = 8
HEAD_DIM = 64
D_ATTN = 512
D_CONV = 512
CONV_K = 31
CONV_HALO = 32
MAX_REL = 128
N_REL = 2 * MAX_REL + 1
D_FF = 2816
FFN_HALO = 8
FFN_COLS = 256
EPS = 1e-6
NEG_INF = -1e30
N_DEV = 8

ADAM_LR = 0.001
ADAM_B1 = 0.9
ADAM_B2 = 0.999
ADAM_EPS = 1e-08
ADAM_WD = 0.01
ADAM_STEP = 10

VMEM_LIMIT_BYTES = 56 * 1024 * 1024
ADAMW_BLOCK_BYTES = 768 * 1024

MESH = pl.DeviceIdType.MESH
ANY = pl.BlockSpec(memory_space=pl.ANY)

SH_M, SC_M, GT_M, SH_F, SC_F, GT_F = range(6)

SMALL = (("b_ada", 6144), ("g_pre_mix", 1024), ("g_post_mix", 1024), ("b_in", 4608), ("b_dw_conv", 512),
         ("g_conv_ln", 512), ("b_conv_ln", 512), ("b_conv_o", 1024), ("g_pre_ffn", 1024), ("g_post_ffn", 1024),
         ("b_dw_ffn", 5632))
PACKED_TOTAL = sum(n for _, n in SMALL) + CONV_K * D_CONV + 3 * 2 * D_FF


def _cparams(n_axes):
    return pltpu.CompilerParams(vmem_limit_bytes=VMEM_LIMIT_BYTES,
                                dimension_semantics=("arbitrary",) * n_axes)


def _sig(v):
    return 1.0 / (1.0 + jnp.exp(-v))


def _pick(n, target):
    if n <= target:
        return n
    t = target - target % 128
    while n % t:
        t -= 128
    return t


def _tile(rows, cols, col=0):
    return pl.BlockSpec((rows, cols), lambda i: (i, col))


def _full(shape):
    zeros = (0,) * len(shape)
    return pl.BlockSpec(shape, lambda i: zeros)


def _prev(halo, cols, rows, col=0):
    return pl.BlockSpec((halo, cols), lambda i: (jnp.maximum(i * (rows // halo) - 1, 0), col))


def _next(halo, cols, rows, n_blocks, col=0):
    return pl.BlockSpec((halo, cols), lambda i: (jnp.minimum((i + 1) * (rows // halo), n_blocks - 1), col))


class _Comm:
    def __init__(self, inputs, out_shapes, sems, start, finish, relay=None, early=None):
        self.inputs, self.out_shapes, self.sems, self.start, self.finish = inputs, out_shapes, sems, start, finish
        self.relay, self.early = relay, early


def _host_call(body, name, grid, in_specs, out_specs, out_shape, scratch_shapes, args, comm=None):
    n_in, n_out, n_scr = len(args), len(out_shape), len(scratch_shapes)
    c_in = list(comm.inputs) if comm else []
    c_out = list(comm.out_shapes) if comm else []
    c_sem = list(comm.sems) if comm else []

    def full(*refs):
        bounds = [0, n_in, len(c_in), n_out, len(c_out), n_scr, len(c_sem)]
        cuts = [sum(bounds[:i + 1]) for i in range(len(bounds))]
        ins, cins, outs, couts, scr, csems = (refs[lo:hi] for lo, hi in zip(cuts[:-1], cuts[1:]))
        if comm:
            first = functools.reduce(jnp.logical_and, [pl.program_id(ax) == 0 for ax in range(len(grid))])
            pl.when(first)(lambda: comm.start(cins, couts, csems))
            if comm.early is not None:
                strides = [math.prod(grid[ax + 1:]) for ax in range(len(grid))]
                step = sum(pl.program_id(ax) * strides[ax] for ax in range(len(grid)))
                pl.when(step == 1)(lambda: comm.early(cins, couts, csems))
            last = functools.reduce(jnp.logical_and, [pl.program_id(ax) == grid[ax] - 1 for ax in range(len(grid))])
            if comm.relay is not None:
                pl.when(last)(lambda: comm.relay(cins, couts, csems))
        body(ins, outs, scr)
        if comm:
            pl.when(last)(lambda: comm.finish(cins, couts, csems))

    res = pl.pallas_call(
        full, name=name, grid=grid, in_specs=list(in_specs) + [ANY] * len(c_in),
        out_specs=list(out_specs) + [ANY] * len(c_out), out_shape=list(out_shape) + c_out,
        scratch_shapes=list(scratch_shapes) + c_sem, compiler_params=_cparams(len(grid)),
    )(*args, *c_in)
    return list(res[:n_out]), list(res[n_out:])


def _run_comm(comm, name):
    n_in, n_out = len(comm.inputs), len(comm.out_shapes)

    def body(*refs):
        ins, outs, sems = refs[:n_in], refs[n_in:n_in + n_out], refs[n_in + n_out:]
        comm.start(ins, outs, sems)
        if comm.relay is not None:
            comm.relay(ins, outs, sems)
        comm.finish(ins, outs, sems)

    return pl.pallas_call(
        body, name=name, out_shape=list(comm.out_shapes), in_specs=[ANY] * n_in, out_specs=[ANY] * n_out,
        scratch_shapes=list(comm.sems),
    )(*comm.inputs)


def _place():
    return lax.axis_index("x"), lax.axis_index("y"), lax.axis_index("c")


def _gather_comm(arrs):
    n = len(arrs)

    def plan(ins, outs, sems):
        send_sems, recv_sems, local_sems = sems
        x, y, c = _place()
        me, sibling = (x, y, c), (x, y, 1 - c)
        chips = [(1 - x, y), (x, 1 - y), (1 - x, 1 - y)]

        def block(k, p):
            return outs[k].at[4 * p[0] + 2 * p[1] + p[2]]

        def copy(k, s, blk, to, src=None):
            return pltpu.make_async_remote_copy(
                src_ref=block(k, blk) if src is None else src, dst_ref=block(k, blk),
                send_sem=send_sems.at[7 * k + s], recv_sem=recv_sems.at[7 * k + s],
                device_id=to, device_id_type=MESH)

        mine = [pltpu.make_async_copy(ins[k], block(k, me), local_sems.at[k]) for k in range(n)]
        first = []
        for k in range(n):
            first.append(copy(k, 0, me, sibling, src=ins[k]))
            for j, chip in enumerate(chips):
                first.append(copy(k, 1 + j, me, (*chip, c), src=ins[k]))
        return me, sibling, chips, c, copy, mine, first

    def start(ins, outs, sems):
        *_, mine, first = plan(ins, outs, sems)
        for cp in mine + first:
            cp.start()

    def relay(ins, outs, sems):
        me, sibling, chips, c, copy, _, _ = plan(ins, outs, sems)
        for j, chip in enumerate(chips):
            for k in range(n):
                copy(k, 1 + j, (*chip, c), me).wait_recv()
                copy(k, 4 + j, (*chip, c), sibling).start()

    def finish(ins, outs, sems):
        me, sibling, chips, c, copy, mine, first = plan(ins, outs, sems)
        passed = [copy(k, 4 + j, (*chip, c), sibling) for j, chip in enumerate(chips) for k in range(n)]
        for k in range(n):
            copy(k, 0, sibling, me).wait_recv()
        for j, chip in enumerate(chips):
            for k in range(n):
                copy(k, 4 + j, (*chip, 1 - c), me).wait_recv()
        for cp in first + passed:
            cp.wait_send()
        for cp in mine:
            cp.wait()

    return _Comm(list(arrs), [jax.ShapeDtypeStruct((N_DEV,) + a.shape, a.dtype) for a in arrs],
                 [pltpu.SemaphoreType.DMA((7 * n,)), pltpu.SemaphoreType.DMA((7 * n,)),
                  pltpu.SemaphoreType.DMA((n,))], start, finish, relay)


def _scatter_comm(blocks):
    n = len(blocks)

    def plan(ins, outs, sems, arrivals):
        send_sems, recv_sems, local_sems = sems
        x, y, c = _place()
        me = 4 * x + 2 * y + c
        local = [pltpu.make_async_copy(ins[k].at[me], outs[k].at[me], local_sems.at[k]) for k in range(n)]
        sends, recvs = [], []
        for k in range(n):
            for mask in range(1, N_DEV):
                px = 1 - x if mask & 4 else x
                py = 1 - y if mask & 2 else y
                pc = 1 - c if mask & 1 else c
                peer = 4 * px + 2 * py + pc
                sem = 7 * k + mask - 1
                both = dict(send_sem=send_sems.at[sem], recv_sem=recv_sems.at[sem], device_id=(px, py, pc),
                            device_id_type=MESH)
                sends.append(pltpu.make_async_remote_copy(src_ref=ins[k].at[peer], dst_ref=outs[k].at[me], **both))
                if arrivals:
                    recvs.append(pltpu.make_async_remote_copy(src_ref=ins[k].at[me], dst_ref=outs[k].at[peer],
                                                              **both))
        return local, sends, recvs

    def start(ins, outs, sems):
        local, sends, _ = plan(ins, outs, sems, arrivals=False)
        for cp in local + sends:
            cp.start()

    def finish(ins, outs, sems):
        local, sends, recvs = plan(ins, outs, sems, arrivals=True)
        for cp in recvs:
            cp.wait_recv()
        for cp in sends:
            cp.wait_send()
        for cp in local:
            cp.wait()

    return _Comm(list(blocks), [jax.ShapeDtypeStruct(b.shape, b.dtype) for b in blocks],
                 [pltpu.SemaphoreType.DMA((7 * n,)), pltpu.SemaphoreType.DMA((7 * n,)),
                  pltpu.SemaphoreType.DMA((n,))], start, finish)


def _pair_scatter_comm(block):
    _, r, c = block.shape
    quarter = jax.ShapeDtypeStruct((4, r, c), block.dtype)

    def plan(ins, outs, sems):
        parts, got, pair = outs
        d2d_send, d2d_recv, ici_send, ici_recv, local, *bufs = sems
        x, y, cc = _place()
        mine = 2 * x + y
        chips = [(1 - x, y), (x, 1 - y), (1 - x, 1 - y)]
        to_sibling = [pltpu.make_async_remote_copy(
            src_ref=ins[0].at[2 * q + 1 - cc], dst_ref=got.at[q], send_sem=d2d_send.at[q], recv_sem=d2d_recv.at[q],
            device_id=(x, y, 1 - cc), device_id_type=MESH) for q in range(4)]
        to_chips = [pltpu.make_async_remote_copy(
            src_ref=pair.at[2 * px + py], dst_ref=parts.at[mine], send_sem=ici_send.at[j], recv_sem=ici_recv.at[j],
            device_id=(px, py, cc), device_id_type=MESH) for j, (px, py) in enumerate(chips)]
        from_chips = [pltpu.make_async_remote_copy(
            src_ref=pair.at[mine], dst_ref=parts.at[2 * px + py], send_sem=ici_send.at[j], recv_sem=ici_recv.at[j],
            device_id=(px, py, cc), device_id_type=MESH) for j, (px, py) in enumerate(chips)]
        own = pltpu.make_async_copy(pair.at[mine], parts.at[mine], local.at[5])
        order = [2 * px + py for px, py in chips] + [mine]
        return cc, got, pair, local, bufs, order, to_sibling, to_chips, from_chips, own

    def start(ins, outs, sems):
        for cp in plan(ins, outs, sems)[6]:
            cp.start()

    def early(ins, outs, sems):
        cc, got, pair, local, bufs, order, to_sibling, to_chips, _, own = plan(ins, outs, sems)
        for cp in to_sibling:
            cp.wait_recv()

        def loads(k):
            return [pltpu.make_async_copy(ins[0].at[2 * order[k] + cc], bufs[2 * (k % 2)], local.at[k % 2]),
                    pltpu.make_async_copy(got.at[order[k]], bufs[2 * (k % 2) + 1], local.at[2 + k % 2])]

        for cp in loads(0):
            cp.start()
        for k, send in enumerate(to_chips + [own]):
            if k + 1 < len(order):
                for cp in loads(k + 1):
                    cp.start()
            for cp in loads(k):
                cp.wait()
            kept, came = bufs[2 * (k % 2)], bufs[2 * (k % 2) + 1]
            kept[...] = (kept[...].astype(F32) + came[...].astype(F32)).astype(block.dtype)
            store = pltpu.make_async_copy(kept, pair.at[order[k]], local.at[4])
            store.start()
            store.wait()
            send.start()

    def finish(ins, outs, sems):
        *_, to_sibling, to_chips, from_chips, own = plan(ins, outs, sems)
        for cp in from_chips:
            cp.wait_recv()
        for cp in to_chips + to_sibling:
            cp.wait_send()
        own.wait()

    return _Comm([block], [quarter, quarter, quarter],
                 [pltpu.SemaphoreType.DMA((4,)), pltpu.SemaphoreType.DMA((4,)), pltpu.SemaphoreType.DMA((3,)),
                  pltpu.SemaphoreType.DMA((3,)), pltpu.SemaphoreType.DMA((6,))]
                 + [pltpu.VMEM((r, c), block.dtype)] * 4, start, finish, early=early)


_DIMS = {"nn": (((1,), (0,)), ((), ())), "nt": (((1,), (1,)), ((), ())), "tn": (((0,), (0,)), ((), ()))}


class _Epilogue:
    def __init__(self, args, in_specs, out_shapes, out_specs, fn, keep_product):
        self.args, self.in_specs, self.out_shapes, self.out_specs = args, in_specs, out_shapes, out_specs
        self.fn, self.keep_product = fn, keep_product


def _row_tile(rows, cols):
    return pl.BlockSpec((rows, cols), lambda i, j: (i, 0))


def _whole(shape):
    zeros = (0,) * len(shape)
    return pl.BlockSpec(shape, lambda i, j: zeros)


def _mm(a, b, mode, out_dtype, name, bias=None, tm=512, tn=512, comm=None, cols=None, epilogue=None, k_piece=None):
    pieces = a if isinstance(a, (list, tuple)) else [a]
    piece_cols = [0] * len(pieces)
    if k_piece is not None:
        pieces, piece_cols = [p for p, _ in a], [c for _, c in a]
    assert all(p.dtype == BF16 for p in pieces) and b.dtype == BF16
    a = pieces[0]
    if mode == "tn":
        k_dim, m_dim = a.shape
    else:
        m_dim, k_dim = a.shape[0], k_piece or a.shape[1]
    n_dim = b.shape[0] if mode == "nt" else b.shape[1]
    col0 = 0
    if cols is not None:
        assert mode != "tn" and cols[0] % tn == 0 and cols[1] % tn == 0
        col0, n_dim = cols[0] // tn, cols[1]
    tm, tn = _pick(m_dim, tm), _pick(n_dim, tn)
    assert mode != "tn" or len(pieces) == 1
    a_specs = [pl.BlockSpec((k_dim, tm), lambda i, j: (0, i)) if mode == "tn"
               else pl.BlockSpec((tm, k_dim), lambda i, j, c=c: (i, c)) for c in piece_cols]
    once = dict(pipeline_mode=pl.Buffered(1)) if tn == n_dim else {}
    if mode == "nt":
        b_specs = [pl.BlockSpec((tn, k_dim), lambda i, j, p=p: (j + col0, p), **once) for p in range(len(pieces))]
    else:
        b_specs = [pl.BlockSpec((k_dim, tn), lambda i, j, p=p: (p, j + col0), **once) for p in range(len(pieces))]
    in_specs = a_specs + b_specs
    args = list(pieces) + [b] * len(pieces)
    if bias is not None:
        in_specs.append(pl.BlockSpec((1, tn), lambda i, j: (0, j + col0)))
        args.append(bias)
    dims = _DIMS[mode]
    n_pieces = len(pieces)
    n_own = len(args)
    keep = epilogue is None or epilogue.keep_product
    out_specs = [pl.BlockSpec((tm, tn), lambda i, j: (i, j))] if keep else []
    out_shape = [jax.ShapeDtypeStruct((m_dim, n_dim), out_dtype)] if keep else []
    if epilogue is not None:
        assert tn == n_dim
        in_specs, args = in_specs + list(epilogue.in_specs), args + list(epilogue.args)
        out_specs, out_shape = out_specs + list(epilogue.out_specs), out_shape + list(epilogue.out_shapes)

    def body(ins, outs, scratch):
        total = lax.dot_general(ins[0][...], ins[n_pieces][...], dims, preferred_element_type=F32)
        for p in range(1, n_pieces):
            total = total + lax.dot_general(ins[p][...], ins[n_pieces + p][...], dims, preferred_element_type=F32)
        if bias is not None:
            total = total + ins[2 * n_pieces][...]
        if keep:
            outs[0][...] = total.astype(out_dtype)
        if epilogue is not None:
            epilogue.fn(total, pl.program_id(0) == 0, ins[n_own:], outs[1:] if keep else outs)

    outs, extra = _host_call(body, name, grid=(m_dim // tm, n_dim // tn), in_specs=in_specs, out_specs=out_specs,
                             out_shape=out_shape, scratch_shapes=[], args=args, comm=comm)
    product = outs[0] if keep else None
    if comm is None and epilogue is None:
        return product
    return product, outs[1:] if keep else outs, extra


def _mm_tn_rows(pieces, b, name, tm=256):
    k_dim, n_dim = b.shape
    counts = [p.shape[1] // tm for p in pieces]
    assert all(p.shape[1] % tm == 0 for p in pieces)
    firsts = [sum(counts[:q]) for q in range(len(pieces))]

    def a_spec(first, count):
        return pl.BlockSpec((k_dim, tm), lambda i: (0, jnp.clip(i - first, 0, count - 1)))

    def body(ins, outs, scratch):
        i = pl.program_id(0)
        for a_ref, first, count in zip(ins[:-1], firsts, counts):
            @pl.when(jnp.logical_and(i >= first, i < first + count))
            def _(a_ref=a_ref):
                outs[0][...] = lax.dot_general(a_ref[...], ins[-1][...], _DIMS["tn"],
                                               preferred_element_type=F32).astype(BF16)

    (out,), _ = _host_call(
        body, name, grid=(sum(counts),),
        in_specs=[a_spec(f, c) for f, c in zip(firsts, counts)]
        + [pl.BlockSpec((k_dim, n_dim), lambda i: (0, 0), pipeline_mode=pl.Buffered(1))],
        out_specs=[_tile(tm, n_dim)], out_shape=[jax.ShapeDtypeStruct((sum(counts) * tm, n_dim), BF16)],
        scratch_shapes=[], args=list(pieces) + [b])
    return out


def _adam_math(w, g, m, v):
    m = ADAM_B1 * m + (1.0 - ADAM_B1) * g
    v = ADAM_B2 * v + (1.0 - ADAM_B2) * (g * g)
    m_hat = m / (1.0 - ADAM_B1 ** ADAM_STEP)
    v_hat = v / (1.0 - ADAM_B2 ** ADAM_STEP)
    delta = -ADAM_LR * (m_hat / (jnp.sqrt(v_hat) + ADAM_EPS) + ADAM_WD * w)
    return delta, m, v


def _adamw(w, m, v, name, g=None, parts=None):
    rows, cols = w.shape[0], w.shape[-1]
    tr = rows
    if rows * cols * 4 > ADAMW_BLOCK_BYTES:
        tr = max(t for t in range(16, rows, 16) if rows % t == 0 and t * cols * 4 <= ADAMW_BLOCK_BYTES)

    def body(w_ref, m_ref, v_ref, g_ref, go_ref, d_ref, mo_ref, vo_ref):
        if parts is None:
            grad = g_ref[...]
        else:
            grad = g_ref[0].astype(F32)
            for d in range(1, parts.shape[0]):
                grad = grad + g_ref[d].astype(F32)
        delta, m_new, v_new = _adam_math(w_ref[...], grad, m_ref[...], v_ref[...])
        go_ref[...] = grad
        d_ref[...] = delta
        mo_ref[...] = m_new
        vo_ref[...] = v_new

    spec = _tile(tr, cols) if w.ndim == 2 else _full(w.shape)
    g_spec = spec if parts is None else pl.BlockSpec((parts.shape[0], tr, cols), lambda i: (0, i, 0))
    shape = jax.ShapeDtypeStruct(w.shape, F32)
    return pl.pallas_call(
        body, name=name, out_shape=[shape] * 4, grid=(rows // tr,),
        in_specs=[spec, spec, spec, g_spec], out_specs=[spec] * 4, compiler_params=_cparams(1),
    )(w, m, v, g if parts is None else parts)


def _pack_grads(small_x, small_m, small_f, small_g, small_a, small_c, dbv, dbg, dwv, dwg, dw_conv):
    pieces = [
        (small_x, 2, D_MODEL), (small_x, 1, D_MODEL), (small_m, 4, D_MODEL), (small_m, 2, D_MODEL),
        (small_m, 1, D_MODEL), (small_f, 1, D_MODEL),
        (small_x, 0, D_MODEL), (small_m, 3, D_MODEL),
        (small_a, 0, D_ATTN), (small_a, 1, D_ATTN), (small_a, 2, D_ATTN), (small_c, 3, D_CONV),
        (small_c, 4, D_CONV), (small_g, 0, D_MODEL), (small_g, 1, D_MODEL),
        (small_c, 0, D_CONV), (small_c, 1, D_CONV), (small_c, 2, D_CONV),
        (small_g, 2, D_MODEL), (small_m, 0, D_MODEL), (small_f, 0, D_MODEL),
        (dbv, 0, D_FF), (dbg, 0, D_FF),
    ]
    pieces += [(dw_conv, j, D_CONV) for j in range(CONV_K)]
    pieces += [(src, tap, D_FF) for tap in range(3) for src in (dwv, dwg)]
    sources = [small_x, small_m, small_f, small_g, small_a, small_c, dbv, dbg, dwv, dwg, dw_conv]
    assert sum(width for _, _, width in pieces) == PACKED_TOTAL

    def body(*refs):
        o_ref = refs[-1]
        ref_of = {id(src): ref for src, ref in zip(sources, refs)}
        off = 0
        for src, row, width in pieces:
            o_ref[:, off:off + width] = ref_of[id(src)][row:row + 1, :]
            off += width

    return pl.pallas_call(body, name="pack_grads", out_shape=jax.ShapeDtypeStruct((1, PACKED_TOTAL), F32))(*sources)


def _small_adamw(gathered, gathered_rel, gathered_loss, weights, mom_m, mom_v):
    vec_names = [name for name, _ in SMALL]
    states = []
    for name in vec_names + ["rel_bias"]:
        states += [weights[name], mom_m[name], mom_v[name]]
    states = [a.reshape(a.shape[1:]) if a.ndim == 3 else a for a in states]
    n_state = len(states)

    def body(*refs):
        g_ref, rel_ref, loss_ref = refs[0], refs[1], refs[2]
        state_refs, out_refs = refs[3:3 + n_state], refs[3 + n_state:]
        total = g_ref[0:1, :]
        rel = rel_ref[0]
        loss = loss_ref[0]
        for d in range(1, N_DEV):
            total = total + g_ref[d:d + 1, :]
            rel = rel + rel_ref[d]
            loss = loss + loss_ref[d]
        off = 0
        for n, (name, width) in enumerate(SMALL):
            grad = total[:, off:off + width]
            w_ref, m_ref, v_ref = state_refs[3 * n:3 * n + 3]
            for ref, val in zip(out_refs[4 * n:4 * n + 4], (grad,) + _adam_math(w_ref[...], grad, m_ref[...], v_ref[...])):
                ref[...] = val
            off += width
        n = len(SMALL)
        w_ref, m_ref, v_ref = state_refs[3 * n:3 * n + 3]
        for ref, val in zip(out_refs[4 * n:4 * n + 4], (rel,) + _adam_math(w_ref[...], rel, m_ref[...], v_ref[...])):
            ref[...] = val
        dwc_ref, dwf_ref, loss_out = out_refs[4 * n + 4:]
        loss_out[...] = 0.5 * loss
        dwc_ref[...] = jnp.zeros_like(dwc_ref)
        dwf_ref[...] = jnp.zeros_like(dwf_ref)
        for j in range(CONV_K):
            dwc_ref[j:j + 1, :] = total[:, off:off + D_CONV]
            off += D_CONV
        for tap in range(3):
            dwf_ref[tap:tap + 1, :] = total[:, off:off + 2 * D_FF]
            off += 2 * D_FF

    out_shape = []
    for k in range(n_state // 3):
        out_shape += [jax.ShapeDtypeStruct(states[3 * k].shape, F32)] * 4
    out_shape += [jax.ShapeDtypeStruct((CONV_HALO, D_CONV), F32), jax.ShapeDtypeStruct((8, 2 * D_FF), F32),
                  jax.ShapeDtypeStruct((1, 128), F32)]
    res = pl.pallas_call(
        body, name="small_adamw", out_shape=out_shape,
        compiler_params=pltpu.CompilerParams(vmem_limit_bytes=VMEM_LIMIT_BYTES),
    )(gathered, gathered_rel, gathered_loss, *states)
    updates = {name: tuple(res[4 * n:4 * n + 4]) for n, name in enumerate(vec_names + ["rel_bias"])}
    return updates, res[-3], res[-2], res[-1]


def _ada_mod(c, w_shard):
    cols = w_shard.shape[1]

    def body(c_ref, w_ref, call_ref, mod_ref, send_sems, recv_sems):
        x, y, cc = _place()
        me = 4 * x + 2 * y + cc

        def exchange(ref, phase):
            sends, arrivals = [], []
            for mask in range(1, N_DEV):
                px = 1 - x if mask & 4 else x
                py = 1 - y if mask & 2 else y
                pc = 1 - cc if mask & 1 else cc
                both = dict(send_sem=send_sems.at[7 * phase + mask - 1], recv_sem=recv_sems.at[7 * phase + mask - 1],
                            device_id=(px, py, pc), device_id_type=MESH)
                sends.append(pltpu.make_async_remote_copy(src_ref=ref.at[me], dst_ref=ref.at[me], **both))
                arrivals.append(pltpu.make_async_remote_copy(src_ref=ref.at[me], dst_ref=ref.at[4 * px + 2 * py + pc],
                                                             **both))
            for cp in sends:
                cp.start()
            for cp in arrivals:
                cp.wait_recv()
            for cp in sends:
                cp.wait_send()

        v = c_ref[...]
        call_ref[me] = v * _sig(v)
        exchange(call_ref, 0)
        c_all = jnp.concatenate([call_ref[d] for d in range(N_DEV)], axis=0)
        mod_ref[me] = jnp.dot(c_all, w_ref[...], precision=HIGHEST, preferred_element_type=F32)
        exchange(mod_ref, 1)

    return pl.pallas_call(
        body, name="ada_mod",
        out_shape=[jax.ShapeDtypeStruct((N_DEV, 1, D_MODEL), F32), jax.ShapeDtypeStruct((N_DEV, N_DEV, cols), F32)],
        scratch_shapes=[pltpu.SemaphoreType.DMA((14,)), pltpu.SemaphoreType.DMA((14,))],
        compiler_params=pltpu.CompilerParams(vmem_limit_bytes=VMEM_LIMIT_BYTES),
    )(c, w_shard)


def _ada_grad(c_all, dmod_shard):
    def body(c_ref, d_ref, o_ref):
        o_ref[...] = lax.dot_general(c_ref[...], d_ref[...], _DIMS["tn"], precision=HIGHEST,
                                     preferred_element_type=F32)

    return pl.pallas_call(
        body, name="ada_grad", out_shape=jax.ShapeDtypeStruct((D_MODEL, dmod_shard.shape[1]), F32),
        compiler_params=pltpu.CompilerParams(vmem_limit_bytes=VMEM_LIMIT_BYTES),
    )(c_all, dmod_shard)


ROWS = 256


def _rms(v):
    r = lax.rsqrt(jnp.mean(v * v, axis=-1, keepdims=True) + EPS)
    return v * r, r


def _rms_bwd(dxn, xn, r):
    return r * (dxn - xn * jnp.mean(dxn * xn, axis=-1, keepdims=True))


def _colsum(v):
    return jnp.sum(v, axis=0, keepdims=True)


def _pre_mix(x, mod6, g1, comm=None):
    seq = x.shape[0]

    def body(ins, outs, scratch):
        x_ref, mod_ref, g_ref = ins
        xn, _ = _rms(x_ref[...])
        y = xn * g_ref[...]
        outs[0][...] = (y * (1.0 + mod_ref[SC_M:SC_M + 1, :]) + mod_ref[SH_M:SH_M + 1, :]).astype(BF16)

    (h,), extra = _host_call(
        body, "pre_mix", grid=(seq // ROWS,),
        in_specs=[_tile(ROWS, D_MODEL), _full((6, D_MODEL)), _full((1, D_MODEL))], out_specs=[_tile(ROWS, D_MODEL)],
        out_shape=[jax.ShapeDtypeStruct((seq, D_MODEL), BF16)], scratch_shapes=[], args=[x, mod6, g1], comm=comm)
    return h, extra


def _post_mix_pre_ffn(x, mod6, g2, g3, rows):
    seq = x.shape[0]

    def fn(y, first, ins, outs):
        x_ref, mod_ref, g2_ref, g3_ref = ins
        x1_ref, h_ref = outs
        yn, _ = _rms(y)
        x1 = x_ref[...] + mod_ref[GT_M:GT_M + 1, :] * (yn * g2_ref[...])
        x1_ref[...] = x1
        xn, _ = _rms(x1)
        y3 = xn * g3_ref[...]
        h_ref[...] = (y3 * (1.0 + mod_ref[SC_F:SC_F + 1, :]) + mod_ref[SH_F:SH_F + 1, :]).astype(BF16)

    return _Epilogue(
        [x, mod6, g2, g3], [_row_tile(rows, D_MODEL), _whole((6, D_MODEL)), _whole((1, D_MODEL)), _whole((1, D_MODEL))],
        [jax.ShapeDtypeStruct((seq, D_MODEL), F32), jax.ShapeDtypeStruct((seq, D_MODEL), BF16)],
        [_row_tile(rows, D_MODEL), _row_tile(rows, D_MODEL)], fn, keep_product=True)


def _final(x1, target, mod6, g4, rows):
    seq = x1.shape[0]

    def fn(y, first, ins, outs):
        x1_ref, t_ref, mod_ref, g_ref = ins
        loss_ref, dout_ref, dyf_ref, small_ref = outs

        @pl.when(first)
        def _():
            loss_ref[...] = jnp.zeros_like(loss_ref)
            small_ref[...] = jnp.zeros_like(small_ref)

        gt = mod_ref[GT_F:GT_F + 1, :]
        g4v = g_ref[...]
        yn, r = _rms(y)
        out = x1_ref[...] + gt * (yn * g4v)
        err = out - t_ref[...]
        loss_ref[...] += jnp.sum(jnp.mean(err * err, axis=-1, keepdims=True))
        dout = err * (1.0 / D_MODEL)
        dout_ref[...] = dout
        small_ref[0:1, :] += _colsum(dout * gt * yn)
        small_ref[1:2, :] += _colsum(dout * (yn * g4v))
        dyf_ref[...] = _rms_bwd(dout * gt * g4v, yn, r).astype(BF16)

    return _Epilogue(
        [x1, target, mod6, g4],
        [_row_tile(rows, D_MODEL), _row_tile(rows, D_MODEL), _whole((6, D_MODEL)), _whole((1, D_MODEL))],
        [jax.ShapeDtypeStruct((1, 128), F32), jax.ShapeDtypeStruct((seq, D_MODEL), F32),
         jax.ShapeDtypeStruct((seq, D_MODEL), BF16), jax.ShapeDtypeStruct((8, D_MODEL), F32)],
        [_whole((1, 128)), _row_tile(rows, D_MODEL), _row_tile(rows, D_MODEL), _whole((8, D_MODEL))],
        fn, keep_product=False)


def _mid_bwd(x1, dout, ymix, mod6, g3, g2, rows):
    seq = x1.shape[0]

    def fn(dh, first, ins, outs):
        x1_ref, dout_ref, y_ref, mod_ref, g3_ref, g2_ref = ins
        dx1_ref, dy_ref, small_ref = outs

        @pl.when(first)
        def _():
            small_ref[...] = jnp.zeros_like(small_ref)

        g3v, g2v = g3_ref[...], g2_ref[...]
        xn, r3 = _rms(x1_ref[...])
        y3 = xn * g3v
        dy3 = dh * (1.0 + mod_ref[SC_F:SC_F + 1, :])
        small_ref[0:1, :] += _colsum(dy3 * xn)
        small_ref[1:2, :] += _colsum(dh * y3)
        small_ref[2:3, :] += _colsum(dh)
        dx1 = dout_ref[...] + _rms_bwd(dy3 * g3v, xn, r3)
        dx1_ref[...] = dx1
        gt = mod_ref[GT_M:GT_M + 1, :]
        yn, r2 = _rms(y_ref[...])
        small_ref[3:4, :] += _colsum(dx1 * gt * yn)
        small_ref[4:5, :] += _colsum(dx1 * (yn * g2v))
        dy_ref[...] = _rms_bwd(dx1 * gt * g2v, yn, r2).astype(BF16)

    return _Epilogue(
        [x1, dout, ymix, mod6, g3, g2],
        [_row_tile(rows, D_MODEL)] * 3 + [_whole((6, D_MODEL)), _whole((1, D_MODEL)), _whole((1, D_MODEL))],
        [jax.ShapeDtypeStruct((seq, D_MODEL), F32), jax.ShapeDtypeStruct((seq, D_MODEL), BF16),
         jax.ShapeDtypeStruct((8, D_MODEL), F32)],
        [_row_tile(rows, D_MODEL), _row_tile(rows, D_MODEL), _whole((8, D_MODEL))], fn, keep_product=False)


def _pre_mix_bwd(x, dx1, mod6, g1, rows):
    seq = x.shape[0]

    def fn(dh, first, ins, outs):
        x_ref, dx1_ref, mod_ref, g_ref = ins
        dx_ref, small_ref = outs

        @pl.when(first)
        def _():
            small_ref[...] = jnp.zeros_like(small_ref)

        g1v = g_ref[...]
        xn, r = _rms(x_ref[...])
        dy = dh * (1.0 + mod_ref[SC_M:SC_M + 1, :])
        small_ref[0:1, :] += _colsum(dy * xn)
        small_ref[1:2, :] += _colsum(dh * (xn * g1v))
        small_ref[2:3, :] += _colsum(dh)
        dx_ref[...] = dx1_ref[...] + _rms_bwd(dy * g1v, xn, r)

    return _Epilogue(
        [x, dx1, mod6, g1],
        [_row_tile(rows, D_MODEL), _row_tile(rows, D_MODEL), _whole((6, D_MODEL)), _whole((1, D_MODEL))],
        [jax.ShapeDtypeStruct((seq, D_MODEL), F32), jax.ShapeDtypeStruct((8, D_MODEL), F32)],
        [_row_tile(rows, D_MODEL), _whole((8, D_MODEL))], fn, keep_product=False)


def _toeplitz_onehot(shape, offset_axis, top):
    m = lax.broadcasted_iota(jnp.int32, shape, offset_axis)
    i = lax.broadcasted_iota(jnp.int32, shape, 1 - offset_axis)
    return (i == jnp.clip(top - m, -MAX_REL, MAX_REL) + MAX_REL).astype(F32)


def _bias_table(rel_bias):
    width = GROUP_Q + GROUP_K

    def body(rb_ref, o_ref, t_ref):
        t_ref[...] = jnp.dot(rb_ref[...], _toeplitz_onehot((N_REL, width), 1, GROUP_K - 1), precision=HIGHEST,
                             preferred_element_type=F32)
        lane = lax.broadcasted_iota(jnp.int32, (N_HEADS, GROUP_K), 1)
        for r in range(GROUP_Q):
            first_key = (r // CHUNK) * CHUNK
            band = jnp.logical_and(lane >= first_key, lane < first_key + BAND)
            o_ref[r] = jnp.where(band, t_ref[:, GROUP_Q - 1 - r:GROUP_Q - 1 - r + GROUP_K], NEG_INF)

    return pl.pallas_call(
        body, name="bias_table", out_shape=jax.ShapeDtypeStruct((GROUP_Q, N_HEADS, GROUP_K), F32),
        scratch_shapes=[pltpu.VMEM((N_HEADS, width), F32)],
    )(rel_bias)


def _bias_grad(dbias_q):
    def body(d_ref, o_ref, t_ref):
        t_ref[...] = jnp.zeros_like(t_ref)
        for qi in range(CHUNK):
            t_ref[:, CHUNK - 1 - qi:CHUNK - 1 - qi + BAND] += d_ref[qi]
        o_ref[...] = jnp.dot(t_ref[...], _toeplitz_onehot((TOEPLITZ, N_REL), 0, BAND - 1), precision=HIGHEST,
                             preferred_element_type=F32)

    return pl.pallas_call(
        body, name="bias_grad", out_shape=jax.ShapeDtypeStruct((N_HEADS, N_REL), F32),
        scratch_shapes=[pltpu.VMEM((N_HEADS, TOEPLITZ), F32)],
    )(dbias_q)


def _resident_copies(qkv_hbm, t_hbm, k_ref, v_ref, t_ref, sems):
    inside = pl.ds(PAD_ROWS, qkv_hbm.shape[0])
    return (pltpu.make_async_copy(qkv_hbm.at[:, pl.ds(D_ATTN, D_ATTN)], k_ref.at[inside, :], sems.at[0]),
            pltpu.make_async_copy(qkv_hbm.at[:, pl.ds(2 * D_ATTN, D_ATTN)], v_ref.at[inside, :], sems.at[1]),
            pltpu.make_async_copy(t_hbm, t_ref, sems.at[2]))


def _start_resident(copies, k_ref, v_ref):
    k_ref[0:PAD_ROWS, :] = jnp.zeros((PAD_ROWS, D_ATTN), BF16)
    v_ref[0:PAD_ROWS, :] = jnp.zeros((PAD_ROWS, D_ATTN), BF16)
    for cp in copies:
        cp.start()


def _softmax_rows(s_ref, t_ref, h, before_start, rows):
    s = s_ref[rows, :] * (HEAD_DIM ** -0.5) + t_ref[h, rows, :] + before_start
    e = jnp.exp(s - jnp.max(s, axis=-1, keepdims=True))
    return e / jnp.sum(e, axis=-1, keepdims=True)


def _before_start(g):
    kj = lax.broadcasted_iota(jnp.int32, (8, GROUP_K), 1)
    return jnp.where(kj >= PAD_ROWS - g * GROUP_Q, 0.0, NEG_INF)


def _attn_fwd(qkv, table, comm=None):
    seq = qkv.shape[0]

    def body(ins, outs, scratch):
        q_ref, qkv_hbm, t_hbm = ins
        (o_ref,) = outs
        k_ref, v_ref, t_ref, s_ref, p_ref, sems = scratch
        g = pl.program_id(0)
        load_k, load_v, load_t = _resident_copies(qkv_hbm, t_hbm, k_ref, v_ref, t_ref, sems)

        @pl.when(g == 0)
        def _():
            _start_resident((load_k, load_v, load_t), k_ref, v_ref)
            load_k.wait()

        window = pl.ds(pl.multiple_of(g * GROUP_Q, GROUP_Q), GROUP_K)
        before_start = _before_start(g)
        for h in range(N_HEADS):
            cols = slice(h * HEAD_DIM, (h + 1) * HEAD_DIM)
            buf = h % 2
            s_ref[buf] = lax.dot_general(q_ref[:, cols], k_ref[window, cols], _DIMS["nt"],
                                         preferred_element_type=F32)
            if h == 0:
                pl.when(g == 0)(load_t.wait)
            for row in range(0, GROUP_Q, SOFTMAX_ROWS):
                halves = [_softmax_rows(s_ref.at[buf], t_ref, h, before_start, slice(r, r + 8))
                          for r in (row, row + 8)]
                p_ref[buf, row:row + SOFTMAX_ROWS, :] = jnp.concatenate(halves, axis=0).astype(BF16)
            if h == 0:
                pl.when(g == 0)(load_v.wait)
            o_ref[:, cols] = jnp.dot(p_ref[buf], v_ref[window, cols], preferred_element_type=F32).astype(BF16)

    (ao,), extra = _host_call(
        body, "attn_fwd", grid=(seq // GROUP_Q,),
        in_specs=[_tile(GROUP_Q, D_ATTN), ANY, ANY], out_specs=[_tile(GROUP_Q, D_ATTN)],
        out_shape=[jax.ShapeDtypeStruct((seq, D_ATTN), BF16)],
        scratch_shapes=[pltpu.VMEM((seq + PAD_ROWS, D_ATTN), BF16), pltpu.VMEM((seq + PAD_ROWS, D_ATTN), BF16),
                        pltpu.VMEM(table.shape, F32),
                        pltpu.VMEM((2, GROUP_Q, GROUP_K), F32), pltpu.VMEM((2, GROUP_Q, GROUP_K), BF16),
                        pltpu.SemaphoreType.DMA((3,))],
        args=[qkv, qkv, table], comm=comm)
    return ao, extra


def _attn_bwd(qkv, table, dao, comm=None):
    seq = qkv.shape[0]
    n_groups = seq // GROUP_Q
    fold_w = GROUP_K + (GROUP - 1) * CHUNK

    def body(ins, outs, scratch):
        q_ref, do_ref, qkv_hbm, t_hbm = ins
        dq_ref, dkt_hbm, dvt_hbm, db_ref, cs_ref = outs
        k_ref, v_ref, t_ref, db_acc, dkt_acc, dvt_acc, s_ref, dp_ref, p_ref, ds_ref, sems = scratch
        g = pl.program_id(0)

        load_k, load_v, load_t = _resident_copies(qkv_hbm, t_hbm, k_ref, v_ref, t_ref, sems)

        @pl.when(g == 0)
        def _():
            _start_resident((load_k, load_v, load_t), k_ref, v_ref)
            db_acc[...] = jnp.zeros_like(db_acc)
            dkt_acc[...] = jnp.zeros_like(dkt_acc)
            dvt_acc[...] = jnp.zeros_like(dvt_acc)
            cs_ref[...] = jnp.zeros_like(cs_ref)
            load_k.wait()
            load_v.wait()

        window = pl.ds(pl.multiple_of(g * GROUP_Q, GROUP_Q), GROUP_K)
        before_start = _before_start(g)
        for h in range(N_HEADS):
            cols = slice(h * HEAD_DIM, (h + 1) * HEAD_DIM)
            buf = h % 2
            qh, doh = q_ref[:, cols], do_ref[:, cols]
            kh, vh = k_ref[window, cols], v_ref[window, cols]
            s_ref[buf] = lax.dot_general(qh, kh, _DIMS["nt"], preferred_element_type=F32)
            dp_ref[buf] = lax.dot_general(doh, vh, _DIMS["nt"], preferred_element_type=F32)
            if h == 0:
                pl.when(g == 0)(load_t.wait)
            for row in range(0, GROUP_Q, SOFTMAX_ROWS):
                p_halves, ds_halves = [], []
                for r in (row, row + 8):
                    p = _softmax_rows(s_ref.at[buf], t_ref, h, before_start, slice(r, r + 8))
                    dp = dp_ref[buf, r:r + 8, :]
                    ds = p * (dp - jnp.sum(dp * p, axis=-1, keepdims=True))
                    chunk = r // CHUNK
                    shift = (GROUP - 1 - chunk) * CHUNK
                    db_acc[h, r - chunk * CHUNK:r - chunk * CHUNK + 8, shift:shift + GROUP_K] += ds
                    p_halves.append(p)
                    ds_halves.append(ds * (HEAD_DIM ** -0.5))
                p_ref[buf, row:row + SOFTMAX_ROWS, :] = jnp.concatenate(p_halves, axis=0).astype(BF16)
                ds_ref[buf, row:row + SOFTMAX_ROWS, :] = jnp.concatenate(ds_halves, axis=0).astype(BF16)
            dq_ref[:, cols] = jnp.dot(ds_ref[buf], kh, preferred_element_type=F32).astype(BF16)
            dkt_acc[cols, window] += lax.dot_general(qh, ds_ref[buf], _DIMS["tn"], preferred_element_type=F32)
            dvt_acc[cols, window] += lax.dot_general(doh, p_ref[buf], _DIMS["tn"], preferred_element_type=F32)
        cs_ref[0:1, :] += _colsum(dq_ref[...].astype(F32))

        @pl.when(g == n_groups - 1)
        def _():
            lo = (GROUP - 1) * CHUNK
            for h in range(N_HEADS):
                db_ref[h] = db_acc[h, :, lo:lo + BAND]
            inside = pl.ds(PAD_ROWS, seq)
            on_diagonal = (lax.broadcasted_iota(jnp.int32, (D_ATTN, D_ATTN), 0)
                           == lax.broadcasted_iota(jnp.int32, (D_ATTN, D_ATTN), 1))
            for row, acc in ((1, dkt_acc), (2, dvt_acc)):
                column = jnp.sum(acc[:, inside], axis=1, keepdims=True)
                cs_ref[row:row + 1, :] = _colsum(jnp.where(on_diagonal, column, 0.0))
            out_k = pltpu.make_async_copy(dkt_acc.at[:, inside], dkt_hbm, sems.at[0])
            out_v = pltpu.make_async_copy(dvt_acc.at[:, inside], dvt_hbm, sems.at[1])
            out_k.start()
            out_v.start()
            out_k.wait()
            out_v.wait()

    t_shape = (D_ATTN, seq + PAD_ROWS)
    outs, extra = _host_call(
        body, "attn_bwd", grid=(n_groups,),
        in_specs=[_tile(GROUP_Q, D_ATTN), _tile(GROUP_Q, D_ATTN), ANY, ANY],
        out_specs=[_tile(GROUP_Q, D_ATTN), ANY, ANY, _full((N_HEADS, CHUNK, BAND)), _full((8, D_ATTN))],
        out_shape=[jax.ShapeDtypeStruct((seq, D_ATTN), BF16), jax.ShapeDtypeStruct((D_ATTN, seq), F32),
                   jax.ShapeDtypeStruct((D_ATTN, seq), F32), jax.ShapeDtypeStruct((N_HEADS, CHUNK, BAND), F32),
                   jax.ShapeDtypeStruct((8, D_ATTN), F32)],
        scratch_shapes=[pltpu.VMEM((seq + PAD_ROWS, D_ATTN), BF16), pltpu.VMEM((seq + PAD_ROWS, D_ATTN), BF16),
                        pltpu.VMEM(table.shape, F32), pltpu.VMEM((N_HEADS, CHUNK, fold_w), F32), pltpu.VMEM(t_shape, F32),
                        pltpu.VMEM(t_shape, F32), pltpu.VMEM((2, GROUP_Q, GROUP_K), F32),
                        pltpu.VMEM((2, GROUP_Q, GROUP_K), F32), pltpu.VMEM((2, GROUP_Q, GROUP_K), BF16),
                        pltpu.VMEM((2, GROUP_Q, GROUP_K), BF16), pltpu.SemaphoreType.DMA((3,))],
        args=[qkv, dao, qkv, table], comm=comm)
    return outs, extra


def _dk_dv(dkt, dvt, seq):
    rows = 512
    transposed = pl.BlockSpec((D_ATTN, rows), lambda i: (0, i))

    def body(dkt_ref, dvt_ref, dk_ref, dv_ref):
        dk_ref[...] = dkt_ref[...].T.astype(BF16)
        dv_ref[...] = dvt_ref[...].T.astype(BF16)

    return pl.pallas_call(
        body, name="dk_dv", out_shape=[jax.ShapeDtypeStruct((seq, D_ATTN), BF16)] * 2, grid=(seq // rows,),
        in_specs=[transposed, transposed], out_specs=[_tile(rows, D_ATTN)] * 2, compiler_params=_cparams(1),
    )(dkt, dvt)


CONV_ROWS = 256


def _ln_silu(u1, g, b):
    mu = jnp.mean(u1, axis=-1, keepdims=True)
    xc = u1 - mu
    rs = lax.rsqrt(jnp.mean(xc * xc, axis=-1, keepdims=True) + EPS)
    xhat = xc * rs
    u2 = xhat * g + b
    return xhat, rs, u2


def _glu_into(s_ref, a_ref, b_ref, ah_ref, bh_ref, first):
    halo = ah_ref[...] * _sig(bh_ref[...])
    s_ref[0:CONV_HALO, :] = jnp.where(first, 0.0, halo)
    s_ref[CONV_HALO:CONV_HALO + CONV_ROWS, :] = a_ref[...] * _sig(b_ref[...])


CONV_LANES = 128
CONV_TILES = CONV_ROWS // 8


def _lag_weights(w_ref, lanes):
    return {e: jnp.broadcast_to(w_ref[CONV_K - 1 - e:CONV_K - e, lanes], (8, CONV_LANES)) for e in range(CONV_K)}


def _class_sums(w, tiles, k):
    total = None
    for a, tile in enumerate(tiles):
        if 8 * a + k < CONV_K:
            term = w[8 * a + k] * tile
            total = term if total is None else total + term
    return total


def _conv_back(src_ref, first_tile, w, lanes, row_id, emit):
    before = None
    for m in range(-1, CONV_TILES):
        tiles = [src_ref[8 * (first_tile + m - a):8 * (first_tile + m - a) + 8, lanes] for a in range(4)]
        rolled = [None] + [pltpu.roll(_class_sums(w, tiles, k), k, 0) for k in range(1, 8)]
        if m >= 0:
            out = _class_sums(w, tiles, 0)
            for k in range(1, 8):
                out = out + jnp.where(row_id < k, before[k], rolled[k])
            emit(m, out)
        before = rolled


def _conv_ahead(src_ref, w, lanes, row_id, emit):
    before = None
    for m in range(CONV_TILES + 1):
        tiles = [src_ref[8 * (m + a):8 * (m + a) + 8, lanes] for a in range(4)]
        rolled = [None] + [pltpu.roll(_class_sums(w, tiles, k), 8 - k, 0) for k in range(1, 8)]
        if m >= 1:
            out = before[0]
            for k in range(1, 8):
                out = out + jnp.where(row_id < 8 - k, before[k], rolled[k])
            emit(m - 1, out)
        before = [_class_sums(w, tiles, 0) if m < CONV_TILES else None] + rolled[1:]


def _conv_weight_sums(d_ref, s_ref, lanes, row_id, whole_shifts):
    zero = jnp.zeros((8, CONV_LANES), F32)
    sums = {8 * a + k: zero for a in whole_shifts for k in range(8) if 8 * a + k < CONV_K}

    def d_tile(m):
        return d_ref[8 * m:8 * m + 8, lanes] if 0 <= m < CONV_TILES else zero

    rolled = [None] + [zero] * 7
    for m in range(-1, CONV_TILES):
        cur, nxt = d_tile(m), d_tile(m + 1)
        rolled_next = [None] + [pltpu.roll(nxt, 8 - k, 0) for k in range(1, 8)]
        shifted = [cur] + [jnp.where(row_id < 8 - k, rolled[k], rolled_next[k]) for k in range(1, 8)]
        for a in whole_shifts:
            tile = s_ref[8 * (CONV_HALO // 8 + m - a):8 * (CONV_HALO // 8 + m - a) + 8, lanes]
            for k in range(8):
                if 8 * a + k < CONV_K and not (m < 0 and k == 0):
                    sums[8 * a + k] = sums[8 * a + k] + shifted[k] * tile
        rolled = rolled_next
    return sums


def _conv_fwd(zr, w_dw, b_dw, g_ln, b_ln, comm=None):
    seq = zr.shape[0]

    def body(a_ref, b_ref, ah_ref, bh_ref, w_ref, bias_ref, g_ref, bl_ref, u1_ref, u3_ref, s_ref):
        _glu_into(s_ref, a_ref, b_ref, ah_ref, bh_ref, pl.program_id(0) == 0)
        row_id = lax.broadcasted_iota(jnp.int32, (8, CONV_LANES), 0)
        for lo in range(0, D_CONV, CONV_LANES):
            lanes = slice(lo, lo + CONV_LANES)
            bias = jnp.broadcast_to(bias_ref[:, lanes], (8, CONV_LANES))

            def emit(m, out, lanes=lanes, bias=bias):
                u1_ref[8 * m:8 * m + 8, lanes] = out + bias

            _conv_back(s_ref, CONV_HALO // 8, _lag_weights(w_ref, lanes), lanes, row_id, emit)
        _, _, u2 = _ln_silu(u1_ref[...], g_ref[...], bl_ref[...])
        u3_ref[...] = (u2 * _sig(u2)).astype(BF16)

    return _host_call(
        lambda ins, outs, scratch: body(*ins, *outs, *scratch), "conv_fwd", grid=(seq // CONV_ROWS,),
        in_specs=[_tile(CONV_ROWS, D_CONV, 0), _tile(CONV_ROWS, D_CONV, 1),
                  _prev(CONV_HALO, D_CONV, CONV_ROWS, 0), _prev(CONV_HALO, D_CONV, CONV_ROWS, 1),
                  _full((CONV_K, D_CONV)), _full((1, D_CONV)), _full((1, D_CONV)), _full((1, D_CONV))],
        out_specs=[_tile(CONV_ROWS, D_CONV), _tile(CONV_ROWS, D_CONV)],
        out_shape=[jax.ShapeDtypeStruct((seq, D_CONV), F32), jax.ShapeDtypeStruct((seq, D_CONV), BF16)],
        scratch_shapes=[pltpu.VMEM((CONV_HALO + CONV_ROWS, D_CONV), F32)],
        args=[zr, zr, zr, zr, w_dw, b_dw, g_ln, b_ln], comm=comm)


def _conv_bwd(zr, u1, du3, w_dw, g_ln, b_ln, comm=None):
    seq = zr.shape[0]
    n_tiles = seq // CONV_ROWS
    n_halo = seq // CONV_HALO
    ext = CONV_ROWS + CONV_HALO

    def body(a_ref, b_ref, ah_ref, bh_ref, u1_ref, u1n_ref, d3_ref, d3n_ref, w_ref, g_ref, bl_ref,
             da_ref, db_ref, dw_ref, small_ref, s_ref, d_ref, du0_ref):
        i = pl.program_id(0)

        @pl.when(i == 0)
        def _():
            dw_ref[...] = jnp.zeros_like(dw_ref)
            small_ref[...] = jnp.zeros_like(small_ref)

        _glu_into(s_ref, a_ref, b_ref, ah_ref, bh_ref, i == 0)
        gv, bv = g_ref[...], bl_ref[...]

        def du1_of(u1, d3):
            xhat, rs, u2 = _ln_silu(u1, gv, bv)
            sg = _sig(u2)
            du2 = d3 * (sg * (1.0 + u2 * (1.0 - sg)))
            dxh = du2 * gv
            du1 = rs * (dxh - jnp.mean(dxh, axis=-1, keepdims=True)
                        - xhat * jnp.mean(dxh * xhat, axis=-1, keepdims=True))
            return du1, du2, xhat

        du1, du2, xhat = du1_of(u1_ref[...], d3_ref[...])
        du1n, _, _ = du1_of(u1n_ref[...], d3n_ref[...])
        d_ref[0:CONV_ROWS, :] = du1
        d_ref[CONV_ROWS:ext, :] = jnp.where(i == n_tiles - 1, 0.0, du1n)
        small_ref[0:1, :] += _colsum(du1)
        small_ref[1:2, :] += _colsum(du2 * xhat)
        small_ref[2:3, :] += _colsum(du2)
        row_id = lax.broadcasted_iota(jnp.int32, (8, CONV_LANES), 0)
        for lo in range(0, D_CONV, CONV_LANES):
            lanes = slice(lo, lo + CONV_LANES)

            def emit(m, out, lanes=lanes):
                du0_ref[8 * m:8 * m + 8, lanes] = out

            _conv_ahead(d_ref, _lag_weights(w_ref, lanes), lanes, row_id, emit)
            for whole_shifts in ((0, 1), (2, 3)):
                for e, total in _conv_weight_sums(d_ref, s_ref, lanes, row_id, whole_shifts).items():
                    dw_ref[CONV_K - 1 - e:CONV_K - e, lanes] += _colsum(total)
        du0 = du0_ref[...]
        sb = _sig(b_ref[...])
        da = du0 * sb
        dbv = du0 * a_ref[...] * sb * (1.0 - sb)
        da_ref[...] = da.astype(BF16)
        db_ref[...] = dbv.astype(BF16)
        small_ref[3:4, :] += _colsum(da)
        small_ref[4:5, :] += _colsum(dbv)

    return _host_call(
        lambda ins, outs, scratch: body(*ins, *outs, *scratch), "conv_bwd", grid=(n_tiles,),
        in_specs=[_tile(CONV_ROWS, D_CONV, 0), _tile(CONV_ROWS, D_CONV, 1),
                  _prev(CONV_HALO, D_CONV, CONV_ROWS, 0), _prev(CONV_HALO, D_CONV, CONV_ROWS, 1),
                  _tile(CONV_ROWS, D_CONV), _next(CONV_HALO, D_CONV, CONV_ROWS, n_halo),
                  _tile(CONV_ROWS, D_CONV), _next(CONV_HALO, D_CONV, CONV_ROWS, n_halo),
                  _full((CONV_K, D_CONV)), _full((1, D_CONV)), _full((1, D_CONV))],
        out_specs=[_tile(CONV_ROWS, D_CONV), _tile(CONV_ROWS, D_CONV), _full((CONV_HALO, D_CONV)),
                   _full((8, D_CONV))],
        out_shape=[jax.ShapeDtypeStruct((seq, D_CONV), BF16), jax.ShapeDtypeStruct((seq, D_CONV), BF16),
                   jax.ShapeDtypeStruct((CONV_HALO, D_CONV), F32), jax.ShapeDtypeStruct((8, D_CONV), F32)],
        scratch_shapes=[pltpu.VMEM((ext, D_CONV), F32), pltpu.VMEM((ext, D_CONV), F32),
                        pltpu.VMEM((CONV_ROWS, D_CONV), F32)],
        args=[zr, zr, zr, zr, u1, u1, du3, du3, w_dw, g_ln, b_ln], comm=comm)


MERGE_ROWS = 256


def _merge_fwd(ao, u3, zr, w_ao, w_co, b_co):
    seq = ao.shape[0]

    def body(ao_ref, u3_ref, ga_ref, gb_ref, wa_ref, wc_ref, bc_ref, y_ref, a_ref, cb_ref):
        a = jnp.dot(ao_ref[...], wa_ref[...], preferred_element_type=F32)
        cb = jnp.dot(u3_ref[...], wc_ref[...], preferred_element_type=F32) + bc_ref[...]
        a_ref[...] = a
        cb_ref[...] = cb
        y_ref[...] = (_sig(ga_ref[...]) * a + _sig(gb_ref[...]) * cb).astype(BF16)

    f32_out = jax.ShapeDtypeStruct((seq, D_MODEL), F32)
    return pl.pallas_call(
        body, name="merge_fwd",
        out_shape=[jax.ShapeDtypeStruct((seq, D_MODEL), BF16), f32_out, f32_out],
        grid=(seq // MERGE_ROWS,),
        in_specs=[_tile(MERGE_ROWS, D_ATTN), _tile(MERGE_ROWS, D_CONV), _tile(MERGE_ROWS, D_MODEL, 1),
                  _tile(MERGE_ROWS, D_MODEL, 2), _full(w_ao.shape), _full(w_co.shape), _full((1, D_MODEL))],
        out_specs=[_tile(MERGE_ROWS, D_MODEL)] * 3, compiler_params=_cparams(1),
    )(ao, u3, zr, zr, w_ao, w_co, b_co)


def _merge_bwd(a, cb, zr, rows):
    seq = a.shape[0]

    def fn(dy_v, first, ins, outs):
        a_ref, cb_ref, ga_ref, gb_ref = ins
        da_ref, dcb_ref, dga_ref, dgb_ref, small_ref = outs

        @pl.when(first)
        def _():
            small_ref[...] = jnp.zeros_like(small_ref)

        sa, sb = _sig(ga_ref[...]), _sig(gb_ref[...])
        dcb = dy_v * sb
        dga = dy_v * a_ref[...] * sa * (1.0 - sa)
        dgb = dy_v * cb_ref[...] * sb * (1.0 - sb)
        da_ref[...] = (dy_v * sa).astype(BF16)
        dcb_ref[...] = dcb.astype(BF16)
        dga_ref[...] = dga.astype(BF16)
        dgb_ref[...] = dgb.astype(BF16)
        small_ref[0:1, :] += _colsum(dga)
        small_ref[1:2, :] += _colsum(dgb)
        small_ref[2:3, :] += _colsum(dcb)

    bf = jax.ShapeDtypeStruct((seq, D_MODEL), BF16)
    gate = lambda col: pl.BlockSpec((rows, D_MODEL), lambda i, j: (i, col))
    return _Epilogue(
        [a, cb, zr, zr], [_row_tile(rows, D_MODEL), _row_tile(rows, D_MODEL), gate(1), gate(2)],
        [bf, bf, bf, bf, jax.ShapeDtypeStruct((8, D_MODEL), F32)],
        [_row_tile(rows, D_MODEL)] * 4 + [_whole((8, D_MODEL))], fn, keep_product=False)


FFN_ROWS = 2048
FFN_BLOCKS = D_FF // FFN_COLS
GELU_C = math.sqrt(2.0 / math.pi)


def _gelu(v):
    t = jnp.tanh(GELU_C * (v + 0.044715 * (v * v * v)))
    return 0.5 * v * (1.0 + t), t


def _gelu_grad(v, t):
    return 0.5 * (1.0 + t) + 0.5 * v * (1.0 - t * t) * (GELU_C * (1.0 + 3.0 * 0.044715 * (v * v)))


def _sublane_rows(ref, n):
    return [jnp.broadcast_to(ref[r:r + 1, :], (8, FFN_COLS)) for r in range(n)]


def _rolls(tile, shifts):
    return tuple(pltpu.roll(tile, s, 0) for s in shifts)


def _behind(prev_rolls, cur, row_id):
    rolls = _rolls(cur, (1, 2))
    x1 = jnp.where(row_id < 1, prev_rolls[0], rolls[0])
    x2 = jnp.where(row_id < 2, prev_rolls[1], rolls[1])
    return (x2, x1, cur), rolls


def _ahead(cur_rolls, next_rolls, row_id):
    return (jnp.where(row_id < 7, cur_rolls[0], next_rolls[0]), jnp.where(row_id < 6, cur_rolls[1], next_rolls[1]))


def _conv3(taps, w, bias):
    return w[0] * taps[0] + w[1] * taps[1] + w[2] * taps[2] + bias


def _ffn_specs(rows):
    tile = lambda off: pl.BlockSpec((rows, FFN_COLS), lambda j, i: (i, j + off))
    prev = lambda off: pl.BlockSpec((FFN_HALO, FFN_COLS),
                                    lambda j, i: (jnp.maximum(i * (rows // FFN_HALO) - 1, 0), j + off))
    wgt = lambda off: pl.BlockSpec((3, FFN_COLS), lambda j, i: (0, j + off))
    vec = lambda off: pl.BlockSpec((1, FFN_COLS), lambda j, i: (0, j + off))
    return tile, prev, wgt, vec


def _ffn_act(up, w_dw, b_dw):
    seq = up.shape[0]
    tile, prev, wgt, vec = _ffn_specs(FFN_ROWS)

    def body(v_ref, g_ref, vp_ref, gp_ref, wv_ref, wg_ref, bv_ref, bg_ref, act_ref):
        first = pl.program_id(1) == 0
        row_id = lax.broadcasted_iota(jnp.int32, (8, FFN_COLS), 0)
        wv, wg = _sublane_rows(wv_ref, 3), _sublane_rows(wg_ref, 3)
        (bv,), (bg,) = _sublane_rows(bv_ref, 1), _sublane_rows(bg_ref, 1)
        rolls_v = _rolls(jnp.where(first, 0.0, vp_ref[...]), (1, 2))
        rolls_g = _rolls(jnp.where(first, 0.0, gp_ref[...]), (1, 2))
        for row in range(0, FFN_ROWS, 16):
            halves = []
            for r in (row, row + 8):
                taps_v, rolls_v = _behind(rolls_v, v_ref[r:r + 8, :], row_id)
                taps_g, rolls_g = _behind(rolls_g, g_ref[r:r + 8, :], row_id)
                halves.append(_gelu(_conv3(taps_g, wg, bg))[0] * _conv3(taps_v, wv, bv))
            act_ref[row:row + 16, :] = jnp.concatenate(halves, axis=0).astype(BF16)

    return pl.pallas_call(
        body, name="ffn_act", out_shape=jax.ShapeDtypeStruct((seq, D_FF), BF16),
        grid=(FFN_BLOCKS, seq // FFN_ROWS),
        in_specs=[tile(0), tile(FFN_BLOCKS), prev(0), prev(FFN_BLOCKS), wgt(0), wgt(FFN_BLOCKS),
                  vec(0), vec(FFN_BLOCKS)],
        out_specs=tile(0), compiler_params=_cparams(2),
    )(up, up, up, up, w_dw, w_dw, b_dw, b_dw)


def _ffn_act_bwd(up, dact, w_dw, b_dw, comm=None):
    seq = up.shape[0]
    n_tiles = seq // FFN_ROWS
    n_halo = seq // FFN_HALO
    tile, prev, wgt, vec = _ffn_specs(FFN_ROWS)
    nxt = lambda off: pl.BlockSpec(
        (FFN_HALO, FFN_COLS), lambda j, i: (jnp.minimum((i + 1) * (FFN_ROWS // FFN_HALO), n_halo - 1), j + off))
    acc = lambda off: pl.BlockSpec((8, FFN_COLS), lambda j, i: (0, j + off))

    def body(v_ref, g_ref, vp_ref, gp_ref, vn_ref, gn_ref, da_ref, dan_ref, wv_ref, wg_ref, bv_ref, bg_ref,
             dv_out, dg_out, dwv_ref, dwg_ref, dbv_ref, dbg_ref):
        i = pl.program_id(1)
        first, last = i == 0, i == n_tiles - 1

        @pl.when(first)
        def _():
            for r in (dwv_ref, dwg_ref, dbv_ref, dbg_ref):
                r[...] = jnp.zeros_like(r)

        row_id = lax.broadcasted_iota(jnp.int32, (8, FFN_COLS), 0)
        wv, wg = _sublane_rows(wv_ref, 3), _sublane_rows(wg_ref, 3)
        (bv,), (bg,) = _sublane_rows(bv_ref, 1), _sublane_rows(bg_ref, 1)
        zero = jnp.zeros((8, FFN_COLS), F32)
        sums_v, sums_g = [zero] * 4, [zero] * 4
        rolls_v = _rolls(jnp.where(first, 0.0, vp_ref[...]), (1, 2))
        rolls_g = _rolls(jnp.where(first, 0.0, gp_ref[...]), (1, 2))
        behind = None
        done_v, done_g = [], []

        def grads(v_tile, g_tile, dact, rolls_v, rolls_g):
            taps_v, rolls_v = _behind(rolls_v, v_tile, row_id)
            taps_g, rolls_g = _behind(rolls_g, g_tile, row_id)
            val, gate = _conv3(taps_v, wv, bv), _conv3(taps_g, wg, bg)
            gel, t = _gelu(gate)
            return dact * gel, dact * val * _gelu_grad(gate, t), taps_v, taps_g, rolls_v, rolls_g

        def finish(tile, nxt, row):
            for (d, d_rolls), (_, n_rolls), w, done, o_ref in ((tile[0], nxt[0], wv, done_v, dv_out),
                                                               (tile[1], nxt[1], wg, done_g, dg_out)):
                d1, d2 = _ahead(d_rolls, n_rolls, row_id)
                done.append(w[2] * d + w[1] * d1 + w[0] * d2)
                if len(done) == 2:
                    o_ref[row - 16:row, :] = jnp.concatenate(done, axis=0).astype(BF16)
                    done.clear()

        for row in range(0, FFN_ROWS, 16):
            dact16 = da_ref[row:row + 16, :].astype(F32)
            for r, dact in ((row, dact16[0:8, :]), (row + 8, dact16[8:16, :])):
                dval, dgate, taps_v, taps_g, rolls_v, rolls_g = grads(v_ref[r:r + 8, :], g_ref[r:r + 8, :], dact,
                                                                      rolls_v, rolls_g)
                sums_v = [s + dval * x for s, x in zip(sums_v, taps_v)] + [sums_v[3] + dval]
                sums_g = [s + dgate * x for s, x in zip(sums_g, taps_g)] + [sums_g[3] + dgate]
                tile = ((dval, _rolls(dval, (7, 6))), (dgate, _rolls(dgate, (7, 6))))
                if behind is not None:
                    finish(behind, tile, r)
                behind = tile
        dact_next = jnp.where(last, 0.0, dan_ref[...].astype(F32)[0:FFN_HALO, :])
        dval, dgate, *_ = grads(vn_ref[...], gn_ref[...], dact_next, rolls_v, rolls_g)
        finish(behind, ((dval, _rolls(dval, (7, 6))), (dgate, _rolls(dgate, (7, 6)))), FFN_ROWS)
        for sums, dw_ref, db_ref in ((sums_v, dwv_ref, dbv_ref), (sums_g, dwg_ref, dbg_ref)):
            for tap in range(3):
                dw_ref[tap:tap + 1, :] += _colsum(sums[tap])
            db_ref[0:1, :] += _colsum(sums[3])

    half = jax.ShapeDtypeStruct((seq, D_FF), BF16)
    acc_shape = jax.ShapeDtypeStruct((8, D_FF), F32)
    return _host_call(
        lambda ins, outs, scratch: body(*ins, *outs, *scratch), "ffn_act_bwd", grid=(FFN_BLOCKS, n_tiles),
        in_specs=[tile(0), tile(FFN_BLOCKS), prev(0), prev(FFN_BLOCKS), nxt(0), nxt(FFN_BLOCKS),
                  tile(0), pl.BlockSpec((16, FFN_COLS), lambda j, i: (
                      jnp.minimum((i + 1) * (FFN_ROWS // 16), seq // 16 - 1), j)),
                  wgt(0), wgt(FFN_BLOCKS), vec(0), vec(FFN_BLOCKS)],
        out_specs=[tile(0), tile(0), acc(0), acc(0), acc(0), acc(0)],
        out_shape=[half, half, acc_shape, acc_shape, acc_shape, acc_shape],
        scratch_shapes=[], args=[up, up, up, up, up, up, dact, dact, w_dw, w_dw, b_dw, b_dw], comm=comm)


def _cols_to_blocks(full_cols):
    k, n8 = full_cols.shape
    return jnp.transpose(full_cols.reshape(k, N_DEV, n8 // N_DEV), (1, 0, 2))


def _rows_to_blocks(full_rows):
    r8, n = full_rows.shape
    return full_rows.reshape(N_DEV, r8 // N_DEV, n)


def _blocks_to_cols(gathered):
    _, k, n = gathered.shape
    return jnp.transpose(gathered, (1, 0, 2)).reshape(k, N_DEV * n)


def kernel(x, c, w_ada, b_ada, g_pre_mix, g_post_mix, w_in, b_in, rel_bias, w_attn_o, w_dw_conv, b_dw_conv, g_conv_ln, b_conv_ln, w_conv_o, b_conv_o, w_mix_o, g_pre_ffn, g_post_ffn, w_up, w_dw_ffn, b_dw_ffn, w_down, loss_target, m_w_ada, m_b_ada, m_g_pre_mix, m_g_post_mix, m_w_in, m_b_in, m_rel_bias, m_w_attn_o, m_w_dw_conv, m_b_dw_conv, m_g_conv_ln, m_b_conv_ln, m_w_conv_o, m_b_conv_o, m_w_mix_o, m_g_pre_ffn, m_g_post_ffn, m_w_up, m_w_dw_ffn, m_b_dw_ffn, m_w_down, v_w_ada, v_b_ada, v_g_pre_mix, v_g_post_mix, v_w_in, v_b_in, v_rel_bias, v_w_attn_o, v_w_dw_conv, v_b_dw_conv, v_g_conv_ln, v_b_conv_ln, v_w_conv_o, v_b_conv_o, v_w_mix_o, v_g_pre_ffn, v_g_post_ffn, v_w_up, v_w_dw_ffn, v_b_dw_ffn, v_w_down):
    names = ["w_ada", "b_ada", "g_pre_mix", "g_post_mix", "w_in", "b_in", "rel_bias", "w_attn_o", "w_dw_conv",
             "b_dw_conv", "g_conv_ln", "b_conv_ln", "w_conv_o", "b_conv_o", "w_mix_o", "g_pre_ffn", "g_post_ffn",
             "w_up", "w_dw_ffn", "b_dw_ffn", "w_down"]
    weights = dict(zip(names, [w_ada, b_ada, g_pre_mix, g_post_mix, w_in, b_in, rel_bias, w_attn_o, w_dw_conv,
                               b_dw_conv, g_conv_ln, b_conv_ln, w_conv_o, b_conv_o, w_mix_o, g_pre_ffn,
                               g_post_ffn, w_up, w_dw_ffn, b_dw_ffn, w_down]))
    mom_m = dict(zip(names, [m_w_ada, m_b_ada, m_g_pre_mix, m_g_post_mix, m_w_in, m_b_in, m_rel_bias, m_w_attn_o,
                             m_w_dw_conv, m_b_dw_conv, m_g_conv_ln, m_b_conv_ln, m_w_conv_o, m_b_conv_o,
                             m_w_mix_o, m_g_pre_ffn, m_g_post_ffn, m_w_up, m_w_dw_ffn, m_b_dw_ffn, m_w_down]))
    mom_v = dict(zip(names, [v_w_ada, v_b_ada, v_g_pre_mix, v_g_post_mix, v_w_in, v_b_in, v_rel_bias, v_w_attn_o,
                             v_w_dw_conv, v_b_dw_conv, v_g_conv_ln, v_b_conv_ln, v_w_conv_o, v_b_conv_o,
                             v_w_mix_o, v_g_pre_ffn, v_g_post_ffn, v_w_up, v_w_dw_ffn, v_b_dw_ffn, v_w_down]))
    shapes = {n: w.shape for n, w in weights.items()}

    seq = x.shape[1]
    me = 4 * lax.axis_index("x") + 2 * lax.axis_index("y") + lax.axis_index("c")
    x2 = x.reshape(seq, D_MODEL)
    target = loss_target.reshape(seq, D_MODEL)
    sq = lambda a: a.reshape(a.shape[1:])
    bf = lambda a: sq(a).astype(BF16)

    transposed = lambda a: jnp.swapaxes(sq(a), 0, 1)

    c_all, mod_all = _ada_mod(c, sq(w_ada))
    c_all = c_all.reshape(N_DEV, D_MODEL)
    mod = lax.dynamic_index_in_dim(mod_all, me, axis=1, keepdims=False)
    mod6 = (mod.reshape(1, 6 * D_MODEL) + b_ada).reshape(6, D_MODEL)

    h1, (g_in, g_dwc, g_dwf) = _pre_mix(
        x2, mod6, g_pre_mix, comm=_gather_comm([transposed(w_in).astype(BF16), sq(w_dw_conv), sq(w_dw_ffn)]))
    wt_in = g_in.reshape(g_in.shape[0] * g_in.shape[1], D_MODEL)
    wf_dwc = _blocks_to_cols(g_dwc)
    wf_dwf = _blocks_to_cols(g_dwf)
    qkv = _mm(h1, wt_in, "nt", BF16, "in_proj_qkv", bias=b_in, tm=1024, tn=768, cols=(0, 3 * D_ATTN))
    zr, _, (g_ao, g_co, g_mo) = _mm(h1, wt_in, "nt", F32, "in_proj_rest", bias=b_in, tm=1024, tn=3 * D_ATTN,
                                 cols=(3 * D_ATTN, 2 * D_CONV + 2 * D_MODEL),
                                 comm=_gather_comm([bf(w_attn_o), bf(w_conv_o), bf(w_mix_o)]))
    table = jnp.transpose(_bias_table(sq(rel_bias)), (1, 0, 2))
    ao, (g_up,) = _attn_fwd(qkv, table, comm=_gather_comm([transposed(w_up).astype(BF16)]))
    (u1, u3), (g_dn,) = _conv_fwd(zr, wf_dwc, b_dw_conv, g_conv_ln, b_conv_ln, comm=_gather_comm([bf(w_down)]))
    wf_ao = _blocks_to_cols(g_ao)
    wf_co = _blocks_to_cols(g_co)
    wf_mo = g_mo.reshape(D_MODEL, D_MODEL)
    wt_up = g_up.reshape(g_up.shape[0] * g_up.shape[1], D_MODEL)
    wf_dn = g_dn.reshape(D_FF, D_MODEL)
    y, a_br, cb_br = _merge_fwd(ao, u3, zr, wf_ao, wf_co, b_conv_o)
    ymix, (x1, h2), _ = _mm(y, wf_mo, "nn", F32, "mix_o", tm=512, tn=D_MODEL,
                            epilogue=_post_mix_pre_ffn(x2, mod6, g_post_mix, g_pre_ffn, 512))
    up = _mm(h2, wt_up, "nt", F32, "ffn_up", tm=1024, tn=1408)
    act = _ffn_act(up, wf_dwf, b_dw_ffn)
    _, (loss_lanes, dout, dyf, small_f), _ = _mm(act, wf_dn, "nn", F32, "ffn_down", tm=512, tn=D_MODEL,
                                                 epilogue=_final(x1, target, mod6, g_post_ffn, 512))

    dact = _mm(dyf, wf_dn, "nt", BF16, "ffn_down_dx", tm=1024, tn=1408)
    gw_down = _mm(act, dyf, "tn", BF16, "ffn_down_dw", tm=256, tn=1024)
    (dup_v, dup_g, dwv, dwg, dbv, dbg), (parts_down,) = _ffn_act_bwd(
        up, dact, wf_dwf, b_dw_ffn, comm=_scatter_comm([_rows_to_blocks(gw_down)]))
    _, (dx1, dymix, small_m), _ = _mm([dup_v, dup_g], wt_up, "nn", F32, "ffn_up_dx", tm=512, tn=D_MODEL,
                                      epilogue=_mid_bwd(x1, dout, ymix, mod6, g_pre_ffn, g_post_mix, 512))
    blocks_up = _rows_to_blocks(_mm_tn_rows([dup_v, dup_g], h2, "ffn_up_dw"))
    _, (da, dcb, dga, dgb, small_g), _ = _mm(dymix, wf_mo, "nt", F32, "mix_o_dx", tm=512, tn=D_MODEL,
                                             epilogue=_merge_bwd(a_br, cb_br, zr, 512))
    gw_mo = _mm(y, dymix, "tn", BF16, "mix_o_dw")
    dao = _mm(da, wf_ao, "nt", BF16, "attn_o_dx", tm=1024)
    gw_ao = _mm(ao, da, "tn", BF16, "attn_o_dw")
    du3 = _mm(dcb, wf_co, "nt", F32, "conv_o_dx", tm=1024)
    gw_co = _mm(u3, dcb, "tn", BF16, "conv_o_dw")
    (dq, dkt, dvt, dbias, small_a), (parts_up,) = _attn_bwd(
        qkv, table, dao, comm=_scatter_comm([blocks_up]))
    g_rel = _bias_grad(jnp.transpose(dbias, (1, 0, 2)))
    (dglu_a, dglu_b, dw_conv, small_c), (parts_mo, parts_ao, parts_co) = _conv_bwd(
        zr, u1, du3, wf_dwc, g_conv_ln, b_conv_ln,
        comm=_scatter_comm([_rows_to_blocks(gw_mo), _cols_to_blocks(gw_ao), _cols_to_blocks(gw_co)]))
    dk, dv = _dk_dv(dkt, dvt, seq)
    dz = [dq, dk, dv, dglu_a, dglu_b, dga, dgb]
    dz_halves = [(p, c) for p in dz for c in range(p.shape[1] // D_ATTN)]
    blocks_in = _rows_to_blocks(_mm_tn_rows(dz, h1, "in_proj_dw"))
    _, (grad_x, small_x), (parts_in, _, _) = _mm(dz_halves, wt_in, "nn", F32, "in_proj_dx", tm=512, tn=D_MODEL,
                                                 k_piece=D_ATTN,
                                                 comm=_pair_scatter_comm(blocks_in),
                                                 epilogue=_pre_mix_bwd(x2, dx1, mod6, g_pre_mix, 512))

    packed = _pack_grads(small_x, small_m, small_f, small_g, small_a, small_c, dbv, dbg, dwv, dwg, dw_conv)
    gathered, gathered_rel, gathered_loss = _run_comm(_gather_comm([packed, g_rel, loss_lanes]), "gather_small")
    gathered = gathered.reshape(N_DEV, PACKED_TOTAL)
    updates, g_dwc_full, g_dwf_full, loss_all = _small_adamw(gathered, gathered_rel, gathered_loss, weights, mom_m,
                                                             mom_v)
    loss = loss_all[0, 0]

    grads, deltas, new_m, new_v = {}, {}, {}, {}

    def record(name, update, is_transposed=False):
        for dst, val in zip((grads, deltas, new_m, new_v), update):
            dst[name] = (jnp.swapaxes(val, 0, 1) if is_transposed else val).reshape(shapes[name])

    for name, update in updates.items():
        record(name, update)

    def local_update(name, grad, view=sq):
        record(name, _adamw(view(weights[name]), view(mom_m[name]), view(mom_v[name]), "adamw_" + name, g=view(grad)))

    def taps_major(a):
        return a.reshape(a.shape[1], 1, a.shape[2])

    conv_cols, ffn_cols, ada_cols = D_CONV // N_DEV, 2 * D_FF // N_DEV, 6 * D_MODEL // N_DEV
    local_update("w_dw_conv", lax.dynamic_slice(g_dwc_full, (0, me * conv_cols), (CONV_K, conv_cols))[None], taps_major)
    local_update("w_dw_ffn", lax.dynamic_slice(g_dwf_full, (0, me * ffn_cols), (3, ffn_cols))[None], taps_major)
    local_update("w_ada", _ada_grad(c_all, lax.dynamic_slice(gathered, (0, me * ada_cols), (N_DEV, ada_cols)))[None])

    for name, part in (("w_attn_o", parts_ao), ("w_conv_o", parts_co), ("w_mix_o", parts_mo), ("w_down", parts_down)):
        record(name, _adamw(sq(weights[name]), sq(mom_m[name]), sq(mom_v[name]), "adamw_" + name, parts=part))
    for name, part in (("w_in", parts_in), ("w_up", parts_up)):
        record(name, _adamw(transposed(weights[name]), transposed(mom_m[name]), transposed(mom_v[name]),
                            "adamw_" + name, parts=part), is_transposed=True)

    return (loss, grad_x.reshape(x.shape), *[grads[n] for n in names], *[deltas[n] for n in names],
            *[new_m[n] for n in names], *[new_v[n] for n in names])
```

```python
import functools
import math

import jax
import jax.numpy as jnp
from jax import lax
from jax.experimental import pallas as pl
from jax.experimental.pallas import tpu as pltpu

F32 = jnp.float32
BF16 = jnp.bfloat16
HIGHEST = lax.Precision.HIGHEST

D_MODEL = 1024
CHUNK = 64
LEFT_CHUNKS = 8
BAND = (LEFT_CHUNKS + 1) * CHUNK
PAD_ROWS = LEFT_CHUNKS * CHUNK
GROUP = 4
GROUP_Q = GROUP * CHUNK
GROUP_K = GROUP_Q + PAD_ROWS
SOFTMAX_ROWS = 16
TOEPLITZ = 640
N_HEADS = 8
HEAD_DIM = 64
D_ATTN = 512
D_CONV = 512
CONV_K = 31
CONV_HALO = 32
MAX_REL = 128
N_REL = 2 * MAX_REL + 1
D_FF = 2816
FFN_HALO = 8
FFN_COLS = 256
EPS = 1e-6
NEG_INF = -1e30
N_DEV = 8

ADAM_LR = 0.001
ADAM_B1 = 0.9
ADAM_B2 = 0.999
ADAM_EPS = 1e-08
ADAM_WD = 0.01
ADAM_STEP = 10

VMEM_LIMIT_BYTES = 56 * 1024 * 1024
ADAMW_BLOCK_BYTES = 768 * 1024

MESH = pl.DeviceIdType.MESH
ANY = pl.BlockSpec(memory_space=pl.ANY)

SH_M, SC_M, GT_M, SH_F, SC_F, GT_F = range(6)

SMALL = (("b_ada", 6144), ("g_pre_mix", 1024), ("g_post_mix", 1024), ("b_in", 4608), ("b_dw_conv", 512),
         ("g_conv_ln", 512), ("b_conv_ln", 512), ("b_conv_o", 1024), ("g_pre_ffn", 1024), ("g_post_ffn", 1024),
         ("b_dw_ffn", 5632))
PACKED_TOTAL = sum(n for _, n in SMALL) + CONV_K * D_CONV + 3 * 2 * D_FF


def _cparams(n_axes):
    return pltpu.CompilerParams(vmem_limit_bytes=VMEM_LIMIT_BYTES,
                                dimension_semantics=("arbitrary",) * n_axes)


def _sig(v):
    return 1.0 / (1.0 + jnp.exp(-v))


def _pick(n, target):
    if n <= target:
        return n
    t = target - target % 128
    while n % t:
        t -= 128
    return t


def _tile(rows, cols, col=0):
    return pl.BlockSpec((rows, cols), lambda i: (i, col))


def _full(shape):
    zeros = (0,) * len(shape)
    return pl.BlockSpec(shape, lambda i: zeros)


def _prev(halo, cols, rows, col=0):
    return pl.BlockSpec((halo, cols), lambda i: (jnp.maximum(i * (rows // halo) - 1, 0), col))


def _next(halo, cols, rows, n_blocks, col=0):
    return pl.BlockSpec((halo, cols), lambda i: (jnp.minimum((i + 1) * (rows // halo), n_blocks - 1), col))


class _Comm:
    def __init__(self, inputs, out_shapes, sems, start, finish, relay=None, early=None):
        self.inputs, self.out_shapes, self.sems, self.start, self.finish = inputs, out_shapes, sems, start, finish
        self.relay, self.early = relay, early


def _host_call(body, name, grid, in_specs, out_specs, out_shape, scratch_shapes, args, comm=None):
    n_in, n_out, n_scr = len(args), len(out_shape), len(scratch_shapes)
    c_in = list(comm.inputs) if comm else []
    c_out = list(comm.out_shapes) if comm else []
    c_sem = list(comm.sems) if comm else []

    def full(*refs):
        bounds = [0, n_in, len(c_in), n_out, len(c_out), n_scr, len(c_sem)]
        cuts = [sum(bounds[:i + 1]) for i in range(len(bounds))]
        ins, cins, outs, couts, scr, csems = (refs[lo:hi] for lo, hi in zip(cuts[:-1], cuts[1:]))
        if comm:
            first = functools.reduce(jnp.logical_and, [pl.program_id(ax) == 0 for ax in range(len(grid))])
            pl.when(first)(lambda: comm.start(cins, couts, csems))
            if comm.early is not None:
                strides = [math.prod(grid[ax + 1:]) for ax in range(len(grid))]
                step = sum(pl.program_id(ax) * strides[ax] for ax in range(len(grid)))
                pl.when(step == 1)(lambda: comm.early(cins, couts, csems))
            last = functools.reduce(jnp.logical_and, [pl.program_id(ax) == grid[ax] - 1 for ax in range(len(grid))])
            if comm.relay is not None:
                pl.when(last)(lambda: comm.relay(cins, couts, csems))
        body(ins, outs, scr)
        if comm:
            pl.when(last)(lambda: comm.finish(cins, couts, csems))

    res = pl.pallas_call(
        full, name=name, grid=grid, in_specs=list(in_specs) + [ANY] * len(c_in),
        out_specs=list(out_specs) + [ANY] * len(c_out), out_shape=list(out_shape) + c_out,
        scratch_shapes=list(scratch_shapes) + c_sem, compiler_params=_cparams(len(grid)),
    )(*args, *c_in)
    return list(res[:n_out]), list(res[n_out:])


def _run_comm(comm, name):
    n_in, n_out = len(comm.inputs), len(comm.out_shapes)

    def body(*refs):
        ins, outs, sems = refs[:n_in], refs[n_in:n_in + n_out], refs[n_in + n_out:]
        comm.start(ins, outs, sems)
        if comm.relay is not None:
            comm.relay(ins, outs, sems)
        comm.finish(ins, outs, sems)

    return pl.pallas_call(
        body, name=name, out_shape=list(comm.out_shapes), in_specs=[ANY] * n_in, out_specs=[ANY] * n_out,
        scratch_shapes=list(comm.sems),
    )(*comm.inputs)


def _place():
    return lax.axis_index("x"), lax.axis_index("y"), lax.axis_index("c")


def _gather_comm(arrs):
    n = len(arrs)

    def plan(ins, outs, sems):
        send_sems, recv_sems, local_sems = sems
        x, y, c = _place()
        me, sibling = (x, y, c), (x, y, 1 - c)
        chips = [(1 - x, y), (x, 1 - y), (1 - x, 1 - y)]

        def block(k, p):
            return outs[k].at[4 * p[0] + 2 * p[1] + p[2]]

        def copy(k, s, blk, to, src=None):
            return pltpu.make_async_remote_copy(
                src_ref=block(k, blk) if src is None else src, dst_ref=block(k, blk),
                send_sem=send_sems.at[7 * k + s], recv_sem=recv_sems.at[7 * k + s],
                device_id=to, device_id_type=MESH)

        mine = [pltpu.make_async_copy(ins[k], block(k, me), local_sems.at[k]) for k in range(n)]
        first = []
        for k in range(n):
            first.append(copy(k, 0, me, sibling, src=ins[k]))
            for j, chip in enumerate(chips):
                first.append(copy(k, 1 + j, me, (*chip, c), src=ins[k]))
        return me, sibling, chips, c, copy, mine, first

    def start(ins, outs, sems):
        *_, mine, first = plan(ins, outs, sems)
        for cp in mine + first:
            cp.start()

    def relay(ins, outs, sems):
        me, sibling, chips, c, copy, _, _ = plan(ins, outs, sems)
        for j, chip in enumerate(chips):
            for k in range(n):
                copy(k, 1 + j, (*chip, c), me).wait_recv()
                copy(k, 4 + j, (*chip, c), sibling).start()

    def finish(ins, outs, sems):
        me, sibling, chips, c, copy, mine, first = plan(ins, outs, sems)
        passed = [copy(k, 4 + j, (*chip, c), sibling) for j, chip in enumerate(chips) for k in range(n)]
        for k in range(n):
            copy(k, 0, sibling, me).wait_recv()
        for j, chip in enumerate(chips):
            for k in range(n):
                copy(k, 4 + j, (*chip, 1 - c), me).wait_recv()
        for cp in first + passed:
            cp.wait_send()
        for cp in mine:
            cp.wait()

    comm = _Comm(list(arrs), [jax.ShapeDtypeStruct((N_DEV,) + a.shape, a.dtype) for a in arrs],
                 [pltpu.SemaphoreType.DMA((7 * n,)), pltpu.SemaphoreType.DMA((7 * n,)),
                  pltpu.SemaphoreType.DMA((n,))], start, finish, relay)
    comm.plan = plan
    return comm


def _scatter_comm(blocks):
    n = len(blocks)

    def plan(ins, outs, sems, arrivals):
        send_sems, recv_sems, local_sems = sems
        x, y, c = _place()
        me = 4 * x + 2 * y + c
        local = [pltpu.make_async_copy(ins[k].at[me], outs[k].at[me], local_sems.at[k]) for k in range(n)]
        sends, recvs = [], []
        for k in range(n):
            for mask in range(1, N_DEV):
                px = 1 - x if mask & 4 else x
                py = 1 - y if mask & 2 else y
                pc = 1 - c if mask & 1 else c
                peer = 4 * px + 2 * py + pc
                sem = 7 * k + mask - 1
                both = dict(send_sem=send_sems.at[sem], recv_sem=recv_sems.at[sem], device_id=(px, py, pc),
                            device_id_type=MESH)
                sends.append(pltpu.make_async_remote_copy(src_ref=ins[k].at[peer], dst_ref=outs[k].at[me], **both))
                if arrivals:
                    recvs.append(pltpu.make_async_remote_copy(src_ref=ins[k].at[me], dst_ref=outs[k].at[peer],
                                                              **both))
        return local, sends, recvs

    def start(ins, outs, sems):
        local, sends, _ = plan(ins, outs, sems, arrivals=False)
        for cp in local + sends:
            cp.start()

    def finish(ins, outs, sems):
        local, sends, recvs = plan(ins, outs, sems, arrivals=True)
        for cp in recvs:
            cp.wait_recv()
        for cp in sends:
            cp.wait_send()
        for cp in local:
            cp.wait()

    return _Comm(list(blocks), [jax.ShapeDtypeStruct(b.shape, b.dtype) for b in blocks],
                 [pltpu.SemaphoreType.DMA((7 * n,)), pltpu.SemaphoreType.DMA((7 * n,)),
                  pltpu.SemaphoreType.DMA((n,))], start, finish)


def _pair_scatter_comm(block):
    _, r, c = block.shape
    quarter = jax.ShapeDtypeStruct((4, r, c), block.dtype)

    def plan(ins, outs, sems):
        parts, got, pair = outs
        d2d_send, d2d_recv, ici_send, ici_recv, local, *bufs = sems
        x, y, cc = _place()
        mine = 2 * x + y
        chips = [(1 - x, y), (x, 1 - y), (1 - x, 1 - y)]
        to_sibling = [pltpu.make_async_remote_copy(
            src_ref=ins[0].at[2 * q + 1 - cc], dst_ref=got.at[q], send_sem=d2d_send.at[q], recv_sem=d2d_recv.at[q],
            device_id=(x, y, 1 - cc), device_id_type=MESH) for q in range(4)]
        to_chips = [pltpu.make_async_remote_copy(
            src_ref=pair.at[2 * px + py], dst_ref=parts.at[mine], send_sem=ici_send.at[j], recv_sem=ici_recv.at[j],
            device_id=(px, py, cc), device_id_type=MESH) for j, (px, py) in enumerate(chips)]
        from_chips = [pltpu.make_async_remote_copy(
            src_ref=pair.at[mine], dst_ref=parts.at[2 * px + py], send_sem=ici_send.at[j], recv_sem=ici_recv.at[j],
            device_id=(px, py, cc), device_id_type=MESH) for j, (px, py) in enumerate(chips)]
        own = pltpu.make_async_copy(pair.at[mine], parts.at[mine], local.at[5])
        order = [2 * px + py for px, py in chips] + [mine]
        return cc, got, pair, local, bufs, order, to_sibling, to_chips, from_chips, own

    def start(ins, outs, sems):
        for cp in plan(ins, outs, sems)[6]:
            cp.start()

    def early(ins, outs, sems):
        cc, got, pair, local, bufs, order, to_sibling, to_chips, _, own = plan(ins, outs, sems)
        for cp in to_sibling:
            cp.wait_recv()

        def loads(k):
            return [pltpu.make_async_copy(ins[0].at[2 * order[k] + cc], bufs[2 * (k % 2)], local.at[k % 2]),
                    pltpu.make_async_copy(got.at[order[k]], bufs[2 * (k % 2) + 1], local.at[2 + k % 2])]

        for cp in loads(0):
            cp.start()
        for k, send in enumerate(to_chips + [own]):
            if k + 1 < len(order):
                for cp in loads(k + 1):
                    cp.start()
            for cp in loads(k):
                cp.wait()
            kept, came = bufs[2 * (k % 2)], bufs[2 * (k % 2) + 1]
            kept[...] = (kept[...].astype(F32) + came[...].astype(F32)).astype(block.dtype)
            store = pltpu.make_async_copy(kept, pair.at[order[k]], local.at[4])
            store.start()
            store.wait()
            send.start()

    def finish(ins, outs, sems):
        *_, to_sibling, to_chips, from_chips, own = plan(ins, outs, sems)
        for cp in from_chips:
            cp.wait_recv()
        for cp in to_chips + to_sibling:
            cp.wait_send()
        own.wait()

    return _Comm([block], [quarter, quarter, quarter],
                 [pltpu.SemaphoreType.DMA((4,)), pltpu.SemaphoreType.DMA((4,)), pltpu.SemaphoreType.DMA((3,)),
                  pltpu.SemaphoreType.DMA((3,)), pltpu.SemaphoreType.DMA((6,))]
                 + [pltpu.VMEM((r, c), block.dtype)] * 4, start, finish, early=early)


_DIMS = {"nn": (((1,), (0,)), ((), ())), "nt": (((1,), (1,)), ((), ())), "tn": (((0,), (0,)), ((), ()))}


class _Epilogue:
    def __init__(self, args, in_specs, out_shapes, out_specs, fn, keep_product):
        self.args, self.in_specs, self.out_shapes, self.out_specs = args, in_specs, out_shapes, out_specs
        self.fn, self.keep_product = fn, keep_product


def _row_tile(rows, cols):
    return pl.BlockSpec((rows, cols), lambda i, j: (i, 0))


def _whole(shape):
    zeros = (0,) * len(shape)
    return pl.BlockSpec(shape, lambda i, j: zeros)


def _mm(a, b, mode, out_dtype, name, bias=None, tm=512, tn=512, comm=None, cols=None, epilogue=None, k_piece=None):
    pieces = a if isinstance(a, (list, tuple)) else [a]
    piece_cols = [0] * len(pieces)
    if k_piece is not None:
        pieces, piece_cols = [p for p, _ in a], [c for _, c in a]
    assert all(p.dtype == BF16 for p in pieces) and b.dtype == BF16
    a = pieces[0]
    if mode == "tn":
        k_dim, m_dim = a.shape
    else:
        m_dim, k_dim = a.shape[0], k_piece or a.shape[1]
    n_dim = b.shape[0] if mode == "nt" else b.shape[1]
    col0 = 0
    if cols is not None:
        assert mode != "tn" and cols[0] % tn == 0 and cols[1] % tn == 0
        col0, n_dim = cols[0] // tn, cols[1]
    tm, tn = _pick(m_dim, tm), _pick(n_dim, tn)
    assert mode != "tn" or len(pieces) == 1
    a_specs = [pl.BlockSpec((k_dim, tm), lambda i, j: (0, i)) if mode == "tn"
               else pl.BlockSpec((tm, k_dim), lambda i, j, c=c: (i, c)) for c in piece_cols]
    once = dict(pipeline_mode=pl.Buffered(1)) if tn == n_dim else {}
    if mode == "nt":
        b_specs = [pl.BlockSpec((tn, k_dim), lambda i, j, p=p: (j + col0, p), **once) for p in range(len(pieces))]
    else:
        b_specs = [pl.BlockSpec((k_dim, tn), lambda i, j, p=p: (p, j + col0), **once) for p in range(len(pieces))]
    in_specs = a_specs + b_specs
    args = list(pieces) + [b] * len(pieces)
    if bias is not None:
        in_specs.append(pl.BlockSpec((1, tn), lambda i, j: (0, j + col0)))
        args.append(bias)
    dims = _DIMS[mode]
    n_pieces = len(pieces)
    n_own = len(args)
    keep = epilogue is None or epilogue.keep_product
    out_specs = [pl.BlockSpec((tm, tn), lambda i, j: (i, j))] if keep else []
    out_shape = [jax.ShapeDtypeStruct((m_dim, n_dim), out_dtype)] if keep else []
    if epilogue is not None:
        assert tn == n_dim
        in_specs, args = in_specs + list(epilogue.in_specs), args + list(epilogue.args)
        out_specs, out_shape = out_specs + list(epilogue.out_specs), out_shape + list(epilogue.out_shapes)

    def body(ins, outs, scratch):
        total = lax.dot_general(ins[0][...], ins[n_pieces][...], dims, preferred_element_type=F32)
        for p in range(1, n_pieces):
            total = total + lax.dot_general(ins[p][...], ins[n_pieces + p][...], dims, preferred_element_type=F32)
        if bias is not None:
            total = total + ins[2 * n_pieces][...]
        if keep:
            outs[0][...] = total.astype(out_dtype)
        if epilogue is not None:
            epilogue.fn(total, pl.program_id(0) == 0, ins[n_own:], outs[1:] if keep else outs)

    outs, extra = _host_call(body, name, grid=(m_dim // tm, n_dim // tn), in_specs=in_specs, out_specs=out_specs,
                             out_shape=out_shape, scratch_shapes=[], args=args, comm=comm)
    product = outs[0] if keep else None
    if comm is None and epilogue is None:
        return product
    return product, outs[1:] if keep else outs, extra


def _mm_tn_rows(pieces, b, name, tm=256):
    k_dim, n_dim = b.shape
    counts = [p.shape[1] // tm for p in pieces]
    assert all(p.shape[1] % tm == 0 for p in pieces)
    firsts = [sum(counts[:q]) for q in range(len(pieces))]

    def a_spec(first, count):
        return pl.BlockSpec((k_dim, tm), lambda i: (0, jnp.clip(i - first, 0, count - 1)))

    def body(ins, outs, scratch):
        i = pl.program_id(0)
        for a_ref, first, count in zip(ins[:-1], firsts, counts):
            @pl.when(jnp.logical_and(i >= first, i < first + count))
            def _(a_ref=a_ref):
                outs[0][...] = lax.dot_general(a_ref[...], ins[-1][...], _DIMS["tn"],
                                               preferred_element_type=F32).astype(BF16)

    (out,), _ = _host_call(
        body, name, grid=(sum(counts),),
        in_specs=[a_spec(f, c) for f, c in zip(firsts, counts)]
        + [pl.BlockSpec((k_dim, n_dim), lambda i: (0, 0), pipeline_mode=pl.Buffered(1))],
        out_specs=[_tile(tm, n_dim)], out_shape=[jax.ShapeDtypeStruct((sum(counts) * tm, n_dim), BF16)],
        scratch_shapes=[], args=list(pieces) + [b])
    return out


IN_PROJ_ROWS = 1024
OUT_COLS = 384


def _in_proj(h1, w_mine, bias):
    seq, shard = h1.shape[0], w_mine.shape[0]
    chip_cols = 2 * shard
    per_chip = chip_cols // OUT_COLS
    n_qkv = 3 * D_ATTN // OUT_COLS
    n_rows = seq // IN_PROJ_ROWS
    n_chips = N_DEV // 2
    comm = _gather_comm([w_mine])

    def body(h_ref, b_ref, w_hbm, wall_hbm, qkv_hbm, zr_hbm, w_ref, stage_b, stage_f, w_sems, o_sems, *csems):
        t, i = pl.program_id(0), pl.program_id(1)
        step = t * n_rows + i
        me, sibling, chips, c, copy, mine, first = comm.plan([w_hbm], [wall_hbm], csems)
        slots = [2 * me[0] + me[1]] + [2 * px + py for px, py in chips]

        def chip_of(tile):
            p = slots[0]
            for k in range(1, n_chips):
                p = jnp.where(tile == k, slots[k], p)
            return p

        def out_copies(tile, row_tile):
            p = chip_of(tile)
            rows = pl.ds(pl.multiple_of(row_tile * IN_PROJ_ROWS, IN_PROJ_ROWS), IN_PROJ_ROWS)
            res = []
            for s in range(per_chip):
                b = per_chip * p + s
                q_col = pl.multiple_of(jnp.minimum(b, n_qkv - 1) * OUT_COLS, 128)
                z_col = pl.multiple_of(jnp.maximum(b - n_qkv, 0) * OUT_COLS, 128)
                res.append((b < n_qkv,
                            pltpu.make_async_copy(stage_b.at[s], qkv_hbm.at[rows, pl.ds(q_col, OUT_COLS)], o_sems.at[s]),
                            pltpu.make_async_copy(stage_f.at[s], zr_hbm.at[rows, pl.ds(z_col, OUT_COLS)],
                                                  o_sems.at[per_chip + s])))
            return res

        def wait_out(tile, row_tile):
            for to_qkv_cond, to_qkv, to_zr in out_copies(tile, row_tile):
                pl.when(to_qkv_cond)(to_qkv.wait)
                pl.when(jnp.logical_not(to_qkv_cond))(to_zr.wait)

        def load_pair(p):
            loads = [pltpu.make_async_copy(wall_hbm.at[2 * p + h], w_ref.at[pl.ds(h * shard, shard)], w_sems.at[h])
                     for h in range(2)]
            for cp in loads:
                cp.start()
            for cp in loads:
                cp.wait()

        @pl.when(step == 0)
        def _():
            for cp in mine + first:
                cp.start()
            mine[0].wait()
            copy(0, 0, sibling, me).wait_recv()
            load_pair(slots[0])

        for j, chip in enumerate(chips):
            @pl.when(step == (j + 1) * n_rows)
            def _(j=j, chip=chip):
                copy(0, 1 + j, (*chip, c), me).wait_recv()
                copy(0, 4 + j, (*chip, c), sibling).start()
                copy(0, 4 + j, (*chip, 1 - c), me).wait_recv()
                load_pair(slots[j + 1])

        rows = pl.ds(pl.multiple_of(i * IN_PROJ_ROWS, IN_PROJ_ROWS), IN_PROJ_ROWS)
        prod = lax.dot_general(h_ref[rows, :], w_ref[...], _DIMS["nt"], preferred_element_type=F32) + b_ref[chip_of(t)]

        @pl.when(step > 0)
        def _():
            wait_out((step - 1) // n_rows, (step - 1) % n_rows)

        for s, (to_qkv_cond, to_qkv, to_zr) in enumerate(out_copies(t, i)):
            part = prod[:, s * OUT_COLS:(s + 1) * OUT_COLS]

            @pl.when(to_qkv_cond)
            def _(s=s, part=part, to_qkv=to_qkv):
                stage_b[s] = part.astype(BF16)
                to_qkv.start()

            @pl.when(jnp.logical_not(to_qkv_cond))
            def _(s=s, part=part, to_zr=to_zr):
                stage_f[s] = part
                to_zr.start()

        @pl.when(step == n_chips * n_rows - 1)
        def _():
            wait_out(t, i)
            passed = [copy(0, 4 + j, (*chip, c), sibling) for j, chip in enumerate(chips)]
            for cp in first + passed:
                cp.wait_send()

    return pl.pallas_call(
        body, name="in_proj", grid=(n_chips, n_rows),
        in_specs=[pl.BlockSpec((seq, D_MODEL), lambda t, i: (0, 0), pipeline_mode=pl.Buffered(1)),
                  pl.BlockSpec((n_chips, 1, chip_cols), lambda t, i: (0, 0, 0)), ANY],
        out_specs=[ANY, ANY, ANY],
        out_shape=[comm.out_shapes[0], jax.ShapeDtypeStruct((seq, n_qkv * OUT_COLS), BF16),
                   jax.ShapeDtypeStruct((seq, N_DEV * shard - n_qkv * OUT_COLS), F32)],
        scratch_shapes=[pltpu.VMEM((chip_cols, D_MODEL), BF16), pltpu.VMEM((per_chip, IN_PROJ_ROWS, OUT_COLS), BF16),
                        pltpu.VMEM((per_chip, IN_PROJ_ROWS, OUT_COLS), F32), pltpu.SemaphoreType.DMA((2,)),
                        pltpu.SemaphoreType.DMA((2 * per_chip,))] + list(comm.sems),
        compiler_params=_cparams(2),
    )(h1, bias.reshape(n_chips, 1, chip_cols), w_mine)


def _adam_math(w, g, m, v):
    m = ADAM_B1 * m + (1.0 - ADAM_B1) * g
    v = ADAM_B2 * v + (1.0 - ADAM_B2) * (g * g)
    m_hat = m / (1.0 - ADAM_B1 ** ADAM_STEP)
    v_hat = v / (1.0 - ADAM_B2 ** ADAM_STEP)
    delta = -ADAM_LR * (m_hat / (jnp.sqrt(v_hat) + ADAM_EPS) + ADAM_WD * w)
    return delta, m, v


def _adamw(w, m, v, name, g=None, parts=None):
    rows, cols = w.shape[0], w.shape[-1]
    tr = rows
    if rows * cols * 4 > ADAMW_BLOCK_BYTES:
        tr = max(t for t in range(16, rows, 16) if rows % t == 0 and t * cols * 4 <= ADAMW_BLOCK_BYTES)

    def body(w_ref, m_ref, v_ref, g_ref, go_ref, d_ref, mo_ref, vo_ref):
        if parts is None:
            grad = g_ref[...]
        else:
            grad = g_ref[0].astype(F32)
            for d in range(1, parts.shape[0]):
                grad = grad + g_ref[d].astype(F32)
        delta, m_new, v_new = _adam_math(w_ref[...], grad, m_ref[...], v_ref[...])
        go_ref[...] = grad
        d_ref[...] = delta
        mo_ref[...] = m_new
        vo_ref[...] = v_new

    spec = _tile(tr, cols) if w.ndim == 2 else _full(w.shape)
    g_spec = spec if parts is None else pl.BlockSpec((parts.shape[0], tr, cols), lambda i: (0, i, 0))
    shape = jax.ShapeDtypeStruct(w.shape, F32)
    return pl.pallas_call(
        body, name=name, out_shape=[shape] * 4, grid=(rows // tr,),
        in_specs=[spec, spec, spec, g_spec], out_specs=[spec] * 4, compiler_params=_cparams(1),
    )(w, m, v, g if parts is None else parts)


def _pack_grads(small_x, small_m, small_f, small_g, small_a, small_c, dbv, dbg, dwv, dwg, dw_conv):
    pieces = [
        (small_x, 2, D_MODEL), (small_x, 1, D_MODEL), (small_m, 4, D_MODEL), (small_m, 2, D_MODEL),
        (small_m, 1, D_MODEL), (small_f, 1, D_MODEL),
        (small_x, 0, D_MODEL), (small_m, 3, D_MODEL),
        (small_a, 0, D_ATTN), (small_a, 1, D_ATTN), (small_a, 2, D_ATTN), (small_c, 3, D_CONV),
        (small_c, 4, D_CONV), (small_g, 0, D_MODEL), (small_g, 1, D_MODEL),
        (small_c, 0, D_CONV), (small_c, 1, D_CONV), (small_c, 2, D_CONV),
        (small_g, 2, D_MODEL), (small_m, 0, D_MODEL), (small_f, 0, D_MODEL),
        (dbv, 0, D_FF), (dbg, 0, D_FF),
    ]
    pieces += [(dw_conv, j, D_CONV) for j in range(CONV_K)]
    pieces += [(src, tap, D_FF) for tap in range(3) for src in (dwv, dwg)]
    sources = [small_x, small_m, small_f, small_g, small_a, small_c, dbv, dbg, dwv, dwg, dw_conv]
    assert sum(width for _, _, width in pieces) == PACKED_TOTAL

    def body(*refs):
        o_ref = refs[-1]
        ref_of = {id(src): ref for src, ref in zip(sources, refs)}
        off = 0
        for src, row, width in pieces:
            o_ref[:, off:off + width] = ref_of[id(src)][row:row + 1, :]
            off += width

    return pl.pallas_call(body, name="pack_grads", out_shape=jax.ShapeDtypeStruct((1, PACKED_TOTAL), F32))(*sources)


def _small_adamw(gathered, gathered_rel, gathered_loss, weights, mom_m, mom_v):
    vec_names = [name for name, _ in SMALL]
    states = []
    for name in vec_names + ["rel_bias"]:
        states += [weights[name], mom_m[name], mom_v[name]]
    states = [a.reshape(a.shape[1:]) if a.ndim == 3 else a for a in states]
    n_state = len(states)

    def body(*refs):
        g_ref, rel_ref, loss_ref = refs[0], refs[1], refs[2]
        state_refs, out_refs = refs[3:3 + n_state], refs[3 + n_state:]
        total = g_ref[0:1, :]
        rel = rel_ref[0]
        loss = loss_ref[0]
        for d in range(1, N_DEV):
            total = total + g_ref[d:d + 1, :]
            rel = rel + rel_ref[d]
            loss = loss + loss_ref[d]
        off = 0
        for n, (name, width) in enumerate(SMALL):
            grad = total[:, off:off + width]
            w_ref, m_ref, v_ref = state_refs[3 * n:3 * n + 3]
            for ref, val in zip(out_refs[4 * n:4 * n + 4], (grad,) + _adam_math(w_ref[...], grad, m_ref[...], v_ref[...])):
                ref[...] = val
            off += width
        n = len(SMALL)
        w_ref, m_ref, v_ref = state_refs[3 * n:3 * n + 3]
        for ref, val in zip(out_refs[4 * n:4 * n + 4], (rel,) + _adam_math(w_ref[...], rel, m_ref[...], v_ref[...])):
            ref[...] = val
        dwc_ref, dwf_ref, loss_out = out_refs[4 * n + 4:]
        loss_out[...] = 0.5 * loss
        dwc_ref[...] = jnp.zeros_like(dwc_ref)
        dwf_ref[...] = jnp.zeros_like(dwf_ref)
        for j in range(CONV_K):
            dwc_ref[j:j + 1, :] = total[:, off:off + D_CONV]
            off += D_CONV
        for tap in range(3):
            dwf_ref[tap:tap + 1, :] = total[:, off:off + 2 * D_FF]
            off += 2 * D_FF

    out_shape = []
    for k in range(n_state // 3):
        out_shape += [jax.ShapeDtypeStruct(states[3 * k].shape, F32)] * 4
    out_shape += [jax.ShapeDtypeStruct((CONV_HALO, D_CONV), F32), jax.ShapeDtypeStruct((8, 2 * D_FF), F32),
                  jax.ShapeDtypeStruct((1, 128), F32)]
    res = pl.pallas_call(
        body, name="small_adamw", out_shape=out_shape,
        compiler_params=pltpu.CompilerParams(vmem_limit_bytes=VMEM_LIMIT_BYTES),
    )(gathered, gathered_rel, gathered_loss, *states)
    updates = {name: tuple(res[4 * n:4 * n + 4]) for n, name in enumerate(vec_names + ["rel_bias"])}
    return updates, res[-3], res[-2], res[-1]


def _ada_mod(c, w_shard):
    cols = w_shard.shape[1]

    def body(c_ref, w_ref, call_ref, mod_ref, send_sems, recv_sems):
        x, y, cc = _place()
        me = 4 * x + 2 * y + cc

        def exchange(ref, phase):
            sends, arrivals = [], []
            for mask in range(1, N_DEV):
                px = 1 - x if mask & 4 else x
                py = 1 - y if mask & 2 else y
                pc = 1 - cc if mask & 1 else cc
                both = dict(send_sem=send_sems.at[7 * phase + mask - 1], recv_sem=recv_sems.at[7 * phase + mask - 1],
                            device_id=(px, py, pc), device_id_type=MESH)
                sends.append(pltpu.make_async_remote_copy(src_ref=ref.at[me], dst_ref=ref.at[me], **both))
                arrivals.append(pltpu.make_async_remote_copy(src_ref=ref.at[me], dst_ref=ref.at[4 * px + 2 * py + pc],
                                                             **both))
            for cp in sends:
                cp.start()
            for cp in arrivals:
                cp.wait_recv()
            for cp in sends:
                cp.wait_send()

        v = c_ref[...]
        call_ref[me] = v * _sig(v)
        exchange(call_ref, 0)
        c_all = jnp.concatenate([call_ref[d] for d in range(N_DEV)], axis=0)
        mod_ref[me] = jnp.dot(c_all, w_ref[...], precision=HIGHEST, preferred_element_type=F32)
        exchange(mod_ref, 1)

    return pl.pallas_call(
        body, name="ada_mod",
        out_shape=[jax.ShapeDtypeStruct((N_DEV, 1, D_MODEL), F32), jax.ShapeDtypeStruct((N_DEV, N_DEV, cols), F32)],
        scratch_shapes=[pltpu.SemaphoreType.DMA((14,)), pltpu.SemaphoreType.DMA((14,))],
        compiler_params=pltpu.CompilerParams(vmem_limit_bytes=VMEM_LIMIT_BYTES),
    )(c, w_shard)


def _ada_grad(c_all, dmod_shard):
    def body(c_ref, d_ref, o_ref):
        o_ref[...] = lax.dot_general(c_ref[...], d_ref[...], _DIMS["tn"], precision=HIGHEST,
                                     preferred_element_type=F32)

    return pl.pallas_call(
        body, name="ada_grad", out_shape=jax.ShapeDtypeStruct((D_MODEL, dmod_shard.shape[1]), F32),
        compiler_params=pltpu.CompilerParams(vmem_limit_bytes=VMEM_LIMIT_BYTES),
    )(c_all, dmod_shard)


ROWS = 256


def _rms(v):
    r = lax.rsqrt(jnp.mean(v * v, axis=-1, keepdims=True) + EPS)
    return v * r, r


def _rms_bwd(dxn, xn, r):
    return r * (dxn - xn * jnp.mean(dxn * xn, axis=-1, keepdims=True))


def _colsum(v):
    return jnp.sum(v, axis=0, keepdims=True)


def _pre_mix(x, mod6, g1, comm=None):
    seq = x.shape[0]

    def body(ins, outs, scratch):
        x_ref, mod_ref, g_ref = ins
        xn, _ = _rms(x_ref[...])
        y = xn * g_ref[...]
        outs[0][...] = (y * (1.0 + mod_ref[SC_M:SC_M + 1, :]) + mod_ref[SH_M:SH_M + 1, :]).astype(BF16)

    (h,), extra = _host_call(
        body, "pre_mix", grid=(seq // ROWS,),
        in_specs=[_tile(ROWS, D_MODEL), _full((6, D_MODEL)), _full((1, D_MODEL))], out_specs=[_tile(ROWS, D_MODEL)],
        out_shape=[jax.ShapeDtypeStruct((seq, D_MODEL), BF16)], scratch_shapes=[], args=[x, mod6, g1], comm=comm)
    return h, extra


def _post_mix_pre_ffn(x, mod6, g2, g3, rows):
    seq = x.shape[0]

    def fn(y, first, ins, outs):
        x_ref, mod_ref, g2_ref, g3_ref = ins
        x1_ref, h_ref = outs
        yn, _ = _rms(y)
        x1 = x_ref[...] + mod_ref[GT_M:GT_M + 1, :] * (yn * g2_ref[...])
        x1_ref[...] = x1
        xn, _ = _rms(x1)
        y3 = xn * g3_ref[...]
        h_ref[...] = (y3 * (1.0 + mod_ref[SC_F:SC_F + 1, :]) + mod_ref[SH_F:SH_F + 1, :]).astype(BF16)

    return _Epilogue(
        [x, mod6, g2, g3], [_row_tile(rows, D_MODEL), _whole((6, D_MODEL)), _whole((1, D_MODEL)), _whole((1, D_MODEL))],
        [jax.ShapeDtypeStruct((seq, D_MODEL), F32), jax.ShapeDtypeStruct((seq, D_MODEL), BF16)],
        [_row_tile(rows, D_MODEL), _row_tile(rows, D_MODEL)], fn, keep_product=True)


def _final(x1, target, mod6, g4, rows):
    seq = x1.shape[0]

    def fn(y, first, ins, outs):
        x1_ref, t_ref, mod_ref, g_ref = ins
        loss_ref, dout_ref, dyf_ref, small_ref = outs

        @pl.when(first)
        def _():
            loss_ref[...] = jnp.zeros_like(loss_ref)
            small_ref[...] = jnp.zeros_like(small_ref)

        gt = mod_ref[GT_F:GT_F + 1, :]
        g4v = g_ref[...]
        yn, r = _rms(y)
        out = x1_ref[...] + gt * (yn * g4v)
        err = out - t_ref[...]
        loss_ref[...] += jnp.sum(jnp.mean(err * err, axis=-1, keepdims=True))
        dout = err * (1.0 / D_MODEL)
        dout_ref[...] = dout
        small_ref[0:1, :] += _colsum(dout * gt * yn)
        small_ref[1:2, :] += _colsum(dout * (yn * g4v))
        dyf_ref[...] = _rms_bwd(dout * gt * g4v, yn, r).astype(BF16)

    return _Epilogue(
        [x1, target, mod6, g4],
        [_row_tile(rows, D_MODEL), _row_tile(rows, D_MODEL), _whole((6, D_MODEL)), _whole((1, D_MODEL))],
        [jax.ShapeDtypeStruct((1, 128), F32), jax.ShapeDtypeStruct((seq, D_MODEL), F32),
         jax.ShapeDtypeStruct((seq, D_MODEL), BF16), jax.ShapeDtypeStruct((8, D_MODEL), F32)],
        [_whole((1, 128)), _row_tile(rows, D_MODEL), _row_tile(rows, D_MODEL), _whole((8, D_MODEL))],
        fn, keep_product=False)


def _mid_bwd(x1, dout, ymix, mod6, g3, g2, rows):
    seq = x1.shape[0]

    def fn(dh, first, ins, outs):
        x1_ref, dout_ref, y_ref, mod_ref, g3_ref, g2_ref = ins
        dx1_ref, dy_ref, small_ref = outs

        @pl.when(first)
        def _():
            small_ref[...] = jnp.zeros_like(small_ref)

        g3v, g2v = g3_ref[...], g2_ref[...]
        xn, r3 = _rms(x1_ref[...])
        y3 = xn * g3v
        dy3 = dh * (1.0 + mod_ref[SC_F:SC_F + 1, :])
        small_ref[0:1, :] += _colsum(dy3 * xn)
        small_ref[1:2, :] += _colsum(dh * y3)
        small_ref[2:3, :] += _colsum(dh)
        dx1 = dout_ref[...] + _rms_bwd(dy3 * g3v, xn, r3)
        dx1_ref[...] = dx1
        gt = mod_ref[GT_M:GT_M + 1, :]
        yn, r2 = _rms(y_ref[...])
        small_ref[3:4, :] += _colsum(dx1 * gt * yn)
        small_ref[4:5, :] += _colsum(dx1 * (yn * g2v))
        dy_ref[...] = _rms_bwd(dx1 * gt * g2v, yn, r2).astype(BF16)

    return _Epilogue(
        [x1, dout, ymix, mod6, g3, g2],
        [_row_tile(rows, D_MODEL)] * 3 + [_whole((6, D_MODEL)), _whole((1, D_MODEL)), _whole((1, D_MODEL))],
        [jax.ShapeDtypeStruct((seq, D_MODEL), F32), jax.ShapeDtypeStruct((seq, D_MODEL), BF16),
         jax.ShapeDtypeStruct((8, D_MODEL), F32)],
        [_row_tile(rows, D_MODEL), _row_tile(rows, D_MODEL), _whole((8, D_MODEL))], fn, keep_product=False)


def _pre_mix_bwd(x, dx1, mod6, g1, rows):
    seq = x.shape[0]

    def fn(dh, first, ins, outs):
        x_ref, dx1_ref, mod_ref, g_ref = ins
        dx_ref, small_ref = outs

        @pl.when(first)
        def _():
            small_ref[...] = jnp.zeros_like(small_ref)

        g1v = g_ref[...]
        xn, r = _rms(x_ref[...])
        dy = dh * (1.0 + mod_ref[SC_M:SC_M + 1, :])
        small_ref[0:1, :] += _colsum(dy * xn)
        small_ref[1:2, :] += _colsum(dh * (xn * g1v))
        small_ref[2:3, :] += _colsum(dh)
        dx_ref[...] = dx1_ref[...] + _rms_bwd(dy * g1v, xn, r)

    return _Epilogue(
        [x, dx1, mod6, g1],
        [_row_tile(rows, D_MODEL), _row_tile(rows, D_MODEL), _whole((6, D_MODEL)), _whole((1, D_MODEL))],
        [jax.ShapeDtypeStruct((seq, D_MODEL), F32), jax.ShapeDtypeStruct((8, D_MODEL), F32)],
        [_row_tile(rows, D_MODEL), _whole((8, D_MODEL))], fn, keep_product=False)


def _toeplitz_onehot(shape, offset_axis, top):
    m = lax.broadcasted_iota(jnp.int32, shape, offset_axis)
    i = lax.broadcasted_iota(jnp.int32, shape, 1 - offset_axis)
    return (i == jnp.clip(top - m, -MAX_REL, MAX_REL) + MAX_REL).astype(F32)


def _bias_table(rel_bias):
    width = GROUP_Q + GROUP_K

    def body(rb_ref, o_ref, t_ref):
        t_ref[...] = jnp.dot(rb_ref[...], _toeplitz_onehot((N_REL, width), 1, GROUP_K - 1), precision=HIGHEST,
                             preferred_element_type=F32)
        lane = lax.broadcasted_iota(jnp.int32, (N_HEADS, GROUP_K), 1)
        for r in range(GROUP_Q):
            first_key = (r // CHUNK) * CHUNK
            band = jnp.logical_and(lane >= first_key, lane < first_key + BAND)
            o_ref[r] = jnp.where(band, t_ref[:, GROUP_Q - 1 - r:GROUP_Q - 1 - r + GROUP_K], NEG_INF)

    return pl.pallas_call(
        body, name="bias_table", out_shape=jax.ShapeDtypeStruct((GROUP_Q, N_HEADS, GROUP_K), F32),
        scratch_shapes=[pltpu.VMEM((N_HEADS, width), F32)],
    )(rel_bias)


def _bias_grad(dbias_q):
    def body(d_ref, o_ref, t_ref):
        t_ref[...] = jnp.zeros_like(t_ref)
        for qi in range(CHUNK):
            t_ref[:, CHUNK - 1 - qi:CHUNK - 1 - qi + BAND] += d_ref[qi]
        o_ref[...] = jnp.dot(t_ref[...], _toeplitz_onehot((TOEPLITZ, N_REL), 0, BAND - 1), precision=HIGHEST,
                             preferred_element_type=F32)

    return pl.pallas_call(
        body, name="bias_grad", out_shape=jax.ShapeDtypeStruct((N_HEADS, N_REL), F32),
        scratch_shapes=[pltpu.VMEM((N_HEADS, TOEPLITZ), F32)],
    )(dbias_q)


def _resident_copies(qkv_hbm, t_hbm, k_ref, v_ref, t_ref, sems):
    inside = pl.ds(PAD_ROWS, qkv_hbm.shape[0])
    return (pltpu.make_async_copy(qkv_hbm.at[:, pl.ds(D_ATTN, D_ATTN)], k_ref.at[inside, :], sems.at[0]),
            pltpu.make_async_copy(qkv_hbm.at[:, pl.ds(2 * D_ATTN, D_ATTN)], v_ref.at[inside, :], sems.at[1]),
            pltpu.make_async_copy(t_hbm, t_ref, sems.at[2]))


def _start_resident(copies, k_ref, v_ref):
    k_ref[0:PAD_ROWS, :] = jnp.zeros((PAD_ROWS, D_ATTN), BF16)
    v_ref[0:PAD_ROWS, :] = jnp.zeros((PAD_ROWS, D_ATTN), BF16)
    for cp in copies:
        cp.start()


def _softmax_rows(s_ref, t_ref, h, before_start, rows):
    s = s_ref[rows, :] * (HEAD_DIM ** -0.5) + t_ref[h, rows, :] + before_start
    e = jnp.exp(s - jnp.max(s, axis=-1, keepdims=True))
    return e / jnp.sum(e, axis=-1, keepdims=True)


def _before_start(g):
    kj = lax.broadcasted_iota(jnp.int32, (8, GROUP_K), 1)
    return jnp.where(kj >= PAD_ROWS - g * GROUP_Q, 0.0, NEG_INF)


def _attn_fwd(qkv, table, comm=None):
    seq = qkv.shape[0]

    def body(ins, outs, scratch):
        q_ref, qkv_hbm, t_hbm = ins
        (o_ref,) = outs
        k_ref, v_ref, t_ref, s_ref, p_ref, sems = scratch
        g = pl.program_id(0)
        load_k, load_v, load_t = _resident_copies(qkv_hbm, t_hbm, k_ref, v_ref, t_ref, sems)

        @pl.when(g == 0)
        def _():
            _start_resident((load_k, load_v, load_t), k_ref, v_ref)
            load_k.wait()

        window = pl.ds(pl.multiple_of(g * GROUP_Q, GROUP_Q), GROUP_K)
        before_start = _before_start(g)
        for h in range(N_HEADS):
            cols = slice(h * HEAD_DIM, (h + 1) * HEAD_DIM)
            buf = h % 2
            s_ref[buf] = lax.dot_general(q_ref[:, cols], k_ref[window, cols], _DIMS["nt"],
                                         preferred_element_type=F32)
            if h == 0:
                pl.when(g == 0)(load_t.wait)
            for row in range(0, GROUP_Q, SOFTMAX_ROWS):
                halves = [_softmax_rows(s_ref.at[buf], t_ref, h, before_start, slice(r, r + 8))
                          for r in (row, row + 8)]
                p_ref[buf, row:row + SOFTMAX_ROWS, :] = jnp.concatenate(halves, axis=0).astype(BF16)
            if h == 0:
                pl.when(g == 0)(load_v.wait)
            o_ref[:, cols] = jnp.dot(p_ref[buf], v_ref[window, cols], preferred_element_type=F32).astype(BF16)

    (ao,), extra = _host_call(
        body, "attn_fwd", grid=(seq // GROUP_Q,),
        in_specs=[_tile(GROUP_Q, D_ATTN), ANY, ANY], out_specs=[_tile(GROUP_Q, D_ATTN)],
        out_shape=[jax.ShapeDtypeStruct((seq, D_ATTN), BF16)],
        scratch_shapes=[pltpu.VMEM((seq + PAD_ROWS, D_ATTN), BF16), pltpu.VMEM((seq + PAD_ROWS, D_ATTN), BF16),
                        pltpu.VMEM(table.shape, F32),
                        pltpu.VMEM((2, GROUP_Q, GROUP_K), F32), pltpu.VMEM((2, GROUP_Q, GROUP_K), BF16),
                        pltpu.SemaphoreType.DMA((3,))],
        args=[qkv, qkv, table], comm=comm)
    return ao, extra


def _attn_bwd(qkv, table, dao, comm=None):
    seq = qkv.shape[0]
    n_groups = seq // GROUP_Q
    fold_w = GROUP_K + (GROUP - 1) * CHUNK

    def body(ins, outs, scratch):
        q_ref, do_ref, qkv_hbm, t_hbm = ins
        dq_ref, dkt_hbm, dvt_hbm, db_ref, cs_ref = outs
        k_ref, v_ref, t_ref, db_acc, dkt_acc, dvt_acc, s_ref, dp_ref, p_ref, ds_ref, sems = scratch
        g = pl.program_id(0)

        load_k, load_v, load_t = _resident_copies(qkv_hbm, t_hbm, k_ref, v_ref, t_ref, sems)

        @pl.when(g == 0)
        def _():
            _start_resident((load_k, load_v, load_t), k_ref, v_ref)
            db_acc[...] = jnp.zeros_like(db_acc)
            dkt_acc[...] = jnp.zeros_like(dkt_acc)
            dvt_acc[...] = jnp.zeros_like(dvt_acc)
            cs_ref[...] = jnp.zeros_like(cs_ref)
            load_k.wait()
            load_v.wait()

        window = pl.ds(pl.multiple_of(g * GROUP_Q, GROUP_Q), GROUP_K)
        before_start = _before_start(g)
        for h in range(N_HEADS):
            cols = slice(h * HEAD_DIM, (h + 1) * HEAD_DIM)
            buf = h % 2
            qh, doh = q_ref[:, cols], do_ref[:, cols]
            kh, vh = k_ref[window, cols], v_ref[window, cols]
            s_ref[buf] = lax.dot_general(qh, kh, _DIMS["nt"], preferred_element_type=F32)
            dp_ref[buf] = lax.dot_general(doh, vh, _DIMS["nt"], preferred_element_type=F32)
            if h == 0:
                pl.when(g == 0)(load_t.wait)
            for row in range(0, GROUP_Q, SOFTMAX_ROWS):
                p_halves, ds_halves = [], []
                for r in (row, row + 8):
                    p = _softmax_rows(s_ref.at[buf], t_ref, h, before_start, slice(r, r + 8))
                    dp = dp_ref[buf, r:r + 8, :]
                    ds = p * (dp - jnp.sum(dp * p, axis=-1, keepdims=True))
                    chunk = r // CHUNK
                    shift = (GROUP - 1 - chunk) * CHUNK
                    db_acc[h, r - chunk * CHUNK:r - chunk * CHUNK + 8, shift:shift + GROUP_K] += ds
                    p_halves.append(p)
                    ds_halves.append(ds * (HEAD_DIM ** -0.5))
                p_ref[buf, row:row + SOFTMAX_ROWS, :] = jnp.concatenate(p_halves, axis=0).astype(BF16)
                ds_ref[buf, row:row + SOFTMAX_ROWS, :] = jnp.concatenate(ds_halves, axis=0).astype(BF16)
            dq_ref[:, cols] = jnp.dot(ds_ref[buf], kh, preferred_element_type=F32).astype(BF16)
            dkt_acc[cols, window] += lax.dot_general(qh, ds_ref[buf], _DIMS["tn"], preferred_element_type=F32)
            dvt_acc[cols, window] += lax.dot_general(doh, p_ref[buf], _DIMS["tn"], preferred_element_type=F32)
        cs_ref[0:1, :] += _colsum(dq_ref[...].astype(F32))

        @pl.when(g == n_groups - 1)
        def _():
            lo = (GROUP - 1) * CHUNK
            for h in range(N_HEADS):
                db_ref[h] = db_acc[h, :, lo:lo + BAND]
            inside = pl.ds(PAD_ROWS, seq)
            on_diagonal = (lax.broadcasted_iota(jnp.int32, (D_ATTN, D_ATTN), 0)
                           == lax.broadcasted_iota(jnp.int32, (D_ATTN, D_ATTN), 1))
            for row, acc in ((1, dkt_acc), (2, dvt_acc)):
                column = jnp.sum(acc[:, inside], axis=1, keepdims=True)
                cs_ref[row:row + 1, :] = _colsum(jnp.where(on_diagonal, column, 0.0))
            out_k = pltpu.make_async_copy(dkt_acc.at[:, inside], dkt_hbm, sems.at[0])
            out_v = pltpu.make_async_copy(dvt_acc.at[:, inside], dvt_hbm, sems.at[1])
            out_k.start()
            out_v.start()
            out_k.wait()
            out_v.wait()

    t_shape = (D_ATTN, seq + PAD_ROWS)
    outs, extra = _host_call(
        body, "attn_bwd", grid=(n_groups,),
        in_specs=[_tile(GROUP_Q, D_ATTN), _tile(GROUP_Q, D_ATTN), ANY, ANY],
        out_specs=[_tile(GROUP_Q, D_ATTN), ANY, ANY, _full((N_HEADS, CHUNK, BAND)), _full((8, D_ATTN))],
        out_shape=[jax.ShapeDtypeStruct((seq, D_ATTN), BF16), jax.ShapeDtypeStruct((D_ATTN, seq), F32),
                   jax.ShapeDtypeStruct((D_ATTN, seq), F32), jax.ShapeDtypeStruct((N_HEADS, CHUNK, BAND), F32),
                   jax.ShapeDtypeStruct((8, D_ATTN), F32)],
        scratch_shapes=[pltpu.VMEM((seq + PAD_ROWS, D_ATTN), BF16), pltpu.VMEM((seq + PAD_ROWS, D_ATTN), BF16),
                        pltpu.VMEM(table.shape, F32), pltpu.VMEM((N_HEADS, CHUNK, fold_w), F32), pltpu.VMEM(t_shape, F32),
                        pltpu.VMEM(t_shape, F32), pltpu.VMEM((2, GROUP_Q, GROUP_K), F32),
                        pltpu.VMEM((2, GROUP_Q, GROUP_K), F32), pltpu.VMEM((2, GROUP_Q, GROUP_K), BF16),
                        pltpu.VMEM((2, GROUP_Q, GROUP_K), BF16), pltpu.SemaphoreType.DMA((3,))],
        args=[qkv, dao, qkv, table], comm=comm)
    return outs, extra


def _dk_dv(dkt, dvt, seq):
    rows = 512
    transposed = pl.BlockSpec((D_ATTN, rows), lambda i: (0, i))

    def body(dkt_ref, dvt_ref, dk_ref, dv_ref):
        dk_ref[...] = dkt_ref[...].T.astype(BF16)
        dv_ref[...] = dvt_ref[...].T.astype(BF16)

    return pl.pallas_call(
        body, name="dk_dv", out_shape=[jax.ShapeDtypeStruct((seq, D_ATTN), BF16)] * 2, grid=(seq // rows,),
        in_specs=[transposed, transposed], out_specs=[_tile(rows, D_ATTN)] * 2, compiler_params=_cparams(1),
    )(dkt, dvt)


CONV_ROWS = 256


def _ln_silu(u1, g, b):
    mu = jnp.mean(u1, axis=-1, keepdims=True)
    xc = u1 - mu
    rs = lax.rsqrt(jnp.mean(xc * xc, axis=-1, keepdims=True) + EPS)
    xhat = xc * rs
    u2 = xhat * g + b
    return xhat, rs, u2


def _glu_into(s_ref, a_ref, b_ref, ah_ref, bh_ref, first):
    halo = ah_ref[...] * _sig(bh_ref[...])
    s_ref[0:CONV_HALO, :] = jnp.where(first, 0.0, halo)
    s_ref[CONV_HALO:CONV_HALO + CONV_ROWS, :] = a_ref[...] * _sig(b_ref[...])


CONV_LANES = 128
CONV_TILES = CONV_ROWS // 8


def _lag_weights(w_ref, lanes):
    return {e: jnp.broadcast_to(w_ref[CONV_K - 1 - e:CONV_K - e, lanes], (8, CONV_LANES)) for e in range(CONV_K)}


def _class_sums(w, tiles, k):
    total = None
    for a, tile in enumerate(tiles):
        if 8 * a + k < CONV_K:
            term = w[8 * a + k] * tile
            total = term if total is None else total + term
    return total


def _conv_back(src_ref, first_tile, w, lanes, row_id, emit):
    before = None
    for m in range(-1, CONV_TILES):
        tiles = [src_ref[8 * (first_tile + m - a):8 * (first_tile + m - a) + 8, lanes] for a in range(4)]
        rolled = [None] + [pltpu.roll(_class_sums(w, tiles, k), k, 0) for k in range(1, 8)]
        if m >= 0:
            out = _class_sums(w, tiles, 0)
            for k in range(1, 8):
                out = out + jnp.where(row_id < k, before[k], rolled[k])
            emit(m, out)
        before = rolled


def _conv_ahead(src_ref, w, lanes, row_id, emit):
    before = None
    for m in range(CONV_TILES + 1):
        tiles = [src_ref[8 * (m + a):8 * (m + a) + 8, lanes] for a in range(4)]
        rolled = [None] + [pltpu.roll(_class_sums(w, tiles, k), 8 - k, 0) for k in range(1, 8)]
        if m >= 1:
            out = before[0]
            for k in range(1, 8):
                out = out + jnp.where(row_id < 8 - k, before[k], rolled[k])
            emit(m - 1, out)
        before = [_class_sums(w, tiles, 0) if m < CONV_TILES else None] + rolled[1:]


def _conv_weight_sums(d_ref, s_ref, lanes, row_id, whole_shifts):
    zero = jnp.zeros((8, CONV_LANES), F32)
    sums = {8 * a + k: zero for a in whole_shifts for k in range(8) if 8 * a + k < CONV_K}

    def d_tile(m):
        return d_ref[8 * m:8 * m + 8, lanes] if 0 <= m < CONV_TILES else zero

    rolled = [None] + [zero] * 7
    for m in range(-1, CONV_TILES):
        cur, nxt = d_tile(m), d_tile(m + 1)
        rolled_next = [None] + [pltpu.roll(nxt, 8 - k, 0) for k in range(1, 8)]
        shifted = [cur] + [jnp.where(row_id < 8 - k, rolled[k], rolled_next[k]) for k in range(1, 8)]
        for a in whole_shifts:
            tile = s_ref[8 * (CONV_HALO // 8 + m - a):8 * (CONV_HALO // 8 + m - a) + 8, lanes]
            for k in range(8):
                if 8 * a + k < CONV_K and not (m < 0 and k == 0):
                    sums[8 * a + k] = sums[8 * a + k] + shifted[k] * tile
        rolled = rolled_next
    return sums


def _conv_fwd(zr, w_dw, b_dw, g_ln, b_ln, comm=None):
    seq = zr.shape[0]

    def body(a_ref, b_ref, ah_ref, bh_ref, w_ref, bias_ref, g_ref, bl_ref, u1_ref, u3_ref, s_ref):
        _glu_into(s_ref, a_ref, b_ref, ah_ref, bh_ref, pl.program_id(0) == 0)
        row_id = lax.broadcasted_iota(jnp.int32, (8, CONV_LANES), 0)
        for lo in range(0, D_CONV, CONV_LANES):
            lanes = slice(lo, lo + CONV_LANES)
            bias = jnp.broadcast_to(bias_ref[:, lanes], (8, CONV_LANES))

            def emit(m, out, lanes=lanes, bias=bias):
                u1_ref[8 * m:8 * m + 8, lanes] = out + bias

            _conv_back(s_ref, CONV_HALO // 8, _lag_weights(w_ref, lanes), lanes, row_id, emit)
        _, _, u2 = _ln_silu(u1_ref[...], g_ref[...], bl_ref[...])
        u3_ref[...] = (u2 * _sig(u2)).astype(BF16)

    return _host_call(
        lambda ins, outs, scratch: body(*ins, *outs, *scratch), "conv_fwd", grid=(seq // CONV_ROWS,),
        in_specs=[_tile(CONV_ROWS, D_CONV, 0), _tile(CONV_ROWS, D_CONV, 1),
                  _prev(CONV_HALO, D_CONV, CONV_ROWS, 0), _prev(CONV_HALO, D_CONV, CONV_ROWS, 1),
                  _full((CONV_K, D_CONV)), _full((1, D_CONV)), _full((1, D_CONV)), _full((1, D_CONV))],
        out_specs=[_tile(CONV_ROWS, D_CONV), _tile(CONV_ROWS, D_CONV)],
        out_shape=[jax.ShapeDtypeStruct((seq, D_CONV), F32), jax.ShapeDtypeStruct((seq, D_CONV), BF16)],
        scratch_shapes=[pltpu.VMEM((CONV_HALO + CONV_ROWS, D_CONV), F32)],
        args=[zr, zr, zr, zr, w_dw, b_dw, g_ln, b_ln], comm=comm)


def _conv_bwd(zr, u1, du3, w_dw, g_ln, b_ln, comm=None):
    seq = zr.shape[0]
    n_tiles = seq // CONV_ROWS
    n_halo = seq // CONV_HALO
    ext = CONV_ROWS + CONV_HALO

    def body(a_ref, b_ref, ah_ref, bh_ref, u1_ref, u1n_ref, d3_ref, d3n_ref, w_ref, g_ref, bl_ref,
             da_ref, db_ref, dw_ref, small_ref, s_ref, d_ref, du0_ref):
        i = pl.program_id(0)

        @pl.when(i == 0)
        def _():
            dw_ref[...] = jnp.zeros_like(dw_ref)
            small_ref[...] = jnp.zeros_like(small_ref)

        _glu_into(s_ref, a_ref, b_ref, ah_ref, bh_ref, i == 0)
        gv, bv = g_ref[...], bl_ref[...]

        def du1_of(u1, d3):
            xhat, rs, u2 = _ln_silu(u1, gv, bv)
            sg = _sig(u2)
            du2 = d3 * (sg * (1.0 + u2 * (1.0 - sg)))
            dxh = du2 * gv
            du1 = rs * (dxh - jnp.mean(dxh, axis=-1, keepdims=True)
                        - xhat * jnp.mean(dxh * xhat, axis=-1, keepdims=True))
            return du1, du2, xhat

        du1, du2, xhat = du1_of(u1_ref[...], d3_ref[...])
        du1n, _, _ = du1_of(u1n_ref[...], d3n_ref[...])
        d_ref[0:CONV_ROWS, :] = du1
        d_ref[CONV_ROWS:ext, :] = jnp.where(i == n_tiles - 1, 0.0, du1n)
        small_ref[0:1, :] += _colsum(du1)
        small_ref[1:2, :] += _colsum(du2 * xhat)
        small_ref[2:3, :] += _colsum(du2)
        row_id = lax.broadcasted_iota(jnp.int32, (8, CONV_LANES), 0)
        for lo in range(0, D_CONV, CONV_LANES):
            lanes = slice(lo, lo + CONV_LANES)

            def emit(m, out, lanes=lanes):
                du0_ref[8 * m:8 * m + 8, lanes] = out

            _conv_ahead(d_ref, _lag_weights(w_ref, lanes), lanes, row_id, emit)
            for whole_shifts in ((0, 1), (2, 3)):
                for e, total in _conv_weight_sums(d_ref, s_ref, lanes, row_id, whole_shifts).items():
                    dw_ref[CONV_K - 1 - e:CONV_K - e, lanes] += _colsum(total)
        du0 = du0_ref[...]
        sb = _sig(b_ref[...])
        da = du0 * sb
        dbv = du0 * a_ref[...] * sb * (1.0 - sb)
        da_ref[...] = da.astype(BF16)
        db_ref[...] = dbv.astype(BF16)
        small_ref[3:4, :] += _colsum(da)
        small_ref[4:5, :] += _colsum(dbv)

    return _host_call(
        lambda ins, outs, scratch: body(*ins, *outs, *scratch), "conv_bwd", grid=(n_tiles,),
        in_specs=[_tile(CONV_ROWS, D_CONV, 0), _tile(CONV_ROWS, D_CONV, 1),
                  _prev(CONV_HALO, D_CONV, CONV_ROWS, 0), _prev(CONV_HALO, D_CONV, CONV_ROWS, 1),
                  _tile(CONV_ROWS, D_CONV), _next(CONV_HALO, D_CONV, CONV_ROWS, n_halo),
                  _tile(CONV_ROWS, D_CONV), _next(CONV_HALO, D_CONV, CONV_ROWS, n_halo),
                  _full((CONV_K, D_CONV)), _full((1, D_CONV)), _full((1, D_CONV))],
        out_specs=[_tile(CONV_ROWS, D_CONV), _tile(CONV_ROWS, D_CONV), _full((CONV_HALO, D_CONV)),
                   _full((8, D_CONV))],
        out_shape=[jax.ShapeDtypeStruct((seq, D_CONV), BF16), jax.ShapeDtypeStruct((seq, D_CONV), BF16),
                   jax.ShapeDtypeStruct((CONV_HALO, D_CONV), F32), jax.ShapeDtypeStruct((8, D_CONV), F32)],
        scratch_shapes=[pltpu.VMEM((ext, D_CONV), F32), pltpu.VMEM((ext, D_CONV), F32),
                        pltpu.VMEM((CONV_ROWS, D_CONV), F32)],
        args=[zr, zr, zr, zr, u1, u1, du3, du3, w_dw, g_ln, b_ln], comm=comm)


MERGE_ROWS = 256


def _merge_fwd(ao, u3, zr, w_ao, w_co, b_co):
    seq = ao.shape[0]

    def body(ao_ref, u3_ref, ga_ref, gb_ref, wa_ref, wc_ref, bc_ref, y_ref, a_ref, cb_ref):
        a = jnp.dot(ao_ref[...], wa_ref[...], preferred_element_type=F32)
        cb = jnp.dot(u3_ref[...], wc_ref[...], preferred_element_type=F32) + bc_ref[...]
        a_ref[...] = a
        cb_ref[...] = cb
        y_ref[...] = (_sig(ga_ref[...]) * a + _sig(gb_ref[...]) * cb).astype(BF16)

    f32_out = jax.ShapeDtypeStruct((seq, D_MODEL), F32)
    return pl.pallas_call(
        body, name="merge_fwd",
        out_shape=[jax.ShapeDtypeStruct((seq, D_MODEL), BF16), f32_out, f32_out],
        grid=(seq // MERGE_ROWS,),
        in_specs=[_tile(MERGE_ROWS, D_ATTN), _tile(MERGE_ROWS, D_CONV), _tile(MERGE_ROWS, D_MODEL, 1),
                  _tile(MERGE_ROWS, D_MODEL, 2), _full(w_ao.shape), _full(w_co.shape), _full((1, D_MODEL))],
        out_specs=[_tile(MERGE_ROWS, D_MODEL)] * 3, compiler_params=_cparams(1),
    )(ao, u3, zr, zr, w_ao, w_co, b_co)


def _merge_bwd(a, cb, zr, rows):
    seq = a.shape[0]

    def fn(dy_v, first, ins, outs):
        a_ref, cb_ref, ga_ref, gb_ref = ins
        da_ref, dcb_ref, dga_ref, dgb_ref, small_ref = outs

        @pl.when(first)
        def _():
            small_ref[...] = jnp.zeros_like(small_ref)

        sa, sb = _sig(ga_ref[...]), _sig(gb_ref[...])
        dcb = dy_v * sb
        dga = dy_v * a_ref[...] * sa * (1.0 - sa)
        dgb = dy_v * cb_ref[...] * sb * (1.0 - sb)
        da_ref[...] = (dy_v * sa).astype(BF16)
        dcb_ref[...] = dcb.astype(BF16)
        dga_ref[...] = dga.astype(BF16)
        dgb_ref[...] = dgb.astype(BF16)
        small_ref[0:1, :] += _colsum(dga)
        small_ref[1:2, :] += _colsum(dgb)
        small_ref[2:3, :] += _colsum(dcb)

    bf = jax.ShapeDtypeStruct((seq, D_MODEL), BF16)
    gate = lambda col: pl.BlockSpec((rows, D_MODEL), lambda i, j: (i, col))
    return _Epilogue(
        [a, cb, zr, zr], [_row_tile(rows, D_MODEL), _row_tile(rows, D_MODEL), gate(1), gate(2)],
        [bf, bf, bf, bf, jax.ShapeDtypeStruct((8, D_MODEL), F32)],
        [_row_tile(rows, D_MODEL)] * 4 + [_whole((8, D_MODEL))], fn, keep_product=False)


FFN_ROWS = 2048
FFN_BLOCKS = D_FF // FFN_COLS
GELU_C = math.sqrt(2.0 / math.pi)


def _gelu(v):
    t = jnp.tanh(GELU_C * (v + 0.044715 * (v * v * v)))
    return 0.5 * v * (1.0 + t), t


def _gelu_grad(v, t):
    return 0.5 * (1.0 + t) + 0.5 * v * (1.0 - t * t) * (GELU_C * (1.0 + 3.0 * 0.044715 * (v * v)))


def _sublane_rows(ref, n):
    return [jnp.broadcast_to(ref[r:r + 1, :], (8, FFN_COLS)) for r in range(n)]


def _rolls(tile, shifts):
    return tuple(pltpu.roll(tile, s, 0) for s in shifts)


def _behind(prev_rolls, cur, row_id):
    rolls = _rolls(cur, (1, 2))
    x1 = jnp.where(row_id < 1, prev_rolls[0], rolls[0])
    x2 = jnp.where(row_id < 2, prev_rolls[1], rolls[1])
    return (x2, x1, cur), rolls


def _ahead(cur_rolls, next_rolls, row_id):
    return (jnp.where(row_id < 7, cur_rolls[0], next_rolls[0]), jnp.where(row_id < 6, cur_rolls[1], next_rolls[1]))


def _conv3(taps, w, bias):
    return w[0] * taps[0] + w[1] * taps[1] + w[2] * taps[2] + bias


def _ffn_specs(rows):
    tile = lambda off: pl.BlockSpec((rows, FFN_COLS), lambda j, i: (i, j + off))
    prev = lambda off: pl.BlockSpec((FFN_HALO, FFN_COLS),
                                    lambda j, i: (jnp.maximum(i * (rows // FFN_HALO) - 1, 0), j + off))
    wgt = lambda off: pl.BlockSpec((3, FFN_COLS), lambda j, i: (0, j + off))
    vec = lambda off: pl.BlockSpec((1, FFN_COLS), lambda j, i: (0, j + off))
    return tile, prev, wgt, vec


def _ffn_act(up, w_dw, b_dw):
    seq = up.shape[0]
    tile, prev, wgt, vec = _ffn_specs(FFN_ROWS)

    def body(v_ref, g_ref, vp_ref, gp_ref, wv_ref, wg_ref, bv_ref, bg_ref, act_ref):
        first = pl.program_id(1) == 0
        row_id = lax.broadcasted_iota(jnp.int32, (8, FFN_COLS), 0)
        wv, wg = _sublane_rows(wv_ref, 3), _sublane_rows(wg_ref, 3)
        (bv,), (bg,) = _sublane_rows(bv_ref, 1), _sublane_rows(bg_ref, 1)
        rolls_v = _rolls(jnp.where(first, 0.0, vp_ref[...]), (1, 2))
        rolls_g = _rolls(jnp.where(first, 0.0, gp_ref[...]), (1, 2))
        for row in range(0, FFN_ROWS, 16):
            halves = []
            for r in (row, row + 8):
                taps_v, rolls_v = _behind(rolls_v, v_ref[r:r + 8, :], row_id)
                taps_g, rolls_g = _behind(rolls_g, g_ref[r:r + 8, :], row_id)
                halves.append(_gelu(_conv3(taps_g, wg, bg))[0] * _conv3(taps_v, wv, bv))
            act_ref[row:row + 16, :] = jnp.concatenate(halves, axis=0).astype(BF16)

    return pl.pallas_call(
        body, name="ffn_act", out_shape=jax.ShapeDtypeStruct((seq, D_FF), BF16),
        grid=(FFN_BLOCKS, seq // FFN_ROWS),
        in_specs=[tile(0), tile(FFN_BLOCKS), prev(0), prev(FFN_BLOCKS), wgt(0), wgt(FFN_BLOCKS),
                  vec(0), vec(FFN_BLOCKS)],
        out_specs=tile(0), compiler_params=_cparams(2),
    )(up, up, up, up, w_dw, w_dw, b_dw, b_dw)


def _ffn_act_bwd(up, dact, w_dw, b_dw, comm=None):
    seq = up.shape[0]
    n_tiles = seq // FFN_ROWS
    n_halo = seq // FFN_HALO
    tile, prev, wgt, vec = _ffn_specs(FFN_ROWS)
    nxt = lambda off: pl.BlockSpec(
        (FFN_HALO, FFN_COLS), lambda j, i: (jnp.minimum((i + 1) * (FFN_ROWS // FFN_HALO), n_halo - 1), j + off))
    acc = lambda off: pl.BlockSpec((8, FFN_COLS), lambda j, i: (0, j + off))

    def body(v_ref, g_ref, vp_ref, gp_ref, vn_ref, gn_ref, da_ref, dan_ref, wv_ref, wg_ref, bv_ref, bg_ref,
             dv_out, dg_out, dwv_ref, dwg_ref, dbv_ref, dbg_ref):
        i = pl.program_id(1)
        first, last = i == 0, i == n_tiles - 1

        @pl.when(first)
        def _():
            for r in (dwv_ref, dwg_ref, dbv_ref, dbg_ref):
                r[...] = jnp.zeros_like(r)

        row_id = lax.broadcasted_iota(jnp.int32, (8, FFN_COLS), 0)
        wv, wg = _sublane_rows(wv_ref, 3), _sublane_rows(wg_ref, 3)
        (bv,), (bg,) = _sublane_rows(bv_ref, 1), _sublane_rows(bg_ref, 1)
        zero = jnp.zeros((8, FFN_COLS), F32)
        sums_v, sums_g = [zero] * 4, [zero] * 4
        rolls_v = _rolls(jnp.where(first, 0.0, vp_ref[...]), (1, 2))
        rolls_g = _rolls(jnp.where(first, 0.0, gp_ref[...]), (1, 2))
        behind = None
        done_v, done_g = [], []

        def grads(v_tile, g_tile, dact, rolls_v, rolls_g):
            taps_v, rolls_v = _behind(rolls_v, v_tile, row_id)
            taps_g, rolls_g = _behind(rolls_g, g_tile, row_id)
            val, gate = _conv3(taps_v, wv, bv), _conv3(taps_g, wg, bg)
            gel, t = _gelu(gate)
            return dact * gel, dact * val * _gelu_grad(gate, t), taps_v, taps_g, rolls_v, rolls_g

        def finish(tile, nxt, row):
            for (d, d_rolls), (_, n_rolls), w, done, o_ref in ((tile[0], nxt[0], wv, done_v, dv_out),
                                                               (tile[1], nxt[1], wg, done_g, dg_out)):
                d1, d2 = _ahead(d_rolls, n_rolls, row_id)
                done.append(w[2] * d + w[1] * d1 + w[0] * d2)
                if len(done) == 2:
                    o_ref[row - 16:row, :] = jnp.concatenate(done, axis=0).astype(BF16)
                    done.clear()

        for row in range(0, FFN_ROWS, 16):
            dact16 = da_ref[row:row + 16, :].astype(F32)
            for r, dact in ((row, dact16[0:8, :]), (row + 8, dact16[8:16, :])):
                dval, dgate, taps_v, taps_g, rolls_v, rolls_g = grads(v_ref[r:r + 8, :], g_ref[r:r + 8, :], dact,
                                                                      rolls_v, rolls_g)
                sums_v = [s + dval * x for s, x in zip(sums_v, taps_v)] + [sums_v[3] + dval]
                sums_g = [s + dgate * x for s, x in zip(sums_g, taps_g)] + [sums_g[3] + dgate]
                tile = ((dval, _rolls(dval, (7, 6))), (dgate, _rolls(dgate, (7, 6))))
                if behind is not None:
                    finish(behind, tile, r)
                behind = tile
        dact_next = jnp.where(last, 0.0, dan_ref[...].astype(F32)[0:FFN_HALO, :])
        dval, dgate, *_ = grads(vn_ref[...], gn_ref[...], dact_next, rolls_v, rolls_g)
        finish(behind, ((dval, _rolls(dval, (7, 6))), (dgate, _rolls(dgate, (7, 6)))), FFN_ROWS)
        for sums, dw_ref, db_ref in ((sums_v, dwv_ref, dbv_ref), (sums_g, dwg_ref, dbg_ref)):
            for tap in range(3):
                dw_ref[tap:tap + 1, :] += _colsum(sums[tap])
            db_ref[0:1, :] += _colsum(sums[3])

    half = jax.ShapeDtypeStruct((seq, D_FF), BF16)
    acc_shape = jax.ShapeDtypeStruct((8, D_FF), F32)
    return _host_call(
        lambda ins, outs, scratch: body(*ins, *outs, *scratch), "ffn_act_bwd", grid=(FFN_BLOCKS, n_tiles),
        in_specs=[tile(0), tile(FFN_BLOCKS), prev(0), prev(FFN_BLOCKS), nxt(0), nxt(FFN_BLOCKS),
                  tile(0), pl.BlockSpec((16, FFN_COLS), lambda j, i: (
                      jnp.minimum((i + 1) * (FFN_ROWS // 16), seq // 16 - 1), j)),
                  wgt(0), wgt(FFN_BLOCKS), vec(0), vec(FFN_BLOCKS)],
        out_specs=[tile(0), tile(0), acc(0), acc(0), acc(0), acc(0)],
        out_shape=[half, half, acc_shape, acc_shape, acc_shape, acc_shape],
        scratch_shapes=[], args=[up, up, up, up, up, up, dact, dact, w_dw, w_dw, b_dw, b_dw], comm=comm)


def _cols_to_blocks(full_cols):
    k, n8 = full_cols.shape
    return jnp.transpose(full_cols.reshape(k, N_DEV, n8 // N_DEV), (1, 0, 2))


def _rows_to_blocks(full_rows):
    r8, n = full_rows.shape
    return full_rows.reshape(N_DEV, r8 // N_DEV, n)


def _blocks_to_cols(gathered):
    _, k, n = gathered.shape
    return jnp.transpose(gathered, (1, 0, 2)).reshape(k, N_DEV * n)


def kernel(x, c, w_ada, b_ada, g_pre_mix, g_post_mix, w_in, b_in, rel_bias, w_attn_o, w_dw_conv, b_dw_conv, g_conv_ln, b_conv_ln, w_conv_o, b_conv_o, w_mix_o, g_pre_ffn, g_post_ffn, w_up, w_dw_ffn, b_dw_ffn, w_down, loss_target, m_w_ada, m_b_ada, m_g_pre_mix, m_g_post_mix, m_w_in, m_b_in, m_rel_bias, m_w_attn_o, m_w_dw_conv, m_b_dw_conv, m_g_conv_ln, m_b_conv_ln, m_w_conv_o, m_b_conv_o, m_w_mix_o, m_g_pre_ffn, m_g_post_ffn, m_w_up, m_w_dw_ffn, m_b_dw_ffn, m_w_down, v_w_ada, v_b_ada, v_g_pre_mix, v_g_post_mix, v_w_in, v_b_in, v_rel_bias, v_w_attn_o, v_w_dw_conv, v_b_dw_conv, v_g_conv_ln, v_b_conv_ln, v_w_conv_o, v_b_conv_o, v_w_mix_o, v_g_pre_ffn, v_g_post_ffn, v_w_up, v_w_dw_ffn, v_b_dw_ffn, v_w_down):
    names = ["w_ada", "b_ada", "g_pre_mix", "g_post_mix", "w_in", "b_in", "rel_bias", "w_attn_o", "w_dw_conv",
             "b_dw_conv", "g_conv_ln", "b_conv_ln", "w_conv_o", "b_conv_o", "w_mix_o", "g_pre_ffn", "g_post_ffn",
             "w_up", "w_dw_ffn", "b_dw_ffn", "w_down"]
    weights = dict(zip(names, [w_ada, b_ada, g_pre_mix, g_post_mix, w_in, b_in, rel_bias, w_attn_o, w_dw_conv,
                               b_dw_conv, g_conv_ln, b_conv_ln, w_conv_o, b_conv_o, w_mix_o, g_pre_ffn,
                               g_post_ffn, w_up, w_dw_ffn, b_dw_ffn, w_down]))
    mom_m = dict(zip(names, [m_w_ada, m_b_ada, m_g_pre_mix, m_g_post_mix, m_w_in, m_b_in, m_rel_bias, m_w_attn_o,
                             m_w_dw_conv, m_b_dw_conv, m_g_conv_ln, m_b_conv_ln, m_w_conv_o, m_b_conv_o,
                             m_w_mix_o, m_g_pre_ffn, m_g_post_ffn, m_w_up, m_w_dw_ffn, m_b_dw_ffn, m_w_down]))
    mom_v = dict(zip(names, [v_w_ada, v_b_ada, v_g_pre_mix, v_g_post_mix, v_w_in, v_b_in, v_rel_bias, v_w_attn_o,
                             v_w_dw_conv, v_b_dw_conv, v_g_conv_ln, v_b_conv_ln, v_w_conv_o, v_b_conv_o,
                             v_w_mix_o, v_g_pre_ffn, v_g_post_ffn, v_w_up, v_w_dw_ffn, v_b_dw_ffn, v_w_down]))
    shapes = {n: w.shape for n, w in weights.items()}

    seq = x.shape[1]
    me = 4 * lax.axis_index("x") + 2 * lax.axis_index("y") + lax.axis_index("c")
    x2 = x.reshape(seq, D_MODEL)
    target = loss_target.reshape(seq, D_MODEL)
    sq = lambda a: a.reshape(a.shape[1:])
    bf = lambda a: sq(a).astype(BF16)

    transposed = lambda a: jnp.swapaxes(sq(a), 0, 1)

    c_all, mod_all = _ada_mod(c, sq(w_ada))
    c_all = c_all.reshape(N_DEV, D_MODEL)
    mod = lax.dynamic_index_in_dim(mod_all, me, axis=1, keepdims=False)
    mod6 = (mod.reshape(1, 6 * D_MODEL) + b_ada).reshape(6, D_MODEL)

    h1, (g_dwc, g_dwf) = _pre_mix(x2, mod6, g_pre_mix, comm=_gather_comm([sq(w_dw_conv), sq(w_dw_ffn)]))
    wf_dwc = _blocks_to_cols(g_dwc)
    wf_dwf = _blocks_to_cols(g_dwf)
    g_in, qkv, zr = _in_proj(h1, transposed(w_in).astype(BF16), b_in)
    wt_in = g_in.reshape(g_in.shape[0] * g_in.shape[1], D_MODEL)
    table = jnp.transpose(_bias_table(sq(rel_bias)), (1, 0, 2))
    ao, (g_up, g_ao, g_co, g_mo) = _attn_fwd(
        qkv, table, comm=_gather_comm([transposed(w_up).astype(BF16), bf(w_attn_o), bf(w_conv_o), bf(w_mix_o)]))
    (u1, u3), (g_dn,) = _conv_fwd(zr, wf_dwc, b_dw_conv, g_conv_ln, b_conv_ln, comm=_gather_comm([bf(w_down)]))
    wf_ao = _blocks_to_cols(g_ao)
    wf_co = _blocks_to_cols(g_co)
    wf_mo = g_mo.reshape(D_MODEL, D_MODEL)
    wt_up = g_up.reshape(g_up.shape[0] * g_up.shape[1], D_MODEL)
    wf_dn = g_dn.reshape(D_FF, D_MODEL)
    y, a_br, cb_br = _merge_fwd(ao, u3, zr, wf_ao, wf_co, b_conv_o)
    ymix, (x1, h2), _ = _mm(y, wf_mo, "nn", F32, "mix_o", tm=512, tn=D_MODEL,
                            epilogue=_post_mix_pre_ffn(x2, mod6, g_post_mix, g_pre_ffn, 512))
    up = _mm(h2, wt_up, "nt", F32, "ffn_up", tm=1024, tn=1408)
    act = _ffn_act(up, wf_dwf, b_dw_ffn)
    _, (loss_lanes, dout, dyf, small_f), _ = _mm(act, wf_dn, "nn", F32, "ffn_down", tm=512, tn=D_MODEL,
                                                 epilogue=_final(x1, target, mod6, g_post_ffn, 512))

    dact = _mm(dyf, wf_dn, "nt", BF16, "ffn_down_dx", tm=1024, tn=1408)
    gw_down = _mm(act, dyf, "tn", BF16, "ffn_down_dw", tm=256, tn=1024)
    (dup_v, dup_g, dwv, dwg, dbv, dbg), (parts_down,) = _ffn_act_bwd(
        up, dact, wf_dwf, b_dw_ffn, comm=_scatter_comm([_rows_to_blocks(gw_down)]))
    _, (dx1, dymix, small_m), _ = _mm([dup_v, dup_g], wt_up, "nn", F32, "ffn_up_dx", tm=512, tn=D_MODEL,
                                      epilogue=_mid_bwd(x1, dout, ymix, mod6, g_pre_ffn, g_post_mix, 512))
    blocks_up = _rows_to_blocks(_mm_tn_rows([dup_v, dup_g], h2, "ffn_up_dw"))
    _, (da, dcb, dga, dgb, small_g), _ = _mm(dymix, wf_mo, "nt", F32, "mix_o_dx", tm=512, tn=D_MODEL,
                                             epilogue=_merge_bwd(a_br, cb_br, zr, 512))
    gw_mo = _mm(y, dymix, "tn", BF16, "mix_o_dw")
    dao = _mm(da, wf_ao, "nt", BF16, "attn_o_dx", tm=1024)
    gw_ao = _mm(ao, da, "tn", BF16, "attn_o_dw")
    du3 = _mm(dcb, wf_co, "nt", F32, "conv_o_dx", tm=1024)
    gw_co = _mm(u3, dcb, "tn", BF16, "conv_o_dw")
    (dq, dkt, dvt, dbias, small_a), (parts_up,) = _attn_bwd(
        qkv, table, dao, comm=_scatter_comm([blocks_up]))
    g_rel = _bias_grad(jnp.transpose(dbias, (1, 0, 2)))
    (dglu_a, dglu_b, dw_conv, small_c), (parts_mo, parts_ao, parts_co) = _conv_bwd(
        zr, u1, du3, wf_dwc, g_conv_ln, b_conv_ln,
        comm=_scatter_comm([_rows_to_blocks(gw_mo), _cols_to_blocks(gw_ao), _cols_to_blocks(gw_co)]))
    dk, dv = _dk_dv(dkt, dvt, seq)
    dz = [dq, dk, dv, dglu_a, dglu_b, dga, dgb]
    dz_halves = [(p, c) for p in dz for c in range(p.shape[1] // D_ATTN)]
    blocks_in = _rows_to_blocks(_mm_tn_rows(dz, h1, "in_proj_dw"))
    _, (grad_x, small_x), (parts_in, _, _) = _mm(dz_halves, wt_in, "nn", F32, "in_proj_dx", tm=512, tn=D_MODEL,
                                                 k_piece=D_ATTN,
                                                 comm=_pair_scatter_comm(blocks_in),
                                                 epilogue=_pre_mix_bwd(x2, dx1, mod6, g_pre_mix, 512))

    packed = _pack_grads(small_x, small_m, small_f, small_g, small_a, small_c, dbv, dbg, dwv, dwg, dw_conv)
    gathered, gathered_rel, gathered_loss = _run_comm(_gather_comm([packed, g_rel, loss_lanes]), "gather_small")
    gathered = gathered.reshape(N_DEV, PACKED_TOTAL)
    updates, g_dwc_full, g_dwf_full, loss_all = _small_adamw(gathered, gathered_rel, gathered_loss, weights, mom_m,
                                                             mom_v)
    loss = loss_all[0, 0]

    grads, deltas, new_m, new_v = {}, {}, {}, {}

    def record(name, update, is_transposed=False):
        for dst, val in zip((grads, deltas, new_m, new_v), update):
            dst[name] = (jnp.swapaxes(val, 0, 1) if is_transposed else val).reshape(shapes[name])

    for name, update in updates.items():
        record(name, update)

    def local_update(name, grad, view=sq):
        record(name, _adamw(view(weights[name]), view(mom_m[name]), view(mom_v[name]), "adamw_" + name, g=view(grad)))

    def taps_major(a):
        return a.reshape(a.shape[1], 1, a.shape[2])

    conv_cols, ffn_cols, ada_cols = D_CONV // N_DEV, 2 * D_FF // N_DEV, 6 * D_MODEL // N_DEV
    local_update("w_dw_conv", lax.dynamic_slice(g_dwc_full, (0, me * conv_cols), (CONV_K, conv_cols))[None], taps_major)
    local_update("w_dw_ffn", lax.dynamic_slice(g_dwf_full, (0, me * ffn_cols), (3, ffn_cols))[None], taps_major)
    local_update("w_ada", _ada_grad(c_all, lax.dynamic_slice(gathered, (0, me * ada_cols), (N_DEV, ada_cols)))[None])

    for name, part in (("w_attn_o", parts_ao), ("w_conv_o", parts_co), ("w_mix_o", parts_mo), ("w_down", parts_down)):
        record(name, _adamw(sq(weights[name]), sq(mom_m[name]), sq(mom_v[name]), "adamw_" + name, parts=part))
    for name, part in (("w_in", parts_in), ("w_up", parts_up)):
        record(name, _adamw(transposed(weights[name]), transposed(mom_m[name]), transposed(mom_v[name]),
                            "adamw_" + name, parts=part), is_transposed=True)

    return (loss, grad_x.reshape(x.shape), *[grads[n] for n in names], *[deltas[n] for n in names],
            *[new_m[n] for n in names], *[new_v[n] for n in names])
```

```python
import functools
import math

import jax
import jax.numpy as jnp
from jax import lax
from jax.experimental import pallas as pl
from jax.experimental.pallas import tpu as pltpu

F32 = jnp.float32
BF16 = jnp.bfloat16
HIGHEST = lax.Precision.HIGHEST

D_MODEL = 1024
CHUNK = 64
LEFT_CHUNKS = 8
BAND = (LEFT_CHUNKS + 1) * CHUNK
PAD_ROWS = LEFT_CHUNKS * CHUNK
GROUP = 4
GROUP_Q = GROUP * CHUNK
GROUP_K = GROUP_Q + PAD_ROWS
SOFTMAX_ROWS = 16
TOEPLITZ = 640
N_HEADS = 8
HEAD_DIM = 64
D_ATTN = 512
D_CONV = 512
CONV_K = 31
CONV_HALO = 32
MAX_REL = 128
N_REL = 2 * MAX_REL + 1
D_FF = 2816
FFN_HALO = 8
FFN_COLS = 256
EPS = 1e-6
NEG_INF = -1e30
N_DEV = 8

ADAM_LR = 0.001
ADAM_B1 = 0.9
ADAM_B2 = 0.999
ADAM_EPS = 1e-08
ADAM_WD = 0.01
ADAM_STEP = 10

VMEM_LIMIT_BYTES = 56 * 1024 * 1024
ADAMW_BLOCK_BYTES = 768 * 1024

MESH = pl.DeviceIdType.MESH
ANY = pl.BlockSpec(memory_space=pl.ANY)

SH_M, SC_M, GT_M, SH_F, SC_F, GT_F = range(6)

SMALL = (("b_ada", 6144), ("g_pre_mix", 1024), ("g_post_mix", 1024), ("b_in", 4608), ("b_dw_conv", 512),
         ("g_conv_ln", 512), ("b_conv_ln", 512), ("b_conv_o", 1024), ("g_pre_ffn", 1024), ("g_post_ffn", 1024),
         ("b_dw_ffn", 5632))
PACKED_TOTAL = sum(n for _, n in SMALL) + CONV_K * D_CONV + 3 * 2 * D_FF


def _cparams(n_axes):
    return pltpu.CompilerParams(vmem_limit_bytes=VMEM_LIMIT_BYTES,
                                dimension_semantics=("arbitrary",) * n_axes)


def _sig(v):
    return 1.0 / (1.0 + jnp.exp(-v))


def _pick(n, target):
    if n <= target:
        return n
    t = target - target % 128
    while n % t:
        t -= 128
    return t


def _tile(rows, cols, col=0):
    return pl.BlockSpec((rows, cols), lambda i: (i, col))


def _full(shape):
    zeros = (0,) * len(shape)
    return pl.BlockSpec(shape, lambda i: zeros)


def _prev(halo, cols, rows, col=0):
    return pl.BlockSpec((halo, cols), lambda i: (jnp.maximum(i * (rows // halo) - 1, 0), col))


def _next(halo, cols, rows, n_blocks, col=0):
    return pl.BlockSpec((halo, cols), lambda i: (jnp.minimum((i + 1) * (rows // halo), n_blocks - 1), col))


class _Comm:
    def __init__(self, inputs, out_shapes, sems, start, finish, relay=None, early=None):
        self.inputs, self.out_shapes, self.sems, self.start, self.finish = inputs, out_shapes, sems, start, finish
        self.relay, self.early = relay, early


def _host_call(body, name, grid, in_specs, out_specs, out_shape, scratch_shapes, args, comm=None):
    n_in, n_out, n_scr = len(args), len(out_shape), len(scratch_shapes)
    c_in = list(comm.inputs) if comm else []
    c_out = list(comm.out_shapes) if comm else []
    c_sem = list(comm.sems) if comm else []

    def full(*refs):
        bounds = [0, n_in, len(c_in), n_out, len(c_out), n_scr, len(c_sem)]
        cuts = [sum(bounds[:i + 1]) for i in range(len(bounds))]
        ins, cins, outs, couts, scr, csems = (refs[lo:hi] for lo, hi in zip(cuts[:-1], cuts[1:]))
        if comm:
            first = functools.reduce(jnp.logical_and, [pl.program_id(ax) == 0 for ax in range(len(grid))])
            pl.when(first)(lambda: comm.start(cins, couts, csems))
            if comm.early is not None:
                strides = [math.prod(grid[ax + 1:]) for ax in range(len(grid))]
                step = sum(pl.program_id(ax) * strides[ax] for ax in range(len(grid)))
                pl.when(step == 1)(lambda: comm.early(cins, couts, csems))
            last = functools.reduce(jnp.logical_and, [pl.program_id(ax) == grid[ax] - 1 for ax in range(len(grid))])
            if comm.relay is not None:
                pl.when(last)(lambda: comm.relay(cins, couts, csems))
        body(ins, outs, scr)
        if comm:
            pl.when(last)(lambda: comm.finish(cins, couts, csems))

    res = pl.pallas_call(
        full, name=name, grid=grid, in_specs=list(in_specs) + [ANY] * len(c_in),
        out_specs=list(out_specs) + [ANY] * len(c_out), out_shape=list(out_shape) + c_out,
        scratch_shapes=list(scratch_shapes) + c_sem, compiler_params=_cparams(len(grid)),
    )(*args, *c_in)
    return list(res[:n_out]), list(res[n_out:])


def _run_comm(comm, name):
    n_in, n_out = len(comm.inputs), len(comm.out_shapes)

    def body(*refs):
        ins, outs, sems = refs[:n_in], refs[n_in:n_in + n_out], refs[n_in + n_out:]
        comm.start(ins, outs, sems)
        if comm.relay is not None:
            comm.relay(ins, outs, sems)
        comm.finish(ins, outs, sems)

    return pl.pallas_call(
        body, name=name, out_shape=list(comm.out_shapes), in_specs=[ANY] * n_in, out_specs=[ANY] * n_out,
        scratch_shapes=list(comm.sems),
    )(*comm.inputs)


def _place():
    return lax.axis_index("x"), lax.axis_index("y"), lax.axis_index("c")


def _gather_comm(arrs):
    n = len(arrs)

    def plan(ins, outs, sems):
        send_sems, recv_sems, local_sems = sems
        x, y, c = _place()
        me, sibling = (x, y, c), (x, y, 1 - c)
        chips = [(1 - x, y), (x, 1 - y), (1 - x, 1 - y)]

        def block(k, p):
            return outs[k].at[4 * p[0] + 2 * p[1] + p[2]]

        def copy(k, s, blk, to, src=None):
            return pltpu.make_async_remote_copy(
                src_ref=block(k, blk) if src is None else src, dst_ref=block(k, blk),
                send_sem=send_sems.at[7 * k + s], recv_sem=recv_sems.at[7 * k + s],
                device_id=to, device_id_type=MESH)

        mine = [pltpu.make_async_copy(ins[k], block(k, me), local_sems.at[k]) for k in range(n)]
        first = []
        for k in range(n):
            first.append(copy(k, 0, me, sibling, src=ins[k]))
            for j, chip in enumerate(chips):
                first.append(copy(k, 1 + j, me, (*chip, c), src=ins[k]))
        return me, sibling, chips, c, copy, mine, first

    def start(ins, outs, sems):
        *_, mine, first = plan(ins, outs, sems)
        for cp in mine + first:
            cp.start()

    def relay(ins, outs, sems):
        me, sibling, chips, c, copy, _, _ = plan(ins, outs, sems)
        for j, chip in enumerate(chips):
            for k in range(n):
                copy(k, 1 + j, (*chip, c), me).wait_recv()
                copy(k, 4 + j, (*chip, c), sibling).start()

    def finish(ins, outs, sems):
        me, sibling, chips, c, copy, mine, first = plan(ins, outs, sems)
        passed = [copy(k, 4 + j, (*chip, c), sibling) for j, chip in enumerate(chips) for k in range(n)]
        for k in range(n):
            copy(k, 0, sibling, me).wait_recv()
        for j, chip in enumerate(chips):
            for k in range(n):
                copy(k, 4 + j, (*chip, 1 - c), me).wait_recv()
        for cp in first + passed:
            cp.wait_send()
        for cp in mine:
            cp.wait()

    comm = _Comm(list(arrs), [jax.ShapeDtypeStruct((N_DEV,) + a.shape, a.dtype) for a in arrs],
                 [pltpu.SemaphoreType.DMA((7 * n,)), pltpu.SemaphoreType.DMA((7 * n,)),
                  pltpu.SemaphoreType.DMA((n,))], start, finish, relay)
    comm.plan = plan
    return comm


def _scatter_comm(blocks):
    n = len(blocks)

    def plan(ins, outs, sems, arrivals):
        send_sems, recv_sems, local_sems = sems
        x, y, c = _place()
        me = 4 * x + 2 * y + c
        local = [pltpu.make_async_copy(ins[k].at[me], outs[k].at[me], local_sems.at[k]) for k in range(n)]
        sends, recvs = [], []
        for k in range(n):
            for mask in range(1, N_DEV):
                px = 1 - x if mask & 4 else x
                py = 1 - y if mask & 2 else y
                pc = 1 - c if mask & 1 else c
                peer = 4 * px + 2 * py + pc
                sem = 7 * k + mask - 1
                both = dict(send_sem=send_sems.at[sem], recv_sem=recv_sems.at[sem], device_id=(px, py, pc),
                            device_id_type=MESH)
                sends.append(pltpu.make_async_remote_copy(src_ref=ins[k].at[peer], dst_ref=outs[k].at[me], **both))
                if arrivals:
                    recvs.append(pltpu.make_async_remote_copy(src_ref=ins[k].at[me], dst_ref=outs[k].at[peer],
                                                              **both))
        return local, sends, recvs

    def start(ins, outs, sems):
        local, sends, _ = plan(ins, outs, sems, arrivals=False)
        for cp in local + sends:
            cp.start()

    def finish(ins, outs, sems):
        local, sends, recvs = plan(ins, outs, sems, arrivals=True)
        for cp in recvs:
            cp.wait_recv()
        for cp in sends:
            cp.wait_send()
        for cp in local:
            cp.wait()

    return _Comm(list(blocks), [jax.ShapeDtypeStruct(b.shape, b.dtype) for b in blocks],
                 [pltpu.SemaphoreType.DMA((7 * n,)), pltpu.SemaphoreType.DMA((7 * n,)),
                  pltpu.SemaphoreType.DMA((n,))], start, finish)


def _pair_scatter_comm(block):
    _, r, c = block.shape
    quarter = jax.ShapeDtypeStruct((4, r, c), block.dtype)

    def plan(ins, outs, sems):
        parts, got, pair = outs
        d2d_send, d2d_recv, ici_send, ici_recv, local, *bufs = sems
        x, y, cc = _place()
        mine = 2 * x + y
        chips = [(1 - x, y), (x, 1 - y), (1 - x, 1 - y)]
        to_sibling = [pltpu.make_async_remote_copy(
            src_ref=ins[0].at[2 * q + 1 - cc], dst_ref=got.at[q], send_sem=d2d_send.at[q], recv_sem=d2d_recv.at[q],
            device_id=(x, y, 1 - cc), device_id_type=MESH) for q in range(4)]
        to_chips = [pltpu.make_async_remote_copy(
            src_ref=pair.at[2 * px + py], dst_ref=parts.at[mine], send_sem=ici_send.at[j], recv_sem=ici_recv.at[j],
            device_id=(px, py, cc), device_id_type=MESH) for j, (px, py) in enumerate(chips)]
        from_chips = [pltpu.make_async_remote_copy(
            src_ref=pair.at[mine], dst_ref=parts.at[2 * px + py], send_sem=ici_send.at[j], recv_sem=ici_recv.at[j],
            device_id=(px, py, cc), device_id_type=MESH) for j, (px, py) in enumerate(chips)]
        own = pltpu.make_async_copy(pair.at[mine], parts.at[mine], local.at[5])
        order = [2 * px + py for px, py in chips] + [mine]
        return cc, got, pair, local, bufs, order, to_sibling, to_chips, from_chips, own

    def start(ins, outs, sems):
        for cp in plan(ins, outs, sems)[6]:
            cp.start()

    def early(ins, outs, sems):
        cc, got, pair, local, bufs, order, to_sibling, to_chips, _, own = plan(ins, outs, sems)
        for cp in to_sibling:
            cp.wait_recv()

        def loads(k):
            return [pltpu.make_async_copy(ins[0].at[2 * order[k] + cc], bufs[2 * (k % 2)], local.at[k % 2]),
                    pltpu.make_async_copy(got.at[order[k]], bufs[2 * (k % 2) + 1], local.at[2 + k % 2])]

        for cp in loads(0):
            cp.start()
        for k, send in enumerate(to_chips + [own]):
            if k + 1 < len(order):
                for cp in loads(k + 1):
                    cp.start()
            for cp in loads(k):
                cp.wait()
            kept, came = bufs[2 * (k % 2)], bufs[2 * (k % 2) + 1]
            kept[...] = (kept[...].astype(F32) + came[...].astype(F32)).astype(block.dtype)
            store = pltpu.make_async_copy(kept, pair.at[order[k]], local.at[4])
            store.start()
            store.wait()
            send.start()

    def finish(ins, outs, sems):
        *_, to_sibling, to_chips, from_chips, own = plan(ins, outs, sems)
        for cp in from_chips:
            cp.wait_recv()
        for cp in to_chips + to_sibling:
            cp.wait_send()
        own.wait()

    return _Comm([block], [quarter, quarter, quarter],
                 [pltpu.SemaphoreType.DMA((4,)), pltpu.SemaphoreType.DMA((4,)), pltpu.SemaphoreType.DMA((3,)),
                  pltpu.SemaphoreType.DMA((3,)), pltpu.SemaphoreType.DMA((6,))]
                 + [pltpu.VMEM((r, c), block.dtype)] * 4, start, finish, early=early)


_DIMS = {"nn": (((1,), (0,)), ((), ())), "nt": (((1,), (1,)), ((), ())), "tn": (((0,), (0,)), ((), ()))}


class _Epilogue:
    def __init__(self, args, in_specs, out_shapes, out_specs, fn, keep_product):
        self.args, self.in_specs, self.out_shapes, self.out_specs = args, in_specs, out_shapes, out_specs
        self.fn, self.keep_product = fn, keep_product


def _row_tile(rows, cols):
    return pl.BlockSpec((rows, cols), lambda i, j: (i, 0))


def _whole(shape):
    zeros = (0,) * len(shape)
    return pl.BlockSpec(shape, lambda i, j: zeros)


def _mm(a, b, mode, out_dtype, name, bias=None, tm=512, tn=512, comm=None, cols=None, epilogue=None, k_piece=None):
    pieces = a if isinstance(a, (list, tuple)) else [a]
    piece_cols = [0] * len(pieces)
    if k_piece is not None:
        pieces, piece_cols = [p for p, _ in a], [c for _, c in a]
    assert all(p.dtype == BF16 for p in pieces) and b.dtype == BF16
    a = pieces[0]
    if mode == "tn":
        k_dim, m_dim = a.shape
    else:
        m_dim, k_dim = a.shape[0], k_piece or a.shape[1]
    n_dim = b.shape[0] if mode == "nt" else b.shape[1]
    col0 = 0
    if cols is not None:
        assert mode != "tn" and cols[0] % tn == 0 and cols[1] % tn == 0
        col0, n_dim = cols[0] // tn, cols[1]
    tm, tn = _pick(m_dim, tm), _pick(n_dim, tn)
    assert mode != "tn" or len(pieces) == 1
    a_specs = [pl.BlockSpec((k_dim, tm), lambda i, j: (0, i)) if mode == "tn"
               else pl.BlockSpec((tm, k_dim), lambda i, j, c=c: (i, c)) for c in piece_cols]
    once = dict(pipeline_mode=pl.Buffered(1)) if tn == n_dim else {}
    if mode == "nt":
        b_specs = [pl.BlockSpec((tn, k_dim), lambda i, j, p=p: (j + col0, p), **once) for p in range(len(pieces))]
    else:
        b_specs = [pl.BlockSpec((k_dim, tn), lambda i, j, p=p: (p, j + col0), **once) for p in range(len(pieces))]
    in_specs = a_specs + b_specs
    args = list(pieces) + [b] * len(pieces)
    if bias is not None:
        in_specs.append(pl.BlockSpec((1, tn), lambda i, j: (0, j + col0)))
        args.append(bias)
    dims = _DIMS[mode]
    n_pieces = len(pieces)
    n_own = len(args)
    keep = epilogue is None or epilogue.keep_product
    out_specs = [pl.BlockSpec((tm, tn), lambda i, j: (i, j))] if keep else []
    out_shape = [jax.ShapeDtypeStruct((m_dim, n_dim), out_dtype)] if keep else []
    if epilogue is not None:
        assert tn == n_dim
        in_specs, args = in_specs + list(epilogue.in_specs), args + list(epilogue.args)
        out_specs, out_shape = out_specs + list(epilogue.out_specs), out_shape + list(epilogue.out_shapes)

    def body(ins, outs, scratch):
        total = lax.dot_general(ins[0][...], ins[n_pieces][...], dims, preferred_element_type=F32)
        for p in range(1, n_pieces):
            total = total + lax.dot_general(ins[p][...], ins[n_pieces + p][...], dims, preferred_element_type=F32)
        if bias is not None:
            total = total + ins[2 * n_pieces][...]
        if keep:
            outs[0][...] = total.astype(out_dtype)
        if epilogue is not None:
            epilogue.fn(total, pl.program_id(0) == 0, ins[n_own:], outs[1:] if keep else outs)

    outs, extra = _host_call(body, name, grid=(m_dim // tm, n_dim // tn), in_specs=in_specs, out_specs=out_specs,
                             out_shape=out_shape, scratch_shapes=[], args=args, comm=comm)
    product = outs[0] if keep else None
    if comm is None and epilogue is None:
        return product
    return product, outs[1:] if keep else outs, extra


def _mm_tn_rows(pieces, b, name, tm=256):
    k_dim, n_dim = b.shape
    counts = [p.shape[1] // tm for p in pieces]
    assert all(p.shape[1] % tm == 0 for p in pieces)
    firsts = [sum(counts[:q]) for q in range(len(pieces))]

    def a_spec(first, count):
        return pl.BlockSpec((k_dim, tm), lambda i: (0, jnp.clip(i - first, 0, count - 1)))

    def body(ins, outs, scratch):
        i = pl.program_id(0)
        for a_ref, first, count in zip(ins[:-1], firsts, counts):
            @pl.when(jnp.logical_and(i >= first, i < first + count))
            def _(a_ref=a_ref):
                outs[0][...] = lax.dot_general(a_ref[...], ins[-1][...], _DIMS["tn"],
                                               preferred_element_type=F32).astype(BF16)

    (out,), _ = _host_call(
        body, name, grid=(sum(counts),),
        in_specs=[a_spec(f, c) for f, c in zip(firsts, counts)]
        + [pl.BlockSpec((k_dim, n_dim), lambda i: (0, 0), pipeline_mode=pl.Buffered(1))],
        out_specs=[_tile(tm, n_dim)], out_shape=[jax.ShapeDtypeStruct((sum(counts) * tm, n_dim), BF16)],
        scratch_shapes=[], args=list(pieces) + [b])
    return out


IN_PROJ_ROWS = 1024
OUT_COLS = 384


def _in_proj(h1, w_mine, bias):
    seq, shard = h1.shape[0], w_mine.shape[0]
    chip_cols = 2 * shard
    per_chip = chip_cols // OUT_COLS
    n_qkv = 3 * D_ATTN // OUT_COLS
    n_rows = seq // IN_PROJ_ROWS
    n_chips = N_DEV // 2
    comm = _gather_comm([w_mine])

    def body(h_ref, b_ref, w_hbm, wall_hbm, qkv_hbm, zr_hbm, w_ref, stage_b, stage_f, w_sems, o_sems, *csems):
        t, i = pl.program_id(0), pl.program_id(1)
        step = t * n_rows + i
        me, sibling, chips, c, copy, mine, first = comm.plan([w_hbm], [wall_hbm], csems)
        slots = [2 * me[0] + me[1]] + [2 * px + py for px, py in chips]

        def chip_of(tile):
            p = slots[0]
            for k in range(1, n_chips):
                p = jnp.where(tile == k, slots[k], p)
            return p

        def out_copies(tile, row_tile):
            p = chip_of(tile)
            rows = pl.ds(pl.multiple_of(row_tile * IN_PROJ_ROWS, IN_PROJ_ROWS), IN_PROJ_ROWS)
            res = []
            for s in range(per_chip):
                b = per_chip * p + s
                q_col = pl.multiple_of(jnp.minimum(b, n_qkv - 1) * OUT_COLS, 128)
                z_col = pl.multiple_of(jnp.maximum(b - n_qkv, 0) * OUT_COLS, 128)
                res.append((b < n_qkv,
                            pltpu.make_async_copy(stage_b.at[s], qkv_hbm.at[rows, pl.ds(q_col, OUT_COLS)], o_sems.at[s]),
                            pltpu.make_async_copy(stage_f.at[s], zr_hbm.at[rows, pl.ds(z_col, OUT_COLS)],
                                                  o_sems.at[per_chip + s])))
            return res

        def wait_out(tile, row_tile):
            for to_qkv_cond, to_qkv, to_zr in out_copies(tile, row_tile):
                pl.when(to_qkv_cond)(to_qkv.wait)
                pl.when(jnp.logical_not(to_qkv_cond))(to_zr.wait)

        def load_pair(p):
            loads = [pltpu.make_async_copy(wall_hbm.at[2 * p + h], w_ref.at[pl.ds(h * shard, shard)], w_sems.at[h])
                     for h in range(2)]
            for cp in loads:
                cp.start()
            for cp in loads:
                cp.wait()

        to_sibling, to_x, to_y, to_diagonal = first

        @pl.when(step == 0)
        def _():
            for cp in mine + [to_sibling, to_x, to_y]:
                cp.start()
            mine[0].wait()
            copy(0, 0, sibling, me).wait_recv()
            load_pair(slots[0])

        for j, chip in enumerate(chips):
            @pl.when(step == (j + 1) * n_rows)
            def _(j=j, chip=chip):
                copy(0, 1 + j, (*chip, c), me).wait_recv()
                copy(0, 4 + j, (*chip, c), sibling).start()
                if j == 0:
                    to_x.wait_send()
                    to_y.wait_send()
                    to_diagonal.start()
                copy(0, 4 + j, (*chip, 1 - c), me).wait_recv()
                load_pair(slots[j + 1])

        rows = pl.ds(pl.multiple_of(i * IN_PROJ_ROWS, IN_PROJ_ROWS), IN_PROJ_ROWS)
        prod = lax.dot_general(h_ref[rows, :], w_ref[...], _DIMS["nt"], preferred_element_type=F32) + b_ref[chip_of(t)]

        @pl.when(step > 0)
        def _():
            wait_out((step - 1) // n_rows, (step - 1) % n_rows)

        for s, (to_qkv_cond, to_qkv, to_zr) in enumerate(out_copies(t, i)):
            part = prod[:, s * OUT_COLS:(s + 1) * OUT_COLS]

            @pl.when(to_qkv_cond)
            def _(s=s, part=part, to_qkv=to_qkv):
                stage_b[s] = part.astype(BF16)
                to_qkv.start()

            @pl.when(jnp.logical_not(to_qkv_cond))
            def _(s=s, part=part, to_zr=to_zr):
                stage_f[s] = part
                to_zr.start()

        @pl.when(step == n_chips * n_rows - 1)
        def _():
            wait_out(t, i)
            passed = [copy(0, 4 + j, (*chip, c), sibling) for j, chip in enumerate(chips)]
            for cp in [to_sibling, to_diagonal] + passed:
                cp.wait_send()

    return pl.pallas_call(
        body, name="in_proj", grid=(n_chips, n_rows),
        in_specs=[pl.BlockSpec((seq, D_MODEL), lambda t, i: (0, 0), pipeline_mode=pl.Buffered(1)),
                  pl.BlockSpec((n_chips, 1, chip_cols), lambda t, i: (0, 0, 0)), ANY],
        out_specs=[ANY, ANY, ANY],
        out_shape=[comm.out_shapes[0], jax.ShapeDtypeStruct((seq, n_qkv * OUT_COLS), BF16),
                   jax.ShapeDtypeStruct((seq, N_DEV * shard - n_qkv * OUT_COLS), F32)],
        scratch_shapes=[pltpu.VMEM((chip_cols, D_MODEL), BF16), pltpu.VMEM((per_chip, IN_PROJ_ROWS, OUT_COLS), BF16),
                        pltpu.VMEM((per_chip, IN_PROJ_ROWS, OUT_COLS), F32), pltpu.SemaphoreType.DMA((2,)),
                        pltpu.SemaphoreType.DMA((2 * per_chip,))] + list(comm.sems),
        compiler_params=_cparams(2),
    )(h1, bias.reshape(n_chips, 1, chip_cols), w_mine)


def _adam_math(w, g, m, v):
    m = ADAM_B1 * m + (1.0 - ADAM_B1) * g
    v = ADAM_B2 * v + (1.0 - ADAM_B2) * (g * g)
    m_hat = m / (1.0 - ADAM_B1 ** ADAM_STEP)
    v_hat = v / (1.0 - ADAM_B2 ** ADAM_STEP)
    delta = -ADAM_LR * (m_hat / (jnp.sqrt(v_hat) + ADAM_EPS) + ADAM_WD * w)
    return delta, m, v


def _adamw(w, m, v, name, g=None, parts=None):
    rows, cols = w.shape[0], w.shape[-1]
    tr = rows
    if rows * cols * 4 > ADAMW_BLOCK_BYTES:
        tr = max(t for t in range(16, rows, 16) if rows % t == 0 and t * cols * 4 <= ADAMW_BLOCK_BYTES)

    def body(w_ref, m_ref, v_ref, g_ref, go_ref, d_ref, mo_ref, vo_ref):
        if parts is None:
            grad = g_ref[...]
        else:
            grad = g_ref[0].astype(F32)
            for d in range(1, parts.shape[0]):
                grad = grad + g_ref[d].astype(F32)
        delta, m_new, v_new = _adam_math(w_ref[...], grad, m_ref[...], v_ref[...])
        go_ref[...] = grad
        d_ref[...] = delta
        mo_ref[...] = m_new
        vo_ref[...] = v_new

    spec = _tile(tr, cols) if w.ndim == 2 else _full(w.shape)
    g_spec = spec if parts is None else pl.BlockSpec((parts.shape[0], tr, cols), lambda i: (0, i, 0))
    shape = jax.ShapeDtypeStruct(w.shape, F32)
    return pl.pallas_call(
        body, name=name, out_shape=[shape] * 4, grid=(rows // tr,),
        in_specs=[spec, spec, spec, g_spec], out_specs=[spec] * 4, compiler_params=_cparams(1),
    )(w, m, v, g if parts is None else parts)


def _pack_grads(small_x, small_m, small_f, small_g, small_a, small_c, dbv, dbg, dwv, dwg, dw_conv):
    pieces = [
        (small_x, 2, D_MODEL), (small_x, 1, D_MODEL), (small_m, 4, D_MODEL), (small_m, 2, D_MODEL),
        (small_m, 1, D_MODEL), (small_f, 1, D_MODEL),
        (small_x, 0, D_MODEL), (small_m, 3, D_MODEL),
        (small_a, 0, D_ATTN), (small_a, 1, D_ATTN), (small_a, 2, D_ATTN), (small_c, 3, D_CONV),
        (small_c, 4, D_CONV), (small_g, 0, D_MODEL), (small_g, 1, D_MODEL),
        (small_c, 0, D_CONV), (small_c, 1, D_CONV), (small_c, 2, D_CONV),
        (small_g, 2, D_MODEL), (small_m, 0, D_MODEL), (small_f, 0, D_MODEL),
        (dbv, 0, D_FF), (dbg, 0, D_FF),
    ]
    pieces += [(dw_conv, j, D_CONV) for j in range(CONV_K)]
    pieces += [(src, tap, D_FF) for tap in range(3) for src in (dwv, dwg)]
    sources = [small_x, small_m, small_f, small_g, small_a, small_c, dbv, dbg, dwv, dwg, dw_conv]
    assert sum(width for _, _, width in pieces) == PACKED_TOTAL

    def body(*refs):
        o_ref = refs[-1]
        ref_of = {id(src): ref for src, ref in zip(sources, refs)}
        off = 0
        for src, row, width in pieces:
            o_ref[:, off:off + width] = ref_of[id(src)][row:row + 1, :]
            off += width

    return pl.pallas_call(body, name="pack_grads", out_shape=jax.ShapeDtypeStruct((1, PACKED_TOTAL), F32))(*sources)


def _small_adamw(gathered, gathered_rel, gathered_loss, weights, mom_m, mom_v):
    vec_names = [name for name, _ in SMALL]
    states = []
    for name in vec_names + ["rel_bias"]:
        states += [weights[name], mom_m[name], mom_v[name]]
    states = [a.reshape(a.shape[1:]) if a.ndim == 3 else a for a in states]
    n_state = len(states)

    def body(*refs):
        g_ref, rel_ref, loss_ref = refs[0], refs[1], refs[2]
        state_refs, out_refs = refs[3:3 + n_state], refs[3 + n_state:]
        total = g_ref[0:1, :]
        rel = rel_ref[0]
        loss = loss_ref[0]
        for d in range(1, N_DEV):
            total = total + g_ref[d:d + 1, :]
            rel = rel + rel_ref[d]
            loss = loss + loss_ref[d]
        off = 0
        for n, (name, width) in enumerate(SMALL):
            grad = total[:, off:off + width]
            w_ref, m_ref, v_ref = state_refs[3 * n:3 * n + 3]
            for ref, val in zip(out_refs[4 * n:4 * n + 4], (grad,) + _adam_math(w_ref[...], grad, m_ref[...], v_ref[...])):
                ref[...] = val
            off += width
        n = len(SMALL)
        w_ref, m_ref, v_ref = state_refs[3 * n:3 * n + 3]
        for ref, val in zip(out_refs[4 * n:4 * n + 4], (rel,) + _adam_math(w_ref[...], rel, m_ref[...], v_ref[...])):
            ref[...] = val
        dwc_ref, dwf_ref, loss_out = out_refs[4 * n + 4:]
        loss_out[...] = 0.5 * loss
        dwc_ref[...] = jnp.zeros_like(dwc_ref)
        dwf_ref[...] = jnp.zeros_like(dwf_ref)
        for j in range(CONV_K):
            dwc_ref[j:j + 1, :] = total[:, off:off + D_CONV]
            off += D_CONV
        for tap in range(3):
            dwf_ref[tap:tap + 1, :] = total[:, off:off + 2 * D_FF]
            off += 2 * D_FF

    out_shape = []
    for k in range(n_state // 3):
        out_shape += [jax.ShapeDtypeStruct(states[3 * k].shape, F32)] * 4
    out_shape += [jax.ShapeDtypeStruct((CONV_HALO, D_CONV), F32), jax.ShapeDtypeStruct((8, 2 * D_FF), F32),
                  jax.ShapeDtypeStruct((1, 128), F32)]
    res = pl.pallas_call(
        body, name="small_adamw", out_shape=out_shape,
        compiler_params=pltpu.CompilerParams(vmem_limit_bytes=VMEM_LIMIT_BYTES),
    )(gathered, gathered_rel, gathered_loss, *states)
    updates = {name: tuple(res[4 * n:4 * n + 4]) for n, name in enumerate(vec_names + ["rel_bias"])}
    return updates, res[-3], res[-2], res[-1]


def _ada_mod(c, w_shard):
    cols = w_shard.shape[1]

    def body(c_ref, w_ref, call_ref, mod_ref, send_sems, recv_sems):
        x, y, cc = _place()
        me = 4 * x + 2 * y + cc

        def exchange(ref, phase):
            sends, arrivals = [], []
            for mask in range(1, N_DEV):
                px = 1 - x if mask & 4 else x
                py = 1 - y if mask & 2 else y
                pc = 1 - cc if mask & 1 else cc
                both = dict(send_sem=send_sems.at[7 * phase + mask - 1], recv_sem=recv_sems.at[7 * phase + mask - 1],
                            device_id=(px, py, pc), device_id_type=MESH)
                sends.append(pltpu.make_async_remote_copy(src_ref=ref.at[me], dst_ref=ref.at[me], **both))
                arrivals.append(pltpu.make_async_remote_copy(src_ref=ref.at[me], dst_ref=ref.at[4 * px + 2 * py + pc],
                                                             **both))
            for cp in sends:
                cp.start()
            for cp in arrivals:
                cp.wait_recv()
            for cp in sends:
                cp.wait_send()

        v = c_ref[...]
        call_ref[me] = v * _sig(v)
        exchange(call_ref, 0)
        c_all = jnp.concatenate([call_ref[d] for d in range(N_DEV)], axis=0)
        mod_ref[me] = jnp.dot(c_all, w_ref[...], precision=HIGHEST, preferred_element_type=F32)
        exchange(mod_ref, 1)

    return pl.pallas_call(
        body, name="ada_mod",
        out_shape=[jax.ShapeDtypeStruct((N_DEV, 1, D_MODEL), F32), jax.ShapeDtypeStruct((N_DEV, N_DEV, cols), F32)],
        scratch_shapes=[pltpu.SemaphoreType.DMA((14,)), pltpu.SemaphoreType.DMA((14,))],
        compiler_params=pltpu.CompilerParams(vmem_limit_bytes=VMEM_LIMIT_BYTES),
    )(c, w_shard)


def _ada_grad(c_all, dmod_shard):
    def body(c_ref, d_ref, o_ref):
        o_ref[...] = lax.dot_general(c_ref[...], d_ref[...], _DIMS["tn"], precision=HIGHEST,
                                     preferred_element_type=F32)

    return pl.pallas_call(
        body, name="ada_grad", out_shape=jax.ShapeDtypeStruct((D_MODEL, dmod_shard.shape[1]), F32),
        compiler_params=pltpu.CompilerParams(vmem_limit_bytes=VMEM_LIMIT_BYTES),
    )(c_all, dmod_shard)


ROWS = 256


def _rms(v):
    r = lax.rsqrt(jnp.mean(v * v, axis=-1, keepdims=True) + EPS)
    return v * r, r


def _rms_bwd(dxn, xn, r):
    return r * (dxn - xn * jnp.mean(dxn * xn, axis=-1, keepdims=True))


def _colsum(v):
    return jnp.sum(v, axis=0, keepdims=True)


def _pre_mix(x, mod6, g1, comm=None):
    seq = x.shape[0]

    def body(ins, outs, scratch):
        x_ref, mod_ref, g_ref = ins
        xn, _ = _rms(x_ref[...])
        y = xn * g_ref[...]
        outs[0][...] = (y * (1.0 + mod_ref[SC_M:SC_M + 1, :]) + mod_ref[SH_M:SH_M + 1, :]).astype(BF16)

    (h,), extra = _host_call(
        body, "pre_mix", grid=(seq // ROWS,),
        in_specs=[_tile(ROWS, D_MODEL), _full((6, D_MODEL)), _full((1, D_MODEL))], out_specs=[_tile(ROWS, D_MODEL)],
        out_shape=[jax.ShapeDtypeStruct((seq, D_MODEL), BF16)], scratch_shapes=[], args=[x, mod6, g1], comm=comm)
    return h, extra


def _post_mix_pre_ffn(x, mod6, g2, g3, rows):
    seq = x.shape[0]

    def fn(y, first, ins, outs):
        x_ref, mod_ref, g2_ref, g3_ref = ins
        x1_ref, h_ref = outs
        yn, _ = _rms(y)
        x1 = x_ref[...] + mod_ref[GT_M:GT_M + 1, :] * (yn * g2_ref[...])
        x1_ref[...] = x1
        xn, _ = _rms(x1)
        y3 = xn * g3_ref[...]
        h_ref[...] = (y3 * (1.0 + mod_ref[SC_F:SC_F + 1, :]) + mod_ref[SH_F:SH_F + 1, :]).astype(BF16)

    return _Epilogue(
        [x, mod6, g2, g3], [_row_tile(rows, D_MODEL), _whole((6, D_MODEL)), _whole((1, D_MODEL)), _whole((1, D_MODEL))],
        [jax.ShapeDtypeStruct((seq, D_MODEL), F32), jax.ShapeDtypeStruct((seq, D_MODEL), BF16)],
        [_row_tile(rows, D_MODEL), _row_tile(rows, D_MODEL)], fn, keep_product=True)


def _final(x1, target, mod6, g4, rows):
    seq = x1.shape[0]

    def fn(y, first, ins, outs):
        x1_ref, t_ref, mod_ref, g_ref = ins
        loss_ref, dout_ref, dyf_ref, small_ref = outs

        @pl.when(first)
        def _():
            loss_ref[...] = jnp.zeros_like(loss_ref)
            small_ref[...] = jnp.zeros_like(small_ref)

        gt = mod_ref[GT_F:GT_F + 1, :]
        g4v = g_ref[...]
        yn, r = _rms(y)
        out = x1_ref[...] + gt * (yn * g4v)
        err = out - t_ref[...]
        loss_ref[...] += jnp.sum(jnp.mean(err * err, axis=-1, keepdims=True))
        dout = err * (1.0 / D_MODEL)
        dout_ref[...] = dout
        small_ref[0:1, :] += _colsum(dout * gt * yn)
        small_ref[1:2, :] += _colsum(dout * (yn * g4v))
        dyf_ref[...] = _rms_bwd(dout * gt * g4v, yn, r).astype(BF16)

    return _Epilogue(
        [x1, target, mod6, g4],
        [_row_tile(rows, D_MODEL), _row_tile(rows, D_MODEL), _whole((6, D_MODEL)), _whole((1, D_MODEL))],
        [jax.ShapeDtypeStruct((1, 128), F32), jax.ShapeDtypeStruct((seq, D_MODEL), F32),
         jax.ShapeDtypeStruct((seq, D_MODEL), BF16), jax.ShapeDtypeStruct((8, D_MODEL), F32)],
        [_whole((1, 128)), _row_tile(rows, D_MODEL), _row_tile(rows, D_MODEL), _whole((8, D_MODEL))],
        fn, keep_product=False)


def _mid_bwd(x1, dout, ymix, mod6, g3, g2, rows):
    seq = x1.shape[0]

    def fn(dh, first, ins, outs):
        x1_ref, dout_ref, y_ref, mod_ref, g3_ref, g2_ref = ins
        dx1_ref, dy_ref, small_ref = outs

        @pl.when(first)
        def _():
            small_ref[...] = jnp.zeros_like(small_ref)

        g3v, g2v = g3_ref[...], g2_ref[...]
        xn, r3 = _rms(x1_ref[...])
        y3 = xn * g3v
        dy3 = dh * (1.0 + mod_ref[SC_F:SC_F + 1, :])
        small_ref[0:1, :] += _colsum(dy3 * xn)
        small_ref[1:2, :] += _colsum(dh * y3)
        small_ref[2:3, :] += _colsum(dh)
        dx1 = dout_ref[...] + _rms_bwd(dy3 * g3v, xn, r3)
        dx1_ref[...] = dx1
        gt = mod_ref[GT_M:GT_M + 1, :]
        yn, r2 = _rms(y_ref[...])
        small_ref[3:4, :] += _colsum(dx1 * gt * yn)
        small_ref[4:5, :] += _colsum(dx1 * (yn * g2v))
        dy_ref[...] = _rms_bwd(dx1 * gt * g2v, yn, r2).astype(BF16)

    return _Epilogue(
        [x1, dout, ymix, mod6, g3, g2],
        [_row_tile(rows, D_MODEL)] * 3 + [_whole((6, D_MODEL)), _whole((1, D_MODEL)), _whole((1, D_MODEL))],
        [jax.ShapeDtypeStruct((seq, D_MODEL), F32), jax.ShapeDtypeStruct((seq, D_MODEL), BF16),
         jax.ShapeDtypeStruct((8, D_MODEL), F32)],
        [_row_tile(rows, D_MODEL), _row_tile(rows, D_MODEL), _whole((8, D_MODEL))], fn, keep_product=False)


def _pre_mix_bwd(x, dx1, mod6, g1, rows):
    seq = x.shape[0]

    def fn(dh, first, ins, outs):
        x_ref, dx1_ref, mod_ref, g_ref = ins
        dx_ref, small_ref = outs

        @pl.when(first)
        def _():
            small_ref[...] = jnp.zeros_like(small_ref)

        g1v = g_ref[...]
        xn, r = _rms(x_ref[...])
        dy = dh * (1.0 + mod_ref[SC_M:SC_M + 1, :])
        small_ref[0:1, :] += _colsum(dy * xn)
        small_ref[1:2, :] += _colsum(dh * (xn * g1v))
        small_ref[2:3, :] += _colsum(dh)
        dx_ref[...] = dx1_ref[...] + _rms_bwd(dy * g1v, xn, r)

    return _Epilogue(
        [x, dx1, mod6, g1],
        [_row_tile(rows, D_MODEL), _row_tile(rows, D_MODEL), _whole((6, D_MODEL)), _whole((1, D_MODEL))],
        [jax.ShapeDtypeStruct((seq, D_MODEL), F32), jax.ShapeDtypeStruct((8, D_MODEL), F32)],
        [_row_tile(rows, D_MODEL), _whole((8, D_MODEL))], fn, keep_product=False)


def _toeplitz_onehot(shape, offset_axis, top):
    m = lax.broadcasted_iota(jnp.int32, shape, offset_axis)
    i = lax.broadcasted_iota(jnp.int32, shape, 1 - offset_axis)
    return (i == jnp.clip(top - m, -MAX_REL, MAX_REL) + MAX_REL).astype(F32)


def _bias_table(rel_bias):
    width = GROUP_Q + GROUP_K

    def body(rb_ref, o_ref, t_ref):
        t_ref[...] = jnp.dot(rb_ref[...], _toeplitz_onehot((N_REL, width), 1, GROUP_K - 1), precision=HIGHEST,
                             preferred_element_type=F32)
        lane = lax.broadcasted_iota(jnp.int32, (N_HEADS, GROUP_K), 1)
        for r in range(GROUP_Q):
            first_key = (r // CHUNK) * CHUNK
            band = jnp.logical_and(lane >= first_key, lane < first_key + BAND)
            o_ref[r] = jnp.where(band, t_ref[:, GROUP_Q - 1 - r:GROUP_Q - 1 - r + GROUP_K], NEG_INF)

    return pl.pallas_call(
        body, name="bias_table", out_shape=jax.ShapeDtypeStruct((GROUP_Q, N_HEADS, GROUP_K), F32),
        scratch_shapes=[pltpu.VMEM((N_HEADS, width), F32)],
    )(rel_bias)


def _bias_grad(dbias_q):
    def body(d_ref, o_ref, t_ref):
        t_ref[...] = jnp.zeros_like(t_ref)
        for qi in range(CHUNK):
            t_ref[:, CHUNK - 1 - qi:CHUNK - 1 - qi + BAND] += d_ref[qi]
        o_ref[...] = jnp.dot(t_ref[...], _toeplitz_onehot((TOEPLITZ, N_REL), 0, BAND - 1), precision=HIGHEST,
                             preferred_element_type=F32)

    return pl.pallas_call(
        body, name="bias_grad", out_shape=jax.ShapeDtypeStruct((N_HEADS, N_REL), F32),
        scratch_shapes=[pltpu.VMEM((N_HEADS, TOEPLITZ), F32)],
    )(dbias_q)


def _resident_copies(qkv_hbm, t_hbm, k_ref, v_ref, t_ref, sems):
    inside = pl.ds(PAD_ROWS, qkv_hbm.shape[0])
    return (pltpu.make_async_copy(qkv_hbm.at[:, pl.ds(D_ATTN, D_ATTN)], k_ref.at[inside, :], sems.at[0]),
            pltpu.make_async_copy(qkv_hbm.at[:, pl.ds(2 * D_ATTN, D_ATTN)], v_ref.at[inside, :], sems.at[1]),
            pltpu.make_async_copy(t_hbm, t_ref, sems.at[2]))


def _start_resident(copies, k_ref, v_ref):
    k_ref[0:PAD_ROWS, :] = jnp.zeros((PAD_ROWS, D_ATTN), BF16)
    v_ref[0:PAD_ROWS, :] = jnp.zeros((PAD_ROWS, D_ATTN), BF16)
    for cp in copies:
        cp.start()


def _softmax_rows(s_ref, t_ref, h, before_start, rows):
    s = s_ref[rows, :] * (HEAD_DIM ** -0.5) + t_ref[h, rows, :] + before_start
    e = jnp.exp(s - jnp.max(s, axis=-1, keepdims=True))
    return e / jnp.sum(e, axis=-1, keepdims=True)


def _before_start(g):
    kj = lax.broadcasted_iota(jnp.int32, (8, GROUP_K), 1)
    return jnp.where(kj >= PAD_ROWS - g * GROUP_Q, 0.0, NEG_INF)


def _attn_fwd(qkv, table, comm=None):
    seq = qkv.shape[0]

    def body(ins, outs, scratch):
        q_ref, qkv_hbm, t_hbm = ins
        (o_ref,) = outs
        k_ref, v_ref, t_ref, s_ref, p_ref, sems = scratch
        g = pl.program_id(0)
        load_k, load_v, load_t = _resident_copies(qkv_hbm, t_hbm, k_ref, v_ref, t_ref, sems)

        @pl.when(g == 0)
        def _():
            _start_resident((load_k, load_v, load_t), k_ref, v_ref)
            load_k.wait()

        window = pl.ds(pl.multiple_of(g * GROUP_Q, GROUP_Q), GROUP_K)
        before_start = _before_start(g)
        for h in range(N_HEADS):
            cols = slice(h * HEAD_DIM, (h + 1) * HEAD_DIM)
            buf = h % 2
            s_ref[buf] = lax.dot_general(q_ref[:, cols], k_ref[window, cols], _DIMS["nt"],
                                         preferred_element_type=F32)
            if h == 0:
                pl.when(g == 0)(load_t.wait)
            for row in range(0, GROUP_Q, SOFTMAX_ROWS):
                halves = [_softmax_rows(s_ref.at[buf], t_ref, h, before_start, slice(r, r + 8))
                          for r in (row, row + 8)]
                p_ref[buf, row:row + SOFTMAX_ROWS, :] = jnp.concatenate(halves, axis=0).astype(BF16)
            if h == 0:
                pl.when(g == 0)(load_v.wait)
            o_ref[:, cols] = jnp.dot(p_ref[buf], v_ref[window, cols], preferred_element_type=F32).astype(BF16)

    (ao,), extra = _host_call(
        body, "attn_fwd", grid=(seq // GROUP_Q,),
        in_specs=[_tile(GROUP_Q, D_ATTN), ANY, ANY], out_specs=[_tile(GROUP_Q, D_ATTN)],
        out_shape=[jax.ShapeDtypeStruct((seq, D_ATTN), BF16)],
        scratch_shapes=[pltpu.VMEM((seq + PAD_ROWS, D_ATTN), BF16), pltpu.VMEM((seq + PAD_ROWS, D_ATTN), BF16),
                        pltpu.VMEM(table.shape, F32),
                        pltpu.VMEM((2, GROUP_Q, GROUP_K), F32), pltpu.VMEM((2, GROUP_Q, GROUP_K), BF16),
                        pltpu.SemaphoreType.DMA((3,))],
        args=[qkv, qkv, table], comm=comm)
    return ao, extra


def _attn_bwd(qkv, table, dao, comm=None):
    seq = qkv.shape[0]
    n_groups = seq // GROUP_Q
    fold_w = GROUP_K + (GROUP - 1) * CHUNK

    def body(ins, outs, scratch):
        q_ref, do_ref, qkv_hbm, t_hbm = ins
        dq_ref, dkt_hbm, dvt_hbm, db_ref, cs_ref = outs
        k_ref, v_ref, t_ref, db_acc, dkt_acc, dvt_acc, s_ref, dp_ref, p_ref, ds_ref, sems = scratch
        g = pl.program_id(0)

        load_k, load_v, load_t = _resident_copies(qkv_hbm, t_hbm, k_ref, v_ref, t_ref, sems)

        @pl.when(g == 0)
        def _():
            _start_resident((load_k, load_v, load_t), k_ref, v_ref)
            db_acc[...] = jnp.zeros_like(db_acc)
            dkt_acc[...] = jnp.zeros_like(dkt_acc)
            dvt_acc[...] = jnp.zeros_like(dvt_acc)
            cs_ref[...] = jnp.zeros_like(cs_ref)
            load_k.wait()
            load_v.wait()

        window = pl.ds(pl.multiple_of(g * GROUP_Q, GROUP_Q), GROUP_K)
        before_start = _before_start(g)
        for h in range(N_HEADS):
            cols = slice(h * HEAD_DIM, (h + 1) * HEAD_DIM)
            buf = h % 2
            qh, doh = q_ref[:, cols], do_ref[:, cols]
            kh, vh = k_ref[window, cols], v_ref[window, cols]
            s_ref[buf] = lax.dot_general(qh, kh, _DIMS["nt"], preferred_element_type=F32)
            dp_ref[buf] = lax.dot_general(doh, vh, _DIMS["nt"], preferred_element_type=F32)
            if h == 0:
                pl.when(g == 0)(load_t.wait)
            for row in range(0, GROUP_Q, SOFTMAX_ROWS):
                p_halves, ds_halves = [], []
                for r in (row, row + 8):
                    p = _softmax_rows(s_ref.at[buf], t_ref, h, before_start, slice(r, r + 8))
                    dp = dp_ref[buf, r:r + 8, :]
                    ds = p * (dp - jnp.sum(dp * p, axis=-1, keepdims=True))
                    chunk = r // CHUNK
                    shift = (GROUP - 1 - chunk) * CHUNK
                    db_acc[h, r - chunk * CHUNK:r - chunk * CHUNK + 8, shift:shift + GROUP_K] += ds
                    p_halves.append(p)
                    ds_halves.append(ds * (HEAD_DIM ** -0.5))
                p_ref[buf, row:row + SOFTMAX_ROWS, :] = jnp.concatenate(p_halves, axis=0).astype(BF16)
                ds_ref[buf, row:row + SOFTMAX_ROWS, :] = jnp.concatenate(ds_halves, axis=0).astype(BF16)
            dq_ref[:, cols] = jnp.dot(ds_ref[buf], kh, preferred_element_type=F32).astype(BF16)
            dkt_acc[cols, window] += lax.dot_general(qh, ds_ref[buf], _DIMS["tn"], preferred_element_type=F32)
            dvt_acc[cols, window] += lax.dot_general(doh, p_ref[buf], _DIMS["tn"], preferred_element_type=F32)
        cs_ref[0:1, :] += _colsum(dq_ref[...].astype(F32))

        @pl.when(g == n_groups - 1)
        def _():
            lo = (GROUP - 1) * CHUNK
            for h in range(N_HEADS):
                db_ref[h] = db_acc[h, :, lo:lo + BAND]
            inside = pl.ds(PAD_ROWS, seq)
            on_diagonal = (lax.broadcasted_iota(jnp.int32, (D_ATTN, D_ATTN), 0)
                           == lax.broadcasted_iota(jnp.int32, (D_ATTN, D_ATTN), 1))
            for row, acc in ((1, dkt_acc), (2, dvt_acc)):
                column = jnp.sum(acc[:, inside], axis=1, keepdims=True)
                cs_ref[row:row + 1, :] = _colsum(jnp.where(on_diagonal, column, 0.0))
            out_k = pltpu.make_async_copy(dkt_acc.at[:, inside], dkt_hbm, sems.at[0])
            out_v = pltpu.make_async_copy(dvt_acc.at[:, inside], dvt_hbm, sems.at[1])
            out_k.start()
            out_v.start()
            out_k.wait()
            out_v.wait()

    t_shape = (D_ATTN, seq + PAD_ROWS)
    outs, extra = _host_call(
        body, "attn_bwd", grid=(n_groups,),
        in_specs=[_tile(GROUP_Q, D_ATTN), _tile(GROUP_Q, D_ATTN), ANY, ANY],
        out_specs=[_tile(GROUP_Q, D_ATTN), ANY, ANY, _full((N_HEADS, CHUNK, BAND)), _full((8, D_ATTN))],
        out_shape=[jax.ShapeDtypeStruct((seq, D_ATTN), BF16), jax.ShapeDtypeStruct((D_ATTN, seq), F32),
                   jax.ShapeDtypeStruct((D_ATTN, seq), F32), jax.ShapeDtypeStruct((N_HEADS, CHUNK, BAND), F32),
                   jax.ShapeDtypeStruct((8, D_ATTN), F32)],
        scratch_shapes=[pltpu.VMEM((seq + PAD_ROWS, D_ATTN), BF16), pltpu.VMEM((seq + PAD_ROWS, D_ATTN), BF16),
                        pltpu.VMEM(table.shape, F32), pltpu.VMEM((N_HEADS, CHUNK, fold_w), F32), pltpu.VMEM(t_shape, F32),
                        pltpu.VMEM(t_shape, F32), pltpu.VMEM((2, GROUP_Q, GROUP_K), F32),
                        pltpu.VMEM((2, GROUP_Q, GROUP_K), F32), pltpu.VMEM((2, GROUP_Q, GROUP_K), BF16),
                        pltpu.VMEM((2, GROUP_Q, GROUP_K), BF16), pltpu.SemaphoreType.DMA((3,))],
        args=[qkv, dao, qkv, table], comm=comm)
    return outs, extra


def _dk_dv(dkt, dvt, seq):
    rows = 512
    transposed = pl.BlockSpec((D_ATTN, rows), lambda i: (0, i))

    def body(dkt_ref, dvt_ref, dk_ref, dv_ref):
        dk_ref[...] = dkt_ref[...].T.astype(BF16)
        dv_ref[...] = dvt_ref[...].T.astype(BF16)

    return pl.pallas_call(
        body, name="dk_dv", out_shape=[jax.ShapeDtypeStruct((seq, D_ATTN), BF16)] * 2, grid=(seq // rows,),
        in_specs=[transposed, transposed], out_specs=[_tile(rows, D_ATTN)] * 2, compiler_params=_cparams(1),
    )(dkt, dvt)


CONV_ROWS = 256


def _ln_silu(u1, g, b):
    mu = jnp.mean(u1, axis=-1, keepdims=True)
    xc = u1 - mu
    rs = lax.rsqrt(jnp.mean(xc * xc, axis=-1, keepdims=True) + EPS)
    xhat = xc * rs
    u2 = xhat * g + b
    return xhat, rs, u2


def _glu_into(s_ref, a_ref, b_ref, ah_ref, bh_ref, first):
    halo = ah_ref[...] * _sig(bh_ref[...])
    s_ref[0:CONV_HALO, :] = jnp.where(first, 0.0, halo)
    s_ref[CONV_HALO:CONV_HALO + CONV_ROWS, :] = a_ref[...] * _sig(b_ref[...])


CONV_LANES = 128
CONV_TILES = CONV_ROWS // 8


def _lag_weights(w_ref, lanes):
    return {e: jnp.broadcast_to(w_ref[CONV_K - 1 - e:CONV_K - e, lanes], (8, CONV_LANES)) for e in range(CONV_K)}


def _class_sums(w, tiles, k):
    total = None
    for a, tile in enumerate(tiles):
        if 8 * a + k < CONV_K:
            term = w[8 * a + k] * tile
            total = term if total is None else total + term
    return total


def _conv_back(src_ref, first_tile, w, lanes, row_id, emit):
    before = None
    for m in range(-1, CONV_TILES):
        tiles = [src_ref[8 * (first_tile + m - a):8 * (first_tile + m - a) + 8, lanes] for a in range(4)]
        rolled = [None] + [pltpu.roll(_class_sums(w, tiles, k), k, 0) for k in range(1, 8)]
        if m >= 0:
            out = _class_sums(w, tiles, 0)
            for k in range(1, 8):
                out = out + jnp.where(row_id < k, before[k], rolled[k])
            emit(m, out)
        before = rolled


def _conv_ahead(src_ref, w, lanes, row_id, emit):
    before = None
    for m in range(CONV_TILES + 1):
        tiles = [src_ref[8 * (m + a):8 * (m + a) + 8, lanes] for a in range(4)]
        rolled = [None] + [pltpu.roll(_class_sums(w, tiles, k), 8 - k, 0) for k in range(1, 8)]
        if m >= 1:
            out = before[0]
            for k in range(1, 8):
                out = out + jnp.where(row_id < 8 - k, before[k], rolled[k])
            emit(m - 1, out)
        before = [_class_sums(w, tiles, 0) if m < CONV_TILES else None] + rolled[1:]


def _conv_weight_sums(d_ref, s_ref, lanes, row_id, whole_shifts):
    zero = jnp.zeros((8, CONV_LANES), F32)
    sums = {8 * a + k: zero for a in whole_shifts for k in range(8) if 8 * a + k < CONV_K}

    def d_tile(m):
        return d_ref[8 * m:8 * m + 8, lanes] if 0 <= m < CONV_TILES else zero

    rolled = [None] + [zero] * 7
    for m in range(-1, CONV_TILES):
        cur, nxt = d_tile(m), d_tile(m + 1)
        rolled_next = [None] + [pltpu.roll(nxt, 8 - k, 0) for k in range(1, 8)]
        shifted = [cur] + [jnp.where(row_id < 8 - k, rolled[k], rolled_next[k]) for k in range(1, 8)]
        for a in whole_shifts:
            tile = s_ref[8 * (CONV_HALO // 8 + m - a):8 * (CONV_HALO // 8 + m - a) + 8, lanes]
            for k in range(8):
                if 8 * a + k < CONV_K and not (m < 0 and k == 0):
                    sums[8 * a + k] = sums[8 * a + k] + shifted[k] * tile
        rolled = rolled_next
    return sums


def _conv_fwd(zr, w_dw, b_dw, g_ln, b_ln, comm=None):
    seq = zr.shape[0]

    def body(a_ref, b_ref, ah_ref, bh_ref, w_ref, bias_ref, g_ref, bl_ref, u1_ref, u3_ref, s_ref):
        _glu_into(s_ref, a_ref, b_ref, ah_ref, bh_ref, pl.program_id(0) == 0)
        row_id = lax.broadcasted_iota(jnp.int32, (8, CONV_LANES), 0)
        for lo in range(0, D_CONV, CONV_LANES):
            lanes = slice(lo, lo + CONV_LANES)
            bias = jnp.broadcast_to(bias_ref[:, lanes], (8, CONV_LANES))

            def emit(m, out, lanes=lanes, bias=bias):
                u1_ref[8 * m:8 * m + 8, lanes] = out + bias

            _conv_back(s_ref, CONV_HALO // 8, _lag_weights(w_ref, lanes), lanes, row_id, emit)
        _, _, u2 = _ln_silu(u1_ref[...], g_ref[...], bl_ref[...])
        u3_ref[...] = (u2 * _sig(u2)).astype(BF16)

    return _host_call(
        lambda ins, outs, scratch: body(*ins, *outs, *scratch), "conv_fwd", grid=(seq // CONV_ROWS,),
        in_specs=[_tile(CONV_ROWS, D_CONV, 0), _tile(CONV_ROWS, D_CONV, 1),
                  _prev(CONV_HALO, D_CONV, CONV_ROWS, 0), _prev(CONV_HALO, D_CONV, CONV_ROWS, 1),
                  _full((CONV_K, D_CONV)), _full((1, D_CONV)), _full((1, D_CONV)), _full((1, D_CONV))],
        out_specs=[_tile(CONV_ROWS, D_CONV), _tile(CONV_ROWS, D_CONV)],
        out_shape=[jax.ShapeDtypeStruct((seq, D_CONV), F32), jax.ShapeDtypeStruct((seq, D_CONV), BF16)],
        scratch_shapes=[pltpu.VMEM((CONV_HALO + CONV_ROWS, D_CONV), F32)],
        args=[zr, zr, zr, zr, w_dw, b_dw, g_ln, b_ln], comm=comm)


def _conv_bwd(zr, u1, du3, w_dw, g_ln, b_ln, comm=None):
    seq = zr.shape[0]
    n_tiles = seq // CONV_ROWS
    n_halo = seq // CONV_HALO
    ext = CONV_ROWS + CONV_HALO

    def body(a_ref, b_ref, ah_ref, bh_ref, u1_ref, u1n_ref, d3_ref, d3n_ref, w_ref, g_ref, bl_ref,
             da_ref, db_ref, dw_ref, small_ref, s_ref, d_ref, du0_ref):
        i = pl.program_id(0)

        @pl.when(i == 0)
        def _():
            dw_ref[...] = jnp.zeros_like(dw_ref)
            small_ref[...] = jnp.zeros_like(small_ref)

        _glu_into(s_ref, a_ref, b_ref, ah_ref, bh_ref, i == 0)
        gv, bv = g_ref[...], bl_ref[...]

        def du1_of(u1, d3):
            xhat, rs, u2 = _ln_silu(u1, gv, bv)
            sg = _sig(u2)
            du2 = d3 * (sg * (1.0 + u2 * (1.0 - sg)))
            dxh = du2 * gv
            du1 = rs * (dxh - jnp.mean(dxh, axis=-1, keepdims=True)
                        - xhat * jnp.mean(dxh * xhat, axis=-1, keepdims=True))
            return du1, du2, xhat

        du1, du2, xhat = du1_of(u1_ref[...], d3_ref[...])
        du1n, _, _ = du1_of(u1n_ref[...], d3n_ref[...])
        d_ref[0:CONV_ROWS, :] = du1
        d_ref[CONV_ROWS:ext, :] = jnp.where(i == n_tiles - 1, 0.0, du1n)
        small_ref[0:1, :] += _colsum(du1)
        small_ref[1:2, :] += _colsum(du2 * xhat)
        small_ref[2:3, :] += _colsum(du2)
        row_id = lax.broadcasted_iota(jnp.int32, (8, CONV_LANES), 0)
        for lo in range(0, D_CONV, CONV_LANES):
            lanes = slice(lo, lo + CONV_LANES)

            def emit(m, out, lanes=lanes):
                du0_ref[8 * m:8 * m + 8, lanes] = out

            _conv_ahead(d_ref, _lag_weights(w_ref, lanes), lanes, row_id, emit)
            for whole_shifts in ((0, 1), (2, 3)):
                for e, total in _conv_weight_sums(d_ref, s_ref, lanes, row_id, whole_shifts).items():
                    dw_ref[CONV_K - 1 - e:CONV_K - e, lanes] += _colsum(total)
        du0 = du0_ref[...]
        sb = _sig(b_ref[...])
        da = du0 * sb
        dbv = du0 * a_ref[...] * sb * (1.0 - sb)
        da_ref[...] = da.astype(BF16)
        db_ref[...] = dbv.astype(BF16)
        small_ref[3:4, :] += _colsum(da)
        small_ref[4:5, :] += _colsum(dbv)

    return _host_call(
        lambda ins, outs, scratch: body(*ins, *outs, *scratch), "conv_bwd", grid=(n_tiles,),
        in_specs=[_tile(CONV_ROWS, D_CONV, 0), _tile(CONV_ROWS, D_CONV, 1),
                  _prev(CONV_HALO, D_CONV, CONV_ROWS, 0), _prev(CONV_HALO, D_CONV, CONV_ROWS, 1),
                  _tile(CONV_ROWS, D_CONV), _next(CONV_HALO, D_CONV, CONV_ROWS, n_halo),
                  _tile(CONV_ROWS, D_CONV), _next(CONV_HALO, D_CONV, CONV_ROWS, n_halo),
                  _full((CONV_K, D_CONV)), _full((1, D_CONV)), _full((1, D_CONV))],
        out_specs=[_tile(CONV_ROWS, D_CONV), _tile(CONV_ROWS, D_CONV), _full((CONV_HALO, D_CONV)),
                   _full((8, D_CONV))],
        out_shape=[jax.ShapeDtypeStruct((seq, D_CONV), BF16), jax.ShapeDtypeStruct((seq, D_CONV), BF16),
                   jax.ShapeDtypeStruct((CONV_HALO, D_CONV), F32), jax.ShapeDtypeStruct((8, D_CONV), F32)],
        scratch_shapes=[pltpu.VMEM((ext, D_CONV), F32), pltpu.VMEM((ext, D_CONV), F32),
                        pltpu.VMEM((CONV_ROWS, D_CONV), F32)],
        args=[zr, zr, zr, zr, u1, u1, du3, du3, w_dw, g_ln, b_ln], comm=comm)


MERGE_ROWS = 256


def _merge_fwd(ao, u3, zr, w_ao, w_co, b_co):
    seq = ao.shape[0]

    def body(ao_ref, u3_ref, ga_ref, gb_ref, wa_ref, wc_ref, bc_ref, y_ref, a_ref, cb_ref):
        a = jnp.dot(ao_ref[...], wa_ref[...], preferred_element_type=F32)
        cb = jnp.dot(u3_ref[...], wc_ref[...], preferred_element_type=F32) + bc_ref[...]
        a_ref[...] = a
        cb_ref[...] = cb
        y_ref[...] = (_sig(ga_ref[...]) * a + _sig(gb_ref[...]) * cb).astype(BF16)

    f32_out = jax.ShapeDtypeStruct((seq, D_MODEL), F32)
    return pl.pallas_call(
        body, name="merge_fwd",
        out_shape=[jax.ShapeDtypeStruct((seq, D_MODEL), BF16), f32_out, f32_out],
        grid=(seq // MERGE_ROWS,),
        in_specs=[_tile(MERGE_ROWS, D_ATTN), _tile(MERGE_ROWS, D_CONV), _tile(MERGE_ROWS, D_MODEL, 1),
                  _tile(MERGE_ROWS, D_MODEL, 2), _full(w_ao.shape), _full(w_co.shape), _full((1, D_MODEL))],
        out_specs=[_tile(MERGE_ROWS, D_MODEL)] * 3, compiler_params=_cparams(1),
    )(ao, u3, zr, zr, w_ao, w_co, b_co)


def _merge_bwd(a, cb, zr, rows):
    seq = a.shape[0]

    def fn(dy_v, first, ins, outs):
        a_ref, cb_ref, ga_ref, gb_ref = ins
        da_ref, dcb_ref, dga_ref, dgb_ref, small_ref = outs

        @pl.when(first)
        def _():
            small_ref[...] = jnp.zeros_like(small_ref)

        sa, sb = _sig(ga_ref[...]), _sig(gb_ref[...])
        dcb = dy_v * sb
        dga = dy_v * a_ref[...] * sa * (1.0 - sa)
        dgb = dy_v * cb_ref[...] * sb * (1.0 - sb)
        da_ref[...] = (dy_v * sa).astype(BF16)
        dcb_ref[...] = dcb.astype(BF16)
        dga_ref[...] = dga.astype(BF16)
        dgb_ref[...] = dgb.astype(BF16)
        small_ref[0:1, :] += _colsum(dga)
        small_ref[1:2, :] += _colsum(dgb)
        small_ref[2:3, :] += _colsum(dcb)

    bf = jax.ShapeDtypeStruct((seq, D_MODEL), BF16)
    gate = lambda col: pl.BlockSpec((rows, D_MODEL), lambda i, j: (i, col))
    return _Epilogue(
        [a, cb, zr, zr], [_row_tile(rows, D_MODEL), _row_tile(rows, D_MODEL), gate(1), gate(2)],
        [bf, bf, bf, bf, jax.ShapeDtypeStruct((8, D_MODEL), F32)],
        [_row_tile(rows, D_MODEL)] * 4 + [_whole((8, D_MODEL))], fn, keep_product=False)


FFN_ROWS = 2048
FFN_BLOCKS = D_FF // FFN_COLS
GELU_C = math.sqrt(2.0 / math.pi)


def _gelu(v):
    t = jnp.tanh(GELU_C * (v + 0.044715 * (v * v * v)))
    return 0.5 * v * (1.0 + t), t


def _gelu_grad(v, t):
    return 0.5 * (1.0 + t) + 0.5 * v * (1.0 - t * t) * (GELU_C * (1.0 + 3.0 * 0.044715 * (v * v)))


def _sublane_rows(ref, n):
    return [jnp.broadcast_to(ref[r:r + 1, :], (8, FFN_COLS)) for r in range(n)]


def _rolls(tile, shifts):
    return tuple(pltpu.roll(tile, s, 0) for s in shifts)


def _behind(prev_rolls, cur, row_id):
    rolls = _rolls(cur, (1, 2))
    x1 = jnp.where(row_id < 1, prev_rolls[0], rolls[0])
    x2 = jnp.where(row_id < 2, prev_rolls[1], rolls[1])
    return (x2, x1, cur), rolls


def _ahead(cur_rolls, next_rolls, row_id):
    return (jnp.where(row_id < 7, cur_rolls[0], next_rolls[0]), jnp.where(row_id < 6, cur_rolls[1], next_rolls[1]))


def _conv3(taps, w, bias):
    return w[0] * taps[0] + w[1] * taps[1] + w[2] * taps[2] + bias


def _ffn_specs(rows):
    tile = lambda off: pl.BlockSpec((rows, FFN_COLS), lambda j, i: (i, j + off))
    prev = lambda off: pl.BlockSpec((FFN_HALO, FFN_COLS),
                                    lambda j, i: (jnp.maximum(i * (rows // FFN_HALO) - 1, 0), j + off))
    wgt = lambda off: pl.BlockSpec((3, FFN_COLS), lambda j, i: (0, j + off))
    vec = lambda off: pl.BlockSpec((1, FFN_COLS), lambda j, i: (0, j + off))
    return tile, prev, wgt, vec


def _ffn_act(up, w_dw, b_dw):
    seq = up.shape[0]
    tile, prev, wgt, vec = _ffn_specs(FFN_ROWS)

    def body(v_ref, g_ref, vp_ref, gp_ref, wv_ref, wg_ref, bv_ref, bg_ref, act_ref):
        first = pl.program_id(1) == 0
        row_id = lax.broadcasted_iota(jnp.int32, (8, FFN_COLS), 0)
        wv, wg = _sublane_rows(wv_ref, 3), _sublane_rows(wg_ref, 3)
        (bv,), (bg,) = _sublane_rows(bv_ref, 1), _sublane_rows(bg_ref, 1)
        rolls_v = _rolls(jnp.where(first, 0.0, vp_ref[...]), (1, 2))
        rolls_g = _rolls(jnp.where(first, 0.0, gp_ref[...]), (1, 2))
        for row in range(0, FFN_ROWS, 16):
            halves = []
            for r in (row, row + 8):
                taps_v, rolls_v = _behind(rolls_v, v_ref[r:r + 8, :], row_id)
                taps_g, rolls_g = _behind(rolls_g, g_ref[r:r + 8, :], row_id)
                halves.append(_gelu(_conv3(taps_g, wg, bg))[0] * _conv3(taps_v, wv, bv))
            act_ref[row:row + 16, :] = jnp.concatenate(halves, axis=0).astype(BF16)

    return pl.pallas_call(
        body, name="ffn_act", out_shape=jax.ShapeDtypeStruct((seq, D_FF), BF16),
        grid=(FFN_BLOCKS, seq // FFN_ROWS),
        in_specs=[tile(0), tile(FFN_BLOCKS), prev(0), prev(FFN_BLOCKS), wgt(0), wgt(FFN_BLOCKS),
                  vec(0), vec(FFN_BLOCKS)],
        out_specs=tile(0), compiler_params=_cparams(2),
    )(up, up, up, up, w_dw, w_dw, b_dw, b_dw)


def _ffn_act_bwd(up, dact, w_dw, b_dw, comm=None):
    seq = up.shape[0]
    n_tiles = seq // FFN_ROWS
    n_halo = seq // FFN_HALO
    tile, prev, wgt, vec = _ffn_specs(FFN_ROWS)
    nxt = lambda off: pl.BlockSpec(
        (FFN_HALO, FFN_COLS), lambda j, i: (jnp.minimum((i + 1) * (FFN_ROWS // FFN_HALO), n_halo - 1), j + off))
    acc = lambda off: pl.BlockSpec((8, FFN_COLS), lambda j, i: (0, j + off))

    def body(v_ref, g_ref, vp_ref, gp_ref, vn_ref, gn_ref, da_ref, dan_ref, wv_ref, wg_ref, bv_ref, bg_ref,
             dv_out, dg_out, dwv_ref, dwg_ref, dbv_ref, dbg_ref):
        i = pl.program_id(1)
        first, last = i == 0, i == n_tiles - 1

        @pl.when(first)
        def _():
            for r in (dwv_ref, dwg_ref, dbv_ref, dbg_ref):
                r[...] = jnp.zeros_like(r)

        row_id = lax.broadcasted_iota(jnp.int32, (8, FFN_COLS), 0)
        wv, wg = _sublane_rows(wv_ref, 3), _sublane_rows(wg_ref, 3)
        (bv,), (bg,) = _sublane_rows(bv_ref, 1), _sublane_rows(bg_ref, 1)
        zero = jnp.zeros((8, FFN_COLS), F32)
        sums_v, sums_g = [zero] * 4, [zero] * 4
        rolls_v = _rolls(jnp.where(first, 0.0, vp_ref[...]), (1, 2))
        rolls_g = _rolls(jnp.where(first, 0.0, gp_ref[...]), (1, 2))
        behind = None
        done_v, done_g = [], []

        def grads(v_tile, g_tile, dact, rolls_v, rolls_g):
            taps_v, rolls_v = _behind(rolls_v, v_tile, row_id)
            taps_g, rolls_g = _behind(rolls_g, g_tile, row_id)
            val, gate = _conv3(taps_v, wv, bv), _conv3(taps_g, wg, bg)
            gel, t = _gelu(gate)
            return dact * gel, dact * val * _gelu_grad(gate, t), taps_v, taps_g, rolls_v, rolls_g

        def finish(tile, nxt, row):
            for (d, d_rolls), (_, n_rolls), w, done, o_ref in ((tile[0], nxt[0], wv, done_v, dv_out),
                                                               (tile[1], nxt[1], wg, done_g, dg_out)):
                d1, d2 = _ahead(d_rolls, n_rolls, row_id)
                done.append(w[2] * d + w[1] * d1 + w[0] * d2)
                if len(done) == 2:
                    o_ref[row - 16:row, :] = jnp.concatenate(done, axis=0).astype(BF16)
                    done.clear()

        for row in range(0, FFN_ROWS, 16):
            dact16 = da_ref[row:row + 16, :].astype(F32)
            for r, dact in ((row, dact16[0:8, :]), (row + 8, dact16[8:16, :])):
                dval, dgate, taps_v, taps_g, rolls_v, rolls_g = grads(v_ref[r:r + 8, :], g_ref[r:r + 8, :], dact,
                                                                      rolls_v, rolls_g)
                sums_v = [s + dval * x for s, x in zip(sums_v, taps_v)] + [sums_v[3] + dval]
                sums_g = [s + dgate * x for s, x in zip(sums_g, taps_g)] + [sums_g[3] + dgate]
                tile = ((dval, _rolls(dval, (7, 6))), (dgate, _rolls(dgate, (7, 6))))
                if behind is not None:
                    finish(behind, tile, r)
                behind = tile
        dact_next = jnp.where(last, 0.0, dan_ref[...].astype(F32)[0:FFN_HALO, :])
        dval, dgate, *_ = grads(vn_ref[...], gn_ref[...], dact_next, rolls_v, rolls_g)
        finish(behind, ((dval, _rolls(dval, (7, 6))), (dgate, _rolls(dgate, (7, 6)))), FFN_ROWS)
        for sums, dw_ref, db_ref in ((sums_v, dwv_ref, dbv_ref), (sums_g, dwg_ref, dbg_ref)):
            for tap in range(3):
                dw_ref[tap:tap + 1, :] += _colsum(sums[tap])
            db_ref[0:1, :] += _colsum(sums[3])

    half = jax.ShapeDtypeStruct((seq, D_FF), BF16)
    acc_shape = jax.ShapeDtypeStruct((8, D_FF), F32)
    return _host_call(
        lambda ins, outs, scratch: body(*ins, *outs, *scratch), "ffn_act_bwd", grid=(FFN_BLOCKS, n_tiles),
        in_specs=[tile(0), tile(FFN_BLOCKS), prev(0), prev(FFN_BLOCKS), nxt(0), nxt(FFN_BLOCKS),
                  tile(0), pl.BlockSpec((16, FFN_COLS), lambda j, i: (
                      jnp.minimum((i + 1) * (FFN_ROWS // 16), seq // 16 - 1), j)),
                  wgt(0), wgt(FFN_BLOCKS), vec(0), vec(FFN_BLOCKS)],
        out_specs=[tile(0), tile(0), acc(0), acc(0), acc(0), acc(0)],
        out_shape=[half, half, acc_shape, acc_shape, acc_shape, acc_shape],
        scratch_shapes=[], args=[up, up, up, up, up, up, dact, dact, w_dw, w_dw, b_dw, b_dw], comm=comm)


def _cols_to_blocks(full_cols):
    k, n8 = full_cols.shape
    return jnp.transpose(full_cols.reshape(k, N_DEV, n8 // N_DEV), (1, 0, 2))


def _rows_to_blocks(full_rows):
    r8, n = full_rows.shape
    return full_rows.reshape(N_DEV, r8 // N_DEV, n)


def _blocks_to_cols(gathered):
    _, k, n = gathered.shape
    return jnp.transpose(gathered, (1, 0, 2)).reshape(k, N_DEV * n)


def kernel(x, c, w_ada, b_ada, g_pre_mix, g_post_mix, w_in, b_in, rel_bias, w_attn_o, w_dw_conv, b_dw_conv, g_conv_ln, b_conv_ln, w_conv_o, b_conv_o, w_mix_o, g_pre_ffn, g_post_ffn, w_up, w_dw_ffn, b_dw_ffn, w_down, loss_target, m_w_ada, m_b_ada, m_g_pre_mix, m_g_post_mix, m_w_in, m_b_in, m_rel_bias, m_w_attn_o, m_w_dw_conv, m_b_dw_conv, m_g_conv_ln, m_b_conv_ln, m_w_conv_o, m_b_conv_o, m_w_mix_o, m_g_pre_ffn, m_g_post_ffn, m_w_up, m_w_dw_ffn, m_b_dw_ffn, m_w_down, v_w_ada, v_b_ada, v_g_pre_mix, v_g_post_mix, v_w_in, v_b_in, v_rel_bias, v_w_attn_o, v_w_dw_conv, v_b_dw_conv, v_g_conv_ln, v_b_conv_ln, v_w_conv_o, v_b_conv_o, v_w_mix_o, v_g_pre_ffn, v_g_post_ffn, v_w_up, v_w_dw_ffn, v_b_dw_ffn, v_w_down):
    names = ["w_ada", "b_ada", "g_pre_mix", "g_post_mix", "w_in", "b_in", "rel_bias", "w_attn_o", "w_dw_conv",
             "b_dw_conv", "g_conv_ln", "b_conv_ln", "w_conv_o", "b_conv_o", "w_mix_o", "g_pre_ffn", "g_post_ffn",
             "w_up", "w_dw_ffn", "b_dw_ffn", "w_down"]
    weights = dict(zip(names, [w_ada, b_ada, g_pre_mix, g_post_mix, w_in, b_in, rel_bias, w_attn_o, w_dw_conv,
                               b_dw_conv, g_conv_ln, b_conv_ln, w_conv_o, b_conv_o, w_mix_o, g_pre_ffn,
                               g_post_ffn, w_up, w_dw_ffn, b_dw_ffn, w_down]))
    mom_m = dict(zip(names, [m_w_ada, m_b_ada, m_g_pre_mix, m_g_post_mix, m_w_in, m_b_in, m_rel_bias, m_w_attn_o,
                             m_w_dw_conv, m_b_dw_conv, m_g_conv_ln, m_b_conv_ln, m_w_conv_o, m_b_conv_o,
                             m_w_mix_o, m_g_pre_ffn, m_g_post_ffn, m_w_up, m_w_dw_ffn, m_b_dw_ffn, m_w_down]))
    mom_v = dict(zip(names, [v_w_ada, v_b_ada, v_g_pre_mix, v_g_post_mix, v_w_in, v_b_in, v_rel_bias, v_w_attn_o,
                             v_w_dw_conv, v_b_dw_conv, v_g_conv_ln, v_b_conv_ln, v_w_conv_o, v_b_conv_o,
                             v_w_mix_o, v_g_pre_ffn, v_g_post_ffn, v_w_up, v_w_dw_ffn, v_b_dw_ffn, v_w_down]))
    shapes = {n: w.shape for n, w in weights.items()}

    seq = x.shape[1]
    me = 4 * lax.axis_index("x") + 2 * lax.axis_index("y") + lax.axis_index("c")
    x2 = x.reshape(seq, D_MODEL)
    target = loss_target.reshape(seq, D_MODEL)
    sq = lambda a: a.reshape(a.shape[1:])
    bf = lambda a: sq(a).astype(BF16)

    transposed = lambda a: jnp.swapaxes(sq(a), 0, 1)

    c_all, mod_all = _ada_mod(c, sq(w_ada))
    c_all = c_all.reshape(N_DEV, D_MODEL)
    mod = lax.dynamic_index_in_dim(mod_all, me, axis=1, keepdims=False)
    mod6 = (mod.reshape(1, 6 * D_MODEL) + b_ada).reshape(6, D_MODEL)

    h1, (g_dwc, g_dwf) = _pre_mix(x2, mod6, g_pre_mix, comm=_gather_comm([sq(w_dw_conv), sq(w_dw_ffn)]))
    wf_dwc = _blocks_to_cols(g_dwc)
    wf_dwf = _blocks_to_cols(g_dwf)
    g_in, qkv, zr = _in_proj(h1, transposed(w_in).astype(BF16), b_in)
    wt_in = g_in.reshape(g_in.shape[0] * g_in.shape[1], D_MODEL)
    table = jnp.transpose(_bias_table(sq(rel_bias)), (1, 0, 2))
    ao, (g_up, g_ao, g_co, g_mo) = _attn_fwd(
        qkv, table, comm=_gather_comm([transposed(w_up).astype(BF16), bf(w_attn_o), bf(w_conv_o), bf(w_mix_o)]))
    (u1, u3), (g_dn,) = _conv_fwd(zr, wf_dwc, b_dw_conv, g_conv_ln, b_conv_ln, comm=_gather_comm([bf(w_down)]))
    wf_ao = _blocks_to_cols(g_ao)
    wf_co = _blocks_to_cols(g_co)
    wf_mo = g_mo.reshape(D_MODEL, D_MODEL)
    wt_up = g_up.reshape(g_up.shape[0] * g_up.shape[1], D_MODEL)
    wf_dn = g_dn.reshape(D_FF, D_MODEL)
    y, a_br, cb_br = _merge_fwd(ao, u3, zr, wf_ao, wf_co, b_conv_o)
    ymix, (x1, h2), _ = _mm(y, wf_mo, "nn", F32, "mix_o", tm=512, tn=D_MODEL,
                            epilogue=_post_mix_pre_ffn(x2, mod6, g_post_mix, g_pre_ffn, 512))
    up = _mm(h2, wt_up, "nt", F32, "ffn_up", tm=1024, tn=1408)
    act = _ffn_act(up, wf_dwf, b_dw_ffn)
    _, (loss_lanes, dout, dyf, small_f), _ = _mm(act, wf_dn, "nn", F32, "ffn_down", tm=512, tn=D_MODEL,
                                                 epilogue=_final(x1, target, mod6, g_post_ffn, 512))

    dact = _mm(dyf, wf_dn, "nt", BF16, "ffn_down_dx", tm=1024, tn=1408)
    gw_down = _mm(act, dyf, "tn", BF16, "ffn_down_dw", tm=256, tn=1024)
    (dup_v, dup_g, dwv, dwg, dbv, dbg), (parts_down,) = _ffn_act_bwd(
        up, dact, wf_dwf, b_dw_ffn, comm=_scatter_comm([_rows_to_blocks(gw_down)]))
    _, (dx1, dymix, small_m), _ = _mm([dup_v, dup_g], wt_up, "nn", F32, "ffn_up_dx", tm=512, tn=D_MODEL,
                                      epilogue=_mid_bwd(x1, dout, ymix, mod6, g_pre_ffn, g_post_mix, 512))
    blocks_up = _rows_to_blocks(_mm_tn_rows([dup_v, dup_g], h2, "ffn_up_dw"))
    _, (da, dcb, dga, dgb, small_g), _ = _mm(dymix, wf_mo, "nt", F32, "mix_o_dx", tm=512, tn=D_MODEL,
                                             epilogue=_merge_bwd(a_br, cb_br, zr, 512))
    gw_mo = _mm(y, dymix, "tn", BF16, "mix_o_dw")
    dao = _mm(da, wf_ao, "nt", BF16, "attn_o_dx", tm=1024)
    gw_ao = _mm(ao, da, "tn", BF16, "attn_o_dw")
    du3 = _mm(dcb, wf_co, "nt", F32, "conv_o_dx", tm=1024)
    gw_co = _mm(u3, dcb, "tn", BF16, "conv_o_dw")
    (dq, dkt, dvt, dbias, small_a), (parts_up,) = _attn_bwd(
        qkv, table, dao, comm=_scatter_comm([blocks_up]))
    g_rel = _bias_grad(jnp.transpose(dbias, (1, 0, 2)))
    (dglu_a, dglu_b, dw_conv, small_c), (parts_mo, parts_ao, parts_co) = _conv_bwd(
        zr, u1, du3, wf_dwc, g_conv_ln, b_conv_ln,
        comm=_scatter_comm([_rows_to_blocks(gw_mo), _cols_to_blocks(gw_ao), _cols_to_blocks(gw_co)]))
    dk, dv = _dk_dv(dkt, dvt, seq)
    dz = [dq, dk, dv, dglu_a, dglu_b, dga, dgb]
    dz_halves = [(p, c) for p in dz for c in range(p.shape[1] // D_ATTN)]
    blocks_in = _rows_to_blocks(_mm_tn_rows(dz, h1, "in_proj_dw"))
    _, (grad_x, small_x), (parts_in, _, _) = _mm(dz_halves, wt_in, "nn", F32, "in_proj_dx", tm=512, tn=D_MODEL,
                                                 k_piece=D_ATTN,
                                                 comm=_pair_scatter_comm(blocks_in),
                                                 epilogue=_pre_mix_bwd(x2, dx1, mod6, g_pre_mix, 512))

    packed = _pack_grads(small_x, small_m, small_f, small_g, small_a, small_c, dbv, dbg, dwv, dwg, dw_conv)
    gathered, gathered_rel, gathered_loss = _run_comm(_gather_comm([packed, g_rel, loss_lanes]), "gather_small")
    gathered = gathered.reshape(N_DEV, PACKED_TOTAL)
    updates, g_dwc_full, g_dwf_full, loss_all = _small_adamw(gathered, gathered_rel, gathered_loss, weights, mom_m,
                                                             mom_v)
    loss = loss_all[0, 0]

    grads, deltas, new_m, new_v = {}, {}, {}, {}

    def record(name, update, is_transposed=False):
        for dst, val in zip((grads, deltas, new_m, new_v), update):
            dst[name] = (jnp.swapaxes(val, 0, 1) if is_transposed else val).reshape(shapes[name])

    for name, update in updates.items():
        record(name, update)

    def local_update(name, grad, view=sq):
        record(name, _adamw(view(weights[name]), view(mom_m[name]), view(mom_v[name]), "adamw_" + name, g=view(grad)))

    def taps_major(a):
        return a.reshape(a.shape[1], 1, a.shape[2])

    conv_cols, ffn_cols, ada_cols = D_CONV // N_DEV, 2 * D_FF // N_DEV, 6 * D_MODEL // N_DEV
    local_update("w_dw_conv", lax.dynamic_slice(g_dwc_full, (0, me * conv_cols), (CONV_K, conv_cols))[None], taps_major)
    local_update("w_dw_ffn", lax.dynamic_slice(g_dwf_full, (0, me * ffn_cols), (3, ffn_cols))[None], taps_major)
    local_update("w_ada", _ada_grad(c_all, lax.dynamic_slice(gathered, (0, me * ada_cols), (N_DEV, ada_cols)))[None])

    for name, part in (("w_attn_o", parts_ao), ("w_conv_o", parts_co), ("w_mix_o", parts_mo), ("w_down", parts_down)):
        record(name, _adamw(sq(weights[name]), sq(mom_m[name]), sq(mom_v[name]), "adamw_" + name, parts=part))
    for name, part in (("w_in", parts_in), ("w_up", parts_up)):
        record(name, _adamw(transposed(weights[name]), transposed(mom_m[name]), transposed(mom_v[name]),
                            "adamw_" + name, parts=part), is_transposed=True)

    return (loss, grad_x.reshape(x.shape), *[grads[n] for n in names], *[deltas[n] for n in names],
            *[new_m[n] for n in names], *[new_v[n] for n in names])
```

```python
import functools
import math

import jax
import jax.numpy as jnp
from jax import lax
from jax.experimental import pallas as pl
from jax.experimental.pallas import tpu as pltpu

F32 = jnp.float32
BF16 = jnp.bfloat16
HIGHEST = lax.Precision.HIGHEST

D_MODEL = 1024
CHUNK = 64
LEFT_CHUNKS = 8
BAND = (LEFT_CHUNKS + 1) * CHUNK
PAD_ROWS = LEFT_CHUNKS * CHUNK
GROUP = 4
GROUP_Q = GROUP * CHUNK
GROUP_K = GROUP_Q + PAD_ROWS
SOFTMAX_ROWS = 16
TOEPLITZ = 640
N_HEADS = 8
HEAD_DIM = 64
D_ATTN = 512
D_CONV = 512
CONV_K = 31
CONV_HALO = 32
MAX_REL = 128
N_REL = 2 * MAX_REL + 1
D_FF = 2816
FFN_HALO = 8
FFN_COLS = 256
EPS = 1e-6
NEG_INF = -1e30
N_DEV = 8

ADAM_LR = 0.001
ADAM_B1 = 0.9
ADAM_B2 = 0.999
ADAM_EPS = 1e-08
ADAM_WD = 0.01
ADAM_STEP = 10

VMEM_LIMIT_BYTES = 56 * 1024 * 1024
ADAMW_BLOCK_BYTES = 768 * 1024

MESH = pl.DeviceIdType.MESH
ANY = pl.BlockSpec(memory_space=pl.ANY)

SH_M, SC_M, GT_M, SH_F, SC_F, GT_F = range(6)

SMALL = (("b_ada", 6144), ("g_pre_mix", 1024), ("g_post_mix", 1024), ("b_in", 4608), ("b_dw_conv", 512),
         ("g_conv_ln", 512), ("b_conv_ln", 512), ("b_conv_o", 1024), ("g_pre_ffn", 1024), ("g_post_ffn", 1024),
         ("b_dw_ffn", 5632))
PACKED_TOTAL = sum(n for _, n in SMALL) + CONV_K * D_CONV + 3 * 2 * D_FF


def _cparams(n_axes):
    return pltpu.CompilerParams(vmem_limit_bytes=VMEM_LIMIT_BYTES,
                                dimension_semantics=("arbitrary",) * n_axes)


def _sig(v):
    return 1.0 / (1.0 + jnp.exp(-v))


def _pick(n, target):
    if n <= target:
        return n
    t = target - target % 128
    while n % t:
        t -= 128
    return t


def _tile(rows, cols, col=0):
    return pl.BlockSpec((rows, cols), lambda i: (i, col))


def _full(shape):
    zeros = (0,) * len(shape)
    return pl.BlockSpec(shape, lambda i: zeros)


def _prev(halo, cols, rows, col=0):
    return pl.BlockSpec((halo, cols), lambda i: (jnp.maximum(i * (rows // halo) - 1, 0), col))


def _next(halo, cols, rows, n_blocks, col=0):
    return pl.BlockSpec((halo, cols), lambda i: (jnp.minimum((i + 1) * (rows // halo), n_blocks - 1), col))


class _Comm:
    def __init__(self, inputs, out_shapes, sems, start, finish, relay=None, early=None):
        self.inputs, self.out_shapes, self.sems, self.start, self.finish = inputs, out_shapes, sems, start, finish
        self.relay, self.early = relay, early


def _host_call(body, name, grid, in_specs, out_specs, out_shape, scratch_shapes, args, comm=None):
    n_in, n_out, n_scr = len(args), len(out_shape), len(scratch_shapes)
    c_in = list(comm.inputs) if comm else []
    c_out = list(comm.out_shapes) if comm else []
    c_sem = list(comm.sems) if comm else []

    def full(*refs):
        bounds = [0, n_in, len(c_in), n_out, len(c_out), n_scr, len(c_sem)]
        cuts = [sum(bounds[:i + 1]) for i in range(len(bounds))]
        ins, cins, outs, couts, scr, csems = (refs[lo:hi] for lo, hi in zip(cuts[:-1], cuts[1:]))
        if comm:
            first = functools.reduce(jnp.logical_and, [pl.program_id(ax) == 0 for ax in range(len(grid))])
            pl.when(first)(lambda: comm.start(cins, couts, csems))
            if comm.early is not None:
                strides = [math.prod(grid[ax + 1:]) for ax in range(len(grid))]
                step = sum(pl.program_id(ax) * strides[ax] for ax in range(len(grid)))
                pl.when(step == 1)(lambda: comm.early(cins, couts, csems))
            last = functools.reduce(jnp.logical_and, [pl.program_id(ax) == grid[ax] - 1 for ax in range(len(grid))])
            if comm.relay is not None:
                pl.when(last)(lambda: comm.relay(cins, couts, csems))
        body(ins, outs, scr)
        if comm:
            pl.when(last)(lambda: comm.finish(cins, couts, csems))

    res = pl.pallas_call(
        full, name=name, grid=grid, in_specs=list(in_specs) + [ANY] * len(c_in),
        out_specs=list(out_specs) + [ANY] * len(c_out), out_shape=list(out_shape) + c_out,
        scratch_shapes=list(scratch_shapes) + c_sem, compiler_params=_cparams(len(grid)),
    )(*args, *c_in)
    return list(res[:n_out]), list(res[n_out:])


def _run_comm(comm, name):
    n_in, n_out = len(comm.inputs), len(comm.out_shapes)

    def body(*refs):
        ins, outs, sems = refs[:n_in], refs[n_in:n_in + n_out], refs[n_in + n_out:]
        comm.start(ins, outs, sems)
        if comm.relay is not None:
            comm.relay(ins, outs, sems)
        comm.finish(ins, outs, sems)

    return pl.pallas_call(
        body, name=name, out_shape=list(comm.out_shapes), in_specs=[ANY] * n_in, out_specs=[ANY] * n_out,
        scratch_shapes=list(comm.sems),
    )(*comm.inputs)


def _place():
    return lax.axis_index("x"), lax.axis_index("y"), lax.axis_index("c")


def _gather_comm(arrs):
    n = len(arrs)

    def plan(ins, outs, sems):
        send_sems, recv_sems, local_sems = sems
        x, y, c = _place()
        me, sibling = (x, y, c), (x, y, 1 - c)
        chips = [(1 - x, y), (x, 1 - y), (1 - x, 1 - y)]

        def block(k, p):
            return outs[k].at[4 * p[0] + 2 * p[1] + p[2]]

        def copy(k, s, blk, to, src=None):
            return pltpu.make_async_remote_copy(
                src_ref=block(k, blk) if src is None else src, dst_ref=block(k, blk),
                send_sem=send_sems.at[7 * k + s], recv_sem=recv_sems.at[7 * k + s],
                device_id=to, device_id_type=MESH)

        mine = [pltpu.make_async_copy(ins[k], block(k, me), local_sems.at[k]) for k in range(n)]
        first = []
        for k in range(n):
            first.append(copy(k, 0, me, sibling, src=ins[k]))
            for j, chip in enumerate(chips):
                first.append(copy(k, 1 + j, me, (*chip, c), src=ins[k]))
        return me, sibling, chips, c, copy, mine, first

    def start(ins, outs, sems):
        *_, mine, first = plan(ins, outs, sems)
        for cp in mine + first:
            cp.start()

    def relay(ins, outs, sems):
        me, sibling, chips, c, copy, _, _ = plan(ins, outs, sems)
        for j, chip in enumerate(chips):
            for k in range(n):
                copy(k, 1 + j, (*chip, c), me).wait_recv()
                copy(k, 4 + j, (*chip, c), sibling).start()

    def finish(ins, outs, sems):
        me, sibling, chips, c, copy, mine, first = plan(ins, outs, sems)
        passed = [copy(k, 4 + j, (*chip, c), sibling) for j, chip in enumerate(chips) for k in range(n)]
        for k in range(n):
            copy(k, 0, sibling, me).wait_recv()
        for j, chip in enumerate(chips):
            for k in range(n):
                copy(k, 4 + j, (*chip, 1 - c), me).wait_recv()
        for cp in first + passed:
            cp.wait_send()
        for cp in mine:
            cp.wait()

    return _Comm(list(arrs), [jax.ShapeDtypeStruct((N_DEV,) + a.shape, a.dtype) for a in arrs],
                 [pltpu.SemaphoreType.DMA((7 * n,)), pltpu.SemaphoreType.DMA((7 * n,)),
                  pltpu.SemaphoreType.DMA((n,))], start, finish, relay)


def _scatter_comm(blocks):
    n = len(blocks)

    def plan(ins, outs, sems, arrivals):
        send_sems, recv_sems, local_sems = sems
        x, y, c = _place()
        me = 4 * x + 2 * y + c
        local = [pltpu.make_async_copy(ins[k].at[me], outs[k].at[me], local_sems.at[k]) for k in range(n)]
        sends, recvs = [], []
        for k in range(n):
            for mask in range(1, N_DEV):
                px = 1 - x if mask & 4 else x
                py = 1 - y if mask & 2 else y
                pc = 1 - c if mask & 1 else c
                peer = 4 * px + 2 * py + pc
                sem = 7 * k + mask - 1
                both = dict(send_sem=send_sems.at[sem], recv_sem=recv_sems.at[sem], device_id=(px, py, pc),
                            device_id_type=MESH)
                sends.append(pltpu.make_async_remote_copy(src_ref=ins[k].at[peer], dst_ref=outs[k].at[me], **both))
                if arrivals:
                    recvs.append(pltpu.make_async_remote_copy(src_ref=ins[k].at[me], dst_ref=outs[k].at[peer],
                                                              **both))
        return local, sends, recvs

    def start(ins, outs, sems):
        local, sends, _ = plan(ins, outs, sems, arrivals=False)
        for cp in local + sends:
            cp.start()

    def finish(ins, outs, sems):
        local, sends, recvs = plan(ins, outs, sems, arrivals=True)
        for cp in recvs:
            cp.wait_recv()
        for cp in sends:
            cp.wait_send()
        for cp in local:
            cp.wait()

    return _Comm(list(blocks), [jax.ShapeDtypeStruct(b.shape, b.dtype) for b in blocks],
                 [pltpu.SemaphoreType.DMA((7 * n,)), pltpu.SemaphoreType.DMA((7 * n,)),
                  pltpu.SemaphoreType.DMA((n,))], start, finish)


def _pair_scatter_comm(block):
    _, r, c = block.shape
    quarter = jax.ShapeDtypeStruct((4, r, c), block.dtype)

    def plan(ins, outs, sems):
        parts, got, pair = outs
        d2d_send, d2d_recv, ici_send, ici_recv, local, *bufs = sems
        x, y, cc = _place()
        mine = 2 * x + y
        chips = [(1 - x, y), (x, 1 - y), (1 - x, 1 - y)]
        to_sibling = [pltpu.make_async_remote_copy(
            src_ref=ins[0].at[2 * q + 1 - cc], dst_ref=got.at[q], send_sem=d2d_send.at[q], recv_sem=d2d_recv.at[q],
            device_id=(x, y, 1 - cc), device_id_type=MESH) for q in range(4)]
        to_chips = [pltpu.make_async_remote_copy(
            src_ref=pair.at[2 * px + py], dst_ref=parts.at[mine], send_sem=ici_send.at[j], recv_sem=ici_recv.at[j],
            device_id=(px, py, cc), device_id_type=MESH) for j, (px, py) in enumerate(chips)]
        from_chips = [pltpu.make_async_remote_copy(
            src_ref=pair.at[mine], dst_ref=parts.at[2 * px + py], send_sem=ici_send.at[j], recv_sem=ici_recv.at[j],
            device_id=(px, py, cc), device_id_type=MESH) for j, (px, py) in enumerate(chips)]
        own = pltpu.make_async_copy(pair.at[mine], parts.at[mine], local.at[5])
        order = [2 * px + py for px, py in chips] + [mine]
        return cc, got, pair, local, bufs, order, to_sibling, to_chips, from_chips, own

    def start(ins, outs, sems):
        for cp in plan(ins, outs, sems)[6]:
            cp.start()

    def early(ins, outs, sems):
        cc, got, pair, local, bufs, order, to_sibling, to_chips, _, own = plan(ins, outs, sems)
        for cp in to_sibling:
            cp.wait_recv()

        def loads(k):
            return [pltpu.make_async_copy(ins[0].at[2 * order[k] + cc], bufs[2 * (k % 2)], local.at[k % 2]),
                    pltpu.make_async_copy(got.at[order[k]], bufs[2 * (k % 2) + 1], local.at[2 + k % 2])]

        for cp in loads(0):
            cp.start()
        for k, send in enumerate(to_chips + [own]):
            if k + 1 < len(order):
                for cp in loads(k + 1):
                    cp.start()
            for cp in loads(k):
                cp.wait()
            kept, came = bufs[2 * (k % 2)], bufs[2 * (k % 2) + 1]
            kept[...] = (kept[...].astype(F32) + came[...].astype(F32)).astype(block.dtype)
            store = pltpu.make_async_copy(kept, pair.at[order[k]], local.at[4])
            store.start()
            store.wait()
            send.start()

    def finish(ins, outs, sems):
        *_, to_sibling, to_chips, from_chips, own = plan(ins, outs, sems)
        for cp in from_chips:
            cp.wait_recv()
        for cp in to_chips + to_sibling:
            cp.wait_send()
        own.wait()

    return _Comm([block], [quarter, quarter, quarter],
                 [pltpu.SemaphoreType.DMA((4,)), pltpu.SemaphoreType.DMA((4,)), pltpu.SemaphoreType.DMA((3,)),
                  pltpu.SemaphoreType.DMA((3,)), pltpu.SemaphoreType.DMA((6,))]
                 + [pltpu.VMEM((r, c), block.dtype)] * 4, start, finish, early=early)


_DIMS = {"nn": (((1,), (0,)), ((), ())), "nt": (((1,), (1,)), ((), ())), "tn": (((0,), (0,)), ((), ()))}


class _Epilogue:
    def __init__(self, args, in_specs, out_shapes, out_specs, fn, keep_product):
        self.args, self.in_specs, self.out_shapes, self.out_specs = args, in_specs, out_shapes, out_specs
        self.fn, self.keep_product = fn, keep_product


def _row_tile(rows, cols):
    return pl.BlockSpec((rows, cols), lambda i, j: (i, 0))


def _whole(shape):
    zeros = (0,) * len(shape)
    return pl.BlockSpec(shape, lambda i, j: zeros)


def _mm(a, b, mode, out_dtype, name, bias=None, tm=512, tn=512, comm=None, cols=None, epilogue=None, k_piece=None):
    pieces = a if isinstance(a, (list, tuple)) else [a]
    piece_cols = [0] * len(pieces)
    if k_piece is not None:
        pieces, piece_cols = [p for p, _ in a], [c for _, c in a]
    assert all(p.dtype == BF16 for p in pieces) and b.dtype == BF16
    a = pieces[0]
    if mode == "tn":
        k_dim, m_dim = a.shape
    else:
        m_dim, k_dim = a.shape[0], k_piece or a.shape[1]
    n_dim = b.shape[0] if mode == "nt" else b.shape[1]
    col0 = 0
    if cols is not None:
        assert mode != "tn" and cols[0] % tn == 0 and cols[1] % tn == 0
        col0, n_dim = cols[0] // tn, cols[1]
    tm, tn = _pick(m_dim, tm), _pick(n_dim, tn)
    assert mode != "tn" or len(pieces) == 1
    a_specs = [pl.BlockSpec((k_dim, tm), lambda i, j: (0, i)) if mode == "tn"
               else pl.BlockSpec((tm, k_dim), lambda i, j, c=c: (i, c)) for c in piece_cols]
    once = dict(pipeline_mode=pl.Buffered(1)) if tn == n_dim else {}
    if mode == "nt":
        b_specs = [pl.BlockSpec((tn, k_dim), lambda i, j, p=p: (j + col0, p), **once) for p in range(len(pieces))]
    else:
        b_specs = [pl.BlockSpec((k_dim, tn), lambda i, j, p=p: (p, j + col0), **once) for p in range(len(pieces))]
    in_specs = a_specs + b_specs
    args = list(pieces) + [b] * len(pieces)
    if bias is not None:
        in_specs.append(pl.BlockSpec((1, tn), lambda i, j: (0, j + col0)))
        args.append(bias)
    dims = _DIMS[mode]
    n_pieces = len(pieces)
    n_own = len(args)
    keep = epilogue is None or epilogue.keep_product
    out_specs = [pl.BlockSpec((tm, tn), lambda i, j: (i, j))] if keep else []
    out_shape = [jax.ShapeDtypeStruct((m_dim, n_dim), out_dtype)] if keep else []
    if epilogue is not None:
        assert tn == n_dim
        in_specs, args = in_specs + list(epilogue.in_specs), args + list(epilogue.args)
        out_specs, out_shape = out_specs + list(epilogue.out_specs), out_shape + list(epilogue.out_shapes)

    def body(ins, outs, scratch):
        total = lax.dot_general(ins[0][...], ins[n_pieces][...], dims, preferred_element_type=F32)
        for p in range(1, n_pieces):
            total = total + lax.dot_general(ins[p][...], ins[n_pieces + p][...], dims, preferred_element_type=F32)
        if bias is not None:
            total = total + ins[2 * n_pieces][...]
        if keep:
            outs[0][...] = total.astype(out_dtype)
        if epilogue is not None:
            epilogue.fn(total, pl.program_id(0) == 0, ins[n_own:], outs[1:] if keep else outs)

    outs, extra = _host_call(body, name, grid=(m_dim // tm, n_dim // tn), in_specs=in_specs, out_specs=out_specs,
                             out_shape=out_shape, scratch_shapes=[], args=args, comm=comm)
    product = outs[0] if keep else None
    if comm is None and epilogue is None:
        return product
    return product, outs[1:] if keep else outs, extra


def _mm_tn_rows(pieces, b, name, tm=256):
    k_dim, n_dim = b.shape
    counts = [p.shape[1] // tm for p in pieces]
    assert all(p.shape[1] % tm == 0 for p in pieces)
    firsts = [sum(counts[:q]) for q in range(len(pieces))]

    def a_spec(first, count):
        return pl.BlockSpec((k_dim, tm), lambda i: (0, jnp.clip(i - first, 0, count - 1)))

    def body(ins, outs, scratch):
        i = pl.program_id(0)
        for a_ref, first, count in zip(ins[:-1], firsts, counts):
            @pl.when(jnp.logical_and(i >= first, i < first + count))
            def _(a_ref=a_ref):
                outs[0][...] = lax.dot_general(a_ref[...], ins[-1][...], _DIMS["tn"],
                                               preferred_element_type=F32).astype(BF16)

    (out,), _ = _host_call(
        body, name, grid=(sum(counts),),
        in_specs=[a_spec(f, c) for f, c in zip(firsts, counts)]
        + [pl.BlockSpec((k_dim, n_dim), lambda i: (0, 0), pipeline_mode=pl.Buffered(1))],
        out_specs=[_tile(tm, n_dim)], out_shape=[jax.ShapeDtypeStruct((sum(counts) * tm, n_dim), BF16)],
        scratch_shapes=[], args=list(pieces) + [b])
    return out


def _adam_math(w, g, m, v):
    m = ADAM_B1 * m + (1.0 - ADAM_B1) * g
    v = ADAM_B2 * v + (1.0 - ADAM_B2) * (g * g)
    m_hat = m / (1.0 - ADAM_B1 ** ADAM_STEP)
    v_hat = v / (1.0 - ADAM_B2 ** ADAM_STEP)
    delta = -ADAM_LR * (m_hat / (jnp.sqrt(v_hat) + ADAM_EPS) + ADAM_WD * w)
    return delta, m, v


def _adamw(w, m, v, name, g=None, parts=None):
    rows, cols = w.shape[0], w.shape[-1]
    tr = rows
    if rows * cols * 4 > ADAMW_BLOCK_BYTES:
        tr = max(t for t in range(16, rows, 16) if rows % t == 0 and t * cols * 4 <= ADAMW_BLOCK_BYTES)

    def body(w_ref, m_ref, v_ref, g_ref, go_ref, d_ref, mo_ref, vo_ref):
        if parts is None:
            grad = g_ref[...]
        else:
            grad = g_ref[0].astype(F32)
            for d in range(1, parts.shape[0]):
                grad = grad + g_ref[d].astype(F32)
        delta, m_new, v_new = _adam_math(w_ref[...], grad, m_ref[...], v_ref[...])
        go_ref[...] = grad
        d_ref[...] = delta
        mo_ref[...] = m_new
        vo_ref[...] = v_new

    spec = _tile(tr, cols) if w.ndim == 2 else _full(w.shape)
    g_spec = spec if parts is None else pl.BlockSpec((parts.shape[0], tr, cols), lambda i: (0, i, 0))
    shape = jax.ShapeDtypeStruct(w.shape, F32)
    return pl.pallas_call(
        body, name=name, out_shape=[shape] * 4, grid=(rows // tr,),
        in_specs=[spec, spec, spec, g_spec], out_specs=[spec] * 4, compiler_params=_cparams(1),
    )(w, m, v, g if parts is None else parts)


def _pack_grads(small_x, small_m, small_f, small_g, small_a, small_c, dbv, dbg, dwv, dwg, dw_conv):
    pieces = [
        (small_x, 2, D_MODEL), (small_x, 1, D_MODEL), (small_m, 4, D_MODEL), (small_m, 2, D_MODEL),
        (small_m, 1, D_MODEL), (small_f, 1, D_MODEL),
        (small_x, 0, D_MODEL), (small_m, 3, D_MODEL),
        (small_a, 0, D_ATTN), (small_a, 1, D_ATTN), (small_a, 2, D_ATTN), (small_c, 3, D_CONV),
        (small_c, 4, D_CONV), (small_g, 0, D_MODEL), (small_g, 1, D_MODEL),
        (small_c, 0, D_CONV), (small_c, 1, D_CONV), (small_c, 2, D_CONV),
        (small_g, 2, D_MODEL), (small_m, 0, D_MODEL), (small_f, 0, D_MODEL),
        (dbv, 0, D_FF), (dbg, 0, D_FF),
    ]
    pieces += [(dw_conv, j, D_CONV) for j in range(CONV_K)]
    pieces += [(src, tap, D_FF) for tap in range(3) for src in (dwv, dwg)]
    sources = [small_x, small_m, small_f, small_g, small_a, small_c, dbv, dbg, dwv, dwg, dw_conv]
    assert sum(width for _, _, width in pieces) == PACKED_TOTAL

    def body(*refs):
        o_ref = refs[-1]
        ref_of = {id(src): ref for src, ref in zip(sources, refs)}
        off = 0
        for src, row, width in pieces:
            o_ref[:, off:off + width] = ref_of[id(src)][row:row + 1, :]
            off += width

    return pl.pallas_call(body, name="pack_grads", out_shape=jax.ShapeDtypeStruct((1, PACKED_TOTAL), F32))(*sources)


def _small_adamw(gathered, gathered_rel, gathered_loss, weights, mom_m, mom_v):
    vec_names = [name for name, _ in SMALL]
    states = []
    for name in vec_names + ["rel_bias"]:
        states += [weights[name], mom_m[name], mom_v[name]]
    states = [a.reshape(a.shape[1:]) if a.ndim == 3 else a for a in states]
    n_state = len(states)

    def body(*refs):
        g_ref, rel_ref, loss_ref = refs[0], refs[1], refs[2]
        state_refs, out_refs = refs[3:3 + n_state], refs[3 + n_state:]
        total = g_ref[0:1, :]
        rel = rel_ref[0]
        loss = loss_ref[0]
        for d in range(1, N_DEV):
            total = total + g_ref[d:d + 1, :]
            rel = rel + rel_ref[d]
            loss = loss + loss_ref[d]
        off = 0
        for n, (name, width) in enumerate(SMALL):
            grad = total[:, off:off + width]
            w_ref, m_ref, v_ref = state_refs[3 * n:3 * n + 3]
            for ref, val in zip(out_refs[4 * n:4 * n + 4], (grad,) + _adam_math(w_ref[...], grad, m_ref[...], v_ref[...])):
                ref[...] = val
            off += width
        n = len(SMALL)
        w_ref, m_ref, v_ref = state_refs[3 * n:3 * n + 3]
        for ref, val in zip(out_refs[4 * n:4 * n + 4], (rel,) + _adam_math(w_ref[...], rel, m_ref[...], v_ref[...])):
            ref[...] = val
        dwc_ref, dwf_ref, loss_out = out_refs[4 * n + 4:]
        loss_out[...] = 0.5 * loss
        dwc_ref[...] = jnp.zeros_like(dwc_ref)
        dwf_ref[...] = jnp.zeros_like(dwf_ref)
        for j in range(CONV_K):
            dwc_ref[j:j + 1, :] = total[:, off:off + D_CONV]
            off += D_CONV
        for tap in range(3):
            dwf_ref[tap:tap + 1, :] = total[:, off:off + 2 * D_FF]
            off += 2 * D_FF

    out_shape = []
    for k in range(n_state // 3):
        out_shape += [jax.ShapeDtypeStruct(states[3 * k].shape, F32)] * 4
    out_shape += [jax.ShapeDtypeStruct((CONV_HALO, D_CONV), F32), jax.ShapeDtypeStruct((8, 2 * D_FF), F32),
                  jax.ShapeDtypeStruct((1, 128), F32)]
    res = pl.pallas_call(
        body, name="small_adamw", out_shape=out_shape,
        compiler_params=pltpu.CompilerParams(vmem_limit_bytes=VMEM_LIMIT_BYTES),
    )(gathered, gathered_rel, gathered_loss, *states)
    updates = {name: tuple(res[4 * n:4 * n + 4]) for n, name in enumerate(vec_names + ["rel_bias"])}
    return updates, res[-3], res[-2], res[-1]


def _ada_mod(c, w_shard):
    cols = w_shard.shape[1]

    def body(c_ref, w_ref, call_ref, mod_ref, send_sems, recv_sems):
        x, y, cc = _place()
        me = 4 * x + 2 * y + cc

        def exchange(ref, phase):
            sends, arrivals = [], []
            for mask in range(1, N_DEV):
                px = 1 - x if mask & 4 else x
                py = 1 - y if mask & 2 else y
                pc = 1 - cc if mask & 1 else cc
                both = dict(send_sem=send_sems.at[7 * phase + mask - 1], recv_sem=recv_sems.at[7 * phase + mask - 1],
                            device_id=(px, py, pc), device_id_type=MESH)
                sends.append(pltpu.make_async_remote_copy(src_ref=ref.at[me], dst_ref=ref.at[me], **both))
                arrivals.append(pltpu.make_async_remote_copy(src_ref=ref.at[me], dst_ref=ref.at[4 * px + 2 * py + pc],
                                                             **both))
            for cp in sends:
                cp.start()
            for cp in arrivals:
                cp.wait_recv()
            for cp in sends:
                cp.wait_send()

        v = c_ref[...]
        call_ref[me] = v * _sig(v)
        exchange(call_ref, 0)
        c_all = jnp.concatenate([call_ref[d] for d in range(N_DEV)], axis=0)
        mod_ref[me] = jnp.dot(c_all, w_ref[...], precision=HIGHEST, preferred_element_type=F32)
        exchange(mod_ref, 1)

    return pl.pallas_call(
        body, name="ada_mod",
        out_shape=[jax.ShapeDtypeStruct((N_DEV, 1, D_MODEL), F32), jax.ShapeDtypeStruct((N_DEV, N_DEV, cols), F32)],
        scratch_shapes=[pltpu.SemaphoreType.DMA((14,)), pltpu.SemaphoreType.DMA((14,))],
        compiler_params=pltpu.CompilerParams(vmem_limit_bytes=VMEM_LIMIT_BYTES),
    )(c, w_shard)


def _ada_grad(c_all, dmod_shard):
    def body(c_ref, d_ref, o_ref):
        o_ref[...] = lax.dot_general(c_ref[...], d_ref[...], _DIMS["tn"], precision=HIGHEST,
                                     preferred_element_type=F32)

    return pl.pallas_call(
        body, name="ada_grad", out_shape=jax.ShapeDtypeStruct((D_MODEL, dmod_shard.shape[1]), F32),
        compiler_params=pltpu.CompilerParams(vmem_limit_bytes=VMEM_LIMIT_BYTES),
    )(c_all, dmod_shard)


ROWS = 256


def _rms(v):
    r = lax.rsqrt(jnp.mean(v * v, axis=-1, keepdims=True) + EPS)
    return v * r, r


def _rms_bwd(dxn, xn, r):
    return r * (dxn - xn * jnp.mean(dxn * xn, axis=-1, keepdims=True))


def _colsum(v):
    return jnp.sum(v, axis=0, keepdims=True)


def _pre_mix(x, mod6, g1, comm=None):
    seq = x.shape[0]

    def body(ins, outs, scratch):
        x_ref, mod_ref, g_ref = ins
        xn, _ = _rms(x_ref[...])
        y = xn * g_ref[...]
        outs[0][...] = (y * (1.0 + mod_ref[SC_M:SC_M + 1, :]) + mod_ref[SH_M:SH_M + 1, :]).astype(BF16)

    (h,), extra = _host_call(
        body, "pre_mix", grid=(seq // ROWS,),
        in_specs=[_tile(ROWS, D_MODEL), _full((6, D_MODEL)), _full((1, D_MODEL))], out_specs=[_tile(ROWS, D_MODEL)],
        out_shape=[jax.ShapeDtypeStruct((seq, D_MODEL), BF16)], scratch_shapes=[], args=[x, mod6, g1], comm=comm)
    return h, extra


def _post_mix_pre_ffn(x, mod6, g2, g3, rows):
    seq = x.shape[0]

    def fn(y, first, ins, outs):
        x_ref, mod_ref, g2_ref, g3_ref = ins
        x1_ref, h_ref = outs
        yn, _ = _rms(y)
        x1 = x_ref[...] + mod_ref[GT_M:GT_M + 1, :] * (yn * g2_ref[...])
        x1_ref[...] = x1
        xn, _ = _rms(x1)
        y3 = xn * g3_ref[...]
        h_ref[...] = (y3 * (1.0 + mod_ref[SC_F:SC_F + 1, :]) + mod_ref[SH_F:SH_F + 1, :]).astype(BF16)

    return _Epilogue(
        [x, mod6, g2, g3], [_row_tile(rows, D_MODEL), _whole((6, D_MODEL)), _whole((1, D_MODEL)), _whole((1, D_MODEL))],
        [jax.ShapeDtypeStruct((seq, D_MODEL), F32), jax.ShapeDtypeStruct((seq, D_MODEL), BF16)],
        [_row_tile(rows, D_MODEL), _row_tile(rows, D_MODEL)], fn, keep_product=True)


def _final(x1, target, mod6, g4, rows):
    seq = x1.shape[0]

    def fn(y, first, ins, outs):
        x1_ref, t_ref, mod_ref, g_ref = ins
        loss_ref, dout_ref, dyf_ref, small_ref = outs

        @pl.when(first)
        def _():
            loss_ref[...] = jnp.zeros_like(loss_ref)
            small_ref[...] = jnp.zeros_like(small_ref)

        gt = mod_ref[GT_F:GT_F + 1, :]
        g4v = g_ref[...]
        yn, r = _rms(y)
        out = x1_ref[...] + gt * (yn * g4v)
        err = out - t_ref[...]
        loss_ref[...] += jnp.sum(jnp.mean(err * err, axis=-1, keepdims=True))
        dout = err * (1.0 / D_MODEL)
        dout_ref[...] = dout
        small_ref[0:1, :] += _colsum(dout * gt * yn)
        small_ref[1:2, :] += _colsum(dout * (yn * g4v))
        dyf_ref[...] = _rms_bwd(dout * gt * g4v, yn, r).astype(BF16)

    return _Epilogue(
        [x1, target, mod6, g4],
        [_row_tile(rows, D_MODEL), _row_tile(rows, D_MODEL), _whole((6, D_MODEL)), _whole((1, D_MODEL))],
        [jax.ShapeDtypeStruct((1, 128), F32), jax.ShapeDtypeStruct((seq, D_MODEL), F32),
         jax.ShapeDtypeStruct((seq, D_MODEL), BF16), jax.ShapeDtypeStruct((8, D_MODEL), F32)],
        [_whole((1, 128)), _row_tile(rows, D_MODEL), _row_tile(rows, D_MODEL), _whole((8, D_MODEL))],
        fn, keep_product=False)


def _mid_bwd(x1, dout, ymix, mod6, g3, g2, rows):
    seq = x1.shape[0]

    def fn(dh, first, ins, outs):
        x1_ref, dout_ref, y_ref, mod_ref, g3_ref, g2_ref = ins
        dx1_ref, dy_ref, small_ref = outs

        @pl.when(first)
        def _():
            small_ref[...] = jnp.zeros_like(small_ref)

        g3v, g2v = g3_ref[...], g2_ref[...]
        xn, r3 = _rms(x1_ref[...])
        y3 = xn * g3v
        dy3 = dh * (1.0 + mod_ref[SC_F:SC_F + 1, :])
        small_ref[0:1, :] += _colsum(dy3 * xn)
        small_ref[1:2, :] += _colsum(dh * y3)
        small_ref[2:3, :] += _colsum(dh)
        dx1 = dout_ref[...] + _rms_bwd(dy3 * g3v, xn, r3)
        dx1_ref[...] = dx1
        gt = mod_ref[GT_M:GT_M + 1, :]
        yn, r2 = _rms(y_ref[...])
        small_ref[3:4, :] += _colsum(dx1 * gt * yn)
        small_ref[4:5, :] += _colsum(dx1 * (yn * g2v))
        dy_ref[...] = _rms_bwd(dx1 * gt * g2v, yn, r2).astype(BF16)

    return _Epilogue(
        [x1, dout, ymix, mod6, g3, g2],
        [_row_tile(rows, D_MODEL)] * 3 + [_whole((6, D_MODEL)), _whole((1, D_MODEL)), _whole((1, D_MODEL))],
        [jax.ShapeDtypeStruct((seq, D_MODEL), F32), jax.ShapeDtypeStruct((seq, D_MODEL), BF16),
         jax.ShapeDtypeStruct((8, D_MODEL), F32)],
        [_row_tile(rows, D_MODEL), _row_tile(rows, D_MODEL), _whole((8, D_MODEL))], fn, keep_product=False)


def _pre_mix_bwd(x, dx1, mod6, g1, rows):
    seq = x.shape[0]

    def fn(dh, first, ins, outs):
        x_ref, dx1_ref, mod_ref, g_ref = ins
        dx_ref, small_ref = outs

        @pl.when(first)
        def _():
            small_ref[...] = jnp.zeros_like(small_ref)

        g1v = g_ref[...]
        xn, r = _rms(x_ref[...])
        dy = dh * (1.0 + mod_ref[SC_M:SC_M + 1, :])
        small_ref[0:1, :] += _colsum(dy * xn)
        small_ref[1:2, :] += _colsum(dh * (xn * g1v))
        small_ref[2:3, :] += _colsum(dh)
        dx_ref[...] = dx1_ref[...] + _rms_bwd(dy * g1v, xn, r)

    return _Epilogue(
        [x, dx1, mod6, g1],
        [_row_tile(rows, D_MODEL), _row_tile(rows, D_MODEL), _whole((6, D_MODEL)), _whole((1, D_MODEL))],
        [jax.ShapeDtypeStruct((seq, D_MODEL), F32), jax.ShapeDtypeStruct((8, D_MODEL), F32)],
        [_row_tile(rows, D_MODEL), _whole((8, D_MODEL))], fn, keep_product=False)


def _toeplitz_onehot(shape, offset_axis, top):
    m = lax.broadcasted_iota(jnp.int32, shape, offset_axis)
    i = lax.broadcasted_iota(jnp.int32, shape, 1 - offset_axis)
    return (i == jnp.clip(top - m, -MAX_REL, MAX_REL) + MAX_REL).astype(F32)


def _bias_table(rel_bias):
    width = GROUP_Q + GROUP_K

    def body(rb_ref, o_ref, t_ref):
        t_ref[...] = jnp.dot(rb_ref[...], _toeplitz_onehot((N_REL, width), 1, GROUP_K - 1), precision=HIGHEST,
                             preferred_element_type=F32)
        lane = lax.broadcasted_iota(jnp.int32, (N_HEADS, GROUP_K), 1)
        for r in range(GROUP_Q):
            first_key = (r // CHUNK) * CHUNK
            band = jnp.logical_and(lane >= first_key, lane < first_key + BAND)
            o_ref[r] = jnp.where(band, t_ref[:, GROUP_Q - 1 - r:GROUP_Q - 1 - r + GROUP_K], NEG_INF)

    return pl.pallas_call(
        body, name="bias_table", out_shape=jax.ShapeDtypeStruct((GROUP_Q, N_HEADS, GROUP_K), F32),
        scratch_shapes=[pltpu.VMEM((N_HEADS, width), F32)],
    )(rel_bias)


def _bias_grad(dbias_q):
    def body(d_ref, o_ref, t_ref):
        t_ref[...] = jnp.zeros_like(t_ref)
        for qi in range(CHUNK):
            t_ref[:, CHUNK - 1 - qi:CHUNK - 1 - qi + BAND] += d_ref[qi]
        o_ref[...] = jnp.dot(t_ref[...], _toeplitz_onehot((TOEPLITZ, N_REL), 0, BAND - 1), precision=HIGHEST,
                             preferred_element_type=F32)

    return pl.pallas_call(
        body, name="bias_grad", out_shape=jax.ShapeDtypeStruct((N_HEADS, N_REL), F32),
        scratch_shapes=[pltpu.VMEM((N_HEADS, TOEPLITZ), F32)],
    )(dbias_q)


def _resident_copies(qkv_hbm, t_hbm, k_ref, v_ref, t_ref, sems):
    inside = pl.ds(PAD_ROWS, qkv_hbm.shape[0])
    return (pltpu.make_async_copy(qkv_hbm.at[:, pl.ds(D_ATTN, D_ATTN)], k_ref.at[inside, :], sems.at[0]),
            pltpu.make_async_copy(qkv_hbm.at[:, pl.ds(2 * D_ATTN, D_ATTN)], v_ref.at[inside, :], sems.at[1]),
            pltpu.make_async_copy(t_hbm, t_ref, sems.at[2]))


def _start_resident(copies, k_ref, v_ref):
    k_ref[0:PAD_ROWS, :] = jnp.zeros((PAD_ROWS, D_ATTN), BF16)
    v_ref[0:PAD_ROWS, :] = jnp.zeros((PAD_ROWS, D_ATTN), BF16)
    for cp in copies:
        cp.start()


def _softmax_rows(s_ref, t_ref, h, before_start, rows):
    s = s_ref[rows, :] * (HEAD_DIM ** -0.5) + t_ref[h, rows, :] + before_start
    e = jnp.exp(s - jnp.max(s, axis=-1, keepdims=True))
    return e / jnp.sum(e, axis=-1, keepdims=True)


def _before_start(g):
    kj = lax.broadcasted_iota(jnp.int32, (8, GROUP_K), 1)
    return jnp.where(kj >= PAD_ROWS - g * GROUP_Q, 0.0, NEG_INF)


def _attn_fwd(qkv, table, comm=None):
    seq = qkv.shape[0]

    def body(ins, outs, scratch):
        q_ref, qkv_hbm, t_hbm = ins
        (o_ref,) = outs
        k_ref, v_ref, t_ref, s_ref, p_ref, sems = scratch
        g = pl.program_id(0)
        load_k, load_v, load_t = _resident_copies(qkv_hbm, t_hbm, k_ref, v_ref, t_ref, sems)

        @pl.when(g == 0)
        def _():
            _start_resident((load_k, load_v, load_t), k_ref, v_ref)
            load_k.wait()

        window = pl.ds(pl.multiple_of(g * GROUP_Q, GROUP_Q), GROUP_K)
        before_start = _before_start(g)
        for h in range(N_HEADS):
            cols = slice(h * HEAD_DIM, (h + 1) * HEAD_DIM)
            buf = h % 2
            s_ref[buf] = lax.dot_general(q_ref[:, cols], k_ref[window, cols], _DIMS["nt"],
                                         preferred_element_type=F32)
            if h == 0:
                pl.when(g == 0)(load_t.wait)
            for row in range(0, GROUP_Q, SOFTMAX_ROWS):
                halves = [_softmax_rows(s_ref.at[buf], t_ref, h, before_start, slice(r, r + 8))
                          for r in (row, row + 8)]
                p_ref[buf, row:row + SOFTMAX_ROWS, :] = jnp.concatenate(halves, axis=0).astype(BF16)
            if h == 0:
                pl.when(g == 0)(load_v.wait)
            o_ref[:, cols] = jnp.dot(p_ref[buf], v_ref[window, cols], preferred_element_type=F32).astype(BF16)

    (ao,), extra = _host_call(
        body, "attn_fwd", grid=(seq // GROUP_Q,),
        in_specs=[_tile(GROUP_Q, D_ATTN), ANY, ANY], out_specs=[_tile(GROUP_Q, D_ATTN)],
        out_shape=[jax.ShapeDtypeStruct((seq, D_ATTN), BF16)],
        scratch_shapes=[pltpu.VMEM((seq + PAD_ROWS, D_ATTN), BF16), pltpu.VMEM((seq + PAD_ROWS, D_ATTN), BF16),
                        pltpu.VMEM(table.shape, F32),
                        pltpu.VMEM((2, GROUP_Q, GROUP_K), F32), pltpu.VMEM((2, GROUP_Q, GROUP_K), BF16),
                        pltpu.SemaphoreType.DMA((3,))],
        args=[qkv, qkv, table], comm=comm)
    return ao, extra


def _attn_bwd(qkv, table, dao, comm=None):
    seq = qkv.shape[0]
    n_groups = seq // GROUP_Q
    fold_w = GROUP_K + (GROUP - 1) * CHUNK

    def body(ins, outs, scratch):
        q_ref, do_ref, qkv_hbm, t_hbm = ins
        dq_ref, dkt_hbm, dvt_hbm, db_ref, cs_ref = outs
        k_ref, v_ref, t_ref, db_acc, dkt_acc, dvt_acc, s_ref, dp_ref, p_ref, ds_ref, sems = scratch
        g = pl.program_id(0)

        load_k, load_v, load_t = _resident_copies(qkv_hbm, t_hbm, k_ref, v_ref, t_ref, sems)

        @pl.when(g == 0)
        def _():
            _start_resident((load_k, load_v, load_t), k_ref, v_ref)
            db_acc[...] = jnp.zeros_like(db_acc)
            dkt_acc[...] = jnp.zeros_like(dkt_acc)
            dvt_acc[...] = jnp.zeros_like(dvt_acc)
            cs_ref[...] = jnp.zeros_like(cs_ref)
            load_k.wait()
            load_v.wait()

        window = pl.ds(pl.multiple_of(g * GROUP_Q, GROUP_Q), GROUP_K)
        before_start = _before_start(g)
        for h in range(N_HEADS):
            cols = slice(h * HEAD_DIM, (h + 1) * HEAD_DIM)
            buf = h % 2
            qh, doh = q_ref[:, cols], do_ref[:, cols]
            kh, vh = k_ref[window, cols], v_ref[window, cols]
            s_ref[buf] = lax.dot_general(qh, kh, _DIMS["nt"], preferred_element_type=F32)
            dp_ref[buf] = lax.dot_general(doh, vh, _DIMS["nt"], preferred_element_type=F32)
            if h == 0:
                pl.when(g == 0)(load_t.wait)
            for row in range(0, GROUP_Q, SOFTMAX_ROWS):
                p_halves, ds_halves = [], []
                for r in (row, row + 8):
                    p = _softmax_rows(s_ref.at[buf], t_ref, h, before_start, slice(r, r + 8))
                    dp = dp_ref[buf, r:r + 8, :]
                    ds = p * (dp - jnp.sum(dp * p, axis=-1, keepdims=True))
                    chunk = r // CHUNK
                    shift = (GROUP - 1 - chunk) * CHUNK
                    db_acc[h, r - chunk * CHUNK:r - chunk * CHUNK + 8, shift:shift + GROUP_K] += ds
                    p_halves.append(p)
                    ds_halves.append(ds * (HEAD_DIM ** -0.5))
                p_ref[buf, row:row + SOFTMAX_ROWS, :] = jnp.concatenate(p_halves, axis=0).astype(BF16)
                ds_ref[buf, row:row + SOFTMAX_ROWS, :] = jnp.concatenate(ds_halves, axis=0).astype(BF16)
            dq_ref[:, cols] = jnp.dot(ds_ref[buf], kh, preferred_element_type=F32).astype(BF16)
            dkt_acc[cols, window] += lax.dot_general(qh, ds_ref[buf], _DIMS["tn"], preferred_element_type=F32)
            dvt_acc[cols, window] += lax.dot_general(doh, p_ref[buf], _DIMS["tn"], preferred_element_type=F32)
        cs_ref[0:1, :] += _colsum(dq_ref[...].astype(F32))

        @pl.when(g == n_groups - 1)
        def _():
            lo = (GROUP - 1) * CHUNK
            for h in range(N_HEADS):
                db_ref[h] = db_acc[h, :, lo:lo + BAND]
            inside = pl.ds(PAD_ROWS, seq)
            on_diagonal = (lax.broadcasted_iota(jnp.int32, (D_ATTN, D_ATTN), 0)
                           == lax.broadcasted_iota(jnp.int32, (D_ATTN, D_ATTN), 1))
            for row, acc in ((1, dkt_acc), (2, dvt_acc)):
                column = jnp.sum(acc[:, inside], axis=1, keepdims=True)
                cs_ref[row:row + 1, :] = _colsum(jnp.where(on_diagonal, column, 0.0))
            out_k = pltpu.make_async_copy(dkt_acc.at[:, inside], dkt_hbm, sems.at[0])
            out_v = pltpu.make_async_copy(dvt_acc.at[:, inside], dvt_hbm, sems.at[1])
            out_k.start()
            out_v.start()
            out_k.wait()
            out_v.wait()

    t_shape = (D_ATTN, seq + PAD_ROWS)
    outs, extra = _host_call(
        body, "attn_bwd", grid=(n_groups,),
        in_specs=[_tile(GROUP_Q, D_ATTN), _tile(GROUP_Q, D_ATTN), ANY, ANY],
        out_specs=[_tile(GROUP_Q, D_ATTN), ANY, ANY, _full((N_HEADS, CHUNK, BAND)), _full((8, D_ATTN))],
        out_shape=[jax.ShapeDtypeStruct((seq, D_ATTN), BF16), jax.ShapeDtypeStruct((D_ATTN, seq), F32),
                   jax.ShapeDtypeStruct((D_ATTN, seq), F32), jax.ShapeDtypeStruct((N_HEADS, CHUNK, BAND), F32),
                   jax.ShapeDtypeStruct((8, D_ATTN), F32)],
        scratch_shapes=[pltpu.VMEM((seq + PAD_ROWS, D_ATTN), BF16), pltpu.VMEM((seq + PAD_ROWS, D_ATTN), BF16),
                        pltpu.VMEM(table.shape, F32), pltpu.VMEM((N_HEADS, CHUNK, fold_w), F32), pltpu.VMEM(t_shape, F32),
                        pltpu.VMEM(t_shape, F32), pltpu.VMEM((2, GROUP_Q, GROUP_K), F32),
                        pltpu.VMEM((2, GROUP_Q, GROUP_K), F32), pltpu.VMEM((2, GROUP_Q, GROUP_K), BF16),
                        pltpu.VMEM((2, GROUP_Q, GROUP_K), BF16), pltpu.SemaphoreType.DMA((3,))],
        args=[qkv, dao, qkv, table], comm=comm)
    return outs, extra


def _dk_dv(dkt, dvt, seq):
    rows = 512
    transposed = pl.BlockSpec((D_ATTN, rows), lambda i: (0, i))

    def body(dkt_ref, dvt_ref, dk_ref, dv_ref):
        dk_ref[...] = dkt_ref[...].T.astype(BF16)
        dv_ref[...] = dvt_ref[...].T.astype(BF16)

    return pl.pallas_call(
        body, name="dk_dv", out_shape=[jax.ShapeDtypeStruct((seq, D_ATTN), BF16)] * 2, grid=(seq // rows,),
        in_specs=[transposed, transposed], out_specs=[_tile(rows, D_ATTN)] * 2, compiler_params=_cparams(1),
    )(dkt, dvt)


CONV_ROWS = 256


def _ln_silu(u1, g, b):
    mu = jnp.mean(u1, axis=-1, keepdims=True)
    xc = u1 - mu
    rs = lax.rsqrt(jnp.mean(xc * xc, axis=-1, keepdims=True) + EPS)
    xhat = xc * rs
    u2 = xhat * g + b
    return xhat, rs, u2


def _glu_into(s_ref, a_ref, b_ref, ah_ref, bh_ref, first):
    halo = ah_ref[...] * _sig(bh_ref[...])
    s_ref[0:CONV_HALO, :] = jnp.where(first, 0.0, halo)
    s_ref[CONV_HALO:CONV_HALO + CONV_ROWS, :] = a_ref[...] * _sig(b_ref[...])


CONV_LANES = 128
CONV_TILES = CONV_ROWS // 8


def _lag_weights(w_ref, lanes):
    return {e: jnp.broadcast_to(w_ref[CONV_K - 1 - e:CONV_K - e, lanes], (8, CONV_LANES)) for e in range(CONV_K)}


def _class_sums(w, tiles, k):
    total = None
    for a, tile in enumerate(tiles):
        if 8 * a + k < CONV_K:
            term = w[8 * a + k] * tile
            total = term if total is None else total + term
    return total


def _conv_back(src_ref, first_tile, w, lanes, row_id, emit):
    before = None
    for m in range(-1, CONV_TILES):
        tiles = [src_ref[8 * (first_tile + m - a):8 * (first_tile + m - a) + 8, lanes] for a in range(4)]
        rolled = [None] + [pltpu.roll(_class_sums(w, tiles, k), k, 0) for k in range(1, 8)]
        if m >= 0:
            out = _class_sums(w, tiles, 0)
            for k in range(1, 8):
                out = out + jnp.where(row_id < k, before[k], rolled[k])
            emit(m, out)
        before = rolled


def _conv_ahead(src_ref, w, lanes, row_id, emit):
    before = None
    for m in range(CONV_TILES + 1):
        tiles = [src_ref[8 * (m + a):8 * (m + a) + 8, lanes] for a in range(4)]
        rolled = [None] + [pltpu.roll(_class_sums(w, tiles, k), 8 - k, 0) for k in range(1, 8)]
        if m >= 1:
            out = before[0]
            for k in range(1, 8):
                out = out + jnp.where(row_id < 8 - k, before[k], rolled[k])
            emit(m - 1, out)
        before = [_class_sums(w, tiles, 0) if m < CONV_TILES else None] + rolled[1:]


def _conv_weight_sums(d_ref, s_ref, lanes, row_id, whole_shifts):
    zero = jnp.zeros((8, CONV_LANES), F32)
    sums = {8 * a + k: zero for a in whole_shifts for k in range(8) if 8 * a + k < CONV_K}

    def d_tile(m):
        return d_ref[8 * m:8 * m + 8, lanes] if 0 <= m < CONV_TILES else zero

    rolled = [None] + [zero] * 7
    for m in range(-1, CONV_TILES):
        cur, nxt = d_tile(m), d_tile(m + 1)
        rolled_next = [None] + [pltpu.roll(nxt, 8 - k, 0) for k in range(1, 8)]
        shifted = [cur] + [jnp.where(row_id < 8 - k, rolled[k], rolled_next[k]) for k in range(1, 8)]
        for a in whole_shifts:
            tile = s_ref[8 * (CONV_HALO // 8 + m - a):8 * (CONV_HALO // 8 + m - a) + 8, lanes]
            for k in range(8):
                if 8 * a + k < CONV_K and not (m < 0 and k == 0):
                    sums[8 * a + k] = sums[8 * a + k] + shifted[k] * tile
        rolled = rolled_next
    return sums


def _conv_fwd(zr, w_dw, b_dw, g_ln, b_ln, comm=None):
    seq = zr.shape[0]

    def body(a_ref, b_ref, ah_ref, bh_ref, w_ref, bias_ref, g_ref, bl_ref, u1_ref, u3_ref, s_ref):
        _glu_into(s_ref, a_ref, b_ref, ah_ref, bh_ref, pl.program_id(0) == 0)
        row_id = lax.broadcasted_iota(jnp.int32, (8, CONV_LANES), 0)
        for lo in range(0, D_CONV, CONV_LANES):
            lanes = slice(lo, lo + CONV_LANES)
            bias = jnp.broadcast_to(bias_ref[:, lanes], (8, CONV_LANES))

            def emit(m, out, lanes=lanes, bias=bias):
                u1_ref[8 * m:8 * m + 8, lanes] = out + bias

            _conv_back(s_ref, CONV_HALO // 8, _lag_weights(w_ref, lanes), lanes, row_id, emit)
        _, _, u2 = _ln_silu(u1_ref[...], g_ref[...], bl_ref[...])
        u3_ref[...] = (u2 * _sig(u2)).astype(BF16)

    return _host_call(
        lambda ins, outs, scratch: body(*ins, *outs, *scratch), "conv_fwd", grid=(seq // CONV_ROWS,),
        in_specs=[_tile(CONV_ROWS, D_CONV, 0), _tile(CONV_ROWS, D_CONV, 1),
                  _prev(CONV_HALO, D_CONV, CONV_ROWS, 0), _prev(CONV_HALO, D_CONV, CONV_ROWS, 1),
                  _full((CONV_K, D_CONV)), _full((1, D_CONV)), _full((1, D_CONV)), _full((1, D_CONV))],
        out_specs=[_tile(CONV_ROWS, D_CONV), _tile(CONV_ROWS, D_CONV)],
        out_shape=[jax.ShapeDtypeStruct((seq, D_CONV), F32), jax.ShapeDtypeStruct((seq, D_CONV), BF16)],
        scratch_shapes=[pltpu.VMEM((CONV_HALO + CONV_ROWS, D_CONV), F32)],
        args=[zr, zr, zr, zr, w_dw, b_dw, g_ln, b_ln], comm=comm)


def _conv_bwd(zr, u1, du3, w_dw, g_ln, b_ln, comm=None):
    seq = zr.shape[0]
    n_tiles = seq // CONV_ROWS
    n_halo = seq // CONV_HALO
    ext = CONV_ROWS + CONV_HALO

    def body(a_ref, b_ref, ah_ref, bh_ref, u1_ref, u1n_ref, d3_ref, d3n_ref, w_ref, g_ref, bl_ref,
             da_ref, db_ref, dw_ref, small_ref, s_ref, d_ref, du0_ref):
        i = pl.program_id(0)

        @pl.when(i == 0)
        def _():
            dw_ref[...] = jnp.zeros_like(dw_ref)
            small_ref[...] = jnp.zeros_like(small_ref)

        _glu_into(s_ref, a_ref, b_ref, ah_ref, bh_ref, i == 0)
        gv, bv = g_ref[...], bl_ref[...]

        def du1_of(u1, d3):
            xhat, rs, u2 = _ln_silu(u1, gv, bv)
            sg = _sig(u2)
            du2 = d3 * (sg * (1.0 + u2 * (1.0 - sg)))
            dxh = du2 * gv
            du1 = rs * (dxh - jnp.mean(dxh, axis=-1, keepdims=True)
                        - xhat * jnp.mean(dxh * xhat, axis=-1, keepdims=True))
            return du1, du2, xhat

        du1, du2, xhat = du1_of(u1_ref[...], d3_ref[...])
        du1n, _, _ = du1_of(u1n_ref[...], d3n_ref[...])
        d_ref[0:CONV_ROWS, :] = du1
        d_ref[CONV_ROWS:ext, :] = jnp.where(i == n_tiles - 1, 0.0, du1n)
        small_ref[0:1, :] += _colsum(du1)
        small_ref[1:2, :] += _colsum(du2 * xhat)
        small_ref[2:3, :] += _colsum(du2)
        row_id = lax.broadcasted_iota(jnp.int32, (8, CONV_LANES), 0)
        for lo in range(0, D_CONV, CONV_LANES):
            lanes = slice(lo, lo + CONV_LANES)

            def emit(m, out, lanes=lanes):
                du0_ref[8 * m:8 * m + 8, lanes] = out

            _conv_ahead(d_ref, _lag_weights(w_ref, lanes), lanes, row_id, emit)
            for whole_shifts in ((0, 1), (2, 3)):
                for e, total in _conv_weight_sums(d_ref, s_ref, lanes, row_id, whole_shifts).items():
                    dw_ref[CONV_K - 1 - e:CONV_K - e, lanes] += _colsum(total)
        du0 = du0_ref[...]
        sb = _sig(b_ref[...])
        da = du0 * sb
        dbv = du0 * a_ref[...] * sb * (1.0 - sb)
        da_ref[...] = da.astype(BF16)
        db_ref[...] = dbv.astype(BF16)
        small_ref[3:4, :] += _colsum(da)
        small_ref[4:5, :] += _colsum(dbv)

    return _host_call(
        lambda ins, outs, scratch: body(*ins, *outs, *scratch), "conv_bwd", grid=(n_tiles,),
        in_specs=[_tile(CONV_ROWS, D_CONV, 0), _tile(CONV_ROWS, D_CONV, 1),
                  _prev(CONV_HALO, D_CONV, CONV_ROWS, 0), _prev(CONV_HALO, D_CONV, CONV_ROWS, 1),
                  _tile(CONV_ROWS, D_CONV), _next(CONV_HALO, D_CONV, CONV_ROWS, n_halo),
                  _tile(CONV_ROWS, D_CONV), _next(CONV_HALO, D_CONV, CONV_ROWS, n_halo),
                  _full((CONV_K, D_CONV)), _full((1, D_CONV)), _full((1, D_CONV))],
        out_specs=[_tile(CONV_ROWS, D_CONV), _tile(CONV_ROWS, D_CONV), _full((CONV_HALO, D_CONV)),
                   _full((8, D_CONV))],
        out_shape=[jax.ShapeDtypeStruct((seq, D_CONV), BF16), jax.ShapeDtypeStruct((seq, D_CONV), BF16),
                   jax.ShapeDtypeStruct((CONV_HALO, D_CONV), F32), jax.ShapeDtypeStruct((8, D_CONV), F32)],
        scratch_shapes=[pltpu.VMEM((ext, D_CONV), F32), pltpu.VMEM((ext, D_CONV), F32),
                        pltpu.VMEM((CONV_ROWS, D_CONV), F32)],
        args=[zr, zr, zr, zr, u1, u1, du3, du3, w_dw, g_ln, b_ln], comm=comm)


MERGE_ROWS = 256


def _merge_fwd(ao, u3, zr, w_ao, w_co, b_co):
    seq = ao.shape[0]

    def body(ao_ref, u3_ref, ga_ref, gb_ref, wa_ref, wc_ref, bc_ref, y_ref, a_ref, cb_ref):
        a = jnp.dot(ao_ref[...], wa_ref[...], preferred_element_type=F32)
        cb = jnp.dot(u3_ref[...], wc_ref[...], preferred_element_type=F32) + bc_ref[...]
        a_ref[...] = a
        cb_ref[...] = cb
        y_ref[...] = (_sig(ga_ref[...]) * a + _sig(gb_ref[...]) * cb).astype(BF16)

    f32_out = jax.ShapeDtypeStruct((seq, D_MODEL), F32)
    return pl.pallas_call(
        body, name="merge_fwd",
        out_shape=[jax.ShapeDtypeStruct((seq, D_MODEL), BF16), f32_out, f32_out],
        grid=(seq // MERGE_ROWS,),
        in_specs=[_tile(MERGE_ROWS, D_ATTN), _tile(MERGE_ROWS, D_CONV), _tile(MERGE_ROWS, D_MODEL, 1),
                  _tile(MERGE_ROWS, D_MODEL, 2), _full(w_ao.shape), _full(w_co.shape), _full((1, D_MODEL))],
        out_specs=[_tile(MERGE_ROWS, D_MODEL)] * 3, compiler_params=_cparams(1),
    )(ao, u3, zr, zr, w_ao, w_co, b_co)


def _merge_bwd(a, cb, zr, rows):
    seq = a.shape[0]

    def fn(dy_v, first, ins, outs):
        a_ref, cb_ref, ga_ref, gb_ref = ins
        da_ref, dcb_ref, dga_ref, dgb_ref, small_ref = outs

        @pl.when(first)
        def _():
            small_ref[...] = jnp.zeros_like(small_ref)

        sa, sb = _sig(ga_ref[...]), _sig(gb_ref[...])
        dcb = dy_v * sb
        dga = dy_v * a_ref[...] * sa * (1.0 - sa)
        dgb = dy_v * cb_ref[...] * sb * (1.0 - sb)
        da_ref[...] = (dy_v * sa).astype(BF16)
        dcb_ref[...] = dcb.astype(BF16)
        dga_ref[...] = dga.astype(BF16)
        dgb_ref[...] = dgb.astype(BF16)
        small_ref[0:1, :] += _colsum(dga)
        small_ref[1:2, :] += _colsum(dgb)
        small_ref[2:3, :] += _colsum(dcb)

    bf = jax.ShapeDtypeStruct((seq, D_MODEL), BF16)
    gate = lambda col: pl.BlockSpec((rows, D_MODEL), lambda i, j: (i, col))
    return _Epilogue(
        [a, cb, zr, zr], [_row_tile(rows, D_MODEL), _row_tile(rows, D_MODEL), gate(1), gate(2)],
        [bf, bf, bf, bf, jax.ShapeDtypeStruct((8, D_MODEL), F32)],
        [_row_tile(rows, D_MODEL)] * 4 + [_whole((8, D_MODEL))], fn, keep_product=False)


FFN_ROWS = 2048
FFN_BLOCKS = D_FF // FFN_COLS
GELU_C = math.sqrt(2.0 / math.pi)


def _gelu(v):
    t = jnp.tanh(GELU_C * (v + 0.044715 * (v * v * v)))
    return 0.5 * v * (1.0 + t), t


def _gelu_grad(v, t):
    return 0.5 * (1.0 + t) + 0.5 * v * (1.0 - t * t) * (GELU_C * (1.0 + 3.0 * 0.044715 * (v * v)))


def _sublane_rows(ref, n):
    return [jnp.broadcast_to(ref[r:r + 1, :], (8, FFN_COLS)) for r in range(n)]


def _rolls(tile, shifts):
    return tuple(pltpu.roll(tile, s, 0) for s in shifts)


def _behind(prev_rolls, cur, row_id):
    rolls = _rolls(cur, (1, 2))
    x1 = jnp.where(row_id < 1, prev_rolls[0], rolls[0])
    x2 = jnp.where(row_id < 2, prev_rolls[1], rolls[1])
    return (x2, x1, cur), rolls


def _ahead(cur_rolls, next_rolls, row_id):
    return (jnp.where(row_id < 7, cur_rolls[0], next_rolls[0]), jnp.where(row_id < 6, cur_rolls[1], next_rolls[1]))


def _conv3(taps, w, bias):
    return w[0] * taps[0] + w[1] * taps[1] + w[2] * taps[2] + bias


def _ffn_specs(rows):
    tile = lambda off: pl.BlockSpec((rows, FFN_COLS), lambda j, i: (i, j + off))
    prev = lambda off: pl.BlockSpec((FFN_HALO, FFN_COLS),
                                    lambda j, i: (jnp.maximum(i * (rows // FFN_HALO) - 1, 0), j + off))
    wgt = lambda off: pl.BlockSpec((3, FFN_COLS), lambda j, i: (0, j + off))
    vec = lambda off: pl.BlockSpec((1, FFN_COLS), lambda j, i: (0, j + off))
    return tile, prev, wgt, vec


def _ffn_act(up, w_dw, b_dw):
    seq = up.shape[0]
    tile, prev, wgt, vec = _ffn_specs(FFN_ROWS)

    def body(v_ref, g_ref, vp_ref, gp_ref, wv_ref, wg_ref, bv_ref, bg_ref, act_ref):
        first = pl.program_id(1) == 0
        row_id = lax.broadcasted_iota(jnp.int32, (8, FFN_COLS), 0)
        wv, wg = _sublane_rows(wv_ref, 3), _sublane_rows(wg_ref, 3)
        (bv,), (bg,) = _sublane_rows(bv_ref, 1), _sublane_rows(bg_ref, 1)
        rolls_v = _rolls(jnp.where(first, 0.0, vp_ref[...]), (1, 2))
        rolls_g = _rolls(jnp.where(first, 0.0, gp_ref[...]), (1, 2))
        for row in range(0, FFN_ROWS, 16):
            halves = []
            for r in (row, row + 8):
                taps_v, rolls_v = _behind(rolls_v, v_ref[r:r + 8, :], row_id)
                taps_g, rolls_g = _behind(rolls_g, g_ref[r:r + 8, :], row_id)
                halves.append(_gelu(_conv3(taps_g, wg, bg))[0] * _conv3(taps_v, wv, bv))
            act_ref[row:row + 16, :] = jnp.concatenate(halves, axis=0).astype(BF16)

    return pl.pallas_call(
        body, name="ffn_act", out_shape=jax.ShapeDtypeStruct((seq, D_FF), BF16),
        grid=(FFN_BLOCKS, seq // FFN_ROWS),
        in_specs=[tile(0), tile(FFN_BLOCKS), prev(0), prev(FFN_BLOCKS), wgt(0), wgt(FFN_BLOCKS),
                  vec(0), vec(FFN_BLOCKS)],
        out_specs=tile(0), compiler_params=_cparams(2),
    )(up, up, up, up, w_dw, w_dw, b_dw, b_dw)


def _ffn_act_bwd(up, dact, w_dw, b_dw, comm=None):
    seq = up.shape[0]
    n_tiles = seq // FFN_ROWS
    n_halo = seq // FFN_HALO
    tile, prev, wgt, vec = _ffn_specs(FFN_ROWS)
    nxt = lambda off: pl.BlockSpec(
        (FFN_HALO, FFN_COLS), lambda j, i: (jnp.minimum((i + 1) * (FFN_ROWS // FFN_HALO), n_halo - 1), j + off))
    acc = lambda off: pl.BlockSpec((8, FFN_COLS), lambda j, i: (0, j + off))

    def body(v_ref, g_ref, vp_ref, gp_ref, vn_ref, gn_ref, da_ref, dan_ref, wv_ref, wg_ref, bv_ref, bg_ref,
             dv_out, dg_out, dwv_ref, dwg_ref, dbv_ref, dbg_ref):
        i = pl.program_id(1)
        first, last = i == 0, i == n_tiles - 1

        @pl.when(first)
        def _():
            for r in (dwv_ref, dwg_ref, dbv_ref, dbg_ref):
                r[...] = jnp.zeros_like(r)

        row_id = lax.broadcasted_iota(jnp.int32, (8, FFN_COLS), 0)
        wv, wg = _sublane_rows(wv_ref, 3), _sublane_rows(wg_ref, 3)
        (bv,), (bg,) = _sublane_rows(bv_ref, 1), _sublane_rows(bg_ref, 1)
        zero = jnp.zeros((8, FFN_COLS), F32)
        sums_v, sums_g = [zero] * 4, [zero] * 4
        rolls_v = _rolls(jnp.where(first, 0.0, vp_ref[...]), (1, 2))
        rolls_g = _rolls(jnp.where(first, 0.0, gp_ref[...]), (1, 2))
        behind = None
        done_v, done_g = [], []

        def grads(v_tile, g_tile, dact, rolls_v, rolls_g):
            taps_v, rolls_v = _behind(rolls_v, v_tile, row_id)
            taps_g, rolls_g = _behind(rolls_g, g_tile, row_id)
            val, gate = _conv3(taps_v, wv, bv), _conv3(taps_g, wg, bg)
            gel, t = _gelu(gate)
            return dact * gel, dact * val * _gelu_grad(gate, t), taps_v, taps_g, rolls_v, rolls_g

        def finish(tile, nxt, row):
            for (d, d_rolls), (_, n_rolls), w, done, o_ref in ((tile[0], nxt[0], wv, done_v, dv_out),
                                                               (tile[1], nxt[1], wg, done_g, dg_out)):
                d1, d2 = _ahead(d_rolls, n_rolls, row_id)
                done.append(w[2] * d + w[1] * d1 + w[0] * d2)
                if len(done) == 2:
                    o_ref[row - 16:row, :] = jnp.concatenate(done, axis=0).astype(BF16)
                    done.clear()

        for row in range(0, FFN_ROWS, 16):
            dact16 = da_ref[row:row + 16, :].astype(F32)
            for r, dact in ((row, dact16[0:8, :]), (row + 8, dact16[8:16, :])):
                dval, dgate, taps_v, taps_g, rolls_v, rolls_g = grads(v_ref[r:r + 8, :], g_ref[r:r + 8, :], dact,
                                                                      rolls_v, rolls_g)
                sums_v = [s + dval * x for s, x in zip(sums_v, taps_v)] + [sums_v[3] + dval]
                sums_g = [s + dgate * x for s, x in zip(sums_g, taps_g)] + [sums_g[3] + dgate]
                tile = ((dval, _rolls(dval, (7, 6))), (dgate, _rolls(dgate, (7, 6))))
                if behind is not None:
                    finish(behind, tile, r)
                behind = tile
        dact_next = jnp.where(last, 0.0, dan_ref[...].astype(F32)[0:FFN_HALO, :])
        dval, dgate, *_ = grads(vn_ref[...], gn_ref[...], dact_next, rolls_v, rolls_g)
        finish(behind, ((dval, _rolls(dval, (7, 6))), (dgate, _rolls(dgate, (7, 6)))), FFN_ROWS)
        for sums, dw_ref, db_ref in ((sums_v, dwv_ref, dbv_ref), (sums_g, dwg_ref, dbg_ref)):
            for tap in range(3):
                dw_ref[tap:tap + 1, :] += _colsum(sums[tap])
            db_ref[0:1, :] += _colsum(sums[3])

    half = jax.ShapeDtypeStruct((seq, D_FF), BF16)
    acc_shape = jax.ShapeDtypeStruct((8, D_FF), F32)
    return _host_call(
        lambda ins, outs, scratch: body(*ins, *outs, *scratch), "ffn_act_bwd", grid=(FFN_BLOCKS, n_tiles),
        in_specs=[tile(0), tile(FFN_BLOCKS), prev(0), prev(FFN_BLOCKS), nxt(0), nxt(FFN_BLOCKS),
                  tile(0), pl.BlockSpec((16, FFN_COLS), lambda j, i: (
                      jnp.minimum((i + 1) * (FFN_ROWS // 16), seq // 16 - 1), j)),
                  wgt(0), wgt(FFN_BLOCKS), vec(0), vec(FFN_BLOCKS)],
        out_specs=[tile(0), tile(0), acc(0), acc(0), acc(0), acc(0)],
        out_shape=[half, half, acc_shape, acc_shape, acc_shape, acc_shape],
        scratch_shapes=[], args=[up, up, up, up, up, up, dact, dact, w_dw, w_dw, b_dw, b_dw], comm=comm)


def _cols_to_blocks(full_cols):
    k, n8 = full_cols.shape
    return jnp.transpose(full_cols.reshape(k, N_DEV, n8 // N_DEV), (1, 0, 2))


def _rows_to_blocks(full_rows):
    r8, n = full_rows.shape
    return full_rows.reshape(N_DEV, r8 // N_DEV, n)


def _blocks_to_cols(gathered):
    _, k, n = gathered.shape
    return jnp.transpose(gathered, (1, 0, 2)).reshape(k, N_DEV * n)


def kernel(x, c, w_ada, b_ada, g_pre_mix, g_post_mix, w_in, b_in, rel_bias, w_attn_o, w_dw_conv, b_dw_conv, g_conv_ln, b_conv_ln, w_conv_o, b_conv_o, w_mix_o, g_pre_ffn, g_post_ffn, w_up, w_dw_ffn, b_dw_ffn, w_down, loss_target, m_w_ada, m_b_ada, m_g_pre_mix, m_g_post_mix, m_w_in, m_b_in, m_rel_bias, m_w_attn_o, m_w_dw_conv, m_b_dw_conv, m_g_conv_ln, m_b_conv_ln, m_w_conv_o, m_b_conv_o, m_w_mix_o, m_g_pre_ffn, m_g_post_ffn, m_w_up, m_w_dw_ffn, m_b_dw_ffn, m_w_down, v_w_ada, v_b_ada, v_g_pre_mix, v_g_post_mix, v_w_in, v_b_in, v_rel_bias, v_w_attn_o, v_w_dw_conv, v_b_dw_conv, v_g_conv_ln, v_b_conv_ln, v_w_conv_o, v_b_conv_o, v_w_mix_o, v_g_pre_ffn, v_g_post_ffn, v_w_up, v_w_dw_ffn, v_b_dw_ffn, v_w_down):
    names = ["w_ada", "b_ada", "g_pre_mix", "g_post_mix", "w_in", "b_in", "rel_bias", "w_attn_o", "w_dw_conv",
             "b_dw_conv", "g_conv_ln", "b_conv_ln", "w_conv_o", "b_conv_o", "w_mix_o", "g_pre_ffn", "g_post_ffn",
             "w_up", "w_dw_ffn", "b_dw_ffn", "w_down"]
    weights = dict(zip(names, [w_ada, b_ada, g_pre_mix, g_post_mix, w_in, b_in, rel_bias, w_attn_o, w_dw_conv,
                               b_dw_conv, g_conv_ln, b_conv_ln, w_conv_o, b_conv_o, w_mix_o, g_pre_ffn,
                               g_post_ffn, w_up, w_dw_ffn, b_dw_ffn, w_down]))
    mom_m = dict(zip(names, [m_w_ada, m_b_ada, m_g_pre_mix, m_g_post_mix, m_w_in, m_b_in, m_rel_bias, m_w_attn_o,
                             m_w_dw_conv, m_b_dw_conv, m_g_conv_ln, m_b_conv_ln, m_w_conv_o, m_b_conv_o,
                             m_w_mix_o, m_g_pre_ffn, m_g_post_ffn, m_w_up, m_w_dw_ffn, m_b_dw_ffn, m_w_down]))
    mom_v = dict(zip(names, [v_w_ada, v_b_ada, v_g_pre_mix, v_g_post_mix, v_w_in, v_b_in, v_rel_bias, v_w_attn_o,
                             v_w_dw_conv, v_b_dw_conv, v_g_conv_ln, v_b_conv_ln, v_w_conv_o, v_b_conv_o,
                             v_w_mix_o, v_g_pre_ffn, v_g_post_ffn, v_w_up, v_w_dw_ffn, v_b_dw_ffn, v_w_down]))
    shapes = {n: w.shape for n, w in weights.items()}

    seq = x.shape[1]
    me = 4 * lax.axis_index("x") + 2 * lax.axis_index("y") + lax.axis_index("c")
    x2 = x.reshape(seq, D_MODEL)
    target = loss_target.reshape(seq, D_MODEL)
    sq = lambda a: a.reshape(a.shape[1:])
    bf = lambda a: sq(a).astype(BF16)

    transposed = lambda a: jnp.swapaxes(sq(a), 0, 1)

    c_all, mod_all = _ada_mod(c, sq(w_ada))
    c_all = c_all.reshape(N_DEV, D_MODEL)
    mod = lax.dynamic_index_in_dim(mod_all, me, axis=1, keepdims=False)
    mod6 = (mod.reshape(1, 6 * D_MODEL) + b_ada).reshape(6, D_MODEL)

    h1, (g_in, g_dwc, g_dwf) = _pre_mix(
        x2, mod6, g_pre_mix, comm=_gather_comm([transposed(w_in).astype(BF16), sq(w_dw_conv), sq(w_dw_ffn)]))
    wt_in = g_in.reshape(g_in.shape[0] * g_in.shape[1], D_MODEL)
    wf_dwc = _blocks_to_cols(g_dwc)
    wf_dwf = _blocks_to_cols(g_dwf)
    qkv = _mm(h1, wt_in, "nt", BF16, "in_proj_qkv", bias=b_in, tm=1024, tn=768, cols=(0, 3 * D_ATTN))
    zr, _, (g_ao, g_co, g_mo) = _mm(h1, wt_in, "nt", F32, "in_proj_rest", bias=b_in, tm=1024, tn=3 * D_ATTN,
                                 cols=(3 * D_ATTN, 2 * D_CONV + 2 * D_MODEL),
                                 comm=_gather_comm([bf(w_attn_o), bf(w_conv_o), bf(w_mix_o)]))
    table = jnp.transpose(_bias_table(sq(rel_bias)), (1, 0, 2))
    ao, (g_up,) = _attn_fwd(qkv, table, comm=_gather_comm([transposed(w_up).astype(BF16)]))
    (u1, u3), (g_dn,) = _conv_fwd(zr, wf_dwc, b_dw_conv, g_conv_ln, b_conv_ln, comm=_gather_comm([bf(w_down)]))
    wf_ao = _blocks_to_cols(g_ao)
    wf_co = _blocks_to_cols(g_co)
    wf_mo = g_mo.reshape(D_MODEL, D_MODEL)
    wt_up = g_up.reshape(g_up.shape[0] * g_up.shape[1], D_MODEL)
    wf_dn = g_dn.reshape(D_FF, D_MODEL)
    y, a_br, cb_br = _merge_fwd(ao, u3, zr, wf_ao, wf_co, b_conv_o)
    ymix, (x1, h2), _ = _mm(y, wf_mo, "nn", F32, "mix_o", tm=512, tn=D_MODEL,
                            epilogue=_post_mix_pre_ffn(x2, mod6, g_post_mix, g_pre_ffn, 512))
    up = _mm(h2, wt_up, "nt", F32, "ffn_up", tm=1024, tn=1408)
    act = _ffn_act(up, wf_dwf, b_dw_ffn)
    _, (loss_lanes, dout, dyf, small_f), _ = _mm(act, wf_dn, "nn", F32, "ffn_down", tm=512, tn=D_MODEL,
                                                 epilogue=_final(x1, target, mod6, g_post_ffn, 512))

    dact = _mm(dyf, wf_dn, "nt", BF16, "ffn_down_dx", tm=1024, tn=1408)
    gw_down = _mm(act, dyf, "tn", BF16, "ffn_down_dw", tm=256, tn=1024)
    (dup_v, dup_g, dwv, dwg, dbv, dbg), (parts_down,) = _ffn_act_bwd(
        up, dact, wf_dwf, b_dw_ffn, comm=_scatter_comm([_rows_to_blocks(gw_down)]))
    _, (dx1, dymix, small_m), _ = _mm([dup_v, dup_g], wt_up, "nn", F32, "ffn_up_dx", tm=512, tn=D_MODEL,
                                      epilogue=_mid_bwd(x1, dout, ymix, mod6, g_pre_ffn, g_post_mix, 512))
    blocks_up = _rows_to_blocks(_mm_tn_rows([dup_v, dup_g], h2, "ffn_up_dw"))
    _, (da, dcb, dga, dgb, small_g), _ = _mm(dymix, wf_mo, "nt", F32, "mix_o_dx", tm=512, tn=D_MODEL,
                                             epilogue=_merge_bwd(a_br, cb_br, zr, 512))
    gw_mo = _mm(y, dymix, "tn", BF16, "mix_o_dw", tm=256, tn=D_MODEL)
    dao = _mm(da, wf_ao, "nt", BF16, "attn_o_dx", tm=1024)
    gw_ao = _mm(ao, da, "tn", BF16, "attn_o_dw")
    du3 = _mm(dcb, wf_co, "nt", F32, "conv_o_dx", tm=1024)
    gw_co = _mm(u3, dcb, "tn", BF16, "conv_o_dw")
    (dq, dkt, dvt, dbias, small_a), (parts_up,) = _attn_bwd(
        qkv, table, dao, comm=_scatter_comm([blocks_up]))
    g_rel = _bias_grad(jnp.transpose(dbias, (1, 0, 2)))
    (dglu_a, dglu_b, dw_conv, small_c), (parts_mo, parts_ao, parts_co) = _conv_bwd(
        zr, u1, du3, wf_dwc, g_conv_ln, b_conv_ln,
        comm=_scatter_comm([_rows_to_blocks(gw_mo), _cols_to_blocks(gw_ao), _cols_to_blocks(gw_co)]))
    dk, dv = _dk_dv(dkt, dvt, seq)
    dz = [dq, dk, dv, dglu_a, dglu_b, dga, dgb]
    dz_halves = [(p, c) for p in dz for c in range(p.shape[1] // D_ATTN)]
    blocks_in = _rows_to_blocks(_mm_tn_rows(dz, h1, "in_proj_dw"))
    _, (grad_x, small_x), (parts_in, _, _) = _mm(dz_halves, wt_in, "nn", F32, "in_proj_dx", tm=512, tn=D_MODEL,
                                                 k_piece=D_ATTN,
                                                 comm=_pair_scatter_comm(blocks_in),
                                                 epilogue=_pre_mix_bwd(x2, dx1, mod6, g_pre_mix, 512))

    packed = _pack_grads(small_x, small_m, small_f, small_g, small_a, small_c, dbv, dbg, dwv, dwg, dw_conv)
    gathered, gathered_rel, gathered_loss = _run_comm(_gather_comm([packed, g_rel, loss_lanes]), "gather_small")
    gathered = gathered.reshape(N_DEV, PACKED_TOTAL)
    updates, g_dwc_full, g_dwf_full, loss_all = _small_adamw(gathered, gathered_rel, gathered_loss, weights, mom_m,
                                                             mom_v)
    loss = loss_all[0, 0]

    grads, deltas, new_m, new_v = {}, {}, {}, {}

    def record(name, update, is_transposed=False):
        for dst, val in zip((grads, deltas, new_m, new_v), update):
            dst[name] = (jnp.swapaxes(val, 0, 1) if is_transposed else val).reshape(shapes[name])

    for name, update in updates.items():
        record(name, update)

    def local_update(name, grad, view=sq):
        record(name, _adamw(view(weights[name]), view(mom_m[name]), view(mom_v[name]), "adamw_" + name, g=view(grad)))

    def taps_major(a):
        return a.reshape(a.shape[1], 1, a.shape[2])

    conv_cols, ffn_cols, ada_cols = D_CONV // N_DEV, 2 * D_FF // N_DEV, 6 * D_MODEL // N_DEV
    local_update("w_dw_conv", lax.dynamic_slice(g_dwc_full, (0, me * conv_cols), (CONV_K, conv_cols))[None], taps_major)
    local_update("w_dw_ffn", lax.dynamic_slice(g_dwf_full, (0, me * ffn_cols), (3, ffn_cols))[None], taps_major)
    local_update("w_ada", _ada_grad(c_all, lax.dynamic_slice(gathered, (0, me * ada_cols), (N_DEV, ada_cols)))[None])

    for name, part in (("w_attn_o", parts_ao), ("w_conv_o", parts_co), ("w_mix_o", parts_mo), ("w_down", parts_down)):
        record(name, _adamw(sq(weights[name]), sq(mom_m[name]), sq(mom_v[name]), "adamw_" + name, parts=part))
    for name, part in (("w_in", parts_in), ("w_up", parts_up)):
        record(name, _adamw(transposed(weights[name]), transposed(mom_m[name]), transposed(mom_v[name]),
                            "adamw_" + name, parts=part), is_transposed=True)

    return (loss, grad_x.reshape(x.shape), *[grads[n] for n in names], *[deltas[n] for n in names],
            *[new_m[n] for n in names], *[new_v[n] for n in names])
```
